```python
import math
import jax, jax.numpy as jnp
from jax import lax
import numpy as np

D_MODEL = 1024
BATCH = 8
SEQ = 2048
DEPTH = 1

MEM_LEN = 256
HEAD_DIM = 64
SWA_HEADS = 8
SWA_KV_HEADS = 2
WINDOW = 128
BLOCK = 128
N_BUCKETS = 32
MAX_DISTANCE = 128
GLA_HEADS = 4
GLA_DK = 32
GLA_DV = 64
GLA_RANK = 16
GLA_TAU = 16.0
GLA_CHUNK = 32
MEM_HEADS = 4
D_FF = 2816
EPS = 1e-6

SWA_Q_W = SWA_HEADS * HEAD_DIM
SWA_KV_W = SWA_KV_HEADS * HEAD_DIM
GLA_QK_W = GLA_HEADS * GLA_DK
GLA_V_W = GLA_HEADS * GLA_DV
MEM_Q_W = MEM_HEADS * HEAD_DIM
MIX_W = SWA_Q_W + GLA_V_W + MEM_Q_W
IN_SIZES = (SWA_Q_W, SWA_KV_W, SWA_KV_W, GLA_QK_W, GLA_QK_W, GLA_V_W, GLA_V_W, GLA_RANK, MEM_Q_W)
IN_W = SWA_Q_W + 2 * SWA_KV_W + 2 * GLA_QK_W + 2 * GLA_V_W + GLA_RANK + MEM_Q_W

kernel_name = "hymba_swa_sink_gla_memxattn_macaron"


def _split_points():
    return [int(s) for s in np.cumsum(IN_SIZES)[:-1]]


def rmsnorm(x, g):
    x32 = x.astype(jnp.float32)
    y = x32 * lax.rsqrt(jnp.mean(x32 * x32, axis=-1, keepdims=True) + EPS)
    return (y * g.astype(jnp.float32)).astype(x.dtype)


def swiglu(x, w_gate, w_up, w_down):
    return (jax.nn.silu(x @ w_gate) * (x @ w_up)) @ w_down


def t5_bucket(dist):
    max_exact = N_BUCKETS // 2
    d = jnp.maximum(dist, 1).astype(jnp.float32)
    large = max_exact + (jnp.log(d / max_exact) / math.log(MAX_DISTANCE / max_exact)
                         * (N_BUCKETS - max_exact)).astype(jnp.int32)
    large = jnp.minimum(large, N_BUCKETS - 1)
    return jnp.where(dist < max_exact, dist, large)


def swa_attention(q, k, v, q_gain, k_gain, sinks, rel_bias):
    B, S = q.shape[0], q.shape[1]
    nb = S // BLOCK
    G = SWA_HEADS // SWA_KV_HEADS
    q = rmsnorm(q, q_gain)
    k = rmsnorm(k, k_gain)
    qb = q.reshape(B, nb, BLOCK, SWA_KV_HEADS, G, HEAD_DIM)
    pad = jnp.zeros((B, BLOCK, SWA_KV_HEADS, HEAD_DIM), k.dtype)

    def band(t):
        tp = jnp.concatenate([pad, t], axis=1).reshape(B, nb + 1, BLOCK, SWA_KV_HEADS, HEAD_DIM)
        return jnp.concatenate([tp[:, :-1], tp[:, 1:]], axis=2)

    kb, vb = band(k), band(v)
    scores = jnp.einsum('bnqhgd,bnkhd->bhgnqk', qb, kb).astype(jnp.float32) * (HEAD_DIM ** -0.5)

    qi = jnp.arange(BLOCK)[:, None]
    kj = jnp.arange(2 * BLOCK)[None, :]
    dist = qi + BLOCK - kj
    in_win = (dist >= 0) & (dist < WINDOW)
    key_pos = jnp.arange(nb)[:, None, None] * BLOCK - BLOCK + kj[None]
    valid = in_win[None] & (key_pos >= 0)

    bias = rel_bias[t5_bucket(jnp.maximum(dist, 0))].astype(jnp.float32)
    bias = bias.transpose(2, 0, 1).reshape(SWA_KV_HEADS, G, BLOCK, 2 * BLOCK)
    scores = scores + bias[:, :, None]
    scores = jnp.where(valid[None, None, None], scores, -jnp.inf)

    sink = sinks.astype(jnp.float32).reshape(SWA_KV_HEADS, G)[None, :, :, None, None, None]
    m = jnp.maximum(jnp.max(scores, axis=-1, keepdims=True), sink)
    p = jnp.exp(scores - m)
    p = p / (jnp.sum(p, axis=-1, keepdims=True) + jnp.exp(sink - m))
    out = jnp.einsum('bhgnqk,bnkhd->bnqhgd', p.astype(v.dtype), vb)
    return out.reshape(B, S, SWA_Q_W)


def gla(q, k, v, out_gate, gate_lr, w_gate_up, b_gate, norm_gain):
    B, S = q.shape[0], q.shape[1]
    N, C = S // GLA_CHUNK, GLA_CHUNK
    log_a = jax.nn.log_sigmoid((gate_lr @ w_gate_up + b_gate).astype(jnp.float32)) / GLA_TAU

    def chunked(t, d):
        return t.astype(jnp.float32).reshape(B, N, C, GLA_HEADS, d).transpose(0, 3, 1, 2, 4)

    qc = chunked(q, GLA_DK) * (GLA_DK ** -0.5)
    kc = chunked(k, GLA_DK)
    vc = chunked(v, GLA_DV)
    b = jnp.cumsum(chunked(log_a, GLA_DK), axis=3)

    causal = jnp.tril(jnp.ones((C, C), dtype=bool))
    diff = b[..., :, None, :] - b[..., None, :, :]
    decay = jnp.exp(jnp.where(causal[:, :, None], diff, -jnp.inf))
    A = jnp.sum(qc[..., :, None, :] * kc[..., None, :, :] * decay, axis=-1)
    o_intra = jnp.einsum('bhnij,bhnjd->bhnid', A, vc)

    b_last = b[..., -1:, :]
    chunk_kv = jnp.einsum('bhncd,bhnce->bhnde', kc * jnp.exp(b_last - b), vc)
    chunk_decay = jnp.exp(b_last[..., 0, :])

    def step(state, inp):
        dec, kv = inp
        return dec[..., None] * state + kv, state

    s0 = jnp.zeros((B, GLA_HEADS, GLA_DK, GLA_DV), jnp.float32)
    _, s_prev = lax.scan(step, s0, (jnp.moveaxis(chunk_decay, 2, 0), jnp.moveaxis(chunk_kv, 2, 0)))
    s_prev = jnp.moveaxis(s_prev, 0, 2)
    o_inter = jnp.einsum('bhncd,bhnde->bhnce', qc * jnp.exp(b), s_prev)

    o = (o_intra + o_inter).transpose(0, 2, 3, 1, 4).reshape(B, S, GLA_HEADS, GLA_DV)
    o = rmsnorm(o, norm_gain).reshape(B, S, GLA_V_W)
    return (o * jax.nn.silu(out_gate.astype(jnp.float32))).astype(q.dtype)


def memory_attention(q, k, v, q_gain, k_gain):
    B, S = q.shape[0], q.shape[1]
    q = rmsnorm(q, q_gain)
    k = rmsnorm(k, k_gain)
    s = jnp.einsum('bshd,bmhd->bhsm', q, k).astype(jnp.float32) * (HEAD_DIM ** -0.5)
    p = jax.nn.softmax(s, axis=-1)
    out = jnp.einsum('bhsm,bmhd->bshd', p.astype(v.dtype), v)
    return out.reshape(B, S, MEM_Q_W)


def _normal(k, shape, scale):
    return jax.random.normal(k, shape, jnp.float32) * scale


def _gain(k, shape):
    return 1.0 + 0.05 * jax.random.normal(k, shape, jnp.float32)


def _fwd_setup_inputs(seed: int = 0) -> dict:
    key = jax.random.key(seed)
    ks = jax.random.split(key, 24)
    L, D = DEPTH, D_MODEL
    return {
        "x": _normal(ks[0], (BATCH, SEQ, D), 1.0),
        "mem": _normal(ks[1], (BATCH, MEM_LEN, D), 1.0),
        "ffn1_norm": _gain(ks[2], (L, D)),
        "ffn1_w_gate": _normal(ks[3], (L, D, D_FF), D ** -0.5),
        "ffn1_w_up": _normal(ks[4], (L, D, D_FF), D ** -0.5),
        "ffn1_w_down": _normal(ks[5], (L, D_FF, D), D_FF ** -0.5),
        "mix_norm": _gain(ks[6], (L, D)),
        "mem_norm": _gain(ks[7], (L, D)),
        "w_in": _normal(ks[8], (L, D, IN_W), D ** -0.5),
        "w_mem_kv": _normal(ks[9], (L, D, 2 * MEM_Q_W), D ** -0.5),
        "swa_q_norm": _gain(ks[10], (L, HEAD_DIM)),
        "swa_k_norm": _gain(ks[11], (L, HEAD_DIM)),
        "swa_sinks": _normal(ks[12], (L, SWA_HEADS), 0.5),
        "rel_bias": _normal(ks[13], (N_BUCKETS, SWA_HEADS), 0.2),
        "gla_w_gate_up": _normal(ks[14], (L, GLA_RANK, GLA_QK_W), GLA_RANK ** -0.5),
        "gla_b_gate": _normal(ks[15], (L, GLA_QK_W), 0.01),
        "gla_out_norm": _gain(ks[16], (L, GLA_DV)),
        "mem_q_norm": _gain(ks[17], (L, HEAD_DIM)),
        "mem_k_norm": _gain(ks[18], (L, HEAD_DIM)),
        "w_out": _normal(ks[19], (L, MIX_W, D), MIX_W ** -0.5),
        "ffn2_norm": _gain(ks[20], (L, D)),
        "ffn2_w_gate": _normal(ks[21], (L, D, D_FF), D ** -0.5),
        "ffn2_w_up": _normal(ks[22], (L, D, D_FF), D ** -0.5),
        "ffn2_w_down": _normal(ks[23], (L, D_FF, D), D_FF ** -0.5),
    }


def _fwd_reference(x, mem, ffn1_norm, ffn1_w_gate, ffn1_w_up, ffn1_w_down, mix_norm, mem_norm,
              w_in, w_mem_kv, swa_q_norm, swa_k_norm, swa_sinks, rel_bias, gla_w_gate_up,
              gla_b_gate, gla_out_norm, mem_q_norm, mem_k_norm, w_out, ffn2_norm,
              ffn2_w_gate, ffn2_w_up, ffn2_w_down):
    B, S, _ = x.shape
    M = mem.shape[1]
    splits = _split_points()
    for l in range(DEPTH):
        x = x + 0.5 * swiglu(rmsnorm(x, ffn1_norm[l]), ffn1_w_gate[l], ffn1_w_up[l], ffn1_w_down[l])

        h = rmsnorm(x, mix_norm[l])
        sq, sk, sv, gq, gk, gv, gg, glr, mq = jnp.split(h @ w_in[l], splits, axis=-1)

        y_swa = swa_attention(sq.reshape(B, S, SWA_HEADS, HEAD_DIM),
                              sk.reshape(B, S, SWA_KV_HEADS, HEAD_DIM),
                              sv.reshape(B, S, SWA_KV_HEADS, HEAD_DIM),
                              swa_q_norm[l], swa_k_norm[l], swa_sinks[l], rel_bias)
        y_gla = gla(gq, gk, gv, gg, glr, gla_w_gate_up[l], gla_b_gate[l], gla_out_norm[l])
        mk, mv = jnp.split(rmsnorm(mem, mem_norm[l]) @ w_mem_kv[l], 2, axis=-1)
        y_mem = memory_attention(mq.reshape(B, S, MEM_HEADS, HEAD_DIM),
                                 mk.reshape(B, M, MEM_HEADS, HEAD_DIM),
                                 mv.reshape(B, M, MEM_HEADS, HEAD_DIM),
                                 mem_q_norm[l], mem_k_norm[l])
        x = x + jnp.concatenate([y_swa, y_gla, y_mem], axis=-1) @ w_out[l]

        x = x + 0.5 * swiglu(rmsnorm(x, ffn2_norm[l]), ffn2_w_gate[l], ffn2_w_up[l], ffn2_w_down[l])
    return x


import jax as _jax
import jax.numpy as _jnp

TWIN_FORMAT = 'train_step'
FWD_PARAMS = ['x', 'mem', 'ffn1_norm', 'ffn1_w_gate', 'ffn1_w_up', 'ffn1_w_down', 'mix_norm', 'mem_norm', 'w_in', 'w_mem_kv', 'swa_q_norm', 'swa_k_norm', 'swa_sinks', 'rel_bias', 'gla_w_gate_up', 'gla_b_gate', 'gla_out_norm', 'mem_q_norm', 'mem_k_norm', 'w_out', 'ffn2_norm', 'ffn2_w_gate', 'ffn2_w_up', 'ffn2_w_down']
TWIN_WEIGHTS = ['ffn1_norm', 'ffn1_w_gate', 'ffn1_w_up', 'ffn1_w_down', 'mix_norm', 'mem_norm', 'w_in', 'w_mem_kv', 'swa_q_norm', 'swa_k_norm', 'swa_sinks', 'rel_bias', 'gla_w_gate_up', 'gla_b_gate', 'gla_out_norm', 'mem_q_norm', 'mem_k_norm', 'w_out', 'ffn2_norm', 'ffn2_w_gate', 'ffn2_w_up', 'ffn2_w_down']
TWIN_DIFF_INPUT = 'x'
TWIN_INPUTS = ['x', 'mem', 'ffn1_norm', 'ffn1_w_gate', 'ffn1_w_up', 'ffn1_w_down', 'mix_norm', 'mem_norm', 'w_in', 'w_mem_kv', 'swa_q_norm', 'swa_k_norm', 'swa_sinks', 'rel_bias', 'gla_w_gate_up', 'gla_b_gate', 'gla_out_norm', 'mem_q_norm', 'mem_k_norm', 'w_out', 'ffn2_norm', 'ffn2_w_gate', 'ffn2_w_up', 'ffn2_w_down', 'loss_target', 'm_ffn1_norm', 'm_ffn1_w_gate', 'm_ffn1_w_up', 'm_ffn1_w_down', 'm_mix_norm', 'm_mem_norm', 'm_w_in', 'm_w_mem_kv', 'm_swa_q_norm', 'm_swa_k_norm', 'm_swa_sinks', 'm_rel_bias', 'm_gla_w_gate_up', 'm_gla_b_gate', 'm_gla_out_norm', 'm_mem_q_norm', 'm_mem_k_norm', 'm_w_out', 'm_ffn2_norm', 'm_ffn2_w_gate', 'm_ffn2_w_up', 'm_ffn2_w_down', 'v_ffn1_norm', 'v_ffn1_w_gate', 'v_ffn1_w_up', 'v_ffn1_w_down', 'v_mix_norm', 'v_mem_norm', 'v_w_in', 'v_w_mem_kv', 'v_swa_q_norm', 'v_swa_k_norm', 'v_swa_sinks', 'v_rel_bias', 'v_gla_w_gate_up', 'v_gla_b_gate', 'v_gla_out_norm', 'v_mem_q_norm', 'v_mem_k_norm', 'v_w_out', 'v_ffn2_norm', 'v_ffn2_w_gate', 'v_ffn2_w_up', 'v_ffn2_w_down']
TWIN_OUTPUTS = ['loss', 'grad_x', 'grad_ffn1_norm', 'grad_ffn1_w_gate', 'grad_ffn1_w_up', 'grad_ffn1_w_down', 'grad_mix_norm', 'grad_mem_norm', 'grad_w_in', 'grad_w_mem_kv', 'grad_swa_q_norm', 'grad_swa_k_norm', 'grad_swa_sinks', 'grad_rel_bias', 'grad_gla_w_gate_up', 'grad_gla_b_gate', 'grad_gla_out_norm', 'grad_mem_q_norm', 'grad_mem_k_norm', 'grad_w_out', 'grad_ffn2_norm', 'grad_ffn2_w_gate', 'grad_ffn2_w_up', 'grad_ffn2_w_down', 'delta_ffn1_norm', 'delta_ffn1_w_gate', 'delta_ffn1_w_up', 'delta_ffn1_w_down', 'delta_mix_norm', 'delta_mem_norm', 'delta_w_in', 'delta_w_mem_kv', 'delta_swa_q_norm', 'delta_swa_k_norm', 'delta_swa_sinks', 'delta_rel_bias', 'delta_gla_w_gate_up', 'delta_gla_b_gate', 'delta_gla_out_norm', 'delta_mem_q_norm', 'delta_mem_k_norm', 'delta_w_out', 'delta_ffn2_norm', 'delta_ffn2_w_gate', 'delta_ffn2_w_up', 'delta_ffn2_w_down', 'new_m_ffn1_norm', 'new_m_ffn1_w_gate', 'new_m_ffn1_w_up', 'new_m_ffn1_w_down', 'new_m_mix_norm', 'new_m_mem_norm', 'new_m_w_in', 'new_m_w_mem_kv', 'new_m_swa_q_norm', 'new_m_swa_k_norm', 'new_m_swa_sinks', 'new_m_rel_bias', 'new_m_gla_w_gate_up', 'new_m_gla_b_gate', 'new_m_gla_out_norm', 'new_m_mem_q_norm', 'new_m_mem_k_norm', 'new_m_w_out', 'new_m_ffn2_norm', 'new_m_ffn2_w_gate', 'new_m_ffn2_w_up', 'new_m_ffn2_w_down', 'new_v_ffn1_norm', 'new_v_ffn1_w_gate', 'new_v_ffn1_w_up', 'new_v_ffn1_w_down', 'new_v_mix_norm', 'new_v_mem_norm', 'new_v_w_in', 'new_v_w_mem_kv', 'new_v_swa_q_norm', 'new_v_swa_k_norm', 'new_v_swa_sinks', 'new_v_rel_bias', 'new_v_gla_w_gate_up', 'new_v_gla_b_gate', 'new_v_gla_out_norm', 'new_v_mem_q_norm', 'new_v_mem_k_norm', 'new_v_w_out', 'new_v_ffn2_norm', 'new_v_ffn2_w_gate', 'new_v_ffn2_w_up', 'new_v_ffn2_w_down']
TWIN_LEAF_KINDS = {'loss': 'loss', 'grad_x': 'grad_x', 'grad_ffn1_norm': 'grad_w', 'grad_ffn1_w_gate': 'grad_w', 'grad_ffn1_w_up': 'grad_w', 'grad_ffn1_w_down': 'grad_w', 'grad_mix_norm': 'grad_w', 'grad_mem_norm': 'grad_w', 'grad_w_in': 'grad_w', 'grad_w_mem_kv': 'grad_w', 'grad_swa_q_norm': 'grad_w', 'grad_swa_k_norm': 'grad_w', 'grad_swa_sinks': 'grad_w', 'grad_rel_bias': 'grad_w', 'grad_gla_w_gate_up': 'grad_w', 'grad_gla_b_gate': 'grad_w', 'grad_gla_out_norm': 'grad_w', 'grad_mem_q_norm': 'grad_w', 'grad_mem_k_norm': 'grad_w', 'grad_w_out': 'grad_w', 'grad_ffn2_norm': 'grad_w', 'grad_ffn2_w_gate': 'grad_w', 'grad_ffn2_w_up': 'grad_w', 'grad_ffn2_w_down': 'grad_w', 'delta_ffn1_norm': 'delta_w', 'delta_ffn1_w_gate': 'delta_w', 'delta_ffn1_w_up': 'delta_w', 'delta_ffn1_w_down': 'delta_w', 'delta_mix_norm': 'delta_w', 'delta_mem_norm': 'delta_w', 'delta_w_in': 'delta_w', 'delta_w_mem_kv': 'delta_w', 'delta_swa_q_norm': 'delta_w', 'delta_swa_k_norm': 'delta_w', 'delta_swa_sinks': 'delta_w', 'delta_rel_bias': 'delta_w', 'delta_gla_w_gate_up': 'delta_w', 'delta_gla_b_gate': 'delta_w', 'delta_gla_out_norm': 'delta_w', 'delta_mem_q_norm': 'delta_w', 'delta_mem_k_norm': 'delta_w', 'delta_w_out': 'delta_w', 'delta_ffn2_norm': 'delta_w', 'delta_ffn2_w_gate': 'delta_w', 'delta_ffn2_w_up': 'delta_w', 'delta_ffn2_w_down': 'delta_w', 'new_m_ffn1_norm': 'new_m', 'new_m_ffn1_w_gate': 'new_m', 'new_m_ffn1_w_up': 'new_m', 'new_m_ffn1_w_down': 'new_m', 'new_m_mix_norm': 'new_m', 'new_m_mem_norm': 'new_m', 'new_m_w_in': 'new_m', 'new_m_w_mem_kv': 'new_m', 'new_m_swa_q_norm': 'new_m', 'new_m_swa_k_norm': 'new_m', 'new_m_swa_sinks': 'new_m', 'new_m_rel_bias': 'new_m', 'new_m_gla_w_gate_up': 'new_m', 'new_m_gla_b_gate': 'new_m', 'new_m_gla_out_norm': 'new_m', 'new_m_mem_q_norm': 'new_m', 'new_m_mem_k_norm': 'new_m', 'new_m_w_out': 'new_m', 'new_m_ffn2_norm': 'new_m', 'new_m_ffn2_w_gate': 'new_m', 'new_m_ffn2_w_up': 'new_m', 'new_m_ffn2_w_down': 'new_m', 'new_v_ffn1_norm': 'new_v', 'new_v_ffn1_w_gate': 'new_v', 'new_v_ffn1_w_up': 'new_v', 'new_v_ffn1_w_down': 'new_v', 'new_v_mix_norm': 'new_v', 'new_v_mem_norm': 'new_v', 'new_v_w_in': 'new_v', 'new_v_w_mem_kv': 'new_v', 'new_v_swa_q_norm': 'new_v', 'new_v_swa_k_norm': 'new_v', 'new_v_swa_sinks': 'new_v', 'new_v_rel_bias': 'new_v', 'new_v_gla_w_gate_up': 'new_v', 'new_v_gla_b_gate': 'new_v', 'new_v_gla_out_norm': 'new_v', 'new_v_mem_q_norm': 'new_v', 'new_v_mem_k_norm': 'new_v', 'new_v_w_out': 'new_v', 'new_v_ffn2_norm': 'new_v', 'new_v_ffn2_w_gate': 'new_v', 'new_v_ffn2_w_up': 'new_v', 'new_v_ffn2_w_down': 'new_v'}


def _forward(args):
    return _fwd_reference(*[args[k] for k in FWD_PARAMS])


def _output_shape():
    out = _jax.eval_shape(lambda: _forward(_fwd_setup_inputs(0)))
    return out.shape, out.dtype

N_MICROBATCH = 1
ADAM_LR = 0.001
ADAM_B1 = 0.9
ADAM_B2 = 0.999
ADAM_EPS = 1e-08
ADAM_WD = 0.01
ADAM_STEP = 10
PER_EXAMPLE_BATCH_AXIS = {'x': 0, 'mem': 0, 'loss_target': 0}
SHARED_INPUTS = []
_WEIGHT_DTYPES = {'ffn1_norm': _jnp.float32, 'ffn1_w_gate': _jnp.float32, 'ffn1_w_up': _jnp.float32, 'ffn1_w_down': _jnp.float32, 'mix_norm': _jnp.float32, 'mem_norm': _jnp.float32, 'w_in': _jnp.float32, 'w_mem_kv': _jnp.float32, 'swa_q_norm': _jnp.float32, 'swa_k_norm': _jnp.float32, 'swa_sinks': _jnp.float32, 'rel_bias': _jnp.float32, 'gla_w_gate_up': _jnp.float32, 'gla_b_gate': _jnp.float32, 'gla_out_norm': _jnp.float32, 'mem_q_norm': _jnp.float32, 'mem_k_norm': _jnp.float32, 'w_out': _jnp.float32, 'ffn2_norm': _jnp.float32, 'ffn2_w_gate': _jnp.float32, 'ffn2_w_up': _jnp.float32, 'ffn2_w_down': _jnp.float32}
MOMENT_SCALE = {'ffn1_norm': 3.091680e+00, 'ffn1_w_gate': 5.236601e-02, 'ffn1_w_up': 5.480897e-02, 'ffn1_w_down': 8.932534e-02, 'mix_norm': 1.697684e+00, 'mem_norm': 4.543311e-02, 'w_in': 1.432630e-01, 'w_mem_kv': 3.406185e-02, 'swa_q_norm': 1.776662e+00, 'swa_k_norm': 1.778140e+00, 'swa_sinks': 5.122855e-01, 'rel_bias': 1.018160e-01, 'gla_w_gate_up': 2.694376e-02, 'gla_b_gate': 1.155591e-01, 'gla_out_norm': 2.242707e+01, 'mem_q_norm': 5.611659e-01, 'mem_k_norm': 5.512719e-01, 'w_out': 9.078242e-02, 'ffn2_norm': 3.102443e+00, 'ffn2_w_gate': 3.995512e-02, 'ffn2_w_up': 4.496453e-02, 'ffn2_w_down': 7.220973e-02}


def _to_microbatches(a, axis):
    t = _jnp.moveaxis(a, axis, 0)
    t = t.reshape((N_MICROBATCH, t.shape[0] // N_MICROBATCH) + t.shape[1:])
    return _jnp.moveaxis(t, 1, axis + 1)


def setup_inputs(seed: int = 0) -> dict:
    inp = _fwd_setup_inputs(seed)
    key = _jax.random.fold_in(_jax.random.key(seed), 7919)
    shape, _ = _output_shape()
    out = dict(inp)
    out["loss_target"] = _jax.random.normal(_jax.random.fold_in(key, 0), shape, _jnp.float32)
    for i, name in enumerate(TWIN_WEIGHTS):
        w = inp[name].astype(_jnp.float32)
        if MOMENT_SCALE is None:
            s = _jnp.sqrt(_jnp.mean(_jnp.square(w)) + 1e-30)
        else:
            s = MOMENT_SCALE[name]
        km, kv = _jax.random.split(_jax.random.fold_in(key, i + 1))
        out[name] = w
        out["m_" + name] = s * _jax.random.normal(km, w.shape, _jnp.float32)
        out["v_" + name] = (s * s) * _jax.random.uniform(kv, w.shape, _jnp.float32, 0.5, 1.5)
    if N_MICROBATCH > 1:
        for name, axis in PER_EXAMPLE_BATCH_AXIS.items():
            out[name] = _to_microbatches(out[name], axis)
    return {'x': out['x'], 'mem': out['mem'], 'ffn1_norm': out['ffn1_norm'], 'ffn1_w_gate': out['ffn1_w_gate'], 'ffn1_w_up': out['ffn1_w_up'], 'ffn1_w_down': out['ffn1_w_down'], 'mix_norm': out['mix_norm'], 'mem_norm': out['mem_norm'], 'w_in': out['w_in'], 'w_mem_kv': out['w_mem_kv'], 'swa_q_norm': out['swa_q_norm'], 'swa_k_norm': out['swa_k_norm'], 'swa_sinks': out['swa_sinks'], 'rel_bias': out['rel_bias'], 'gla_w_gate_up': out['gla_w_gate_up'], 'gla_b_gate': out['gla_b_gate'], 'gla_out_norm': out['gla_out_norm'], 'mem_q_norm': out['mem_q_norm'], 'mem_k_norm': out['mem_k_norm'], 'w_out': out['w_out'], 'ffn2_norm': out['ffn2_norm'], 'ffn2_w_gate': out['ffn2_w_gate'], 'ffn2_w_up': out['ffn2_w_up'], 'ffn2_w_down': out['ffn2_w_down'], 'loss_target': out['loss_target'], 'm_ffn1_norm': out['m_ffn1_norm'], 'm_ffn1_w_gate': out['m_ffn1_w_gate'], 'm_ffn1_w_up': out['m_ffn1_w_up'], 'm_ffn1_w_down': out['m_ffn1_w_down'], 'm_mix_norm': out['m_mix_norm'], 'm_mem_norm': out['m_mem_norm'], 'm_w_in': out['m_w_in'], 'm_w_mem_kv': out['m_w_mem_kv'], 'm_swa_q_norm': out['m_swa_q_norm'], 'm_swa_k_norm': out['m_swa_k_norm'], 'm_swa_sinks': out['m_swa_sinks'], 'm_rel_bias': out['m_rel_bias'], 'm_gla_w_gate_up': out['m_gla_w_gate_up'], 'm_gla_b_gate': out['m_gla_b_gate'], 'm_gla_out_norm': out['m_gla_out_norm'], 'm_mem_q_norm': out['m_mem_q_norm'], 'm_mem_k_norm': out['m_mem_k_norm'], 'm_w_out': out['m_w_out'], 'm_ffn2_norm': out['m_ffn2_norm'], 'm_ffn2_w_gate': out['m_ffn2_w_gate'], 'm_ffn2_w_up': out['m_ffn2_w_up'], 'm_ffn2_w_down': out['m_ffn2_w_down'], 'v_ffn1_norm': out['v_ffn1_norm'], 'v_ffn1_w_gate': out['v_ffn1_w_gate'], 'v_ffn1_w_up': out['v_ffn1_w_up'], 'v_ffn1_w_down': out['v_ffn1_w_down'], 'v_mix_norm': out['v_mix_norm'], 'v_mem_norm': out['v_mem_norm'], 'v_w_in': out['v_w_in'], 'v_w_mem_kv': out['v_w_mem_kv'], 'v_swa_q_norm': out['v_swa_q_norm'], 'v_swa_k_norm': out['v_swa_k_norm'], 'v_swa_sinks': out['v_swa_sinks'], 'v_rel_bias': out['v_rel_bias'], 'v_gla_w_gate_up': out['v_gla_w_gate_up'], 'v_gla_b_gate': out['v_gla_b_gate'], 'v_gla_out_norm': out['v_gla_out_norm'], 'v_mem_q_norm': out['v_mem_q_norm'], 'v_mem_k_norm': out['v_mem_k_norm'], 'v_w_out': out['v_w_out'], 'v_ffn2_norm': out['v_ffn2_norm'], 'v_ffn2_w_gate': out['v_ffn2_w_gate'], 'v_ffn2_w_up': out['v_ffn2_w_up'], 'v_ffn2_w_down': out['v_ffn2_w_down']}


def _loss(weights, diff, rest, loss_target):
    with _jax.named_scope("forward"):
        args = {**rest, TWIN_DIFF_INPUT: diff, **{k: w.astype(_WEIGHT_DTYPES[k]) for k, w in weights.items()}}
        y = _forward(args)
    with _jax.named_scope("loss_head"):
        err = _jnp.square(y.astype(_jnp.float32) - loss_target)
        return 0.5 * _jnp.sum(_jnp.mean(err, axis=-1)) if err.ndim else 0.5 * err


def _adamw(w, g, m, v):
    m = ADAM_B1 * m + (1.0 - ADAM_B1) * g
    v = ADAM_B2 * v + (1.0 - ADAM_B2) * _jnp.square(g)
    m_hat = m / (1.0 - ADAM_B1 ** ADAM_STEP)
    v_hat = v / (1.0 - ADAM_B2 ** ADAM_STEP)
    delta = -ADAM_LR * (m_hat / (_jnp.sqrt(v_hat) + ADAM_EPS) + ADAM_WD * w)
    return delta, m, v


def reference(x, mem, ffn1_norm, ffn1_w_gate, ffn1_w_up, ffn1_w_down, mix_norm, mem_norm, w_in, w_mem_kv, swa_q_norm, swa_k_norm, swa_sinks, rel_bias, gla_w_gate_up, gla_b_gate, gla_out_norm, mem_q_norm, mem_k_norm, w_out, ffn2_norm, ffn2_w_gate, ffn2_w_up, ffn2_w_down, loss_target, m_ffn1_norm, m_ffn1_w_gate, m_ffn1_w_up, m_ffn1_w_down, m_mix_norm, m_mem_norm, m_w_in, m_w_mem_kv, m_swa_q_norm, m_swa_k_norm, m_swa_sinks, m_rel_bias, m_gla_w_gate_up, m_gla_b_gate, m_gla_out_norm, m_mem_q_norm, m_mem_k_norm, m_w_out, m_ffn2_norm, m_ffn2_w_gate, m_ffn2_w_up, m_ffn2_w_down, v_ffn1_norm, v_ffn1_w_gate, v_ffn1_w_up, v_ffn1_w_down, v_mix_norm, v_mem_norm, v_w_in, v_w_mem_kv, v_swa_q_norm, v_swa_k_norm, v_swa_sinks, v_rel_bias, v_gla_w_gate_up, v_gla_b_gate, v_gla_out_norm, v_mem_q_norm, v_mem_k_norm, v_w_out, v_ffn2_norm, v_ffn2_w_gate, v_ffn2_w_up, v_ffn2_w_down):
    given = dict(x=x, mem=mem, ffn1_norm=ffn1_norm, ffn1_w_gate=ffn1_w_gate, ffn1_w_up=ffn1_w_up, ffn1_w_down=ffn1_w_down, mix_norm=mix_norm, mem_norm=mem_norm, w_in=w_in, w_mem_kv=w_mem_kv, swa_q_norm=swa_q_norm, swa_k_norm=swa_k_norm, swa_sinks=swa_sinks, rel_bias=rel_bias, gla_w_gate_up=gla_w_gate_up, gla_b_gate=gla_b_gate, gla_out_norm=gla_out_norm, mem_q_norm=mem_q_norm, mem_k_norm=mem_k_norm, w_out=w_out, ffn2_norm=ffn2_norm, ffn2_w_gate=ffn2_w_gate, ffn2_w_up=ffn2_w_up, ffn2_w_down=ffn2_w_down, loss_target=loss_target, m_ffn1_norm=m_ffn1_norm, m_ffn1_w_gate=m_ffn1_w_gate, m_ffn1_w_up=m_ffn1_w_up, m_ffn1_w_down=m_ffn1_w_down, m_mix_norm=m_mix_norm, m_mem_norm=m_mem_norm, m_w_in=m_w_in, m_w_mem_kv=m_w_mem_kv, m_swa_q_norm=m_swa_q_norm, m_swa_k_norm=m_swa_k_norm, m_swa_sinks=m_swa_sinks, m_rel_bias=m_rel_bias, m_gla_w_gate_up=m_gla_w_gate_up, m_gla_b_gate=m_gla_b_gate, m_gla_out_norm=m_gla_out_norm, m_mem_q_norm=m_mem_q_norm, m_mem_k_norm=m_mem_k_norm, m_w_out=m_w_out, m_ffn2_norm=m_ffn2_norm, m_ffn2_w_gate=m_ffn2_w_gate, m_ffn2_w_up=m_ffn2_w_up, m_ffn2_w_down=m_ffn2_w_down, v_ffn1_norm=v_ffn1_norm, v_ffn1_w_gate=v_ffn1_w_gate, v_ffn1_w_up=v_ffn1_w_up, v_ffn1_w_down=v_ffn1_w_down, v_mix_norm=v_mix_norm, v_mem_norm=v_mem_norm, v_w_in=v_w_in, v_w_mem_kv=v_w_mem_kv, v_swa_q_norm=v_swa_q_norm, v_swa_k_norm=v_swa_k_norm, v_swa_sinks=v_swa_sinks, v_rel_bias=v_rel_bias, v_gla_w_gate_up=v_gla_w_gate_up, v_gla_b_gate=v_gla_b_gate, v_gla_out_norm=v_gla_out_norm, v_mem_q_norm=v_mem_q_norm, v_mem_k_norm=v_mem_k_norm, v_w_out=v_w_out, v_ffn2_norm=v_ffn2_norm, v_ffn2_w_gate=v_ffn2_w_gate, v_ffn2_w_up=v_ffn2_w_up, v_ffn2_w_down=v_ffn2_w_down)
    weights = {n: given[n] for n in TWIN_WEIGHTS}
    shared = {n: given[n] for n in SHARED_INPUTS}
    per_example = {n: given[n] for n in ['x', 'mem']}
    grad_fn = _jax.value_and_grad(_loss, argnums=(0, 1))

    def one_microbatch(ex, loss_target):
        ex = dict(ex)
        diff = ex.pop(TWIN_DIFF_INPUT)
        return grad_fn(weights, diff, {**shared, **ex}, loss_target)

    if N_MICROBATCH == 1:
        loss, (grad_w, grad_x) = one_microbatch(per_example, given["loss_target"])
    else:
        def body(carry, xs):
            loss_sum, grad_sum = carry
            l_k, (gw_k, gx_k) = one_microbatch(xs[0], xs[1])
            with _jax.named_scope("update"):
                return (loss_sum + l_k, _jax.tree.map(_jnp.add, grad_sum, gw_k)), gx_k

        init = (_jnp.zeros((), _jnp.float32), _jax.tree.map(_jnp.zeros_like, weights))
        (loss, grad_w), grad_x = _jax.lax.scan(body, init, (per_example, given["loss_target"]))
    with _jax.named_scope("update"):
        delta_w, new_m, new_v = {}, {}, {}
        for n in TWIN_WEIGHTS:
            delta_w[n], new_m[n], new_v[n] = _adamw(weights[n], grad_w[n], given["m_" + n], given["v_" + n])
    return (loss, grad_x, *[grad_w[n] for n in TWIN_WEIGHTS], *[delta_w[n] for n in TWIN_WEIGHTS],
            *[new_m[n] for n in TWIN_WEIGHTS], *[new_v[n] for n in TWIN_WEIGHTS])
```

```python
import functools
import math

import numpy as np
import jax
import jax.numpy as jnp
from jax import lax
from jax.experimental import pallas as pl
from jax.experimental.pallas import tpu as pltpu

F32 = jnp.float32
BF16 = jnp.bfloat16
SDS = jax.ShapeDtypeStruct

EPS = 1e-6
HEAD_DIM = 64
SWA_HEADS = 8
SWA_KV_HEADS = 2
SWA_GROUP = SWA_HEADS // SWA_KV_HEADS
BLOCK = 128
N_BUCKETS = 32
MAX_DISTANCE = 128
GLA_HEADS = 4
GLA_DK = 32
GLA_DV = 64
GLA_RANK = 16
GLA_TAU = 16.0
GLA_CHUNK = 32
MEM_HEADS = 4
SWA_Q_W = SWA_HEADS * HEAD_DIM
SWA_KV_W = SWA_KV_HEADS * HEAD_DIM
GLA_QK_W = GLA_HEADS * GLA_DK
GLA_V_W = GLA_HEADS * GLA_DV
MEM_Q_W = MEM_HEADS * HEAD_DIM
IN_W = 1808
IN_W_PAD = 1920
COL_SQ, COL_SKV, COL_GQ, COL_GK, COL_GV, COL_GG, COL_MQ, COL_GLR = 0, 512, 768, 896, 1024, 1280, 1536, 1792

ADAM_LR = 0.001
ADAM_B1 = 0.9
ADAM_B2 = 0.999
ADAM_EPS = 1e-08
ADAM_WD = 0.01
ADAM_STEP = 10

N_DEV = 8
VMEM_LIMIT_BYTES = 56 * 1024 * 1024
MESH = pl.DeviceIdType.MESH


def _params(*sem):
    return pltpu.CompilerParams(dimension_semantics=sem or None, vmem_limit_bytes=VMEM_LIMIT_BYTES)


def _dot(a, b, ta, tb, precision=None):
    dims = (((0 if ta else 1,), (1 if tb else 0,)), ((), ()))
    return lax.dot_general(a, b, dims, preferred_element_type=F32, precision=precision)


def _mm_raw(a, b, ta=False, tb=False):
    return _dot(a.astype(BF16), b.astype(BF16), ta, tb)


def _mmf_raw(a, b, ta=False, tb=False):
    return _dot(a, b, ta, tb, lax.Precision.HIGHEST)


def _make_mm(raw):
    @functools.partial(jax.custom_vjp, nondiff_argnums=(2, 3))
    def mm(a, b, ta=False, tb=False):
        return raw(a, b, ta, tb)

    def fwd(a, b, ta, tb):
        return raw(a, b, ta, tb), (a, b)

    def bwd(ta, tb, res, g):
        a, b = res
        da = raw(b, g, tb, True) if ta else raw(g, b, False, not tb)
        db = raw(g, a, True, ta) if tb else raw(a, g, not ta, False)
        return da, db

    mm.defvjp(fwd, bwd)
    return mm


_mm = _make_mm(_mm_raw)
_mmf = _make_mm(_mmf_raw)


def _rms(x, g):
    return x * lax.rsqrt(jnp.mean(x * x, axis=-1, keepdims=True) + EPS) * g


def _silu_mul(g, u):
    return jax.nn.silu(g) * u


def _log_sigmoid(z):
    return jnp.minimum(z, 0.0) - jnp.log(1.0 + jnp.exp(-jnp.abs(z)))


def _matmul(a_list, b, *, ta=False, tb=False, tm, tn, b_blocks=None, res=None, scale=1.0, out_dtype=F32, name):
    if not isinstance(a_list, (list, tuple)):
        a_list = [a_list]
    n_a = len(a_list)
    m = a_list[0].shape[1] if ta else a_list[0].shape[0]
    ks = [a.shape[0] if ta else a.shape[1] for a in a_list]
    n = b.shape[0] if tb else b.shape[1]
    if b_blocks is None:
        assert n_a == 1
        b_blocks = [0]
    assert m % tm == 0 and n % tn == 0, (m, n, tm, tn)

    def body(*refs):
        a_refs, b_refs = refs[:n_a], refs[n_a:2 * n_a]
        r_ref = refs[2 * n_a] if res is not None else None
        o_ref = refs[-1]
        acc = _mm_raw(a_refs[0][...], b_refs[0][...], ta, tb)
        for k in range(1, n_a):
            acc = acc + _mm_raw(a_refs[k][...], b_refs[k][...], ta, tb)
        if scale != 1.0:
            acc = acc * scale
        if r_ref is not None:
            acc = r_ref[...] + acc
        o_ref[...] = acc.astype(out_dtype)

    in_specs = []
    for k in ks:
        in_specs.append(pl.BlockSpec((k, tm), lambda i, j: (0, i)) if ta else pl.BlockSpec((tm, k), lambda i, j: (i, 0)))
    for k, blk in zip(ks, b_blocks):
        if tb:
            in_specs.append(pl.BlockSpec((tn, k), functools.partial(lambda i, j, blk: (j, blk), blk=blk)))
        else:
            in_specs.append(pl.BlockSpec((k, tn), functools.partial(lambda i, j, blk: (blk, j), blk=blk)))
    args = list(a_list) + [b] * n_a
    if res is not None:
        in_specs.append(pl.BlockSpec((tm, tn), lambda i, j: (i, j)))
        args.append(res)
    return pl.pallas_call(
        body, name=name, grid=(m // tm, n // tn), in_specs=in_specs,
        out_specs=pl.BlockSpec((tm, tn), lambda i, j: (i, j)), out_shape=SDS((m, n), out_dtype),
        compiler_params=_params("parallel", "parallel"),
    )(*args)


def _rms_fwd(x, g, *, tm, name):
    s, d = x.shape

    def body(x_ref, g_ref, h_ref):
        h_ref[...] = _rms(x_ref[...], g_ref[...]).astype(BF16)

    return pl.pallas_call(
        body, name=name, grid=(s // tm,),
        in_specs=[pl.BlockSpec((tm, d), lambda i: (i, 0)), pl.BlockSpec((1, d), lambda i: (0, 0))],
        out_specs=pl.BlockSpec((tm, d), lambda i: (i, 0)), out_shape=SDS((s, d), BF16),
        compiler_params=_params("parallel"),
    )(x, g)


def _rms_bwd(x, g, dh, dres, *, tm, name):
    s, d = x.shape
    want_dx = dres is not None

    def body(*refs):
        if want_dx:
            x_ref, g_ref, dh_ref, dres_ref, dx_ref, dg_ref = refs
        else:
            x_ref, g_ref, dh_ref, dg_ref = refs
        _, vjp = jax.vjp(_rms, x_ref[...], g_ref[...])
        dx, dg = vjp(dh_ref[...])
        if want_dx:
            dx_ref[...] = dres_ref[...] + dx

        @pl.when(pl.program_id(0) == 0)
        def _():
            dg_ref[...] = jnp.zeros_like(dg_ref)

        dg_ref[...] += dg

    row = pl.BlockSpec((tm, d), lambda i: (i, 0))
    vec = pl.BlockSpec((1, d), lambda i: (0, 0))
    if want_dx:
        return pl.pallas_call(
            body, name=name, grid=(s // tm,), in_specs=[row, vec, row, row], out_specs=[row, vec],
            out_shape=[SDS((s, d), F32), SDS((1, d), F32)], compiler_params=_params("arbitrary"),
        )(x, g, dh, dres)
    return None, pl.pallas_call(
        body, name=name, grid=(s // tm,), in_specs=[row, vec, row], out_specs=vec,
        out_shape=SDS((1, d), F32), compiler_params=_params("arbitrary"),
    )(x, g, dh)


def _ffn_up(h, wg, wu, *, tm, tn, name):
    s, d = h.shape
    f = wg.shape[1]

    def body(h_ref, wg_ref, wu_ref, g_ref, u_ref, a_ref):
        hv = h_ref[...]
        g = _mm_raw(hv, wg_ref[...])
        u = _mm_raw(hv, wu_ref[...])
        g_ref[...] = g
        u_ref[...] = u
        a_ref[...] = _silu_mul(g, u).astype(BF16)

    tile = pl.BlockSpec((tm, tn), lambda i, j: (i, j))
    wcol = pl.BlockSpec((d, tn), lambda i, j: (0, j))
    return pl.pallas_call(
        body, name=name, grid=(s // tm, f // tn),
        in_specs=[pl.BlockSpec((tm, d), lambda i, j: (i, 0)), wcol, wcol], out_specs=[tile, tile, tile],
        out_shape=[SDS((s, f), F32), SDS((s, f), F32), SDS((s, f), BF16)],
        compiler_params=_params("parallel", "parallel"),
    )(h, wg, wu)


def _ffn_bwd_act(dy, wd, g, u, *, tm, tn, name):
    s, d = dy.shape
    f = wd.shape[0]

    def body(dy_ref, wd_ref, g_ref, u_ref, dg_ref, du_ref):
        da = _mm_raw(dy_ref[...], wd_ref[...], False, True) * 0.5
        _, vjp = jax.vjp(_silu_mul, g_ref[...], u_ref[...])
        dg, du = vjp(da)
        dg_ref[...] = dg.astype(BF16)
        du_ref[...] = du.astype(BF16)

    tile = pl.BlockSpec((tm, tn), lambda i, j: (i, j))
    return pl.pallas_call(
        body, name=name, grid=(s // tm, f // tn),
        in_specs=[pl.BlockSpec((tm, d), lambda i, j: (i, 0)), pl.BlockSpec((tn, d), lambda i, j: (j, 0)), tile, tile],
        out_specs=[tile, tile], out_shape=[SDS((s, f), BF16), SDS((s, f), BF16)],
        compiler_params=_params("parallel", "parallel"),
    )(dy, wd, g, u)


def _ffn_fwd(x, gain, wg, wu, wd, tag):
    h = _rms_fwd(x, gain, tm=256, name=f"{tag}_rms")
    g, u, a = _ffn_up(h, wg, wu, tm=512, tn=256, name=f"{tag}_up")
    y = _matmul(a, wd, tm=512, tn=512, res=x, scale=0.5, name=f"{tag}_down")
    return y, (h, g, u, a)


def _ffn_bwd(dy, x, gain, wg, wu, wd, saved, tag):
    h, g, u, a = saved
    dg, du = _ffn_bwd_act(dy, wd, g, u, tm=512, tn=256, name=f"{tag}_dact")
    dh = _matmul(dg, wg, tb=True, tm=512, tn=512, name=f"{tag}_dh_gate")
    dh = _matmul(du, wu, tb=True, tm=512, tn=512, res=dh, name=f"{tag}_dh_up")
    dx, dgain = _rms_bwd(x, gain, dh, dy, tm=256, name=f"{tag}_drms")
    dwg = _matmul(h, dg, ta=True, tm=512, tn=256, name=f"{tag}_dw_gate")
    dwu = _matmul(h, du, ta=True, tm=512, tn=256, name=f"{tag}_dw_up")
    dwd = _matmul(a, dy, ta=True, tm=256, tn=512, scale=0.5, name=f"{tag}_dw_down")
    return dx, dgain, dwg, dwu, dwd


def _loss_bwd(y, target, *, tm, name):
    s, d = y.shape

    def body(y_ref, t_ref, dy_ref, l_ref):
        diff = y_ref[...] - t_ref[...]
        dy_ref[...] = diff * (1.0 / d)

        @pl.when(pl.program_id(0) == 0)
        def _():
            l_ref[...] = jnp.zeros_like(l_ref)

        l_ref[...] += 0.5 * jnp.sum(jnp.mean(diff * diff, axis=-1, keepdims=True), axis=0, keepdims=True)

    row = pl.BlockSpec((tm, d), lambda i: (i, 0))
    return pl.pallas_call(
        body, name=name, grid=(s // tm,), in_specs=[row, row],
        out_specs=[row, pl.BlockSpec((1, 1), lambda i: (0, 0))], out_shape=[SDS((s, d), F32), SDS((1, 1), F32)],
        compiler_params=_params("arbitrary"),
    )(y, target)


def _bucket_table():
    qi = np.arange(BLOCK)[:, None]
    kj = np.arange(2 * BLOCK)[None, :]
    dist = np.maximum(qi + BLOCK - kj, 0)
    max_exact = N_BUCKETS // 2
    d = np.maximum(dist, 1).astype(np.float32)
    large = max_exact + (np.log(d / np.float32(max_exact)) / np.float32(math.log(MAX_DISTANCE / max_exact))
                         * np.float32(N_BUCKETS - max_exact)).astype(np.int32)
    large = np.minimum(large, N_BUCKETS - 1)
    return np.where(dist < max_exact, dist, large).astype(np.int32)


def _swa_valid(n):
    qi = lax.broadcasted_iota(jnp.int32, (BLOCK, 2 * BLOCK), 0)
    kj = lax.broadcasted_iota(jnp.int32, (BLOCK, 2 * BLOCK), 1)
    dist = qi + BLOCK - kj
    return (dist >= 0) & (dist < BLOCK) & ((kj >= BLOCK) | (n > 0))


def _swa_head(q, kb, vb, qg, kg, sink, bias, valid):
    qn = _rms(q, qg)
    kn = _rms(kb, kg)
    s = _mm(qn, kn, False, True) * (HEAD_DIM ** -0.5) + bias
    s = jnp.where(valid, s, -jnp.inf)
    m = lax.stop_gradient(jnp.maximum(jnp.max(s, axis=-1, keepdims=True), sink))
    p = jnp.exp(s - m)
    p = p / (jnp.sum(p, axis=-1, keepdims=True) + jnp.exp(sink - m))
    return _mm(p, vb)


def _swa_bias_table(rb_ref, bucket, bias_s):
    for h in range(SWA_HEADS):
        acc = jnp.zeros((BLOCK, 2 * BLOCK), F32)
        for b in range(N_BUCKETS):
            acc = jnp.where(bucket == b, rb_ref[b, h], acc)
        bias_s[h] = acc


def _swa_band(kvp_ref, kvc_ref, g):
    lo = g * HEAD_DIM
    kb = jnp.concatenate([kvp_ref[:, lo:lo + HEAD_DIM], kvc_ref[:, lo:lo + HEAD_DIM]], axis=0)
    lo += SWA_KV_W
    vb = jnp.concatenate([kvp_ref[:, lo:lo + HEAD_DIM], kvc_ref[:, lo:lo + HEAD_DIM]], axis=0)
    return kb, vb


def _swa_specs(order):
    kvc = COL_SKV // (2 * SWA_KV_W)
    return [
        pl.BlockSpec((BLOCK, SWA_Q_W), lambda t: (order(t), 0)),
        pl.BlockSpec((BLOCK, 2 * SWA_KV_W), lambda t: (jnp.maximum(order(t) - 1, 0), kvc)),
        pl.BlockSpec((BLOCK, 2 * SWA_KV_W), lambda t: (order(t), kvc)),
        pl.BlockSpec((1, HEAD_DIM), lambda t: (0, 0)),
        pl.BlockSpec((1, HEAD_DIM), lambda t: (0, 0)),
        pl.BlockSpec(memory_space=pltpu.SMEM),
        pl.BlockSpec(memory_space=pltpu.SMEM),
        pl.BlockSpec((BLOCK, 2 * BLOCK), lambda t: (0, 0)),
    ]


def _swa_fwd(p, qg, kg, sinks, rel_bias, *, name):
    s = p.shape[0]
    nb = s // BLOCK

    def body(q_ref, kvp_ref, kvc_ref, qg_ref, kg_ref, sink_ref, rb_ref, bucket_ref, y_ref, bias_s):
        n = pl.program_id(0)

        @pl.when(n == 0)
        def _():
            _swa_bias_table(rb_ref, bucket_ref[...], bias_s)

        valid = _swa_valid(n)
        for g in range(SWA_KV_HEADS):
            kb, vb = _swa_band(kvp_ref, kvc_ref, g)
            for hh in range(SWA_GROUP):
                h = g * SWA_GROUP + hh
                cols = slice(h * HEAD_DIM, (h + 1) * HEAD_DIM)
                sink = jnp.full((1, 1), sink_ref[h], F32)
                y_ref[:, cols] = _swa_head(q_ref[:, cols], kb, vb, qg_ref[...], kg_ref[...], sink, bias_s[h], valid)

    return pl.pallas_call(
        body, name=name, grid=(nb,), in_specs=_swa_specs(lambda t: t),
        out_specs=pl.BlockSpec((BLOCK, SWA_Q_W), lambda t: (t, 0)), out_shape=SDS((s, SWA_Q_W), F32),
        scratch_shapes=[pltpu.VMEM((SWA_HEADS, BLOCK, 2 * BLOCK), F32)],
        compiler_params=_params("arbitrary"),
    )(p, p, p, qg, kg, sinks, rel_bias, jnp.asarray(_bucket_table()))


def _swa_bwd(p, qg, kg, sinks, rel_bias, dy_all, *, name):
    s = p.shape[0]
    nb = s // BLOCK

    def body(q_ref, kvp_ref, kvc_ref, qg_ref, kg_ref, sink_ref, rb_ref, bucket_ref, dy_ref,
             dq_ref, dkv_ref, dqg_ref, dkg_ref, dsink_ref, drb_ref, bias_s, dbias_s, carry_s):
        t = pl.program_id(0)
        n = nb - 1 - t

        @pl.when(t == 0)
        def _():
            _swa_bias_table(rb_ref, bucket_ref[...], bias_s)
            dbias_s[...] = jnp.zeros_like(dbias_s)
            carry_s[...] = jnp.zeros_like(carry_s)
            dqg_ref[...] = jnp.zeros_like(dqg_ref)
            dkg_ref[...] = jnp.zeros_like(dkg_ref)
            dsink_ref[...] = jnp.zeros_like(dsink_ref)
            drb_ref[...] = jnp.zeros_like(drb_ref)

        valid = _swa_valid(n)
        lane = lax.broadcasted_iota(jnp.int32, (1, BLOCK), 1)
        dqg = jnp.zeros((1, HEAD_DIM), F32)
        dkg = jnp.zeros((1, HEAD_DIM), F32)
        dsink_vec = jnp.zeros((1, BLOCK), F32)
        for g in range(SWA_KV_HEADS):
            kb, vb = _swa_band(kvp_ref, kvc_ref, g)
            dkb = jnp.zeros((2 * BLOCK, HEAD_DIM), F32)
            dvb = jnp.zeros((2 * BLOCK, HEAD_DIM), F32)
            for hh in range(SWA_GROUP):
                h = g * SWA_GROUP + hh
                cols = slice(h * HEAD_DIM, (h + 1) * HEAD_DIM)
                sink = jnp.full((1, 1), sink_ref[h], F32)
                _, vjp = jax.vjp(functools.partial(_swa_head, valid=valid),
                                 q_ref[:, cols], kb, vb, qg_ref[...], kg_ref[...], sink, bias_s[h])
                dq, dkb_h, dvb_h, dqg_h, dkg_h, dsink_h, dbias_h = vjp(dy_ref[:, cols])
                dq_ref[:, cols] = dq
                dkb += dkb_h
                dvb += dvb_h
                dqg += dqg_h
                dkg += dkg_h
                dsink_vec += jnp.where(lane == h, dsink_h, 0.0)
                dbias_s[h] += dbias_h
            lo = g * HEAD_DIM
            dkv_ref[:, lo:lo + HEAD_DIM] = dkb[BLOCK:] + carry_s[g]
            carry_s[g] = dkb[:BLOCK]
            lo += SWA_KV_W
            dkv_ref[:, lo:lo + HEAD_DIM] = dvb[BLOCK:] + carry_s[SWA_KV_HEADS + g]
            carry_s[SWA_KV_HEADS + g] = dvb[:BLOCK]
        dqg_ref[...] += dqg
        dkg_ref[...] += dkg
        dsink_ref[...] += dsink_vec

        @pl.when(t == nb - 1)
        def _():
            bucket = bucket_ref[...]
            row = lax.broadcasted_iota(jnp.int32, (N_BUCKETS, BLOCK), 0)
            col = lax.broadcasted_iota(jnp.int32, (N_BUCKETS, BLOCK), 1)
            acc = jnp.zeros((N_BUCKETS, BLOCK), F32)
            for h in range(SWA_HEADS):
                dbias = dbias_s[h]
                for b in range(N_BUCKETS):
                    part = jnp.sum(jnp.where(bucket == b, dbias, 0.0), axis=1, keepdims=True)
                    val = jnp.sum(part, axis=0, keepdims=True)
                    acc = acc + jnp.where((row == b) & (col == h), val, 0.0)
            drb_ref[...] = acc

    order = lambda t: nb - 1 - t
    vec = pl.BlockSpec((1, HEAD_DIM), lambda t: (0, 0))
    return pl.pallas_call(
        body, name=name, grid=(nb,),
        in_specs=_swa_specs(order) + [pl.BlockSpec((BLOCK, SWA_Q_W), lambda t: (order(t), 0))],
        out_specs=[pl.BlockSpec((BLOCK, SWA_Q_W), lambda t: (order(t), 0)),
                   pl.BlockSpec((BLOCK, 2 * SWA_KV_W), lambda t: (order(t), 0)),
                   vec, vec, pl.BlockSpec((1, BLOCK), lambda t: (0, 0)),
                   pl.BlockSpec((N_BUCKETS, BLOCK), lambda t: (0, 0))],
        out_shape=[SDS((s, SWA_Q_W), F32), SDS((s, 2 * SWA_KV_W), F32), SDS((1, HEAD_DIM), F32),
                   SDS((1, HEAD_DIM), F32), SDS((1, BLOCK), F32), SDS((N_BUCKETS, BLOCK), F32)],
        scratch_shapes=[pltpu.VMEM((SWA_HEADS, BLOCK, 2 * BLOCK), F32), pltpu.VMEM((SWA_HEADS, BLOCK, 2 * BLOCK), F32),
                        pltpu.VMEM((2 * SWA_KV_HEADS, BLOCK, HEAD_DIM), F32)],
        compiler_params=_params("arbitrary"),
    )(p, p, p, qg, kg, sinks, rel_bias, jnp.asarray(_bucket_table()), dy_all)


def _mem_head(q, k, v, qg, kg):
    qn = _rms(q, qg)
    kn = _rms(k, kg)
    s = _mm(qn, kn, False, True) * (HEAD_DIM ** -0.5)
    m = lax.stop_gradient(jnp.max(s, axis=-1, keepdims=True))
    e = jnp.exp(s - m)
    return _mm(e / jnp.sum(e, axis=-1, keepdims=True), v)


def _mem_fwd(p, kv, qg, kg, *, tq, name):
    s = p.shape[0]
    m = kv.shape[0]

    def body(q_ref, kv_ref, qg_ref, kg_ref, y_ref):
        for h in range(MEM_HEADS):
            cols = slice(h * HEAD_DIM, (h + 1) * HEAD_DIM)
            vcols = slice(MEM_Q_W + h * HEAD_DIM, MEM_Q_W + (h + 1) * HEAD_DIM)
            y_ref[:, cols] = _mem_head(q_ref[:, cols], kv_ref[:, cols], kv_ref[:, vcols], qg_ref[...], kg_ref[...])

    vec = pl.BlockSpec((1, HEAD_DIM), lambda t: (0, 0))
    return pl.pallas_call(
        body, name=name, grid=(s // tq,),
        in_specs=[pl.BlockSpec((tq, MEM_Q_W), lambda t: (t, COL_MQ // MEM_Q_W)),
                  pl.BlockSpec((m, 2 * MEM_Q_W), lambda t: (0, 0)), vec, vec],
        out_specs=pl.BlockSpec((tq, MEM_Q_W), lambda t: (t, 0)), out_shape=SDS((s, MEM_Q_W), F32),
        compiler_params=_params("parallel"),
    )(p, kv, qg, kg)


def _mem_bwd(p, kv, qg, kg, dy_all, *, tq, name):
    s = p.shape[0]
    m = kv.shape[0]

    def body(q_ref, kv_ref, qg_ref, kg_ref, dy_ref, dq_ref, dkv_ref, dqg_ref, dkg_ref):
        @pl.when(pl.program_id(0) == 0)
        def _():
            dkv_ref[...] = jnp.zeros_like(dkv_ref)
            dqg_ref[...] = jnp.zeros_like(dqg_ref)
            dkg_ref[...] = jnp.zeros_like(dkg_ref)

        dqg = jnp.zeros((1, HEAD_DIM), F32)
        dkg = jnp.zeros((1, HEAD_DIM), F32)
        for h in range(MEM_HEADS):
            cols = slice(h * HEAD_DIM, (h + 1) * HEAD_DIM)
            vcols = slice(MEM_Q_W + h * HEAD_DIM, MEM_Q_W + (h + 1) * HEAD_DIM)
            _, vjp = jax.vjp(_mem_head, q_ref[:, cols], kv_ref[:, cols], kv_ref[:, vcols], qg_ref[...], kg_ref[...])
            dq, dk, dv, dqg_h, dkg_h = vjp(dy_ref[:, cols])
            dq_ref[:, cols] = dq
            dkv_ref[:, cols] += dk
            dkv_ref[:, vcols] += dv
            dqg += dqg_h
            dkg += dkg_h
        dqg_ref[...] += dqg
        dkg_ref[...] += dkg

    vec = pl.BlockSpec((1, HEAD_DIM), lambda t: (0, 0))
    full = pl.BlockSpec((m, 2 * MEM_Q_W), lambda t: (0, 0))
    dy_col = (SWA_Q_W + GLA_V_W) // MEM_Q_W
    return pl.pallas_call(
        body, name=name, grid=(s // tq,),
        in_specs=[pl.BlockSpec((tq, MEM_Q_W), lambda t: (t, COL_MQ // MEM_Q_W)), full, vec, vec,
                  pl.BlockSpec((tq, MEM_Q_W), lambda t: (t, dy_col))],
        out_specs=[pl.BlockSpec((tq, MEM_Q_W), lambda t: (t, 0)), full, vec, vec],
        out_shape=[SDS((s, MEM_Q_W), F32), SDS((m, 2 * MEM_Q_W), F32), SDS((1, HEAD_DIM), F32), SDS((1, HEAD_DIM), F32)],
        compiler_params=_params("arbitrary"),
    )(p, kv, qg, kg, dy_all)


GLA_ROWS = 256


def _gla_consts():
    c, h = GLA_CHUNK, GLA_HEADS
    i2 = lax.broadcasted_iota(jnp.int32, (c, c), 0)
    j2 = lax.broadcasted_iota(jnp.int32, (c, c), 1)
    slab_q = lax.broadcasted_iota(jnp.int32, (h, c, GLA_QK_W), 0)
    lane_q = lax.broadcasted_iota(jnp.int32, (h, c, GLA_QK_W), 2)
    row_a = lax.broadcasted_iota(jnp.int32, (h * c, c), 0)
    col_a = lax.broadcasted_iota(jnp.int32, (h * c, c), 1)
    slab_o = lax.broadcasted_iota(jnp.int32, (h, c, GLA_V_W), 0)
    lane_o = lax.broadcasted_iota(jnp.int32, (h, c, GLA_V_W), 2)
    row_s = lax.broadcasted_iota(jnp.int32, (GLA_V_W, GLA_QK_W), 0)
    col_s = lax.broadcasted_iota(jnp.int32, (GLA_V_W, GLA_QK_W), 1)
    return dict(
        ltri=(j2 <= i2).astype(F32),
        m_q=(slab_q == lane_q // GLA_DK).astype(F32),
        causal=col_a <= row_a % c,
        m_o=(slab_o == lane_o // GLA_DV).astype(F32),
        m_s=(row_s // GLA_DV == col_s // GLA_DK).astype(F32),
    )


def _gla_chunk(q, k, v, z, bg, st, c):
    h, n = GLA_HEADS, GLA_CHUNK
    la = _log_sigmoid(z + bg) * (1.0 / GLA_TAU)
    b = _mmf(c["ltri"], la)
    bl = jnp.sum(la, axis=0, keepdims=True)
    qs = q * (GLA_DK ** -0.5)
    kt = k * jnp.exp(bl - b)
    qt = qs * jnp.exp(b - bl)
    qe = qs * jnp.exp(b)
    q_stack = (jnp.broadcast_to(qt[None], (h, n, GLA_QK_W)) * c["m_q"]).reshape(h * n, GLA_QK_W)
    a = jnp.where(c["causal"], _mmf(q_stack, kt, False, True), 0.0)
    o_stack = _mm(a, v)
    o_intra = jnp.sum(o_stack.reshape(h, n, GLA_V_W) * c["m_o"], axis=0)
    o_inter = _mm(qe, st, False, True)
    st_next = st * jnp.exp(bl) + _mm(v, kt, True, False) * c["m_s"]
    return o_intra + o_inter, st_next


def _gla_post(o, gg, gain, g64):
    ms = _mmf(o * o, g64) * (1.0 / GLA_DV)
    return o * lax.rsqrt(ms + EPS) * gain * jax.nn.silu(gg)


def _gla_g64():
    r = lax.broadcasted_iota(jnp.int32, (GLA_V_W, GLA_V_W), 0)
    c = lax.broadcasted_iota(jnp.int32, (GLA_V_W, GLA_V_W), 1)
    return (r // GLA_DV == c // GLA_DV).astype(F32)


def _gla_in_specs(order):
    r = GLA_ROWS
    return [
        pl.BlockSpec((r, GLA_QK_W), lambda t: (order(t), COL_GQ // GLA_QK_W)),
        pl.BlockSpec((r, GLA_QK_W), lambda t: (order(t), COL_GK // GLA_QK_W)),
        pl.BlockSpec((r, GLA_V_W), lambda t: (order(t), COL_GV // GLA_V_W)),
        pl.BlockSpec((r, GLA_V_W), lambda t: (order(t), COL_GG // GLA_V_W)),
        pl.BlockSpec((r, GLA_QK_W), lambda t: (order(t), 0)),
        pl.BlockSpec((1, GLA_QK_W), lambda t: (0, 0)),
        pl.BlockSpec((1, GLA_V_W), lambda t: (0, 0)),
    ]


def _gla_fwd(p, z, bg, gain, *, name):
    s = p.shape[0]
    r = GLA_ROWS
    cps = r // GLA_CHUNK

    def body(q_ref, k_ref, v_ref, gg_ref, z_ref, bg_ref, gain_ref, y_ref, oraw_ref, stsave_ref, st_s):
        @pl.when(pl.program_id(0) == 0)
        def _():
            st_s[...] = jnp.zeros_like(st_s)

        c = _gla_consts()
        st = st_s[...]
        for ci in range(cps):
            rows = slice(ci * GLA_CHUNK, (ci + 1) * GLA_CHUNK)
            stsave_ref[ci] = st
            o, st = _gla_chunk(q_ref[rows, :], k_ref[rows, :], v_ref[rows, :], z_ref[rows, :], bg_ref[...], st, c)
            oraw_ref[rows, :] = o
        st_s[...] = st
        y_ref[...] = _gla_post(oraw_ref[...], gg_ref[...], gain_ref[...], _gla_g64())

    rowv = pl.BlockSpec((r, GLA_V_W), lambda t: (t, 0))
    return pl.pallas_call(
        body, name=name, grid=(s // r,), in_specs=_gla_in_specs(lambda t: t),
        out_specs=[rowv, rowv, pl.BlockSpec((cps, GLA_V_W, GLA_QK_W), lambda t: (t, 0, 0))],
        out_shape=[SDS((s, GLA_V_W), F32), SDS((s, GLA_V_W), F32), SDS((s // GLA_CHUNK, GLA_V_W, GLA_QK_W), F32)],
        scratch_shapes=[pltpu.VMEM((GLA_V_W, GLA_QK_W), F32)],
        compiler_params=_params("arbitrary"),
    )(p, p, p, p, z, bg, gain)


def _gla_bwd(p, z, bg, gain, oraw, stsave, dy_all, *, name):
    s = p.shape[0]
    r = GLA_ROWS
    cps = r // GLA_CHUNK
    nsteps = s // r
    w_qkvg = 2 * GLA_QK_W + 2 * GLA_V_W

    def body(q_ref, k_ref, v_ref, gg_ref, z_ref, bg_ref, gain_ref, oraw_ref, stsave_ref, dy_ref,
             dqkvg_ref, dz_ref, dbg_ref, dgain_ref, dst_s):
        @pl.when(pl.program_id(0) == 0)
        def _():
            dst_s[...] = jnp.zeros_like(dst_s)
            dbg_ref[...] = jnp.zeros_like(dbg_ref)
            dgain_ref[...] = jnp.zeros_like(dgain_ref)

        c = _gla_consts()
        _, vjp = jax.vjp(functools.partial(_gla_post, g64=_gla_g64()), oraw_ref[...], gg_ref[...], gain_ref[...])
        do, dgg, dgain = vjp(dy_ref[...])
        dqkvg_ref[:, 2 * GLA_QK_W + GLA_V_W:] = dgg
        dgain_ref[...] += dgain
        dst = dst_s[...]
        dbg = jnp.zeros((1, GLA_QK_W), F32)
        for ci in reversed(range(cps)):
            rows = slice(ci * GLA_CHUNK, (ci + 1) * GLA_CHUNK)
            _, vjp = jax.vjp(functools.partial(_gla_chunk, c=c), q_ref[rows, :], k_ref[rows, :], v_ref[rows, :],
                             z_ref[rows, :], bg_ref[...], stsave_ref[ci])
            dq, dk, dv, dz, dbg_c, dst = vjp((do[rows, :], dst))
            dqkvg_ref[rows, 0:GLA_QK_W] = dq
            dqkvg_ref[rows, GLA_QK_W:2 * GLA_QK_W] = dk
            dqkvg_ref[rows, 2 * GLA_QK_W:2 * GLA_QK_W + GLA_V_W] = dv
            dz_ref[rows, :] = dz
            dbg += dbg_c
        dst_s[...] = dst
        dbg_ref[...] += dbg

    order = lambda t: nsteps - 1 - t
    rowv = pl.BlockSpec((r, GLA_V_W), lambda t: (order(t), 0))
    return pl.pallas_call(
        body, name=name, grid=(nsteps,),
        in_specs=_gla_in_specs(order) + [
            rowv, pl.BlockSpec((cps, GLA_V_W, GLA_QK_W), lambda t: (order(t), 0, 0)),
            pl.BlockSpec((r, GLA_V_W), lambda t: (order(t), SWA_Q_W // GLA_V_W))],
        out_specs=[pl.BlockSpec((r, w_qkvg), lambda t: (order(t), 0)), pl.BlockSpec((r, GLA_QK_W), lambda t: (order(t), 0)),
                   pl.BlockSpec((1, GLA_QK_W), lambda t: (0, 0)), pl.BlockSpec((1, GLA_V_W), lambda t: (0, 0))],
        out_shape=[SDS((s, w_qkvg), F32), SDS((s, GLA_QK_W), F32), SDS((1, GLA_QK_W), F32), SDS((1, GLA_V_W), F32)],
        scratch_shapes=[pltpu.VMEM((GLA_V_W, GLA_QK_W), F32)],
        compiler_params=_params("arbitrary"),
    )(p, p, p, p, z, bg, gain, oraw, stsave, dy_all)


def _local_step(x, mem, target, small, big):
    g1, gmix, gmem, g2, sqg, skg, sinks, rel_bias, wgu, bg, gla_gain, mqg, mkg = small
    wg1, wu1, wd1, win_p, wkv, wout, wg2, wu2, wd2 = big
    wgu_pad = jnp.zeros((GLA_QK_W, GLA_QK_W), BF16).at[:GLA_RANK].set(wgu.astype(BF16))
    gain256 = jnp.tile(gla_gain, (1, GLA_HEADS))

    x1, saved1 = _ffn_fwd(x, g1, wg1, wu1, wd1, "ffn1")
    h = _rms_fwd(x1, gmix, tm=256, name="mix_rms")
    p = _matmul(h, win_p, tm=512, tn=384, name="mix_in")
    hm = _rms_fwd(mem, gmem, tm=256, name="mem_rms")
    kv = _matmul(hm, wkv, tm=256, tn=512, name="mem_kv")
    p_glr = p[:, COL_GLR:]
    z = _matmul(p_glr, wgu_pad, tm=512, tn=GLA_QK_W, name="gla_gate")
    y_swa = _swa_fwd(p, sqg, skg, sinks, rel_bias, name="swa_fwd")
    y_gla, oraw, stsave = _gla_fwd(p, z, bg, gain256, name="gla_fwd")
    y_mem = _mem_fwd(p, kv, mqg, mkg, tq=256, name="mem_fwd")
    x2 = _matmul([y_swa, y_gla, y_mem], wout, b_blocks=[0, 2, 3], tm=512, tn=512, res=x1, name="mix_out")
    x3, saved2 = _ffn_fwd(x2, g2, wg2, wu2, wd2, "ffn2")

    dy, loss = _loss_bwd(x3, target, tm=256, name="loss")
    dx2, dg2, dwg2, dwu2, dwd2 = _ffn_bwd(dy, x2, g2, wg2, wu2, wd2, saved2, "ffn2")
    dy_all = _matmul(dx2, wout, tb=True, tm=512, tn=512, name="mix_dy")
    dwout = _matmul(jnp.concatenate([y_swa, y_gla, y_mem], axis=1), dx2, ta=True, tm=512, tn=512, name="mix_dw_out")
    dq_swa, dkv_swa, dsqg, dskg, dsink, drb = _swa_bwd(p, sqg, skg, sinks, rel_bias, dy_all, name="swa_bwd")
    dqkvg, dz, dbg, dgain256 = _gla_bwd(p, z, bg, gain256, oraw, stsave, dy_all, name="gla_bwd")
    dmq, dkv_mem, dmqg, dmkg = _mem_bwd(p, kv, mqg, mkg, dy_all, tq=256, name="mem_bwd")
    dglr = _matmul(dz, wgu_pad, tb=True, tm=512, tn=GLA_QK_W, name="gla_gate_dx")
    dwgu_pad = _matmul(p_glr, dz, ta=True, tm=GLA_QK_W, tn=GLA_QK_W, name="gla_gate_dw")
    dp = jnp.concatenate([dq_swa, dkv_swa, dqkvg, dmq, dglr], axis=1)
    dh = _matmul(dp, win_p, tb=True, tm=512, tn=512, name="mix_dh")
    dwin_p = _matmul(h, dp, ta=True, tm=512, tn=384, name="mix_dw_in")
    dx1, dgmix = _rms_bwd(x1, gmix, dh, dx2, tm=256, name="mix_drms")
    dwkv = _matmul(hm, dkv_mem, ta=True, tm=512, tn=512, name="mem_dw_kv")
    dhm = _matmul(dkv_mem, wkv, tb=True, tm=256, tn=512, name="mem_dh")
    _, dgmem = _rms_bwd(mem, gmem, dhm, None, tm=256, name="mem_drms")
    dx, dg1, dwg1, dwu1, dwd1 = _ffn_bwd(dx1, x, g1, wg1, wu1, wd1, saved1, "ffn1")

    dgla_gain = dgain256.reshape(GLA_HEADS, GLA_DV).sum(axis=0, keepdims=True)
    dsmall = (dg1, dgmix, dgmem, dg2, dsqg, dskg, dsink[0, :SWA_HEADS], drb[:, :SWA_HEADS], dwgu_pad[:GLA_RANK], dbg,
              dgla_gain, dmqg, dmkg)
    dbig = (dwg1, dwu1, dwd1, dwin_p, dwkv, dwout, dwg2, dwu2, dwd2)
    return loss, dx, dsmall, dbig


HBM_SPEC = pl.BlockSpec(memory_space=pltpu.HBM)


def _mesh_place():
    x, y, c = lax.axis_index("x"), lax.axis_index("y"), lax.axis_index("c")
    other_chips = [(1 - x, y), (x, 1 - y), (1 - x, 1 - y)]
    return x, y, c, other_chips


def _all_gather(shards, *, name):
    nt = len(shards)

    def body(*refs):
        x_refs, o_refs = refs[:nt], refs[nt:2 * nt]
        send_sems, recv_sems, local_sems = refs[2 * nt:]
        x, y, c, chips = _mesh_place()
        me, sibling = (x, y, c), (x, y, 1 - c)

        def copy(k, t, block, to, from_shard=False):
            bx, by, bc = block
            rows = o_refs[t].at[4 * bx + 2 * by + bc]
            return pltpu.make_async_remote_copy(
                src_ref=x_refs[t] if from_shard else rows, dst_ref=rows,
                send_sem=send_sems.at[k, t], recv_sem=recv_sems.at[k, t], device_id=to, device_id_type=MESH)

        mine = [pltpu.make_async_copy(x_refs[t], o_refs[t].at[4 * x + 2 * y + c], local_sems.at[t]) for t in range(nt)]
        for cp in mine:
            cp.start()
        first = [copy(0, t, me, sibling, True) for t in range(nt)]
        first += [copy(1 + j, t, me, (*chip, c), True) for j, chip in enumerate(chips) for t in range(nt)]
        for cp in first:
            cp.start()
        passed = []
        for j, chip in enumerate(chips):
            for t in range(nt):
                copy(1 + j, t, (*chip, c), me).wait_recv()
                fwd = copy(4 + j, t, (*chip, c), sibling)
                fwd.start()
                passed.append(fwd)
        for t in range(nt):
            copy(0, t, sibling, me).wait_recv()
        for j, chip in enumerate(chips):
            for t in range(nt):
                copy(4 + j, t, (*chip, 1 - c), me).wait_recv()
        for cp in first + passed:
            cp.wait_send()
        for cp in mine:
            cp.wait()

    return pl.pallas_call(
        body, name=name, in_specs=[HBM_SPEC] * nt, out_specs=[HBM_SPEC] * nt,
        out_shape=[SDS((N_DEV,) + s.shape, s.dtype) for s in shards],
        scratch_shapes=[pltpu.SemaphoreType.DMA((7, nt)), pltpu.SemaphoreType.DMA((7, nt)), pltpu.SemaphoreType.DMA((nt,))],
    )(*shards)


def _pair_exchange(parts, *, name):
    nt = len(parts)

    def body(*refs):
        g_refs, o_refs = refs[:nt], refs[nt:2 * nt]
        send_sems, recv_sems = refs[2 * nt:]
        x, y, c, _ = _mesh_place()
        copies = [pltpu.make_async_remote_copy(
            src_ref=g_refs[t].at[1 - c], dst_ref=o_refs[t], send_sem=send_sems.at[t], recv_sem=recv_sems.at[t],
            device_id=(x, y, 1 - c), device_id_type=MESH) for t in range(nt)]
        for cp in copies:
            cp.start()
        for cp in copies:
            cp.wait()

    return pl.pallas_call(
        body, name=name, in_specs=[HBM_SPEC] * nt, out_specs=[HBM_SPEC] * nt,
        out_shape=[SDS(g.shape[1:], g.dtype) for g in parts],
        scratch_shapes=[pltpu.SemaphoreType.DMA((nt,)), pltpu.SemaphoreType.DMA((nt,))],
    )(*parts)


def _chip_exchange(parts, small, *, name):
    nt = len(parts)

    def body(*refs):
        s_refs, small_ref = refs[:nt], refs[nt]
        o_refs, small_all = refs[nt + 1:2 * nt + 1], refs[2 * nt + 1]
        send_sems, recv_sems, small_send, small_recv, local_sem = refs[2 * nt + 2:]
        x, y, c, chips = _mesh_place()

        def copy(j, t, chip):
            return pltpu.make_async_remote_copy(
                src_ref=s_refs[t].at[2 * chip[0] + chip[1]], dst_ref=o_refs[t].at[j],
                send_sem=send_sems.at[j, t], recv_sem=recv_sems.at[j, t], device_id=(*chip, c), device_id_type=MESH)

        flips = [(fx, fy, fc) for fx in (0, 1) for fy in (0, 1) for fc in (0, 1)][1:]

        def small_copy(k):
            fx, fy, fc = flips[k]
            to = (x ^ fx if fx else x, y ^ fy if fy else y, c ^ fc if fc else c)
            rows = small_all.at[4 * x + 2 * y + c]
            return pltpu.make_async_remote_copy(
                src_ref=small_ref, dst_ref=rows, send_sem=small_send.at[k], recv_sem=small_recv.at[k],
                device_id=to, device_id_type=MESH)

        own = pltpu.make_async_copy(small_ref, small_all.at[4 * x + 2 * y + c], local_sem)
        own.start()
        copies = [copy(j, t, chip) for j, chip in enumerate(chips) for t in range(nt)]
        smalls = [small_copy(k) for k in range(7)]
        for cp in smalls + copies:
            cp.start()
        for cp in smalls + copies:
            cp.wait()
        own.wait()

    return pl.pallas_call(
        body, name=name, in_specs=[HBM_SPEC] * (nt + 1), out_specs=[HBM_SPEC] * (nt + 1),
        out_shape=[SDS((3,) + s.shape[1:], s.dtype) for s in parts] + [SDS((N_DEV,) + small.shape, small.dtype)],
        scratch_shapes=[pltpu.SemaphoreType.DMA((3, nt)), pltpu.SemaphoreType.DMA((3, nt)),
                        pltpu.SemaphoreType.DMA((7,)), pltpu.SemaphoreType.DMA((7,)), pltpu.SemaphoreType.DMA],
    )(*parts, small)


def _pair_sum(mine, theirs, c, *, tr, name):
    _, r, l = mine.shape

    def body(c_ref, a_ref, b_ref, o_ref):
        o_ref[...] = (a_ref[0].astype(F32) + b_ref[...].astype(F32)).astype(BF16)

    return pl.pallas_call(
        body, name=name,
        grid_spec=pltpu.PrefetchScalarGridSpec(
            num_scalar_prefetch=1, grid=(r // tr,),
            in_specs=[pl.BlockSpec((1, tr, l), lambda i, c_ref: (c_ref[0], i, 0)),
                      pl.BlockSpec((tr, l), lambda i, c_ref: (i, 0))],
            out_specs=pl.BlockSpec((tr, l), lambda i, c_ref: (i, 0))),
        out_shape=SDS((r, l), BF16), compiler_params=_params("parallel"),
    )(c, mine, theirs)


def _adamw(w, g, m, v):
    m = ADAM_B1 * m + (1.0 - ADAM_B1) * g
    v = ADAM_B2 * v + (1.0 - ADAM_B2) * jnp.square(g)
    m_hat = m / (1.0 - ADAM_B1 ** ADAM_STEP)
    v_hat = v / (1.0 - ADAM_B2 ** ADAM_STEP)
    delta = -ADAM_LR * (m_hat / (jnp.sqrt(v_hat) + ADAM_EPS) + ADAM_WD * w)
    return delta, m, v


def _adam_big(own, others, mat, xy, w, m, v, *, tr, name):
    _, r, l = w.shape

    def body(xy_ref, own_ref, oth_ref, w_ref, m_ref, v_ref, g_out, d_out, m_out, v_out):
        g = own_ref[0, 0].astype(F32)
        for j in range(3):
            g = g + oth_ref[j, 0].astype(F32)
        delta, m_new, v_new = _adamw(w_ref[0], g, m_ref[0], v_ref[0])
        g_out[0] = g
        d_out[0] = delta
        m_out[0] = m_new
        v_out[0] = v_new

    blk = pl.BlockSpec((1, tr, l), lambda i, xy_ref: (0, i, 0))
    return pl.pallas_call(
        body, name=name,
        grid_spec=pltpu.PrefetchScalarGridSpec(
            num_scalar_prefetch=1, grid=(r // tr,),
            in_specs=[pl.BlockSpec((1, 1, tr, l), lambda i, xy_ref: (xy_ref[0], mat, i, 0)),
                      pl.BlockSpec((3, 1, tr, l), lambda i, xy_ref: (0, mat, i, 0)), blk, blk, blk],
            out_specs=[blk, blk, blk, blk]),
        out_shape=[SDS(w.shape, F32)] * 4, compiler_params=_params("parallel"),
    )(xy, own, others, w, m, v)


def _adam_small(g_all, w, m, v, *, name):
    def body(g_ref, w_ref, m_ref, v_ref, g_out, d_out, m_out, v_out):
        g = g_ref[0]
        for k in range(1, N_DEV):
            g = g + g_ref[k]
        delta, m_new, v_new = _adamw(w_ref[...], g, m_ref[...], v_ref[...])
        g_out[...] = g
        d_out[...] = delta
        m_out[...] = m_new
        v_out[...] = v_new

    return pl.pallas_call(body, name=name, out_shape=[SDS(w.shape, F32)] * 4)(g_all, w, m, v)


SMALL_ROWS = 56


def _pack_small(parts):
    flat = jnp.concatenate([a.reshape(-1) for a in parts])
    return jnp.pad(flat, (0, SMALL_ROWS * 128 - flat.shape[0])).reshape(SMALL_ROWS, 128)


def _unpack_small(packed, shapes):
    flat = packed.reshape(-1)
    out, at = [], 0
    for s in shapes:
        n = math.prod(s)
        out.append(flat[at:at + n].reshape(s))
        at += n
    return out


def kernel(x, mem, ffn1_norm, ffn1_w_gate, ffn1_w_up, ffn1_w_down, mix_norm, mem_norm, w_in, w_mem_kv, swa_q_norm, swa_k_norm, swa_sinks, rel_bias, gla_w_gate_up, gla_b_gate, gla_out_norm, mem_q_norm, mem_k_norm, w_out, ffn2_norm, ffn2_w_gate, ffn2_w_up, ffn2_w_down, loss_target, m_ffn1_norm, m_ffn1_w_gate, m_ffn1_w_up, m_ffn1_w_down, m_mix_norm, m_mem_norm, m_w_in, m_w_mem_kv, m_swa_q_norm, m_swa_k_norm, m_swa_sinks, m_rel_bias, m_gla_w_gate_up, m_gla_b_gate, m_gla_out_norm, m_mem_q_norm, m_mem_k_norm, m_w_out, m_ffn2_norm, m_ffn2_w_gate, m_ffn2_w_up, m_ffn2_w_down, v_ffn1_norm, v_ffn1_w_gate, v_ffn1_w_up, v_ffn1_w_down, v_mix_norm, v_mem_norm, v_w_in, v_w_mem_kv, v_swa_q_norm, v_swa_k_norm, v_swa_sinks, v_rel_bias, v_gla_w_gate_up, v_gla_b_gate, v_gla_out_norm, v_mem_q_norm, v_mem_k_norm, v_w_out, v_ffn2_norm, v_ffn2_w_gate, v_ffn2_w_up, v_ffn2_w_down):
    xi, yi, ci = lax.axis_index("x"), lax.axis_index("y"), lax.axis_index("c")
    c_arr = jnp.reshape(ci, (1,)).astype(jnp.int32)
    xy_arr = jnp.reshape(2 * xi + yi, (1,)).astype(jnp.int32)
    d = x.shape[-1]

    col_f = [ffn1_w_gate, ffn1_w_up, ffn2_w_gate, ffn2_w_up]
    row_f = [ffn1_w_down, ffn2_w_down]
    shards = [jnp.concatenate(col_f, axis=0).astype(BF16), jnp.concatenate(row_f, axis=0).astype(BF16),
              w_in[0].astype(BF16), w_mem_kv[0].astype(BF16), w_out[0].astype(BF16)]
    col_all, row_all, win_all, wkv_all, wout_all = _all_gather(shards, name="gather_weights")

    def col_full(a):
        return a.transpose(1, 0, 2).reshape(a.shape[1], -1)

    wg1, wu1, wg2, wu2 = (col_full(col_all[:, k]) for k in range(4))
    wd1, wd2 = (row_all[:, k].reshape(-1, d) for k in range(2))
    win = col_full(win_all)
    glr_lo, glr_hi = COL_MQ, COL_MQ + GLA_RANK
    win_p = jnp.concatenate([win[:, :glr_lo], win[:, glr_hi:], win[:, glr_lo:glr_hi],
                             jnp.zeros((d, IN_W_PAD - IN_W), BF16)], axis=1)
    wkv = wkv_all.reshape(-1, wkv_all.shape[-1])
    wout = wout_all.reshape(-1, d)

    small_w = [ffn1_norm, mix_norm, mem_norm, ffn2_norm, swa_q_norm, swa_k_norm, swa_sinks[0], rel_bias,
               gla_w_gate_up[0], gla_b_gate, gla_out_norm, mem_q_norm, mem_k_norm]
    loss, grad_x, dsmall, dbig = _local_step(x[0], mem[0], loss_target[0], small_w,
                                             (wg1, wu1, wd1, win_p, wkv, wout, wg2, wu2, wd2))
    dwg1, dwu1, dwd1, dwin_p, dwkv, dwout, dwg2, dwu2, dwd2 = dbig
    loss = lax.psum(loss[0, 0], ("x", "y", "c"))

    def col_dst(g):
        w = g.shape[1] // N_DEV
        return g.reshape(g.shape[0], 4, 2, w).transpose(2, 1, 0, 3).astype(BF16)

    def row_dst(g):
        return g.reshape(4, 2, g.shape[0] // N_DEV, g.shape[1]).transpose(1, 0, 2, 3).astype(BF16)

    dwin = jnp.concatenate([dwin_p[:, :glr_lo], dwin_p[:, COL_GLR:COL_GLR + GLA_RANK], dwin_p[:, COL_MQ:COL_GLR]], axis=1)
    parts = [jnp.stack([col_dst(g) for g in (dwg1, dwu1, dwg2, dwu2)], axis=2),
             jnp.stack([row_dst(g) for g in (dwd1, dwd2)], axis=2),
             col_dst(dwin)[:, :, None], row_dst(dwkv)[:, :, None], row_dst(dwout)[:, :, None]]
    from_sibling = _pair_exchange(parts, name="pair_exchange")
    chip_sums = []
    for t, (mine, theirs) in enumerate(zip(parts, from_sibling)):
        l = mine.shape[-1]
        flat = _pair_sum(mine.reshape(2, -1, l), theirs.reshape(-1, l), c_arr, tr=256, name=f"pair_sum_{t}")
        chip_sums.append(flat.reshape(mine.shape[1:]))
    small_shapes = [a.shape for a in small_w]
    *from_chips, small_all = _chip_exchange(chip_sums, _pack_small(dsmall), name="chip_exchange")

    big_w = [(0, 0, ffn1_w_gate, m_ffn1_w_gate, v_ffn1_w_gate), (0, 1, ffn1_w_up, m_ffn1_w_up, v_ffn1_w_up),
             (1, 0, ffn1_w_down, m_ffn1_w_down, v_ffn1_w_down), (2, 0, w_in, m_w_in, v_w_in),
             (3, 0, w_mem_kv, m_w_mem_kv, v_w_mem_kv), (4, 0, w_out, m_w_out, v_w_out),
             (0, 2, ffn2_w_gate, m_ffn2_w_gate, v_ffn2_w_gate), (0, 3, ffn2_w_up, m_ffn2_w_up, v_ffn2_w_up),
             (1, 1, ffn2_w_down, m_ffn2_w_down, v_ffn2_w_down)]
    big_names = ["ffn1_w_gate", "ffn1_w_up", "ffn1_w_down", "w_in", "w_mem_kv", "w_out", "ffn2_w_gate", "ffn2_w_up",
                 "ffn2_w_down"]
    res = {}
    for nm, (t, mat, w, m, v) in zip(big_names, big_w):
        r = w.shape[1]
        tr = 256 if r % 256 == 0 else r
        res[nm] = _adam_big(chip_sums[t], from_chips[t], mat, xy_arr, w, m, v, tr=tr, name=f"adam_{nm}")
    small_names = ["ffn1_norm", "mix_norm", "mem_norm", "ffn2_norm", "swa_q_norm", "swa_k_norm", "swa_sinks", "rel_bias",
                   "gla_w_gate_up", "gla_b_gate", "gla_out_norm", "mem_q_norm", "mem_k_norm"]
    small_m = [m_ffn1_norm, m_mix_norm, m_mem_norm, m_ffn2_norm, m_swa_q_norm, m_swa_k_norm, m_swa_sinks, m_rel_bias,
               m_gla_w_gate_up, m_gla_b_gate, m_gla_out_norm, m_mem_q_norm, m_mem_k_norm]
    small_v = [v_ffn1_norm, v_mix_norm, v_mem_norm, v_ffn2_norm, v_swa_q_norm, v_swa_k_norm, v_swa_sinks, v_rel_bias,
               v_gla_w_gate_up, v_gla_b_gate, v_gla_out_norm, v_mem_q_norm, v_mem_k_norm]
    small_full = [ffn1_norm, mix_norm, mem_norm, ffn2_norm, swa_q_norm, swa_k_norm, swa_sinks, rel_bias,
                  gla_w_gate_up, gla_b_gate, gla_out_norm, mem_q_norm, mem_k_norm]
    packed = _adam_small(small_all, _pack_small(small_full), _pack_small(small_m), _pack_small(small_v), name="adam_small")
    full_shapes = [a.shape for a in small_full]
    unpacked = [_unpack_small(pk, full_shapes) for pk in packed]
    for k, nm in enumerate(small_names):
        res[nm] = [unpacked[q][k] for q in range(4)]

    order = ["ffn1_norm", "ffn1_w_gate", "ffn1_w_up", "ffn1_w_down", "mix_norm", "mem_norm", "w_in", "w_mem_kv",
             "swa_q_norm", "swa_k_norm", "swa_sinks", "rel_bias", "gla_w_gate_up", "gla_b_gate", "gla_out_norm",
             "mem_q_norm", "mem_k_norm", "w_out", "ffn2_norm", "ffn2_w_gate", "ffn2_w_up", "ffn2_w_down"]
    outs = [loss, grad_x[None]]
    for q in range(4):
        outs += [res[nm][q] for nm in order]
    return tuple(outs)
```

```python
import functools
import math

import numpy as np
import jax
import jax.numpy as jnp
from jax import lax
from jax.experimental import pallas as pl
from jax.experimental.pallas import tpu as pltpu

F32 = jnp.float32
BF16 = jnp.bfloat16
SDS = jax.ShapeDtypeStruct

EPS = 1e-6
HEAD_DIM = 64
SWA_HEADS = 8
SWA_KV_HEADS = 2
SWA_GROUP = SWA_HEADS // SWA_KV_HEADS
BLOCK = 128
N_BUCKETS = 32
MAX_DISTANCE = 128
GLA_HEADS = 4
GLA_DK = 32
GLA_DV = 64
GLA_RANK = 16
GLA_TAU = 16.0
GLA_CHUNK = 32
MEM_HEADS = 4
SWA_Q_W = SWA_HEADS * HEAD_DIM
SWA_KV_W = SWA_KV_HEADS * HEAD_DIM
GLA_QK_W = GLA_HEADS * GLA_DK
GLA_V_W = GLA_HEADS * GLA_DV
MEM_Q_W = MEM_HEADS * HEAD_DIM
IN_W = 1808
IN_W_PAD = 1920
COL_SQ, COL_SKV, COL_GQ, COL_GK, COL_GV, COL_GG, COL_MQ, COL_GLR = 0, 512, 768, 896, 1024, 1280, 1536, 1792

ADAM_LR = 0.001
ADAM_B1 = 0.9
ADAM_B2 = 0.999
ADAM_EPS = 1e-08
ADAM_WD = 0.01
ADAM_STEP = 10

N_DEV = 8
VMEM_LIMIT_BYTES = 56 * 1024 * 1024
MESH = pl.DeviceIdType.MESH


def _params(*sem):
    return pltpu.CompilerParams(dimension_semantics=sem or None, vmem_limit_bytes=VMEM_LIMIT_BYTES)


def _dot(a, b, ta, tb, precision=None):
    dims = (((0 if ta else 1,), (1 if tb else 0,)), ((), ()))
    return lax.dot_general(a, b, dims, preferred_element_type=F32, precision=precision)


def _mm_raw(a, b, ta=False, tb=False):
    return _dot(a.astype(BF16), b.astype(BF16), ta, tb)


def _mmf_raw(a, b, ta=False, tb=False):
    return _dot(a, b, ta, tb, lax.Precision.HIGHEST)


def _make_mm(raw):
    @functools.partial(jax.custom_vjp, nondiff_argnums=(2, 3))
    def mm(a, b, ta=False, tb=False):
        return raw(a, b, ta, tb)

    def fwd(a, b, ta, tb):
        return raw(a, b, ta, tb), (a, b)

    def bwd(ta, tb, res, g):
        a, b = res
        da = raw(b, g, tb, True) if ta else raw(g, b, False, not tb)
        db = raw(g, a, True, ta) if tb else raw(a, g, not ta, False)
        return da, db

    mm.defvjp(fwd, bwd)
    return mm


_mm = _make_mm(_mm_raw)
_mmf = _make_mm(_mmf_raw)


def _rms(x, g):
    return x * lax.rsqrt(jnp.mean(x * x, axis=-1, keepdims=True) + EPS) * g


def _silu_mul(g, u):
    return jax.nn.silu(g) * u


def _log_sigmoid(z):
    return jnp.minimum(z, 0.0) - jnp.log(1.0 + jnp.exp(-jnp.abs(z)))


def _matmul(a_list, b, *, ta=False, tb=False, tm, tn, b_blocks=None, res=None, scale=1.0, out_dtype=F32, name):
    if not isinstance(a_list, (list, tuple)):
        a_list = [a_list]
    n_a = len(a_list)
    m = a_list[0].shape[1] if ta else a_list[0].shape[0]
    ks = [a.shape[0] if ta else a.shape[1] for a in a_list]
    n = b.shape[0] if tb else b.shape[1]
    if b_blocks is None:
        assert n_a == 1
        b_blocks = [0]
    assert m % tm == 0 and n % tn == 0, (m, n, tm, tn)

    def body(*refs):
        a_refs, b_refs = refs[:n_a], refs[n_a:2 * n_a]
        r_ref = refs[2 * n_a] if res is not None else None
        o_ref = refs[-1]
        acc = _mm_raw(a_refs[0][...], b_refs[0][...], ta, tb)
        for k in range(1, n_a):
            acc = acc + _mm_raw(a_refs[k][...], b_refs[k][...], ta, tb)
        if scale != 1.0:
            acc = acc * scale
        if r_ref is not None:
            acc = r_ref[...] + acc
        o_ref[...] = acc.astype(out_dtype)

    in_specs = []
    for k in ks:
        in_specs.append(pl.BlockSpec((k, tm), lambda i, j: (0, i)) if ta else pl.BlockSpec((tm, k), lambda i, j: (i, 0)))
    for k, blk in zip(ks, b_blocks):
        if tb:
            in_specs.append(pl.BlockSpec((tn, k), functools.partial(lambda i, j, blk: (j, blk), blk=blk)))
        else:
            in_specs.append(pl.BlockSpec((k, tn), functools.partial(lambda i, j, blk: (blk, j), blk=blk)))
    args = list(a_list) + [b] * n_a
    if res is not None:
        in_specs.append(pl.BlockSpec((tm, tn), lambda i, j: (i, j)))
        args.append(res)
    return pl.pallas_call(
        body, name=name, grid=(m // tm, n // tn), in_specs=in_specs,
        out_specs=pl.BlockSpec((tm, tn), lambda i, j: (i, j)), out_shape=SDS((m, n), out_dtype),
        compiler_params=_params("parallel", "parallel"),
    )(*args)


def _rms_fwd(x, g, *, tm, name):
    s, d = x.shape

    def body(x_ref, g_ref, h_ref):
        h_ref[...] = _rms(x_ref[...], g_ref[...]).astype(BF16)

    return pl.pallas_call(
        body, name=name, grid=(s // tm,),
        in_specs=[pl.BlockSpec((tm, d), lambda i: (i, 0)), pl.BlockSpec((1, d), lambda i: (0, 0))],
        out_specs=pl.BlockSpec((tm, d), lambda i: (i, 0)), out_shape=SDS((s, d), BF16),
        compiler_params=_params("parallel"),
    )(x, g)


def _rms_bwd(x, g, dh, dres, *, tm, name):
    s, d = x.shape
    want_dx = dres is not None

    def body(*refs):
        if want_dx:
            x_ref, g_ref, dh_ref, dres_ref, dx_ref, dxb_ref, dg_ref = refs
        else:
            x_ref, g_ref, dh_ref, dg_ref = refs
        _, vjp = jax.vjp(_rms, x_ref[...], g_ref[...])
        dx, dg = vjp(dh_ref[...])
        if want_dx:
            dx = dres_ref[...] + dx
            dx_ref[...] = dx
            dxb_ref[...] = dx.astype(BF16)

        @pl.when(pl.program_id(0) == 0)
        def _():
            dg_ref[...] = jnp.zeros_like(dg_ref)

        dg_ref[...] += dg

    row = pl.BlockSpec((tm, d), lambda i: (i, 0))
    vec = pl.BlockSpec((1, d), lambda i: (0, 0))
    if want_dx:
        return pl.pallas_call(
            body, name=name, grid=(s // tm,), in_specs=[row, vec, row, row], out_specs=[row, row, vec],
            out_shape=[SDS((s, d), F32), SDS((s, d), BF16), SDS((1, d), F32)], compiler_params=_params("arbitrary"),
        )(x, g, dh, dres)
    return None, None, pl.pallas_call(
        body, name=name, grid=(s // tm,), in_specs=[row, vec, row], out_specs=vec,
        out_shape=SDS((1, d), F32), compiler_params=_params("arbitrary"),
    )(x, g, dh)


FFN_TN = 256


def _ffn_fwd(x, gain, wg, wu, wd, tag, *, tm=1024):
    s, d = x.shape
    f = wg.shape[1]
    tn = FFN_TN
    nj = f // tn
    tm = min(tm, s)

    def body(x_ref, gain_ref, wg_ref, wu_ref, wd_ref, y_ref, h_ref, g_ref, u_ref, acc_s):
        j = pl.program_id(1)

        @pl.when(j == 0)
        def _():
            h_ref[...] = _rms(x_ref[...], gain_ref[...]).astype(BF16)
            acc_s[...] = jnp.zeros_like(acc_s)

        hv = h_ref[...]
        g = _mm_raw(hv, wg_ref[...])
        u = _mm_raw(hv, wu_ref[...])
        g_ref[...] = g.astype(BF16)
        u_ref[...] = u.astype(BF16)
        acc_s[...] += _mm_raw(_silu_mul(g, u), wd_ref[...])

        @pl.when(j == nj - 1)
        def _():
            y_ref[...] = x_ref[...] + 0.5 * acc_s[...]

    row = pl.BlockSpec((tm, d), lambda i, j: (i, 0))
    tile = pl.BlockSpec((tm, tn), lambda i, j: (i, j))
    wcol = pl.BlockSpec((d, tn), lambda i, j: (0, j))
    y, h, g, u = pl.pallas_call(
        body, name=f"{tag}_fwd", grid=(s // tm, nj),
        in_specs=[row, pl.BlockSpec((1, d), lambda i, j: (0, 0)), wcol, wcol, pl.BlockSpec((tn, d), lambda i, j: (j, 0))],
        out_specs=[row, row, tile, tile],
        out_shape=[SDS((s, d), F32), SDS((s, d), BF16), SDS((s, f), BF16), SDS((s, f), BF16)],
        scratch_shapes=[pltpu.VMEM((tm, d), F32)],
        compiler_params=_params("parallel", "arbitrary"),
    )(x, gain, wg, wu, wd)
    return y, (h, g, u)


def _ffn_bwd(dy, dyb, x, gain, wg, wu, wd, saved, tag):
    h, g, u = saved
    s, d = x.shape
    f = wg.shape[1]
    tn = FFN_TN

    def body(dy_ref, ht_ref, wg_ref, wu_ref, wd_ref, g_ref, u_ref, dh_ref, dwg_ref, dwu_ref, dwd_ref):
        @pl.when(pl.program_id(0) == 0)
        def _():
            dh_ref[...] = jnp.zeros_like(dh_ref)

        dyv = dy_ref[...]
        da = _mm_raw(dyv, wd_ref[...], False, True) * 0.5
        a, vjp = jax.vjp(_silu_mul, g_ref[...].astype(F32), u_ref[...].astype(F32))
        dg, du = vjp(da)
        dg = dg.astype(BF16)
        du = du.astype(BF16)
        dh_ref[...] += _mm_raw(dg, wg_ref[...], False, True) + _mm_raw(du, wu_ref[...], False, True)
        dwg_ref[...] = _mm_raw(ht_ref[...], dg)
        dwu_ref[...] = _mm_raw(ht_ref[...], du)
        dwd_ref[...] = _mm_raw(a, dyv, True, False) * 0.5

    full = pl.BlockSpec((s, d), lambda j: (0, 0))
    tile = pl.BlockSpec((s, tn), lambda j: (0, j))
    wcol = pl.BlockSpec((d, tn), lambda j: (0, j))
    wrow = pl.BlockSpec((tn, d), lambda j: (j, 0))
    dh, dwg, dwu, dwd = pl.pallas_call(
        body, name=f"{tag}_bwd", grid=(f // tn,),
        in_specs=[full, pl.BlockSpec((d, s), lambda j: (0, 0)), wcol, wcol, wrow, tile, tile],
        out_specs=[full, wcol, wcol, wrow],
        out_shape=[SDS((s, d), F32), SDS((d, f), F32), SDS((d, f), F32), SDS((f, d), F32)],
        compiler_params=_params("arbitrary"),
    )(dyb, h.T, wg, wu, wd, g, u)
    dx, dxb, dgain = _rms_bwd(x, gain, dh, dy, tm=256, name=f"{tag}_drms")
    return dx, dxb, dgain, dwg, dwu, dwd


def _loss_bwd(y, target, *, tm, name):
    s, d = y.shape

    def body(y_ref, t_ref, dy_ref, dyb_ref, l_ref):
        diff = y_ref[...] - t_ref[...]
        dy_ref[...] = diff * (1.0 / d)
        dyb_ref[...] = (diff * (1.0 / d)).astype(BF16)

        @pl.when(pl.program_id(0) == 0)
        def _():
            l_ref[...] = jnp.zeros_like(l_ref)

        l_ref[...] += 0.5 * jnp.sum(jnp.mean(diff * diff, axis=-1, keepdims=True), axis=0, keepdims=True)

    row = pl.BlockSpec((tm, d), lambda i: (i, 0))
    return pl.pallas_call(
        body, name=name, grid=(s // tm,), in_specs=[row, row],
        out_specs=[row, row, pl.BlockSpec((1, 1), lambda i: (0, 0))],
        out_shape=[SDS((s, d), F32), SDS((s, d), BF16), SDS((1, 1), F32)],
        compiler_params=_params("arbitrary"),
    )(y, target)


def _bucket_table():
    qi = np.arange(BLOCK)[:, None]
    kj = np.arange(2 * BLOCK)[None, :]
    dist = np.maximum(qi + BLOCK - kj, 0)
    max_exact = N_BUCKETS // 2
    d = np.maximum(dist, 1).astype(np.float32)
    large = max_exact + (np.log(d / np.float32(max_exact)) / np.float32(math.log(MAX_DISTANCE / max_exact))
                         * np.float32(N_BUCKETS - max_exact)).astype(np.int32)
    large = np.minimum(large, N_BUCKETS - 1)
    return np.where(dist < max_exact, dist, large).astype(np.int32)


def _swa_valid(n):
    qi = lax.broadcasted_iota(jnp.int32, (BLOCK, 2 * BLOCK), 0)
    kj = lax.broadcasted_iota(jnp.int32, (BLOCK, 2 * BLOCK), 1)
    dist = qi + BLOCK - kj
    return (dist >= 0) & (dist < BLOCK) & ((kj >= BLOCK) | (n > 0))


def _swa_head(q, kb, vb, qg, kg, sink, bias, valid):
    qn = _rms(q, qg)
    kn = _rms(kb, kg)
    s = _mm(qn, kn, False, True) * (HEAD_DIM ** -0.5) + bias
    s = jnp.where(valid, s, -jnp.inf)
    m = lax.stop_gradient(jnp.maximum(jnp.max(s, axis=-1, keepdims=True), sink))
    p = jnp.exp(s - m)
    p = p / (jnp.sum(p, axis=-1, keepdims=True) + jnp.exp(sink - m))
    return _mm(p, vb)


def _swa_bias_table(rb_ref, bucket, bias_s):
    for h in range(SWA_HEADS):
        acc = jnp.zeros((BLOCK, 2 * BLOCK), F32)
        for b in range(N_BUCKETS):
            acc = jnp.where(bucket == b, rb_ref[b, h], acc)
        bias_s[h] = acc


def _swa_band(kvp_ref, kvc_ref, g):
    lo = g * HEAD_DIM
    kb = jnp.concatenate([kvp_ref[:, lo:lo + HEAD_DIM], kvc_ref[:, lo:lo + HEAD_DIM]], axis=0)
    lo += SWA_KV_W
    vb = jnp.concatenate([kvp_ref[:, lo:lo + HEAD_DIM], kvc_ref[:, lo:lo + HEAD_DIM]], axis=0)
    return kb, vb


def _swa_specs(order):
    kvc = COL_SKV // (2 * SWA_KV_W)
    return [
        pl.BlockSpec((BLOCK, SWA_Q_W), lambda t: (order(t), 0)),
        pl.BlockSpec((BLOCK, 2 * SWA_KV_W), lambda t: (jnp.maximum(order(t) - 1, 0), kvc)),
        pl.BlockSpec((BLOCK, 2 * SWA_KV_W), lambda t: (order(t), kvc)),
        pl.BlockSpec((1, HEAD_DIM), lambda t: (0, 0)),
        pl.BlockSpec((1, HEAD_DIM), lambda t: (0, 0)),
        pl.BlockSpec(memory_space=pltpu.SMEM),
        pl.BlockSpec(memory_space=pltpu.SMEM),
        pl.BlockSpec((BLOCK, 2 * BLOCK), lambda t: (0, 0)),
    ]


def _swa_fwd(p, qg, kg, sinks, rel_bias, *, name):
    s = p.shape[0]
    nb = s // BLOCK

    def body(q_ref, kvp_ref, kvc_ref, qg_ref, kg_ref, sink_ref, rb_ref, bucket_ref, y_ref, bias_s):
        n = pl.program_id(0)

        @pl.when(n == 0)
        def _():
            _swa_bias_table(rb_ref, bucket_ref[...], bias_s)

        valid = _swa_valid(n)
        for g in range(SWA_KV_HEADS):
            kb, vb = _swa_band(kvp_ref, kvc_ref, g)
            for hh in range(SWA_GROUP):
                h = g * SWA_GROUP + hh
                cols = slice(h * HEAD_DIM, (h + 1) * HEAD_DIM)
                sink = jnp.full((1, 1), sink_ref[h], F32)
                y_ref[:, cols] = _swa_head(q_ref[:, cols], kb, vb, qg_ref[...], kg_ref[...], sink, bias_s[h], valid)

    return pl.pallas_call(
        body, name=name, grid=(nb,), in_specs=_swa_specs(lambda t: t),
        out_specs=pl.BlockSpec((BLOCK, SWA_Q_W), lambda t: (t, 0)), out_shape=SDS((s, SWA_Q_W), F32),
        scratch_shapes=[pltpu.VMEM((SWA_HEADS, BLOCK, 2 * BLOCK), F32)],
        compiler_params=_params("arbitrary"),
    )(p, p, p, qg, kg, sinks, rel_bias, jnp.asarray(_bucket_table()))


def _swa_bwd(p, qg, kg, sinks, rel_bias, dy_all, *, name):
    s = p.shape[0]
    nb = s // BLOCK

    def body(q_ref, kvp_ref, kvc_ref, qg_ref, kg_ref, sink_ref, rb_ref, bucket_ref, dy_ref,
             dq_ref, dkv_ref, dqg_ref, dkg_ref, dsink_ref, drb_ref, bias_s, dbias_s, carry_s):
        t = pl.program_id(0)
        n = nb - 1 - t

        @pl.when(t == 0)
        def _():
            _swa_bias_table(rb_ref, bucket_ref[...], bias_s)
            dbias_s[...] = jnp.zeros_like(dbias_s)
            carry_s[...] = jnp.zeros_like(carry_s)
            dqg_ref[...] = jnp.zeros_like(dqg_ref)
            dkg_ref[...] = jnp.zeros_like(dkg_ref)
            dsink_ref[...] = jnp.zeros_like(dsink_ref)
            drb_ref[...] = jnp.zeros_like(drb_ref)

        valid = _swa_valid(n)
        lane = lax.broadcasted_iota(jnp.int32, (1, BLOCK), 1)
        dqg = jnp.zeros((1, HEAD_DIM), F32)
        dkg = jnp.zeros((1, HEAD_DIM), F32)
        dsink_vec = jnp.zeros((1, BLOCK), F32)
        for g in range(SWA_KV_HEADS):
            kb, vb = _swa_band(kvp_ref, kvc_ref, g)
            dkb = jnp.zeros((2 * BLOCK, HEAD_DIM), F32)
            dvb = jnp.zeros((2 * BLOCK, HEAD_DIM), F32)
            for hh in range(SWA_GROUP):
                h = g * SWA_GROUP + hh
                cols = slice(h * HEAD_DIM, (h + 1) * HEAD_DIM)
                sink = jnp.full((1, 1), sink_ref[h], F32)
                _, vjp = jax.vjp(functools.partial(_swa_head, valid=valid),
                                 q_ref[:, cols], kb, vb, qg_ref[...], kg_ref[...], sink, bias_s[h])
                dq, dkb_h, dvb_h, dqg_h, dkg_h, dsink_h, dbias_h = vjp(dy_ref[:, cols])
                dq_ref[:, cols] = dq
                dkb += dkb_h
                dvb += dvb_h
                dqg += dqg_h
                dkg += dkg_h
                dsink_vec += jnp.where(lane == h, dsink_h, 0.0)
                dbias_s[h] += dbias_h
            lo = g * HEAD_DIM
            dkv_ref[:, lo:lo + HEAD_DIM] = dkb[BLOCK:] + carry_s[g]
            carry_s[g] = dkb[:BLOCK]
            lo += SWA_KV_W
            dkv_ref[:, lo:lo + HEAD_DIM] = dvb[BLOCK:] + carry_s[SWA_KV_HEADS + g]
            carry_s[SWA_KV_HEADS + g] = dvb[:BLOCK]
        dqg_ref[...] += dqg
        dkg_ref[...] += dkg
        dsink_ref[...] += dsink_vec

        @pl.when(t == nb - 1)
        def _():
            bucket = bucket_ref[...]
            row = lax.broadcasted_iota(jnp.int32, (N_BUCKETS, BLOCK), 0)
            col = lax.broadcasted_iota(jnp.int32, (N_BUCKETS, BLOCK), 1)
            acc = jnp.zeros((N_BUCKETS, BLOCK), F32)
            for h in range(SWA_HEADS):
                dbias = dbias_s[h]
                for b in range(N_BUCKETS):
                    part = jnp.sum(jnp.where(bucket == b, dbias, 0.0), axis=1, keepdims=True)
                    val = jnp.sum(part, axis=0, keepdims=True)
                    acc = acc + jnp.where((row == b) & (col == h), val, 0.0)
            drb_ref[...] = acc

    order = lambda t: nb - 1 - t
    vec = pl.BlockSpec((1, HEAD_DIM), lambda t: (0, 0))
    return pl.pallas_call(
        body, name=name, grid=(nb,),
        in_specs=_swa_specs(order) + [pl.BlockSpec((BLOCK, SWA_Q_W), lambda t: (order(t), 0))],
        out_specs=[pl.BlockSpec((BLOCK, SWA_Q_W), lambda t: (order(t), 0)),
                   pl.BlockSpec((BLOCK, 2 * SWA_KV_W), lambda t: (order(t), 0)),
                   vec, vec, pl.BlockSpec((1, BLOCK), lambda t: (0, 0)),
                   pl.BlockSpec((N_BUCKETS, BLOCK), lambda t: (0, 0))],
        out_shape=[SDS((s, SWA_Q_W), F32), SDS((s, 2 * SWA_KV_W), F32), SDS((1, HEAD_DIM), F32),
                   SDS((1, HEAD_DIM), F32), SDS((1, BLOCK), F32), SDS((N_BUCKETS, BLOCK), F32)],
        scratch_shapes=[pltpu.VMEM((SWA_HEADS, BLOCK, 2 * BLOCK), F32), pltpu.VMEM((SWA_HEADS, BLOCK, 2 * BLOCK), F32),
                        pltpu.VMEM((2 * SWA_KV_HEADS, BLOCK, HEAD_DIM), F32)],
        compiler_params=_params("arbitrary"),
    )(p, p, p, qg, kg, sinks, rel_bias, jnp.asarray(_bucket_table()), dy_all)


def _mem_head(q, k, v, qg, kg):
    qn = _rms(q, qg)
    kn = _rms(k, kg)
    s = _mm(qn, kn, False, True) * (HEAD_DIM ** -0.5)
    m = lax.stop_gradient(jnp.max(s, axis=-1, keepdims=True))
    e = jnp.exp(s - m)
    return _mm(e / jnp.sum(e, axis=-1, keepdims=True), v)


def _mem_fwd(p, kv, qg, kg, *, tq, name):
    s = p.shape[0]
    m = kv.shape[0]

    def body(q_ref, kv_ref, qg_ref, kg_ref, y_ref):
        for h in range(MEM_HEADS):
            cols = slice(h * HEAD_DIM, (h + 1) * HEAD_DIM)
            vcols = slice(MEM_Q_W + h * HEAD_DIM, MEM_Q_W + (h + 1) * HEAD_DIM)
            y_ref[:, cols] = _mem_head(q_ref[:, cols], kv_ref[:, cols], kv_ref[:, vcols], qg_ref[...], kg_ref[...])

    vec = pl.BlockSpec((1, HEAD_DIM), lambda t: (0, 0))
    return pl.pallas_call(
        body, name=name, grid=(s // tq,),
        in_specs=[pl.BlockSpec((tq, MEM_Q_W), lambda t: (t, COL_MQ // MEM_Q_W)),
                  pl.BlockSpec((m, 2 * MEM_Q_W), lambda t: (0, 0)), vec, vec],
        out_specs=pl.BlockSpec((tq, MEM_Q_W), lambda t: (t, 0)), out_shape=SDS((s, MEM_Q_W), F32),
        compiler_params=_params("parallel"),
    )(p, kv, qg, kg)


def _mem_bwd(p, kv, qg, kg, dy_all, *, tq, name):
    s = p.shape[0]
    m = kv.shape[0]

    def body(q_ref, kv_ref, qg_ref, kg_ref, dy_ref, dq_ref, dkv_ref, dqg_ref, dkg_ref):
        @pl.when(pl.program_id(0) == 0)
        def _():
            dkv_ref[...] = jnp.zeros_like(dkv_ref)
            dqg_ref[...] = jnp.zeros_like(dqg_ref)
            dkg_ref[...] = jnp.zeros_like(dkg_ref)

        dqg = jnp.zeros((1, HEAD_DIM), F32)
        dkg = jnp.zeros((1, HEAD_DIM), F32)
        for h in range(MEM_HEADS):
            cols = slice(h * HEAD_DIM, (h + 1) * HEAD_DIM)
            vcols = slice(MEM_Q_W + h * HEAD_DIM, MEM_Q_W + (h + 1) * HEAD_DIM)
            _, vjp = jax.vjp(_mem_head, q_ref[:, cols], kv_ref[:, cols], kv_ref[:, vcols], qg_ref[...], kg_ref[...])
            dq, dk, dv, dqg_h, dkg_h = vjp(dy_ref[:, cols])
            dq_ref[:, cols] = dq
            dkv_ref[:, cols] += dk
            dkv_ref[:, vcols] += dv
            dqg += dqg_h
            dkg += dkg_h
        dqg_ref[...] += dqg
        dkg_ref[...] += dkg

    vec = pl.BlockSpec((1, HEAD_DIM), lambda t: (0, 0))
    full = pl.BlockSpec((m, 2 * MEM_Q_W), lambda t: (0, 0))
    dy_col = (SWA_Q_W + GLA_V_W) // MEM_Q_W
    return pl.pallas_call(
        body, name=name, grid=(s // tq,),
        in_specs=[pl.BlockSpec((tq, MEM_Q_W), lambda t: (t, COL_MQ // MEM_Q_W)), full, vec, vec,
                  pl.BlockSpec((tq, MEM_Q_W), lambda t: (t, dy_col))],
        out_specs=[pl.BlockSpec((tq, MEM_Q_W), lambda t: (t, 0)), full, vec, vec],
        out_shape=[SDS((s, MEM_Q_W), F32), SDS((m, 2 * MEM_Q_W), F32), SDS((1, HEAD_DIM), F32), SDS((1, HEAD_DIM), F32)],
        compiler_params=_params("arbitrary"),
    )(p, kv, qg, kg, dy_all)


GLA_ROWS = 256


def _gla_consts():
    c, h = GLA_CHUNK, GLA_HEADS
    i2 = lax.broadcasted_iota(jnp.int32, (c, c), 0)
    j2 = lax.broadcasted_iota(jnp.int32, (c, c), 1)
    slab_q = lax.broadcasted_iota(jnp.int32, (h, c, GLA_QK_W), 0)
    lane_q = lax.broadcasted_iota(jnp.int32, (h, c, GLA_QK_W), 2)
    row_a = lax.broadcasted_iota(jnp.int32, (h * c, c), 0)
    col_a = lax.broadcasted_iota(jnp.int32, (h * c, c), 1)
    slab_o = lax.broadcasted_iota(jnp.int32, (h, c, GLA_V_W), 0)
    lane_o = lax.broadcasted_iota(jnp.int32, (h, c, GLA_V_W), 2)
    row_s = lax.broadcasted_iota(jnp.int32, (GLA_V_W, GLA_QK_W), 0)
    col_s = lax.broadcasted_iota(jnp.int32, (GLA_V_W, GLA_QK_W), 1)
    return dict(
        ltri=(j2 <= i2).astype(F32),
        m_q=(slab_q == lane_q // GLA_DK).astype(F32),
        causal=col_a <= row_a % c,
        m_o=(slab_o == lane_o // GLA_DV).astype(F32),
        m_s=(row_s // GLA_DV == col_s // GLA_DK).astype(F32),
    )


def _gla_chunk(q, k, v, z, bg, st, c):
    h, n = GLA_HEADS, GLA_CHUNK
    la = _log_sigmoid(z + bg) * (1.0 / GLA_TAU)
    b = _mmf(c["ltri"], la)
    bl = jnp.sum(la, axis=0, keepdims=True)
    qs = q * (GLA_DK ** -0.5)
    kt = k * jnp.exp(bl - b)
    qt = qs * jnp.exp(b - bl)
    qe = qs * jnp.exp(b)
    q_stack = (jnp.broadcast_to(qt[None], (h, n, GLA_QK_W)) * c["m_q"]).reshape(h * n, GLA_QK_W)
    a = jnp.where(c["causal"], _mmf(q_stack, kt, False, True), 0.0)
    o_stack = _mm(a, v)
    o_intra = jnp.sum(o_stack.reshape(h, n, GLA_V_W) * c["m_o"], axis=0)
    o_inter = _mm(qe, st, False, True)
    st_next = st * jnp.exp(bl) + _mm(v, kt, True, False) * c["m_s"]
    return o_intra + o_inter, st_next


def _gla_post(o, gg, gain, g64):
    ms = _mmf(o * o, g64) * (1.0 / GLA_DV)
    return o * lax.rsqrt(ms + EPS) * gain * jax.nn.silu(gg)


def _gla_g64():
    r = lax.broadcasted_iota(jnp.int32, (GLA_V_W, GLA_V_W), 0)
    c = lax.broadcasted_iota(jnp.int32, (GLA_V_W, GLA_V_W), 1)
    return (r // GLA_DV == c // GLA_DV).astype(F32)


def _gla_in_specs(order):
    r = GLA_ROWS
    return [
        pl.BlockSpec((r, GLA_QK_W), lambda t: (order(t), COL_GQ // GLA_QK_W)),
        pl.BlockSpec((r, GLA_QK_W), lambda t: (order(t), COL_GK // GLA_QK_W)),
        pl.BlockSpec((r, GLA_V_W), lambda t: (order(t), COL_GV // GLA_V_W)),
        pl.BlockSpec((r, GLA_V_W), lambda t: (order(t), COL_GG // GLA_V_W)),
        pl.BlockSpec((r, GLA_QK_W), lambda t: (order(t), 0)),
        pl.BlockSpec((1, GLA_QK_W), lambda t: (0, 0)),
        pl.BlockSpec((1, GLA_V_W), lambda t: (0, 0)),
    ]


def _gla_fwd(p, z, bg, gain, *, name):
    s = p.shape[0]
    r = GLA_ROWS
    cps = r // GLA_CHUNK

    def body(q_ref, k_ref, v_ref, gg_ref, z_ref, bg_ref, gain_ref, y_ref, oraw_ref, stsave_ref, st_s):
        @pl.when(pl.program_id(0) == 0)
        def _():
            st_s[...] = jnp.zeros_like(st_s)

        c = _gla_consts()
        st = st_s[...]
        for ci in range(cps):
            rows = slice(ci * GLA_CHUNK, (ci + 1) * GLA_CHUNK)
            stsave_ref[ci] = st
            o, st = _gla_chunk(q_ref[rows, :], k_ref[rows, :], v_ref[rows, :], z_ref[rows, :], bg_ref[...], st, c)
            oraw_ref[rows, :] = o
        st_s[...] = st
        y_ref[...] = _gla_post(oraw_ref[...], gg_ref[...], gain_ref[...], _gla_g64())

    rowv = pl.BlockSpec((r, GLA_V_W), lambda t: (t, 0))
    return pl.pallas_call(
        body, name=name, grid=(s // r,), in_specs=_gla_in_specs(lambda t: t),
        out_specs=[rowv, rowv, pl.BlockSpec((cps, GLA_V_W, GLA_QK_W), lambda t: (t, 0, 0))],
        out_shape=[SDS((s, GLA_V_W), F32), SDS((s, GLA_V_W), F32), SDS((s // GLA_CHUNK, GLA_V_W, GLA_QK_W), F32)],
        scratch_shapes=[pltpu.VMEM((GLA_V_W, GLA_QK_W), F32)],
        compiler_params=_params("arbitrary"),
    )(p, p, p, p, z, bg, gain)


def _gla_bwd(p, z, bg, gain, oraw, stsave, dy_all, *, name):
    s = p.shape[0]
    r = GLA_ROWS
    cps = r // GLA_CHUNK
    nsteps = s // r
    w_qkvg = 2 * GLA_QK_W + 2 * GLA_V_W

    def body(q_ref, k_ref, v_ref, gg_ref, z_ref, bg_ref, gain_ref, oraw_ref, stsave_ref, dy_ref,
             dqkvg_ref, dz_ref, dbg_ref, dgain_ref, dst_s):
        @pl.when(pl.program_id(0) == 0)
        def _():
            dst_s[...] = jnp.zeros_like(dst_s)
            dbg_ref[...] = jnp.zeros_like(dbg_ref)
            dgain_ref[...] = jnp.zeros_like(dgain_ref)

        c = _gla_consts()
        _, vjp = jax.vjp(functools.partial(_gla_post, g64=_gla_g64()), oraw_ref[...], gg_ref[...], gain_ref[...])
        do, dgg, dgain = vjp(dy_ref[...])
        dqkvg_ref[:, 2 * GLA_QK_W + GLA_V_W:] = dgg
        dgain_ref[...] += dgain
        dst = dst_s[...]
        dbg = jnp.zeros((1, GLA_QK_W), F32)
        for ci in reversed(range(cps)):
            rows = slice(ci * GLA_CHUNK, (ci + 1) * GLA_CHUNK)
            _, vjp = jax.vjp(functools.partial(_gla_chunk, c=c), q_ref[rows, :], k_ref[rows, :], v_ref[rows, :],
                             z_ref[rows, :], bg_ref[...], stsave_ref[ci])
            dq, dk, dv, dz, dbg_c, dst = vjp((do[rows, :], dst))
            dqkvg_ref[rows, 0:GLA_QK_W] = dq
            dqkvg_ref[rows, GLA_QK_W:2 * GLA_QK_W] = dk
            dqkvg_ref[rows, 2 * GLA_QK_W:2 * GLA_QK_W + GLA_V_W] = dv
            dz_ref[rows, :] = dz
            dbg += dbg_c
        dst_s[...] = dst
        dbg_ref[...] += dbg

    order = lambda t: nsteps - 1 - t
    rowv = pl.BlockSpec((r, GLA_V_W), lambda t: (order(t), 0))
    return pl.pallas_call(
        body, name=name, grid=(nsteps,),
        in_specs=_gla_in_specs(order) + [
            rowv, pl.BlockSpec((cps, GLA_V_W, GLA_QK_W), lambda t: (order(t), 0, 0)),
            pl.BlockSpec((r, GLA_V_W), lambda t: (order(t), SWA_Q_W // GLA_V_W))],
        out_specs=[pl.BlockSpec((r, w_qkvg), lambda t: (order(t), 0)), pl.BlockSpec((r, GLA_QK_W), lambda t: (order(t), 0)),
                   pl.BlockSpec((1, GLA_QK_W), lambda t: (0, 0)), pl.BlockSpec((1, GLA_V_W), lambda t: (0, 0))],
        out_shape=[SDS((s, w_qkvg), F32), SDS((s, GLA_QK_W), F32), SDS((1, GLA_QK_W), F32), SDS((1, GLA_V_W), F32)],
        scratch_shapes=[pltpu.VMEM((GLA_V_W, GLA_QK_W), F32)],
        compiler_params=_params("arbitrary"),
    )(p, p, p, p, z, bg, gain, oraw, stsave, dy_all)


def _local_step(x, mem, target, small, big):
    g1, gmix, gmem, g2, sqg, skg, sinks, rel_bias, wgu, bg, gla_gain, mqg, mkg = small
    wg1, wu1, wd1, win_p, wkv, wout, wg2, wu2, wd2 = big
    wgu_pad = jnp.zeros((GLA_QK_W, GLA_QK_W), BF16).at[:GLA_RANK].set(wgu.astype(BF16))
    gain256 = jnp.tile(gla_gain, (1, GLA_HEADS))

    x1, saved1 = _ffn_fwd(x, g1, wg1, wu1, wd1, "ffn1")
    h = _rms_fwd(x1, gmix, tm=256, name="mix_rms")
    p = _matmul(h, win_p, tm=512, tn=384, name="mix_in")
    hm = _rms_fwd(mem, gmem, tm=256, name="mem_rms")
    kv = _matmul(hm, wkv, tm=256, tn=512, name="mem_kv")
    p_glr = p[:, COL_GLR:]
    z = _matmul(p_glr, wgu_pad, tm=512, tn=GLA_QK_W, name="gla_gate")
    y_swa = _swa_fwd(p, sqg, skg, sinks, rel_bias, name="swa_fwd")
    y_gla, oraw, stsave = _gla_fwd(p, z, bg, gain256, name="gla_fwd")
    y_mem = _mem_fwd(p, kv, mqg, mkg, tq=256, name="mem_fwd")
    x2 = _matmul([y_swa, y_gla, y_mem], wout, b_blocks=[0, 2, 3], tm=512, tn=512, res=x1, name="mix_out")
    x3, saved2 = _ffn_fwd(x2, g2, wg2, wu2, wd2, "ffn2")

    dy, dyb, loss = _loss_bwd(x3, target, tm=256, name="loss")
    dx2, dx2b, dg2, dwg2, dwu2, dwd2 = _ffn_bwd(dy, dyb, x2, g2, wg2, wu2, wd2, saved2, "ffn2")
    dy_all = _matmul(dx2b, wout, tb=True, tm=512, tn=512, name="mix_dy")
    dwout = _matmul(jnp.concatenate([y_swa, y_gla, y_mem], axis=1), dx2b, ta=True, tm=512, tn=512, name="mix_dw_out")
    dq_swa, dkv_swa, dsqg, dskg, dsink, drb = _swa_bwd(p, sqg, skg, sinks, rel_bias, dy_all, name="swa_bwd")
    dqkvg, dz, dbg, dgain256 = _gla_bwd(p, z, bg, gain256, oraw, stsave, dy_all, name="gla_bwd")
    dmq, dkv_mem, dmqg, dmkg = _mem_bwd(p, kv, mqg, mkg, dy_all, tq=256, name="mem_bwd")
    dglr = _matmul(dz, wgu_pad, tb=True, tm=512, tn=GLA_QK_W, name="gla_gate_dx")
    dwgu_pad = _matmul(p_glr, dz, ta=True, tm=GLA_QK_W, tn=GLA_QK_W, name="gla_gate_dw")
    dp = jnp.concatenate([dq_swa, dkv_swa, dqkvg, dmq, dglr], axis=1)
    dh = _matmul(dp, win_p, tb=True, tm=512, tn=512, name="mix_dh")
    dwin_p = _matmul(h, dp, ta=True, tm=512, tn=384, name="mix_dw_in")
    dx1, dx1b, dgmix = _rms_bwd(x1, gmix, dh, dx2, tm=256, name="mix_drms")
    dwkv = _matmul(hm, dkv_mem, ta=True, tm=512, tn=512, name="mem_dw_kv")
    dhm = _matmul(dkv_mem, wkv, tb=True, tm=256, tn=512, name="mem_dh")
    _, _, dgmem = _rms_bwd(mem, gmem, dhm, None, tm=256, name="mem_drms")
    dx, _, dg1, dwg1, dwu1, dwd1 = _ffn_bwd(dx1, dx1b, x, g1, wg1, wu1, wd1, saved1, "ffn1")

    dgla_gain = dgain256.reshape(GLA_HEADS, GLA_DV).sum(axis=0, keepdims=True)
    dsmall = (dg1, dgmix, dgmem, dg2, dsqg, dskg, dsink[0, :SWA_HEADS], drb[:, :SWA_HEADS], dwgu_pad[:GLA_RANK], dbg,
              dgla_gain, dmqg, dmkg)
    dbig = (dwg1, dwu1, dwd1, dwin_p, dwkv, dwout, dwg2, dwu2, dwd2)
    return loss, dx, dsmall, dbig


HBM_SPEC = pl.BlockSpec(memory_space=pltpu.HBM)


def _mesh_place():
    x, y, c = lax.axis_index("x"), lax.axis_index("y"), lax.axis_index("c")
    other_chips = [(1 - x, y), (x, 1 - y), (1 - x, 1 - y)]
    return x, y, c, other_chips


def _all_gather(shards, *, name):
    nt = len(shards)

    def body(*refs):
        x_refs, o_refs = refs[:nt], refs[nt:2 * nt]
        send_sems, recv_sems, local_sems = refs[2 * nt:]
        x, y, c, chips = _mesh_place()
        me, sibling = (x, y, c), (x, y, 1 - c)

        def copy(k, t, block, to, from_shard=False):
            bx, by, bc = block
            rows = o_refs[t].at[4 * bx + 2 * by + bc]
            return pltpu.make_async_remote_copy(
                src_ref=x_refs[t] if from_shard else rows, dst_ref=rows,
                send_sem=send_sems.at[k, t], recv_sem=recv_sems.at[k, t], device_id=to, device_id_type=MESH)

        mine = [pltpu.make_async_copy(x_refs[t], o_refs[t].at[4 * x + 2 * y + c], local_sems.at[t]) for t in range(nt)]
        for cp in mine:
            cp.start()
        first = [copy(0, t, me, sibling, True) for t in range(nt)]
        first += [copy(1 + j, t, me, (*chip, c), True) for j, chip in enumerate(chips) for t in range(nt)]
        for cp in first:
            cp.start()
        passed = []
        for j, chip in enumerate(chips):
            for t in range(nt):
                copy(1 + j, t, (*chip, c), me).wait_recv()
                fwd = copy(4 + j, t, (*chip, c), sibling)
                fwd.start()
                passed.append(fwd)
        for t in range(nt):
            copy(0, t, sibling, me).wait_recv()
        for j, chip in enumerate(chips):
            for t in range(nt):
                copy(4 + j, t, (*chip, 1 - c), me).wait_recv()
        for cp in first + passed:
            cp.wait_send()
        for cp in mine:
            cp.wait()

    return pl.pallas_call(
        body, name=name, in_specs=[HBM_SPEC] * nt, out_specs=[HBM_SPEC] * nt,
        out_shape=[SDS((N_DEV,) + s.shape, s.dtype) for s in shards],
        scratch_shapes=[pltpu.SemaphoreType.DMA((7, nt)), pltpu.SemaphoreType.DMA((7, nt)), pltpu.SemaphoreType.DMA((nt,))],
    )(*shards)


def _pair_exchange(parts, *, name):
    nt = len(parts)

    def body(*refs):
        g_refs, o_refs = refs[:nt], refs[nt:2 * nt]
        send_sems, recv_sems = refs[2 * nt:]
        x, y, c, _ = _mesh_place()
        copies = [pltpu.make_async_remote_copy(
            src_ref=g_refs[t].at[1 - c], dst_ref=o_refs[t], send_sem=send_sems.at[t], recv_sem=recv_sems.at[t],
            device_id=(x, y, 1 - c), device_id_type=MESH) for t in range(nt)]
        for cp in copies:
            cp.start()
        for cp in copies:
            cp.wait()

    return pl.pallas_call(
        body, name=name, in_specs=[HBM_SPEC] * nt, out_specs=[HBM_SPEC] * nt,
        out_shape=[SDS(g.shape[1:], g.dtype) for g in parts],
        scratch_shapes=[pltpu.SemaphoreType.DMA((nt,)), pltpu.SemaphoreType.DMA((nt,))],
    )(*parts)


def _chip_exchange(parts, small, *, name):
    nt = len(parts)

    def body(*refs):
        s_refs, small_ref = refs[:nt], refs[nt]
        o_refs, small_all = refs[nt + 1:2 * nt + 1], refs[2 * nt + 1]
        send_sems, recv_sems, small_send, small_recv, local_sem = refs[2 * nt + 2:]
        x, y, c, chips = _mesh_place()

        def copy(j, t, chip):
            return pltpu.make_async_remote_copy(
                src_ref=s_refs[t].at[2 * chip[0] + chip[1]], dst_ref=o_refs[t].at[j],
                send_sem=send_sems.at[j, t], recv_sem=recv_sems.at[j, t], device_id=(*chip, c), device_id_type=MESH)

        flips = [(fx, fy, fc) for fx in (0, 1) for fy in (0, 1) for fc in (0, 1)][1:]

        def small_copy(k):
            fx, fy, fc = flips[k]
            to = (x ^ fx if fx else x, y ^ fy if fy else y, c ^ fc if fc else c)
            rows = small_all.at[4 * x + 2 * y + c]
            return pltpu.make_async_remote_copy(
                src_ref=small_ref, dst_ref=rows, send_sem=small_send.at[k], recv_sem=small_recv.at[k],
                device_id=to, device_id_type=MESH)

        own = pltpu.make_async_copy(small_ref, small_all.at[4 * x + 2 * y + c], local_sem)
        own.start()
        copies = [copy(j, t, chip) for j, chip in enumerate(chips) for t in range(nt)]
        smalls = [small_copy(k) for k in range(7)]
        for cp in smalls + copies:
            cp.start()
        for cp in smalls + copies:
            cp.wait()
        own.wait()

    return pl.pallas_call(
        body, name=name, in_specs=[HBM_SPEC] * (nt + 1), out_specs=[HBM_SPEC] * (nt + 1),
        out_shape=[SDS((3,) + s.shape[1:], s.dtype) for s in parts] + [SDS((N_DEV,) + small.shape, small.dtype)],
        scratch_shapes=[pltpu.SemaphoreType.DMA((3, nt)), pltpu.SemaphoreType.DMA((3, nt)),
                        pltpu.SemaphoreType.DMA((7,)), pltpu.SemaphoreType.DMA((7,)), pltpu.SemaphoreType.DMA],
    )(*parts, small)


def _pair_sum(mine, theirs, c, *, tr, name):
    _, r, l = mine.shape

    def body(c_ref, a_ref, b_ref, o_ref):
        o_ref[...] = (a_ref[0].astype(F32) + b_ref[...].astype(F32)).astype(BF16)

    return pl.pallas_call(
        body, name=name,
        grid_spec=pltpu.PrefetchScalarGridSpec(
            num_scalar_prefetch=1, grid=(r // tr,),
            in_specs=[pl.BlockSpec((1, tr, l), lambda i, c_ref: (c_ref[0], i, 0)),
                      pl.BlockSpec((tr, l), lambda i, c_ref: (i, 0))],
            out_specs=pl.BlockSpec((tr, l), lambda i, c_ref: (i, 0))),
        out_shape=SDS((r, l), BF16), compiler_params=_params("parallel"),
    )(c, mine, theirs)


def _adamw(w, g, m, v):
    m = ADAM_B1 * m + (1.0 - ADAM_B1) * g
    v = ADAM_B2 * v + (1.0 - ADAM_B2) * jnp.square(g)
    m_hat = m / (1.0 - ADAM_B1 ** ADAM_STEP)
    v_hat = v / (1.0 - ADAM_B2 ** ADAM_STEP)
    delta = -ADAM_LR * (m_hat / (jnp.sqrt(v_hat) + ADAM_EPS) + ADAM_WD * w)
    return delta, m, v


def _adam_big(own, others, mat, xy, w, m, v, *, tr, name):
    _, r, l = w.shape

    def body(xy_ref, own_ref, oth_ref, w_ref, m_ref, v_ref, g_out, d_out, m_out, v_out):
        g = own_ref[0, 0].astype(F32)
        for j in range(3):
            g = g + oth_ref[j, 0].astype(F32)
        delta, m_new, v_new = _adamw(w_ref[0], g, m_ref[0], v_ref[0])
        g_out[0] = g
        d_out[0] = delta
        m_out[0] = m_new
        v_out[0] = v_new

    blk = pl.BlockSpec((1, tr, l), lambda i, xy_ref: (0, i, 0))
    return pl.pallas_call(
        body, name=name,
        grid_spec=pltpu.PrefetchScalarGridSpec(
            num_scalar_prefetch=1, grid=(r // tr,),
            in_specs=[pl.BlockSpec((1, 1, tr, l), lambda i, xy_ref: (xy_ref[0], mat, i, 0)),
                      pl.BlockSpec((3, 1, tr, l), lambda i, xy_ref: (0, mat, i, 0)), blk, blk, blk],
            out_specs=[blk, blk, blk, blk]),
        out_shape=[SDS(w.shape, F32)] * 4, compiler_params=_params("parallel"),
    )(xy, own, others, w, m, v)


def _adam_small(g_all, w, m, v, *, name):
    def body(g_ref, w_ref, m_ref, v_ref, g_out, d_out, m_out, v_out):
        g = g_ref[0]
        for k in range(1, N_DEV):
            g = g + g_ref[k]
        delta, m_new, v_new = _adamw(w_ref[...], g, m_ref[...], v_ref[...])
        g_out[...] = g
        d_out[...] = delta
        m_out[...] = m_new
        v_out[...] = v_new

    return pl.pallas_call(body, name=name, out_shape=[SDS(w.shape, F32)] * 4)(g_all, w, m, v)


SMALL_ROWS = 56


def _pack_small(parts):
    flat = jnp.concatenate([a.reshape(-1) for a in parts])
    return jnp.pad(flat, (0, SMALL_ROWS * 128 - flat.shape[0])).reshape(SMALL_ROWS, 128)


def _unpack_small(packed, shapes):
    flat = packed.reshape(-1)
    out, at = [], 0
    for s in shapes:
        n = math.prod(s)
        out.append(flat[at:at + n].reshape(s))
        at += n
    return out


def kernel(x, mem, ffn1_norm, ffn1_w_gate, ffn1_w_up, ffn1_w_down, mix_norm, mem_norm, w_in, w_mem_kv, swa_q_norm, swa_k_norm, swa_sinks, rel_bias, gla_w_gate_up, gla_b_gate, gla_out_norm, mem_q_norm, mem_k_norm, w_out, ffn2_norm, ffn2_w_gate, ffn2_w_up, ffn2_w_down, loss_target, m_ffn1_norm, m_ffn1_w_gate, m_ffn1_w_up, m_ffn1_w_down, m_mix_norm, m_mem_norm, m_w_in, m_w_mem_kv, m_swa_q_norm, m_swa_k_norm, m_swa_sinks, m_rel_bias, m_gla_w_gate_up, m_gla_b_gate, m_gla_out_norm, m_mem_q_norm, m_mem_k_norm, m_w_out, m_ffn2_norm, m_ffn2_w_gate, m_ffn2_w_up, m_ffn2_w_down, v_ffn1_norm, v_ffn1_w_gate, v_ffn1_w_up, v_ffn1_w_down, v_mix_norm, v_mem_norm, v_w_in, v_w_mem_kv, v_swa_q_norm, v_swa_k_norm, v_swa_sinks, v_rel_bias, v_gla_w_gate_up, v_gla_b_gate, v_gla_out_norm, v_mem_q_norm, v_mem_k_norm, v_w_out, v_ffn2_norm, v_ffn2_w_gate, v_ffn2_w_up, v_ffn2_w_down):
    xi, yi, ci = lax.axis_index("x"), lax.axis_index("y"), lax.axis_index("c")
    c_arr = jnp.reshape(ci, (1,)).astype(jnp.int32)
    xy_arr = jnp.reshape(2 * xi + yi, (1,)).astype(jnp.int32)
    d = x.shape[-1]

    col_f = [ffn1_w_gate, ffn1_w_up, ffn2_w_gate, ffn2_w_up]
    row_f = [ffn1_w_down, ffn2_w_down]
    shards = [jnp.concatenate(col_f, axis=0).astype(BF16), jnp.concatenate(row_f, axis=0).astype(BF16),
              w_in[0].astype(BF16), w_mem_kv[0].astype(BF16), w_out[0].astype(BF16)]
    col_all, row_all, win_all, wkv_all, wout_all = _all_gather(shards, name="gather_weights")

    def col_full(a):
        return a.transpose(1, 0, 2).reshape(a.shape[1], -1)

    wg1, wu1, wg2, wu2 = (col_full(col_all[:, k]) for k in range(4))
    wd1, wd2 = (row_all[:, k].reshape(-1, d) for k in range(2))
    win = col_full(win_all)
    glr_lo, glr_hi = COL_MQ, COL_MQ + GLA_RANK
    win_p = jnp.concatenate([win[:, :glr_lo], win[:, glr_hi:], win[:, glr_lo:glr_hi],
                             jnp.zeros((d, IN_W_PAD - IN_W), BF16)], axis=1)
    wkv = wkv_all.reshape(-1, wkv_all.shape[-1])
    wout = wout_all.reshape(-1, d)

    small_w = [ffn1_norm, mix_norm, mem_norm, ffn2_norm, swa_q_norm, swa_k_norm, swa_sinks[0], rel_bias,
               gla_w_gate_up[0], gla_b_gate, gla_out_norm, mem_q_norm, mem_k_norm]
    loss, grad_x, dsmall, dbig = _local_step(x[0], mem[0], loss_target[0], small_w,
                                             (wg1, wu1, wd1, win_p, wkv, wout, wg2, wu2, wd2))
    dwg1, dwu1, dwd1, dwin_p, dwkv, dwout, dwg2, dwu2, dwd2 = dbig

    def col_dst(g):
        w = g.shape[1] // N_DEV
        return g.reshape(g.shape[0], 4, 2, w).transpose(2, 1, 0, 3).astype(BF16)

    def row_dst(g):
        return g.reshape(4, 2, g.shape[0] // N_DEV, g.shape[1]).transpose(1, 0, 2, 3).astype(BF16)

    dwin = jnp.concatenate([dwin_p[:, :glr_lo], dwin_p[:, COL_GLR:COL_GLR + GLA_RANK], dwin_p[:, COL_MQ:COL_GLR]], axis=1)
    parts = [jnp.stack([col_dst(g) for g in (dwg1, dwu1, dwg2, dwu2)], axis=2),
             jnp.stack([row_dst(g) for g in (dwd1, dwd2)], axis=2),
             col_dst(dwin)[:, :, None], row_dst(dwkv)[:, :, None], row_dst(dwout)[:, :, None]]
    from_sibling = _pair_exchange(parts, name="pair_exchange")
    chip_sums = []
    for t, (mine, theirs) in enumerate(zip(parts, from_sibling)):
        l = mine.shape[-1]
        flat = _pair_sum(mine.reshape(2, -1, l), theirs.reshape(-1, l), c_arr, tr=256, name=f"pair_sum_{t}")
        chip_sums.append(flat.reshape(mine.shape[1:]))
    *from_chips, small_all = _chip_exchange(chip_sums, _pack_small(list(dsmall) + [loss]), name="chip_exchange")

    big_w = [(0, 0, ffn1_w_gate, m_ffn1_w_gate, v_ffn1_w_gate), (0, 1, ffn1_w_up, m_ffn1_w_up, v_ffn1_w_up),
             (1, 0, ffn1_w_down, m_ffn1_w_down, v_ffn1_w_down), (2, 0, w_in, m_w_in, v_w_in),
             (3, 0, w_mem_kv, m_w_mem_kv, v_w_mem_kv), (4, 0, w_out, m_w_out, v_w_out),
             (0, 2, ffn2_w_gate, m_ffn2_w_gate, v_ffn2_w_gate), (0, 3, ffn2_w_up, m_ffn2_w_up, v_ffn2_w_up),
             (1, 1, ffn2_w_down, m_ffn2_w_down, v_ffn2_w_down)]
    big_names = ["ffn1_w_gate", "ffn1_w_up", "ffn1_w_down", "w_in", "w_mem_kv", "w_out", "ffn2_w_gate", "ffn2_w_up",
                 "ffn2_w_down"]
    res = {}
    for nm, (t, mat, w, m, v) in zip(big_names, big_w):
        r = w.shape[1]
        tr = 256 if r % 256 == 0 else r
        res[nm] = _adam_big(chip_sums[t], from_chips[t], mat, xy_arr, w, m, v, tr=tr, name=f"adam_{nm}")
    small_names = ["ffn1_norm", "mix_norm", "mem_norm", "ffn2_norm", "swa_q_norm", "swa_k_norm", "swa_sinks", "rel_bias",
                   "gla_w_gate_up", "gla_b_gate", "gla_out_norm", "mem_q_norm", "mem_k_norm"]
    small_m = [m_ffn1_norm, m_mix_norm, m_mem_norm, m_ffn2_norm, m_swa_q_norm, m_swa_k_norm, m_swa_sinks, m_rel_bias,
               m_gla_w_gate_up, m_gla_b_gate, m_gla_out_norm, m_mem_q_norm, m_mem_k_norm]
    small_v = [v_ffn1_norm, v_mix_norm, v_mem_norm, v_ffn2_norm, v_swa_q_norm, v_swa_k_norm, v_swa_sinks, v_rel_bias,
               v_gla_w_gate_up, v_gla_b_gate, v_gla_out_norm, v_mem_q_norm, v_mem_k_norm]
    small_full = [ffn1_norm, mix_norm, mem_norm, ffn2_norm, swa_q_norm, swa_k_norm, swa_sinks, rel_bias,
                  gla_w_gate_up, gla_b_gate, gla_out_norm, mem_q_norm, mem_k_norm]
    packed = _adam_small(small_all, _pack_small(small_full), _pack_small(small_m), _pack_small(small_v), name="adam_small")
    full_shapes = [a.shape for a in small_full]
    unpacked = [_unpack_small(pk, full_shapes + [()]) for pk in packed]
    for k, nm in enumerate(small_names):
        res[nm] = [unpacked[q][k] for q in range(4)]
    loss = unpacked[0][len(small_names)]

    order = ["ffn1_norm", "ffn1_w_gate", "ffn1_w_up", "ffn1_w_down", "mix_norm", "mem_norm", "w_in", "w_mem_kv",
             "swa_q_norm", "swa_k_norm", "swa_sinks", "rel_bias", "gla_w_gate_up", "gla_b_gate", "gla_out_norm",
             "mem_q_norm", "mem_k_norm", "w_out", "ffn2_norm", "ffn2_w_gate", "ffn2_w_up", "ffn2_w_down"]
    outs = [loss, grad_x[None]]
    for q in range(4):
        outs += [res[nm][q] for nm in order]
    return tuple(outs)
```

```python
import functools
import math

import numpy as np
import jax
import jax.numpy as jnp
from jax import lax
from jax.experimental import pallas as pl
from jax.experimental.pallas import tpu as pltpu
from jax.experimental.pallas import tpu_sc as plsc

F32 = jnp.float32
BF16 = jnp.bfloat16
SDS = jax.ShapeDtypeStruct

EPS = 1e-6
HEAD_DIM = 64
SWA_HEADS = 8
SWA_KV_HEADS = 2
SWA_GROUP = SWA_HEADS // SWA_KV_HEADS
BLOCK = 128
N_BUCKETS = 32
MAX_DISTANCE = 128
GLA_HEADS = 4
GLA_DK = 32
GLA_DV = 64
GLA_RANK = 16
GLA_TAU = 16.0
GLA_CHUNK = 32
MEM_HEADS = 4
SWA_Q_W = SWA_HEADS * HEAD_DIM
SWA_KV_W = SWA_KV_HEADS * HEAD_DIM
GLA_QK_W = GLA_HEADS * GLA_DK
GLA_V_W = GLA_HEADS * GLA_DV
MEM_Q_W = MEM_HEADS * HEAD_DIM
IN_W = 1808
IN_W_PAD = 1920
COL_SQ, COL_SKV, COL_GQ, COL_GK, COL_GV, COL_GG, COL_MQ, COL_GLR = 0, 512, 768, 896, 1024, 1280, 1536, 1792

ADAM_LR = 0.001
ADAM_B1 = 0.9
ADAM_B2 = 0.999
ADAM_EPS = 1e-08
ADAM_WD = 0.01
ADAM_STEP = 10

N_DEV = 8
VMEM_LIMIT_BYTES = 56 * 1024 * 1024
MESH = pl.DeviceIdType.MESH


def _params(*sem):
    return pltpu.CompilerParams(dimension_semantics=sem or None, vmem_limit_bytes=VMEM_LIMIT_BYTES)


def _dot(a, b, ta, tb, precision=None):
    dims = (((0 if ta else 1,), (1 if tb else 0,)), ((), ()))
    return lax.dot_general(a, b, dims, preferred_element_type=F32, precision=precision)


def _mm_raw(a, b, ta=False, tb=False):
    return _dot(a.astype(BF16), b.astype(BF16), ta, tb)


def _mmf_raw(a, b, ta=False, tb=False):
    return _dot(a, b, ta, tb, lax.Precision.HIGHEST)


def _make_mm(raw):
    @functools.partial(jax.custom_vjp, nondiff_argnums=(2, 3))
    def mm(a, b, ta=False, tb=False):
        return raw(a, b, ta, tb)

    def fwd(a, b, ta, tb):
        return raw(a, b, ta, tb), (a, b)

    def bwd(ta, tb, res, g):
        a, b = res
        da = raw(b, g, tb, True) if ta else raw(g, b, False, not tb)
        db = raw(g, a, True, ta) if tb else raw(a, g, not ta, False)
        return da, db

    mm.defvjp(fwd, bwd)
    return mm


_mm = _make_mm(_mm_raw)
_mmf = _make_mm(_mmf_raw)


def _rms(x, g):
    return x * lax.rsqrt(jnp.mean(x * x, axis=-1, keepdims=True) + EPS) * g


def _silu_mul(g, u):
    return jax.nn.silu(g) * u


def _log_sigmoid(z):
    return jnp.minimum(z, 0.0) - jnp.log(1.0 + jnp.exp(-jnp.abs(z)))


def _matmul(a_list, b, *, ta=False, tb=False, tm, tn, b_blocks=None, res=None, scale=1.0, out_dtype=F32, name):
    if not isinstance(a_list, (list, tuple)):
        a_list = [a_list]
    n_a = len(a_list)
    m = a_list[0].shape[1] if ta else a_list[0].shape[0]
    ks = [a.shape[0] if ta else a.shape[1] for a in a_list]
    n = b.shape[0] if tb else b.shape[1]
    if b_blocks is None:
        assert n_a == 1
        b_blocks = [0]
    assert m % tm == 0 and n % tn == 0, (m, n, tm, tn)

    def body(*refs):
        a_refs, b_refs = refs[:n_a], refs[n_a:2 * n_a]
        r_ref = refs[2 * n_a] if res is not None else None
        o_ref = refs[-1]
        acc = _mm_raw(a_refs[0][...], b_refs[0][...], ta, tb)
        for k in range(1, n_a):
            acc = acc + _mm_raw(a_refs[k][...], b_refs[k][...], ta, tb)
        if scale != 1.0:
            acc = acc * scale
        if r_ref is not None:
            acc = r_ref[...] + acc
        o_ref[...] = acc.astype(out_dtype)

    in_specs = []
    for k in ks:
        in_specs.append(pl.BlockSpec((k, tm), lambda i, j: (0, i)) if ta else pl.BlockSpec((tm, k), lambda i, j: (i, 0)))
    for k, blk in zip(ks, b_blocks):
        if tb:
            in_specs.append(pl.BlockSpec((tn, k), functools.partial(lambda i, j, blk: (j, blk), blk=blk)))
        else:
            in_specs.append(pl.BlockSpec((k, tn), functools.partial(lambda i, j, blk: (blk, j), blk=blk)))
    args = list(a_list) + [b] * n_a
    if res is not None:
        in_specs.append(pl.BlockSpec((tm, tn), lambda i, j: (i, j)))
        args.append(res)
    return pl.pallas_call(
        body, name=name, grid=(m // tm, n // tn), in_specs=in_specs,
        out_specs=pl.BlockSpec((tm, tn), lambda i, j: (i, j)), out_shape=SDS((m, n), out_dtype),
        compiler_params=_params("parallel", "parallel"),
    )(*args)


def _rms_fwd(x, g, *, tm, name):
    s, d = x.shape

    def body(x_ref, g_ref, h_ref):
        h_ref[...] = _rms(x_ref[...], g_ref[...]).astype(BF16)

    return pl.pallas_call(
        body, name=name, grid=(s // tm,),
        in_specs=[pl.BlockSpec((tm, d), lambda i: (i, 0)), pl.BlockSpec((1, d), lambda i: (0, 0))],
        out_specs=pl.BlockSpec((tm, d), lambda i: (i, 0)), out_shape=SDS((s, d), BF16),
        compiler_params=_params("parallel"),
    )(x, g)


def _rms_bwd(x, g, dh, dres, *, tm, name):
    s, d = x.shape
    want_dx = dres is not None

    def body(*refs):
        if want_dx:
            x_ref, g_ref, dh_ref, dres_ref, dx_ref, dxb_ref, dg_ref = refs
        else:
            x_ref, g_ref, dh_ref, dg_ref = refs
        _, vjp = jax.vjp(_rms, x_ref[...], g_ref[...])
        dx, dg = vjp(dh_ref[...])
        if want_dx:
            dx = dres_ref[...] + dx
            dx_ref[...] = dx
            dxb_ref[...] = dx.astype(BF16)

        @pl.when(pl.program_id(0) == 0)
        def _():
            dg_ref[...] = jnp.zeros_like(dg_ref)

        dg_ref[...] += dg

    row = pl.BlockSpec((tm, d), lambda i: (i, 0))
    vec = pl.BlockSpec((1, d), lambda i: (0, 0))
    if want_dx:
        return pl.pallas_call(
            body, name=name, grid=(s // tm,), in_specs=[row, vec, row, row], out_specs=[row, row, vec],
            out_shape=[SDS((s, d), F32), SDS((s, d), BF16), SDS((1, d), F32)], compiler_params=_params("arbitrary"),
        )(x, g, dh, dres)
    return None, None, pl.pallas_call(
        body, name=name, grid=(s // tm,), in_specs=[row, vec, row], out_specs=vec,
        out_shape=SDS((1, d), F32), compiler_params=_params("arbitrary"),
    )(x, g, dh)


FFN_TN = 256


def _ffn_fwd(x, gain, wg, wu, wd, tag, *, tm=1024):
    s, d = x.shape
    f = wg.shape[1]
    tn = FFN_TN
    nj = f // tn
    tm = min(tm, s)

    def body(x_ref, gain_ref, wg_ref, wu_ref, wd_ref, y_ref, h_ref, g_ref, u_ref, acc_s):
        j = pl.program_id(1)

        @pl.when(j == 0)
        def _():
            h_ref[...] = _rms(x_ref[...], gain_ref[...]).astype(BF16)
            acc_s[...] = jnp.zeros_like(acc_s)

        hv = h_ref[...]
        g = _mm_raw(hv, wg_ref[...])
        u = _mm_raw(hv, wu_ref[...])
        g_ref[...] = g.astype(BF16)
        u_ref[...] = u.astype(BF16)
        acc_s[...] += _mm_raw(_silu_mul(g, u), wd_ref[...])

        @pl.when(j == nj - 1)
        def _():
            y_ref[...] = x_ref[...] + 0.5 * acc_s[...]

    row = pl.BlockSpec((tm, d), lambda i, j: (i, 0))
    tile = pl.BlockSpec((tm, tn), lambda i, j: (i, j))
    wcol = pl.BlockSpec((d, tn), lambda i, j: (0, j))
    y, h, g, u = pl.pallas_call(
        body, name=f"{tag}_fwd", grid=(s // tm, nj),
        in_specs=[row, pl.BlockSpec((1, d), lambda i, j: (0, 0)), wcol, wcol, pl.BlockSpec((tn, d), lambda i, j: (j, 0))],
        out_specs=[row, row, tile, tile],
        out_shape=[SDS((s, d), F32), SDS((s, d), BF16), SDS((s, f), BF16), SDS((s, f), BF16)],
        scratch_shapes=[pltpu.VMEM((tm, d), F32)],
        compiler_params=_params("parallel", "arbitrary"),
    )(x, gain, wg, wu, wd)
    return y, (h, g, u)


def _ffn_bwd(dy, dyb, x, gain, wg, wu, wd, saved, tag):
    h, g, u = saved
    s, d = x.shape
    f = wg.shape[1]
    tn = FFN_TN

    def body(dy_ref, ht_ref, wg_ref, wu_ref, wd_ref, g_ref, u_ref, dh_ref, dwg_ref, dwu_ref, dwd_ref):
        @pl.when(pl.program_id(0) == 0)
        def _():
            dh_ref[...] = jnp.zeros_like(dh_ref)

        dyv = dy_ref[...]
        da = _mm_raw(dyv, wd_ref[...], False, True) * 0.5
        a, vjp = jax.vjp(_silu_mul, g_ref[...].astype(F32), u_ref[...].astype(F32))
        dg, du = vjp(da)
        dg = dg.astype(BF16)
        du = du.astype(BF16)
        dh_ref[...] += _mm_raw(dg, wg_ref[...], False, True) + _mm_raw(du, wu_ref[...], False, True)
        dwg_ref[...] = _mm_raw(ht_ref[...], dg)
        dwu_ref[...] = _mm_raw(ht_ref[...], du)
        dwd_ref[...] = _mm_raw(a, dyv, True, False) * 0.5

    full = pl.BlockSpec((s, d), lambda j: (0, 0))
    tile = pl.BlockSpec((s, tn), lambda j: (0, j))
    wcol = pl.BlockSpec((d, tn), lambda j: (0, j))
    wrow = pl.BlockSpec((tn, d), lambda j: (j, 0))
    dh, dwg, dwu, dwd = pl.pallas_call(
        body, name=f"{tag}_bwd", grid=(f // tn,),
        in_specs=[full, pl.BlockSpec((d, s), lambda j: (0, 0)), wcol, wcol, wrow, tile, tile],
        out_specs=[full, wcol, wcol, wrow],
        out_shape=[SDS((s, d), F32), SDS((d, f), F32), SDS((d, f), F32), SDS((f, d), F32)],
        compiler_params=_params("arbitrary"),
    )(dyb, h.T, wg, wu, wd, g, u)
    dx, dxb, dgain = _rms_bwd(x, gain, dh, dy, tm=256, name=f"{tag}_drms")
    return dx, dxb, dgain, dwg, dwu, dwd


def _loss_bwd(y, target, *, tm, name):
    s, d = y.shape

    def body(y_ref, t_ref, dy_ref, dyb_ref, l_ref):
        diff = y_ref[...] - t_ref[...]
        dy_ref[...] = diff * (1.0 / d)
        dyb_ref[...] = (diff * (1.0 / d)).astype(BF16)

        @pl.when(pl.program_id(0) == 0)
        def _():
            l_ref[...] = jnp.zeros_like(l_ref)

        l_ref[...] += 0.5 * jnp.sum(jnp.mean(diff * diff, axis=-1, keepdims=True), axis=0, keepdims=True)

    row = pl.BlockSpec((tm, d), lambda i: (i, 0))
    return pl.pallas_call(
        body, name=name, grid=(s // tm,), in_specs=[row, row],
        out_specs=[row, row, pl.BlockSpec((1, 1), lambda i: (0, 0))],
        out_shape=[SDS((s, d), F32), SDS((s, d), BF16), SDS((1, 1), F32)],
        compiler_params=_params("arbitrary"),
    )(y, target)


def _bucket_table():
    qi = np.arange(BLOCK)[:, None]
    kj = np.arange(2 * BLOCK)[None, :]
    dist = np.maximum(qi + BLOCK - kj, 0)
    max_exact = N_BUCKETS // 2
    d = np.maximum(dist, 1).astype(np.float32)
    large = max_exact + (np.log(d / np.float32(max_exact)) / np.float32(math.log(MAX_DISTANCE / max_exact))
                         * np.float32(N_BUCKETS - max_exact)).astype(np.int32)
    large = np.minimum(large, N_BUCKETS - 1)
    return np.where(dist < max_exact, dist, large).astype(np.int32)


def _swa_valid(n):
    qi = lax.broadcasted_iota(jnp.int32, (BLOCK, 2 * BLOCK), 0)
    kj = lax.broadcasted_iota(jnp.int32, (BLOCK, 2 * BLOCK), 1)
    dist = qi + BLOCK - kj
    return (dist >= 0) & (dist < BLOCK) & ((kj >= BLOCK) | (n > 0))


def _swa_head(q, kb, vb, qg, kg, sink, bias, valid):
    qn = _rms(q, qg)
    kn = _rms(kb, kg)
    s = _mm(qn, kn, False, True) * (HEAD_DIM ** -0.5) + bias
    s = jnp.where(valid, s, -jnp.inf)
    m = lax.stop_gradient(jnp.maximum(jnp.max(s, axis=-1, keepdims=True), sink))
    p = jnp.exp(s - m)
    p = p / (jnp.sum(p, axis=-1, keepdims=True) + jnp.exp(sink - m))
    return _mm(p, vb)


def _swa_bias_table(rb_ref, bucket, bias_s):
    for h in range(SWA_HEADS):
        acc = jnp.zeros((BLOCK, 2 * BLOCK), F32)
        for b in range(N_BUCKETS):
            acc = jnp.where(bucket == b, rb_ref[b, h], acc)
        bias_s[h] = acc


def _swa_band(kvp_ref, kvc_ref, g):
    lo = g * HEAD_DIM
    kb = jnp.concatenate([kvp_ref[:, lo:lo + HEAD_DIM], kvc_ref[:, lo:lo + HEAD_DIM]], axis=0)
    lo += SWA_KV_W
    vb = jnp.concatenate([kvp_ref[:, lo:lo + HEAD_DIM], kvc_ref[:, lo:lo + HEAD_DIM]], axis=0)
    return kb, vb


def _swa_specs(order):
    kvc = COL_SKV // (2 * SWA_KV_W)
    return [
        pl.BlockSpec((BLOCK, SWA_Q_W), lambda t: (order(t), 0)),
        pl.BlockSpec((BLOCK, 2 * SWA_KV_W), lambda t: (jnp.maximum(order(t) - 1, 0), kvc)),
        pl.BlockSpec((BLOCK, 2 * SWA_KV_W), lambda t: (order(t), kvc)),
        pl.BlockSpec((1, HEAD_DIM), lambda t: (0, 0)),
        pl.BlockSpec((1, HEAD_DIM), lambda t: (0, 0)),
        pl.BlockSpec(memory_space=pltpu.SMEM),
        pl.BlockSpec(memory_space=pltpu.SMEM),
        pl.BlockSpec((BLOCK, 2 * BLOCK), lambda t: (0, 0)),
    ]


def _swa_fwd(p, qg, kg, sinks, rel_bias, *, name):
    s = p.shape[0]
    nb = s // BLOCK

    def body(q_ref, kvp_ref, kvc_ref, qg_ref, kg_ref, sink_ref, rb_ref, bucket_ref, y_ref, bias_s):
        n = pl.program_id(0)

        @pl.when(n == 0)
        def _():
            _swa_bias_table(rb_ref, bucket_ref[...], bias_s)

        valid = _swa_valid(n)
        for g in range(SWA_KV_HEADS):
            kb, vb = _swa_band(kvp_ref, kvc_ref, g)
            for hh in range(SWA_GROUP):
                h = g * SWA_GROUP + hh
                cols = slice(h * HEAD_DIM, (h + 1) * HEAD_DIM)
                sink = jnp.full((1, 1), sink_ref[h], F32)
                y_ref[:, cols] = _swa_head(q_ref[:, cols], kb, vb, qg_ref[...], kg_ref[...], sink, bias_s[h], valid)

    return pl.pallas_call(
        body, name=name, grid=(nb,), in_specs=_swa_specs(lambda t: t),
        out_specs=pl.BlockSpec((BLOCK, SWA_Q_W), lambda t: (t, 0)), out_shape=SDS((s, SWA_Q_W), F32),
        scratch_shapes=[pltpu.VMEM((SWA_HEADS, BLOCK, 2 * BLOCK), F32)],
        compiler_params=_params("arbitrary"),
    )(p, p, p, qg, kg, sinks, rel_bias, jnp.asarray(_bucket_table()))


def _swa_bwd(p, qg, kg, sinks, rel_bias, dy_all, *, name):
    s = p.shape[0]
    nb = s // BLOCK

    def body(q_ref, kvp_ref, kvc_ref, qg_ref, kg_ref, sink_ref, rb_ref, bucket_ref, dy_ref,
             dq_ref, dkv_ref, dqg_ref, dkg_ref, dsink_ref, drb_ref, bias_s, dbias_s, carry_s):
        t = pl.program_id(0)
        n = nb - 1 - t

        @pl.when(t == 0)
        def _():
            _swa_bias_table(rb_ref, bucket_ref[...], bias_s)
            dbias_s[...] = jnp.zeros_like(dbias_s)
            carry_s[...] = jnp.zeros_like(carry_s)
            dqg_ref[...] = jnp.zeros_like(dqg_ref)
            dkg_ref[...] = jnp.zeros_like(dkg_ref)
            dsink_ref[...] = jnp.zeros_like(dsink_ref)
            drb_ref[...] = jnp.zeros_like(drb_ref)

        valid = _swa_valid(n)
        lane = lax.broadcasted_iota(jnp.int32, (1, BLOCK), 1)
        dqg = jnp.zeros((1, HEAD_DIM), F32)
        dkg = jnp.zeros((1, HEAD_DIM), F32)
        dsink_vec = jnp.zeros((1, BLOCK), F32)
        for g in range(SWA_KV_HEADS):
            kb, vb = _swa_band(kvp_ref, kvc_ref, g)
            dkb = jnp.zeros((2 * BLOCK, HEAD_DIM), F32)
            dvb = jnp.zeros((2 * BLOCK, HEAD_DIM), F32)
            for hh in range(SWA_GROUP):
                h = g * SWA_GROUP + hh
                cols = slice(h * HEAD_DIM, (h + 1) * HEAD_DIM)
                sink = jnp.full((1, 1), sink_ref[h], F32)
                _, vjp = jax.vjp(functools.partial(_swa_head, valid=valid),
                                 q_ref[:, cols], kb, vb, qg_ref[...], kg_ref[...], sink, bias_s[h])
                dq, dkb_h, dvb_h, dqg_h, dkg_h, dsink_h, dbias_h = vjp(dy_ref[:, cols])
                dq_ref[:, cols] = dq
                dkb += dkb_h
                dvb += dvb_h
                dqg += dqg_h
                dkg += dkg_h
                dsink_vec += jnp.where(lane == h, dsink_h, 0.0)
                dbias_s[h] += dbias_h
            lo = g * HEAD_DIM
            dkv_ref[:, lo:lo + HEAD_DIM] = dkb[BLOCK:] + carry_s[g]
            carry_s[g] = dkb[:BLOCK]
            lo += SWA_KV_W
            dkv_ref[:, lo:lo + HEAD_DIM] = dvb[BLOCK:] + carry_s[SWA_KV_HEADS + g]
            carry_s[SWA_KV_HEADS + g] = dvb[:BLOCK]
        dqg_ref[...] += dqg
        dkg_ref[...] += dkg
        dsink_ref[...] += dsink_vec

        @pl.when(t == nb - 1)
        def _():
            bucket = bucket_ref[...]
            row = lax.broadcasted_iota(jnp.int32, (N_BUCKETS, BLOCK), 0)
            col = lax.broadcasted_iota(jnp.int32, (N_BUCKETS, BLOCK), 1)
            acc = jnp.zeros((N_BUCKETS, BLOCK), F32)
            for h in range(SWA_HEADS):
                dbias = dbias_s[h]
                for b in range(N_BUCKETS):
                    part = jnp.sum(jnp.where(bucket == b, dbias, 0.0), axis=1, keepdims=True)
                    val = jnp.sum(part, axis=0, keepdims=True)
                    acc = acc + jnp.where((row == b) & (col == h), val, 0.0)
            drb_ref[...] = acc

    order = lambda t: nb - 1 - t
    vec = pl.BlockSpec((1, HEAD_DIM), lambda t: (0, 0))
    return pl.pallas_call(
        body, name=name, grid=(nb,),
        in_specs=_swa_specs(order) + [pl.BlockSpec((BLOCK, SWA_Q_W), lambda t: (order(t), 0))],
        out_specs=[pl.BlockSpec((BLOCK, SWA_Q_W), lambda t: (order(t), 0)),
                   pl.BlockSpec((BLOCK, 2 * SWA_KV_W), lambda t: (order(t), 0)),
                   vec, vec, pl.BlockSpec((1, BLOCK), lambda t: (0, 0)),
                   pl.BlockSpec((N_BUCKETS, BLOCK), lambda t: (0, 0))],
        out_shape=[SDS((s, SWA_Q_W), F32), SDS((s, 2 * SWA_KV_W), F32), SDS((1, HEAD_DIM), F32),
                   SDS((1, HEAD_DIM), F32), SDS((1, BLOCK), F32), SDS((N_BUCKETS, BLOCK), F32)],
        scratch_shapes=[pltpu.VMEM((SWA_HEADS, BLOCK, 2 * BLOCK), F32), pltpu.VMEM((SWA_HEADS, BLOCK, 2 * BLOCK), F32),
                        pltpu.VMEM((2 * SWA_KV_HEADS, BLOCK, HEAD_DIM), F32)],
        compiler_params=_params("arbitrary"),
    )(p, p, p, qg, kg, sinks, rel_bias, jnp.asarray(_bucket_table()), dy_all)


def _mem_head(q, k, v, qg, kg):
    qn = _rms(q, qg)
    kn = _rms(k, kg)
    s = _mm(qn, kn, False, True) * (HEAD_DIM ** -0.5)
    m = lax.stop_gradient(jnp.max(s, axis=-1, keepdims=True))
    e = jnp.exp(s - m)
    return _mm(e / jnp.sum(e, axis=-1, keepdims=True), v)


def _mem_fwd(p, kv, qg, kg, *, tq, name):
    s = p.shape[0]
    m = kv.shape[0]

    def body(q_ref, kv_ref, qg_ref, kg_ref, y_ref):
        for h in range(MEM_HEADS):
            cols = slice(h * HEAD_DIM, (h + 1) * HEAD_DIM)
            vcols = slice(MEM_Q_W + h * HEAD_DIM, MEM_Q_W + (h + 1) * HEAD_DIM)
            y_ref[:, cols] = _mem_head(q_ref[:, cols], kv_ref[:, cols], kv_ref[:, vcols], qg_ref[...], kg_ref[...])

    vec = pl.BlockSpec((1, HEAD_DIM), lambda t: (0, 0))
    return pl.pallas_call(
        body, name=name, grid=(s // tq,),
        in_specs=[pl.BlockSpec((tq, MEM_Q_W), lambda t: (t, COL_MQ // MEM_Q_W)),
                  pl.BlockSpec((m, 2 * MEM_Q_W), lambda t: (0, 0)), vec, vec],
        out_specs=pl.BlockSpec((tq, MEM_Q_W), lambda t: (t, 0)), out_shape=SDS((s, MEM_Q_W), F32),
        compiler_params=_params("parallel"),
    )(p, kv, qg, kg)


def _mem_bwd(p, kv, qg, kg, dy_all, *, tq, name):
    s = p.shape[0]
    m = kv.shape[0]

    def body(q_ref, kv_ref, qg_ref, kg_ref, dy_ref, dq_ref, dkv_ref, dqg_ref, dkg_ref):
        @pl.when(pl.program_id(0) == 0)
        def _():
            dkv_ref[...] = jnp.zeros_like(dkv_ref)
            dqg_ref[...] = jnp.zeros_like(dqg_ref)
            dkg_ref[...] = jnp.zeros_like(dkg_ref)

        dqg = jnp.zeros((1, HEAD_DIM), F32)
        dkg = jnp.zeros((1, HEAD_DIM), F32)
        for h in range(MEM_HEADS):
            cols = slice(h * HEAD_DIM, (h + 1) * HEAD_DIM)
            vcols = slice(MEM_Q_W + h * HEAD_DIM, MEM_Q_W + (h + 1) * HEAD_DIM)
            _, vjp = jax.vjp(_mem_head, q_ref[:, cols], kv_ref[:, cols], kv_ref[:, vcols], qg_ref[...], kg_ref[...])
            dq, dk, dv, dqg_h, dkg_h = vjp(dy_ref[:, cols])
            dq_ref[:, cols] = dq
            dkv_ref[:, cols] += dk
            dkv_ref[:, vcols] += dv
            dqg += dqg_h
            dkg += dkg_h
        dqg_ref[...] += dqg
        dkg_ref[...] += dkg

    vec = pl.BlockSpec((1, HEAD_DIM), lambda t: (0, 0))
    full = pl.BlockSpec((m, 2 * MEM_Q_W), lambda t: (0, 0))
    dy_col = (SWA_Q_W + GLA_V_W) // MEM_Q_W
    return pl.pallas_call(
        body, name=name, grid=(s // tq,),
        in_specs=[pl.BlockSpec((tq, MEM_Q_W), lambda t: (t, COL_MQ // MEM_Q_W)), full, vec, vec,
                  pl.BlockSpec((tq, MEM_Q_W), lambda t: (t, dy_col))],
        out_specs=[pl.BlockSpec((tq, MEM_Q_W), lambda t: (t, 0)), full, vec, vec],
        out_shape=[SDS((s, MEM_Q_W), F32), SDS((m, 2 * MEM_Q_W), F32), SDS((1, HEAD_DIM), F32), SDS((1, HEAD_DIM), F32)],
        compiler_params=_params("arbitrary"),
    )(p, kv, qg, kg, dy_all)


GLA_ROWS = 256


def _gla_consts():
    c, h = GLA_CHUNK, GLA_HEADS
    i2 = lax.broadcasted_iota(jnp.int32, (c, c), 0)
    j2 = lax.broadcasted_iota(jnp.int32, (c, c), 1)
    slab_q = lax.broadcasted_iota(jnp.int32, (h, c, GLA_QK_W), 0)
    lane_q = lax.broadcasted_iota(jnp.int32, (h, c, GLA_QK_W), 2)
    row_a = lax.broadcasted_iota(jnp.int32, (h * c, c), 0)
    col_a = lax.broadcasted_iota(jnp.int32, (h * c, c), 1)
    slab_o = lax.broadcasted_iota(jnp.int32, (h, c, GLA_V_W), 0)
    lane_o = lax.broadcasted_iota(jnp.int32, (h, c, GLA_V_W), 2)
    row_s = lax.broadcasted_iota(jnp.int32, (GLA_V_W, GLA_QK_W), 0)
    col_s = lax.broadcasted_iota(jnp.int32, (GLA_V_W, GLA_QK_W), 1)
    return dict(
        ltri=(j2 <= i2).astype(F32),
        m_q=(slab_q == lane_q // GLA_DK).astype(F32),
        causal=col_a <= row_a % c,
        m_o=(slab_o == lane_o // GLA_DV).astype(F32),
        m_s=(row_s // GLA_DV == col_s // GLA_DK).astype(F32),
    )


def _gla_chunk(q, k, v, z, bg, st, c):
    h, n = GLA_HEADS, GLA_CHUNK
    la = _log_sigmoid(z + bg) * (1.0 / GLA_TAU)
    b = _mmf(c["ltri"], la)
    bl = jnp.sum(la, axis=0, keepdims=True)
    qs = q * (GLA_DK ** -0.5)
    kt = k * jnp.exp(bl - b)
    qt = qs * jnp.exp(b - bl)
    qe = qs * jnp.exp(b)
    q_stack = (jnp.broadcast_to(qt[None], (h, n, GLA_QK_W)) * c["m_q"]).reshape(h * n, GLA_QK_W)
    a = jnp.where(c["causal"], _mmf(q_stack, kt, False, True), 0.0)
    o_stack = _mm(a, v)
    o_intra = jnp.sum(o_stack.reshape(h, n, GLA_V_W) * c["m_o"], axis=0)
    o_inter = _mm(qe, st, False, True)
    st_next = st * jnp.exp(bl) + _mm(v, kt, True, False) * c["m_s"]
    return o_intra + o_inter, st_next


def _gla_post(o, gg, gain, g64):
    ms = _mmf(o * o, g64) * (1.0 / GLA_DV)
    return o * lax.rsqrt(ms + EPS) * gain * jax.nn.silu(gg)


def _gla_g64():
    r = lax.broadcasted_iota(jnp.int32, (GLA_V_W, GLA_V_W), 0)
    c = lax.broadcasted_iota(jnp.int32, (GLA_V_W, GLA_V_W), 1)
    return (r // GLA_DV == c // GLA_DV).astype(F32)


def _gla_in_specs(order):
    r = GLA_ROWS
    return [
        pl.BlockSpec((r, GLA_QK_W), lambda t: (order(t), COL_GQ // GLA_QK_W)),
        pl.BlockSpec((r, GLA_QK_W), lambda t: (order(t), COL_GK // GLA_QK_W)),
        pl.BlockSpec((r, GLA_V_W), lambda t: (order(t), COL_GV // GLA_V_W)),
        pl.BlockSpec((r, GLA_V_W), lambda t: (order(t), COL_GG // GLA_V_W)),
        pl.BlockSpec((r, GLA_QK_W), lambda t: (order(t), 0)),
        pl.BlockSpec((1, GLA_QK_W), lambda t: (0, 0)),
        pl.BlockSpec((1, GLA_V_W), lambda t: (0, 0)),
    ]


def _gla_fwd(p, z, bg, gain, *, name):
    s = p.shape[0]
    r = GLA_ROWS
    cps = r // GLA_CHUNK

    def body(q_ref, k_ref, v_ref, gg_ref, z_ref, bg_ref, gain_ref, y_ref, oraw_ref, stsave_ref, st_s):
        @pl.when(pl.program_id(0) == 0)
        def _():
            st_s[...] = jnp.zeros_like(st_s)

        c = _gla_consts()
        st = st_s[...]
        for ci in range(cps):
            rows = slice(ci * GLA_CHUNK, (ci + 1) * GLA_CHUNK)
            stsave_ref[ci] = st
            o, st = _gla_chunk(q_ref[rows, :], k_ref[rows, :], v_ref[rows, :], z_ref[rows, :], bg_ref[...], st, c)
            oraw_ref[rows, :] = o
        st_s[...] = st
        y_ref[...] = _gla_post(oraw_ref[...], gg_ref[...], gain_ref[...], _gla_g64())

    rowv = pl.BlockSpec((r, GLA_V_W), lambda t: (t, 0))
    return pl.pallas_call(
        body, name=name, grid=(s // r,), in_specs=_gla_in_specs(lambda t: t),
        out_specs=[rowv, rowv, pl.BlockSpec((cps, GLA_V_W, GLA_QK_W), lambda t: (t, 0, 0))],
        out_shape=[SDS((s, GLA_V_W), F32), SDS((s, GLA_V_W), F32), SDS((s // GLA_CHUNK, GLA_V_W, GLA_QK_W), F32)],
        scratch_shapes=[pltpu.VMEM((GLA_V_W, GLA_QK_W), F32)],
        compiler_params=_params("arbitrary"),
    )(p, p, p, p, z, bg, gain)


def _gla_bwd(p, z, bg, gain, oraw, stsave, dy_all, *, name):
    s = p.shape[0]
    r = GLA_ROWS
    cps = r // GLA_CHUNK
    nsteps = s // r
    w_qkvg = 2 * GLA_QK_W + 2 * GLA_V_W

    def body(q_ref, k_ref, v_ref, gg_ref, z_ref, bg_ref, gain_ref, oraw_ref, stsave_ref, dy_ref,
             dqkvg_ref, dz_ref, dbg_ref, dgain_ref, dst_s):
        @pl.when(pl.program_id(0) == 0)
        def _():
            dst_s[...] = jnp.zeros_like(dst_s)
            dbg_ref[...] = jnp.zeros_like(dbg_ref)
            dgain_ref[...] = jnp.zeros_like(dgain_ref)

        c = _gla_consts()
        _, vjp = jax.vjp(functools.partial(_gla_post, g64=_gla_g64()), oraw_ref[...], gg_ref[...], gain_ref[...])
        do, dgg, dgain = vjp(dy_ref[...])
        dqkvg_ref[:, 2 * GLA_QK_W + GLA_V_W:] = dgg
        dgain_ref[...] += dgain
        dst = dst_s[...]
        dbg = jnp.zeros((1, GLA_QK_W), F32)
        for ci in reversed(range(cps)):
            rows = slice(ci * GLA_CHUNK, (ci + 1) * GLA_CHUNK)
            _, vjp = jax.vjp(functools.partial(_gla_chunk, c=c), q_ref[rows, :], k_ref[rows, :], v_ref[rows, :],
                             z_ref[rows, :], bg_ref[...], stsave_ref[ci])
            dq, dk, dv, dz, dbg_c, dst = vjp((do[rows, :], dst))
            dqkvg_ref[rows, 0:GLA_QK_W] = dq
            dqkvg_ref[rows, GLA_QK_W:2 * GLA_QK_W] = dk
            dqkvg_ref[rows, 2 * GLA_QK_W:2 * GLA_QK_W + GLA_V_W] = dv
            dz_ref[rows, :] = dz
            dbg += dbg_c
        dst_s[...] = dst
        dbg_ref[...] += dbg

    order = lambda t: nsteps - 1 - t
    rowv = pl.BlockSpec((r, GLA_V_W), lambda t: (order(t), 0))
    return pl.pallas_call(
        body, name=name, grid=(nsteps,),
        in_specs=_gla_in_specs(order) + [
            rowv, pl.BlockSpec((cps, GLA_V_W, GLA_QK_W), lambda t: (order(t), 0, 0)),
            pl.BlockSpec((r, GLA_V_W), lambda t: (order(t), SWA_Q_W // GLA_V_W))],
        out_specs=[pl.BlockSpec((r, w_qkvg), lambda t: (order(t), 0)), pl.BlockSpec((r, GLA_QK_W), lambda t: (order(t), 0)),
                   pl.BlockSpec((1, GLA_QK_W), lambda t: (0, 0)), pl.BlockSpec((1, GLA_V_W), lambda t: (0, 0))],
        out_shape=[SDS((s, w_qkvg), F32), SDS((s, GLA_QK_W), F32), SDS((1, GLA_QK_W), F32), SDS((1, GLA_V_W), F32)],
        scratch_shapes=[pltpu.VMEM((GLA_V_W, GLA_QK_W), F32)],
        compiler_params=_params("arbitrary"),
    )(p, p, p, p, z, bg, gain, oraw, stsave, dy_all)


def _local_step(x, mem, target, small, big, on_grads):
    g1, gmix, gmem, g2, sqg, skg, sinks, rel_bias, wgu, bg, gla_gain, mqg, mkg = small
    wg1, wu1, wd1, win_p, wkv, wout, wg2, wu2, wd2 = big
    wgu_pad = jnp.zeros((GLA_QK_W, GLA_QK_W), BF16).at[:GLA_RANK].set(wgu.astype(BF16))
    gain256 = jnp.tile(gla_gain, (1, GLA_HEADS))

    x1, saved1 = _ffn_fwd(x, g1, wg1, wu1, wd1, "ffn1")
    h = _rms_fwd(x1, gmix, tm=256, name="mix_rms")
    p = _matmul(h, win_p, tm=512, tn=384, name="mix_in")
    hm = _rms_fwd(mem, gmem, tm=256, name="mem_rms")
    kv = _matmul(hm, wkv, tm=256, tn=512, name="mem_kv")
    p_glr = p[:, COL_GLR:]
    z = _matmul(p_glr, wgu_pad, tm=512, tn=GLA_QK_W, name="gla_gate")
    y_swa = _swa_fwd(p, sqg, skg, sinks, rel_bias, name="swa_fwd")
    y_gla, oraw, stsave = _gla_fwd(p, z, bg, gain256, name="gla_fwd")
    y_mem = _mem_fwd(p, kv, mqg, mkg, tq=256, name="mem_fwd")
    x2 = _matmul([y_swa, y_gla, y_mem], wout, b_blocks=[0, 2, 3], tm=512, tn=512, res=x1, name="mix_out")
    x3, saved2 = _ffn_fwd(x2, g2, wg2, wu2, wd2, "ffn2")

    dy, dyb, loss = _loss_bwd(x3, target, tm=256, name="loss")
    dx2, dx2b, dg2, dwg2, dwu2, dwd2 = _ffn_bwd(dy, dyb, x2, g2, wg2, wu2, wd2, saved2, "ffn2")
    on_grads("ffn2", (dwg2, dwu2, dwd2))
    dy_all = _matmul(dx2b, wout, tb=True, tm=512, tn=512, name="mix_dy")
    dwout = _matmul(jnp.concatenate([y_swa, y_gla, y_mem], axis=1), dx2b, ta=True, tm=512, tn=512, name="mix_dw_out")
    dq_swa, dkv_swa, dsqg, dskg, dsink, drb = _swa_bwd(p, sqg, skg, sinks, rel_bias, dy_all, name="swa_bwd")
    dqkvg, dz, dbg, dgain256 = _gla_bwd(p, z, bg, gain256, oraw, stsave, dy_all, name="gla_bwd")
    dmq, dkv_mem, dmqg, dmkg = _mem_bwd(p, kv, mqg, mkg, dy_all, tq=256, name="mem_bwd")
    dglr = _matmul(dz, wgu_pad, tb=True, tm=512, tn=GLA_QK_W, name="gla_gate_dx")
    dwgu_pad = _matmul(p_glr, dz, ta=True, tm=GLA_QK_W, tn=GLA_QK_W, name="gla_gate_dw")
    dp = jnp.concatenate([dq_swa, dkv_swa, dqkvg, dmq, dglr], axis=1)
    dh = _matmul(dp, win_p, tb=True, tm=512, tn=512, name="mix_dh")
    dwin_p = _matmul(h, dp, ta=True, tm=512, tn=384, name="mix_dw_in")
    dx1, dx1b, dgmix = _rms_bwd(x1, gmix, dh, dx2, tm=256, name="mix_drms")
    dwkv = _matmul(hm, dkv_mem, ta=True, tm=512, tn=512, name="mem_dw_kv")
    on_grads("mix", (dwin_p, dwkv, dwout))
    dhm = _matmul(dkv_mem, wkv, tb=True, tm=256, tn=512, name="mem_dh")
    _, _, dgmem = _rms_bwd(mem, gmem, dhm, None, tm=256, name="mem_drms")
    dx, _, dg1, dwg1, dwu1, dwd1 = _ffn_bwd(dx1, dx1b, x, g1, wg1, wu1, wd1, saved1, "ffn1")

    dgla_gain = dgain256.reshape(GLA_HEADS, GLA_DV).sum(axis=0, keepdims=True)
    dsmall = (dg1, dgmix, dgmem, dg2, dsqg, dskg, dsink[0, :SWA_HEADS], drb[:, :SWA_HEADS], dwgu_pad[:GLA_RANK], dbg,
              dgla_gain, dmqg, dmkg)
    on_grads("ffn1", (dwg1, dwu1, dwd1), small=list(dsmall) + [loss])
    return dx


def _mesh_place():
    x, y, c = lax.axis_index("x"), lax.axis_index("y"), lax.axis_index("c")
    other_chips = [(1 - x, y), (x, 1 - y), (1 - x, 1 - y)]
    return x, y, c, other_chips


def _handshake(peers):
    barrier = pltpu.get_barrier_semaphore()
    for peer in peers:
        pl.semaphore_signal(barrier, inc=1, device_id=peer, device_id_type=MESH)
    pl.semaphore_wait(barrier, len(peers))


def _sequencer_call(body, operands, out_shapes, sems, *, name, collective_id):
    return pl.kernel(
        body, name=name, out_type=out_shapes, mesh=plsc.ScalarSubcoreMesh(axis_name="sequencer", num_cores=1),
        scratch_types=sems, compiler_params=pltpu.CompilerParams(collective_id=collective_id),
    )(*operands)


def _all_gather(shards, *, name, collective_id):
    nt = len(shards)

    def body(*refs):
        x_refs, o_refs = refs[:nt], refs[nt:2 * nt]
        send_sems, recv_sems, local_sems = refs[2 * nt:]
        x, y, c, chips = _mesh_place()
        me, sibling = (x, y, c), (x, y, 1 - c)
        _handshake([sibling] + [(*chip, c) for chip in chips])

        def copy(k, t, block, to, from_shard=False):
            bx, by, bc = block
            rows = o_refs[t].at[4 * bx + 2 * by + bc]
            return pltpu.make_async_remote_copy(
                src_ref=x_refs[t] if from_shard else rows, dst_ref=rows,
                send_sem=send_sems.at[k, t], recv_sem=recv_sems.at[k, t], device_id=to, device_id_type=MESH)

        mine = [pltpu.make_async_copy(x_refs[t], o_refs[t].at[4 * x + 2 * y + c], local_sems.at[t]) for t in range(nt)]
        for cp in mine:
            cp.start()
        first = [copy(0, t, me, sibling, True) for t in range(nt)]
        first += [copy(1 + j, t, me, (*chip, c), True) for j, chip in enumerate(chips) for t in range(nt)]
        for cp in first:
            cp.start()
        passed = []
        for j, chip in enumerate(chips):
            for t in range(nt):
                copy(1 + j, t, (*chip, c), me).wait_recv()
                fwd = copy(4 + j, t, (*chip, c), sibling)
                fwd.start()
                passed.append(fwd)
        for t in range(nt):
            copy(0, t, sibling, me).wait_recv()
        for j, chip in enumerate(chips):
            for t in range(nt):
                copy(4 + j, t, (*chip, 1 - c), me).wait_recv()
        for cp in first + passed:
            cp.wait_send()
        for cp in mine:
            cp.wait()

    return _sequencer_call(
        body, shards, [SDS((N_DEV,) + s.shape, s.dtype) for s in shards],
        [pltpu.SemaphoreType.DMA((7, nt)), pltpu.SemaphoreType.DMA((7, nt)), pltpu.SemaphoreType.DMA((nt,))],
        name=name, collective_id=collective_id)


def _pair_exchange(parts, *, name, collective_id):
    nt = len(parts)

    def body(*refs):
        g_refs, o_refs = refs[:nt], refs[nt:2 * nt]
        send_sems, recv_sems = refs[2 * nt:]
        x, y, c, _ = _mesh_place()
        _handshake([(x, y, 1 - c)])
        copies = [pltpu.make_async_remote_copy(
            src_ref=g_refs[t].at[1 - c], dst_ref=o_refs[t], send_sem=send_sems.at[t], recv_sem=recv_sems.at[t],
            device_id=(x, y, 1 - c), device_id_type=MESH) for t in range(nt)]
        for cp in copies:
            cp.start()
        for cp in copies:
            cp.wait()

    return _sequencer_call(
        body, parts, [SDS(g.shape[1:], g.dtype) for g in parts],
        [pltpu.SemaphoreType.DMA((nt,)), pltpu.SemaphoreType.DMA((nt,))], name=name, collective_id=collective_id)


def _chip_exchange(parts, small, *, name, collective_id):
    nt = len(parts)
    if small is None:
        def body_plain(*refs):
            s_refs, o_refs = refs[:nt], refs[nt:2 * nt]
            send_sems, recv_sems = refs[2 * nt:]
            x, y, c, chips = _mesh_place()
            _handshake([(*chip, c) for chip in chips])
            copies = [pltpu.make_async_remote_copy(
                src_ref=s_refs[t].at[2 * chip[0] + chip[1]], dst_ref=o_refs[t].at[j],
                send_sem=send_sems.at[j, t], recv_sem=recv_sems.at[j, t], device_id=(*chip, c), device_id_type=MESH)
                for j, chip in enumerate(chips) for t in range(nt)]
            for cp in copies:
                cp.start()
            for cp in copies:
                cp.wait()

        return _sequencer_call(
            body_plain, parts, [SDS((3,) + s.shape[1:], s.dtype) for s in parts],
            [pltpu.SemaphoreType.DMA((3, nt)), pltpu.SemaphoreType.DMA((3, nt))], name=name, collective_id=collective_id)

    def body(*refs):
        s_refs, small_ref = refs[:nt], refs[nt]
        o_refs, small_all = refs[nt + 1:2 * nt + 1], refs[2 * nt + 1]
        send_sems, recv_sems, small_send, small_recv, local_sem = refs[2 * nt + 2:]
        x, y, c, chips = _mesh_place()
        _handshake([(px, py, pc) for px in (x, 1 - x) for py in (y, 1 - y) for pc in (c, 1 - c)][1:])

        def copy(j, t, chip):
            return pltpu.make_async_remote_copy(
                src_ref=s_refs[t].at[2 * chip[0] + chip[1]], dst_ref=o_refs[t].at[j],
                send_sem=send_sems.at[j, t], recv_sem=recv_sems.at[j, t], device_id=(*chip, c), device_id_type=MESH)

        flips = [(fx, fy, fc) for fx in (0, 1) for fy in (0, 1) for fc in (0, 1)][1:]

        def small_copy(k):
            fx, fy, fc = flips[k]
            to = (x ^ fx if fx else x, y ^ fy if fy else y, c ^ fc if fc else c)
            rows = small_all.at[4 * x + 2 * y + c]
            return pltpu.make_async_remote_copy(
                src_ref=small_ref, dst_ref=rows, send_sem=small_send.at[k], recv_sem=small_recv.at[k],
                device_id=to, device_id_type=MESH)

        own = pltpu.make_async_copy(small_ref, small_all.at[4 * x + 2 * y + c], local_sem)
        own.start()
        copies = [copy(j, t, chip) for j, chip in enumerate(chips) for t in range(nt)]
        smalls = [small_copy(k) for k in range(7)]
        for cp in smalls + copies:
            cp.start()
        for cp in smalls + copies:
            cp.wait()
        own.wait()

    return _sequencer_call(
        body, list(parts) + [small],
        [SDS((3,) + s.shape[1:], s.dtype) for s in parts] + [SDS((N_DEV,) + small.shape, small.dtype)],
        [pltpu.SemaphoreType.DMA((3, nt)), pltpu.SemaphoreType.DMA((3, nt)),
         pltpu.SemaphoreType.DMA((7,)), pltpu.SemaphoreType.DMA((7,)), pltpu.SemaphoreType.DMA],
        name=name, collective_id=collective_id)


def _pair_sum(mine, theirs, c, *, tr, name):
    _, r, l = mine.shape
    assert r % tr == 0, (r, tr)

    def body(c_ref, a_ref, b_ref, o_ref):
        o_ref[...] = (a_ref[0].astype(F32) + b_ref[...].astype(F32)).astype(BF16)

    return pl.pallas_call(
        body, name=name,
        grid_spec=pltpu.PrefetchScalarGridSpec(
            num_scalar_prefetch=1, grid=(r // tr,),
            in_specs=[pl.BlockSpec((1, tr, l), lambda i, c_ref: (c_ref[0], i, 0)),
                      pl.BlockSpec((tr, l), lambda i, c_ref: (i, 0))],
            out_specs=pl.BlockSpec((tr, l), lambda i, c_ref: (i, 0))),
        out_shape=SDS((r, l), BF16), compiler_params=_params("parallel"),
    )(c, mine, theirs)


def _adamw(w, g, m, v):
    m = ADAM_B1 * m + (1.0 - ADAM_B1) * g
    v = ADAM_B2 * v + (1.0 - ADAM_B2) * jnp.square(g)
    m_hat = m / (1.0 - ADAM_B1 ** ADAM_STEP)
    v_hat = v / (1.0 - ADAM_B2 ** ADAM_STEP)
    delta = -ADAM_LR * (m_hat / (jnp.sqrt(v_hat) + ADAM_EPS) + ADAM_WD * w)
    return delta, m, v


def _adam_big(own, others, mat, xy, w, m, v, *, tr, name):
    _, r, l = w.shape

    def body(xy_ref, own_ref, oth_ref, w_ref, m_ref, v_ref, g_out, d_out, m_out, v_out):
        g = own_ref[0, 0].astype(F32)
        for j in range(3):
            g = g + oth_ref[j, 0].astype(F32)
        delta, m_new, v_new = _adamw(w_ref[0], g, m_ref[0], v_ref[0])
        g_out[0] = g
        d_out[0] = delta
        m_out[0] = m_new
        v_out[0] = v_new

    blk = pl.BlockSpec((1, tr, l), lambda i, xy_ref: (0, i, 0))
    return pl.pallas_call(
        body, name=name,
        grid_spec=pltpu.PrefetchScalarGridSpec(
            num_scalar_prefetch=1, grid=(r // tr,),
            in_specs=[pl.BlockSpec((1, 1, tr, l), lambda i, xy_ref: (xy_ref[0], mat, i, 0)),
                      pl.BlockSpec((3, 1, tr, l), lambda i, xy_ref: (0, mat, i, 0)), blk, blk, blk],
            out_specs=[blk, blk, blk, blk]),
        out_shape=[SDS(w.shape, F32)] * 4, compiler_params=_params("parallel"),
    )(xy, own, others, w, m, v)


def _adam_small(g_all, w, m, v, *, name):
    def body(g_ref, w_ref, m_ref, v_ref, g_out, d_out, m_out, v_out):
        g = g_ref[0]
        for k in range(1, N_DEV):
            g = g + g_ref[k]
        delta, m_new, v_new = _adamw(w_ref[...], g, m_ref[...], v_ref[...])
        g_out[...] = g
        d_out[...] = delta
        m_out[...] = m_new
        v_out[...] = v_new

    return pl.pallas_call(body, name=name, out_shape=[SDS(w.shape, F32)] * 4)(g_all, w, m, v)


SMALL_ROWS = 56


def _pack_small(parts):
    flat = jnp.concatenate([a.reshape(-1) for a in parts])
    return jnp.pad(flat, (0, SMALL_ROWS * 128 - flat.shape[0])).reshape(SMALL_ROWS, 128)


def _unpack_small(packed, shapes):
    flat = packed.reshape(-1)
    out, at = [], 0
    for s in shapes:
        n = math.prod(s)
        out.append(flat[at:at + n].reshape(s))
        at += n
    return out


def kernel(x, mem, ffn1_norm, ffn1_w_gate, ffn1_w_up, ffn1_w_down, mix_norm, mem_norm, w_in, w_mem_kv, swa_q_norm, swa_k_norm, swa_sinks, rel_bias, gla_w_gate_up, gla_b_gate, gla_out_norm, mem_q_norm, mem_k_norm, w_out, ffn2_norm, ffn2_w_gate, ffn2_w_up, ffn2_w_down, loss_target, m_ffn1_norm, m_ffn1_w_gate, m_ffn1_w_up, m_ffn1_w_down, m_mix_norm, m_mem_norm, m_w_in, m_w_mem_kv, m_swa_q_norm, m_swa_k_norm, m_swa_sinks, m_rel_bias, m_gla_w_gate_up, m_gla_b_gate, m_gla_out_norm, m_mem_q_norm, m_mem_k_norm, m_w_out, m_ffn2_norm, m_ffn2_w_gate, m_ffn2_w_up, m_ffn2_w_down, v_ffn1_norm, v_ffn1_w_gate, v_ffn1_w_up, v_ffn1_w_down, v_mix_norm, v_mem_norm, v_w_in, v_w_mem_kv, v_swa_q_norm, v_swa_k_norm, v_swa_sinks, v_rel_bias, v_gla_w_gate_up, v_gla_b_gate, v_gla_out_norm, v_mem_q_norm, v_mem_k_norm, v_w_out, v_ffn2_norm, v_ffn2_w_gate, v_ffn2_w_up, v_ffn2_w_down):
    xi, yi, ci = lax.axis_index("x"), lax.axis_index("y"), lax.axis_index("c")
    c_arr = jnp.reshape(ci, (1,)).astype(jnp.int32)
    xy_arr = jnp.reshape(2 * xi + yi, (1,)).astype(jnp.int32)
    d = x.shape[-1]

    def col_full(a):
        return a.transpose(1, 0, 2).reshape(a.shape[1], -1)

    def gather_ffn(wg_s, wu_s, wd_s, name, collective_id):
        col_all, row_all = _all_gather([jnp.concatenate([wg_s, wu_s], axis=0).astype(BF16), wd_s[0].astype(BF16)],
                                       name=name, collective_id=collective_id)
        return col_full(col_all[:, 0]), col_full(col_all[:, 1]), row_all.reshape(-1, d)

    wg1, wu1, wd1 = gather_ffn(ffn1_w_gate, ffn1_w_up, ffn1_w_down, "gather_ffn1", 0)
    win_all, wkv_all, wout_all = _all_gather([w_in[0].astype(BF16), w_mem_kv[0].astype(BF16), w_out[0].astype(BF16)],
                                             name="gather_mix", collective_id=1)
    wg2, wu2, wd2 = gather_ffn(ffn2_w_gate, ffn2_w_up, ffn2_w_down, "gather_ffn2", 2)
    win = col_full(win_all)
    glr_lo, glr_hi = COL_MQ, COL_MQ + GLA_RANK
    win_p = jnp.concatenate([win[:, :glr_lo], win[:, glr_hi:], win[:, glr_lo:glr_hi],
                             jnp.zeros((d, IN_W_PAD - IN_W), BF16)], axis=1)
    wkv = wkv_all.reshape(-1, wkv_all.shape[-1])
    wout = wout_all.reshape(-1, d)

    small_w = [ffn1_norm, mix_norm, mem_norm, ffn2_norm, swa_q_norm, swa_k_norm, swa_sinks[0], rel_bias,
               gla_w_gate_up[0], gla_b_gate, gla_out_norm, mem_q_norm, mem_k_norm]
    def col_dst(g):
        w = g.shape[1] // N_DEV
        return g.reshape(g.shape[0], 4, 2, w).transpose(2, 1, 0, 3).astype(BF16)

    def row_dst(g):
        return g.reshape(4, 2, g.shape[0] // N_DEV, g.shape[1]).transpose(1, 0, 2, 3).astype(BF16)

    collective_ids = {"ffn2": (3, 4), "mix": (5, 6), "ffn1": (7, 8)}
    reduced, small_box = {}, []

    def on_grads(group, grads, small=None):
        if group == "mix":
            dwin_p, dwkv, dwout = grads
            dwin = jnp.concatenate([dwin_p[:, :glr_lo], dwin_p[:, COL_GLR:COL_GLR + GLA_RANK], dwin_p[:, COL_MQ:COL_GLR]],
                                   axis=1)
            parts = [col_dst(dwin)[:, :, None], row_dst(dwkv)[:, :, None], row_dst(dwout)[:, :, None]]
        else:
            dwg, dwu, dwd = grads
            parts = [jnp.stack([col_dst(dwg), col_dst(dwu)], axis=2), row_dst(dwd)[:, :, None]]
        id_pair, id_chip = collective_ids[group]
        from_sibling = _pair_exchange(parts, name=f"pair_exchange_{group}", collective_id=id_pair)
        chip_sums = []
        for t, (mine, theirs) in enumerate(zip(parts, from_sibling)):
            l = mine.shape[-1]
            rows = theirs.size // l
            tr = next(t for t in (512, 352, 256, 128) if rows % t == 0)
            flat = _pair_sum(mine.reshape(2, -1, l), theirs.reshape(-1, l), c_arr, tr=tr, name=f"pair_sum_{group}_{t}")
            chip_sums.append(flat.reshape(mine.shape[1:]))
        if small is None:
            from_chips = _chip_exchange(chip_sums, None, name=f"chip_exchange_{group}", collective_id=id_chip)
        else:
            *from_chips, small_all = _chip_exchange(chip_sums, _pack_small(small), name=f"chip_exchange_{group}",
                                                    collective_id=id_chip)
            small_box.append(small_all)
        reduced[group] = (chip_sums, from_chips)

    grad_x = _local_step(x[0], mem[0], loss_target[0], small_w, (wg1, wu1, wd1, win_p, wkv, wout, wg2, wu2, wd2), on_grads)
    small_all = small_box[0]

    big_w = {"ffn1_w_gate": ("ffn1", 0, 0, ffn1_w_gate, m_ffn1_w_gate, v_ffn1_w_gate),
             "ffn1_w_up": ("ffn1", 0, 1, ffn1_w_up, m_ffn1_w_up, v_ffn1_w_up),
             "ffn1_w_down": ("ffn1", 1, 0, ffn1_w_down, m_ffn1_w_down, v_ffn1_w_down),
             "w_in": ("mix", 0, 0, w_in, m_w_in, v_w_in),
             "w_mem_kv": ("mix", 1, 0, w_mem_kv, m_w_mem_kv, v_w_mem_kv),
             "w_out": ("mix", 2, 0, w_out, m_w_out, v_w_out),
             "ffn2_w_gate": ("ffn2", 0, 0, ffn2_w_gate, m_ffn2_w_gate, v_ffn2_w_gate),
             "ffn2_w_up": ("ffn2", 0, 1, ffn2_w_up, m_ffn2_w_up, v_ffn2_w_up),
             "ffn2_w_down": ("ffn2", 1, 0, ffn2_w_down, m_ffn2_w_down, v_ffn2_w_down)}
    res = {}
    for nm, (group, t, mat, w, m, v) in big_w.items():
        r = w.shape[1]
        tr = 256 if r % 256 == 0 else r
        chip_sums, from_chips = reduced[group]
        res[nm] = _adam_big(chip_sums[t], from_chips[t], mat, xy_arr, w, m, v, tr=tr, name=f"adam_{nm}")
    small_names = ["ffn1_norm", "mix_norm", "mem_norm", "ffn2_norm", "swa_q_norm", "swa_k_norm", "swa_sinks", "rel_bias",
                   "gla_w_gate_up", "gla_b_gate", "gla_out_norm", "mem_q_norm", "mem_k_norm"]
    small_m = [m_ffn1_norm, m_mix_norm, m_mem_norm, m_ffn2_norm, m_swa_q_norm, m_swa_k_norm, m_swa_sinks, m_rel_bias,
               m_gla_w_gate_up, m_gla_b_gate, m_gla_out_norm, m_mem_q_norm, m_mem_k_norm]
    small_v = [v_ffn1_norm, v_mix_norm, v_mem_norm, v_ffn2_norm, v_swa_q_norm, v_swa_k_norm, v_swa_sinks, v_rel_bias,
               v_gla_w_gate_up, v_gla_b_gate, v_gla_out_norm, v_mem_q_norm, v_mem_k_norm]
    small_full = [ffn1_norm, mix_norm, mem_norm, ffn2_norm, swa_q_norm, swa_k_norm, swa_sinks, rel_bias,
                  gla_w_gate_up, gla_b_gate, gla_out_norm, mem_q_norm, mem_k_norm]
    packed = _adam_small(small_all, _pack_small(small_full), _pack_small(small_m), _pack_small(small_v), name="adam_small")
    full_shapes = [a.shape for a in small_full]
    unpacked = [_unpack_small(pk, full_shapes + [()]) for pk in packed]
    for k, nm in enumerate(small_names):
        res[nm] = [unpacked[q][k] for q in range(4)]
    loss = unpacked[0][len(small_names)]

    order = ["ffn1_norm", "ffn1_w_gate", "ffn1_w_up", "ffn1_w_down", "mix_norm", "mem_norm", "w_in", "w_mem_kv",
             "swa_q_norm", "swa_k_norm", "swa_sinks", "rel_bias", "gla_w_gate_up", "gla_b_gate", "gla_out_norm",
             "mem_q_norm", "mem_k_norm", "w_out", "ffn2_norm", "ffn2_w_gate", "ffn2_w_up", "ffn2_w_down"]
    outs = [loss, grad_x[None]]
    for q in range(4):
        outs += [res[nm][q] for nm in order]
    return tuple(outs)
```

```python
import functools
import math

import numpy as np
import jax
import jax.numpy as jnp
from jax import lax
from jax.experimental import pallas as pl
from jax.experimental.pallas import tpu as pltpu
from jax.experimental.pallas import tpu_sc as plsc

F32 = jnp.float32
BF16 = jnp.bfloat16
SDS = jax.ShapeDtypeStruct

EPS = 1e-6
HEAD_DIM = 64
SWA_HEADS = 8
SWA_KV_HEADS = 2
SWA_GROUP = SWA_HEADS // SWA_KV_HEADS
BLOCK = 128
N_BUCKETS = 32
MAX_DISTANCE = 128
GLA_HEADS = 4
GLA_DK = 32
GLA_DV = 64
GLA_RANK = 16
GLA_TAU = 16.0
GLA_CHUNK = 32
MEM_HEADS = 4
SWA_Q_W = SWA_HEADS * HEAD_DIM
SWA_KV_W = SWA_KV_HEADS * HEAD_DIM
GLA_QK_W = GLA_HEADS * GLA_DK
GLA_V_W = GLA_HEADS * GLA_DV
MEM_Q_W = MEM_HEADS * HEAD_DIM
IN_W = 1808
IN_W_PAD = 1920
COL_SQ, COL_SKV, COL_GQ, COL_GK, COL_GV, COL_GG, COL_MQ, COL_GLR = 0, 512, 768, 896, 1024, 1280, 1536, 1792

ADAM_LR = 0.001
ADAM_B1 = 0.9
ADAM_B2 = 0.999
ADAM_EPS = 1e-08
ADAM_WD = 0.01
ADAM_STEP = 10

N_DEV = 8
VMEM_LIMIT_BYTES = 56 * 1024 * 1024
MESH = pl.DeviceIdType.MESH


def _params(*sem):
    return pltpu.CompilerParams(dimension_semantics=sem or None, vmem_limit_bytes=VMEM_LIMIT_BYTES)


def _dot(a, b, ta, tb, precision=None):
    dims = (((0 if ta else 1,), (1 if tb else 0,)), ((), ()))
    return lax.dot_general(a, b, dims, preferred_element_type=F32, precision=precision)


def _mm_raw(a, b, ta=False, tb=False):
    return _dot(a.astype(BF16), b.astype(BF16), ta, tb)


def _mmf_raw(a, b, ta=False, tb=False):
    return _dot(a, b, ta, tb, lax.Precision.HIGHEST)


def _make_mm(raw):
    @functools.partial(jax.custom_vjp, nondiff_argnums=(2, 3))
    def mm(a, b, ta=False, tb=False):
        return raw(a, b, ta, tb)

    def fwd(a, b, ta, tb):
        return raw(a, b, ta, tb), (a, b)

    def bwd(ta, tb, res, g):
        a, b = res
        da = raw(b, g, tb, True) if ta else raw(g, b, False, not tb)
        db = raw(g, a, True, ta) if tb else raw(a, g, not ta, False)
        return da, db

    mm.defvjp(fwd, bwd)
    return mm


_mm = _make_mm(_mm_raw)
_mmf = _make_mm(_mmf_raw)


def _rms(x, g):
    return x * lax.rsqrt(jnp.mean(x * x, axis=-1, keepdims=True) + EPS) * g


def _silu_mul(g, u):
    return jax.nn.silu(g) * u


def _log_sigmoid(z):
    return jnp.minimum(z, 0.0) - jnp.log(1.0 + jnp.exp(-jnp.abs(z)))


def _matmul(a_list, b, *, ta=False, tb=False, tm, tn, b_blocks=None, res=None, scale=1.0, out_dtype=F32, name):
    if not isinstance(a_list, (list, tuple)):
        a_list = [a_list]
    n_a = len(a_list)
    m = a_list[0].shape[1] if ta else a_list[0].shape[0]
    ks = [a.shape[0] if ta else a.shape[1] for a in a_list]
    n = b.shape[0] if tb else b.shape[1]
    if b_blocks is None:
        assert n_a == 1
        b_blocks = [0]
    assert m % tm == 0 and n % tn == 0, (m, n, tm, tn)

    def body(*refs):
        a_refs, b_refs = refs[:n_a], refs[n_a:2 * n_a]
        r_ref = refs[2 * n_a] if res is not None else None
        o_ref = refs[-1]
        acc = _mm_raw(a_refs[0][...], b_refs[0][...], ta, tb)
        for k in range(1, n_a):
            acc = acc + _mm_raw(a_refs[k][...], b_refs[k][...], ta, tb)
        if scale != 1.0:
            acc = acc * scale
        if r_ref is not None:
            acc = r_ref[...] + acc
        o_ref[...] = acc.astype(out_dtype)

    in_specs = []
    for k in ks:
        in_specs.append(pl.BlockSpec((k, tm), lambda i, j: (0, i)) if ta else pl.BlockSpec((tm, k), lambda i, j: (i, 0)))
    for k, blk in zip(ks, b_blocks):
        if tb:
            in_specs.append(pl.BlockSpec((tn, k), functools.partial(lambda i, j, blk: (j, blk), blk=blk)))
        else:
            in_specs.append(pl.BlockSpec((k, tn), functools.partial(lambda i, j, blk: (blk, j), blk=blk)))
    args = list(a_list) + [b] * n_a
    if res is not None:
        in_specs.append(pl.BlockSpec((tm, tn), lambda i, j: (i, j)))
        args.append(res)
    return pl.pallas_call(
        body, name=name, grid=(m // tm, n // tn), in_specs=in_specs,
        out_specs=pl.BlockSpec((tm, tn), lambda i, j: (i, j)), out_shape=SDS((m, n), out_dtype),
        compiler_params=_params("parallel", "parallel"),
    )(*args)


def _rms_fwd(x, g, *, tm, name):
    s, d = x.shape

    def body(x_ref, g_ref, h_ref):
        h_ref[...] = _rms(x_ref[...], g_ref[...]).astype(BF16)

    return pl.pallas_call(
        body, name=name, grid=(s // tm,),
        in_specs=[pl.BlockSpec((tm, d), lambda i: (i, 0)), pl.BlockSpec((1, d), lambda i: (0, 0))],
        out_specs=pl.BlockSpec((tm, d), lambda i: (i, 0)), out_shape=SDS((s, d), BF16),
        compiler_params=_params("parallel"),
    )(x, g)


def _rms_bwd(x, g, dh, dres, *, tm, name):
    s, d = x.shape
    want_dx = dres is not None

    def body(*refs):
        if want_dx:
            x_ref, g_ref, dh_ref, dres_ref, dx_ref, dxb_ref, dg_ref = refs
        else:
            x_ref, g_ref, dh_ref, dg_ref = refs
        _, vjp = jax.vjp(_rms, x_ref[...], g_ref[...])
        dx, dg = vjp(dh_ref[...])
        if want_dx:
            dx = dres_ref[...] + dx
            dx_ref[...] = dx
            dxb_ref[...] = dx.astype(BF16)

        @pl.when(pl.program_id(0) == 0)
        def _():
            dg_ref[...] = jnp.zeros_like(dg_ref)

        dg_ref[...] += dg

    row = pl.BlockSpec((tm, d), lambda i: (i, 0))
    vec = pl.BlockSpec((1, d), lambda i: (0, 0))
    if want_dx:
        return pl.pallas_call(
            body, name=name, grid=(s // tm,), in_specs=[row, vec, row, row], out_specs=[row, row, vec],
            out_shape=[SDS((s, d), F32), SDS((s, d), BF16), SDS((1, d), F32)], compiler_params=_params("arbitrary"),
        )(x, g, dh, dres)
    return None, None, pl.pallas_call(
        body, name=name, grid=(s // tm,), in_specs=[row, vec, row], out_specs=vec,
        out_shape=SDS((1, d), F32), compiler_params=_params("arbitrary"),
    )(x, g, dh)


FFN_TN = 256
FFN_SHARD_PAD = 384


def _ffn_fwd(x, gain, wgu, wd, tag, *, tm=1024):
    s, d = x.shape
    f = wd.shape[0]
    tn = FFN_TN
    nj = f // tn
    tm = min(tm, s)

    def body(x_ref, gain_ref, wg_ref, wu_ref, wd_ref, y_ref, h_ref, g_ref, u_ref, acc_s):
        j = pl.program_id(1)

        @pl.when(j == 0)
        def _():
            h_ref[...] = _rms(x_ref[...], gain_ref[...]).astype(BF16)
            acc_s[...] = jnp.zeros_like(acc_s)

        hv = h_ref[...]
        g = _mm_raw(hv, wg_ref[...])
        u = _mm_raw(hv, wu_ref[...])
        g_ref[...] = g.astype(BF16)
        u_ref[...] = u.astype(BF16)
        acc_s[...] += _mm_raw(_silu_mul(g, u), wd_ref[...])

        @pl.when(j == nj - 1)
        def _():
            y_ref[...] = x_ref[...] + 0.5 * acc_s[...]

    row = pl.BlockSpec((tm, d), lambda i, j: (i, 0))
    tile = pl.BlockSpec((tm, tn), lambda i, j: (i, j))
    y, h, g, u = pl.pallas_call(
        body, name=f"{tag}_fwd", grid=(s // tm, nj),
        in_specs=[row, pl.BlockSpec((1, d), lambda i, j: (0, 0)),
                  pl.BlockSpec((None, d, tn), lambda i, j: (0, 0, j)), pl.BlockSpec((None, d, tn), lambda i, j: (1, 0, j)),
                  pl.BlockSpec((tn, d), lambda i, j: (j, 0))],
        out_specs=[row, row, tile, tile],
        out_shape=[SDS((s, d), F32), SDS((s, d), BF16), SDS((s, f), BF16), SDS((s, f), BF16)],
        scratch_shapes=[pltpu.VMEM((tm, d), F32)],
        compiler_params=_params("parallel", "arbitrary"),
    )(x, gain, wgu, wgu, wd)
    return y, (h, g, u)


def _ffn_bwd(dy, dyb, x, gain, wgu, wd, saved, tag):
    h, g, u = saved
    s, d = x.shape
    f = wd.shape[0]
    tn = FFN_TN

    def body(dy_ref, ht_ref, wg_ref, wu_ref, wd_ref, g_ref, u_ref, dh_ref, dwgu_ref, dwd_ref):
        @pl.when(pl.program_id(0) == 0)
        def _():
            dh_ref[...] = jnp.zeros_like(dh_ref)

        dyv = dy_ref[...]
        da = _mm_raw(dyv, wd_ref[...], False, True) * 0.5
        a, vjp = jax.vjp(_silu_mul, g_ref[...].astype(F32), u_ref[...].astype(F32))
        dg, du = vjp(da)
        dg = dg.astype(BF16)
        du = du.astype(BF16)
        dh_ref[...] += _mm_raw(dg, wg_ref[...], False, True) + _mm_raw(du, wu_ref[...], False, True)
        dwgu_ref[0] = _mm_raw(ht_ref[...], dg).astype(BF16)
        dwgu_ref[1] = _mm_raw(ht_ref[...], du).astype(BF16)
        dwd_ref[...] = (_mm_raw(a, dyv, True, False) * 0.5).astype(BF16)

    full = pl.BlockSpec((s, d), lambda j: (0, 0))
    tile = pl.BlockSpec((s, tn), lambda j: (0, j))
    wrow = pl.BlockSpec((tn, d), lambda j: (j, 0))
    dh, dwgu, dwd = pl.pallas_call(
        body, name=f"{tag}_bwd", grid=(f // tn,),
        in_specs=[full, pl.BlockSpec((d, s), lambda j: (0, 0)),
                  pl.BlockSpec((None, d, tn), lambda j: (0, 0, j)), pl.BlockSpec((None, d, tn), lambda j: (1, 0, j)),
                  wrow, tile, tile],
        out_specs=[full, pl.BlockSpec((2, d, tn), lambda j: (0, 0, j)), wrow],
        out_shape=[SDS((s, d), F32), SDS((2, d, f), BF16), SDS((f, d), BF16)],
        compiler_params=_params("arbitrary"),
    )(dyb, h.T, wgu, wgu, wd, g, u)
    dx, dxb, dgain = _rms_bwd(x, gain, dh, dy, tm=256, name=f"{tag}_drms")
    return dx, dxb, dgain, dwgu, dwd


def _loss_bwd(y, target, *, tm, name):
    s, d = y.shape

    def body(y_ref, t_ref, dy_ref, dyb_ref, l_ref):
        diff = y_ref[...] - t_ref[...]
        dy_ref[...] = diff * (1.0 / d)
        dyb_ref[...] = (diff * (1.0 / d)).astype(BF16)

        @pl.when(pl.program_id(0) == 0)
        def _():
            l_ref[...] = jnp.zeros_like(l_ref)

        l_ref[...] += 0.5 * jnp.sum(jnp.mean(diff * diff, axis=-1, keepdims=True), axis=0, keepdims=True)

    row = pl.BlockSpec((tm, d), lambda i: (i, 0))
    return pl.pallas_call(
        body, name=name, grid=(s // tm,), in_specs=[row, row],
        out_specs=[row, row, pl.BlockSpec((1, 1), lambda i: (0, 0))],
        out_shape=[SDS((s, d), F32), SDS((s, d), BF16), SDS((1, 1), F32)],
        compiler_params=_params("arbitrary"),
    )(y, target)


def _bucket_table():
    qi = np.arange(BLOCK)[:, None]
    kj = np.arange(2 * BLOCK)[None, :]
    dist = np.maximum(qi + BLOCK - kj, 0)
    max_exact = N_BUCKETS // 2
    d = np.maximum(dist, 1).astype(np.float32)
    large = max_exact + (np.log(d / np.float32(max_exact)) / np.float32(math.log(MAX_DISTANCE / max_exact))
                         * np.float32(N_BUCKETS - max_exact)).astype(np.int32)
    large = np.minimum(large, N_BUCKETS - 1)
    return np.where(dist < max_exact, dist, large).astype(np.int32)


def _swa_valid(n):
    qi = lax.broadcasted_iota(jnp.int32, (BLOCK, 2 * BLOCK), 0)
    kj = lax.broadcasted_iota(jnp.int32, (BLOCK, 2 * BLOCK), 1)
    dist = qi + BLOCK - kj
    return (dist >= 0) & (dist < BLOCK) & ((kj >= BLOCK) | (n > 0))


def _swa_head(q, kb, vb, qg, kg, sink, bias, valid):
    qn = _rms(q, qg)
    kn = _rms(kb, kg)
    s = _mm(qn, kn, False, True) * (HEAD_DIM ** -0.5) + bias
    s = jnp.where(valid, s, -jnp.inf)
    m = lax.stop_gradient(jnp.maximum(jnp.max(s, axis=-1, keepdims=True), sink))
    p = jnp.exp(s - m)
    p = p / (jnp.sum(p, axis=-1, keepdims=True) + jnp.exp(sink - m))
    return _mm(p, vb)


def _swa_bias_table(rb_ref, bucket, bias_s):
    for h in range(SWA_HEADS):
        acc = jnp.zeros((BLOCK, 2 * BLOCK), F32)
        for b in range(N_BUCKETS):
            acc = jnp.where(bucket == b, rb_ref[b, h], acc)
        bias_s[h] = acc


def _swa_band(kvp_ref, kvc_ref, g):
    lo = g * HEAD_DIM
    kb = jnp.concatenate([kvp_ref[:, lo:lo + HEAD_DIM], kvc_ref[:, lo:lo + HEAD_DIM]], axis=0)
    lo += SWA_KV_W
    vb = jnp.concatenate([kvp_ref[:, lo:lo + HEAD_DIM], kvc_ref[:, lo:lo + HEAD_DIM]], axis=0)
    return kb, vb


def _swa_specs(order):
    kvc = COL_SKV // (2 * SWA_KV_W)
    return [
        pl.BlockSpec((BLOCK, SWA_Q_W), lambda t: (order(t), 0)),
        pl.BlockSpec((BLOCK, 2 * SWA_KV_W), lambda t: (jnp.maximum(order(t) - 1, 0), kvc)),
        pl.BlockSpec((BLOCK, 2 * SWA_KV_W), lambda t: (order(t), kvc)),
        pl.BlockSpec((1, HEAD_DIM), lambda t: (0, 0)),
        pl.BlockSpec((1, HEAD_DIM), lambda t: (0, 0)),
        pl.BlockSpec(memory_space=pltpu.SMEM),
        pl.BlockSpec(memory_space=pltpu.SMEM),
        pl.BlockSpec((BLOCK, 2 * BLOCK), lambda t: (0, 0)),
    ]


def _swa_fwd(p, qg, kg, sinks, rel_bias, *, name):
    s = p.shape[0]
    nb = s // BLOCK

    def body(q_ref, kvp_ref, kvc_ref, qg_ref, kg_ref, sink_ref, rb_ref, bucket_ref, y_ref, bias_s):
        n = pl.program_id(0)

        @pl.when(n == 0)
        def _():
            _swa_bias_table(rb_ref, bucket_ref[...], bias_s)

        valid = _swa_valid(n)
        for g in range(SWA_KV_HEADS):
            kb, vb = _swa_band(kvp_ref, kvc_ref, g)
            for hh in range(SWA_GROUP):
                h = g * SWA_GROUP + hh
                cols = slice(h * HEAD_DIM, (h + 1) * HEAD_DIM)
                sink = jnp.full((1, 1), sink_ref[h], F32)
                y_ref[:, cols] = _swa_head(q_ref[:, cols], kb, vb, qg_ref[...], kg_ref[...], sink, bias_s[h], valid)

    return pl.pallas_call(
        body, name=name, grid=(nb,), in_specs=_swa_specs(lambda t: t),
        out_specs=pl.BlockSpec((BLOCK, SWA_Q_W), lambda t: (t, 0)), out_shape=SDS((s, SWA_Q_W), F32),
        scratch_shapes=[pltpu.VMEM((SWA_HEADS, BLOCK, 2 * BLOCK), F32)],
        compiler_params=_params("arbitrary"),
    )(p, p, p, qg, kg, sinks, rel_bias, jnp.asarray(_bucket_table()))


def _swa_bwd(p, qg, kg, sinks, rel_bias, dy_all, *, name):
    s = p.shape[0]
    nb = s // BLOCK

    def body(q_ref, kvp_ref, kvc_ref, qg_ref, kg_ref, sink_ref, rb_ref, bucket_ref, dy_ref,
             dq_ref, dkv_ref, dqg_ref, dkg_ref, dsink_ref, drb_ref, bias_s, dbias_s, carry_s):
        t = pl.program_id(0)
        n = nb - 1 - t

        @pl.when(t == 0)
        def _():
            _swa_bias_table(rb_ref, bucket_ref[...], bias_s)
            dbias_s[...] = jnp.zeros_like(dbias_s)
            carry_s[...] = jnp.zeros_like(carry_s)
            dqg_ref[...] = jnp.zeros_like(dqg_ref)
            dkg_ref[...] = jnp.zeros_like(dkg_ref)
            dsink_ref[...] = jnp.zeros_like(dsink_ref)
            drb_ref[...] = jnp.zeros_like(drb_ref)

        valid = _swa_valid(n)
        lane = lax.broadcasted_iota(jnp.int32, (1, BLOCK), 1)
        dqg = jnp.zeros((1, HEAD_DIM), F32)
        dkg = jnp.zeros((1, HEAD_DIM), F32)
        dsink_vec = jnp.zeros((1, BLOCK), F32)
        for g in range(SWA_KV_HEADS):
            kb, vb = _swa_band(kvp_ref, kvc_ref, g)
            dkb = jnp.zeros((2 * BLOCK, HEAD_DIM), F32)
            dvb = jnp.zeros((2 * BLOCK, HEAD_DIM), F32)
            for hh in range(SWA_GROUP):
                h = g * SWA_GROUP + hh
                cols = slice(h * HEAD_DIM, (h + 1) * HEAD_DIM)
                sink = jnp.full((1, 1), sink_ref[h], F32)
                _, vjp = jax.vjp(functools.partial(_swa_head, valid=valid),
                                 q_ref[:, cols], kb, vb, qg_ref[...], kg_ref[...], sink, bias_s[h])
                dq, dkb_h, dvb_h, dqg_h, dkg_h, dsink_h, dbias_h = vjp(dy_ref[:, cols])
                dq_ref[:, cols] = dq
                dkb += dkb_h
                dvb += dvb_h
                dqg += dqg_h
                dkg += dkg_h
                dsink_vec += jnp.where(lane == h, dsink_h, 0.0)
                dbias_s[h] += dbias_h
            lo = g * HEAD_DIM
            dkv_ref[:, lo:lo + HEAD_DIM] = dkb[BLOCK:] + carry_s[g]
            carry_s[g] = dkb[:BLOCK]
            lo += SWA_KV_W
            dkv_ref[:, lo:lo + HEAD_DIM] = dvb[BLOCK:] + carry_s[SWA_KV_HEADS + g]
            carry_s[SWA_KV_HEADS + g] = dvb[:BLOCK]
        dqg_ref[...] += dqg
        dkg_ref[...] += dkg
        dsink_ref[...] += dsink_vec

        @pl.when(t == nb - 1)
        def _():
            bucket = bucket_ref[...]
            row = lax.broadcasted_iota(jnp.int32, (N_BUCKETS, BLOCK), 0)
            col = lax.broadcasted_iota(jnp.int32, (N_BUCKETS, BLOCK), 1)
            acc = jnp.zeros((N_BUCKETS, BLOCK), F32)
            for h in range(SWA_HEADS):
                dbias = dbias_s[h]
                for b in range(N_BUCKETS):
                    part = jnp.sum(jnp.where(bucket == b, dbias, 0.0), axis=1, keepdims=True)
                    val = jnp.sum(part, axis=0, keepdims=True)
                    acc = acc + jnp.where((row == b) & (col == h), val, 0.0)
            drb_ref[...] = acc

    order = lambda t: nb - 1 - t
    vec = pl.BlockSpec((1, HEAD_DIM), lambda t: (0, 0))
    return pl.pallas_call(
        body, name=name, grid=(nb,),
        in_specs=_swa_specs(order) + [pl.BlockSpec((BLOCK, SWA_Q_W), lambda t: (order(t), 0))],
        out_specs=[pl.BlockSpec((BLOCK, SWA_Q_W), lambda t: (order(t), 0)),
                   pl.BlockSpec((BLOCK, 2 * SWA_KV_W), lambda t: (order(t), 0)),
                   vec, vec, pl.BlockSpec((1, BLOCK), lambda t: (0, 0)),
                   pl.BlockSpec((N_BUCKETS, BLOCK), lambda t: (0, 0))],
        out_shape=[SDS((s, SWA_Q_W), F32), SDS((s, 2 * SWA_KV_W), F32), SDS((1, HEAD_DIM), F32),
                   SDS((1, HEAD_DIM), F32), SDS((1, BLOCK), F32), SDS((N_BUCKETS, BLOCK), F32)],
        scratch_shapes=[pltpu.VMEM((SWA_HEADS, BLOCK, 2 * BLOCK), F32), pltpu.VMEM((SWA_HEADS, BLOCK, 2 * BLOCK), F32),
                        pltpu.VMEM((2 * SWA_KV_HEADS, BLOCK, HEAD_DIM), F32)],
        compiler_params=_params("arbitrary"),
    )(p, p, p, qg, kg, sinks, rel_bias, jnp.asarray(_bucket_table()), dy_all)


def _mem_head(q, k, v, qg, kg):
    qn = _rms(q, qg)
    kn = _rms(k, kg)
    s = _mm(qn, kn, False, True) * (HEAD_DIM ** -0.5)
    m = lax.stop_gradient(jnp.max(s, axis=-1, keepdims=True))
    e = jnp.exp(s - m)
    return _mm(e / jnp.sum(e, axis=-1, keepdims=True), v)


def _mem_fwd(p, kv, qg, kg, *, tq, name):
    s = p.shape[0]
    m = kv.shape[0]

    def body(q_ref, kv_ref, qg_ref, kg_ref, y_ref):
        for h in range(MEM_HEADS):
            cols = slice(h * HEAD_DIM, (h + 1) * HEAD_DIM)
            vcols = slice(MEM_Q_W + h * HEAD_DIM, MEM_Q_W + (h + 1) * HEAD_DIM)
            y_ref[:, cols] = _mem_head(q_ref[:, cols], kv_ref[:, cols], kv_ref[:, vcols], qg_ref[...], kg_ref[...])

    vec = pl.BlockSpec((1, HEAD_DIM), lambda t: (0, 0))
    return pl.pallas_call(
        body, name=name, grid=(s // tq,),
        in_specs=[pl.BlockSpec((tq, MEM_Q_W), lambda t: (t, COL_MQ // MEM_Q_W)),
                  pl.BlockSpec((m, 2 * MEM_Q_W), lambda t: (0, 0)), vec, vec],
        out_specs=pl.BlockSpec((tq, MEM_Q_W), lambda t: (t, 0)), out_shape=SDS((s, MEM_Q_W), F32),
        compiler_params=_params("parallel"),
    )(p, kv, qg, kg)


def _mem_bwd(p, kv, qg, kg, dy_all, *, tq, name):
    s = p.shape[0]
    m = kv.shape[0]

    def body(q_ref, kv_ref, qg_ref, kg_ref, dy_ref, dq_ref, dkv_ref, dqg_ref, dkg_ref):
        @pl.when(pl.program_id(0) == 0)
        def _():
            dkv_ref[...] = jnp.zeros_like(dkv_ref)
            dqg_ref[...] = jnp.zeros_like(dqg_ref)
            dkg_ref[...] = jnp.zeros_like(dkg_ref)

        dqg = jnp.zeros((1, HEAD_DIM), F32)
        dkg = jnp.zeros((1, HEAD_DIM), F32)
        for h in range(MEM_HEADS):
            cols = slice(h * HEAD_DIM, (h + 1) * HEAD_DIM)
            vcols = slice(MEM_Q_W + h * HEAD_DIM, MEM_Q_W + (h + 1) * HEAD_DIM)
            _, vjp = jax.vjp(_mem_head, q_ref[:, cols], kv_ref[:, cols], kv_ref[:, vcols], qg_ref[...], kg_ref[...])
            dq, dk, dv, dqg_h, dkg_h = vjp(dy_ref[:, cols])
            dq_ref[:, cols] = dq
            dkv_ref[:, cols] += dk
            dkv_ref[:, vcols] += dv
            dqg += dqg_h
            dkg += dkg_h
        dqg_ref[...] += dqg
        dkg_ref[...] += dkg

    vec = pl.BlockSpec((1, HEAD_DIM), lambda t: (0, 0))
    full = pl.BlockSpec((m, 2 * MEM_Q_W), lambda t: (0, 0))
    dy_col = (SWA_Q_W + GLA_V_W) // MEM_Q_W
    return pl.pallas_call(
        body, name=name, grid=(s // tq,),
        in_specs=[pl.BlockSpec((tq, MEM_Q_W), lambda t: (t, COL_MQ // MEM_Q_W)), full, vec, vec,
                  pl.BlockSpec((tq, MEM_Q_W), lambda t: (t, dy_col))],
        out_specs=[pl.BlockSpec((tq, MEM_Q_W), lambda t: (t, 0)), full, vec, vec],
        out_shape=[SDS((s, MEM_Q_W), F32), SDS((m, 2 * MEM_Q_W), F32), SDS((1, HEAD_DIM), F32), SDS((1, HEAD_DIM), F32)],
        compiler_params=_params("arbitrary"),
    )(p, kv, qg, kg, dy_all)


GLA_ROWS = 256


def _gla_consts():
    c, h = GLA_CHUNK, GLA_HEADS
    i2 = lax.broadcasted_iota(jnp.int32, (c, c), 0)
    j2 = lax.broadcasted_iota(jnp.int32, (c, c), 1)
    slab_q = lax.broadcasted_iota(jnp.int32, (h, c, GLA_QK_W), 0)
    lane_q = lax.broadcasted_iota(jnp.int32, (h, c, GLA_QK_W), 2)
    row_a = lax.broadcasted_iota(jnp.int32, (h * c, c), 0)
    col_a = lax.broadcasted_iota(jnp.int32, (h * c, c), 1)
    slab_o = lax.broadcasted_iota(jnp.int32, (h, c, GLA_V_W), 0)
    lane_o = lax.broadcasted_iota(jnp.int32, (h, c, GLA_V_W), 2)
    row_s = lax.broadcasted_iota(jnp.int32, (GLA_V_W, GLA_QK_W), 0)
    col_s = lax.broadcasted_iota(jnp.int32, (GLA_V_W, GLA_QK_W), 1)
    return dict(
        ltri=(j2 <= i2).astype(F32),
        m_q=(slab_q == lane_q // GLA_DK).astype(F32),
        causal=col_a <= row_a % c,
        m_o=(slab_o == lane_o // GLA_DV).astype(F32),
        m_s=(row_s // GLA_DV == col_s // GLA_DK).astype(F32),
    )


def _gla_chunk(q, k, v, z, bg, st, c):
    h, n = GLA_HEADS, GLA_CHUNK
    la = _log_sigmoid(z + bg) * (1.0 / GLA_TAU)
    b = _mmf(c["ltri"], la)
    bl = jnp.sum(la, axis=0, keepdims=True)
    qs = q * (GLA_DK ** -0.5)
    kt = k * jnp.exp(bl - b)
    qt = qs * jnp.exp(b - bl)
    qe = qs * jnp.exp(b)
    q_stack = (jnp.broadcast_to(qt[None], (h, n, GLA_QK_W)) * c["m_q"]).reshape(h * n, GLA_QK_W)
    a = jnp.where(c["causal"], _mmf(q_stack, kt, False, True), 0.0)
    o_stack = _mm(a, v)
    o_intra = jnp.sum(o_stack.reshape(h, n, GLA_V_W) * c["m_o"], axis=0)
    o_inter = _mm(qe, st, False, True)
    st_next = st * jnp.exp(bl) + _mm(v, kt, True, False) * c["m_s"]
    return o_intra + o_inter, st_next


def _gla_post(o, gg, gain, g64):
    ms = _mmf(o * o, g64) * (1.0 / GLA_DV)
    return o * lax.rsqrt(ms + EPS) * gain * jax.nn.silu(gg)


def _gla_g64():
    r = lax.broadcasted_iota(jnp.int32, (GLA_V_W, GLA_V_W), 0)
    c = lax.broadcasted_iota(jnp.int32, (GLA_V_W, GLA_V_W), 1)
    return (r // GLA_DV == c // GLA_DV).astype(F32)


def _gla_in_specs(order):
    r = GLA_ROWS
    return [
        pl.BlockSpec((r, GLA_QK_W), lambda t: (order(t), COL_GQ // GLA_QK_W)),
        pl.BlockSpec((r, GLA_QK_W), lambda t: (order(t), COL_GK // GLA_QK_W)),
        pl.BlockSpec((r, GLA_V_W), lambda t: (order(t), COL_GV // GLA_V_W)),
        pl.BlockSpec((r, GLA_V_W), lambda t: (order(t), COL_GG // GLA_V_W)),
        pl.BlockSpec((r, GLA_QK_W), lambda t: (order(t), 0)),
        pl.BlockSpec((1, GLA_QK_W), lambda t: (0, 0)),
        pl.BlockSpec((1, GLA_V_W), lambda t: (0, 0)),
    ]


def _gla_fwd(p, z, bg, gain, *, name):
    s = p.shape[0]
    r = GLA_ROWS
    cps = r // GLA_CHUNK

    def body(q_ref, k_ref, v_ref, gg_ref, z_ref, bg_ref, gain_ref, y_ref, oraw_ref, stsave_ref, st_s):
        @pl.when(pl.program_id(0) == 0)
        def _():
            st_s[...] = jnp.zeros_like(st_s)

        c = _gla_consts()
        st = st_s[...]
        for ci in range(cps):
            rows = slice(ci * GLA_CHUNK, (ci + 1) * GLA_CHUNK)
            stsave_ref[ci] = st
            o, st = _gla_chunk(q_ref[rows, :], k_ref[rows, :], v_ref[rows, :], z_ref[rows, :], bg_ref[...], st, c)
            oraw_ref[rows, :] = o
        st_s[...] = st
        y_ref[...] = _gla_post(oraw_ref[...], gg_ref[...], gain_ref[...], _gla_g64())

    rowv = pl.BlockSpec((r, GLA_V_W), lambda t: (t, 0))
    return pl.pallas_call(
        body, name=name, grid=(s // r,), in_specs=_gla_in_specs(lambda t: t),
        out_specs=[rowv, rowv, pl.BlockSpec((cps, GLA_V_W, GLA_QK_W), lambda t: (t, 0, 0))],
        out_shape=[SDS((s, GLA_V_W), F32), SDS((s, GLA_V_W), F32), SDS((s // GLA_CHUNK, GLA_V_W, GLA_QK_W), F32)],
        scratch_shapes=[pltpu.VMEM((GLA_V_W, GLA_QK_W), F32)],
        compiler_params=_params("arbitrary"),
    )(p, p, p, p, z, bg, gain)


def _gla_bwd(p, z, bg, gain, oraw, stsave, dy_all, *, name):
    s = p.shape[0]
    r = GLA_ROWS
    cps = r // GLA_CHUNK
    nsteps = s // r
    w_qkvg = 2 * GLA_QK_W + 2 * GLA_V_W

    def body(q_ref, k_ref, v_ref, gg_ref, z_ref, bg_ref, gain_ref, oraw_ref, stsave_ref, dy_ref,
             dqkvg_ref, dz_ref, dbg_ref, dgain_ref, dst_s):
        @pl.when(pl.program_id(0) == 0)
        def _():
            dst_s[...] = jnp.zeros_like(dst_s)
            dbg_ref[...] = jnp.zeros_like(dbg_ref)
            dgain_ref[...] = jnp.zeros_like(dgain_ref)

        c = _gla_consts()
        _, vjp = jax.vjp(functools.partial(_gla_post, g64=_gla_g64()), oraw_ref[...], gg_ref[...], gain_ref[...])
        do, dgg, dgain = vjp(dy_ref[...])
        dqkvg_ref[:, 2 * GLA_QK_W + GLA_V_W:] = dgg
        dgain_ref[...] += dgain
        dst = dst_s[...]
        dbg = jnp.zeros((1, GLA_QK_W), F32)
        for ci in reversed(range(cps)):
            rows = slice(ci * GLA_CHUNK, (ci + 1) * GLA_CHUNK)
            _, vjp = jax.vjp(functools.partial(_gla_chunk, c=c), q_ref[rows, :], k_ref[rows, :], v_ref[rows, :],
                             z_ref[rows, :], bg_ref[...], stsave_ref[ci])
            dq, dk, dv, dz, dbg_c, dst = vjp((do[rows, :], dst))
            dqkvg_ref[rows, 0:GLA_QK_W] = dq
            dqkvg_ref[rows, GLA_QK_W:2 * GLA_QK_W] = dk
            dqkvg_ref[rows, 2 * GLA_QK_W:2 * GLA_QK_W + GLA_V_W] = dv
            dz_ref[rows, :] = dz
            dbg += dbg_c
        dst_s[...] = dst
        dbg_ref[...] += dbg

    order = lambda t: nsteps - 1 - t
    rowv = pl.BlockSpec((r, GLA_V_W), lambda t: (order(t), 0))
    return pl.pallas_call(
        body, name=name, grid=(nsteps,),
        in_specs=_gla_in_specs(order) + [
            rowv, pl.BlockSpec((cps, GLA_V_W, GLA_QK_W), lambda t: (order(t), 0, 0)),
            pl.BlockSpec((r, GLA_V_W), lambda t: (order(t), SWA_Q_W // GLA_V_W))],
        out_specs=[pl.BlockSpec((r, w_qkvg), lambda t: (order(t), 0)), pl.BlockSpec((r, GLA_QK_W), lambda t: (order(t), 0)),
                   pl.BlockSpec((1, GLA_QK_W), lambda t: (0, 0)), pl.BlockSpec((1, GLA_V_W), lambda t: (0, 0))],
        out_shape=[SDS((s, w_qkvg), F32), SDS((s, GLA_QK_W), F32), SDS((1, GLA_QK_W), F32), SDS((1, GLA_V_W), F32)],
        scratch_shapes=[pltpu.VMEM((GLA_V_W, GLA_QK_W), F32)],
        compiler_params=_params("arbitrary"),
    )(p, p, p, p, z, bg, gain, oraw, stsave, dy_all)


def _local_step(x, mem, target, small, big, on_grads):
    g1, gmix, gmem, g2, sqg, skg, sinks, rel_bias, wgu, bg, gla_gain, mqg, mkg = small
    wgu1, wd1, win_p, wkv, wout, wgu2, wd2 = big
    wgu_pad = jnp.zeros((GLA_QK_W, GLA_QK_W), BF16).at[:GLA_RANK].set(wgu.astype(BF16))
    gain256 = jnp.tile(gla_gain, (1, GLA_HEADS))

    x1, saved1 = _ffn_fwd(x, g1, wgu1, wd1, "ffn1")
    h = _rms_fwd(x1, gmix, tm=256, name="mix_rms")
    p = _matmul(h, win_p, tm=512, tn=384, name="mix_in")
    hm = _rms_fwd(mem, gmem, tm=256, name="mem_rms")
    kv = _matmul(hm, wkv, tm=256, tn=512, name="mem_kv")
    p_glr = p[:, COL_GLR:]
    z = _matmul(p_glr, wgu_pad, tm=512, tn=GLA_QK_W, name="gla_gate")
    y_swa = _swa_fwd(p, sqg, skg, sinks, rel_bias, name="swa_fwd")
    y_gla, oraw, stsave = _gla_fwd(p, z, bg, gain256, name="gla_fwd")
    y_mem = _mem_fwd(p, kv, mqg, mkg, tq=256, name="mem_fwd")
    x2 = _matmul([y_swa, y_gla, y_mem], wout, b_blocks=[0, 2, 3], tm=512, tn=512, res=x1, name="mix_out")
    x3, saved2 = _ffn_fwd(x2, g2, wgu2, wd2, "ffn2")

    dy, dyb, loss = _loss_bwd(x3, target, tm=256, name="loss")
    dx2, dx2b, dg2, dwgu2, dwd2 = _ffn_bwd(dy, dyb, x2, g2, wgu2, wd2, saved2, "ffn2")
    dx2b = on_grads("ffn2", (dwgu2, dwd2), dx2b)
    dy_all = _matmul(dx2b, wout, tb=True, tm=512, tn=512, name="mix_dy")
    dwout = _matmul(jnp.concatenate([y_swa, y_gla, y_mem], axis=1), dx2b, ta=True, tm=512, tn=512, out_dtype=BF16,
                    name="mix_dw_out")
    dq_swa, dkv_swa, dsqg, dskg, dsink, drb = _swa_bwd(p, sqg, skg, sinks, rel_bias, dy_all, name="swa_bwd")
    dqkvg, dz, dbg, dgain256 = _gla_bwd(p, z, bg, gain256, oraw, stsave, dy_all, name="gla_bwd")
    dmq, dkv_mem, dmqg, dmkg = _mem_bwd(p, kv, mqg, mkg, dy_all, tq=256, name="mem_bwd")
    dglr = _matmul(dz, wgu_pad, tb=True, tm=512, tn=GLA_QK_W, name="gla_gate_dx")
    dwgu_pad = _matmul(p_glr, dz, ta=True, tm=GLA_QK_W, tn=GLA_QK_W, name="gla_gate_dw")
    dp = jnp.concatenate([dq_swa, dkv_swa, dqkvg, dmq, dglr], axis=1)
    dh = _matmul(dp, win_p, tb=True, tm=512, tn=512, name="mix_dh")
    dwin_p = _matmul(h, dp, ta=True, tm=512, tn=384, name="mix_dw_in")
    dx1, dx1b, dgmix = _rms_bwd(x1, gmix, dh, dx2, tm=256, name="mix_drms")
    dwkv = _matmul(hm, dkv_mem, ta=True, tm=512, tn=512, out_dtype=BF16, name="mem_dw_kv")
    dx1b = on_grads("mix", (dwin_p, dwkv, dwout), dx1b)
    dhm = _matmul(dkv_mem, wkv, tb=True, tm=256, tn=512, name="mem_dh")
    _, _, dgmem = _rms_bwd(mem, gmem, dhm, None, tm=256, name="mem_drms")
    dx, _, dg1, dwgu1, dwd1 = _ffn_bwd(dx1, dx1b, x, g1, wgu1, wd1, saved1, "ffn1")

    dgla_gain = dgain256.reshape(GLA_HEADS, GLA_DV).sum(axis=0, keepdims=True)
    dsmall = (dg1, dgmix, dgmem, dg2, dsqg, dskg, dsink[0, :SWA_HEADS], drb[:, :SWA_HEADS], dwgu_pad[:GLA_RANK], dbg,
              dgla_gain, dmqg, dmkg)
    on_grads("ffn1", (dwgu1, dwd1), None, small=list(dsmall) + [loss])
    return dx


def _mesh_place():
    x, y, c = lax.axis_index("x"), lax.axis_index("y"), lax.axis_index("c")
    other_chips = [(1 - x, y), (x, 1 - y), (1 - x, 1 - y)]
    return x, y, c, other_chips


def _handshake(peers):
    barrier = pltpu.get_barrier_semaphore()
    for peer in peers:
        pl.semaphore_signal(barrier, inc=1, device_id=peer, device_id_type=MESH)
    pl.semaphore_wait(barrier, len(peers))


def _sequencer_call(body, operands, out_shapes, sems, *, name, collective_id):
    return pl.kernel(
        body, name=name, out_type=out_shapes, mesh=plsc.ScalarSubcoreMesh(axis_name="sequencer", num_cores=1),
        scratch_types=sems, compiler_params=pltpu.CompilerParams(collective_id=collective_id),
    )(*operands)


def _window(ref, kind, slot, shape):
    if kind == "col":
        return ref.at[:, :, pl.ds(pl.multiple_of(slot * shape[-1], 128), shape[-1])]
    if kind == "row":
        return ref.at[pl.ds(pl.multiple_of(slot * shape[0], 8), shape[0])]
    return ref.at[slot]


def _gathered(shape, kind):
    if kind == "col":
        return tuple(shape[:-1]) + (N_DEV * shape[-1],)
    if kind == "row":
        return (N_DEV * shape[0],) + tuple(shape[1:])
    return (N_DEV,) + tuple(shape)


def _all_gather(shards, kinds, *, name, collective_id):
    nt = len(shards)

    def body(*refs):
        x_refs, o_refs = refs[:nt], refs[nt:2 * nt]
        send_sems, recv_sems, local_sems = refs[2 * nt:]
        x, y, c, chips = _mesh_place()
        me, sibling = (x, y, c), (x, y, 1 - c)
        _handshake([sibling] + [(*chip, c) for chip in chips])

        def copy(k, t, block, to, from_shard=False):
            bx, by, bc = block
            rows = _window(o_refs[t], kinds[t], 4 * bx + 2 * by + bc, shards[t].shape)
            return pltpu.make_async_remote_copy(
                src_ref=x_refs[t] if from_shard else rows, dst_ref=rows,
                send_sem=send_sems.at[k, t], recv_sem=recv_sems.at[k, t], device_id=to, device_id_type=MESH)

        mine = [pltpu.make_async_copy(x_refs[t], _window(o_refs[t], kinds[t], 4 * x + 2 * y + c, shards[t].shape),
                                      local_sems.at[t]) for t in range(nt)]
        for cp in mine:
            cp.start()
        first = [copy(0, t, me, sibling, True) for t in range(nt)]
        first += [copy(1 + j, t, me, (*chip, c), True) for j, chip in enumerate(chips) for t in range(nt)]
        for cp in first:
            cp.start()
        passed = []
        for j, chip in enumerate(chips):
            for t in range(nt):
                copy(1 + j, t, (*chip, c), me).wait_recv()
                fwd = copy(4 + j, t, (*chip, c), sibling)
                fwd.start()
                passed.append(fwd)
        for t in range(nt):
            copy(0, t, sibling, me).wait_recv()
        for j, chip in enumerate(chips):
            for t in range(nt):
                copy(4 + j, t, (*chip, 1 - c), me).wait_recv()
        for cp in first + passed:
            cp.wait_send()
        for cp in mine:
            cp.wait()

    return _sequencer_call(
        body, shards, [SDS(_gathered(s.shape, k), s.dtype) for s, k in zip(shards, kinds)],
        [pltpu.SemaphoreType.DMA((7, nt)), pltpu.SemaphoreType.DMA((7, nt)), pltpu.SemaphoreType.DMA((nt,))],
        name=name, collective_id=collective_id)


def _part_shape(shape, kind):
    if kind == "col":
        return tuple(shape[:-1]) + (shape[-1] // N_DEV,)
    if kind == "row":
        return (shape[0] // N_DEV,) + tuple(shape[1:])
    return tuple(shape[2:])


def _pair_exchange(grads, kinds, *, name, collective_id):
    nt = len(grads)
    part = [_part_shape(g.shape, k) for g, k in zip(grads, kinds)]

    def body(*refs):
        g_refs, o_refs = refs[:nt], refs[nt:2 * nt]
        send_sems, recv_sems = refs[2 * nt:]
        x, y, c, _ = _mesh_place()
        _handshake([(x, y, 1 - c)])
        copies = []
        for t in range(nt):
            for xy in range(4):
                src = g_refs[t].at[1 - c, xy] if kinds[t] == "stack" else _window(g_refs[t], kinds[t], 2 * xy + 1 - c, part[t])
                copies.append(pltpu.make_async_remote_copy(
                    src_ref=src, dst_ref=o_refs[t].at[xy], send_sem=send_sems.at[xy, t], recv_sem=recv_sems.at[xy, t],
                    device_id=(x, y, 1 - c), device_id_type=MESH))
        for cp in copies:
            cp.start()
        for cp in copies:
            cp.wait()

    return _sequencer_call(
        body, grads, [SDS((4,) + p, g.dtype) for p, g in zip(part, grads)],
        [pltpu.SemaphoreType.DMA((4, nt)), pltpu.SemaphoreType.DMA((4, nt))], name=name, collective_id=collective_id)


def _chip_exchange(parts, small, *, name, collective_id):
    nt = len(parts)
    if small is None:
        def body_plain(*refs):
            s_refs, o_refs = refs[:nt], refs[nt:2 * nt]
            send_sems, recv_sems = refs[2 * nt:]
            x, y, c, chips = _mesh_place()
            _handshake([(*chip, c) for chip in chips])
            copies = [pltpu.make_async_remote_copy(
                src_ref=s_refs[t].at[2 * chip[0] + chip[1]], dst_ref=o_refs[t].at[j],
                send_sem=send_sems.at[j, t], recv_sem=recv_sems.at[j, t], device_id=(*chip, c), device_id_type=MESH)
                for j, chip in enumerate(chips) for t in range(nt)]
            for cp in copies:
                cp.start()
            for cp in copies:
                cp.wait()

        return _sequencer_call(
            body_plain, parts, [SDS((3,) + s.shape[1:], s.dtype) for s in parts],
            [pltpu.SemaphoreType.DMA((3, nt)), pltpu.SemaphoreType.DMA((3, nt))], name=name, collective_id=collective_id)

    def body(*refs):
        s_refs, small_ref = refs[:nt], refs[nt]
        o_refs, small_all = refs[nt + 1:2 * nt + 1], refs[2 * nt + 1]
        send_sems, recv_sems, small_send, small_recv, local_sem = refs[2 * nt + 2:]
        x, y, c, chips = _mesh_place()
        _handshake([(px, py, pc) for px in (x, 1 - x) for py in (y, 1 - y) for pc in (c, 1 - c)][1:])

        def copy(j, t, chip):
            return pltpu.make_async_remote_copy(
                src_ref=s_refs[t].at[2 * chip[0] + chip[1]], dst_ref=o_refs[t].at[j],
                send_sem=send_sems.at[j, t], recv_sem=recv_sems.at[j, t], device_id=(*chip, c), device_id_type=MESH)

        flips = [(fx, fy, fc) for fx in (0, 1) for fy in (0, 1) for fc in (0, 1)][1:]

        def small_copy(k):
            fx, fy, fc = flips[k]
            to = (x ^ fx if fx else x, y ^ fy if fy else y, c ^ fc if fc else c)
            rows = small_all.at[4 * x + 2 * y + c]
            return pltpu.make_async_remote_copy(
                src_ref=small_ref, dst_ref=rows, send_sem=small_send.at[k], recv_sem=small_recv.at[k],
                device_id=to, device_id_type=MESH)

        own = pltpu.make_async_copy(small_ref, small_all.at[4 * x + 2 * y + c], local_sem)
        own.start()
        copies = [copy(j, t, chip) for j, chip in enumerate(chips) for t in range(nt)]
        smalls = [small_copy(k) for k in range(7)]
        for cp in smalls + copies:
            cp.start()
        for cp in smalls + copies:
            cp.wait()
        own.wait()

    return _sequencer_call(
        body, list(parts) + [small],
        [SDS((3,) + s.shape[1:], s.dtype) for s in parts] + [SDS((N_DEV,) + small.shape, small.dtype)],
        [pltpu.SemaphoreType.DMA((3, nt)), pltpu.SemaphoreType.DMA((3, nt)),
         pltpu.SemaphoreType.DMA((7,)), pltpu.SemaphoreType.DMA((7,)), pltpu.SemaphoreType.DMA],
        name=name, collective_id=collective_id)


def _pair_sum(grad, theirs, kind, c, *, name):
    if kind == "col":
        n, r, l = theirs.shape[1:]
        mine_spec = pl.BlockSpec((None, r, l), lambda xy, m, c_ref: (m, 0, 2 * xy + c_ref[0]))
    elif kind == "row":
        (r, l), n = theirs.shape[1:], 1
        theirs = theirs[:, None]
        mine_spec = pl.BlockSpec((r, l), lambda xy, m, c_ref: (2 * xy + c_ref[0], 0))
    else:
        r, l = theirs.shape[-2:]
        n = theirs.size // (4 * r * l)
        theirs = theirs.reshape(4, n, r, l)
        grad = grad.reshape(2, 4, n, r, l)
        mine_spec = pl.BlockSpec((None, None, None, r, l), lambda xy, m, c_ref: (c_ref[0], xy, m, 0, 0))

    def body(c_ref, a_ref, b_ref, o_ref):
        o_ref[...] = (a_ref[...].astype(F32) + b_ref[...].astype(F32)).astype(BF16)

    part = pl.BlockSpec((None, None, r, l), lambda xy, m, c_ref: (xy, m, 0, 0))
    return pl.pallas_call(
        body, name=name,
        grid_spec=pltpu.PrefetchScalarGridSpec(num_scalar_prefetch=1, grid=(4, n), in_specs=[mine_spec, part], out_specs=part),
        out_shape=SDS((4, n, r, l), BF16), compiler_params=_params("parallel", "parallel"),
    )(c, grad, theirs)


def _adamw(w, g, m, v):
    m = ADAM_B1 * m + (1.0 - ADAM_B1) * g
    v = ADAM_B2 * v + (1.0 - ADAM_B2) * jnp.square(g)
    m_hat = m / (1.0 - ADAM_B1 ** ADAM_STEP)
    v_hat = v / (1.0 - ADAM_B2 ** ADAM_STEP)
    delta = -ADAM_LR * (m_hat / (jnp.sqrt(v_hat) + ADAM_EPS) + ADAM_WD * w)
    return delta, m, v


def _adam_big(own, others, mat, xy, w, m, v, *, tr, name):
    _, r, l = w.shape
    lp = own.shape[-1]

    def body(xy_ref, own_ref, oth_ref, w_ref, m_ref, v_ref, g_out, d_out, m_out, v_out):
        g = own_ref[0, 0].astype(F32)
        for j in range(3):
            g = g + oth_ref[j, 0].astype(F32)
        g = g[:, :l]
        delta, m_new, v_new = _adamw(w_ref[0], g, m_ref[0], v_ref[0])
        g_out[0] = g
        d_out[0] = delta
        m_out[0] = m_new
        v_out[0] = v_new

    blk = pl.BlockSpec((1, tr, l), lambda i, xy_ref: (0, i, 0))
    return pl.pallas_call(
        body, name=name,
        grid_spec=pltpu.PrefetchScalarGridSpec(
            num_scalar_prefetch=1, grid=(r // tr,),
            in_specs=[pl.BlockSpec((1, 1, tr, lp), lambda i, xy_ref: (xy_ref[0], mat, i, 0)),
                      pl.BlockSpec((3, 1, tr, lp), lambda i, xy_ref: (0, mat, i, 0)), blk, blk, blk],
            out_specs=[blk, blk, blk, blk]),
        out_shape=[SDS(w.shape, F32)] * 4, compiler_params=_params("parallel"),
    )(xy, own, others, w, m, v)


def _adam_small(g_all, w, m, v, *, name):
    def body(g_ref, w_ref, m_ref, v_ref, g_out, d_out, m_out, v_out):
        g = g_ref[0]
        for k in range(1, N_DEV):
            g = g + g_ref[k]
        delta, m_new, v_new = _adamw(w_ref[...], g, m_ref[...], v_ref[...])
        g_out[...] = g
        d_out[...] = delta
        m_out[...] = m_new
        v_out[...] = v_new

    return pl.pallas_call(body, name=name, out_shape=[SDS(w.shape, F32)] * 4)(g_all, w, m, v)


SMALL_ROWS = 56


def _pack_small(parts):
    flat = jnp.concatenate([a.reshape(-1) for a in parts])
    return jnp.pad(flat, (0, SMALL_ROWS * 128 - flat.shape[0])).reshape(SMALL_ROWS, 128)


def _unpack_small(packed, shapes):
    flat = packed.reshape(-1)
    out, at = [], 0
    for s in shapes:
        n = math.prod(s)
        out.append(flat[at:at + n].reshape(s))
        at += n
    return out


def kernel(x, mem, ffn1_norm, ffn1_w_gate, ffn1_w_up, ffn1_w_down, mix_norm, mem_norm, w_in, w_mem_kv, swa_q_norm, swa_k_norm, swa_sinks, rel_bias, gla_w_gate_up, gla_b_gate, gla_out_norm, mem_q_norm, mem_k_norm, w_out, ffn2_norm, ffn2_w_gate, ffn2_w_up, ffn2_w_down, loss_target, m_ffn1_norm, m_ffn1_w_gate, m_ffn1_w_up, m_ffn1_w_down, m_mix_norm, m_mem_norm, m_w_in, m_w_mem_kv, m_swa_q_norm, m_swa_k_norm, m_swa_sinks, m_rel_bias, m_gla_w_gate_up, m_gla_b_gate, m_gla_out_norm, m_mem_q_norm, m_mem_k_norm, m_w_out, m_ffn2_norm, m_ffn2_w_gate, m_ffn2_w_up, m_ffn2_w_down, v_ffn1_norm, v_ffn1_w_gate, v_ffn1_w_up, v_ffn1_w_down, v_mix_norm, v_mem_norm, v_w_in, v_w_mem_kv, v_swa_q_norm, v_swa_k_norm, v_swa_sinks, v_rel_bias, v_gla_w_gate_up, v_gla_b_gate, v_gla_out_norm, v_mem_q_norm, v_mem_k_norm, v_w_out, v_ffn2_norm, v_ffn2_w_gate, v_ffn2_w_up, v_ffn2_w_down):
    xi, yi, ci = lax.axis_index("x"), lax.axis_index("y"), lax.axis_index("c")
    c_arr = jnp.reshape(ci, (1,)).astype(jnp.int32)
    xy_arr = jnp.reshape(2 * xi + yi, (1,)).astype(jnp.int32)
    d = x.shape[-1]

    pad_h = FFN_SHARD_PAD - ffn1_w_gate.shape[-1]

    def gather_ffn(wg_s, wu_s, wd_s, name, collective_id, after):
        wgu_s = jnp.pad(jnp.concatenate([wg_s, wu_s], axis=0), ((0, 0), (0, 0), (0, pad_h))).astype(BF16)
        wd_s = jnp.pad(wd_s[0], ((0, pad_h), (0, 0))).astype(BF16)
        if after is not None:
            wgu_s, wd_s, _ = lax.optimization_barrier((wgu_s, wd_s, after))
        return _all_gather([wgu_s, wd_s], ["col", "row"], name=name, collective_id=collective_id)

    wgu1, wd1 = gather_ffn(ffn1_w_gate, ffn1_w_up, ffn1_w_down, "gather_ffn1", 0, None)
    mix_s = lax.optimization_barrier((w_in[0].astype(BF16), w_mem_kv[0].astype(BF16), w_out[0].astype(BF16), wd1))[:3]
    win_all, wkv, wout = _all_gather(list(mix_s), ["stack", "row", "row"], name="gather_mix", collective_id=1)
    wgu2, wd2 = gather_ffn(ffn2_w_gate, ffn2_w_up, ffn2_w_down, "gather_ffn2", 2, wout)
    win = win_all.transpose(1, 0, 2).reshape(d, -1)
    glr_lo, glr_hi = COL_MQ, COL_MQ + GLA_RANK
    win_p = jnp.concatenate([win[:, :glr_lo], win[:, glr_hi:], win[:, glr_lo:glr_hi],
                             jnp.zeros((d, IN_W_PAD - IN_W), BF16)], axis=1)

    small_w = [ffn1_norm, mix_norm, mem_norm, ffn2_norm, swa_q_norm, swa_k_norm, swa_sinks[0], rel_bias,
               gla_w_gate_up[0], gla_b_gate, gla_out_norm, mem_q_norm, mem_k_norm]
    collective_ids = {"ffn2": (3, 4), "mix": (5, 6), "ffn1": (7, 8)}
    reduced, small_box = {}, []

    def on_grads(group, grads, carry, small=None):
        if group == "mix":
            dwin_p, dwkv, dwout = grads
            dwin = jnp.concatenate([dwin_p[:, :glr_lo], dwin_p[:, COL_GLR:COL_GLR + GLA_RANK], dwin_p[:, COL_MQ:COL_GLR]],
                                   axis=1)
            w = dwin.shape[1] // N_DEV
            grads = [dwin.reshape(d, 4, 2, w).transpose(2, 1, 0, 3).astype(BF16), dwkv, dwout]
            kinds = ["stack", "row", "row"]
        else:
            kinds = ["col", "row"]
        id_pair, id_chip = collective_ids[group]
        from_sibling = _pair_exchange(grads, kinds, name=f"pair_exchange_{group}", collective_id=id_pair)
        chip_sums = [_pair_sum(g, theirs, k, c_arr, name=f"pair_sum_{group}_{t}")
                     for t, (g, theirs, k) in enumerate(zip(grads, from_sibling, kinds))]
        if carry is not None:
            *chip_sums, carry = lax.optimization_barrier((*chip_sums, carry))
        if small is None:
            from_chips = _chip_exchange(chip_sums, None, name=f"chip_exchange_{group}", collective_id=id_chip)
        else:
            *from_chips, small_all = _chip_exchange(chip_sums, _pack_small(small), name=f"chip_exchange_{group}",
                                                    collective_id=id_chip)
            small_box.append(small_all)
        reduced[group] = (chip_sums, from_chips)
        return carry

    grad_x = _local_step(x[0], mem[0], loss_target[0], small_w, (wgu1, wd1, win_p, wkv, wout, wgu2, wd2), on_grads)
    small_all = small_box[0]

    big_w = {"ffn1_w_gate": ("ffn1", 0, 0, ffn1_w_gate, m_ffn1_w_gate, v_ffn1_w_gate),
             "ffn1_w_up": ("ffn1", 0, 1, ffn1_w_up, m_ffn1_w_up, v_ffn1_w_up),
             "ffn1_w_down": ("ffn1", 1, 0, ffn1_w_down, m_ffn1_w_down, v_ffn1_w_down),
             "w_in": ("mix", 0, 0, w_in, m_w_in, v_w_in),
             "w_mem_kv": ("mix", 1, 0, w_mem_kv, m_w_mem_kv, v_w_mem_kv),
             "w_out": ("mix", 2, 0, w_out, m_w_out, v_w_out),
             "ffn2_w_gate": ("ffn2", 0, 0, ffn2_w_gate, m_ffn2_w_gate, v_ffn2_w_gate),
             "ffn2_w_up": ("ffn2", 0, 1, ffn2_w_up, m_ffn2_w_up, v_ffn2_w_up),
             "ffn2_w_down": ("ffn2", 1, 0, ffn2_w_down, m_ffn2_w_down, v_ffn2_w_down)}
    res = {}
    for nm, (group, t, mat, w, m, v) in big_w.items():
        r = w.shape[1]
        tr = 256 if r % 256 == 0 else r
        chip_sums, from_chips = reduced[group]
        res[nm] = _adam_big(chip_sums[t], from_chips[t], mat, xy_arr, w, m, v, tr=tr, name=f"adam_{nm}")
    small_names = ["ffn1_norm", "mix_norm", "mem_norm", "ffn2_norm", "swa_q_norm", "swa_k_norm", "swa_sinks", "rel_bias",
                   "gla_w_gate_up", "gla_b_gate", "gla_out_norm", "mem_q_norm", "mem_k_norm"]
    small_m = [m_ffn1_norm, m_mix_norm, m_mem_norm, m_ffn2_norm, m_swa_q_norm, m_swa_k_norm, m_swa_sinks, m_rel_bias,
               m_gla_w_gate_up, m_gla_b_gate, m_gla_out_norm, m_mem_q_norm, m_mem_k_norm]
    small_v = [v_ffn1_norm, v_mix_norm, v_mem_norm, v_ffn2_norm, v_swa_q_norm, v_swa_k_norm, v_swa_sinks, v_rel_bias,
               v_gla_w_gate_up, v_gla_b_gate, v_gla_out_norm, v_mem_q_norm, v_mem_k_norm]
    small_full = [ffn1_norm, mix_norm, mem_norm, ffn2_norm, swa_q_norm, swa_k_norm, swa_sinks, rel_bias,
                  gla_w_gate_up, gla_b_gate, gla_out_norm, mem_q_norm, mem_k_norm]
    packed = _adam_small(small_all, _pack_small(small_full), _pack_small(small_m), _pack_small(small_v), name="adam_small")
    full_shapes = [a.shape for a in small_full]
    unpacked = [_unpack_small(pk, full_shapes + [()]) for pk in packed]
    for k, nm in enumerate(small_names):
        res[nm] = [unpacked[q][k] for q in range(4)]
    loss = unpacked[0][len(small_names)]

    order = ["ffn1_norm", "ffn1_w_gate", "ffn1_w_up", "ffn1_w_down", "mix_norm", "mem_norm", "w_in", "w_mem_kv",
             "swa_q_norm", "swa_k_norm", "swa_sinks", "rel_bias", "gla_w_gate_up", "gla_b_gate", "gla_out_norm",
             "mem_q_norm", "mem_k_norm", "w_out", "ffn2_norm", "ffn2_w_gate", "ffn2_w_up", "ffn2_w_down"]
    outs = [loss, grad_x[None]]
    for q in range(4):
        outs += [res[nm][q] for nm in order]
    return tuple(outs)
```

```python
import functools
import math

import numpy as np
import jax
import jax.numpy as jnp
from jax import lax
from jax.experimental import pallas as pl
from jax.experimental.pallas import tpu as pltpu
from jax.experimental.pallas import tpu_sc as plsc

F32 = jnp.float32
BF16 = jnp.bfloat16
SDS = jax.ShapeDtypeStruct

EPS = 1e-6
HEAD_DIM = 64
SWA_HEADS = 8
SWA_KV_HEADS = 2
SWA_GROUP = SWA_HEADS // SWA_KV_HEADS
BLOCK = 128
N_BUCKETS = 32
MAX_DISTANCE = 128
GLA_HEADS = 4
GLA_DK = 32
GLA_DV = 64
GLA_RANK = 16
GLA_TAU = 16.0
GLA_CHUNK = 32
MEM_HEADS = 4
SWA_Q_W = SWA_HEADS * HEAD_DIM
SWA_KV_W = SWA_KV_HEADS * HEAD_DIM
GLA_QK_W = GLA_HEADS * GLA_DK
GLA_V_W = GLA_HEADS * GLA_DV
MEM_Q_W = MEM_HEADS * HEAD_DIM
IN_W = 1808
IN_W_PAD = 1920
COL_SQ, COL_SKV, COL_GQ, COL_GK, COL_GV, COL_GG, COL_MQ, COL_GLR = 0, 512, 768, 896, 1024, 1280, 1536, 1792

ADAM_LR = 0.001
ADAM_B1 = 0.9
ADAM_B2 = 0.999
ADAM_EPS = 1e-08
ADAM_WD = 0.01
ADAM_STEP = 10

N_DEV = 8
VMEM_LIMIT_BYTES = 56 * 1024 * 1024
MESH = pl.DeviceIdType.MESH


def _params(*sem):
    return pltpu.CompilerParams(dimension_semantics=sem or None, vmem_limit_bytes=VMEM_LIMIT_BYTES)


def _dot(a, b, ta, tb, precision=None):
    dims = (((0 if ta else 1,), (1 if tb else 0,)), ((), ()))
    return lax.dot_general(a, b, dims, preferred_element_type=F32, precision=precision)


def _mm_raw(a, b, ta=False, tb=False):
    return _dot(a.astype(BF16), b.astype(BF16), ta, tb)


def _mmf_raw(a, b, ta=False, tb=False):
    return _dot(a, b, ta, tb, lax.Precision.HIGHEST)


def _make_mm(raw):
    @functools.partial(jax.custom_vjp, nondiff_argnums=(2, 3))
    def mm(a, b, ta=False, tb=False):
        return raw(a, b, ta, tb)

    def fwd(a, b, ta, tb):
        return raw(a, b, ta, tb), (a, b)

    def bwd(ta, tb, res, g):
        a, b = res
        da = raw(b, g, tb, True) if ta else raw(g, b, False, not tb)
        db = raw(g, a, True, ta) if tb else raw(a, g, not ta, False)
        return da, db

    mm.defvjp(fwd, bwd)
    return mm


_mm = _make_mm(_mm_raw)
_mmf = _make_mm(_mmf_raw)


def _rms(x, g):
    return x * lax.rsqrt(jnp.mean(x * x, axis=-1, keepdims=True) + EPS) * g


def _silu_mul(g, u):
    return jax.nn.silu(g) * u


def _log_sigmoid(z):
    return jnp.minimum(z, 0.0) - jnp.log(1.0 + jnp.exp(-jnp.abs(z)))


def _matmul(a_list, b, *, ta=False, tb=False, tm, tn, b_blocks=None, res=None, scale=1.0, out_dtype=F32, name):
    if not isinstance(a_list, (list, tuple)):
        a_list = [a_list]
    n_a = len(a_list)
    m = a_list[0].shape[1] if ta else a_list[0].shape[0]
    ks = [a.shape[0] if ta else a.shape[1] for a in a_list]
    n = b.shape[0] if tb else b.shape[1]
    if b_blocks is None:
        assert n_a == 1
        b_blocks = [0]
    assert m % tm == 0 and n % tn == 0, (m, n, tm, tn)

    def body(*refs):
        a_refs, b_refs = refs[:n_a], refs[n_a:2 * n_a]
        r_ref = refs[2 * n_a] if res is not None else None
        o_ref = refs[-1]
        acc = _mm_raw(a_refs[0][...], b_refs[0][...], ta, tb)
        for k in range(1, n_a):
            acc = acc + _mm_raw(a_refs[k][...], b_refs[k][...], ta, tb)
        if scale != 1.0:
            acc = acc * scale
        if r_ref is not None:
            acc = r_ref[...] + acc
        o_ref[...] = acc.astype(out_dtype)

    in_specs = []
    for k in ks:
        in_specs.append(pl.BlockSpec((k, tm), lambda i, j: (0, i)) if ta else pl.BlockSpec((tm, k), lambda i, j: (i, 0)))
    for k, blk in zip(ks, b_blocks):
        if tb:
            in_specs.append(pl.BlockSpec((tn, k), functools.partial(lambda i, j, blk: (j, blk), blk=blk)))
        else:
            in_specs.append(pl.BlockSpec((k, tn), functools.partial(lambda i, j, blk: (blk, j), blk=blk)))
    args = list(a_list) + [b] * n_a
    if res is not None:
        in_specs.append(pl.BlockSpec((tm, tn), lambda i, j: (i, j)))
        args.append(res)
    return pl.pallas_call(
        body, name=name, grid=(m // tm, n // tn), in_specs=in_specs,
        out_specs=pl.BlockSpec((tm, tn), lambda i, j: (i, j)), out_shape=SDS((m, n), out_dtype),
        compiler_params=_params("parallel", "parallel"),
    )(*args)


def _rms_fwd(x, g, *, tm, name):
    s, d = x.shape

    def body(x_ref, g_ref, h_ref):
        h_ref[...] = _rms(x_ref[...], g_ref[...]).astype(BF16)

    return pl.pallas_call(
        body, name=name, grid=(s // tm,),
        in_specs=[pl.BlockSpec((tm, d), lambda i: (i, 0)), pl.BlockSpec((1, d), lambda i: (0, 0))],
        out_specs=pl.BlockSpec((tm, d), lambda i: (i, 0)), out_shape=SDS((s, d), BF16),
        compiler_params=_params("parallel"),
    )(x, g)


def _rms_bwd(x, g, dh, dres, *, tm, name):
    s, d = x.shape
    want_dx = dres is not None

    def body(*refs):
        if want_dx:
            x_ref, g_ref, dh_ref, dres_ref, dx_ref, dxb_ref, dg_ref = refs
        else:
            x_ref, g_ref, dh_ref, dg_ref = refs
        _, vjp = jax.vjp(_rms, x_ref[...], g_ref[...])
        dx, dg = vjp(dh_ref[...])
        if want_dx:
            dx = dres_ref[...] + dx
            dx_ref[...] = dx
            dxb_ref[...] = dx.astype(BF16)

        @pl.when(pl.program_id(0) == 0)
        def _():
            dg_ref[...] = jnp.zeros_like(dg_ref)

        dg_ref[...] += dg

    row = pl.BlockSpec((tm, d), lambda i: (i, 0))
    vec = pl.BlockSpec((1, d), lambda i: (0, 0))
    if want_dx:
        return pl.pallas_call(
            body, name=name, grid=(s // tm,), in_specs=[row, vec, row, row], out_specs=[row, row, vec],
            out_shape=[SDS((s, d), F32), SDS((s, d), BF16), SDS((1, d), F32)], compiler_params=_params("arbitrary"),
        )(x, g, dh, dres)
    return None, None, pl.pallas_call(
        body, name=name, grid=(s // tm,), in_specs=[row, vec, row], out_specs=vec,
        out_shape=SDS((1, d), F32), compiler_params=_params("arbitrary"),
    )(x, g, dh)


FFN_TN = 256
FFN_SHARD_PAD = 384


def _ffn_fwd(x, gain, wgu, wd, tag, *, tm=1024):
    s, d = x.shape
    f = wd.shape[0]
    tn = FFN_TN
    nj = f // tn
    tm = min(tm, s)

    def body(x_ref, gain_ref, wg_ref, wu_ref, wd_ref, y_ref, h_ref, g_ref, u_ref, acc_s):
        j = pl.program_id(1)

        @pl.when(j == 0)
        def _():
            h_ref[...] = _rms(x_ref[...], gain_ref[...]).astype(BF16)
            acc_s[...] = jnp.zeros_like(acc_s)

        hv = h_ref[...]
        g = _mm_raw(hv, wg_ref[...])
        u = _mm_raw(hv, wu_ref[...])
        g_ref[...] = g.astype(BF16)
        u_ref[...] = u.astype(BF16)
        acc_s[...] += _mm_raw(_silu_mul(g, u), wd_ref[...])

        @pl.when(j == nj - 1)
        def _():
            y_ref[...] = x_ref[...] + 0.5 * acc_s[...]

    row = pl.BlockSpec((tm, d), lambda i, j: (i, 0))
    tile = pl.BlockSpec((tm, tn), lambda i, j: (i, j))
    y, h, g, u = pl.pallas_call(
        body, name=f"{tag}_fwd", grid=(s // tm, nj),
        in_specs=[row, pl.BlockSpec((1, d), lambda i, j: (0, 0)),
                  pl.BlockSpec((None, d, tn), lambda i, j: (0, 0, j)), pl.BlockSpec((None, d, tn), lambda i, j: (1, 0, j)),
                  pl.BlockSpec((tn, d), lambda i, j: (j, 0))],
        out_specs=[row, row, tile, tile],
        out_shape=[SDS((s, d), F32), SDS((s, d), BF16), SDS((s, f), BF16), SDS((s, f), BF16)],
        scratch_shapes=[pltpu.VMEM((tm, d), F32)],
        compiler_params=_params("parallel", "arbitrary"),
    )(x, gain, wgu, wgu, wd)
    return y, (h, g, u)


def _ffn_bwd(dy, dyb, x, gain, wgu, wd, saved, tag):
    h, g, u = saved
    s, d = x.shape
    f = wd.shape[0]
    tn = FFN_TN

    def body(dy_ref, ht_ref, wg_ref, wu_ref, wd_ref, g_ref, u_ref, dh_ref, dwgu_ref, dwd_ref):
        @pl.when(pl.program_id(0) == 0)
        def _():
            dh_ref[...] = jnp.zeros_like(dh_ref)

        dyv = dy_ref[...]
        da = _mm_raw(dyv, wd_ref[...], False, True) * 0.5
        a, vjp = jax.vjp(_silu_mul, g_ref[...].astype(F32), u_ref[...].astype(F32))
        dg, du = vjp(da)
        dg = dg.astype(BF16)
        du = du.astype(BF16)
        dh_ref[...] += _mm_raw(dg, wg_ref[...], False, True) + _mm_raw(du, wu_ref[...], False, True)
        dwgu_ref[0] = _mm_raw(ht_ref[...], dg).astype(BF16)
        dwgu_ref[1] = _mm_raw(ht_ref[...], du).astype(BF16)
        dwd_ref[...] = (_mm_raw(a, dyv, True, False) * 0.5).astype(BF16)

    full = pl.BlockSpec((s, d), lambda j: (0, 0))
    tile = pl.BlockSpec((s, tn), lambda j: (0, j))
    wrow = pl.BlockSpec((tn, d), lambda j: (j, 0))
    dh, dwgu, dwd = pl.pallas_call(
        body, name=f"{tag}_bwd", grid=(f // tn,),
        in_specs=[full, pl.BlockSpec((d, s), lambda j: (0, 0)),
                  pl.BlockSpec((None, d, tn), lambda j: (0, 0, j)), pl.BlockSpec((None, d, tn), lambda j: (1, 0, j)),
                  wrow, tile, tile],
        out_specs=[full, pl.BlockSpec((2, d, tn), lambda j: (0, 0, j)), wrow],
        out_shape=[SDS((s, d), F32), SDS((2, d, f), BF16), SDS((f, d), BF16)],
        compiler_params=_params("arbitrary"),
    )(dyb, h.T, wgu, wgu, wd, g, u)
    dx, dxb, dgain = _rms_bwd(x, gain, dh, dy, tm=256, name=f"{tag}_drms")
    return dx, dxb, dgain, dwgu, dwd


def _loss_bwd(y, target, *, tm, name):
    s, d = y.shape

    def body(y_ref, t_ref, dy_ref, dyb_ref, l_ref):
        diff = y_ref[...] - t_ref[...]
        dy_ref[...] = diff * (1.0 / d)
        dyb_ref[...] = (diff * (1.0 / d)).astype(BF16)

        @pl.when(pl.program_id(0) == 0)
        def _():
            l_ref[...] = jnp.zeros_like(l_ref)

        l_ref[...] += 0.5 * jnp.sum(jnp.mean(diff * diff, axis=-1, keepdims=True), axis=0, keepdims=True)

    row = pl.BlockSpec((tm, d), lambda i: (i, 0))
    return pl.pallas_call(
        body, name=name, grid=(s // tm,), in_specs=[row, row],
        out_specs=[row, row, pl.BlockSpec((1, 1), lambda i: (0, 0))],
        out_shape=[SDS((s, d), F32), SDS((s, d), BF16), SDS((1, 1), F32)],
        compiler_params=_params("arbitrary"),
    )(y, target)


def _bucket_table():
    qi = np.arange(BLOCK)[:, None]
    kj = np.arange(2 * BLOCK)[None, :]
    dist = np.maximum(qi + BLOCK - kj, 0)
    max_exact = N_BUCKETS // 2
    d = np.maximum(dist, 1).astype(np.float32)
    large = max_exact + (np.log(d / np.float32(max_exact)) / np.float32(math.log(MAX_DISTANCE / max_exact))
                         * np.float32(N_BUCKETS - max_exact)).astype(np.int32)
    large = np.minimum(large, N_BUCKETS - 1)
    return np.where(dist < max_exact, dist, large).astype(np.int32)


def _swa_valid(n):
    qi = lax.broadcasted_iota(jnp.int32, (BLOCK, 2 * BLOCK), 0)
    kj = lax.broadcasted_iota(jnp.int32, (BLOCK, 2 * BLOCK), 1)
    dist = qi + BLOCK - kj
    return (dist >= 0) & (dist < BLOCK) & ((kj >= BLOCK) | (n > 0))


def _swa_head(q, kb, vb, qg, kg, sink, bias, valid):
    qn = _rms(q, qg)
    kn = _rms(kb, kg)
    s = _mm(qn, kn, False, True) * (HEAD_DIM ** -0.5) + bias
    s = jnp.where(valid, s, -jnp.inf)
    m = lax.stop_gradient(jnp.maximum(jnp.max(s, axis=-1, keepdims=True), sink))
    p = jnp.exp(s - m)
    p = p / (jnp.sum(p, axis=-1, keepdims=True) + jnp.exp(sink - m))
    return _mm(p, vb)


def _swa_bias_table(rb_ref, bucket, bias_s):
    for h in range(SWA_HEADS):
        acc = jnp.zeros((BLOCK, 2 * BLOCK), F32)
        for b in range(N_BUCKETS):
            acc = jnp.where(bucket == b, rb_ref[b, h], acc)
        bias_s[h] = acc


def _swa_band(kvp_ref, kvc_ref, g):
    lo = g * HEAD_DIM
    kb = jnp.concatenate([kvp_ref[:, lo:lo + HEAD_DIM], kvc_ref[:, lo:lo + HEAD_DIM]], axis=0)
    lo += SWA_KV_W
    vb = jnp.concatenate([kvp_ref[:, lo:lo + HEAD_DIM], kvc_ref[:, lo:lo + HEAD_DIM]], axis=0)
    return kb, vb


def _swa_specs(order):
    kvc = COL_SKV // (2 * SWA_KV_W)
    return [
        pl.BlockSpec((BLOCK, SWA_Q_W), lambda t: (order(t), 0)),
        pl.BlockSpec((BLOCK, 2 * SWA_KV_W), lambda t: (jnp.maximum(order(t) - 1, 0), kvc)),
        pl.BlockSpec((BLOCK, 2 * SWA_KV_W), lambda t: (order(t), kvc)),
        pl.BlockSpec((1, HEAD_DIM), lambda t: (0, 0)),
        pl.BlockSpec((1, HEAD_DIM), lambda t: (0, 0)),
        pl.BlockSpec(memory_space=pltpu.SMEM),
        pl.BlockSpec(memory_space=pltpu.SMEM),
        pl.BlockSpec((BLOCK, 2 * BLOCK), lambda t: (0, 0)),
    ]


def _swa_fwd(p, qg, kg, sinks, rel_bias, *, name):
    s = p.shape[0]
    nb = s // BLOCK

    def body(q_ref, kvp_ref, kvc_ref, qg_ref, kg_ref, sink_ref, rb_ref, bucket_ref, y_ref, bias_s):
        n = pl.program_id(0)

        @pl.when(n == 0)
        def _():
            _swa_bias_table(rb_ref, bucket_ref[...], bias_s)

        valid = _swa_valid(n)
        for g in range(SWA_KV_HEADS):
            kb, vb = _swa_band(kvp_ref, kvc_ref, g)
            for hh in range(SWA_GROUP):
                h = g * SWA_GROUP + hh
                cols = slice(h * HEAD_DIM, (h + 1) * HEAD_DIM)
                sink = jnp.full((1, 1), sink_ref[h], F32)
                y_ref[:, cols] = _swa_head(q_ref[:, cols], kb, vb, qg_ref[...], kg_ref[...], sink, bias_s[h], valid)

    return pl.pallas_call(
        body, name=name, grid=(nb,), in_specs=_swa_specs(lambda t: t),
        out_specs=pl.BlockSpec((BLOCK, SWA_Q_W), lambda t: (t, 0)), out_shape=SDS((s, SWA_Q_W), F32),
        scratch_shapes=[pltpu.VMEM((SWA_HEADS, BLOCK, 2 * BLOCK), F32)],
        compiler_params=_params("arbitrary"),
    )(p, p, p, qg, kg, sinks, rel_bias, jnp.asarray(_bucket_table()))


def _swa_bwd(p, qg, kg, sinks, rel_bias, dy_all, *, name):
    s = p.shape[0]
    nb = s // BLOCK

    def body(q_ref, kvp_ref, kvc_ref, qg_ref, kg_ref, sink_ref, rb_ref, bucket_ref, dy_ref,
             dq_ref, dkv_ref, dqg_ref, dkg_ref, dsink_ref, drb_ref, bias_s, dbias_s, carry_s):
        t = pl.program_id(0)
        n = nb - 1 - t

        @pl.when(t == 0)
        def _():
            _swa_bias_table(rb_ref, bucket_ref[...], bias_s)
            dbias_s[...] = jnp.zeros_like(dbias_s)
            carry_s[...] = jnp.zeros_like(carry_s)
            dqg_ref[...] = jnp.zeros_like(dqg_ref)
            dkg_ref[...] = jnp.zeros_like(dkg_ref)
            dsink_ref[...] = jnp.zeros_like(dsink_ref)
            drb_ref[...] = jnp.zeros_like(drb_ref)

        valid = _swa_valid(n)
        lane = lax.broadcasted_iota(jnp.int32, (1, BLOCK), 1)
        dqg = jnp.zeros((1, HEAD_DIM), F32)
        dkg = jnp.zeros((1, HEAD_DIM), F32)
        dsink_vec = jnp.zeros((1, BLOCK), F32)
        for g in range(SWA_KV_HEADS):
            kb, vb = _swa_band(kvp_ref, kvc_ref, g)
            dkb = jnp.zeros((2 * BLOCK, HEAD_DIM), F32)
            dvb = jnp.zeros((2 * BLOCK, HEAD_DIM), F32)
            for hh in range(SWA_GROUP):
                h = g * SWA_GROUP + hh
                cols = slice(h * HEAD_DIM, (h + 1) * HEAD_DIM)
                sink = jnp.full((1, 1), sink_ref[h], F32)
                _, vjp = jax.vjp(functools.partial(_swa_head, valid=valid),
                                 q_ref[:, cols], kb, vb, qg_ref[...], kg_ref[...], sink, bias_s[h])
                dq, dkb_h, dvb_h, dqg_h, dkg_h, dsink_h, dbias_h = vjp(dy_ref[:, cols])
                dq_ref[:, cols] = dq
                dkb += dkb_h
                dvb += dvb_h
                dqg += dqg_h
                dkg += dkg_h
                dsink_vec += jnp.where(lane == h, dsink_h, 0.0)
                dbias_s[h] += dbias_h
            lo = g * HEAD_DIM
            dkv_ref[:, lo:lo + HEAD_DIM] = dkb[BLOCK:] + carry_s[g]
            carry_s[g] = dkb[:BLOCK]
            lo += SWA_KV_W
            dkv_ref[:, lo:lo + HEAD_DIM] = dvb[BLOCK:] + carry_s[SWA_KV_HEADS + g]
            carry_s[SWA_KV_HEADS + g] = dvb[:BLOCK]
        dqg_ref[...] += dqg
        dkg_ref[...] += dkg
        dsink_ref[...] += dsink_vec

        @pl.when(t == nb - 1)
        def _():
            bucket = bucket_ref[...]
            row = lax.broadcasted_iota(jnp.int32, (N_BUCKETS, BLOCK), 0)
            col = lax.broadcasted_iota(jnp.int32, (N_BUCKETS, BLOCK), 1)
            acc = jnp.zeros((N_BUCKETS, BLOCK), F32)
            for h in range(SWA_HEADS):
                dbias = dbias_s[h]
                for b in range(N_BUCKETS):
                    part = jnp.sum(jnp.where(bucket == b, dbias, 0.0), axis=1, keepdims=True)
                    val = jnp.sum(part, axis=0, keepdims=True)
                    acc = acc + jnp.where((row == b) & (col == h), val, 0.0)
            drb_ref[...] = acc

    order = lambda t: nb - 1 - t
    vec = pl.BlockSpec((1, HEAD_DIM), lambda t: (0, 0))
    return pl.pallas_call(
        body, name=name, grid=(nb,),
        in_specs=_swa_specs(order) + [pl.BlockSpec((BLOCK, SWA_Q_W), lambda t: (order(t), 0))],
        out_specs=[pl.BlockSpec((BLOCK, SWA_Q_W), lambda t: (order(t), 0)),
                   pl.BlockSpec((BLOCK, 2 * SWA_KV_W), lambda t: (order(t), 0)),
                   vec, vec, pl.BlockSpec((1, BLOCK), lambda t: (0, 0)),
                   pl.BlockSpec((N_BUCKETS, BLOCK), lambda t: (0, 0))],
        out_shape=[SDS((s, SWA_Q_W), F32), SDS((s, 2 * SWA_KV_W), F32), SDS((1, HEAD_DIM), F32),
                   SDS((1, HEAD_DIM), F32), SDS((1, BLOCK), F32), SDS((N_BUCKETS, BLOCK), F32)],
        scratch_shapes=[pltpu.VMEM((SWA_HEADS, BLOCK, 2 * BLOCK), F32), pltpu.VMEM((SWA_HEADS, BLOCK, 2 * BLOCK), F32),
                        pltpu.VMEM((2 * SWA_KV_HEADS, BLOCK, HEAD_DIM), F32)],
        compiler_params=_params("arbitrary"),
    )(p, p, p, qg, kg, sinks, rel_bias, jnp.asarray(_bucket_table()), dy_all)


def _mem_head(q, k, v, qg, kg):
    qn = _rms(q, qg)
    kn = _rms(k, kg)
    s = _mm(qn, kn, False, True) * (HEAD_DIM ** -0.5)
    m = lax.stop_gradient(jnp.max(s, axis=-1, keepdims=True))
    e = jnp.exp(s - m)
    return _mm(e / jnp.sum(e, axis=-1, keepdims=True), v)


def _mem_fwd(p, kv, qg, kg, *, tq, name):
    s = p.shape[0]
    m = kv.shape[0]

    def body(q_ref, kv_ref, qg_ref, kg_ref, y_ref):
        for h in range(MEM_HEADS):
            cols = slice(h * HEAD_DIM, (h + 1) * HEAD_DIM)
            vcols = slice(MEM_Q_W + h * HEAD_DIM, MEM_Q_W + (h + 1) * HEAD_DIM)
            y_ref[:, cols] = _mem_head(q_ref[:, cols], kv_ref[:, cols], kv_ref[:, vcols], qg_ref[...], kg_ref[...])

    vec = pl.BlockSpec((1, HEAD_DIM), lambda t: (0, 0))
    return pl.pallas_call(
        body, name=name, grid=(s // tq,),
        in_specs=[pl.BlockSpec((tq, MEM_Q_W), lambda t: (t, COL_MQ // MEM_Q_W)),
                  pl.BlockSpec((m, 2 * MEM_Q_W), lambda t: (0, 0)), vec, vec],
        out_specs=pl.BlockSpec((tq, MEM_Q_W), lambda t: (t, 0)), out_shape=SDS((s, MEM_Q_W), F32),
        compiler_params=_params("parallel"),
    )(p, kv, qg, kg)


def _mem_bwd(p, kv, qg, kg, dy_all, *, tq, name):
    s = p.shape[0]
    m = kv.shape[0]

    def body(q_ref, kv_ref, qg_ref, kg_ref, dy_ref, dq_ref, dkv_ref, dqg_ref, dkg_ref):
        @pl.when(pl.program_id(0) == 0)
        def _():
            dkv_ref[...] = jnp.zeros_like(dkv_ref)
            dqg_ref[...] = jnp.zeros_like(dqg_ref)
            dkg_ref[...] = jnp.zeros_like(dkg_ref)

        dqg = jnp.zeros((1, HEAD_DIM), F32)
        dkg = jnp.zeros((1, HEAD_DIM), F32)
        for h in range(MEM_HEADS):
            cols = slice(h * HEAD_DIM, (h + 1) * HEAD_DIM)
            vcols = slice(MEM_Q_W + h * HEAD_DIM, MEM_Q_W + (h + 1) * HEAD_DIM)
            _, vjp = jax.vjp(_mem_head, q_ref[:, cols], kv_ref[:, cols], kv_ref[:, vcols], qg_ref[...], kg_ref[...])
            dq, dk, dv, dqg_h, dkg_h = vjp(dy_ref[:, cols])
            dq_ref[:, cols] = dq
            dkv_ref[:, cols] += dk
            dkv_ref[:, vcols] += dv
            dqg += dqg_h
            dkg += dkg_h
        dqg_ref[...] += dqg
        dkg_ref[...] += dkg

    vec = pl.BlockSpec((1, HEAD_DIM), lambda t: (0, 0))
    full = pl.BlockSpec((m, 2 * MEM_Q_W), lambda t: (0, 0))
    dy_col = (SWA_Q_W + GLA_V_W) // MEM_Q_W
    return pl.pallas_call(
        body, name=name, grid=(s // tq,),
        in_specs=[pl.BlockSpec((tq, MEM_Q_W), lambda t: (t, COL_MQ // MEM_Q_W)), full, vec, vec,
                  pl.BlockSpec((tq, MEM_Q_W), lambda t: (t, dy_col))],
        out_specs=[pl.BlockSpec((tq, MEM_Q_W), lambda t: (t, 0)), full, vec, vec],
        out_shape=[SDS((s, MEM_Q_W), F32), SDS((m, 2 * MEM_Q_W), F32), SDS((1, HEAD_DIM), F32), SDS((1, HEAD_DIM), F32)],
        compiler_params=_params("arbitrary"),
    )(p, kv, qg, kg, dy_all)


GLA_ROWS = 256


def _gla_consts():
    c, h = GLA_CHUNK, GLA_HEADS
    i2 = lax.broadcasted_iota(jnp.int32, (c, c), 0)
    j2 = lax.broadcasted_iota(jnp.int32, (c, c), 1)
    slab_q = lax.broadcasted_iota(jnp.int32, (h, c, GLA_QK_W), 0)
    lane_q = lax.broadcasted_iota(jnp.int32, (h, c, GLA_QK_W), 2)
    row_a = lax.broadcasted_iota(jnp.int32, (h * c, c), 0)
    col_a = lax.broadcasted_iota(jnp.int32, (h * c, c), 1)
    slab_o = lax.broadcasted_iota(jnp.int32, (h, c, GLA_V_W), 0)
    lane_o = lax.broadcasted_iota(jnp.int32, (h, c, GLA_V_W), 2)
    row_s = lax.broadcasted_iota(jnp.int32, (GLA_V_W, GLA_QK_W), 0)
    col_s = lax.broadcasted_iota(jnp.int32, (GLA_V_W, GLA_QK_W), 1)
    return dict(
        ltri=(j2 <= i2).astype(F32),
        m_q=(slab_q == lane_q // GLA_DK).astype(F32),
        causal=col_a <= row_a % c,
        m_o=(slab_o == lane_o // GLA_DV).astype(F32),
        m_s=(row_s // GLA_DV == col_s // GLA_DK).astype(F32),
    )


def _gla_chunk(q, k, v, z, bg, st, c):
    h, n = GLA_HEADS, GLA_CHUNK
    la = _log_sigmoid(z + bg) * (1.0 / GLA_TAU)
    b = _mmf(c["ltri"], la)
    bl = jnp.sum(la, axis=0, keepdims=True)
    qs = q * (GLA_DK ** -0.5)
    kt = k * jnp.exp(bl - b)
    qt = qs * jnp.exp(b - bl)
    qe = qs * jnp.exp(b)
    q_stack = (jnp.broadcast_to(qt[None], (h, n, GLA_QK_W)) * c["m_q"]).reshape(h * n, GLA_QK_W)
    a = jnp.where(c["causal"], _mmf(q_stack, kt, False, True), 0.0)
    o_stack = _mm(a, v)
    o_intra = jnp.sum(o_stack.reshape(h, n, GLA_V_W) * c["m_o"], axis=0)
    o_inter = _mm(qe, st, False, True)
    st_next = st * jnp.exp(bl) + _mm(v, kt, True, False) * c["m_s"]
    return o_intra + o_inter, st_next


def _gla_post(o, gg, gain, g64):
    ms = _mmf(o * o, g64) * (1.0 / GLA_DV)
    return o * lax.rsqrt(ms + EPS) * gain * jax.nn.silu(gg)


def _gla_g64():
    r = lax.broadcasted_iota(jnp.int32, (GLA_V_W, GLA_V_W), 0)
    c = lax.broadcasted_iota(jnp.int32, (GLA_V_W, GLA_V_W), 1)
    return (r // GLA_DV == c // GLA_DV).astype(F32)


def _gla_in_specs(order):
    r = GLA_ROWS
    return [
        pl.BlockSpec((r, GLA_QK_W), lambda t: (order(t), COL_GQ // GLA_QK_W)),
        pl.BlockSpec((r, GLA_QK_W), lambda t: (order(t), COL_GK // GLA_QK_W)),
        pl.BlockSpec((r, GLA_V_W), lambda t: (order(t), COL_GV // GLA_V_W)),
        pl.BlockSpec((r, GLA_V_W), lambda t: (order(t), COL_GG // GLA_V_W)),
        pl.BlockSpec((r, GLA_QK_W), lambda t: (order(t), 0)),
        pl.BlockSpec((1, GLA_QK_W), lambda t: (0, 0)),
        pl.BlockSpec((1, GLA_V_W), lambda t: (0, 0)),
    ]


def _gla_fwd(p, z, bg, gain, *, name):
    s = p.shape[0]
    r = GLA_ROWS
    cps = r // GLA_CHUNK

    def body(q_ref, k_ref, v_ref, gg_ref, z_ref, bg_ref, gain_ref, y_ref, oraw_ref, stsave_ref, st_s):
        @pl.when(pl.program_id(0) == 0)
        def _():
            st_s[...] = jnp.zeros_like(st_s)

        c = _gla_consts()
        st = st_s[...]
        for ci in range(cps):
            rows = slice(ci * GLA_CHUNK, (ci + 1) * GLA_CHUNK)
            stsave_ref[ci] = st
            o, st = _gla_chunk(q_ref[rows, :], k_ref[rows, :], v_ref[rows, :], z_ref[rows, :], bg_ref[...], st, c)
            oraw_ref[rows, :] = o
        st_s[...] = st
        y_ref[...] = _gla_post(oraw_ref[...], gg_ref[...], gain_ref[...], _gla_g64())

    rowv = pl.BlockSpec((r, GLA_V_W), lambda t: (t, 0))
    return pl.pallas_call(
        body, name=name, grid=(s // r,), in_specs=_gla_in_specs(lambda t: t),
        out_specs=[rowv, rowv, pl.BlockSpec((cps, GLA_V_W, GLA_QK_W), lambda t: (t, 0, 0))],
        out_shape=[SDS((s, GLA_V_W), F32), SDS((s, GLA_V_W), F32), SDS((s // GLA_CHUNK, GLA_V_W, GLA_QK_W), F32)],
        scratch_shapes=[pltpu.VMEM((GLA_V_W, GLA_QK_W), F32)],
        compiler_params=_params("arbitrary"),
    )(p, p, p, p, z, bg, gain)


def _gla_bwd(p, z, bg, gain, oraw, stsave, dy_all, *, name):
    s = p.shape[0]
    r = GLA_ROWS
    cps = r // GLA_CHUNK
    nsteps = s // r
    w_qkvg = 2 * GLA_QK_W + 2 * GLA_V_W

    def body(q_ref, k_ref, v_ref, gg_ref, z_ref, bg_ref, gain_ref, oraw_ref, stsave_ref, dy_ref,
             dqkvg_ref, dz_ref, dbg_ref, dgain_ref, dst_s):
        @pl.when(pl.program_id(0) == 0)
        def _():
            dst_s[...] = jnp.zeros_like(dst_s)
            dbg_ref[...] = jnp.zeros_like(dbg_ref)
            dgain_ref[...] = jnp.zeros_like(dgain_ref)

        c = _gla_consts()
        _, vjp = jax.vjp(functools.partial(_gla_post, g64=_gla_g64()), oraw_ref[...], gg_ref[...], gain_ref[...])
        do, dgg, dgain = vjp(dy_ref[...])
        dqkvg_ref[:, 2 * GLA_QK_W + GLA_V_W:] = dgg
        dgain_ref[...] += dgain
        dst = dst_s[...]
        dbg = jnp.zeros((1, GLA_QK_W), F32)
        for ci in reversed(range(cps)):
            rows = slice(ci * GLA_CHUNK, (ci + 1) * GLA_CHUNK)
            _, vjp = jax.vjp(functools.partial(_gla_chunk, c=c), q_ref[rows, :], k_ref[rows, :], v_ref[rows, :],
                             z_ref[rows, :], bg_ref[...], stsave_ref[ci])
            dq, dk, dv, dz, dbg_c, dst = vjp((do[rows, :], dst))
            dqkvg_ref[rows, 0:GLA_QK_W] = dq
            dqkvg_ref[rows, GLA_QK_W:2 * GLA_QK_W] = dk
            dqkvg_ref[rows, 2 * GLA_QK_W:2 * GLA_QK_W + GLA_V_W] = dv
            dz_ref[rows, :] = dz
            dbg += dbg_c
        dst_s[...] = dst
        dbg_ref[...] += dbg

    order = lambda t: nsteps - 1 - t
    rowv = pl.BlockSpec((r, GLA_V_W), lambda t: (order(t), 0))
    return pl.pallas_call(
        body, name=name, grid=(nsteps,),
        in_specs=_gla_in_specs(order) + [
            rowv, pl.BlockSpec((cps, GLA_V_W, GLA_QK_W), lambda t: (order(t), 0, 0)),
            pl.BlockSpec((r, GLA_V_W), lambda t: (order(t), SWA_Q_W // GLA_V_W))],
        out_specs=[pl.BlockSpec((r, w_qkvg), lambda t: (order(t), 0)), pl.BlockSpec((r, GLA_QK_W), lambda t: (order(t), 0)),
                   pl.BlockSpec((1, GLA_QK_W), lambda t: (0, 0)), pl.BlockSpec((1, GLA_V_W), lambda t: (0, 0))],
        out_shape=[SDS((s, w_qkvg), F32), SDS((s, GLA_QK_W), F32), SDS((1, GLA_QK_W), F32), SDS((1, GLA_V_W), F32)],
        scratch_shapes=[pltpu.VMEM((GLA_V_W, GLA_QK_W), F32)],
        compiler_params=_params("arbitrary"),
    )(p, p, p, p, z, bg, gain, oraw, stsave, dy_all)


def _local_step(x, mem, target, small, big, on_grads):
    g1, gmix, gmem, g2, sqg, skg, sinks, rel_bias, wgu, bg, gla_gain, mqg, mkg = small
    wgu1, wd1, win_p, wkv, wout, gather_ffn2 = big
    wgu_pad = jnp.zeros((GLA_QK_W, GLA_QK_W), BF16).at[:GLA_RANK].set(wgu.astype(BF16))
    gain256 = jnp.tile(gla_gain, (1, GLA_HEADS))

    x1, saved1 = _ffn_fwd(x, g1, wgu1, wd1, "ffn1")
    wgu2, wd2 = gather_ffn2(x1)
    h = _rms_fwd(x1, gmix, tm=256, name="mix_rms")
    p = _matmul(h, win_p, tm=512, tn=384, name="mix_in")
    hm = _rms_fwd(mem, gmem, tm=256, name="mem_rms")
    kv = _matmul(hm, wkv, tm=256, tn=512, name="mem_kv")
    p_glr = p[:, COL_GLR:]
    z = _matmul(p_glr, wgu_pad, tm=512, tn=GLA_QK_W, name="gla_gate")
    y_swa = _swa_fwd(p, sqg, skg, sinks, rel_bias, name="swa_fwd")
    y_gla, oraw, stsave = _gla_fwd(p, z, bg, gain256, name="gla_fwd")
    y_mem = _mem_fwd(p, kv, mqg, mkg, tq=256, name="mem_fwd")
    x2 = _matmul([y_swa, y_gla, y_mem], wout, b_blocks=[0, 2, 3], tm=512, tn=512, res=x1, name="mix_out")
    x3, saved2 = _ffn_fwd(x2, g2, wgu2, wd2, "ffn2")

    dy, dyb, loss = _loss_bwd(x3, target, tm=256, name="loss")
    dx2, dx2b, dg2, dwgu2, dwd2 = _ffn_bwd(dy, dyb, x2, g2, wgu2, wd2, saved2, "ffn2")
    dx2b = on_grads("ffn2", (dwgu2, dwd2), dx2b)
    dy_all = _matmul(dx2b, wout, tb=True, tm=512, tn=512, name="mix_dy")
    dwout = _matmul(jnp.concatenate([y_swa, y_gla, y_mem], axis=1), dx2b, ta=True, tm=512, tn=512, out_dtype=BF16,
                    name="mix_dw_out")
    dq_swa, dkv_swa, dsqg, dskg, dsink, drb = _swa_bwd(p, sqg, skg, sinks, rel_bias, dy_all, name="swa_bwd")
    dqkvg, dz, dbg, dgain256 = _gla_bwd(p, z, bg, gain256, oraw, stsave, dy_all, name="gla_bwd")
    dmq, dkv_mem, dmqg, dmkg = _mem_bwd(p, kv, mqg, mkg, dy_all, tq=256, name="mem_bwd")
    dglr = _matmul(dz, wgu_pad, tb=True, tm=512, tn=GLA_QK_W, name="gla_gate_dx")
    dwgu_pad = _matmul(p_glr, dz, ta=True, tm=GLA_QK_W, tn=GLA_QK_W, name="gla_gate_dw")
    dp = jnp.concatenate([dq_swa, dkv_swa, dqkvg, dmq, dglr], axis=1)
    dh = _matmul(dp, win_p, tb=True, tm=512, tn=512, name="mix_dh")
    dwin_p = _matmul(h, dp, ta=True, tm=512, tn=384, out_dtype=BF16, name="mix_dw_in")
    dx1, dx1b, dgmix = _rms_bwd(x1, gmix, dh, dx2, tm=256, name="mix_drms")
    dwkv = _matmul(hm, dkv_mem, ta=True, tm=512, tn=512, out_dtype=BF16, name="mem_dw_kv")
    dx1b = on_grads("mix", (dwin_p, dwkv, dwout), dx1b)
    dhm = _matmul(dkv_mem, wkv, tb=True, tm=256, tn=512, name="mem_dh")
    _, _, dgmem = _rms_bwd(mem, gmem, dhm, None, tm=256, name="mem_drms")
    dx, _, dg1, dwgu1, dwd1 = _ffn_bwd(dx1, dx1b, x, g1, wgu1, wd1, saved1, "ffn1")

    dgla_gain = dgain256.reshape(GLA_HEADS, GLA_DV).sum(axis=0, keepdims=True)
    dsmall = (dg1, dgmix, dgmem, dg2, dsqg, dskg, dsink[0, :SWA_HEADS], drb[:, :SWA_HEADS], dwgu_pad[:GLA_RANK], dbg,
              dgla_gain, dmqg, dmkg)
    on_grads("ffn1", (dwgu1, dwd1), None, small=list(dsmall) + [loss])
    return dx


def _mesh_place():
    x, y, c = lax.axis_index("x"), lax.axis_index("y"), lax.axis_index("c")
    other_chips = [(1 - x, y), (x, 1 - y), (1 - x, 1 - y)]
    return x, y, c, other_chips


def _handshake(peers):
    barrier = pltpu.get_barrier_semaphore()
    for peer in peers:
        pl.semaphore_signal(barrier, inc=1, device_id=peer, device_id_type=MESH)
    pl.semaphore_wait(barrier, len(peers))


def _sequencer_call(body, operands, out_shapes, sems, *, name, collective_id):
    return pl.kernel(
        body, name=name, out_type=out_shapes, mesh=plsc.ScalarSubcoreMesh(axis_name="sequencer", num_cores=1),
        scratch_types=sems, compiler_params=pltpu.CompilerParams(collective_id=collective_id),
    )(*operands)


def _window(ref, kind, slot, shape):
    if kind == "col":
        return ref.at[:, :, pl.ds(pl.multiple_of(slot * shape[-1], 128), shape[-1])]
    if kind == "row":
        return ref.at[pl.ds(pl.multiple_of(slot * shape[0], 8), shape[0])]
    return ref.at[slot]


def _gathered(shape, kind):
    if kind == "col":
        return tuple(shape[:-1]) + (N_DEV * shape[-1],)
    if kind == "row":
        return (N_DEV * shape[0],) + tuple(shape[1:])
    return (N_DEV,) + tuple(shape)


def _all_gather(shards, kinds, *, name, collective_id):
    nt = len(shards)

    def body(*refs):
        x_refs, o_refs = refs[:nt], refs[nt:2 * nt]
        send_sems, recv_sems, local_sems = refs[2 * nt:]
        x, y, c, chips = _mesh_place()
        me, sibling = (x, y, c), (x, y, 1 - c)
        _handshake([sibling] + [(*chip, c) for chip in chips])

        def copy(k, t, block, to, from_shard=False):
            bx, by, bc = block
            rows = _window(o_refs[t], kinds[t], 4 * bx + 2 * by + bc, shards[t].shape)
            return pltpu.make_async_remote_copy(
                src_ref=x_refs[t] if from_shard else rows, dst_ref=rows,
                send_sem=send_sems.at[k, t], recv_sem=recv_sems.at[k, t], device_id=to, device_id_type=MESH)

        mine = [pltpu.make_async_copy(x_refs[t], _window(o_refs[t], kinds[t], 4 * x + 2 * y + c, shards[t].shape),
                                      local_sems.at[t]) for t in range(nt)]
        for cp in mine:
            cp.start()
        first = [copy(0, t, me, sibling, True) for t in range(nt)]
        first += [copy(1 + j, t, me, (*chip, c), True) for j, chip in enumerate(chips) for t in range(nt)]
        for cp in first:
            cp.start()
        passed = []
        for j, chip in enumerate(chips):
            for t in range(nt):
                copy(1 + j, t, (*chip, c), me).wait_recv()
                fwd = copy(4 + j, t, (*chip, c), sibling)
                fwd.start()
                passed.append(fwd)
        for t in range(nt):
            copy(0, t, sibling, me).wait_recv()
        for j, chip in enumerate(chips):
            for t in range(nt):
                copy(4 + j, t, (*chip, 1 - c), me).wait_recv()
        for cp in first + passed:
            cp.wait_send()
        for cp in mine:
            cp.wait()

    return _sequencer_call(
        body, shards, [SDS(_gathered(s.shape, k), s.dtype) for s, k in zip(shards, kinds)],
        [pltpu.SemaphoreType.DMA((7, nt)), pltpu.SemaphoreType.DMA((7, nt)), pltpu.SemaphoreType.DMA((nt,))],
        name=name, collective_id=collective_id)


def _part_shape(shape, kind):
    if kind == "col":
        return tuple(shape[:-1]) + (shape[-1] // N_DEV,)
    if kind == "row":
        return (shape[0] // N_DEV,) + tuple(shape[1:])
    return tuple(shape[2:])


def _pair_exchange(grads, kinds, *, name, collective_id):
    nt = len(grads)
    part = [_part_shape(g.shape, k) for g, k in zip(grads, kinds)]

    def body(*refs):
        g_refs, o_refs = refs[:nt], refs[nt:2 * nt]
        send_sems, recv_sems = refs[2 * nt:]
        x, y, c, _ = _mesh_place()
        _handshake([(x, y, 1 - c)])
        copies = []
        for t in range(nt):
            for xy in range(4):
                src = g_refs[t].at[1 - c, xy] if kinds[t] == "stack" else _window(g_refs[t], kinds[t], 2 * xy + 1 - c, part[t])
                copies.append(pltpu.make_async_remote_copy(
                    src_ref=src, dst_ref=o_refs[t].at[xy], send_sem=send_sems.at[xy, t], recv_sem=recv_sems.at[xy, t],
                    device_id=(x, y, 1 - c), device_id_type=MESH))
        for cp in copies:
            cp.start()
        for cp in copies:
            cp.wait()

    return _sequencer_call(
        body, grads, [SDS((4,) + p, g.dtype) for p, g in zip(part, grads)],
        [pltpu.SemaphoreType.DMA((4, nt)), pltpu.SemaphoreType.DMA((4, nt))], name=name, collective_id=collective_id)


def _chip_exchange(parts, small, *, name, collective_id):
    nt = len(parts)
    if small is None:
        def body_plain(*refs):
            s_refs, o_refs = refs[:nt], refs[nt:2 * nt]
            send_sems, recv_sems = refs[2 * nt:]
            x, y, c, chips = _mesh_place()
            _handshake([(*chip, c) for chip in chips])
            copies = [pltpu.make_async_remote_copy(
                src_ref=s_refs[t].at[2 * chip[0] + chip[1]], dst_ref=o_refs[t].at[j],
                send_sem=send_sems.at[j, t], recv_sem=recv_sems.at[j, t], device_id=(*chip, c), device_id_type=MESH)
                for j, chip in enumerate(chips) for t in range(nt)]
            for cp in copies:
                cp.start()
            for cp in copies:
                cp.wait()

        return _sequencer_call(
            body_plain, parts, [SDS((3,) + s.shape[1:], s.dtype) for s in parts],
            [pltpu.SemaphoreType.DMA((3, nt)), pltpu.SemaphoreType.DMA((3, nt))], name=name, collective_id=collective_id)

    def body(*refs):
        s_refs, small_ref = refs[:nt], refs[nt]
        o_refs, small_all = refs[nt + 1:2 * nt + 1], refs[2 * nt + 1]
        send_sems, recv_sems, small_send, small_recv, local_sem = refs[2 * nt + 2:]
        x, y, c, chips = _mesh_place()
        _handshake([(px, py, pc) for px in (x, 1 - x) for py in (y, 1 - y) for pc in (c, 1 - c)][1:])

        def copy(j, t, chip):
            return pltpu.make_async_remote_copy(
                src_ref=s_refs[t].at[2 * chip[0] + chip[1]], dst_ref=o_refs[t].at[j],
                send_sem=send_sems.at[j, t], recv_sem=recv_sems.at[j, t], device_id=(*chip, c), device_id_type=MESH)

        flips = [(fx, fy, fc) for fx in (0, 1) for fy in (0, 1) for fc in (0, 1)][1:]

        def small_copy(k):
            fx, fy, fc = flips[k]
            to = (x ^ fx if fx else x, y ^ fy if fy else y, c ^ fc if fc else c)
            rows = small_all.at[4 * x + 2 * y + c]
            return pltpu.make_async_remote_copy(
                src_ref=small_ref, dst_ref=rows, send_sem=small_send.at[k], recv_sem=small_recv.at[k],
                device_id=to, device_id_type=MESH)

        own = pltpu.make_async_copy(small_ref, small_all.at[4 * x + 2 * y + c], local_sem)
        own.start()
        copies = [copy(j, t, chip) for j, chip in enumerate(chips) for t in range(nt)]
        smalls = [small_copy(k) for k in range(7)]
        for cp in smalls + copies:
            cp.start()
        for cp in smalls + copies:
            cp.wait()
        own.wait()

    return _sequencer_call(
        body, list(parts) + [small],
        [SDS((3,) + s.shape[1:], s.dtype) for s in parts] + [SDS((N_DEV,) + small.shape, small.dtype)],
        [pltpu.SemaphoreType.DMA((3, nt)), pltpu.SemaphoreType.DMA((3, nt)),
         pltpu.SemaphoreType.DMA((7,)), pltpu.SemaphoreType.DMA((7,)), pltpu.SemaphoreType.DMA],
        name=name, collective_id=collective_id)


def _pair_sum(grad, theirs, kind, c, *, name):
    if kind == "col":
        n, r, l = theirs.shape[1:]
        mine_spec = pl.BlockSpec((None, r, l), lambda xy, m, c_ref: (m, 0, 2 * xy + c_ref[0]))
    elif kind == "row":
        (r, l), n = theirs.shape[1:], 1
        theirs = theirs[:, None]
        mine_spec = pl.BlockSpec((r, l), lambda xy, m, c_ref: (2 * xy + c_ref[0], 0))
    else:
        r, l = theirs.shape[-2:]
        n = theirs.size // (4 * r * l)
        theirs = theirs.reshape(4, n, r, l)
        grad = grad.reshape(2, 4, n, r, l)
        mine_spec = pl.BlockSpec((None, None, None, r, l), lambda xy, m, c_ref: (c_ref[0], xy, m, 0, 0))

    def body(c_ref, a_ref, b_ref, o_ref):
        o_ref[...] = (a_ref[...].astype(F32) + b_ref[...].astype(F32)).astype(BF16)

    part = pl.BlockSpec((None, None, r, l), lambda xy, m, c_ref: (xy, m, 0, 0))
    return pl.pallas_call(
        body, name=name,
        grid_spec=pltpu.PrefetchScalarGridSpec(num_scalar_prefetch=1, grid=(4, n), in_specs=[mine_spec, part], out_specs=part),
        out_shape=SDS((4, n, r, l), BF16), compiler_params=_params("parallel", "parallel"),
    )(c, grad, theirs)


def _adamw(w, g, m, v):
    m = ADAM_B1 * m + (1.0 - ADAM_B1) * g
    v = ADAM_B2 * v + (1.0 - ADAM_B2) * jnp.square(g)
    m_hat = m / (1.0 - ADAM_B1 ** ADAM_STEP)
    v_hat = v / (1.0 - ADAM_B2 ** ADAM_STEP)
    delta = -ADAM_LR * (m_hat / (jnp.sqrt(v_hat) + ADAM_EPS) + ADAM_WD * w)
    return delta, m, v


def _adam_big(own, others, mat, xy, w, m, v, *, tr, name):
    _, r, l = w.shape
    lp = own.shape[-1]

    def body(xy_ref, own_ref, oth_ref, w_ref, m_ref, v_ref, g_out, d_out, m_out, v_out):
        g = own_ref[0, 0].astype(F32)
        for j in range(3):
            g = g + oth_ref[j, 0].astype(F32)
        g = g[:, :l]
        delta, m_new, v_new = _adamw(w_ref[0], g, m_ref[0], v_ref[0])
        g_out[0] = g
        d_out[0] = delta
        m_out[0] = m_new
        v_out[0] = v_new

    blk = pl.BlockSpec((1, tr, l), lambda i, xy_ref: (0, i, 0))
    return pl.pallas_call(
        body, name=name,
        grid_spec=pltpu.PrefetchScalarGridSpec(
            num_scalar_prefetch=1, grid=(r // tr,),
            in_specs=[pl.BlockSpec((1, 1, tr, lp), lambda i, xy_ref: (xy_ref[0], mat, i, 0)),
                      pl.BlockSpec((3, 1, tr, lp), lambda i, xy_ref: (0, mat, i, 0)), blk, blk, blk],
            out_specs=[blk, blk, blk, blk]),
        out_shape=[SDS(w.shape, F32)] * 4, compiler_params=_params("parallel"),
    )(xy, own, others, w, m, v)


def _adam_small(g_all, w, m, v, *, name):
    def body(g_ref, w_ref, m_ref, v_ref, g_out, d_out, m_out, v_out):
        g = g_ref[0]
        for k in range(1, N_DEV):
            g = g + g_ref[k]
        delta, m_new, v_new = _adamw(w_ref[...], g, m_ref[...], v_ref[...])
        g_out[...] = g
        d_out[...] = delta
        m_out[...] = m_new
        v_out[...] = v_new

    return pl.pallas_call(body, name=name, out_shape=[SDS(w.shape, F32)] * 4)(g_all, w, m, v)


SMALL_ROWS = 56


def _pack_small(parts):
    flat = jnp.concatenate([a.reshape(-1) for a in parts])
    return jnp.pad(flat, (0, SMALL_ROWS * 128 - flat.shape[0])).reshape(SMALL_ROWS, 128)


def _unpack_small(packed, shapes):
    flat = packed.reshape(-1)
    out, at = [], 0
    for s in shapes:
        n = math.prod(s)
        out.append(flat[at:at + n].reshape(s))
        at += n
    return out


def kernel(x, mem, ffn1_norm, ffn1_w_gate, ffn1_w_up, ffn1_w_down, mix_norm, mem_norm, w_in, w_mem_kv, swa_q_norm, swa_k_norm, swa_sinks, rel_bias, gla_w_gate_up, gla_b_gate, gla_out_norm, mem_q_norm, mem_k_norm, w_out, ffn2_norm, ffn2_w_gate, ffn2_w_up, ffn2_w_down, loss_target, m_ffn1_norm, m_ffn1_w_gate, m_ffn1_w_up, m_ffn1_w_down, m_mix_norm, m_mem_norm, m_w_in, m_w_mem_kv, m_swa_q_norm, m_swa_k_norm, m_swa_sinks, m_rel_bias, m_gla_w_gate_up, m_gla_b_gate, m_gla_out_norm, m_mem_q_norm, m_mem_k_norm, m_w_out, m_ffn2_norm, m_ffn2_w_gate, m_ffn2_w_up, m_ffn2_w_down, v_ffn1_norm, v_ffn1_w_gate, v_ffn1_w_up, v_ffn1_w_down, v_mix_norm, v_mem_norm, v_w_in, v_w_mem_kv, v_swa_q_norm, v_swa_k_norm, v_swa_sinks, v_rel_bias, v_gla_w_gate_up, v_gla_b_gate, v_gla_out_norm, v_mem_q_norm, v_mem_k_norm, v_w_out, v_ffn2_norm, v_ffn2_w_gate, v_ffn2_w_up, v_ffn2_w_down):
    xi, yi, ci = lax.axis_index("x"), lax.axis_index("y"), lax.axis_index("c")
    c_arr = jnp.reshape(ci, (1,)).astype(jnp.int32)
    xy_arr = jnp.reshape(2 * xi + yi, (1,)).astype(jnp.int32)
    d = x.shape[-1]

    pad_h = FFN_SHARD_PAD - ffn1_w_gate.shape[-1]

    def gather_ffn(wg_s, wu_s, wd_s, name, collective_id, after):
        wgu_s = jnp.pad(jnp.concatenate([wg_s, wu_s], axis=0), ((0, 0), (0, 0), (0, pad_h))).astype(BF16)
        wd_s = jnp.pad(wd_s[0], ((0, pad_h), (0, 0))).astype(BF16)
        if after is not None:
            wgu_s, wd_s, _ = lax.optimization_barrier((wgu_s, wd_s, after))
        return _all_gather([wgu_s, wd_s], ["col", "row"], name=name, collective_id=collective_id)

    wgu1, wd1 = gather_ffn(ffn1_w_gate, ffn1_w_up, ffn1_w_down, "gather_ffn1", 0, None)
    mix_s = lax.optimization_barrier((w_in[0].astype(BF16), w_mem_kv[0].astype(BF16), w_out[0].astype(BF16), wd1))[:3]
    win_all, wkv, wout = _all_gather(list(mix_s), ["stack", "row", "row"], name="gather_mix", collective_id=1)

    def gather_ffn2(x1):
        return gather_ffn(ffn2_w_gate, ffn2_w_up, ffn2_w_down, "gather_ffn2", 2, (wout, x1))

    win = win_all.transpose(1, 0, 2).reshape(d, -1)
    glr_lo, glr_hi = COL_MQ, COL_MQ + GLA_RANK
    win_p = jnp.concatenate([win[:, :glr_lo], win[:, glr_hi:], win[:, glr_lo:glr_hi],
                             jnp.zeros((d, IN_W_PAD - IN_W), BF16)], axis=1)

    small_w = [ffn1_norm, mix_norm, mem_norm, ffn2_norm, swa_q_norm, swa_k_norm, swa_sinks[0], rel_bias,
               gla_w_gate_up[0], gla_b_gate, gla_out_norm, mem_q_norm, mem_k_norm]
    collective_ids = {"ffn2": (3, 4), "mix": (5, 6), "ffn1": (7, 8)}
    reduced, small_box = {}, []

    def on_grads(group, grads, carry, small=None):
        if group == "mix":
            dwin_p, dwkv, dwout = grads
            dwin = jnp.concatenate([dwin_p[:, :glr_lo], dwin_p[:, COL_GLR:COL_GLR + GLA_RANK], dwin_p[:, COL_MQ:COL_GLR]],
                                   axis=1)
            w = dwin.shape[1] // N_DEV
            grads = [dwin.reshape(d, 4, 2, w).transpose(2, 1, 0, 3).astype(BF16), dwkv, dwout]
            kinds = ["stack", "row", "row"]
        else:
            kinds = ["col", "row"]
        if reduced:
            earlier = list(reduced.values())[-1][1]
            *grads, _ = lax.optimization_barrier((*grads, earlier[0]))
        id_pair, id_chip = collective_ids[group]
        from_sibling = _pair_exchange(grads, kinds, name=f"pair_exchange_{group}", collective_id=id_pair)
        chip_sums = [_pair_sum(g, theirs, k, c_arr, name=f"pair_sum_{group}_{t}")
                     for t, (g, theirs, k) in enumerate(zip(grads, from_sibling, kinds))]
        if carry is not None:
            *chip_sums, carry = lax.optimization_barrier((*chip_sums, carry))
        if small is None:
            from_chips = _chip_exchange(chip_sums, None, name=f"chip_exchange_{group}", collective_id=id_chip)
        else:
            *from_chips, small_all = _chip_exchange(chip_sums, _pack_small(small), name=f"chip_exchange_{group}",
                                                    collective_id=id_chip)
            small_box.append(small_all)
        reduced[group] = (chip_sums, from_chips)
        return carry

    grad_x = _local_step(x[0], mem[0], loss_target[0], small_w, (wgu1, wd1, win_p, wkv, wout, gather_ffn2), on_grads)
    small_all = small_box[0]

    big_w = {"ffn1_w_gate": ("ffn1", 0, 0, ffn1_w_gate, m_ffn1_w_gate, v_ffn1_w_gate),
             "ffn1_w_up": ("ffn1", 0, 1, ffn1_w_up, m_ffn1_w_up, v_ffn1_w_up),
             "ffn1_w_down": ("ffn1", 1, 0, ffn1_w_down, m_ffn1_w_down, v_ffn1_w_down),
             "w_in": ("mix", 0, 0, w_in, m_w_in, v_w_in),
             "w_mem_kv": ("mix", 1, 0, w_mem_kv, m_w_mem_kv, v_w_mem_kv),
             "w_out": ("mix", 2, 0, w_out, m_w_out, v_w_out),
             "ffn2_w_gate": ("ffn2", 0, 0, ffn2_w_gate, m_ffn2_w_gate, v_ffn2_w_gate),
             "ffn2_w_up": ("ffn2", 0, 1, ffn2_w_up, m_ffn2_w_up, v_ffn2_w_up),
             "ffn2_w_down": ("ffn2", 1, 0, ffn2_w_down, m_ffn2_w_down, v_ffn2_w_down)}
    res = {}
    for nm, (group, t, mat, w, m, v) in big_w.items():
        r = w.shape[1]
        tr = 256 if r % 256 == 0 else r
        chip_sums, from_chips = reduced[group]
        res[nm] = _adam_big(chip_sums[t], from_chips[t], mat, xy_arr, w, m, v, tr=tr, name=f"adam_{nm}")
    small_names = ["ffn1_norm", "mix_norm", "mem_norm", "ffn2_norm", "swa_q_norm", "swa_k_norm", "swa_sinks", "rel_bias",
                   "gla_w_gate_up", "gla_b_gate", "gla_out_norm", "mem_q_norm", "mem_k_norm"]
    small_m = [m_ffn1_norm, m_mix_norm, m_mem_norm, m_ffn2_norm, m_swa_q_norm, m_swa_k_norm, m_swa_sinks, m_rel_bias,
               m_gla_w_gate_up, m_gla_b_gate, m_gla_out_norm, m_mem_q_norm, m_mem_k_norm]
    small_v = [v_ffn1_norm, v_mix_norm, v_mem_norm, v_ffn2_norm, v_swa_q_norm, v_swa_k_norm, v_swa_sinks, v_rel_bias,
               v_gla_w_gate_up, v_gla_b_gate, v_gla_out_norm, v_mem_q_norm, v_mem_k_norm]
    small_full = [ffn1_norm, mix_norm, mem_norm, ffn2_norm, swa_q_norm, swa_k_norm, swa_sinks, rel_bias,
                  gla_w_gate_up, gla_b_gate, gla_out_norm, mem_q_norm, mem_k_norm]
    packed = _adam_small(small_all, _pack_small(small_full), _pack_small(small_m), _pack_small(small_v), name="adam_small")
    full_shapes = [a.shape for a in small_full]
    unpacked = [_unpack_small(pk, full_shapes + [()]) for pk in packed]
    for k, nm in enumerate(small_names):
        res[nm] = [unpacked[q][k] for q in range(4)]
    loss = unpacked[0][len(small_names)]

    order = ["ffn1_norm", "ffn1_w_gate", "ffn1_w_up", "ffn1_w_down", "mix_norm", "mem_norm", "w_in", "w_mem_kv",
             "swa_q_norm", "swa_k_norm", "swa_sinks", "rel_bias", "gla_w_gate_up", "gla_b_gate", "gla_out_norm",
             "mem_q_norm", "mem_k_norm", "w_out", "ffn2_norm", "ffn2_w_gate", "ffn2_w_up", "ffn2_w_down"]
    outs = [loss, grad_x[None]]
    for q in range(4):
        outs += [res[nm][q] for nm in order]
    return tuple(outs)
```

```python
import functools
import math

import numpy as np
import jax
import jax.numpy as jnp
from jax import lax
from jax.experimental import pallas as pl
from jax.experimental.pallas import tpu as pltpu
from jax.experimental.pallas import tpu_sc as plsc

F32 = jnp.float32
BF16 = jnp.bfloat16
SDS = jax.ShapeDtypeStruct

EPS = 1e-6
HEAD_DIM = 64
SWA_HEADS = 8
SWA_KV_HEADS = 2
SWA_GROUP = SWA_HEADS // SWA_KV_HEADS
BLOCK = 128
N_BUCKETS = 32
MAX_DISTANCE = 128
GLA_HEADS = 4
GLA_DK = 32
GLA_DV = 64
GLA_RANK = 16
GLA_TAU = 16.0
GLA_CHUNK = 32
MEM_HEADS = 4
SWA_Q_W = SWA_HEADS * HEAD_DIM
SWA_KV_W = SWA_KV_HEADS * HEAD_DIM
GLA_QK_W = GLA_HEADS * GLA_DK
GLA_V_W = GLA_HEADS * GLA_DV
MEM_Q_W = MEM_HEADS * HEAD_DIM
IN_W = 1808
IN_W_PAD = 1920
COL_SQ, COL_SKV, COL_GQ, COL_GK, COL_GV, COL_GG, COL_MQ, COL_GLR = 0, 512, 768, 896, 1024, 1280, 1536, 1792

ADAM_LR = 0.001
ADAM_B1 = 0.9
ADAM_B2 = 0.999
ADAM_EPS = 1e-08
ADAM_WD = 0.01
ADAM_STEP = 10

N_DEV = 8
VMEM_LIMIT_BYTES = 56 * 1024 * 1024
MESH = pl.DeviceIdType.MESH


def _params(*sem):
    return pltpu.CompilerParams(dimension_semantics=sem or None, vmem_limit_bytes=VMEM_LIMIT_BYTES)


def _dot(a, b, ta, tb, precision=None):
    dims = (((0 if ta else 1,), (1 if tb else 0,)), ((), ()))
    return lax.dot_general(a, b, dims, preferred_element_type=F32, precision=precision)


def _mm_raw(a, b, ta=False, tb=False):
    return _dot(a.astype(BF16), b.astype(BF16), ta, tb)


def _mmf_raw(a, b, ta=False, tb=False):
    return _dot(a, b, ta, tb, lax.Precision.HIGHEST)


def _make_mm(raw):
    @functools.partial(jax.custom_vjp, nondiff_argnums=(2, 3))
    def mm(a, b, ta=False, tb=False):
        return raw(a, b, ta, tb)

    def fwd(a, b, ta, tb):
        return raw(a, b, ta, tb), (a, b)

    def bwd(ta, tb, res, g):
        a, b = res
        da = raw(b, g, tb, True) if ta else raw(g, b, False, not tb)
        db = raw(g, a, True, ta) if tb else raw(a, g, not ta, False)
        return da, db

    mm.defvjp(fwd, bwd)
    return mm


_mm = _make_mm(_mm_raw)
_mmf = _make_mm(_mmf_raw)


def _rms(x, g):
    return x * lax.rsqrt(jnp.mean(x * x, axis=-1, keepdims=True) + EPS) * g


def _silu_mul(g, u):
    return jax.nn.silu(g) * u


def _log_sigmoid(z):
    return jnp.minimum(z, 0.0) - jnp.log(1.0 + jnp.exp(-jnp.abs(z)))


def _matmul(a_list, b, *, ta=False, tb=False, tm, tn, b_blocks=None, res=None, scale=1.0, out_dtype=F32, name):
    if not isinstance(a_list, (list, tuple)):
        a_list = [a_list]
    n_a = len(a_list)
    m = a_list[0].shape[1] if ta else a_list[0].shape[0]
    ks = [a.shape[0] if ta else a.shape[1] for a in a_list]
    n = b.shape[0] if tb else b.shape[1]
    if b_blocks is None:
        assert n_a == 1
        b_blocks = [0]
    assert m % tm == 0 and n % tn == 0, (m, n, tm, tn)

    def body(*refs):
        a_refs, b_refs = refs[:n_a], refs[n_a:2 * n_a]
        r_ref = refs[2 * n_a] if res is not None else None
        o_ref = refs[-1]
        acc = _mm_raw(a_refs[0][...], b_refs[0][...], ta, tb)
        for k in range(1, n_a):
            acc = acc + _mm_raw(a_refs[k][...], b_refs[k][...], ta, tb)
        if scale != 1.0:
            acc = acc * scale
        if r_ref is not None:
            acc = r_ref[...] + acc
        o_ref[...] = acc.astype(out_dtype)

    in_specs = []
    for k in ks:
        in_specs.append(pl.BlockSpec((k, tm), lambda i, j: (0, i)) if ta else pl.BlockSpec((tm, k), lambda i, j: (i, 0)))
    for k, blk in zip(ks, b_blocks):
        if tb:
            in_specs.append(pl.BlockSpec((tn, k), functools.partial(lambda i, j, blk: (j, blk), blk=blk)))
        else:
            in_specs.append(pl.BlockSpec((k, tn), functools.partial(lambda i, j, blk: (blk, j), blk=blk)))
    args = list(a_list) + [b] * n_a
    if res is not None:
        in_specs.append(pl.BlockSpec((tm, tn), lambda i, j: (i, j)))
        args.append(res)
    return pl.pallas_call(
        body, name=name, grid=(m // tm, n // tn), in_specs=in_specs,
        out_specs=pl.BlockSpec((tm, tn), lambda i, j: (i, j)), out_shape=SDS((m, n), out_dtype),
        compiler_params=_params("parallel", "parallel"),
    )(*args)


def _rms_fwd(x, g, *, tm, name):
    s, d = x.shape

    def body(x_ref, g_ref, h_ref):
        h_ref[...] = _rms(x_ref[...], g_ref[...]).astype(BF16)

    return pl.pallas_call(
        body, name=name, grid=(s // tm,),
        in_specs=[pl.BlockSpec((tm, d), lambda i: (i, 0)), pl.BlockSpec((1, d), lambda i: (0, 0))],
        out_specs=pl.BlockSpec((tm, d), lambda i: (i, 0)), out_shape=SDS((s, d), BF16),
        compiler_params=_params("parallel"),
    )(x, g)


def _rms_bwd(x, g, dh, dres, *, tm, name):
    s, d = x.shape
    want_dx = dres is not None

    def body(*refs):
        if want_dx:
            x_ref, g_ref, dh_ref, dres_ref, dx_ref, dxb_ref, dg_ref = refs
        else:
            x_ref, g_ref, dh_ref, dg_ref = refs
        _, vjp = jax.vjp(_rms, x_ref[...], g_ref[...])
        dx, dg = vjp(dh_ref[...])
        if want_dx:
            dx = dres_ref[...] + dx
            dx_ref[...] = dx
            dxb_ref[...] = dx.astype(BF16)

        @pl.when(pl.program_id(0) == 0)
        def _():
            dg_ref[...] = jnp.zeros_like(dg_ref)

        dg_ref[...] += dg

    row = pl.BlockSpec((tm, d), lambda i: (i, 0))
    vec = pl.BlockSpec((1, d), lambda i: (0, 0))
    if want_dx:
        return pl.pallas_call(
            body, name=name, grid=(s // tm,), in_specs=[row, vec, row, row], out_specs=[row, row, vec],
            out_shape=[SDS((s, d), F32), SDS((s, d), BF16), SDS((1, d), F32)], compiler_params=_params("arbitrary"),
        )(x, g, dh, dres)
    return None, None, pl.pallas_call(
        body, name=name, grid=(s // tm,), in_specs=[row, vec, row], out_specs=vec,
        out_shape=SDS((1, d), F32), compiler_params=_params("arbitrary"),
    )(x, g, dh)


FFN_TN = 256
FFN_SHARD_PAD = 384


def _ffn_fwd(x, gain, wgu, wd, tag, *, tm=1024):
    s, d = x.shape
    f = wd.shape[0]
    tn = FFN_TN
    nj = f // tn
    tm = min(tm, s)

    def body(x_ref, gain_ref, wg_ref, wu_ref, wd_ref, y_ref, h_ref, g_ref, u_ref, acc_s):
        j = pl.program_id(1)

        @pl.when(j == 0)
        def _():
            h_ref[...] = _rms(x_ref[...], gain_ref[...]).astype(BF16)
            acc_s[...] = jnp.zeros_like(acc_s)

        hv = h_ref[...]
        g = _mm_raw(hv, wg_ref[...])
        u = _mm_raw(hv, wu_ref[...])
        g_ref[...] = g.astype(BF16)
        u_ref[...] = u.astype(BF16)
        acc_s[...] += _mm_raw(_silu_mul(g, u), wd_ref[...])

        @pl.when(j == nj - 1)
        def _():
            y_ref[...] = x_ref[...] + 0.5 * acc_s[...]

    row = pl.BlockSpec((tm, d), lambda i, j: (i, 0))
    tile = pl.BlockSpec((tm, tn), lambda i, j: (i, j))
    y, h, g, u = pl.pallas_call(
        body, name=f"{tag}_fwd", grid=(s // tm, nj),
        in_specs=[row, pl.BlockSpec((1, d), lambda i, j: (0, 0)),
                  pl.BlockSpec((None, d, tn), lambda i, j: (0, 0, j)), pl.BlockSpec((None, d, tn), lambda i, j: (1, 0, j)),
                  pl.BlockSpec((tn, d), lambda i, j: (j, 0))],
        out_specs=[row, row, tile, tile],
        out_shape=[SDS((s, d), F32), SDS((s, d), BF16), SDS((s, f), BF16), SDS((s, f), BF16)],
        scratch_shapes=[pltpu.VMEM((tm, d), F32)],
        compiler_params=_params("parallel", "arbitrary"),
    )(x, gain, wgu, wgu, wd)
    return y, (h, g, u)


def _ffn_bwd(dy, dyb, x, gain, wgu, wd, saved, tag):
    h, g, u = saved
    s, d = x.shape
    f = wd.shape[0]
    tn = FFN_TN

    def body(dy_ref, ht_ref, wg_ref, wu_ref, wd_ref, g_ref, u_ref, dh_ref, dwgu_ref, dwd_ref):
        @pl.when(pl.program_id(0) == 0)
        def _():
            dh_ref[...] = jnp.zeros_like(dh_ref)

        dyv = dy_ref[...]
        da = _mm_raw(dyv, wd_ref[...], False, True) * 0.5
        a, vjp = jax.vjp(_silu_mul, g_ref[...].astype(F32), u_ref[...].astype(F32))
        dg, du = vjp(da)
        dg = dg.astype(BF16)
        du = du.astype(BF16)
        dh_ref[...] += _mm_raw(dg, wg_ref[...], False, True) + _mm_raw(du, wu_ref[...], False, True)
        dwgu_ref[0] = _mm_raw(ht_ref[...], dg).astype(BF16)
        dwgu_ref[1] = _mm_raw(ht_ref[...], du).astype(BF16)
        dwd_ref[...] = (_mm_raw(a, dyv, True, False) * 0.5).astype(BF16)

    full = pl.BlockSpec((s, d), lambda j: (0, 0))
    tile = pl.BlockSpec((s, tn), lambda j: (0, j))
    wrow = pl.BlockSpec((tn, d), lambda j: (j, 0))
    dh, dwgu, dwd = pl.pallas_call(
        body, name=f"{tag}_bwd", grid=(f // tn,),
        in_specs=[full, pl.BlockSpec((d, s), lambda j: (0, 0)),
                  pl.BlockSpec((None, d, tn), lambda j: (0, 0, j)), pl.BlockSpec((None, d, tn), lambda j: (1, 0, j)),
                  wrow, tile, tile],
        out_specs=[full, pl.BlockSpec((2, d, tn), lambda j: (0, 0, j)), wrow],
        out_shape=[SDS((s, d), F32), SDS((2, d, f), BF16), SDS((f, d), BF16)],
        compiler_params=_params("arbitrary"),
    )(dyb, h.T, wgu, wgu, wd, g, u)
    dx, dxb, dgain = _rms_bwd(x, gain, dh, dy, tm=256, name=f"{tag}_drms")
    return dx, dxb, dgain, dwgu, dwd


def _loss_bwd(y, target, *, tm, name):
    s, d = y.shape

    def body(y_ref, t_ref, dy_ref, dyb_ref, l_ref):
        diff = y_ref[...] - t_ref[...]
        dy_ref[...] = diff * (1.0 / d)
        dyb_ref[...] = (diff * (1.0 / d)).astype(BF16)

        @pl.when(pl.program_id(0) == 0)
        def _():
            l_ref[...] = jnp.zeros_like(l_ref)

        l_ref[...] += 0.5 * jnp.sum(jnp.mean(diff * diff, axis=-1, keepdims=True), axis=0, keepdims=True)

    row = pl.BlockSpec((tm, d), lambda i: (i, 0))
    return pl.pallas_call(
        body, name=name, grid=(s // tm,), in_specs=[row, row],
        out_specs=[row, row, pl.BlockSpec((1, 1), lambda i: (0, 0))],
        out_shape=[SDS((s, d), F32), SDS((s, d), BF16), SDS((1, 1), F32)],
        compiler_params=_params("arbitrary"),
    )(y, target)


def _bucket_table():
    qi = np.arange(BLOCK)[:, None]
    kj = np.arange(2 * BLOCK)[None, :]
    dist = np.maximum(qi + BLOCK - kj, 0)
    max_exact = N_BUCKETS // 2
    d = np.maximum(dist, 1).astype(np.float32)
    large = max_exact + (np.log(d / np.float32(max_exact)) / np.float32(math.log(MAX_DISTANCE / max_exact))
                         * np.float32(N_BUCKETS - max_exact)).astype(np.int32)
    large = np.minimum(large, N_BUCKETS - 1)
    return np.where(dist < max_exact, dist, large).astype(np.int32)


SWA_STACK = SWA_GROUP * BLOCK


def _swa_valid(n):
    qi = lax.broadcasted_iota(jnp.int32, (SWA_STACK, 2 * BLOCK), 0) % BLOCK
    kj = lax.broadcasted_iota(jnp.int32, (SWA_STACK, 2 * BLOCK), 1)
    dist = qi + BLOCK - kj
    return (dist >= 0) & (dist < BLOCK) & ((kj >= BLOCK) | (n > 0))


def _swa_group(q, kb, vb, qg, kg, sink, bias, valid):
    qn = _rms(q, qg)
    kn = _rms(kb, kg)
    s = _mm(qn, kn, False, True) * (HEAD_DIM ** -0.5) + bias
    s = jnp.where(valid, s, -jnp.inf)
    m = lax.stop_gradient(jnp.maximum(jnp.max(s, axis=-1, keepdims=True), sink))
    p = jnp.exp(s - m)
    p = p / (jnp.sum(p, axis=-1, keepdims=True) + jnp.exp(sink - m))
    return _mm(p, vb)


def _swa_bias_table(rb_ref, bucket, bias_s):
    for h in range(SWA_HEADS):
        acc = jnp.zeros((BLOCK, 2 * BLOCK), F32)
        for b in range(N_BUCKETS):
            acc = jnp.where(bucket == b, rb_ref[b, h], acc)
        bias_s[h // SWA_GROUP, (h % SWA_GROUP) * BLOCK:(h % SWA_GROUP + 1) * BLOCK, :] = acc


def _swa_stack(ref, g):
    return jnp.concatenate([ref[:, (g * SWA_GROUP + hh) * HEAD_DIM:(g * SWA_GROUP + hh + 1) * HEAD_DIM]
                            for hh in range(SWA_GROUP)], axis=0)


def _swa_unstack(ref, g, stacked):
    for hh in range(SWA_GROUP):
        h = g * SWA_GROUP + hh
        ref[:, h * HEAD_DIM:(h + 1) * HEAD_DIM] = stacked[hh * BLOCK:(hh + 1) * BLOCK]


def _swa_sink_column(sink_ref, g):
    head = lax.broadcasted_iota(jnp.int32, (SWA_STACK, 1), 0) // BLOCK
    col = jnp.zeros((SWA_STACK, 1), F32)
    for hh in range(SWA_GROUP):
        col = jnp.where(head == hh, sink_ref[g * SWA_GROUP + hh], col)
    return col


def _swa_band(kvp_ref, kvc_ref, g):
    lo = g * HEAD_DIM
    kb = jnp.concatenate([kvp_ref[:, lo:lo + HEAD_DIM], kvc_ref[:, lo:lo + HEAD_DIM]], axis=0)
    lo += SWA_KV_W
    vb = jnp.concatenate([kvp_ref[:, lo:lo + HEAD_DIM], kvc_ref[:, lo:lo + HEAD_DIM]], axis=0)
    return kb, vb


def _swa_specs(order):
    kvc = COL_SKV // (2 * SWA_KV_W)
    return [
        pl.BlockSpec((BLOCK, SWA_Q_W), lambda t: (order(t), 0)),
        pl.BlockSpec((BLOCK, 2 * SWA_KV_W), lambda t: (jnp.maximum(order(t) - 1, 0), kvc)),
        pl.BlockSpec((BLOCK, 2 * SWA_KV_W), lambda t: (order(t), kvc)),
        pl.BlockSpec((1, HEAD_DIM), lambda t: (0, 0)),
        pl.BlockSpec((1, HEAD_DIM), lambda t: (0, 0)),
        pl.BlockSpec(memory_space=pltpu.SMEM),
        pl.BlockSpec(memory_space=pltpu.SMEM),
        pl.BlockSpec((BLOCK, 2 * BLOCK), lambda t: (0, 0)),
    ]


def _swa_fwd(p, qg, kg, sinks, rel_bias, *, name):
    s = p.shape[0]
    nb = s // BLOCK

    def body(q_ref, kvp_ref, kvc_ref, qg_ref, kg_ref, sink_ref, rb_ref, bucket_ref, y_ref, bias_s):
        n = pl.program_id(0)

        @pl.when(n == 0)
        def _():
            _swa_bias_table(rb_ref, bucket_ref[...], bias_s)

        valid = _swa_valid(n)
        for g in range(SWA_KV_HEADS):
            kb, vb = _swa_band(kvp_ref, kvc_ref, g)
            out = _swa_group(_swa_stack(q_ref, g), kb, vb, qg_ref[...], kg_ref[...], _swa_sink_column(sink_ref, g),
                             bias_s[g], valid)
            _swa_unstack(y_ref, g, out)

    return pl.pallas_call(
        body, name=name, grid=(nb,), in_specs=_swa_specs(lambda t: t),
        out_specs=pl.BlockSpec((BLOCK, SWA_Q_W), lambda t: (t, 0)), out_shape=SDS((s, SWA_Q_W), F32),
        scratch_shapes=[pltpu.VMEM((SWA_KV_HEADS, SWA_STACK, 2 * BLOCK), F32)],
        compiler_params=_params("arbitrary"),
    )(p, p, p, qg, kg, sinks, rel_bias, jnp.asarray(_bucket_table()))


def _swa_bwd(p, qg, kg, sinks, rel_bias, dy_all, *, name):
    s = p.shape[0]
    nb = s // BLOCK

    def body(q_ref, kvp_ref, kvc_ref, qg_ref, kg_ref, sink_ref, rb_ref, bucket_ref, dy_ref,
             dq_ref, dkv_ref, dqg_ref, dkg_ref, dsink_ref, drb_ref, bias_s, dbias_s, carry_s):
        t = pl.program_id(0)
        n = nb - 1 - t

        @pl.when(t == 0)
        def _():
            _swa_bias_table(rb_ref, bucket_ref[...], bias_s)
            dbias_s[...] = jnp.zeros_like(dbias_s)
            carry_s[...] = jnp.zeros_like(carry_s)
            dqg_ref[...] = jnp.zeros_like(dqg_ref)
            dkg_ref[...] = jnp.zeros_like(dkg_ref)
            dsink_ref[...] = jnp.zeros_like(dsink_ref)
            drb_ref[...] = jnp.zeros_like(drb_ref)

        valid = _swa_valid(n)
        lane = lax.broadcasted_iota(jnp.int32, (1, BLOCK), 1)
        dqg = jnp.zeros((1, HEAD_DIM), F32)
        dkg = jnp.zeros((1, HEAD_DIM), F32)
        dsink_vec = jnp.zeros((1, BLOCK), F32)
        for g in range(SWA_KV_HEADS):
            kb, vb = _swa_band(kvp_ref, kvc_ref, g)
            _, vjp = jax.vjp(functools.partial(_swa_group, valid=valid), _swa_stack(q_ref, g), kb, vb, qg_ref[...],
                             kg_ref[...], _swa_sink_column(sink_ref, g), bias_s[g])
            dq, dkb, dvb, dqg_g, dkg_g, dsink_col, dbias = vjp(_swa_stack(dy_ref, g))
            _swa_unstack(dq_ref, g, dq)
            dqg += dqg_g
            dkg += dkg_g
            dbias_s[g] += dbias
            for hh in range(SWA_GROUP):
                dsink_h = jnp.sum(dsink_col[hh * BLOCK:(hh + 1) * BLOCK], axis=0, keepdims=True)
                dsink_vec += jnp.where(lane == g * SWA_GROUP + hh, dsink_h, 0.0)
            lo = g * HEAD_DIM
            dkv_ref[:, lo:lo + HEAD_DIM] = dkb[BLOCK:] + carry_s[g]
            carry_s[g] = dkb[:BLOCK]
            lo += SWA_KV_W
            dkv_ref[:, lo:lo + HEAD_DIM] = dvb[BLOCK:] + carry_s[SWA_KV_HEADS + g]
            carry_s[SWA_KV_HEADS + g] = dvb[:BLOCK]
        dqg_ref[...] += dqg
        dkg_ref[...] += dkg
        dsink_ref[...] += dsink_vec

        @pl.when(t == nb - 1)
        def _():
            bucket = bucket_ref[...]
            row = lax.broadcasted_iota(jnp.int32, (N_BUCKETS, BLOCK), 0)
            col = lax.broadcasted_iota(jnp.int32, (N_BUCKETS, BLOCK), 1)
            acc = jnp.zeros((N_BUCKETS, BLOCK), F32)
            for h in range(SWA_HEADS):
                dbias = dbias_s[h // SWA_GROUP, (h % SWA_GROUP) * BLOCK:(h % SWA_GROUP + 1) * BLOCK, :]
                for b in range(N_BUCKETS):
                    part = jnp.sum(jnp.where(bucket == b, dbias, 0.0), axis=1, keepdims=True)
                    val = jnp.sum(part, axis=0, keepdims=True)
                    acc = acc + jnp.where((row == b) & (col == h), val, 0.0)
            drb_ref[...] = acc

    order = lambda t: nb - 1 - t
    vec = pl.BlockSpec((1, HEAD_DIM), lambda t: (0, 0))
    return pl.pallas_call(
        body, name=name, grid=(nb,),
        in_specs=_swa_specs(order) + [pl.BlockSpec((BLOCK, SWA_Q_W), lambda t: (order(t), 0))],
        out_specs=[pl.BlockSpec((BLOCK, SWA_Q_W), lambda t: (order(t), 0)),
                   pl.BlockSpec((BLOCK, 2 * SWA_KV_W), lambda t: (order(t), 0)),
                   vec, vec, pl.BlockSpec((1, BLOCK), lambda t: (0, 0)),
                   pl.BlockSpec((N_BUCKETS, BLOCK), lambda t: (0, 0))],
        out_shape=[SDS((s, SWA_Q_W), F32), SDS((s, 2 * SWA_KV_W), F32), SDS((1, HEAD_DIM), F32),
                   SDS((1, HEAD_DIM), F32), SDS((1, BLOCK), F32), SDS((N_BUCKETS, BLOCK), F32)],
        scratch_shapes=[pltpu.VMEM((SWA_KV_HEADS, SWA_STACK, 2 * BLOCK), F32),
                        pltpu.VMEM((SWA_KV_HEADS, SWA_STACK, 2 * BLOCK), F32),
                        pltpu.VMEM((2 * SWA_KV_HEADS, BLOCK, HEAD_DIM), F32)],
        compiler_params=_params("arbitrary"),
    )(p, p, p, qg, kg, sinks, rel_bias, jnp.asarray(_bucket_table()), dy_all)


def _mem_head(q, k, v, qg, kg):
    qn = _rms(q, qg)
    kn = _rms(k, kg)
    s = _mm(qn, kn, False, True) * (HEAD_DIM ** -0.5)
    m = lax.stop_gradient(jnp.max(s, axis=-1, keepdims=True))
    e = jnp.exp(s - m)
    return _mm(e / jnp.sum(e, axis=-1, keepdims=True), v)


def _mem_fwd(p, kv, qg, kg, *, tq, name):
    s = p.shape[0]
    m = kv.shape[0]

    def body(q_ref, kv_ref, qg_ref, kg_ref, y_ref):
        for h in range(MEM_HEADS):
            cols = slice(h * HEAD_DIM, (h + 1) * HEAD_DIM)
            vcols = slice(MEM_Q_W + h * HEAD_DIM, MEM_Q_W + (h + 1) * HEAD_DIM)
            y_ref[:, cols] = _mem_head(q_ref[:, cols], kv_ref[:, cols], kv_ref[:, vcols], qg_ref[...], kg_ref[...])

    vec = pl.BlockSpec((1, HEAD_DIM), lambda t: (0, 0))
    return pl.pallas_call(
        body, name=name, grid=(s // tq,),
        in_specs=[pl.BlockSpec((tq, MEM_Q_W), lambda t: (t, COL_MQ // MEM_Q_W)),
                  pl.BlockSpec((m, 2 * MEM_Q_W), lambda t: (0, 0)), vec, vec],
        out_specs=pl.BlockSpec((tq, MEM_Q_W), lambda t: (t, 0)), out_shape=SDS((s, MEM_Q_W), F32),
        compiler_params=_params("parallel"),
    )(p, kv, qg, kg)


def _mem_bwd(p, kv, qg, kg, dy_all, *, tq, name):
    s = p.shape[0]
    m = kv.shape[0]

    def body(q_ref, kv_ref, qg_ref, kg_ref, dy_ref, dq_ref, dkv_ref, dqg_ref, dkg_ref):
        @pl.when(pl.program_id(0) == 0)
        def _():
            dkv_ref[...] = jnp.zeros_like(dkv_ref)
            dqg_ref[...] = jnp.zeros_like(dqg_ref)
            dkg_ref[...] = jnp.zeros_like(dkg_ref)

        dqg = jnp.zeros((1, HEAD_DIM), F32)
        dkg = jnp.zeros((1, HEAD_DIM), F32)
        for h in range(MEM_HEADS):
            cols = slice(h * HEAD_DIM, (h + 1) * HEAD_DIM)
            vcols = slice(MEM_Q_W + h * HEAD_DIM, MEM_Q_W + (h + 1) * HEAD_DIM)
            _, vjp = jax.vjp(_mem_head, q_ref[:, cols], kv_ref[:, cols], kv_ref[:, vcols], qg_ref[...], kg_ref[...])
            dq, dk, dv, dqg_h, dkg_h = vjp(dy_ref[:, cols])
            dq_ref[:, cols] = dq
            dkv_ref[:, cols] += dk
            dkv_ref[:, vcols] += dv
            dqg += dqg_h
            dkg += dkg_h
        dqg_ref[...] += dqg
        dkg_ref[...] += dkg

    vec = pl.BlockSpec((1, HEAD_DIM), lambda t: (0, 0))
    full = pl.BlockSpec((m, 2 * MEM_Q_W), lambda t: (0, 0))
    dy_col = (SWA_Q_W + GLA_V_W) // MEM_Q_W
    return pl.pallas_call(
        body, name=name, grid=(s // tq,),
        in_specs=[pl.BlockSpec((tq, MEM_Q_W), lambda t: (t, COL_MQ // MEM_Q_W)), full, vec, vec,
                  pl.BlockSpec((tq, MEM_Q_W), lambda t: (t, dy_col))],
        out_specs=[pl.BlockSpec((tq, MEM_Q_W), lambda t: (t, 0)), full, vec, vec],
        out_shape=[SDS((s, MEM_Q_W), F32), SDS((m, 2 * MEM_Q_W), F32), SDS((1, HEAD_DIM), F32), SDS((1, HEAD_DIM), F32)],
        compiler_params=_params("arbitrary"),
    )(p, kv, qg, kg, dy_all)


GLA_ROWS = 256


def _gla_consts():
    c, h = GLA_CHUNK, GLA_HEADS
    i2 = lax.broadcasted_iota(jnp.int32, (c, c), 0)
    j2 = lax.broadcasted_iota(jnp.int32, (c, c), 1)
    slab_q = lax.broadcasted_iota(jnp.int32, (h, c, GLA_QK_W), 0)
    lane_q = lax.broadcasted_iota(jnp.int32, (h, c, GLA_QK_W), 2)
    row_a = lax.broadcasted_iota(jnp.int32, (h * c, c), 0)
    col_a = lax.broadcasted_iota(jnp.int32, (h * c, c), 1)
    slab_o = lax.broadcasted_iota(jnp.int32, (h, c, GLA_V_W), 0)
    lane_o = lax.broadcasted_iota(jnp.int32, (h, c, GLA_V_W), 2)
    row_s = lax.broadcasted_iota(jnp.int32, (GLA_V_W, GLA_QK_W), 0)
    col_s = lax.broadcasted_iota(jnp.int32, (GLA_V_W, GLA_QK_W), 1)
    return dict(
        ltri=(j2 <= i2).astype(F32),
        m_q=(slab_q == lane_q // GLA_DK).astype(F32),
        causal=col_a <= row_a % c,
        m_o=(slab_o == lane_o // GLA_DV).astype(F32),
        m_s=(row_s // GLA_DV == col_s // GLA_DK).astype(F32),
    )


def _gla_chunk(q, k, v, z, bg, st, c):
    h, n = GLA_HEADS, GLA_CHUNK
    la = _log_sigmoid(z + bg) * (1.0 / GLA_TAU)
    b = _mmf(c["ltri"], la)
    bl = jnp.sum(la, axis=0, keepdims=True)
    qs = q * (GLA_DK ** -0.5)
    kt = k * jnp.exp(bl - b)
    qt = qs * jnp.exp(b - bl)
    qe = qs * jnp.exp(b)
    q_stack = (jnp.broadcast_to(qt[None], (h, n, GLA_QK_W)) * c["m_q"]).reshape(h * n, GLA_QK_W)
    a = jnp.where(c["causal"], _mmf(q_stack, kt, False, True), 0.0)
    o_stack = _mm(a, v)
    o_intra = jnp.sum(o_stack.reshape(h, n, GLA_V_W) * c["m_o"], axis=0)
    o_inter = _mm(qe, st, False, True)
    st_next = st * jnp.exp(bl) + _mm(v, kt, True, False) * c["m_s"]
    return o_intra + o_inter, st_next


def _gla_post(o, gg, gain, g64):
    ms = _mmf(o * o, g64) * (1.0 / GLA_DV)
    return o * lax.rsqrt(ms + EPS) * gain * jax.nn.silu(gg)


def _gla_g64():
    r = lax.broadcasted_iota(jnp.int32, (GLA_V_W, GLA_V_W), 0)
    c = lax.broadcasted_iota(jnp.int32, (GLA_V_W, GLA_V_W), 1)
    return (r // GLA_DV == c // GLA_DV).astype(F32)


def _gla_in_specs(order):
    r = GLA_ROWS
    return [
        pl.BlockSpec((r, GLA_QK_W), lambda t: (order(t), COL_GQ // GLA_QK_W)),
        pl.BlockSpec((r, GLA_QK_W), lambda t: (order(t), COL_GK // GLA_QK_W)),
        pl.BlockSpec((r, GLA_V_W), lambda t: (order(t), COL_GV // GLA_V_W)),
        pl.BlockSpec((r, GLA_V_W), lambda t: (order(t), COL_GG // GLA_V_W)),
        pl.BlockSpec((r, GLA_QK_W), lambda t: (order(t), 0)),
        pl.BlockSpec((1, GLA_QK_W), lambda t: (0, 0)),
        pl.BlockSpec((1, GLA_V_W), lambda t: (0, 0)),
    ]


def _gla_fwd(p, z, bg, gain, *, name):
    s = p.shape[0]
    r = GLA_ROWS
    cps = r // GLA_CHUNK

    def body(q_ref, k_ref, v_ref, gg_ref, z_ref, bg_ref, gain_ref, y_ref, oraw_ref, stsave_ref, st_s):
        @pl.when(pl.program_id(0) == 0)
        def _():
            st_s[...] = jnp.zeros_like(st_s)

        c = _gla_consts()
        st = st_s[...]
        for ci in range(cps):
            rows = slice(ci * GLA_CHUNK, (ci + 1) * GLA_CHUNK)
            stsave_ref[ci] = st
            o, st = _gla_chunk(q_ref[rows, :], k_ref[rows, :], v_ref[rows, :], z_ref[rows, :], bg_ref[...], st, c)
            oraw_ref[rows, :] = o
        st_s[...] = st
        y_ref[...] = _gla_post(oraw_ref[...], gg_ref[...], gain_ref[...], _gla_g64())

    rowv = pl.BlockSpec((r, GLA_V_W), lambda t: (t, 0))
    return pl.pallas_call(
        body, name=name, grid=(s // r,), in_specs=_gla_in_specs(lambda t: t),
        out_specs=[rowv, rowv, pl.BlockSpec((cps, GLA_V_W, GLA_QK_W), lambda t: (t, 0, 0))],
        out_shape=[SDS((s, GLA_V_W), F32), SDS((s, GLA_V_W), F32), SDS((s // GLA_CHUNK, GLA_V_W, GLA_QK_W), F32)],
        scratch_shapes=[pltpu.VMEM((GLA_V_W, GLA_QK_W), F32)],
        compiler_params=_params("arbitrary"),
    )(p, p, p, p, z, bg, gain)


def _gla_bwd(p, z, bg, gain, oraw, stsave, dy_all, *, name):
    s = p.shape[0]
    r = GLA_ROWS
    cps = r // GLA_CHUNK
    nsteps = s // r
    w_qkvg = 2 * GLA_QK_W + 2 * GLA_V_W

    def body(q_ref, k_ref, v_ref, gg_ref, z_ref, bg_ref, gain_ref, oraw_ref, stsave_ref, dy_ref,
             dqkvg_ref, dz_ref, dbg_ref, dgain_ref, dst_s):
        @pl.when(pl.program_id(0) == 0)
        def _():
            dst_s[...] = jnp.zeros_like(dst_s)
            dbg_ref[...] = jnp.zeros_like(dbg_ref)
            dgain_ref[...] = jnp.zeros_like(dgain_ref)

        c = _gla_consts()
        _, vjp = jax.vjp(functools.partial(_gla_post, g64=_gla_g64()), oraw_ref[...], gg_ref[...], gain_ref[...])
        do, dgg, dgain = vjp(dy_ref[...])
        dqkvg_ref[:, 2 * GLA_QK_W + GLA_V_W:] = dgg
        dgain_ref[...] += dgain
        dst = dst_s[...]
        dbg = jnp.zeros((1, GLA_QK_W), F32)
        for ci in reversed(range(cps)):
            rows = slice(ci * GLA_CHUNK, (ci + 1) * GLA_CHUNK)
            _, vjp = jax.vjp(functools.partial(_gla_chunk, c=c), q_ref[rows, :], k_ref[rows, :], v_ref[rows, :],
                             z_ref[rows, :], bg_ref[...], stsave_ref[ci])
            dq, dk, dv, dz, dbg_c, dst = vjp((do[rows, :], dst))
            dqkvg_ref[rows, 0:GLA_QK_W] = dq
            dqkvg_ref[rows, GLA_QK_W:2 * GLA_QK_W] = dk
            dqkvg_ref[rows, 2 * GLA_QK_W:2 * GLA_QK_W + GLA_V_W] = dv
            dz_ref[rows, :] = dz
            dbg += dbg_c
        dst_s[...] = dst
        dbg_ref[...] += dbg

    order = lambda t: nsteps - 1 - t
    rowv = pl.BlockSpec((r, GLA_V_W), lambda t: (order(t), 0))
    return pl.pallas_call(
        body, name=name, grid=(nsteps,),
        in_specs=_gla_in_specs(order) + [
            rowv, pl.BlockSpec((cps, GLA_V_W, GLA_QK_W), lambda t: (order(t), 0, 0)),
            pl.BlockSpec((r, GLA_V_W), lambda t: (order(t), SWA_Q_W // GLA_V_W))],
        out_specs=[pl.BlockSpec((r, w_qkvg), lambda t: (order(t), 0)), pl.BlockSpec((r, GLA_QK_W), lambda t: (order(t), 0)),
                   pl.BlockSpec((1, GLA_QK_W), lambda t: (0, 0)), pl.BlockSpec((1, GLA_V_W), lambda t: (0, 0))],
        out_shape=[SDS((s, w_qkvg), F32), SDS((s, GLA_QK_W), F32), SDS((1, GLA_QK_W), F32), SDS((1, GLA_V_W), F32)],
        scratch_shapes=[pltpu.VMEM((GLA_V_W, GLA_QK_W), F32)],
        compiler_params=_params("arbitrary"),
    )(p, p, p, p, z, bg, gain, oraw, stsave, dy_all)


def _local_step(x, mem, target, small, big, on_grads):
    g1, gmix, gmem, g2, sqg, skg, sinks, rel_bias, wgu, bg, gla_gain, mqg, mkg = small
    wgu1, wd1, win_p, wkv, wout, gather_ffn2 = big
    wgu_pad = jnp.zeros((GLA_QK_W, GLA_QK_W), BF16).at[:GLA_RANK].set(wgu.astype(BF16))
    gain256 = jnp.tile(gla_gain, (1, GLA_HEADS))

    x1, saved1 = _ffn_fwd(x, g1, wgu1, wd1, "ffn1")
    wgu2, wd2 = gather_ffn2(x1)
    h = _rms_fwd(x1, gmix, tm=256, name="mix_rms")
    p = _matmul(h, win_p, tm=512, tn=384, name="mix_in")
    hm = _rms_fwd(mem, gmem, tm=256, name="mem_rms")
    kv = _matmul(hm, wkv, tm=256, tn=512, name="mem_kv")
    p_glr = p[:, COL_GLR:]
    z = _matmul(p_glr, wgu_pad, tm=512, tn=GLA_QK_W, name="gla_gate")
    y_swa = _swa_fwd(p, sqg, skg, sinks, rel_bias, name="swa_fwd")
    y_gla, oraw, stsave = _gla_fwd(p, z, bg, gain256, name="gla_fwd")
    y_mem = _mem_fwd(p, kv, mqg, mkg, tq=512, name="mem_fwd")
    x2 = _matmul([y_swa, y_gla, y_mem], wout, b_blocks=[0, 2, 3], tm=512, tn=512, res=x1, name="mix_out")
    x3, saved2 = _ffn_fwd(x2, g2, wgu2, wd2, "ffn2")

    dy, dyb, loss = _loss_bwd(x3, target, tm=256, name="loss")
    dx2, dx2b, dg2, dwgu2, dwd2 = _ffn_bwd(dy, dyb, x2, g2, wgu2, wd2, saved2, "ffn2")
    dx2b = on_grads("ffn2", (dwgu2, dwd2), dx2b)
    dy_all = _matmul(dx2b, wout, tb=True, tm=512, tn=512, name="mix_dy")
    dwout = _matmul(jnp.concatenate([y_swa, y_gla, y_mem], axis=1), dx2b, ta=True, tm=512, tn=512, out_dtype=BF16,
                    name="mix_dw_out")
    dq_swa, dkv_swa, dsqg, dskg, dsink, drb = _swa_bwd(p, sqg, skg, sinks, rel_bias, dy_all, name="swa_bwd")
    dqkvg, dz, dbg, dgain256 = _gla_bwd(p, z, bg, gain256, oraw, stsave, dy_all, name="gla_bwd")
    dmq, dkv_mem, dmqg, dmkg = _mem_bwd(p, kv, mqg, mkg, dy_all, tq=512, name="mem_bwd")
    dglr = _matmul(dz, wgu_pad, tb=True, tm=512, tn=GLA_QK_W, name="gla_gate_dx")
    dwgu_pad = _matmul(p_glr, dz, ta=True, tm=GLA_QK_W, tn=GLA_QK_W, name="gla_gate_dw")
    dp = jnp.concatenate([dq_swa, dkv_swa, dqkvg, dmq, dglr], axis=1)
    dh = _matmul(dp, win_p, tb=True, tm=512, tn=512, name="mix_dh")
    dwin_p = _matmul(h, dp, ta=True, tm=512, tn=384, out_dtype=BF16, name="mix_dw_in")
    dx1, dx1b, dgmix = _rms_bwd(x1, gmix, dh, dx2, tm=256, name="mix_drms")
    dwkv = _matmul(hm, dkv_mem, ta=True, tm=512, tn=512, out_dtype=BF16, name="mem_dw_kv")
    dx1b = on_grads("mix", (dwin_p, dwkv, dwout), dx1b)
    dhm = _matmul(dkv_mem, wkv, tb=True, tm=256, tn=512, name="mem_dh")
    _, _, dgmem = _rms_bwd(mem, gmem, dhm, None, tm=256, name="mem_drms")
    dx, _, dg1, dwgu1, dwd1 = _ffn_bwd(dx1, dx1b, x, g1, wgu1, wd1, saved1, "ffn1")

    dgla_gain = dgain256.reshape(GLA_HEADS, GLA_DV).sum(axis=0, keepdims=True)
    dsmall = (dg1, dgmix, dgmem, dg2, dsqg, dskg, dsink[0, :SWA_HEADS], drb[:, :SWA_HEADS], dwgu_pad[:GLA_RANK], dbg,
              dgla_gain, dmqg, dmkg)
    on_grads("ffn1", (dwgu1, dwd1), None, small=list(dsmall) + [loss])
    return dx


def _mesh_place():
    x, y, c = lax.axis_index("x"), lax.axis_index("y"), lax.axis_index("c")
    other_chips = [(1 - x, y), (x, 1 - y), (1 - x, 1 - y)]
    return x, y, c, other_chips


def _handshake(peers):
    barrier = pltpu.get_barrier_semaphore()
    for peer in peers:
        pl.semaphore_signal(barrier, inc=1, device_id=peer, device_id_type=MESH)
    pl.semaphore_wait(barrier, len(peers))


def _sequencer_call(body, operands, out_shapes, sems, *, name, collective_id):
    return pl.kernel(
        body, name=name, out_type=out_shapes, mesh=plsc.ScalarSubcoreMesh(axis_name="sequencer", num_cores=1),
        scratch_types=sems, compiler_params=pltpu.CompilerParams(collective_id=collective_id),
    )(*operands)


def _window(ref, kind, slot, shape):
    if kind == "col":
        return ref.at[:, :, pl.ds(pl.multiple_of(slot * shape[-1], 128), shape[-1])]
    if kind == "row":
        return ref.at[pl.ds(pl.multiple_of(slot * shape[0], 8), shape[0])]
    return ref.at[slot]


def _gathered(shape, kind):
    if kind == "col":
        return tuple(shape[:-1]) + (N_DEV * shape[-1],)
    if kind == "row":
        return (N_DEV * shape[0],) + tuple(shape[1:])
    return (N_DEV,) + tuple(shape)


def _all_gather(shards, kinds, *, name, collective_id):
    nt = len(shards)

    def body(*refs):
        x_refs, o_refs = refs[:nt], refs[nt:2 * nt]
        send_sems, recv_sems, local_sems = refs[2 * nt:]
        x, y, c, chips = _mesh_place()
        me, sibling = (x, y, c), (x, y, 1 - c)
        _handshake([sibling] + [(*chip, c) for chip in chips])

        def copy(k, t, block, to, from_shard=False):
            bx, by, bc = block
            rows = _window(o_refs[t], kinds[t], 4 * bx + 2 * by + bc, shards[t].shape)
            return pltpu.make_async_remote_copy(
                src_ref=x_refs[t] if from_shard else rows, dst_ref=rows,
                send_sem=send_sems.at[k, t], recv_sem=recv_sems.at[k, t], device_id=to, device_id_type=MESH)

        mine = [pltpu.make_async_copy(x_refs[t], _window(o_refs[t], kinds[t], 4 * x + 2 * y + c, shards[t].shape),
                                      local_sems.at[t]) for t in range(nt)]
        for cp in mine:
            cp.start()
        first = [copy(0, t, me, sibling, True) for t in range(nt)]
        first += [copy(1 + j, t, me, (*chip, c), True) for j, chip in enumerate(chips) for t in range(nt)]
        for cp in first:
            cp.start()
        passed = []
        for j, chip in enumerate(chips):
            for t in range(nt):
                copy(1 + j, t, (*chip, c), me).wait_recv()
                fwd = copy(4 + j, t, (*chip, c), sibling)
                fwd.start()
                passed.append(fwd)
        for t in range(nt):
            copy(0, t, sibling, me).wait_recv()
        for j, chip in enumerate(chips):
            for t in range(nt):
                copy(4 + j, t, (*chip, 1 - c), me).wait_recv()
        for cp in first + passed:
            cp.wait_send()
        for cp in mine:
            cp.wait()

    return _sequencer_call(
        body, shards, [SDS(_gathered(s.shape, k), s.dtype) for s, k in zip(shards, kinds)],
        [pltpu.SemaphoreType.DMA((7, nt)), pltpu.SemaphoreType.DMA((7, nt)), pltpu.SemaphoreType.DMA((nt,))],
        name=name, collective_id=collective_id)


def _part_shape(shape, kind):
    if kind == "col":
        return tuple(shape[:-1]) + (shape[-1] // N_DEV,)
    if kind == "row":
        return (shape[0] // N_DEV,) + tuple(shape[1:])
    return tuple(shape[2:])


def _pair_exchange(grads, kinds, *, name, collective_id):
    nt = len(grads)
    part = [_part_shape(g.shape, k) for g, k in zip(grads, kinds)]

    def body(*refs):
        g_refs, o_refs = refs[:nt], refs[nt:2 * nt]
        send_sems, recv_sems = refs[2 * nt:]
        x, y, c, _ = _mesh_place()
        _handshake([(x, y, 1 - c)])
        copies = []
        for t in range(nt):
            for xy in range(4):
                src = g_refs[t].at[1 - c, xy] if kinds[t] == "stack" else _window(g_refs[t], kinds[t], 2 * xy + 1 - c, part[t])
                copies.append(pltpu.make_async_remote_copy(
                    src_ref=src, dst_ref=o_refs[t].at[xy], send_sem=send_sems.at[xy, t], recv_sem=recv_sems.at[xy, t],
                    device_id=(x, y, 1 - c), device_id_type=MESH))
        for cp in copies:
            cp.start()
        for cp in copies:
            cp.wait()

    return _sequencer_call(
        body, grads, [SDS((4,) + p, g.dtype) for p, g in zip(part, grads)],
        [pltpu.SemaphoreType.DMA((4, nt)), pltpu.SemaphoreType.DMA((4, nt))], name=name, collective_id=collective_id)


def _chip_exchange(parts, small, *, name, collective_id):
    nt = len(parts)
    if small is None:
        def body_plain(*refs):
            s_refs, o_refs = refs[:nt], refs[nt:2 * nt]
            send_sems, recv_sems = refs[2 * nt:]
            x, y, c, chips = _mesh_place()
            _handshake([(*chip, c) for chip in chips])
            copies = [pltpu.make_async_remote_copy(
                src_ref=s_refs[t].at[2 * chip[0] + chip[1]], dst_ref=o_refs[t].at[j],
                send_sem=send_sems.at[j, t], recv_sem=recv_sems.at[j, t], device_id=(*chip, c), device_id_type=MESH)
                for j, chip in enumerate(chips) for t in range(nt)]
            for cp in copies:
                cp.start()
            for cp in copies:
                cp.wait()

        return _sequencer_call(
            body_plain, parts, [SDS((3,) + s.shape[1:], s.dtype) for s in parts],
            [pltpu.SemaphoreType.DMA((3, nt)), pltpu.SemaphoreType.DMA((3, nt))], name=name, collective_id=collective_id)

    def body(*refs):
        s_refs, small_ref = refs[:nt], refs[nt]
        o_refs, small_all = refs[nt + 1:2 * nt + 1], refs[2 * nt + 1]
        send_sems, recv_sems, small_send, small_recv, local_sem = refs[2 * nt + 2:]
        x, y, c, chips = _mesh_place()
        _handshake([(px, py, pc) for px in (x, 1 - x) for py in (y, 1 - y) for pc in (c, 1 - c)][1:])

        def copy(j, t, chip):
            return pltpu.make_async_remote_copy(
                src_ref=s_refs[t].at[2 * chip[0] + chip[1]], dst_ref=o_refs[t].at[j],
                send_sem=send_sems.at[j, t], recv_sem=recv_sems.at[j, t], device_id=(*chip, c), device_id_type=MESH)

        flips = [(fx, fy, fc) for fx in (0, 1) for fy in (0, 1) for fc in (0, 1)][1:]

        def small_copy(k):
            fx, fy, fc = flips[k]
            to = (x ^ fx if fx else x, y ^ fy if fy else y, c ^ fc if fc else c)
            rows = small_all.at[4 * x + 2 * y + c]
            return pltpu.make_async_remote_copy(
                src_ref=small_ref, dst_ref=rows, send_sem=small_send.at[k], recv_sem=small_recv.at[k],
                device_id=to, device_id_type=MESH)

        own = pltpu.make_async_copy(small_ref, small_all.at[4 * x + 2 * y + c], local_sem)
        own.start()
        copies = [copy(j, t, chip) for j, chip in enumerate(chips) for t in range(nt)]
        smalls = [small_copy(k) for k in range(7)]
        for cp in smalls + copies:
            cp.start()
        for cp in smalls + copies:
            cp.wait()
        own.wait()

    return _sequencer_call(
        body, list(parts) + [small],
        [SDS((3,) + s.shape[1:], s.dtype) for s in parts] + [SDS((N_DEV,) + small.shape, small.dtype)],
        [pltpu.SemaphoreType.DMA((3, nt)), pltpu.SemaphoreType.DMA((3, nt)),
         pltpu.SemaphoreType.DMA((7,)), pltpu.SemaphoreType.DMA((7,)), pltpu.SemaphoreType.DMA],
        name=name, collective_id=collective_id)


def _pair_sum(grad, theirs, kind, c, *, name):
    if kind == "col":
        n, r, l = theirs.shape[1:]
        mine_spec = pl.BlockSpec((None, r, l), lambda xy, m, c_ref: (m, 0, 2 * xy + c_ref[0]))
    elif kind == "row":
        (r, l), n = theirs.shape[1:], 1
        theirs = theirs[:, None]
        mine_spec = pl.BlockSpec((r, l), lambda xy, m, c_ref: (2 * xy + c_ref[0], 0))
    else:
        r, l = theirs.shape[-2:]
        n = theirs.size // (4 * r * l)
        theirs = theirs.reshape(4, n, r, l)
        grad = grad.reshape(2, 4, n, r, l)
        mine_spec = pl.BlockSpec((None, None, None, r, l), lambda xy, m, c_ref: (c_ref[0], xy, m, 0, 0))

    def body(c_ref, a_ref, b_ref, o_ref):
        o_ref[...] = (a_ref[...].astype(F32) + b_ref[...].astype(F32)).astype(BF16)

    part = pl.BlockSpec((None, None, r, l), lambda xy, m, c_ref: (xy, m, 0, 0))
    return pl.pallas_call(
        body, name=name,
        grid_spec=pltpu.PrefetchScalarGridSpec(num_scalar_prefetch=1, grid=(4, n), in_specs=[mine_spec, part], out_specs=part),
        out_shape=SDS((4, n, r, l), BF16), compiler_params=_params("parallel", "parallel"),
    )(c, grad, theirs)


def _adamw(w, g, m, v):
    m = ADAM_B1 * m + (1.0 - ADAM_B1) * g
    v = ADAM_B2 * v + (1.0 - ADAM_B2) * jnp.square(g)
    m_hat = m / (1.0 - ADAM_B1 ** ADAM_STEP)
    v_hat = v / (1.0 - ADAM_B2 ** ADAM_STEP)
    delta = -ADAM_LR * (m_hat / (jnp.sqrt(v_hat) + ADAM_EPS) + ADAM_WD * w)
    return delta, m, v


def _adam_big(own, others, mat, xy, w, m, v, *, tr, name):
    _, r, l = w.shape
    lp = own.shape[-1]

    def body(xy_ref, own_ref, oth_ref, w_ref, m_ref, v_ref, g_out, d_out, m_out, v_out):
        g = own_ref[0, 0].astype(F32)
        for j in range(3):
            g = g + oth_ref[j, 0].astype(F32)
        g = g[:, :l]
        delta, m_new, v_new = _adamw(w_ref[0], g, m_ref[0], v_ref[0])
        g_out[0] = g
        d_out[0] = delta
        m_out[0] = m_new
        v_out[0] = v_new

    blk = pl.BlockSpec((1, tr, l), lambda i, xy_ref: (0, i, 0))
    return pl.pallas_call(
        body, name=name,
        grid_spec=pltpu.PrefetchScalarGridSpec(
            num_scalar_prefetch=1, grid=(r // tr,),
            in_specs=[pl.BlockSpec((1, 1, tr, lp), lambda i, xy_ref: (xy_ref[0], mat, i, 0)),
                      pl.BlockSpec((3, 1, tr, lp), lambda i, xy_ref: (0, mat, i, 0)), blk, blk, blk],
            out_specs=[blk, blk, blk, blk]),
        out_shape=[SDS(w.shape, F32)] * 4, compiler_params=_params("parallel"),
    )(xy, own, others, w, m, v)


def _adam_small(g_all, w, m, v, *, name):
    def body(g_ref, w_ref, m_ref, v_ref, g_out, d_out, m_out, v_out):
        g = g_ref[0]
        for k in range(1, N_DEV):
            g = g + g_ref[k]
        delta, m_new, v_new = _adamw(w_ref[...], g, m_ref[...], v_ref[...])
        g_out[...] = g
        d_out[...] = delta
        m_out[...] = m_new
        v_out[...] = v_new

    return pl.pallas_call(body, name=name, out_shape=[SDS(w.shape, F32)] * 4)(g_all, w, m, v)


SMALL_ROWS = 56


def _pack_small(parts):
    flat = jnp.concatenate([a.reshape(-1) for a in parts])
    return jnp.pad(flat, (0, SMALL_ROWS * 128 - flat.shape[0])).reshape(SMALL_ROWS, 128)


def _unpack_small(packed, shapes):
    flat = packed.reshape(-1)
    out, at = [], 0
    for s in shapes:
        n = math.prod(s)
        out.append(flat[at:at + n].reshape(s))
        at += n
    return out


def kernel(x, mem, ffn1_norm, ffn1_w_gate, ffn1_w_up, ffn1_w_down, mix_norm, mem_norm, w_in, w_mem_kv, swa_q_norm, swa_k_norm, swa_sinks, rel_bias, gla_w_gate_up, gla_b_gate, gla_out_norm, mem_q_norm, mem_k_norm, w_out, ffn2_norm, ffn2_w_gate, ffn2_w_up, ffn2_w_down, loss_target, m_ffn1_norm, m_ffn1_w_gate, m_ffn1_w_up, m_ffn1_w_down, m_mix_norm, m_mem_norm, m_w_in, m_w_mem_kv, m_swa_q_norm, m_swa_k_norm, m_swa_sinks, m_rel_bias, m_gla_w_gate_up, m_gla_b_gate, m_gla_out_norm, m_mem_q_norm, m_mem_k_norm, m_w_out, m_ffn2_norm, m_ffn2_w_gate, m_ffn2_w_up, m_ffn2_w_down, v_ffn1_norm, v_ffn1_w_gate, v_ffn1_w_up, v_ffn1_w_down, v_mix_norm, v_mem_norm, v_w_in, v_w_mem_kv, v_swa_q_norm, v_swa_k_norm, v_swa_sinks, v_rel_bias, v_gla_w_gate_up, v_gla_b_gate, v_gla_out_norm, v_mem_q_norm, v_mem_k_norm, v_w_out, v_ffn2_norm, v_ffn2_w_gate, v_ffn2_w_up, v_ffn2_w_down):
    xi, yi, ci = lax.axis_index("x"), lax.axis_index("y"), lax.axis_index("c")
    c_arr = jnp.reshape(ci, (1,)).astype(jnp.int32)
    xy_arr = jnp.reshape(2 * xi + yi, (1,)).astype(jnp.int32)
    d = x.shape[-1]

    pad_h = FFN_SHARD_PAD - ffn1_w_gate.shape[-1]

    def gather_ffn(wg_s, wu_s, wd_s, name, collective_id, after):
        wgu_s = jnp.pad(jnp.concatenate([wg_s, wu_s], axis=0), ((0, 0), (0, 0), (0, pad_h))).astype(BF16)
        wd_s = jnp.pad(wd_s[0], ((0, pad_h), (0, 0))).astype(BF16)
        if after is not None:
            wgu_s, wd_s, _ = lax.optimization_barrier((wgu_s, wd_s, after))
        return _all_gather([wgu_s, wd_s], ["col", "row"], name=name, collective_id=collective_id)

    wgu1, wd1 = gather_ffn(ffn1_w_gate, ffn1_w_up, ffn1_w_down, "gather_ffn1", 0, None)
    mix_s = lax.optimization_barrier((w_in[0].astype(BF16), w_mem_kv[0].astype(BF16), w_out[0].astype(BF16), wd1))[:3]
    win_all, wkv, wout = _all_gather(list(mix_s), ["stack", "row", "row"], name="gather_mix", collective_id=1)

    def gather_ffn2(x1):
        return gather_ffn(ffn2_w_gate, ffn2_w_up, ffn2_w_down, "gather_ffn2", 2, (wout, x1))

    win = win_all.transpose(1, 0, 2).reshape(d, -1)
    glr_lo, glr_hi = COL_MQ, COL_MQ + GLA_RANK
    win_p = jnp.concatenate([win[:, :glr_lo], win[:, glr_hi:], win[:, glr_lo:glr_hi],
                             jnp.zeros((d, IN_W_PAD - IN_W), BF16)], axis=1)

    small_w = [ffn1_norm, mix_norm, mem_norm, ffn2_norm, swa_q_norm, swa_k_norm, swa_sinks[0], rel_bias,
               gla_w_gate_up[0], gla_b_gate, gla_out_norm, mem_q_norm, mem_k_norm]
    collective_ids = {"ffn2": (3, 4), "mix": (5, 6), "ffn1": (7, 8)}
    reduced, small_box = {}, []

    def on_grads(group, grads, carry, small=None):
        if group == "mix":
            dwin_p, dwkv, dwout = grads
            dwin = jnp.concatenate([dwin_p[:, :glr_lo], dwin_p[:, COL_GLR:COL_GLR + GLA_RANK], dwin_p[:, COL_MQ:COL_GLR]],
                                   axis=1)
            w = dwin.shape[1] // N_DEV
            grads = [dwin.reshape(d, 4, 2, w).transpose(2, 1, 0, 3).astype(BF16), dwkv, dwout]
            kinds = ["stack", "row", "row"]
        else:
            kinds = ["col", "row"]
        if reduced:
            earlier = list(reduced.values())[-1][1]
            *grads, _ = lax.optimization_barrier((*grads, earlier[0]))
        id_pair, id_chip = collective_ids[group]
        from_sibling = _pair_exchange(grads, kinds, name=f"pair_exchange_{group}", collective_id=id_pair)
        chip_sums = [_pair_sum(g, theirs, k, c_arr, name=f"pair_sum_{group}_{t}")
                     for t, (g, theirs, k) in enumerate(zip(grads, from_sibling, kinds))]
        if carry is not None:
            *chip_sums, carry = lax.optimization_barrier((*chip_sums, carry))
        if small is None:
            from_chips = _chip_exchange(chip_sums, None, name=f"chip_exchange_{group}", collective_id=id_chip)
        else:
            *from_chips, small_all = _chip_exchange(chip_sums, _pack_small(small), name=f"chip_exchange_{group}",
                                                    collective_id=id_chip)
            small_box.append(small_all)
        reduced[group] = (chip_sums, from_chips)
        return carry

    grad_x = _local_step(x[0], mem[0], loss_target[0], small_w, (wgu1, wd1, win_p, wkv, wout, gather_ffn2), on_grads)
    small_all = small_box[0]

    big_w = {"ffn1_w_gate": ("ffn1", 0, 0, ffn1_w_gate, m_ffn1_w_gate, v_ffn1_w_gate),
             "ffn1_w_up": ("ffn1", 0, 1, ffn1_w_up, m_ffn1_w_up, v_ffn1_w_up),
             "ffn1_w_down": ("ffn1", 1, 0, ffn1_w_down, m_ffn1_w_down, v_ffn1_w_down),
             "w_in": ("mix", 0, 0, w_in, m_w_in, v_w_in),
             "w_mem_kv": ("mix", 1, 0, w_mem_kv, m_w_mem_kv, v_w_mem_kv),
             "w_out": ("mix", 2, 0, w_out, m_w_out, v_w_out),
             "ffn2_w_gate": ("ffn2", 0, 0, ffn2_w_gate, m_ffn2_w_gate, v_ffn2_w_gate),
             "ffn2_w_up": ("ffn2", 0, 1, ffn2_w_up, m_ffn2_w_up, v_ffn2_w_up),
             "ffn2_w_down": ("ffn2", 1, 0, ffn2_w_down, m_ffn2_w_down, v_ffn2_w_down)}
    res = {}
    for nm, (group, t, mat, w, m, v) in big_w.items():
        r = w.shape[1]
        tr = 256 if r % 256 == 0 else r
        chip_sums, from_chips = reduced[group]
        res[nm] = _adam_big(chip_sums[t], from_chips[t], mat, xy_arr, w, m, v, tr=tr, name=f"adam_{nm}")
    small_names = ["ffn1_norm", "mix_norm", "mem_norm", "ffn2_norm", "swa_q_norm", "swa_k_norm", "swa_sinks", "rel_bias",
                   "gla_w_gate_up", "gla_b_gate", "gla_out_norm", "mem_q_norm", "mem_k_norm"]
    small_m = [m_ffn1_norm, m_mix_norm, m_mem_norm, m_ffn2_norm, m_swa_q_norm, m_swa_k_norm, m_swa_sinks, m_rel_bias,
               m_gla_w_gate_up, m_gla_b_gate, m_gla_out_norm, m_mem_q_norm, m_mem_k_norm]
    small_v = [v_ffn1_norm, v_mix_norm, v_mem_norm, v_ffn2_norm, v_swa_q_norm, v_swa_k_norm, v_swa_sinks, v_rel_bias,
               v_gla_w_gate_up, v_gla_b_gate, v_gla_out_norm, v_mem_q_norm, v_mem_k_norm]
    small_full = [ffn1_norm, mix_norm, mem_norm, ffn2_norm, swa_q_norm, swa_k_norm, swa_sinks, rel_bias,
                  gla_w_gate_up, gla_b_gate, gla_out_norm, mem_q_norm, mem_k_norm]
    packed = _adam_small(small_all, _pack_small(small_full), _pack_small(small_m), _pack_small(small_v), name="adam_small")
    full_shapes = [a.shape for a in small_full]
    unpacked = [_unpack_small(pk, full_shapes + [()]) for pk in packed]
    for k, nm in enumerate(small_names):
        res[nm] = [unpacked[q][k] for q in range(4)]
    loss = unpacked[0][len(small_names)]

    order = ["ffn1_norm", "ffn1_w_gate", "ffn1_w_up", "ffn1_w_down", "mix_norm", "mem_norm", "w_in", "w_mem_kv",
             "swa_q_norm", "swa_k_norm", "swa_sinks", "rel_bias", "gla_w_gate_up", "gla_b_gate", "gla_out_norm",
             "mem_q_norm", "mem_k_norm", "w_out", "ffn2_norm", "ffn2_w_gate", "ffn2_w_up", "ffn2_w_down"]
    outs = [loss, grad_x[None]]
    for q in range(4):
        outs += [res[nm][q] for nm in order]
    return tuple(outs)
```

```python
import functools
import math

import numpy as np
import jax
import jax.numpy as jnp
from jax import lax
from jax.experimental import pallas as pl
from jax.experimental.pallas import tpu as pltpu
from jax.experimental.pallas import tpu_sc as plsc

F32 = jnp.float32
BF16 = jnp.bfloat16
SDS = jax.ShapeDtypeStruct

EPS = 1e-6
HEAD_DIM = 64
SWA_HEADS = 8
SWA_KV_HEADS = 2
SWA_GROUP = SWA_HEADS // SWA_KV_HEADS
BLOCK = 128
N_BUCKETS = 32
MAX_DISTANCE = 128
GLA_HEADS = 4
GLA_DK = 32
GLA_DV = 64
GLA_RANK = 16
GLA_TAU = 16.0
GLA_CHUNK = 32
MEM_HEADS = 4
SWA_Q_W = SWA_HEADS * HEAD_DIM
SWA_KV_W = SWA_KV_HEADS * HEAD_DIM
GLA_QK_W = GLA_HEADS * GLA_DK
GLA_V_W = GLA_HEADS * GLA_DV
MEM_Q_W = MEM_HEADS * HEAD_DIM
IN_W = 1808
IN_W_PAD = 1920
COL_SQ, COL_SKV, COL_GQ, COL_GK, COL_GV, COL_GG, COL_MQ, COL_GLR = 0, 512, 768, 896, 1024, 1280, 1536, 1792

ADAM_LR = 0.001
ADAM_B1 = 0.9
ADAM_B2 = 0.999
ADAM_EPS = 1e-08
ADAM_WD = 0.01
ADAM_STEP = 10

N_DEV = 8
VMEM_LIMIT_BYTES = 56 * 1024 * 1024
MESH = pl.DeviceIdType.MESH


def _params(*sem):
    return pltpu.CompilerParams(dimension_semantics=sem or None, vmem_limit_bytes=VMEM_LIMIT_BYTES)


def _dot(a, b, ta, tb, precision=None):
    dims = (((0 if ta else 1,), (1 if tb else 0,)), ((), ()))
    return lax.dot_general(a, b, dims, preferred_element_type=F32, precision=precision)


def _mm_raw(a, b, ta=False, tb=False):
    return _dot(a.astype(BF16), b.astype(BF16), ta, tb)


def _mmf_raw(a, b, ta=False, tb=False):
    return _dot(a, b, ta, tb, lax.Precision.HIGHEST)


def _make_mm(raw):
    @functools.partial(jax.custom_vjp, nondiff_argnums=(2, 3))
    def mm(a, b, ta=False, tb=False):
        return raw(a, b, ta, tb)

    def fwd(a, b, ta, tb):
        return raw(a, b, ta, tb), (a, b)

    def bwd(ta, tb, res, g):
        a, b = res
        da = raw(b, g, tb, True) if ta else raw(g, b, False, not tb)
        db = raw(g, a, True, ta) if tb else raw(a, g, not ta, False)
        return da, db

    mm.defvjp(fwd, bwd)
    return mm


_mm = _make_mm(_mm_raw)
_mmf = _make_mm(_mmf_raw)


def _rms(x, g):
    return x * lax.rsqrt(jnp.mean(x * x, axis=-1, keepdims=True) + EPS) * g


def _silu_mul(g, u):
    return jax.nn.silu(g) * u


def _log_sigmoid(z):
    return jnp.minimum(z, 0.0) - jnp.log(1.0 + jnp.exp(-jnp.abs(z)))


def _matmul(a_list, b, *, ta=False, tb=False, tm, tn, b_blocks=None, res=None, scale=1.0, out_dtype=F32, name):
    if not isinstance(a_list, (list, tuple)):
        a_list = [a_list]
    n_a = len(a_list)
    m = a_list[0].shape[1] if ta else a_list[0].shape[0]
    ks = [a.shape[0] if ta else a.shape[1] for a in a_list]
    n = b.shape[0] if tb else b.shape[1]
    if b_blocks is None:
        assert n_a == 1
        b_blocks = [0]
    assert m % tm == 0 and n % tn == 0, (m, n, tm, tn)

    def body(*refs):
        a_refs, b_refs = refs[:n_a], refs[n_a:2 * n_a]
        r_ref = refs[2 * n_a] if res is not None else None
        o_ref = refs[-1]
        acc = _mm_raw(a_refs[0][...], b_refs[0][...], ta, tb)
        for k in range(1, n_a):
            acc = acc + _mm_raw(a_refs[k][...], b_refs[k][...], ta, tb)
        if scale != 1.0:
            acc = acc * scale
        if r_ref is not None:
            acc = r_ref[...] + acc
        o_ref[...] = acc.astype(out_dtype)

    in_specs = []
    for k in ks:
        in_specs.append(pl.BlockSpec((k, tm), lambda i, j: (0, i)) if ta else pl.BlockSpec((tm, k), lambda i, j: (i, 0)))
    for k, blk in zip(ks, b_blocks):
        if tb:
            in_specs.append(pl.BlockSpec((tn, k), functools.partial(lambda i, j, blk: (j, blk), blk=blk)))
        else:
            in_specs.append(pl.BlockSpec((k, tn), functools.partial(lambda i, j, blk: (blk, j), blk=blk)))
    args = list(a_list) + [b] * n_a
    if res is not None:
        in_specs.append(pl.BlockSpec((tm, tn), lambda i, j: (i, j)))
        args.append(res)
    return pl.pallas_call(
        body, name=name, grid=(m // tm, n // tn), in_specs=in_specs,
        out_specs=pl.BlockSpec((tm, tn), lambda i, j: (i, j)), out_shape=SDS((m, n), out_dtype),
        compiler_params=_params("parallel", "parallel"),
    )(*args)


def _win_pieces(w):
    glr_lo, glr_hi = COL_MQ, COL_MQ + GLA_RANK
    out = []
    for j in range(N_DEV):
        for lo, hi, shift in ((0, glr_lo, 0), (glr_lo, glr_hi, COL_GLR - glr_lo), (glr_hi, IN_W, COL_MQ - glr_hi)):
            s, e = max(j * w, lo), min((j + 1) * w, hi)
            if s < e:
                out.append((j, s - j * w, e - j * w, s + shift))
    return out


def _pack_win(win_all, *, tr, name):
    _, d, w = win_all.shape

    def body(i_ref, o_ref):
        for j, a, b, dst in _win_pieces(w):
            o_ref[:, dst:dst + b - a] = i_ref[j][:, a:b]
        o_ref[:, IN_W:] = jnp.zeros((tr, IN_W_PAD - IN_W), o_ref.dtype)

    return pl.pallas_call(
        body, name=name, grid=(d // tr,), in_specs=[pl.BlockSpec((N_DEV, tr, w), lambda i: (0, i, 0))],
        out_specs=pl.BlockSpec((tr, IN_W_PAD), lambda i: (i, 0)), out_shape=SDS((d, IN_W_PAD), win_all.dtype),
        compiler_params=_params("parallel"),
    )(win_all)


def _unpack_win(dwin_p, *, tr, name):
    d = dwin_p.shape[0]
    w = IN_W // N_DEV

    def body(i_ref, o_ref):
        for j, a, b, src in _win_pieces(w):
            o_ref[j % 2, j // 2, :, a:b] = i_ref[:, src:src + b - a]

    return pl.pallas_call(
        body, name=name, grid=(d // tr,), in_specs=[pl.BlockSpec((tr, IN_W_PAD), lambda i: (i, 0))],
        out_specs=pl.BlockSpec((2, 4, tr, w), lambda i: (0, 0, i, 0)), out_shape=SDS((2, 4, d, w), dwin_p.dtype),
        compiler_params=_params("parallel"),
    )(dwin_p)


def _rms_fwd(x, g, *, tm, name):
    s, d = x.shape

    def body(x_ref, g_ref, h_ref):
        h_ref[...] = _rms(x_ref[...], g_ref[...]).astype(BF16)

    return pl.pallas_call(
        body, name=name, grid=(s // tm,),
        in_specs=[pl.BlockSpec((tm, d), lambda i: (i, 0)), pl.BlockSpec((1, d), lambda i: (0, 0))],
        out_specs=pl.BlockSpec((tm, d), lambda i: (i, 0)), out_shape=SDS((s, d), BF16),
        compiler_params=_params("parallel"),
    )(x, g)


def _rms_bwd(x, g, dh, dres, *, tm, name):
    s, d = x.shape
    want_dx = dres is not None

    def body(*refs):
        if want_dx:
            x_ref, g_ref, dh_ref, dres_ref, dx_ref, dxb_ref, dg_ref = refs
        else:
            x_ref, g_ref, dh_ref, dg_ref = refs
        _, vjp = jax.vjp(_rms, x_ref[...], g_ref[...])
        dx, dg = vjp(dh_ref[...])
        if want_dx:
            dx = dres_ref[...] + dx
            dx_ref[...] = dx
            dxb_ref[...] = dx.astype(BF16)

        @pl.when(pl.program_id(0) == 0)
        def _():
            dg_ref[...] = jnp.zeros_like(dg_ref)

        dg_ref[...] += dg

    row = pl.BlockSpec((tm, d), lambda i: (i, 0))
    vec = pl.BlockSpec((1, d), lambda i: (0, 0))
    if want_dx:
        return pl.pallas_call(
            body, name=name, grid=(s // tm,), in_specs=[row, vec, row, row], out_specs=[row, row, vec],
            out_shape=[SDS((s, d), F32), SDS((s, d), BF16), SDS((1, d), F32)], compiler_params=_params("arbitrary"),
        )(x, g, dh, dres)
    return None, None, pl.pallas_call(
        body, name=name, grid=(s // tm,), in_specs=[row, vec, row], out_specs=vec,
        out_shape=SDS((1, d), F32), compiler_params=_params("arbitrary"),
    )(x, g, dh)


FFN_TN = 256
FFN_SHARD_PAD = 384


def _ffn_fwd(x, gain, wgu, wd, tag, *, tm=1024):
    s, d = x.shape
    f = wd.shape[0]
    tn = FFN_TN
    nj = f // tn
    tm = min(tm, s)

    def body(x_ref, gain_ref, wg_ref, wu_ref, wd_ref, y_ref, h_ref, g_ref, u_ref, acc_s):
        j = pl.program_id(1)

        @pl.when(j == 0)
        def _():
            h_ref[...] = _rms(x_ref[...], gain_ref[...]).astype(BF16)
            acc_s[...] = jnp.zeros_like(acc_s)

        hv = h_ref[...]
        g = _mm_raw(hv, wg_ref[...])
        u = _mm_raw(hv, wu_ref[...])
        g_ref[...] = g.astype(BF16)
        u_ref[...] = u.astype(BF16)
        acc_s[...] += _mm_raw(_silu_mul(g, u), wd_ref[...])

        @pl.when(j == nj - 1)
        def _():
            y_ref[...] = x_ref[...] + 0.5 * acc_s[...]

    row = pl.BlockSpec((tm, d), lambda i, j: (i, 0))
    tile = pl.BlockSpec((tm, tn), lambda i, j: (i, j))
    y, h, g, u = pl.pallas_call(
        body, name=f"{tag}_fwd", grid=(s // tm, nj),
        in_specs=[row, pl.BlockSpec((1, d), lambda i, j: (0, 0)),
                  pl.BlockSpec((None, d, tn), lambda i, j: (0, 0, j)), pl.BlockSpec((None, d, tn), lambda i, j: (1, 0, j)),
                  pl.BlockSpec((tn, d), lambda i, j: (j, 0))],
        out_specs=[row, row, tile, tile],
        out_shape=[SDS((s, d), F32), SDS((s, d), BF16), SDS((s, f), BF16), SDS((s, f), BF16)],
        scratch_shapes=[pltpu.VMEM((tm, d), F32)],
        compiler_params=_params("parallel", "arbitrary"),
    )(x, gain, wgu, wgu, wd)
    return y, (h, g, u)


def _ffn_bwd(dy, dyb, x, gain, wgu, wd, saved, tag):
    h, g, u = saved
    s, d = x.shape
    f = wd.shape[0]
    tn = FFN_TN

    def body(dy_ref, ht_ref, wg_ref, wu_ref, wd_ref, g_ref, u_ref, dh_ref, dwgu_ref, dwd_ref):
        @pl.when(pl.program_id(0) == 0)
        def _():
            dh_ref[...] = jnp.zeros_like(dh_ref)

        dyv = dy_ref[...]
        da = _mm_raw(dyv, wd_ref[...], False, True) * 0.5
        a, vjp = jax.vjp(_silu_mul, g_ref[...].astype(F32), u_ref[...].astype(F32))
        dg, du = vjp(da)
        dg = dg.astype(BF16)
        du = du.astype(BF16)
        dh_ref[...] += _mm_raw(dg, wg_ref[...], False, True) + _mm_raw(du, wu_ref[...], False, True)
        dwgu_ref[0] = _mm_raw(ht_ref[...], dg).astype(BF16)
        dwgu_ref[1] = _mm_raw(ht_ref[...], du).astype(BF16)
        dwd_ref[...] = (_mm_raw(a, dyv, True, False) * 0.5).astype(BF16)

    full = pl.BlockSpec((s, d), lambda j: (0, 0))
    tile = pl.BlockSpec((s, tn), lambda j: (0, j))
    wrow = pl.BlockSpec((tn, d), lambda j: (j, 0))
    dh, dwgu, dwd = pl.pallas_call(
        body, name=f"{tag}_bwd", grid=(f // tn,),
        in_specs=[full, pl.BlockSpec((d, s), lambda j: (0, 0)),
                  pl.BlockSpec((None, d, tn), lambda j: (0, 0, j)), pl.BlockSpec((None, d, tn), lambda j: (1, 0, j)),
                  wrow, tile, tile],
        out_specs=[full, pl.BlockSpec((2, d, tn), lambda j: (0, 0, j)), wrow],
        out_shape=[SDS((s, d), F32), SDS((2, d, f), BF16), SDS((f, d), BF16)],
        compiler_params=_params("arbitrary"),
    )(dyb, h.T, wgu, wgu, wd, g, u)
    dx, dxb, dgain = _rms_bwd(x, gain, dh, dy, tm=256, name=f"{tag}_drms")
    return dx, dxb, dgain, dwgu, dwd


def _loss_bwd(y, target, *, tm, name):
    s, d = y.shape

    def body(y_ref, t_ref, dy_ref, dyb_ref, l_ref):
        diff = y_ref[...] - t_ref[...]
        dy_ref[...] = diff * (1.0 / d)
        dyb_ref[...] = (diff * (1.0 / d)).astype(BF16)

        @pl.when(pl.program_id(0) == 0)
        def _():
            l_ref[...] = jnp.zeros_like(l_ref)

        l_ref[...] += 0.5 * jnp.sum(jnp.mean(diff * diff, axis=-1, keepdims=True), axis=0, keepdims=True)

    row = pl.BlockSpec((tm, d), lambda i: (i, 0))
    return pl.pallas_call(
        body, name=name, grid=(s // tm,), in_specs=[row, row],
        out_specs=[row, row, pl.BlockSpec((1, 1), lambda i: (0, 0))],
        out_shape=[SDS((s, d), F32), SDS((s, d), BF16), SDS((1, 1), F32)],
        compiler_params=_params("arbitrary"),
    )(y, target)


def _bucket_table():
    qi = np.arange(BLOCK)[:, None]
    kj = np.arange(2 * BLOCK)[None, :]
    dist = np.maximum(qi + BLOCK - kj, 0)
    max_exact = N_BUCKETS // 2
    d = np.maximum(dist, 1).astype(np.float32)
    large = max_exact + (np.log(d / np.float32(max_exact)) / np.float32(math.log(MAX_DISTANCE / max_exact))
                         * np.float32(N_BUCKETS - max_exact)).astype(np.int32)
    large = np.minimum(large, N_BUCKETS - 1)
    return np.where(dist < max_exact, dist, large).astype(np.int32)


SWA_STACK = SWA_GROUP * BLOCK


def _swa_valid(n):
    qi = lax.broadcasted_iota(jnp.int32, (SWA_STACK, 2 * BLOCK), 0) % BLOCK
    kj = lax.broadcasted_iota(jnp.int32, (SWA_STACK, 2 * BLOCK), 1)
    dist = qi + BLOCK - kj
    return (dist >= 0) & (dist < BLOCK) & ((kj >= BLOCK) | (n > 0))


def _swa_group(q, kb, vb, qg, kg, sink, bias, valid):
    qn = _rms(q, qg)
    kn = _rms(kb, kg)
    s = _mm(qn, kn, False, True) * (HEAD_DIM ** -0.5) + bias
    s = jnp.where(valid, s, -jnp.inf)
    m = lax.stop_gradient(jnp.maximum(jnp.max(s, axis=-1, keepdims=True), sink))
    p = jnp.exp(s - m)
    p = p / (jnp.sum(p, axis=-1, keepdims=True) + jnp.exp(sink - m))
    return _mm(p, vb)


def _swa_bias_table(rb_ref, bucket, bias_s):
    for h in range(SWA_HEADS):
        acc = jnp.zeros((BLOCK, 2 * BLOCK), F32)
        for b in range(N_BUCKETS):
            acc = jnp.where(bucket == b, rb_ref[b, h], acc)
        bias_s[h // SWA_GROUP, (h % SWA_GROUP) * BLOCK:(h % SWA_GROUP + 1) * BLOCK, :] = acc


def _swa_stack(ref, g):
    return jnp.concatenate([ref[:, (g * SWA_GROUP + hh) * HEAD_DIM:(g * SWA_GROUP + hh + 1) * HEAD_DIM]
                            for hh in range(SWA_GROUP)], axis=0)


def _swa_unstack(ref, g, stacked):
    for hh in range(SWA_GROUP):
        h = g * SWA_GROUP + hh
        ref[:, h * HEAD_DIM:(h + 1) * HEAD_DIM] = stacked[hh * BLOCK:(hh + 1) * BLOCK]


def _swa_sink_column(sink_ref, g):
    head = lax.broadcasted_iota(jnp.int32, (SWA_STACK, 1), 0) // BLOCK
    col = jnp.zeros((SWA_STACK, 1), F32)
    for hh in range(SWA_GROUP):
        col = jnp.where(head == hh, sink_ref[g * SWA_GROUP + hh], col)
    return col


def _swa_band(kvp_ref, kvc_ref, g):
    lo = g * HEAD_DIM
    kb = jnp.concatenate([kvp_ref[:, lo:lo + HEAD_DIM], kvc_ref[:, lo:lo + HEAD_DIM]], axis=0)
    lo += SWA_KV_W
    vb = jnp.concatenate([kvp_ref[:, lo:lo + HEAD_DIM], kvc_ref[:, lo:lo + HEAD_DIM]], axis=0)
    return kb, vb


def _swa_specs(order):
    kvc = COL_SKV // (2 * SWA_KV_W)
    return [
        pl.BlockSpec((BLOCK, SWA_Q_W), lambda t: (order(t), 0)),
        pl.BlockSpec((BLOCK, 2 * SWA_KV_W), lambda t: (jnp.maximum(order(t) - 1, 0), kvc)),
        pl.BlockSpec((BLOCK, 2 * SWA_KV_W), lambda t: (order(t), kvc)),
        pl.BlockSpec((1, HEAD_DIM), lambda t: (0, 0)),
        pl.BlockSpec((1, HEAD_DIM), lambda t: (0, 0)),
        pl.BlockSpec(memory_space=pltpu.SMEM),
        pl.BlockSpec(memory_space=pltpu.SMEM),
        pl.BlockSpec((BLOCK, 2 * BLOCK), lambda t: (0, 0)),
    ]


def _swa_fwd(p, qg, kg, sinks, rel_bias, *, name):
    s = p.shape[0]
    nb = s // BLOCK

    def body(q_ref, kvp_ref, kvc_ref, qg_ref, kg_ref, sink_ref, rb_ref, bucket_ref, y_ref, bias_s):
        n = pl.program_id(0)

        @pl.when(n == 0)
        def _():
            _swa_bias_table(rb_ref, bucket_ref[...], bias_s)

        valid = _swa_valid(n)
        for g in range(SWA_KV_HEADS):
            kb, vb = _swa_band(kvp_ref, kvc_ref, g)
            out = _swa_group(_swa_stack(q_ref, g), kb, vb, qg_ref[...], kg_ref[...], _swa_sink_column(sink_ref, g),
                             bias_s[g], valid)
            _swa_unstack(y_ref, g, out)

    return pl.pallas_call(
        body, name=name, grid=(nb,), in_specs=_swa_specs(lambda t: t),
        out_specs=pl.BlockSpec((BLOCK, SWA_Q_W), lambda t: (t, 0)), out_shape=SDS((s, SWA_Q_W), F32),
        scratch_shapes=[pltpu.VMEM((SWA_KV_HEADS, SWA_STACK, 2 * BLOCK), F32)],
        compiler_params=_params("arbitrary"),
    )(p, p, p, qg, kg, sinks, rel_bias, jnp.asarray(_bucket_table()))


def _swa_bwd(p, qg, kg, sinks, rel_bias, dy_all, *, name):
    s = p.shape[0]
    nb = s // BLOCK

    def body(q_ref, kvp_ref, kvc_ref, qg_ref, kg_ref, sink_ref, rb_ref, bucket_ref, dy_ref,
             dq_ref, dkv_ref, dqg_ref, dkg_ref, dsink_ref, drb_ref, bias_s, dbias_s, carry_s):
        t = pl.program_id(0)
        n = nb - 1 - t

        @pl.when(t == 0)
        def _():
            _swa_bias_table(rb_ref, bucket_ref[...], bias_s)
            dbias_s[...] = jnp.zeros_like(dbias_s)
            carry_s[...] = jnp.zeros_like(carry_s)
            dqg_ref[...] = jnp.zeros_like(dqg_ref)
            dkg_ref[...] = jnp.zeros_like(dkg_ref)
            dsink_ref[...] = jnp.zeros_like(dsink_ref)
            drb_ref[...] = jnp.zeros_like(drb_ref)

        valid = _swa_valid(n)
        lane = lax.broadcasted_iota(jnp.int32, (1, BLOCK), 1)
        dqg = jnp.zeros((1, HEAD_DIM), F32)
        dkg = jnp.zeros((1, HEAD_DIM), F32)
        dsink_vec = jnp.zeros((1, BLOCK), F32)
        for g in range(SWA_KV_HEADS):
            kb, vb = _swa_band(kvp_ref, kvc_ref, g)
            _, vjp = jax.vjp(functools.partial(_swa_group, valid=valid), _swa_stack(q_ref, g), kb, vb, qg_ref[...],
                             kg_ref[...], _swa_sink_column(sink_ref, g), bias_s[g])
            dq, dkb, dvb, dqg_g, dkg_g, dsink_col, dbias = vjp(_swa_stack(dy_ref, g))
            _swa_unstack(dq_ref, g, dq)
            dqg += dqg_g
            dkg += dkg_g
            dbias_s[g] += dbias
            for hh in range(SWA_GROUP):
                dsink_h = jnp.sum(dsink_col[hh * BLOCK:(hh + 1) * BLOCK], axis=0, keepdims=True)
                dsink_vec += jnp.where(lane == g * SWA_GROUP + hh, dsink_h, 0.0)
            lo = g * HEAD_DIM
            dkv_ref[:, lo:lo + HEAD_DIM] = dkb[BLOCK:] + carry_s[g]
            carry_s[g] = dkb[:BLOCK]
            lo += SWA_KV_W
            dkv_ref[:, lo:lo + HEAD_DIM] = dvb[BLOCK:] + carry_s[SWA_KV_HEADS + g]
            carry_s[SWA_KV_HEADS + g] = dvb[:BLOCK]
        dqg_ref[...] += dqg
        dkg_ref[...] += dkg
        dsink_ref[...] += dsink_vec

        @pl.when(t == nb - 1)
        def _():
            bucket = bucket_ref[...]
            row = lax.broadcasted_iota(jnp.int32, (N_BUCKETS, BLOCK), 0)
            col = lax.broadcasted_iota(jnp.int32, (N_BUCKETS, BLOCK), 1)
            acc = jnp.zeros((N_BUCKETS, BLOCK), F32)
            for h in range(SWA_HEADS):
                dbias = dbias_s[h // SWA_GROUP, (h % SWA_GROUP) * BLOCK:(h % SWA_GROUP + 1) * BLOCK, :]
                for b in range(N_BUCKETS):
                    part = jnp.sum(jnp.where(bucket == b, dbias, 0.0), axis=1, keepdims=True)
                    val = jnp.sum(part, axis=0, keepdims=True)
                    acc = acc + jnp.where((row == b) & (col == h), val, 0.0)
            drb_ref[...] = acc

    order = lambda t: nb - 1 - t
    vec = pl.BlockSpec((1, HEAD_DIM), lambda t: (0, 0))
    return pl.pallas_call(
        body, name=name, grid=(nb,),
        in_specs=_swa_specs(order) + [pl.BlockSpec((BLOCK, SWA_Q_W), lambda t: (order(t), 0))],
        out_specs=[pl.BlockSpec((BLOCK, SWA_Q_W), lambda t: (order(t), 0)),
                   pl.BlockSpec((BLOCK, 2 * SWA_KV_W), lambda t: (order(t), 0)),
                   vec, vec, pl.BlockSpec((1, BLOCK), lambda t: (0, 0)),
                   pl.BlockSpec((N_BUCKETS, BLOCK), lambda t: (0, 0))],
        out_shape=[SDS((s, SWA_Q_W), F32), SDS((s, 2 * SWA_KV_W), F32), SDS((1, HEAD_DIM), F32),
                   SDS((1, HEAD_DIM), F32), SDS((1, BLOCK), F32), SDS((N_BUCKETS, BLOCK), F32)],
        scratch_shapes=[pltpu.VMEM((SWA_KV_HEADS, SWA_STACK, 2 * BLOCK), F32),
                        pltpu.VMEM((SWA_KV_HEADS, SWA_STACK, 2 * BLOCK), F32),
                        pltpu.VMEM((2 * SWA_KV_HEADS, BLOCK, HEAD_DIM), F32)],
        compiler_params=_params("arbitrary"),
    )(p, p, p, qg, kg, sinks, rel_bias, jnp.asarray(_bucket_table()), dy_all)


def _mem_head(q, k, v, qg, kg):
    qn = _rms(q, qg)
    kn = _rms(k, kg)
    s = _mm(qn, kn, False, True) * (HEAD_DIM ** -0.5)
    m = lax.stop_gradient(jnp.max(s, axis=-1, keepdims=True))
    e = jnp.exp(s - m)
    return _mm(e / jnp.sum(e, axis=-1, keepdims=True), v)


def _mem_fwd(p, kv, qg, kg, *, tq, name):
    s = p.shape[0]
    m = kv.shape[0]

    def body(q_ref, kv_ref, qg_ref, kg_ref, y_ref):
        for h in range(MEM_HEADS):
            cols = slice(h * HEAD_DIM, (h + 1) * HEAD_DIM)
            vcols = slice(MEM_Q_W + h * HEAD_DIM, MEM_Q_W + (h + 1) * HEAD_DIM)
            y_ref[:, cols] = _mem_head(q_ref[:, cols], kv_ref[:, cols], kv_ref[:, vcols], qg_ref[...], kg_ref[...])

    vec = pl.BlockSpec((1, HEAD_DIM), lambda t: (0, 0))
    return pl.pallas_call(
        body, name=name, grid=(s // tq,),
        in_specs=[pl.BlockSpec((tq, MEM_Q_W), lambda t: (t, COL_MQ // MEM_Q_W)),
                  pl.BlockSpec((m, 2 * MEM_Q_W), lambda t: (0, 0)), vec, vec],
        out_specs=pl.BlockSpec((tq, MEM_Q_W), lambda t: (t, 0)), out_shape=SDS((s, MEM_Q_W), F32),
        compiler_params=_params("parallel"),
    )(p, kv, qg, kg)


def _mem_bwd(p, kv, qg, kg, dy_all, *, tq, name):
    s = p.shape[0]
    m = kv.shape[0]

    def body(q_ref, kv_ref, qg_ref, kg_ref, dy_ref, dq_ref, dkv_ref, dqg_ref, dkg_ref):
        @pl.when(pl.program_id(0) == 0)
        def _():
            dkv_ref[...] = jnp.zeros_like(dkv_ref)
            dqg_ref[...] = jnp.zeros_like(dqg_ref)
            dkg_ref[...] = jnp.zeros_like(dkg_ref)

        dqg = jnp.zeros((1, HEAD_DIM), F32)
        dkg = jnp.zeros((1, HEAD_DIM), F32)
        for h in range(MEM_HEADS):
            cols = slice(h * HEAD_DIM, (h + 1) * HEAD_DIM)
            vcols = slice(MEM_Q_W + h * HEAD_DIM, MEM_Q_W + (h + 1) * HEAD_DIM)
            _, vjp = jax.vjp(_mem_head, q_ref[:, cols], kv_ref[:, cols], kv_ref[:, vcols], qg_ref[...], kg_ref[...])
            dq, dk, dv, dqg_h, dkg_h = vjp(dy_ref[:, cols])
            dq_ref[:, cols] = dq
            dkv_ref[:, cols] += dk
            dkv_ref[:, vcols] += dv
            dqg += dqg_h
            dkg += dkg_h
        dqg_ref[...] += dqg
        dkg_ref[...] += dkg

    vec = pl.BlockSpec((1, HEAD_DIM), lambda t: (0, 0))
    full = pl.BlockSpec((m, 2 * MEM_Q_W), lambda t: (0, 0))
    dy_col = (SWA_Q_W + GLA_V_W) // MEM_Q_W
    return pl.pallas_call(
        body, name=name, grid=(s // tq,),
        in_specs=[pl.BlockSpec((tq, MEM_Q_W), lambda t: (t, COL_MQ // MEM_Q_W)), full, vec, vec,
                  pl.BlockSpec((tq, MEM_Q_W), lambda t: (t, dy_col))],
        out_specs=[pl.BlockSpec((tq, MEM_Q_W), lambda t: (t, 0)), full, vec, vec],
        out_shape=[SDS((s, MEM_Q_W), F32), SDS((m, 2 * MEM_Q_W), F32), SDS((1, HEAD_DIM), F32), SDS((1, HEAD_DIM), F32)],
        compiler_params=_params("arbitrary"),
    )(p, kv, qg, kg, dy_all)


GLA_ROWS = 256


def _gla_consts():
    c, h = GLA_CHUNK, GLA_HEADS
    i2 = lax.broadcasted_iota(jnp.int32, (c, c), 0)
    j2 = lax.broadcasted_iota(jnp.int32, (c, c), 1)
    slab_q = lax.broadcasted_iota(jnp.int32, (h, c, GLA_QK_W), 0)
    lane_q = lax.broadcasted_iota(jnp.int32, (h, c, GLA_QK_W), 2)
    row_a = lax.broadcasted_iota(jnp.int32, (h * c, c), 0)
    col_a = lax.broadcasted_iota(jnp.int32, (h * c, c), 1)
    slab_o = lax.broadcasted_iota(jnp.int32, (h, c, GLA_V_W), 0)
    lane_o = lax.broadcasted_iota(jnp.int32, (h, c, GLA_V_W), 2)
    row_s = lax.broadcasted_iota(jnp.int32, (GLA_V_W, GLA_QK_W), 0)
    col_s = lax.broadcasted_iota(jnp.int32, (GLA_V_W, GLA_QK_W), 1)
    return dict(
        ltri=(j2 <= i2).astype(F32),
        m_q=(slab_q == lane_q // GLA_DK).astype(F32),
        causal=col_a <= row_a % c,
        m_o=(slab_o == lane_o // GLA_DV).astype(F32),
        m_s=(row_s // GLA_DV == col_s // GLA_DK).astype(F32),
    )


def _gla_chunk(q, k, v, z, bg, st, c):
    h, n = GLA_HEADS, GLA_CHUNK
    la = _log_sigmoid(z + bg) * (1.0 / GLA_TAU)
    b = _mmf(c["ltri"], la)
    bl = jnp.sum(la, axis=0, keepdims=True)
    qs = q * (GLA_DK ** -0.5)
    kt = k * jnp.exp(bl - b)
    qt = qs * jnp.exp(b - bl)
    qe = qs * jnp.exp(b)
    q_stack = (jnp.broadcast_to(qt[None], (h, n, GLA_QK_W)) * c["m_q"]).reshape(h * n, GLA_QK_W)
    a = jnp.where(c["causal"], _mmf(q_stack, kt, False, True), 0.0)
    o_stack = _mm(a, v)
    o_intra = jnp.sum(o_stack.reshape(h, n, GLA_V_W) * c["m_o"], axis=0)
    o_inter = _mm(qe, st, False, True)
    st_next = st * jnp.exp(bl) + _mm(v, kt, True, False) * c["m_s"]
    return o_intra + o_inter, st_next


def _gla_post(o, gg, gain, g64):
    ms = _mmf(o * o, g64) * (1.0 / GLA_DV)
    return o * lax.rsqrt(ms + EPS) * gain * jax.nn.silu(gg)


def _gla_g64():
    r = lax.broadcasted_iota(jnp.int32, (GLA_V_W, GLA_V_W), 0)
    c = lax.broadcasted_iota(jnp.int32, (GLA_V_W, GLA_V_W), 1)
    return (r // GLA_DV == c // GLA_DV).astype(F32)


def _gla_in_specs(order):
    r = GLA_ROWS
    return [
        pl.BlockSpec((r, GLA_QK_W), lambda t: (order(t), COL_GQ // GLA_QK_W)),
        pl.BlockSpec((r, GLA_QK_W), lambda t: (order(t), COL_GK // GLA_QK_W)),
        pl.BlockSpec((r, GLA_V_W), lambda t: (order(t), COL_GV // GLA_V_W)),
        pl.BlockSpec((r, GLA_V_W), lambda t: (order(t), COL_GG // GLA_V_W)),
        pl.BlockSpec((r, GLA_QK_W), lambda t: (order(t), 0)),
        pl.BlockSpec((1, GLA_QK_W), lambda t: (0, 0)),
        pl.BlockSpec((1, GLA_V_W), lambda t: (0, 0)),
    ]


def _gla_fwd(p, z, bg, gain, *, name):
    s = p.shape[0]
    r = GLA_ROWS
    cps = r // GLA_CHUNK

    def body(q_ref, k_ref, v_ref, gg_ref, z_ref, bg_ref, gain_ref, y_ref, oraw_ref, stsave_ref, st_s):
        @pl.when(pl.program_id(0) == 0)
        def _():
            st_s[...] = jnp.zeros_like(st_s)

        c = _gla_consts()
        st = st_s[...]
        for ci in range(cps):
            rows = slice(ci * GLA_CHUNK, (ci + 1) * GLA_CHUNK)
            stsave_ref[ci] = st
            o, st = _gla_chunk(q_ref[rows, :], k_ref[rows, :], v_ref[rows, :], z_ref[rows, :], bg_ref[...], st, c)
            oraw_ref[rows, :] = o
        st_s[...] = st
        y_ref[...] = _gla_post(oraw_ref[...], gg_ref[...], gain_ref[...], _gla_g64())

    rowv = pl.BlockSpec((r, GLA_V_W), lambda t: (t, 0))
    return pl.pallas_call(
        body, name=name, grid=(s // r,), in_specs=_gla_in_specs(lambda t: t),
        out_specs=[rowv, rowv, pl.BlockSpec((cps, GLA_V_W, GLA_QK_W), lambda t: (t, 0, 0))],
        out_shape=[SDS((s, GLA_V_W), F32), SDS((s, GLA_V_W), F32), SDS((s // GLA_CHUNK, GLA_V_W, GLA_QK_W), F32)],
        scratch_shapes=[pltpu.VMEM((GLA_V_W, GLA_QK_W), F32)],
        compiler_params=_params("arbitrary"),
    )(p, p, p, p, z, bg, gain)


def _gla_bwd(p, z, bg, gain, oraw, stsave, dy_all, *, name):
    s = p.shape[0]
    r = GLA_ROWS
    cps = r // GLA_CHUNK
    nsteps = s // r
    w_qkvg = 2 * GLA_QK_W + 2 * GLA_V_W

    def body(q_ref, k_ref, v_ref, gg_ref, z_ref, bg_ref, gain_ref, oraw_ref, stsave_ref, dy_ref,
             dqkvg_ref, dz_ref, dbg_ref, dgain_ref, dst_s):
        @pl.when(pl.program_id(0) == 0)
        def _():
            dst_s[...] = jnp.zeros_like(dst_s)
            dbg_ref[...] = jnp.zeros_like(dbg_ref)
            dgain_ref[...] = jnp.zeros_like(dgain_ref)

        c = _gla_consts()
        _, vjp = jax.vjp(functools.partial(_gla_post, g64=_gla_g64()), oraw_ref[...], gg_ref[...], gain_ref[...])
        do, dgg, dgain = vjp(dy_ref[...])
        dqkvg_ref[:, 2 * GLA_QK_W + GLA_V_W:] = dgg
        dgain_ref[...] += dgain
        dst = dst_s[...]
        dbg = jnp.zeros((1, GLA_QK_W), F32)
        for ci in reversed(range(cps)):
            rows = slice(ci * GLA_CHUNK, (ci + 1) * GLA_CHUNK)
            _, vjp = jax.vjp(functools.partial(_gla_chunk, c=c), q_ref[rows, :], k_ref[rows, :], v_ref[rows, :],
                             z_ref[rows, :], bg_ref[...], stsave_ref[ci])
            dq, dk, dv, dz, dbg_c, dst = vjp((do[rows, :], dst))
            dqkvg_ref[rows, 0:GLA_QK_W] = dq
            dqkvg_ref[rows, GLA_QK_W:2 * GLA_QK_W] = dk
            dqkvg_ref[rows, 2 * GLA_QK_W:2 * GLA_QK_W + GLA_V_W] = dv
            dz_ref[rows, :] = dz
            dbg += dbg_c
        dst_s[...] = dst
        dbg_ref[...] += dbg

    order = lambda t: nsteps - 1 - t
    rowv = pl.BlockSpec((r, GLA_V_W), lambda t: (order(t), 0))
    return pl.pallas_call(
        body, name=name, grid=(nsteps,),
        in_specs=_gla_in_specs(order) + [
            rowv, pl.BlockSpec((cps, GLA_V_W, GLA_QK_W), lambda t: (order(t), 0, 0)),
            pl.BlockSpec((r, GLA_V_W), lambda t: (order(t), SWA_Q_W // GLA_V_W))],
        out_specs=[pl.BlockSpec((r, w_qkvg), lambda t: (order(t), 0)), pl.BlockSpec((r, GLA_QK_W), lambda t: (order(t), 0)),
                   pl.BlockSpec((1, GLA_QK_W), lambda t: (0, 0)), pl.BlockSpec((1, GLA_V_W), lambda t: (0, 0))],
        out_shape=[SDS((s, w_qkvg), F32), SDS((s, GLA_QK_W), F32), SDS((1, GLA_QK_W), F32), SDS((1, GLA_V_W), F32)],
        scratch_shapes=[pltpu.VMEM((GLA_V_W, GLA_QK_W), F32)],
        compiler_params=_params("arbitrary"),
    )(p, p, p, p, z, bg, gain, oraw, stsave, dy_all)


def _local_step(x, mem, target, small, big, on_grads):
    g1, gmix, gmem, g2, sqg, skg, sinks, rel_bias, wgu, bg, gla_gain, mqg, mkg = small
    wgu1, wd1, win_p, wkv, wout, gather_ffn2 = big
    wgu_pad = jnp.zeros((GLA_QK_W, GLA_QK_W), BF16).at[:GLA_RANK].set(wgu.astype(BF16))
    gain256 = jnp.tile(gla_gain, (1, GLA_HEADS))

    x1, saved1 = _ffn_fwd(x, g1, wgu1, wd1, "ffn1")
    wgu2, wd2 = gather_ffn2(x1)
    h = _rms_fwd(x1, gmix, tm=256, name="mix_rms")
    p = _matmul(h, win_p, tm=1024, tn=IN_W_PAD, name="mix_in")
    hm = _rms_fwd(mem, gmem, tm=256, name="mem_rms")
    kv = _matmul(hm, wkv, tm=256, tn=512, name="mem_kv")
    p_glr = p[:, COL_GLR:]
    z = _matmul(p_glr, wgu_pad, tm=1024, tn=GLA_QK_W, name="gla_gate")
    y_swa = _swa_fwd(p, sqg, skg, sinks, rel_bias, name="swa_fwd")
    y_gla, oraw, stsave = _gla_fwd(p, z, bg, gain256, name="gla_fwd")
    y_mem = _mem_fwd(p, kv, mqg, mkg, tq=512, name="mem_fwd")
    x2 = _matmul([y_swa, y_gla, y_mem], wout, b_blocks=[0, 2, 3], tm=1024, tn=1024, res=x1, name="mix_out")
    x3, saved2 = _ffn_fwd(x2, g2, wgu2, wd2, "ffn2")

    dy, dyb, loss = _loss_bwd(x3, target, tm=256, name="loss")
    dx2, dx2b, dg2, dwgu2, dwd2 = _ffn_bwd(dy, dyb, x2, g2, wgu2, wd2, saved2, "ffn2")
    dx2b = on_grads("ffn2", (dwgu2, dwd2), dx2b)
    dy_all = _matmul(dx2b, wout, tb=True, tm=1024, tn=1024, name="mix_dy")
    dwout = _matmul(jnp.concatenate([y_swa, y_gla, y_mem], axis=1), dx2b, ta=True, tm=512, tn=1024, out_dtype=BF16,
                    name="mix_dw_out")
    dq_swa, dkv_swa, dsqg, dskg, dsink, drb = _swa_bwd(p, sqg, skg, sinks, rel_bias, dy_all, name="swa_bwd")
    dqkvg, dz, dbg, dgain256 = _gla_bwd(p, z, bg, gain256, oraw, stsave, dy_all, name="gla_bwd")
    dmq, dkv_mem, dmqg, dmkg = _mem_bwd(p, kv, mqg, mkg, dy_all, tq=512, name="mem_bwd")
    dglr = _matmul(dz, wgu_pad, tb=True, tm=1024, tn=GLA_QK_W, name="gla_gate_dx")
    dwgu_pad = _matmul(p_glr, dz, ta=True, tm=GLA_QK_W, tn=GLA_QK_W, name="gla_gate_dw")
    dp = jnp.concatenate([dq_swa, dkv_swa, dqkvg, dmq, dglr], axis=1)
    dh = _matmul(dp, win_p, tb=True, tm=1024, tn=1024, name="mix_dh")
    dwin_p = _matmul(h, dp, ta=True, tm=1024, tn=640, out_dtype=BF16, name="mix_dw_in")
    dx1, dx1b, dgmix = _rms_bwd(x1, gmix, dh, dx2, tm=256, name="mix_drms")
    dwkv = _matmul(hm, dkv_mem, ta=True, tm=512, tn=512, out_dtype=BF16, name="mem_dw_kv")
    dx1b = on_grads("mix", (dwin_p, dwkv, dwout), dx1b)
    dhm = _matmul(dkv_mem, wkv, tb=True, tm=256, tn=512, name="mem_dh")
    _, _, dgmem = _rms_bwd(mem, gmem, dhm, None, tm=256, name="mem_drms")
    dx, _, dg1, dwgu1, dwd1 = _ffn_bwd(dx1, dx1b, x, g1, wgu1, wd1, saved1, "ffn1")

    dgla_gain = dgain256.reshape(GLA_HEADS, GLA_DV).sum(axis=0, keepdims=True)
    dsmall = (dg1, dgmix, dgmem, dg2, dsqg, dskg, dsink[0, :SWA_HEADS], drb[:, :SWA_HEADS], dwgu_pad[:GLA_RANK], dbg,
              dgla_gain, dmqg, dmkg)
    on_grads("ffn1", (dwgu1, dwd1), None, small=list(dsmall) + [loss])
    return dx


def _mesh_place():
    x, y, c = lax.axis_index("x"), lax.axis_index("y"), lax.axis_index("c")
    other_chips = [(1 - x, y), (x, 1 - y), (1 - x, 1 - y)]
    return x, y, c, other_chips


def _handshake(peers):
    barrier = pltpu.get_barrier_semaphore()
    for peer in peers:
        pl.semaphore_signal(barrier, inc=1, device_id=peer, device_id_type=MESH)
    pl.semaphore_wait(barrier, len(peers))


def _sequencer_call(body, operands, out_shapes, sems, *, name, collective_id):
    return pl.kernel(
        body, name=name, out_type=out_shapes, mesh=plsc.ScalarSubcoreMesh(axis_name="sequencer", num_cores=1),
        scratch_types=sems, compiler_params=pltpu.CompilerParams(collective_id=collective_id),
    )(*operands)


def _window(ref, kind, slot, shape):
    if kind == "col":
        return ref.at[:, :, pl.ds(pl.multiple_of(slot * shape[-1], 128), shape[-1])]
    if kind == "row":
        return ref.at[pl.ds(pl.multiple_of(slot * shape[0], 8), shape[0])]
    return ref.at[slot]


def _gathered(shape, kind):
    if kind == "col":
        return tuple(shape[:-1]) + (N_DEV * shape[-1],)
    if kind == "row":
        return (N_DEV * shape[0],) + tuple(shape[1:])
    return (N_DEV,) + tuple(shape)


def _all_gather(shards, kinds, *, name, collective_id):
    nt = len(shards)

    def body(*refs):
        x_refs, o_refs = refs[:nt], refs[nt:2 * nt]
        send_sems, recv_sems, local_sems = refs[2 * nt:]
        x, y, c, chips = _mesh_place()
        me, sibling = (x, y, c), (x, y, 1 - c)
        _handshake([sibling] + [(*chip, c) for chip in chips])

        def copy(k, t, block, to, from_shard=False):
            bx, by, bc = block
            rows = _window(o_refs[t], kinds[t], 4 * bx + 2 * by + bc, shards[t].shape)
            return pltpu.make_async_remote_copy(
                src_ref=x_refs[t] if from_shard else rows, dst_ref=rows,
                send_sem=send_sems.at[k, t], recv_sem=recv_sems.at[k, t], device_id=to, device_id_type=MESH)

        mine = [pltpu.make_async_copy(x_refs[t], _window(o_refs[t], kinds[t], 4 * x + 2 * y + c, shards[t].shape),
                                      local_sems.at[t]) for t in range(nt)]
        for cp in mine:
            cp.start()
        first = [copy(0, t, me, sibling, True) for t in range(nt)]
        first += [copy(1 + j, t, me, (*chip, c), True) for j, chip in enumerate(chips) for t in range(nt)]
        for cp in first:
            cp.start()
        passed = []
        for j, chip in enumerate(chips):
            for t in range(nt):
                copy(1 + j, t, (*chip, c), me).wait_recv()
                fwd = copy(4 + j, t, (*chip, c), sibling)
                fwd.start()
                passed.append(fwd)
        for t in range(nt):
            copy(0, t, sibling, me).wait_recv()
        for j, chip in enumerate(chips):
            for t in range(nt):
                copy(4 + j, t, (*chip, 1 - c), me).wait_recv()
        for cp in first + passed:
            cp.wait_send()
        for cp in mine:
            cp.wait()

    return _sequencer_call(
        body, shards, [SDS(_gathered(s.shape, k), s.dtype) for s, k in zip(shards, kinds)],
        [pltpu.SemaphoreType.DMA((7, nt)), pltpu.SemaphoreType.DMA((7, nt)), pltpu.SemaphoreType.DMA((nt,))],
        name=name, collective_id=collective_id)


def _part_shape(shape, kind):
    if kind == "col":
        return tuple(shape[:-1]) + (shape[-1] // N_DEV,)
    if kind == "row":
        return (shape[0] // N_DEV,) + tuple(shape[1:])
    return tuple(shape[2:])


def _pair_exchange(grads, kinds, *, name, collective_id):
    nt = len(grads)
    part = [_part_shape(g.shape, k) for g, k in zip(grads, kinds)]

    def body(*refs):
        g_refs, o_refs = refs[:nt], refs[nt:2 * nt]
        send_sems, recv_sems = refs[2 * nt:]
        x, y, c, _ = _mesh_place()
        _handshake([(x, y, 1 - c)])
        copies = []
        for t in range(nt):
            for xy in range(4):
                src = g_refs[t].at[1 - c, xy] if kinds[t] == "stack" else _window(g_refs[t], kinds[t], 2 * xy + 1 - c, part[t])
                copies.append(pltpu.make_async_remote_copy(
                    src_ref=src, dst_ref=o_refs[t].at[xy], send_sem=send_sems.at[xy, t], recv_sem=recv_sems.at[xy, t],
                    device_id=(x, y, 1 - c), device_id_type=MESH))
        for cp in copies:
            cp.start()
        for cp in copies:
            cp.wait()

    return _sequencer_call(
        body, grads, [SDS((4,) + p, g.dtype) for p, g in zip(part, grads)],
        [pltpu.SemaphoreType.DMA((4, nt)), pltpu.SemaphoreType.DMA((4, nt))], name=name, collective_id=collective_id)


def _chip_exchange(parts, small, *, name, collective_id):
    nt = len(parts)
    if small is None:
        def body_plain(*refs):
            s_refs, o_refs = refs[:nt], refs[nt:2 * nt]
            send_sems, recv_sems = refs[2 * nt:]
            x, y, c, chips = _mesh_place()
            _handshake([(*chip, c) for chip in chips])
            copies = [pltpu.make_async_remote_copy(
                src_ref=s_refs[t].at[2 * chip[0] + chip[1]], dst_ref=o_refs[t].at[j],
                send_sem=send_sems.at[j, t], recv_sem=recv_sems.at[j, t], device_id=(*chip, c), device_id_type=MESH)
                for j, chip in enumerate(chips) for t in range(nt)]
            for cp in copies:
                cp.start()
            for cp in copies:
                cp.wait()

        return _sequencer_call(
            body_plain, parts, [SDS((3,) + s.shape[1:], s.dtype) for s in parts],
            [pltpu.SemaphoreType.DMA((3, nt)), pltpu.SemaphoreType.DMA((3, nt))], name=name, collective_id=collective_id)

    def body(*refs):
        s_refs, small_ref = refs[:nt], refs[nt]
        o_refs, small_all = refs[nt + 1:2 * nt + 1], refs[2 * nt + 1]
        send_sems, recv_sems, small_send, small_recv, local_sem = refs[2 * nt + 2:]
        x, y, c, chips = _mesh_place()
        _handshake([(px, py, pc) for px in (x, 1 - x) for py in (y, 1 - y) for pc in (c, 1 - c)][1:])

        def copy(j, t, chip):
            return pltpu.make_async_remote_copy(
                src_ref=s_refs[t].at[2 * chip[0] + chip[1]], dst_ref=o_refs[t].at[j],
                send_sem=send_sems.at[j, t], recv_sem=recv_sems.at[j, t], device_id=(*chip, c), device_id_type=MESH)

        flips = [(fx, fy, fc) for fx in (0, 1) for fy in (0, 1) for fc in (0, 1)][1:]

        def small_copy(k):
            fx, fy, fc = flips[k]
            to = (x ^ fx if fx else x, y ^ fy if fy else y, c ^ fc if fc else c)
            rows = small_all.at[4 * x + 2 * y + c]
            return pltpu.make_async_remote_copy(
                src_ref=small_ref, dst_ref=rows, send_sem=small_send.at[k], recv_sem=small_recv.at[k],
                device_id=to, device_id_type=MESH)

        own = pltpu.make_async_copy(small_ref, small_all.at[4 * x + 2 * y + c], local_sem)
        own.start()
        copies = [copy(j, t, chip) for j, chip in enumerate(chips) for t in range(nt)]
        smalls = [small_copy(k) for k in range(7)]
        for cp in smalls + copies:
            cp.start()
        for cp in smalls + copies:
            cp.wait()
        own.wait()

    return _sequencer_call(
        body, list(parts) + [small],
        [SDS((3,) + s.shape[1:], s.dtype) for s in parts] + [SDS((N_DEV,) + small.shape, small.dtype)],
        [pltpu.SemaphoreType.DMA((3, nt)), pltpu.SemaphoreType.DMA((3, nt)),
         pltpu.SemaphoreType.DMA((7,)), pltpu.SemaphoreType.DMA((7,)), pltpu.SemaphoreType.DMA],
        name=name, collective_id=collective_id)


def _pair_sum(grad, theirs, kind, c, *, name):
    if kind == "col":
        n, r, l = theirs.shape[1:]
        mine_spec = pl.BlockSpec((None, r, l), lambda xy, m, c_ref: (m, 0, 2 * xy + c_ref[0]))
    elif kind == "row":
        (r, l), n = theirs.shape[1:], 1
        theirs = theirs[:, None]
        mine_spec = pl.BlockSpec((r, l), lambda xy, m, c_ref: (2 * xy + c_ref[0], 0))
    else:
        r, l = theirs.shape[-2:]
        n = theirs.size // (4 * r * l)
        theirs = theirs.reshape(4, n, r, l)
        grad = grad.reshape(2, 4, n, r, l)
        mine_spec = pl.BlockSpec((None, None, None, r, l), lambda xy, m, c_ref: (c_ref[0], xy, m, 0, 0))

    def body(c_ref, a_ref, b_ref, o_ref):
        o_ref[...] = (a_ref[...].astype(F32) + b_ref[...].astype(F32)).astype(BF16)

    part = pl.BlockSpec((None, None, r, l), lambda xy, m, c_ref: (xy, m, 0, 0))
    return pl.pallas_call(
        body, name=name,
        grid_spec=pltpu.PrefetchScalarGridSpec(num_scalar_prefetch=1, grid=(4, n), in_specs=[mine_spec, part], out_specs=part),
        out_shape=SDS((4, n, r, l), BF16), compiler_params=_params("parallel", "parallel"),
    )(c, grad, theirs)


def _adamw(w, g, m, v):
    m = ADAM_B1 * m + (1.0 - ADAM_B1) * g
    v = ADAM_B2 * v + (1.0 - ADAM_B2) * jnp.square(g)
    m_hat = m / (1.0 - ADAM_B1 ** ADAM_STEP)
    v_hat = v / (1.0 - ADAM_B2 ** ADAM_STEP)
    delta = -ADAM_LR * (m_hat / (jnp.sqrt(v_hat) + ADAM_EPS) + ADAM_WD * w)
    return delta, m, v


def _adam_big(own, others, mat, xy, w, m, v, *, tr, name):
    _, r, l = w.shape
    lp = own.shape[-1]

    def body(xy_ref, own_ref, oth_ref, w_ref, m_ref, v_ref, g_out, d_out, m_out, v_out):
        g = own_ref[0, 0].astype(F32)
        for j in range(3):
            g = g + oth_ref[j, 0].astype(F32)
        g = g[:, :l]
        delta, m_new, v_new = _adamw(w_ref[0], g, m_ref[0], v_ref[0])
        g_out[0] = g
        d_out[0] = delta
        m_out[0] = m_new
        v_out[0] = v_new

    blk = pl.BlockSpec((1, tr, l), lambda i, xy_ref: (0, i, 0))
    return pl.pallas_call(
        body, name=name,
        grid_spec=pltpu.PrefetchScalarGridSpec(
            num_scalar_prefetch=1, grid=(r // tr,),
            in_specs=[pl.BlockSpec((1, 1, tr, lp), lambda i, xy_ref: (xy_ref[0], mat, i, 0)),
                      pl.BlockSpec((3, 1, tr, lp), lambda i, xy_ref: (0, mat, i, 0)), blk, blk, blk],
            out_specs=[blk, blk, blk, blk]),
        out_shape=[SDS(w.shape, F32)] * 4, compiler_params=_params("parallel"),
    )(xy, own, others, w, m, v)


def _adam_small(g_all, w, m, v, *, name):
    def body(g_ref, w_ref, m_ref, v_ref, g_out, d_out, m_out, v_out):
        g = g_ref[0]
        for k in range(1, N_DEV):
            g = g + g_ref[k]
        delta, m_new, v_new = _adamw(w_ref[...], g, m_ref[...], v_ref[...])
        g_out[...] = g
        d_out[...] = delta
        m_out[...] = m_new
        v_out[...] = v_new

    return pl.pallas_call(body, name=name, out_shape=[SDS(w.shape, F32)] * 4)(g_all, w, m, v)


SMALL_ROWS = 56


def _pack_small(parts):
    flat = jnp.concatenate([a.reshape(-1) for a in parts])
    return jnp.pad(flat, (0, SMALL_ROWS * 128 - flat.shape[0])).reshape(SMALL_ROWS, 128)


def _unpack_small(packed, shapes):
    flat = packed.reshape(-1)
    out, at = [], 0
    for s in shapes:
        n = math.prod(s)
        out.append(flat[at:at + n].reshape(s))
        at += n
    return out


def kernel(x, mem, ffn1_norm, ffn1_w_gate, ffn1_w_up, ffn1_w_down, mix_norm, mem_norm, w_in, w_mem_kv, swa_q_norm, swa_k_norm, swa_sinks, rel_bias, gla_w_gate_up, gla_b_gate, gla_out_norm, mem_q_norm, mem_k_norm, w_out, ffn2_norm, ffn2_w_gate, ffn2_w_up, ffn2_w_down, loss_target, m_ffn1_norm, m_ffn1_w_gate, m_ffn1_w_up, m_ffn1_w_down, m_mix_norm, m_mem_norm, m_w_in, m_w_mem_kv, m_swa_q_norm, m_swa_k_norm, m_swa_sinks, m_rel_bias, m_gla_w_gate_up, m_gla_b_gate, m_gla_out_norm, m_mem_q_norm, m_mem_k_norm, m_w_out, m_ffn2_norm, m_ffn2_w_gate, m_ffn2_w_up, m_ffn2_w_down, v_ffn1_norm, v_ffn1_w_gate, v_ffn1_w_up, v_ffn1_w_down, v_mix_norm, v_mem_norm, v_w_in, v_w_mem_kv, v_swa_q_norm, v_swa_k_norm, v_swa_sinks, v_rel_bias, v_gla_w_gate_up, v_gla_b_gate, v_gla_out_norm, v_mem_q_norm, v_mem_k_norm, v_w_out, v_ffn2_norm, v_ffn2_w_gate, v_ffn2_w_up, v_ffn2_w_down):
    xi, yi, ci = lax.axis_index("x"), lax.axis_index("y"), lax.axis_index("c")
    c_arr = jnp.reshape(ci, (1,)).astype(jnp.int32)
    xy_arr = jnp.reshape(2 * xi + yi, (1,)).astype(jnp.int32)
    d = x.shape[-1]

    pad_h = FFN_SHARD_PAD - ffn1_w_gate.shape[-1]

    def gather_ffn(wg_s, wu_s, wd_s, name, collective_id, after):
        wgu_s = jnp.pad(jnp.concatenate([wg_s, wu_s], axis=0), ((0, 0), (0, 0), (0, pad_h))).astype(BF16)
        wd_s = jnp.pad(wd_s[0], ((0, pad_h), (0, 0))).astype(BF16)
        if after is not None:
            wgu_s, wd_s, _ = lax.optimization_barrier((wgu_s, wd_s, after))
        return _all_gather([wgu_s, wd_s], ["col", "row"], name=name, collective_id=collective_id)

    wgu1, wd1 = gather_ffn(ffn1_w_gate, ffn1_w_up, ffn1_w_down, "gather_ffn1", 0, None)
    mix_s = lax.optimization_barrier((w_in[0].astype(BF16), w_mem_kv[0].astype(BF16), w_out[0].astype(BF16), wd1))[:3]
    win_all, wkv, wout = _all_gather(list(mix_s), ["stack", "row", "row"], name="gather_mix", collective_id=1)

    def gather_ffn2(x1):
        return gather_ffn(ffn2_w_gate, ffn2_w_up, ffn2_w_down, "gather_ffn2", 2, (wout, x1))

    win_p = _pack_win(win_all, tr=256, name="pack_w_in")

    small_w = [ffn1_norm, mix_norm, mem_norm, ffn2_norm, swa_q_norm, swa_k_norm, swa_sinks[0], rel_bias,
               gla_w_gate_up[0], gla_b_gate, gla_out_norm, mem_q_norm, mem_k_norm]
    collective_ids = {"ffn2": (3, 4), "mix": (5, 6), "ffn1": (7, 8)}
    reduced, small_box = {}, []

    def on_grads(group, grads, carry, small=None):
        if group == "mix":
            dwin_p, dwkv, dwout = grads
            grads = [_unpack_win(dwin_p, tr=256, name="unpack_dw_in"), dwkv, dwout]
            kinds = ["stack", "row", "row"]
        else:
            kinds = ["col", "row"]
        if reduced:
            earlier = list(reduced.values())[-1][1]
            *grads, _ = lax.optimization_barrier((*grads, earlier[0]))
        id_pair, id_chip = collective_ids[group]
        from_sibling = _pair_exchange(grads, kinds, name=f"pair_exchange_{group}", collective_id=id_pair)
        chip_sums = [_pair_sum(g, theirs, k, c_arr, name=f"pair_sum_{group}_{t}")
                     for t, (g, theirs, k) in enumerate(zip(grads, from_sibling, kinds))]
        if carry is not None:
            *chip_sums, carry = lax.optimization_barrier((*chip_sums, carry))
        if small is None:
            from_chips = _chip_exchange(chip_sums, None, name=f"chip_exchange_{group}", collective_id=id_chip)
        else:
            *from_chips, small_all = _chip_exchange(chip_sums, _pack_small(small), name=f"chip_exchange_{group}",
                                                    collective_id=id_chip)
            small_box.append(small_all)
        reduced[group] = (chip_sums, from_chips)
        return carry

    grad_x = _local_step(x[0], mem[0], loss_target[0], small_w, (wgu1, wd1, win_p, wkv, wout, gather_ffn2), on_grads)
    small_all = small_box[0]

    big_w = {"ffn1_w_gate": ("ffn1", 0, 0, ffn1_w_gate, m_ffn1_w_gate, v_ffn1_w_gate),
             "ffn1_w_up": ("ffn1", 0, 1, ffn1_w_up, m_ffn1_w_up, v_ffn1_w_up),
             "ffn1_w_down": ("ffn1", 1, 0, ffn1_w_down, m_ffn1_w_down, v_ffn1_w_down),
             "w_in": ("mix", 0, 0, w_in, m_w_in, v_w_in),
             "w_mem_kv": ("mix", 1, 0, w_mem_kv, m_w_mem_kv, v_w_mem_kv),
             "w_out": ("mix", 2, 0, w_out, m_w_out, v_w_out),
             "ffn2_w_gate": ("ffn2", 0, 0, ffn2_w_gate, m_ffn2_w_gate, v_ffn2_w_gate),
             "ffn2_w_up": ("ffn2", 0, 1, ffn2_w_up, m_ffn2_w_up, v_ffn2_w_up),
             "ffn2_w_down": ("ffn2", 1, 0, ffn2_w_down, m_ffn2_w_down, v_ffn2_w_down)}
    res = {}
    for nm, (group, t, mat, w, m, v) in big_w.items():
        r = w.shape[1]
        tr = 256 if r % 256 == 0 else r
        chip_sums, from_chips = reduced[group]
        res[nm] = _adam_big(chip_sums[t], from_chips[t], mat, xy_arr, w, m, v, tr=tr, name=f"adam_{nm}")
    small_names = ["ffn1_norm", "mix_norm", "mem_norm", "ffn2_norm", "swa_q_norm", "swa_k_norm", "swa_sinks", "rel_bias",
                   "gla_w_gate_up", "gla_b_gate", "gla_out_norm", "mem_q_norm", "mem_k_norm"]
    small_m = [m_ffn1_norm, m_mix_norm, m_mem_norm, m_ffn2_norm, m_swa_q_norm, m_swa_k_norm, m_swa_sinks, m_rel_bias,
               m_gla_w_gate_up, m_gla_b_gate, m_gla_out_norm, m_mem_q_norm, m_mem_k_norm]
    small_v = [v_ffn1_norm, v_mix_norm, v_mem_norm, v_ffn2_norm, v_swa_q_norm, v_swa_k_norm, v_swa_sinks, v_rel_bias,
               v_gla_w_gate_up, v_gla_b_gate, v_gla_out_norm, v_mem_q_norm, v_mem_k_norm]
    small_full = [ffn1_norm, mix_norm, mem_norm, ffn2_norm, swa_q_norm, swa_k_norm, swa_sinks, rel_bias,
                  gla_w_gate_up, gla_b_gate, gla_out_norm, mem_q_norm, mem_k_norm]
    packed = _adam_small(small_all, _pack_small(small_full), _pack_small(small_m), _pack_small(small_v), name="adam_small")
    full_shapes = [a.shape for a in small_full]
    unpacked = [_unpack_small(pk, full_shapes + [()]) for pk in packed]
    for k, nm in enumerate(small_names):
        res[nm] = [unpacked[q][k] for q in range(4)]
    loss = unpacked[0][len(small_names)]

    order = ["ffn1_norm", "ffn1_w_gate", "ffn1_w_up", "ffn1_w_down", "mix_norm", "mem_norm", "w_in", "w_mem_kv",
             "swa_q_norm", "swa_k_norm", "swa_sinks", "rel_bias", "gla_w_gate_up", "gla_b_gate", "gla_out_norm",
             "mem_q_norm", "mem_k_norm", "w_out", "ffn2_norm", "ffn2_w_gate", "ffn2_w_up", "ffn2_w_down"]
    outs = [loss, grad_x[None]]
    for q in range(4):
        outs += [res[nm][q] for nm in order]
    return tuple(outs)
```

```python
import functools
import math

import numpy as np
import jax
import jax.numpy as jnp
from jax import lax
from jax.experimental import pallas as pl
from jax.experimental.pallas import tpu as pltpu
from jax.experimental.pallas import tpu_sc as plsc

F32 = jnp.float32
BF16 = jnp.bfloat16
SDS = jax.ShapeDtypeStruct

EPS = 1e-6
HEAD_DIM = 64
SWA_HEADS = 8
SWA_KV_HEADS = 2
SWA_GROUP = SWA_HEADS // SWA_KV_HEADS
BLOCK = 128
N_BUCKETS = 32
MAX_DISTANCE = 128
GLA_HEADS = 4
GLA_DK = 32
GLA_DV = 64
GLA_RANK = 16
GLA_TAU = 16.0
GLA_CHUNK = 32
MEM_HEADS = 4
SWA_Q_W = SWA_HEADS * HEAD_DIM
SWA_KV_W = SWA_KV_HEADS * HEAD_DIM
GLA_QK_W = GLA_HEADS * GLA_DK
GLA_V_W = GLA_HEADS * GLA_DV
MEM_Q_W = MEM_HEADS * HEAD_DIM
IN_W = 1808
IN_W_PAD = 1920
COL_SQ, COL_SKV, COL_GQ, COL_GK, COL_GV, COL_GG, COL_MQ, COL_GLR = 0, 512, 768, 896, 1024, 1280, 1536, 1792

ADAM_LR = 0.001
ADAM_B1 = 0.9
ADAM_B2 = 0.999
ADAM_EPS = 1e-08
ADAM_WD = 0.01
ADAM_STEP = 10

N_DEV = 8
VMEM_LIMIT_BYTES = 56 * 1024 * 1024
MESH = pl.DeviceIdType.MESH


def _params(*sem):
    return pltpu.CompilerParams(dimension_semantics=sem or None, vmem_limit_bytes=VMEM_LIMIT_BYTES)


def _dot(a, b, ta, tb, precision=None):
    dims = (((0 if ta else 1,), (1 if tb else 0,)), ((), ()))
    return lax.dot_general(a, b, dims, preferred_element_type=F32, precision=precision)


def _mm_raw(a, b, ta=False, tb=False):
    return _dot(a.astype(BF16), b.astype(BF16), ta, tb)


def _mmf_raw(a, b, ta=False, tb=False):
    return _dot(a, b, ta, tb, lax.Precision.HIGHEST)


def _make_mm(raw):
    @functools.partial(jax.custom_vjp, nondiff_argnums=(2, 3))
    def mm(a, b, ta=False, tb=False):
        return raw(a, b, ta, tb)

    def fwd(a, b, ta, tb):
        return raw(a, b, ta, tb), (a, b)

    def bwd(ta, tb, res, g):
        a, b = res
        da = raw(b, g, tb, True) if ta else raw(g, b, False, not tb)
        db = raw(g, a, True, ta) if tb else raw(a, g, not ta, False)
        return da, db

    mm.defvjp(fwd, bwd)
    return mm


_mm = _make_mm(_mm_raw)
_mmf = _make_mm(_mmf_raw)


def _rms(x, g):
    return x * lax.rsqrt(jnp.mean(x * x, axis=-1, keepdims=True) + EPS) * g


def _silu_mul(g, u):
    return jax.nn.silu(g) * u


def _log_sigmoid(z):
    return jnp.minimum(z, 0.0) - jnp.log(1.0 + jnp.exp(-jnp.abs(z)))


def _matmul(a_list, b, *, ta=False, tb=False, tm, tn, b_blocks=None, res=None, scale=1.0, out_dtype=F32, name):
    if not isinstance(a_list, (list, tuple)):
        a_list = [a_list]
    n_a = len(a_list)
    m = a_list[0].shape[1] if ta else a_list[0].shape[0]
    ks = [a.shape[0] if ta else a.shape[1] for a in a_list]
    n = b.shape[0] if tb else b.shape[1]
    if b_blocks is None:
        assert n_a == 1
        b_blocks = [0]
    tm, tn = min(tm, m), min(tn, n)
    assert m % tm == 0 and n % tn == 0, (m, n, tm, tn)

    def body(*refs):
        a_refs, b_refs = refs[:n_a], refs[n_a:2 * n_a]
        r_ref = refs[2 * n_a] if res is not None else None
        o_ref = refs[-1]
        acc = _mm_raw(a_refs[0][...], b_refs[0][...], ta, tb)
        for k in range(1, n_a):
            acc = acc + _mm_raw(a_refs[k][...], b_refs[k][...], ta, tb)
        if scale != 1.0:
            acc = acc * scale
        if r_ref is not None:
            acc = r_ref[...] + acc
        o_ref[...] = acc.astype(out_dtype)

    in_specs = []
    for k in ks:
        in_specs.append(pl.BlockSpec((k, tm), lambda i, j: (0, i)) if ta else pl.BlockSpec((tm, k), lambda i, j: (i, 0)))
    for k, blk in zip(ks, b_blocks):
        if tb:
            in_specs.append(pl.BlockSpec((tn, k), functools.partial(lambda i, j, blk: (j, blk), blk=blk)))
        else:
            in_specs.append(pl.BlockSpec((k, tn), functools.partial(lambda i, j, blk: (blk, j), blk=blk)))
    args = list(a_list) + [b] * n_a
    if res is not None:
        in_specs.append(pl.BlockSpec((tm, tn), lambda i, j: (i, j)))
        args.append(res)
    return pl.pallas_call(
        body, name=name, grid=(m // tm, n // tn), in_specs=in_specs,
        out_specs=pl.BlockSpec((tm, tn), lambda i, j: (i, j)), out_shape=SDS((m, n), out_dtype),
        compiler_params=_params("parallel", "parallel"),
    )(*args)


def _win_pieces(w):
    glr_lo, glr_hi = COL_MQ, COL_MQ + GLA_RANK
    out = []
    for j in range(N_DEV):
        for lo, hi, shift in ((0, glr_lo, 0), (glr_lo, glr_hi, COL_GLR - glr_lo), (glr_hi, IN_W, COL_MQ - glr_hi)):
            s, e = max(j * w, lo), min((j + 1) * w, hi)
            if s < e:
                out.append((j, s - j * w, e - j * w, s + shift))
    return out


def _pack_win(win_all, *, tr, name):
    _, d, w = win_all.shape

    def body(i_ref, o_ref):
        for j, a, b, dst in _win_pieces(w):
            o_ref[:, dst:dst + b - a] = i_ref[j][:, a:b]
        o_ref[:, IN_W:] = jnp.zeros((tr, IN_W_PAD - IN_W), o_ref.dtype)

    return pl.pallas_call(
        body, name=name, grid=(d // tr,), in_specs=[pl.BlockSpec((N_DEV, tr, w), lambda i: (0, i, 0))],
        out_specs=pl.BlockSpec((tr, IN_W_PAD), lambda i: (i, 0)), out_shape=SDS((d, IN_W_PAD), win_all.dtype),
        compiler_params=_params("parallel"),
    )(win_all)


def _unpack_win(dwin_p, *, tr, name):
    d = dwin_p.shape[0]
    w = IN_W // N_DEV

    def body(i_ref, o_ref):
        for j, a, b, src in _win_pieces(w):
            o_ref[j % 2, j // 2, :, a:b] = i_ref[:, src:src + b - a]

    return pl.pallas_call(
        body, name=name, grid=(d // tr,), in_specs=[pl.BlockSpec((tr, IN_W_PAD), lambda i: (i, 0))],
        out_specs=pl.BlockSpec((2, 4, tr, w), lambda i: (0, 0, i, 0)), out_shape=SDS((2, 4, d, w), dwin_p.dtype),
        compiler_params=_params("parallel"),
    )(dwin_p)


def _rms_fwd(x, g, *, tm, name):
    s, d = x.shape

    def body(x_ref, g_ref, h_ref):
        h_ref[...] = _rms(x_ref[...], g_ref[...]).astype(BF16)

    return pl.pallas_call(
        body, name=name, grid=(s // tm,),
        in_specs=[pl.BlockSpec((tm, d), lambda i: (i, 0)), pl.BlockSpec((1, d), lambda i: (0, 0))],
        out_specs=pl.BlockSpec((tm, d), lambda i: (i, 0)), out_shape=SDS((s, d), BF16),
        compiler_params=_params("parallel"),
    )(x, g)


def _rms_bwd(x, g, dh, dres, *, tm, name):
    s, d = x.shape
    want_dx = dres is not None

    def body(*refs):
        if want_dx:
            x_ref, g_ref, dh_ref, dres_ref, dx_ref, dxb_ref, dg_ref = refs
        else:
            x_ref, g_ref, dh_ref, dg_ref = refs
        _, vjp = jax.vjp(_rms, x_ref[...], g_ref[...])
        dx, dg = vjp(dh_ref[...])
        if want_dx:
            dx = dres_ref[...] + dx
            dx_ref[...] = dx
            dxb_ref[...] = dx.astype(BF16)

        @pl.when(pl.program_id(0) == 0)
        def _():
            dg_ref[...] = jnp.zeros_like(dg_ref)

        dg_ref[...] += dg

    row = pl.BlockSpec((tm, d), lambda i: (i, 0))
    vec = pl.BlockSpec((1, d), lambda i: (0, 0))
    if want_dx:
        return pl.pallas_call(
            body, name=name, grid=(s // tm,), in_specs=[row, vec, row, row], out_specs=[row, row, vec],
            out_shape=[SDS((s, d), F32), SDS((s, d), BF16), SDS((1, d), F32)], compiler_params=_params("arbitrary"),
        )(x, g, dh, dres)
    return None, None, pl.pallas_call(
        body, name=name, grid=(s // tm,), in_specs=[row, vec, row], out_specs=vec,
        out_shape=SDS((1, d), F32), compiler_params=_params("arbitrary"),
    )(x, g, dh)


FFN_TN = 256
FFN_SHARD_PAD = 384


def _ffn_fwd(x, gain, w3, tag, *, tm=1024):
    s, d = x.shape
    f = w3.shape[1]
    tn = FFN_TN
    nj = f // tn
    tm = min(tm, s)

    def body(x_ref, gain_ref, wg_ref, wu_ref, wd_ref, y_ref, h_ref, g_ref, u_ref, acc_s):
        j = pl.program_id(1)

        @pl.when(j == 0)
        def _():
            h_ref[...] = _rms(x_ref[...], gain_ref[...]).astype(BF16)
            acc_s[...] = jnp.zeros_like(acc_s)

        hv = h_ref[...]
        g = _mm_raw(hv, wg_ref[...], False, True)
        u = _mm_raw(hv, wu_ref[...], False, True)
        g_ref[...] = g.astype(BF16)
        u_ref[...] = u.astype(BF16)
        acc_s[...] += _mm_raw(_silu_mul(g, u), wd_ref[...])

        @pl.when(j == nj - 1)
        def _():
            y_ref[...] = x_ref[...] + 0.5 * acc_s[...]

    row = pl.BlockSpec((tm, d), lambda i, j: (i, 0))
    tile = pl.BlockSpec((tm, tn), lambda i, j: (i, j))
    y, h, g, u = pl.pallas_call(
        body, name=f"{tag}_fwd", grid=(s // tm, nj),
        in_specs=[row, pl.BlockSpec((1, d), lambda i, j: (0, 0))]
        + [pl.BlockSpec((None, tn, d), functools.partial(lambda i, j, k: (k, j, 0), k=k)) for k in range(3)],
        out_specs=[row, row, tile, tile],
        out_shape=[SDS((s, d), F32), SDS((s, d), BF16), SDS((s, f), BF16), SDS((s, f), BF16)],
        scratch_shapes=[pltpu.VMEM((tm, d), F32)],
        compiler_params=_params("parallel", "arbitrary"),
    )(x, gain, w3, w3, w3)
    return y, (h, g, u)


def _ffn_bwd(dy, dyb, x, gain, w3, saved, tag):
    h, g, u = saved
    s, d = x.shape
    f = w3.shape[1]
    tn = FFN_TN

    def body(dy_ref, h_ref, wg_ref, wu_ref, wd_ref, g_ref, u_ref, dh_ref, dw3_ref):
        @pl.when(pl.program_id(0) == 0)
        def _():
            dh_ref[...] = jnp.zeros_like(dh_ref)

        dyv = dy_ref[...]
        da = _mm_raw(dyv, wd_ref[...], False, True) * 0.5
        a, vjp = jax.vjp(_silu_mul, g_ref[...].astype(F32), u_ref[...].astype(F32))
        dg, du = vjp(da)
        dg = dg.astype(BF16)
        du = du.astype(BF16)
        dh_ref[...] += _mm_raw(dg, wg_ref[...]) + _mm_raw(du, wu_ref[...])
        hv = h_ref[...]
        dw3_ref[0] = _mm_raw(dg, hv, True, False).astype(BF16)
        dw3_ref[1] = _mm_raw(du, hv, True, False).astype(BF16)
        dw3_ref[2] = (_mm_raw(a, dyv, True, False) * 0.5).astype(BF16)

    full = pl.BlockSpec((s, d), lambda j: (0, 0))
    tile = pl.BlockSpec((s, tn), lambda j: (0, j))
    dh, dw3 = pl.pallas_call(
        body, name=f"{tag}_bwd", grid=(f // tn,),
        in_specs=[full, full] + [pl.BlockSpec((None, tn, d), functools.partial(lambda j, k: (k, j, 0), k=k)) for k in range(3)]
        + [tile, tile],
        out_specs=[full, pl.BlockSpec((3, tn, d), lambda j: (0, j, 0))],
        out_shape=[SDS((s, d), F32), SDS((3, f, d), BF16)],
        compiler_params=_params("arbitrary"),
    )(dyb, h, w3, w3, w3, g, u)
    dx, dxb, dgain = _rms_bwd(x, gain, dh, dy, tm=256, name=f"{tag}_drms")
    return dx, dxb, dgain, dw3


def _loss_bwd(y, target, *, tm, name):
    s, d = y.shape

    def body(y_ref, t_ref, dy_ref, dyb_ref, l_ref):
        diff = y_ref[...] - t_ref[...]
        dy_ref[...] = diff * (1.0 / d)
        dyb_ref[...] = (diff * (1.0 / d)).astype(BF16)

        @pl.when(pl.program_id(0) == 0)
        def _():
            l_ref[...] = jnp.zeros_like(l_ref)

        l_ref[...] += 0.5 * jnp.sum(jnp.mean(diff * diff, axis=-1, keepdims=True), axis=0, keepdims=True)

    row = pl.BlockSpec((tm, d), lambda i: (i, 0))
    return pl.pallas_call(
        body, name=name, grid=(s // tm,), in_specs=[row, row],
        out_specs=[row, row, pl.BlockSpec((1, 1), lambda i: (0, 0))],
        out_shape=[SDS((s, d), F32), SDS((s, d), BF16), SDS((1, 1), F32)],
        compiler_params=_params("arbitrary"),
    )(y, target)


def _bucket_table():
    qi = np.arange(BLOCK)[:, None]
    kj = np.arange(2 * BLOCK)[None, :]
    dist = np.maximum(qi + BLOCK - kj, 0)
    max_exact = N_BUCKETS // 2
    d = np.maximum(dist, 1).astype(np.float32)
    large = max_exact + (np.log(d / np.float32(max_exact)) / np.float32(math.log(MAX_DISTANCE / max_exact))
                         * np.float32(N_BUCKETS - max_exact)).astype(np.int32)
    large = np.minimum(large, N_BUCKETS - 1)
    return np.where(dist < max_exact, dist, large).astype(np.int32)


SWA_STACK = SWA_GROUP * BLOCK


def _swa_valid(n):
    qi = lax.broadcasted_iota(jnp.int32, (SWA_STACK, 2 * BLOCK), 0) % BLOCK
    kj = lax.broadcasted_iota(jnp.int32, (SWA_STACK, 2 * BLOCK), 1)
    dist = qi + BLOCK - kj
    return (dist >= 0) & (dist < BLOCK) & ((kj >= BLOCK) | (n > 0))


def _swa_group(q, kb, vb, qg, kg, sink, bias, valid):
    qn = _rms(q, qg)
    kn = _rms(kb, kg)
    s = _mm(qn, kn, False, True) * (HEAD_DIM ** -0.5) + bias
    s = jnp.where(valid, s, -jnp.inf)
    m = lax.stop_gradient(jnp.maximum(jnp.max(s, axis=-1, keepdims=True), sink))
    p = jnp.exp(s - m)
    p = p / (jnp.sum(p, axis=-1, keepdims=True) + jnp.exp(sink - m))
    return _mm(p, vb)


def _swa_bias_table(rb_ref, bucket, bias_s):
    for h in range(SWA_HEADS):
        acc = jnp.zeros((BLOCK, 2 * BLOCK), F32)
        for b in range(N_BUCKETS):
            acc = jnp.where(bucket == b, rb_ref[b, h], acc)
        bias_s[h // SWA_GROUP, (h % SWA_GROUP) * BLOCK:(h % SWA_GROUP + 1) * BLOCK, :] = acc


def _swa_stack(ref, g):
    return jnp.concatenate([ref[:, (g * SWA_GROUP + hh) * HEAD_DIM:(g * SWA_GROUP + hh + 1) * HEAD_DIM]
                            for hh in range(SWA_GROUP)], axis=0)


def _swa_unstack(ref, g, stacked):
    for hh in range(SWA_GROUP):
        h = g * SWA_GROUP + hh
        ref[:, h * HEAD_DIM:(h + 1) * HEAD_DIM] = stacked[hh * BLOCK:(hh + 1) * BLOCK]


def _swa_sink_column(sink_ref, g):
    head = lax.broadcasted_iota(jnp.int32, (SWA_STACK, 1), 0) // BLOCK
    col = jnp.zeros((SWA_STACK, 1), F32)
    for hh in range(SWA_GROUP):
        col = jnp.where(head == hh, sink_ref[g * SWA_GROUP + hh], col)
    return col


def _swa_band(kvp_ref, kvc_ref, g):
    lo = g * HEAD_DIM
    kb = jnp.concatenate([kvp_ref[:, lo:lo + HEAD_DIM], kvc_ref[:, lo:lo + HEAD_DIM]], axis=0)
    lo += SWA_KV_W
    vb = jnp.concatenate([kvp_ref[:, lo:lo + HEAD_DIM], kvc_ref[:, lo:lo + HEAD_DIM]], axis=0)
    return kb, vb


def _swa_specs(order):
    kvc = COL_SKV // (2 * SWA_KV_W)
    return [
        pl.BlockSpec((BLOCK, SWA_Q_W), lambda t: (order(t), 0)),
        pl.BlockSpec((BLOCK, 2 * SWA_KV_W), lambda t: (jnp.maximum(order(t) - 1, 0), kvc)),
        pl.BlockSpec((BLOCK, 2 * SWA_KV_W), lambda t: (order(t), kvc)),
        pl.BlockSpec((1, HEAD_DIM), lambda t: (0, 0)),
        pl.BlockSpec((1, HEAD_DIM), lambda t: (0, 0)),
        pl.BlockSpec(memory_space=pltpu.SMEM),
        pl.BlockSpec(memory_space=pltpu.SMEM),
        pl.BlockSpec((BLOCK, 2 * BLOCK), lambda t: (0, 0)),
    ]


def _swa_fwd(p, qg, kg, sinks, rel_bias, *, name):
    s = p.shape[0]
    nb = s // BLOCK

    def body(q_ref, kvp_ref, kvc_ref, qg_ref, kg_ref, sink_ref, rb_ref, bucket_ref, y_ref, bias_s):
        n = pl.program_id(0)

        @pl.when(n == 0)
        def _():
            _swa_bias_table(rb_ref, bucket_ref[...], bias_s)

        valid = _swa_valid(n)
        for g in range(SWA_KV_HEADS):
            kb, vb = _swa_band(kvp_ref, kvc_ref, g)
            out = _swa_group(_swa_stack(q_ref, g), kb, vb, qg_ref[...], kg_ref[...], _swa_sink_column(sink_ref, g),
                             bias_s[g], valid)
            _swa_unstack(y_ref, g, out)

    return pl.pallas_call(
        body, name=name, grid=(nb,), in_specs=_swa_specs(lambda t: t),
        out_specs=pl.BlockSpec((BLOCK, SWA_Q_W), lambda t: (t, 0)), out_shape=SDS((s, SWA_Q_W), F32),
        scratch_shapes=[pltpu.VMEM((SWA_KV_HEADS, SWA_STACK, 2 * BLOCK), F32)],
        compiler_params=_params("arbitrary"),
    )(p, p, p, qg, kg, sinks, rel_bias, jnp.asarray(_bucket_table()))


def _swa_bwd(p, qg, kg, sinks, rel_bias, dy_all, *, name):
    s = p.shape[0]
    nb = s // BLOCK

    def body(q_ref, kvp_ref, kvc_ref, qg_ref, kg_ref, sink_ref, rb_ref, bucket_ref, dy_ref,
             dq_ref, dkv_ref, dqg_ref, dkg_ref, dsink_ref, drb_ref, bias_s, dbias_s, carry_s):
        t = pl.program_id(0)
        n = nb - 1 - t

        @pl.when(t == 0)
        def _():
            _swa_bias_table(rb_ref, bucket_ref[...], bias_s)
            dbias_s[...] = jnp.zeros_like(dbias_s)
            carry_s[...] = jnp.zeros_like(carry_s)
            dqg_ref[...] = jnp.zeros_like(dqg_ref)
            dkg_ref[...] = jnp.zeros_like(dkg_ref)
            dsink_ref[...] = jnp.zeros_like(dsink_ref)
            drb_ref[...] = jnp.zeros_like(drb_ref)

        valid = _swa_valid(n)
        lane = lax.broadcasted_iota(jnp.int32, (1, BLOCK), 1)
        dqg = jnp.zeros((1, HEAD_DIM), F32)
        dkg = jnp.zeros((1, HEAD_DIM), F32)
        dsink_vec = jnp.zeros((1, BLOCK), F32)
        for g in range(SWA_KV_HEADS):
            kb, vb = _swa_band(kvp_ref, kvc_ref, g)
            _, vjp = jax.vjp(functools.partial(_swa_group, valid=valid), _swa_stack(q_ref, g), kb, vb, qg_ref[...],
                             kg_ref[...], _swa_sink_column(sink_ref, g), bias_s[g])
            dq, dkb, dvb, dqg_g, dkg_g, dsink_col, dbias = vjp(_swa_stack(dy_ref, g))
            _swa_unstack(dq_ref, g, dq)
            dqg += dqg_g
            dkg += dkg_g
            dbias_s[g] += dbias
            for hh in range(SWA_GROUP):
                dsink_h = jnp.sum(dsink_col[hh * BLOCK:(hh + 1) * BLOCK], axis=0, keepdims=True)
                dsink_vec += jnp.where(lane == g * SWA_GROUP + hh, dsink_h, 0.0)
            lo = g * HEAD_DIM
            dkv_ref[:, lo:lo + HEAD_DIM] = dkb[BLOCK:] + carry_s[g]
            carry_s[g] = dkb[:BLOCK]
            lo += SWA_KV_W
            dkv_ref[:, lo:lo + HEAD_DIM] = dvb[BLOCK:] + carry_s[SWA_KV_HEADS + g]
            carry_s[SWA_KV_HEADS + g] = dvb[:BLOCK]
        dqg_ref[...] += dqg
        dkg_ref[...] += dkg
        dsink_ref[...] += dsink_vec

        @pl.when(t == nb - 1)
        def _():
            bucket = bucket_ref[...]
            row = lax.broadcasted_iota(jnp.int32, (N_BUCKETS, BLOCK), 0)
            col = lax.broadcasted_iota(jnp.int32, (N_BUCKETS, BLOCK), 1)
            acc = jnp.zeros((N_BUCKETS, BLOCK), F32)
            for h in range(SWA_HEADS):
                dbias = dbias_s[h // SWA_GROUP, (h % SWA_GROUP) * BLOCK:(h % SWA_GROUP + 1) * BLOCK, :]
                for b in range(N_BUCKETS):
                    part = jnp.sum(jnp.where(bucket == b, dbias, 0.0), axis=1, keepdims=True)
                    val = jnp.sum(part, axis=0, keepdims=True)
                    acc = acc + jnp.where((row == b) & (col == h), val, 0.0)
            drb_ref[...] = acc

    order = lambda t: nb - 1 - t
    vec = pl.BlockSpec((1, HEAD_DIM), lambda t: (0, 0))
    return pl.pallas_call(
        body, name=name, grid=(nb,),
        in_specs=_swa_specs(order) + [pl.BlockSpec((BLOCK, SWA_Q_W), lambda t: (order(t), 0))],
        out_specs=[pl.BlockSpec((BLOCK, SWA_Q_W), lambda t: (order(t), 0)),
                   pl.BlockSpec((BLOCK, 2 * SWA_KV_W), lambda t: (order(t), 0)),
                   vec, vec, pl.BlockSpec((1, BLOCK), lambda t: (0, 0)),
                   pl.BlockSpec((N_BUCKETS, BLOCK), lambda t: (0, 0))],
        out_shape=[SDS((s, SWA_Q_W), F32), SDS((s, 2 * SWA_KV_W), F32), SDS((1, HEAD_DIM), F32),
                   SDS((1, HEAD_DIM), F32), SDS((1, BLOCK), F32), SDS((N_BUCKETS, BLOCK), F32)],
        scratch_shapes=[pltpu.VMEM((SWA_KV_HEADS, SWA_STACK, 2 * BLOCK), F32),
                        pltpu.VMEM((SWA_KV_HEADS, SWA_STACK, 2 * BLOCK), F32),
                        pltpu.VMEM((2 * SWA_KV_HEADS, BLOCK, HEAD_DIM), F32)],
        compiler_params=_params("arbitrary"),
    )(p, p, p, qg, kg, sinks, rel_bias, jnp.asarray(_bucket_table()), dy_all)


def _mem_head(q, k, v, qg, kg):
    qn = _rms(q, qg)
    kn = _rms(k, kg)
    s = _mm(qn, kn, False, True) * (HEAD_DIM ** -0.5)
    m = lax.stop_gradient(jnp.max(s, axis=-1, keepdims=True))
    e = jnp.exp(s - m)
    return _mm(e / jnp.sum(e, axis=-1, keepdims=True), v)


def _mem_fwd(p, kv, qg, kg, *, tq, name):
    s = p.shape[0]
    m = kv.shape[0]

    def body(q_ref, kv_ref, qg_ref, kg_ref, y_ref):
        for h in range(MEM_HEADS):
            cols = slice(h * HEAD_DIM, (h + 1) * HEAD_DIM)
            vcols = slice(MEM_Q_W + h * HEAD_DIM, MEM_Q_W + (h + 1) * HEAD_DIM)
            y_ref[:, cols] = _mem_head(q_ref[:, cols], kv_ref[:, cols], kv_ref[:, vcols], qg_ref[...], kg_ref[...])

    vec = pl.BlockSpec((1, HEAD_DIM), lambda t: (0, 0))
    return pl.pallas_call(
        body, name=name, grid=(s // tq,),
        in_specs=[pl.BlockSpec((tq, MEM_Q_W), lambda t: (t, COL_MQ // MEM_Q_W)),
                  pl.BlockSpec((m, 2 * MEM_Q_W), lambda t: (0, 0)), vec, vec],
        out_specs=pl.BlockSpec((tq, MEM_Q_W), lambda t: (t, 0)), out_shape=SDS((s, MEM_Q_W), F32),
        compiler_params=_params("parallel"),
    )(p, kv, qg, kg)


def _mem_bwd(p, kv, qg, kg, dy_all, *, tq, name):
    s = p.shape[0]
    m = kv.shape[0]

    def body(q_ref, kv_ref, qg_ref, kg_ref, dy_ref, dq_ref, dkv_ref, dqg_ref, dkg_ref):
        @pl.when(pl.program_id(0) == 0)
        def _():
            dkv_ref[...] = jnp.zeros_like(dkv_ref)
            dqg_ref[...] = jnp.zeros_like(dqg_ref)
            dkg_ref[...] = jnp.zeros_like(dkg_ref)

        dqg = jnp.zeros((1, HEAD_DIM), F32)
        dkg = jnp.zeros((1, HEAD_DIM), F32)
        for h in range(MEM_HEADS):
            cols = slice(h * HEAD_DIM, (h + 1) * HEAD_DIM)
            vcols = slice(MEM_Q_W + h * HEAD_DIM, MEM_Q_W + (h + 1) * HEAD_DIM)
            _, vjp = jax.vjp(_mem_head, q_ref[:, cols], kv_ref[:, cols], kv_ref[:, vcols], qg_ref[...], kg_ref[...])
            dq, dk, dv, dqg_h, dkg_h = vjp(dy_ref[:, cols])
            dq_ref[:, cols] = dq
            dkv_ref[:, cols] += dk
            dkv_ref[:, vcols] += dv
            dqg += dqg_h
            dkg += dkg_h
        dqg_ref[...] += dqg
        dkg_ref[...] += dkg

    vec = pl.BlockSpec((1, HEAD_DIM), lambda t: (0, 0))
    full = pl.BlockSpec((m, 2 * MEM_Q_W), lambda t: (0, 0))
    dy_col = (SWA_Q_W + GLA_V_W) // MEM_Q_W
    return pl.pallas_call(
        body, name=name, grid=(s // tq,),
        in_specs=[pl.BlockSpec((tq, MEM_Q_W), lambda t: (t, COL_MQ // MEM_Q_W)), full, vec, vec,
                  pl.BlockSpec((tq, MEM_Q_W), lambda t: (t, dy_col))],
        out_specs=[pl.BlockSpec((tq, MEM_Q_W), lambda t: (t, 0)), full, vec, vec],
        out_shape=[SDS((s, MEM_Q_W), F32), SDS((m, 2 * MEM_Q_W), F32), SDS((1, HEAD_DIM), F32), SDS((1, HEAD_DIM), F32)],
        compiler_params=_params("arbitrary"),
    )(p, kv, qg, kg, dy_all)


GLA_ROWS = 256


def _gla_consts():
    c, h = GLA_CHUNK, GLA_HEADS
    i2 = lax.broadcasted_iota(jnp.int32, (c, c), 0)
    j2 = lax.broadcasted_iota(jnp.int32, (c, c), 1)
    slab_q = lax.broadcasted_iota(jnp.int32, (h, c, GLA_QK_W), 0)
    lane_q = lax.broadcasted_iota(jnp.int32, (h, c, GLA_QK_W), 2)
    row_a = lax.broadcasted_iota(jnp.int32, (h * c, c), 0)
    col_a = lax.broadcasted_iota(jnp.int32, (h * c, c), 1)
    slab_o = lax.broadcasted_iota(jnp.int32, (h, c, GLA_V_W), 0)
    lane_o = lax.broadcasted_iota(jnp.int32, (h, c, GLA_V_W), 2)
    row_s = lax.broadcasted_iota(jnp.int32, (GLA_V_W, GLA_QK_W), 0)
    col_s = lax.broadcasted_iota(jnp.int32, (GLA_V_W, GLA_QK_W), 1)
    return dict(
        ltri=(j2 <= i2).astype(F32),
        m_q=(slab_q == lane_q // GLA_DK).astype(F32),
        causal=col_a <= row_a % c,
        m_o=(slab_o == lane_o // GLA_DV).astype(F32),
        m_s=(row_s // GLA_DV == col_s // GLA_DK).astype(F32),
    )


def _gla_chunk(q, k, v, z, bg, st, c):
    h, n = GLA_HEADS, GLA_CHUNK
    la = _log_sigmoid(z + bg) * (1.0 / GLA_TAU)
    b = _mmf(c["ltri"], la)
    bl = jnp.sum(la, axis=0, keepdims=True)
    qs = q * (GLA_DK ** -0.5)
    kt = k * jnp.exp(bl - b)
    qt = qs * jnp.exp(b - bl)
    qe = qs * jnp.exp(b)
    q_stack = (jnp.broadcast_to(qt[None], (h, n, GLA_QK_W)) * c["m_q"]).reshape(h * n, GLA_QK_W)
    a = jnp.where(c["causal"], _mmf(q_stack, kt, False, True), 0.0)
    o_stack = _mm(a, v)
    o_intra = jnp.sum(o_stack.reshape(h, n, GLA_V_W) * c["m_o"], axis=0)
    o_inter = _mm(qe, st, False, True)
    st_next = st * jnp.exp(bl) + _mm(v, kt, True, False) * c["m_s"]
    return o_intra + o_inter, st_next


def _gla_post(o, gg, gain, g64):
    ms = _mmf(o * o, g64) * (1.0 / GLA_DV)
    return o * lax.rsqrt(ms + EPS) * gain * jax.nn.silu(gg)


def _gla_g64():
    r = lax.broadcasted_iota(jnp.int32, (GLA_V_W, GLA_V_W), 0)
    c = lax.broadcasted_iota(jnp.int32, (GLA_V_W, GLA_V_W), 1)
    return (r // GLA_DV == c // GLA_DV).astype(F32)


def _gla_in_specs(order):
    r = GLA_ROWS
    return [
        pl.BlockSpec((r, GLA_QK_W), lambda t: (order(t), COL_GQ // GLA_QK_W)),
        pl.BlockSpec((r, GLA_QK_W), lambda t: (order(t), COL_GK // GLA_QK_W)),
        pl.BlockSpec((r, GLA_V_W), lambda t: (order(t), COL_GV // GLA_V_W)),
        pl.BlockSpec((r, GLA_V_W), lambda t: (order(t), COL_GG // GLA_V_W)),
        pl.BlockSpec((r, GLA_QK_W), lambda t: (order(t), 0)),
        pl.BlockSpec((1, GLA_QK_W), lambda t: (0, 0)),
        pl.BlockSpec((1, GLA_V_W), lambda t: (0, 0)),
    ]


def _gla_fwd(p, z, bg, gain, *, name):
    s = p.shape[0]
    r = GLA_ROWS
    cps = r // GLA_CHUNK

    def body(q_ref, k_ref, v_ref, gg_ref, z_ref, bg_ref, gain_ref, y_ref, oraw_ref, stsave_ref, st_s):
        @pl.when(pl.program_id(0) == 0)
        def _():
            st_s[...] = jnp.zeros_like(st_s)

        c = _gla_consts()
        st = st_s[...]
        for ci in range(cps):
            rows = slice(ci * GLA_CHUNK, (ci + 1) * GLA_CHUNK)
            stsave_ref[ci] = st
            o, st = _gla_chunk(q_ref[rows, :], k_ref[rows, :], v_ref[rows, :], z_ref[rows, :], bg_ref[...], st, c)
            oraw_ref[rows, :] = o
        st_s[...] = st
        y_ref[...] = _gla_post(oraw_ref[...], gg_ref[...], gain_ref[...], _gla_g64())

    rowv = pl.BlockSpec((r, GLA_V_W), lambda t: (t, 0))
    return pl.pallas_call(
        body, name=name, grid=(s // r,), in_specs=_gla_in_specs(lambda t: t),
        out_specs=[rowv, rowv, pl.BlockSpec((cps, GLA_V_W, GLA_QK_W), lambda t: (t, 0, 0))],
        out_shape=[SDS((s, GLA_V_W), F32), SDS((s, GLA_V_W), F32), SDS((s // GLA_CHUNK, GLA_V_W, GLA_QK_W), F32)],
        scratch_shapes=[pltpu.VMEM((GLA_V_W, GLA_QK_W), F32)],
        compiler_params=_params("arbitrary"),
    )(p, p, p, p, z, bg, gain)


def _gla_bwd(p, z, bg, gain, oraw, stsave, dy_all, *, name):
    s = p.shape[0]
    r = GLA_ROWS
    cps = r // GLA_CHUNK
    nsteps = s // r
    w_qkvg = 2 * GLA_QK_W + 2 * GLA_V_W

    def body(q_ref, k_ref, v_ref, gg_ref, z_ref, bg_ref, gain_ref, oraw_ref, stsave_ref, dy_ref,
             dqkvg_ref, dz_ref, dbg_ref, dgain_ref, dst_s):
        @pl.when(pl.program_id(0) == 0)
        def _():
            dst_s[...] = jnp.zeros_like(dst_s)
            dbg_ref[...] = jnp.zeros_like(dbg_ref)
            dgain_ref[...] = jnp.zeros_like(dgain_ref)

        c = _gla_consts()
        _, vjp = jax.vjp(functools.partial(_gla_post, g64=_gla_g64()), oraw_ref[...], gg_ref[...], gain_ref[...])
        do, dgg, dgain = vjp(dy_ref[...])
        dqkvg_ref[:, 2 * GLA_QK_W + GLA_V_W:] = dgg
        dgain_ref[...] += dgain
        dst = dst_s[...]
        dbg = jnp.zeros((1, GLA_QK_W), F32)
        for ci in reversed(range(cps)):
            rows = slice(ci * GLA_CHUNK, (ci + 1) * GLA_CHUNK)
            _, vjp = jax.vjp(functools.partial(_gla_chunk, c=c), q_ref[rows, :], k_ref[rows, :], v_ref[rows, :],
                             z_ref[rows, :], bg_ref[...], stsave_ref[ci])
            dq, dk, dv, dz, dbg_c, dst = vjp((do[rows, :], dst))
            dqkvg_ref[rows, 0:GLA_QK_W] = dq
            dqkvg_ref[rows, GLA_QK_W:2 * GLA_QK_W] = dk
            dqkvg_ref[rows, 2 * GLA_QK_W:2 * GLA_QK_W + GLA_V_W] = dv
            dz_ref[rows, :] = dz
            dbg += dbg_c
        dst_s[...] = dst
        dbg_ref[...] += dbg

    order = lambda t: nsteps - 1 - t
    rowv = pl.BlockSpec((r, GLA_V_W), lambda t: (order(t), 0))
    return pl.pallas_call(
        body, name=name, grid=(nsteps,),
        in_specs=_gla_in_specs(order) + [
            rowv, pl.BlockSpec((cps, GLA_V_W, GLA_QK_W), lambda t: (order(t), 0, 0)),
            pl.BlockSpec((r, GLA_V_W), lambda t: (order(t), SWA_Q_W // GLA_V_W))],
        out_specs=[pl.BlockSpec((r, w_qkvg), lambda t: (order(t), 0)), pl.BlockSpec((r, GLA_QK_W), lambda t: (order(t), 0)),
                   pl.BlockSpec((1, GLA_QK_W), lambda t: (0, 0)), pl.BlockSpec((1, GLA_V_W), lambda t: (0, 0))],
        out_shape=[SDS((s, w_qkvg), F32), SDS((s, GLA_QK_W), F32), SDS((1, GLA_QK_W), F32), SDS((1, GLA_V_W), F32)],
        scratch_shapes=[pltpu.VMEM((GLA_V_W, GLA_QK_W), F32)],
        compiler_params=_params("arbitrary"),
    )(p, p, p, p, z, bg, gain, oraw, stsave, dy_all)


def _local_step(x, mem, target, small, big, on_grads):
    g1, gmix, gmem, g2, sqg, skg, sinks, rel_bias, wgu, bg, gla_gain, mqg, mkg = small
    w3_1, win_p, wkv, wout, gather_ffn2 = big
    wgu_pad = jnp.zeros((GLA_QK_W, GLA_QK_W), BF16).at[:GLA_RANK].set(wgu.astype(BF16))
    gain256 = jnp.tile(gla_gain, (1, GLA_HEADS))

    x1, saved1 = _ffn_fwd(x, g1, w3_1, "ffn1")
    w3_2 = gather_ffn2(x1)
    h = _rms_fwd(x1, gmix, tm=256, name="mix_rms")
    p = _matmul(h, win_p, tm=1024, tn=IN_W_PAD, name="mix_in")
    hm = _rms_fwd(mem, gmem, tm=256, name="mem_rms")
    kv = _matmul(hm, wkv, tm=256, tn=512, name="mem_kv")
    p_glr = p[:, COL_GLR:]
    z = _matmul(p_glr, wgu_pad, tm=1024, tn=GLA_QK_W, name="gla_gate")
    y_swa = _swa_fwd(p, sqg, skg, sinks, rel_bias, name="swa_fwd")
    y_gla, oraw, stsave = _gla_fwd(p, z, bg, gain256, name="gla_fwd")
    y_mem = _mem_fwd(p, kv, mqg, mkg, tq=512, name="mem_fwd")
    x2 = _matmul([y_swa, y_gla, y_mem], wout, b_blocks=[0, 2, 3], tm=1024, tn=1024, res=x1, name="mix_out")
    x3, saved2 = _ffn_fwd(x2, g2, w3_2, "ffn2")

    dy, dyb, loss = _loss_bwd(x3, target, tm=256, name="loss")
    dx2, dx2b, dg2, dw3_2 = _ffn_bwd(dy, dyb, x2, g2, w3_2, saved2, "ffn2")
    dx2b = on_grads("ffn2", [dw3_2], dx2b)
    dy_all = _matmul(dx2b, wout, tb=True, tm=1024, tn=1024, name="mix_dy")
    dwout = _matmul(jnp.concatenate([y_swa, y_gla, y_mem], axis=1), dx2b, ta=True, tm=512, tn=1024, out_dtype=BF16,
                    name="mix_dw_out")
    dq_swa, dkv_swa, dsqg, dskg, dsink, drb = _swa_bwd(p, sqg, skg, sinks, rel_bias, dy_all, name="swa_bwd")
    dqkvg, dz, dbg, dgain256 = _gla_bwd(p, z, bg, gain256, oraw, stsave, dy_all, name="gla_bwd")
    dmq, dkv_mem, dmqg, dmkg = _mem_bwd(p, kv, mqg, mkg, dy_all, tq=512, name="mem_bwd")
    dglr = _matmul(dz, wgu_pad, tb=True, tm=1024, tn=GLA_QK_W, name="gla_gate_dx")
    dwgu_pad = _matmul(p_glr, dz, ta=True, tm=GLA_QK_W, tn=GLA_QK_W, name="gla_gate_dw")
    dp = jnp.concatenate([dq_swa, dkv_swa, dqkvg, dmq, dglr], axis=1)
    dh = _matmul(dp, win_p, tb=True, tm=1024, tn=1024, name="mix_dh")
    dwin_p = _matmul(h, dp, ta=True, tm=1024, tn=640, out_dtype=BF16, name="mix_dw_in")
    dx1, dx1b, dgmix = _rms_bwd(x1, gmix, dh, dx2, tm=256, name="mix_drms")
    dwkv = _matmul(hm, dkv_mem, ta=True, tm=512, tn=512, out_dtype=BF16, name="mem_dw_kv")
    dx1b = on_grads("mix", (dwin_p, dwkv, dwout), dx1b)
    dhm = _matmul(dkv_mem, wkv, tb=True, tm=256, tn=512, name="mem_dh")
    _, _, dgmem = _rms_bwd(mem, gmem, dhm, None, tm=256, name="mem_drms")
    dx, _, dg1, dw3_1 = _ffn_bwd(dx1, dx1b, x, g1, w3_1, saved1, "ffn1")

    dgla_gain = dgain256.reshape(GLA_HEADS, GLA_DV).sum(axis=0, keepdims=True)
    dsmall = (dg1, dgmix, dgmem, dg2, dsqg, dskg, dsink[0, :SWA_HEADS], drb[:, :SWA_HEADS], dwgu_pad[:GLA_RANK], dbg,
              dgla_gain, dmqg, dmkg)
    on_grads("ffn1", [dw3_1], None, small=list(dsmall) + [loss])
    return dx


def _mesh_place():
    x, y, c = lax.axis_index("x"), lax.axis_index("y"), lax.axis_index("c")
    other_chips = [(1 - x, y), (x, 1 - y), (1 - x, 1 - y)]
    return x, y, c, other_chips


def _handshake(peers):
    barrier = pltpu.get_barrier_semaphore()
    for peer in peers:
        pl.semaphore_signal(barrier, inc=1, device_id=peer, device_id_type=MESH)
    pl.semaphore_wait(barrier, len(peers))


def _sequencer_call(body, operands, out_shapes, sems, *, name, collective_id):
    return pl.kernel(
        body, name=name, out_type=out_shapes, mesh=plsc.ScalarSubcoreMesh(axis_name="sequencer", num_cores=1),
        scratch_types=sems, compiler_params=pltpu.CompilerParams(collective_id=collective_id),
    )(*operands)


def _window(ref, kind, slot, shape):
    if kind == "row":
        rows = pl.ds(pl.multiple_of(slot * shape[-2], 8), shape[-2])
        return ref.at[rows] if len(shape) == 2 else ref.at[:, rows]
    return ref.at[slot]


def _gathered(shape, kind):
    if kind == "row":
        return tuple(shape[:-2]) + (N_DEV * shape[-2], shape[-1])
    return (N_DEV,) + tuple(shape)


def _all_gather(shards, kinds, *, name, collective_id):
    nt = len(shards)

    def body(*refs):
        x_refs, o_refs = refs[:nt], refs[nt:2 * nt]
        send_sems, recv_sems, local_sems = refs[2 * nt:]
        x, y, c, chips = _mesh_place()
        me, sibling = (x, y, c), (x, y, 1 - c)
        _handshake([sibling] + [(*chip, c) for chip in chips])

        def copy(k, t, block, to, from_shard=False):
            bx, by, bc = block
            rows = _window(o_refs[t], kinds[t], 4 * bx + 2 * by + bc, shards[t].shape)
            return pltpu.make_async_remote_copy(
                src_ref=x_refs[t] if from_shard else rows, dst_ref=rows,
                send_sem=send_sems.at[k, t], recv_sem=recv_sems.at[k, t], device_id=to, device_id_type=MESH)

        mine = [pltpu.make_async_copy(x_refs[t], _window(o_refs[t], kinds[t], 4 * x + 2 * y + c, shards[t].shape),
                                      local_sems.at[t]) for t in range(nt)]
        for cp in mine:
            cp.start()
        first = [copy(0, t, me, sibling, True) for t in range(nt)]
        first += [copy(1 + j, t, me, (*chip, c), True) for j, chip in enumerate(chips) for t in range(nt)]
        for cp in first:
            cp.start()
        passed = []
        for j, chip in enumerate(chips):
            for t in range(nt):
                copy(1 + j, t, (*chip, c), me).wait_recv()
                fwd = copy(4 + j, t, (*chip, c), sibling)
                fwd.start()
                passed.append(fwd)
        for t in range(nt):
            copy(0, t, sibling, me).wait_recv()
        for j, chip in enumerate(chips):
            for t in range(nt):
                copy(4 + j, t, (*chip, 1 - c), me).wait_recv()
        for cp in first + passed:
            cp.wait_send()
        for cp in mine:
            cp.wait()

    return _sequencer_call(
        body, shards, [SDS(_gathered(s.shape, k), s.dtype) for s, k in zip(shards, kinds)],
        [pltpu.SemaphoreType.DMA((7, nt)), pltpu.SemaphoreType.DMA((7, nt)), pltpu.SemaphoreType.DMA((nt,))],
        name=name, collective_id=collective_id)


def _part_shape(shape, kind):
    if kind == "row":
        return tuple(shape[:-2]) + (shape[-2] // N_DEV, shape[-1])
    return tuple(shape[2:])


def _pair_exchange(grads, kinds, *, name, collective_id):
    nt = len(grads)
    part = [_part_shape(g.shape, k) for g, k in zip(grads, kinds)]

    def body(*refs):
        g_refs, o_refs = refs[:nt], refs[nt:2 * nt]
        send_sems, recv_sems = refs[2 * nt:]
        x, y, c, _ = _mesh_place()
        _handshake([(x, y, 1 - c)])
        copies = []
        for t in range(nt):
            for xy in range(4):
                src = g_refs[t].at[1 - c, xy] if kinds[t] == "stack" else _window(g_refs[t], kinds[t], 2 * xy + 1 - c, part[t])
                copies.append(pltpu.make_async_remote_copy(
                    src_ref=src, dst_ref=o_refs[t].at[xy], send_sem=send_sems.at[xy, t], recv_sem=recv_sems.at[xy, t],
                    device_id=(x, y, 1 - c), device_id_type=MESH))
        for cp in copies:
            cp.start()
        for cp in copies:
            cp.wait()

    return _sequencer_call(
        body, grads, [SDS((4,) + p, g.dtype) for p, g in zip(part, grads)],
        [pltpu.SemaphoreType.DMA((4, nt)), pltpu.SemaphoreType.DMA((4, nt))], name=name, collective_id=collective_id)


def _chip_exchange(parts, small, *, name, collective_id):
    nt = len(parts)
    if small is None:
        def body_plain(*refs):
            s_refs, o_refs = refs[:nt], refs[nt:2 * nt]
            send_sems, recv_sems = refs[2 * nt:]
            x, y, c, chips = _mesh_place()
            _handshake([(*chip, c) for chip in chips])
            copies = [pltpu.make_async_remote_copy(
                src_ref=s_refs[t].at[2 * chip[0] + chip[1]], dst_ref=o_refs[t].at[j],
                send_sem=send_sems.at[j, t], recv_sem=recv_sems.at[j, t], device_id=(*chip, c), device_id_type=MESH)
                for j, chip in enumerate(chips) for t in range(nt)]
            for cp in copies:
                cp.start()
            for cp in copies:
                cp.wait()

        return _sequencer_call(
            body_plain, parts, [SDS((3,) + s.shape[1:], s.dtype) for s in parts],
            [pltpu.SemaphoreType.DMA((3, nt)), pltpu.SemaphoreType.DMA((3, nt))], name=name, collective_id=collective_id)

    def body(*refs):
        s_refs, small_ref = refs[:nt], refs[nt]
        o_refs, small_all = refs[nt + 1:2 * nt + 1], refs[2 * nt + 1]
        send_sems, recv_sems, small_send, small_recv, local_sem = refs[2 * nt + 2:]
        x, y, c, chips = _mesh_place()
        _handshake([(px, py, pc) for px in (x, 1 - x) for py in (y, 1 - y) for pc in (c, 1 - c)][1:])

        def copy(j, t, chip):
            return pltpu.make_async_remote_copy(
                src_ref=s_refs[t].at[2 * chip[0] + chip[1]], dst_ref=o_refs[t].at[j],
                send_sem=send_sems.at[j, t], recv_sem=recv_sems.at[j, t], device_id=(*chip, c), device_id_type=MESH)

        flips = [(fx, fy, fc) for fx in (0, 1) for fy in (0, 1) for fc in (0, 1)][1:]

        def small_copy(k):
            fx, fy, fc = flips[k]
            to = (x ^ fx if fx else x, y ^ fy if fy else y, c ^ fc if fc else c)
            rows = small_all.at[4 * x + 2 * y + c]
            return pltpu.make_async_remote_copy(
                src_ref=small_ref, dst_ref=rows, send_sem=small_send.at[k], recv_sem=small_recv.at[k],
                device_id=to, device_id_type=MESH)

        own = pltpu.make_async_copy(small_ref, small_all.at[4 * x + 2 * y + c], local_sem)
        own.start()
        copies = [copy(j, t, chip) for j, chip in enumerate(chips) for t in range(nt)]
        smalls = [small_copy(k) for k in range(7)]
        for cp in smalls + copies:
            cp.start()
        for cp in smalls + copies:
            cp.wait()
        own.wait()

    return _sequencer_call(
        body, list(parts) + [small],
        [SDS((3,) + s.shape[1:], s.dtype) for s in parts] + [SDS((N_DEV,) + small.shape, small.dtype)],
        [pltpu.SemaphoreType.DMA((3, nt)), pltpu.SemaphoreType.DMA((3, nt)),
         pltpu.SemaphoreType.DMA((7,)), pltpu.SemaphoreType.DMA((7,)), pltpu.SemaphoreType.DMA],
        name=name, collective_id=collective_id)


def _pair_sum(grad, theirs, kind, c, *, name):
    if kind == "row":
        if theirs.ndim == 3:
            grad, theirs = grad[None], theirs[:, None]
        n, r, l = theirs.shape[1:]
        mine_spec = pl.BlockSpec((None, r, l), lambda xy, m, c_ref: (m, 2 * xy + c_ref[0], 0))
    else:
        r, l = theirs.shape[-2:]
        n = theirs.size // (4 * r * l)
        theirs = theirs.reshape(4, n, r, l)
        grad = grad.reshape(2, 4, n, r, l)
        mine_spec = pl.BlockSpec((None, None, None, r, l), lambda xy, m, c_ref: (c_ref[0], xy, m, 0, 0))

    def body(c_ref, a_ref, b_ref, o_ref):
        o_ref[...] = (a_ref[...].astype(F32) + b_ref[...].astype(F32)).astype(BF16)

    part = pl.BlockSpec((None, None, r, l), lambda xy, m, c_ref: (xy, m, 0, 0))
    return pl.pallas_call(
        body, name=name,
        grid_spec=pltpu.PrefetchScalarGridSpec(num_scalar_prefetch=1, grid=(4, n), in_specs=[mine_spec, part], out_specs=part),
        out_shape=SDS((4, n, r, l), BF16), compiler_params=_params("parallel", "parallel"),
    )(c, grad, theirs)


def _adamw(w, g, m, v):
    m = ADAM_B1 * m + (1.0 - ADAM_B1) * g
    v = ADAM_B2 * v + (1.0 - ADAM_B2) * jnp.square(g)
    m_hat = m / (1.0 - ADAM_B1 ** ADAM_STEP)
    v_hat = v / (1.0 - ADAM_B2 ** ADAM_STEP)
    delta = -ADAM_LR * (m_hat / (jnp.sqrt(v_hat) + ADAM_EPS) + ADAM_WD * w)
    return delta, m, v


def _adam_big(own, others, mat, xy, w, m, v, *, tr, name):
    _, r, l = w.shape
    lp = own.shape[-1]

    def body(xy_ref, own_ref, oth_ref, w_ref, m_ref, v_ref, g_out, d_out, m_out, v_out):
        g = own_ref[0, 0].astype(F32)
        for j in range(3):
            g = g + oth_ref[j, 0].astype(F32)
        g = g[:, :l]
        delta, m_new, v_new = _adamw(w_ref[0], g, m_ref[0], v_ref[0])
        g_out[0] = g
        d_out[0] = delta
        m_out[0] = m_new
        v_out[0] = v_new

    blk = pl.BlockSpec((1, tr, l), lambda i, xy_ref: (0, i, 0))
    return pl.pallas_call(
        body, name=name,
        grid_spec=pltpu.PrefetchScalarGridSpec(
            num_scalar_prefetch=1, grid=(r // tr,),
            in_specs=[pl.BlockSpec((1, 1, tr, lp), lambda i, xy_ref: (xy_ref[0], mat, i, 0)),
                      pl.BlockSpec((3, 1, tr, lp), lambda i, xy_ref: (0, mat, i, 0)), blk, blk, blk],
            out_specs=[blk, blk, blk, blk]),
        out_shape=[SDS(w.shape, F32)] * 4, compiler_params=_params("parallel"),
    )(xy, own, others, w, m, v)


def _adam_small(g_all, w, m, v, *, name):
    def body(g_ref, w_ref, m_ref, v_ref, g_out, d_out, m_out, v_out):
        g = g_ref[0]
        for k in range(1, N_DEV):
            g = g + g_ref[k]
        delta, m_new, v_new = _adamw(w_ref[...], g, m_ref[...], v_ref[...])
        g_out[...] = g
        d_out[...] = delta
        m_out[...] = m_new
        v_out[...] = v_new

    return pl.pallas_call(body, name=name, out_shape=[SDS(w.shape, F32)] * 4)(g_all, w, m, v)


SMALL_ROWS = 56


def _pack_small(parts):
    flat = jnp.concatenate([a.reshape(-1) for a in parts])
    return jnp.pad(flat, (0, SMALL_ROWS * 128 - flat.shape[0])).reshape(SMALL_ROWS, 128)


def _unpack_small(packed, shapes):
    flat = packed.reshape(-1)
    out, at = [], 0
    for s in shapes:
        n = math.prod(s)
        out.append(flat[at:at + n].reshape(s))
        at += n
    return out


def kernel(x, mem, ffn1_norm, ffn1_w_gate, ffn1_w_up, ffn1_w_down, mix_norm, mem_norm, w_in, w_mem_kv, swa_q_norm, swa_k_norm, swa_sinks, rel_bias, gla_w_gate_up, gla_b_gate, gla_out_norm, mem_q_norm, mem_k_norm, w_out, ffn2_norm, ffn2_w_gate, ffn2_w_up, ffn2_w_down, loss_target, m_ffn1_norm, m_ffn1_w_gate, m_ffn1_w_up, m_ffn1_w_down, m_mix_norm, m_mem_norm, m_w_in, m_w_mem_kv, m_swa_q_norm, m_swa_k_norm, m_swa_sinks, m_rel_bias, m_gla_w_gate_up, m_gla_b_gate, m_gla_out_norm, m_mem_q_norm, m_mem_k_norm, m_w_out, m_ffn2_norm, m_ffn2_w_gate, m_ffn2_w_up, m_ffn2_w_down, v_ffn1_norm, v_ffn1_w_gate, v_ffn1_w_up, v_ffn1_w_down, v_mix_norm, v_mem_norm, v_w_in, v_w_mem_kv, v_swa_q_norm, v_swa_k_norm, v_swa_sinks, v_rel_bias, v_gla_w_gate_up, v_gla_b_gate, v_gla_out_norm, v_mem_q_norm, v_mem_k_norm, v_w_out, v_ffn2_norm, v_ffn2_w_gate, v_ffn2_w_up, v_ffn2_w_down):
    xi, yi, ci = lax.axis_index("x"), lax.axis_index("y"), lax.axis_index("c")
    c_arr = jnp.reshape(ci, (1,)).astype(jnp.int32)
    xy_arr = jnp.reshape(2 * xi + yi, (1,)).astype(jnp.int32)
    d = x.shape[-1]

    pad_h = FFN_SHARD_PAD - ffn1_w_gate.shape[-1]

    def gather_ffn(wg_s, wu_s, wd_s, name, collective_id, after):
        w3_s = jnp.concatenate([wg_s.transpose(0, 2, 1), wu_s.transpose(0, 2, 1), wd_s], axis=0)
        w3_s = jnp.pad(w3_s, ((0, 0), (0, pad_h), (0, 0))).astype(BF16)
        if after is not None:
            w3_s, _ = lax.optimization_barrier((w3_s, after))
        return _all_gather([w3_s], ["row"], name=name, collective_id=collective_id)[0]

    w3_1 = gather_ffn(ffn1_w_gate, ffn1_w_up, ffn1_w_down, "gather_ffn1", 0, None)
    mix_s = lax.optimization_barrier((w_in[0].astype(BF16), w_mem_kv[0].astype(BF16), w_out[0].astype(BF16), w3_1))[:3]
    win_all, wkv, wout = _all_gather(list(mix_s), ["stack", "row", "row"], name="gather_mix", collective_id=1)

    def gather_ffn2(x1):
        return gather_ffn(ffn2_w_gate, ffn2_w_up, ffn2_w_down, "gather_ffn2", 2, (wout, x1))

    win_p = _pack_win(win_all, tr=256, name="pack_w_in")

    small_w = [ffn1_norm, mix_norm, mem_norm, ffn2_norm, swa_q_norm, swa_k_norm, swa_sinks[0], rel_bias,
               gla_w_gate_up[0], gla_b_gate, gla_out_norm, mem_q_norm, mem_k_norm]
    collective_ids = {"ffn2": (3, 4), "mix": (5, 6), "ffn1": (7, 8)}
    reduced, small_box = {}, []

    def on_grads(group, grads, carry, small=None):
        if group == "mix":
            dwin_p, dwkv, dwout = grads
            grads = [_unpack_win(dwin_p, tr=256, name="unpack_dw_in"), dwkv, dwout]
            kinds = ["stack", "row", "row"]
        else:
            kinds = ["row"]
        if reduced:
            earlier = list(reduced.values())[-1][1]
            *grads, _ = lax.optimization_barrier((*grads, earlier[0]))
        id_pair, id_chip = collective_ids[group]
        from_sibling = _pair_exchange(grads, kinds, name=f"pair_exchange_{group}", collective_id=id_pair)
        chip_sums = [_pair_sum(g, theirs, k, c_arr, name=f"pair_sum_{group}_{t}")
                     for t, (g, theirs, k) in enumerate(zip(grads, from_sibling, kinds))]
        if carry is not None:
            *chip_sums, carry = lax.optimization_barrier((*chip_sums, carry))
        if small is None:
            from_chips = _chip_exchange(chip_sums, None, name=f"chip_exchange_{group}", collective_id=id_chip)
        else:
            *from_chips, small_all = _chip_exchange(chip_sums, _pack_small(small), name=f"chip_exchange_{group}",
                                                    collective_id=id_chip)
            small_box.append(small_all)
        reduced[group] = (chip_sums, from_chips)
        return carry

    grad_x = _local_step(x[0], mem[0], loss_target[0], small_w, (w3_1, win_p, wkv, wout, gather_ffn2), on_grads)
    small_all = small_box[0]

    big_w = {"ffn1_w_gate": ("ffn1", 0, 0, True, ffn1_w_gate, m_ffn1_w_gate, v_ffn1_w_gate),
             "ffn1_w_up": ("ffn1", 0, 1, True, ffn1_w_up, m_ffn1_w_up, v_ffn1_w_up),
             "ffn1_w_down": ("ffn1", 0, 2, False, ffn1_w_down, m_ffn1_w_down, v_ffn1_w_down),
             "w_in": ("mix", 0, 0, False, w_in, m_w_in, v_w_in),
             "w_mem_kv": ("mix", 1, 0, False, w_mem_kv, m_w_mem_kv, v_w_mem_kv),
             "w_out": ("mix", 2, 0, False, w_out, m_w_out, v_w_out),
             "ffn2_w_gate": ("ffn2", 0, 0, True, ffn2_w_gate, m_ffn2_w_gate, v_ffn2_w_gate),
             "ffn2_w_up": ("ffn2", 0, 1, True, ffn2_w_up, m_ffn2_w_up, v_ffn2_w_up),
             "ffn2_w_down": ("ffn2", 0, 2, False, ffn2_w_down, m_ffn2_w_down, v_ffn2_w_down)}
    res = {}
    for nm, (group, t, mat, transposed, w, m, v) in big_w.items():
        if transposed:
            w, m, v = (a.transpose(0, 2, 1) for a in (w, m, v))
        r = w.shape[1]
        tr = 256 if r % 256 == 0 else r
        chip_sums, from_chips = reduced[group]
        out = _adam_big(chip_sums[t], from_chips[t], mat, xy_arr, w, m, v, tr=tr, name=f"adam_{nm}")
        res[nm] = [a.transpose(0, 2, 1) for a in out] if transposed else out
    small_names = ["ffn1_norm", "mix_norm", "mem_norm", "ffn2_norm", "swa_q_norm", "swa_k_norm", "swa_sinks", "rel_bias",
                   "gla_w_gate_up", "gla_b_gate", "gla_out_norm", "mem_q_norm", "mem_k_norm"]
    small_m = [m_ffn1_norm, m_mix_norm, m_mem_norm, m_ffn2_norm, m_swa_q_norm, m_swa_k_norm, m_swa_sinks, m_rel_bias,
               m_gla_w_gate_up, m_gla_b_gate, m_gla_out_norm, m_mem_q_norm, m_mem_k_norm]
    small_v = [v_ffn1_norm, v_mix_norm, v_mem_norm, v_ffn2_norm, v_swa_q_norm, v_swa_k_norm, v_swa_sinks, v_rel_bias,
               v_gla_w_gate_up, v_gla_b_gate, v_gla_out_norm, v_mem_q_norm, v_mem_k_norm]
    small_full = [ffn1_norm, mix_norm, mem_norm, ffn2_norm, swa_q_norm, swa_k_norm, swa_sinks, rel_bias,
                  gla_w_gate_up, gla_b_gate, gla_out_norm, mem_q_norm, mem_k_norm]
    packed = _adam_small(small_all, _pack_small(small_full), _pack_small(small_m), _pack_small(small_v), name="adam_small")
    full_shapes = [a.shape for a in small_full]
    unpacked = [_unpack_small(pk, full_shapes + [()]) for pk in packed]
    for k, nm in enumerate(small_names):
        res[nm] = [unpacked[q][k] for q in range(4)]
    loss = unpacked[0][len(small_names)]

    order = ["ffn1_norm", "ffn1_w_gate", "ffn1_w_up", "ffn1_w_down", "mix_norm", "mem_norm", "w_in", "w_mem_kv",
             "swa_q_norm", "swa_k_norm", "swa_sinks", "rel_bias", "gla_w_gate_up", "gla_b_gate", "gla_out_norm",
             "mem_q_norm", "mem_k_norm", "w_out", "ffn2_norm", "ffn2_w_gate", "ffn2_w_up", "ffn2_w_down"]
    outs = [loss, grad_x[None]]
    for q in range(4):
        outs += [res[nm][q] for nm in order]
    return tuple(outs)
```

```python
import functools
import math

import numpy as np
import jax
import jax.numpy as jnp
from jax import lax
from jax.experimental import pallas as pl
from jax.experimental.pallas import tpu as pltpu
from jax.experimental.pallas import tpu_sc as plsc

F32 = jnp.float32
BF16 = jnp.bfloat16
SDS = jax.ShapeDtypeStruct

EPS = 1e-6
HEAD_DIM = 64
SWA_HEADS = 8
SWA_KV_HEADS = 2
SWA_GROUP = SWA_HEADS // SWA_KV_HEADS
BLOCK = 128
N_BUCKETS = 32
MAX_DISTANCE = 128
GLA_HEADS = 4
GLA_DK = 32
GLA_DV = 64
GLA_RANK = 16
GLA_TAU = 16.0
GLA_CHUNK = 32
MEM_HEADS = 4
SWA_Q_W = SWA_HEADS * HEAD_DIM
SWA_KV_W = SWA_KV_HEADS * HEAD_DIM
GLA_QK_W = GLA_HEADS * GLA_DK
GLA_V_W = GLA_HEADS * GLA_DV
MEM_Q_W = MEM_HEADS * HEAD_DIM
IN_W = 1808
IN_W_PAD = 1920
COL_SQ, COL_SKV, COL_GQ, COL_GK, COL_GV, COL_GG, COL_MQ, COL_GLR = 0, 512, 768, 896, 1024, 1280, 1536, 1792

ADAM_LR = 0.001
ADAM_B1 = 0.9
ADAM_B2 = 0.999
ADAM_EPS = 1e-08
ADAM_WD = 0.01
ADAM_STEP = 10

N_DEV = 8
VMEM_LIMIT_BYTES = 56 * 1024 * 1024
MESH = pl.DeviceIdType.MESH


def _params(*sem):
    return pltpu.CompilerParams(dimension_semantics=sem or None, vmem_limit_bytes=VMEM_LIMIT_BYTES)


def _dot(a, b, ta, tb, precision=None):
    dims = (((0 if ta else 1,), (1 if tb else 0,)), ((), ()))
    return lax.dot_general(a, b, dims, preferred_element_type=F32, precision=precision)


def _mm_raw(a, b, ta=False, tb=False):
    return _dot(a.astype(BF16), b.astype(BF16), ta, tb)


def _mmf_raw(a, b, ta=False, tb=False):
    return _dot(a, b, ta, tb, lax.Precision.HIGHEST)


def _make_mm(raw):
    @functools.partial(jax.custom_vjp, nondiff_argnums=(2, 3))
    def mm(a, b, ta=False, tb=False):
        return raw(a, b, ta, tb)

    def fwd(a, b, ta, tb):
        return raw(a, b, ta, tb), (a, b)

    def bwd(ta, tb, res, g):
        a, b = res
        da = raw(b, g, tb, True) if ta else raw(g, b, False, not tb)
        db = raw(g, a, True, ta) if tb else raw(a, g, not ta, False)
        return da, db

    mm.defvjp(fwd, bwd)
    return mm


_mm = _make_mm(_mm_raw)
_mmf = _make_mm(_mmf_raw)


def _rms(x, g):
    return x * lax.rsqrt(jnp.mean(x * x, axis=-1, keepdims=True) + EPS) * g


def _silu_mul(g, u):
    return jax.nn.silu(g) * u


def _log_sigmoid(z):
    return jnp.minimum(z, 0.0) - jnp.log(1.0 + jnp.exp(-jnp.abs(z)))


def _matmul(a_list, b, *, ta=False, tb=False, tm, tn, b_blocks=None, res=None, scale=1.0, out_dtype=F32, name):
    if not isinstance(a_list, (list, tuple)):
        a_list = [a_list]
    n_a = len(a_list)
    m = a_list[0].shape[1] if ta else a_list[0].shape[0]
    ks = [a.shape[0] if ta else a.shape[1] for a in a_list]
    n = b.shape[0] if tb else b.shape[1]
    if b_blocks is None:
        assert n_a == 1
        b_blocks = [0]
    tm, tn = min(tm, m), min(tn, n)
    assert m % tm == 0 and n % tn == 0, (m, n, tm, tn)

    def body(*refs):
        a_refs, b_refs = refs[:n_a], refs[n_a:2 * n_a]
        r_ref = refs[2 * n_a] if res is not None else None
        o_ref = refs[-1]
        acc = _mm_raw(a_refs[0][...], b_refs[0][...], ta, tb)
        for k in range(1, n_a):
            acc = acc + _mm_raw(a_refs[k][...], b_refs[k][...], ta, tb)
        if scale != 1.0:
            acc = acc * scale
        if r_ref is not None:
            acc = r_ref[...] + acc
        o_ref[...] = acc.astype(out_dtype)

    in_specs = []
    for k in ks:
        in_specs.append(pl.BlockSpec((k, tm), lambda i, j: (0, i)) if ta else pl.BlockSpec((tm, k), lambda i, j: (i, 0)))
    for k, blk in zip(ks, b_blocks):
        if tb:
            in_specs.append(pl.BlockSpec((tn, k), functools.partial(lambda i, j, blk: (j, blk), blk=blk)))
        else:
            in_specs.append(pl.BlockSpec((k, tn), functools.partial(lambda i, j, blk: (blk, j), blk=blk)))
    args = list(a_list) + [b] * n_a
    if res is not None:
        in_specs.append(pl.BlockSpec((tm, tn), lambda i, j: (i, j)))
        args.append(res)
    return pl.pallas_call(
        body, name=name, grid=(m // tm, n // tn), in_specs=in_specs,
        out_specs=pl.BlockSpec((tm, tn), lambda i, j: (i, j)), out_shape=SDS((m, n), out_dtype),
        compiler_params=_params("parallel", "parallel"),
    )(*args)


def _win_pieces(w):
    glr_lo, glr_hi = COL_MQ, COL_MQ + GLA_RANK
    out = []
    for j in range(N_DEV):
        for lo, hi, shift in ((0, glr_lo, 0), (glr_lo, glr_hi, COL_GLR - glr_lo), (glr_hi, IN_W, COL_MQ - glr_hi)):
            s, e = max(j * w, lo), min((j + 1) * w, hi)
            if s < e:
                out.append((j, s - j * w, e - j * w, s + shift))
    return out


def _pack_win(win_all, *, tr, name):
    _, d, w = win_all.shape

    def body(i_ref, o_ref):
        for j, a, b, dst in _win_pieces(w):
            o_ref[:, dst:dst + b - a] = i_ref[j][:, a:b]
        o_ref[:, IN_W:] = jnp.zeros((tr, IN_W_PAD - IN_W), o_ref.dtype)

    return pl.pallas_call(
        body, name=name, grid=(d // tr,), in_specs=[pl.BlockSpec((N_DEV, tr, w), lambda i: (0, i, 0))],
        out_specs=pl.BlockSpec((tr, IN_W_PAD), lambda i: (i, 0)), out_shape=SDS((d, IN_W_PAD), win_all.dtype),
        compiler_params=_params("parallel"),
    )(win_all)


def _unpack_win(dwin_p, *, tr, name):
    d = dwin_p.shape[0]
    w = IN_W // N_DEV

    def body(i_ref, o_ref):
        for j, a, b, src in _win_pieces(w):
            o_ref[j % 2, j // 2, :, a:b] = i_ref[:, src:src + b - a]

    return pl.pallas_call(
        body, name=name, grid=(d // tr,), in_specs=[pl.BlockSpec((tr, IN_W_PAD), lambda i: (i, 0))],
        out_specs=pl.BlockSpec((2, 4, tr, w), lambda i: (0, 0, i, 0)), out_shape=SDS((2, 4, d, w), dwin_p.dtype),
        compiler_params=_params("parallel"),
    )(dwin_p)


def _rms_fwd(x, g, *, tm, name):
    s, d = x.shape

    def body(x_ref, g_ref, h_ref):
        h_ref[...] = _rms(x_ref[...], g_ref[...]).astype(BF16)

    return pl.pallas_call(
        body, name=name, grid=(s // tm,),
        in_specs=[pl.BlockSpec((tm, d), lambda i: (i, 0)), pl.BlockSpec((1, d), lambda i: (0, 0))],
        out_specs=pl.BlockSpec((tm, d), lambda i: (i, 0)), out_shape=SDS((s, d), BF16),
        compiler_params=_params("parallel"),
    )(x, g)


def _rms_bwd(x, g, dh, dres, *, tm, name):
    s, d = x.shape
    want_dx = dres is not None

    def body(*refs):
        if want_dx:
            x_ref, g_ref, dh_ref, dres_ref, dx_ref, dxb_ref, dg_ref = refs
        else:
            x_ref, g_ref, dh_ref, dg_ref = refs
        _, vjp = jax.vjp(_rms, x_ref[...], g_ref[...])
        dx, dg = vjp(dh_ref[...])
        if want_dx:
            dx = dres_ref[...] + dx
            dx_ref[...] = dx
            dxb_ref[...] = dx.astype(BF16)

        @pl.when(pl.program_id(0) == 0)
        def _():
            dg_ref[...] = jnp.zeros_like(dg_ref)

        dg_ref[...] += dg

    row = pl.BlockSpec((tm, d), lambda i: (i, 0))
    vec = pl.BlockSpec((1, d), lambda i: (0, 0))
    if want_dx:
        return pl.pallas_call(
            body, name=name, grid=(s // tm,), in_specs=[row, vec, row, row], out_specs=[row, row, vec],
            out_shape=[SDS((s, d), F32), SDS((s, d), BF16), SDS((1, d), F32)], compiler_params=_params("arbitrary"),
        )(x, g, dh, dres)
    return None, None, pl.pallas_call(
        body, name=name, grid=(s // tm,), in_specs=[row, vec, row], out_specs=vec,
        out_shape=SDS((1, d), F32), compiler_params=_params("arbitrary"),
    )(x, g, dh)


FFN_TN = 256
FFN_HALF_PAD = 192


def _ffn_fwd(x, gain, w3, tag, *, tm=1024):
    s, d = x.shape
    f = w3.shape[1]
    tn = FFN_TN
    nj = f // tn
    tm = min(tm, s)

    def body(x_ref, gain_ref, wg_ref, wu_ref, wd_ref, y_ref, h_ref, g_ref, u_ref, acc_s):
        j = pl.program_id(1)

        @pl.when(j == 0)
        def _():
            h_ref[...] = _rms(x_ref[...], gain_ref[...]).astype(BF16)
            acc_s[...] = jnp.zeros_like(acc_s)

        hv = h_ref[...]
        g = _mm_raw(hv, wg_ref[...], False, True)
        u = _mm_raw(hv, wu_ref[...], False, True)
        g_ref[...] = g.astype(BF16)
        u_ref[...] = u.astype(BF16)
        acc_s[...] += _mm_raw(_silu_mul(g, u), wd_ref[...])

        @pl.when(j == nj - 1)
        def _():
            y_ref[...] = x_ref[...] + 0.5 * acc_s[...]

    row = pl.BlockSpec((tm, d), lambda i, j: (i, 0))
    tile = pl.BlockSpec((tm, tn), lambda i, j: (i, j))
    y, h, g, u = pl.pallas_call(
        body, name=f"{tag}_fwd", grid=(s // tm, nj),
        in_specs=[row, pl.BlockSpec((1, d), lambda i, j: (0, 0))]
        + [pl.BlockSpec((None, tn, d), functools.partial(lambda i, j, k: (k, j, 0), k=k)) for k in range(3)],
        out_specs=[row, row, tile, tile],
        out_shape=[SDS((s, d), F32), SDS((s, d), BF16), SDS((s, f), BF16), SDS((s, f), BF16)],
        scratch_shapes=[pltpu.VMEM((tm, d), F32)],
        compiler_params=_params("parallel", "arbitrary"),
    )(x, gain, w3, w3, w3)
    return y, (h, g, u)


def _ffn_bwd_part(dyb, w3, saved, first, count, dh_init, *, name):
    h, g, u = saved
    s, d = h.shape
    tn = FFN_TN

    def body(*refs):
        if dh_init is None:
            dy_ref, h_ref, wg_ref, wu_ref, wd_ref, g_ref, u_ref, dh_ref, dw3_ref = refs
        else:
            dy_ref, h_ref, wg_ref, wu_ref, wd_ref, g_ref, u_ref, dh0_ref, dh_ref, dw3_ref = refs

        @pl.when(pl.program_id(0) == 0)
        def _():
            dh_ref[...] = jnp.zeros_like(dh_ref) if dh_init is None else dh0_ref[...]

        dyv = dy_ref[...]
        da = _mm_raw(dyv, wd_ref[...], False, True) * 0.5
        a, vjp = jax.vjp(_silu_mul, g_ref[...].astype(F32), u_ref[...].astype(F32))
        dg, du = vjp(da)
        dg = dg.astype(BF16)
        du = du.astype(BF16)
        dh_ref[...] += _mm_raw(dg, wg_ref[...]) + _mm_raw(du, wu_ref[...])
        hv = h_ref[...]
        dw3_ref[0] = _mm_raw(dg, hv, True, False).astype(BF16)
        dw3_ref[1] = _mm_raw(du, hv, True, False).astype(BF16)
        dw3_ref[2] = (_mm_raw(a, dyv, True, False) * 0.5).astype(BF16)

    full = pl.BlockSpec((s, d), lambda j: (0, 0))
    once = pl.BlockSpec((s, d), lambda j: (0, 0), pipeline_mode=pl.Buffered(1))
    tile = pl.BlockSpec((s, tn), lambda j: (0, first + j))
    in_specs = ([once, once]
                + [pl.BlockSpec((None, tn, d), functools.partial(lambda j, k: (k, first + j, 0), k=k)) for k in range(3)]
                + [tile, tile])
    args = [dyb, h, w3, w3, w3, g, u]
    if dh_init is not None:
        in_specs.append(once)
        args.append(dh_init)
    return pl.pallas_call(
        body, name=name, grid=(count,), in_specs=in_specs,
        out_specs=[full, pl.BlockSpec((3, tn, d), lambda j: (0, j, 0))],
        out_shape=[SDS((s, d), F32), SDS((3, count * tn, d), BF16)],
        compiler_params=_params("arbitrary"),
    )(*args)


def _loss_bwd(y, target, *, tm, name):
    s, d = y.shape

    def body(y_ref, t_ref, dy_ref, dyb_ref, l_ref):
        diff = y_ref[...] - t_ref[...]
        dy_ref[...] = diff * (1.0 / d)
        dyb_ref[...] = (diff * (1.0 / d)).astype(BF16)

        @pl.when(pl.program_id(0) == 0)
        def _():
            l_ref[...] = jnp.zeros_like(l_ref)

        l_ref[...] += 0.5 * jnp.sum(jnp.mean(diff * diff, axis=-1, keepdims=True), axis=0, keepdims=True)

    row = pl.BlockSpec((tm, d), lambda i: (i, 0))
    return pl.pallas_call(
        body, name=name, grid=(s // tm,), in_specs=[row, row],
        out_specs=[row, row, pl.BlockSpec((1, 1), lambda i: (0, 0))],
        out_shape=[SDS((s, d), F32), SDS((s, d), BF16), SDS((1, 1), F32)],
        compiler_params=_params("arbitrary"),
    )(y, target)


def _bucket_table():
    qi = np.arange(BLOCK)[:, None]
    kj = np.arange(2 * BLOCK)[None, :]
    dist = np.maximum(qi + BLOCK - kj, 0)
    max_exact = N_BUCKETS // 2
    d = np.maximum(dist, 1).astype(np.float32)
    large = max_exact + (np.log(d / np.float32(max_exact)) / np.float32(math.log(MAX_DISTANCE / max_exact))
                         * np.float32(N_BUCKETS - max_exact)).astype(np.int32)
    large = np.minimum(large, N_BUCKETS - 1)
    return np.where(dist < max_exact, dist, large).astype(np.int32)


SWA_STACK = SWA_GROUP * BLOCK


def _swa_valid(n):
    qi = lax.broadcasted_iota(jnp.int32, (SWA_STACK, 2 * BLOCK), 0) % BLOCK
    kj = lax.broadcasted_iota(jnp.int32, (SWA_STACK, 2 * BLOCK), 1)
    dist = qi + BLOCK - kj
    return (dist >= 0) & (dist < BLOCK) & ((kj >= BLOCK) | (n > 0))


def _swa_group(q, kb, vb, qg, kg, sink, bias, valid):
    qn = _rms(q, qg)
    kn = _rms(kb, kg)
    s = _mm(qn, kn, False, True) * (HEAD_DIM ** -0.5) + bias
    s = jnp.where(valid, s, -jnp.inf)
    m = lax.stop_gradient(jnp.maximum(jnp.max(s, axis=-1, keepdims=True), sink))
    p = jnp.exp(s - m)
    p = p / (jnp.sum(p, axis=-1, keepdims=True) + jnp.exp(sink - m))
    return _mm(p, vb)


def _swa_bias_table(rb_ref, bucket, bias_s):
    for h in range(SWA_HEADS):
        acc = jnp.zeros((BLOCK, 2 * BLOCK), F32)
        for b in range(N_BUCKETS):
            acc = jnp.where(bucket == b, rb_ref[b, h], acc)
        bias_s[h // SWA_GROUP, (h % SWA_GROUP) * BLOCK:(h % SWA_GROUP + 1) * BLOCK, :] = acc


def _swa_stack(ref, g):
    return jnp.concatenate([ref[:, (g * SWA_GROUP + hh) * HEAD_DIM:(g * SWA_GROUP + hh + 1) * HEAD_DIM]
                            for hh in range(SWA_GROUP)], axis=0)


def _swa_unstack(ref, g, stacked):
    for hh in range(SWA_GROUP):
        h = g * SWA_GROUP + hh
        ref[:, h * HEAD_DIM:(h + 1) * HEAD_DIM] = stacked[hh * BLOCK:(hh + 1) * BLOCK]


def _swa_sink_column(sink_ref, g):
    head = lax.broadcasted_iota(jnp.int32, (SWA_STACK, 1), 0) // BLOCK
    col = jnp.zeros((SWA_STACK, 1), F32)
    for hh in range(SWA_GROUP):
        col = jnp.where(head == hh, sink_ref[g * SWA_GROUP + hh], col)
    return col


def _swa_band(kvp_ref, kvc_ref, g):
    lo = g * HEAD_DIM
    kb = jnp.concatenate([kvp_ref[:, lo:lo + HEAD_DIM], kvc_ref[:, lo:lo + HEAD_DIM]], axis=0)
    lo += SWA_KV_W
    vb = jnp.concatenate([kvp_ref[:, lo:lo + HEAD_DIM], kvc_ref[:, lo:lo + HEAD_DIM]], axis=0)
    return kb, vb


def _swa_specs(order):
    kvc = COL_SKV // (2 * SWA_KV_W)
    return [
        pl.BlockSpec((BLOCK, SWA_Q_W), lambda t: (order(t), 0)),
        pl.BlockSpec((BLOCK, 2 * SWA_KV_W), lambda t: (jnp.maximum(order(t) - 1, 0), kvc)),
        pl.BlockSpec((BLOCK, 2 * SWA_KV_W), lambda t: (order(t), kvc)),
        pl.BlockSpec((1, HEAD_DIM), lambda t: (0, 0)),
        pl.BlockSpec((1, HEAD_DIM), lambda t: (0, 0)),
        pl.BlockSpec(memory_space=pltpu.SMEM),
        pl.BlockSpec(memory_space=pltpu.SMEM),
        pl.BlockSpec((BLOCK, 2 * BLOCK), lambda t: (0, 0)),
    ]


def _swa_fwd(p, qg, kg, sinks, rel_bias, *, name):
    s = p.shape[0]
    nb = s // BLOCK

    def body(q_ref, kvp_ref, kvc_ref, qg_ref, kg_ref, sink_ref, rb_ref, bucket_ref, y_ref, bias_s):
        n = pl.program_id(0)

        @pl.when(n == 0)
        def _():
            _swa_bias_table(rb_ref, bucket_ref[...], bias_s)

        valid = _swa_valid(n)
        for g in range(SWA_KV_HEADS):
            kb, vb = _swa_band(kvp_ref, kvc_ref, g)
            out = _swa_group(_swa_stack(q_ref, g), kb, vb, qg_ref[...], kg_ref[...], _swa_sink_column(sink_ref, g),
                             bias_s[g], valid)
            _swa_unstack(y_ref, g, out)

    return pl.pallas_call(
        body, name=name, grid=(nb,), in_specs=_swa_specs(lambda t: t),
        out_specs=pl.BlockSpec((BLOCK, SWA_Q_W), lambda t: (t, 0)), out_shape=SDS((s, SWA_Q_W), F32),
        scratch_shapes=[pltpu.VMEM((SWA_KV_HEADS, SWA_STACK, 2 * BLOCK), F32)],
        compiler_params=_params("arbitrary"),
    )(p, p, p, qg, kg, sinks, rel_bias, jnp.asarray(_bucket_table()))


def _swa_bwd(p, qg, kg, sinks, rel_bias, dy_all, *, name):
    s = p.shape[0]
    nb = s // BLOCK

    def body(q_ref, kvp_ref, kvc_ref, qg_ref, kg_ref, sink_ref, rb_ref, bucket_ref, dy_ref,
             dq_ref, dkv_ref, dqg_ref, dkg_ref, dsink_ref, drb_ref, bias_s, dbias_s, carry_s):
        t = pl.program_id(0)
        n = nb - 1 - t

        @pl.when(t == 0)
        def _():
            _swa_bias_table(rb_ref, bucket_ref[...], bias_s)
            dbias_s[...] = jnp.zeros_like(dbias_s)
            carry_s[...] = jnp.zeros_like(carry_s)
            dqg_ref[...] = jnp.zeros_like(dqg_ref)
            dkg_ref[...] = jnp.zeros_like(dkg_ref)
            dsink_ref[...] = jnp.zeros_like(dsink_ref)
            drb_ref[...] = jnp.zeros_like(drb_ref)

        valid = _swa_valid(n)
        lane = lax.broadcasted_iota(jnp.int32, (1, BLOCK), 1)
        dqg = jnp.zeros((1, HEAD_DIM), F32)
        dkg = jnp.zeros((1, HEAD_DIM), F32)
        dsink_vec = jnp.zeros((1, BLOCK), F32)
        for g in range(SWA_KV_HEADS):
            kb, vb = _swa_band(kvp_ref, kvc_ref, g)
            _, vjp = jax.vjp(functools.partial(_swa_group, valid=valid), _swa_stack(q_ref, g), kb, vb, qg_ref[...],
                             kg_ref[...], _swa_sink_column(sink_ref, g), bias_s[g])
            dq, dkb, dvb, dqg_g, dkg_g, dsink_col, dbias = vjp(_swa_stack(dy_ref, g))
            _swa_unstack(dq_ref, g, dq)
            dqg += dqg_g
            dkg += dkg_g
            dbias_s[g] += dbias
            for hh in range(SWA_GROUP):
                dsink_h = jnp.sum(dsink_col[hh * BLOCK:(hh + 1) * BLOCK], axis=0, keepdims=True)
                dsink_vec += jnp.where(lane == g * SWA_GROUP + hh, dsink_h, 0.0)
            lo = g * HEAD_DIM
            dkv_ref[:, lo:lo + HEAD_DIM] = dkb[BLOCK:] + carry_s[g]
            carry_s[g] = dkb[:BLOCK]
            lo += SWA_KV_W
            dkv_ref[:, lo:lo + HEAD_DIM] = dvb[BLOCK:] + carry_s[SWA_KV_HEADS + g]
            carry_s[SWA_KV_HEADS + g] = dvb[:BLOCK]
        dqg_ref[...] += dqg
        dkg_ref[...] += dkg
        dsink_ref[...] += dsink_vec

        @pl.when(t == nb - 1)
        def _():
            bucket = bucket_ref[...]
            row = lax.broadcasted_iota(jnp.int32, (N_BUCKETS, BLOCK), 0)
            col = lax.broadcasted_iota(jnp.int32, (N_BUCKETS, BLOCK), 1)
            acc = jnp.zeros((N_BUCKETS, BLOCK), F32)
            for h in range(SWA_HEADS):
                dbias = dbias_s[h // SWA_GROUP, (h % SWA_GROUP) * BLOCK:(h % SWA_GROUP + 1) * BLOCK, :]
                for b in range(N_BUCKETS):
                    part = jnp.sum(jnp.where(bucket == b, dbias, 0.0), axis=1, keepdims=True)
                    val = jnp.sum(part, axis=0, keepdims=True)
                    acc = acc + jnp.where((row == b) & (col == h), val, 0.0)
            drb_ref[...] = acc

    order = lambda t: nb - 1 - t
    vec = pl.BlockSpec((1, HEAD_DIM), lambda t: (0, 0))
    return pl.pallas_call(
        body, name=name, grid=(nb,),
        in_specs=_swa_specs(order) + [pl.BlockSpec((BLOCK, SWA_Q_W), lambda t: (order(t), 0))],
        out_specs=[pl.BlockSpec((BLOCK, SWA_Q_W), lambda t: (order(t), 0)),
                   pl.BlockSpec((BLOCK, 2 * SWA_KV_W), lambda t: (order(t), 0)),
                   vec, vec, pl.BlockSpec((1, BLOCK), lambda t: (0, 0)),
                   pl.BlockSpec((N_BUCKETS, BLOCK), lambda t: (0, 0))],
        out_shape=[SDS((s, SWA_Q_W), F32), SDS((s, 2 * SWA_KV_W), F32), SDS((1, HEAD_DIM), F32),
                   SDS((1, HEAD_DIM), F32), SDS((1, BLOCK), F32), SDS((N_BUCKETS, BLOCK), F32)],
        scratch_shapes=[pltpu.VMEM((SWA_KV_HEADS, SWA_STACK, 2 * BLOCK), F32),
                        pltpu.VMEM((SWA_KV_HEADS, SWA_STACK, 2 * BLOCK), F32),
                        pltpu.VMEM((2 * SWA_KV_HEADS, BLOCK, HEAD_DIM), F32)],
        compiler_params=_params("arbitrary"),
    )(p, p, p, qg, kg, sinks, rel_bias, jnp.asarray(_bucket_table()), dy_all)


def _mem_head(q, k, v, qg, kg):
    qn = _rms(q, qg)
    kn = _rms(k, kg)
    s = _mm(qn, kn, False, True) * (HEAD_DIM ** -0.5)
    m = lax.stop_gradient(jnp.max(s, axis=-1, keepdims=True))
    e = jnp.exp(s - m)
    return _mm(e / jnp.sum(e, axis=-1, keepdims=True), v)


def _mem_fwd(p, kv, qg, kg, *, tq, name):
    s = p.shape[0]
    m = kv.shape[0]

    def body(q_ref, kv_ref, qg_ref, kg_ref, y_ref):
        for h in range(MEM_HEADS):
            cols = slice(h * HEAD_DIM, (h + 1) * HEAD_DIM)
            vcols = slice(MEM_Q_W + h * HEAD_DIM, MEM_Q_W + (h + 1) * HEAD_DIM)
            y_ref[:, cols] = _mem_head(q_ref[:, cols], kv_ref[:, cols], kv_ref[:, vcols], qg_ref[...], kg_ref[...])

    vec = pl.BlockSpec((1, HEAD_DIM), lambda t: (0, 0))
    return pl.pallas_call(
        body, name=name, grid=(s // tq,),
        in_specs=[pl.BlockSpec((tq, MEM_Q_W), lambda t: (t, COL_MQ // MEM_Q_W)),
                  pl.BlockSpec((m, 2 * MEM_Q_W), lambda t: (0, 0)), vec, vec],
        out_specs=pl.BlockSpec((tq, MEM_Q_W), lambda t: (t, 0)), out_shape=SDS((s, MEM_Q_W), F32),
        compiler_params=_params("parallel"),
    )(p, kv, qg, kg)


def _mem_bwd(p, kv, qg, kg, dy_all, *, tq, name):
    s = p.shape[0]
    m = kv.shape[0]

    def body(q_ref, kv_ref, qg_ref, kg_ref, dy_ref, dq_ref, dkv_ref, dqg_ref, dkg_ref):
        @pl.when(pl.program_id(0) == 0)
        def _():
            dkv_ref[...] = jnp.zeros_like(dkv_ref)
            dqg_ref[...] = jnp.zeros_like(dqg_ref)
            dkg_ref[...] = jnp.zeros_like(dkg_ref)

        dqg = jnp.zeros((1, HEAD_DIM), F32)
        dkg = jnp.zeros((1, HEAD_DIM), F32)
        for h in range(MEM_HEADS):
            cols = slice(h * HEAD_DIM, (h + 1) * HEAD_DIM)
            vcols = slice(MEM_Q_W + h * HEAD_DIM, MEM_Q_W + (h + 1) * HEAD_DIM)
            _, vjp = jax.vjp(_mem_head, q_ref[:, cols], kv_ref[:, cols], kv_ref[:, vcols], qg_ref[...], kg_ref[...])
            dq, dk, dv, dqg_h, dkg_h = vjp(dy_ref[:, cols])
            dq_ref[:, cols] = dq
            dkv_ref[:, cols] += dk
            dkv_ref[:, vcols] += dv
            dqg += dqg_h
            dkg += dkg_h
        dqg_ref[...] += dqg
        dkg_ref[...] += dkg

    vec = pl.BlockSpec((1, HEAD_DIM), lambda t: (0, 0))
    full = pl.BlockSpec((m, 2 * MEM_Q_W), lambda t: (0, 0))
    dy_col = (SWA_Q_W + GLA_V_W) // MEM_Q_W
    return pl.pallas_call(
        body, name=name, grid=(s // tq,),
        in_specs=[pl.BlockSpec((tq, MEM_Q_W), lambda t: (t, COL_MQ // MEM_Q_W)), full, vec, vec,
                  pl.BlockSpec((tq, MEM_Q_W), lambda t: (t, dy_col))],
        out_specs=[pl.BlockSpec((tq, MEM_Q_W), lambda t: (t, 0)), full, vec, vec],
        out_shape=[SDS((s, MEM_Q_W), F32), SDS((m, 2 * MEM_Q_W), F32), SDS((1, HEAD_DIM), F32), SDS((1, HEAD_DIM), F32)],
        compiler_params=_params("arbitrary"),
    )(p, kv, qg, kg, dy_all)


GLA_ROWS = 256


def _gla_consts():
    c, h = GLA_CHUNK, GLA_HEADS
    i2 = lax.broadcasted_iota(jnp.int32, (c, c), 0)
    j2 = lax.broadcasted_iota(jnp.int32, (c, c), 1)
    slab_q = lax.broadcasted_iota(jnp.int32, (h, c, GLA_QK_W), 0)
    lane_q = lax.broadcasted_iota(jnp.int32, (h, c, GLA_QK_W), 2)
    row_a = lax.broadcasted_iota(jnp.int32, (h * c, c), 0)
    col_a = lax.broadcasted_iota(jnp.int32, (h * c, c), 1)
    slab_o = lax.broadcasted_iota(jnp.int32, (h, c, GLA_V_W), 0)
    lane_o = lax.broadcasted_iota(jnp.int32, (h, c, GLA_V_W), 2)
    row_s = lax.broadcasted_iota(jnp.int32, (GLA_V_W, GLA_QK_W), 0)
    col_s = lax.broadcasted_iota(jnp.int32, (GLA_V_W, GLA_QK_W), 1)
    return dict(
        ltri=(j2 <= i2).astype(F32),
        m_q=(slab_q == lane_q // GLA_DK).astype(F32),
        causal=col_a <= row_a % c,
        m_o=(slab_o == lane_o // GLA_DV).astype(F32),
        m_s=(row_s // GLA_DV == col_s // GLA_DK).astype(F32),
    )


def _gla_chunk(q, k, v, z, bg, st, c):
    h, n = GLA_HEADS, GLA_CHUNK
    la = _log_sigmoid(z + bg) * (1.0 / GLA_TAU)
    b = _mmf(c["ltri"], la)
    bl = jnp.sum(la, axis=0, keepdims=True)
    qs = q * (GLA_DK ** -0.5)
    kt = k * jnp.exp(bl - b)
    qt = qs * jnp.exp(b - bl)
    qe = qs * jnp.exp(b)
    q_stack = (jnp.broadcast_to(qt[None], (h, n, GLA_QK_W)) * c["m_q"]).reshape(h * n, GLA_QK_W)
    a = jnp.where(c["causal"], _mmf(q_stack, kt, False, True), 0.0)
    o_stack = _mm(a, v)
    o_intra = jnp.sum(o_stack.reshape(h, n, GLA_V_W) * c["m_o"], axis=0)
    o_inter = _mm(qe, st, False, True)
    st_next = st * jnp.exp(bl) + _mm(v, kt, True, False) * c["m_s"]
    return o_intra + o_inter, st_next


def _gla_post(o, gg, gain, g64):
    ms = _mmf(o * o, g64) * (1.0 / GLA_DV)
    return o * lax.rsqrt(ms + EPS) * gain * jax.nn.silu(gg)


def _gla_g64():
    r = lax.broadcasted_iota(jnp.int32, (GLA_V_W, GLA_V_W), 0)
    c = lax.broadcasted_iota(jnp.int32, (GLA_V_W, GLA_V_W), 1)
    return (r // GLA_DV == c // GLA_DV).astype(F32)


def _gla_in_specs(order):
    r = GLA_ROWS
    return [
        pl.BlockSpec((r, GLA_QK_W), lambda t: (order(t), COL_GQ // GLA_QK_W)),
        pl.BlockSpec((r, GLA_QK_W), lambda t: (order(t), COL_GK // GLA_QK_W)),
        pl.BlockSpec((r, GLA_V_W), lambda t: (order(t), COL_GV // GLA_V_W)),
        pl.BlockSpec((r, GLA_V_W), lambda t: (order(t), COL_GG // GLA_V_W)),
        pl.BlockSpec((r, GLA_QK_W), lambda t: (order(t), 0)),
        pl.BlockSpec((1, GLA_QK_W), lambda t: (0, 0)),
        pl.BlockSpec((1, GLA_V_W), lambda t: (0, 0)),
    ]


def _gla_fwd(p, z, bg, gain, *, name):
    s = p.shape[0]
    r = GLA_ROWS
    cps = r // GLA_CHUNK

    def body(q_ref, k_ref, v_ref, gg_ref, z_ref, bg_ref, gain_ref, y_ref, oraw_ref, stsave_ref, st_s):
        @pl.when(pl.program_id(0) == 0)
        def _():
            st_s[...] = jnp.zeros_like(st_s)

        c = _gla_consts()
        st = st_s[...]
        for ci in range(cps):
            rows = slice(ci * GLA_CHUNK, (ci + 1) * GLA_CHUNK)
            stsave_ref[ci] = st
            o, st = _gla_chunk(q_ref[rows, :], k_ref[rows, :], v_ref[rows, :], z_ref[rows, :], bg_ref[...], st, c)
            oraw_ref[rows, :] = o
        st_s[...] = st
        y_ref[...] = _gla_post(oraw_ref[...], gg_ref[...], gain_ref[...], _gla_g64())

    rowv = pl.BlockSpec((r, GLA_V_W), lambda t: (t, 0))
    return pl.pallas_call(
        body, name=name, grid=(s // r,), in_specs=_gla_in_specs(lambda t: t),
        out_specs=[rowv, rowv, pl.BlockSpec((cps, GLA_V_W, GLA_QK_W), lambda t: (t, 0, 0))],
        out_shape=[SDS((s, GLA_V_W), F32), SDS((s, GLA_V_W), F32), SDS((s // GLA_CHUNK, GLA_V_W, GLA_QK_W), F32)],
        scratch_shapes=[pltpu.VMEM((GLA_V_W, GLA_QK_W), F32)],
        compiler_params=_params("arbitrary"),
    )(p, p, p, p, z, bg, gain)


def _gla_bwd(p, z, bg, gain, oraw, stsave, dy_all, *, name):
    s = p.shape[0]
    r = GLA_ROWS
    cps = r // GLA_CHUNK
    nsteps = s // r
    w_qkvg = 2 * GLA_QK_W + 2 * GLA_V_W

    def body(q_ref, k_ref, v_ref, gg_ref, z_ref, bg_ref, gain_ref, oraw_ref, stsave_ref, dy_ref,
             dqkvg_ref, dz_ref, dbg_ref, dgain_ref, dst_s):
        @pl.when(pl.program_id(0) == 0)
        def _():
            dst_s[...] = jnp.zeros_like(dst_s)
            dbg_ref[...] = jnp.zeros_like(dbg_ref)
            dgain_ref[...] = jnp.zeros_like(dgain_ref)

        c = _gla_consts()
        _, vjp = jax.vjp(functools.partial(_gla_post, g64=_gla_g64()), oraw_ref[...], gg_ref[...], gain_ref[...])
        do, dgg, dgain = vjp(dy_ref[...])
        dqkvg_ref[:, 2 * GLA_QK_W + GLA_V_W:] = dgg
        dgain_ref[...] += dgain
        dst = dst_s[...]
        dbg = jnp.zeros((1, GLA_QK_W), F32)
        for ci in reversed(range(cps)):
            rows = slice(ci * GLA_CHUNK, (ci + 1) * GLA_CHUNK)
            _, vjp = jax.vjp(functools.partial(_gla_chunk, c=c), q_ref[rows, :], k_ref[rows, :], v_ref[rows, :],
                             z_ref[rows, :], bg_ref[...], stsave_ref[ci])
            dq, dk, dv, dz, dbg_c, dst = vjp((do[rows, :], dst))
            dqkvg_ref[rows, 0:GLA_QK_W] = dq
            dqkvg_ref[rows, GLA_QK_W:2 * GLA_QK_W] = dk
            dqkvg_ref[rows, 2 * GLA_QK_W:2 * GLA_QK_W + GLA_V_W] = dv
            dz_ref[rows, :] = dz
            dbg += dbg_c
        dst_s[...] = dst
        dbg_ref[...] += dbg

    order = lambda t: nsteps - 1 - t
    rowv = pl.BlockSpec((r, GLA_V_W), lambda t: (order(t), 0))
    return pl.pallas_call(
        body, name=name, grid=(nsteps,),
        in_specs=_gla_in_specs(order) + [
            rowv, pl.BlockSpec((cps, GLA_V_W, GLA_QK_W), lambda t: (order(t), 0, 0)),
            pl.BlockSpec((r, GLA_V_W), lambda t: (order(t), SWA_Q_W // GLA_V_W))],
        out_specs=[pl.BlockSpec((r, w_qkvg), lambda t: (order(t), 0)), pl.BlockSpec((r, GLA_QK_W), lambda t: (order(t), 0)),
                   pl.BlockSpec((1, GLA_QK_W), lambda t: (0, 0)), pl.BlockSpec((1, GLA_V_W), lambda t: (0, 0))],
        out_shape=[SDS((s, w_qkvg), F32), SDS((s, GLA_QK_W), F32), SDS((1, GLA_QK_W), F32), SDS((1, GLA_V_W), F32)],
        scratch_shapes=[pltpu.VMEM((GLA_V_W, GLA_QK_W), F32)],
        compiler_params=_params("arbitrary"),
    )(p, p, p, p, z, bg, gain, oraw, stsave, dy_all)


def _local_step(x, mem, target, small, big, on_grads):
    g1, gmix, gmem, g2, sqg, skg, sinks, rel_bias, wgu, bg, gla_gain, mqg, mkg = small
    w3_1, win_p, wkv, wout, gather_ffn2 = big
    wgu_pad = jnp.zeros((GLA_QK_W, GLA_QK_W), BF16).at[:GLA_RANK].set(wgu.astype(BF16))
    gain256 = jnp.tile(gla_gain, (1, GLA_HEADS))

    x1, saved1 = _ffn_fwd(x, g1, w3_1, "ffn1")
    w3_2 = gather_ffn2(x1)
    h = _rms_fwd(x1, gmix, tm=256, name="mix_rms")
    p = _matmul(h, win_p, tm=1024, tn=IN_W_PAD, name="mix_in")
    hm = _rms_fwd(mem, gmem, tm=256, name="mem_rms")
    kv = _matmul(hm, wkv, tm=256, tn=512, name="mem_kv")
    p_glr = p[:, COL_GLR:]
    z = _matmul(p_glr, wgu_pad, tm=1024, tn=GLA_QK_W, name="gla_gate")
    y_swa = _swa_fwd(p, sqg, skg, sinks, rel_bias, name="swa_fwd")
    y_gla, oraw, stsave = _gla_fwd(p, z, bg, gain256, name="gla_fwd")
    y_mem = _mem_fwd(p, kv, mqg, mkg, tq=512, name="mem_fwd")
    x2 = _matmul([y_swa, y_gla, y_mem], wout, b_blocks=[0, 2, 3], tm=1024, tn=1024, res=x1, name="mix_out")
    x3, saved2 = _ffn_fwd(x2, g2, w3_2, "ffn2")

    dy, dyb, loss = _loss_bwd(x3, target, tm=256, name="loss")
    tiles = w3_2.shape[1] // FFN_TN
    dh2, dw3_2 = _ffn_bwd_part(dyb, w3_2, saved2, 0, tiles, None, name="ffn2_bwd")
    dx2, dx2b, dg2 = _rms_bwd(x2, g2, dh2, dy, tm=256, name="ffn2_drms")
    dx2b = on_grads("ffn2", [dw3_2.reshape(3, 2, -1, dw3_2.shape[-1])], dx2b)
    dy_all = _matmul(dx2b, wout, tb=True, tm=1024, tn=1024, name="mix_dy")
    dwout = _matmul(jnp.concatenate([y_swa, y_gla, y_mem], axis=1), dx2b, ta=True, tm=512, tn=1024, out_dtype=BF16,
                    name="mix_dw_out")
    dq_swa, dkv_swa, dsqg, dskg, dsink, drb = _swa_bwd(p, sqg, skg, sinks, rel_bias, dy_all, name="swa_bwd")
    dqkvg, dz, dbg, dgain256 = _gla_bwd(p, z, bg, gain256, oraw, stsave, dy_all, name="gla_bwd")
    dmq, dkv_mem, dmqg, dmkg = _mem_bwd(p, kv, mqg, mkg, dy_all, tq=512, name="mem_bwd")
    dglr = _matmul(dz, wgu_pad, tb=True, tm=1024, tn=GLA_QK_W, name="gla_gate_dx")
    dwgu_pad = _matmul(p_glr, dz, ta=True, tm=GLA_QK_W, tn=GLA_QK_W, name="gla_gate_dw")
    dp = jnp.concatenate([dq_swa, dkv_swa, dqkvg, dmq, dglr], axis=1)
    dh = _matmul(dp, win_p, tb=True, tm=1024, tn=1024, name="mix_dh")
    dwin_p = _matmul(h, dp, ta=True, tm=1024, tn=640, out_dtype=BF16, name="mix_dw_in")
    dx1, dx1b, dgmix = _rms_bwd(x1, gmix, dh, dx2, tm=256, name="mix_drms")
    dwkv = _matmul(hm, dkv_mem, ta=True, tm=512, tn=512, out_dtype=BF16, name="mem_dw_kv")
    dx1b = on_grads("mix", (dwin_p, dwkv, dwout), dx1b)
    dhm = _matmul(dkv_mem, wkv, tb=True, tm=256, tn=512, name="mem_dh")
    _, _, dgmem = _rms_bwd(mem, gmem, dhm, None, tm=256, name="mem_drms")
    dh1, dw3_1a = _ffn_bwd_part(dx1b, w3_1, saved1, 0, tiles // 2, None, name="ffn1_bwd_a")
    dh1 = on_grads("ffn1a", [dw3_1a[:, None]], dh1)
    dh1, dw3_1b = _ffn_bwd_part(dx1b, w3_1, saved1, tiles // 2, tiles // 2, dh1, name="ffn1_bwd_b")
    dx, _, dg1 = _rms_bwd(x, g1, dh1, dx1, tm=256, name="ffn1_drms")

    dgla_gain = dgain256.reshape(GLA_HEADS, GLA_DV).sum(axis=0, keepdims=True)
    dsmall = (dg1, dgmix, dgmem, dg2, dsqg, dskg, dsink[0, :SWA_HEADS], drb[:, :SWA_HEADS], dwgu_pad[:GLA_RANK], dbg,
              dgla_gain, dmqg, dmkg)
    on_grads("ffn1b", [dw3_1b[:, None]], None, small=list(dsmall) + [loss])
    return dx


def _mesh_place():
    x, y, c = lax.axis_index("x"), lax.axis_index("y"), lax.axis_index("c")
    other_chips = [(1 - x, y), (x, 1 - y), (1 - x, 1 - y)]
    return x, y, c, other_chips


def _handshake(peers):
    barrier = pltpu.get_barrier_semaphore()
    for peer in peers:
        pl.semaphore_signal(barrier, inc=1, device_id=peer, device_id_type=MESH)
    pl.semaphore_wait(barrier, len(peers))


def _sequencer_call(body, operands, out_shapes, sems, *, name, collective_id):
    return pl.kernel(
        body, name=name, out_type=out_shapes, mesh=plsc.ScalarSubcoreMesh(axis_name="sequencer", num_cores=1),
        scratch_types=sems, compiler_params=pltpu.CompilerParams(collective_id=collective_id),
    )(*operands)


def _window(ref, kind, slot, shape):
    if kind == "row":
        rows = pl.ds(pl.multiple_of(slot * shape[-2], 8), shape[-2])
        return ref.at[(slice(None),) * (len(shape) - 2) + (rows,)]
    return ref.at[slot]


def _gathered(shape, kind):
    if kind == "row":
        return tuple(shape[:-2]) + (N_DEV * shape[-2], shape[-1])
    return (N_DEV,) + tuple(shape)


def _all_gather(shards, kinds, *, name, collective_id):
    nt = len(shards)

    def body(*refs):
        x_refs, o_refs = refs[:nt], refs[nt:2 * nt]
        send_sems, recv_sems, local_sems = refs[2 * nt:]
        x, y, c, chips = _mesh_place()
        me, sibling = (x, y, c), (x, y, 1 - c)
        _handshake([sibling] + [(*chip, c) for chip in chips])

        def copy(k, t, block, to, from_shard=False):
            bx, by, bc = block
            rows = _window(o_refs[t], kinds[t], 4 * bx + 2 * by + bc, shards[t].shape)
            return pltpu.make_async_remote_copy(
                src_ref=x_refs[t] if from_shard else rows, dst_ref=rows,
                send_sem=send_sems.at[k, t], recv_sem=recv_sems.at[k, t], device_id=to, device_id_type=MESH)

        mine = [pltpu.make_async_copy(x_refs[t], _window(o_refs[t], kinds[t], 4 * x + 2 * y + c, shards[t].shape),
                                      local_sems.at[t]) for t in range(nt)]
        for cp in mine:
            cp.start()
        first = [copy(0, t, me, sibling, True) for t in range(nt)]
        first += [copy(1 + j, t, me, (*chip, c), True) for j, chip in enumerate(chips) for t in range(nt)]
        for cp in first:
            cp.start()
        passed = []
        for j, chip in enumerate(chips):
            for t in range(nt):
                copy(1 + j, t, (*chip, c), me).wait_recv()
                fwd = copy(4 + j, t, (*chip, c), sibling)
                fwd.start()
                passed.append(fwd)
        for t in range(nt):
            copy(0, t, sibling, me).wait_recv()
        for j, chip in enumerate(chips):
            for t in range(nt):
                copy(4 + j, t, (*chip, 1 - c), me).wait_recv()
        for cp in first + passed:
            cp.wait_send()
        for cp in mine:
            cp.wait()

    return _sequencer_call(
        body, shards, [SDS(_gathered(s.shape, k), s.dtype) for s, k in zip(shards, kinds)],
        [pltpu.SemaphoreType.DMA((7, nt)), pltpu.SemaphoreType.DMA((7, nt)), pltpu.SemaphoreType.DMA((nt,))],
        name=name, collective_id=collective_id)


def _part_shape(shape, kind):
    if kind == "row":
        return tuple(shape[:-2]) + (shape[-2] // N_DEV, shape[-1])
    return tuple(shape[2:])


def _pair_exchange(grads, kinds, *, name, collective_id):
    nt = len(grads)
    part = [_part_shape(g.shape, k) for g, k in zip(grads, kinds)]

    def body(*refs):
        g_refs, o_refs = refs[:nt], refs[nt:2 * nt]
        send_sems, recv_sems = refs[2 * nt:]
        x, y, c, _ = _mesh_place()
        _handshake([(x, y, 1 - c)])
        copies = []
        for t in range(nt):
            for xy in range(4):
                src = g_refs[t].at[1 - c, xy] if kinds[t] == "stack" else _window(g_refs[t], kinds[t], 2 * xy + 1 - c, part[t])
                copies.append(pltpu.make_async_remote_copy(
                    src_ref=src, dst_ref=o_refs[t].at[xy], send_sem=send_sems.at[xy, t], recv_sem=recv_sems.at[xy, t],
                    device_id=(x, y, 1 - c), device_id_type=MESH))
        for cp in copies:
            cp.start()
        for cp in copies:
            cp.wait()

    return _sequencer_call(
        body, grads, [SDS((4,) + p, g.dtype) for p, g in zip(part, grads)],
        [pltpu.SemaphoreType.DMA((4, nt)), pltpu.SemaphoreType.DMA((4, nt))], name=name, collective_id=collective_id)


def _chip_exchange(parts, small, *, name, collective_id):
    nt = len(parts)
    if small is None:
        def body_plain(*refs):
            s_refs, o_refs = refs[:nt], refs[nt:2 * nt]
            send_sems, recv_sems = refs[2 * nt:]
            x, y, c, chips = _mesh_place()
            _handshake([(*chip, c) for chip in chips])
            copies = [pltpu.make_async_remote_copy(
                src_ref=s_refs[t].at[2 * chip[0] + chip[1]], dst_ref=o_refs[t].at[j],
                send_sem=send_sems.at[j, t], recv_sem=recv_sems.at[j, t], device_id=(*chip, c), device_id_type=MESH)
                for j, chip in enumerate(chips) for t in range(nt)]
            for cp in copies:
                cp.start()
            for cp in copies:
                cp.wait()

        return _sequencer_call(
            body_plain, parts, [SDS((3,) + s.shape[1:], s.dtype) for s in parts],
            [pltpu.SemaphoreType.DMA((3, nt)), pltpu.SemaphoreType.DMA((3, nt))], name=name, collective_id=collective_id)

    def body(*refs):
        s_refs, small_ref = refs[:nt], refs[nt]
        o_refs, small_all = refs[nt + 1:2 * nt + 1], refs[2 * nt + 1]
        send_sems, recv_sems, small_send, small_recv, local_sem = refs[2 * nt + 2:]
        x, y, c, chips = _mesh_place()
        _handshake([(px, py, pc) for px in (x, 1 - x) for py in (y, 1 - y) for pc in (c, 1 - c)][1:])

        def copy(j, t, chip):
            return pltpu.make_async_remote_copy(
                src_ref=s_refs[t].at[2 * chip[0] + chip[1]], dst_ref=o_refs[t].at[j],
                send_sem=send_sems.at[j, t], recv_sem=recv_sems.at[j, t], device_id=(*chip, c), device_id_type=MESH)

        flips = [(fx, fy, fc) for fx in (0, 1) for fy in (0, 1) for fc in (0, 1)][1:]

        def small_copy(k):
            fx, fy, fc = flips[k]
            to = (x ^ fx if fx else x, y ^ fy if fy else y, c ^ fc if fc else c)
            rows = small_all.at[4 * x + 2 * y + c]
            return pltpu.make_async_remote_copy(
                src_ref=small_ref, dst_ref=rows, send_sem=small_send.at[k], recv_sem=small_recv.at[k],
                device_id=to, device_id_type=MESH)

        own = pltpu.make_async_copy(small_ref, small_all.at[4 * x + 2 * y + c], local_sem)
        own.start()
        copies = [copy(j, t, chip) for j, chip in enumerate(chips) for t in range(nt)]
        smalls = [small_copy(k) for k in range(7)]
        for cp in smalls + copies:
            cp.start()
        for cp in smalls + copies:
            cp.wait()
        own.wait()

    return _sequencer_call(
        body, list(parts) + [small],
        [SDS((3,) + s.shape[1:], s.dtype) for s in parts] + [SDS((N_DEV,) + small.shape, small.dtype)],
        [pltpu.SemaphoreType.DMA((3, nt)), pltpu.SemaphoreType.DMA((3, nt)),
         pltpu.SemaphoreType.DMA((7,)), pltpu.SemaphoreType.DMA((7,)), pltpu.SemaphoreType.DMA],
        name=name, collective_id=collective_id)


def _pair_sum(grad, theirs, kind, c, *, name):
    if kind == "row":
        r, l = theirs.shape[-2:]
        n = theirs.size // (4 * r * l)
        grad, theirs = grad.reshape(n, N_DEV * r, l), theirs.reshape(4, n, r, l)
        mine_spec = pl.BlockSpec((None, r, l), lambda xy, m, c_ref: (m, 2 * xy + c_ref[0], 0))
    else:
        r, l = theirs.shape[-2:]
        n = theirs.size // (4 * r * l)
        theirs = theirs.reshape(4, n, r, l)
        grad = grad.reshape(2, 4, n, r, l)
        mine_spec = pl.BlockSpec((None, None, None, r, l), lambda xy, m, c_ref: (c_ref[0], xy, m, 0, 0))

    def body(c_ref, a_ref, b_ref, o_ref):
        o_ref[...] = (a_ref[...].astype(F32) + b_ref[...].astype(F32)).astype(BF16)

    part = pl.BlockSpec((None, None, r, l), lambda xy, m, c_ref: (xy, m, 0, 0))
    return pl.pallas_call(
        body, name=name,
        grid_spec=pltpu.PrefetchScalarGridSpec(num_scalar_prefetch=1, grid=(4, n), in_specs=[mine_spec, part], out_specs=part),
        out_shape=SDS((4, n, r, l), BF16), compiler_params=_params("parallel", "parallel"),
    )(c, grad, theirs)


def _adamw(w, g, m, v):
    m = ADAM_B1 * m + (1.0 - ADAM_B1) * g
    v = ADAM_B2 * v + (1.0 - ADAM_B2) * jnp.square(g)
    m_hat = m / (1.0 - ADAM_B1 ** ADAM_STEP)
    v_hat = v / (1.0 - ADAM_B2 ** ADAM_STEP)
    delta = -ADAM_LR * (m_hat / (jnp.sqrt(v_hat) + ADAM_EPS) + ADAM_WD * w)
    return delta, m, v


def _adam_big(own, others, mat, xy, w, m, v, *, tr, name):
    nw, r, l = w.shape
    lp = own.shape[-1]

    def body(xy_ref, own_ref, oth_ref, w_ref, m_ref, v_ref, g_out, d_out, m_out, v_out):
        g = own_ref[0, 0].astype(F32)
        for j in range(3):
            g = g + oth_ref[j, 0].astype(F32)
        g = g[:, :l]
        delta, m_new, v_new = _adamw(w_ref[0], g, m_ref[0], v_ref[0])
        g_out[0] = g
        d_out[0] = delta
        m_out[0] = m_new
        v_out[0] = v_new

    blk = pl.BlockSpec((1, tr, l), lambda p, i, xy_ref: (p, i, 0))
    return pl.pallas_call(
        body, name=name,
        grid_spec=pltpu.PrefetchScalarGridSpec(
            num_scalar_prefetch=1, grid=(nw, r // tr),
            in_specs=[pl.BlockSpec((1, 1, tr, lp), lambda p, i, xy_ref: (xy_ref[0], mat * nw + p, i, 0)),
                      pl.BlockSpec((3, 1, tr, lp), lambda p, i, xy_ref: (0, mat * nw + p, i, 0)), blk, blk, blk],
            out_specs=[blk, blk, blk, blk]),
        out_shape=[SDS(w.shape, F32)] * 4, compiler_params=_params("parallel", "parallel"),
    )(xy, own, others, w, m, v)


def _adam_small(g_all, w, m, v, *, name):
    def body(g_ref, w_ref, m_ref, v_ref, g_out, d_out, m_out, v_out):
        g = g_ref[0]
        for k in range(1, N_DEV):
            g = g + g_ref[k]
        delta, m_new, v_new = _adamw(w_ref[...], g, m_ref[...], v_ref[...])
        g_out[...] = g
        d_out[...] = delta
        m_out[...] = m_new
        v_out[...] = v_new

    return pl.pallas_call(body, name=name, out_shape=[SDS(w.shape, F32)] * 4)(g_all, w, m, v)


SMALL_ROWS = 56


def _pack_small(parts):
    flat = jnp.concatenate([a.reshape(-1) for a in parts])
    return jnp.pad(flat, (0, SMALL_ROWS * 128 - flat.shape[0])).reshape(SMALL_ROWS, 128)


def _unpack_small(packed, shapes):
    flat = packed.reshape(-1)
    out, at = [], 0
    for s in shapes:
        n = math.prod(s)
        out.append(flat[at:at + n].reshape(s))
        at += n
    return out


def kernel(x, mem, ffn1_norm, ffn1_w_gate, ffn1_w_up, ffn1_w_down, mix_norm, mem_norm, w_in, w_mem_kv, swa_q_norm, swa_k_norm, swa_sinks, rel_bias, gla_w_gate_up, gla_b_gate, gla_out_norm, mem_q_norm, mem_k_norm, w_out, ffn2_norm, ffn2_w_gate, ffn2_w_up, ffn2_w_down, loss_target, m_ffn1_norm, m_ffn1_w_gate, m_ffn1_w_up, m_ffn1_w_down, m_mix_norm, m_mem_norm, m_w_in, m_w_mem_kv, m_swa_q_norm, m_swa_k_norm, m_swa_sinks, m_rel_bias, m_gla_w_gate_up, m_gla_b_gate, m_gla_out_norm, m_mem_q_norm, m_mem_k_norm, m_w_out, m_ffn2_norm, m_ffn2_w_gate, m_ffn2_w_up, m_ffn2_w_down, v_ffn1_norm, v_ffn1_w_gate, v_ffn1_w_up, v_ffn1_w_down, v_mix_norm, v_mem_norm, v_w_in, v_w_mem_kv, v_swa_q_norm, v_swa_k_norm, v_swa_sinks, v_rel_bias, v_gla_w_gate_up, v_gla_b_gate, v_gla_out_norm, v_mem_q_norm, v_mem_k_norm, v_w_out, v_ffn2_norm, v_ffn2_w_gate, v_ffn2_w_up, v_ffn2_w_down):
    xi, yi, ci = lax.axis_index("x"), lax.axis_index("y"), lax.axis_index("c")
    c_arr = jnp.reshape(ci, (1,)).astype(jnp.int32)
    xy_arr = jnp.reshape(2 * xi + yi, (1,)).astype(jnp.int32)
    d = x.shape[-1]

    half_h = ffn1_w_gate.shape[-1] // 2

    def gather_ffn(wg_s, wu_s, wd_s, name, collective_id, after):
        w3_s = jnp.concatenate([wg_s.transpose(0, 2, 1), wu_s.transpose(0, 2, 1), wd_s], axis=0)
        w3_s = jnp.pad(w3_s.reshape(3, 2, half_h, d), ((0, 0), (0, 0), (0, FFN_HALF_PAD - half_h), (0, 0))).astype(BF16)
        if after is not None:
            w3_s, _ = lax.optimization_barrier((w3_s, after))
        return _all_gather([w3_s], ["row"], name=name, collective_id=collective_id)[0].reshape(3, -1, d)

    w3_1 = gather_ffn(ffn1_w_gate, ffn1_w_up, ffn1_w_down, "gather_ffn1", 0, None)
    mix_s = lax.optimization_barrier((w_in[0].astype(BF16), w_mem_kv[0].astype(BF16), w_out[0].astype(BF16), w3_1))[:3]
    win_all, wkv, wout = _all_gather(list(mix_s), ["stack", "row", "row"], name="gather_mix", collective_id=1)

    def gather_ffn2(x1):
        return gather_ffn(ffn2_w_gate, ffn2_w_up, ffn2_w_down, "gather_ffn2", 2, (wout, x1))

    win_p = _pack_win(win_all, tr=256, name="pack_w_in")

    small_w = [ffn1_norm, mix_norm, mem_norm, ffn2_norm, swa_q_norm, swa_k_norm, swa_sinks[0], rel_bias,
               gla_w_gate_up[0], gla_b_gate, gla_out_norm, mem_q_norm, mem_k_norm]
    collective_ids = {"ffn2": (3, 4), "mix": (5, 6), "ffn1a": (7, 8), "ffn1b": (9, 10)}
    reduced, small_box = {}, []

    def on_grads(group, grads, carry, small=None):
        if group == "mix":
            dwin_p, dwkv, dwout = grads
            grads = [_unpack_win(dwin_p, tr=256, name="unpack_dw_in"), dwkv, dwout]
            kinds = ["stack", "row", "row"]
        else:
            kinds = ["row"]
        if reduced:
            earlier = list(reduced.values())[-1][1]
            *grads, _ = lax.optimization_barrier((*grads, earlier[0]))
        id_pair, id_chip = collective_ids[group]
        from_sibling = _pair_exchange(grads, kinds, name=f"pair_exchange_{group}", collective_id=id_pair)
        chip_sums = [_pair_sum(g, theirs, k, c_arr, name=f"pair_sum_{group}_{t}")
                     for t, (g, theirs, k) in enumerate(zip(grads, from_sibling, kinds))]
        if carry is not None:
            *chip_sums, carry = lax.optimization_barrier((*chip_sums, carry))
        if small is None:
            from_chips = _chip_exchange(chip_sums, None, name=f"chip_exchange_{group}", collective_id=id_chip)
        else:
            *from_chips, small_all = _chip_exchange(chip_sums, _pack_small(small), name=f"chip_exchange_{group}",
                                                    collective_id=id_chip)
            small_box.append(small_all)
        reduced[group] = (chip_sums, from_chips)
        return carry

    grad_x = _local_step(x[0], mem[0], loss_target[0], small_w, (w3_1, win_p, wkv, wout, gather_ffn2), on_grads)
    small_all = small_box[0]

    big_w = {"ffn1_w_gate": ("ffn1", 0, 0, True, ffn1_w_gate, m_ffn1_w_gate, v_ffn1_w_gate),
             "ffn1_w_up": ("ffn1", 0, 1, True, ffn1_w_up, m_ffn1_w_up, v_ffn1_w_up),
             "ffn1_w_down": ("ffn1", 0, 2, False, ffn1_w_down, m_ffn1_w_down, v_ffn1_w_down),
             "w_in": ("mix", 0, 0, False, w_in, m_w_in, v_w_in),
             "w_mem_kv": ("mix", 1, 0, False, w_mem_kv, m_w_mem_kv, v_w_mem_kv),
             "w_out": ("mix", 2, 0, False, w_out, m_w_out, v_w_out),
             "ffn2_w_gate": ("ffn2", 0, 0, True, ffn2_w_gate, m_ffn2_w_gate, v_ffn2_w_gate),
             "ffn2_w_up": ("ffn2", 0, 1, True, ffn2_w_up, m_ffn2_w_up, v_ffn2_w_up),
             "ffn2_w_down": ("ffn2", 0, 2, False, ffn2_w_down, m_ffn2_w_down, v_ffn2_w_down)}
    reduced["ffn1"] = tuple([jnp.stack([a, b], axis=2).reshape((a.shape[0], -1) + a.shape[2:])]
                            for a, b in zip((reduced["ffn1a"][0][0], reduced["ffn1a"][1][0]),
                                            (reduced["ffn1b"][0][0], reduced["ffn1b"][1][0])))
    res = {}
    for nm, (group, t, mat, transposed, w, m, v) in big_w.items():
        shape = w.shape
        if transposed:
            w, m, v = (a.transpose(0, 2, 1) for a in (w, m, v))
        if group != "mix":
            w, m, v = (a.reshape(2, half_h, d) for a in (w, m, v))
        r = w.shape[1]
        tr = 256 if r % 256 == 0 else r
        chip_sums, from_chips = reduced[group]
        out = _adam_big(chip_sums[t], from_chips[t], mat, xy_arr, w, m, v, tr=tr, name=f"adam_{nm}")
        if transposed:
            out = [a.reshape(1, -1, d).transpose(0, 2, 1) for a in out]
        res[nm] = [a.reshape(shape) for a in out]
    small_names = ["ffn1_norm", "mix_norm", "mem_norm", "ffn2_norm", "swa_q_norm", "swa_k_norm", "swa_sinks", "rel_bias",
                   "gla_w_gate_up", "gla_b_gate", "gla_out_norm", "mem_q_norm", "mem_k_norm"]
    small_m = [m_ffn1_norm, m_mix_norm, m_mem_norm, m_ffn2_norm, m_swa_q_norm, m_swa_k_norm, m_swa_sinks, m_rel_bias,
               m_gla_w_gate_up, m_gla_b_gate, m_gla_out_norm, m_mem_q_norm, m_mem_k_norm]
    small_v = [v_ffn1_norm, v_mix_norm, v_mem_norm, v_ffn2_norm, v_swa_q_norm, v_swa_k_norm, v_swa_sinks, v_rel_bias,
               v_gla_w_gate_up, v_gla_b_gate, v_gla_out_norm, v_mem_q_norm, v_mem_k_norm]
    small_full = [ffn1_norm, mix_norm, mem_norm, ffn2_norm, swa_q_norm, swa_k_norm, swa_sinks, rel_bias,
                  gla_w_gate_up, gla_b_gate, gla_out_norm, mem_q_norm, mem_k_norm]
    packed = _adam_small(small_all, _pack_small(small_full), _pack_small(small_m), _pack_small(small_v), name="adam_small")
    full_shapes = [a.shape for a in small_full]
    unpacked = [_unpack_small(pk, full_shapes + [()]) for pk in packed]
    for k, nm in enumerate(small_names):
        res[nm] = [unpacked[q][k] for q in range(4)]
    loss = unpacked[0][len(small_names)]

    order = ["ffn1_norm", "ffn1_w_gate", "ffn1_w_up", "ffn1_w_down", "mix_norm", "mem_norm", "w_in", "w_mem_kv",
             "swa_q_norm", "swa_k_norm", "swa_sinks", "rel_bias", "gla_w_gate_up", "gla_b_gate", "gla_out_norm",
             "mem_q_norm", "mem_k_norm", "w_out", "ffn2_norm", "ffn2_w_gate", "ffn2_w_up", "ffn2_w_down"]
    outs = [loss, grad_x[None]]
    for q in range(4):
        outs += [res[nm][q] for nm in order]
    return tuple(outs)
```

```python
import functools
import math

import numpy as np
import jax
import jax.numpy as jnp
from jax import lax
from jax.experimental import pallas as pl
from jax.experimental.pallas import tpu as pltpu
from jax.experimental.pallas import tpu_sc as plsc

F32 = jnp.float32
BF16 = jnp.bfloat16
SDS = jax.ShapeDtypeStruct

EPS = 1e-6
HEAD_DIM = 64
SWA_HEADS = 8
SWA_KV_HEADS = 2
SWA_GROUP = SWA_HEADS // SWA_KV_HEADS
BLOCK = 128
N_BUCKETS = 32
MAX_DISTANCE = 128
GLA_HEADS = 4
GLA_DK = 32
GLA_DV = 64
GLA_RANK = 16
GLA_TAU = 16.0
GLA_CHUNK = 32
MEM_HEADS = 4
SWA_Q_W = SWA_HEADS * HEAD_DIM
SWA_KV_W = SWA_KV_HEADS * HEAD_DIM
GLA_QK_W = GLA_HEADS * GLA_DK
GLA_V_W = GLA_HEADS * GLA_DV
MEM_Q_W = MEM_HEADS * HEAD_DIM
IN_W = 1808
IN_W_PAD = 1920
COL_SQ, COL_SKV, COL_GQ, COL_GK, COL_GV, COL_GG, COL_MQ, COL_GLR = 0, 512, 768, 896, 1024, 1280, 1536, 1792

ADAM_LR = 0.001
ADAM_B1 = 0.9
ADAM_B2 = 0.999
ADAM_EPS = 1e-08
ADAM_WD = 0.01
ADAM_STEP = 10

N_DEV = 8
VMEM_LIMIT_BYTES = 56 * 1024 * 1024
MESH = pl.DeviceIdType.MESH


def _params(*sem):
    return pltpu.CompilerParams(dimension_semantics=sem or None, vmem_limit_bytes=VMEM_LIMIT_BYTES)


def _dot(a, b, ta, tb, precision=None):
    dims = (((0 if ta else 1,), (1 if tb else 0,)), ((), ()))
    return lax.dot_general(a, b, dims, preferred_element_type=F32, precision=precision)


def _mm_raw(a, b, ta=False, tb=False):
    return _dot(a.astype(BF16), b.astype(BF16), ta, tb)


def _mmf_raw(a, b, ta=False, tb=False):
    return _dot(a, b, ta, tb, lax.Precision.HIGHEST)


def _make_mm(raw):
    @functools.partial(jax.custom_vjp, nondiff_argnums=(2, 3))
    def mm(a, b, ta=False, tb=False):
        return raw(a, b, ta, tb)

    def fwd(a, b, ta, tb):
        return raw(a, b, ta, tb), (a, b)

    def bwd(ta, tb, res, g):
        a, b = res
        da = raw(b, g, tb, True) if ta else raw(g, b, False, not tb)
        db = raw(g, a, True, ta) if tb else raw(a, g, not ta, False)
        return da, db

    mm.defvjp(fwd, bwd)
    return mm


_mm = _make_mm(_mm_raw)
_mmf = _make_mm(_mmf_raw)


def _rms(x, g):
    return x * lax.rsqrt(jnp.mean(x * x, axis=-1, keepdims=True) + EPS) * g


def _silu_mul(g, u):
    return jax.nn.silu(g) * u


def _log_sigmoid(z):
    return jnp.minimum(z, 0.0) - jnp.log(1.0 + jnp.exp(-jnp.abs(z)))


def _matmul(a_list, b, *, ta=False, tb=False, tm, tn, b_blocks=None, res=None, scale=1.0, out_dtype=F32, name):
    if not isinstance(a_list, (list, tuple)):
        a_list = [a_list]
    n_a = len(a_list)
    m = a_list[0].shape[1] if ta else a_list[0].shape[0]
    ks = [a.shape[0] if ta else a.shape[1] for a in a_list]
    n = b.shape[0] if tb else b.shape[1]
    if b_blocks is None:
        assert n_a == 1
        b_blocks = [0]
    tm, tn = min(tm, m), min(tn, n)
    assert m % tm == 0 and n % tn == 0, (m, n, tm, tn)

    def body(*refs):
        a_refs, b_refs = refs[:n_a], refs[n_a:2 * n_a]
        r_ref = refs[2 * n_a] if res is not None else None
        o_ref = refs[-1]
        acc = _mm_raw(a_refs[0][...], b_refs[0][...], ta, tb)
        for k in range(1, n_a):
            acc = acc + _mm_raw(a_refs[k][...], b_refs[k][...], ta, tb)
        if scale != 1.0:
            acc = acc * scale
        if r_ref is not None:
            acc = r_ref[...] + acc
        o_ref[...] = acc.astype(out_dtype)

    in_specs = []
    for k in ks:
        in_specs.append(pl.BlockSpec((k, tm), lambda i, j: (0, i)) if ta else pl.BlockSpec((tm, k), lambda i, j: (i, 0)))
    for k, blk in zip(ks, b_blocks):
        if tb:
            in_specs.append(pl.BlockSpec((tn, k), functools.partial(lambda i, j, blk: (j, blk), blk=blk)))
        else:
            in_specs.append(pl.BlockSpec((k, tn), functools.partial(lambda i, j, blk: (blk, j), blk=blk)))
    args = list(a_list) + [b] * n_a
    if res is not None:
        in_specs.append(pl.BlockSpec((tm, tn), lambda i, j: (i, j)))
        args.append(res)
    return pl.pallas_call(
        body, name=name, grid=(m // tm, n // tn), in_specs=in_specs,
        out_specs=pl.BlockSpec((tm, tn), lambda i, j: (i, j)), out_shape=SDS((m, n), out_dtype),
        compiler_params=_params("parallel", "parallel"),
    )(*args)


def _win_pieces(w):
    glr_lo, glr_hi = COL_MQ, COL_MQ + GLA_RANK
    out = []
    for j in range(N_DEV):
        for lo, hi, shift in ((0, glr_lo, 0), (glr_lo, glr_hi, COL_GLR - glr_lo), (glr_hi, IN_W, COL_MQ - glr_hi)):
            s, e = max(j * w, lo), min((j + 1) * w, hi)
            if s < e:
                out.append((j, s - j * w, e - j * w, s + shift))
    return out


def _pack_win(win_all, *, tr, name):
    _, d, w = win_all.shape

    def body(i_ref, o_ref):
        for j, a, b, dst in _win_pieces(w):
            o_ref[:, dst:dst + b - a] = i_ref[j][:, a:b]
        o_ref[:, IN_W:] = jnp.zeros((tr, IN_W_PAD - IN_W), o_ref.dtype)

    return pl.pallas_call(
        body, name=name, grid=(d // tr,), in_specs=[pl.BlockSpec((N_DEV, tr, w), lambda i: (0, i, 0))],
        out_specs=pl.BlockSpec((tr, IN_W_PAD), lambda i: (i, 0)), out_shape=SDS((d, IN_W_PAD), win_all.dtype),
        compiler_params=_params("parallel"),
    )(win_all)


def _unpack_win(dwin_p, *, tr, name):
    d = dwin_p.shape[0]
    w = IN_W // N_DEV

    def body(i_ref, o_ref):
        for j, a, b, src in _win_pieces(w):
            o_ref[j % 2, j // 2, :, a:b] = i_ref[:, src:src + b - a]

    return pl.pallas_call(
        body, name=name, grid=(d // tr,), in_specs=[pl.BlockSpec((tr, IN_W_PAD), lambda i: (i, 0))],
        out_specs=pl.BlockSpec((2, 4, tr, w), lambda i: (0, 0, i, 0)), out_shape=SDS((2, 4, d, w), dwin_p.dtype),
        compiler_params=_params("parallel"),
    )(dwin_p)


def _rms_fwd(x, g, *, tm, name):
    s, d = x.shape

    def body(x_ref, g_ref, h_ref):
        h_ref[...] = _rms(x_ref[...], g_ref[...]).astype(BF16)

    return pl.pallas_call(
        body, name=name, grid=(s // tm,),
        in_specs=[pl.BlockSpec((tm, d), lambda i: (i, 0)), pl.BlockSpec((1, d), lambda i: (0, 0))],
        out_specs=pl.BlockSpec((tm, d), lambda i: (i, 0)), out_shape=SDS((s, d), BF16),
        compiler_params=_params("parallel"),
    )(x, g)


def _rms_bwd(x, g, dh, dres, *, tm, name):
    s, d = x.shape
    want_dx = dres is not None

    def body(*refs):
        if want_dx:
            x_ref, g_ref, dh_ref, dres_ref, dx_ref, dxb_ref, dg_ref = refs
        else:
            x_ref, g_ref, dh_ref, dg_ref = refs
        _, vjp = jax.vjp(_rms, x_ref[...], g_ref[...])
        dx, dg = vjp(dh_ref[...])
        if want_dx:
            dx = dres_ref[...] + dx
            dx_ref[...] = dx
            dxb_ref[...] = dx.astype(BF16)

        @pl.when(pl.program_id(0) == 0)
        def _():
            dg_ref[...] = jnp.zeros_like(dg_ref)

        dg_ref[...] += dg

    row = pl.BlockSpec((tm, d), lambda i: (i, 0))
    vec = pl.BlockSpec((1, d), lambda i: (0, 0))
    if want_dx:
        return pl.pallas_call(
            body, name=name, grid=(s // tm,), in_specs=[row, vec, row, row], out_specs=[row, row, vec],
            out_shape=[SDS((s, d), F32), SDS((s, d), BF16), SDS((1, d), F32)], compiler_params=_params("arbitrary"),
        )(x, g, dh, dres)
    return None, None, pl.pallas_call(
        body, name=name, grid=(s // tm,), in_specs=[row, vec, row], out_specs=vec,
        out_shape=SDS((1, d), F32), compiler_params=_params("arbitrary"),
    )(x, g, dh)


FFN_TN = 256
FFN_HALF_PAD = 192


def _ffn_fwd(x, gain, w3, tag, *, tm=1024):
    s, d = x.shape
    f = w3.shape[1]
    tn = FFN_TN
    nj = f // tn
    tm = min(tm, s)

    def body(x_ref, gain_ref, wg_ref, wu_ref, wd_ref, y_ref, h_ref, g_ref, u_ref, acc_s):
        j = pl.program_id(1)

        @pl.when(j == 0)
        def _():
            h_ref[...] = _rms(x_ref[...], gain_ref[...]).astype(BF16)
            acc_s[...] = jnp.zeros_like(acc_s)

        hv = h_ref[...]
        g = _mm_raw(hv, wg_ref[...], False, True)
        u = _mm_raw(hv, wu_ref[...], False, True)
        g_ref[...] = g.astype(BF16)
        u_ref[...] = u.astype(BF16)
        acc_s[...] += _mm_raw(_silu_mul(g, u), wd_ref[...])

        @pl.when(j == nj - 1)
        def _():
            y_ref[...] = x_ref[...] + 0.5 * acc_s[...]

    row = pl.BlockSpec((tm, d), lambda i, j: (i, 0))
    tile = pl.BlockSpec((tm, tn), lambda i, j: (i, j))
    y, h, g, u = pl.pallas_call(
        body, name=f"{tag}_fwd", grid=(s // tm, nj),
        in_specs=[row, pl.BlockSpec((1, d), lambda i, j: (0, 0))]
        + [pl.BlockSpec((None, tn, d), functools.partial(lambda i, j, k: (k, j, 0), k=k)) for k in range(3)],
        out_specs=[row, row, tile, tile],
        out_shape=[SDS((s, d), F32), SDS((s, d), BF16), SDS((s, f), BF16), SDS((s, f), BF16)],
        scratch_shapes=[pltpu.VMEM((tm, d), F32)],
        compiler_params=_params("parallel", "arbitrary"),
    )(x, gain, w3, w3, w3)
    return y, (h, g, u)


def _ffn_bwd_part(dyb, w3, saved, first, count, dh_init, *, name):
    h, g, u = saved
    s, d = h.shape
    tn = FFN_TN

    def body(*refs):
        if dh_init is None:
            dy_ref, h_ref, wg_ref, wu_ref, wd_ref, g_ref, u_ref, dh_ref, dw3_ref = refs
        else:
            dy_ref, h_ref, wg_ref, wu_ref, wd_ref, g_ref, u_ref, dh0_ref, dh_ref, dw3_ref = refs

        @pl.when(pl.program_id(0) == 0)
        def _():
            dh_ref[...] = jnp.zeros_like(dh_ref) if dh_init is None else dh0_ref[...]

        dyv = dy_ref[...]
        da = _mm_raw(dyv, wd_ref[...], False, True) * 0.5
        a, vjp = jax.vjp(_silu_mul, g_ref[...].astype(F32), u_ref[...].astype(F32))
        dg, du = vjp(da)
        dg = dg.astype(BF16)
        du = du.astype(BF16)
        dh_ref[...] += _mm_raw(dg, wg_ref[...]) + _mm_raw(du, wu_ref[...])
        hv = h_ref[...]
        dw3_ref[0] = _mm_raw(dg, hv, True, False).astype(BF16)
        dw3_ref[1] = _mm_raw(du, hv, True, False).astype(BF16)
        dw3_ref[2] = (_mm_raw(a, dyv, True, False) * 0.5).astype(BF16)

    full = pl.BlockSpec((s, d), lambda j: (0, 0))
    once = pl.BlockSpec((s, d), lambda j: (0, 0), pipeline_mode=pl.Buffered(1))
    tile = pl.BlockSpec((s, tn), lambda j: (0, first + j))
    in_specs = ([once, once]
                + [pl.BlockSpec((None, tn, d), functools.partial(lambda j, k: (k, first + j, 0), k=k)) for k in range(3)]
                + [tile, tile])
    args = [dyb, h, w3, w3, w3, g, u]
    if dh_init is not None:
        in_specs.append(once)
        args.append(dh_init)
    return pl.pallas_call(
        body, name=name, grid=(count,), in_specs=in_specs,
        out_specs=[full, pl.BlockSpec((3, tn, d), lambda j: (0, j, 0))],
        out_shape=[SDS((s, d), F32), SDS((3, count * tn, d), BF16)],
        compiler_params=_params("arbitrary"),
    )(*args)


def _loss_bwd(y, target, *, tm, name):
    s, d = y.shape

    def body(y_ref, t_ref, dy_ref, dyb_ref, l_ref):
        diff = y_ref[...] - t_ref[...]
        dy_ref[...] = diff * (1.0 / d)
        dyb_ref[...] = (diff * (1.0 / d)).astype(BF16)

        @pl.when(pl.program_id(0) == 0)
        def _():
            l_ref[...] = jnp.zeros_like(l_ref)

        l_ref[...] += 0.5 * jnp.sum(jnp.mean(diff * diff, axis=-1, keepdims=True), axis=0, keepdims=True)

    row = pl.BlockSpec((tm, d), lambda i: (i, 0))
    return pl.pallas_call(
        body, name=name, grid=(s // tm,), in_specs=[row, row],
        out_specs=[row, row, pl.BlockSpec((1, 1), lambda i: (0, 0))],
        out_shape=[SDS((s, d), F32), SDS((s, d), BF16), SDS((1, 1), F32)],
        compiler_params=_params("arbitrary"),
    )(y, target)


def _bucket_table():
    qi = np.arange(BLOCK)[:, None]
    kj = np.arange(2 * BLOCK)[None, :]
    dist = np.maximum(qi + BLOCK - kj, 0)
    max_exact = N_BUCKETS // 2
    d = np.maximum(dist, 1).astype(np.float32)
    large = max_exact + (np.log(d / np.float32(max_exact)) / np.float32(math.log(MAX_DISTANCE / max_exact))
                         * np.float32(N_BUCKETS - max_exact)).astype(np.int32)
    large = np.minimum(large, N_BUCKETS - 1)
    return np.where(dist < max_exact, dist, large).astype(np.int32)


SWA_STACK = SWA_GROUP * BLOCK


def _swa_valid(n):
    qi = lax.broadcasted_iota(jnp.int32, (SWA_STACK, 2 * BLOCK), 0) % BLOCK
    kj = lax.broadcasted_iota(jnp.int32, (SWA_STACK, 2 * BLOCK), 1)
    dist = qi + BLOCK - kj
    return (dist >= 0) & (dist < BLOCK) & ((kj >= BLOCK) | (n > 0))


def _swa_group(q, kb, vb, qg, kg, sink, bias, valid):
    qn = _rms(q, qg)
    kn = _rms(kb, kg)
    s = _mm(qn, kn, False, True) * (HEAD_DIM ** -0.5) + bias
    s = jnp.where(valid, s, -jnp.inf)
    m = lax.stop_gradient(jnp.maximum(jnp.max(s, axis=-1, keepdims=True), sink))
    p = jnp.exp(s - m)
    p = p / (jnp.sum(p, axis=-1, keepdims=True) + jnp.exp(sink - m))
    return _mm(p, vb)


def _swa_bias_table(rb_ref, bucket, bias_s):
    for h in range(SWA_HEADS):
        acc = jnp.zeros((BLOCK, 2 * BLOCK), F32)
        for b in range(N_BUCKETS):
            acc = jnp.where(bucket == b, rb_ref[b, h], acc)
        bias_s[h // SWA_GROUP, (h % SWA_GROUP) * BLOCK:(h % SWA_GROUP + 1) * BLOCK, :] = acc


def _swa_stack(ref, g):
    return jnp.concatenate([ref[:, (g * SWA_GROUP + hh) * HEAD_DIM:(g * SWA_GROUP + hh + 1) * HEAD_DIM]
                            for hh in range(SWA_GROUP)], axis=0)


def _swa_unstack(ref, g, stacked):
    for hh in range(SWA_GROUP):
        h = g * SWA_GROUP + hh
        ref[:, h * HEAD_DIM:(h + 1) * HEAD_DIM] = stacked[hh * BLOCK:(hh + 1) * BLOCK]


def _swa_sink_column(sink_ref, g):
    head = lax.broadcasted_iota(jnp.int32, (SWA_STACK, 1), 0) // BLOCK
    col = jnp.zeros((SWA_STACK, 1), F32)
    for hh in range(SWA_GROUP):
        col = jnp.where(head == hh, sink_ref[g * SWA_GROUP + hh], col)
    return col


def _swa_band(kvp_ref, kvc_ref, g):
    lo = g * HEAD_DIM
    kb = jnp.concatenate([kvp_ref[:, lo:lo + HEAD_DIM], kvc_ref[:, lo:lo + HEAD_DIM]], axis=0)
    lo += SWA_KV_W
    vb = jnp.concatenate([kvp_ref[:, lo:lo + HEAD_DIM], kvc_ref[:, lo:lo + HEAD_DIM]], axis=0)
    return kb, vb


def _swa_specs(order):
    kvc = COL_SKV // (2 * SWA_KV_W)
    return [
        pl.BlockSpec((BLOCK, SWA_Q_W), lambda t: (order(t), 0)),
        pl.BlockSpec((BLOCK, 2 * SWA_KV_W), lambda t: (jnp.maximum(order(t) - 1, 0), kvc)),
        pl.BlockSpec((BLOCK, 2 * SWA_KV_W), lambda t: (order(t), kvc)),
        pl.BlockSpec((1, HEAD_DIM), lambda t: (0, 0)),
        pl.BlockSpec((1, HEAD_DIM), lambda t: (0, 0)),
        pl.BlockSpec(memory_space=pltpu.SMEM),
        pl.BlockSpec(memory_space=pltpu.SMEM),
        pl.BlockSpec((BLOCK, 2 * BLOCK), lambda t: (0, 0)),
    ]


def _swa_fwd(p, qg, kg, sinks, rel_bias, *, name):
    s = p.shape[0]
    nb = s // BLOCK

    def body(q_ref, kvp_ref, kvc_ref, qg_ref, kg_ref, sink_ref, rb_ref, bucket_ref, y_ref, bias_s):
        n = pl.program_id(0)

        @pl.when(n == 0)
        def _():
            _swa_bias_table(rb_ref, bucket_ref[...], bias_s)

        valid = _swa_valid(n)
        for g in range(SWA_KV_HEADS):
            kb, vb = _swa_band(kvp_ref, kvc_ref, g)
            out = _swa_group(_swa_stack(q_ref, g), kb, vb, qg_ref[...], kg_ref[...], _swa_sink_column(sink_ref, g),
                             bias_s[g], valid)
            _swa_unstack(y_ref, g, out)

    return pl.pallas_call(
        body, name=name, grid=(nb,), in_specs=_swa_specs(lambda t: t),
        out_specs=pl.BlockSpec((BLOCK, SWA_Q_W), lambda t: (t, 0)), out_shape=SDS((s, SWA_Q_W), F32),
        scratch_shapes=[pltpu.VMEM((SWA_KV_HEADS, SWA_STACK, 2 * BLOCK), F32)],
        compiler_params=_params("arbitrary"),
    )(p, p, p, qg, kg, sinks, rel_bias, jnp.asarray(_bucket_table()))


def _swa_bwd(p, qg, kg, sinks, rel_bias, dy_all, *, name):
    s = p.shape[0]
    nb = s // BLOCK

    def body(q_ref, kvp_ref, kvc_ref, qg_ref, kg_ref, sink_ref, rb_ref, bucket_ref, dy_ref,
             dq_ref, dkv_ref, dqg_ref, dkg_ref, dsink_ref, drb_ref, bias_s, dbias_s, carry_s):
        t = pl.program_id(0)
        n = nb - 1 - t

        @pl.when(t == 0)
        def _():
            _swa_bias_table(rb_ref, bucket_ref[...], bias_s)
            dbias_s[...] = jnp.zeros_like(dbias_s)
            carry_s[...] = jnp.zeros_like(carry_s)
            dqg_ref[...] = jnp.zeros_like(dqg_ref)
            dkg_ref[...] = jnp.zeros_like(dkg_ref)
            dsink_ref[...] = jnp.zeros_like(dsink_ref)
            drb_ref[...] = jnp.zeros_like(drb_ref)

        valid = _swa_valid(n)
        lane = lax.broadcasted_iota(jnp.int32, (1, BLOCK), 1)
        dqg = jnp.zeros((1, HEAD_DIM), F32)
        dkg = jnp.zeros((1, HEAD_DIM), F32)
        dsink_vec = jnp.zeros((1, BLOCK), F32)
        for g in range(SWA_KV_HEADS):
            kb, vb = _swa_band(kvp_ref, kvc_ref, g)
            _, vjp = jax.vjp(functools.partial(_swa_group, valid=valid), _swa_stack(q_ref, g), kb, vb, qg_ref[...],
                             kg_ref[...], _swa_sink_column(sink_ref, g), bias_s[g])
            dq, dkb, dvb, dqg_g, dkg_g, dsink_col, dbias = vjp(_swa_stack(dy_ref, g))
            _swa_unstack(dq_ref, g, dq)
            dqg += dqg_g
            dkg += dkg_g
            dbias_s[g] += dbias
            for hh in range(SWA_GROUP):
                dsink_h = jnp.sum(dsink_col[hh * BLOCK:(hh + 1) * BLOCK], axis=0, keepdims=True)
                dsink_vec += jnp.where(lane == g * SWA_GROUP + hh, dsink_h, 0.0)
            lo = g * HEAD_DIM
            dkv_ref[:, lo:lo + HEAD_DIM] = dkb[BLOCK:] + carry_s[g]
            carry_s[g] = dkb[:BLOCK]
            lo += SWA_KV_W
            dkv_ref[:, lo:lo + HEAD_DIM] = dvb[BLOCK:] + carry_s[SWA_KV_HEADS + g]
            carry_s[SWA_KV_HEADS + g] = dvb[:BLOCK]
        dqg_ref[...] += dqg
        dkg_ref[...] += dkg
        dsink_ref[...] += dsink_vec

        @pl.when(t == nb - 1)
        def _():
            bucket = bucket_ref[...]
            row = lax.broadcasted_iota(jnp.int32, (N_BUCKETS, BLOCK), 0)
            col = lax.broadcasted_iota(jnp.int32, (N_BUCKETS, BLOCK), 1)
            acc = jnp.zeros((N_BUCKETS, BLOCK), F32)
            for h in range(SWA_HEADS):
                dbias = dbias_s[h // SWA_GROUP, (h % SWA_GROUP) * BLOCK:(h % SWA_GROUP + 1) * BLOCK, :]
                for b in range(N_BUCKETS):
                    part = jnp.sum(jnp.where(bucket == b, dbias, 0.0), axis=1, keepdims=True)
                    val = jnp.sum(part, axis=0, keepdims=True)
                    acc = acc + jnp.where((row == b) & (col == h), val, 0.0)
            drb_ref[...] = acc

    order = lambda t: nb - 1 - t
    vec = pl.BlockSpec((1, HEAD_DIM), lambda t: (0, 0))
    return pl.pallas_call(
        body, name=name, grid=(nb,),
        in_specs=_swa_specs(order) + [pl.BlockSpec((BLOCK, SWA_Q_W), lambda t: (order(t), 0))],
        out_specs=[pl.BlockSpec((BLOCK, SWA_Q_W), lambda t: (order(t), 0)),
                   pl.BlockSpec((BLOCK, 2 * SWA_KV_W), lambda t: (order(t), 0)),
                   vec, vec, pl.BlockSpec((1, BLOCK), lambda t: (0, 0)),
                   pl.BlockSpec((N_BUCKETS, BLOCK), lambda t: (0, 0))],
        out_shape=[SDS((s, SWA_Q_W), F32), SDS((s, 2 * SWA_KV_W), F32), SDS((1, HEAD_DIM), F32),
                   SDS((1, HEAD_DIM), F32), SDS((1, BLOCK), F32), SDS((N_BUCKETS, BLOCK), F32)],
        scratch_shapes=[pltpu.VMEM((SWA_KV_HEADS, SWA_STACK, 2 * BLOCK), F32),
                        pltpu.VMEM((SWA_KV_HEADS, SWA_STACK, 2 * BLOCK), F32),
                        pltpu.VMEM((2 * SWA_KV_HEADS, BLOCK, HEAD_DIM), F32)],
        compiler_params=_params("arbitrary"),
    )(p, p, p, qg, kg, sinks, rel_bias, jnp.asarray(_bucket_table()), dy_all)


def _mem_head(q, k, v, qg, kg):
    qn = _rms(q, qg)
    kn = _rms(k, kg)
    s = _mm(qn, kn, False, True) * (HEAD_DIM ** -0.5)
    m = lax.stop_gradient(jnp.max(s, axis=-1, keepdims=True))
    e = jnp.exp(s - m)
    return _mm(e / jnp.sum(e, axis=-1, keepdims=True), v)


def _mem_fwd(p, kv, qg, kg, *, tq, name):
    s = p.shape[0]
    m = kv.shape[0]

    def body(q_ref, kv_ref, qg_ref, kg_ref, y_ref):
        for h in range(MEM_HEADS):
            cols = slice(h * HEAD_DIM, (h + 1) * HEAD_DIM)
            vcols = slice(MEM_Q_W + h * HEAD_DIM, MEM_Q_W + (h + 1) * HEAD_DIM)
            y_ref[:, cols] = _mem_head(q_ref[:, cols], kv_ref[:, cols], kv_ref[:, vcols], qg_ref[...], kg_ref[...])

    vec = pl.BlockSpec((1, HEAD_DIM), lambda t: (0, 0))
    return pl.pallas_call(
        body, name=name, grid=(s // tq,),
        in_specs=[pl.BlockSpec((tq, MEM_Q_W), lambda t: (t, COL_MQ // MEM_Q_W)),
                  pl.BlockSpec((m, 2 * MEM_Q_W), lambda t: (0, 0)), vec, vec],
        out_specs=pl.BlockSpec((tq, MEM_Q_W), lambda t: (t, 0)), out_shape=SDS((s, MEM_Q_W), F32),
        compiler_params=_params("parallel"),
    )(p, kv, qg, kg)


def _mem_bwd(p, kv, qg, kg, dy_all, *, tq, name):
    s = p.shape[0]
    m = kv.shape[0]

    def body(q_ref, kv_ref, qg_ref, kg_ref, dy_ref, dq_ref, dkv_ref, dqg_ref, dkg_ref):
        @pl.when(pl.program_id(0) == 0)
        def _():
            dkv_ref[...] = jnp.zeros_like(dkv_ref)
            dqg_ref[...] = jnp.zeros_like(dqg_ref)
            dkg_ref[...] = jnp.zeros_like(dkg_ref)

        dqg = jnp.zeros((1, HEAD_DIM), F32)
        dkg = jnp.zeros((1, HEAD_DIM), F32)
        for h in range(MEM_HEADS):
            cols = slice(h * HEAD_DIM, (h + 1) * HEAD_DIM)
            vcols = slice(MEM_Q_W + h * HEAD_DIM, MEM_Q_W + (h + 1) * HEAD_DIM)
            _, vjp = jax.vjp(_mem_head, q_ref[:, cols], kv_ref[:, cols], kv_ref[:, vcols], qg_ref[...], kg_ref[...])
            dq, dk, dv, dqg_h, dkg_h = vjp(dy_ref[:, cols])
            dq_ref[:, cols] = dq
            dkv_ref[:, cols] += dk
            dkv_ref[:, vcols] += dv
            dqg += dqg_h
            dkg += dkg_h
        dqg_ref[...] += dqg
        dkg_ref[...] += dkg

    vec = pl.BlockSpec((1, HEAD_DIM), lambda t: (0, 0))
    full = pl.BlockSpec((m, 2 * MEM_Q_W), lambda t: (0, 0))
    dy_col = (SWA_Q_W + GLA_V_W) // MEM_Q_W
    return pl.pallas_call(
        body, name=name, grid=(s // tq,),
        in_specs=[pl.BlockSpec((tq, MEM_Q_W), lambda t: (t, COL_MQ // MEM_Q_W)), full, vec, vec,
                  pl.BlockSpec((tq, MEM_Q_W), lambda t: (t, dy_col))],
        out_specs=[pl.BlockSpec((tq, MEM_Q_W), lambda t: (t, 0)), full, vec, vec],
        out_shape=[SDS((s, MEM_Q_W), F32), SDS((m, 2 * MEM_Q_W), F32), SDS((1, HEAD_DIM), F32), SDS((1, HEAD_DIM), F32)],
        compiler_params=_params("arbitrary"),
    )(p, kv, qg, kg, dy_all)


GLA_ROWS = 256


def _gla_consts():
    c, h = GLA_CHUNK, GLA_HEADS
    i2 = lax.broadcasted_iota(jnp.int32, (c, c), 0)
    j2 = lax.broadcasted_iota(jnp.int32, (c, c), 1)
    slab_q = lax.broadcasted_iota(jnp.int32, (h, c, GLA_QK_W), 0)
    lane_q = lax.broadcasted_iota(jnp.int32, (h, c, GLA_QK_W), 2)
    row_a = lax.broadcasted_iota(jnp.int32, (h * c, c), 0)
    col_a = lax.broadcasted_iota(jnp.int32, (h * c, c), 1)
    slab_o = lax.broadcasted_iota(jnp.int32, (h, c, GLA_V_W), 0)
    lane_o = lax.broadcasted_iota(jnp.int32, (h, c, GLA_V_W), 2)
    row_s = lax.broadcasted_iota(jnp.int32, (GLA_V_W, GLA_QK_W), 0)
    col_s = lax.broadcasted_iota(jnp.int32, (GLA_V_W, GLA_QK_W), 1)
    return dict(
        ltri=(j2 <= i2).astype(F32),
        m_q=(slab_q == lane_q // GLA_DK).astype(F32),
        causal=col_a <= row_a % c,
        m_o=(slab_o == lane_o // GLA_DV).astype(F32),
        m_s=(row_s // GLA_DV == col_s // GLA_DK).astype(F32),
    )


def _gla_chunk(q, k, v, z, bg, st, c):
    h, n = GLA_HEADS, GLA_CHUNK
    la = _log_sigmoid(z + bg) * (1.0 / GLA_TAU)
    b = _mmf(c["ltri"], la)
    bl = jnp.sum(la, axis=0, keepdims=True)
    qs = q * (GLA_DK ** -0.5)
    kt = k * jnp.exp(bl - b)
    qt = qs * jnp.exp(b - bl)
    qe = qs * jnp.exp(b)
    q_stack = (jnp.broadcast_to(qt[None], (h, n, GLA_QK_W)) * c["m_q"]).reshape(h * n, GLA_QK_W)
    a = jnp.where(c["causal"], _mmf(q_stack, kt, False, True), 0.0)
    o_stack = _mm(a, v)
    o_intra = jnp.sum(o_stack.reshape(h, n, GLA_V_W) * c["m_o"], axis=0)
    o_inter = _mm(qe, st, False, True)
    st_next = st * jnp.exp(bl) + _mm(v, kt, True, False) * c["m_s"]
    return o_intra + o_inter, st_next


def _gla_post(o, gg, gain, g64):
    ms = _mmf(o * o, g64) * (1.0 / GLA_DV)
    return o * lax.rsqrt(ms + EPS) * gain * jax.nn.silu(gg)


def _gla_g64():
    r = lax.broadcasted_iota(jnp.int32, (GLA_V_W, GLA_V_W), 0)
    c = lax.broadcasted_iota(jnp.int32, (GLA_V_W, GLA_V_W), 1)
    return (r // GLA_DV == c // GLA_DV).astype(F32)


def _gla_in_specs(order):
    r = GLA_ROWS
    return [
        pl.BlockSpec((r, GLA_QK_W), lambda t: (order(t), COL_GQ // GLA_QK_W)),
        pl.BlockSpec((r, GLA_QK_W), lambda t: (order(t), COL_GK // GLA_QK_W)),
        pl.BlockSpec((r, GLA_V_W), lambda t: (order(t), COL_GV // GLA_V_W)),
        pl.BlockSpec((r, GLA_V_W), lambda t: (order(t), COL_GG // GLA_V_W)),
        pl.BlockSpec((r, GLA_QK_W), lambda t: (order(t), 0)),
        pl.BlockSpec((1, GLA_QK_W), lambda t: (0, 0)),
        pl.BlockSpec((1, GLA_V_W), lambda t: (0, 0)),
    ]


def _gla_fwd(p, z, bg, gain, *, name):
    s = p.shape[0]
    r = GLA_ROWS
    cps = r // GLA_CHUNK

    def body(q_ref, k_ref, v_ref, gg_ref, z_ref, bg_ref, gain_ref, y_ref, oraw_ref, stsave_ref, st_s):
        @pl.when(pl.program_id(0) == 0)
        def _():
            st_s[...] = jnp.zeros_like(st_s)

        c = _gla_consts()
        st = st_s[...]
        for ci in range(cps):
            rows = slice(ci * GLA_CHUNK, (ci + 1) * GLA_CHUNK)
            stsave_ref[ci] = st
            o, st = _gla_chunk(q_ref[rows, :], k_ref[rows, :], v_ref[rows, :], z_ref[rows, :], bg_ref[...], st, c)
            oraw_ref[rows, :] = o
        st_s[...] = st
        y_ref[...] = _gla_post(oraw_ref[...], gg_ref[...], gain_ref[...], _gla_g64())

    rowv = pl.BlockSpec((r, GLA_V_W), lambda t: (t, 0))
    return pl.pallas_call(
        body, name=name, grid=(s // r,), in_specs=_gla_in_specs(lambda t: t),
        out_specs=[rowv, rowv, pl.BlockSpec((cps, GLA_V_W, GLA_QK_W), lambda t: (t, 0, 0))],
        out_shape=[SDS((s, GLA_V_W), F32), SDS((s, GLA_V_W), F32), SDS((s // GLA_CHUNK, GLA_V_W, GLA_QK_W), F32)],
        scratch_shapes=[pltpu.VMEM((GLA_V_W, GLA_QK_W), F32)],
        compiler_params=_params("arbitrary"),
    )(p, p, p, p, z, bg, gain)


def _gla_bwd(p, z, bg, gain, oraw, stsave, dy_all, *, name):
    s = p.shape[0]
    r = GLA_ROWS
    cps = r // GLA_CHUNK
    nsteps = s // r
    w_qkvg = 2 * GLA_QK_W + 2 * GLA_V_W

    def body(q_ref, k_ref, v_ref, gg_ref, z_ref, bg_ref, gain_ref, oraw_ref, stsave_ref, dy_ref,
             dqkvg_ref, dz_ref, dbg_ref, dgain_ref, dst_s):
        @pl.when(pl.program_id(0) == 0)
        def _():
            dst_s[...] = jnp.zeros_like(dst_s)
            dbg_ref[...] = jnp.zeros_like(dbg_ref)
            dgain_ref[...] = jnp.zeros_like(dgain_ref)

        c = _gla_consts()
        _, vjp = jax.vjp(functools.partial(_gla_post, g64=_gla_g64()), oraw_ref[...], gg_ref[...], gain_ref[...])
        do, dgg, dgain = vjp(dy_ref[...])
        dqkvg_ref[:, 2 * GLA_QK_W + GLA_V_W:] = dgg
        dgain_ref[...] += dgain
        dst = dst_s[...]
        dbg = jnp.zeros((1, GLA_QK_W), F32)
        for ci in reversed(range(cps)):
            rows = slice(ci * GLA_CHUNK, (ci + 1) * GLA_CHUNK)
            _, vjp = jax.vjp(functools.partial(_gla_chunk, c=c), q_ref[rows, :], k_ref[rows, :], v_ref[rows, :],
                             z_ref[rows, :], bg_ref[...], stsave_ref[ci])
            dq, dk, dv, dz, dbg_c, dst = vjp((do[rows, :], dst))
            dqkvg_ref[rows, 0:GLA_QK_W] = dq
            dqkvg_ref[rows, GLA_QK_W:2 * GLA_QK_W] = dk
            dqkvg_ref[rows, 2 * GLA_QK_W:2 * GLA_QK_W + GLA_V_W] = dv
            dz_ref[rows, :] = dz
            dbg += dbg_c
        dst_s[...] = dst
        dbg_ref[...] += dbg

    order = lambda t: nsteps - 1 - t
    rowv = pl.BlockSpec((r, GLA_V_W), lambda t: (order(t), 0))
    return pl.pallas_call(
        body, name=name, grid=(nsteps,),
        in_specs=_gla_in_specs(order) + [
            rowv, pl.BlockSpec((cps, GLA_V_W, GLA_QK_W), lambda t: (order(t), 0, 0)),
            pl.BlockSpec((r, GLA_V_W), lambda t: (order(t), SWA_Q_W // GLA_V_W))],
        out_specs=[pl.BlockSpec((r, w_qkvg), lambda t: (order(t), 0)), pl.BlockSpec((r, GLA_QK_W), lambda t: (order(t), 0)),
                   pl.BlockSpec((1, GLA_QK_W), lambda t: (0, 0)), pl.BlockSpec((1, GLA_V_W), lambda t: (0, 0))],
        out_shape=[SDS((s, w_qkvg), F32), SDS((s, GLA_QK_W), F32), SDS((1, GLA_QK_W), F32), SDS((1, GLA_V_W), F32)],
        scratch_shapes=[pltpu.VMEM((GLA_V_W, GLA_QK_W), F32)],
        compiler_params=_params("arbitrary"),
    )(p, p, p, p, z, bg, gain, oraw, stsave, dy_all)


def _local_step(x, mem, target, small, big, on_grads):
    g1, gmix, gmem, g2, sqg, skg, sinks, rel_bias, wgu, bg, gla_gain, mqg, mkg = small
    w3_1, win_p, wkv, wout, gather_ffn2 = big
    wgu_pad = jnp.zeros((GLA_QK_W, GLA_QK_W), BF16).at[:GLA_RANK].set(wgu.astype(BF16))
    gain256 = jnp.tile(gla_gain, (1, GLA_HEADS))

    x1, saved1 = _ffn_fwd(x, g1, w3_1, "ffn1")
    w3_2 = gather_ffn2(x1)
    h = _rms_fwd(x1, gmix, tm=256, name="mix_rms")
    p = _matmul(h, win_p, tm=1024, tn=IN_W_PAD, name="mix_in")
    hm = _rms_fwd(mem, gmem, tm=256, name="mem_rms")
    kv = _matmul(hm, wkv, tm=256, tn=512, name="mem_kv")
    p_glr = p[:, COL_GLR:]
    z = _matmul(p_glr, wgu_pad, tm=1024, tn=GLA_QK_W, name="gla_gate")
    y_swa = _swa_fwd(p, sqg, skg, sinks, rel_bias, name="swa_fwd")
    y_gla, oraw, stsave = _gla_fwd(p, z, bg, gain256, name="gla_fwd")
    y_mem = _mem_fwd(p, kv, mqg, mkg, tq=512, name="mem_fwd")
    x2 = _matmul([y_swa, y_gla, y_mem], wout, b_blocks=[0, 2, 3], tm=1024, tn=1024, res=x1, name="mix_out")
    x3, saved2 = _ffn_fwd(x2, g2, w3_2, "ffn2")

    dy, dyb, loss = _loss_bwd(x3, target, tm=256, name="loss")
    tiles = w3_2.shape[1] // FFN_TN
    dh2, dw3_2 = _ffn_bwd_part(dyb, w3_2, saved2, 0, tiles, None, name="ffn2_bwd")
    dx2, dx2b, dg2 = _rms_bwd(x2, g2, dh2, dy, tm=256, name="ffn2_drms")
    dx2b = on_grads("ffn2", [dw3_2.reshape(3, 2, -1, dw3_2.shape[-1])], dx2b)
    dy_all = _matmul(dx2b, wout, tb=True, tm=1024, tn=1024, name="mix_dy")
    dwout = _matmul(jnp.concatenate([y_swa, y_gla, y_mem], axis=1), dx2b, ta=True, tm=512, tn=1024, out_dtype=BF16,
                    name="mix_dw_out")
    dq_swa, dkv_swa, dsqg, dskg, dsink, drb = _swa_bwd(p, sqg, skg, sinks, rel_bias, dy_all, name="swa_bwd")
    dqkvg, dz, dbg, dgain256 = _gla_bwd(p, z, bg, gain256, oraw, stsave, dy_all, name="gla_bwd")
    dmq, dkv_mem, dmqg, dmkg = _mem_bwd(p, kv, mqg, mkg, dy_all, tq=512, name="mem_bwd")
    dglr = _matmul(dz, wgu_pad, tb=True, tm=1024, tn=GLA_QK_W, name="gla_gate_dx")
    dwgu_pad = _matmul(p_glr, dz, ta=True, tm=GLA_QK_W, tn=GLA_QK_W, name="gla_gate_dw")
    dp = jnp.concatenate([dq_swa, dkv_swa, dqkvg, dmq, dglr], axis=1)
    dh = _matmul(dp, win_p, tb=True, tm=1024, tn=1024, name="mix_dh")
    dwin_p = _matmul(h, dp, ta=True, tm=1024, tn=640, out_dtype=BF16, name="mix_dw_in")
    dx1, dx1b, dgmix = _rms_bwd(x1, gmix, dh, dx2, tm=256, name="mix_drms")
    dwkv = _matmul(hm, dkv_mem, ta=True, tm=512, tn=512, out_dtype=BF16, name="mem_dw_kv")
    dx1b = on_grads("mix", (dwin_p, dwkv, dwout), dx1b)
    dhm = _matmul(dkv_mem, wkv, tb=True, tm=256, tn=512, name="mem_dh")
    _, _, dgmem = _rms_bwd(mem, gmem, dhm, None, tm=256, name="mem_drms")
    dh1, dw3_1a = _ffn_bwd_part(dx1b, w3_1, saved1, 0, tiles // 2, None, name="ffn1_bwd_a")
    dh1, dgmem, dwgu_pad = on_grads("ffn1a", [dw3_1a[:, None]], (dh1, dgmem, dwgu_pad))
    dh1, dw3_1b = _ffn_bwd_part(dx1b, w3_1, saved1, tiles // 2, tiles // 2, dh1, name="ffn1_bwd_b")
    dx, _, dg1 = _rms_bwd(x, g1, dh1, dx1, tm=256, name="ffn1_drms")

    dgla_gain = dgain256.reshape(GLA_HEADS, GLA_DV).sum(axis=0, keepdims=True)
    dsmall = (dg1, dgmix, dgmem, dg2, dsqg, dskg, dsink[0, :SWA_HEADS], drb[:, :SWA_HEADS], dwgu_pad[:GLA_RANK], dbg,
              dgla_gain, dmqg, dmkg)
    on_grads("ffn1b", [dw3_1b[:, None]], None, small=list(dsmall) + [loss])
    return dx


def _mesh_place():
    x, y, c = lax.axis_index("x"), lax.axis_index("y"), lax.axis_index("c")
    other_chips = [(1 - x, y), (x, 1 - y), (1 - x, 1 - y)]
    return x, y, c, other_chips


def _handshake(peers):
    barrier = pltpu.get_barrier_semaphore()
    for peer in peers:
        pl.semaphore_signal(barrier, inc=1, device_id=peer, device_id_type=MESH)
    pl.semaphore_wait(barrier, len(peers))


def _sequencer_call(body, operands, out_shapes, sems, *, name, collective_id):
    return pl.kernel(
        body, name=name, out_type=out_shapes, mesh=plsc.ScalarSubcoreMesh(axis_name="sequencer", num_cores=1),
        scratch_types=sems, compiler_params=pltpu.CompilerParams(collective_id=collective_id),
    )(*operands)


def _window(ref, kind, slot, shape):
    if kind == "row":
        rows = pl.ds(pl.multiple_of(slot * shape[-2], 8), shape[-2])
        return ref.at[(slice(None),) * (len(shape) - 2) + (rows,)]
    return ref.at[slot]


def _gathered(shape, kind):
    if kind == "row":
        return tuple(shape[:-2]) + (N_DEV * shape[-2], shape[-1])
    return (N_DEV,) + tuple(shape)


def _all_gather(shards, kinds, *, name, collective_id):
    nt = len(shards)

    def body(*refs):
        x_refs, o_refs = refs[:nt], refs[nt:2 * nt]
        send_sems, recv_sems, local_sems = refs[2 * nt:]
        x, y, c, chips = _mesh_place()
        me, sibling = (x, y, c), (x, y, 1 - c)
        _handshake([sibling] + [(*chip, c) for chip in chips])

        def copy(k, t, block, to, from_shard=False):
            bx, by, bc = block
            rows = _window(o_refs[t], kinds[t], 4 * bx + 2 * by + bc, shards[t].shape)
            return pltpu.make_async_remote_copy(
                src_ref=x_refs[t] if from_shard else rows, dst_ref=rows,
                send_sem=send_sems.at[k, t], recv_sem=recv_sems.at[k, t], device_id=to, device_id_type=MESH)

        mine = [pltpu.make_async_copy(x_refs[t], _window(o_refs[t], kinds[t], 4 * x + 2 * y + c, shards[t].shape),
                                      local_sems.at[t]) for t in range(nt)]
        for cp in mine:
            cp.start()
        first = [copy(0, t, me, sibling, True) for t in range(nt)]
        first += [copy(1 + j, t, me, (*chip, c), True) for j, chip in enumerate(chips) for t in range(nt)]
        for cp in first:
            cp.start()
        passed = []
        for j, chip in enumerate(chips):
            for t in range(nt):
                copy(1 + j, t, (*chip, c), me).wait_recv()
                fwd = copy(4 + j, t, (*chip, c), sibling)
                fwd.start()
                passed.append(fwd)
        for t in range(nt):
            copy(0, t, sibling, me).wait_recv()
        for j, chip in enumerate(chips):
            for t in range(nt):
                copy(4 + j, t, (*chip, 1 - c), me).wait_recv()
        for cp in first + passed:
            cp.wait_send()
        for cp in mine:
            cp.wait()

    return _sequencer_call(
        body, shards, [SDS(_gathered(s.shape, k), s.dtype) for s, k in zip(shards, kinds)],
        [pltpu.SemaphoreType.DMA((7, nt)), pltpu.SemaphoreType.DMA((7, nt)), pltpu.SemaphoreType.DMA((nt,))],
        name=name, collective_id=collective_id)


def _part_shape(shape, kind):
    if kind == "row":
        return tuple(shape[:-2]) + (shape[-2] // N_DEV, shape[-1])
    return tuple(shape[2:])


def _pair_exchange(grads, kinds, *, name, collective_id):
    nt = len(grads)
    part = [_part_shape(g.shape, k) for g, k in zip(grads, kinds)]

    def body(*refs):
        g_refs, o_refs = refs[:nt], refs[nt:2 * nt]
        send_sems, recv_sems = refs[2 * nt:]
        x, y, c, _ = _mesh_place()
        _handshake([(x, y, 1 - c)])
        copies = []
        for t in range(nt):
            for xy in range(4):
                src = g_refs[t].at[1 - c, xy] if kinds[t] == "stack" else _window(g_refs[t], kinds[t], 2 * xy + 1 - c, part[t])
                copies.append(pltpu.make_async_remote_copy(
                    src_ref=src, dst_ref=o_refs[t].at[xy], send_sem=send_sems.at[xy, t], recv_sem=recv_sems.at[xy, t],
                    device_id=(x, y, 1 - c), device_id_type=MESH))
        for cp in copies:
            cp.start()
        for cp in copies:
            cp.wait()

    return _sequencer_call(
        body, grads, [SDS((4,) + p, g.dtype) for p, g in zip(part, grads)],
        [pltpu.SemaphoreType.DMA((4, nt)), pltpu.SemaphoreType.DMA((4, nt))], name=name, collective_id=collective_id)


def _chip_exchange(parts, small, *, name, collective_id):
    nt = len(parts)
    if small is None:
        def body_plain(*refs):
            s_refs, o_refs = refs[:nt], refs[nt:2 * nt]
            send_sems, recv_sems = refs[2 * nt:]
            x, y, c, chips = _mesh_place()
            _handshake([(*chip, c) for chip in chips])
            copies = [pltpu.make_async_remote_copy(
                src_ref=s_refs[t].at[2 * chip[0] + chip[1]], dst_ref=o_refs[t].at[j],
                send_sem=send_sems.at[j, t], recv_sem=recv_sems.at[j, t], device_id=(*chip, c), device_id_type=MESH)
                for j, chip in enumerate(chips) for t in range(nt)]
            for cp in copies:
                cp.start()
            for cp in copies:
                cp.wait()

        return _sequencer_call(
            body_plain, parts, [SDS((3,) + s.shape[1:], s.dtype) for s in parts],
            [pltpu.SemaphoreType.DMA((3, nt)), pltpu.SemaphoreType.DMA((3, nt))], name=name, collective_id=collective_id)

    def body(*refs):
        s_refs, small_ref = refs[:nt], refs[nt]
        o_refs, small_all = refs[nt + 1:2 * nt + 1], refs[2 * nt + 1]
        send_sems, recv_sems, small_send, small_recv, local_sem = refs[2 * nt + 2:]
        x, y, c, chips = _mesh_place()
        _handshake([(px, py, pc) for px in (x, 1 - x) for py in (y, 1 - y) for pc in (c, 1 - c)][1:])

        def copy(j, t, chip):
            return pltpu.make_async_remote_copy(
                src_ref=s_refs[t].at[2 * chip[0] + chip[1]], dst_ref=o_refs[t].at[j],
                send_sem=send_sems.at[j, t], recv_sem=recv_sems.at[j, t], device_id=(*chip, c), device_id_type=MESH)

        flips = [(fx, fy, fc) for fx in (0, 1) for fy in (0, 1) for fc in (0, 1)][1:]

        def small_copy(k):
            fx, fy, fc = flips[k]
            to = (x ^ fx if fx else x, y ^ fy if fy else y, c ^ fc if fc else c)
            rows = small_all.at[4 * x + 2 * y + c]
            return pltpu.make_async_remote_copy(
                src_ref=small_ref, dst_ref=rows, send_sem=small_send.at[k], recv_sem=small_recv.at[k],
                device_id=to, device_id_type=MESH)

        own = pltpu.make_async_copy(small_ref, small_all.at[4 * x + 2 * y + c], local_sem)
        own.start()
        copies = [copy(j, t, chip) for j, chip in enumerate(chips) for t in range(nt)]
        smalls = [small_copy(k) for k in range(7)]
        for cp in smalls + copies:
            cp.start()
        for cp in smalls + copies:
            cp.wait()
        own.wait()

    return _sequencer_call(
        body, list(parts) + [small],
        [SDS((3,) + s.shape[1:], s.dtype) for s in parts] + [SDS((N_DEV,) + small.shape, small.dtype)],
        [pltpu.SemaphoreType.DMA((3, nt)), pltpu.SemaphoreType.DMA((3, nt)),
         pltpu.SemaphoreType.DMA((7,)), pltpu.SemaphoreType.DMA((7,)), pltpu.SemaphoreType.DMA],
        name=name, collective_id=collective_id)


def _pair_sum(grad, theirs, kind, c, *, name):
    if kind == "row":
        r, l = theirs.shape[-2:]
        n = theirs.size // (4 * r * l)
        grad, theirs = grad.reshape(n, N_DEV * r, l), theirs.reshape(4, n, r, l)
        mine_spec = pl.BlockSpec((None, r, l), lambda xy, m, c_ref: (m, 2 * xy + c_ref[0], 0))
    else:
        r, l = theirs.shape[-2:]
        n = theirs.size // (4 * r * l)
        theirs = theirs.reshape(4, n, r, l)
        grad = grad.reshape(2, 4, n, r, l)
        mine_spec = pl.BlockSpec((None, None, None, r, l), lambda xy, m, c_ref: (c_ref[0], xy, m, 0, 0))

    def body(c_ref, a_ref, b_ref, o_ref):
        o_ref[...] = (a_ref[...].astype(F32) + b_ref[...].astype(F32)).astype(BF16)

    part = pl.BlockSpec((None, None, r, l), lambda xy, m, c_ref: (xy, m, 0, 0))
    return pl.pallas_call(
        body, name=name,
        grid_spec=pltpu.PrefetchScalarGridSpec(num_scalar_prefetch=1, grid=(4, n), in_specs=[mine_spec, part], out_specs=part),
        out_shape=SDS((4, n, r, l), BF16), compiler_params=_params("parallel", "parallel"),
    )(c, grad, theirs)


def _adamw(w, g, m, v):
    m = ADAM_B1 * m + (1.0 - ADAM_B1) * g
    v = ADAM_B2 * v + (1.0 - ADAM_B2) * jnp.square(g)
    m_hat = m / (1.0 - ADAM_B1 ** ADAM_STEP)
    v_hat = v / (1.0 - ADAM_B2 ** ADAM_STEP)
    delta = -ADAM_LR * (m_hat / (jnp.sqrt(v_hat) + ADAM_EPS) + ADAM_WD * w)
    return delta, m, v


def _adam_big(owns, others, mat, xy, w, m, v, *, tr, name):
    nw, r, l = w.shape
    lp = owns[0].shape[-1]
    nq = len(owns)
    assert nq in (1, nw)

    def body(xy_ref, *refs):
        own_refs, oth_refs = refs[:nq], refs[nq:2 * nq]
        w_ref, m_ref, v_ref, g_out, d_out, m_out, v_out = refs[2 * nq:]
        g = None
        for q in range(nq):
            gq = own_refs[q][0, 0].astype(F32)
            for j in range(3):
                gq = gq + oth_refs[q][j, 0].astype(F32)
            g = gq if g is None else jnp.where(pl.program_id(0) == q, gq, g)
        g = g[:, :l]
        delta, m_new, v_new = _adamw(w_ref[0], g, m_ref[0], v_ref[0])
        g_out[0] = g
        d_out[0] = delta
        m_out[0] = m_new
        v_out[0] = v_new

    def at(p):
        return mat * nw + p if nq == 1 else mat

    blk = pl.BlockSpec((1, tr, l), lambda p, i, xy_ref: (p, i, 0))
    return pl.pallas_call(
        body, name=name,
        grid_spec=pltpu.PrefetchScalarGridSpec(
            num_scalar_prefetch=1, grid=(nw, r // tr),
            in_specs=[pl.BlockSpec((1, 1, tr, lp), lambda p, i, xy_ref: (xy_ref[0], at(p), i, 0))] * nq
            + [pl.BlockSpec((3, 1, tr, lp), lambda p, i, xy_ref: (0, at(p), i, 0))] * nq + [blk, blk, blk],
            out_specs=[blk, blk, blk, blk]),
        out_shape=[SDS(w.shape, F32)] * 4, compiler_params=_params("parallel", "parallel"),
    )(xy, *owns, *others, w, m, v)


def _adam_small(g_all, w, m, v, *, name):
    def body(g_ref, w_ref, m_ref, v_ref, g_out, d_out, m_out, v_out):
        g = g_ref[0]
        for k in range(1, N_DEV):
            g = g + g_ref[k]
        delta, m_new, v_new = _adamw(w_ref[...], g, m_ref[...], v_ref[...])
        g_out[...] = g
        d_out[...] = delta
        m_out[...] = m_new
        v_out[...] = v_new

    return pl.pallas_call(body, name=name, out_shape=[SDS(w.shape, F32)] * 4)(g_all, w, m, v)


SMALL_ROWS = 56


def _pack_small(parts):
    flat = jnp.concatenate([a.reshape(-1) for a in parts])
    return jnp.pad(flat, (0, SMALL_ROWS * 128 - flat.shape[0])).reshape(SMALL_ROWS, 128)


def _unpack_small(packed, shapes):
    flat = packed.reshape(-1)
    out, at = [], 0
    for s in shapes:
        n = math.prod(s)
        out.append(flat[at:at + n].reshape(s))
        at += n
    return out


def kernel(x, mem, ffn1_norm, ffn1_w_gate, ffn1_w_up, ffn1_w_down, mix_norm, mem_norm, w_in, w_mem_kv, swa_q_norm, swa_k_norm, swa_sinks, rel_bias, gla_w_gate_up, gla_b_gate, gla_out_norm, mem_q_norm, mem_k_norm, w_out, ffn2_norm, ffn2_w_gate, ffn2_w_up, ffn2_w_down, loss_target, m_ffn1_norm, m_ffn1_w_gate, m_ffn1_w_up, m_ffn1_w_down, m_mix_norm, m_mem_norm, m_w_in, m_w_mem_kv, m_swa_q_norm, m_swa_k_norm, m_swa_sinks, m_rel_bias, m_gla_w_gate_up, m_gla_b_gate, m_gla_out_norm, m_mem_q_norm, m_mem_k_norm, m_w_out, m_ffn2_norm, m_ffn2_w_gate, m_ffn2_w_up, m_ffn2_w_down, v_ffn1_norm, v_ffn1_w_gate, v_ffn1_w_up, v_ffn1_w_down, v_mix_norm, v_mem_norm, v_w_in, v_w_mem_kv, v_swa_q_norm, v_swa_k_norm, v_swa_sinks, v_rel_bias, v_gla_w_gate_up, v_gla_b_gate, v_gla_out_norm, v_mem_q_norm, v_mem_k_norm, v_w_out, v_ffn2_norm, v_ffn2_w_gate, v_ffn2_w_up, v_ffn2_w_down):
    xi, yi, ci = lax.axis_index("x"), lax.axis_index("y"), lax.axis_index("c")
    c_arr = jnp.reshape(ci, (1,)).astype(jnp.int32)
    xy_arr = jnp.reshape(2 * xi + yi, (1,)).astype(jnp.int32)
    d = x.shape[-1]

    half_h = ffn1_w_gate.shape[-1] // 2

    def gather_ffn(wg_s, wu_s, wd_s, name, collective_id, after):
        w3_s = jnp.concatenate([wg_s.transpose(0, 2, 1), wu_s.transpose(0, 2, 1), wd_s], axis=0)
        w3_s = jnp.pad(w3_s.reshape(3, 2, half_h, d), ((0, 0), (0, 0), (0, FFN_HALF_PAD - half_h), (0, 0))).astype(BF16)
        if after is not None:
            w3_s, _ = lax.optimization_barrier((w3_s, after))
        return _all_gather([w3_s], ["row"], name=name, collective_id=collective_id)[0].reshape(3, -1, d)

    w3_1 = gather_ffn(ffn1_w_gate, ffn1_w_up, ffn1_w_down, "gather_ffn1", 0, None)
    mix_s = lax.optimization_barrier((w_in[0].astype(BF16), w_mem_kv[0].astype(BF16), w_out[0].astype(BF16), w3_1))[:3]
    win_all, wkv, wout = _all_gather(list(mix_s), ["stack", "row", "row"], name="gather_mix", collective_id=1)

    def gather_ffn2(x1):
        return gather_ffn(ffn2_w_gate, ffn2_w_up, ffn2_w_down, "gather_ffn2", 2, (wout, x1))

    win_p = _pack_win(win_all, tr=256, name="pack_w_in")

    small_w = [ffn1_norm, mix_norm, mem_norm, ffn2_norm, swa_q_norm, swa_k_norm, swa_sinks[0], rel_bias,
               gla_w_gate_up[0], gla_b_gate, gla_out_norm, mem_q_norm, mem_k_norm]
    collective_ids = {"ffn2": (3, 4), "mix": (5, 6), "ffn1a": (7, 8), "ffn1b": (9, 10)}
    reduced, small_box = {}, []

    def on_grads(group, grads, carry, small=None):
        if group == "mix":
            dwin_p, dwkv, dwout = grads
            grads = [_unpack_win(dwin_p, tr=256, name="unpack_dw_in"), dwkv, dwout]
            kinds = ["stack", "row", "row"]
        else:
            kinds = ["row"]
        if reduced:
            earlier = list(reduced.values())[-1][1]
            *grads, _ = lax.optimization_barrier((*grads, earlier[0]))
        id_pair, id_chip = collective_ids[group]
        from_sibling = _pair_exchange(grads, kinds, name=f"pair_exchange_{group}", collective_id=id_pair)
        chip_sums = [_pair_sum(g, theirs, k, c_arr, name=f"pair_sum_{group}_{t}")
                     for t, (g, theirs, k) in enumerate(zip(grads, from_sibling, kinds))]
        if carry is not None:
            *chip_sums, carry = lax.optimization_barrier((*chip_sums, carry))
        if small is None:
            from_chips = _chip_exchange(chip_sums, None, name=f"chip_exchange_{group}", collective_id=id_chip)
        else:
            *from_chips, small_all = _chip_exchange(chip_sums, _pack_small(small), name=f"chip_exchange_{group}",
                                                    collective_id=id_chip)
            small_box.append(small_all)
        reduced[group] = (chip_sums, from_chips)
        return carry

    grad_x = _local_step(x[0], mem[0], loss_target[0], small_w, (w3_1, win_p, wkv, wout, gather_ffn2), on_grads)
    small_all = small_box[0]

    big_w = {"ffn1_w_gate": ("ffn1", 0, 0, True, ffn1_w_gate, m_ffn1_w_gate, v_ffn1_w_gate),
             "ffn1_w_up": ("ffn1", 0, 1, True, ffn1_w_up, m_ffn1_w_up, v_ffn1_w_up),
             "ffn1_w_down": ("ffn1", 0, 2, False, ffn1_w_down, m_ffn1_w_down, v_ffn1_w_down),
             "w_in": ("mix", 0, 0, False, w_in, m_w_in, v_w_in),
             "w_mem_kv": ("mix", 1, 0, False, w_mem_kv, m_w_mem_kv, v_w_mem_kv),
             "w_out": ("mix", 2, 0, False, w_out, m_w_out, v_w_out),
             "ffn2_w_gate": ("ffn2", 0, 0, True, ffn2_w_gate, m_ffn2_w_gate, v_ffn2_w_gate),
             "ffn2_w_up": ("ffn2", 0, 1, True, ffn2_w_up, m_ffn2_w_up, v_ffn2_w_up),
             "ffn2_w_down": ("ffn2", 0, 2, False, ffn2_w_down, m_ffn2_w_down, v_ffn2_w_down)}
    res = {}
    for nm, (group, t, mat, transposed, w, m, v) in big_w.items():
        shape = w.shape
        if transposed:
            w, m, v = (a.transpose(0, 2, 1) for a in (w, m, v))
        if group != "mix":
            w, m, v = (a.reshape(2, half_h, d) for a in (w, m, v))
        r = w.shape[1]
        tr = 256 if r % 256 == 0 else r
        halves = ["ffn1a", "ffn1b"] if group == "ffn1" else [group]
        out = _adam_big([reduced[k][0][t] for k in halves], [reduced[k][1][t] for k in halves], mat, xy_arr, w, m, v,
                        tr=tr, name=f"adam_{nm}")
        if transposed:
            out = [a.reshape(1, -1, d).transpose(0, 2, 1) for a in out]
        res[nm] = [a.reshape(shape) for a in out]
    small_names = ["ffn1_norm", "mix_norm", "mem_norm", "ffn2_norm", "swa_q_norm", "swa_k_norm", "swa_sinks", "rel_bias",
                   "gla_w_gate_up", "gla_b_gate", "gla_out_norm", "mem_q_norm", "mem_k_norm"]
    small_m = [m_ffn1_norm, m_mix_norm, m_mem_norm, m_ffn2_norm, m_swa_q_norm, m_swa_k_norm, m_swa_sinks, m_rel_bias,
               m_gla_w_gate_up, m_gla_b_gate, m_gla_out_norm, m_mem_q_norm, m_mem_k_norm]
    small_v = [v_ffn1_norm, v_mix_norm, v_mem_norm, v_ffn2_norm, v_swa_q_norm, v_swa_k_norm, v_swa_sinks, v_rel_bias,
               v_gla_w_gate_up, v_gla_b_gate, v_gla_out_norm, v_mem_q_norm, v_mem_k_norm]
    small_full = [ffn1_norm, mix_norm, mem_norm, ffn2_norm, swa_q_norm, swa_k_norm, swa_sinks, rel_bias,
                  gla_w_gate_up, gla_b_gate, gla_out_norm, mem_q_norm, mem_k_norm]
    packed = _adam_small(small_all, _pack_small(small_full), _pack_small(small_m), _pack_small(small_v), name="adam_small")
    full_shapes = [a.shape for a in small_full]
    unpacked = [_unpack_small(pk, full_shapes + [()]) for pk in packed]
    for k, nm in enumerate(small_names):
        res[nm] = [unpacked[q][k] for q in range(4)]
    loss = unpacked[0][len(small_names)]

    order = ["ffn1_norm", "ffn1_w_gate", "ffn1_w_up", "ffn1_w_down", "mix_norm", "mem_norm", "w_in", "w_mem_kv",
             "swa_q_norm", "swa_k_norm", "swa_sinks", "rel_bias", "gla_w_gate_up", "gla_b_gate", "gla_out_norm",
             "mem_q_norm", "mem_k_norm", "w_out", "ffn2_norm", "ffn2_w_gate", "ffn2_w_up", "ffn2_w_down"]
    outs = [loss, grad_x[None]]
    for q in range(4):
        outs += [res[nm][q] for nm in order]
    return tuple(outs)
```

```python
import functools
import math

import numpy as np
import jax
import jax.numpy as jnp
from jax import lax
from jax.experimental import pallas as pl
from jax.experimental.pallas import tpu as pltpu
from jax.experimental.pallas import tpu_sc as plsc

F32 = jnp.float32
BF16 = jnp.bfloat16
SDS = jax.ShapeDtypeStruct

EPS = 1e-6
HEAD_DIM = 64
SWA_HEADS = 8
SWA_KV_HEADS = 2
SWA_GROUP = SWA_HEADS // SWA_KV_HEADS
BLOCK = 128
N_BUCKETS = 32
MAX_DISTANCE = 128
GLA_HEADS = 4
GLA_DK = 32
GLA_DV = 64
GLA_RANK = 16
GLA_TAU = 16.0
GLA_CHUNK = 32
MEM_HEADS = 4
SWA_Q_W = SWA_HEADS * HEAD_DIM
SWA_KV_W = SWA_KV_HEADS * HEAD_DIM
GLA_QK_W = GLA_HEADS * GLA_DK
GLA_V_W = GLA_HEADS * GLA_DV
MEM_Q_W = MEM_HEADS * HEAD_DIM
IN_W = 1808
IN_W_PAD = 1920
COL_SQ, COL_SKV, COL_GQ, COL_GK, COL_GV, COL_GG, COL_MQ, COL_GLR = 0, 512, 768, 896, 1024, 1280, 1536, 1792

ADAM_LR = 0.001
ADAM_B1 = 0.9
ADAM_B2 = 0.999
ADAM_EPS = 1e-08
ADAM_WD = 0.01
ADAM_STEP = 10

N_DEV = 8
VMEM_LIMIT_BYTES = 56 * 1024 * 1024
MESH = pl.DeviceIdType.MESH


def _params(*sem):
    return pltpu.CompilerParams(dimension_semantics=sem or None, vmem_limit_bytes=VMEM_LIMIT_BYTES)


def _dot(a, b, ta, tb, precision=None):
    dims = (((0 if ta else 1,), (1 if tb else 0,)), ((), ()))
    return lax.dot_general(a, b, dims, preferred_element_type=F32, precision=precision)


def _mm_raw(a, b, ta=False, tb=False):
    return _dot(a.astype(BF16), b.astype(BF16), ta, tb)


def _mmf_raw(a, b, ta=False, tb=False):
    return _dot(a, b, ta, tb, lax.Precision.HIGHEST)


def _make_mm(raw):
    @functools.partial(jax.custom_vjp, nondiff_argnums=(2, 3))
    def mm(a, b, ta=False, tb=False):
        return raw(a, b, ta, tb)

    def fwd(a, b, ta, tb):
        return raw(a, b, ta, tb), (a, b)

    def bwd(ta, tb, res, g):
        a, b = res
        da = raw(b, g, tb, True) if ta else raw(g, b, False, not tb)
        db = raw(g, a, True, ta) if tb else raw(a, g, not ta, False)
        return da, db

    mm.defvjp(fwd, bwd)
    return mm


_mm = _make_mm(_mm_raw)
_mmf = _make_mm(_mmf_raw)


def _rms(x, g):
    return x * lax.rsqrt(jnp.mean(x * x, axis=-1, keepdims=True) + EPS) * g


def _silu_mul(g, u):
    return jax.nn.silu(g) * u


def _log_sigmoid(z):
    return jnp.minimum(z, 0.0) - jnp.log(1.0 + jnp.exp(-jnp.abs(z)))


def _matmul(a_list, b, *, ta=False, tb=False, tm, tn, b_blocks=None, res=None, scale=1.0, out_dtype=F32, name):
    if not isinstance(a_list, (list, tuple)):
        a_list = [a_list]
    n_a = len(a_list)
    m = a_list[0].shape[1] if ta else a_list[0].shape[0]
    ks = [a.shape[0] if ta else a.shape[1] for a in a_list]
    n = b.shape[0] if tb else b.shape[1]
    if b_blocks is None:
        assert n_a == 1
        b_blocks = [0]
    tm, tn = min(tm, m), min(tn, n)
    assert m % tm == 0 and n % tn == 0, (m, n, tm, tn)

    def body(*refs):
        a_refs, b_refs = refs[:n_a], refs[n_a:2 * n_a]
        r_ref = refs[2 * n_a] if res is not None else None
        o_ref = refs[-1]
        acc = _mm_raw(a_refs[0][...], b_refs[0][...], ta, tb)
        for k in range(1, n_a):
            acc = acc + _mm_raw(a_refs[k][...], b_refs[k][...], ta, tb)
        if scale != 1.0:
            acc = acc * scale
        if r_ref is not None:
            acc = r_ref[...] + acc
        o_ref[...] = acc.astype(out_dtype)

    in_specs = []
    for k in ks:
        in_specs.append(pl.BlockSpec((k, tm), lambda i, j: (0, i)) if ta else pl.BlockSpec((tm, k), lambda i, j: (i, 0)))
    for k, blk in zip(ks, b_blocks):
        if tb:
            in_specs.append(pl.BlockSpec((tn, k), functools.partial(lambda i, j, blk: (j, blk), blk=blk)))
        else:
            in_specs.append(pl.BlockSpec((k, tn), functools.partial(lambda i, j, blk: (blk, j), blk=blk)))
    args = list(a_list) + [b] * n_a
    if res is not None:
        in_specs.append(pl.BlockSpec((tm, tn), lambda i, j: (i, j)))
        args.append(res)
    return pl.pallas_call(
        body, name=name, grid=(m // tm, n // tn), in_specs=in_specs,
        out_specs=pl.BlockSpec((tm, tn), lambda i, j: (i, j)), out_shape=SDS((m, n), out_dtype),
        compiler_params=_params("parallel", "parallel"),
    )(*args)


def _win_pieces(w):
    glr_lo, glr_hi = COL_MQ, COL_MQ + GLA_RANK
    out = []
    for j in range(N_DEV):
        for lo, hi, shift in ((0, glr_lo, 0), (glr_lo, glr_hi, COL_GLR - glr_lo), (glr_hi, IN_W, COL_MQ - glr_hi)):
            s, e = max(j * w, lo), min((j + 1) * w, hi)
            if s < e:
                out.append((j, s - j * w, e - j * w, s + shift))
    return out


def _pack_win(win_all, *, tr, name):
    _, d, w = win_all.shape

    def body(i_ref, o_ref):
        for j, a, b, dst in _win_pieces(w):
            o_ref[:, dst:dst + b - a] = i_ref[j][:, a:b]
        o_ref[:, IN_W:] = jnp.zeros((tr, IN_W_PAD - IN_W), o_ref.dtype)

    return pl.pallas_call(
        body, name=name, grid=(d // tr,), in_specs=[pl.BlockSpec((N_DEV, tr, w), lambda i: (0, i, 0))],
        out_specs=pl.BlockSpec((tr, IN_W_PAD), lambda i: (i, 0)), out_shape=SDS((d, IN_W_PAD), win_all.dtype),
        compiler_params=_params("parallel"),
    )(win_all)


def _unpack_win(dwin_p, *, tr, name):
    d = dwin_p.shape[0]
    w = IN_W // N_DEV

    def body(i_ref, o_ref):
        for j, a, b, src in _win_pieces(w):
            o_ref[j % 2, j // 2, :, a:b] = i_ref[:, src:src + b - a]

    return pl.pallas_call(
        body, name=name, grid=(d // tr,), in_specs=[pl.BlockSpec((tr, IN_W_PAD), lambda i: (i, 0))],
        out_specs=pl.BlockSpec((2, 4, tr, w), lambda i: (0, 0, i, 0)), out_shape=SDS((2, 4, d, w), dwin_p.dtype),
        compiler_params=_params("parallel"),
    )(dwin_p)


def _rms_fwd(x, g, *, tm, name):
    s, d = x.shape

    def body(x_ref, g_ref, h_ref):
        h_ref[...] = _rms(x_ref[...], g_ref[...]).astype(BF16)

    return pl.pallas_call(
        body, name=name, grid=(s // tm,),
        in_specs=[pl.BlockSpec((tm, d), lambda i: (i, 0)), pl.BlockSpec((1, d), lambda i: (0, 0))],
        out_specs=pl.BlockSpec((tm, d), lambda i: (i, 0)), out_shape=SDS((s, d), BF16),
        compiler_params=_params("parallel"),
    )(x, g)


def _rms_bwd(x, g, dh, dres, *, tm, name):
    s, d = x.shape
    want_dx = dres is not None

    def body(*refs):
        if want_dx:
            x_ref, g_ref, dh_ref, dres_ref, dx_ref, dxb_ref, dg_ref = refs
        else:
            x_ref, g_ref, dh_ref, dg_ref = refs
        _, vjp = jax.vjp(_rms, x_ref[...], g_ref[...])
        dx, dg = vjp(dh_ref[...])
        if want_dx:
            dx = dres_ref[...] + dx
            dx_ref[...] = dx
            dxb_ref[...] = dx.astype(BF16)

        @pl.when(pl.program_id(0) == 0)
        def _():
            dg_ref[...] = jnp.zeros_like(dg_ref)

        dg_ref[...] += dg

    row = pl.BlockSpec((tm, d), lambda i: (i, 0))
    vec = pl.BlockSpec((1, d), lambda i: (0, 0))
    if want_dx:
        return pl.pallas_call(
            body, name=name, grid=(s // tm,), in_specs=[row, vec, row, row], out_specs=[row, row, vec],
            out_shape=[SDS((s, d), F32), SDS((s, d), BF16), SDS((1, d), F32)], compiler_params=_params("arbitrary"),
        )(x, g, dh, dres)
    return None, None, pl.pallas_call(
        body, name=name, grid=(s // tm,), in_specs=[row, vec, row], out_specs=vec,
        out_shape=SDS((1, d), F32), compiler_params=_params("arbitrary"),
    )(x, g, dh)


FFN_TN = 256
FFN_HALF_PAD = 192


def _ffn_fwd(x, gain, w3, tag, *, tm=1024):
    s, d = x.shape
    f = w3.shape[1]
    tn = FFN_TN
    nj = f // tn
    tm = min(tm, s)

    def body(x_ref, gain_ref, wg_ref, wu_ref, wd_ref, y_ref, h_ref, g_ref, u_ref, acc_s):
        j = pl.program_id(1)

        @pl.when(j == 0)
        def _():
            h_ref[...] = _rms(x_ref[...], gain_ref[...]).astype(BF16)
            acc_s[...] = jnp.zeros_like(acc_s)

        hv = h_ref[...]
        g = _mm_raw(hv, wg_ref[...], False, True)
        u = _mm_raw(hv, wu_ref[...], False, True)
        g_ref[...] = g.astype(BF16)
        u_ref[...] = u.astype(BF16)
        acc_s[...] += _mm_raw(_silu_mul(g, u), wd_ref[...])

        @pl.when(j == nj - 1)
        def _():
            y_ref[...] = x_ref[...] + 0.5 * acc_s[...]

    row = pl.BlockSpec((tm, d), lambda i, j: (i, 0))
    tile = pl.BlockSpec((tm, tn), lambda i, j: (i, j))
    y, h, g, u = pl.pallas_call(
        body, name=f"{tag}_fwd", grid=(s // tm, nj),
        in_specs=[row, pl.BlockSpec((1, d), lambda i, j: (0, 0))]
        + [pl.BlockSpec((None, tn, d), functools.partial(lambda i, j, k: (k, j, 0), k=k)) for k in range(3)],
        out_specs=[row, row, tile, tile],
        out_shape=[SDS((s, d), F32), SDS((s, d), BF16), SDS((s, f), BF16), SDS((s, f), BF16)],
        scratch_shapes=[pltpu.VMEM((tm, d), F32)],
        compiler_params=_params("parallel", "arbitrary"),
    )(x, gain, w3, w3, w3)
    return y, (h, g, u)


def _ffn_bwd_part(dyb, w3, saved, first, count, dh_init, *, name):
    h, g, u = saved
    s, d = h.shape
    tn = FFN_TN

    def body(*refs):
        if dh_init is None:
            dy_ref, h_ref, wg_ref, wu_ref, wd_ref, g_ref, u_ref, dh_ref, dw3_ref = refs
        else:
            dy_ref, h_ref, wg_ref, wu_ref, wd_ref, g_ref, u_ref, dh0_ref, dh_ref, dw3_ref = refs

        @pl.when(pl.program_id(0) == 0)
        def _():
            dh_ref[...] = jnp.zeros_like(dh_ref) if dh_init is None else dh0_ref[...]

        dyv = dy_ref[...]
        da = _mm_raw(dyv, wd_ref[...], False, True) * 0.5
        a, vjp = jax.vjp(_silu_mul, g_ref[...].astype(F32), u_ref[...].astype(F32))
        dg, du = vjp(da)
        dg = dg.astype(BF16)
        du = du.astype(BF16)
        dh_ref[...] += _mm_raw(dg, wg_ref[...]) + _mm_raw(du, wu_ref[...])
        hv = h_ref[...]
        dw3_ref[0] = _mm_raw(dg, hv, True, False).astype(BF16)
        dw3_ref[1] = _mm_raw(du, hv, True, False).astype(BF16)
        dw3_ref[2] = (_mm_raw(a, dyv, True, False) * 0.5).astype(BF16)

    full = pl.BlockSpec((s, d), lambda j: (0, 0))
    once = pl.BlockSpec((s, d), lambda j: (0, 0), pipeline_mode=pl.Buffered(1))
    tile = pl.BlockSpec((s, tn), lambda j: (0, first + j))
    in_specs = ([once, once]
                + [pl.BlockSpec((None, tn, d), functools.partial(lambda j, k: (k, first + j, 0), k=k)) for k in range(3)]
                + [tile, tile])
    args = [dyb, h, w3, w3, w3, g, u]
    if dh_init is not None:
        in_specs.append(once)
        args.append(dh_init)
    return pl.pallas_call(
        body, name=name, grid=(count,), in_specs=in_specs,
        out_specs=[full, pl.BlockSpec((3, tn, d), lambda j: (0, j, 0))],
        out_shape=[SDS((s, d), F32), SDS((3, count * tn, d), BF16)],
        compiler_params=_params("arbitrary"),
    )(*args)


def _loss_bwd(y, target, *, tm, name):
    s, d = y.shape

    def body(y_ref, t_ref, dy_ref, dyb_ref, l_ref):
        diff = y_ref[...] - t_ref[...]
        dy_ref[...] = diff * (1.0 / d)
        dyb_ref[...] = (diff * (1.0 / d)).astype(BF16)

        @pl.when(pl.program_id(0) == 0)
        def _():
            l_ref[...] = jnp.zeros_like(l_ref)

        l_ref[...] += 0.5 * jnp.sum(jnp.mean(diff * diff, axis=-1, keepdims=True), axis=0, keepdims=True)

    row = pl.BlockSpec((tm, d), lambda i: (i, 0))
    return pl.pallas_call(
        body, name=name, grid=(s // tm,), in_specs=[row, row],
        out_specs=[row, row, pl.BlockSpec((1, 1), lambda i: (0, 0))],
        out_shape=[SDS((s, d), F32), SDS((s, d), BF16), SDS((1, 1), F32)],
        compiler_params=_params("arbitrary"),
    )(y, target)


def _bucket_table():
    qi = np.arange(BLOCK)[:, None]
    kj = np.arange(2 * BLOCK)[None, :]
    dist = np.maximum(qi + BLOCK - kj, 0)
    max_exact = N_BUCKETS // 2
    d = np.maximum(dist, 1).astype(np.float32)
    large = max_exact + (np.log(d / np.float32(max_exact)) / np.float32(math.log(MAX_DISTANCE / max_exact))
                         * np.float32(N_BUCKETS - max_exact)).astype(np.int32)
    large = np.minimum(large, N_BUCKETS - 1)
    return np.where(dist < max_exact, dist, large).astype(np.int32)


SWA_STACK = SWA_GROUP * BLOCK


def _swa_valid(n):
    qi = lax.broadcasted_iota(jnp.int32, (SWA_STACK, 2 * BLOCK), 0) % BLOCK
    kj = lax.broadcasted_iota(jnp.int32, (SWA_STACK, 2 * BLOCK), 1)
    dist = qi + BLOCK - kj
    return (dist >= 0) & (dist < BLOCK) & ((kj >= BLOCK) | (n > 0))


def _swa_group(q, kb, vb, qg, kg, sink, bias, valid):
    qn = _rms(q, qg)
    kn = _rms(kb, kg)
    s = _mm(qn, kn, False, True) * (HEAD_DIM ** -0.5) + bias
    s = jnp.where(valid, s, -jnp.inf)
    m = lax.stop_gradient(jnp.maximum(jnp.max(s, axis=-1, keepdims=True), sink))
    p = jnp.exp(s - m)
    p = p / (jnp.sum(p, axis=-1, keepdims=True) + jnp.exp(sink - m))
    return _mm(p, vb)


def _swa_bias_table(rb_ref, bucket, bias_s):
    for h in range(SWA_HEADS):
        acc = jnp.zeros((BLOCK, 2 * BLOCK), F32)
        for b in range(N_BUCKETS):
            acc = jnp.where(bucket == b, rb_ref[b, h], acc)
        bias_s[h // SWA_GROUP, (h % SWA_GROUP) * BLOCK:(h % SWA_GROUP + 1) * BLOCK, :] = acc


def _swa_stack(ref, g):
    return jnp.concatenate([ref[:, (g * SWA_GROUP + hh) * HEAD_DIM:(g * SWA_GROUP + hh + 1) * HEAD_DIM]
                            for hh in range(SWA_GROUP)], axis=0)


def _swa_unstack(ref, g, stacked):
    for hh in range(SWA_GROUP):
        h = g * SWA_GROUP + hh
        ref[:, h * HEAD_DIM:(h + 1) * HEAD_DIM] = stacked[hh * BLOCK:(hh + 1) * BLOCK]


def _swa_sink_column(sink_ref, g):
    head = lax.broadcasted_iota(jnp.int32, (SWA_STACK, 1), 0) // BLOCK
    col = jnp.zeros((SWA_STACK, 1), F32)
    for hh in range(SWA_GROUP):
        col = jnp.where(head == hh, sink_ref[g * SWA_GROUP + hh], col)
    return col


def _swa_band(kvp_ref, kvc_ref, g):
    lo = g * HEAD_DIM
    kb = jnp.concatenate([kvp_ref[:, lo:lo + HEAD_DIM], kvc_ref[:, lo:lo + HEAD_DIM]], axis=0)
    lo += SWA_KV_W
    vb = jnp.concatenate([kvp_ref[:, lo:lo + HEAD_DIM], kvc_ref[:, lo:lo + HEAD_DIM]], axis=0)
    return kb, vb


def _swa_specs(order):
    kvc = COL_SKV // (2 * SWA_KV_W)
    return [
        pl.BlockSpec((BLOCK, SWA_Q_W), lambda t: (order(t), 0)),
        pl.BlockSpec((BLOCK, 2 * SWA_KV_W), lambda t: (jnp.maximum(order(t) - 1, 0), kvc)),
        pl.BlockSpec((BLOCK, 2 * SWA_KV_W), lambda t: (order(t), kvc)),
        pl.BlockSpec((1, HEAD_DIM), lambda t: (0, 0)),
        pl.BlockSpec((1, HEAD_DIM), lambda t: (0, 0)),
        pl.BlockSpec(memory_space=pltpu.SMEM),
        pl.BlockSpec(memory_space=pltpu.SMEM),
        pl.BlockSpec((BLOCK, 2 * BLOCK), lambda t: (0, 0)),
    ]


def _swa_fwd(p, qg, kg, sinks, rel_bias, *, name):
    s = p.shape[0]
    nb = s // BLOCK

    def body(q_ref, kvp_ref, kvc_ref, qg_ref, kg_ref, sink_ref, rb_ref, bucket_ref, y_ref, bias_s):
        n = pl.program_id(0)

        @pl.when(n == 0)
        def _():
            _swa_bias_table(rb_ref, bucket_ref[...], bias_s)

        valid = _swa_valid(n)
        for g in range(SWA_KV_HEADS):
            kb, vb = _swa_band(kvp_ref, kvc_ref, g)
            out = _swa_group(_swa_stack(q_ref, g), kb, vb, qg_ref[...], kg_ref[...], _swa_sink_column(sink_ref, g),
                             bias_s[g], valid)
            _swa_unstack(y_ref, g, out)

    return pl.pallas_call(
        body, name=name, grid=(nb,), in_specs=_swa_specs(lambda t: t),
        out_specs=pl.BlockSpec((BLOCK, SWA_Q_W), lambda t: (t, 0)), out_shape=SDS((s, SWA_Q_W), F32),
        scratch_shapes=[pltpu.VMEM((SWA_KV_HEADS, SWA_STACK, 2 * BLOCK), F32)],
        compiler_params=_params("arbitrary"),
    )(p, p, p, qg, kg, sinks, rel_bias, jnp.asarray(_bucket_table()))


def _swa_bwd(p, qg, kg, sinks, rel_bias, dy_all, *, name):
    s = p.shape[0]
    nb = s // BLOCK

    def body(q_ref, kvp_ref, kvc_ref, qg_ref, kg_ref, sink_ref, rb_ref, bucket_ref, dy_ref,
             dq_ref, dkv_ref, dqg_ref, dkg_ref, dsink_ref, drb_ref, bias_s, dbias_s, carry_s):
        t = pl.program_id(0)
        n = nb - 1 - t

        @pl.when(t == 0)
        def _():
            _swa_bias_table(rb_ref, bucket_ref[...], bias_s)
            dbias_s[...] = jnp.zeros_like(dbias_s)
            carry_s[...] = jnp.zeros_like(carry_s)
            dqg_ref[...] = jnp.zeros_like(dqg_ref)
            dkg_ref[...] = jnp.zeros_like(dkg_ref)
            dsink_ref[...] = jnp.zeros_like(dsink_ref)
            drb_ref[...] = jnp.zeros_like(drb_ref)

        valid = _swa_valid(n)
        lane = lax.broadcasted_iota(jnp.int32, (1, BLOCK), 1)
        dqg = jnp.zeros((1, HEAD_DIM), F32)
        dkg = jnp.zeros((1, HEAD_DIM), F32)
        dsink_vec = jnp.zeros((1, BLOCK), F32)
        for g in range(SWA_KV_HEADS):
            kb, vb = _swa_band(kvp_ref, kvc_ref, g)
            _, vjp = jax.vjp(functools.partial(_swa_group, valid=valid), _swa_stack(q_ref, g), kb, vb, qg_ref[...],
                             kg_ref[...], _swa_sink_column(sink_ref, g), bias_s[g])
            dq, dkb, dvb, dqg_g, dkg_g, dsink_col, dbias = vjp(_swa_stack(dy_ref, g))
            _swa_unstack(dq_ref, g, dq)
            dqg += dqg_g
            dkg += dkg_g
            dbias_s[g] += dbias
            for hh in range(SWA_GROUP):
                dsink_h = jnp.sum(dsink_col[hh * BLOCK:(hh + 1) * BLOCK], axis=0, keepdims=True)
                dsink_vec += jnp.where(lane == g * SWA_GROUP + hh, dsink_h, 0.0)
            lo = g * HEAD_DIM
            dkv_ref[:, lo:lo + HEAD_DIM] = dkb[BLOCK:] + carry_s[g]
            carry_s[g] = dkb[:BLOCK]
            lo += SWA_KV_W
            dkv_ref[:, lo:lo + HEAD_DIM] = dvb[BLOCK:] + carry_s[SWA_KV_HEADS + g]
            carry_s[SWA_KV_HEADS + g] = dvb[:BLOCK]
        dqg_ref[...] += dqg
        dkg_ref[...] += dkg
        dsink_ref[...] += dsink_vec

        @pl.when(t == nb - 1)
        def _():
            bucket = bucket_ref[...]
            row = lax.broadcasted_iota(jnp.int32, (N_BUCKETS, BLOCK), 0)
            col = lax.broadcasted_iota(jnp.int32, (N_BUCKETS, BLOCK), 1)
            acc = jnp.zeros((N_BUCKETS, BLOCK), F32)
            for h in range(SWA_HEADS):
                dbias = dbias_s[h // SWA_GROUP, (h % SWA_GROUP) * BLOCK:(h % SWA_GROUP + 1) * BLOCK, :]
                for b in range(N_BUCKETS):
                    part = jnp.sum(jnp.where(bucket == b, dbias, 0.0), axis=1, keepdims=True)
                    val = jnp.sum(part, axis=0, keepdims=True)
                    acc = acc + jnp.where((row == b) & (col == h), val, 0.0)
            drb_ref[...] = acc

    order = lambda t: nb - 1 - t
    vec = pl.BlockSpec((1, HEAD_DIM), lambda t: (0, 0))
    return pl.pallas_call(
        body, name=name, grid=(nb,),
        in_specs=_swa_specs(order) + [pl.BlockSpec((BLOCK, SWA_Q_W), lambda t: (order(t), 0))],
        out_specs=[pl.BlockSpec((BLOCK, SWA_Q_W), lambda t: (order(t), 0)),
                   pl.BlockSpec((BLOCK, 2 * SWA_KV_W), lambda t: (order(t), 0)),
                   vec, vec, pl.BlockSpec((1, BLOCK), lambda t: (0, 0)),
                   pl.BlockSpec((N_BUCKETS, BLOCK), lambda t: (0, 0))],
        out_shape=[SDS((s, SWA_Q_W), F32), SDS((s, 2 * SWA_KV_W), F32), SDS((1, HEAD_DIM), F32),
                   SDS((1, HEAD_DIM), F32), SDS((1, BLOCK), F32), SDS((N_BUCKETS, BLOCK), F32)],
        scratch_shapes=[pltpu.VMEM((SWA_KV_HEADS, SWA_STACK, 2 * BLOCK), F32),
                        pltpu.VMEM((SWA_KV_HEADS, SWA_STACK, 2 * BLOCK), F32),
                        pltpu.VMEM((2 * SWA_KV_HEADS, BLOCK, HEAD_DIM), F32)],
        compiler_params=_params("arbitrary"),
    )(p, p, p, qg, kg, sinks, rel_bias, jnp.asarray(_bucket_table()), dy_all)


def _mem_head(q, k, v, qg, kg):
    qn = _rms(q, qg)
    kn = _rms(k, kg)
    s = _mm(qn, kn, False, True) * (HEAD_DIM ** -0.5)
    m = lax.stop_gradient(jnp.max(s, axis=-1, keepdims=True))
    e = jnp.exp(s - m)
    return _mm(e / jnp.sum(e, axis=-1, keepdims=True), v)


def _mem_fwd(p, kv, qg, kg, *, tq, name):
    s = p.shape[0]
    m = kv.shape[0]

    def body(q_ref, kv_ref, qg_ref, kg_ref, y_ref):
        for h in range(MEM_HEADS):
            cols = slice(h * HEAD_DIM, (h + 1) * HEAD_DIM)
            vcols = slice(MEM_Q_W + h * HEAD_DIM, MEM_Q_W + (h + 1) * HEAD_DIM)
            y_ref[:, cols] = _mem_head(q_ref[:, cols], kv_ref[:, cols], kv_ref[:, vcols], qg_ref[...], kg_ref[...])

    vec = pl.BlockSpec((1, HEAD_DIM), lambda t: (0, 0))
    return pl.pallas_call(
        body, name=name, grid=(s // tq,),
        in_specs=[pl.BlockSpec((tq, MEM_Q_W), lambda t: (t, COL_MQ // MEM_Q_W)),
                  pl.BlockSpec((m, 2 * MEM_Q_W), lambda t: (0, 0)), vec, vec],
        out_specs=pl.BlockSpec((tq, MEM_Q_W), lambda t: (t, 0)), out_shape=SDS((s, MEM_Q_W), F32),
        compiler_params=_params("parallel"),
    )(p, kv, qg, kg)


def _mem_bwd(p, kv, qg, kg, dy_all, *, tq, name):
    s = p.shape[0]
    m = kv.shape[0]

    def body(q_ref, kv_ref, qg_ref, kg_ref, dy_ref, dq_ref, dkv_ref, dqg_ref, dkg_ref):
        @pl.when(pl.program_id(0) == 0)
        def _():
            dkv_ref[...] = jnp.zeros_like(dkv_ref)
            dqg_ref[...] = jnp.zeros_like(dqg_ref)
            dkg_ref[...] = jnp.zeros_like(dkg_ref)

        dqg = jnp.zeros((1, HEAD_DIM), F32)
        dkg = jnp.zeros((1, HEAD_DIM), F32)
        for h in range(MEM_HEADS):
            cols = slice(h * HEAD_DIM, (h + 1) * HEAD_DIM)
            vcols = slice(MEM_Q_W + h * HEAD_DIM, MEM_Q_W + (h + 1) * HEAD_DIM)
            _, vjp = jax.vjp(_mem_head, q_ref[:, cols], kv_ref[:, cols], kv_ref[:, vcols], qg_ref[...], kg_ref[...])
            dq, dk, dv, dqg_h, dkg_h = vjp(dy_ref[:, cols])
            dq_ref[:, cols] = dq
            dkv_ref[:, cols] += dk
            dkv_ref[:, vcols] += dv
            dqg += dqg_h
            dkg += dkg_h
        dqg_ref[...] += dqg
        dkg_ref[...] += dkg

    vec = pl.BlockSpec((1, HEAD_DIM), lambda t: (0, 0))
    full = pl.BlockSpec((m, 2 * MEM_Q_W), lambda t: (0, 0))
    dy_col = (SWA_Q_W + GLA_V_W) // MEM_Q_W
    return pl.pallas_call(
        body, name=name, grid=(s // tq,),
        in_specs=[pl.BlockSpec((tq, MEM_Q_W), lambda t: (t, COL_MQ // MEM_Q_W)), full, vec, vec,
                  pl.BlockSpec((tq, MEM_Q_W), lambda t: (t, dy_col))],
        out_specs=[pl.BlockSpec((tq, MEM_Q_W), lambda t: (t, 0)), full, vec, vec],
        out_shape=[SDS((s, MEM_Q_W), F32), SDS((m, 2 * MEM_Q_W), F32), SDS((1, HEAD_DIM), F32), SDS((1, HEAD_DIM), F32)],
        compiler_params=_params("arbitrary"),
    )(p, kv, qg, kg, dy_all)


GLA_ROWS = 256


def _gla_consts():
    c, h = GLA_CHUNK, GLA_HEADS
    i2 = lax.broadcasted_iota(jnp.int32, (c, c), 0)
    j2 = lax.broadcasted_iota(jnp.int32, (c, c), 1)
    slab_q = lax.broadcasted_iota(jnp.int32, (h, c, GLA_QK_W), 0)
    lane_q = lax.broadcasted_iota(jnp.int32, (h, c, GLA_QK_W), 2)
    row_a = lax.broadcasted_iota(jnp.int32, (h * c, c), 0)
    col_a = lax.broadcasted_iota(jnp.int32, (h * c, c), 1)
    slab_o = lax.broadcasted_iota(jnp.int32, (h, c, GLA_V_W), 0)
    lane_o = lax.broadcasted_iota(jnp.int32, (h, c, GLA_V_W), 2)
    row_s = lax.broadcasted_iota(jnp.int32, (GLA_V_W, GLA_QK_W), 0)
    col_s = lax.broadcasted_iota(jnp.int32, (GLA_V_W, GLA_QK_W), 1)
    return dict(
        ltri=(j2 <= i2).astype(F32),
        m_q=(slab_q == lane_q // GLA_DK).astype(F32),
        causal=col_a <= row_a % c,
        m_o=(slab_o == lane_o // GLA_DV).astype(F32),
        m_s=(row_s // GLA_DV == col_s // GLA_DK).astype(F32),
    )


def _gla_chunk(q, k, v, z, bg, st, c):
    h, n = GLA_HEADS, GLA_CHUNK
    la = _log_sigmoid(z + bg) * (1.0 / GLA_TAU)
    b = _mmf(c["ltri"], la)
    bl = jnp.sum(la, axis=0, keepdims=True)
    qs = q * (GLA_DK ** -0.5)
    kt = k * jnp.exp(bl - b)
    qt = qs * jnp.exp(b - bl)
    qe = qs * jnp.exp(b)
    q_stack = (jnp.broadcast_to(qt[None], (h, n, GLA_QK_W)) * c["m_q"]).reshape(h * n, GLA_QK_W)
    a = jnp.where(c["causal"], _mmf(q_stack, kt, False, True), 0.0)
    o_stack = _mm(a, v)
    o_intra = jnp.sum(o_stack.reshape(h, n, GLA_V_W) * c["m_o"], axis=0)
    o_inter = _mm(qe, st, False, True)
    st_next = st * jnp.exp(bl) + _mm(v, kt, True, False) * c["m_s"]
    return o_intra + o_inter, st_next


def _gla_post(o, gg, gain, g64):
    ms = _mmf(o * o, g64) * (1.0 / GLA_DV)
    return o * lax.rsqrt(ms + EPS) * gain * jax.nn.silu(gg)


def _gla_g64():
    r = lax.broadcasted_iota(jnp.int32, (GLA_V_W, GLA_V_W), 0)
    c = lax.broadcasted_iota(jnp.int32, (GLA_V_W, GLA_V_W), 1)
    return (r // GLA_DV == c // GLA_DV).astype(F32)


def _gla_in_specs(order):
    r = GLA_ROWS
    return [
        pl.BlockSpec((r, GLA_QK_W), lambda t: (order(t), COL_GQ // GLA_QK_W)),
        pl.BlockSpec((r, GLA_QK_W), lambda t: (order(t), COL_GK // GLA_QK_W)),
        pl.BlockSpec((r, GLA_V_W), lambda t: (order(t), COL_GV // GLA_V_W)),
        pl.BlockSpec((r, GLA_V_W), lambda t: (order(t), COL_GG // GLA_V_W)),
        pl.BlockSpec((r, GLA_QK_W), lambda t: (order(t), 0)),
        pl.BlockSpec((1, GLA_QK_W), lambda t: (0, 0)),
        pl.BlockSpec((1, GLA_V_W), lambda t: (0, 0)),
    ]


def _gla_fwd(p, z, bg, gain, *, name):
    s = p.shape[0]
    r = GLA_ROWS
    cps = r // GLA_CHUNK

    def body(q_ref, k_ref, v_ref, gg_ref, z_ref, bg_ref, gain_ref, y_ref, oraw_ref, stsave_ref, st_s):
        @pl.when(pl.program_id(0) == 0)
        def _():
            st_s[...] = jnp.zeros_like(st_s)

        c = _gla_consts()
        st = st_s[...]
        for ci in range(cps):
            rows = slice(ci * GLA_CHUNK, (ci + 1) * GLA_CHUNK)
            stsave_ref[ci] = st
            o, st = _gla_chunk(q_ref[rows, :], k_ref[rows, :], v_ref[rows, :], z_ref[rows, :], bg_ref[...], st, c)
            oraw_ref[rows, :] = o
        st_s[...] = st
        y_ref[...] = _gla_post(oraw_ref[...], gg_ref[...], gain_ref[...], _gla_g64())

    rowv = pl.BlockSpec((r, GLA_V_W), lambda t: (t, 0))
    return pl.pallas_call(
        body, name=name, grid=(s // r,), in_specs=_gla_in_specs(lambda t: t),
        out_specs=[rowv, rowv, pl.BlockSpec((cps, GLA_V_W, GLA_QK_W), lambda t: (t, 0, 0))],
        out_shape=[SDS((s, GLA_V_W), F32), SDS((s, GLA_V_W), F32), SDS((s // GLA_CHUNK, GLA_V_W, GLA_QK_W), F32)],
        scratch_shapes=[pltpu.VMEM((GLA_V_W, GLA_QK_W), F32)],
        compiler_params=_params("arbitrary"),
    )(p, p, p, p, z, bg, gain)


def _gla_bwd(p, z, bg, gain, oraw, stsave, dy_all, *, name):
    s = p.shape[0]
    r = GLA_ROWS
    cps = r // GLA_CHUNK
    nsteps = s // r
    w_qkvg = 2 * GLA_QK_W + 2 * GLA_V_W

    def body(q_ref, k_ref, v_ref, gg_ref, z_ref, bg_ref, gain_ref, oraw_ref, stsave_ref, dy_ref,
             dqkvg_ref, dz_ref, dbg_ref, dgain_ref, dst_s):
        @pl.when(pl.program_id(0) == 0)
        def _():
            dst_s[...] = jnp.zeros_like(dst_s)
            dbg_ref[...] = jnp.zeros_like(dbg_ref)
            dgain_ref[...] = jnp.zeros_like(dgain_ref)

        c = _gla_consts()
        _, vjp = jax.vjp(functools.partial(_gla_post, g64=_gla_g64()), oraw_ref[...], gg_ref[...], gain_ref[...])
        do, dgg, dgain = vjp(dy_ref[...])
        dqkvg_ref[:, 2 * GLA_QK_W + GLA_V_W:] = dgg
        dgain_ref[...] += dgain
        dst = dst_s[...]
        dbg = jnp.zeros((1, GLA_QK_W), F32)
        for ci in reversed(range(cps)):
            rows = slice(ci * GLA_CHUNK, (ci + 1) * GLA_CHUNK)
            _, vjp = jax.vjp(functools.partial(_gla_chunk, c=c), q_ref[rows, :], k_ref[rows, :], v_ref[rows, :],
                             z_ref[rows, :], bg_ref[...], stsave_ref[ci])
            dq, dk, dv, dz, dbg_c, dst = vjp((do[rows, :], dst))
            dqkvg_ref[rows, 0:GLA_QK_W] = dq
            dqkvg_ref[rows, GLA_QK_W:2 * GLA_QK_W] = dk
            dqkvg_ref[rows, 2 * GLA_QK_W:2 * GLA_QK_W + GLA_V_W] = dv
            dz_ref[rows, :] = dz
            dbg += dbg_c
        dst_s[...] = dst
        dbg_ref[...] += dbg

    order = lambda t: nsteps - 1 - t
    rowv = pl.BlockSpec((r, GLA_V_W), lambda t: (order(t), 0))
    return pl.pallas_call(
        body, name=name, grid=(nsteps,),
        in_specs=_gla_in_specs(order) + [
            rowv, pl.BlockSpec((cps, GLA_V_W, GLA_QK_W), lambda t: (order(t), 0, 0)),
            pl.BlockSpec((r, GLA_V_W), lambda t: (order(t), SWA_Q_W // GLA_V_W))],
        out_specs=[pl.BlockSpec((r, w_qkvg), lambda t: (order(t), 0)), pl.BlockSpec((r, GLA_QK_W), lambda t: (order(t), 0)),
                   pl.BlockSpec((1, GLA_QK_W), lambda t: (0, 0)), pl.BlockSpec((1, GLA_V_W), lambda t: (0, 0))],
        out_shape=[SDS((s, w_qkvg), F32), SDS((s, GLA_QK_W), F32), SDS((1, GLA_QK_W), F32), SDS((1, GLA_V_W), F32)],
        scratch_shapes=[pltpu.VMEM((GLA_V_W, GLA_QK_W), F32)],
        compiler_params=_params("arbitrary"),
    )(p, p, p, p, z, bg, gain, oraw, stsave, dy_all)


def _local_step(x, mem, target, small, big, on_grads):
    g1, gmix, gmem, g2, sqg, skg, sinks, rel_bias, wgu, bg, gla_gain, mqg, mkg = small
    w3_1, win_p, wkv, wout, gather_ffn2 = big
    wgu_pad = jnp.zeros((GLA_QK_W, GLA_QK_W), BF16).at[:GLA_RANK].set(wgu.astype(BF16))
    gain256 = jnp.tile(gla_gain, (1, GLA_HEADS))

    x1, saved1 = _ffn_fwd(x, g1, w3_1, "ffn1")
    w3_2 = gather_ffn2(x1)
    h = _rms_fwd(x1, gmix, tm=256, name="mix_rms")
    p = _matmul(h, win_p, tm=1024, tn=IN_W_PAD, name="mix_in")
    hm = _rms_fwd(mem, gmem, tm=256, name="mem_rms")
    kv = _matmul(hm, wkv, tm=256, tn=512, name="mem_kv")
    p_glr = p[:, COL_GLR:]
    z = _matmul(p_glr, wgu_pad, tm=1024, tn=GLA_QK_W, name="gla_gate")
    y_swa = _swa_fwd(p, sqg, skg, sinks, rel_bias, name="swa_fwd")
    y_gla, oraw, stsave = _gla_fwd(p, z, bg, gain256, name="gla_fwd")
    y_mem = _mem_fwd(p, kv, mqg, mkg, tq=512, name="mem_fwd")
    x2 = _matmul([y_swa, y_gla, y_mem], wout, b_blocks=[0, 2, 3], tm=1024, tn=1024, res=x1, name="mix_out")
    x3, saved2 = _ffn_fwd(x2, g2, w3_2, "ffn2")

    dy, dyb, loss = _loss_bwd(x3, target, tm=256, name="loss")
    tiles = w3_2.shape[1] // FFN_TN
    dh2, dw3_2 = _ffn_bwd_part(dyb, w3_2, saved2, 0, tiles, None, name="ffn2_bwd")
    dx2, dx2b, dg2 = _rms_bwd(x2, g2, dh2, dy, tm=256, name="ffn2_drms")
    dx2b = on_grads("ffn2", [dw3_2.reshape(3, 2, -1, dw3_2.shape[-1])], dx2b)
    dy_all = _matmul(dx2b, wout, tb=True, tm=1024, tn=1024, name="mix_dy")
    dwout = _matmul(jnp.concatenate([y_swa, y_gla, y_mem], axis=1), dx2b, ta=True, tm=512, tn=1024, out_dtype=BF16,
                    name="mix_dw_out")
    dq_swa, dkv_swa, dsqg, dskg, dsink, drb = _swa_bwd(p, sqg, skg, sinks, rel_bias, dy_all, name="swa_bwd")
    dqkvg, dz, dbg, dgain256 = _gla_bwd(p, z, bg, gain256, oraw, stsave, dy_all, name="gla_bwd")
    dmq, dkv_mem, dmqg, dmkg = _mem_bwd(p, kv, mqg, mkg, dy_all, tq=512, name="mem_bwd")
    dglr = _matmul(dz, wgu_pad, tb=True, tm=1024, tn=GLA_QK_W, name="gla_gate_dx")
    dwgu_pad = _matmul(p_glr, dz, ta=True, tm=GLA_QK_W, tn=GLA_QK_W, name="gla_gate_dw")
    dp = jnp.concatenate([dq_swa, dkv_swa, dqkvg, dmq, dglr], axis=1)
    dh = _matmul(dp, win_p, tb=True, tm=1024, tn=1024, name="mix_dh")
    dwin_p = _matmul(h, dp, ta=True, tm=1024, tn=640, out_dtype=BF16, name="mix_dw_in")
    dx1, dx1b, dgmix = _rms_bwd(x1, gmix, dh, dx2, tm=256, name="mix_drms")
    dwkv = _matmul(hm, dkv_mem, ta=True, tm=512, tn=512, out_dtype=BF16, name="mem_dw_kv")
    dx1b = on_grads("mix", (dwin_p, dwkv, dwout), dx1b)
    dhm = _matmul(dkv_mem, wkv, tb=True, tm=256, tn=512, name="mem_dh")
    _, _, dgmem = _rms_bwd(mem, gmem, dhm, None, tm=256, name="mem_drms")
    dh1, dw3_1a = _ffn_bwd_part(dx1b, w3_1, saved1, 0, tiles // 2, None, name="ffn1_bwd_a")
    dgla_gain = dgain256.reshape(GLA_HEADS, GLA_DV).sum(axis=0, keepdims=True)
    dsmall = [dgmix, dgmem, dg2, dsqg, dskg, dsink[0, :SWA_HEADS], drb[:, :SWA_HEADS], dwgu_pad[:GLA_RANK], dbg,
              dgla_gain, dmqg, dmkg, loss]
    dh1, dgmem, dwgu_pad = on_grads("ffn1a", [dw3_1a[:, None]], (dh1, dgmem, dwgu_pad), small=dsmall)
    dh1, dw3_1b = _ffn_bwd_part(dx1b, w3_1, saved1, tiles // 2, tiles // 2, dh1, name="ffn1_bwd_b")
    dx, _, dg1 = _rms_bwd(x, g1, dh1, dx1, tm=256, name="ffn1_drms")
    on_grads("ffn1b", [dw3_1b[:, None]], None, small=[dg1])
    return dx


def _mesh_place():
    x, y, c = lax.axis_index("x"), lax.axis_index("y"), lax.axis_index("c")
    other_chips = [(1 - x, y), (x, 1 - y), (1 - x, 1 - y)]
    return x, y, c, other_chips


def _handshake(peers):
    barrier = pltpu.get_barrier_semaphore()
    for peer in peers:
        pl.semaphore_signal(barrier, inc=1, device_id=peer, device_id_type=MESH)
    pl.semaphore_wait(barrier, len(peers))


def _sequencer_call(body, operands, out_shapes, sems, *, name, collective_id):
    return pl.kernel(
        body, name=name, out_type=out_shapes, mesh=plsc.ScalarSubcoreMesh(axis_name="sequencer", num_cores=1),
        scratch_types=sems, compiler_params=pltpu.CompilerParams(collective_id=collective_id),
    )(*operands)


def _window(ref, kind, slot, shape):
    if kind == "row":
        rows = pl.ds(pl.multiple_of(slot * shape[-2], 8), shape[-2])
        return ref.at[(slice(None),) * (len(shape) - 2) + (rows,)]
    return ref.at[slot]


def _gathered(shape, kind):
    if kind == "row":
        return tuple(shape[:-2]) + (N_DEV * shape[-2], shape[-1])
    return (N_DEV,) + tuple(shape)


def _all_gather(shards, kinds, *, name, collective_id):
    nt = len(shards)

    def body(*refs):
        x_refs, o_refs = refs[:nt], refs[nt:2 * nt]
        send_sems, recv_sems, local_sems = refs[2 * nt:]
        x, y, c, chips = _mesh_place()
        me, sibling = (x, y, c), (x, y, 1 - c)
        _handshake([sibling] + [(*chip, c) for chip in chips])

        def copy(k, t, block, to, from_shard=False):
            bx, by, bc = block
            rows = _window(o_refs[t], kinds[t], 4 * bx + 2 * by + bc, shards[t].shape)
            return pltpu.make_async_remote_copy(
                src_ref=x_refs[t] if from_shard else rows, dst_ref=rows,
                send_sem=send_sems.at[k, t], recv_sem=recv_sems.at[k, t], device_id=to, device_id_type=MESH)

        mine = [pltpu.make_async_copy(x_refs[t], _window(o_refs[t], kinds[t], 4 * x + 2 * y + c, shards[t].shape),
                                      local_sems.at[t]) for t in range(nt)]
        for cp in mine:
            cp.start()
        first = [copy(0, t, me, sibling, True) for t in range(nt)]
        first += [copy(1 + j, t, me, (*chip, c), True) for j, chip in enumerate(chips) for t in range(nt)]
        for cp in first:
            cp.start()
        passed = []
        for j, chip in enumerate(chips):
            for t in range(nt):
                copy(1 + j, t, (*chip, c), me).wait_recv()
                fwd = copy(4 + j, t, (*chip, c), sibling)
                fwd.start()
                passed.append(fwd)
        for t in range(nt):
            copy(0, t, sibling, me).wait_recv()
        for j, chip in enumerate(chips):
            for t in range(nt):
                copy(4 + j, t, (*chip, 1 - c), me).wait_recv()
        for cp in first + passed:
            cp.wait_send()
        for cp in mine:
            cp.wait()

    return _sequencer_call(
        body, shards, [SDS(_gathered(s.shape, k), s.dtype) for s, k in zip(shards, kinds)],
        [pltpu.SemaphoreType.DMA((7, nt)), pltpu.SemaphoreType.DMA((7, nt)), pltpu.SemaphoreType.DMA((nt,))],
        name=name, collective_id=collective_id)


def _part_shape(shape, kind):
    if kind == "row":
        return tuple(shape[:-2]) + (shape[-2] // N_DEV, shape[-1])
    return tuple(shape[2:])


def _pair_exchange(grads, kinds, *, name, collective_id):
    nt = len(grads)
    part = [_part_shape(g.shape, k) for g, k in zip(grads, kinds)]

    def body(*refs):
        g_refs, o_refs = refs[:nt], refs[nt:2 * nt]
        send_sems, recv_sems = refs[2 * nt:]
        x, y, c, _ = _mesh_place()
        _handshake([(x, y, 1 - c)])
        copies = []
        for t in range(nt):
            for xy in range(4):
                src = g_refs[t].at[1 - c, xy] if kinds[t] == "stack" else _window(g_refs[t], kinds[t], 2 * xy + 1 - c, part[t])
                copies.append(pltpu.make_async_remote_copy(
                    src_ref=src, dst_ref=o_refs[t].at[xy], send_sem=send_sems.at[xy, t], recv_sem=recv_sems.at[xy, t],
                    device_id=(x, y, 1 - c), device_id_type=MESH))
        for cp in copies:
            cp.start()
        for cp in copies:
            cp.wait()

    return _sequencer_call(
        body, grads, [SDS((4,) + p, g.dtype) for p, g in zip(part, grads)],
        [pltpu.SemaphoreType.DMA((4, nt)), pltpu.SemaphoreType.DMA((4, nt))], name=name, collective_id=collective_id)


def _chip_exchange(parts, small, *, name, collective_id):
    nt = len(parts)
    if small is None:
        def body_plain(*refs):
            s_refs, o_refs = refs[:nt], refs[nt:2 * nt]
            send_sems, recv_sems = refs[2 * nt:]
            x, y, c, chips = _mesh_place()
            _handshake([(*chip, c) for chip in chips])
            copies = [pltpu.make_async_remote_copy(
                src_ref=s_refs[t].at[2 * chip[0] + chip[1]], dst_ref=o_refs[t].at[j],
                send_sem=send_sems.at[j, t], recv_sem=recv_sems.at[j, t], device_id=(*chip, c), device_id_type=MESH)
                for j, chip in enumerate(chips) for t in range(nt)]
            for cp in copies:
                cp.start()
            for cp in copies:
                cp.wait()

        return _sequencer_call(
            body_plain, parts, [SDS((3,) + s.shape[1:], s.dtype) for s in parts],
            [pltpu.SemaphoreType.DMA((3, nt)), pltpu.SemaphoreType.DMA((3, nt))], name=name, collective_id=collective_id)

    def body(*refs):
        s_refs, small_ref = refs[:nt], refs[nt]
        o_refs, small_all = refs[nt + 1:2 * nt + 1], refs[2 * nt + 1]
        send_sems, recv_sems, small_send, small_recv, local_sem = refs[2 * nt + 2:]
        x, y, c, chips = _mesh_place()
        _handshake([(px, py, pc) for px in (x, 1 - x) for py in (y, 1 - y) for pc in (c, 1 - c)][1:])

        def copy(j, t, chip):
            return pltpu.make_async_remote_copy(
                src_ref=s_refs[t].at[2 * chip[0] + chip[1]], dst_ref=o_refs[t].at[j],
                send_sem=send_sems.at[j, t], recv_sem=recv_sems.at[j, t], device_id=(*chip, c), device_id_type=MESH)

        flips = [(fx, fy, fc) for fx in (0, 1) for fy in (0, 1) for fc in (0, 1)][1:]

        def small_copy(k):
            fx, fy, fc = flips[k]
            to = (x ^ fx if fx else x, y ^ fy if fy else y, c ^ fc if fc else c)
            rows = small_all.at[4 * x + 2 * y + c]
            return pltpu.make_async_remote_copy(
                src_ref=small_ref, dst_ref=rows, send_sem=small_send.at[k], recv_sem=small_recv.at[k],
                device_id=to, device_id_type=MESH)

        own = pltpu.make_async_copy(small_ref, small_all.at[4 * x + 2 * y + c], local_sem)
        own.start()
        copies = [copy(j, t, chip) for j, chip in enumerate(chips) for t in range(nt)]
        smalls = [small_copy(k) for k in range(7)]
        for cp in smalls + copies:
            cp.start()
        for cp in smalls + copies:
            cp.wait()
        own.wait()

    return _sequencer_call(
        body, list(parts) + [small],
        [SDS((3,) + s.shape[1:], s.dtype) for s in parts] + [SDS((N_DEV,) + small.shape, small.dtype)],
        [pltpu.SemaphoreType.DMA((3, nt)), pltpu.SemaphoreType.DMA((3, nt)),
         pltpu.SemaphoreType.DMA((7,)), pltpu.SemaphoreType.DMA((7,)), pltpu.SemaphoreType.DMA],
        name=name, collective_id=collective_id)


def _pair_sum(grad, theirs, kind, c, *, name):
    if kind == "row":
        r, l = theirs.shape[-2:]
        n = theirs.size // (4 * r * l)
        grad, theirs = grad.reshape(n, N_DEV * r, l), theirs.reshape(4, n, r, l)
        mine_spec = pl.BlockSpec((None, r, l), lambda xy, m, c_ref: (m, 2 * xy + c_ref[0], 0))
    else:
        r, l = theirs.shape[-2:]
        n = theirs.size // (4 * r * l)
        theirs = theirs.reshape(4, n, r, l)
        grad = grad.reshape(2, 4, n, r, l)
        mine_spec = pl.BlockSpec((None, None, None, r, l), lambda xy, m, c_ref: (c_ref[0], xy, m, 0, 0))

    def body(c_ref, a_ref, b_ref, o_ref):
        o_ref[...] = (a_ref[...].astype(F32) + b_ref[...].astype(F32)).astype(BF16)

    part = pl.BlockSpec((None, None, r, l), lambda xy, m, c_ref: (xy, m, 0, 0))
    return pl.pallas_call(
        body, name=name,
        grid_spec=pltpu.PrefetchScalarGridSpec(num_scalar_prefetch=1, grid=(4, n), in_specs=[mine_spec, part], out_specs=part),
        out_shape=SDS((4, n, r, l), BF16), compiler_params=_params("parallel", "parallel"),
    )(c, grad, theirs)


def _adamw(w, g, m, v):
    m = ADAM_B1 * m + (1.0 - ADAM_B1) * g
    v = ADAM_B2 * v + (1.0 - ADAM_B2) * jnp.square(g)
    m_hat = m / (1.0 - ADAM_B1 ** ADAM_STEP)
    v_hat = v / (1.0 - ADAM_B2 ** ADAM_STEP)
    delta = -ADAM_LR * (m_hat / (jnp.sqrt(v_hat) + ADAM_EPS) + ADAM_WD * w)
    return delta, m, v


def _adam_big(owns, others, mat, xy, w, m, v, *, tr, name):
    nw, r, l = w.shape
    lp = owns[0].shape[-1]
    nq = len(owns)
    assert nq in (1, nw)

    def body(xy_ref, *refs):
        own_refs, oth_refs = refs[:nq], refs[nq:2 * nq]
        w_ref, m_ref, v_ref, g_out, d_out, m_out, v_out = refs[2 * nq:]
        g = None
        for q in range(nq):
            gq = own_refs[q][0, 0].astype(F32)
            for j in range(3):
                gq = gq + oth_refs[q][j, 0].astype(F32)
            g = gq if g is None else jnp.where(pl.program_id(0) == q, gq, g)
        g = g[:, :l]
        delta, m_new, v_new = _adamw(w_ref[0], g, m_ref[0], v_ref[0])
        g_out[0] = g
        d_out[0] = delta
        m_out[0] = m_new
        v_out[0] = v_new

    def at(p):
        return mat * nw + p if nq == 1 else mat

    blk = pl.BlockSpec((1, tr, l), lambda p, i, xy_ref: (p, i, 0))
    return pl.pallas_call(
        body, name=name,
        grid_spec=pltpu.PrefetchScalarGridSpec(
            num_scalar_prefetch=1, grid=(nw, r // tr),
            in_specs=[pl.BlockSpec((1, 1, tr, lp), lambda p, i, xy_ref: (xy_ref[0], at(p), i, 0))] * nq
            + [pl.BlockSpec((3, 1, tr, lp), lambda p, i, xy_ref: (0, at(p), i, 0))] * nq + [blk, blk, blk],
            out_specs=[blk, blk, blk, blk]),
        out_shape=[SDS(w.shape, F32)] * 4, compiler_params=_params("parallel", "parallel"),
    )(xy, *owns, *others, w, m, v)


def _adam_small(g_all, w, m, v, *, name):
    def body(g_ref, w_ref, m_ref, v_ref, g_out, d_out, m_out, v_out):
        g = g_ref[0]
        for k in range(1, N_DEV):
            g = g + g_ref[k]
        delta, m_new, v_new = _adamw(w_ref[...], g, m_ref[...], v_ref[...])
        g_out[...] = g
        d_out[...] = delta
        m_out[...] = m_new
        v_out[...] = v_new

    return pl.pallas_call(body, name=name, out_shape=[SDS(w.shape, F32)] * 4)(g_all, w, m, v)


def _pack_small(parts):
    flat = jnp.concatenate([a.reshape(-1) for a in parts])
    rows = -(-flat.shape[0] // 1024) * 8
    return jnp.pad(flat, (0, rows * 128 - flat.shape[0])).reshape(rows, 128)


def _unpack_small(packed, shapes):
    flat = packed.reshape(-1)
    out, at = [], 0
    for s in shapes:
        n = math.prod(s)
        out.append(flat[at:at + n].reshape(s))
        at += n
    return out


def kernel(x, mem, ffn1_norm, ffn1_w_gate, ffn1_w_up, ffn1_w_down, mix_norm, mem_norm, w_in, w_mem_kv, swa_q_norm, swa_k_norm, swa_sinks, rel_bias, gla_w_gate_up, gla_b_gate, gla_out_norm, mem_q_norm, mem_k_norm, w_out, ffn2_norm, ffn2_w_gate, ffn2_w_up, ffn2_w_down, loss_target, m_ffn1_norm, m_ffn1_w_gate, m_ffn1_w_up, m_ffn1_w_down, m_mix_norm, m_mem_norm, m_w_in, m_w_mem_kv, m_swa_q_norm, m_swa_k_norm, m_swa_sinks, m_rel_bias, m_gla_w_gate_up, m_gla_b_gate, m_gla_out_norm, m_mem_q_norm, m_mem_k_norm, m_w_out, m_ffn2_norm, m_ffn2_w_gate, m_ffn2_w_up, m_ffn2_w_down, v_ffn1_norm, v_ffn1_w_gate, v_ffn1_w_up, v_ffn1_w_down, v_mix_norm, v_mem_norm, v_w_in, v_w_mem_kv, v_swa_q_norm, v_swa_k_norm, v_swa_sinks, v_rel_bias, v_gla_w_gate_up, v_gla_b_gate, v_gla_out_norm, v_mem_q_norm, v_mem_k_norm, v_w_out, v_ffn2_norm, v_ffn2_w_gate, v_ffn2_w_up, v_ffn2_w_down):
    xi, yi, ci = lax.axis_index("x"), lax.axis_index("y"), lax.axis_index("c")
    c_arr = jnp.reshape(ci, (1,)).astype(jnp.int32)
    xy_arr = jnp.reshape(2 * xi + yi, (1,)).astype(jnp.int32)
    d = x.shape[-1]

    half_h = ffn1_w_gate.shape[-1] // 2

    def gather_ffn(wg_s, wu_s, wd_s, name, collective_id, after):
        w3_s = jnp.concatenate([wg_s.transpose(0, 2, 1), wu_s.transpose(0, 2, 1), wd_s], axis=0)
        w3_s = jnp.pad(w3_s.reshape(3, 2, half_h, d), ((0, 0), (0, 0), (0, FFN_HALF_PAD - half_h), (0, 0))).astype(BF16)
        if after is not None:
            w3_s, _ = lax.optimization_barrier((w3_s, after))
        return _all_gather([w3_s], ["row"], name=name, collective_id=collective_id)[0].reshape(3, -1, d)

    w3_1 = gather_ffn(ffn1_w_gate, ffn1_w_up, ffn1_w_down, "gather_ffn1", 0, None)
    mix_s = lax.optimization_barrier((w_in[0].astype(BF16), w_mem_kv[0].astype(BF16), w_out[0].astype(BF16), w3_1))[:3]
    win_all, wkv, wout = _all_gather(list(mix_s), ["stack", "row", "row"], name="gather_mix", collective_id=1)

    def gather_ffn2(x1):
        return gather_ffn(ffn2_w_gate, ffn2_w_up, ffn2_w_down, "gather_ffn2", 2, (wout, x1))

    win_p = _pack_win(win_all, tr=256, name="pack_w_in")

    small_w = [ffn1_norm, mix_norm, mem_norm, ffn2_norm, swa_q_norm, swa_k_norm, swa_sinks[0], rel_bias,
               gla_w_gate_up[0], gla_b_gate, gla_out_norm, mem_q_norm, mem_k_norm]
    collective_ids = {"ffn2": (3, 4), "mix": (5, 6), "ffn1a": (7, 8), "ffn1b": (9, 10)}
    reduced, small_box = {}, {}

    def on_grads(group, grads, carry, small=None):
        if group == "mix":
            dwin_p, dwkv, dwout = grads
            grads = [_unpack_win(dwin_p, tr=256, name="unpack_dw_in"), dwkv, dwout]
            kinds = ["stack", "row", "row"]
        else:
            kinds = ["row"]
        if reduced:
            earlier = list(reduced.values())[-1][1]
            *grads, _ = lax.optimization_barrier((*grads, earlier[0]))
        id_pair, id_chip = collective_ids[group]
        from_sibling = _pair_exchange(grads, kinds, name=f"pair_exchange_{group}", collective_id=id_pair)
        chip_sums = [_pair_sum(g, theirs, k, c_arr, name=f"pair_sum_{group}_{t}")
                     for t, (g, theirs, k) in enumerate(zip(grads, from_sibling, kinds))]
        if carry is not None:
            *chip_sums, carry = lax.optimization_barrier((*chip_sums, carry))
        if small is None:
            from_chips = _chip_exchange(chip_sums, None, name=f"chip_exchange_{group}", collective_id=id_chip)
        else:
            *from_chips, small_all = _chip_exchange(chip_sums, _pack_small(small), name=f"chip_exchange_{group}",
                                                    collective_id=id_chip)
            small_box[group] = small_all
        reduced[group] = (chip_sums, from_chips)
        return carry

    grad_x = _local_step(x[0], mem[0], loss_target[0], small_w, (w3_1, win_p, wkv, wout, gather_ffn2), on_grads)

    big_w = {"ffn1_w_gate": ("ffn1", 0, 0, True, ffn1_w_gate, m_ffn1_w_gate, v_ffn1_w_gate),
             "ffn1_w_up": ("ffn1", 0, 1, True, ffn1_w_up, m_ffn1_w_up, v_ffn1_w_up),
             "ffn1_w_down": ("ffn1", 0, 2, False, ffn1_w_down, m_ffn1_w_down, v_ffn1_w_down),
             "w_in": ("mix", 0, 0, False, w_in, m_w_in, v_w_in),
             "w_mem_kv": ("mix", 1, 0, False, w_mem_kv, m_w_mem_kv, v_w_mem_kv),
             "w_out": ("mix", 2, 0, False, w_out, m_w_out, v_w_out),
             "ffn2_w_gate": ("ffn2", 0, 0, True, ffn2_w_gate, m_ffn2_w_gate, v_ffn2_w_gate),
             "ffn2_w_up": ("ffn2", 0, 1, True, ffn2_w_up, m_ffn2_w_up, v_ffn2_w_up),
             "ffn2_w_down": ("ffn2", 0, 2, False, ffn2_w_down, m_ffn2_w_down, v_ffn2_w_down)}
    res = {}
    for nm, (group, t, mat, transposed, w, m, v) in big_w.items():
        shape = w.shape
        if transposed:
            w, m, v = (a.transpose(0, 2, 1) for a in (w, m, v))
        if group != "mix":
            w, m, v = (a.reshape(2, half_h, d) for a in (w, m, v))
        r = w.shape[1]
        tr = 256 if r % 256 == 0 else r
        halves = ["ffn1a", "ffn1b"] if group == "ffn1" else [group]
        out = _adam_big([reduced[k][0][t] for k in halves], [reduced[k][1][t] for k in halves], mat, xy_arr, w, m, v,
                        tr=tr, name=f"adam_{nm}")
        if transposed:
            out = [a.reshape(1, -1, d).transpose(0, 2, 1) for a in out]
        res[nm] = [a.reshape(shape) for a in out]
    small_names = ["ffn1_norm", "mix_norm", "mem_norm", "ffn2_norm", "swa_q_norm", "swa_k_norm", "swa_sinks", "rel_bias",
                   "gla_w_gate_up", "gla_b_gate", "gla_out_norm", "mem_q_norm", "mem_k_norm"]
    small_m = [m_ffn1_norm, m_mix_norm, m_mem_norm, m_ffn2_norm, m_swa_q_norm, m_swa_k_norm, m_swa_sinks, m_rel_bias,
               m_gla_w_gate_up, m_gla_b_gate, m_gla_out_norm, m_mem_q_norm, m_mem_k_norm]
    small_v = [v_ffn1_norm, v_mix_norm, v_mem_norm, v_ffn2_norm, v_swa_q_norm, v_swa_k_norm, v_swa_sinks, v_rel_bias,
               v_gla_w_gate_up, v_gla_b_gate, v_gla_out_norm, v_mem_q_norm, v_mem_k_norm]
    small_full = [ffn1_norm, mix_norm, mem_norm, ffn2_norm, swa_q_norm, swa_k_norm, swa_sinks, rel_bias,
                  gla_w_gate_up, gla_b_gate, gla_out_norm, mem_q_norm, mem_k_norm]
    zero = jnp.zeros((), F32)
    for group, sel in (("ffn1a", slice(1, None)), ("ffn1b", slice(0, 1))):
        late = group == "ffn1b"
        extra = [] if late else [zero]
        packed = _adam_small(small_box[group], _pack_small(small_full[sel] + extra), _pack_small(small_m[sel] + extra),
                             _pack_small(small_v[sel] + extra), name=f"adam_small_{group}")
        shapes = [a.shape for a in small_full[sel]] + [()] * len(extra)
        unpacked = [_unpack_small(pk, shapes) for pk in packed]
        for k, nm in enumerate(small_names[sel]):
            res[nm] = [unpacked[q][k] for q in range(4)]
        if not late:
            loss = unpacked[0][-1]

    order = ["ffn1_norm", "ffn1_w_gate", "ffn1_w_up", "ffn1_w_down", "mix_norm", "mem_norm", "w_in", "w_mem_kv",
             "swa_q_norm", "swa_k_norm", "swa_sinks", "rel_bias", "gla_w_gate_up", "gla_b_gate", "gla_out_norm",
             "mem_q_norm", "mem_k_norm", "w_out", "ffn2_norm", "ffn2_w_gate", "ffn2_w_up", "ffn2_w_down"]
    outs = [loss, grad_x[None]]
    for q in range(4):
        outs += [res[nm][q] for nm in order]
    return tuple(outs)
```

```python
import functools
import math

import numpy as np
import jax
import jax.numpy as jnp
from jax import lax
from jax.experimental import pallas as pl
from jax.experimental.pallas import tpu as pltpu
from jax.experimental.pallas import tpu_sc as plsc

F32 = jnp.float32
BF16 = jnp.bfloat16
SDS = jax.ShapeDtypeStruct

EPS = 1e-6
HEAD_DIM = 64
SWA_HEADS = 8
SWA_KV_HEADS = 2
SWA_GROUP = SWA_HEADS // SWA_KV_HEADS
BLOCK = 128
N_BUCKETS = 32
MAX_DISTANCE = 128
GLA_HEADS = 4
GLA_DK = 32
GLA_DV = 64
GLA_RANK = 16
GLA_TAU = 16.0
GLA_CHUNK = 32
MEM_HEADS = 4
SWA_Q_W = SWA_HEADS * HEAD_DIM
SWA_KV_W = SWA_KV_HEADS * HEAD_DIM
GLA_QK_W = GLA_HEADS * GLA_DK
GLA_V_W = GLA_HEADS * GLA_DV
MEM_Q_W = MEM_HEADS * HEAD_DIM
IN_W = 1808
IN_W_PAD = 1920
COL_SQ, COL_SKV, COL_GQ, COL_GK, COL_GV, COL_GG, COL_MQ, COL_GLR = 0, 512, 768, 896, 1024, 1280, 1536, 1792

ADAM_LR = 0.001
ADAM_B1 = 0.9
ADAM_B2 = 0.999
ADAM_EPS = 1e-08
ADAM_WD = 0.01
ADAM_STEP = 10

N_DEV = 8
VMEM_LIMIT_BYTES = 56 * 1024 * 1024
MESH = pl.DeviceIdType.MESH


def _params(*sem):
    return pltpu.CompilerParams(dimension_semantics=sem or None, vmem_limit_bytes=VMEM_LIMIT_BYTES)


def _dot(a, b, ta, tb, precision=None):
    dims = (((0 if ta else 1,), (1 if tb else 0,)), ((), ()))
    return lax.dot_general(a, b, dims, preferred_element_type=F32, precision=precision)


def _mm_raw(a, b, ta=False, tb=False):
    return _dot(a.astype(BF16), b.astype(BF16), ta, tb)


def _mmf_raw(a, b, ta=False, tb=False):
    return _dot(a, b, ta, tb, lax.Precision.HIGHEST)


def _make_mm(raw):
    @functools.partial(jax.custom_vjp, nondiff_argnums=(2, 3))
    def mm(a, b, ta=False, tb=False):
        return raw(a, b, ta, tb)

    def fwd(a, b, ta, tb):
        return raw(a, b, ta, tb), (a, b)

    def bwd(ta, tb, res, g):
        a, b = res
        da = raw(b, g, tb, True) if ta else raw(g, b, False, not tb)
        db = raw(g, a, True, ta) if tb else raw(a, g, not ta, False)
        return da, db

    mm.defvjp(fwd, bwd)
    return mm


_mm = _make_mm(_mm_raw)
_mmf = _make_mm(_mmf_raw)


def _mm3(a, b, ta=False, tb=False):
    a_hi, b_hi = a.astype(BF16).astype(F32), b.astype(BF16).astype(F32)
    return _mm(a_hi, b_hi, ta, tb) + _mm(a_hi, b - b_hi, ta, tb) + _mm(a - a_hi, b_hi, ta, tb)


def _rms(x, g):
    return x * lax.rsqrt(jnp.mean(x * x, axis=-1, keepdims=True) + EPS) * g


def _silu_mul(g, u):
    return jax.nn.silu(g) * u


def _log_sigmoid(z):
    return jnp.minimum(z, 0.0) - jnp.log(1.0 + jnp.exp(-jnp.abs(z)))


def _matmul(a_list, b, *, ta=False, tb=False, tm, tn, b_blocks=None, res=None, scale=1.0, out_dtype=F32, name):
    if not isinstance(a_list, (list, tuple)):
        a_list = [a_list]
    n_a = len(a_list)
    m = a_list[0].shape[1] if ta else a_list[0].shape[0]
    ks = [a.shape[0] if ta else a.shape[1] for a in a_list]
    n = b.shape[0] if tb else b.shape[1]
    if b_blocks is None:
        assert n_a == 1
        b_blocks = [0]
    tm, tn = min(tm, m), min(tn, n)
    assert m % tm == 0 and n % tn == 0, (m, n, tm, tn)

    def body(*refs):
        a_refs, b_refs = refs[:n_a], refs[n_a:2 * n_a]
        r_ref = refs[2 * n_a] if res is not None else None
        o_ref = refs[-1]
        acc = _mm_raw(a_refs[0][...], b_refs[0][...], ta, tb)
        for k in range(1, n_a):
            acc = acc + _mm_raw(a_refs[k][...], b_refs[k][...], ta, tb)
        if scale != 1.0:
            acc = acc * scale
        if r_ref is not None:
            acc = r_ref[...] + acc
        o_ref[...] = acc.astype(out_dtype)

    in_specs = []
    for k in ks:
        in_specs.append(pl.BlockSpec((k, tm), lambda i, j: (0, i)) if ta else pl.BlockSpec((tm, k), lambda i, j: (i, 0)))
    for k, blk in zip(ks, b_blocks):
        if tb:
            in_specs.append(pl.BlockSpec((tn, k), functools.partial(lambda i, j, blk: (j, blk), blk=blk)))
        else:
            in_specs.append(pl.BlockSpec((k, tn), functools.partial(lambda i, j, blk: (blk, j), blk=blk)))
    args = list(a_list) + [b] * n_a
    if res is not None:
        in_specs.append(pl.BlockSpec((tm, tn), lambda i, j: (i, j)))
        args.append(res)
    return pl.pallas_call(
        body, name=name, grid=(m // tm, n // tn), in_specs=in_specs,
        out_specs=pl.BlockSpec((tm, tn), lambda i, j: (i, j)), out_shape=SDS((m, n), out_dtype),
        compiler_params=_params("parallel", "parallel"),
    )(*args)


def _win_pieces(w):
    glr_lo, glr_hi = COL_MQ, COL_MQ + GLA_RANK
    out = []
    for j in range(N_DEV):
        for lo, hi, shift in ((0, glr_lo, 0), (glr_lo, glr_hi, COL_GLR - glr_lo), (glr_hi, IN_W, COL_MQ - glr_hi)):
            s, e = max(j * w, lo), min((j + 1) * w, hi)
            if s < e:
                out.append((j, s - j * w, e - j * w, s + shift))
    return out


def _pack_win(win_all, *, tr, name):
    _, d, w = win_all.shape

    def body(i_ref, o_ref):
        for j, a, b, dst in _win_pieces(w):
            o_ref[:, dst:dst + b - a] = i_ref[j][:, a:b]
        o_ref[:, IN_W:] = jnp.zeros((tr, IN_W_PAD - IN_W), o_ref.dtype)

    return pl.pallas_call(
        body, name=name, grid=(d // tr,), in_specs=[pl.BlockSpec((N_DEV, tr, w), lambda i: (0, i, 0))],
        out_specs=pl.BlockSpec((tr, IN_W_PAD), lambda i: (i, 0)), out_shape=SDS((d, IN_W_PAD), win_all.dtype),
        compiler_params=_params("parallel"),
    )(win_all)


def _unpack_win(dwin_p, *, tr, name):
    d = dwin_p.shape[0]
    w = IN_W // N_DEV

    def body(i_ref, o_ref):
        for j, a, b, src in _win_pieces(w):
            o_ref[j % 2, j // 2, :, a:b] = i_ref[:, src:src + b - a]

    return pl.pallas_call(
        body, name=name, grid=(d // tr,), in_specs=[pl.BlockSpec((tr, IN_W_PAD), lambda i: (i, 0))],
        out_specs=pl.BlockSpec((2, 4, tr, w), lambda i: (0, 0, i, 0)), out_shape=SDS((2, 4, d, w), dwin_p.dtype),
        compiler_params=_params("parallel"),
    )(dwin_p)


def _rms_fwd(x, g, *, tm, name):
    s, d = x.shape

    def body(x_ref, g_ref, h_ref):
        h_ref[...] = _rms(x_ref[...], g_ref[...]).astype(BF16)

    return pl.pallas_call(
        body, name=name, grid=(s // tm,),
        in_specs=[pl.BlockSpec((tm, d), lambda i: (i, 0)), pl.BlockSpec((1, d), lambda i: (0, 0))],
        out_specs=pl.BlockSpec((tm, d), lambda i: (i, 0)), out_shape=SDS((s, d), BF16),
        compiler_params=_params("parallel"),
    )(x, g)


def _rms_bwd(x, g, dh, dres, *, tm, name):
    s, d = x.shape
    want_dx = dres is not None

    def body(*refs):
        if want_dx:
            x_ref, g_ref, dh_ref, dres_ref, dx_ref, dxb_ref, dg_ref = refs
        else:
            x_ref, g_ref, dh_ref, dg_ref = refs
        _, vjp = jax.vjp(_rms, x_ref[...], g_ref[...])
        dx, dg = vjp(dh_ref[...])
        if want_dx:
            dx = dres_ref[...] + dx
            dx_ref[...] = dx
            dxb_ref[...] = dx.astype(BF16)

        @pl.when(pl.program_id(0) == 0)
        def _():
            dg_ref[...] = jnp.zeros_like(dg_ref)

        dg_ref[...] += dg

    row = pl.BlockSpec((tm, d), lambda i: (i, 0))
    vec = pl.BlockSpec((1, d), lambda i: (0, 0))
    if want_dx:
        return pl.pallas_call(
            body, name=name, grid=(s // tm,), in_specs=[row, vec, row, row], out_specs=[row, row, vec],
            out_shape=[SDS((s, d), F32), SDS((s, d), BF16), SDS((1, d), F32)], compiler_params=_params("arbitrary"),
        )(x, g, dh, dres)
    return None, None, pl.pallas_call(
        body, name=name, grid=(s // tm,), in_specs=[row, vec, row], out_specs=vec,
        out_shape=SDS((1, d), F32), compiler_params=_params("arbitrary"),
    )(x, g, dh)


FFN_TN = 256
FFN_HALF_PAD = 192


def _ffn_fwd(x, gain, w3, tag, *, tm=1024):
    s, d = x.shape
    f = w3.shape[1]
    tn = FFN_TN
    nj = f // tn
    tm = min(tm, s)

    def body(x_ref, gain_ref, wg_ref, wu_ref, wd_ref, y_ref, h_ref, g_ref, u_ref, acc_s):
        j = pl.program_id(1)

        @pl.when(j == 0)
        def _():
            h_ref[...] = _rms(x_ref[...], gain_ref[...]).astype(BF16)
            acc_s[...] = jnp.zeros_like(acc_s)

        hv = h_ref[...]
        g = _mm_raw(hv, wg_ref[...], False, True)
        u = _mm_raw(hv, wu_ref[...], False, True)
        g_ref[...] = g.astype(BF16)
        u_ref[...] = u.astype(BF16)
        acc_s[...] += _mm_raw(_silu_mul(g, u), wd_ref[...])

        @pl.when(j == nj - 1)
        def _():
            y_ref[...] = x_ref[...] + 0.5 * acc_s[...]

    row = pl.BlockSpec((tm, d), lambda i, j: (i, 0))
    tile = pl.BlockSpec((tm, tn), lambda i, j: (i, j))
    y, h, g, u = pl.pallas_call(
        body, name=f"{tag}_fwd", grid=(s // tm, nj),
        in_specs=[row, pl.BlockSpec((1, d), lambda i, j: (0, 0))]
        + [pl.BlockSpec((None, tn, d), functools.partial(lambda i, j, k: (k, j, 0), k=k)) for k in range(3)],
        out_specs=[row, row, tile, tile],
        out_shape=[SDS((s, d), F32), SDS((s, d), BF16), SDS((s, f), BF16), SDS((s, f), BF16)],
        scratch_shapes=[pltpu.VMEM((tm, d), F32)],
        compiler_params=_params("parallel", "arbitrary"),
    )(x, gain, w3, w3, w3)
    return y, (h, g, u)


def _ffn_bwd_part(dyb, w3, saved, first, count, dh_init, *, name):
    h, g, u = saved
    s, d = h.shape
    tn = FFN_TN

    def body(*refs):
        if dh_init is None:
            dy_ref, h_ref, wg_ref, wu_ref, wd_ref, g_ref, u_ref, dh_ref, dw3_ref = refs
        else:
            dy_ref, h_ref, wg_ref, wu_ref, wd_ref, g_ref, u_ref, dh0_ref, dh_ref, dw3_ref = refs

        @pl.when(pl.program_id(0) == 0)
        def _():
            dh_ref[...] = jnp.zeros_like(dh_ref) if dh_init is None else dh0_ref[...]

        dyv = dy_ref[...]
        da = _mm_raw(dyv, wd_ref[...], False, True) * 0.5
        a, vjp = jax.vjp(_silu_mul, g_ref[...].astype(F32), u_ref[...].astype(F32))
        dg, du = vjp(da)
        dg = dg.astype(BF16)
        du = du.astype(BF16)
        dh_ref[...] += _mm_raw(dg, wg_ref[...]) + _mm_raw(du, wu_ref[...])
        hv = h_ref[...]
        dw3_ref[0] = _mm_raw(dg, hv, True, False).astype(BF16)
        dw3_ref[1] = _mm_raw(du, hv, True, False).astype(BF16)
        dw3_ref[2] = (_mm_raw(a, dyv, True, False) * 0.5).astype(BF16)

    full = pl.BlockSpec((s, d), lambda j: (0, 0))
    once = pl.BlockSpec((s, d), lambda j: (0, 0), pipeline_mode=pl.Buffered(1))
    tile = pl.BlockSpec((s, tn), lambda j: (0, first + j))
    in_specs = ([once, once]
                + [pl.BlockSpec((None, tn, d), functools.partial(lambda j, k: (k, first + j, 0), k=k)) for k in range(3)]
                + [tile, tile])
    args = [dyb, h, w3, w3, w3, g, u]
    if dh_init is not None:
        in_specs.append(once)
        args.append(dh_init)
    return pl.pallas_call(
        body, name=name, grid=(count,), in_specs=in_specs,
        out_specs=[full, pl.BlockSpec((3, tn, d), lambda j: (0, j, 0))],
        out_shape=[SDS((s, d), F32), SDS((3, count * tn, d), BF16)],
        compiler_params=_params("arbitrary"),
    )(*args)


def _loss_bwd(y, target, *, tm, name):
    s, d = y.shape

    def body(y_ref, t_ref, dy_ref, dyb_ref, l_ref):
        diff = y_ref[...] - t_ref[...]
        dy_ref[...] = diff * (1.0 / d)
        dyb_ref[...] = (diff * (1.0 / d)).astype(BF16)

        @pl.when(pl.program_id(0) == 0)
        def _():
            l_ref[...] = jnp.zeros_like(l_ref)

        l_ref[...] += 0.5 * jnp.sum(jnp.mean(diff * diff, axis=-1, keepdims=True), axis=0, keepdims=True)

    row = pl.BlockSpec((tm, d), lambda i: (i, 0))
    return pl.pallas_call(
        body, name=name, grid=(s // tm,), in_specs=[row, row],
        out_specs=[row, row, pl.BlockSpec((1, 1), lambda i: (0, 0))],
        out_shape=[SDS((s, d), F32), SDS((s, d), BF16), SDS((1, 1), F32)],
        compiler_params=_params("arbitrary"),
    )(y, target)


def _bucket_table():
    qi = np.arange(BLOCK)[:, None]
    kj = np.arange(2 * BLOCK)[None, :]
    dist = np.maximum(qi + BLOCK - kj, 0)
    max_exact = N_BUCKETS // 2
    d = np.maximum(dist, 1).astype(np.float32)
    large = max_exact + (np.log(d / np.float32(max_exact)) / np.float32(math.log(MAX_DISTANCE / max_exact))
                         * np.float32(N_BUCKETS - max_exact)).astype(np.int32)
    large = np.minimum(large, N_BUCKETS - 1)
    return np.where(dist < max_exact, dist, large).astype(np.int32)


SWA_STACK = SWA_GROUP * BLOCK


def _swa_valid(n):
    qi = lax.broadcasted_iota(jnp.int32, (SWA_STACK, 2 * BLOCK), 0) % BLOCK
    kj = lax.broadcasted_iota(jnp.int32, (SWA_STACK, 2 * BLOCK), 1)
    dist = qi + BLOCK - kj
    return (dist >= 0) & (dist < BLOCK) & ((kj >= BLOCK) | (n > 0))


def _swa_group(q, kb, vb, qg, kg, sink, bias, valid):
    qn = _rms(q, qg)
    kn = _rms(kb, kg)
    s = _mm(qn, kn, False, True) * (HEAD_DIM ** -0.5) + bias
    s = jnp.where(valid, s, -jnp.inf)
    m = lax.stop_gradient(jnp.maximum(jnp.max(s, axis=-1, keepdims=True), sink))
    p = jnp.exp(s - m)
    p = p / (jnp.sum(p, axis=-1, keepdims=True) + jnp.exp(sink - m))
    return _mm(p, vb)


def _swa_bias_table(rb_ref, bucket, bias_s):
    for h in range(SWA_HEADS):
        acc = jnp.zeros((BLOCK, 2 * BLOCK), F32)
        for b in range(N_BUCKETS):
            acc = jnp.where(bucket == b, rb_ref[b, h], acc)
        bias_s[h // SWA_GROUP, (h % SWA_GROUP) * BLOCK:(h % SWA_GROUP + 1) * BLOCK, :] = acc


def _swa_stack(ref, g):
    return jnp.concatenate([ref[:, (g * SWA_GROUP + hh) * HEAD_DIM:(g * SWA_GROUP + hh + 1) * HEAD_DIM]
                            for hh in range(SWA_GROUP)], axis=0)


def _swa_unstack(ref, g, stacked):
    for hh in range(SWA_GROUP):
        h = g * SWA_GROUP + hh
        ref[:, h * HEAD_DIM:(h + 1) * HEAD_DIM] = stacked[hh * BLOCK:(hh + 1) * BLOCK]


def _swa_sink_column(sink_ref, g):
    head = lax.broadcasted_iota(jnp.int32, (SWA_STACK, 1), 0) // BLOCK
    col = jnp.zeros((SWA_STACK, 1), F32)
    for hh in range(SWA_GROUP):
        col = jnp.where(head == hh, sink_ref[g * SWA_GROUP + hh], col)
    return col


def _swa_band(kvp_ref, kvc_ref, g):
    lo = g * HEAD_DIM
    kb = jnp.concatenate([kvp_ref[:, lo:lo + HEAD_DIM], kvc_ref[:, lo:lo + HEAD_DIM]], axis=0)
    lo += SWA_KV_W
    vb = jnp.concatenate([kvp_ref[:, lo:lo + HEAD_DIM], kvc_ref[:, lo:lo + HEAD_DIM]], axis=0)
    return kb, vb


def _swa_specs(order):
    kvc = COL_SKV // (2 * SWA_KV_W)
    return [
        pl.BlockSpec((BLOCK, SWA_Q_W), lambda t: (order(t), 0)),
        pl.BlockSpec((BLOCK, 2 * SWA_KV_W), lambda t: (jnp.maximum(order(t) - 1, 0), kvc)),
        pl.BlockSpec((BLOCK, 2 * SWA_KV_W), lambda t: (order(t), kvc)),
        pl.BlockSpec((1, HEAD_DIM), lambda t: (0, 0)),
        pl.BlockSpec((1, HEAD_DIM), lambda t: (0, 0)),
        pl.BlockSpec(memory_space=pltpu.SMEM),
        pl.BlockSpec(memory_space=pltpu.SMEM),
        pl.BlockSpec((BLOCK, 2 * BLOCK), lambda t: (0, 0)),
    ]


def _swa_fwd(p, qg, kg, sinks, rel_bias, *, name):
    s = p.shape[0]
    nb = s // BLOCK

    def body(q_ref, kvp_ref, kvc_ref, qg_ref, kg_ref, sink_ref, rb_ref, bucket_ref, y_ref, bias_s):
        n = pl.program_id(0)

        @pl.when(n == 0)
        def _():
            _swa_bias_table(rb_ref, bucket_ref[...], bias_s)

        valid = _swa_valid(n)
        for g in range(SWA_KV_HEADS):
            kb, vb = _swa_band(kvp_ref, kvc_ref, g)
            out = _swa_group(_swa_stack(q_ref, g), kb, vb, qg_ref[...], kg_ref[...], _swa_sink_column(sink_ref, g),
                             bias_s[g], valid)
            _swa_unstack(y_ref, g, out)

    return pl.pallas_call(
        body, name=name, grid=(nb,), in_specs=_swa_specs(lambda t: t),
        out_specs=pl.BlockSpec((BLOCK, SWA_Q_W), lambda t: (t, 0)), out_shape=SDS((s, SWA_Q_W), F32),
        scratch_shapes=[pltpu.VMEM((SWA_KV_HEADS, SWA_STACK, 2 * BLOCK), F32)],
        compiler_params=_params("arbitrary"),
    )(p, p, p, qg, kg, sinks, rel_bias, jnp.asarray(_bucket_table()))


def _swa_bwd(p, qg, kg, sinks, rel_bias, dy_all, *, name):
    s = p.shape[0]
    nb = s // BLOCK

    def body(q_ref, kvp_ref, kvc_ref, qg_ref, kg_ref, sink_ref, rb_ref, bucket_ref, dy_ref,
             dq_ref, dkv_ref, dqg_ref, dkg_ref, dsink_ref, drb_ref, bias_s, dbias_s, carry_s):
        t = pl.program_id(0)
        n = nb - 1 - t

        @pl.when(t == 0)
        def _():
            _swa_bias_table(rb_ref, bucket_ref[...], bias_s)
            dbias_s[...] = jnp.zeros_like(dbias_s)
            carry_s[...] = jnp.zeros_like(carry_s)
            dqg_ref[...] = jnp.zeros_like(dqg_ref)
            dkg_ref[...] = jnp.zeros_like(dkg_ref)
            dsink_ref[...] = jnp.zeros_like(dsink_ref)
            drb_ref[...] = jnp.zeros_like(drb_ref)

        valid = _swa_valid(n)
        lane = lax.broadcasted_iota(jnp.int32, (1, BLOCK), 1)
        dqg = jnp.zeros((1, HEAD_DIM), F32)
        dkg = jnp.zeros((1, HEAD_DIM), F32)
        dsink_vec = jnp.zeros((1, BLOCK), F32)
        for g in range(SWA_KV_HEADS):
            kb, vb = _swa_band(kvp_ref, kvc_ref, g)
            _, vjp = jax.vjp(functools.partial(_swa_group, valid=valid), _swa_stack(q_ref, g), kb, vb, qg_ref[...],
                             kg_ref[...], _swa_sink_column(sink_ref, g), bias_s[g])
            dq, dkb, dvb, dqg_g, dkg_g, dsink_col, dbias = vjp(_swa_stack(dy_ref, g))
            _swa_unstack(dq_ref, g, dq)
            dqg += dqg_g
            dkg += dkg_g
            dbias_s[g] += dbias
            for hh in range(SWA_GROUP):
                dsink_h = jnp.sum(dsink_col[hh * BLOCK:(hh + 1) * BLOCK], axis=0, keepdims=True)
                dsink_vec += jnp.where(lane == g * SWA_GROUP + hh, dsink_h, 0.0)
            lo = g * HEAD_DIM
            dkv_ref[:, lo:lo + HEAD_DIM] = dkb[BLOCK:] + carry_s[g]
            carry_s[g] = dkb[:BLOCK]
            lo += SWA_KV_W
            dkv_ref[:, lo:lo + HEAD_DIM] = dvb[BLOCK:] + carry_s[SWA_KV_HEADS + g]
            carry_s[SWA_KV_HEADS + g] = dvb[:BLOCK]
        dqg_ref[...] += dqg
        dkg_ref[...] += dkg
        dsink_ref[...] += dsink_vec

        @pl.when(t == nb - 1)
        def _():
            bucket = bucket_ref[...]
            row = lax.broadcasted_iota(jnp.int32, (N_BUCKETS, BLOCK), 0)
            col = lax.broadcasted_iota(jnp.int32, (N_BUCKETS, BLOCK), 1)
            acc = jnp.zeros((N_BUCKETS, BLOCK), F32)
            for h in range(SWA_HEADS):
                dbias = dbias_s[h // SWA_GROUP, (h % SWA_GROUP) * BLOCK:(h % SWA_GROUP + 1) * BLOCK, :]
                for b in range(N_BUCKETS):
                    part = jnp.sum(jnp.where(bucket == b, dbias, 0.0), axis=1, keepdims=True)
                    val = jnp.sum(part, axis=0, keepdims=True)
                    acc = acc + jnp.where((row == b) & (col == h), val, 0.0)
            drb_ref[...] = acc

    order = lambda t: nb - 1 - t
    vec = pl.BlockSpec((1, HEAD_DIM), lambda t: (0, 0))
    return pl.pallas_call(
        body, name=name, grid=(nb,),
        in_specs=_swa_specs(order) + [pl.BlockSpec((BLOCK, SWA_Q_W), lambda t: (order(t), 0))],
        out_specs=[pl.BlockSpec((BLOCK, SWA_Q_W), lambda t: (order(t), 0)),
                   pl.BlockSpec((BLOCK, 2 * SWA_KV_W), lambda t: (order(t), 0)),
                   vec, vec, pl.BlockSpec((1, BLOCK), lambda t: (0, 0)),
                   pl.BlockSpec((N_BUCKETS, BLOCK), lambda t: (0, 0))],
        out_shape=[SDS((s, SWA_Q_W), F32), SDS((s, 2 * SWA_KV_W), F32), SDS((1, HEAD_DIM), F32),
                   SDS((1, HEAD_DIM), F32), SDS((1, BLOCK), F32), SDS((N_BUCKETS, BLOCK), F32)],
        scratch_shapes=[pltpu.VMEM((SWA_KV_HEADS, SWA_STACK, 2 * BLOCK), F32),
                        pltpu.VMEM((SWA_KV_HEADS, SWA_STACK, 2 * BLOCK), F32),
                        pltpu.VMEM((2 * SWA_KV_HEADS, BLOCK, HEAD_DIM), F32)],
        compiler_params=_params("arbitrary"),
    )(p, p, p, qg, kg, sinks, rel_bias, jnp.asarray(_bucket_table()), dy_all)


def _mem_head(q, k, v, qg, kg):
    qn = _rms(q, qg)
    kn = _rms(k, kg)
    s = _mm(qn, kn, False, True) * (HEAD_DIM ** -0.5)
    m = lax.stop_gradient(jnp.max(s, axis=-1, keepdims=True))
    e = jnp.exp(s - m)
    return _mm(e / jnp.sum(e, axis=-1, keepdims=True), v)


def _mem_fwd(p, kv, qg, kg, *, tq, name):
    s = p.shape[0]
    m = kv.shape[0]

    def body(q_ref, kv_ref, qg_ref, kg_ref, y_ref):
        for h in range(MEM_HEADS):
            cols = slice(h * HEAD_DIM, (h + 1) * HEAD_DIM)
            vcols = slice(MEM_Q_W + h * HEAD_DIM, MEM_Q_W + (h + 1) * HEAD_DIM)
            y_ref[:, cols] = _mem_head(q_ref[:, cols], kv_ref[:, cols], kv_ref[:, vcols], qg_ref[...], kg_ref[...])

    vec = pl.BlockSpec((1, HEAD_DIM), lambda t: (0, 0))
    return pl.pallas_call(
        body, name=name, grid=(s // tq,),
        in_specs=[pl.BlockSpec((tq, MEM_Q_W), lambda t: (t, COL_MQ // MEM_Q_W)),
                  pl.BlockSpec((m, 2 * MEM_Q_W), lambda t: (0, 0)), vec, vec],
        out_specs=pl.BlockSpec((tq, MEM_Q_W), lambda t: (t, 0)), out_shape=SDS((s, MEM_Q_W), F32),
        compiler_params=_params("parallel"),
    )(p, kv, qg, kg)


def _mem_bwd(p, kv, qg, kg, dy_all, *, tq, name):
    s = p.shape[0]
    m = kv.shape[0]

    def body(q_ref, kv_ref, qg_ref, kg_ref, dy_ref, dq_ref, dkv_ref, dqg_ref, dkg_ref):
        @pl.when(pl.program_id(0) == 0)
        def _():
            dkv_ref[...] = jnp.zeros_like(dkv_ref)
            dqg_ref[...] = jnp.zeros_like(dqg_ref)
            dkg_ref[...] = jnp.zeros_like(dkg_ref)

        dqg = jnp.zeros((1, HEAD_DIM), F32)
        dkg = jnp.zeros((1, HEAD_DIM), F32)
        for h in range(MEM_HEADS):
            cols = slice(h * HEAD_DIM, (h + 1) * HEAD_DIM)
            vcols = slice(MEM_Q_W + h * HEAD_DIM, MEM_Q_W + (h + 1) * HEAD_DIM)
            _, vjp = jax.vjp(_mem_head, q_ref[:, cols], kv_ref[:, cols], kv_ref[:, vcols], qg_ref[...], kg_ref[...])
            dq, dk, dv, dqg_h, dkg_h = vjp(dy_ref[:, cols])
            dq_ref[:, cols] = dq
            dkv_ref[:, cols] += dk
            dkv_ref[:, vcols] += dv
            dqg += dqg_h
            dkg += dkg_h
        dqg_ref[...] += dqg
        dkg_ref[...] += dkg

    vec = pl.BlockSpec((1, HEAD_DIM), lambda t: (0, 0))
    full = pl.BlockSpec((m, 2 * MEM_Q_W), lambda t: (0, 0))
    dy_col = (SWA_Q_W + GLA_V_W) // MEM_Q_W
    return pl.pallas_call(
        body, name=name, grid=(s // tq,),
        in_specs=[pl.BlockSpec((tq, MEM_Q_W), lambda t: (t, COL_MQ // MEM_Q_W)), full, vec, vec,
                  pl.BlockSpec((tq, MEM_Q_W), lambda t: (t, dy_col))],
        out_specs=[pl.BlockSpec((tq, MEM_Q_W), lambda t: (t, 0)), full, vec, vec],
        out_shape=[SDS((s, MEM_Q_W), F32), SDS((m, 2 * MEM_Q_W), F32), SDS((1, HEAD_DIM), F32), SDS((1, HEAD_DIM), F32)],
        compiler_params=_params("arbitrary"),
    )(p, kv, qg, kg, dy_all)


GLA_ROWS = 256


GLA_GROUP = 4


def _gla_consts():
    c, h, r = GLA_CHUNK, GLA_HEADS, GLA_GROUP * GLA_CHUNK
    i2 = lax.broadcasted_iota(jnp.int32, (c, c), 0)
    j2 = lax.broadcasted_iota(jnp.int32, (c, c), 1)
    slab_q = lax.broadcasted_iota(jnp.int32, (h, r, GLA_QK_W), 0)
    lane_q = lax.broadcasted_iota(jnp.int32, (h, r, GLA_QK_W), 2)
    row_a = lax.broadcasted_iota(jnp.int32, (h * r, r), 0) % r
    col_a = lax.broadcasted_iota(jnp.int32, (h * r, r), 1)
    slab_o = lax.broadcasted_iota(jnp.int32, (h, r, GLA_V_W), 0)
    lane_o = lax.broadcasted_iota(jnp.int32, (h, r, GLA_V_W), 2)
    row_s = lax.broadcasted_iota(jnp.int32, (GLA_V_W, GLA_QK_W), 0)
    col_s = lax.broadcasted_iota(jnp.int32, (GLA_V_W, GLA_QK_W), 1)
    return dict(
        ltri=(j2 <= i2).astype(F32),
        m_q=(slab_q == lane_q // GLA_DK).astype(F32),
        causal=(col_a <= row_a) & (col_a // c == row_a // c),
        m_o=(slab_o == lane_o // GLA_DV).astype(F32),
        m_s=(row_s // GLA_DV == col_s // GLA_DK).astype(F32),
    )


def _gla_step(q, k, v, z, bg, st, c):
    h = GLA_HEADS
    kt, qt, qe, decay = [], [], [], []
    for qc, kc, zc in zip(q, k, z):
        la = _log_sigmoid(zc + bg) * (1.0 / GLA_TAU)
        b = _mmf(c["ltri"], la)
        bl = jnp.sum(la, axis=0, keepdims=True)
        qs = qc * (GLA_DK ** -0.5)
        kt.append(kc * jnp.exp(bl - b))
        qt.append(qs * jnp.exp(b - bl))
        qe.append(qs * jnp.exp(b))
        decay.append(jnp.exp(bl))
    o_intra = []
    rows = GLA_GROUP * GLA_CHUNK
    for lo in range(0, len(q), GLA_GROUP):
        qt_all, kt_all, v_all = (jnp.concatenate(parts[lo:lo + GLA_GROUP], axis=0) for parts in (qt, kt, v))
        q_stack = (jnp.broadcast_to(qt_all[None], (h, rows, GLA_QK_W)) * c["m_q"]).reshape(h * rows, GLA_QK_W)
        a = jnp.where(c["causal"], _mm3(q_stack, kt_all, False, True), 0.0)
        o_stack = _mm(a, v_all)
        o_intra.append(jnp.sum(o_stack.reshape(h, rows, GLA_V_W) * c["m_o"], axis=0))
    o_intra = jnp.concatenate(o_intra, axis=0)
    o_inter = []
    for qec, ktc, vc, dc in zip(qe, kt, v, decay):
        o_inter.append(_mm(qec, st, False, True))
        st = st * dc + _mm(vc, ktc, True, False) * c["m_s"]
    return o_intra + jnp.concatenate(o_inter, axis=0), st


def _gla_post(o, gg, gain, g64):
    ms = _mmf(o * o, g64) * (1.0 / GLA_DV)
    return o * lax.rsqrt(ms + EPS) * gain * jax.nn.silu(gg)


def _gla_g64():
    r = lax.broadcasted_iota(jnp.int32, (GLA_V_W, GLA_V_W), 0)
    c = lax.broadcasted_iota(jnp.int32, (GLA_V_W, GLA_V_W), 1)
    return (r // GLA_DV == c // GLA_DV).astype(F32)


def _gla_in_specs(order):
    r = GLA_ROWS
    return [
        pl.BlockSpec((r, GLA_QK_W), lambda t: (order(t), COL_GQ // GLA_QK_W)),
        pl.BlockSpec((r, GLA_QK_W), lambda t: (order(t), COL_GK // GLA_QK_W)),
        pl.BlockSpec((r, GLA_V_W), lambda t: (order(t), COL_GV // GLA_V_W)),
        pl.BlockSpec((r, GLA_V_W), lambda t: (order(t), COL_GG // GLA_V_W)),
        pl.BlockSpec((r, GLA_QK_W), lambda t: (order(t), 0)),
        pl.BlockSpec((1, GLA_QK_W), lambda t: (0, 0)),
        pl.BlockSpec((1, GLA_V_W), lambda t: (0, 0)),
    ]


def _gla_pieces(q_ref, k_ref, v_ref, z_ref, cps):
    chunk = lambda ref: [ref[ci * GLA_CHUNK:(ci + 1) * GLA_CHUNK, :] for ci in range(cps)]
    return chunk(q_ref), chunk(k_ref), chunk(v_ref), chunk(z_ref)


def _gla_fwd(p, z, bg, gain, *, name):
    s = p.shape[0]
    r = GLA_ROWS
    cps = r // GLA_CHUNK

    def body(q_ref, k_ref, v_ref, gg_ref, z_ref, bg_ref, gain_ref, y_ref, oraw_ref, stsave_ref, st_s):
        @pl.when(pl.program_id(0) == 0)
        def _():
            st_s[...] = jnp.zeros_like(st_s)

        st = st_s[...]
        stsave_ref[0] = st
        o, st = _gla_step(*_gla_pieces(q_ref, k_ref, v_ref, z_ref, cps), bg_ref[...], st, _gla_consts())
        oraw_ref[...] = o
        st_s[...] = st
        y_ref[...] = _gla_post(o, gg_ref[...], gain_ref[...], _gla_g64())

    rowv = pl.BlockSpec((r, GLA_V_W), lambda t: (t, 0))
    return pl.pallas_call(
        body, name=name, grid=(s // r,), in_specs=_gla_in_specs(lambda t: t),
        out_specs=[rowv, rowv, pl.BlockSpec((1, GLA_V_W, GLA_QK_W), lambda t: (t, 0, 0))],
        out_shape=[SDS((s, GLA_V_W), F32), SDS((s, GLA_V_W), F32), SDS((s // r, GLA_V_W, GLA_QK_W), F32)],
        scratch_shapes=[pltpu.VMEM((GLA_V_W, GLA_QK_W), F32)],
        compiler_params=_params("arbitrary"),
    )(p, p, p, p, z, bg, gain)


def _gla_bwd(p, z, bg, gain, oraw, stsave, dy_all, *, name):
    s = p.shape[0]
    r = GLA_ROWS
    cps = r // GLA_CHUNK
    nsteps = s // r
    w_qkvg = 2 * GLA_QK_W + 2 * GLA_V_W

    def body(q_ref, k_ref, v_ref, gg_ref, z_ref, bg_ref, gain_ref, oraw_ref, stsave_ref, dy_ref,
             dqkvg_ref, dz_ref, dbg_ref, dgain_ref, dst_s):
        @pl.when(pl.program_id(0) == 0)
        def _():
            dst_s[...] = jnp.zeros_like(dst_s)
            dbg_ref[...] = jnp.zeros_like(dbg_ref)
            dgain_ref[...] = jnp.zeros_like(dgain_ref)

        _, vjp = jax.vjp(functools.partial(_gla_post, g64=_gla_g64()), oraw_ref[...], gg_ref[...], gain_ref[...])
        do, dgg, dgain = vjp(dy_ref[...])
        dqkvg_ref[:, 2 * GLA_QK_W + GLA_V_W:] = dgg
        dgain_ref[...] += dgain
        _, vjp = jax.vjp(functools.partial(_gla_step, c=_gla_consts()), *_gla_pieces(q_ref, k_ref, v_ref, z_ref, cps),
                         bg_ref[...], stsave_ref[0])
        dq, dk, dv, dz, dbg, dst = vjp((do, dst_s[...]))
        for ci in range(cps):
            rows = slice(ci * GLA_CHUNK, (ci + 1) * GLA_CHUNK)
            dqkvg_ref[rows, 0:GLA_QK_W] = dq[ci]
            dqkvg_ref[rows, GLA_QK_W:2 * GLA_QK_W] = dk[ci]
            dqkvg_ref[rows, 2 * GLA_QK_W:2 * GLA_QK_W + GLA_V_W] = dv[ci]
            dz_ref[rows, :] = dz[ci]
        dst_s[...] = dst
        dbg_ref[...] += dbg

    order = lambda t: nsteps - 1 - t
    rowv = pl.BlockSpec((r, GLA_V_W), lambda t: (order(t), 0))
    return pl.pallas_call(
        body, name=name, grid=(nsteps,),
        in_specs=_gla_in_specs(order) + [
            rowv, pl.BlockSpec((1, GLA_V_W, GLA_QK_W), lambda t: (order(t), 0, 0)),
            pl.BlockSpec((r, GLA_V_W), lambda t: (order(t), SWA_Q_W // GLA_V_W))],
        out_specs=[pl.BlockSpec((r, w_qkvg), lambda t: (order(t), 0)), pl.BlockSpec((r, GLA_QK_W), lambda t: (order(t), 0)),
                   pl.BlockSpec((1, GLA_QK_W), lambda t: (0, 0)), pl.BlockSpec((1, GLA_V_W), lambda t: (0, 0))],
        out_shape=[SDS((s, w_qkvg), F32), SDS((s, GLA_QK_W), F32), SDS((1, GLA_QK_W), F32), SDS((1, GLA_V_W), F32)],
        scratch_shapes=[pltpu.VMEM((GLA_V_W, GLA_QK_W), F32)],
        compiler_params=_params("arbitrary"),
    )(p, p, p, p, z, bg, gain, oraw, stsave, dy_all)


def _local_step(x, mem, target, small, big, on_grads):
    g1, gmix, gmem, g2, sqg, skg, sinks, rel_bias, wgu, bg, gla_gain, mqg, mkg = small
    w3_1, win_p, wkv, wout, gather_ffn2 = big
    wgu_pad = jnp.zeros((GLA_QK_W, GLA_QK_W), BF16).at[:GLA_RANK].set(wgu.astype(BF16))
    gain256 = jnp.tile(gla_gain, (1, GLA_HEADS))

    x1, saved1 = _ffn_fwd(x, g1, w3_1, "ffn1")
    w3_2 = gather_ffn2(x1)
    h = _rms_fwd(x1, gmix, tm=256, name="mix_rms")
    p = _matmul(h, win_p, tm=1024, tn=IN_W_PAD, name="mix_in")
    hm = _rms_fwd(mem, gmem, tm=256, name="mem_rms")
    kv = _matmul(hm, wkv, tm=256, tn=512, name="mem_kv")
    p_glr = p[:, COL_GLR:]
    z = _matmul(p_glr, wgu_pad, tm=1024, tn=GLA_QK_W, name="gla_gate")
    y_swa = _swa_fwd(p, sqg, skg, sinks, rel_bias, name="swa_fwd")
    y_gla, oraw, stsave = _gla_fwd(p, z, bg, gain256, name="gla_fwd")
    y_mem = _mem_fwd(p, kv, mqg, mkg, tq=512, name="mem_fwd")
    x2 = _matmul([y_swa, y_gla, y_mem], wout, b_blocks=[0, 2, 3], tm=1024, tn=1024, res=x1, name="mix_out")
    x3, saved2 = _ffn_fwd(x2, g2, w3_2, "ffn2")

    dy, dyb, loss = _loss_bwd(x3, target, tm=256, name="loss")
    tiles = w3_2.shape[1] // FFN_TN
    dh2, dw3_2 = _ffn_bwd_part(dyb, w3_2, saved2, 0, tiles, None, name="ffn2_bwd")
    dx2, dx2b, dg2 = _rms_bwd(x2, g2, dh2, dy, tm=256, name="ffn2_drms")
    dx2b = on_grads("ffn2", [dw3_2.reshape(3, 2, -1, dw3_2.shape[-1])], dx2b)
    dy_all = _matmul(dx2b, wout, tb=True, tm=1024, tn=1024, name="mix_dy")
    dwout = _matmul(jnp.concatenate([y_swa, y_gla, y_mem], axis=1), dx2b, ta=True, tm=512, tn=1024, out_dtype=BF16,
                    name="mix_dw_out")
    dq_swa, dkv_swa, dsqg, dskg, dsink, drb = _swa_bwd(p, sqg, skg, sinks, rel_bias, dy_all, name="swa_bwd")
    dqkvg, dz, dbg, dgain256 = _gla_bwd(p, z, bg, gain256, oraw, stsave, dy_all, name="gla_bwd")
    dmq, dkv_mem, dmqg, dmkg = _mem_bwd(p, kv, mqg, mkg, dy_all, tq=512, name="mem_bwd")
    dglr = _matmul(dz, wgu_pad, tb=True, tm=1024, tn=GLA_QK_W, name="gla_gate_dx")
    dwgu_pad = _matmul(p_glr, dz, ta=True, tm=GLA_QK_W, tn=GLA_QK_W, name="gla_gate_dw")
    dp = jnp.concatenate([dq_swa, dkv_swa, dqkvg, dmq, dglr], axis=1)
    dh = _matmul(dp, win_p, tb=True, tm=1024, tn=1024, name="mix_dh")
    dwin_p = _matmul(h, dp, ta=True, tm=1024, tn=640, out_dtype=BF16, name="mix_dw_in")
    dx1, dx1b, dgmix = _rms_bwd(x1, gmix, dh, dx2, tm=256, name="mix_drms")
    dwkv = _matmul(hm, dkv_mem, ta=True, tm=512, tn=512, out_dtype=BF16, name="mem_dw_kv")
    dx1b = on_grads("mix", (dwin_p, dwkv, dwout), dx1b)
    dhm = _matmul(dkv_mem, wkv, tb=True, tm=256, tn=512, name="mem_dh")
    _, _, dgmem = _rms_bwd(mem, gmem, dhm, None, tm=256, name="mem_drms")
    dh1, dw3_1a = _ffn_bwd_part(dx1b, w3_1, saved1, 0, tiles // 2, None, name="ffn1_bwd_a")
    dgla_gain = dgain256.reshape(GLA_HEADS, GLA_DV).sum(axis=0, keepdims=True)
    dsmall = [dgmix, dgmem, dg2, dsqg, dskg, dsink[0, :SWA_HEADS], drb[:, :SWA_HEADS], dwgu_pad[:GLA_RANK], dbg,
              dgla_gain, dmqg, dmkg, loss]
    dh1, dgmem, dwgu_pad = on_grads("ffn1a", [dw3_1a[:, None]], (dh1, dgmem, dwgu_pad), small=dsmall)
    dh1, dw3_1b = _ffn_bwd_part(dx1b, w3_1, saved1, tiles // 2, tiles // 2, dh1, name="ffn1_bwd_b")
    dx, _, dg1 = _rms_bwd(x, g1, dh1, dx1, tm=256, name="ffn1_drms")
    on_grads("ffn1b", [dw3_1b[:, None]], None, small=[dg1])
    return dx


def _mesh_place():
    x, y, c = lax.axis_index("x"), lax.axis_index("y"), lax.axis_index("c")
    other_chips = [(1 - x, y), (x, 1 - y), (1 - x, 1 - y)]
    return x, y, c, other_chips


def _handshake(peers):
    barrier = pltpu.get_barrier_semaphore()
    for peer in peers:
        pl.semaphore_signal(barrier, inc=1, device_id=peer, device_id_type=MESH)
    pl.semaphore_wait(barrier, len(peers))


def _sequencer_call(body, operands, out_shapes, sems, *, name, collective_id):
    return pl.kernel(
        body, name=name, out_type=out_shapes, mesh=plsc.ScalarSubcoreMesh(axis_name="sequencer", num_cores=1),
        scratch_types=sems, compiler_params=pltpu.CompilerParams(collective_id=collective_id),
    )(*operands)


def _window(ref, kind, slot, shape):
    if kind == "row":
        rows = pl.ds(pl.multiple_of(slot * shape[-2], 8), shape[-2])
        return ref.at[(slice(None),) * (len(shape) - 2) + (rows,)]
    return ref.at[slot]


def _gathered(shape, kind):
    if kind == "row":
        return tuple(shape[:-2]) + (N_DEV * shape[-2], shape[-1])
    return (N_DEV,) + tuple(shape)


def _all_gather(shards, kinds, *, name, collective_id):
    nt = len(shards)

    def body(*refs):
        x_refs, o_refs = refs[:nt], refs[nt:2 * nt]
        send_sems, recv_sems, local_sems = refs[2 * nt:]
        x, y, c, chips = _mesh_place()
        me, sibling = (x, y, c), (x, y, 1 - c)
        _handshake([sibling] + [(*chip, c) for chip in chips])

        def copy(k, t, block, to, from_shard=False):
            bx, by, bc = block
            rows = _window(o_refs[t], kinds[t], 4 * bx + 2 * by + bc, shards[t].shape)
            return pltpu.make_async_remote_copy(
                src_ref=x_refs[t] if from_shard else rows, dst_ref=rows,
                send_sem=send_sems.at[k, t], recv_sem=recv_sems.at[k, t], device_id=to, device_id_type=MESH)

        mine = [pltpu.make_async_copy(x_refs[t], _window(o_refs[t], kinds[t], 4 * x + 2 * y + c, shards[t].shape),
                                      local_sems.at[t]) for t in range(nt)]
        for cp in mine:
            cp.start()
        first = [copy(0, t, me, sibling, True) for t in range(nt)]
        first += [copy(1 + j, t, me, (*chip, c), True) for j, chip in enumerate(chips) for t in range(nt)]
        for cp in first:
            cp.start()
        passed = []
        for j, chip in enumerate(chips):
            for t in range(nt):
                copy(1 + j, t, (*chip, c), me).wait_recv()
                fwd = copy(4 + j, t, (*chip, c), sibling)
                fwd.start()
                passed.append(fwd)
        for t in range(nt):
            copy(0, t, sibling, me).wait_recv()
        for j, chip in enumerate(chips):
            for t in range(nt):
                copy(4 + j, t, (*chip, 1 - c), me).wait_recv()
        for cp in first + passed:
            cp.wait_send()
        for cp in mine:
            cp.wait()

    return _sequencer_call(
        body, shards, [SDS(_gathered(s.shape, k), s.dtype) for s, k in zip(shards, kinds)],
        [pltpu.SemaphoreType.DMA((7, nt)), pltpu.SemaphoreType.DMA((7, nt)), pltpu.SemaphoreType.DMA((nt,))],
        name=name, collective_id=collective_id)


def _part_shape(shape, kind):
    if kind == "row":
        return tuple(shape[:-2]) + (shape[-2] // N_DEV, shape[-1])
    return tuple(shape[2:])


def _pair_exchange(grads, kinds, *, name, collective_id):
    nt = len(grads)
    part = [_part_shape(g.shape, k) for g, k in zip(grads, kinds)]

    def body(*refs):
        g_refs, o_refs = refs[:nt], refs[nt:2 * nt]
        send_sems, recv_sems = refs[2 * nt:]
        x, y, c, _ = _mesh_place()
        _handshake([(x, y, 1 - c)])
        copies = []
        for t in range(nt):
            for xy in range(4):
                src = g_refs[t].at[1 - c, xy] if kinds[t] == "stack" else _window(g_refs[t], kinds[t], 2 * xy + 1 - c, part[t])
                copies.append(pltpu.make_async_remote_copy(
                    src_ref=src, dst_ref=o_refs[t].at[xy], send_sem=send_sems.at[xy, t], recv_sem=recv_sems.at[xy, t],
                    device_id=(x, y, 1 - c), device_id_type=MESH))
        for cp in copies:
            cp.start()
        for cp in copies:
            cp.wait()

    return _sequencer_call(
        body, grads, [SDS((4,) + p, g.dtype) for p, g in zip(part, grads)],
        [pltpu.SemaphoreType.DMA((4, nt)), pltpu.SemaphoreType.DMA((4, nt))], name=name, collective_id=collective_id)


def _chip_exchange(parts, small, *, name, collective_id):
    nt = len(parts)
    if small is None:
        def body_plain(*refs):
            s_refs, o_refs = refs[:nt], refs[nt:2 * nt]
            send_sems, recv_sems = refs[2 * nt:]
            x, y, c, chips = _mesh_place()
            _handshake([(*chip, c) for chip in chips])
            copies = [pltpu.make_async_remote_copy(
                src_ref=s_refs[t].at[2 * chip[0] + chip[1]], dst_ref=o_refs[t].at[j],
                send_sem=send_sems.at[j, t], recv_sem=recv_sems.at[j, t], device_id=(*chip, c), device_id_type=MESH)
                for j, chip in enumerate(chips) for t in range(nt)]
            for cp in copies:
                cp.start()
            for cp in copies:
                cp.wait()

        return _sequencer_call(
            body_plain, parts, [SDS((3,) + s.shape[1:], s.dtype) for s in parts],
            [pltpu.SemaphoreType.DMA((3, nt)), pltpu.SemaphoreType.DMA((3, nt))], name=name, collective_id=collective_id)

    def body(*refs):
        s_refs, small_ref = refs[:nt], refs[nt]
        o_refs, small_all = refs[nt + 1:2 * nt + 1], refs[2 * nt + 1]
        send_sems, recv_sems, small_send, small_recv, local_sem = refs[2 * nt + 2:]
        x, y, c, chips = _mesh_place()
        _handshake([(px, py, pc) for px in (x, 1 - x) for py in (y, 1 - y) for pc in (c, 1 - c)][1:])

        def copy(j, t, chip):
            return pltpu.make_async_remote_copy(
                src_ref=s_refs[t].at[2 * chip[0] + chip[1]], dst_ref=o_refs[t].at[j],
                send_sem=send_sems.at[j, t], recv_sem=recv_sems.at[j, t], device_id=(*chip, c), device_id_type=MESH)

        flips = [(fx, fy, fc) for fx in (0, 1) for fy in (0, 1) for fc in (0, 1)][1:]

        def small_copy(k):
            fx, fy, fc = flips[k]
            to = (x ^ fx if fx else x, y ^ fy if fy else y, c ^ fc if fc else c)
            rows = small_all.at[4 * x + 2 * y + c]
            return pltpu.make_async_remote_copy(
                src_ref=small_ref, dst_ref=rows, send_sem=small_send.at[k], recv_sem=small_recv.at[k],
                device_id=to, device_id_type=MESH)

        own = pltpu.make_async_copy(small_ref, small_all.at[4 * x + 2 * y + c], local_sem)
        own.start()
        copies = [copy(j, t, chip) for j, chip in enumerate(chips) for t in range(nt)]
        smalls = [small_copy(k) for k in range(7)]
        for cp in smalls + copies:
            cp.start()
        for cp in smalls + copies:
            cp.wait()
        own.wait()

    return _sequencer_call(
        body, list(parts) + [small],
        [SDS((3,) + s.shape[1:], s.dtype) for s in parts] + [SDS((N_DEV,) + small.shape, small.dtype)],
        [pltpu.SemaphoreType.DMA((3, nt)), pltpu.SemaphoreType.DMA((3, nt)),
         pltpu.SemaphoreType.DMA((7,)), pltpu.SemaphoreType.DMA((7,)), pltpu.SemaphoreType.DMA],
        name=name, collective_id=collective_id)


def _pair_sum(grad, theirs, kind, c, *, name):
    if kind == "row":
        r, l = theirs.shape[-2:]
        n = theirs.size // (4 * r * l)
        grad, theirs = grad.reshape(n, N_DEV * r, l), theirs.reshape(4, n, r, l)
        mine_spec = pl.BlockSpec((None, r, l), lambda xy, m, c_ref: (m, 2 * xy + c_ref[0], 0))
    else:
        r, l = theirs.shape[-2:]
        n = theirs.size // (4 * r * l)
        theirs = theirs.reshape(4, n, r, l)
        grad = grad.reshape(2, 4, n, r, l)
        mine_spec = pl.BlockSpec((None, None, None, r, l), lambda xy, m, c_ref: (c_ref[0], xy, m, 0, 0))

    def body(c_ref, a_ref, b_ref, o_ref):
        o_ref[...] = (a_ref[...].astype(F32) + b_ref[...].astype(F32)).astype(BF16)

    part = pl.BlockSpec((None, None, r, l), lambda xy, m, c_ref: (xy, m, 0, 0))
    return pl.pallas_call(
        body, name=name,
        grid_spec=pltpu.PrefetchScalarGridSpec(num_scalar_prefetch=1, grid=(4, n), in_specs=[mine_spec, part], out_specs=part),
        out_shape=SDS((4, n, r, l), BF16), compiler_params=_params("parallel", "parallel"),
    )(c, grad, theirs)


def _adamw(w, g, m, v):
    m = ADAM_B1 * m + (1.0 - ADAM_B1) * g
    v = ADAM_B2 * v + (1.0 - ADAM_B2) * jnp.square(g)
    m_hat = m / (1.0 - ADAM_B1 ** ADAM_STEP)
    v_hat = v / (1.0 - ADAM_B2 ** ADAM_STEP)
    delta = -ADAM_LR * (m_hat / (jnp.sqrt(v_hat) + ADAM_EPS) + ADAM_WD * w)
    return delta, m, v


def _adam_big(owns, others, mat, xy, w, m, v, *, tr, name):
    nw, r, l = w.shape
    lp = owns[0].shape[-1]
    nq = len(owns)
    assert nq in (1, nw)

    def body(xy_ref, *refs):
        own_refs, oth_refs = refs[:nq], refs[nq:2 * nq]
        w_ref, m_ref, v_ref, g_out, d_out, m_out, v_out = refs[2 * nq:]
        g = None
        for q in range(nq):
            gq = own_refs[q][0, 0].astype(F32)
            for j in range(3):
                gq = gq + oth_refs[q][j, 0].astype(F32)
            g = gq if g is None else jnp.where(pl.program_id(0) == q, gq, g)
        g = g[:, :l]
        delta, m_new, v_new = _adamw(w_ref[0], g, m_ref[0], v_ref[0])
        g_out[0] = g
        d_out[0] = delta
        m_out[0] = m_new
        v_out[0] = v_new

    def at(p):
        return mat * nw + p if nq == 1 else mat

    blk = pl.BlockSpec((1, tr, l), lambda p, i, xy_ref: (p, i, 0))
    return pl.pallas_call(
        body, name=name,
        grid_spec=pltpu.PrefetchScalarGridSpec(
            num_scalar_prefetch=1, grid=(nw, r // tr),
            in_specs=[pl.BlockSpec((1, 1, tr, lp), lambda p, i, xy_ref: (xy_ref[0], at(p), i, 0))] * nq
            + [pl.BlockSpec((3, 1, tr, lp), lambda p, i, xy_ref: (0, at(p), i, 0))] * nq + [blk, blk, blk],
            out_specs=[blk, blk, blk, blk]),
        out_shape=[SDS(w.shape, F32)] * 4, compiler_params=_params("parallel", "parallel"),
    )(xy, *owns, *others, w, m, v)


def _adam_small(g_all, w, m, v, *, name):
    def body(g_ref, w_ref, m_ref, v_ref, g_out, d_out, m_out, v_out):
        g = g_ref[0]
        for k in range(1, N_DEV):
            g = g + g_ref[k]
        delta, m_new, v_new = _adamw(w_ref[...], g, m_ref[...], v_ref[...])
        g_out[...] = g
        d_out[...] = delta
        m_out[...] = m_new
        v_out[...] = v_new

    return pl.pallas_call(body, name=name, out_shape=[SDS(w.shape, F32)] * 4)(g_all, w, m, v)


def _pack_small(parts):
    flat = jnp.concatenate([a.reshape(-1) for a in parts])
    rows = -(-flat.shape[0] // 1024) * 8
    return jnp.pad(flat, (0, rows * 128 - flat.shape[0])).reshape(rows, 128)


def _unpack_small(packed, shapes):
    flat = packed.reshape(-1)
    out, at = [], 0
    for s in shapes:
        n = math.prod(s)
        out.append(flat[at:at + n].reshape(s))
        at += n
    return out


def kernel(x, mem, ffn1_norm, ffn1_w_gate, ffn1_w_up, ffn1_w_down, mix_norm, mem_norm, w_in, w_mem_kv, swa_q_norm, swa_k_norm, swa_sinks, rel_bias, gla_w_gate_up, gla_b_gate, gla_out_norm, mem_q_norm, mem_k_norm, w_out, ffn2_norm, ffn2_w_gate, ffn2_w_up, ffn2_w_down, loss_target, m_ffn1_norm, m_ffn1_w_gate, m_ffn1_w_up, m_ffn1_w_down, m_mix_norm, m_mem_norm, m_w_in, m_w_mem_kv, m_swa_q_norm, m_swa_k_norm, m_swa_sinks, m_rel_bias, m_gla_w_gate_up, m_gla_b_gate, m_gla_out_norm, m_mem_q_norm, m_mem_k_norm, m_w_out, m_ffn2_norm, m_ffn2_w_gate, m_ffn2_w_up, m_ffn2_w_down, v_ffn1_norm, v_ffn1_w_gate, v_ffn1_w_up, v_ffn1_w_down, v_mix_norm, v_mem_norm, v_w_in, v_w_mem_kv, v_swa_q_norm, v_swa_k_norm, v_swa_sinks, v_rel_bias, v_gla_w_gate_up, v_gla_b_gate, v_gla_out_norm, v_mem_q_norm, v_mem_k_norm, v_w_out, v_ffn2_norm, v_ffn2_w_gate, v_ffn2_w_up, v_ffn2_w_down):
    xi, yi, ci = lax.axis_index("x"), lax.axis_index("y"), lax.axis_index("c")
    c_arr = jnp.reshape(ci, (1,)).astype(jnp.int32)
    xy_arr = jnp.reshape(2 * xi + yi, (1,)).astype(jnp.int32)
    d = x.shape[-1]

    half_h = ffn1_w_gate.shape[-1] // 2

    def gather_ffn(wg_s, wu_s, wd_s, name, collective_id, after):
        w3_s = jnp.concatenate([wg_s.transpose(0, 2, 1), wu_s.transpose(0, 2, 1), wd_s], axis=0)
        w3_s = jnp.pad(w3_s.reshape(3, 2, half_h, d), ((0, 0), (0, 0), (0, FFN_HALF_PAD - half_h), (0, 0))).astype(BF16)
        if after is not None:
            w3_s, _ = lax.optimization_barrier((w3_s, after))
        return _all_gather([w3_s], ["row"], name=name, collective_id=collective_id)[0].reshape(3, -1, d)

    w3_1 = gather_ffn(ffn1_w_gate, ffn1_w_up, ffn1_w_down, "gather_ffn1", 0, None)
    mix_s = lax.optimization_barrier((w_in[0].astype(BF16), w_mem_kv[0].astype(BF16), w_out[0].astype(BF16), w3_1))[:3]
    win_all, wkv, wout = _all_gather(list(mix_s), ["stack", "row", "row"], name="gather_mix", collective_id=1)

    def gather_ffn2(x1):
        return gather_ffn(ffn2_w_gate, ffn2_w_up, ffn2_w_down, "gather_ffn2", 2, (wout, x1))

    win_p = _pack_win(win_all, tr=256, name="pack_w_in")

    small_w = [ffn1_norm, mix_norm, mem_norm, ffn2_norm, swa_q_norm, swa_k_norm, swa_sinks[0], rel_bias,
               gla_w_gate_up[0], gla_b_gate, gla_out_norm, mem_q_norm, mem_k_norm]
    collective_ids = {"ffn2": (3, 4), "mix": (5, 6), "ffn1a": (7, 8), "ffn1b": (9, 10)}
    reduced, small_box = {}, {}

    def on_grads(group, grads, carry, small=None):
        if group == "mix":
            dwin_p, dwkv, dwout = grads
            grads = [_unpack_win(dwin_p, tr=256, name="unpack_dw_in"), dwkv, dwout]
            kinds = ["stack", "row", "row"]
        else:
            kinds = ["row"]
        if reduced:
            earlier = list(reduced.values())[-1][1]
            *grads, _ = lax.optimization_barrier((*grads, earlier[0]))
        id_pair, id_chip = collective_ids[group]
        from_sibling = _pair_exchange(grads, kinds, name=f"pair_exchange_{group}", collective_id=id_pair)
        chip_sums = [_pair_sum(g, theirs, k, c_arr, name=f"pair_sum_{group}_{t}")
                     for t, (g, theirs, k) in enumerate(zip(grads, from_sibling, kinds))]
        if carry is not None:
            *chip_sums, carry = lax.optimization_barrier((*chip_sums, carry))
        if small is None:
            from_chips = _chip_exchange(chip_sums, None, name=f"chip_exchange_{group}", collective_id=id_chip)
        else:
            *from_chips, small_all = _chip_exchange(chip_sums, _pack_small(small), name=f"chip_exchange_{group}",
                                                    collective_id=id_chip)
            small_box[group] = small_all
        reduced[group] = (chip_sums, from_chips)
        return carry

    grad_x = _local_step(x[0], mem[0], loss_target[0], small_w, (w3_1, win_p, wkv, wout, gather_ffn2), on_grads)

    big_w = {"ffn1_w_gate": ("ffn1", 0, 0, True, ffn1_w_gate, m_ffn1_w_gate, v_ffn1_w_gate),
             "ffn1_w_up": ("ffn1", 0, 1, True, ffn1_w_up, m_ffn1_w_up, v_ffn1_w_up),
             "ffn1_w_down": ("ffn1", 0, 2, False, ffn1_w_down, m_ffn1_w_down, v_ffn1_w_down),
             "w_in": ("mix", 0, 0, False, w_in, m_w_in, v_w_in),
             "w_mem_kv": ("mix", 1, 0, False, w_mem_kv, m_w_mem_kv, v_w_mem_kv),
             "w_out": ("mix", 2, 0, False, w_out, m_w_out, v_w_out),
             "ffn2_w_gate": ("ffn2", 0, 0, True, ffn2_w_gate, m_ffn2_w_gate, v_ffn2_w_gate),
             "ffn2_w_up": ("ffn2", 0, 1, True, ffn2_w_up, m_ffn2_w_up, v_ffn2_w_up),
             "ffn2_w_down": ("ffn2", 0, 2, False, ffn2_w_down, m_ffn2_w_down, v_ffn2_w_down)}
    res = {}
    for nm, (group, t, mat, transposed, w, m, v) in big_w.items():
        shape = w.shape
        if transposed:
            w, m, v = (a.transpose(0, 2, 1) for a in (w, m, v))
        if group != "mix":
            w, m, v = (a.reshape(2, half_h, d) for a in (w, m, v))
        r = w.shape[1]
        tr = 256 if r % 256 == 0 else r
        halves = ["ffn1a", "ffn1b"] if group == "ffn1" else [group]
        out = _adam_big([reduced[k][0][t] for k in halves], [reduced[k][1][t] for k in halves], mat, xy_arr, w, m, v,
                        tr=tr, name=f"adam_{nm}")
        if transposed:
            out = [a.reshape(1, -1, d).transpose(0, 2, 1) for a in out]
        res[nm] = [a.reshape(shape) for a in out]
    small_names = ["ffn1_norm", "mix_norm", "mem_norm", "ffn2_norm", "swa_q_norm", "swa_k_norm", "swa_sinks", "rel_bias",
                   "gla_w_gate_up", "gla_b_gate", "gla_out_norm", "mem_q_norm", "mem_k_norm"]
    small_m = [m_ffn1_norm, m_mix_norm, m_mem_norm, m_ffn2_norm, m_swa_q_norm, m_swa_k_norm, m_swa_sinks, m_rel_bias,
               m_gla_w_gate_up, m_gla_b_gate, m_gla_out_norm, m_mem_q_norm, m_mem_k_norm]
    small_v = [v_ffn1_norm, v_mix_norm, v_mem_norm, v_ffn2_norm, v_swa_q_norm, v_swa_k_norm, v_swa_sinks, v_rel_bias,
               v_gla_w_gate_up, v_gla_b_gate, v_gla_out_norm, v_mem_q_norm, v_mem_k_norm]
    small_full = [ffn1_norm, mix_norm, mem_norm, ffn2_norm, swa_q_norm, swa_k_norm, swa_sinks, rel_bias,
                  gla_w_gate_up, gla_b_gate, gla_out_norm, mem_q_norm, mem_k_norm]
    zero = jnp.zeros((), F32)
    for group, sel in (("ffn1a", slice(1, None)), ("ffn1b", slice(0, 1))):
        late = group == "ffn1b"
        extra = [] if late else [zero]
        packed = _adam_small(small_box[group], _pack_small(small_full[sel] + extra), _pack_small(small_m[sel] + extra),
                             _pack_small(small_v[sel] + extra), name=f"adam_small_{group}")
        shapes = [a.shape for a in small_full[sel]] + [()] * len(extra)
        unpacked = [_unpack_small(pk, shapes) for pk in packed]
        for k, nm in enumerate(small_names[sel]):
            res[nm] = [unpacked[q][k] for q in range(4)]
        if not late:
            loss = unpacked[0][-1]

    order = ["ffn1_norm", "ffn1_w_gate", "ffn1_w_up", "ffn1_w_down", "mix_norm", "mem_norm", "w_in", "w_mem_kv",
             "swa_q_norm", "swa_k_norm", "swa_sinks", "rel_bias", "gla_w_gate_up", "gla_b_gate", "gla_out_norm",
             "mem_q_norm", "mem_k_norm", "w_out", "ffn2_norm", "ffn2_w_gate", "ffn2_w_up", "ffn2_w_down"]
    outs = [loss, grad_x[None]]
    for q in range(4):
        outs += [res[nm][q] for nm in order]
    return tuple(outs)
```

```python
import functools
import math

import numpy as np
import jax
import jax.numpy as jnp
from jax import lax
from jax.experimental import pallas as pl
from jax.experimental.pallas import tpu as pltpu
from jax.experimental.pallas import tpu_sc as plsc

F32 = jnp.float32
BF16 = jnp.bfloat16
SDS = jax.ShapeDtypeStruct

EPS = 1e-6
HEAD_DIM = 64
SWA_HEADS = 8
SWA_KV_HEADS = 2
SWA_GROUP = SWA_HEADS // SWA_KV_HEADS
BLOCK = 128
N_BUCKETS = 32
MAX_DISTANCE = 128
GLA_HEADS = 4
GLA_DK = 32
GLA_DV = 64
GLA_RANK = 16
GLA_TAU = 16.0
GLA_CHUNK = 32
MEM_HEADS = 4
SWA_Q_W = SWA_HEADS * HEAD_DIM
SWA_KV_W = SWA_KV_HEADS * HEAD_DIM
GLA_QK_W = GLA_HEADS * GLA_DK
GLA_V_W = GLA_HEADS * GLA_DV
MEM_Q_W = MEM_HEADS * HEAD_DIM
IN_W = 1808
IN_W_PAD = 1920
COL_SQ, COL_SKV, COL_GQ, COL_GK, COL_GV, COL_GG, COL_MQ, COL_GLR = 0, 512, 768, 896, 1024, 1280, 1536, 1792

ADAM_LR = 0.001
ADAM_B1 = 0.9
ADAM_B2 = 0.999
ADAM_EPS = 1e-08
ADAM_WD = 0.01
ADAM_STEP = 10

N_DEV = 8
VMEM_LIMIT_BYTES = 56 * 1024 * 1024
MESH = pl.DeviceIdType.MESH


def _params(*sem):
    return pltpu.CompilerParams(dimension_semantics=sem or None, vmem_limit_bytes=VMEM_LIMIT_BYTES)


def _dot(a, b, ta, tb, precision=None):
    dims = (((0 if ta else 1,), (1 if tb else 0,)), ((), ()))
    return lax.dot_general(a, b, dims, preferred_element_type=F32, precision=precision)


def _mm_raw(a, b, ta=False, tb=False):
    return _dot(a.astype(BF16), b.astype(BF16), ta, tb)


def _mmf_raw(a, b, ta=False, tb=False):
    return _dot(a, b, ta, tb, lax.Precision.HIGHEST)


def _make_mm(raw):
    @functools.partial(jax.custom_vjp, nondiff_argnums=(2, 3))
    def mm(a, b, ta=False, tb=False):
        return raw(a, b, ta, tb)

    def fwd(a, b, ta, tb):
        return raw(a, b, ta, tb), (a, b)

    def bwd(ta, tb, res, g):
        a, b = res
        da = raw(b, g, tb, True) if ta else raw(g, b, False, not tb)
        db = raw(g, a, True, ta) if tb else raw(a, g, not ta, False)
        return da, db

    mm.defvjp(fwd, bwd)
    return mm


_mm = _make_mm(_mm_raw)
_mmf = _make_mm(_mmf_raw)


def _mm3(a, b, ta=False, tb=False):
    a_hi, b_hi = a.astype(BF16).astype(F32), b.astype(BF16).astype(F32)
    return _mm(a_hi, b_hi, ta, tb) + _mm(a_hi, b - b_hi, ta, tb) + _mm(a - a_hi, b_hi, ta, tb)


def _rms(x, g):
    return x * lax.rsqrt(jnp.mean(x * x, axis=-1, keepdims=True) + EPS) * g


def _silu_mul(g, u):
    return jax.nn.silu(g) * u


def _log_sigmoid(z):
    return jnp.minimum(z, 0.0) - jnp.log(1.0 + jnp.exp(-jnp.abs(z)))


def _matmul(a_list, b, *, ta=False, tb=False, tm, tn, b_blocks=None, res=None, scale=1.0, out_dtype=F32, name):
    if not isinstance(a_list, (list, tuple)):
        a_list = [a_list]
    n_a = len(a_list)
    m = a_list[0].shape[1] if ta else a_list[0].shape[0]
    ks = [a.shape[0] if ta else a.shape[1] for a in a_list]
    n = b.shape[0] if tb else b.shape[1]
    if b_blocks is None:
        assert n_a == 1
        b_blocks = [0]
    tm, tn = min(tm, m), min(tn, n)
    assert m % tm == 0 and n % tn == 0, (m, n, tm, tn)

    def body(*refs):
        a_refs, b_refs = refs[:n_a], refs[n_a:2 * n_a]
        r_ref = refs[2 * n_a] if res is not None else None
        o_ref = refs[-1]
        acc = _mm_raw(a_refs[0][...], b_refs[0][...], ta, tb)
        for k in range(1, n_a):
            acc = acc + _mm_raw(a_refs[k][...], b_refs[k][...], ta, tb)
        if scale != 1.0:
            acc = acc * scale
        if r_ref is not None:
            acc = r_ref[...] + acc
        o_ref[...] = acc.astype(out_dtype)

    in_specs = []
    for k in ks:
        in_specs.append(pl.BlockSpec((k, tm), lambda i, j: (0, i)) if ta else pl.BlockSpec((tm, k), lambda i, j: (i, 0)))
    for k, blk in zip(ks, b_blocks):
        if tb:
            in_specs.append(pl.BlockSpec((tn, k), functools.partial(lambda i, j, blk: (j, blk), blk=blk)))
        else:
            in_specs.append(pl.BlockSpec((k, tn), functools.partial(lambda i, j, blk: (blk, j), blk=blk)))
    args = list(a_list) + [b] * n_a
    if res is not None:
        in_specs.append(pl.BlockSpec((tm, tn), lambda i, j: (i, j)))
        args.append(res)
    return pl.pallas_call(
        body, name=name, grid=(m // tm, n // tn), in_specs=in_specs,
        out_specs=pl.BlockSpec((tm, tn), lambda i, j: (i, j)), out_shape=SDS((m, n), out_dtype),
        compiler_params=_params("parallel", "parallel"),
    )(*args)


def _win_pieces(w):
    glr_lo, glr_hi = COL_MQ, COL_MQ + GLA_RANK
    out = []
    for j in range(N_DEV):
        for lo, hi, shift in ((0, glr_lo, 0), (glr_lo, glr_hi, COL_GLR - glr_lo), (glr_hi, IN_W, COL_MQ - glr_hi)):
            s, e = max(j * w, lo), min((j + 1) * w, hi)
            if s < e:
                out.append((j, s - j * w, e - j * w, s + shift))
    return out


def _pack_win(win_all, *, tr, name):
    _, d, w = win_all.shape

    def body(i_ref, o_ref):
        for j, a, b, dst in _win_pieces(w):
            o_ref[:, dst:dst + b - a] = i_ref[j][:, a:b]
        o_ref[:, IN_W:] = jnp.zeros((tr, IN_W_PAD - IN_W), o_ref.dtype)

    return pl.pallas_call(
        body, name=name, grid=(d // tr,), in_specs=[pl.BlockSpec((N_DEV, tr, w), lambda i: (0, i, 0))],
        out_specs=pl.BlockSpec((tr, IN_W_PAD), lambda i: (i, 0)), out_shape=SDS((d, IN_W_PAD), win_all.dtype),
        compiler_params=_params("parallel"),
    )(win_all)


def _unpack_win(dwin_p, *, tr, name):
    d = dwin_p.shape[0]
    w = IN_W // N_DEV

    def body(i_ref, o_ref):
        for j, a, b, src in _win_pieces(w):
            o_ref[j % 2, j // 2, :, a:b] = i_ref[:, src:src + b - a]

    return pl.pallas_call(
        body, name=name, grid=(d // tr,), in_specs=[pl.BlockSpec((tr, IN_W_PAD), lambda i: (i, 0))],
        out_specs=pl.BlockSpec((2, 4, tr, w), lambda i: (0, 0, i, 0)), out_shape=SDS((2, 4, d, w), dwin_p.dtype),
        compiler_params=_params("parallel"),
    )(dwin_p)


def _rms_fwd(x, g, *, tm, name):
    s, d = x.shape

    def body(x_ref, g_ref, h_ref):
        h_ref[...] = _rms(x_ref[...], g_ref[...]).astype(BF16)

    return pl.pallas_call(
        body, name=name, grid=(s // tm,),
        in_specs=[pl.BlockSpec((tm, d), lambda i: (i, 0)), pl.BlockSpec((1, d), lambda i: (0, 0))],
        out_specs=pl.BlockSpec((tm, d), lambda i: (i, 0)), out_shape=SDS((s, d), BF16),
        compiler_params=_params("parallel"),
    )(x, g)


def _rms_bwd(x, g, dh, dres, *, tm, name):
    s, d = x.shape
    want_dx = dres is not None

    def body(*refs):
        if want_dx:
            x_ref, g_ref, dh_ref, dres_ref, dx_ref, dxb_ref, dg_ref = refs
        else:
            x_ref, g_ref, dh_ref, dg_ref = refs
        _, vjp = jax.vjp(_rms, x_ref[...], g_ref[...])
        dx, dg = vjp(dh_ref[...])
        if want_dx:
            dx = dres_ref[...] + dx
            dx_ref[...] = dx
            dxb_ref[...] = dx.astype(BF16)

        @pl.when(pl.program_id(0) == 0)
        def _():
            dg_ref[...] = jnp.zeros_like(dg_ref)

        dg_ref[...] += dg

    row = pl.BlockSpec((tm, d), lambda i: (i, 0))
    vec = pl.BlockSpec((1, d), lambda i: (0, 0))
    if want_dx:
        return pl.pallas_call(
            body, name=name, grid=(s // tm,), in_specs=[row, vec, row, row], out_specs=[row, row, vec],
            out_shape=[SDS((s, d), F32), SDS((s, d), BF16), SDS((1, d), F32)], compiler_params=_params("arbitrary"),
        )(x, g, dh, dres)
    return None, None, pl.pallas_call(
        body, name=name, grid=(s // tm,), in_specs=[row, vec, row], out_specs=vec,
        out_shape=SDS((1, d), F32), compiler_params=_params("arbitrary"),
    )(x, g, dh)


FFN_TN = 256
FFN_TN_FWD = 512
FFN_HALF_PAD = 192


def _ffn_fwd(x, gain, w3, tag, *, tm=1024):
    s, d = x.shape
    f = w3.shape[1]
    tn = FFN_TN_FWD
    nj = f // tn
    tm = min(tm, s)

    def body(x_ref, gain_ref, wg_ref, wu_ref, wd_ref, y_ref, h_ref, g_ref, u_ref, acc_s):
        j = pl.program_id(1)

        @pl.when(j == 0)
        def _():
            h_ref[...] = _rms(x_ref[...], gain_ref[...]).astype(BF16)
            acc_s[...] = jnp.zeros_like(acc_s)

        hv = h_ref[...]
        g = _mm_raw(hv, wg_ref[...], False, True)
        u = _mm_raw(hv, wu_ref[...], False, True)
        g_ref[...] = g.astype(BF16)
        u_ref[...] = u.astype(BF16)
        acc_s[...] += _mm_raw(_silu_mul(g, u), wd_ref[...])

        @pl.when(j == nj - 1)
        def _():
            y_ref[...] = x_ref[...] + 0.5 * acc_s[...]

    row = pl.BlockSpec((tm, d), lambda i, j: (i, 0))
    tile = pl.BlockSpec((tm, tn), lambda i, j: (i, j))
    y, h, g, u = pl.pallas_call(
        body, name=f"{tag}_fwd", grid=(s // tm, nj),
        in_specs=[row, pl.BlockSpec((1, d), lambda i, j: (0, 0))]
        + [pl.BlockSpec((None, tn, d), functools.partial(lambda i, j, k: (k, j, 0), k=k)) for k in range(3)],
        out_specs=[row, row, tile, tile],
        out_shape=[SDS((s, d), F32), SDS((s, d), BF16), SDS((s, f), BF16), SDS((s, f), BF16)],
        scratch_shapes=[pltpu.VMEM((tm, d), F32)],
        compiler_params=_params("parallel", "arbitrary"),
    )(x, gain, w3, w3, w3)
    return y, (h, g, u)


def _ffn_bwd_part(dyb, w3, saved, first, count, dh_init, *, name):
    h, g, u = saved
    s, d = h.shape
    tn = FFN_TN

    def body(*refs):
        if dh_init is None:
            dy_ref, h_ref, wg_ref, wu_ref, wd_ref, g_ref, u_ref, dh_ref, dw3_ref = refs
        else:
            dy_ref, h_ref, wg_ref, wu_ref, wd_ref, g_ref, u_ref, dh0_ref, dh_ref, dw3_ref = refs

        @pl.when(pl.program_id(0) == 0)
        def _():
            dh_ref[...] = jnp.zeros_like(dh_ref) if dh_init is None else dh0_ref[...]

        dyv = dy_ref[...]
        da = _mm_raw(dyv, wd_ref[...], False, True) * 0.5
        a, vjp = jax.vjp(_silu_mul, g_ref[...].astype(F32), u_ref[...].astype(F32))
        dg, du = vjp(da)
        dg = dg.astype(BF16)
        du = du.astype(BF16)
        dh_ref[...] += _mm_raw(dg, wg_ref[...]) + _mm_raw(du, wu_ref[...])
        hv = h_ref[...]
        dw3_ref[0] = _mm_raw(dg, hv, True, False).astype(BF16)
        dw3_ref[1] = _mm_raw(du, hv, True, False).astype(BF16)
        dw3_ref[2] = (_mm_raw(a, dyv, True, False) * 0.5).astype(BF16)

    full = pl.BlockSpec((s, d), lambda j: (0, 0))
    once = pl.BlockSpec((s, d), lambda j: (0, 0), pipeline_mode=pl.Buffered(1))
    tile = pl.BlockSpec((s, tn), lambda j: (0, first + j))
    in_specs = ([once, once]
                + [pl.BlockSpec((None, tn, d), functools.partial(lambda j, k: (k, first + j, 0), k=k)) for k in range(3)]
                + [tile, tile])
    args = [dyb, h, w3, w3, w3, g, u]
    if dh_init is not None:
        in_specs.append(once)
        args.append(dh_init)
    return pl.pallas_call(
        body, name=name, grid=(count,), in_specs=in_specs,
        out_specs=[full, pl.BlockSpec((3, tn, d), lambda j: (0, j, 0))],
        out_shape=[SDS((s, d), F32), SDS((3, count * tn, d), BF16)],
        compiler_params=_params("arbitrary"),
    )(*args)


def _loss_bwd(y, target, *, tm, name):
    s, d = y.shape

    def body(y_ref, t_ref, dy_ref, dyb_ref, l_ref):
        diff = y_ref[...] - t_ref[...]
        dy_ref[...] = diff * (1.0 / d)
        dyb_ref[...] = (diff * (1.0 / d)).astype(BF16)

        @pl.when(pl.program_id(0) == 0)
        def _():
            l_ref[...] = jnp.zeros_like(l_ref)

        l_ref[...] += 0.5 * jnp.sum(jnp.mean(diff * diff, axis=-1, keepdims=True), axis=0, keepdims=True)

    row = pl.BlockSpec((tm, d), lambda i: (i, 0))
    return pl.pallas_call(
        body, name=name, grid=(s // tm,), in_specs=[row, row],
        out_specs=[row, row, pl.BlockSpec((1, 1), lambda i: (0, 0))],
        out_shape=[SDS((s, d), F32), SDS((s, d), BF16), SDS((1, 1), F32)],
        compiler_params=_params("arbitrary"),
    )(y, target)


def _bucket_table():
    qi = np.arange(BLOCK)[:, None]
    kj = np.arange(2 * BLOCK)[None, :]
    dist = np.maximum(qi + BLOCK - kj, 0)
    max_exact = N_BUCKETS // 2
    d = np.maximum(dist, 1).astype(np.float32)
    large = max_exact + (np.log(d / np.float32(max_exact)) / np.float32(math.log(MAX_DISTANCE / max_exact))
                         * np.float32(N_BUCKETS - max_exact)).astype(np.int32)
    large = np.minimum(large, N_BUCKETS - 1)
    return np.where(dist < max_exact, dist, large).astype(np.int32)


SWA_STACK = SWA_GROUP * BLOCK


def _swa_valid(n):
    qi = lax.broadcasted_iota(jnp.int32, (SWA_STACK, 2 * BLOCK), 0) % BLOCK
    kj = lax.broadcasted_iota(jnp.int32, (SWA_STACK, 2 * BLOCK), 1)
    dist = qi + BLOCK - kj
    return (dist >= 0) & (dist < BLOCK) & ((kj >= BLOCK) | (n > 0))


def _swa_group(q, kb, vb, qg, kg, sink, bias, valid):
    qn = _rms(q, qg)
    kn = _rms(kb, kg)
    s = _mm(qn, kn, False, True) * (HEAD_DIM ** -0.5) + bias
    s = jnp.where(valid, s, -jnp.inf)
    m = lax.stop_gradient(jnp.maximum(jnp.max(s, axis=-1, keepdims=True), sink))
    p = jnp.exp(s - m)
    p = p / (jnp.sum(p, axis=-1, keepdims=True) + jnp.exp(sink - m))
    return _mm(p, vb)


def _swa_bias_table(rb_ref, bucket, bias_s):
    for h in range(SWA_HEADS):
        acc = jnp.zeros((BLOCK, 2 * BLOCK), F32)
        for b in range(N_BUCKETS):
            acc = jnp.where(bucket == b, rb_ref[b, h], acc)
        bias_s[h // SWA_GROUP, (h % SWA_GROUP) * BLOCK:(h % SWA_GROUP + 1) * BLOCK, :] = acc


def _swa_stack(ref, g):
    return jnp.concatenate([ref[:, (g * SWA_GROUP + hh) * HEAD_DIM:(g * SWA_GROUP + hh + 1) * HEAD_DIM]
                            for hh in range(SWA_GROUP)], axis=0)


def _swa_unstack(ref, g, stacked):
    for hh in range(SWA_GROUP):
        h = g * SWA_GROUP + hh
        ref[:, h * HEAD_DIM:(h + 1) * HEAD_DIM] = stacked[hh * BLOCK:(hh + 1) * BLOCK]


def _swa_sink_column(sink_ref, g):
    head = lax.broadcasted_iota(jnp.int32, (SWA_STACK, 1), 0) // BLOCK
    col = jnp.zeros((SWA_STACK, 1), F32)
    for hh in range(SWA_GROUP):
        col = jnp.where(head == hh, sink_ref[g * SWA_GROUP + hh], col)
    return col


def _swa_band(kvp_ref, kvc_ref, g):
    lo = g * HEAD_DIM
    kb = jnp.concatenate([kvp_ref[:, lo:lo + HEAD_DIM], kvc_ref[:, lo:lo + HEAD_DIM]], axis=0)
    lo += SWA_KV_W
    vb = jnp.concatenate([kvp_ref[:, lo:lo + HEAD_DIM], kvc_ref[:, lo:lo + HEAD_DIM]], axis=0)
    return kb, vb


def _swa_specs(order):
    kvc = COL_SKV // (2 * SWA_KV_W)
    return [
        pl.BlockSpec((BLOCK, SWA_Q_W), lambda t: (order(t), 0)),
        pl.BlockSpec((BLOCK, 2 * SWA_KV_W), lambda t: (jnp.maximum(order(t) - 1, 0), kvc)),
        pl.BlockSpec((BLOCK, 2 * SWA_KV_W), lambda t: (order(t), kvc)),
        pl.BlockSpec((1, HEAD_DIM), lambda t: (0, 0)),
        pl.BlockSpec((1, HEAD_DIM), lambda t: (0, 0)),
        pl.BlockSpec(memory_space=pltpu.SMEM),
        pl.BlockSpec(memory_space=pltpu.SMEM),
        pl.BlockSpec((BLOCK, 2 * BLOCK), lambda t: (0, 0)),
    ]


def _swa_fwd(p, qg, kg, sinks, rel_bias, *, name):
    s = p.shape[0]
    nb = s // BLOCK

    def body(q_ref, kvp_ref, kvc_ref, qg_ref, kg_ref, sink_ref, rb_ref, bucket_ref, y_ref, bias_s):
        n = pl.program_id(0)

        @pl.when(n == 0)
        def _():
            _swa_bias_table(rb_ref, bucket_ref[...], bias_s)

        valid = _swa_valid(n)
        for g in range(SWA_KV_HEADS):
            kb, vb = _swa_band(kvp_ref, kvc_ref, g)
            out = _swa_group(_swa_stack(q_ref, g), kb, vb, qg_ref[...], kg_ref[...], _swa_sink_column(sink_ref, g),
                             bias_s[g], valid)
            _swa_unstack(y_ref, g, out)

    return pl.pallas_call(
        body, name=name, grid=(nb,), in_specs=_swa_specs(lambda t: t),
        out_specs=pl.BlockSpec((BLOCK, SWA_Q_W), lambda t: (t, 0)), out_shape=SDS((s, SWA_Q_W), F32),
        scratch_shapes=[pltpu.VMEM((SWA_KV_HEADS, SWA_STACK, 2 * BLOCK), F32)],
        compiler_params=_params("arbitrary"),
    )(p, p, p, qg, kg, sinks, rel_bias, jnp.asarray(_bucket_table()))


def _swa_bwd(p, qg, kg, sinks, rel_bias, dy_all, *, name):
    s = p.shape[0]
    nb = s // BLOCK

    def body(q_ref, kvp_ref, kvc_ref, qg_ref, kg_ref, sink_ref, rb_ref, bucket_ref, dy_ref,
             dq_ref, dkv_ref, dqg_ref, dkg_ref, dsink_ref, drb_ref, bias_s, dbias_s, carry_s):
        t = pl.program_id(0)
        n = nb - 1 - t

        @pl.when(t == 0)
        def _():
            _swa_bias_table(rb_ref, bucket_ref[...], bias_s)
            dbias_s[...] = jnp.zeros_like(dbias_s)
            carry_s[...] = jnp.zeros_like(carry_s)
            dqg_ref[...] = jnp.zeros_like(dqg_ref)
            dkg_ref[...] = jnp.zeros_like(dkg_ref)
            dsink_ref[...] = jnp.zeros_like(dsink_ref)
            drb_ref[...] = jnp.zeros_like(drb_ref)

        valid = _swa_valid(n)
        lane = lax.broadcasted_iota(jnp.int32, (1, BLOCK), 1)
        dqg = jnp.zeros((1, HEAD_DIM), F32)
        dkg = jnp.zeros((1, HEAD_DIM), F32)
        dsink_vec = jnp.zeros((1, BLOCK), F32)
        for g in range(SWA_KV_HEADS):
            kb, vb = _swa_band(kvp_ref, kvc_ref, g)
            _, vjp = jax.vjp(functools.partial(_swa_group, valid=valid), _swa_stack(q_ref, g), kb, vb, qg_ref[...],
                             kg_ref[...], _swa_sink_column(sink_ref, g), bias_s[g])
            dq, dkb, dvb, dqg_g, dkg_g, dsink_col, dbias = vjp(_swa_stack(dy_ref, g))
            _swa_unstack(dq_ref, g, dq)
            dqg += dqg_g
            dkg += dkg_g
            dbias_s[g] += dbias
            for hh in range(SWA_GROUP):
                dsink_h = jnp.sum(dsink_col[hh * BLOCK:(hh + 1) * BLOCK], axis=0, keepdims=True)
                dsink_vec += jnp.where(lane == g * SWA_GROUP + hh, dsink_h, 0.0)
            lo = g * HEAD_DIM
            dkv_ref[:, lo:lo + HEAD_DIM] = dkb[BLOCK:] + carry_s[g]
            carry_s[g] = dkb[:BLOCK]
            lo += SWA_KV_W
            dkv_ref[:, lo:lo + HEAD_DIM] = dvb[BLOCK:] + carry_s[SWA_KV_HEADS + g]
            carry_s[SWA_KV_HEADS + g] = dvb[:BLOCK]
        dqg_ref[...] += dqg
        dkg_ref[...] += dkg
        dsink_ref[...] += dsink_vec

        @pl.when(t == nb - 1)
        def _():
            bucket = bucket_ref[...]
            row = lax.broadcasted_iota(jnp.int32, (N_BUCKETS, BLOCK), 0)
            col = lax.broadcasted_iota(jnp.int32, (N_BUCKETS, BLOCK), 1)
            acc = jnp.zeros((N_BUCKETS, BLOCK), F32)
            for h in range(SWA_HEADS):
                dbias = dbias_s[h // SWA_GROUP, (h % SWA_GROUP) * BLOCK:(h % SWA_GROUP + 1) * BLOCK, :]
                for b in range(N_BUCKETS):
                    part = jnp.sum(jnp.where(bucket == b, dbias, 0.0), axis=1, keepdims=True)
                    val = jnp.sum(part, axis=0, keepdims=True)
                    acc = acc + jnp.where((row == b) & (col == h), val, 0.0)
            drb_ref[...] = acc

    order = lambda t: nb - 1 - t
    vec = pl.BlockSpec((1, HEAD_DIM), lambda t: (0, 0))
    return pl.pallas_call(
        body, name=name, grid=(nb,),
        in_specs=_swa_specs(order) + [pl.BlockSpec((BLOCK, SWA_Q_W), lambda t: (order(t), 0))],
        out_specs=[pl.BlockSpec((BLOCK, SWA_Q_W), lambda t: (order(t), 0)),
                   pl.BlockSpec((BLOCK, 2 * SWA_KV_W), lambda t: (order(t), 0)),
                   vec, vec, pl.BlockSpec((1, BLOCK), lambda t: (0, 0)),
                   pl.BlockSpec((N_BUCKETS, BLOCK), lambda t: (0, 0))],
        out_shape=[SDS((s, SWA_Q_W), F32), SDS((s, 2 * SWA_KV_W), F32), SDS((1, HEAD_DIM), F32),
                   SDS((1, HEAD_DIM), F32), SDS((1, BLOCK), F32), SDS((N_BUCKETS, BLOCK), F32)],
        scratch_shapes=[pltpu.VMEM((SWA_KV_HEADS, SWA_STACK, 2 * BLOCK), F32),
                        pltpu.VMEM((SWA_KV_HEADS, SWA_STACK, 2 * BLOCK), F32),
                        pltpu.VMEM((2 * SWA_KV_HEADS, BLOCK, HEAD_DIM), F32)],
        compiler_params=_params("arbitrary"),
    )(p, p, p, qg, kg, sinks, rel_bias, jnp.asarray(_bucket_table()), dy_all)


def _mem_head(q, k, v, qg, kg):
    qn = _rms(q, qg)
    kn = _rms(k, kg)
    s = _mm(qn, kn, False, True) * (HEAD_DIM ** -0.5)
    m = lax.stop_gradient(jnp.max(s, axis=-1, keepdims=True))
    e = jnp.exp(s - m)
    return _mm(e / jnp.sum(e, axis=-1, keepdims=True), v)


def _mem_fwd(p, kv, qg, kg, *, tq, name):
    s = p.shape[0]
    m = kv.shape[0]

    def body(q_ref, kv_ref, qg_ref, kg_ref, y_ref):
        for h in range(MEM_HEADS):
            cols = slice(h * HEAD_DIM, (h + 1) * HEAD_DIM)
            vcols = slice(MEM_Q_W + h * HEAD_DIM, MEM_Q_W + (h + 1) * HEAD_DIM)
            y_ref[:, cols] = _mem_head(q_ref[:, cols], kv_ref[:, cols], kv_ref[:, vcols], qg_ref[...], kg_ref[...])

    vec = pl.BlockSpec((1, HEAD_DIM), lambda t: (0, 0))
    return pl.pallas_call(
        body, name=name, grid=(s // tq,),
        in_specs=[pl.BlockSpec((tq, MEM_Q_W), lambda t: (t, COL_MQ // MEM_Q_W)),
                  pl.BlockSpec((m, 2 * MEM_Q_W), lambda t: (0, 0)), vec, vec],
        out_specs=pl.BlockSpec((tq, MEM_Q_W), lambda t: (t, 0)), out_shape=SDS((s, MEM_Q_W), F32),
        compiler_params=_params("parallel"),
    )(p, kv, qg, kg)


def _mem_bwd(p, kv, qg, kg, dy_all, *, tq, name):
    s = p.shape[0]
    m = kv.shape[0]

    def body(q_ref, kv_ref, qg_ref, kg_ref, dy_ref, dq_ref, dkv_ref, dqg_ref, dkg_ref):
        @pl.when(pl.program_id(0) == 0)
        def _():
            dkv_ref[...] = jnp.zeros_like(dkv_ref)
            dqg_ref[...] = jnp.zeros_like(dqg_ref)
            dkg_ref[...] = jnp.zeros_like(dkg_ref)

        dqg = jnp.zeros((1, HEAD_DIM), F32)
        dkg = jnp.zeros((1, HEAD_DIM), F32)
        for h in range(MEM_HEADS):
            cols = slice(h * HEAD_DIM, (h + 1) * HEAD_DIM)
            vcols = slice(MEM_Q_W + h * HEAD_DIM, MEM_Q_W + (h + 1) * HEAD_DIM)
            _, vjp = jax.vjp(_mem_head, q_ref[:, cols], kv_ref[:, cols], kv_ref[:, vcols], qg_ref[...], kg_ref[...])
            dq, dk, dv, dqg_h, dkg_h = vjp(dy_ref[:, cols])
            dq_ref[:, cols] = dq
            dkv_ref[:, cols] += dk
            dkv_ref[:, vcols] += dv
            dqg += dqg_h
            dkg += dkg_h
        dqg_ref[...] += dqg
        dkg_ref[...] += dkg

    vec = pl.BlockSpec((1, HEAD_DIM), lambda t: (0, 0))
    full = pl.BlockSpec((m, 2 * MEM_Q_W), lambda t: (0, 0))
    dy_col = (SWA_Q_W + GLA_V_W) // MEM_Q_W
    return pl.pallas_call(
        body, name=name, grid=(s // tq,),
        in_specs=[pl.BlockSpec((tq, MEM_Q_W), lambda t: (t, COL_MQ // MEM_Q_W)), full, vec, vec,
                  pl.BlockSpec((tq, MEM_Q_W), lambda t: (t, dy_col))],
        out_specs=[pl.BlockSpec((tq, MEM_Q_W), lambda t: (t, 0)), full, vec, vec],
        out_shape=[SDS((s, MEM_Q_W), F32), SDS((m, 2 * MEM_Q_W), F32), SDS((1, HEAD_DIM), F32), SDS((1, HEAD_DIM), F32)],
        compiler_params=_params("arbitrary"),
    )(p, kv, qg, kg, dy_all)


GLA_ROWS = 256


GLA_GROUP = 4


def _gla_consts():
    c, h, r = GLA_CHUNK, GLA_HEADS, GLA_GROUP * GLA_CHUNK
    i2 = lax.broadcasted_iota(jnp.int32, (c, c), 0)
    j2 = lax.broadcasted_iota(jnp.int32, (c, c), 1)
    slab_q = lax.broadcasted_iota(jnp.int32, (h, r, GLA_QK_W), 0)
    lane_q = lax.broadcasted_iota(jnp.int32, (h, r, GLA_QK_W), 2)
    row_a = lax.broadcasted_iota(jnp.int32, (h * r, r), 0) % r
    col_a = lax.broadcasted_iota(jnp.int32, (h * r, r), 1)
    slab_o = lax.broadcasted_iota(jnp.int32, (h, r, GLA_V_W), 0)
    lane_o = lax.broadcasted_iota(jnp.int32, (h, r, GLA_V_W), 2)
    row_s = lax.broadcasted_iota(jnp.int32, (GLA_V_W, GLA_QK_W), 0)
    col_s = lax.broadcasted_iota(jnp.int32, (GLA_V_W, GLA_QK_W), 1)
    return dict(
        ltri=(j2 <= i2).astype(F32),
        m_q=(slab_q == lane_q // GLA_DK).astype(F32),
        causal=(col_a <= row_a) & (col_a // c == row_a // c),
        m_o=(slab_o == lane_o // GLA_DV).astype(F32),
        m_s=(row_s // GLA_DV == col_s // GLA_DK).astype(F32),
    )


def _gla_step(q, k, v, z, bg, st, c):
    h = GLA_HEADS
    kt, qt, qe, decay = [], [], [], []
    for qc, kc, zc in zip(q, k, z):
        la = _log_sigmoid(zc + bg) * (1.0 / GLA_TAU)
        b = _mmf(c["ltri"], la)
        bl = jnp.sum(la, axis=0, keepdims=True)
        qs = qc * (GLA_DK ** -0.5)
        kt.append(kc * jnp.exp(bl - b))
        qt.append(qs * jnp.exp(b - bl))
        qe.append(qs * jnp.exp(b))
        decay.append(jnp.exp(bl))
    o_intra = []
    rows = GLA_GROUP * GLA_CHUNK
    for lo in range(0, len(q), GLA_GROUP):
        qt_all, kt_all, v_all = (jnp.concatenate(parts[lo:lo + GLA_GROUP], axis=0) for parts in (qt, kt, v))
        q_stack = (jnp.broadcast_to(qt_all[None], (h, rows, GLA_QK_W)) * c["m_q"]).reshape(h * rows, GLA_QK_W)
        a = jnp.where(c["causal"], _mm3(q_stack, kt_all, False, True), 0.0)
        o_stack = _mm(a, v_all)
        o_intra.append(jnp.sum(o_stack.reshape(h, rows, GLA_V_W) * c["m_o"], axis=0))
    o_intra = jnp.concatenate(o_intra, axis=0)
    o_inter = []
    for qec, ktc, vc, dc in zip(qe, kt, v, decay):
        o_inter.append(_mm(qec, st, False, True))
        st = st * dc + _mm(vc, ktc, True, False) * c["m_s"]
    return o_intra + jnp.concatenate(o_inter, axis=0), st


def _gla_post(o, gg, gain, g64):
    ms = _mmf(o * o, g64) * (1.0 / GLA_DV)
    return o * lax.rsqrt(ms + EPS) * gain * jax.nn.silu(gg)


def _gla_g64():
    r = lax.broadcasted_iota(jnp.int32, (GLA_V_W, GLA_V_W), 0)
    c = lax.broadcasted_iota(jnp.int32, (GLA_V_W, GLA_V_W), 1)
    return (r // GLA_DV == c // GLA_DV).astype(F32)


def _gla_in_specs(order):
    r = GLA_ROWS
    return [
        pl.BlockSpec((r, GLA_QK_W), lambda t: (order(t), COL_GQ // GLA_QK_W)),
        pl.BlockSpec((r, GLA_QK_W), lambda t: (order(t), COL_GK // GLA_QK_W)),
        pl.BlockSpec((r, GLA_V_W), lambda t: (order(t), COL_GV // GLA_V_W)),
        pl.BlockSpec((r, GLA_V_W), lambda t: (order(t), COL_GG // GLA_V_W)),
        pl.BlockSpec((r, GLA_QK_W), lambda t: (order(t), 0)),
        pl.BlockSpec((1, GLA_QK_W), lambda t: (0, 0)),
        pl.BlockSpec((1, GLA_V_W), lambda t: (0, 0)),
    ]


def _gla_pieces(q_ref, k_ref, v_ref, z_ref, cps):
    chunk = lambda ref: [ref[ci * GLA_CHUNK:(ci + 1) * GLA_CHUNK, :] for ci in range(cps)]
    return chunk(q_ref), chunk(k_ref), chunk(v_ref), chunk(z_ref)


def _gla_fwd(p, z, bg, gain, *, name):
    s = p.shape[0]
    r = GLA_ROWS
    cps = r // GLA_CHUNK

    def body(q_ref, k_ref, v_ref, gg_ref, z_ref, bg_ref, gain_ref, y_ref, oraw_ref, stsave_ref, st_s):
        @pl.when(pl.program_id(0) == 0)
        def _():
            st_s[...] = jnp.zeros_like(st_s)

        st = st_s[...]
        stsave_ref[0] = st
        o, st = _gla_step(*_gla_pieces(q_ref, k_ref, v_ref, z_ref, cps), bg_ref[...], st, _gla_consts())
        oraw_ref[...] = o
        st_s[...] = st
        y_ref[...] = _gla_post(o, gg_ref[...], gain_ref[...], _gla_g64())

    rowv = pl.BlockSpec((r, GLA_V_W), lambda t: (t, 0))
    return pl.pallas_call(
        body, name=name, grid=(s // r,), in_specs=_gla_in_specs(lambda t: t),
        out_specs=[rowv, rowv, pl.BlockSpec((1, GLA_V_W, GLA_QK_W), lambda t: (t, 0, 0))],
        out_shape=[SDS((s, GLA_V_W), F32), SDS((s, GLA_V_W), F32), SDS((s // r, GLA_V_W, GLA_QK_W), F32)],
        scratch_shapes=[pltpu.VMEM((GLA_V_W, GLA_QK_W), F32)],
        compiler_params=_params("arbitrary"),
    )(p, p, p, p, z, bg, gain)


def _gla_bwd(p, z, bg, gain, oraw, stsave, dy_all, *, name):
    s = p.shape[0]
    r = GLA_ROWS
    cps = r // GLA_CHUNK
    nsteps = s // r
    w_qkvg = 2 * GLA_QK_W + 2 * GLA_V_W

    def body(q_ref, k_ref, v_ref, gg_ref, z_ref, bg_ref, gain_ref, oraw_ref, stsave_ref, dy_ref,
             dqkvg_ref, dz_ref, dbg_ref, dgain_ref, dst_s):
        @pl.when(pl.program_id(0) == 0)
        def _():
            dst_s[...] = jnp.zeros_like(dst_s)
            dbg_ref[...] = jnp.zeros_like(dbg_ref)
            dgain_ref[...] = jnp.zeros_like(dgain_ref)

        _, vjp = jax.vjp(functools.partial(_gla_post, g64=_gla_g64()), oraw_ref[...], gg_ref[...], gain_ref[...])
        do, dgg, dgain = vjp(dy_ref[...])
        dqkvg_ref[:, 2 * GLA_QK_W + GLA_V_W:] = dgg
        dgain_ref[...] += dgain
        _, vjp = jax.vjp(functools.partial(_gla_step, c=_gla_consts()), *_gla_pieces(q_ref, k_ref, v_ref, z_ref, cps),
                         bg_ref[...], stsave_ref[0])
        dq, dk, dv, dz, dbg, dst = vjp((do, dst_s[...]))
        for ci in range(cps):
            rows = slice(ci * GLA_CHUNK, (ci + 1) * GLA_CHUNK)
            dqkvg_ref[rows, 0:GLA_QK_W] = dq[ci]
            dqkvg_ref[rows, GLA_QK_W:2 * GLA_QK_W] = dk[ci]
            dqkvg_ref[rows, 2 * GLA_QK_W:2 * GLA_QK_W + GLA_V_W] = dv[ci]
            dz_ref[rows, :] = dz[ci]
        dst_s[...] = dst
        dbg_ref[...] += dbg

    order = lambda t: nsteps - 1 - t
    rowv = pl.BlockSpec((r, GLA_V_W), lambda t: (order(t), 0))
    return pl.pallas_call(
        body, name=name, grid=(nsteps,),
        in_specs=_gla_in_specs(order) + [
            rowv, pl.BlockSpec((1, GLA_V_W, GLA_QK_W), lambda t: (order(t), 0, 0)),
            pl.BlockSpec((r, GLA_V_W), lambda t: (order(t), SWA_Q_W // GLA_V_W))],
        out_specs=[pl.BlockSpec((r, w_qkvg), lambda t: (order(t), 0)), pl.BlockSpec((r, GLA_QK_W), lambda t: (order(t), 0)),
                   pl.BlockSpec((1, GLA_QK_W), lambda t: (0, 0)), pl.BlockSpec((1, GLA_V_W), lambda t: (0, 0))],
        out_shape=[SDS((s, w_qkvg), F32), SDS((s, GLA_QK_W), F32), SDS((1, GLA_QK_W), F32), SDS((1, GLA_V_W), F32)],
        scratch_shapes=[pltpu.VMEM((GLA_V_W, GLA_QK_W), F32)],
        compiler_params=_params("arbitrary"),
    )(p, p, p, p, z, bg, gain, oraw, stsave, dy_all)


def _local_step(x, mem, target, small, big, on_grads):
    g1, gmix, gmem, g2, sqg, skg, sinks, rel_bias, wgu, bg, gla_gain, mqg, mkg = small
    w3_1, win_p, wkv, wout, gather_ffn2 = big
    wgu_pad = jnp.zeros((GLA_QK_W, GLA_QK_W), BF16).at[:GLA_RANK].set(wgu.astype(BF16))
    gain256 = jnp.tile(gla_gain, (1, GLA_HEADS))

    x1, saved1 = _ffn_fwd(x, g1, w3_1, "ffn1")
    w3_2 = gather_ffn2(x1)
    h = _rms_fwd(x1, gmix, tm=512, name="mix_rms")
    p = _matmul(h, win_p, tm=1024, tn=IN_W_PAD, name="mix_in")
    hm = _rms_fwd(mem, gmem, tm=256, name="mem_rms")
    kv = _matmul(hm, wkv, tm=256, tn=512, name="mem_kv")
    p_glr = p[:, COL_GLR:]
    z = _matmul(p_glr, wgu_pad, tm=1024, tn=GLA_QK_W, name="gla_gate")
    y_swa = _swa_fwd(p, sqg, skg, sinks, rel_bias, name="swa_fwd")
    y_gla, oraw, stsave = _gla_fwd(p, z, bg, gain256, name="gla_fwd")
    y_mem = _mem_fwd(p, kv, mqg, mkg, tq=512, name="mem_fwd")
    x2 = _matmul([y_swa, y_gla, y_mem], wout, b_blocks=[0, 2, 3], tm=1024, tn=1024, res=x1, name="mix_out")
    x3, saved2 = _ffn_fwd(x2, g2, w3_2, "ffn2")

    dy, dyb, loss = _loss_bwd(x3, target, tm=512, name="loss")
    tiles = w3_2.shape[1] // FFN_TN
    dh2, dw3_2 = _ffn_bwd_part(dyb, w3_2, saved2, 0, tiles, None, name="ffn2_bwd")
    dx2, dx2b, dg2 = _rms_bwd(x2, g2, dh2, dy, tm=512, name="ffn2_drms")
    dx2b = on_grads("ffn2", [dw3_2.reshape(3, 2, -1, dw3_2.shape[-1])], dx2b)
    dy_all = _matmul(dx2b, wout, tb=True, tm=1024, tn=1024, name="mix_dy")
    dwout = _matmul(jnp.concatenate([y_swa, y_gla, y_mem], axis=1), dx2b, ta=True, tm=512, tn=1024, out_dtype=BF16,
                    name="mix_dw_out")
    dq_swa, dkv_swa, dsqg, dskg, dsink, drb = _swa_bwd(p, sqg, skg, sinks, rel_bias, dy_all, name="swa_bwd")
    dqkvg, dz, dbg, dgain256 = _gla_bwd(p, z, bg, gain256, oraw, stsave, dy_all, name="gla_bwd")
    dmq, dkv_mem, dmqg, dmkg = _mem_bwd(p, kv, mqg, mkg, dy_all, tq=512, name="mem_bwd")
    dglr = _matmul(dz, wgu_pad, tb=True, tm=1024, tn=GLA_QK_W, name="gla_gate_dx")
    dwgu_pad = _matmul(p_glr, dz, ta=True, tm=GLA_QK_W, tn=GLA_QK_W, name="gla_gate_dw")
    dp = jnp.concatenate([dq_swa, dkv_swa, dqkvg, dmq, dglr], axis=1)
    dh = _matmul(dp, win_p, tb=True, tm=1024, tn=1024, name="mix_dh")
    dwin_p = _matmul(h, dp, ta=True, tm=1024, tn=640, out_dtype=BF16, name="mix_dw_in")
    dx1, dx1b, dgmix = _rms_bwd(x1, gmix, dh, dx2, tm=512, name="mix_drms")
    dwkv = _matmul(hm, dkv_mem, ta=True, tm=512, tn=512, out_dtype=BF16, name="mem_dw_kv")
    dx1b = on_grads("mix", (dwin_p, dwkv, dwout), dx1b)
    dhm = _matmul(dkv_mem, wkv, tb=True, tm=256, tn=512, name="mem_dh")
    _, _, dgmem = _rms_bwd(mem, gmem, dhm, None, tm=256, name="mem_drms")
    dh1, dw3_1a = _ffn_bwd_part(dx1b, w3_1, saved1, 0, tiles // 2, None, name="ffn1_bwd_a")
    dgla_gain = dgain256.reshape(GLA_HEADS, GLA_DV).sum(axis=0, keepdims=True)
    dsmall = [dgmix, dgmem, dg2, dsqg, dskg, dsink[0, :SWA_HEADS], drb[:, :SWA_HEADS], dwgu_pad[:GLA_RANK], dbg,
              dgla_gain, dmqg, dmkg, loss]
    dh1, dgmem, dwgu_pad = on_grads("ffn1a", [dw3_1a[:, None]], (dh1, dgmem, dwgu_pad), small=dsmall)
    dh1, dw3_1b = _ffn_bwd_part(dx1b, w3_1, saved1, tiles // 2, tiles // 2, dh1, name="ffn1_bwd_b")
    dx, _, dg1 = _rms_bwd(x, g1, dh1, dx1, tm=512, name="ffn1_drms")
    on_grads("ffn1b", [dw3_1b[:, None]], None, small=[dg1])
    return dx


def _mesh_place():
    x, y, c = lax.axis_index("x"), lax.axis_index("y"), lax.axis_index("c")
    other_chips = [(1 - x, y), (x, 1 - y), (1 - x, 1 - y)]
    return x, y, c, other_chips


def _handshake(peers):
    barrier = pltpu.get_barrier_semaphore()
    for peer in peers:
        pl.semaphore_signal(barrier, inc=1, device_id=peer, device_id_type=MESH)
    pl.semaphore_wait(barrier, len(peers))


def _sequencer_call(body, operands, out_shapes, sems, *, name, collective_id):
    return pl.kernel(
        body, name=name, out_type=out_shapes, mesh=plsc.ScalarSubcoreMesh(axis_name="sequencer", num_cores=1),
        scratch_types=sems, compiler_params=pltpu.CompilerParams(collective_id=collective_id),
    )(*operands)


def _window(ref, kind, slot, shape):
    if kind == "row":
        rows = pl.ds(pl.multiple_of(slot * shape[-2], 8), shape[-2])
        return ref.at[(slice(None),) * (len(shape) - 2) + (rows,)]
    return ref.at[slot]


def _gathered(shape, kind):
    if kind == "row":
        return tuple(shape[:-2]) + (N_DEV * shape[-2], shape[-1])
    return (N_DEV,) + tuple(shape)


def _all_gather(shards, kinds, *, name, collective_id):
    nt = len(shards)

    def body(*refs):
        x_refs, o_refs = refs[:nt], refs[nt:2 * nt]
        send_sems, recv_sems, local_sems = refs[2 * nt:]
        x, y, c, chips = _mesh_place()
        me, sibling = (x, y, c), (x, y, 1 - c)
        _handshake([sibling] + [(*chip, c) for chip in chips])

        def copy(k, t, block, to, from_shard=False):
            bx, by, bc = block
            rows = _window(o_refs[t], kinds[t], 4 * bx + 2 * by + bc, shards[t].shape)
            return pltpu.make_async_remote_copy(
                src_ref=x_refs[t] if from_shard else rows, dst_ref=rows,
                send_sem=send_sems.at[k, t], recv_sem=recv_sems.at[k, t], device_id=to, device_id_type=MESH)

        mine = [pltpu.make_async_copy(x_refs[t], _window(o_refs[t], kinds[t], 4 * x + 2 * y + c, shards[t].shape),
                                      local_sems.at[t]) for t in range(nt)]
        for cp in mine:
            cp.start()
        first = [copy(0, t, me, sibling, True) for t in range(nt)]
        first += [copy(1 + j, t, me, (*chip, c), True) for j, chip in enumerate(chips) for t in range(nt)]
        for cp in first:
            cp.start()
        passed = []
        for j, chip in enumerate(chips):
            for t in range(nt):
                copy(1 + j, t, (*chip, c), me).wait_recv()
                fwd = copy(4 + j, t, (*chip, c), sibling)
                fwd.start()
                passed.append(fwd)
        for t in range(nt):
            copy(0, t, sibling, me).wait_recv()
        for j, chip in enumerate(chips):
            for t in range(nt):
                copy(4 + j, t, (*chip, 1 - c), me).wait_recv()
        for cp in first + passed:
            cp.wait_send()
        for cp in mine:
            cp.wait()

    return _sequencer_call(
        body, shards, [SDS(_gathered(s.shape, k), s.dtype) for s, k in zip(shards, kinds)],
        [pltpu.SemaphoreType.DMA((7, nt)), pltpu.SemaphoreType.DMA((7, nt)), pltpu.SemaphoreType.DMA((nt,))],
        name=name, collective_id=collective_id)


def _part_shape(shape, kind):
    if kind == "row":
        return tuple(shape[:-2]) + (shape[-2] // N_DEV, shape[-1])
    return tuple(shape[2:])


def _pair_exchange(grads, kinds, *, name, collective_id):
    nt = len(grads)
    part = [_part_shape(g.shape, k) for g, k in zip(grads, kinds)]

    def body(*refs):
        g_refs, o_refs = refs[:nt], refs[nt:2 * nt]
        send_sems, recv_sems = refs[2 * nt:]
        x, y, c, _ = _mesh_place()
        _handshake([(x, y, 1 - c)])
        copies = []
        for t in range(nt):
            for xy in range(4):
                src = g_refs[t].at[1 - c, xy] if kinds[t] == "stack" else _window(g_refs[t], kinds[t], 2 * xy + 1 - c, part[t])
                copies.append(pltpu.make_async_remote_copy(
                    src_ref=src, dst_ref=o_refs[t].at[xy], send_sem=send_sems.at[xy, t], recv_sem=recv_sems.at[xy, t],
                    device_id=(x, y, 1 - c), device_id_type=MESH))
        for cp in copies:
            cp.start()
        for cp in copies:
            cp.wait()

    return _sequencer_call(
        body, grads, [SDS((4,) + p, g.dtype) for p, g in zip(part, grads)],
        [pltpu.SemaphoreType.DMA((4, nt)), pltpu.SemaphoreType.DMA((4, nt))], name=name, collective_id=collective_id)


def _chip_exchange(parts, small, *, name, collective_id):
    nt = len(parts)
    if small is None:
        def body_plain(*refs):
            s_refs, o_refs = refs[:nt], refs[nt:2 * nt]
            send_sems, recv_sems = refs[2 * nt:]
            x, y, c, chips = _mesh_place()
            _handshake([(*chip, c) for chip in chips])
            copies = [pltpu.make_async_remote_copy(
                src_ref=s_refs[t].at[2 * chip[0] + chip[1]], dst_ref=o_refs[t].at[j],
                send_sem=send_sems.at[j, t], recv_sem=recv_sems.at[j, t], device_id=(*chip, c), device_id_type=MESH)
                for j, chip in enumerate(chips) for t in range(nt)]
            for cp in copies:
                cp.start()
            for cp in copies:
                cp.wait()

        return _sequencer_call(
            body_plain, parts, [SDS((3,) + s.shape[1:], s.dtype) for s in parts],
            [pltpu.SemaphoreType.DMA((3, nt)), pltpu.SemaphoreType.DMA((3, nt))], name=name, collective_id=collective_id)

    def body(*refs):
        s_refs, small_ref = refs[:nt], refs[nt]
        o_refs, small_all = refs[nt + 1:2 * nt + 1], refs[2 * nt + 1]
        send_sems, recv_sems, small_send, small_recv, local_sem = refs[2 * nt + 2:]
        x, y, c, chips = _mesh_place()
        _handshake([(px, py, pc) for px in (x, 1 - x) for py in (y, 1 - y) for pc in (c, 1 - c)][1:])

        def copy(j, t, chip):
            return pltpu.make_async_remote_copy(
                src_ref=s_refs[t].at[2 * chip[0] + chip[1]], dst_ref=o_refs[t].at[j],
                send_sem=send_sems.at[j, t], recv_sem=recv_sems.at[j, t], device_id=(*chip, c), device_id_type=MESH)

        flips = [(fx, fy, fc) for fx in (0, 1) for fy in (0, 1) for fc in (0, 1)][1:]

        def small_copy(k):
            fx, fy, fc = flips[k]
            to = (x ^ fx if fx else x, y ^ fy if fy else y, c ^ fc if fc else c)
            rows = small_all.at[4 * x + 2 * y + c]
            return pltpu.make_async_remote_copy(
                src_ref=small_ref, dst_ref=rows, send_sem=small_send.at[k], recv_sem=small_recv.at[k],
                device_id=to, device_id_type=MESH)

        own = pltpu.make_async_copy(small_ref, small_all.at[4 * x + 2 * y + c], local_sem)
        own.start()
        copies = [copy(j, t, chip) for j, chip in enumerate(chips) for t in range(nt)]
        smalls = [small_copy(k) for k in range(7)]
        for cp in smalls + copies:
            cp.start()
        for cp in smalls + copies:
            cp.wait()
        own.wait()

    return _sequencer_call(
        body, list(parts) + [small],
        [SDS((3,) + s.shape[1:], s.dtype) for s in parts] + [SDS((N_DEV,) + small.shape, small.dtype)],
        [pltpu.SemaphoreType.DMA((3, nt)), pltpu.SemaphoreType.DMA((3, nt)),
         pltpu.SemaphoreType.DMA((7,)), pltpu.SemaphoreType.DMA((7,)), pltpu.SemaphoreType.DMA],
        name=name, collective_id=collective_id)


def _pair_sum(grad, theirs, kind, c, *, name):
    if kind == "row":
        r, l = theirs.shape[-2:]
        n = theirs.size // (4 * r * l)
        grad, theirs = grad.reshape(n, N_DEV * r, l), theirs.reshape(4, n, r, l)
        mine_spec = pl.BlockSpec((n, r, l), lambda xy, c_ref: (0, 2 * xy + c_ref[0], 0))
    else:
        r, l = theirs.shape[-2:]
        n = theirs.size // (4 * r * l)
        theirs = theirs.reshape(4, n, r, l)
        grad = grad.reshape(2, 4, n, r, l)
        mine_spec = pl.BlockSpec((None, None, n, r, l), lambda xy, c_ref: (c_ref[0], xy, 0, 0, 0))

    def body(c_ref, a_ref, b_ref, o_ref):
        o_ref[...] = (a_ref[...].astype(F32) + b_ref[...].astype(F32)).astype(BF16)

    part = pl.BlockSpec((None, n, r, l), lambda xy, c_ref: (xy, 0, 0, 0))
    return pl.pallas_call(
        body, name=name,
        grid_spec=pltpu.PrefetchScalarGridSpec(num_scalar_prefetch=1, grid=(4,), in_specs=[mine_spec, part], out_specs=part),
        out_shape=SDS((4, n, r, l), BF16), compiler_params=_params("parallel"),
    )(c, grad, theirs)


def _adamw(w, g, m, v):
    m = ADAM_B1 * m + (1.0 - ADAM_B1) * g
    v = ADAM_B2 * v + (1.0 - ADAM_B2) * jnp.square(g)
    m_hat = m / (1.0 - ADAM_B1 ** ADAM_STEP)
    v_hat = v / (1.0 - ADAM_B2 ** ADAM_STEP)
    delta = -ADAM_LR * (m_hat / (jnp.sqrt(v_hat) + ADAM_EPS) + ADAM_WD * w)
    return delta, m, v


def _adam_big(owns, others, mat, xy, w, m, v, *, tr, name):
    nw, r, l = w.shape
    lp = owns[0].shape[-1]
    nq = len(owns)
    assert nq in (1, nw)

    def body(xy_ref, *refs):
        own_refs, oth_refs = refs[:nq], refs[nq:2 * nq]
        w_ref, m_ref, v_ref, g_out, d_out, m_out, v_out = refs[2 * nq:]
        g = None
        for q in range(nq):
            gq = own_refs[q][0, 0].astype(F32)
            for j in range(3):
                gq = gq + oth_refs[q][j, 0].astype(F32)
            g = gq if g is None else jnp.where(pl.program_id(0) == q, gq, g)
        g = g[:, :l]
        delta, m_new, v_new = _adamw(w_ref[0], g, m_ref[0], v_ref[0])
        g_out[0] = g
        d_out[0] = delta
        m_out[0] = m_new
        v_out[0] = v_new

    def at(p):
        return mat * nw + p if nq == 1 else mat

    blk = pl.BlockSpec((1, tr, l), lambda p, i, xy_ref: (p, i, 0))
    return pl.pallas_call(
        body, name=name,
        grid_spec=pltpu.PrefetchScalarGridSpec(
            num_scalar_prefetch=1, grid=(nw, r // tr),
            in_specs=[pl.BlockSpec((1, 1, tr, lp), lambda p, i, xy_ref: (xy_ref[0], at(p), i, 0))] * nq
            + [pl.BlockSpec((3, 1, tr, lp), lambda p, i, xy_ref: (0, at(p), i, 0))] * nq + [blk, blk, blk],
            out_specs=[blk, blk, blk, blk]),
        out_shape=[SDS(w.shape, F32)] * 4, compiler_params=_params("parallel", "parallel"),
    )(xy, *owns, *others, w, m, v)


def _adam_small(g_all, w, m, v, *, name):
    def body(g_ref, w_ref, m_ref, v_ref, g_out, d_out, m_out, v_out):
        g = g_ref[0]
        for k in range(1, N_DEV):
            g = g + g_ref[k]
        delta, m_new, v_new = _adamw(w_ref[...], g, m_ref[...], v_ref[...])
        g_out[...] = g
        d_out[...] = delta
        m_out[...] = m_new
        v_out[...] = v_new

    return pl.pallas_call(body, name=name, out_shape=[SDS(w.shape, F32)] * 4)(g_all, w, m, v)


def _pack_small(parts):
    flat = jnp.concatenate([a.reshape(-1) for a in parts])
    rows = -(-flat.shape[0] // 1024) * 8
    return jnp.pad(flat, (0, rows * 128 - flat.shape[0])).reshape(rows, 128)


def _unpack_small(packed, shapes):
    flat = packed.reshape(-1)
    out, at = [], 0
    for s in shapes:
        n = math.prod(s)
        out.append(flat[at:at + n].reshape(s))
        at += n
    return out


def kernel(x, mem, ffn1_norm, ffn1_w_gate, ffn1_w_up, ffn1_w_down, mix_norm, mem_norm, w_in, w_mem_kv, swa_q_norm, swa_k_norm, swa_sinks, rel_bias, gla_w_gate_up, gla_b_gate, gla_out_norm, mem_q_norm, mem_k_norm, w_out, ffn2_norm, ffn2_w_gate, ffn2_w_up, ffn2_w_down, loss_target, m_ffn1_norm, m_ffn1_w_gate, m_ffn1_w_up, m_ffn1_w_down, m_mix_norm, m_mem_norm, m_w_in, m_w_mem_kv, m_swa_q_norm, m_swa_k_norm, m_swa_sinks, m_rel_bias, m_gla_w_gate_up, m_gla_b_gate, m_gla_out_norm, m_mem_q_norm, m_mem_k_norm, m_w_out, m_ffn2_norm, m_ffn2_w_gate, m_ffn2_w_up, m_ffn2_w_down, v_ffn1_norm, v_ffn1_w_gate, v_ffn1_w_up, v_ffn1_w_down, v_mix_norm, v_mem_norm, v_w_in, v_w_mem_kv, v_swa_q_norm, v_swa_k_norm, v_swa_sinks, v_rel_bias, v_gla_w_gate_up, v_gla_b_gate, v_gla_out_norm, v_mem_q_norm, v_mem_k_norm, v_w_out, v_ffn2_norm, v_ffn2_w_gate, v_ffn2_w_up, v_ffn2_w_down):
    xi, yi, ci = lax.axis_index("x"), lax.axis_index("y"), lax.axis_index("c")
    c_arr = jnp.reshape(ci, (1,)).astype(jnp.int32)
    xy_arr = jnp.reshape(2 * xi + yi, (1,)).astype(jnp.int32)
    d = x.shape[-1]

    half_h = ffn1_w_gate.shape[-1] // 2

    def gather_ffn(wg_s, wu_s, wd_s, name, collective_id, after):
        w3_s = jnp.concatenate([wg_s.transpose(0, 2, 1), wu_s.transpose(0, 2, 1), wd_s], axis=0)
        w3_s = jnp.pad(w3_s.reshape(3, 2, half_h, d), ((0, 0), (0, 0), (0, FFN_HALF_PAD - half_h), (0, 0))).astype(BF16)
        if after is not None:
            w3_s, _ = lax.optimization_barrier((w3_s, after))
        return _all_gather([w3_s], ["row"], name=name, collective_id=collective_id)[0].reshape(3, -1, d)

    w3_1 = gather_ffn(ffn1_w_gate, ffn1_w_up, ffn1_w_down, "gather_ffn1", 0, None)
    mix_s = lax.optimization_barrier((w_in[0].astype(BF16), w_mem_kv[0].astype(BF16), w_out[0].astype(BF16), w3_1))[:3]
    win_all, wkv, wout = _all_gather(list(mix_s), ["stack", "row", "row"], name="gather_mix", collective_id=1)

    def gather_ffn2(x1):
        return gather_ffn(ffn2_w_gate, ffn2_w_up, ffn2_w_down, "gather_ffn2", 2, (wout, x1))

    win_p = _pack_win(win_all, tr=256, name="pack_w_in")

    small_w = [ffn1_norm, mix_norm, mem_norm, ffn2_norm, swa_q_norm, swa_k_norm, swa_sinks[0], rel_bias,
               gla_w_gate_up[0], gla_b_gate, gla_out_norm, mem_q_norm, mem_k_norm]
    collective_ids = {"ffn2": (3, 4), "mix": (5, 6), "ffn1a": (7, 8), "ffn1b": (9, 10)}
    reduced, small_box = {}, {}

    def on_grads(group, grads, carry, small=None):
        if group == "mix":
            dwin_p, dwkv, dwout = grads
            grads = [_unpack_win(dwin_p, tr=256, name="unpack_dw_in"), dwkv, dwout]
            kinds = ["stack", "row", "row"]
        else:
            kinds = ["row"]
        if reduced:
            earlier = list(reduced.values())[-1][1]
            *grads, _ = lax.optimization_barrier((*grads, earlier[0]))
        id_pair, id_chip = collective_ids[group]
        from_sibling = _pair_exchange(grads, kinds, name=f"pair_exchange_{group}", collective_id=id_pair)
        chip_sums = [_pair_sum(g, theirs, k, c_arr, name=f"pair_sum_{group}_{t}")
                     for t, (g, theirs, k) in enumerate(zip(grads, from_sibling, kinds))]
        if carry is not None:
            *chip_sums, carry = lax.optimization_barrier((*chip_sums, carry))
        if small is None:
            from_chips = _chip_exchange(chip_sums, None, name=f"chip_exchange_{group}", collective_id=id_chip)
        else:
            *from_chips, small_all = _chip_exchange(chip_sums, _pack_small(small), name=f"chip_exchange_{group}",
                                                    collective_id=id_chip)
            small_box[group] = small_all
        reduced[group] = (chip_sums, from_chips)
        return carry

    grad_x = _local_step(x[0], mem[0], loss_target[0], small_w, (w3_1, win_p, wkv, wout, gather_ffn2), on_grads)

    big_w = {"ffn1_w_gate": ("ffn1", 0, 0, True, ffn1_w_gate, m_ffn1_w_gate, v_ffn1_w_gate),
             "ffn1_w_up": ("ffn1", 0, 1, True, ffn1_w_up, m_ffn1_w_up, v_ffn1_w_up),
             "ffn1_w_down": ("ffn1", 0, 2, False, ffn1_w_down, m_ffn1_w_down, v_ffn1_w_down),
             "w_in": ("mix", 0, 0, False, w_in, m_w_in, v_w_in),
             "w_mem_kv": ("mix", 1, 0, False, w_mem_kv, m_w_mem_kv, v_w_mem_kv),
             "w_out": ("mix", 2, 0, False, w_out, m_w_out, v_w_out),
             "ffn2_w_gate": ("ffn2", 0, 0, True, ffn2_w_gate, m_ffn2_w_gate, v_ffn2_w_gate),
             "ffn2_w_up": ("ffn2", 0, 1, True, ffn2_w_up, m_ffn2_w_up, v_ffn2_w_up),
             "ffn2_w_down": ("ffn2", 0, 2, False, ffn2_w_down, m_ffn2_w_down, v_ffn2_w_down)}
    res = {}
    for nm, (group, t, mat, transposed, w, m, v) in big_w.items():
        shape = w.shape
        if transposed:
            w, m, v = (a.transpose(0, 2, 1) for a in (w, m, v))
        if group != "mix":
            w, m, v = (a.reshape(2, half_h, d) for a in (w, m, v))
        r = w.shape[1]
        tr = 256 if r % 256 == 0 else r
        halves = ["ffn1a", "ffn1b"] if group == "ffn1" else [group]
        out = _adam_big([reduced[k][0][t] for k in halves], [reduced[k][1][t] for k in halves], mat, xy_arr, w, m, v,
                        tr=tr, name=f"adam_{nm}")
        if transposed:
            out = [a.reshape(1, -1, d).transpose(0, 2, 1) for a in out]
        res[nm] = [a.reshape(shape) for a in out]
    small_names = ["ffn1_norm", "mix_norm", "mem_norm", "ffn2_norm", "swa_q_norm", "swa_k_norm", "swa_sinks", "rel_bias",
                   "gla_w_gate_up", "gla_b_gate", "gla_out_norm", "mem_q_norm", "mem_k_norm"]
    small_m = [m_ffn1_norm, m_mix_norm, m_mem_norm, m_ffn2_norm, m_swa_q_norm, m_swa_k_norm, m_swa_sinks, m_rel_bias,
               m_gla_w_gate_up, m_gla_b_gate, m_gla_out_norm, m_mem_q_norm, m_mem_k_norm]
    small_v = [v_ffn1_norm, v_mix_norm, v_mem_norm, v_ffn2_norm, v_swa_q_norm, v_swa_k_norm, v_swa_sinks, v_rel_bias,
               v_gla_w_gate_up, v_gla_b_gate, v_gla_out_norm, v_mem_q_norm, v_mem_k_norm]
    small_full = [ffn1_norm, mix_norm, mem_norm, ffn2_norm, swa_q_norm, swa_k_norm, swa_sinks, rel_bias,
                  gla_w_gate_up, gla_b_gate, gla_out_norm, mem_q_norm, mem_k_norm]
    zero = jnp.zeros((), F32)
    for group, sel in (("ffn1a", slice(1, None)), ("ffn1b", slice(0, 1))):
        late = group == "ffn1b"
        extra = [] if late else [zero]
        packed = _adam_small(small_box[group], _pack_small(small_full[sel] + extra), _pack_small(small_m[sel] + extra),
                             _pack_small(small_v[sel] + extra), name=f"adam_small_{group}")
        shapes = [a.shape for a in small_full[sel]] + [()] * len(extra)
        unpacked = [_unpack_small(pk, shapes) for pk in packed]
        for k, nm in enumerate(small_names[sel]):
            res[nm] = [unpacked[q][k] for q in range(4)]
        if not late:
            loss = unpacked[0][-1]

    order = ["ffn1_norm", "ffn1_w_gate", "ffn1_w_up", "ffn1_w_down", "mix_norm", "mem_norm", "w_in", "w_mem_kv",
             "swa_q_norm", "swa_k_norm", "swa_sinks", "rel_bias", "gla_w_gate_up", "gla_b_gate", "gla_out_norm",
             "mem_q_norm", "mem_k_norm", "w_out", "ffn2_norm", "ffn2_w_gate", "ffn2_w_up", "ffn2_w_down"]
    outs = [loss, grad_x[None]]
    for q in range(4):
        outs += [res[nm][q] for nm in order]
    return tuple(outs)
```

```python
import functools
import math

import numpy as np
import jax
import jax.numpy as jnp
from jax import lax
from jax.experimental import pallas as pl
from jax.experimental.pallas import tpu as pltpu
from jax.experimental.pallas import tpu_sc as plsc

F32 = jnp.float32
BF16 = jnp.bfloat16
SDS = jax.ShapeDtypeStruct

EPS = 1e-6
HEAD_DIM = 64
SWA_HEADS = 8
SWA_KV_HEADS = 2
SWA_GROUP = SWA_HEADS // SWA_KV_HEADS
BLOCK = 128
N_BUCKETS = 32
MAX_DISTANCE = 128
GLA_HEADS = 4
GLA_DK = 32
GLA_DV = 64
GLA_RANK = 16
GLA_TAU = 16.0
GLA_CHUNK = 32
MEM_HEADS = 4
SWA_Q_W = SWA_HEADS * HEAD_DIM
SWA_KV_W = SWA_KV_HEADS * HEAD_DIM
GLA_QK_W = GLA_HEADS * GLA_DK
GLA_V_W = GLA_HEADS * GLA_DV
MEM_Q_W = MEM_HEADS * HEAD_DIM
IN_W = 1808
IN_W_PAD = 1920
COL_SQ, COL_SKV, COL_GQ, COL_GK, COL_GV, COL_GG, COL_MQ, COL_GLR = 0, 512, 768, 896, 1024, 1280, 1536, 1792

ADAM_LR = 0.001
ADAM_B1 = 0.9
ADAM_B2 = 0.999
ADAM_EPS = 1e-08
ADAM_WD = 0.01
ADAM_STEP = 10

N_DEV = 8
VMEM_LIMIT_BYTES = 56 * 1024 * 1024
MESH = pl.DeviceIdType.MESH


def _params(*sem):
    return pltpu.CompilerParams(dimension_semantics=sem or None, vmem_limit_bytes=VMEM_LIMIT_BYTES)


def _dot(a, b, ta, tb, precision=None):
    dims = (((0 if ta else 1,), (1 if tb else 0,)), ((), ()))
    return lax.dot_general(a, b, dims, preferred_element_type=F32, precision=precision)


def _mm_raw(a, b, ta=False, tb=False):
    return _dot(a.astype(BF16), b.astype(BF16), ta, tb)


def _mmf_raw(a, b, ta=False, tb=False):
    return _dot(a, b, ta, tb, lax.Precision.HIGHEST)


def _make_mm(raw):
    @functools.partial(jax.custom_vjp, nondiff_argnums=(2, 3))
    def mm(a, b, ta=False, tb=False):
        return raw(a, b, ta, tb)

    def fwd(a, b, ta, tb):
        return raw(a, b, ta, tb), (a, b)

    def bwd(ta, tb, res, g):
        a, b = res
        da = raw(b, g, tb, True) if ta else raw(g, b, False, not tb)
        db = raw(g, a, True, ta) if tb else raw(a, g, not ta, False)
        return da, db

    mm.defvjp(fwd, bwd)
    return mm


_mm = _make_mm(_mm_raw)
_mmf = _make_mm(_mmf_raw)


def _mm3(a, b, ta=False, tb=False):
    a_hi, b_hi = a.astype(BF16).astype(F32), b.astype(BF16).astype(F32)
    return _mm(a_hi, b_hi, ta, tb) + _mm(a_hi, b - b_hi, ta, tb) + _mm(a - a_hi, b_hi, ta, tb)


def _rms(x, g):
    return x * lax.rsqrt(jnp.mean(x * x, axis=-1, keepdims=True) + EPS) * g


def _silu_mul(g, u):
    return jax.nn.silu(g) * u


def _log_sigmoid(z):
    return jnp.minimum(z, 0.0) - jnp.log(1.0 + jnp.exp(-jnp.abs(z)))


def _matmul(a_list, b, *, ta=False, tb=False, tm, tn, b_blocks=None, res=None, scale=1.0, out_dtype=F32, name):
    if not isinstance(a_list, (list, tuple)):
        a_list = [a_list]
    n_a = len(a_list)
    m = a_list[0].shape[1] if ta else a_list[0].shape[0]
    ks = [a.shape[0] if ta else a.shape[1] for a in a_list]
    n = b.shape[0] if tb else b.shape[1]
    if b_blocks is None:
        assert n_a == 1
        b_blocks = [0]
    tm, tn = min(tm, m), min(tn, n)
    assert m % tm == 0 and n % tn == 0, (m, n, tm, tn)

    def body(*refs):
        a_refs, b_refs = refs[:n_a], refs[n_a:2 * n_a]
        r_ref = refs[2 * n_a] if res is not None else None
        o_ref = refs[-1]
        acc = _mm_raw(a_refs[0][...], b_refs[0][...], ta, tb)
        for k in range(1, n_a):
            acc = acc + _mm_raw(a_refs[k][...], b_refs[k][...], ta, tb)
        if scale != 1.0:
            acc = acc * scale
        if r_ref is not None:
            acc = r_ref[...] + acc
        o_ref[...] = acc.astype(out_dtype)

    in_specs = []
    for k in ks:
        in_specs.append(pl.BlockSpec((k, tm), lambda i, j: (0, i)) if ta else pl.BlockSpec((tm, k), lambda i, j: (i, 0)))
    for k, blk in zip(ks, b_blocks):
        if tb:
            in_specs.append(pl.BlockSpec((tn, k), functools.partial(lambda i, j, blk: (j, blk), blk=blk)))
        else:
            in_specs.append(pl.BlockSpec((k, tn), functools.partial(lambda i, j, blk: (blk, j), blk=blk)))
    args = list(a_list) + [b] * n_a
    if res is not None:
        in_specs.append(pl.BlockSpec((tm, tn), lambda i, j: (i, j)))
        args.append(res)
    return pl.pallas_call(
        body, name=name, grid=(m // tm, n // tn), in_specs=in_specs,
        out_specs=pl.BlockSpec((tm, tn), lambda i, j: (i, j)), out_shape=SDS((m, n), out_dtype),
        compiler_params=_params("parallel", "parallel"),
    )(*args)


def _win_pieces(w):
    glr_lo, glr_hi = COL_MQ, COL_MQ + GLA_RANK
    out = []
    for j in range(N_DEV):
        for lo, hi, shift in ((0, glr_lo, 0), (glr_lo, glr_hi, COL_GLR - glr_lo), (glr_hi, IN_W, COL_MQ - glr_hi)):
            s, e = max(j * w, lo), min((j + 1) * w, hi)
            if s < e:
                out.append((j, s - j * w, e - j * w, s + shift))
    return out


def _pack_win(win_all, *, tr, name):
    _, d, w = win_all.shape

    def body(i_ref, o_ref):
        for j, a, b, dst in _win_pieces(w):
            o_ref[:, dst:dst + b - a] = i_ref[j][:, a:b]
        o_ref[:, IN_W:] = jnp.zeros((tr, IN_W_PAD - IN_W), o_ref.dtype)

    return pl.pallas_call(
        body, name=name, grid=(d // tr,), in_specs=[pl.BlockSpec((N_DEV, tr, w), lambda i: (0, i, 0))],
        out_specs=pl.BlockSpec((tr, IN_W_PAD), lambda i: (i, 0)), out_shape=SDS((d, IN_W_PAD), win_all.dtype),
        compiler_params=_params("parallel"),
    )(win_all)


def _unpack_win(dwin_p, *, tr, name):
    d = dwin_p.shape[0]
    w = IN_W // N_DEV

    def body(i_ref, o_ref):
        for j, a, b, src in _win_pieces(w):
            o_ref[j % 2, j // 2, :, a:b] = i_ref[:, src:src + b - a]

    return pl.pallas_call(
        body, name=name, grid=(d // tr,), in_specs=[pl.BlockSpec((tr, IN_W_PAD), lambda i: (i, 0))],
        out_specs=pl.BlockSpec((2, 4, tr, w), lambda i: (0, 0, i, 0)), out_shape=SDS((2, 4, d, w), dwin_p.dtype),
        compiler_params=_params("parallel"),
    )(dwin_p)


def _rms_fwd(x, g, *, tm, name):
    s, d = x.shape

    def body(x_ref, g_ref, h_ref):
        h_ref[...] = _rms(x_ref[...], g_ref[...]).astype(BF16)

    return pl.pallas_call(
        body, name=name, grid=(s // tm,),
        in_specs=[pl.BlockSpec((tm, d), lambda i: (i, 0)), pl.BlockSpec((1, d), lambda i: (0, 0))],
        out_specs=pl.BlockSpec((tm, d), lambda i: (i, 0)), out_shape=SDS((s, d), BF16),
        compiler_params=_params("parallel"),
    )(x, g)


def _rms_bwd(x, g, dh, dres, *, tm, name):
    s, d = x.shape
    want_dx = dres is not None

    def body(*refs):
        if want_dx:
            x_ref, g_ref, dh_ref, dres_ref, dx_ref, dxb_ref, dg_ref = refs
        else:
            x_ref, g_ref, dh_ref, dg_ref = refs
        _, vjp = jax.vjp(_rms, x_ref[...], g_ref[...])
        dx, dg = vjp(dh_ref[...])
        if want_dx:
            dx = dres_ref[...] + dx
            dx_ref[...] = dx
            dxb_ref[...] = dx.astype(BF16)

        @pl.when(pl.program_id(0) == 0)
        def _():
            dg_ref[...] = jnp.zeros_like(dg_ref)

        dg_ref[...] += dg

    row = pl.BlockSpec((tm, d), lambda i: (i, 0))
    vec = pl.BlockSpec((1, d), lambda i: (0, 0))
    if want_dx:
        return pl.pallas_call(
            body, name=name, grid=(s // tm,), in_specs=[row, vec, row, row], out_specs=[row, row, vec],
            out_shape=[SDS((s, d), F32), SDS((s, d), BF16), SDS((1, d), F32)], compiler_params=_params("arbitrary"),
        )(x, g, dh, dres)
    return None, None, pl.pallas_call(
        body, name=name, grid=(s // tm,), in_specs=[row, vec, row], out_specs=vec,
        out_shape=SDS((1, d), F32), compiler_params=_params("arbitrary"),
    )(x, g, dh)


FFN_TN = 256
FFN_TN_FWD = 512
FFN_HALF_PAD = 192


def _ffn_fwd(x, gain, w3, tag, *, tm=1024):
    s, d = x.shape
    f = w3.shape[1]
    tn = FFN_TN_FWD
    nj = f // tn
    tm = min(tm, s)

    def body(x_ref, gain_ref, wg_ref, wu_ref, wd_ref, y_ref, h_ref, g_ref, u_ref, acc_s):
        j = pl.program_id(1)

        @pl.when(j == 0)
        def _():
            h_ref[...] = _rms(x_ref[...], gain_ref[...]).astype(BF16)
            acc_s[...] = jnp.zeros_like(acc_s)

        hv = h_ref[...]
        g = _mm_raw(hv, wg_ref[...], False, True)
        u = _mm_raw(hv, wu_ref[...], False, True)
        g_ref[...] = g.astype(BF16)
        u_ref[...] = u.astype(BF16)
        acc_s[...] += _mm_raw(_silu_mul(g, u), wd_ref[...])

        @pl.when(j == nj - 1)
        def _():
            y_ref[...] = x_ref[...] + 0.5 * acc_s[...]

    row = pl.BlockSpec((tm, d), lambda i, j: (i, 0))
    tile = pl.BlockSpec((tm, tn), lambda i, j: (i, j))
    y, h, g, u = pl.pallas_call(
        body, name=f"{tag}_fwd", grid=(s // tm, nj),
        in_specs=[row, pl.BlockSpec((1, d), lambda i, j: (0, 0))]
        + [pl.BlockSpec((None, tn, d), functools.partial(lambda i, j, k: (k, j, 0), k=k)) for k in range(3)],
        out_specs=[row, row, tile, tile],
        out_shape=[SDS((s, d), F32), SDS((s, d), BF16), SDS((s, f), BF16), SDS((s, f), BF16)],
        scratch_shapes=[pltpu.VMEM((tm, d), F32)],
        compiler_params=_params("parallel", "arbitrary"),
    )(x, gain, w3, w3, w3)
    return y, (h, g, u)


def _ffn_bwd_part(dyb, w3, saved, first, count, dh_init, *, name):
    h, g, u = saved
    s, d = h.shape
    tn = FFN_TN

    def body(*refs):
        if dh_init is None:
            dy_ref, h_ref, wg_ref, wu_ref, wd_ref, g_ref, u_ref, dh_ref, dw3_ref = refs
        else:
            dy_ref, h_ref, wg_ref, wu_ref, wd_ref, g_ref, u_ref, dh0_ref, dh_ref, dw3_ref = refs

        @pl.when(pl.program_id(0) == 0)
        def _():
            dh_ref[...] = jnp.zeros_like(dh_ref) if dh_init is None else dh0_ref[...]

        dyv = dy_ref[...]
        da = _mm_raw(dyv, wd_ref[...], False, True) * 0.5
        a, vjp = jax.vjp(_silu_mul, g_ref[...].astype(F32), u_ref[...].astype(F32))
        dg, du = vjp(da)
        dg = dg.astype(BF16)
        du = du.astype(BF16)
        dh_ref[...] += _mm_raw(dg, wg_ref[...]) + _mm_raw(du, wu_ref[...])
        hv = h_ref[...]
        dw3_ref[0] = _mm_raw(dg, hv, True, False).astype(BF16)
        dw3_ref[1] = _mm_raw(du, hv, True, False).astype(BF16)
        dw3_ref[2] = (_mm_raw(a, dyv, True, False) * 0.5).astype(BF16)

    full = pl.BlockSpec((s, d), lambda j: (0, 0))
    once = pl.BlockSpec((s, d), lambda j: (0, 0), pipeline_mode=pl.Buffered(1))
    tile = pl.BlockSpec((s, tn), lambda j: (0, first + j))
    in_specs = ([once, once]
                + [pl.BlockSpec((None, tn, d), functools.partial(lambda j, k: (k, first + j, 0), k=k)) for k in range(3)]
                + [tile, tile])
    args = [dyb, h, w3, w3, w3, g, u]
    if dh_init is not None:
        in_specs.append(once)
        args.append(dh_init)
    return pl.pallas_call(
        body, name=name, grid=(count,), in_specs=in_specs,
        out_specs=[full, pl.BlockSpec((3, tn, d), lambda j: (0, j, 0))],
        out_shape=[SDS((s, d), F32), SDS((3, count * tn, d), BF16)],
        compiler_params=_params("arbitrary"),
    )(*args)


def _loss_bwd(y, target, *, tm, name):
    s, d = y.shape

    def body(y_ref, t_ref, dy_ref, dyb_ref, l_ref):
        diff = y_ref[...] - t_ref[...]
        dy_ref[...] = diff * (1.0 / d)
        dyb_ref[...] = (diff * (1.0 / d)).astype(BF16)

        @pl.when(pl.program_id(0) == 0)
        def _():
            l_ref[...] = jnp.zeros_like(l_ref)

        l_ref[...] += 0.5 * jnp.sum(jnp.mean(diff * diff, axis=-1, keepdims=True), axis=0, keepdims=True)

    row = pl.BlockSpec((tm, d), lambda i: (i, 0))
    return pl.pallas_call(
        body, name=name, grid=(s // tm,), in_specs=[row, row],
        out_specs=[row, row, pl.BlockSpec((1, 1), lambda i: (0, 0))],
        out_shape=[SDS((s, d), F32), SDS((s, d), BF16), SDS((1, 1), F32)],
        compiler_params=_params("arbitrary"),
    )(y, target)


def _bucket_table():
    qi = np.arange(BLOCK)[:, None]
    kj = np.arange(2 * BLOCK)[None, :]
    dist = np.maximum(qi + BLOCK - kj, 0)
    max_exact = N_BUCKETS // 2
    d = np.maximum(dist, 1).astype(np.float32)
    large = max_exact + (np.log(d / np.float32(max_exact)) / np.float32(math.log(MAX_DISTANCE / max_exact))
                         * np.float32(N_BUCKETS - max_exact)).astype(np.int32)
    large = np.minimum(large, N_BUCKETS - 1)
    band = np.where(dist < max_exact, dist, large).astype(np.int32)
    return np.where(np.tril(np.ones((BLOCK, BLOCK), bool)), band[:, BLOCK:], band[:, :BLOCK])


SWA_STACK = SWA_GROUP * BLOCK


def _swa_masks(n):
    qi = lax.broadcasted_iota(jnp.int32, (SWA_STACK, BLOCK), 0) % BLOCK
    kj = lax.broadcasted_iota(jnp.int32, (SWA_STACK, BLOCK), 1)
    own = kj <= qi
    return own, own | (n > 0)


def _swa_group(q, kp, kc, vp, vc, qg, kg, sink, bias, own, valid):
    qn = _rms(q, qg)
    s = jnp.where(own, _mm(qn, _rms(kc, kg), False, True), _mm(qn, _rms(kp, kg), False, True))
    s = s * (HEAD_DIM ** -0.5) + bias
    s = jnp.where(valid, s, -jnp.inf)
    m = lax.stop_gradient(jnp.maximum(jnp.max(s, axis=-1, keepdims=True), sink))
    p = jnp.exp(s - m)
    p = p / (jnp.sum(p, axis=-1, keepdims=True) + jnp.exp(sink - m))
    return _mm(jnp.where(own, p, 0.0), vc) + _mm(jnp.where(own, 0.0, p), vp)


def _swa_bias_table(rb_ref, bucket, bias_s):
    for h in range(SWA_HEADS):
        acc = jnp.zeros((BLOCK, BLOCK), F32)
        for b in range(N_BUCKETS):
            acc = jnp.where(bucket == b, rb_ref[b, h], acc)
        bias_s[h // SWA_GROUP, (h % SWA_GROUP) * BLOCK:(h % SWA_GROUP + 1) * BLOCK, :] = acc


def _swa_stack(ref, g):
    return jnp.concatenate([ref[:, (g * SWA_GROUP + hh) * HEAD_DIM:(g * SWA_GROUP + hh + 1) * HEAD_DIM]
                            for hh in range(SWA_GROUP)], axis=0)


def _swa_unstack(ref, g, stacked):
    for hh in range(SWA_GROUP):
        h = g * SWA_GROUP + hh
        ref[:, h * HEAD_DIM:(h + 1) * HEAD_DIM] = stacked[hh * BLOCK:(hh + 1) * BLOCK]


def _swa_sink_column(sink_ref, g):
    head = lax.broadcasted_iota(jnp.int32, (SWA_STACK, 1), 0) // BLOCK
    col = jnp.zeros((SWA_STACK, 1), F32)
    for hh in range(SWA_GROUP):
        col = jnp.where(head == hh, sink_ref[g * SWA_GROUP + hh], col)
    return col


def _swa_band(kvp_ref, kvc_ref, g):
    k = slice(g * HEAD_DIM, (g + 1) * HEAD_DIM)
    v = slice(SWA_KV_W + g * HEAD_DIM, SWA_KV_W + (g + 1) * HEAD_DIM)
    return kvp_ref[:, k], kvc_ref[:, k], kvp_ref[:, v], kvc_ref[:, v]


def _swa_specs(order):
    kvc = COL_SKV // (2 * SWA_KV_W)
    return [
        pl.BlockSpec((BLOCK, SWA_Q_W), lambda t: (order(t), 0)),
        pl.BlockSpec((BLOCK, 2 * SWA_KV_W), lambda t: (jnp.maximum(order(t) - 1, 0), kvc)),
        pl.BlockSpec((BLOCK, 2 * SWA_KV_W), lambda t: (order(t), kvc)),
        pl.BlockSpec((1, HEAD_DIM), lambda t: (0, 0)),
        pl.BlockSpec((1, HEAD_DIM), lambda t: (0, 0)),
        pl.BlockSpec(memory_space=pltpu.SMEM),
        pl.BlockSpec(memory_space=pltpu.SMEM),
        pl.BlockSpec((BLOCK, BLOCK), lambda t: (0, 0)),
    ]


def _swa_fwd(p, qg, kg, sinks, rel_bias, *, name):
    s = p.shape[0]
    nb = s // BLOCK

    def body(q_ref, kvp_ref, kvc_ref, qg_ref, kg_ref, sink_ref, rb_ref, bucket_ref, y_ref, bias_s):
        n = pl.program_id(0)

        @pl.when(n == 0)
        def _():
            _swa_bias_table(rb_ref, bucket_ref[...], bias_s)

        own, valid = _swa_masks(n)
        for g in range(SWA_KV_HEADS):
            out = _swa_group(_swa_stack(q_ref, g), *_swa_band(kvp_ref, kvc_ref, g), qg_ref[...], kg_ref[...],
                             _swa_sink_column(sink_ref, g), bias_s[g], own, valid)
            _swa_unstack(y_ref, g, out)

    return pl.pallas_call(
        body, name=name, grid=(nb,), in_specs=_swa_specs(lambda t: t),
        out_specs=pl.BlockSpec((BLOCK, SWA_Q_W), lambda t: (t, 0)), out_shape=SDS((s, SWA_Q_W), F32),
        scratch_shapes=[pltpu.VMEM((SWA_KV_HEADS, SWA_STACK, BLOCK), F32)],
        compiler_params=_params("arbitrary"),
    )(p, p, p, qg, kg, sinks, rel_bias, jnp.asarray(_bucket_table()))


def _swa_bwd(p, qg, kg, sinks, rel_bias, dy_all, *, name):
    s = p.shape[0]
    nb = s // BLOCK

    def body(q_ref, kvp_ref, kvc_ref, qg_ref, kg_ref, sink_ref, rb_ref, bucket_ref, dy_ref,
             dq_ref, dkv_ref, dqg_ref, dkg_ref, dsink_ref, drb_ref, bias_s, dbias_s, carry_s):
        t = pl.program_id(0)
        n = nb - 1 - t

        @pl.when(t == 0)
        def _():
            _swa_bias_table(rb_ref, bucket_ref[...], bias_s)
            dbias_s[...] = jnp.zeros_like(dbias_s)
            carry_s[...] = jnp.zeros_like(carry_s)
            dqg_ref[...] = jnp.zeros_like(dqg_ref)
            dkg_ref[...] = jnp.zeros_like(dkg_ref)
            dsink_ref[...] = jnp.zeros_like(dsink_ref)
            drb_ref[...] = jnp.zeros_like(drb_ref)

        own, valid = _swa_masks(n)
        lane = lax.broadcasted_iota(jnp.int32, (1, BLOCK), 1)
        dqg = jnp.zeros((1, HEAD_DIM), F32)
        dkg = jnp.zeros((1, HEAD_DIM), F32)
        dsink_vec = jnp.zeros((1, BLOCK), F32)
        for g in range(SWA_KV_HEADS):
            _, vjp = jax.vjp(functools.partial(_swa_group, own=own, valid=valid), _swa_stack(q_ref, g),
                             *_swa_band(kvp_ref, kvc_ref, g), qg_ref[...], kg_ref[...], _swa_sink_column(sink_ref, g),
                             bias_s[g])
            dq, dkp, dkc, dvp, dvc, dqg_g, dkg_g, dsink_col, dbias = vjp(_swa_stack(dy_ref, g))
            _swa_unstack(dq_ref, g, dq)
            dqg += dqg_g
            dkg += dkg_g
            dbias_s[g] += dbias
            for hh in range(SWA_GROUP):
                dsink_h = jnp.sum(dsink_col[hh * BLOCK:(hh + 1) * BLOCK], axis=0, keepdims=True)
                dsink_vec += jnp.where(lane == g * SWA_GROUP + hh, dsink_h, 0.0)
            lo = g * HEAD_DIM
            dkv_ref[:, lo:lo + HEAD_DIM] = dkc + carry_s[g]
            carry_s[g] = dkp
            lo += SWA_KV_W
            dkv_ref[:, lo:lo + HEAD_DIM] = dvc + carry_s[SWA_KV_HEADS + g]
            carry_s[SWA_KV_HEADS + g] = dvp
        dqg_ref[...] += dqg
        dkg_ref[...] += dkg
        dsink_ref[...] += dsink_vec

        @pl.when(t == nb - 1)
        def _():
            bucket = bucket_ref[...]
            row = lax.broadcasted_iota(jnp.int32, (N_BUCKETS, BLOCK), 0)
            col = lax.broadcasted_iota(jnp.int32, (N_BUCKETS, BLOCK), 1)
            acc = jnp.zeros((N_BUCKETS, BLOCK), F32)
            for h in range(SWA_HEADS):
                dbias = dbias_s[h // SWA_GROUP, (h % SWA_GROUP) * BLOCK:(h % SWA_GROUP + 1) * BLOCK, :]
                for b in range(N_BUCKETS):
                    part = jnp.sum(jnp.where(bucket == b, dbias, 0.0), axis=1, keepdims=True)
                    val = jnp.sum(part, axis=0, keepdims=True)
                    acc = acc + jnp.where((row == b) & (col == h), val, 0.0)
            drb_ref[...] = acc

    order = lambda t: nb - 1 - t
    vec = pl.BlockSpec((1, HEAD_DIM), lambda t: (0, 0))
    return pl.pallas_call(
        body, name=name, grid=(nb,),
        in_specs=_swa_specs(order) + [pl.BlockSpec((BLOCK, SWA_Q_W), lambda t: (order(t), 0))],
        out_specs=[pl.BlockSpec((BLOCK, SWA_Q_W), lambda t: (order(t), 0)),
                   pl.BlockSpec((BLOCK, 2 * SWA_KV_W), lambda t: (order(t), 0)),
                   vec, vec, pl.BlockSpec((1, BLOCK), lambda t: (0, 0)),
                   pl.BlockSpec((N_BUCKETS, BLOCK), lambda t: (0, 0))],
        out_shape=[SDS((s, SWA_Q_W), F32), SDS((s, 2 * SWA_KV_W), F32), SDS((1, HEAD_DIM), F32),
                   SDS((1, HEAD_DIM), F32), SDS((1, BLOCK), F32), SDS((N_BUCKETS, BLOCK), F32)],
        scratch_shapes=[pltpu.VMEM((SWA_KV_HEADS, SWA_STACK, BLOCK), F32),
                        pltpu.VMEM((SWA_KV_HEADS, SWA_STACK, BLOCK), F32),
                        pltpu.VMEM((2 * SWA_KV_HEADS, BLOCK, HEAD_DIM), F32)],
        compiler_params=_params("arbitrary"),
    )(p, p, p, qg, kg, sinks, rel_bias, jnp.asarray(_bucket_table()), dy_all)


def _mem_head(q, k, v, qg, kg):
    qn = _rms(q, qg)
    kn = _rms(k, kg)
    s = _mm(qn, kn, False, True) * (HEAD_DIM ** -0.5)
    m = lax.stop_gradient(jnp.max(s, axis=-1, keepdims=True))
    e = jnp.exp(s - m)
    return _mm(e / jnp.sum(e, axis=-1, keepdims=True), v)


def _mem_fwd(p, kv, qg, kg, *, tq, name):
    s = p.shape[0]
    m = kv.shape[0]

    def body(q_ref, kv_ref, qg_ref, kg_ref, y_ref):
        for h in range(MEM_HEADS):
            cols = slice(h * HEAD_DIM, (h + 1) * HEAD_DIM)
            vcols = slice(MEM_Q_W + h * HEAD_DIM, MEM_Q_W + (h + 1) * HEAD_DIM)
            y_ref[:, cols] = _mem_head(q_ref[:, cols], kv_ref[:, cols], kv_ref[:, vcols], qg_ref[...], kg_ref[...])

    vec = pl.BlockSpec((1, HEAD_DIM), lambda t: (0, 0))
    return pl.pallas_call(
        body, name=name, grid=(s // tq,),
        in_specs=[pl.BlockSpec((tq, MEM_Q_W), lambda t: (t, COL_MQ // MEM_Q_W)),
                  pl.BlockSpec((m, 2 * MEM_Q_W), lambda t: (0, 0)), vec, vec],
        out_specs=pl.BlockSpec((tq, MEM_Q_W), lambda t: (t, 0)), out_shape=SDS((s, MEM_Q_W), F32),
        compiler_params=_params("parallel"),
    )(p, kv, qg, kg)


def _mem_bwd(p, kv, qg, kg, dy_all, *, tq, name):
    s = p.shape[0]
    m = kv.shape[0]

    def body(q_ref, kv_ref, qg_ref, kg_ref, dy_ref, dq_ref, dkv_ref, dqg_ref, dkg_ref):
        @pl.when(pl.program_id(0) == 0)
        def _():
            dkv_ref[...] = jnp.zeros_like(dkv_ref)
            dqg_ref[...] = jnp.zeros_like(dqg_ref)
            dkg_ref[...] = jnp.zeros_like(dkg_ref)

        dqg = jnp.zeros((1, HEAD_DIM), F32)
        dkg = jnp.zeros((1, HEAD_DIM), F32)
        for h in range(MEM_HEADS):
            cols = slice(h * HEAD_DIM, (h + 1) * HEAD_DIM)
            vcols = slice(MEM_Q_W + h * HEAD_DIM, MEM_Q_W + (h + 1) * HEAD_DIM)
            _, vjp = jax.vjp(_mem_head, q_ref[:, cols], kv_ref[:, cols], kv_ref[:, vcols], qg_ref[...], kg_ref[...])
            dq, dk, dv, dqg_h, dkg_h = vjp(dy_ref[:, cols])
            dq_ref[:, cols] = dq
            dkv_ref[:, cols] += dk
            dkv_ref[:, vcols] += dv
            dqg += dqg_h
            dkg += dkg_h
        dqg_ref[...] += dqg
        dkg_ref[...] += dkg

    vec = pl.BlockSpec((1, HEAD_DIM), lambda t: (0, 0))
    full = pl.BlockSpec((m, 2 * MEM_Q_W), lambda t: (0, 0))
    dy_col = (SWA_Q_W + GLA_V_W) // MEM_Q_W
    return pl.pallas_call(
        body, name=name, grid=(s // tq,),
        in_specs=[pl.BlockSpec((tq, MEM_Q_W), lambda t: (t, COL_MQ // MEM_Q_W)), full, vec, vec,
                  pl.BlockSpec((tq, MEM_Q_W), lambda t: (t, dy_col))],
        out_specs=[pl.BlockSpec((tq, MEM_Q_W), lambda t: (t, 0)), full, vec, vec],
        out_shape=[SDS((s, MEM_Q_W), F32), SDS((m, 2 * MEM_Q_W), F32), SDS((1, HEAD_DIM), F32), SDS((1, HEAD_DIM), F32)],
        compiler_params=_params("arbitrary"),
    )(p, kv, qg, kg, dy_all)


GLA_ROWS = 256


GLA_GROUP = 4


def _gla_consts():
    c, h, r = GLA_CHUNK, GLA_HEADS, GLA_GROUP * GLA_CHUNK
    i2 = lax.broadcasted_iota(jnp.int32, (c, c), 0)
    j2 = lax.broadcasted_iota(jnp.int32, (c, c), 1)
    slab_q = lax.broadcasted_iota(jnp.int32, (h, r, GLA_QK_W), 0)
    lane_q = lax.broadcasted_iota(jnp.int32, (h, r, GLA_QK_W), 2)
    row_a = lax.broadcasted_iota(jnp.int32, (h * r, r), 0) % r
    col_a = lax.broadcasted_iota(jnp.int32, (h * r, r), 1)
    slab_o = lax.broadcasted_iota(jnp.int32, (h, r, GLA_V_W), 0)
    lane_o = lax.broadcasted_iota(jnp.int32, (h, r, GLA_V_W), 2)
    row_s = lax.broadcasted_iota(jnp.int32, (GLA_V_W, GLA_QK_W), 0)
    col_s = lax.broadcasted_iota(jnp.int32, (GLA_V_W, GLA_QK_W), 1)
    return dict(
        ltri=(j2 <= i2).astype(F32),
        m_q=(slab_q == lane_q // GLA_DK).astype(F32),
        causal=(col_a <= row_a) & (col_a // c == row_a // c),
        m_o=(slab_o == lane_o // GLA_DV).astype(F32),
        m_s=(row_s // GLA_DV == col_s // GLA_DK).astype(F32),
    )


def _gla_step(q, k, v, z, bg, st, c):
    h = GLA_HEADS
    kt, qt, qe, decay = [], [], [], []
    for qc, kc, zc in zip(q, k, z):
        la = _log_sigmoid(zc + bg) * (1.0 / GLA_TAU)
        b = _mmf(c["ltri"], la)
        bl = jnp.sum(la, axis=0, keepdims=True)
        qs = qc * (GLA_DK ** -0.5)
        kt.append(kc * jnp.exp(bl - b))
        qt.append(qs * jnp.exp(b - bl))
        qe.append(qs * jnp.exp(b))
        decay.append(jnp.exp(bl))
    o_intra = []
    rows = GLA_GROUP * GLA_CHUNK
    for lo in range(0, len(q), GLA_GROUP):
        qt_all, kt_all, v_all = (jnp.concatenate(parts[lo:lo + GLA_GROUP], axis=0) for parts in (qt, kt, v))
        q_stack = (jnp.broadcast_to(qt_all[None], (h, rows, GLA_QK_W)) * c["m_q"]).reshape(h * rows, GLA_QK_W)
        a = jnp.where(c["causal"], _mm3(q_stack, kt_all, False, True), 0.0)
        o_stack = _mm(a, v_all)
        o_intra.append(jnp.sum(o_stack.reshape(h, rows, GLA_V_W) * c["m_o"], axis=0))
    o_intra = jnp.concatenate(o_intra, axis=0)
    o_inter = []
    for qec, ktc, vc, dc in zip(qe, kt, v, decay):
        o_inter.append(_mm(qec, st, False, True))
        st = st * dc + _mm(vc, ktc, True, False) * c["m_s"]
    return o_intra + jnp.concatenate(o_inter, axis=0), st


def _gla_post(o, gg, gain, g64):
    ms = _mmf(o * o, g64) * (1.0 / GLA_DV)
    return o * lax.rsqrt(ms + EPS) * gain * jax.nn.silu(gg)


def _gla_g64():
    r = lax.broadcasted_iota(jnp.int32, (GLA_V_W, GLA_V_W), 0)
    c = lax.broadcasted_iota(jnp.int32, (GLA_V_W, GLA_V_W), 1)
    return (r // GLA_DV == c // GLA_DV).astype(F32)


def _gla_in_specs(order):
    r = GLA_ROWS
    return [
        pl.BlockSpec((r, GLA_QK_W), lambda t: (order(t), COL_GQ // GLA_QK_W)),
        pl.BlockSpec((r, GLA_QK_W), lambda t: (order(t), COL_GK // GLA_QK_W)),
        pl.BlockSpec((r, GLA_V_W), lambda t: (order(t), COL_GV // GLA_V_W)),
        pl.BlockSpec((r, GLA_V_W), lambda t: (order(t), COL_GG // GLA_V_W)),
        pl.BlockSpec((r, GLA_QK_W), lambda t: (order(t), 0)),
        pl.BlockSpec((1, GLA_QK_W), lambda t: (0, 0)),
        pl.BlockSpec((1, GLA_V_W), lambda t: (0, 0)),
    ]


def _gla_pieces(q_ref, k_ref, v_ref, z_ref, cps):
    chunk = lambda ref: [ref[ci * GLA_CHUNK:(ci + 1) * GLA_CHUNK, :] for ci in range(cps)]
    return chunk(q_ref), chunk(k_ref), chunk(v_ref), chunk(z_ref)


def _gla_fwd(p, z, bg, gain, *, name):
    s = p.shape[0]
    r = GLA_ROWS
    cps = r // GLA_CHUNK

    def body(q_ref, k_ref, v_ref, gg_ref, z_ref, bg_ref, gain_ref, y_ref, oraw_ref, stsave_ref, st_s):
        @pl.when(pl.program_id(0) == 0)
        def _():
            st_s[...] = jnp.zeros_like(st_s)

        st = st_s[...]
        stsave_ref[0] = st
        o, st = _gla_step(*_gla_pieces(q_ref, k_ref, v_ref, z_ref, cps), bg_ref[...], st, _gla_consts())
        oraw_ref[...] = o
        st_s[...] = st
        y_ref[...] = _gla_post(o, gg_ref[...], gain_ref[...], _gla_g64())

    rowv = pl.BlockSpec((r, GLA_V_W), lambda t: (t, 0))
    return pl.pallas_call(
        body, name=name, grid=(s // r,), in_specs=_gla_in_specs(lambda t: t),
        out_specs=[rowv, rowv, pl.BlockSpec((1, GLA_V_W, GLA_QK_W), lambda t: (t, 0, 0))],
        out_shape=[SDS((s, GLA_V_W), F32), SDS((s, GLA_V_W), F32), SDS((s // r, GLA_V_W, GLA_QK_W), F32)],
        scratch_shapes=[pltpu.VMEM((GLA_V_W, GLA_QK_W), F32)],
        compiler_params=_params("arbitrary"),
    )(p, p, p, p, z, bg, gain)


def _gla_bwd(p, z, bg, gain, oraw, stsave, dy_all, *, name):
    s = p.shape[0]
    r = GLA_ROWS
    cps = r // GLA_CHUNK
    nsteps = s // r
    w_qkvg = 2 * GLA_QK_W + 2 * GLA_V_W

    def body(q_ref, k_ref, v_ref, gg_ref, z_ref, bg_ref, gain_ref, oraw_ref, stsave_ref, dy_ref,
             dqkvg_ref, dz_ref, dbg_ref, dgain_ref, dst_s):
        @pl.when(pl.program_id(0) == 0)
        def _():
            dst_s[...] = jnp.zeros_like(dst_s)
            dbg_ref[...] = jnp.zeros_like(dbg_ref)
            dgain_ref[...] = jnp.zeros_like(dgain_ref)

        _, vjp = jax.vjp(functools.partial(_gla_post, g64=_gla_g64()), oraw_ref[...], gg_ref[...], gain_ref[...])
        do, dgg, dgain = vjp(dy_ref[...])
        dqkvg_ref[:, 2 * GLA_QK_W + GLA_V_W:] = dgg
        dgain_ref[...] += dgain
        _, vjp = jax.vjp(functools.partial(_gla_step, c=_gla_consts()), *_gla_pieces(q_ref, k_ref, v_ref, z_ref, cps),
                         bg_ref[...], stsave_ref[0])
        dq, dk, dv, dz, dbg, dst = vjp((do, dst_s[...]))
        for ci in range(cps):
            rows = slice(ci * GLA_CHUNK, (ci + 1) * GLA_CHUNK)
            dqkvg_ref[rows, 0:GLA_QK_W] = dq[ci]
            dqkvg_ref[rows, GLA_QK_W:2 * GLA_QK_W] = dk[ci]
            dqkvg_ref[rows, 2 * GLA_QK_W:2 * GLA_QK_W + GLA_V_W] = dv[ci]
            dz_ref[rows, :] = dz[ci]
        dst_s[...] = dst
        dbg_ref[...] += dbg

    order = lambda t: nsteps - 1 - t
    rowv = pl.BlockSpec((r, GLA_V_W), lambda t: (order(t), 0))
    return pl.pallas_call(
        body, name=name, grid=(nsteps,),
        in_specs=_gla_in_specs(order) + [
            rowv, pl.BlockSpec((1, GLA_V_W, GLA_QK_W), lambda t: (order(t), 0, 0)),
            pl.BlockSpec((r, GLA_V_W), lambda t: (order(t), SWA_Q_W // GLA_V_W))],
        out_specs=[pl.BlockSpec((r, w_qkvg), lambda t: (order(t), 0)), pl.BlockSpec((r, GLA_QK_W), lambda t: (order(t), 0)),
                   pl.BlockSpec((1, GLA_QK_W), lambda t: (0, 0)), pl.BlockSpec((1, GLA_V_W), lambda t: (0, 0))],
        out_shape=[SDS((s, w_qkvg), F32), SDS((s, GLA_QK_W), F32), SDS((1, GLA_QK_W), F32), SDS((1, GLA_V_W), F32)],
        scratch_shapes=[pltpu.VMEM((GLA_V_W, GLA_QK_W), F32)],
        compiler_params=_params("arbitrary"),
    )(p, p, p, p, z, bg, gain, oraw, stsave, dy_all)


def _local_step(x, mem, target, small, big, on_grads):
    g1, gmix, gmem, g2, sqg, skg, sinks, rel_bias, wgu, bg, gla_gain, mqg, mkg = small
    w3_1, win_p, wkv, wout, gather_ffn2 = big
    wgu_pad = jnp.zeros((GLA_QK_W, GLA_QK_W), BF16).at[:GLA_RANK].set(wgu.astype(BF16))
    gain256 = jnp.tile(gla_gain, (1, GLA_HEADS))

    x1, saved1 = _ffn_fwd(x, g1, w3_1, "ffn1")
    w3_2 = gather_ffn2(x1)
    h = _rms_fwd(x1, gmix, tm=512, name="mix_rms")
    p = _matmul(h, win_p, tm=1024, tn=IN_W_PAD, name="mix_in")
    hm = _rms_fwd(mem, gmem, tm=256, name="mem_rms")
    kv = _matmul(hm, wkv, tm=256, tn=512, name="mem_kv")
    p_glr = p[:, COL_GLR:]
    z = _matmul(p_glr, wgu_pad, tm=1024, tn=GLA_QK_W, name="gla_gate")
    y_swa = _swa_fwd(p, sqg, skg, sinks, rel_bias, name="swa_fwd")
    y_gla, oraw, stsave = _gla_fwd(p, z, bg, gain256, name="gla_fwd")
    y_mem = _mem_fwd(p, kv, mqg, mkg, tq=512, name="mem_fwd")
    x2 = _matmul([y_swa, y_gla, y_mem], wout, b_blocks=[0, 2, 3], tm=1024, tn=1024, res=x1, name="mix_out")
    x3, saved2 = _ffn_fwd(x2, g2, w3_2, "ffn2")

    dy, dyb, loss = _loss_bwd(x3, target, tm=512, name="loss")
    tiles = w3_2.shape[1] // FFN_TN
    dh2, dw3_2 = _ffn_bwd_part(dyb, w3_2, saved2, 0, tiles, None, name="ffn2_bwd")
    dx2, dx2b, dg2 = _rms_bwd(x2, g2, dh2, dy, tm=512, name="ffn2_drms")
    dx2b = on_grads("ffn2", [dw3_2.reshape(3, 2, -1, dw3_2.shape[-1])], dx2b)
    dy_all = _matmul(dx2b, wout, tb=True, tm=1024, tn=1024, name="mix_dy")
    dwout = _matmul(jnp.concatenate([y_swa, y_gla, y_mem], axis=1), dx2b, ta=True, tm=512, tn=1024, out_dtype=BF16,
                    name="mix_dw_out")
    dq_swa, dkv_swa, dsqg, dskg, dsink, drb = _swa_bwd(p, sqg, skg, sinks, rel_bias, dy_all, name="swa_bwd")
    dqkvg, dz, dbg, dgain256 = _gla_bwd(p, z, bg, gain256, oraw, stsave, dy_all, name="gla_bwd")
    dmq, dkv_mem, dmqg, dmkg = _mem_bwd(p, kv, mqg, mkg, dy_all, tq=512, name="mem_bwd")
    dglr = _matmul(dz, wgu_pad, tb=True, tm=1024, tn=GLA_QK_W, name="gla_gate_dx")
    dwgu_pad = _matmul(p_glr, dz, ta=True, tm=GLA_QK_W, tn=GLA_QK_W, name="gla_gate_dw")
    dp = jnp.concatenate([dq_swa, dkv_swa, dqkvg, dmq, dglr], axis=1)
    dh = _matmul(dp, win_p, tb=True, tm=1024, tn=1024, name="mix_dh")
    dwin_p = _matmul(h, dp, ta=True, tm=1024, tn=640, out_dtype=BF16, name="mix_dw_in")
    dx1, dx1b, dgmix = _rms_bwd(x1, gmix, dh, dx2, tm=512, name="mix_drms")
    dwkv = _matmul(hm, dkv_mem, ta=True, tm=512, tn=512, out_dtype=BF16, name="mem_dw_kv")
    dx1b = on_grads("mix", (dwin_p, dwkv, dwout), dx1b)
    dhm = _matmul(dkv_mem, wkv, tb=True, tm=256, tn=512, name="mem_dh")
    _, _, dgmem = _rms_bwd(mem, gmem, dhm, None, tm=256, name="mem_drms")
    dh1, dw3_1a = _ffn_bwd_part(dx1b, w3_1, saved1, 0, tiles // 2, None, name="ffn1_bwd_a")
    dgla_gain = dgain256.reshape(GLA_HEADS, GLA_DV).sum(axis=0, keepdims=True)
    dsmall = [dgmix, dgmem, dg2, dsqg, dskg, dsink[:, :SWA_HEADS], drb[:, :SWA_HEADS], dwgu_pad[:GLA_RANK], dbg,
              dgla_gain, dmqg, dmkg, loss]
    dh1, dgmem, dwgu_pad = on_grads("ffn1a", [dw3_1a[:, None]], (dh1, dgmem, dwgu_pad), small=dsmall)
    dh1, dw3_1b = _ffn_bwd_part(dx1b, w3_1, saved1, tiles // 2, tiles // 2, dh1, name="ffn1_bwd_b")
    dx, _, dg1 = _rms_bwd(x, g1, dh1, dx1, tm=512, name="ffn1_drms")
    on_grads("ffn1b", [dw3_1b[:, None]], None, small=[dg1])
    return dx


def _mesh_place():
    x, y, c = lax.axis_index("x"), lax.axis_index("y"), lax.axis_index("c")
    other_chips = [(1 - x, y), (x, 1 - y), (1 - x, 1 - y)]
    return x, y, c, other_chips


def _handshake(peers):
    barrier = pltpu.get_barrier_semaphore()
    for peer in peers:
        pl.semaphore_signal(barrier, inc=1, device_id=peer, device_id_type=MESH)
    pl.semaphore_wait(barrier, len(peers))


def _sequencer_call(body, operands, out_shapes, sems, *, name, collective_id):
    return pl.kernel(
        body, name=name, out_type=out_shapes, mesh=plsc.ScalarSubcoreMesh(axis_name="sequencer", num_cores=1),
        scratch_types=sems, compiler_params=pltpu.CompilerParams(collective_id=collective_id),
    )(*operands)


def _window(ref, kind, slot, shape):
    if kind == "row":
        rows = pl.ds(pl.multiple_of(slot * shape[-2], 8), shape[-2])
        return ref.at[(slice(None),) * (len(shape) - 2) + (rows,)]
    return ref.at[slot]


def _gathered(shape, kind):
    if kind == "row":
        return tuple(shape[:-2]) + (N_DEV * shape[-2], shape[-1])
    return (N_DEV,) + tuple(shape)


def _all_gather(shards, kinds, *, name, collective_id):
    nt = len(shards)

    def body(*refs):
        x_refs, o_refs = refs[:nt], refs[nt:2 * nt]
        send_sems, recv_sems, local_sems = refs[2 * nt:]
        x, y, c, chips = _mesh_place()
        me, sibling = (x, y, c), (x, y, 1 - c)
        _handshake([sibling] + [(*chip, c) for chip in chips])

        def copy(k, t, block, to, from_shard=False):
            bx, by, bc = block
            rows = _window(o_refs[t], kinds[t], 4 * bx + 2 * by + bc, shards[t].shape)
            return pltpu.make_async_remote_copy(
                src_ref=x_refs[t] if from_shard else rows, dst_ref=rows,
                send_sem=send_sems.at[k, t], recv_sem=recv_sems.at[k, t], device_id=to, device_id_type=MESH)

        mine = [pltpu.make_async_copy(x_refs[t], _window(o_refs[t], kinds[t], 4 * x + 2 * y + c, shards[t].shape),
                                      local_sems.at[t]) for t in range(nt)]
        for cp in mine:
            cp.start()
        first = [copy(0, t, me, sibling, True) for t in range(nt)]
        first += [copy(1 + j, t, me, (*chip, c), True) for j, chip in enumerate(chips) for t in range(nt)]
        for cp in first:
            cp.start()
        passed = []
        for j, chip in enumerate(chips):
            for t in range(nt):
                copy(1 + j, t, (*chip, c), me).wait_recv()
                fwd = copy(4 + j, t, (*chip, c), sibling)
                fwd.start()
                passed.append(fwd)
        for t in range(nt):
            copy(0, t, sibling, me).wait_recv()
        for j, chip in enumerate(chips):
            for t in range(nt):
                copy(4 + j, t, (*chip, 1 - c), me).wait_recv()
        for cp in first + passed:
            cp.wait_send()
        for cp in mine:
            cp.wait()

    return _sequencer_call(
        body, shards, [SDS(_gathered(s.shape, k), s.dtype) for s, k in zip(shards, kinds)],
        [pltpu.SemaphoreType.DMA((7, nt)), pltpu.SemaphoreType.DMA((7, nt)), pltpu.SemaphoreType.DMA((nt,))],
        name=name, collective_id=collective_id)


def _part_shape(shape, kind):
    if kind == "row":
        return tuple(shape[:-2]) + (shape[-2] // N_DEV, shape[-1])
    return tuple(shape[2:])


def _pair_exchange(grads, kinds, *, name, collective_id):
    nt = len(grads)
    part = [_part_shape(g.shape, k) for g, k in zip(grads, kinds)]

    def body(*refs):
        g_refs, o_refs = refs[:nt], refs[nt:2 * nt]
        send_sems, recv_sems = refs[2 * nt:]
        x, y, c, _ = _mesh_place()
        _handshake([(x, y, 1 - c)])
        copies = []
        for t in range(nt):
            for xy in range(4):
                src = g_refs[t].at[1 - c, xy] if kinds[t] == "stack" else _window(g_refs[t], kinds[t], 2 * xy + 1 - c, part[t])
                copies.append(pltpu.make_async_remote_copy(
                    src_ref=src, dst_ref=o_refs[t].at[xy], send_sem=send_sems.at[xy, t], recv_sem=recv_sems.at[xy, t],
                    device_id=(x, y, 1 - c), device_id_type=MESH))
        for cp in copies:
            cp.start()
        for cp in copies:
            cp.wait()

    return _sequencer_call(
        body, grads, [SDS((4,) + p, g.dtype) for p, g in zip(part, grads)],
        [pltpu.SemaphoreType.DMA((4, nt)), pltpu.SemaphoreType.DMA((4, nt))], name=name, collective_id=collective_id)


def _chip_exchange(parts, small, *, name, collective_id):
    nt = len(parts)
    if small is None:
        def body_plain(*refs):
            s_refs, o_refs = refs[:nt], refs[nt:2 * nt]
            send_sems, recv_sems = refs[2 * nt:]
            x, y, c, chips = _mesh_place()
            _handshake([(*chip, c) for chip in chips])
            copies = [pltpu.make_async_remote_copy(
                src_ref=s_refs[t].at[2 * chip[0] + chip[1]], dst_ref=o_refs[t].at[j],
                send_sem=send_sems.at[j, t], recv_sem=recv_sems.at[j, t], device_id=(*chip, c), device_id_type=MESH)
                for j, chip in enumerate(chips) for t in range(nt)]
            for cp in copies:
                cp.start()
            for cp in copies:
                cp.wait()

        return _sequencer_call(
            body_plain, parts, [SDS((3,) + s.shape[1:], s.dtype) for s in parts],
            [pltpu.SemaphoreType.DMA((3, nt)), pltpu.SemaphoreType.DMA((3, nt))], name=name, collective_id=collective_id)

    def body(*refs):
        s_refs, small_ref = refs[:nt], refs[nt]
        o_refs, small_all = refs[nt + 1:2 * nt + 1], refs[2 * nt + 1]
        send_sems, recv_sems, small_send, small_recv, local_sem = refs[2 * nt + 2:]
        x, y, c, chips = _mesh_place()
        _handshake([(px, py, pc) for px in (x, 1 - x) for py in (y, 1 - y) for pc in (c, 1 - c)][1:])

        def copy(j, t, chip):
            return pltpu.make_async_remote_copy(
                src_ref=s_refs[t].at[2 * chip[0] + chip[1]], dst_ref=o_refs[t].at[j],
                send_sem=send_sems.at[j, t], recv_sem=recv_sems.at[j, t], device_id=(*chip, c), device_id_type=MESH)

        flips = [(fx, fy, fc) for fx in (0, 1) for fy in (0, 1) for fc in (0, 1)][1:]

        def small_copy(k):
            fx, fy, fc = flips[k]
            to = (x ^ fx if fx else x, y ^ fy if fy else y, c ^ fc if fc else c)
            rows = small_all.at[4 * x + 2 * y + c]
            return pltpu.make_async_remote_copy(
                src_ref=small_ref, dst_ref=rows, send_sem=small_send.at[k], recv_sem=small_recv.at[k],
                device_id=to, device_id_type=MESH)

        own = pltpu.make_async_copy(small_ref, small_all.at[4 * x + 2 * y + c], local_sem)
        own.start()
        copies = [copy(j, t, chip) for j, chip in enumerate(chips) for t in range(nt)]
        smalls = [small_copy(k) for k in range(7)]
        for cp in smalls + copies:
            cp.start()
        for cp in smalls + copies:
            cp.wait()
        own.wait()

    return _sequencer_call(
        body, list(parts) + [small],
        [SDS((3,) + s.shape[1:], s.dtype) for s in parts] + [SDS((N_DEV,) + small.shape, small.dtype)],
        [pltpu.SemaphoreType.DMA((3, nt)), pltpu.SemaphoreType.DMA((3, nt)),
         pltpu.SemaphoreType.DMA((7,)), pltpu.SemaphoreType.DMA((7,)), pltpu.SemaphoreType.DMA],
        name=name, collective_id=collective_id)


def _pair_sum(grad, theirs, kind, c, *, name):
    if kind == "row":
        r, l = theirs.shape[-2:]
        n = theirs.size // (4 * r * l)
        grad, theirs = grad.reshape(n, N_DEV * r, l), theirs.reshape(4, n, r, l)
        mine_spec = pl.BlockSpec((n, r, l), lambda xy, c_ref: (0, 2 * xy + c_ref[0], 0))
    else:
        r, l = theirs.shape[-2:]
        n = theirs.size // (4 * r * l)
        theirs = theirs.reshape(4, n, r, l)
        grad = grad.reshape(2, 4, n, r, l)
        mine_spec = pl.BlockSpec((None, None, n, r, l), lambda xy, c_ref: (c_ref[0], xy, 0, 0, 0))

    def body(c_ref, a_ref, b_ref, o_ref):
        o_ref[...] = (a_ref[...].astype(F32) + b_ref[...].astype(F32)).astype(BF16)

    part = pl.BlockSpec((None, n, r, l), lambda xy, c_ref: (xy, 0, 0, 0))
    return pl.pallas_call(
        body, name=name,
        grid_spec=pltpu.PrefetchScalarGridSpec(num_scalar_prefetch=1, grid=(4,), in_specs=[mine_spec, part], out_specs=part),
        out_shape=SDS((4, n, r, l), BF16), compiler_params=_params("parallel"),
    )(c, grad, theirs)


def _adamw(w, g, m, v):
    m = ADAM_B1 * m + (1.0 - ADAM_B1) * g
    v = ADAM_B2 * v + (1.0 - ADAM_B2) * jnp.square(g)
    m_hat = m / (1.0 - ADAM_B1 ** ADAM_STEP)
    v_hat = v / (1.0 - ADAM_B2 ** ADAM_STEP)
    delta = -ADAM_LR * (m_hat / (jnp.sqrt(v_hat) + ADAM_EPS) + ADAM_WD * w)
    return delta, m, v


def _adam_big(owns, others, mat, xy, w, m, v, *, tr, name):
    nw, r, l = w.shape
    lp = owns[0].shape[-1]
    nq = len(owns)
    assert nq in (1, nw)

    def body(xy_ref, *refs):
        own_refs, oth_refs = refs[:nq], refs[nq:2 * nq]
        w_ref, m_ref, v_ref, g_out, d_out, m_out, v_out = refs[2 * nq:]
        g = None
        for q in range(nq):
            gq = own_refs[q][0, 0].astype(F32)
            for j in range(3):
                gq = gq + oth_refs[q][j, 0].astype(F32)
            g = gq if g is None else jnp.where(pl.program_id(0) == q, gq, g)
        g = g[:, :l]
        delta, m_new, v_new = _adamw(w_ref[0], g, m_ref[0], v_ref[0])
        g_out[0] = g
        d_out[0] = delta
        m_out[0] = m_new
        v_out[0] = v_new

    def at(p):
        return mat * nw + p if nq == 1 else mat

    blk = pl.BlockSpec((1, tr, l), lambda p, i, xy_ref: (p, i, 0))
    return pl.pallas_call(
        body, name=name,
        grid_spec=pltpu.PrefetchScalarGridSpec(
            num_scalar_prefetch=1, grid=(nw, r // tr),
            in_specs=[pl.BlockSpec((1, 1, tr, lp), lambda p, i, xy_ref: (xy_ref[0], at(p), i, 0))] * nq
            + [pl.BlockSpec((3, 1, tr, lp), lambda p, i, xy_ref: (0, at(p), i, 0))] * nq + [blk, blk, blk],
            out_specs=[blk, blk, blk, blk]),
        out_shape=[SDS(w.shape, F32)] * 4, compiler_params=_params("parallel", "parallel"),
    )(xy, *owns, *others, w, m, v)


def _small_layout(shapes):
    out, at = [], 0
    for r, c in shapes:
        rows = c // 128 if (r == 1 and c > 128) else r
        out.append((at, rows))
        at += -(-rows // 8) * 8
    return out, at


def _pack_small(parts, *, name):
    shapes = [a.shape for a in parts]
    layout, total = _small_layout(shapes)

    def body(*refs):
        o_ref = refs[-1]
        o_ref[...] = jnp.zeros_like(o_ref)
        for x_ref, (r, c), (at, rows) in zip(refs, shapes, layout):
            if r == 1 and c > 128:
                for k in range(rows):
                    o_ref[at + k:at + k + 1, :] = x_ref[:, k * 128:(k + 1) * 128]
            else:
                o_ref[at:at + r, 0:c] = x_ref[...]

    return pl.pallas_call(body, name=name, out_shape=SDS((total, 128), F32))(*parts)


def _adam_small(g_all, ws, ms, vs, *, name):
    n = len(ws)
    shapes = [w.shape for w in ws]
    layout, _ = _small_layout(shapes)

    def body(g_ref, *refs):
        w_refs, m_refs, v_refs, outs = refs[:n], refs[n:2 * n], refs[2 * n:3 * n], refs[3 * n:]
        g_sum = g_ref[0]
        for k in range(1, N_DEV):
            g_sum = g_sum + g_ref[k]
        for i, ((r, c), (at, rows)) in enumerate(zip(shapes, layout)):
            if r == 1 and c > 128:
                g = jnp.concatenate([g_sum[at + k:at + k + 1, :] for k in range(rows)], axis=1)
            else:
                g = g_sum[at:at + r, 0:c]
            delta, m_new, v_new = _adamw(w_refs[i][...], g, m_refs[i][...], v_refs[i][...])
            for q, val in enumerate((g, delta, m_new, v_new)):
                outs[4 * i + q][...] = val

    flat = pl.pallas_call(body, name=name, out_shape=[SDS(s, F32) for s in shapes for _ in range(4)])(g_all, *ws, *ms, *vs)
    return [flat[4 * i:4 * i + 4] for i in range(n)]


def kernel(x, mem, ffn1_norm, ffn1_w_gate, ffn1_w_up, ffn1_w_down, mix_norm, mem_norm, w_in, w_mem_kv, swa_q_norm, swa_k_norm, swa_sinks, rel_bias, gla_w_gate_up, gla_b_gate, gla_out_norm, mem_q_norm, mem_k_norm, w_out, ffn2_norm, ffn2_w_gate, ffn2_w_up, ffn2_w_down, loss_target, m_ffn1_norm, m_ffn1_w_gate, m_ffn1_w_up, m_ffn1_w_down, m_mix_norm, m_mem_norm, m_w_in, m_w_mem_kv, m_swa_q_norm, m_swa_k_norm, m_swa_sinks, m_rel_bias, m_gla_w_gate_up, m_gla_b_gate, m_gla_out_norm, m_mem_q_norm, m_mem_k_norm, m_w_out, m_ffn2_norm, m_ffn2_w_gate, m_ffn2_w_up, m_ffn2_w_down, v_ffn1_norm, v_ffn1_w_gate, v_ffn1_w_up, v_ffn1_w_down, v_mix_norm, v_mem_norm, v_w_in, v_w_mem_kv, v_swa_q_norm, v_swa_k_norm, v_swa_sinks, v_rel_bias, v_gla_w_gate_up, v_gla_b_gate, v_gla_out_norm, v_mem_q_norm, v_mem_k_norm, v_w_out, v_ffn2_norm, v_ffn2_w_gate, v_ffn2_w_up, v_ffn2_w_down):
    xi, yi, ci = lax.axis_index("x"), lax.axis_index("y"), lax.axis_index("c")
    c_arr = jnp.reshape(ci, (1,)).astype(jnp.int32)
    xy_arr = jnp.reshape(2 * xi + yi, (1,)).astype(jnp.int32)
    d = x.shape[-1]

    half_h = ffn1_w_gate.shape[-1] // 2

    def gather_ffn(wg_s, wu_s, wd_s, name, collective_id, after):
        w3_s = jnp.concatenate([wg_s.transpose(0, 2, 1), wu_s.transpose(0, 2, 1), wd_s], axis=0)
        w3_s = jnp.pad(w3_s.reshape(3, 2, half_h, d), ((0, 0), (0, 0), (0, FFN_HALF_PAD - half_h), (0, 0))).astype(BF16)
        if after is not None:
            w3_s, _ = lax.optimization_barrier((w3_s, after))
        return _all_gather([w3_s], ["row"], name=name, collective_id=collective_id)[0].reshape(3, -1, d)

    w3_1 = gather_ffn(ffn1_w_gate, ffn1_w_up, ffn1_w_down, "gather_ffn1", 0, None)
    mix_s = lax.optimization_barrier((w_in[0].astype(BF16), w_mem_kv[0].astype(BF16), w_out[0].astype(BF16), w3_1))[:3]
    win_all, wkv, wout = _all_gather(list(mix_s), ["stack", "row", "row"], name="gather_mix", collective_id=1)

    def gather_ffn2(x1):
        return gather_ffn(ffn2_w_gate, ffn2_w_up, ffn2_w_down, "gather_ffn2", 2, (wout, x1))

    win_p = _pack_win(win_all, tr=256, name="pack_w_in")

    small_w = [ffn1_norm, mix_norm, mem_norm, ffn2_norm, swa_q_norm, swa_k_norm, swa_sinks[0], rel_bias,
               gla_w_gate_up[0], gla_b_gate, gla_out_norm, mem_q_norm, mem_k_norm]
    collective_ids = {"ffn2": (3, 4), "mix": (5, 6), "ffn1a": (7, 8), "ffn1b": (9, 10)}
    reduced, small_box = {}, {}

    def on_grads(group, grads, carry, small=None):
        if group == "mix":
            dwin_p, dwkv, dwout = grads
            grads = [_unpack_win(dwin_p, tr=256, name="unpack_dw_in"), dwkv, dwout]
            kinds = ["stack", "row", "row"]
        else:
            kinds = ["row"]
        if reduced:
            earlier = list(reduced.values())[-1][1]
            *grads, _ = lax.optimization_barrier((*grads, earlier[0]))
        id_pair, id_chip = collective_ids[group]
        from_sibling = _pair_exchange(grads, kinds, name=f"pair_exchange_{group}", collective_id=id_pair)
        chip_sums = [_pair_sum(g, theirs, k, c_arr, name=f"pair_sum_{group}_{t}")
                     for t, (g, theirs, k) in enumerate(zip(grads, from_sibling, kinds))]
        if carry is not None:
            *chip_sums, carry = lax.optimization_barrier((*chip_sums, carry))
        if small is None:
            from_chips = _chip_exchange(chip_sums, None, name=f"chip_exchange_{group}", collective_id=id_chip)
        else:
            packed = _pack_small(small, name=f"pack_small_{group}")
            *from_chips, small_all = _chip_exchange(chip_sums, packed, name=f"chip_exchange_{group}",
                                                    collective_id=id_chip)
            small_box[group] = small_all
        reduced[group] = (chip_sums, from_chips)
        return carry

    grad_x = _local_step(x[0], mem[0], loss_target[0], small_w, (w3_1, win_p, wkv, wout, gather_ffn2), on_grads)

    big_w = {"ffn1_w_gate": ("ffn1", 0, 0, True, ffn1_w_gate, m_ffn1_w_gate, v_ffn1_w_gate),
             "ffn1_w_up": ("ffn1", 0, 1, True, ffn1_w_up, m_ffn1_w_up, v_ffn1_w_up),
             "ffn1_w_down": ("ffn1", 0, 2, False, ffn1_w_down, m_ffn1_w_down, v_ffn1_w_down),
             "w_in": ("mix", 0, 0, False, w_in, m_w_in, v_w_in),
             "w_mem_kv": ("mix", 1, 0, False, w_mem_kv, m_w_mem_kv, v_w_mem_kv),
             "w_out": ("mix", 2, 0, False, w_out, m_w_out, v_w_out),
             "ffn2_w_gate": ("ffn2", 0, 0, True, ffn2_w_gate, m_ffn2_w_gate, v_ffn2_w_gate),
             "ffn2_w_up": ("ffn2", 0, 1, True, ffn2_w_up, m_ffn2_w_up, v_ffn2_w_up),
             "ffn2_w_down": ("ffn2", 0, 2, False, ffn2_w_down, m_ffn2_w_down, v_ffn2_w_down)}
    res = {}
    for nm, (group, t, mat, transposed, w, m, v) in big_w.items():
        shape = w.shape
        if transposed:
            w, m, v = (a.transpose(0, 2, 1) for a in (w, m, v))
        if group != "mix":
            w, m, v = (a.reshape(2, half_h, d) for a in (w, m, v))
        r = w.shape[1]
        tr = 256 if r % 256 == 0 else r
        halves = ["ffn1a", "ffn1b"] if group == "ffn1" else [group]
        out = _adam_big([reduced[k][0][t] for k in halves], [reduced[k][1][t] for k in halves], mat, xy_arr, w, m, v,
                        tr=tr, name=f"adam_{nm}")
        if transposed:
            out = [a.reshape(1, -1, d).transpose(0, 2, 1) for a in out]
        res[nm] = [a.reshape(shape) for a in out]
    small_names = ["ffn1_norm", "mix_norm", "mem_norm", "ffn2_norm", "swa_q_norm", "swa_k_norm", "swa_sinks", "rel_bias",
                   "gla_w_gate_up", "gla_b_gate", "gla_out_norm", "mem_q_norm", "mem_k_norm"]
    small_m = [m_ffn1_norm, m_mix_norm, m_mem_norm, m_ffn2_norm, m_swa_q_norm, m_swa_k_norm, m_swa_sinks, m_rel_bias,
               m_gla_w_gate_up, m_gla_b_gate, m_gla_out_norm, m_mem_q_norm, m_mem_k_norm]
    small_v = [v_ffn1_norm, v_mix_norm, v_mem_norm, v_ffn2_norm, v_swa_q_norm, v_swa_k_norm, v_swa_sinks, v_rel_bias,
               v_gla_w_gate_up, v_gla_b_gate, v_gla_out_norm, v_mem_q_norm, v_mem_k_norm]
    small_full = [ffn1_norm, mix_norm, mem_norm, ffn2_norm, swa_q_norm, swa_k_norm, swa_sinks, rel_bias,
                  gla_w_gate_up, gla_b_gate, gla_out_norm, mem_q_norm, mem_k_norm]
    zero = jnp.zeros((1, 1), F32)
    two_d = lambda a: a.reshape(a.shape[-2:])
    for group, sel in (("ffn1a", slice(1, None)), ("ffn1b", slice(0, 1))):
        extra = [zero] if group == "ffn1a" else []
        ws, ms, vs = ([two_d(a) for a in arrs[sel]] + extra for arrs in (small_full, small_m, small_v))
        updated = _adam_small(small_box[group], ws, ms, vs, name=f"adam_small_{group}")
        for nm, full, out in zip(small_names[sel], small_full[sel], updated):
            res[nm] = [a.reshape(full.shape) for a in out]
        if extra:
            loss = updated[-1][0].reshape(())

    order = ["ffn1_norm", "ffn1_w_gate", "ffn1_w_up", "ffn1_w_down", "mix_norm", "mem_norm", "w_in", "w_mem_kv",
             "swa_q_norm", "swa_k_norm", "swa_sinks", "rel_bias", "gla_w_gate_up", "gla_b_gate", "gla_out_norm",
             "mem_q_norm", "mem_k_norm", "w_out", "ffn2_norm", "ffn2_w_gate", "ffn2_w_up", "ffn2_w_down"]
    outs = [loss, grad_x[None]]
    for q in range(4):
        outs += [res[nm][q] for nm in order]
    return tuple(outs)
```

```python
import functools
import math

import numpy as np
import jax
import jax.numpy as jnp
from jax import lax
from jax.experimental import pallas as pl
from jax.experimental.pallas import tpu as pltpu
from jax.experimental.pallas import tpu_sc as plsc

F32 = jnp.float32
BF16 = jnp.bfloat16
SDS = jax.ShapeDtypeStruct

EPS = 1e-6
HEAD_DIM = 64
SWA_HEADS = 8
SWA_KV_HEADS = 2
SWA_GROUP = SWA_HEADS // SWA_KV_HEADS
BLOCK = 128
N_BUCKETS = 32
MAX_DISTANCE = 128
GLA_HEADS = 4
GLA_DK = 32
GLA_DV = 64
GLA_RANK = 16
GLA_TAU = 16.0
GLA_CHUNK = 32
MEM_HEADS = 4
SWA_Q_W = SWA_HEADS * HEAD_DIM
SWA_KV_W = SWA_KV_HEADS * HEAD_DIM
GLA_QK_W = GLA_HEADS * GLA_DK
GLA_V_W = GLA_HEADS * GLA_DV
MEM_Q_W = MEM_HEADS * HEAD_DIM
IN_W = 1808
IN_W_PAD = 1920
COL_SQ, COL_SKV, COL_GQ, COL_GK, COL_GV, COL_GG, COL_MQ, COL_GLR = 0, 512, 768, 896, 1024, 1280, 1536, 1792

ADAM_LR = 0.001
ADAM_B1 = 0.9
ADAM_B2 = 0.999
ADAM_EPS = 1e-08
ADAM_WD = 0.01
ADAM_STEP = 10

N_DEV = 8
VMEM_LIMIT_BYTES = 56 * 1024 * 1024
MESH = pl.DeviceIdType.MESH


def _params(*sem):
    return pltpu.CompilerParams(dimension_semantics=sem or None, vmem_limit_bytes=VMEM_LIMIT_BYTES)


def _dot(a, b, ta, tb, precision=None):
    dims = (((0 if ta else 1,), (1 if tb else 0,)), ((), ()))
    return lax.dot_general(a, b, dims, preferred_element_type=F32, precision=precision)


def _mm_raw(a, b, ta=False, tb=False):
    return _dot(a.astype(BF16), b.astype(BF16), ta, tb)


def _mmf_raw(a, b, ta=False, tb=False):
    return _dot(a, b, ta, tb, lax.Precision.HIGHEST)


def _make_mm(raw):
    @functools.partial(jax.custom_vjp, nondiff_argnums=(2, 3))
    def mm(a, b, ta=False, tb=False):
        return raw(a, b, ta, tb)

    def fwd(a, b, ta, tb):
        return raw(a, b, ta, tb), (a, b)

    def bwd(ta, tb, res, g):
        a, b = res
        da = raw(b, g, tb, True) if ta else raw(g, b, False, not tb)
        db = raw(g, a, True, ta) if tb else raw(a, g, not ta, False)
        return da, db

    mm.defvjp(fwd, bwd)
    return mm


_mm = _make_mm(_mm_raw)
_mmf = _make_mm(_mmf_raw)


def _mm3(a, b, ta=False, tb=False):
    a_hi, b_hi = a.astype(BF16).astype(F32), b.astype(BF16).astype(F32)
    return _mm(a_hi, b_hi, ta, tb) + _mm(a_hi, b - b_hi, ta, tb) + _mm(a - a_hi, b_hi, ta, tb)


def _rms(x, g):
    return x * lax.rsqrt(jnp.mean(x * x, axis=-1, keepdims=True) + EPS) * g


def _silu_mul(g, u):
    return jax.nn.silu(g) * u


def _log_sigmoid(z):
    return jnp.minimum(z, 0.0) - jnp.log(1.0 + jnp.exp(-jnp.abs(z)))


def _matmul(a_list, b, *, ta=False, tb=False, tm, tn, b_blocks=None, res=None, scale=1.0, out_dtype=F32, name):
    if not isinstance(a_list, (list, tuple)):
        a_list = [a_list]
    n_a = len(a_list)
    m = a_list[0].shape[1] if ta else a_list[0].shape[0]
    ks = [a.shape[0] if ta else a.shape[1] for a in a_list]
    n = b.shape[0] if tb else b.shape[1]
    if b_blocks is None:
        assert n_a == 1
        b_blocks = [0]
    tm, tn = min(tm, m), min(tn, n)
    assert m % tm == 0 and n % tn == 0, (m, n, tm, tn)

    def body(*refs):
        a_refs, b_refs = refs[:n_a], refs[n_a:2 * n_a]
        r_ref = refs[2 * n_a] if res is not None else None
        o_ref = refs[-1]
        acc = _mm_raw(a_refs[0][...], b_refs[0][...], ta, tb)
        for k in range(1, n_a):
            acc = acc + _mm_raw(a_refs[k][...], b_refs[k][...], ta, tb)
        if scale != 1.0:
            acc = acc * scale
        if r_ref is not None:
            acc = r_ref[...] + acc
        o_ref[...] = acc.astype(out_dtype)

    in_specs = []
    for k in ks:
        in_specs.append(pl.BlockSpec((k, tm), lambda i, j: (0, i)) if ta else pl.BlockSpec((tm, k), lambda i, j: (i, 0)))
    for k, blk in zip(ks, b_blocks):
        if tb:
            in_specs.append(pl.BlockSpec((tn, k), functools.partial(lambda i, j, blk: (j, blk), blk=blk)))
        else:
            in_specs.append(pl.BlockSpec((k, tn), functools.partial(lambda i, j, blk: (blk, j), blk=blk)))
    args = list(a_list) + [b] * n_a
    if res is not None:
        in_specs.append(pl.BlockSpec((tm, tn), lambda i, j: (i, j)))
        args.append(res)
    return pl.pallas_call(
        body, name=name, grid=(m // tm, n // tn), in_specs=in_specs,
        out_specs=pl.BlockSpec((tm, tn), lambda i, j: (i, j)), out_shape=SDS((m, n), out_dtype),
        compiler_params=_params("parallel", "parallel"),
    )(*args)


def _win_pieces(w):
    glr_lo, glr_hi = COL_MQ, COL_MQ + GLA_RANK
    out = []
    for j in range(N_DEV):
        for lo, hi, shift in ((0, glr_lo, 0), (glr_lo, glr_hi, COL_GLR - glr_lo), (glr_hi, IN_W, COL_MQ - glr_hi)):
            s, e = max(j * w, lo), min((j + 1) * w, hi)
            if s < e:
                out.append((j, s - j * w, e - j * w, s + shift))
    return out


def _pack_win(win_all, *, tr, name):
    _, d, w = win_all.shape

    def body(i_ref, o_ref):
        for j, a, b, dst in _win_pieces(w):
            o_ref[:, dst:dst + b - a] = i_ref[j][:, a:b]
        o_ref[:, IN_W:] = jnp.zeros((tr, IN_W_PAD - IN_W), o_ref.dtype)

    return pl.pallas_call(
        body, name=name, grid=(d // tr,), in_specs=[pl.BlockSpec((N_DEV, tr, w), lambda i: (0, i, 0))],
        out_specs=pl.BlockSpec((tr, IN_W_PAD), lambda i: (i, 0)), out_shape=SDS((d, IN_W_PAD), win_all.dtype),
        compiler_params=_params("parallel"),
    )(win_all)


def _unpack_win(dwin_p, *, tr, name):
    d = dwin_p.shape[0]
    w = IN_W // N_DEV

    def body(i_ref, o_ref):
        for j, a, b, src in _win_pieces(w):
            o_ref[j % 2, j // 2, :, a:b] = i_ref[:, src:src + b - a]

    return pl.pallas_call(
        body, name=name, grid=(d // tr,), in_specs=[pl.BlockSpec((tr, IN_W_PAD), lambda i: (i, 0))],
        out_specs=pl.BlockSpec((2, 4, tr, w), lambda i: (0, 0, i, 0)), out_shape=SDS((2, 4, d, w), dwin_p.dtype),
        compiler_params=_params("parallel"),
    )(dwin_p)


def _rms_fwd(x, g, *, tm, name):
    s, d = x.shape

    def body(x_ref, g_ref, h_ref):
        h_ref[...] = _rms(x_ref[...], g_ref[...]).astype(BF16)

    return pl.pallas_call(
        body, name=name, grid=(s // tm,),
        in_specs=[pl.BlockSpec((tm, d), lambda i: (i, 0)), pl.BlockSpec((1, d), lambda i: (0, 0))],
        out_specs=pl.BlockSpec((tm, d), lambda i: (i, 0)), out_shape=SDS((s, d), BF16),
        compiler_params=_params("parallel"),
    )(x, g)


def _rms_bwd(x, g, dh, dres, *, tm, name):
    s, d = x.shape
    want_dx = dres is not None

    def body(*refs):
        if want_dx:
            x_ref, g_ref, dh_ref, dres_ref, dx_ref, dxb_ref, dg_ref = refs
        else:
            x_ref, g_ref, dh_ref, dg_ref = refs
        _, vjp = jax.vjp(_rms, x_ref[...], g_ref[...])
        dx, dg = vjp(dh_ref[...])
        if want_dx:
            dx = dres_ref[...] + dx
            dx_ref[...] = dx
            dxb_ref[...] = dx.astype(BF16)

        @pl.when(pl.program_id(0) == 0)
        def _():
            dg_ref[...] = jnp.zeros_like(dg_ref)

        dg_ref[...] += dg

    row = pl.BlockSpec((tm, d), lambda i: (i, 0))
    vec = pl.BlockSpec((1, d), lambda i: (0, 0))
    if want_dx:
        return pl.pallas_call(
            body, name=name, grid=(s // tm,), in_specs=[row, vec, row, row], out_specs=[row, row, vec],
            out_shape=[SDS((s, d), F32), SDS((s, d), BF16), SDS((1, d), F32)], compiler_params=_params("arbitrary"),
        )(x, g, dh, dres)
    return None, None, pl.pallas_call(
        body, name=name, grid=(s // tm,), in_specs=[row, vec, row], out_specs=vec,
        out_shape=SDS((1, d), F32), compiler_params=_params("arbitrary"),
    )(x, g, dh)


FFN_TN = 256
FFN_TN_FWD = 512
FFN_HALF_PAD = 192


def _ffn_fwd(x, gain, w3, tag, *, tm=1024):
    s, d = x.shape
    f = w3.shape[1]
    tn = FFN_TN_FWD
    nj = f // tn
    tm = min(tm, s)

    def body(x_ref, gain_ref, wg_ref, wu_ref, wd_ref, y_ref, h_ref, g_ref, u_ref, acc_s):
        j = pl.program_id(1)

        @pl.when(j == 0)
        def _():
            h_ref[...] = _rms(x_ref[...], gain_ref[...]).astype(BF16)
            acc_s[...] = jnp.zeros_like(acc_s)

        hv = h_ref[...]
        g = _mm_raw(hv, wg_ref[...], False, True)
        u = _mm_raw(hv, wu_ref[...], False, True)
        g_ref[...] = g.astype(BF16)
        u_ref[...] = u.astype(BF16)
        acc_s[...] += _mm_raw(_silu_mul(g, u), wd_ref[...])

        @pl.when(j == nj - 1)
        def _():
            y_ref[...] = x_ref[...] + 0.5 * acc_s[...]

    row = pl.BlockSpec((tm, d), lambda i, j: (i, 0))
    tile = pl.BlockSpec((tm, tn), lambda i, j: (i, j))
    y, h, g, u = pl.pallas_call(
        body, name=f"{tag}_fwd", grid=(s // tm, nj),
        in_specs=[row, pl.BlockSpec((1, d), lambda i, j: (0, 0))]
        + [pl.BlockSpec((None, tn, d), functools.partial(lambda i, j, k: (k, j, 0), k=k)) for k in range(3)],
        out_specs=[row, row, tile, tile],
        out_shape=[SDS((s, d), F32), SDS((s, d), BF16), SDS((s, f), BF16), SDS((s, f), BF16)],
        scratch_shapes=[pltpu.VMEM((tm, d), F32)],
        compiler_params=_params("parallel", "arbitrary"),
    )(x, gain, w3, w3, w3)
    return y, (h, g, u)


def _ffn_bwd_part(dyb, w3, saved, first, count, dh_init, *, name):
    h, g, u = saved
    s, d = h.shape
    tn = FFN_TN

    def body(*refs):
        if dh_init is None:
            dy_ref, h_ref, wd_ref, wg_ref, wu_ref, g_ref, u_ref, dh_ref, dw3_ref, dg_s, du_s, a_s = refs
        else:
            dy_ref, h_ref, wd_ref, wg_ref, wu_ref, g_ref, u_ref, dh0_ref, dh_ref, dw3_ref, dg_s, du_s, a_s = refs
        j = pl.program_id(0)

        @pl.when(j == 0)
        def _():
            dh_ref[...] = jnp.zeros_like(dh_ref) if dh_init is None else dh0_ref[...]
            for ref in (dg_s, du_s, a_s):
                ref[...] = jnp.zeros_like(ref)

        now, before = j % 2, 1 - j % 2
        dyv = dy_ref[...]
        hv = h_ref[...]
        dg, du, a = dg_s[before], du_s[before], a_s[before]
        dh_ref[...] += _mm_raw(dg, wg_ref[...]) + _mm_raw(du, wu_ref[...])
        dw3_ref[0] = _mm_raw(dg, hv, True, False).astype(BF16)
        dw3_ref[1] = _mm_raw(du, hv, True, False).astype(BF16)
        dw3_ref[2] = (_mm_raw(a, dyv, True, False) * 0.5).astype(BF16)

        da = _mm_raw(dyv, wd_ref[...], False, True) * 0.5
        a, vjp = jax.vjp(_silu_mul, g_ref[...].astype(F32), u_ref[...].astype(F32))
        dg, du = vjp(da)
        dg_s[now] = dg.astype(BF16)
        du_s[now] = du.astype(BF16)
        a_s[now] = a.astype(BF16)

    this = lambda j: first + jnp.minimum(j, count - 1)
    last = lambda j: first + jnp.maximum(j - 1, 0)
    full = pl.BlockSpec((s, d), lambda j: (0, 0))
    once = pl.BlockSpec((s, d), lambda j: (0, 0), pipeline_mode=pl.Buffered(1))
    tile = pl.BlockSpec((s, tn), lambda j: (0, this(j)))
    in_specs = [once, once, pl.BlockSpec((None, tn, d), lambda j: (2, this(j), 0)),
                pl.BlockSpec((None, tn, d), lambda j: (0, last(j), 0)), pl.BlockSpec((None, tn, d), lambda j: (1, last(j), 0)),
                tile, tile]
    args = [dyb, h, w3, w3, w3, g, u]
    if dh_init is not None:
        in_specs.append(once)
        args.append(dh_init)
    return pl.pallas_call(
        body, name=name, grid=(count + 1,), in_specs=in_specs,
        out_specs=[full, pl.BlockSpec((3, tn, d), lambda j: (0, jnp.maximum(j - 1, 0), 0))],
        out_shape=[SDS((s, d), F32), SDS((3, count * tn, d), BF16)],
        scratch_shapes=[pltpu.VMEM((2, s, tn), BF16)] * 3,
        compiler_params=_params("arbitrary"),
    )(*args)


def _loss_bwd(y, target, *, tm, name):
    s, d = y.shape

    def body(y_ref, t_ref, dy_ref, dyb_ref, l_ref):
        diff = y_ref[...] - t_ref[...]
        dy_ref[...] = diff * (1.0 / d)
        dyb_ref[...] = (diff * (1.0 / d)).astype(BF16)

        @pl.when(pl.program_id(0) == 0)
        def _():
            l_ref[...] = jnp.zeros_like(l_ref)

        l_ref[...] += 0.5 * jnp.sum(jnp.mean(diff * diff, axis=-1, keepdims=True), axis=0, keepdims=True)

    row = pl.BlockSpec((tm, d), lambda i: (i, 0))
    return pl.pallas_call(
        body, name=name, grid=(s // tm,), in_specs=[row, row],
        out_specs=[row, row, pl.BlockSpec((1, 1), lambda i: (0, 0))],
        out_shape=[SDS((s, d), F32), SDS((s, d), BF16), SDS((1, 1), F32)],
        compiler_params=_params("arbitrary"),
    )(y, target)


def _bucket_table():
    qi = np.arange(BLOCK)[:, None]
    kj = np.arange(2 * BLOCK)[None, :]
    dist = np.maximum(qi + BLOCK - kj, 0)
    max_exact = N_BUCKETS // 2
    d = np.maximum(dist, 1).astype(np.float32)
    large = max_exact + (np.log(d / np.float32(max_exact)) / np.float32(math.log(MAX_DISTANCE / max_exact))
                         * np.float32(N_BUCKETS - max_exact)).astype(np.int32)
    large = np.minimum(large, N_BUCKETS - 1)
    band = np.where(dist < max_exact, dist, large).astype(np.int32)
    return np.where(np.tril(np.ones((BLOCK, BLOCK), bool)), band[:, BLOCK:], band[:, :BLOCK])


SWA_STACK = SWA_GROUP * BLOCK


def _swa_masks(n):
    qi = lax.broadcasted_iota(jnp.int32, (SWA_STACK, BLOCK), 0) % BLOCK
    kj = lax.broadcasted_iota(jnp.int32, (SWA_STACK, BLOCK), 1)
    own = kj <= qi
    return own, own | (n > 0)


def _swa_group(q, kp, kc, vp, vc, qg, kg, sink, bias, own, valid):
    qn = _rms(q, qg)
    s = jnp.where(own, _mm(qn, _rms(kc, kg), False, True), _mm(qn, _rms(kp, kg), False, True))
    s = s * (HEAD_DIM ** -0.5) + bias
    s = jnp.where(valid, s, -jnp.inf)
    m = lax.stop_gradient(jnp.maximum(jnp.max(s, axis=-1, keepdims=True), sink))
    p = jnp.exp(s - m)
    p = p / (jnp.sum(p, axis=-1, keepdims=True) + jnp.exp(sink - m))
    return _mm(jnp.where(own, p, 0.0), vc) + _mm(jnp.where(own, 0.0, p), vp)


def _swa_bias_table(rb_ref, bucket, bias_s):
    for h in range(SWA_HEADS):
        acc = jnp.zeros((BLOCK, BLOCK), F32)
        for b in range(N_BUCKETS):
            acc = jnp.where(bucket == b, rb_ref[b, h], acc)
        bias_s[h // SWA_GROUP, (h % SWA_GROUP) * BLOCK:(h % SWA_GROUP + 1) * BLOCK, :] = acc


def _swa_stack(ref, g):
    return jnp.concatenate([ref[:, (g * SWA_GROUP + hh) * HEAD_DIM:(g * SWA_GROUP + hh + 1) * HEAD_DIM]
                            for hh in range(SWA_GROUP)], axis=0)


def _swa_unstack(ref, g, stacked):
    for hh in range(SWA_GROUP):
        h = g * SWA_GROUP + hh
        ref[:, h * HEAD_DIM:(h + 1) * HEAD_DIM] = stacked[hh * BLOCK:(hh + 1) * BLOCK]


def _swa_sink_column(sink_ref, g):
    head = lax.broadcasted_iota(jnp.int32, (SWA_STACK, 1), 0) // BLOCK
    col = jnp.zeros((SWA_STACK, 1), F32)
    for hh in range(SWA_GROUP):
        col = jnp.where(head == hh, sink_ref[g * SWA_GROUP + hh], col)
    return col


def _swa_band(kvp_ref, kvc_ref, g):
    k = slice(g * HEAD_DIM, (g + 1) * HEAD_DIM)
    v = slice(SWA_KV_W + g * HEAD_DIM, SWA_KV_W + (g + 1) * HEAD_DIM)
    return kvp_ref[:, k], kvc_ref[:, k], kvp_ref[:, v], kvc_ref[:, v]


def _swa_specs(order):
    kvc = COL_SKV // (2 * SWA_KV_W)
    return [
        pl.BlockSpec((BLOCK, SWA_Q_W), lambda t: (order(t), 0)),
        pl.BlockSpec((BLOCK, 2 * SWA_KV_W), lambda t: (jnp.maximum(order(t) - 1, 0), kvc)),
        pl.BlockSpec((BLOCK, 2 * SWA_KV_W), lambda t: (order(t), kvc)),
        pl.BlockSpec((1, HEAD_DIM), lambda t: (0, 0)),
        pl.BlockSpec((1, HEAD_DIM), lambda t: (0, 0)),
        pl.BlockSpec(memory_space=pltpu.SMEM),
        pl.BlockSpec(memory_space=pltpu.SMEM),
        pl.BlockSpec((BLOCK, BLOCK), lambda t: (0, 0)),
    ]


def _swa_fwd(p, qg, kg, sinks, rel_bias, *, name):
    s = p.shape[0]
    nb = s // BLOCK

    def body(q_ref, kvp_ref, kvc_ref, qg_ref, kg_ref, sink_ref, rb_ref, bucket_ref, y_ref, bias_s):
        n = pl.program_id(0)

        @pl.when(n == 0)
        def _():
            _swa_bias_table(rb_ref, bucket_ref[...], bias_s)

        own, valid = _swa_masks(n)
        for g in range(SWA_KV_HEADS):
            out = _swa_group(_swa_stack(q_ref, g), *_swa_band(kvp_ref, kvc_ref, g), qg_ref[...], kg_ref[...],
                             _swa_sink_column(sink_ref, g), bias_s[g], own, valid)
            _swa_unstack(y_ref, g, out)

    return pl.pallas_call(
        body, name=name, grid=(nb,), in_specs=_swa_specs(lambda t: t),
        out_specs=pl.BlockSpec((BLOCK, SWA_Q_W), lambda t: (t, 0)), out_shape=SDS((s, SWA_Q_W), F32),
        scratch_shapes=[pltpu.VMEM((SWA_KV_HEADS, SWA_STACK, BLOCK), F32)],
        compiler_params=_params("arbitrary"),
    )(p, p, p, qg, kg, sinks, rel_bias, jnp.asarray(_bucket_table()))


def _swa_bwd(p, qg, kg, sinks, rel_bias, dy_all, *, name):
    s = p.shape[0]
    nb = s // BLOCK

    def body(q_ref, kvp_ref, kvc_ref, qg_ref, kg_ref, sink_ref, rb_ref, bucket_ref, dy_ref,
             dq_ref, dkv_ref, dqg_ref, dkg_ref, dsink_ref, drb_ref, bias_s, dbias_s, carry_s):
        t = pl.program_id(0)
        n = nb - 1 - t

        @pl.when(t == 0)
        def _():
            _swa_bias_table(rb_ref, bucket_ref[...], bias_s)
            dbias_s[...] = jnp.zeros_like(dbias_s)
            carry_s[...] = jnp.zeros_like(carry_s)
            dqg_ref[...] = jnp.zeros_like(dqg_ref)
            dkg_ref[...] = jnp.zeros_like(dkg_ref)
            dsink_ref[...] = jnp.zeros_like(dsink_ref)
            drb_ref[...] = jnp.zeros_like(drb_ref)

        own, valid = _swa_masks(n)
        lane = lax.broadcasted_iota(jnp.int32, (1, BLOCK), 1)
        dqg = jnp.zeros((1, HEAD_DIM), F32)
        dkg = jnp.zeros((1, HEAD_DIM), F32)
        dsink_vec = jnp.zeros((1, BLOCK), F32)
        for g in range(SWA_KV_HEADS):
            _, vjp = jax.vjp(functools.partial(_swa_group, own=own, valid=valid), _swa_stack(q_ref, g),
                             *_swa_band(kvp_ref, kvc_ref, g), qg_ref[...], kg_ref[...], _swa_sink_column(sink_ref, g),
                             bias_s[g])
            dq, dkp, dkc, dvp, dvc, dqg_g, dkg_g, dsink_col, dbias = vjp(_swa_stack(dy_ref, g))
            _swa_unstack(dq_ref, g, dq)
            dqg += dqg_g
            dkg += dkg_g
            dbias_s[g] += dbias
            for hh in range(SWA_GROUP):
                dsink_h = jnp.sum(dsink_col[hh * BLOCK:(hh + 1) * BLOCK], axis=0, keepdims=True)
                dsink_vec += jnp.where(lane == g * SWA_GROUP + hh, dsink_h, 0.0)
            lo = g * HEAD_DIM
            dkv_ref[:, lo:lo + HEAD_DIM] = dkc + carry_s[g]
            carry_s[g] = dkp
            lo += SWA_KV_W
            dkv_ref[:, lo:lo + HEAD_DIM] = dvc + carry_s[SWA_KV_HEADS + g]
            carry_s[SWA_KV_HEADS + g] = dvp
        dqg_ref[...] += dqg
        dkg_ref[...] += dkg
        dsink_ref[...] += dsink_vec

        @pl.when(t == nb - 1)
        def _():
            bucket = bucket_ref[...]
            row = lax.broadcasted_iota(jnp.int32, (N_BUCKETS, BLOCK), 0)
            col = lax.broadcasted_iota(jnp.int32, (N_BUCKETS, BLOCK), 1)
            acc = jnp.zeros((N_BUCKETS, BLOCK), F32)
            for h in range(SWA_HEADS):
                dbias = dbias_s[h // SWA_GROUP, (h % SWA_GROUP) * BLOCK:(h % SWA_GROUP + 1) * BLOCK, :]
                for b in range(N_BUCKETS):
                    part = jnp.sum(jnp.where(bucket == b, dbias, 0.0), axis=1, keepdims=True)
                    val = jnp.sum(part, axis=0, keepdims=True)
                    acc = acc + jnp.where((row == b) & (col == h), val, 0.0)
            drb_ref[...] = acc

    order = lambda t: nb - 1 - t
    vec = pl.BlockSpec((1, HEAD_DIM), lambda t: (0, 0))
    return pl.pallas_call(
        body, name=name, grid=(nb,),
        in_specs=_swa_specs(order) + [pl.BlockSpec((BLOCK, SWA_Q_W), lambda t: (order(t), 0))],
        out_specs=[pl.BlockSpec((BLOCK, SWA_Q_W), lambda t: (order(t), 0)),
                   pl.BlockSpec((BLOCK, 2 * SWA_KV_W), lambda t: (order(t), 0)),
                   vec, vec, pl.BlockSpec((1, BLOCK), lambda t: (0, 0)),
                   pl.BlockSpec((N_BUCKETS, BLOCK), lambda t: (0, 0))],
        out_shape=[SDS((s, SWA_Q_W), F32), SDS((s, 2 * SWA_KV_W), F32), SDS((1, HEAD_DIM), F32),
                   SDS((1, HEAD_DIM), F32), SDS((1, BLOCK), F32), SDS((N_BUCKETS, BLOCK), F32)],
        scratch_shapes=[pltpu.VMEM((SWA_KV_HEADS, SWA_STACK, BLOCK), F32),
                        pltpu.VMEM((SWA_KV_HEADS, SWA_STACK, BLOCK), F32),
                        pltpu.VMEM((2 * SWA_KV_HEADS, BLOCK, HEAD_DIM), F32)],
        compiler_params=_params("arbitrary"),
    )(p, p, p, qg, kg, sinks, rel_bias, jnp.asarray(_bucket_table()), dy_all)


def _mem_head(q, k, v, qg, kg):
    qn = _rms(q, qg)
    kn = _rms(k, kg)
    s = _mm(qn, kn, False, True) * (HEAD_DIM ** -0.5)
    m = lax.stop_gradient(jnp.max(s, axis=-1, keepdims=True))
    e = jnp.exp(s - m)
    return _mm(e / jnp.sum(e, axis=-1, keepdims=True), v)


def _mem_fwd(p, kv, qg, kg, *, tq, name):
    s = p.shape[0]
    m = kv.shape[0]

    def body(q_ref, kv_ref, qg_ref, kg_ref, y_ref):
        for h in range(MEM_HEADS):
            cols = slice(h * HEAD_DIM, (h + 1) * HEAD_DIM)
            vcols = slice(MEM_Q_W + h * HEAD_DIM, MEM_Q_W + (h + 1) * HEAD_DIM)
            y_ref[:, cols] = _mem_head(q_ref[:, cols], kv_ref[:, cols], kv_ref[:, vcols], qg_ref[...], kg_ref[...])

    vec = pl.BlockSpec((1, HEAD_DIM), lambda t: (0, 0))
    return pl.pallas_call(
        body, name=name, grid=(s // tq,),
        in_specs=[pl.BlockSpec((tq, MEM_Q_W), lambda t: (t, COL_MQ // MEM_Q_W)),
                  pl.BlockSpec((m, 2 * MEM_Q_W), lambda t: (0, 0)), vec, vec],
        out_specs=pl.BlockSpec((tq, MEM_Q_W), lambda t: (t, 0)), out_shape=SDS((s, MEM_Q_W), F32),
        compiler_params=_params("parallel"),
    )(p, kv, qg, kg)


def _mem_bwd(p, kv, qg, kg, dy_all, *, tq, name):
    s = p.shape[0]
    m = kv.shape[0]

    def body(q_ref, kv_ref, qg_ref, kg_ref, dy_ref, dq_ref, dkv_ref, dqg_ref, dkg_ref):
        @pl.when(pl.program_id(0) == 0)
        def _():
            dkv_ref[...] = jnp.zeros_like(dkv_ref)
            dqg_ref[...] = jnp.zeros_like(dqg_ref)
            dkg_ref[...] = jnp.zeros_like(dkg_ref)

        dqg = jnp.zeros((1, HEAD_DIM), F32)
        dkg = jnp.zeros((1, HEAD_DIM), F32)
        for h in range(MEM_HEADS):
            cols = slice(h * HEAD_DIM, (h + 1) * HEAD_DIM)
            vcols = slice(MEM_Q_W + h * HEAD_DIM, MEM_Q_W + (h + 1) * HEAD_DIM)
            _, vjp = jax.vjp(_mem_head, q_ref[:, cols], kv_ref[:, cols], kv_ref[:, vcols], qg_ref[...], kg_ref[...])
            dq, dk, dv, dqg_h, dkg_h = vjp(dy_ref[:, cols])
            dq_ref[:, cols] = dq
            dkv_ref[:, cols] += dk
            dkv_ref[:, vcols] += dv
            dqg += dqg_h
            dkg += dkg_h
        dqg_ref[...] += dqg
        dkg_ref[...] += dkg

    vec = pl.BlockSpec((1, HEAD_DIM), lambda t: (0, 0))
    full = pl.BlockSpec((m, 2 * MEM_Q_W), lambda t: (0, 0))
    dy_col = (SWA_Q_W + GLA_V_W) // MEM_Q_W
    return pl.pallas_call(
        body, name=name, grid=(s // tq,),
        in_specs=[pl.BlockSpec((tq, MEM_Q_W), lambda t: (t, COL_MQ // MEM_Q_W)), full, vec, vec,
                  pl.BlockSpec((tq, MEM_Q_W), lambda t: (t, dy_col))],
        out_specs=[pl.BlockSpec((tq, MEM_Q_W), lambda t: (t, 0)), full, vec, vec],
        out_shape=[SDS((s, MEM_Q_W), F32), SDS((m, 2 * MEM_Q_W), F32), SDS((1, HEAD_DIM), F32), SDS((1, HEAD_DIM), F32)],
        compiler_params=_params("arbitrary"),
    )(p, kv, qg, kg, dy_all)


GLA_ROWS = 256


GLA_GROUP = 4


def _gla_consts():
    c, h, r = GLA_CHUNK, GLA_HEADS, GLA_GROUP * GLA_CHUNK
    i2 = lax.broadcasted_iota(jnp.int32, (c, c), 0)
    j2 = lax.broadcasted_iota(jnp.int32, (c, c), 1)
    slab_q = lax.broadcasted_iota(jnp.int32, (h, r, GLA_QK_W), 0)
    lane_q = lax.broadcasted_iota(jnp.int32, (h, r, GLA_QK_W), 2)
    row_a = lax.broadcasted_iota(jnp.int32, (h * r, r), 0) % r
    col_a = lax.broadcasted_iota(jnp.int32, (h * r, r), 1)
    slab_o = lax.broadcasted_iota(jnp.int32, (h, r, GLA_V_W), 0)
    lane_o = lax.broadcasted_iota(jnp.int32, (h, r, GLA_V_W), 2)
    row_s = lax.broadcasted_iota(jnp.int32, (GLA_V_W, GLA_QK_W), 0)
    col_s = lax.broadcasted_iota(jnp.int32, (GLA_V_W, GLA_QK_W), 1)
    return dict(
        ltri=(j2 <= i2).astype(F32),
        m_q=(slab_q == lane_q // GLA_DK).astype(F32),
        causal=(col_a <= row_a) & (col_a // c == row_a // c),
        m_o=(slab_o == lane_o // GLA_DV).astype(F32),
        m_s=(row_s // GLA_DV == col_s // GLA_DK).astype(F32),
    )


def _gla_step(q, k, v, z, bg, st, c):
    h = GLA_HEADS
    kt, qt, qe, decay = [], [], [], []
    for qc, kc, zc in zip(q, k, z):
        la = _log_sigmoid(zc + bg) * (1.0 / GLA_TAU)
        b = _mmf(c["ltri"], la)
        bl = jnp.sum(la, axis=0, keepdims=True)
        qs = qc * (GLA_DK ** -0.5)
        kt.append(kc * jnp.exp(bl - b))
        qt.append(qs * jnp.exp(b - bl))
        qe.append(qs * jnp.exp(b))
        decay.append(jnp.exp(bl))
    o_intra = []
    rows = GLA_GROUP * GLA_CHUNK
    for lo in range(0, len(q), GLA_GROUP):
        qt_all, kt_all, v_all = (jnp.concatenate(parts[lo:lo + GLA_GROUP], axis=0) for parts in (qt, kt, v))
        q_stack = (jnp.broadcast_to(qt_all[None], (h, rows, GLA_QK_W)) * c["m_q"]).reshape(h * rows, GLA_QK_W)
        a = jnp.where(c["causal"], _mm3(q_stack, kt_all, False, True), 0.0)
        o_stack = _mm(a, v_all)
        o_intra.append(jnp.sum(o_stack.reshape(h, rows, GLA_V_W) * c["m_o"], axis=0))
    o_intra = jnp.concatenate(o_intra, axis=0)
    o_inter = []
    for qec, ktc, vc, dc in zip(qe, kt, v, decay):
        o_inter.append(_mm(qec, st, False, True))
        st = st * dc + _mm(vc, ktc, True, False) * c["m_s"]
    return o_intra + jnp.concatenate(o_inter, axis=0), st


def _gla_post(o, gg, gain, g64):
    ms = _mmf(o * o, g64) * (1.0 / GLA_DV)
    return o * lax.rsqrt(ms + EPS) * gain * jax.nn.silu(gg)


def _gla_g64():
    r = lax.broadcasted_iota(jnp.int32, (GLA_V_W, GLA_V_W), 0)
    c = lax.broadcasted_iota(jnp.int32, (GLA_V_W, GLA_V_W), 1)
    return (r // GLA_DV == c // GLA_DV).astype(F32)


def _gla_in_specs(order):
    r = GLA_ROWS
    return [
        pl.BlockSpec((r, GLA_QK_W), lambda t: (order(t), COL_GQ // GLA_QK_W)),
        pl.BlockSpec((r, GLA_QK_W), lambda t: (order(t), COL_GK // GLA_QK_W)),
        pl.BlockSpec((r, GLA_V_W), lambda t: (order(t), COL_GV // GLA_V_W)),
        pl.BlockSpec((r, GLA_V_W), lambda t: (order(t), COL_GG // GLA_V_W)),
        pl.BlockSpec((r, GLA_QK_W), lambda t: (order(t), 0)),
        pl.BlockSpec((1, GLA_QK_W), lambda t: (0, 0)),
        pl.BlockSpec((1, GLA_V_W), lambda t: (0, 0)),
    ]


def _gla_pieces(q_ref, k_ref, v_ref, z_ref, cps):
    chunk = lambda ref: [ref[ci * GLA_CHUNK:(ci + 1) * GLA_CHUNK, :] for ci in range(cps)]
    return chunk(q_ref), chunk(k_ref), chunk(v_ref), chunk(z_ref)


def _gla_fwd(p, z, bg, gain, *, name):
    s = p.shape[0]
    r = GLA_ROWS
    cps = r // GLA_CHUNK

    def body(q_ref, k_ref, v_ref, gg_ref, z_ref, bg_ref, gain_ref, y_ref, oraw_ref, stsave_ref, st_s):
        @pl.when(pl.program_id(0) == 0)
        def _():
            st_s[...] = jnp.zeros_like(st_s)

        st = st_s[...]
        stsave_ref[0] = st
        o, st = _gla_step(*_gla_pieces(q_ref, k_ref, v_ref, z_ref, cps), bg_ref[...], st, _gla_consts())
        oraw_ref[...] = o
        st_s[...] = st
        y_ref[...] = _gla_post(o, gg_ref[...], gain_ref[...], _gla_g64())

    rowv = pl.BlockSpec((r, GLA_V_W), lambda t: (t, 0))
    return pl.pallas_call(
        body, name=name, grid=(s // r,), in_specs=_gla_in_specs(lambda t: t),
        out_specs=[rowv, rowv, pl.BlockSpec((1, GLA_V_W, GLA_QK_W), lambda t: (t, 0, 0))],
        out_shape=[SDS((s, GLA_V_W), F32), SDS((s, GLA_V_W), F32), SDS((s // r, GLA_V_W, GLA_QK_W), F32)],
        scratch_shapes=[pltpu.VMEM((GLA_V_W, GLA_QK_W), F32)],
        compiler_params=_params("arbitrary"),
    )(p, p, p, p, z, bg, gain)


def _gla_bwd(p, z, bg, gain, oraw, stsave, dy_all, *, name):
    s = p.shape[0]
    r = GLA_ROWS
    cps = r // GLA_CHUNK
    nsteps = s // r
    w_qkvg = 2 * GLA_QK_W + 2 * GLA_V_W

    def body(q_ref, k_ref, v_ref, gg_ref, z_ref, bg_ref, gain_ref, oraw_ref, stsave_ref, dy_ref,
             dqkvg_ref, dz_ref, dbg_ref, dgain_ref, dst_s):
        @pl.when(pl.program_id(0) == 0)
        def _():
            dst_s[...] = jnp.zeros_like(dst_s)
            dbg_ref[...] = jnp.zeros_like(dbg_ref)
            dgain_ref[...] = jnp.zeros_like(dgain_ref)

        _, vjp = jax.vjp(functools.partial(_gla_post, g64=_gla_g64()), oraw_ref[...], gg_ref[...], gain_ref[...])
        do, dgg, dgain = vjp(dy_ref[...])
        dqkvg_ref[:, 2 * GLA_QK_W + GLA_V_W:] = dgg
        dgain_ref[...] += dgain
        _, vjp = jax.vjp(functools.partial(_gla_step, c=_gla_consts()), *_gla_pieces(q_ref, k_ref, v_ref, z_ref, cps),
                         bg_ref[...], stsave_ref[0])
        dq, dk, dv, dz, dbg, dst = vjp((do, dst_s[...]))
        for ci in range(cps):
            rows = slice(ci * GLA_CHUNK, (ci + 1) * GLA_CHUNK)
            dqkvg_ref[rows, 0:GLA_QK_W] = dq[ci]
            dqkvg_ref[rows, GLA_QK_W:2 * GLA_QK_W] = dk[ci]
            dqkvg_ref[rows, 2 * GLA_QK_W:2 * GLA_QK_W + GLA_V_W] = dv[ci]
            dz_ref[rows, :] = dz[ci]
        dst_s[...] = dst
        dbg_ref[...] += dbg

    order = lambda t: nsteps - 1 - t
    rowv = pl.BlockSpec((r, GLA_V_W), lambda t: (order(t), 0))
    return pl.pallas_call(
        body, name=name, grid=(nsteps,),
        in_specs=_gla_in_specs(order) + [
            rowv, pl.BlockSpec((1, GLA_V_W, GLA_QK_W), lambda t: (order(t), 0, 0)),
            pl.BlockSpec((r, GLA_V_W), lambda t: (order(t), SWA_Q_W // GLA_V_W))],
        out_specs=[pl.BlockSpec((r, w_qkvg), lambda t: (order(t), 0)), pl.BlockSpec((r, GLA_QK_W), lambda t: (order(t), 0)),
                   pl.BlockSpec((1, GLA_QK_W), lambda t: (0, 0)), pl.BlockSpec((1, GLA_V_W), lambda t: (0, 0))],
        out_shape=[SDS((s, w_qkvg), F32), SDS((s, GLA_QK_W), F32), SDS((1, GLA_QK_W), F32), SDS((1, GLA_V_W), F32)],
        scratch_shapes=[pltpu.VMEM((GLA_V_W, GLA_QK_W), F32)],
        compiler_params=_params("arbitrary"),
    )(p, p, p, p, z, bg, gain, oraw, stsave, dy_all)


def _local_step(x, mem, target, small, big, on_grads):
    g1, gmix, gmem, g2, sqg, skg, sinks, rel_bias, wgu, bg, gla_gain, mqg, mkg = small
    w3_1, win_p, wkv, wout, gather_ffn2 = big
    wgu_pad = jnp.zeros((GLA_QK_W, GLA_QK_W), BF16).at[:GLA_RANK].set(wgu.astype(BF16))
    gain256 = jnp.tile(gla_gain, (1, GLA_HEADS))

    x1, saved1 = _ffn_fwd(x, g1, w3_1, "ffn1")
    w3_2 = gather_ffn2(x1)
    h = _rms_fwd(x1, gmix, tm=512, name="mix_rms")
    p = _matmul(h, win_p, tm=1024, tn=IN_W_PAD, name="mix_in")
    hm = _rms_fwd(mem, gmem, tm=256, name="mem_rms")
    kv = _matmul(hm, wkv, tm=256, tn=512, name="mem_kv")
    p_glr = p[:, COL_GLR:]
    z = _matmul(p_glr, wgu_pad, tm=1024, tn=GLA_QK_W, name="gla_gate")
    y_swa = _swa_fwd(p, sqg, skg, sinks, rel_bias, name="swa_fwd")
    y_gla, oraw, stsave = _gla_fwd(p, z, bg, gain256, name="gla_fwd")
    y_mem = _mem_fwd(p, kv, mqg, mkg, tq=512, name="mem_fwd")
    x2 = _matmul([y_swa, y_gla, y_mem], wout, b_blocks=[0, 2, 3], tm=1024, tn=1024, res=x1, name="mix_out")
    x3, saved2 = _ffn_fwd(x2, g2, w3_2, "ffn2")

    dy, dyb, loss = _loss_bwd(x3, target, tm=512, name="loss")
    tiles = w3_2.shape[1] // FFN_TN
    dh2, dw3_2 = _ffn_bwd_part(dyb, w3_2, saved2, 0, tiles, None, name="ffn2_bwd")
    dx2, dx2b, dg2 = _rms_bwd(x2, g2, dh2, dy, tm=512, name="ffn2_drms")
    dx2b = on_grads("ffn2", [dw3_2.reshape(3, 2, -1, dw3_2.shape[-1])], dx2b)
    dy_all = _matmul(dx2b, wout, tb=True, tm=1024, tn=1024, name="mix_dy")
    dwout = _matmul(jnp.concatenate([y_swa, y_gla, y_mem], axis=1), dx2b, ta=True, tm=512, tn=1024, out_dtype=BF16,
                    name="mix_dw_out")
    dq_swa, dkv_swa, dsqg, dskg, dsink, drb = _swa_bwd(p, sqg, skg, sinks, rel_bias, dy_all, name="swa_bwd")
    dqkvg, dz, dbg, dgain256 = _gla_bwd(p, z, bg, gain256, oraw, stsave, dy_all, name="gla_bwd")
    dmq, dkv_mem, dmqg, dmkg = _mem_bwd(p, kv, mqg, mkg, dy_all, tq=512, name="mem_bwd")
    dglr = _matmul(dz, wgu_pad, tb=True, tm=1024, tn=GLA_QK_W, name="gla_gate_dx")
    dwgu_pad = _matmul(p_glr, dz, ta=True, tm=GLA_QK_W, tn=GLA_QK_W, name="gla_gate_dw")
    dp = jnp.concatenate([dq_swa, dkv_swa, dqkvg, dmq, dglr], axis=1)
    dh = _matmul(dp, win_p, tb=True, tm=1024, tn=1024, name="mix_dh")
    dwin_p = _matmul(h, dp, ta=True, tm=1024, tn=640, out_dtype=BF16, name="mix_dw_in")
    dx1, dx1b, dgmix = _rms_bwd(x1, gmix, dh, dx2, tm=512, name="mix_drms")
    dwkv = _matmul(hm, dkv_mem, ta=True, tm=512, tn=512, out_dtype=BF16, name="mem_dw_kv")
    dx1b = on_grads("mix", (dwin_p, dwkv, dwout), dx1b)
    dhm = _matmul(dkv_mem, wkv, tb=True, tm=256, tn=512, name="mem_dh")
    _, _, dgmem = _rms_bwd(mem, gmem, dhm, None, tm=256, name="mem_drms")
    dh1, dw3_1a = _ffn_bwd_part(dx1b, w3_1, saved1, 0, tiles // 2, None, name="ffn1_bwd_a")
    dgla_gain = dgain256.reshape(GLA_HEADS, GLA_DV).sum(axis=0, keepdims=True)
    dsmall = [dgmix, dgmem, dg2, dsqg, dskg, dsink[:, :SWA_HEADS], drb[:, :SWA_HEADS], dwgu_pad[:GLA_RANK], dbg,
              dgla_gain, dmqg, dmkg, loss]
    dh1, dgmem, dwgu_pad = on_grads("ffn1a", [dw3_1a[:, None]], (dh1, dgmem, dwgu_pad), small=dsmall)
    dh1, dw3_1b = _ffn_bwd_part(dx1b, w3_1, saved1, tiles // 2, tiles // 2, dh1, name="ffn1_bwd_b")
    dx, _, dg1 = _rms_bwd(x, g1, dh1, dx1, tm=512, name="ffn1_drms")
    on_grads("ffn1b", [dw3_1b[:, None]], None, small=[dg1])
    return dx


def _mesh_place():
    x, y, c = lax.axis_index("x"), lax.axis_index("y"), lax.axis_index("c")
    other_chips = [(1 - x, y), (x, 1 - y), (1 - x, 1 - y)]
    return x, y, c, other_chips


def _handshake(peers):
    barrier = pltpu.get_barrier_semaphore()
    for peer in peers:
        pl.semaphore_signal(barrier, inc=1, device_id=peer, device_id_type=MESH)
    pl.semaphore_wait(barrier, len(peers))


def _sequencer_call(body, operands, out_shapes, sems, *, name, collective_id):
    return pl.kernel(
        body, name=name, out_type=out_shapes, mesh=plsc.ScalarSubcoreMesh(axis_name="sequencer", num_cores=1),
        scratch_types=sems, compiler_params=pltpu.CompilerParams(collective_id=collective_id),
    )(*operands)


def _window(ref, kind, slot, shape):
    if kind == "row":
        rows = pl.ds(pl.multiple_of(slot * shape[-2], 8), shape[-2])
        return ref.at[(slice(None),) * (len(shape) - 2) + (rows,)]
    return ref.at[slot]


def _gathered(shape, kind):
    if kind == "row":
        return tuple(shape[:-2]) + (N_DEV * shape[-2], shape[-1])
    return (N_DEV,) + tuple(shape)


def _all_gather(shards, kinds, *, name, collective_id):
    nt = len(shards)

    def body(*refs):
        x_refs, o_refs = refs[:nt], refs[nt:2 * nt]
        send_sems, recv_sems, local_sems = refs[2 * nt:]
        x, y, c, chips = _mesh_place()
        me, sibling = (x, y, c), (x, y, 1 - c)
        _handshake([sibling] + [(*chip, c) for chip in chips])

        def copy(k, t, block, to, from_shard=False):
            bx, by, bc = block
            rows = _window(o_refs[t], kinds[t], 4 * bx + 2 * by + bc, shards[t].shape)
            return pltpu.make_async_remote_copy(
                src_ref=x_refs[t] if from_shard else rows, dst_ref=rows,
                send_sem=send_sems.at[k, t], recv_sem=recv_sems.at[k, t], device_id=to, device_id_type=MESH)

        mine = [pltpu.make_async_copy(x_refs[t], _window(o_refs[t], kinds[t], 4 * x + 2 * y + c, shards[t].shape),
                                      local_sems.at[t]) for t in range(nt)]
        for cp in mine:
            cp.start()
        first = [copy(0, t, me, sibling, True) for t in range(nt)]
        first += [copy(1 + j, t, me, (*chip, c), True) for j, chip in enumerate(chips) for t in range(nt)]
        for cp in first:
            cp.start()
        passed = []
        for j, chip in enumerate(chips):
            for t in range(nt):
                copy(1 + j, t, (*chip, c), me).wait_recv()
                fwd = copy(4 + j, t, (*chip, c), sibling)
                fwd.start()
                passed.append(fwd)
        for t in range(nt):
            copy(0, t, sibling, me).wait_recv()
        for j, chip in enumerate(chips):
            for t in range(nt):
                copy(4 + j, t, (*chip, 1 - c), me).wait_recv()
        for cp in first + passed:
            cp.wait_send()
        for cp in mine:
            cp.wait()

    return _sequencer_call(
        body, shards, [SDS(_gathered(s.shape, k), s.dtype) for s, k in zip(shards, kinds)],
        [pltpu.SemaphoreType.DMA((7, nt)), pltpu.SemaphoreType.DMA((7, nt)), pltpu.SemaphoreType.DMA((nt,))],
        name=name, collective_id=collective_id)


def _part_shape(shape, kind):
    if kind == "row":
        return tuple(shape[:-2]) + (shape[-2] // N_DEV, shape[-1])
    return tuple(shape[2:])


def _pair_exchange(grads, kinds, *, name, collective_id):
    nt = len(grads)
    part = [_part_shape(g.shape, k) for g, k in zip(grads, kinds)]

    def body(*refs):
        g_refs, o_refs = refs[:nt], refs[nt:2 * nt]
        send_sems, recv_sems = refs[2 * nt:]
        x, y, c, _ = _mesh_place()
        _handshake([(x, y, 1 - c)])
        copies = []
        for t in range(nt):
            for xy in range(4):
                src = g_refs[t].at[1 - c, xy] if kinds[t] == "stack" else _window(g_refs[t], kinds[t], 2 * xy + 1 - c, part[t])
                copies.append(pltpu.make_async_remote_copy(
                    src_ref=src, dst_ref=o_refs[t].at[xy], send_sem=send_sems.at[xy, t], recv_sem=recv_sems.at[xy, t],
                    device_id=(x, y, 1 - c), device_id_type=MESH))
        for cp in copies:
            cp.start()
        for cp in copies:
            cp.wait()

    return _sequencer_call(
        body, grads, [SDS((4,) + p, g.dtype) for p, g in zip(part, grads)],
        [pltpu.SemaphoreType.DMA((4, nt)), pltpu.SemaphoreType.DMA((4, nt))], name=name, collective_id=collective_id)


def _chip_exchange(parts, small, *, name, collective_id):
    nt = len(parts)
    if small is None:
        def body_plain(*refs):
            s_refs, o_refs = refs[:nt], refs[nt:2 * nt]
            send_sems, recv_sems = refs[2 * nt:]
            x, y, c, chips = _mesh_place()
            _handshake([(*chip, c) for chip in chips])
            copies = [pltpu.make_async_remote_copy(
                src_ref=s_refs[t].at[2 * chip[0] + chip[1]], dst_ref=o_refs[t].at[j],
                send_sem=send_sems.at[j, t], recv_sem=recv_sems.at[j, t], device_id=(*chip, c), device_id_type=MESH)
                for j, chip in enumerate(chips) for t in range(nt)]
            for cp in copies:
                cp.start()
            for cp in copies:
                cp.wait()

        return _sequencer_call(
            body_plain, parts, [SDS((3,) + s.shape[1:], s.dtype) for s in parts],
            [pltpu.SemaphoreType.DMA((3, nt)), pltpu.SemaphoreType.DMA((3, nt))], name=name, collective_id=collective_id)

    def body(*refs):
        s_refs, small_ref = refs[:nt], refs[nt]
        o_refs, small_all = refs[nt + 1:2 * nt + 1], refs[2 * nt + 1]
        send_sems, recv_sems, small_send, small_recv, local_sem = refs[2 * nt + 2:]
        x, y, c, chips = _mesh_place()
        _handshake([(px, py, pc) for px in (x, 1 - x) for py in (y, 1 - y) for pc in (c, 1 - c)][1:])

        def copy(j, t, chip):
            return pltpu.make_async_remote_copy(
                src_ref=s_refs[t].at[2 * chip[0] + chip[1]], dst_ref=o_refs[t].at[j],
                send_sem=send_sems.at[j, t], recv_sem=recv_sems.at[j, t], device_id=(*chip, c), device_id_type=MESH)

        flips = [(fx, fy, fc) for fx in (0, 1) for fy in (0, 1) for fc in (0, 1)][1:]

        def small_copy(k):
            fx, fy, fc = flips[k]
            to = (x ^ fx if fx else x, y ^ fy if fy else y, c ^ fc if fc else c)
            rows = small_all.at[4 * x + 2 * y + c]
            return pltpu.make_async_remote_copy(
                src_ref=small_ref, dst_ref=rows, send_sem=small_send.at[k], recv_sem=small_recv.at[k],
                device_id=to, device_id_type=MESH)

        own = pltpu.make_async_copy(small_ref, small_all.at[4 * x + 2 * y + c], local_sem)
        own.start()
        copies = [copy(j, t, chip) for j, chip in enumerate(chips) for t in range(nt)]
        smalls = [small_copy(k) for k in range(7)]
        for cp in smalls + copies:
            cp.start()
        for cp in smalls + copies:
            cp.wait()
        own.wait()

    return _sequencer_call(
        body, list(parts) + [small],
        [SDS((3,) + s.shape[1:], s.dtype) for s in parts] + [SDS((N_DEV,) + small.shape, small.dtype)],
        [pltpu.SemaphoreType.DMA((3, nt)), pltpu.SemaphoreType.DMA((3, nt)),
         pltpu.SemaphoreType.DMA((7,)), pltpu.SemaphoreType.DMA((7,)), pltpu.SemaphoreType.DMA],
        name=name, collective_id=collective_id)


def _pair_sum(grad, theirs, kind, c, *, name):
    if kind == "row":
        r, l = theirs.shape[-2:]
        n = theirs.size // (4 * r * l)
        grad, theirs = grad.reshape(n, N_DEV * r, l), theirs.reshape(4, n, r, l)
        mine_spec = pl.BlockSpec((n, r, l), lambda xy, c_ref: (0, 2 * xy + c_ref[0], 0))
    else:
        r, l = theirs.shape[-2:]
        n = theirs.size // (4 * r * l)
        theirs = theirs.reshape(4, n, r, l)
        grad = grad.reshape(2, 4, n, r, l)
        mine_spec = pl.BlockSpec((None, None, n, r, l), lambda xy, c_ref: (c_ref[0], xy, 0, 0, 0))

    def body(c_ref, a_ref, b_ref, o_ref):
        o_ref[...] = (a_ref[...].astype(F32) + b_ref[...].astype(F32)).astype(BF16)

    part = pl.BlockSpec((None, n, r, l), lambda xy, c_ref: (xy, 0, 0, 0))
    return pl.pallas_call(
        body, name=name,
        grid_spec=pltpu.PrefetchScalarGridSpec(num_scalar_prefetch=1, grid=(4,), in_specs=[mine_spec, part], out_specs=part),
        out_shape=SDS((4, n, r, l), BF16), compiler_params=_params("parallel"),
    )(c, grad, theirs)


def _adamw(w, g, m, v):
    m = ADAM_B1 * m + (1.0 - ADAM_B1) * g
    v = ADAM_B2 * v + (1.0 - ADAM_B2) * jnp.square(g)
    m_hat = m / (1.0 - ADAM_B1 ** ADAM_STEP)
    v_hat = v / (1.0 - ADAM_B2 ** ADAM_STEP)
    delta = -ADAM_LR * (m_hat / (jnp.sqrt(v_hat) + ADAM_EPS) + ADAM_WD * w)
    return delta, m, v


def _adam_big(owns, others, mat, xy, w, m, v, *, tr, name):
    nw, r, l = w.shape
    lp = owns[0].shape[-1]
    nq = len(owns)
    assert nq in (1, nw)

    def body(xy_ref, *refs):
        own_refs, oth_refs = refs[:nq], refs[nq:2 * nq]
        w_ref, m_ref, v_ref, g_out, d_out, m_out, v_out = refs[2 * nq:]
        g = None
        for q in range(nq):
            gq = own_refs[q][0, 0].astype(F32)
            for j in range(3):
                gq = gq + oth_refs[q][j, 0].astype(F32)
            g = gq if g is None else jnp.where(pl.program_id(0) == q, gq, g)
        g = g[:, :l]
        delta, m_new, v_new = _adamw(w_ref[0], g, m_ref[0], v_ref[0])
        g_out[0] = g
        d_out[0] = delta
        m_out[0] = m_new
        v_out[0] = v_new

    def at(p):
        return mat * nw + p if nq == 1 else mat

    blk = pl.BlockSpec((1, tr, l), lambda p, i, xy_ref: (p, i, 0))
    return pl.pallas_call(
        body, name=name,
        grid_spec=pltpu.PrefetchScalarGridSpec(
            num_scalar_prefetch=1, grid=(nw, r // tr),
            in_specs=[pl.BlockSpec((1, 1, tr, lp), lambda p, i, xy_ref: (xy_ref[0], at(p), i, 0))] * nq
            + [pl.BlockSpec((3, 1, tr, lp), lambda p, i, xy_ref: (0, at(p), i, 0))] * nq + [blk, blk, blk],
            out_specs=[blk, blk, blk, blk]),
        out_shape=[SDS(w.shape, F32)] * 4, compiler_params=_params("parallel", "parallel"),
    )(xy, *owns, *others, w, m, v)


def _small_layout(shapes):
    out, at = [], 0
    for r, c in shapes:
        rows = c // 128 if (r == 1 and c > 128) else r
        out.append((at, rows))
        at += -(-rows // 8) * 8
    return out, at


def _pack_small(parts, *, name):
    shapes = [a.shape for a in parts]
    layout, total = _small_layout(shapes)

    def body(*refs):
        o_ref = refs[-1]
        o_ref[...] = jnp.zeros_like(o_ref)
        for x_ref, (r, c), (at, rows) in zip(refs, shapes, layout):
            if r == 1 and c > 128:
                for k in range(rows):
                    o_ref[at + k:at + k + 1, :] = x_ref[:, k * 128:(k + 1) * 128]
            else:
                o_ref[at:at + r, 0:c] = x_ref[...]

    return pl.pallas_call(body, name=name, out_shape=SDS((total, 128), F32))(*parts)


def _adam_small(g_all, ws, ms, vs, *, name):
    n = len(ws)
    shapes = [w.shape for w in ws]
    layout, _ = _small_layout(shapes)

    def body(g_ref, *refs):
        w_refs, m_refs, v_refs, outs = refs[:n], refs[n:2 * n], refs[2 * n:3 * n], refs[3 * n:]
        g_sum = g_ref[0]
        for k in range(1, N_DEV):
            g_sum = g_sum + g_ref[k]
        for i, ((r, c), (at, rows)) in enumerate(zip(shapes, layout)):
            if r == 1 and c > 128:
                g = jnp.concatenate([g_sum[at + k:at + k + 1, :] for k in range(rows)], axis=1)
            else:
                g = g_sum[at:at + r, 0:c]
            delta, m_new, v_new = _adamw(w_refs[i][...], g, m_refs[i][...], v_refs[i][...])
            for q, val in enumerate((g, delta, m_new, v_new)):
                outs[4 * i + q][...] = val

    flat = pl.pallas_call(body, name=name, out_shape=[SDS(s, F32) for s in shapes for _ in range(4)])(g_all, *ws, *ms, *vs)
    return [flat[4 * i:4 * i + 4] for i in range(n)]


def kernel(x, mem, ffn1_norm, ffn1_w_gate, ffn1_w_up, ffn1_w_down, mix_norm, mem_norm, w_in, w_mem_kv, swa_q_norm, swa_k_norm, swa_sinks, rel_bias, gla_w_gate_up, gla_b_gate, gla_out_norm, mem_q_norm, mem_k_norm, w_out, ffn2_norm, ffn2_w_gate, ffn2_w_up, ffn2_w_down, loss_target, m_ffn1_norm, m_ffn1_w_gate, m_ffn1_w_up, m_ffn1_w_down, m_mix_norm, m_mem_norm, m_w_in, m_w_mem_kv, m_swa_q_norm, m_swa_k_norm, m_swa_sinks, m_rel_bias, m_gla_w_gate_up, m_gla_b_gate, m_gla_out_norm, m_mem_q_norm, m_mem_k_norm, m_w_out, m_ffn2_norm, m_ffn2_w_gate, m_ffn2_w_up, m_ffn2_w_down, v_ffn1_norm, v_ffn1_w_gate, v_ffn1_w_up, v_ffn1_w_down, v_mix_norm, v_mem_norm, v_w_in, v_w_mem_kv, v_swa_q_norm, v_swa_k_norm, v_swa_sinks, v_rel_bias, v_gla_w_gate_up, v_gla_b_gate, v_gla_out_norm, v_mem_q_norm, v_mem_k_norm, v_w_out, v_ffn2_norm, v_ffn2_w_gate, v_ffn2_w_up, v_ffn2_w_down):
    xi, yi, ci = lax.axis_index("x"), lax.axis_index("y"), lax.axis_index("c")
    c_arr = jnp.reshape(ci, (1,)).astype(jnp.int32)
    xy_arr = jnp.reshape(2 * xi + yi, (1,)).astype(jnp.int32)
    d = x.shape[-1]

    half_h = ffn1_w_gate.shape[-1] // 2

    def gather_ffn(wg_s, wu_s, wd_s, name, collective_id, after):
        w3_s = jnp.concatenate([wg_s.transpose(0, 2, 1), wu_s.transpose(0, 2, 1), wd_s], axis=0)
        w3_s = jnp.pad(w3_s.reshape(3, 2, half_h, d), ((0, 0), (0, 0), (0, FFN_HALF_PAD - half_h), (0, 0))).astype(BF16)
        if after is not None:
            w3_s, _ = lax.optimization_barrier((w3_s, after))
        return _all_gather([w3_s], ["row"], name=name, collective_id=collective_id)[0].reshape(3, -1, d)

    w3_1 = gather_ffn(ffn1_w_gate, ffn1_w_up, ffn1_w_down, "gather_ffn1", 0, None)
    mix_s = lax.optimization_barrier((w_in[0].astype(BF16), w_mem_kv[0].astype(BF16), w_out[0].astype(BF16), w3_1))[:3]
    win_all, wkv, wout = _all_gather(list(mix_s), ["stack", "row", "row"], name="gather_mix", collective_id=1)

    def gather_ffn2(x1):
        return gather_ffn(ffn2_w_gate, ffn2_w_up, ffn2_w_down, "gather_ffn2", 2, (wout, x1))

    win_p = _pack_win(win_all, tr=256, name="pack_w_in")

    small_w = [ffn1_norm, mix_norm, mem_norm, ffn2_norm, swa_q_norm, swa_k_norm, swa_sinks[0], rel_bias,
               gla_w_gate_up[0], gla_b_gate, gla_out_norm, mem_q_norm, mem_k_norm]
    collective_ids = {"ffn2": (3, 4), "mix": (5, 6), "ffn1a": (7, 8), "ffn1b": (9, 10)}
    reduced, small_box = {}, {}

    def on_grads(group, grads, carry, small=None):
        if group == "mix":
            dwin_p, dwkv, dwout = grads
            grads = [_unpack_win(dwin_p, tr=256, name="unpack_dw_in"), dwkv, dwout]
            kinds = ["stack", "row", "row"]
        else:
            kinds = ["row"]
        if reduced:
            earlier = list(reduced.values())[-1][1]
            *grads, _ = lax.optimization_barrier((*grads, earlier[0]))
        id_pair, id_chip = collective_ids[group]
        from_sibling = _pair_exchange(grads, kinds, name=f"pair_exchange_{group}", collective_id=id_pair)
        chip_sums = [_pair_sum(g, theirs, k, c_arr, name=f"pair_sum_{group}_{t}")
                     for t, (g, theirs, k) in enumerate(zip(grads, from_sibling, kinds))]
        if carry is not None:
            *chip_sums, carry = lax.optimization_barrier((*chip_sums, carry))
        if small is None:
            from_chips = _chip_exchange(chip_sums, None, name=f"chip_exchange_{group}", collective_id=id_chip)
        else:
            packed = _pack_small(small, name=f"pack_small_{group}")
            *from_chips, small_all = _chip_exchange(chip_sums, packed, name=f"chip_exchange_{group}",
                                                    collective_id=id_chip)
            small_box[group] = small_all
        reduced[group] = (chip_sums, from_chips)
        return carry

    grad_x = _local_step(x[0], mem[0], loss_target[0], small_w, (w3_1, win_p, wkv, wout, gather_ffn2), on_grads)

    big_w = {"ffn1_w_gate": ("ffn1", 0, 0, True, ffn1_w_gate, m_ffn1_w_gate, v_ffn1_w_gate),
             "ffn1_w_up": ("ffn1", 0, 1, True, ffn1_w_up, m_ffn1_w_up, v_ffn1_w_up),
             "ffn1_w_down": ("ffn1", 0, 2, False, ffn1_w_down, m_ffn1_w_down, v_ffn1_w_down),
             "w_in": ("mix", 0, 0, False, w_in, m_w_in, v_w_in),
             "w_mem_kv": ("mix", 1, 0, False, w_mem_kv, m_w_mem_kv, v_w_mem_kv),
             "w_out": ("mix", 2, 0, False, w_out, m_w_out, v_w_out),
             "ffn2_w_gate": ("ffn2", 0, 0, True, ffn2_w_gate, m_ffn2_w_gate, v_ffn2_w_gate),
             "ffn2_w_up": ("ffn2", 0, 1, True, ffn2_w_up, m_ffn2_w_up, v_ffn2_w_up),
             "ffn2_w_down": ("ffn2", 0, 2, False, ffn2_w_down, m_ffn2_w_down, v_ffn2_w_down)}
    res = {}
    for nm, (group, t, mat, transposed, w, m, v) in big_w.items():
        shape = w.shape
        if transposed:
            w, m, v = (a.transpose(0, 2, 1) for a in (w, m, v))
        if group != "mix":
            w, m, v = (a.reshape(2, half_h, d) for a in (w, m, v))
        r = w.shape[1]
        tr = 256 if r % 256 == 0 else r
        halves = ["ffn1a", "ffn1b"] if group == "ffn1" else [group]
        out = _adam_big([reduced[k][0][t] for k in halves], [reduced[k][1][t] for k in halves], mat, xy_arr, w, m, v,
                        tr=tr, name=f"adam_{nm}")
        if transposed:
            out = [a.reshape(1, -1, d).transpose(0, 2, 1) for a in out]
        res[nm] = [a.reshape(shape) for a in out]
    small_names = ["ffn1_norm", "mix_norm", "mem_norm", "ffn2_norm", "swa_q_norm", "swa_k_norm", "swa_sinks", "rel_bias",
                   "gla_w_gate_up", "gla_b_gate", "gla_out_norm", "mem_q_norm", "mem_k_norm"]
    small_m = [m_ffn1_norm, m_mix_norm, m_mem_norm, m_ffn2_norm, m_swa_q_norm, m_swa_k_norm, m_swa_sinks, m_rel_bias,
               m_gla_w_gate_up, m_gla_b_gate, m_gla_out_norm, m_mem_q_norm, m_mem_k_norm]
    small_v = [v_ffn1_norm, v_mix_norm, v_mem_norm, v_ffn2_norm, v_swa_q_norm, v_swa_k_norm, v_swa_sinks, v_rel_bias,
               v_gla_w_gate_up, v_gla_b_gate, v_gla_out_norm, v_mem_q_norm, v_mem_k_norm]
    small_full = [ffn1_norm, mix_norm, mem_norm, ffn2_norm, swa_q_norm, swa_k_norm, swa_sinks, rel_bias,
                  gla_w_gate_up, gla_b_gate, gla_out_norm, mem_q_norm, mem_k_norm]
    zero = jnp.zeros((1, 1), F32)
    two_d = lambda a: a.reshape(a.shape[-2:])
    for group, sel in (("ffn1a", slice(1, None)), ("ffn1b", slice(0, 1))):
        extra = [zero] if group == "ffn1a" else []
        ws, ms, vs = ([two_d(a) for a in arrs[sel]] + extra for arrs in (small_full, small_m, small_v))
        updated = _adam_small(small_box[group], ws, ms, vs, name=f"adam_small_{group}")
        for nm, full, out in zip(small_names[sel], small_full[sel], updated):
            res[nm] = [a.reshape(full.shape) for a in out]
        if extra:
            loss = updated[-1][0].reshape(())

    order = ["ffn1_norm", "ffn1_w_gate", "ffn1_w_up", "ffn1_w_down", "mix_norm", "mem_norm", "w_in", "w_mem_kv",
             "swa_q_norm", "swa_k_norm", "swa_sinks", "rel_bias", "gla_w_gate_up", "gla_b_gate", "gla_out_norm",
             "mem_q_norm", "mem_k_norm", "w_out", "ffn2_norm", "ffn2_w_gate", "ffn2_w_up", "ffn2_w_down"]
    outs = [loss, grad_x[None]]
    for q in range(4):
        outs += [res[nm][q] for nm in order]
    return tuple(outs)
```

```python
import functools
import math

import numpy as np
import jax
import jax.numpy as jnp
from jax import lax
from jax.experimental import pallas as pl
from jax.experimental.pallas import tpu as pltpu
from jax.experimental.pallas import tpu_sc as plsc

F32 = jnp.float32
BF16 = jnp.bfloat16
SDS = jax.ShapeDtypeStruct

EPS = 1e-6
HEAD_DIM = 64
SWA_HEADS = 8
SWA_KV_HEADS = 2
SWA_GROUP = SWA_HEADS // SWA_KV_HEADS
BLOCK = 128
N_BUCKETS = 32
MAX_DISTANCE = 128
GLA_HEADS = 4
GLA_DK = 32
GLA_DV = 64
GLA_RANK = 16
GLA_TAU = 16.0
GLA_CHUNK = 32
MEM_HEADS = 4
SWA_Q_W = SWA_HEADS * HEAD_DIM
SWA_KV_W = SWA_KV_HEADS * HEAD_DIM
GLA_QK_W = GLA_HEADS * GLA_DK
GLA_V_W = GLA_HEADS * GLA_DV
MEM_Q_W = MEM_HEADS * HEAD_DIM
IN_W = 1808
IN_W_PAD = 1920
COL_SQ, COL_SKV, COL_GQ, COL_GK, COL_GV, COL_GG, COL_MQ, COL_GLR = 0, 512, 768, 896, 1024, 1280, 1536, 1792

ADAM_LR = 0.001
ADAM_B1 = 0.9
ADAM_B2 = 0.999
ADAM_EPS = 1e-08
ADAM_WD = 0.01
ADAM_STEP = 10

N_DEV = 8
VMEM_LIMIT_BYTES = 56 * 1024 * 1024
MESH = pl.DeviceIdType.MESH


def _params(*sem):
    return pltpu.CompilerParams(dimension_semantics=sem or None, vmem_limit_bytes=VMEM_LIMIT_BYTES)


def _dot(a, b, ta, tb, precision=None):
    dims = (((0 if ta else 1,), (1 if tb else 0,)), ((), ()))
    return lax.dot_general(a, b, dims, preferred_element_type=F32, precision=precision)


def _mm_raw(a, b, ta=False, tb=False):
    return _dot(a.astype(BF16), b.astype(BF16), ta, tb)


def _mmf_raw(a, b, ta=False, tb=False):
    return _dot(a, b, ta, tb, lax.Precision.HIGHEST)


def _make_mm(raw):
    @functools.partial(jax.custom_vjp, nondiff_argnums=(2, 3))
    def mm(a, b, ta=False, tb=False):
        return raw(a, b, ta, tb)

    def fwd(a, b, ta, tb):
        return raw(a, b, ta, tb), (a, b)

    def bwd(ta, tb, res, g):
        a, b = res
        da = raw(b, g, tb, True) if ta else raw(g, b, False, not tb)
        db = raw(g, a, True, ta) if tb else raw(a, g, not ta, False)
        return da, db

    mm.defvjp(fwd, bwd)
    return mm


_mm = _make_mm(_mm_raw)
_mmf = _make_mm(_mmf_raw)


def _mm3(a, b, ta=False, tb=False):
    a_hi, b_hi = a.astype(BF16).astype(F32), b.astype(BF16).astype(F32)
    return _mm(a_hi, b_hi, ta, tb) + _mm(a_hi, b - b_hi, ta, tb) + _mm(a - a_hi, b_hi, ta, tb)


def _rms(x, g):
    return x * lax.rsqrt(jnp.mean(x * x, axis=-1, keepdims=True) + EPS) * g


def _silu_mul(g, u):
    return jax.nn.silu(g) * u


def _log_sigmoid(z):
    return jnp.minimum(z, 0.0) - jnp.log(1.0 + jnp.exp(-jnp.abs(z)))


def _matmul(a_list, b, *, ta=False, tb=False, tm, tn, b_blocks=None, res=None, scale=1.0, out_dtype=F32, name):
    if not isinstance(a_list, (list, tuple)):
        a_list = [a_list]
    n_a = len(a_list)
    m = a_list[0].shape[1] if ta else a_list[0].shape[0]
    ks = [a.shape[0] if ta else a.shape[1] for a in a_list]
    n = b.shape[0] if tb else b.shape[1]
    if b_blocks is None:
        assert n_a == 1
        b_blocks = [0]
    tm, tn = min(tm, m), min(tn, n)
    assert m % tm == 0 and n % tn == 0, (m, n, tm, tn)

    def body(*refs):
        a_refs, b_refs = refs[:n_a], refs[n_a:2 * n_a]
        r_ref = refs[2 * n_a] if res is not None else None
        o_ref = refs[-1]
        acc = _mm_raw(a_refs[0][...], b_refs[0][...], ta, tb)
        for k in range(1, n_a):
            acc = acc + _mm_raw(a_refs[k][...], b_refs[k][...], ta, tb)
        if scale != 1.0:
            acc = acc * scale
        if r_ref is not None:
            acc = r_ref[...] + acc
        o_ref[...] = acc.astype(out_dtype)

    in_specs = []
    for k in ks:
        in_specs.append(pl.BlockSpec((k, tm), lambda i, j: (0, i)) if ta else pl.BlockSpec((tm, k), lambda i, j: (i, 0)))
    for k, blk in zip(ks, b_blocks):
        if tb:
            in_specs.append(pl.BlockSpec((tn, k), functools.partial(lambda i, j, blk: (j, blk), blk=blk)))
        else:
            in_specs.append(pl.BlockSpec((k, tn), functools.partial(lambda i, j, blk: (blk, j), blk=blk)))
    args = list(a_list) + [b] * n_a
    if res is not None:
        in_specs.append(pl.BlockSpec((tm, tn), lambda i, j: (i, j)))
        args.append(res)
    return pl.pallas_call(
        body, name=name, grid=(m // tm, n // tn), in_specs=in_specs,
        out_specs=pl.BlockSpec((tm, tn), lambda i, j: (i, j)), out_shape=SDS((m, n), out_dtype),
        compiler_params=_params("parallel", "parallel"),
    )(*args)


def _win_pieces(w):
    glr_lo, glr_hi = COL_MQ, COL_MQ + GLA_RANK
    out = []
    for j in range(N_DEV):
        for lo, hi, shift in ((0, glr_lo, 0), (glr_lo, glr_hi, COL_GLR - glr_lo), (glr_hi, IN_W, COL_MQ - glr_hi)):
            s, e = max(j * w, lo), min((j + 1) * w, hi)
            if s < e:
                out.append((j, s - j * w, e - j * w, s + shift))
    return out


def _pack_win(win_all, *, tr, name):
    _, d, w = win_all.shape

    def body(i_ref, o_ref):
        for j, a, b, dst in _win_pieces(w):
            o_ref[:, dst:dst + b - a] = i_ref[j][:, a:b]
        o_ref[:, IN_W:] = jnp.zeros((tr, IN_W_PAD - IN_W), o_ref.dtype)

    return pl.pallas_call(
        body, name=name, grid=(d // tr,), in_specs=[pl.BlockSpec((N_DEV, tr, w), lambda i: (0, i, 0))],
        out_specs=pl.BlockSpec((tr, IN_W_PAD), lambda i: (i, 0)), out_shape=SDS((d, IN_W_PAD), win_all.dtype),
        compiler_params=_params("parallel"),
    )(win_all)


def _unpack_win(dwin_p, *, tr, name):
    d = dwin_p.shape[0]
    w = IN_W // N_DEV

    def body(i_ref, o_ref):
        for j, a, b, src in _win_pieces(w):
            o_ref[j % 2, j // 2, :, a:b] = i_ref[:, src:src + b - a]

    return pl.pallas_call(
        body, name=name, grid=(d // tr,), in_specs=[pl.BlockSpec((tr, IN_W_PAD), lambda i: (i, 0))],
        out_specs=pl.BlockSpec((2, 4, tr, w), lambda i: (0, 0, i, 0)), out_shape=SDS((2, 4, d, w), dwin_p.dtype),
        compiler_params=_params("parallel"),
    )(dwin_p)


def _rms_fwd(x, g, *, tm, name):
    s, d = x.shape

    def body(x_ref, g_ref, h_ref):
        h_ref[...] = _rms(x_ref[...], g_ref[...]).astype(BF16)

    return pl.pallas_call(
        body, name=name, grid=(s // tm,),
        in_specs=[pl.BlockSpec((tm, d), lambda i: (i, 0)), pl.BlockSpec((1, d), lambda i: (0, 0))],
        out_specs=pl.BlockSpec((tm, d), lambda i: (i, 0)), out_shape=SDS((s, d), BF16),
        compiler_params=_params("parallel"),
    )(x, g)


def _rms_bwd(x, g, dh, dres, *, tm, name):
    s, d = x.shape
    want_dx = dres is not None

    def body(*refs):
        if want_dx:
            x_ref, g_ref, dh_ref, dres_ref, dx_ref, dxb_ref, dg_ref = refs
        else:
            x_ref, g_ref, dh_ref, dg_ref = refs
        _, vjp = jax.vjp(_rms, x_ref[...], g_ref[...])
        dx, dg = vjp(dh_ref[...])
        if want_dx:
            dx = dres_ref[...] + dx
            dx_ref[...] = dx
            dxb_ref[...] = dx.astype(BF16)

        @pl.when(pl.program_id(0) == 0)
        def _():
            dg_ref[...] = jnp.zeros_like(dg_ref)

        dg_ref[...] += dg

    row = pl.BlockSpec((tm, d), lambda i: (i, 0))
    vec = pl.BlockSpec((1, d), lambda i: (0, 0))
    if want_dx:
        return pl.pallas_call(
            body, name=name, grid=(s // tm,), in_specs=[row, vec, row, row], out_specs=[row, row, vec],
            out_shape=[SDS((s, d), F32), SDS((s, d), BF16), SDS((1, d), F32)], compiler_params=_params("arbitrary"),
        )(x, g, dh, dres)
    return None, None, pl.pallas_call(
        body, name=name, grid=(s // tm,), in_specs=[row, vec, row], out_specs=vec,
        out_shape=SDS((1, d), F32), compiler_params=_params("arbitrary"),
    )(x, g, dh)


FFN_TN = 256
FFN_TN_FWD = 512
FFN_HALF_PAD = 192


def _ffn_fwd(x, gain, w3, tag, *, tm=1024):
    s, d = x.shape
    f = w3.shape[1]
    tn = FFN_TN_FWD
    nj = f // tn
    tm = min(tm, s)

    def body(x_ref, gain_ref, wg_ref, wu_ref, wd_ref, y_ref, h_ref, g_ref, u_ref, acc_s):
        j = pl.program_id(1)

        @pl.when(j == 0)
        def _():
            h_ref[...] = _rms(x_ref[...], gain_ref[...]).astype(BF16)
            acc_s[...] = jnp.zeros_like(acc_s)

        hv = h_ref[...]
        g = _mm_raw(hv, wg_ref[...], False, True)
        u = _mm_raw(hv, wu_ref[...], False, True)
        g_ref[...] = g.astype(BF16)
        u_ref[...] = u.astype(BF16)
        acc_s[...] += _mm_raw(_silu_mul(g, u), wd_ref[...])

        @pl.when(j == nj - 1)
        def _():
            y_ref[...] = x_ref[...] + 0.5 * acc_s[...]

    row = pl.BlockSpec((tm, d), lambda i, j: (i, 0))
    tile = pl.BlockSpec((tm, tn), lambda i, j: (i, j))
    y, h, g, u = pl.pallas_call(
        body, name=f"{tag}_fwd", grid=(s // tm, nj),
        in_specs=[row, pl.BlockSpec((1, d), lambda i, j: (0, 0))]
        + [pl.BlockSpec((None, tn, d), functools.partial(lambda i, j, k: (k, j, 0), k=k)) for k in range(3)],
        out_specs=[row, row, tile, tile],
        out_shape=[SDS((s, d), F32), SDS((s, d), BF16), SDS((s, f), BF16), SDS((s, f), BF16)],
        scratch_shapes=[pltpu.VMEM((tm, d), F32)],
        compiler_params=_params("parallel", "arbitrary"),
    )(x, gain, w3, w3, w3)
    return y, (h, g, u)


def _ffn_bwd_part(dyb, w3, saved, first, count, dh_init, *, name):
    h, g, u = saved
    s, d = h.shape
    tn = FFN_TN

    def body(*refs):
        if dh_init is None:
            dy_ref, h_ref, wd_ref, wg_ref, wu_ref, g_ref, u_ref, dh_ref, dw3_ref, dg_s, du_s, a_s = refs
        else:
            dy_ref, h_ref, wd_ref, wg_ref, wu_ref, g_ref, u_ref, dh0_ref, dh_ref, dw3_ref, dg_s, du_s, a_s = refs
        j = pl.program_id(0)

        @pl.when(j == 0)
        def _():
            dh_ref[...] = jnp.zeros_like(dh_ref) if dh_init is None else dh0_ref[...]
            for ref in (dg_s, du_s, a_s):
                ref[...] = jnp.zeros_like(ref)

        now, before = j % 2, 1 - j % 2
        dyv = dy_ref[...]
        hv = h_ref[...]
        dg, du, a = dg_s[before], du_s[before], a_s[before]
        dh_ref[...] += _mm_raw(dg, wg_ref[...]) + _mm_raw(du, wu_ref[...])
        dw3_ref[0] = _mm_raw(dg, hv, True, False).astype(BF16)
        dw3_ref[1] = _mm_raw(du, hv, True, False).astype(BF16)
        dw3_ref[2] = (_mm_raw(a, dyv, True, False) * 0.5).astype(BF16)

        da = _mm_raw(dyv, wd_ref[...], False, True) * 0.5
        a, vjp = jax.vjp(_silu_mul, g_ref[...].astype(F32), u_ref[...].astype(F32))
        dg, du = vjp(da)
        dg_s[now] = dg.astype(BF16)
        du_s[now] = du.astype(BF16)
        a_s[now] = a.astype(BF16)

    this = lambda j: first + jnp.minimum(j, count - 1)
    last = lambda j: first + jnp.maximum(j - 1, 0)
    full = pl.BlockSpec((s, d), lambda j: (0, 0))
    once = pl.BlockSpec((s, d), lambda j: (0, 0), pipeline_mode=pl.Buffered(1))
    tile = pl.BlockSpec((s, tn), lambda j: (0, this(j)))
    in_specs = [once, once, pl.BlockSpec((None, tn, d), lambda j: (2, this(j), 0)),
                pl.BlockSpec((None, tn, d), lambda j: (0, last(j), 0)), pl.BlockSpec((None, tn, d), lambda j: (1, last(j), 0)),
                tile, tile]
    args = [dyb, h, w3, w3, w3, g, u]
    if dh_init is not None:
        in_specs.append(once)
        args.append(dh_init)
    return pl.pallas_call(
        body, name=name, grid=(count + 1,), in_specs=in_specs,
        out_specs=[full, pl.BlockSpec((3, tn, d), lambda j: (0, jnp.maximum(j - 1, 0), 0))],
        out_shape=[SDS((s, d), F32), SDS((3, count * tn, d), BF16)],
        scratch_shapes=[pltpu.VMEM((2, s, tn), BF16)] * 3,
        compiler_params=_params("arbitrary"),
    )(*args)


def _loss_bwd(y, target, *, tm, name):
    s, d = y.shape

    def body(y_ref, t_ref, dy_ref, dyb_ref, l_ref):
        diff = y_ref[...] - t_ref[...]
        dy_ref[...] = diff * (1.0 / d)
        dyb_ref[...] = (diff * (1.0 / d)).astype(BF16)

        @pl.when(pl.program_id(0) == 0)
        def _():
            l_ref[...] = jnp.zeros_like(l_ref)

        l_ref[...] += 0.5 * jnp.sum(jnp.mean(diff * diff, axis=-1, keepdims=True), axis=0, keepdims=True)

    row = pl.BlockSpec((tm, d), lambda i: (i, 0))
    return pl.pallas_call(
        body, name=name, grid=(s // tm,), in_specs=[row, row],
        out_specs=[row, row, pl.BlockSpec((1, 1), lambda i: (0, 0))],
        out_shape=[SDS((s, d), F32), SDS((s, d), BF16), SDS((1, 1), F32)],
        compiler_params=_params("arbitrary"),
    )(y, target)


def _bucket_table():
    qi = np.arange(BLOCK)[:, None]
    kj = np.arange(2 * BLOCK)[None, :]
    dist = np.maximum(qi + BLOCK - kj, 0)
    max_exact = N_BUCKETS // 2
    d = np.maximum(dist, 1).astype(np.float32)
    large = max_exact + (np.log(d / np.float32(max_exact)) / np.float32(math.log(MAX_DISTANCE / max_exact))
                         * np.float32(N_BUCKETS - max_exact)).astype(np.int32)
    large = np.minimum(large, N_BUCKETS - 1)
    band = np.where(dist < max_exact, dist, large).astype(np.int32)
    return np.where(np.tril(np.ones((BLOCK, BLOCK), bool)), band[:, BLOCK:], band[:, :BLOCK])


SWA_STACK = SWA_GROUP * BLOCK


def _swa_masks(n):
    qi = lax.broadcasted_iota(jnp.int32, (SWA_STACK, BLOCK), 0) % BLOCK
    kj = lax.broadcasted_iota(jnp.int32, (SWA_STACK, BLOCK), 1)
    own = kj <= qi
    return own, own | (n > 0)


def _swa_group(q, kp, kc, vp, vc, qg, kg, sink, bias, own, valid):
    qn = _rms(q, qg)
    s = jnp.where(own, _mm(qn, _rms(kc, kg), False, True), _mm(qn, _rms(kp, kg), False, True))
    s = s * (HEAD_DIM ** -0.5) + bias
    s = jnp.where(valid, s, -jnp.inf)
    m = lax.stop_gradient(jnp.maximum(jnp.max(s, axis=-1, keepdims=True), sink))
    p = jnp.exp(s - m)
    p = p / (jnp.sum(p, axis=-1, keepdims=True) + jnp.exp(sink - m))
    return _mm(jnp.where(own, p, 0.0), vc) + _mm(jnp.where(own, 0.0, p), vp)


def _swa_bias_table(rb_ref, bucket, bias_s):
    for h in range(SWA_HEADS):
        acc = jnp.zeros((BLOCK, BLOCK), F32)
        for b in range(N_BUCKETS):
            acc = jnp.where(bucket == b, rb_ref[b, h], acc)
        bias_s[h // SWA_GROUP, (h % SWA_GROUP) * BLOCK:(h % SWA_GROUP + 1) * BLOCK, :] = acc


def _swa_stack(ref, g):
    return jnp.concatenate([ref[:, (g * SWA_GROUP + hh) * HEAD_DIM:(g * SWA_GROUP + hh + 1) * HEAD_DIM]
                            for hh in range(SWA_GROUP)], axis=0)


def _swa_unstack(ref, g, stacked):
    for hh in range(SWA_GROUP):
        h = g * SWA_GROUP + hh
        ref[:, h * HEAD_DIM:(h + 1) * HEAD_DIM] = stacked[hh * BLOCK:(hh + 1) * BLOCK]


def _swa_sink_column(sink_ref, g):
    head = lax.broadcasted_iota(jnp.int32, (SWA_STACK, 1), 0) // BLOCK
    col = jnp.zeros((SWA_STACK, 1), F32)
    for hh in range(SWA_GROUP):
        col = jnp.where(head == hh, sink_ref[g * SWA_GROUP + hh], col)
    return col


def _swa_band(kvp_ref, kvc_ref, g):
    k = slice(g * HEAD_DIM, (g + 1) * HEAD_DIM)
    v = slice(SWA_KV_W + g * HEAD_DIM, SWA_KV_W + (g + 1) * HEAD_DIM)
    return kvp_ref[:, k], kvc_ref[:, k], kvp_ref[:, v], kvc_ref[:, v]


def _swa_specs(order):
    kvc = COL_SKV // (2 * SWA_KV_W)
    return [
        pl.BlockSpec((BLOCK, SWA_Q_W), lambda t: (order(t), 0)),
        pl.BlockSpec((BLOCK, 2 * SWA_KV_W), lambda t: (jnp.maximum(order(t) - 1, 0), kvc)),
        pl.BlockSpec((BLOCK, 2 * SWA_KV_W), lambda t: (order(t), kvc)),
        pl.BlockSpec((1, HEAD_DIM), lambda t: (0, 0)),
        pl.BlockSpec((1, HEAD_DIM), lambda t: (0, 0)),
        pl.BlockSpec(memory_space=pltpu.SMEM),
        pl.BlockSpec(memory_space=pltpu.SMEM),
        pl.BlockSpec((BLOCK, BLOCK), lambda t: (0, 0)),
    ]


def _swa_fwd(p, qg, kg, sinks, rel_bias, *, name):
    s = p.shape[0]
    nb = s // BLOCK

    def body(q_ref, kvp_ref, kvc_ref, qg_ref, kg_ref, sink_ref, rb_ref, bucket_ref, y_ref, bias_s):
        n = pl.program_id(0)

        @pl.when(n == 0)
        def _():
            _swa_bias_table(rb_ref, bucket_ref[...], bias_s)

        own, valid = _swa_masks(n)
        for g in range(SWA_KV_HEADS):
            out = _swa_group(_swa_stack(q_ref, g), *_swa_band(kvp_ref, kvc_ref, g), qg_ref[...], kg_ref[...],
                             _swa_sink_column(sink_ref, g), bias_s[g], own, valid)
            _swa_unstack(y_ref, g, out)

    return pl.pallas_call(
        body, name=name, grid=(nb,), in_specs=_swa_specs(lambda t: t),
        out_specs=pl.BlockSpec((BLOCK, SWA_Q_W), lambda t: (t, 0)), out_shape=SDS((s, SWA_Q_W), F32),
        scratch_shapes=[pltpu.VMEM((SWA_KV_HEADS, SWA_STACK, BLOCK), F32)],
        compiler_params=_params("arbitrary"),
    )(p, p, p, qg, kg, sinks, rel_bias, jnp.asarray(_bucket_table()))


def _swa_bwd(p, qg, kg, sinks, rel_bias, dy_all, *, name):
    s = p.shape[0]
    nb = s // BLOCK

    def body(q_ref, kvp_ref, kvc_ref, qg_ref, kg_ref, sink_ref, rb_ref, bucket_ref, dy_ref,
             dq_ref, dkv_ref, dqg_ref, dkg_ref, dsink_ref, drb_ref, bias_s, dbias_s, carry_s):
        t = pl.program_id(0)
        n = nb - 1 - t

        @pl.when(t == 0)
        def _():
            _swa_bias_table(rb_ref, bucket_ref[...], bias_s)
            dbias_s[...] = jnp.zeros_like(dbias_s)
            carry_s[...] = jnp.zeros_like(carry_s)
            dqg_ref[...] = jnp.zeros_like(dqg_ref)
            dkg_ref[...] = jnp.zeros_like(dkg_ref)
            dsink_ref[...] = jnp.zeros_like(dsink_ref)
            drb_ref[...] = jnp.zeros_like(drb_ref)

        own, valid = _swa_masks(n)
        lane = lax.broadcasted_iota(jnp.int32, (1, BLOCK), 1)
        dqg = jnp.zeros((1, HEAD_DIM), F32)
        dkg = jnp.zeros((1, HEAD_DIM), F32)
        dsink_vec = jnp.zeros((1, BLOCK), F32)
        for g in range(SWA_KV_HEADS):
            _, vjp = jax.vjp(functools.partial(_swa_group, own=own, valid=valid), _swa_stack(q_ref, g),
                             *_swa_band(kvp_ref, kvc_ref, g), qg_ref[...], kg_ref[...], _swa_sink_column(sink_ref, g),
                             bias_s[g])
            dq, dkp, dkc, dvp, dvc, dqg_g, dkg_g, dsink_col, dbias = vjp(_swa_stack(dy_ref, g))
            _swa_unstack(dq_ref, g, dq)
            dqg += dqg_g
            dkg += dkg_g
            dbias_s[g] += dbias
            for hh in range(SWA_GROUP):
                dsink_h = jnp.sum(dsink_col[hh * BLOCK:(hh + 1) * BLOCK], axis=0, keepdims=True)
                dsink_vec += jnp.where(lane == g * SWA_GROUP + hh, dsink_h, 0.0)
            lo = g * HEAD_DIM
            dkv_ref[:, lo:lo + HEAD_DIM] = dkc + carry_s[g]
            carry_s[g] = dkp
            lo += SWA_KV_W
            dkv_ref[:, lo:lo + HEAD_DIM] = dvc + carry_s[SWA_KV_HEADS + g]
            carry_s[SWA_KV_HEADS + g] = dvp
        dqg_ref[...] += dqg
        dkg_ref[...] += dkg
        dsink_ref[...] += dsink_vec

        @pl.when(t == nb - 1)
        def _():
            bucket = bucket_ref[...]
            row = lax.broadcasted_iota(jnp.int32, (N_BUCKETS, BLOCK), 0)
            col = lax.broadcasted_iota(jnp.int32, (N_BUCKETS, BLOCK), 1)
            acc = jnp.zeros((N_BUCKETS, BLOCK), F32)
            for h in range(SWA_HEADS):
                dbias = dbias_s[h // SWA_GROUP, (h % SWA_GROUP) * BLOCK:(h % SWA_GROUP + 1) * BLOCK, :]
                for b in range(N_BUCKETS):
                    part = jnp.sum(jnp.where(bucket == b, dbias, 0.0), axis=1, keepdims=True)
                    val = jnp.sum(part, axis=0, keepdims=True)
                    acc = acc + jnp.where((row == b) & (col == h), val, 0.0)
            drb_ref[...] = acc

    order = lambda t: nb - 1 - t
    vec = pl.BlockSpec((1, HEAD_DIM), lambda t: (0, 0))
    return pl.pallas_call(
        body, name=name, grid=(nb,),
        in_specs=_swa_specs(order) + [pl.BlockSpec((BLOCK, SWA_Q_W), lambda t: (order(t), 0))],
        out_specs=[pl.BlockSpec((BLOCK, SWA_Q_W), lambda t: (order(t), 0)),
                   pl.BlockSpec((BLOCK, 2 * SWA_KV_W), lambda t: (order(t), 0)),
                   vec, vec, pl.BlockSpec((1, BLOCK), lambda t: (0, 0)),
                   pl.BlockSpec((N_BUCKETS, BLOCK), lambda t: (0, 0))],
        out_shape=[SDS((s, SWA_Q_W), F32), SDS((s, 2 * SWA_KV_W), F32), SDS((1, HEAD_DIM), F32),
                   SDS((1, HEAD_DIM), F32), SDS((1, BLOCK), F32), SDS((N_BUCKETS, BLOCK), F32)],
        scratch_shapes=[pltpu.VMEM((SWA_KV_HEADS, SWA_STACK, BLOCK), F32),
                        pltpu.VMEM((SWA_KV_HEADS, SWA_STACK, BLOCK), F32),
                        pltpu.VMEM((2 * SWA_KV_HEADS, BLOCK, HEAD_DIM), F32)],
        compiler_params=_params("arbitrary"),
    )(p, p, p, qg, kg, sinks, rel_bias, jnp.asarray(_bucket_table()), dy_all)


def _mem_head(q, k, v, qg, kg):
    qn = _rms(q, qg)
    kn = _rms(k, kg)
    s = _mm(qn, kn, False, True) * (HEAD_DIM ** -0.5)
    m = lax.stop_gradient(jnp.max(s, axis=-1, keepdims=True))
    e = jnp.exp(s - m)
    return _mm(e / jnp.sum(e, axis=-1, keepdims=True), v)


def _mem_fwd(p, kv, qg, kg, *, tq, name):
    s = p.shape[0]
    m = kv.shape[0]

    def body(q_ref, kv_ref, qg_ref, kg_ref, y_ref):
        for h in range(MEM_HEADS):
            cols = slice(h * HEAD_DIM, (h + 1) * HEAD_DIM)
            vcols = slice(MEM_Q_W + h * HEAD_DIM, MEM_Q_W + (h + 1) * HEAD_DIM)
            y_ref[:, cols] = _mem_head(q_ref[:, cols], kv_ref[:, cols], kv_ref[:, vcols], qg_ref[...], kg_ref[...])

    vec = pl.BlockSpec((1, HEAD_DIM), lambda t: (0, 0))
    return pl.pallas_call(
        body, name=name, grid=(s // tq,),
        in_specs=[pl.BlockSpec((tq, MEM_Q_W), lambda t: (t, COL_MQ // MEM_Q_W)),
                  pl.BlockSpec((m, 2 * MEM_Q_W), lambda t: (0, 0)), vec, vec],
        out_specs=pl.BlockSpec((tq, MEM_Q_W), lambda t: (t, 0)), out_shape=SDS((s, MEM_Q_W), F32),
        compiler_params=_params("parallel"),
    )(p, kv, qg, kg)


def _mem_bwd(p, kv, qg, kg, dy_all, *, tq, name):
    s = p.shape[0]
    m = kv.shape[0]

    def body(q_ref, kv_ref, qg_ref, kg_ref, dy_ref, dq_ref, dkv_ref, dqg_ref, dkg_ref):
        @pl.when(pl.program_id(0) == 0)
        def _():
            dkv_ref[...] = jnp.zeros_like(dkv_ref)
            dqg_ref[...] = jnp.zeros_like(dqg_ref)
            dkg_ref[...] = jnp.zeros_like(dkg_ref)

        dqg = jnp.zeros((1, HEAD_DIM), F32)
        dkg = jnp.zeros((1, HEAD_DIM), F32)
        for h in range(MEM_HEADS):
            cols = slice(h * HEAD_DIM, (h + 1) * HEAD_DIM)
            vcols = slice(MEM_Q_W + h * HEAD_DIM, MEM_Q_W + (h + 1) * HEAD_DIM)
            _, vjp = jax.vjp(_mem_head, q_ref[:, cols], kv_ref[:, cols], kv_ref[:, vcols], qg_ref[...], kg_ref[...])
            dq, dk, dv, dqg_h, dkg_h = vjp(dy_ref[:, cols])
            dq_ref[:, cols] = dq
            dkv_ref[:, cols] += dk
            dkv_ref[:, vcols] += dv
            dqg += dqg_h
            dkg += dkg_h
        dqg_ref[...] += dqg
        dkg_ref[...] += dkg

    vec = pl.BlockSpec((1, HEAD_DIM), lambda t: (0, 0))
    full = pl.BlockSpec((m, 2 * MEM_Q_W), lambda t: (0, 0))
    dy_col = (SWA_Q_W + GLA_V_W) // MEM_Q_W
    return pl.pallas_call(
        body, name=name, grid=(s // tq,),
        in_specs=[pl.BlockSpec((tq, MEM_Q_W), lambda t: (t, COL_MQ // MEM_Q_W)), full, vec, vec,
                  pl.BlockSpec((tq, MEM_Q_W), lambda t: (t, dy_col))],
        out_specs=[pl.BlockSpec((tq, MEM_Q_W), lambda t: (t, 0)), full, vec, vec],
        out_shape=[SDS((s, MEM_Q_W), F32), SDS((m, 2 * MEM_Q_W), F32), SDS((1, HEAD_DIM), F32), SDS((1, HEAD_DIM), F32)],
        compiler_params=_params("arbitrary"),
    )(p, kv, qg, kg, dy_all)


GLA_ROWS = 256


GLA_GROUP = 4


def _gla_consts():
    c, h, r = GLA_CHUNK, GLA_HEADS, GLA_GROUP * GLA_CHUNK
    i2 = lax.broadcasted_iota(jnp.int32, (c, c), 0)
    j2 = lax.broadcasted_iota(jnp.int32, (c, c), 1)
    slab_q = lax.broadcasted_iota(jnp.int32, (h, r, GLA_QK_W), 0)
    lane_q = lax.broadcasted_iota(jnp.int32, (h, r, GLA_QK_W), 2)
    row_a = lax.broadcasted_iota(jnp.int32, (h * r, r), 0) % r
    col_a = lax.broadcasted_iota(jnp.int32, (h * r, r), 1)
    slab_o = lax.broadcasted_iota(jnp.int32, (h, r, GLA_V_W), 0)
    lane_o = lax.broadcasted_iota(jnp.int32, (h, r, GLA_V_W), 2)
    row_s = lax.broadcasted_iota(jnp.int32, (GLA_V_W, GLA_QK_W), 0)
    col_s = lax.broadcasted_iota(jnp.int32, (GLA_V_W, GLA_QK_W), 1)
    return dict(
        ltri=(j2 <= i2).astype(F32),
        m_q=(slab_q == lane_q // GLA_DK).astype(F32),
        causal=(col_a <= row_a) & (col_a // c == row_a // c),
        m_o=(slab_o == lane_o // GLA_DV).astype(F32),
        m_s=(row_s // GLA_DV == col_s // GLA_DK).astype(F32),
    )


def _gla_step(q, k, v, z, bg, st, c):
    h = GLA_HEADS
    kt, qt, qe, decay = [], [], [], []
    for qc, kc, zc in zip(q, k, z):
        la = _log_sigmoid(zc + bg) * (1.0 / GLA_TAU)
        b = _mmf(c["ltri"], la)
        bl = jnp.sum(la, axis=0, keepdims=True)
        qs = qc * (GLA_DK ** -0.5)
        kt.append(kc * jnp.exp(bl - b))
        qt.append(qs * jnp.exp(b - bl))
        qe.append(qs * jnp.exp(b))
        decay.append(jnp.exp(bl))
    o_intra = []
    rows = GLA_GROUP * GLA_CHUNK
    for lo in range(0, len(q), GLA_GROUP):
        qt_all, kt_all, v_all = (jnp.concatenate(parts[lo:lo + GLA_GROUP], axis=0) for parts in (qt, kt, v))
        q_stack = (jnp.broadcast_to(qt_all[None], (h, rows, GLA_QK_W)) * c["m_q"]).reshape(h * rows, GLA_QK_W)
        a = jnp.where(c["causal"], _mm3(q_stack, kt_all, False, True), 0.0)
        o_stack = _mm(a, v_all)
        o_intra.append(jnp.sum(o_stack.reshape(h, rows, GLA_V_W) * c["m_o"], axis=0))
    o_intra = jnp.concatenate(o_intra, axis=0)
    o_inter = []
    for qec, ktc, vc, dc in zip(qe, kt, v, decay):
        o_inter.append(_mm(qec, st, False, True))
        st = st * dc + _mm(vc, ktc, True, False) * c["m_s"]
    return o_intra + jnp.concatenate(o_inter, axis=0), st


def _gla_post(o, gg, gain, g64):
    ms = _mmf(o * o, g64) * (1.0 / GLA_DV)
    return o * lax.rsqrt(ms + EPS) * gain * jax.nn.silu(gg)


def _gla_g64():
    r = lax.broadcasted_iota(jnp.int32, (GLA_V_W, GLA_V_W), 0)
    c = lax.broadcasted_iota(jnp.int32, (GLA_V_W, GLA_V_W), 1)
    return (r // GLA_DV == c // GLA_DV).astype(F32)


def _gla_in_specs(order):
    r = GLA_ROWS
    return [
        pl.BlockSpec((r, GLA_QK_W), lambda t: (order(t), COL_GQ // GLA_QK_W)),
        pl.BlockSpec((r, GLA_QK_W), lambda t: (order(t), COL_GK // GLA_QK_W)),
        pl.BlockSpec((r, GLA_V_W), lambda t: (order(t), COL_GV // GLA_V_W)),
        pl.BlockSpec((r, GLA_V_W), lambda t: (order(t), COL_GG // GLA_V_W)),
        pl.BlockSpec((r, GLA_QK_W), lambda t: (order(t), 0)),
        pl.BlockSpec((1, GLA_QK_W), lambda t: (0, 0)),
        pl.BlockSpec((1, GLA_V_W), lambda t: (0, 0)),
    ]


def _gla_pieces(q_ref, k_ref, v_ref, z_ref, cps):
    chunk = lambda ref: [ref[ci * GLA_CHUNK:(ci + 1) * GLA_CHUNK, :] for ci in range(cps)]
    return chunk(q_ref), chunk(k_ref), chunk(v_ref), chunk(z_ref)


def _gla_fwd(p, z, bg, gain, *, name):
    s = p.shape[0]
    r = GLA_ROWS
    cps = r // GLA_CHUNK

    def body(q_ref, k_ref, v_ref, gg_ref, z_ref, bg_ref, gain_ref, y_ref, oraw_ref, stsave_ref, st_s):
        @pl.when(pl.program_id(0) == 0)
        def _():
            st_s[...] = jnp.zeros_like(st_s)

        st = st_s[...]
        stsave_ref[0] = st
        o, st = _gla_step(*_gla_pieces(q_ref, k_ref, v_ref, z_ref, cps), bg_ref[...], st, _gla_consts())
        oraw_ref[...] = o
        st_s[...] = st
        y_ref[...] = _gla_post(o, gg_ref[...], gain_ref[...], _gla_g64())

    rowv = pl.BlockSpec((r, GLA_V_W), lambda t: (t, 0))
    return pl.pallas_call(
        body, name=name, grid=(s // r,), in_specs=_gla_in_specs(lambda t: t),
        out_specs=[rowv, rowv, pl.BlockSpec((1, GLA_V_W, GLA_QK_W), lambda t: (t, 0, 0))],
        out_shape=[SDS((s, GLA_V_W), F32), SDS((s, GLA_V_W), F32), SDS((s // r, GLA_V_W, GLA_QK_W), F32)],
        scratch_shapes=[pltpu.VMEM((GLA_V_W, GLA_QK_W), F32)],
        compiler_params=_params("arbitrary"),
    )(p, p, p, p, z, bg, gain)


def _gla_bwd(p, z, bg, gain, oraw, stsave, dy_all, *, name):
    s = p.shape[0]
    r = GLA_ROWS
    cps = r // GLA_CHUNK
    nsteps = s // r
    w_qkvg = 2 * GLA_QK_W + 2 * GLA_V_W

    def body(q_ref, k_ref, v_ref, gg_ref, z_ref, bg_ref, gain_ref, oraw_ref, stsave_ref, dy_ref,
             dqkvg_ref, dz_ref, dbg_ref, dgain_ref, dst_s):
        @pl.when(pl.program_id(0) == 0)
        def _():
            dst_s[...] = jnp.zeros_like(dst_s)
            dbg_ref[...] = jnp.zeros_like(dbg_ref)
            dgain_ref[...] = jnp.zeros_like(dgain_ref)

        _, vjp = jax.vjp(functools.partial(_gla_post, g64=_gla_g64()), oraw_ref[...], gg_ref[...], gain_ref[...])
        do, dgg, dgain = vjp(dy_ref[...])
        dqkvg_ref[:, 2 * GLA_QK_W + GLA_V_W:] = dgg
        dgain_ref[...] += dgain
        _, vjp = jax.vjp(functools.partial(_gla_step, c=_gla_consts()), *_gla_pieces(q_ref, k_ref, v_ref, z_ref, cps),
                         bg_ref[...], stsave_ref[0])
        dq, dk, dv, dz, dbg, dst = vjp((do, dst_s[...]))
        for ci in range(cps):
            rows = slice(ci * GLA_CHUNK, (ci + 1) * GLA_CHUNK)
            dqkvg_ref[rows, 0:GLA_QK_W] = dq[ci]
            dqkvg_ref[rows, GLA_QK_W:2 * GLA_QK_W] = dk[ci]
            dqkvg_ref[rows, 2 * GLA_QK_W:2 * GLA_QK_W + GLA_V_W] = dv[ci]
            dz_ref[rows, :] = dz[ci]
        dst_s[...] = dst
        dbg_ref[...] += dbg

    order = lambda t: nsteps - 1 - t
    rowv = pl.BlockSpec((r, GLA_V_W), lambda t: (order(t), 0))
    return pl.pallas_call(
        body, name=name, grid=(nsteps,),
        in_specs=_gla_in_specs(order) + [
            rowv, pl.BlockSpec((1, GLA_V_W, GLA_QK_W), lambda t: (order(t), 0, 0)),
            pl.BlockSpec((r, GLA_V_W), lambda t: (order(t), SWA_Q_W // GLA_V_W))],
        out_specs=[pl.BlockSpec((r, w_qkvg), lambda t: (order(t), 0)), pl.BlockSpec((r, GLA_QK_W), lambda t: (order(t), 0)),
                   pl.BlockSpec((1, GLA_QK_W), lambda t: (0, 0)), pl.BlockSpec((1, GLA_V_W), lambda t: (0, 0))],
        out_shape=[SDS((s, w_qkvg), F32), SDS((s, GLA_QK_W), F32), SDS((1, GLA_QK_W), F32), SDS((1, GLA_V_W), F32)],
        scratch_shapes=[pltpu.VMEM((GLA_V_W, GLA_QK_W), F32)],
        compiler_params=_params("arbitrary"),
    )(p, p, p, p, z, bg, gain, oraw, stsave, dy_all)


def _local_step(x, mem, target, small, big, on_grads):
    g1, gmix, gmem, g2, sqg, skg, sinks, rel_bias, wgu, bg, gla_gain, mqg, mkg = small
    w3_1, win_p, wkv, wout, gather_ffn2 = big
    wgu_pad = jnp.zeros((GLA_QK_W, GLA_QK_W), BF16).at[:GLA_RANK].set(wgu.astype(BF16))
    gain256 = jnp.tile(gla_gain, (1, GLA_HEADS))

    x1, saved1 = _ffn_fwd(x, g1, w3_1, "ffn1")
    w3_2 = gather_ffn2(x1)
    h = _rms_fwd(x1, gmix, tm=512, name="mix_rms")
    p = _matmul(h, win_p, tm=1024, tn=IN_W_PAD, name="mix_in")
    hm = _rms_fwd(mem, gmem, tm=256, name="mem_rms")
    kv = _matmul(hm, wkv, tm=256, tn=512, name="mem_kv")
    p_glr = p[:, COL_GLR:]
    z = _matmul(p_glr, wgu_pad, tm=1024, tn=GLA_QK_W, name="gla_gate")
    y_swa = _swa_fwd(p, sqg, skg, sinks, rel_bias, name="swa_fwd")
    y_gla, oraw, stsave = _gla_fwd(p, z, bg, gain256, name="gla_fwd")
    y_mem = _mem_fwd(p, kv, mqg, mkg, tq=512, name="mem_fwd")
    x2 = _matmul([y_swa, y_gla, y_mem], wout, b_blocks=[0, 2, 3], tm=1024, tn=1024, res=x1, name="mix_out")
    x3, saved2 = _ffn_fwd(x2, g2, w3_2, "ffn2")

    dy, dyb, loss = _loss_bwd(x3, target, tm=512, name="loss")
    tiles = w3_2.shape[1] // FFN_TN
    dh2, dw3_2 = _ffn_bwd_part(dyb, w3_2, saved2, 0, tiles, None, name="ffn2_bwd")
    dx2, dx2b, dg2 = _rms_bwd(x2, g2, dh2, dy, tm=512, name="ffn2_drms")
    dx2b = on_grads("ffn2", [dw3_2.reshape(3, 2, -1, dw3_2.shape[-1])], dx2b)
    dy_all = _matmul(dx2b, wout, tb=True, tm=1024, tn=1024, name="mix_dy")
    dwout = _matmul(jnp.concatenate([y_swa, y_gla, y_mem], axis=1), dx2b, ta=True, tm=512, tn=1024, out_dtype=BF16,
                    name="mix_dw_out")
    dq_swa, dkv_swa, dsqg, dskg, dsink, drb = _swa_bwd(p, sqg, skg, sinks, rel_bias, dy_all, name="swa_bwd")
    dqkvg, dz, dbg, dgain256 = _gla_bwd(p, z, bg, gain256, oraw, stsave, dy_all, name="gla_bwd")
    dmq, dkv_mem, dmqg, dmkg = _mem_bwd(p, kv, mqg, mkg, dy_all, tq=512, name="mem_bwd")
    dglr = _matmul(dz, wgu_pad, tb=True, tm=1024, tn=GLA_QK_W, name="gla_gate_dx")
    dwgu_pad = _matmul(p_glr, dz, ta=True, tm=GLA_QK_W, tn=GLA_QK_W, name="gla_gate_dw")
    dp = jnp.concatenate([dq_swa, dkv_swa, dqkvg, dmq, dglr], axis=1)
    dh = _matmul(dp, win_p, tb=True, tm=1024, tn=1024, name="mix_dh")
    dwin_p = _matmul(h, dp, ta=True, tm=1024, tn=640, out_dtype=BF16, name="mix_dw_in")
    dx1, dx1b, dgmix = _rms_bwd(x1, gmix, dh, dx2, tm=512, name="mix_drms")
    dwkv = _matmul(hm, dkv_mem, ta=True, tm=512, tn=512, out_dtype=BF16, name="mem_dw_kv")
    dx1b = on_grads("mix", (dwin_p, dwkv, dwout), dx1b)
    dhm = _matmul(dkv_mem, wkv, tb=True, tm=256, tn=512, name="mem_dh")
    _, _, dgmem = _rms_bwd(mem, gmem, dhm, None, tm=256, name="mem_drms")
    dh1, dw3_1a = _ffn_bwd_part(dx1b, w3_1, saved1, 0, tiles // 2, None, name="ffn1_bwd_a")
    dgla_gain = dgain256.reshape(GLA_HEADS, GLA_DV).sum(axis=0, keepdims=True)
    dsmall = [dgmix, dgmem, dg2, dsqg, dskg, dsink[:, :SWA_HEADS], drb[:, :SWA_HEADS], dwgu_pad[:GLA_RANK], dbg,
              dgla_gain, dmqg, dmkg, loss]
    dh1, dgmem, dwgu_pad = on_grads("ffn1a", [dw3_1a[:, None]], (dh1, dgmem, dwgu_pad), small=dsmall)
    dh1, dw3_1b = _ffn_bwd_part(dx1b, w3_1, saved1, tiles // 2, tiles // 2, dh1, name="ffn1_bwd_b")
    dx, _, dg1 = _rms_bwd(x, g1, dh1, dx1, tm=512, name="ffn1_drms")
    on_grads("ffn1b", [dw3_1b[:, None]], None, small=[dg1])
    return dx


def _mesh_place():
    x, y, c = lax.axis_index("x"), lax.axis_index("y"), lax.axis_index("c")
    other_chips = [(1 - x, y), (x, 1 - y), (1 - x, 1 - y)]
    return x, y, c, other_chips


def _handshake(peers):
    barrier = pltpu.get_barrier_semaphore()
    for peer in peers:
        pl.semaphore_signal(barrier, inc=1, device_id=peer, device_id_type=MESH)
    pl.semaphore_wait(barrier, len(peers))


def _sequencer_call(body, operands, out_shapes, sems, *, name, collective_id):
    return pl.kernel(
        body, name=name, out_type=out_shapes, mesh=plsc.ScalarSubcoreMesh(axis_name="sequencer", num_cores=1),
        scratch_types=sems, compiler_params=pltpu.CompilerParams(collective_id=collective_id),
    )(*operands)


def _window(ref, kind, slot, shape):
    if kind == "row":
        rows = pl.ds(pl.multiple_of(slot * shape[-2], 8), shape[-2])
        return ref.at[(slice(None),) * (len(shape) - 2) + (rows,)]
    return ref.at[slot]


def _gathered(shape, kind):
    if kind == "row":
        return tuple(shape[:-2]) + (N_DEV * shape[-2], shape[-1])
    return (N_DEV,) + tuple(shape)


def _half(view, hf):
    if len(view.shape) == 4:
        return view.at[:, hf]
    n = view.shape[0] // 2
    return view.at[pl.ds(hf * n, n)]


def _all_gather(shards, kinds, *, name, collective_id):
    nt = len(shards)

    def body(*refs):
        x_refs, o_refs = refs[:nt], refs[nt:2 * nt]
        send_sems, recv_sems, local_sems = refs[2 * nt:]
        x, y, c, _ = _mesh_place()
        me, sibling, xn, yn, diag = (x, y, c), (x, y, 1 - c), (1 - x, y, c), (x, 1 - y, c), (1 - x, 1 - y, c)
        _handshake([sibling, xn, yn])

        def win(t, block):
            bx, by, bc = block
            return _window(o_refs[t], kinds[t], 4 * bx + 2 * by + bc, shards[t].shape)

        def copy(k, t, src, dst, to):
            return pltpu.make_async_remote_copy(src_ref=src, dst_ref=dst, send_sem=send_sems.at[k, t],
                                                recv_sem=recv_sems.at[k, t], device_id=to, device_id_type=MESH)

        def piece(k, t, block, hf, to, from_shard=False):
            dst = _half(win(t, block), hf)
            return copy(k, t, _half(x_refs[t], hf) if from_shard else dst, dst, to)

        mine = [pltpu.make_async_copy(x_refs[t], win(t, me), local_sems.at[t]) for t in range(nt)]
        sent = []

        def start(cp):
            cp.start()
            sent.append(cp)

        for cp in mine:
            cp.start()
        for t in range(nt):
            start(copy(0, t, x_refs[t], win(t, me), sibling))
        for hf_x, hf_y in ((0, 1), (1, 0)):
            for t in range(nt):
                start(piece(1 + hf_x, t, me, hf_x, xn, True))
                start(piece(3 + hf_y, t, me, hf_y, yn, True))
        for k, block, hf, onward, k_sib in ((1, xn, 0, (5, yn), 7), (4, yn, 1, (6, xn), 10), (2, xn, 1, None, 8),
                                           (3, yn, 0, None, 9), (5, diag, 0, None, 11), (6, diag, 1, None, 12)):
            for t in range(nt):
                piece(k, t, block, hf, me).wait_recv()
                if onward is not None:
                    start(piece(onward[0], t, block, hf, onward[1]))
                start(piece(k_sib, t, block, hf, sibling))
        for t in range(nt):
            copy(0, t, x_refs[t], win(t, sibling), me).wait_recv()
        for k_sib, block, hf in ((7, xn, 0), (10, yn, 1), (8, xn, 1), (9, yn, 0), (11, diag, 0), (12, diag, 1)):
            for t in range(nt):
                bx, by, _ = block
                piece(k_sib, t, (bx, by, 1 - c), hf, me).wait_recv()
        for cp in sent:
            cp.wait_send()
        for cp in mine:
            cp.wait()

    return _sequencer_call(
        body, shards, [SDS(_gathered(s.shape, k), s.dtype) for s, k in zip(shards, kinds)],
        [pltpu.SemaphoreType.DMA((13, nt)), pltpu.SemaphoreType.DMA((13, nt)), pltpu.SemaphoreType.DMA((nt,))],
        name=name, collective_id=collective_id)


def _part_shape(shape, kind):
    if kind == "row":
        return tuple(shape[:-2]) + (shape[-2] // N_DEV, shape[-1])
    return tuple(shape[2:])


def _pair_exchange(grads, kinds, *, name, collective_id):
    nt = len(grads)
    part = [_part_shape(g.shape, k) for g, k in zip(grads, kinds)]

    def body(*refs):
        g_refs, o_refs = refs[:nt], refs[nt:2 * nt]
        send_sems, recv_sems = refs[2 * nt:]
        x, y, c, _ = _mesh_place()
        _handshake([(x, y, 1 - c)])
        copies = []
        for t in range(nt):
            for xy in range(4):
                src = g_refs[t].at[1 - c, xy] if kinds[t] == "stack" else _window(g_refs[t], kinds[t], 2 * xy + 1 - c, part[t])
                copies.append(pltpu.make_async_remote_copy(
                    src_ref=src, dst_ref=o_refs[t].at[xy], send_sem=send_sems.at[xy, t], recv_sem=recv_sems.at[xy, t],
                    device_id=(x, y, 1 - c), device_id_type=MESH))
        for cp in copies:
            cp.start()
        for cp in copies:
            cp.wait()

    return _sequencer_call(
        body, grads, [SDS((4,) + p, g.dtype) for p, g in zip(part, grads)],
        [pltpu.SemaphoreType.DMA((4, nt)), pltpu.SemaphoreType.DMA((4, nt))], name=name, collective_id=collective_id)


def _chip_exchange(parts, small, *, name, collective_id):
    nt = len(parts)
    if small is None:
        def body_plain(*refs):
            s_refs, o_refs = refs[:nt], refs[nt:2 * nt]
            send_sems, recv_sems = refs[2 * nt:]
            x, y, c, chips = _mesh_place()
            _handshake([(*chip, c) for chip in chips])
            copies = [pltpu.make_async_remote_copy(
                src_ref=s_refs[t].at[2 * chip[0] + chip[1]], dst_ref=o_refs[t].at[j],
                send_sem=send_sems.at[j, t], recv_sem=recv_sems.at[j, t], device_id=(*chip, c), device_id_type=MESH)
                for j, chip in enumerate(chips) for t in range(nt)]
            for cp in copies:
                cp.start()
            for cp in copies:
                cp.wait()

        return _sequencer_call(
            body_plain, parts, [SDS((3,) + s.shape[1:], s.dtype) for s in parts],
            [pltpu.SemaphoreType.DMA((3, nt)), pltpu.SemaphoreType.DMA((3, nt))], name=name, collective_id=collective_id)

    def body(*refs):
        s_refs, small_ref = refs[:nt], refs[nt]
        o_refs, small_all = refs[nt + 1:2 * nt + 1], refs[2 * nt + 1]
        send_sems, recv_sems, small_send, small_recv, local_sem = refs[2 * nt + 2:]
        x, y, c, chips = _mesh_place()
        _handshake([(px, py, pc) for px in (x, 1 - x) for py in (y, 1 - y) for pc in (c, 1 - c)][1:])

        def copy(j, t, chip):
            return pltpu.make_async_remote_copy(
                src_ref=s_refs[t].at[2 * chip[0] + chip[1]], dst_ref=o_refs[t].at[j],
                send_sem=send_sems.at[j, t], recv_sem=recv_sems.at[j, t], device_id=(*chip, c), device_id_type=MESH)

        flips = [(fx, fy, fc) for fx in (0, 1) for fy in (0, 1) for fc in (0, 1)][1:]

        def small_copy(k):
            fx, fy, fc = flips[k]
            to = (x ^ fx if fx else x, y ^ fy if fy else y, c ^ fc if fc else c)
            rows = small_all.at[4 * x + 2 * y + c]
            return pltpu.make_async_remote_copy(
                src_ref=small_ref, dst_ref=rows, send_sem=small_send.at[k], recv_sem=small_recv.at[k],
                device_id=to, device_id_type=MESH)

        own = pltpu.make_async_copy(small_ref, small_all.at[4 * x + 2 * y + c], local_sem)
        own.start()
        copies = [copy(j, t, chip) for j, chip in enumerate(chips) for t in range(nt)]
        smalls = [small_copy(k) for k in range(7)]
        for cp in smalls + copies:
            cp.start()
        for cp in smalls + copies:
            cp.wait()
        own.wait()

    return _sequencer_call(
        body, list(parts) + [small],
        [SDS((3,) + s.shape[1:], s.dtype) for s in parts] + [SDS((N_DEV,) + small.shape, small.dtype)],
        [pltpu.SemaphoreType.DMA((3, nt)), pltpu.SemaphoreType.DMA((3, nt)),
         pltpu.SemaphoreType.DMA((7,)), pltpu.SemaphoreType.DMA((7,)), pltpu.SemaphoreType.DMA],
        name=name, collective_id=collective_id)


def _pair_sum(grad, theirs, kind, c, *, name):
    if kind == "row":
        r, l = theirs.shape[-2:]
        n = theirs.size // (4 * r * l)
        grad, theirs = grad.reshape(n, N_DEV * r, l), theirs.reshape(4, n, r, l)
        mine_spec = pl.BlockSpec((n, r, l), lambda xy, c_ref: (0, 2 * xy + c_ref[0], 0))
    else:
        r, l = theirs.shape[-2:]
        n = theirs.size // (4 * r * l)
        theirs = theirs.reshape(4, n, r, l)
        grad = grad.reshape(2, 4, n, r, l)
        mine_spec = pl.BlockSpec((None, None, n, r, l), lambda xy, c_ref: (c_ref[0], xy, 0, 0, 0))

    def body(c_ref, a_ref, b_ref, o_ref):
        o_ref[...] = (a_ref[...].astype(F32) + b_ref[...].astype(F32)).astype(BF16)

    part = pl.BlockSpec((None, n, r, l), lambda xy, c_ref: (xy, 0, 0, 0))
    return pl.pallas_call(
        body, name=name,
        grid_spec=pltpu.PrefetchScalarGridSpec(num_scalar_prefetch=1, grid=(4,), in_specs=[mine_spec, part], out_specs=part),
        out_shape=SDS((4, n, r, l), BF16), compiler_params=_params("parallel"),
    )(c, grad, theirs)


def _adamw(w, g, m, v):
    m = ADAM_B1 * m + (1.0 - ADAM_B1) * g
    v = ADAM_B2 * v + (1.0 - ADAM_B2) * jnp.square(g)
    m_hat = m / (1.0 - ADAM_B1 ** ADAM_STEP)
    v_hat = v / (1.0 - ADAM_B2 ** ADAM_STEP)
    delta = -ADAM_LR * (m_hat / (jnp.sqrt(v_hat) + ADAM_EPS) + ADAM_WD * w)
    return delta, m, v


def _adam_big(owns, others, mat, xy, w, m, v, *, tr, name):
    nw, r, l = w.shape
    lp = owns[0].shape[-1]
    nq = len(owns)
    assert nq in (1, nw)

    def body(xy_ref, *refs):
        own_refs, oth_refs = refs[:nq], refs[nq:2 * nq]
        w_ref, m_ref, v_ref, g_out, d_out, m_out, v_out = refs[2 * nq:]
        g = None
        for q in range(nq):
            gq = own_refs[q][0, 0].astype(F32)
            for j in range(3):
                gq = gq + oth_refs[q][j, 0].astype(F32)
            g = gq if g is None else jnp.where(pl.program_id(0) == q, gq, g)
        g = g[:, :l]
        delta, m_new, v_new = _adamw(w_ref[0], g, m_ref[0], v_ref[0])
        g_out[0] = g
        d_out[0] = delta
        m_out[0] = m_new
        v_out[0] = v_new

    def at(p):
        return mat * nw + p if nq == 1 else mat

    blk = pl.BlockSpec((1, tr, l), lambda p, i, xy_ref: (p, i, 0))
    return pl.pallas_call(
        body, name=name,
        grid_spec=pltpu.PrefetchScalarGridSpec(
            num_scalar_prefetch=1, grid=(nw, r // tr),
            in_specs=[pl.BlockSpec((1, 1, tr, lp), lambda p, i, xy_ref: (xy_ref[0], at(p), i, 0))] * nq
            + [pl.BlockSpec((3, 1, tr, lp), lambda p, i, xy_ref: (0, at(p), i, 0))] * nq + [blk, blk, blk],
            out_specs=[blk, blk, blk, blk]),
        out_shape=[SDS(w.shape, F32)] * 4, compiler_params=_params("parallel", "parallel"),
    )(xy, *owns, *others, w, m, v)


def _small_layout(shapes):
    out, at = [], 0
    for r, c in shapes:
        rows = c // 128 if (r == 1 and c > 128) else r
        out.append((at, rows))
        at += -(-rows // 8) * 8
    return out, at


def _pack_small(parts, *, name):
    shapes = [a.shape for a in parts]
    layout, total = _small_layout(shapes)

    def body(*refs):
        o_ref = refs[-1]
        o_ref[...] = jnp.zeros_like(o_ref)
        for x_ref, (r, c), (at, rows) in zip(refs, shapes, layout):
            if r == 1 and c > 128:
                for k in range(rows):
                    o_ref[at + k:at + k + 1, :] = x_ref[:, k * 128:(k + 1) * 128]
            else:
                o_ref[at:at + r, 0:c] = x_ref[...]

    return pl.pallas_call(body, name=name, out_shape=SDS((total, 128), F32))(*parts)


def _adam_small(g_all, ws, ms, vs, *, name):
    n = len(ws)
    shapes = [w.shape for w in ws]
    layout, _ = _small_layout(shapes)

    def body(g_ref, *refs):
        w_refs, m_refs, v_refs, outs = refs[:n], refs[n:2 * n], refs[2 * n:3 * n], refs[3 * n:]
        g_sum = g_ref[0]
        for k in range(1, N_DEV):
            g_sum = g_sum + g_ref[k]
        for i, ((r, c), (at, rows)) in enumerate(zip(shapes, layout)):
            if r == 1 and c > 128:
                g = jnp.concatenate([g_sum[at + k:at + k + 1, :] for k in range(rows)], axis=1)
            else:
                g = g_sum[at:at + r, 0:c]
            delta, m_new, v_new = _adamw(w_refs[i][...], g, m_refs[i][...], v_refs[i][...])
            for q, val in enumerate((g, delta, m_new, v_new)):
                outs[4 * i + q][...] = val

    flat = pl.pallas_call(body, name=name, out_shape=[SDS(s, F32) for s in shapes for _ in range(4)])(g_all, *ws, *ms, *vs)
    return [flat[4 * i:4 * i + 4] for i in range(n)]


def kernel(x, mem, ffn1_norm, ffn1_w_gate, ffn1_w_up, ffn1_w_down, mix_norm, mem_norm, w_in, w_mem_kv, swa_q_norm, swa_k_norm, swa_sinks, rel_bias, gla_w_gate_up, gla_b_gate, gla_out_norm, mem_q_norm, mem_k_norm, w_out, ffn2_norm, ffn2_w_gate, ffn2_w_up, ffn2_w_down, loss_target, m_ffn1_norm, m_ffn1_w_gate, m_ffn1_w_up, m_ffn1_w_down, m_mix_norm, m_mem_norm, m_w_in, m_w_mem_kv, m_swa_q_norm, m_swa_k_norm, m_swa_sinks, m_rel_bias, m_gla_w_gate_up, m_gla_b_gate, m_gla_out_norm, m_mem_q_norm, m_mem_k_norm, m_w_out, m_ffn2_norm, m_ffn2_w_gate, m_ffn2_w_up, m_ffn2_w_down, v_ffn1_norm, v_ffn1_w_gate, v_ffn1_w_up, v_ffn1_w_down, v_mix_norm, v_mem_norm, v_w_in, v_w_mem_kv, v_swa_q_norm, v_swa_k_norm, v_swa_sinks, v_rel_bias, v_gla_w_gate_up, v_gla_b_gate, v_gla_out_norm, v_mem_q_norm, v_mem_k_norm, v_w_out, v_ffn2_norm, v_ffn2_w_gate, v_ffn2_w_up, v_ffn2_w_down):
    xi, yi, ci = lax.axis_index("x"), lax.axis_index("y"), lax.axis_index("c")
    c_arr = jnp.reshape(ci, (1,)).astype(jnp.int32)
    xy_arr = jnp.reshape(2 * xi + yi, (1,)).astype(jnp.int32)
    d = x.shape[-1]

    half_h = ffn1_w_gate.shape[-1] // 2

    def gather_ffn(wg_s, wu_s, wd_s, name, collective_id, after):
        w3_s = jnp.concatenate([wg_s.transpose(0, 2, 1), wu_s.transpose(0, 2, 1), wd_s], axis=0)
        w3_s = jnp.pad(w3_s.reshape(3, 2, half_h, d), ((0, 0), (0, 0), (0, FFN_HALF_PAD - half_h), (0, 0))).astype(BF16)
        if after is not None:
            w3_s, _ = lax.optimization_barrier((w3_s, after))
        return _all_gather([w3_s], ["row"], name=name, collective_id=collective_id)[0].reshape(3, -1, d)

    w3_1 = gather_ffn(ffn1_w_gate, ffn1_w_up, ffn1_w_down, "gather_ffn1", 0, None)
    mix_s = lax.optimization_barrier((w_in[0].astype(BF16), w_mem_kv[0].astype(BF16), w_out[0].astype(BF16), w3_1))[:3]
    win_all, wkv, wout = _all_gather(list(mix_s), ["stack", "row", "row"], name="gather_mix", collective_id=1)

    def gather_ffn2(x1):
        return gather_ffn(ffn2_w_gate, ffn2_w_up, ffn2_w_down, "gather_ffn2", 2, (wout, x1))

    win_p = _pack_win(win_all, tr=256, name="pack_w_in")

    small_w = [ffn1_norm, mix_norm, mem_norm, ffn2_norm, swa_q_norm, swa_k_norm, swa_sinks[0], rel_bias,
               gla_w_gate_up[0], gla_b_gate, gla_out_norm, mem_q_norm, mem_k_norm]
    collective_ids = {"ffn2": (3, 4), "mix": (5, 6), "ffn1a": (7, 8), "ffn1b": (9, 10)}
    reduced, small_box = {}, {}

    def on_grads(group, grads, carry, small=None):
        if group == "mix":
            dwin_p, dwkv, dwout = grads
            grads = [_unpack_win(dwin_p, tr=256, name="unpack_dw_in"), dwkv, dwout]
            kinds = ["stack", "row", "row"]
        else:
            kinds = ["row"]
        if reduced:
            earlier = list(reduced.values())[-1][1]
            *grads, _ = lax.optimization_barrier((*grads, earlier[0]))
        id_pair, id_chip = collective_ids[group]
        from_sibling = _pair_exchange(grads, kinds, name=f"pair_exchange_{group}", collective_id=id_pair)
        chip_sums = [_pair_sum(g, theirs, k, c_arr, name=f"pair_sum_{group}_{t}")
                     for t, (g, theirs, k) in enumerate(zip(grads, from_sibling, kinds))]
        if carry is not None:
            *chip_sums, carry = lax.optimization_barrier((*chip_sums, carry))
        if small is None:
            from_chips = _chip_exchange(chip_sums, None, name=f"chip_exchange_{group}", collective_id=id_chip)
        else:
            packed = _pack_small(small, name=f"pack_small_{group}")
            *from_chips, small_all = _chip_exchange(chip_sums, packed, name=f"chip_exchange_{group}",
                                                    collective_id=id_chip)
            small_box[group] = small_all
        reduced[group] = (chip_sums, from_chips)
        return carry

    grad_x = _local_step(x[0], mem[0], loss_target[0], small_w, (w3_1, win_p, wkv, wout, gather_ffn2), on_grads)

    big_w = {"ffn1_w_gate": ("ffn1", 0, 0, True, ffn1_w_gate, m_ffn1_w_gate, v_ffn1_w_gate),
             "ffn1_w_up": ("ffn1", 0, 1, True, ffn1_w_up, m_ffn1_w_up, v_ffn1_w_up),
             "ffn1_w_down": ("ffn1", 0, 2, False, ffn1_w_down, m_ffn1_w_down, v_ffn1_w_down),
             "w_in": ("mix", 0, 0, False, w_in, m_w_in, v_w_in),
             "w_mem_kv": ("mix", 1, 0, False, w_mem_kv, m_w_mem_kv, v_w_mem_kv),
             "w_out": ("mix", 2, 0, False, w_out, m_w_out, v_w_out),
             "ffn2_w_gate": ("ffn2", 0, 0, True, ffn2_w_gate, m_ffn2_w_gate, v_ffn2_w_gate),
             "ffn2_w_up": ("ffn2", 0, 1, True, ffn2_w_up, m_ffn2_w_up, v_ffn2_w_up),
             "ffn2_w_down": ("ffn2", 0, 2, False, ffn2_w_down, m_ffn2_w_down, v_ffn2_w_down)}
    res = {}
    for nm, (group, t, mat, transposed, w, m, v) in big_w.items():
        shape = w.shape
        if transposed:
            w, m, v = (a.transpose(0, 2, 1) for a in (w, m, v))
        if group != "mix":
            w, m, v = (a.reshape(2, half_h, d) for a in (w, m, v))
        r = w.shape[1]
        tr = 256 if r % 256 == 0 else r
        halves = ["ffn1a", "ffn1b"] if group == "ffn1" else [group]
        out = _adam_big([reduced[k][0][t] for k in halves], [reduced[k][1][t] for k in halves], mat, xy_arr, w, m, v,
                        tr=tr, name=f"adam_{nm}")
        if transposed:
            out = [a.reshape(1, -1, d).transpose(0, 2, 1) for a in out]
        res[nm] = [a.reshape(shape) for a in out]
    small_names = ["ffn1_norm", "mix_norm", "mem_norm", "ffn2_norm", "swa_q_norm", "swa_k_norm", "swa_sinks", "rel_bias",
                   "gla_w_gate_up", "gla_b_gate", "gla_out_norm", "mem_q_norm", "mem_k_norm"]
    small_m = [m_ffn1_norm, m_mix_norm, m_mem_norm, m_ffn2_norm, m_swa_q_norm, m_swa_k_norm, m_swa_sinks, m_rel_bias,
               m_gla_w_gate_up, m_gla_b_gate, m_gla_out_norm, m_mem_q_norm, m_mem_k_norm]
    small_v = [v_ffn1_norm, v_mix_norm, v_mem_norm, v_ffn2_norm, v_swa_q_norm, v_swa_k_norm, v_swa_sinks, v_rel_bias,
               v_gla_w_gate_up, v_gla_b_gate, v_gla_out_norm, v_mem_q_norm, v_mem_k_norm]
    small_full = [ffn1_norm, mix_norm, mem_norm, ffn2_norm, swa_q_norm, swa_k_norm, swa_sinks, rel_bias,
                  gla_w_gate_up, gla_b_gate, gla_out_norm, mem_q_norm, mem_k_norm]
    zero = jnp.zeros((1, 1), F32)
    two_d = lambda a: a.reshape(a.shape[-2:])
    for group, sel in (("ffn1a", slice(1, None)), ("ffn1b", slice(0, 1))):
        extra = [zero] if group == "ffn1a" else []
        ws, ms, vs = ([two_d(a) for a in arrs[sel]] + extra for arrs in (small_full, small_m, small_v))
        updated = _adam_small(small_box[group], ws, ms, vs, name=f"adam_small_{group}")
        for nm, full, out in zip(small_names[sel], small_full[sel], updated):
            res[nm] = [a.reshape(full.shape) for a in out]
        if extra:
            loss = updated[-1][0].reshape(())

    order = ["ffn1_norm", "ffn1_w_gate", "ffn1_w_up", "ffn1_w_down", "mix_norm", "mem_norm", "w_in", "w_mem_kv",
             "swa_q_norm", "swa_k_norm", "swa_sinks", "rel_bias", "gla_w_gate_up", "gla_b_gate", "gla_out_norm",
             "mem_q_norm", "mem_k_norm", "w_out", "ffn2_norm", "ffn2_w_gate", "ffn2_w_up", "ffn2_w_down"]
    outs = [loss, grad_x[None]]
    for q in range(4):
        outs += [res[nm][q] for nm in order]
    return tuple(outs)
```

```python
import functools
import math

import numpy as np
import jax
import jax.numpy as jnp
from jax import lax
from jax.experimental import pallas as pl
from jax.experimental.pallas import tpu as pltpu
from jax.experimental.pallas import tpu_sc as plsc

F32 = jnp.float32
BF16 = jnp.bfloat16
SDS = jax.ShapeDtypeStruct

EPS = 1e-6
HEAD_DIM = 64
SWA_HEADS = 8
SWA_KV_HEADS = 2
SWA_GROUP = SWA_HEADS // SWA_KV_HEADS
BLOCK = 128
N_BUCKETS = 32
MAX_DISTANCE = 128
GLA_HEADS = 4
GLA_DK = 32
GLA_DV = 64
GLA_RANK = 16
GLA_TAU = 16.0
GLA_CHUNK = 32
MEM_HEADS = 4
SWA_Q_W = SWA_HEADS * HEAD_DIM
SWA_KV_W = SWA_KV_HEADS * HEAD_DIM
GLA_QK_W = GLA_HEADS * GLA_DK
GLA_V_W = GLA_HEADS * GLA_DV
MEM_Q_W = MEM_HEADS * HEAD_DIM
IN_W = 1808
IN_W_PAD = 1920
COL_SQ, COL_SKV, COL_GQ, COL_GK, COL_GV, COL_GG, COL_MQ, COL_GLR = 0, 512, 768, 896, 1024, 1280, 1536, 1792

ADAM_LR = 0.001
ADAM_B1 = 0.9
ADAM_B2 = 0.999
ADAM_EPS = 1e-08
ADAM_WD = 0.01
ADAM_STEP = 10

N_DEV = 8
VMEM_LIMIT_BYTES = 56 * 1024 * 1024
MESH = pl.DeviceIdType.MESH


def _params(*sem):
    return pltpu.CompilerParams(dimension_semantics=sem or None, vmem_limit_bytes=VMEM_LIMIT_BYTES)


def _dot(a, b, ta, tb, precision=None):
    dims = (((0 if ta else 1,), (1 if tb else 0,)), ((), ()))
    return lax.dot_general(a, b, dims, preferred_element_type=F32, precision=precision)


def _mm_raw(a, b, ta=False, tb=False):
    return _dot(a.astype(BF16), b.astype(BF16), ta, tb)


def _mmf_raw(a, b, ta=False, tb=False):
    return _dot(a, b, ta, tb, lax.Precision.HIGHEST)


def _make_mm(raw):
    @functools.partial(jax.custom_vjp, nondiff_argnums=(2, 3))
    def mm(a, b, ta=False, tb=False):
        return raw(a, b, ta, tb)

    def fwd(a, b, ta, tb):
        return raw(a, b, ta, tb), (a, b)

    def bwd(ta, tb, res, g):
        a, b = res
        da = raw(b, g, tb, True) if ta else raw(g, b, False, not tb)
        db = raw(g, a, True, ta) if tb else raw(a, g, not ta, False)
        return da, db

    mm.defvjp(fwd, bwd)
    return mm


_mm = _make_mm(_mm_raw)
_mmf = _make_mm(_mmf_raw)


def _mm3(a, b, ta=False, tb=False):
    a_hi, b_hi = a.astype(BF16).astype(F32), b.astype(BF16).astype(F32)
    return _mm(a_hi, b_hi, ta, tb) + _mm(a_hi, b - b_hi, ta, tb) + _mm(a - a_hi, b_hi, ta, tb)


def _rms(x, g):
    return x * lax.rsqrt(jnp.mean(x * x, axis=-1, keepdims=True) + EPS) * g


def _silu_mul(g, u):
    return jax.nn.silu(g) * u


def _log_sigmoid(z):
    return jnp.minimum(z, 0.0) - jnp.log(1.0 + jnp.exp(-jnp.abs(z)))


def _matmul(a_list, b, *, ta=False, tb=False, tm, tn, b_blocks=None, res=None, scale=1.0, out_dtype=F32, name):
    if not isinstance(a_list, (list, tuple)):
        a_list = [a_list]
    n_a = len(a_list)
    m = a_list[0].shape[1] if ta else a_list[0].shape[0]
    ks = [a.shape[0] if ta else a.shape[1] for a in a_list]
    n = b.shape[0] if tb else b.shape[1]
    if b_blocks is None:
        assert n_a == 1
        b_blocks = [0]
    tm, tn = min(tm, m), min(tn, n)
    assert m % tm == 0 and n % tn == 0, (m, n, tm, tn)

    def body(*refs):
        a_refs, b_refs = refs[:n_a], refs[n_a:2 * n_a]
        r_ref = refs[2 * n_a] if res is not None else None
        o_ref = refs[-1]
        acc = _mm_raw(a_refs[0][...], b_refs[0][...], ta, tb)
        for k in range(1, n_a):
            acc = acc + _mm_raw(a_refs[k][...], b_refs[k][...], ta, tb)
        if scale != 1.0:
            acc = acc * scale
        if r_ref is not None:
            acc = r_ref[...] + acc
        o_ref[...] = acc.astype(out_dtype)

    in_specs = []
    for k in ks:
        in_specs.append(pl.BlockSpec((k, tm), lambda i, j: (0, i)) if ta else pl.BlockSpec((tm, k), lambda i, j: (i, 0)))
    for k, blk in zip(ks, b_blocks):
        if tb:
            in_specs.append(pl.BlockSpec((tn, k), functools.partial(lambda i, j, blk: (j, blk), blk=blk)))
        else:
            in_specs.append(pl.BlockSpec((k, tn), functools.partial(lambda i, j, blk: (blk, j), blk=blk)))
    args = list(a_list) + [b] * n_a
    if res is not None:
        in_specs.append(pl.BlockSpec((tm, tn), lambda i, j: (i, j)))
        args.append(res)
    return pl.pallas_call(
        body, name=name, grid=(m // tm, n // tn), in_specs=in_specs,
        out_specs=pl.BlockSpec((tm, tn), lambda i, j: (i, j)), out_shape=SDS((m, n), out_dtype),
        compiler_params=_params("parallel", "parallel"),
    )(*args)


def _win_pieces(w):
    glr_lo, glr_hi = COL_MQ, COL_MQ + GLA_RANK
    out = []
    for j in range(N_DEV):
        for lo, hi, shift in ((0, glr_lo, 0), (glr_lo, glr_hi, COL_GLR - glr_lo), (glr_hi, IN_W, COL_MQ - glr_hi)):
            s, e = max(j * w, lo), min((j + 1) * w, hi)
            if s < e:
                out.append((j, s - j * w, e - j * w, s + shift))
    return out


def _pack_win(win_all, *, tr, name):
    _, d, w = win_all.shape

    def body(i_ref, o_ref):
        for j, a, b, dst in _win_pieces(w):
            o_ref[:, dst:dst + b - a] = i_ref[j][:, a:b]
        o_ref[:, IN_W:] = jnp.zeros((tr, IN_W_PAD - IN_W), o_ref.dtype)

    return pl.pallas_call(
        body, name=name, grid=(d // tr,), in_specs=[pl.BlockSpec((N_DEV, tr, w), lambda i: (0, i, 0))],
        out_specs=pl.BlockSpec((tr, IN_W_PAD), lambda i: (i, 0)), out_shape=SDS((d, IN_W_PAD), win_all.dtype),
        compiler_params=_params("parallel"),
    )(win_all)


def _unpack_win(dwin_p, *, tr, name):
    d = dwin_p.shape[0]
    w = IN_W // N_DEV

    def body(i_ref, o_ref):
        for j, a, b, src in _win_pieces(w):
            o_ref[j % 2, j // 2, :, a:b] = i_ref[:, src:src + b - a]

    return pl.pallas_call(
        body, name=name, grid=(d // tr,), in_specs=[pl.BlockSpec((tr, IN_W_PAD), lambda i: (i, 0))],
        out_specs=pl.BlockSpec((2, 4, tr, w), lambda i: (0, 0, i, 0)), out_shape=SDS((2, 4, d, w), dwin_p.dtype),
        compiler_params=_params("parallel"),
    )(dwin_p)


def _rms_fwd(x, g, *, tm, name):
    s, d = x.shape

    def body(x_ref, g_ref, h_ref):
        h_ref[...] = _rms(x_ref[...], g_ref[...]).astype(BF16)

    return pl.pallas_call(
        body, name=name, grid=(s // tm,),
        in_specs=[pl.BlockSpec((tm, d), lambda i: (i, 0)), pl.BlockSpec((1, d), lambda i: (0, 0))],
        out_specs=pl.BlockSpec((tm, d), lambda i: (i, 0)), out_shape=SDS((s, d), BF16),
        compiler_params=_params("parallel"),
    )(x, g)


def _rms_bwd(x, g, dh, dres, *, tm, name):
    s, d = x.shape
    want_dx = dres is not None

    def body(*refs):
        if want_dx:
            x_ref, g_ref, dh_ref, dres_ref, dx_ref, dxb_ref, dg_ref = refs
        else:
            x_ref, g_ref, dh_ref, dg_ref = refs
        _, vjp = jax.vjp(_rms, x_ref[...], g_ref[...])
        dx, dg = vjp(dh_ref[...])
        if want_dx:
            dx = dres_ref[...] + dx
            dx_ref[...] = dx
            dxb_ref[...] = dx.astype(BF16)

        @pl.when(pl.program_id(0) == 0)
        def _():
            dg_ref[...] = jnp.zeros_like(dg_ref)

        dg_ref[...] += dg

    row = pl.BlockSpec((tm, d), lambda i: (i, 0))
    vec = pl.BlockSpec((1, d), lambda i: (0, 0))
    if want_dx:
        return pl.pallas_call(
            body, name=name, grid=(s // tm,), in_specs=[row, vec, row, row], out_specs=[row, row, vec],
            out_shape=[SDS((s, d), F32), SDS((s, d), BF16), SDS((1, d), F32)], compiler_params=_params("arbitrary"),
        )(x, g, dh, dres)
    return None, None, pl.pallas_call(
        body, name=name, grid=(s // tm,), in_specs=[row, vec, row], out_specs=vec,
        out_shape=SDS((1, d), F32), compiler_params=_params("arbitrary"),
    )(x, g, dh)


FFN_TN = 256
FFN_TN_FWD = 512
FFN_HALF_PAD = 192


def _ffn_fwd(x, gain, w3, tag, *, tm=1024, next_gain=None, target=None):
    s, d = x.shape
    f = w3.shape[1]
    tn = FFN_TN_FWD
    nj = f // tn
    tm = min(tm, s)
    n_extra = (next_gain is not None) + (target is not None)

    def body(*refs):
        x_ref, gain_ref, wg_ref, wu_ref, wd_ref = refs[:5]
        extra, outs = refs[5:5 + n_extra], refs[5 + n_extra:-1]
        acc_s = refs[-1]
        h_ref, g_ref, u_ref = outs[-3:]
        i, j = pl.program_id(0), pl.program_id(1)

        @pl.when(j == 0)
        def _():
            h_ref[...] = _rms(x_ref[...], gain_ref[...]).astype(BF16)
            acc_s[...] = jnp.zeros_like(acc_s)

        hv = h_ref[...]
        g = _mm_raw(hv, wg_ref[...], False, True)
        u = _mm_raw(hv, wu_ref[...], False, True)
        g_ref[...] = g.astype(BF16)
        u_ref[...] = u.astype(BF16)
        acc_s[...] += _mm_raw(_silu_mul(g, u), wd_ref[...])

        @pl.when(j == nj - 1)
        def _():
            y = x_ref[...] + 0.5 * acc_s[...]
            if target is None:
                outs[0][...] = y
                if next_gain is not None:
                    outs[1][...] = _rms(y, extra[0][...]).astype(BF16)
            else:
                dy_ref, dyb_ref, loss_ref = outs[:3]
                diff = y - extra[0][...]
                dy_ref[...] = diff * (1.0 / d)
                dyb_ref[...] = (diff * (1.0 / d)).astype(BF16)
                part = 0.5 * jnp.sum(jnp.mean(diff * diff, axis=-1, keepdims=True), axis=0, keepdims=True)

                @pl.when(i == 0)
                def _():
                    loss_ref[...] = part

                @pl.when(i > 0)
                def _():
                    loss_ref[...] += part

    row = pl.BlockSpec((tm, d), lambda i, j: (i, 0))
    vec = pl.BlockSpec((1, d), lambda i, j: (0, 0))
    tile = pl.BlockSpec((tm, tn), lambda i, j: (i, j))
    in_specs = [row, vec] + [pl.BlockSpec((None, tn, d), functools.partial(lambda i, j, k: (k, j, 0), k=k)) for k in range(3)]
    args = [x, gain, w3, w3, w3]
    if target is None:
        out_specs, out_shape = [row], [SDS((s, d), F32)]
        if next_gain is not None:
            in_specs.append(vec)
            args.append(next_gain)
            out_specs.append(row)
            out_shape.append(SDS((s, d), BF16))
    else:
        in_specs.append(row)
        args.append(target)
        out_specs = [row, row, pl.BlockSpec((1, 1), lambda i, j: (0, 0))]
        out_shape = [SDS((s, d), F32), SDS((s, d), BF16), SDS((1, 1), F32)]
    *head, h, g, u = pl.pallas_call(
        body, name=f"{tag}_fwd", grid=(s // tm, nj), in_specs=in_specs,
        out_specs=out_specs + [row, tile, tile],
        out_shape=out_shape + [SDS((s, d), BF16), SDS((s, f), BF16), SDS((s, f), BF16)],
        scratch_shapes=[pltpu.VMEM((tm, d), F32)],
        compiler_params=_params("arbitrary", "arbitrary"),
    )(*args)
    return head, (h, g, u)


def _ffn_bwd_part(dyb, w3, saved, first, count, dh_init, *, name):
    h, g, u = saved
    s, d = h.shape
    tn = FFN_TN

    def body(*refs):
        if dh_init is None:
            dy_ref, h_ref, wd_ref, wg_ref, wu_ref, g_ref, u_ref, dh_ref, dw3_ref, dg_s, du_s, a_s = refs
        else:
            dy_ref, h_ref, wd_ref, wg_ref, wu_ref, g_ref, u_ref, dh0_ref, dh_ref, dw3_ref, dg_s, du_s, a_s = refs
        j = pl.program_id(0)

        @pl.when(j == 0)
        def _():
            dh_ref[...] = jnp.zeros_like(dh_ref) if dh_init is None else dh0_ref[...]
            for ref in (dg_s, du_s, a_s):
                ref[...] = jnp.zeros_like(ref)

        now, before = j % 2, 1 - j % 2
        dyv = dy_ref[...]
        hv = h_ref[...]
        dg, du, a = dg_s[before], du_s[before], a_s[before]
        dh_ref[...] += _mm_raw(dg, wg_ref[...]) + _mm_raw(du, wu_ref[...])
        dw3_ref[0] = _mm_raw(dg, hv, True, False).astype(BF16)
        dw3_ref[1] = _mm_raw(du, hv, True, False).astype(BF16)
        dw3_ref[2] = (_mm_raw(a, dyv, True, False) * 0.5).astype(BF16)

        da = _mm_raw(dyv, wd_ref[...], False, True) * 0.5
        a, vjp = jax.vjp(_silu_mul, g_ref[...].astype(F32), u_ref[...].astype(F32))
        dg, du = vjp(da)
        dg_s[now] = dg.astype(BF16)
        du_s[now] = du.astype(BF16)
        a_s[now] = a.astype(BF16)

    this = lambda j: first + jnp.minimum(j, count - 1)
    last = lambda j: first + jnp.maximum(j - 1, 0)
    full = pl.BlockSpec((s, d), lambda j: (0, 0))
    once = pl.BlockSpec((s, d), lambda j: (0, 0), pipeline_mode=pl.Buffered(1))
    tile = pl.BlockSpec((s, tn), lambda j: (0, this(j)))
    in_specs = [once, once, pl.BlockSpec((None, tn, d), lambda j: (2, this(j), 0)),
                pl.BlockSpec((None, tn, d), lambda j: (0, last(j), 0)), pl.BlockSpec((None, tn, d), lambda j: (1, last(j), 0)),
                tile, tile]
    args = [dyb, h, w3, w3, w3, g, u]
    if dh_init is not None:
        in_specs.append(once)
        args.append(dh_init)
    return pl.pallas_call(
        body, name=name, grid=(count + 1,), in_specs=in_specs,
        out_specs=[full, pl.BlockSpec((3, tn, d), lambda j: (0, jnp.maximum(j - 1, 0), 0))],
        out_shape=[SDS((s, d), F32), SDS((3, count * tn, d), BF16)],
        scratch_shapes=[pltpu.VMEM((2, s, tn), BF16)] * 3,
        compiler_params=_params("arbitrary"),
    )(*args)


def _bucket_table():
    qi = np.arange(BLOCK)[:, None]
    kj = np.arange(2 * BLOCK)[None, :]
    dist = np.maximum(qi + BLOCK - kj, 0)
    max_exact = N_BUCKETS // 2
    d = np.maximum(dist, 1).astype(np.float32)
    large = max_exact + (np.log(d / np.float32(max_exact)) / np.float32(math.log(MAX_DISTANCE / max_exact))
                         * np.float32(N_BUCKETS - max_exact)).astype(np.int32)
    large = np.minimum(large, N_BUCKETS - 1)
    band = np.where(dist < max_exact, dist, large).astype(np.int32)
    return np.where(np.tril(np.ones((BLOCK, BLOCK), bool)), band[:, BLOCK:], band[:, :BLOCK])


SWA_STACK = SWA_GROUP * BLOCK


def _swa_masks(n):
    qi = lax.broadcasted_iota(jnp.int32, (SWA_STACK, BLOCK), 0) % BLOCK
    kj = lax.broadcasted_iota(jnp.int32, (SWA_STACK, BLOCK), 1)
    own = kj <= qi
    return own, own | (n > 0)


def _swa_group(q, kp, kc, vp, vc, qg, kg, sink, bias, own, valid):
    qn = _rms(q, qg)
    s = jnp.where(own, _mm(qn, _rms(kc, kg), False, True), _mm(qn, _rms(kp, kg), False, True))
    s = s * (HEAD_DIM ** -0.5) + bias
    s = jnp.where(valid, s, -jnp.inf)
    m = lax.stop_gradient(jnp.maximum(jnp.max(s, axis=-1, keepdims=True), sink))
    p = jnp.exp(s - m)
    p = p / (jnp.sum(p, axis=-1, keepdims=True) + jnp.exp(sink - m))
    return _mm(jnp.where(own, p, 0.0), vc) + _mm(jnp.where(own, 0.0, p), vp)


def _swa_bias_table(rb_ref, bucket, bias_s):
    for h in range(SWA_HEADS):
        acc = jnp.zeros((BLOCK, BLOCK), F32)
        for b in range(N_BUCKETS):
            acc = jnp.where(bucket == b, rb_ref[b, h], acc)
        bias_s[h // SWA_GROUP, (h % SWA_GROUP) * BLOCK:(h % SWA_GROUP + 1) * BLOCK, :] = acc


def _swa_stack(ref, g):
    return jnp.concatenate([ref[:, (g * SWA_GROUP + hh) * HEAD_DIM:(g * SWA_GROUP + hh + 1) * HEAD_DIM]
                            for hh in range(SWA_GROUP)], axis=0)


def _swa_unstack(ref, g, stacked):
    for hh in range(SWA_GROUP):
        h = g * SWA_GROUP + hh
        ref[:, h * HEAD_DIM:(h + 1) * HEAD_DIM] = stacked[hh * BLOCK:(hh + 1) * BLOCK]


def _swa_sink_column(sink_ref, g):
    head = lax.broadcasted_iota(jnp.int32, (SWA_STACK, 1), 0) // BLOCK
    col = jnp.zeros((SWA_STACK, 1), F32)
    for hh in range(SWA_GROUP):
        col = jnp.where(head == hh, sink_ref[g * SWA_GROUP + hh], col)
    return col


def _swa_band(kvp_ref, kvc_ref, g):
    k = slice(g * HEAD_DIM, (g + 1) * HEAD_DIM)
    v = slice(SWA_KV_W + g * HEAD_DIM, SWA_KV_W + (g + 1) * HEAD_DIM)
    return kvp_ref[:, k], kvc_ref[:, k], kvp_ref[:, v], kvc_ref[:, v]


def _swa_specs(order):
    kvc = COL_SKV // (2 * SWA_KV_W)
    return [
        pl.BlockSpec((BLOCK, SWA_Q_W), lambda t: (order(t), 0)),
        pl.BlockSpec((BLOCK, 2 * SWA_KV_W), lambda t: (jnp.maximum(order(t) - 1, 0), kvc)),
        pl.BlockSpec((BLOCK, 2 * SWA_KV_W), lambda t: (order(t), kvc)),
        pl.BlockSpec((1, HEAD_DIM), lambda t: (0, 0)),
        pl.BlockSpec((1, HEAD_DIM), lambda t: (0, 0)),
        pl.BlockSpec(memory_space=pltpu.SMEM),
        pl.BlockSpec(memory_space=pltpu.SMEM),
        pl.BlockSpec((BLOCK, BLOCK), lambda t: (0, 0)),
    ]


def _swa_fwd(p, qg, kg, sinks, rel_bias, *, name):
    s = p.shape[0]
    nb = s // BLOCK

    def body(q_ref, kvp_ref, kvc_ref, qg_ref, kg_ref, sink_ref, rb_ref, bucket_ref, y_ref, bias_s):
        n = pl.program_id(0)

        @pl.when(n == 0)
        def _():
            _swa_bias_table(rb_ref, bucket_ref[...], bias_s)

        own, valid = _swa_masks(n)
        for g in range(SWA_KV_HEADS):
            out = _swa_group(_swa_stack(q_ref, g), *_swa_band(kvp_ref, kvc_ref, g), qg_ref[...], kg_ref[...],
                             _swa_sink_column(sink_ref, g), bias_s[g], own, valid)
            _swa_unstack(y_ref, g, out)

    return pl.pallas_call(
        body, name=name, grid=(nb,), in_specs=_swa_specs(lambda t: t),
        out_specs=pl.BlockSpec((BLOCK, SWA_Q_W), lambda t: (t, 0)), out_shape=SDS((s, SWA_Q_W), F32),
        scratch_shapes=[pltpu.VMEM((SWA_KV_HEADS, SWA_STACK, BLOCK), F32)],
        compiler_params=_params("arbitrary"),
    )(p, p, p, qg, kg, sinks, rel_bias, jnp.asarray(_bucket_table()))


def _swa_bwd(p, qg, kg, sinks, rel_bias, dy_all, *, name):
    s = p.shape[0]
    nb = s // BLOCK

    def body(q_ref, kvp_ref, kvc_ref, qg_ref, kg_ref, sink_ref, rb_ref, bucket_ref, dy_ref,
             dq_ref, dkv_ref, dqg_ref, dkg_ref, dsink_ref, drb_ref, bias_s, dbias_s, carry_s):
        t = pl.program_id(0)
        n = nb - 1 - t

        @pl.when(t == 0)
        def _():
            _swa_bias_table(rb_ref, bucket_ref[...], bias_s)
            dbias_s[...] = jnp.zeros_like(dbias_s)
            carry_s[...] = jnp.zeros_like(carry_s)
            dqg_ref[...] = jnp.zeros_like(dqg_ref)
            dkg_ref[...] = jnp.zeros_like(dkg_ref)
            dsink_ref[...] = jnp.zeros_like(dsink_ref)
            drb_ref[...] = jnp.zeros_like(drb_ref)

        own, valid = _swa_masks(n)
        lane = lax.broadcasted_iota(jnp.int32, (1, BLOCK), 1)
        dqg = jnp.zeros((1, HEAD_DIM), F32)
        dkg = jnp.zeros((1, HEAD_DIM), F32)
        dsink_vec = jnp.zeros((1, BLOCK), F32)
        for g in range(SWA_KV_HEADS):
            _, vjp = jax.vjp(functools.partial(_swa_group, own=own, valid=valid), _swa_stack(q_ref, g),
                             *_swa_band(kvp_ref, kvc_ref, g), qg_ref[...], kg_ref[...], _swa_sink_column(sink_ref, g),
                             bias_s[g])
            dq, dkp, dkc, dvp, dvc, dqg_g, dkg_g, dsink_col, dbias = vjp(_swa_stack(dy_ref, g))
            _swa_unstack(dq_ref, g, dq)
            dqg += dqg_g
            dkg += dkg_g
            dbias_s[g] += dbias
            for hh in range(SWA_GROUP):
                dsink_h = jnp.sum(dsink_col[hh * BLOCK:(hh + 1) * BLOCK], axis=0, keepdims=True)
                dsink_vec += jnp.where(lane == g * SWA_GROUP + hh, dsink_h, 0.0)
            lo = g * HEAD_DIM
            dkv_ref[:, lo:lo + HEAD_DIM] = dkc + carry_s[g]
            carry_s[g] = dkp
            lo += SWA_KV_W
            dkv_ref[:, lo:lo + HEAD_DIM] = dvc + carry_s[SWA_KV_HEADS + g]
            carry_s[SWA_KV_HEADS + g] = dvp
        dqg_ref[...] += dqg
        dkg_ref[...] += dkg
        dsink_ref[...] += dsink_vec

        @pl.when(t == nb - 1)
        def _():
            bucket = bucket_ref[...]
            row = lax.broadcasted_iota(jnp.int32, (N_BUCKETS, BLOCK), 0)
            col = lax.broadcasted_iota(jnp.int32, (N_BUCKETS, BLOCK), 1)
            acc = jnp.zeros((N_BUCKETS, BLOCK), F32)
            for h in range(SWA_HEADS):
                dbias = dbias_s[h // SWA_GROUP, (h % SWA_GROUP) * BLOCK:(h % SWA_GROUP + 1) * BLOCK, :]
                for b in range(N_BUCKETS):
                    part = jnp.sum(jnp.where(bucket == b, dbias, 0.0), axis=1, keepdims=True)
                    val = jnp.sum(part, axis=0, keepdims=True)
                    acc = acc + jnp.where((row == b) & (col == h), val, 0.0)
            drb_ref[...] = acc

    order = lambda t: nb - 1 - t
    vec = pl.BlockSpec((1, HEAD_DIM), lambda t: (0, 0))
    return pl.pallas_call(
        body, name=name, grid=(nb,),
        in_specs=_swa_specs(order) + [pl.BlockSpec((BLOCK, SWA_Q_W), lambda t: (order(t), 0))],
        out_specs=[pl.BlockSpec((BLOCK, SWA_Q_W), lambda t: (order(t), 0)),
                   pl.BlockSpec((BLOCK, 2 * SWA_KV_W), lambda t: (order(t), 0)),
                   vec, vec, pl.BlockSpec((1, BLOCK), lambda t: (0, 0)),
                   pl.BlockSpec((N_BUCKETS, BLOCK), lambda t: (0, 0))],
        out_shape=[SDS((s, SWA_Q_W), F32), SDS((s, 2 * SWA_KV_W), F32), SDS((1, HEAD_DIM), F32),
                   SDS((1, HEAD_DIM), F32), SDS((1, BLOCK), F32), SDS((N_BUCKETS, BLOCK), F32)],
        scratch_shapes=[pltpu.VMEM((SWA_KV_HEADS, SWA_STACK, BLOCK), F32),
                        pltpu.VMEM((SWA_KV_HEADS, SWA_STACK, BLOCK), F32),
                        pltpu.VMEM((2 * SWA_KV_HEADS, BLOCK, HEAD_DIM), F32)],
        compiler_params=_params("arbitrary"),
    )(p, p, p, qg, kg, sinks, rel_bias, jnp.asarray(_bucket_table()), dy_all)


def _mem_head(q, k, v, qg, kg):
    qn = _rms(q, qg)
    kn = _rms(k, kg)
    s = _mm(qn, kn, False, True) * (HEAD_DIM ** -0.5)
    m = lax.stop_gradient(jnp.max(s, axis=-1, keepdims=True))
    e = jnp.exp(s - m)
    return _mm(e / jnp.sum(e, axis=-1, keepdims=True), v)


def _mem_fwd(p, kv, qg, kg, *, tq, name):
    s = p.shape[0]
    m = kv.shape[0]

    def body(q_ref, kv_ref, qg_ref, kg_ref, y_ref):
        for h in range(MEM_HEADS):
            cols = slice(h * HEAD_DIM, (h + 1) * HEAD_DIM)
            vcols = slice(MEM_Q_W + h * HEAD_DIM, MEM_Q_W + (h + 1) * HEAD_DIM)
            y_ref[:, cols] = _mem_head(q_ref[:, cols], kv_ref[:, cols], kv_ref[:, vcols], qg_ref[...], kg_ref[...])

    vec = pl.BlockSpec((1, HEAD_DIM), lambda t: (0, 0))
    return pl.pallas_call(
        body, name=name, grid=(s // tq,),
        in_specs=[pl.BlockSpec((tq, MEM_Q_W), lambda t: (t, COL_MQ // MEM_Q_W)),
                  pl.BlockSpec((m, 2 * MEM_Q_W), lambda t: (0, 0)), vec, vec],
        out_specs=pl.BlockSpec((tq, MEM_Q_W), lambda t: (t, 0)), out_shape=SDS((s, MEM_Q_W), F32),
        compiler_params=_params("parallel"),
    )(p, kv, qg, kg)


def _mem_bwd(p, kv, qg, kg, dy_all, *, tq, name):
    s = p.shape[0]
    m = kv.shape[0]

    def body(q_ref, kv_ref, qg_ref, kg_ref, dy_ref, dq_ref, dkv_ref, dqg_ref, dkg_ref):
        @pl.when(pl.program_id(0) == 0)
        def _():
            dkv_ref[...] = jnp.zeros_like(dkv_ref)
            dqg_ref[...] = jnp.zeros_like(dqg_ref)
            dkg_ref[...] = jnp.zeros_like(dkg_ref)

        dqg = jnp.zeros((1, HEAD_DIM), F32)
        dkg = jnp.zeros((1, HEAD_DIM), F32)
        for h in range(MEM_HEADS):
            cols = slice(h * HEAD_DIM, (h + 1) * HEAD_DIM)
            vcols = slice(MEM_Q_W + h * HEAD_DIM, MEM_Q_W + (h + 1) * HEAD_DIM)
            _, vjp = jax.vjp(_mem_head, q_ref[:, cols], kv_ref[:, cols], kv_ref[:, vcols], qg_ref[...], kg_ref[...])
            dq, dk, dv, dqg_h, dkg_h = vjp(dy_ref[:, cols])
            dq_ref[:, cols] = dq
            dkv_ref[:, cols] += dk
            dkv_ref[:, vcols] += dv
            dqg += dqg_h
            dkg += dkg_h
        dqg_ref[...] += dqg
        dkg_ref[...] += dkg

    vec = pl.BlockSpec((1, HEAD_DIM), lambda t: (0, 0))
    full = pl.BlockSpec((m, 2 * MEM_Q_W), lambda t: (0, 0))
    dy_col = (SWA_Q_W + GLA_V_W) // MEM_Q_W
    return pl.pallas_call(
        body, name=name, grid=(s // tq,),
        in_specs=[pl.BlockSpec((tq, MEM_Q_W), lambda t: (t, COL_MQ // MEM_Q_W)), full, vec, vec,
                  pl.BlockSpec((tq, MEM_Q_W), lambda t: (t, dy_col))],
        out_specs=[pl.BlockSpec((tq, MEM_Q_W), lambda t: (t, 0)), full, vec, vec],
        out_shape=[SDS((s, MEM_Q_W), F32), SDS((m, 2 * MEM_Q_W), F32), SDS((1, HEAD_DIM), F32), SDS((1, HEAD_DIM), F32)],
        compiler_params=_params("arbitrary"),
    )(p, kv, qg, kg, dy_all)


GLA_ROWS = 256


GLA_GROUP = 4


def _gla_consts():
    c, h, r = GLA_CHUNK, GLA_HEADS, GLA_GROUP * GLA_CHUNK
    i2 = lax.broadcasted_iota(jnp.int32, (c, c), 0)
    j2 = lax.broadcasted_iota(jnp.int32, (c, c), 1)
    slab_q = lax.broadcasted_iota(jnp.int32, (h, r, GLA_QK_W), 0)
    lane_q = lax.broadcasted_iota(jnp.int32, (h, r, GLA_QK_W), 2)
    row_a = lax.broadcasted_iota(jnp.int32, (h * r, r), 0) % r
    col_a = lax.broadcasted_iota(jnp.int32, (h * r, r), 1)
    slab_o = lax.broadcasted_iota(jnp.int32, (h, r, GLA_V_W), 0)
    lane_o = lax.broadcasted_iota(jnp.int32, (h, r, GLA_V_W), 2)
    row_s = lax.broadcasted_iota(jnp.int32, (GLA_V_W, GLA_QK_W), 0)
    col_s = lax.broadcasted_iota(jnp.int32, (GLA_V_W, GLA_QK_W), 1)
    return dict(
        ltri=(j2 <= i2).astype(F32),
        m_q=(slab_q == lane_q // GLA_DK).astype(F32),
        causal=(col_a <= row_a) & (col_a // c == row_a // c),
        m_o=(slab_o == lane_o // GLA_DV).astype(F32),
        m_s=(row_s // GLA_DV == col_s // GLA_DK).astype(F32),
    )


def _gla_step(q, k, v, z, bg, st, c):
    h = GLA_HEADS
    kt, qt, qe, decay = [], [], [], []
    for qc, kc, zc in zip(q, k, z):
        la = _log_sigmoid(zc + bg) * (1.0 / GLA_TAU)
        b = _mmf(c["ltri"], la)
        bl = jnp.sum(la, axis=0, keepdims=True)
        qs = qc * (GLA_DK ** -0.5)
        kt.append(kc * jnp.exp(bl - b))
        qt.append(qs * jnp.exp(b - bl))
        qe.append(qs * jnp.exp(b))
        decay.append(jnp.exp(bl))
    o_intra = []
    rows = GLA_GROUP * GLA_CHUNK
    for lo in range(0, len(q), GLA_GROUP):
        qt_all, kt_all, v_all = (jnp.concatenate(parts[lo:lo + GLA_GROUP], axis=0) for parts in (qt, kt, v))
        q_stack = (jnp.broadcast_to(qt_all[None], (h, rows, GLA_QK_W)) * c["m_q"]).reshape(h * rows, GLA_QK_W)
        a = jnp.where(c["causal"], _mm3(q_stack, kt_all, False, True), 0.0)
        o_stack = _mm(a, v_all)
        o_intra.append(jnp.sum(o_stack.reshape(h, rows, GLA_V_W) * c["m_o"], axis=0))
    o_intra = jnp.concatenate(o_intra, axis=0)
    o_inter = []
    for qec, ktc, vc, dc in zip(qe, kt, v, decay):
        o_inter.append(_mm(qec, st, False, True))
        st = st * dc + _mm(vc, ktc, True, False) * c["m_s"]
    return o_intra + jnp.concatenate(o_inter, axis=0), st


def _gla_post(o, gg, gain, g64):
    ms = _mmf(o * o, g64) * (1.0 / GLA_DV)
    return o * lax.rsqrt(ms + EPS) * gain * jax.nn.silu(gg)


def _gla_g64():
    r = lax.broadcasted_iota(jnp.int32, (GLA_V_W, GLA_V_W), 0)
    c = lax.broadcasted_iota(jnp.int32, (GLA_V_W, GLA_V_W), 1)
    return (r // GLA_DV == c // GLA_DV).astype(F32)


def _gla_in_specs(order):
    r = GLA_ROWS
    return [
        pl.BlockSpec((r, GLA_QK_W), lambda t: (order(t), COL_GQ // GLA_QK_W)),
        pl.BlockSpec((r, GLA_QK_W), lambda t: (order(t), COL_GK // GLA_QK_W)),
        pl.BlockSpec((r, GLA_V_W), lambda t: (order(t), COL_GV // GLA_V_W)),
        pl.BlockSpec((r, GLA_V_W), lambda t: (order(t), COL_GG // GLA_V_W)),
        pl.BlockSpec((r, GLA_QK_W), lambda t: (order(t), 0)),
        pl.BlockSpec((1, GLA_QK_W), lambda t: (0, 0)),
        pl.BlockSpec((1, GLA_V_W), lambda t: (0, 0)),
    ]


def _gla_pieces(q_ref, k_ref, v_ref, z_ref, cps):
    chunk = lambda ref: [ref[ci * GLA_CHUNK:(ci + 1) * GLA_CHUNK, :] for ci in range(cps)]
    return chunk(q_ref), chunk(k_ref), chunk(v_ref), chunk(z_ref)


def _gla_fwd(p, z, bg, gain, *, name):
    s = p.shape[0]
    r = GLA_ROWS
    cps = r // GLA_CHUNK

    def body(q_ref, k_ref, v_ref, gg_ref, z_ref, bg_ref, gain_ref, y_ref, oraw_ref, stsave_ref, st_s):
        @pl.when(pl.program_id(0) == 0)
        def _():
            st_s[...] = jnp.zeros_like(st_s)

        st = st_s[...]
        stsave_ref[0] = st
        o, st = _gla_step(*_gla_pieces(q_ref, k_ref, v_ref, z_ref, cps), bg_ref[...], st, _gla_consts())
        oraw_ref[...] = o
        st_s[...] = st
        y_ref[...] = _gla_post(o, gg_ref[...], gain_ref[...], _gla_g64())

    rowv = pl.BlockSpec((r, GLA_V_W), lambda t: (t, 0))
    return pl.pallas_call(
        body, name=name, grid=(s // r,), in_specs=_gla_in_specs(lambda t: t),
        out_specs=[rowv, rowv, pl.BlockSpec((1, GLA_V_W, GLA_QK_W), lambda t: (t, 0, 0))],
        out_shape=[SDS((s, GLA_V_W), F32), SDS((s, GLA_V_W), F32), SDS((s // r, GLA_V_W, GLA_QK_W), F32)],
        scratch_shapes=[pltpu.VMEM((GLA_V_W, GLA_QK_W), F32)],
        compiler_params=_params("arbitrary"),
    )(p, p, p, p, z, bg, gain)


def _gla_bwd(p, z, bg, gain, oraw, stsave, dy_all, *, name):
    s = p.shape[0]
    r = GLA_ROWS
    cps = r // GLA_CHUNK
    nsteps = s // r
    w_qkvg = 2 * GLA_QK_W + 2 * GLA_V_W

    def body(q_ref, k_ref, v_ref, gg_ref, z_ref, bg_ref, gain_ref, oraw_ref, stsave_ref, dy_ref,
             dqkvg_ref, dz_ref, dbg_ref, dgain_ref, dst_s):
        @pl.when(pl.program_id(0) == 0)
        def _():
            dst_s[...] = jnp.zeros_like(dst_s)
            dbg_ref[...] = jnp.zeros_like(dbg_ref)
            dgain_ref[...] = jnp.zeros_like(dgain_ref)

        _, vjp = jax.vjp(functools.partial(_gla_post, g64=_gla_g64()), oraw_ref[...], gg_ref[...], gain_ref[...])
        do, dgg, dgain = vjp(dy_ref[...])
        dqkvg_ref[:, 2 * GLA_QK_W + GLA_V_W:] = dgg
        dgain_ref[...] += dgain
        _, vjp = jax.vjp(functools.partial(_gla_step, c=_gla_consts()), *_gla_pieces(q_ref, k_ref, v_ref, z_ref, cps),
                         bg_ref[...], stsave_ref[0])
        dq, dk, dv, dz, dbg, dst = vjp((do, dst_s[...]))
        for ci in range(cps):
            rows = slice(ci * GLA_CHUNK, (ci + 1) * GLA_CHUNK)
            dqkvg_ref[rows, 0:GLA_QK_W] = dq[ci]
            dqkvg_ref[rows, GLA_QK_W:2 * GLA_QK_W] = dk[ci]
            dqkvg_ref[rows, 2 * GLA_QK_W:2 * GLA_QK_W + GLA_V_W] = dv[ci]
            dz_ref[rows, :] = dz[ci]
        dst_s[...] = dst
        dbg_ref[...] += dbg

    order = lambda t: nsteps - 1 - t
    rowv = pl.BlockSpec((r, GLA_V_W), lambda t: (order(t), 0))
    return pl.pallas_call(
        body, name=name, grid=(nsteps,),
        in_specs=_gla_in_specs(order) + [
            rowv, pl.BlockSpec((1, GLA_V_W, GLA_QK_W), lambda t: (order(t), 0, 0)),
            pl.BlockSpec((r, GLA_V_W), lambda t: (order(t), SWA_Q_W // GLA_V_W))],
        out_specs=[pl.BlockSpec((r, w_qkvg), lambda t: (order(t), 0)), pl.BlockSpec((r, GLA_QK_W), lambda t: (order(t), 0)),
                   pl.BlockSpec((1, GLA_QK_W), lambda t: (0, 0)), pl.BlockSpec((1, GLA_V_W), lambda t: (0, 0))],
        out_shape=[SDS((s, w_qkvg), F32), SDS((s, GLA_QK_W), F32), SDS((1, GLA_QK_W), F32), SDS((1, GLA_V_W), F32)],
        scratch_shapes=[pltpu.VMEM((GLA_V_W, GLA_QK_W), F32)],
        compiler_params=_params("arbitrary"),
    )(p, p, p, p, z, bg, gain, oraw, stsave, dy_all)


def _local_step(x, mem, target, small, big, on_grads):
    g1, gmix, gmem, g2, sqg, skg, sinks, rel_bias, wgu, bg, gla_gain, mqg, mkg = small
    w3_1, win_p, wkv, wout, gather_ffn2 = big
    wgu_pad = jnp.zeros((GLA_QK_W, GLA_QK_W), BF16).at[:GLA_RANK].set(wgu.astype(BF16))
    gain256 = jnp.tile(gla_gain, (1, GLA_HEADS))

    (x1, h), saved1 = _ffn_fwd(x, g1, w3_1, "ffn1", next_gain=gmix)
    w3_2 = gather_ffn2(x1)
    p = _matmul(h, win_p, tm=1024, tn=IN_W_PAD, name="mix_in")
    hm = _rms_fwd(mem, gmem, tm=256, name="mem_rms")
    kv = _matmul(hm, wkv, tm=256, tn=512, name="mem_kv")
    p_glr = p[:, COL_GLR:]
    z = _matmul(p_glr, wgu_pad, tm=1024, tn=GLA_QK_W, name="gla_gate")
    y_swa = _swa_fwd(p, sqg, skg, sinks, rel_bias, name="swa_fwd")
    y_gla, oraw, stsave = _gla_fwd(p, z, bg, gain256, name="gla_fwd")
    y_mem = _mem_fwd(p, kv, mqg, mkg, tq=512, name="mem_fwd")
    x2 = _matmul([y_swa, y_gla, y_mem], wout, b_blocks=[0, 2, 3], tm=1024, tn=1024, res=x1, name="mix_out")
    (dy, dyb, loss), saved2 = _ffn_fwd(x2, g2, w3_2, "ffn2", target=target)

    tiles = w3_2.shape[1] // FFN_TN
    dh2, dw3_2 = _ffn_bwd_part(dyb, w3_2, saved2, 0, tiles, None, name="ffn2_bwd")
    dx2, dx2b, dg2 = _rms_bwd(x2, g2, dh2, dy, tm=512, name="ffn2_drms")
    dx2b = on_grads("ffn2", [dw3_2.reshape(3, 2, -1, dw3_2.shape[-1])], dx2b)
    dy_all = _matmul(dx2b, wout, tb=True, tm=1024, tn=1024, name="mix_dy")
    dwout = _matmul(jnp.concatenate([y_swa, y_gla, y_mem], axis=1), dx2b, ta=True, tm=512, tn=1024, out_dtype=BF16,
                    name="mix_dw_out")
    dq_swa, dkv_swa, dsqg, dskg, dsink, drb = _swa_bwd(p, sqg, skg, sinks, rel_bias, dy_all, name="swa_bwd")
    dqkvg, dz, dbg, dgain256 = _gla_bwd(p, z, bg, gain256, oraw, stsave, dy_all, name="gla_bwd")
    dmq, dkv_mem, dmqg, dmkg = _mem_bwd(p, kv, mqg, mkg, dy_all, tq=512, name="mem_bwd")
    dglr = _matmul(dz, wgu_pad, tb=True, tm=1024, tn=GLA_QK_W, name="gla_gate_dx")
    dwgu_pad = _matmul(p_glr, dz, ta=True, tm=GLA_QK_W, tn=GLA_QK_W, name="gla_gate_dw")
    dp = jnp.concatenate([dq_swa, dkv_swa, dqkvg, dmq, dglr], axis=1)
    dh = _matmul(dp, win_p, tb=True, tm=1024, tn=1024, name="mix_dh")
    dwin_p = _matmul(h, dp, ta=True, tm=1024, tn=640, out_dtype=BF16, name="mix_dw_in")
    dx1, dx1b, dgmix = _rms_bwd(x1, gmix, dh, dx2, tm=512, name="mix_drms")
    dwkv = _matmul(hm, dkv_mem, ta=True, tm=512, tn=512, out_dtype=BF16, name="mem_dw_kv")
    dx1b = on_grads("mix", (dwin_p, dwkv, dwout), dx1b)
    dhm = _matmul(dkv_mem, wkv, tb=True, tm=256, tn=512, name="mem_dh")
    _, _, dgmem = _rms_bwd(mem, gmem, dhm, None, tm=256, name="mem_drms")
    dh1, dw3_1a = _ffn_bwd_part(dx1b, w3_1, saved1, 0, tiles // 2, None, name="ffn1_bwd_a")
    dgla_gain = dgain256.reshape(GLA_HEADS, GLA_DV).sum(axis=0, keepdims=True)
    dsmall = [dgmix, dgmem, dg2, dsqg, dskg, dsink[:, :SWA_HEADS], drb[:, :SWA_HEADS], dwgu_pad[:GLA_RANK], dbg,
              dgla_gain, dmqg, dmkg, loss]
    dh1, dgmem, dwgu_pad = on_grads("ffn1a", [dw3_1a[:, None]], (dh1, dgmem, dwgu_pad), small=dsmall)
    dh1, dw3_1b = _ffn_bwd_part(dx1b, w3_1, saved1, tiles // 2, tiles // 2, dh1, name="ffn1_bwd_b")
    dx, _, dg1 = _rms_bwd(x, g1, dh1, dx1, tm=512, name="ffn1_drms")
    on_grads("ffn1b", [dw3_1b[:, None]], None, small=[dg1])
    return dx


def _mesh_place():
    x, y, c = lax.axis_index("x"), lax.axis_index("y"), lax.axis_index("c")
    other_chips = [(1 - x, y), (x, 1 - y), (1 - x, 1 - y)]
    return x, y, c, other_chips


def _handshake(peers):
    barrier = pltpu.get_barrier_semaphore()
    for peer in peers:
        pl.semaphore_signal(barrier, inc=1, device_id=peer, device_id_type=MESH)
    pl.semaphore_wait(barrier, len(peers))


def _sequencer_call(body, operands, out_shapes, sems, *, name, collective_id):
    return pl.kernel(
        body, name=name, out_type=out_shapes, mesh=plsc.ScalarSubcoreMesh(axis_name="sequencer", num_cores=1),
        scratch_types=sems, compiler_params=pltpu.CompilerParams(collective_id=collective_id),
    )(*operands)


def _window(ref, kind, slot, shape):
    if kind == "row":
        rows = pl.ds(pl.multiple_of(slot * shape[-2], 8), shape[-2])
        return ref.at[(slice(None),) * (len(shape) - 2) + (rows,)]
    return ref.at[slot]


def _gathered(shape, kind):
    if kind == "row":
        return tuple(shape[:-2]) + (N_DEV * shape[-2], shape[-1])
    return (N_DEV,) + tuple(shape)


def _half(view, hf):
    if len(view.shape) == 4:
        return view.at[:, hf]
    n = view.shape[0] // 2
    return view.at[pl.ds(hf * n, n)]


def _all_gather(shards, kinds, *, name, collective_id):
    nt = len(shards)

    def body(*refs):
        x_refs, o_refs = refs[:nt], refs[nt:2 * nt]
        send_sems, recv_sems, local_sems = refs[2 * nt:]
        x, y, c, _ = _mesh_place()
        me, sibling, xn, yn, diag = (x, y, c), (x, y, 1 - c), (1 - x, y, c), (x, 1 - y, c), (1 - x, 1 - y, c)
        _handshake([sibling, xn, yn])

        def win(t, block):
            bx, by, bc = block
            return _window(o_refs[t], kinds[t], 4 * bx + 2 * by + bc, shards[t].shape)

        def copy(k, t, src, dst, to):
            return pltpu.make_async_remote_copy(src_ref=src, dst_ref=dst, send_sem=send_sems.at[k, t],
                                                recv_sem=recv_sems.at[k, t], device_id=to, device_id_type=MESH)

        def piece(k, t, block, hf, to, from_shard=False):
            dst = _half(win(t, block), hf)
            return copy(k, t, _half(x_refs[t], hf) if from_shard else dst, dst, to)

        mine = [pltpu.make_async_copy(x_refs[t], win(t, me), local_sems.at[t]) for t in range(nt)]
        sent = []

        def start(cp):
            cp.start()
            sent.append(cp)

        for cp in mine:
            cp.start()
        for t in range(nt):
            start(copy(0, t, x_refs[t], win(t, me), sibling))
        for hf_x, hf_y in ((0, 1), (1, 0)):
            for t in range(nt):
                start(piece(1 + hf_x, t, me, hf_x, xn, True))
                start(piece(3 + hf_y, t, me, hf_y, yn, True))
        for k, block, hf, onward, k_sib in ((1, xn, 0, (5, yn), 7), (4, yn, 1, (6, xn), 10), (2, xn, 1, None, 8),
                                           (3, yn, 0, None, 9), (5, diag, 0, None, 11), (6, diag, 1, None, 12)):
            for t in range(nt):
                piece(k, t, block, hf, me).wait_recv()
                if onward is not None:
                    start(piece(onward[0], t, block, hf, onward[1]))
                start(piece(k_sib, t, block, hf, sibling))
        for t in range(nt):
            copy(0, t, x_refs[t], win(t, sibling), me).wait_recv()
        for k_sib, block, hf in ((7, xn, 0), (10, yn, 1), (8, xn, 1), (9, yn, 0), (11, diag, 0), (12, diag, 1)):
            for t in range(nt):
                bx, by, _ = block
                piece(k_sib, t, (bx, by, 1 - c), hf, me).wait_recv()
        for cp in sent:
            cp.wait_send()
        for cp in mine:
            cp.wait()

    return _sequencer_call(
        body, shards, [SDS(_gathered(s.shape, k), s.dtype) for s, k in zip(shards, kinds)],
        [pltpu.SemaphoreType.DMA((13, nt)), pltpu.SemaphoreType.DMA((13, nt)), pltpu.SemaphoreType.DMA((nt,))],
        name=name, collective_id=collective_id)


def _part_shape(shape, kind):
    if kind == "row":
        return tuple(shape[:-2]) + (shape[-2] // N_DEV, shape[-1])
    return tuple(shape[2:])


def _pair_exchange(grads, kinds, *, name, collective_id):
    nt = len(grads)
    part = [_part_shape(g.shape, k) for g, k in zip(grads, kinds)]

    def body(*refs):
        g_refs, o_refs = refs[:nt], refs[nt:2 * nt]
        send_sems, recv_sems = refs[2 * nt:]
        x, y, c, _ = _mesh_place()
        _handshake([(x, y, 1 - c)])
        copies = []
        for t in range(nt):
            for xy in range(4):
                src = g_refs[t].at[1 - c, xy] if kinds[t] == "stack" else _window(g_refs[t], kinds[t], 2 * xy + 1 - c, part[t])
                copies.append(pltpu.make_async_remote_copy(
                    src_ref=src, dst_ref=o_refs[t].at[xy], send_sem=send_sems.at[xy, t], recv_sem=recv_sems.at[xy, t],
                    device_id=(x, y, 1 - c), device_id_type=MESH))
        for cp in copies:
            cp.start()
        for cp in copies:
            cp.wait()

    return _sequencer_call(
        body, grads, [SDS((4,) + p, g.dtype) for p, g in zip(part, grads)],
        [pltpu.SemaphoreType.DMA((4, nt)), pltpu.SemaphoreType.DMA((4, nt))], name=name, collective_id=collective_id)


def _chip_exchange(parts, small, *, name, collective_id):
    nt = len(parts)
    if small is None:
        def body_plain(*refs):
            s_refs, o_refs = refs[:nt], refs[nt:2 * nt]
            send_sems, recv_sems = refs[2 * nt:]
            x, y, c, chips = _mesh_place()
            _handshake([(*chip, c) for chip in chips])
            copies = [pltpu.make_async_remote_copy(
                src_ref=s_refs[t].at[2 * chip[0] + chip[1]], dst_ref=o_refs[t].at[j],
                send_sem=send_sems.at[j, t], recv_sem=recv_sems.at[j, t], device_id=(*chip, c), device_id_type=MESH)
                for j, chip in enumerate(chips) for t in range(nt)]
            for cp in copies:
                cp.start()
            for cp in copies:
                cp.wait()

        return _sequencer_call(
            body_plain, parts, [SDS((3,) + s.shape[1:], s.dtype) for s in parts],
            [pltpu.SemaphoreType.DMA((3, nt)), pltpu.SemaphoreType.DMA((3, nt))], name=name, collective_id=collective_id)

    def body(*refs):
        s_refs, small_ref = refs[:nt], refs[nt]
        o_refs, small_all = refs[nt + 1:2 * nt + 1], refs[2 * nt + 1]
        send_sems, recv_sems, small_send, small_recv, local_sem = refs[2 * nt + 2:]
        x, y, c, chips = _mesh_place()
        _handshake([(px, py, pc) for px in (x, 1 - x) for py in (y, 1 - y) for pc in (c, 1 - c)][1:])

        def copy(j, t, chip):
            return pltpu.make_async_remote_copy(
                src_ref=s_refs[t].at[2 * chip[0] + chip[1]], dst_ref=o_refs[t].at[j],
                send_sem=send_sems.at[j, t], recv_sem=recv_sems.at[j, t], device_id=(*chip, c), device_id_type=MESH)

        flips = [(fx, fy, fc) for fx in (0, 1) for fy in (0, 1) for fc in (0, 1)][1:]

        def small_copy(k):
            fx, fy, fc = flips[k]
            to = (x ^ fx if fx else x, y ^ fy if fy else y, c ^ fc if fc else c)
            rows = small_all.at[4 * x + 2 * y + c]
            return pltpu.make_async_remote_copy(
                src_ref=small_ref, dst_ref=rows, send_sem=small_send.at[k], recv_sem=small_recv.at[k],
                device_id=to, device_id_type=MESH)

        own = pltpu.make_async_copy(small_ref, small_all.at[4 * x + 2 * y + c], local_sem)
        own.start()
        copies = [copy(j, t, chip) for j, chip in enumerate(chips) for t in range(nt)]
        smalls = [small_copy(k) for k in range(7)]
        for cp in smalls + copies:
            cp.start()
        for cp in smalls + copies:
            cp.wait()
        own.wait()

    return _sequencer_call(
        body, list(parts) + [small],
        [SDS((3,) + s.shape[1:], s.dtype) for s in parts] + [SDS((N_DEV,) + small.shape, small.dtype)],
        [pltpu.SemaphoreType.DMA((3, nt)), pltpu.SemaphoreType.DMA((3, nt)),
         pltpu.SemaphoreType.DMA((7,)), pltpu.SemaphoreType.DMA((7,)), pltpu.SemaphoreType.DMA],
        name=name, collective_id=collective_id)


def _pair_sum(grad, theirs, kind, c, *, name):
    if kind == "row":
        r, l = theirs.shape[-2:]
        n = theirs.size // (4 * r * l)
        grad, theirs = grad.reshape(n, N_DEV * r, l), theirs.reshape(4, n, r, l)
        mine_spec = pl.BlockSpec((n, r, l), lambda xy, c_ref: (0, 2 * xy + c_ref[0], 0))
    else:
        r, l = theirs.shape[-2:]
        n = theirs.size // (4 * r * l)
        theirs = theirs.reshape(4, n, r, l)
        grad = grad.reshape(2, 4, n, r, l)
        mine_spec = pl.BlockSpec((None, None, n, r, l), lambda xy, c_ref: (c_ref[0], xy, 0, 0, 0))

    def body(c_ref, a_ref, b_ref, o_ref):
        o_ref[...] = (a_ref[...].astype(F32) + b_ref[...].astype(F32)).astype(BF16)

    part = pl.BlockSpec((None, n, r, l), lambda xy, c_ref: (xy, 0, 0, 0))
    return pl.pallas_call(
        body, name=name,
        grid_spec=pltpu.PrefetchScalarGridSpec(num_scalar_prefetch=1, grid=(4,), in_specs=[mine_spec, part], out_specs=part),
        out_shape=SDS((4, n, r, l), BF16), compiler_params=_params("parallel"),
    )(c, grad, theirs)


def _adamw(w, g, m, v):
    m = ADAM_B1 * m + (1.0 - ADAM_B1) * g
    v = ADAM_B2 * v + (1.0 - ADAM_B2) * jnp.square(g)
    m_hat = m / (1.0 - ADAM_B1 ** ADAM_STEP)
    v_hat = v / (1.0 - ADAM_B2 ** ADAM_STEP)
    delta = -ADAM_LR * (m_hat / (jnp.sqrt(v_hat) + ADAM_EPS) + ADAM_WD * w)
    return delta, m, v


def _adam_big(owns, others, mat, xy, w, m, v, *, tr, name):
    nw, r, l = w.shape
    lp = owns[0].shape[-1]
    nq = len(owns)
    assert nq in (1, nw)

    def body(xy_ref, *refs):
        own_refs, oth_refs = refs[:nq], refs[nq:2 * nq]
        w_ref, m_ref, v_ref, g_out, d_out, m_out, v_out = refs[2 * nq:]
        g = None
        for q in range(nq):
            gq = own_refs[q][0, 0].astype(F32)
            for j in range(3):
                gq = gq + oth_refs[q][j, 0].astype(F32)
            g = gq if g is None else jnp.where(pl.program_id(0) == q, gq, g)
        g = g[:, :l]
        delta, m_new, v_new = _adamw(w_ref[0], g, m_ref[0], v_ref[0])
        g_out[0] = g
        d_out[0] = delta
        m_out[0] = m_new
        v_out[0] = v_new

    def at(p):
        return mat * nw + p if nq == 1 else mat

    blk = pl.BlockSpec((1, tr, l), lambda p, i, xy_ref: (p, i, 0))
    return pl.pallas_call(
        body, name=name,
        grid_spec=pltpu.PrefetchScalarGridSpec(
            num_scalar_prefetch=1, grid=(nw, r // tr),
            in_specs=[pl.BlockSpec((1, 1, tr, lp), lambda p, i, xy_ref: (xy_ref[0], at(p), i, 0))] * nq
            + [pl.BlockSpec((3, 1, tr, lp), lambda p, i, xy_ref: (0, at(p), i, 0))] * nq + [blk, blk, blk],
            out_specs=[blk, blk, blk, blk]),
        out_shape=[SDS(w.shape, F32)] * 4, compiler_params=_params("parallel", "parallel"),
    )(xy, *owns, *others, w, m, v)


def _small_layout(shapes):
    out, at = [], 0
    for r, c in shapes:
        rows = c // 128 if (r == 1 and c > 128) else r
        out.append((at, rows))
        at += -(-rows // 8) * 8
    return out, at


def _pack_small(parts, *, name):
    shapes = [a.shape for a in parts]
    layout, total = _small_layout(shapes)

    def body(*refs):
        o_ref = refs[-1]
        o_ref[...] = jnp.zeros_like(o_ref)
        for x_ref, (r, c), (at, rows) in zip(refs, shapes, layout):
            if r == 1 and c > 128:
                for k in range(rows):
                    o_ref[at + k:at + k + 1, :] = x_ref[:, k * 128:(k + 1) * 128]
            else:
                o_ref[at:at + r, 0:c] = x_ref[...]

    return pl.pallas_call(body, name=name, out_shape=SDS((total, 128), F32))(*parts)


def _adam_small(g_all, ws, ms, vs, *, name):
    n = len(ws)
    shapes = [w.shape for w in ws]
    layout, _ = _small_layout(shapes)

    def body(g_ref, *refs):
        w_refs, m_refs, v_refs, outs = refs[:n], refs[n:2 * n], refs[2 * n:3 * n], refs[3 * n:]
        g_sum = g_ref[0]
        for k in range(1, N_DEV):
            g_sum = g_sum + g_ref[k]
        for i, ((r, c), (at, rows)) in enumerate(zip(shapes, layout)):
            if r == 1 and c > 128:
                g = jnp.concatenate([g_sum[at + k:at + k + 1, :] for k in range(rows)], axis=1)
            else:
                g = g_sum[at:at + r, 0:c]
            delta, m_new, v_new = _adamw(w_refs[i][...], g, m_refs[i][...], v_refs[i][...])
            for q, val in enumerate((g, delta, m_new, v_new)):
                outs[4 * i + q][...] = val

    flat = pl.pallas_call(body, name=name, out_shape=[SDS(s, F32) for s in shapes for _ in range(4)])(g_all, *ws, *ms, *vs)
    return [flat[4 * i:4 * i + 4] for i in range(n)]


def kernel(x, mem, ffn1_norm, ffn1_w_gate, ffn1_w_up, ffn1_w_down, mix_norm, mem_norm, w_in, w_mem_kv, swa_q_norm, swa_k_norm, swa_sinks, rel_bias, gla_w_gate_up, gla_b_gate, gla_out_norm, mem_q_norm, mem_k_norm, w_out, ffn2_norm, ffn2_w_gate, ffn2_w_up, ffn2_w_down, loss_target, m_ffn1_norm, m_ffn1_w_gate, m_ffn1_w_up, m_ffn1_w_down, m_mix_norm, m_mem_norm, m_w_in, m_w_mem_kv, m_swa_q_norm, m_swa_k_norm, m_swa_sinks, m_rel_bias, m_gla_w_gate_up, m_gla_b_gate, m_gla_out_norm, m_mem_q_norm, m_mem_k_norm, m_w_out, m_ffn2_norm, m_ffn2_w_gate, m_ffn2_w_up, m_ffn2_w_down, v_ffn1_norm, v_ffn1_w_gate, v_ffn1_w_up, v_ffn1_w_down, v_mix_norm, v_mem_norm, v_w_in, v_w_mem_kv, v_swa_q_norm, v_swa_k_norm, v_swa_sinks, v_rel_bias, v_gla_w_gate_up, v_gla_b_gate, v_gla_out_norm, v_mem_q_norm, v_mem_k_norm, v_w_out, v_ffn2_norm, v_ffn2_w_gate, v_ffn2_w_up, v_ffn2_w_down):
    xi, yi, ci = lax.axis_index("x"), lax.axis_index("y"), lax.axis_index("c")
    c_arr = jnp.reshape(ci, (1,)).astype(jnp.int32)
    xy_arr = jnp.reshape(2 * xi + yi, (1,)).astype(jnp.int32)
    d = x.shape[-1]

    half_h = ffn1_w_gate.shape[-1] // 2

    def gather_ffn(wg_s, wu_s, wd_s, name, collective_id, after):
        w3_s = jnp.concatenate([wg_s.transpose(0, 2, 1), wu_s.transpose(0, 2, 1), wd_s], axis=0)
        w3_s = jnp.pad(w3_s.reshape(3, 2, half_h, d), ((0, 0), (0, 0), (0, FFN_HALF_PAD - half_h), (0, 0))).astype(BF16)
        if after is not None:
            w3_s, _ = lax.optimization_barrier((w3_s, after))
        return _all_gather([w3_s], ["row"], name=name, collective_id=collective_id)[0].reshape(3, -1, d)

    w3_1 = gather_ffn(ffn1_w_gate, ffn1_w_up, ffn1_w_down, "gather_ffn1", 0, None)
    mix_s = lax.optimization_barrier((w_in[0].astype(BF16), w_mem_kv[0].astype(BF16), w_out[0].astype(BF16), w3_1))[:3]
    win_all, wkv, wout = _all_gather(list(mix_s), ["stack", "row", "row"], name="gather_mix", collective_id=1)

    def gather_ffn2(x1):
        return gather_ffn(ffn2_w_gate, ffn2_w_up, ffn2_w_down, "gather_ffn2", 2, (wout, x1))

    win_p = _pack_win(win_all, tr=256, name="pack_w_in")

    small_w = [ffn1_norm, mix_norm, mem_norm, ffn2_norm, swa_q_norm, swa_k_norm, swa_sinks[0], rel_bias,
               gla_w_gate_up[0], gla_b_gate, gla_out_norm, mem_q_norm, mem_k_norm]
    collective_ids = {"ffn2": (3, 4), "mix": (5, 6), "ffn1a": (7, 8), "ffn1b": (9, 10)}
    reduced, small_box = {}, {}

    def on_grads(group, grads, carry, small=None):
        if group == "mix":
            dwin_p, dwkv, dwout = grads
            grads = [_unpack_win(dwin_p, tr=256, name="unpack_dw_in"), dwkv, dwout]
            kinds = ["stack", "row", "row"]
        else:
            kinds = ["row"]
        if reduced:
            earlier = list(reduced.values())[-1][1]
            *grads, _ = lax.optimization_barrier((*grads, earlier[0]))
        id_pair, id_chip = collective_ids[group]
        from_sibling = _pair_exchange(grads, kinds, name=f"pair_exchange_{group}", collective_id=id_pair)
        chip_sums = [_pair_sum(g, theirs, k, c_arr, name=f"pair_sum_{group}_{t}")
                     for t, (g, theirs, k) in enumerate(zip(grads, from_sibling, kinds))]
        if carry is not None:
            *chip_sums, carry = lax.optimization_barrier((*chip_sums, carry))
        if small is None:
            from_chips = _chip_exchange(chip_sums, None, name=f"chip_exchange_{group}", collective_id=id_chip)
        else:
            packed = _pack_small(small, name=f"pack_small_{group}")
            *from_chips, small_all = _chip_exchange(chip_sums, packed, name=f"chip_exchange_{group}",
                                                    collective_id=id_chip)
            small_box[group] = small_all
        reduced[group] = (chip_sums, from_chips)
        return carry

    grad_x = _local_step(x[0], mem[0], loss_target[0], small_w, (w3_1, win_p, wkv, wout, gather_ffn2), on_grads)

    big_w = {"ffn1_w_gate": ("ffn1", 0, 0, True, ffn1_w_gate, m_ffn1_w_gate, v_ffn1_w_gate),
             "ffn1_w_up": ("ffn1", 0, 1, True, ffn1_w_up, m_ffn1_w_up, v_ffn1_w_up),
             "ffn1_w_down": ("ffn1", 0, 2, False, ffn1_w_down, m_ffn1_w_down, v_ffn1_w_down),
             "w_in": ("mix", 0, 0, False, w_in, m_w_in, v_w_in),
             "w_mem_kv": ("mix", 1, 0, False, w_mem_kv, m_w_mem_kv, v_w_mem_kv),
             "w_out": ("mix", 2, 0, False, w_out, m_w_out, v_w_out),
             "ffn2_w_gate": ("ffn2", 0, 0, True, ffn2_w_gate, m_ffn2_w_gate, v_ffn2_w_gate),
             "ffn2_w_up": ("ffn2", 0, 1, True, ffn2_w_up, m_ffn2_w_up, v_ffn2_w_up),
             "ffn2_w_down": ("ffn2", 0, 2, False, ffn2_w_down, m_ffn2_w_down, v_ffn2_w_down)}
    res = {}
    for nm, (group, t, mat, transposed, w, m, v) in big_w.items():
        shape = w.shape
        if transposed:
            w, m, v = (a.transpose(0, 2, 1) for a in (w, m, v))
        if group != "mix":
            w, m, v = (a.reshape(2, half_h, d) for a in (w, m, v))
        r = w.shape[1]
        tr = 256 if r % 256 == 0 else r
        halves = ["ffn1a", "ffn1b"] if group == "ffn1" else [group]
        out = _adam_big([reduced[k][0][t] for k in halves], [reduced[k][1][t] for k in halves], mat, xy_arr, w, m, v,
                        tr=tr, name=f"adam_{nm}")
        if transposed:
            out = [a.reshape(1, -1, d).transpose(0, 2, 1) for a in out]
        res[nm] = [a.reshape(shape) for a in out]
    small_names = ["ffn1_norm", "mix_norm", "mem_norm", "ffn2_norm", "swa_q_norm", "swa_k_norm", "swa_sinks", "rel_bias",
                   "gla_w_gate_up", "gla_b_gate", "gla_out_norm", "mem_q_norm", "mem_k_norm"]
    small_m = [m_ffn1_norm, m_mix_norm, m_mem_norm, m_ffn2_norm, m_swa_q_norm, m_swa_k_norm, m_swa_sinks, m_rel_bias,
               m_gla_w_gate_up, m_gla_b_gate, m_gla_out_norm, m_mem_q_norm, m_mem_k_norm]
    small_v = [v_ffn1_norm, v_mix_norm, v_mem_norm, v_ffn2_norm, v_swa_q_norm, v_swa_k_norm, v_swa_sinks, v_rel_bias,
               v_gla_w_gate_up, v_gla_b_gate, v_gla_out_norm, v_mem_q_norm, v_mem_k_norm]
    small_full = [ffn1_norm, mix_norm, mem_norm, ffn2_norm, swa_q_norm, swa_k_norm, swa_sinks, rel_bias,
                  gla_w_gate_up, gla_b_gate, gla_out_norm, mem_q_norm, mem_k_norm]
    zero = jnp.zeros((1, 1), F32)
    two_d = lambda a: a.reshape(a.shape[-2:])
    for group, sel in (("ffn1a", slice(1, None)), ("ffn1b", slice(0, 1))):
        extra = [zero] if group == "ffn1a" else []
        ws, ms, vs = ([two_d(a) for a in arrs[sel]] + extra for arrs in (small_full, small_m, small_v))
        updated = _adam_small(small_box[group], ws, ms, vs, name=f"adam_small_{group}")
        for nm, full, out in zip(small_names[sel], small_full[sel], updated):
            res[nm] = [a.reshape(full.shape) for a in out]
        if extra:
            loss = updated[-1][0].reshape(())

    order = ["ffn1_norm", "ffn1_w_gate", "ffn1_w_up", "ffn1_w_down", "mix_norm", "mem_norm", "w_in", "w_mem_kv",
             "swa_q_norm", "swa_k_norm", "swa_sinks", "rel_bias", "gla_w_gate_up", "gla_b_gate", "gla_out_norm",
             "mem_q_norm", "mem_k_norm", "w_out", "ffn2_norm", "ffn2_w_gate", "ffn2_w_up", "ffn2_w_down"]
    outs = [loss, grad_x[None]]
    for q in range(4):
        outs += [res[nm][q] for nm in order]
    return tuple(outs)
```

```python
import functools
import math

import numpy as np
import jax
import jax.numpy as jnp
from jax import lax
from jax.experimental import pallas as pl
from jax.experimental.pallas import tpu as pltpu
from jax.experimental.pallas import tpu_sc as plsc

F32 = jnp.float32
BF16 = jnp.bfloat16
SDS = jax.ShapeDtypeStruct

EPS = 1e-6
HEAD_DIM = 64
SWA_HEADS = 8
SWA_KV_HEADS = 2
SWA_GROUP = SWA_HEADS // SWA_KV_HEADS
BLOCK = 128
N_BUCKETS = 32
MAX_DISTANCE = 128
GLA_HEADS = 4
GLA_DK = 32
GLA_DV = 64
GLA_RANK = 16
GLA_TAU = 16.0
GLA_CHUNK = 32
MEM_HEADS = 4
SWA_Q_W = SWA_HEADS * HEAD_DIM
SWA_KV_W = SWA_KV_HEADS * HEAD_DIM
GLA_QK_W = GLA_HEADS * GLA_DK
GLA_V_W = GLA_HEADS * GLA_DV
MEM_Q_W = MEM_HEADS * HEAD_DIM
IN_W = 1808
IN_W_PAD = 1920
COL_SQ, COL_SKV, COL_GQ, COL_GK, COL_GV, COL_GG, COL_MQ, COL_GLR = 0, 512, 768, 896, 1024, 1280, 1536, 1792

ADAM_LR = 0.001
ADAM_B1 = 0.9
ADAM_B2 = 0.999
ADAM_EPS = 1e-08
ADAM_WD = 0.01
ADAM_STEP = 10

N_DEV = 8
VMEM_LIMIT_BYTES = 56 * 1024 * 1024
MESH = pl.DeviceIdType.MESH


def _params(*sem):
    return pltpu.CompilerParams(dimension_semantics=sem or None, vmem_limit_bytes=VMEM_LIMIT_BYTES)


def _dot(a, b, ta, tb, precision=None):
    dims = (((0 if ta else 1,), (1 if tb else 0,)), ((), ()))
    return lax.dot_general(a, b, dims, preferred_element_type=F32, precision=precision)


def _mm_raw(a, b, ta=False, tb=False):
    return _dot(a.astype(BF16), b.astype(BF16), ta, tb)


def _mmf_raw(a, b, ta=False, tb=False):
    return _dot(a, b, ta, tb, lax.Precision.HIGHEST)


def _make_mm(raw):
    @functools.partial(jax.custom_vjp, nondiff_argnums=(2, 3))
    def mm(a, b, ta=False, tb=False):
        return raw(a, b, ta, tb)

    def fwd(a, b, ta, tb):
        return raw(a, b, ta, tb), (a, b)

    def bwd(ta, tb, res, g):
        a, b = res
        da = raw(b, g, tb, True) if ta else raw(g, b, False, not tb)
        db = raw(g, a, True, ta) if tb else raw(a, g, not ta, False)
        return da, db

    mm.defvjp(fwd, bwd)
    return mm


_mm = _make_mm(_mm_raw)
_mmf = _make_mm(_mmf_raw)


def _mm3(a, b, ta=False, tb=False):
    a_hi, b_hi = a.astype(BF16).astype(F32), b.astype(BF16).astype(F32)
    return _mm(a_hi, b_hi, ta, tb) + _mm(a_hi, b - b_hi, ta, tb) + _mm(a - a_hi, b_hi, ta, tb)


def _rms(x, g):
    return x * lax.rsqrt(jnp.mean(x * x, axis=-1, keepdims=True) + EPS) * g


def _silu_mul(g, u):
    return jax.nn.silu(g) * u


def _log_sigmoid(z):
    return jnp.minimum(z, 0.0) - jnp.log(1.0 + jnp.exp(-jnp.abs(z)))


def _matmul(a_list, b, *, ta=False, tb=False, tm, tn, b_blocks=None, res=None, scale=1.0, out_dtype=F32, name):
    if not isinstance(a_list, (list, tuple)):
        a_list = [a_list]
    n_a = len(a_list)
    m = a_list[0].shape[1] if ta else a_list[0].shape[0]
    ks = [a.shape[0] if ta else a.shape[1] for a in a_list]
    n = b.shape[0] if tb else b.shape[1]
    if b_blocks is None:
        assert n_a == 1
        b_blocks = [0]
    tm, tn = min(tm, m), min(tn, n)
    assert m % tm == 0 and n % tn == 0, (m, n, tm, tn)

    def body(*refs):
        a_refs, b_refs = refs[:n_a], refs[n_a:2 * n_a]
        r_ref = refs[2 * n_a] if res is not None else None
        o_ref = refs[-1]
        acc = _mm_raw(a_refs[0][...], b_refs[0][...], ta, tb)
        for k in range(1, n_a):
            acc = acc + _mm_raw(a_refs[k][...], b_refs[k][...], ta, tb)
        if scale != 1.0:
            acc = acc * scale
        if r_ref is not None:
            acc = r_ref[...] + acc
        o_ref[...] = acc.astype(out_dtype)

    in_specs = []
    for k in ks:
        in_specs.append(pl.BlockSpec((k, tm), lambda i, j: (0, i)) if ta else pl.BlockSpec((tm, k), lambda i, j: (i, 0)))
    for k, blk in zip(ks, b_blocks):
        if tb:
            in_specs.append(pl.BlockSpec((tn, k), functools.partial(lambda i, j, blk: (j, blk), blk=blk)))
        else:
            in_specs.append(pl.BlockSpec((k, tn), functools.partial(lambda i, j, blk: (blk, j), blk=blk)))
    args = list(a_list) + [b] * n_a
    if res is not None:
        in_specs.append(pl.BlockSpec((tm, tn), lambda i, j: (i, j)))
        args.append(res)
    return pl.pallas_call(
        body, name=name, grid=(m // tm, n // tn), in_specs=in_specs,
        out_specs=pl.BlockSpec((tm, tn), lambda i, j: (i, j)), out_shape=SDS((m, n), out_dtype),
        compiler_params=_params("parallel", "parallel"),
    )(*args)


def _win_pieces(w):
    glr_lo, glr_hi = COL_MQ, COL_MQ + GLA_RANK
    out = []
    for j in range(N_DEV):
        for lo, hi, shift in ((0, glr_lo, 0), (glr_lo, glr_hi, COL_GLR - glr_lo), (glr_hi, IN_W, COL_MQ - glr_hi)):
            s, e = max(j * w, lo), min((j + 1) * w, hi)
            if s < e:
                out.append((j, s - j * w, e - j * w, s + shift))
    return out


def _pack_win(win_all, *, tr, name):
    _, d, w = win_all.shape

    def body(i_ref, o_ref):
        for j, a, b, dst in _win_pieces(w):
            o_ref[:, dst:dst + b - a] = i_ref[j][:, a:b]
        o_ref[:, IN_W:] = jnp.zeros((tr, IN_W_PAD - IN_W), o_ref.dtype)

    return pl.pallas_call(
        body, name=name, grid=(d // tr,), in_specs=[pl.BlockSpec((N_DEV, tr, w), lambda i: (0, i, 0))],
        out_specs=pl.BlockSpec((tr, IN_W_PAD), lambda i: (i, 0)), out_shape=SDS((d, IN_W_PAD), win_all.dtype),
        compiler_params=_params("parallel"),
    )(win_all)


def _unpack_win(dwin_p, *, tr, name):
    d = dwin_p.shape[0]
    w = IN_W // N_DEV

    def body(i_ref, o_ref):
        for j, a, b, src in _win_pieces(w):
            o_ref[j % 2, j // 2, :, a:b] = i_ref[:, src:src + b - a]

    return pl.pallas_call(
        body, name=name, grid=(d // tr,), in_specs=[pl.BlockSpec((tr, IN_W_PAD), lambda i: (i, 0))],
        out_specs=pl.BlockSpec((2, 4, tr, w), lambda i: (0, 0, i, 0)), out_shape=SDS((2, 4, d, w), dwin_p.dtype),
        compiler_params=_params("parallel"),
    )(dwin_p)


def _rms_fwd(x, g, *, tm, name):
    s, d = x.shape

    def body(x_ref, g_ref, h_ref):
        h_ref[...] = _rms(x_ref[...], g_ref[...]).astype(BF16)

    return pl.pallas_call(
        body, name=name, grid=(s // tm,),
        in_specs=[pl.BlockSpec((tm, d), lambda i: (i, 0)), pl.BlockSpec((1, d), lambda i: (0, 0))],
        out_specs=pl.BlockSpec((tm, d), lambda i: (i, 0)), out_shape=SDS((s, d), BF16),
        compiler_params=_params("parallel"),
    )(x, g)


def _rms_bwd(x, g, dh, dres, *, tm, name):
    s, d = x.shape
    want_dx = dres is not None

    def body(*refs):
        if want_dx:
            x_ref, g_ref, dh_ref, dres_ref, dx_ref, dxb_ref, dg_ref = refs
        else:
            x_ref, g_ref, dh_ref, dg_ref = refs
        _, vjp = jax.vjp(_rms, x_ref[...], g_ref[...])
        dx, dg = vjp(dh_ref[...])
        if want_dx:
            dx = dres_ref[...] + dx
            dx_ref[...] = dx
            dxb_ref[...] = dx.astype(BF16)

        @pl.when(pl.program_id(0) == 0)
        def _():
            dg_ref[...] = jnp.zeros_like(dg_ref)

        dg_ref[...] += dg

    row = pl.BlockSpec((tm, d), lambda i: (i, 0))
    vec = pl.BlockSpec((1, d), lambda i: (0, 0))
    if want_dx:
        return pl.pallas_call(
            body, name=name, grid=(s // tm,), in_specs=[row, vec, row, row], out_specs=[row, row, vec],
            out_shape=[SDS((s, d), F32), SDS((s, d), BF16), SDS((1, d), F32)], compiler_params=_params("arbitrary"),
        )(x, g, dh, dres)
    return None, None, pl.pallas_call(
        body, name=name, grid=(s // tm,), in_specs=[row, vec, row], out_specs=vec,
        out_shape=SDS((1, d), F32), compiler_params=_params("arbitrary"),
    )(x, g, dh)


FFN_TN = 256
FFN_TN_FWD = 512
FFN_HALF_PAD = 192


def _ffn_fwd(x, gain, w3, tag, *, tm=1024, next_gain=None, target=None):
    s, d = x.shape
    f = w3.shape[1]
    tn = FFN_TN_FWD
    nj = f // tn
    tm = min(tm, s)
    n_extra = (next_gain is not None) + (target is not None)

    def body(*refs):
        x_ref, gain_ref, wg_ref, wu_ref, wd_ref = refs[:5]
        extra, outs = refs[5:5 + n_extra], refs[5 + n_extra:-1]
        acc_s = refs[-1]
        h_ref, g_ref, u_ref = outs[-3:]
        i, j = pl.program_id(0), pl.program_id(1)

        @pl.when(j == 0)
        def _():
            h_ref[...] = _rms(x_ref[...], gain_ref[...]).astype(BF16)
            acc_s[...] = jnp.zeros_like(acc_s)

        hv = h_ref[...]
        g = _mm_raw(hv, wg_ref[...], False, True)
        u = _mm_raw(hv, wu_ref[...], False, True)
        g_ref[...] = g.astype(BF16)
        u_ref[...] = u.astype(BF16)
        acc_s[...] += _mm_raw(_silu_mul(g, u), wd_ref[...])

        @pl.when(j == nj - 1)
        def _():
            y = x_ref[...] + 0.5 * acc_s[...]
            if target is None:
                outs[0][...] = y
                if next_gain is not None:
                    outs[1][...] = _rms(y, extra[0][...]).astype(BF16)
            else:
                dy_ref, dyb_ref, loss_ref = outs[:3]
                diff = y - extra[0][...]
                dy_ref[...] = diff * (1.0 / d)
                dyb_ref[...] = (diff * (1.0 / d)).astype(BF16)
                part = 0.5 * jnp.sum(jnp.mean(diff * diff, axis=-1, keepdims=True), axis=0, keepdims=True)

                @pl.when(i == 0)
                def _():
                    loss_ref[...] = part

                @pl.when(i > 0)
                def _():
                    loss_ref[...] += part

    row = pl.BlockSpec((tm, d), lambda i, j: (i, 0))
    vec = pl.BlockSpec((1, d), lambda i, j: (0, 0))
    tile = pl.BlockSpec((tm, tn), lambda i, j: (i, j))
    in_specs = [row, vec] + [pl.BlockSpec((None, tn, d), functools.partial(lambda i, j, k: (k, j, 0), k=k)) for k in range(3)]
    args = [x, gain, w3, w3, w3]
    if target is None:
        out_specs, out_shape = [row], [SDS((s, d), F32)]
        if next_gain is not None:
            in_specs.append(vec)
            args.append(next_gain)
            out_specs.append(row)
            out_shape.append(SDS((s, d), BF16))
    else:
        in_specs.append(row)
        args.append(target)
        out_specs = [row, row, pl.BlockSpec((1, 1), lambda i, j: (0, 0))]
        out_shape = [SDS((s, d), F32), SDS((s, d), BF16), SDS((1, 1), F32)]
    *head, h, g, u = pl.pallas_call(
        body, name=f"{tag}_fwd", grid=(s // tm, nj), in_specs=in_specs,
        out_specs=out_specs + [row, tile, tile],
        out_shape=out_shape + [SDS((s, d), BF16), SDS((s, f), BF16), SDS((s, f), BF16)],
        scratch_shapes=[pltpu.VMEM((tm, d), F32)],
        compiler_params=_params("arbitrary", "arbitrary"),
    )(*args)
    return head, (h, g, u)


def _ffn_bwd_part(dyb, w3, saved, first, count, dh_init, *, name):
    h, g, u = saved
    s, d = h.shape
    tn = FFN_TN

    def body(*refs):
        if dh_init is None:
            dy_ref, h_ref, wd_ref, wg_ref, wu_ref, g_ref, u_ref, dh_ref, dw3_ref, dg_s, du_s, a_s = refs
        else:
            dy_ref, h_ref, wd_ref, wg_ref, wu_ref, g_ref, u_ref, dh0_ref, dh_ref, dw3_ref, dg_s, du_s, a_s = refs
        j = pl.program_id(0)

        @pl.when(j == 0)
        def _():
            dh_ref[...] = jnp.zeros_like(dh_ref) if dh_init is None else dh0_ref[...]
            for ref in (dg_s, du_s, a_s):
                ref[...] = jnp.zeros_like(ref)

        now, before = j % 2, 1 - j % 2
        dyv = dy_ref[...]
        hv = h_ref[...]
        dg, du, a = dg_s[before], du_s[before], a_s[before]
        dh_ref[...] += _mm_raw(dg, wg_ref[...]) + _mm_raw(du, wu_ref[...])
        dw3_ref[0] = _mm_raw(dg, hv, True, False).astype(BF16)
        dw3_ref[1] = _mm_raw(du, hv, True, False).astype(BF16)
        dw3_ref[2] = (_mm_raw(a, dyv, True, False) * 0.5).astype(BF16)

        da = _mm_raw(dyv, wd_ref[...], False, True) * 0.5
        a, vjp = jax.vjp(_silu_mul, g_ref[...].astype(F32), u_ref[...].astype(F32))
        dg, du = vjp(da)
        dg_s[now] = dg.astype(BF16)
        du_s[now] = du.astype(BF16)
        a_s[now] = a.astype(BF16)

    this = lambda j: first + jnp.minimum(j, count - 1)
    last = lambda j: first + jnp.maximum(j - 1, 0)
    full = pl.BlockSpec((s, d), lambda j: (0, 0))
    once = pl.BlockSpec((s, d), lambda j: (0, 0), pipeline_mode=pl.Buffered(1))
    tile = pl.BlockSpec((s, tn), lambda j: (0, this(j)))
    in_specs = [once, once, pl.BlockSpec((None, tn, d), lambda j: (2, this(j), 0)),
                pl.BlockSpec((None, tn, d), lambda j: (0, last(j), 0)), pl.BlockSpec((None, tn, d), lambda j: (1, last(j), 0)),
                tile, tile]
    args = [dyb, h, w3, w3, w3, g, u]
    if dh_init is not None:
        in_specs.append(once)
        args.append(dh_init)
    return pl.pallas_call(
        body, name=name, grid=(count + 1,), in_specs=in_specs,
        out_specs=[full, pl.BlockSpec((3, tn, d), lambda j: (0, jnp.maximum(j - 1, 0), 0))],
        out_shape=[SDS((s, d), F32), SDS((3, count * tn, d), BF16)],
        scratch_shapes=[pltpu.VMEM((2, s, tn), BF16)] * 3,
        compiler_params=_params("arbitrary"),
    )(*args)


def _bucket_table():
    qi = np.arange(BLOCK)[:, None]
    kj = np.arange(2 * BLOCK)[None, :]
    dist = np.maximum(qi + BLOCK - kj, 0)
    max_exact = N_BUCKETS // 2
    d = np.maximum(dist, 1).astype(np.float32)
    large = max_exact + (np.log(d / np.float32(max_exact)) / np.float32(math.log(MAX_DISTANCE / max_exact))
                         * np.float32(N_BUCKETS - max_exact)).astype(np.int32)
    large = np.minimum(large, N_BUCKETS - 1)
    band = np.where(dist < max_exact, dist, large).astype(np.int32)
    return np.where(np.tril(np.ones((BLOCK, BLOCK), bool)), band[:, BLOCK:], band[:, :BLOCK])


SWA_STACK = SWA_GROUP * BLOCK


def _swa_masks(n):
    qi = lax.broadcasted_iota(jnp.int32, (SWA_STACK, BLOCK), 0) % BLOCK
    kj = lax.broadcasted_iota(jnp.int32, (SWA_STACK, BLOCK), 1)
    own = kj <= qi
    return own, own | (n > 0)


def _swa_group(q, kp, kc, vp, vc, qg, kg, sink, bias, own, valid):
    qn = _rms(q, qg)
    s = jnp.where(own, _mm(qn, _rms(kc, kg), False, True), _mm(qn, _rms(kp, kg), False, True))
    s = s * (HEAD_DIM ** -0.5) + bias
    s = jnp.where(valid, s, -jnp.inf)
    m = lax.stop_gradient(jnp.maximum(jnp.max(s, axis=-1, keepdims=True), sink))
    p = jnp.exp(s - m)
    p = p / (jnp.sum(p, axis=-1, keepdims=True) + jnp.exp(sink - m))
    return _mm(jnp.where(own, p, 0.0), vc) + _mm(jnp.where(own, 0.0, p), vp)


def _swa_bias_table(rb_ref, bucket, bias_s):
    for h in range(SWA_HEADS):
        acc = jnp.zeros((BLOCK, BLOCK), F32)
        for b in range(N_BUCKETS):
            acc = jnp.where(bucket == b, rb_ref[b, h], acc)
        bias_s[h // SWA_GROUP, (h % SWA_GROUP) * BLOCK:(h % SWA_GROUP + 1) * BLOCK, :] = acc


def _swa_stack(ref, g):
    return jnp.concatenate([ref[:, (g * SWA_GROUP + hh) * HEAD_DIM:(g * SWA_GROUP + hh + 1) * HEAD_DIM]
                            for hh in range(SWA_GROUP)], axis=0)


def _swa_unstack(ref, g, stacked):
    for hh in range(SWA_GROUP):
        h = g * SWA_GROUP + hh
        ref[:, h * HEAD_DIM:(h + 1) * HEAD_DIM] = stacked[hh * BLOCK:(hh + 1) * BLOCK]


def _swa_sink_column(sink_ref, g):
    head = lax.broadcasted_iota(jnp.int32, (SWA_STACK, 1), 0) // BLOCK
    col = jnp.zeros((SWA_STACK, 1), F32)
    for hh in range(SWA_GROUP):
        col = jnp.where(head == hh, sink_ref[g * SWA_GROUP + hh], col)
    return col


def _swa_band(kvp_ref, kvc_ref, g):
    k = slice(g * HEAD_DIM, (g + 1) * HEAD_DIM)
    v = slice(SWA_KV_W + g * HEAD_DIM, SWA_KV_W + (g + 1) * HEAD_DIM)
    return kvp_ref[:, k], kvc_ref[:, k], kvp_ref[:, v], kvc_ref[:, v]


def _swa_specs(order):
    kvc = COL_SKV // (2 * SWA_KV_W)
    return [
        pl.BlockSpec((BLOCK, SWA_Q_W), lambda t: (order(t), 0)),
        pl.BlockSpec((BLOCK, 2 * SWA_KV_W), lambda t: (jnp.maximum(order(t) - 1, 0), kvc)),
        pl.BlockSpec((BLOCK, 2 * SWA_KV_W), lambda t: (order(t), kvc)),
        pl.BlockSpec((1, HEAD_DIM), lambda t: (0, 0)),
        pl.BlockSpec((1, HEAD_DIM), lambda t: (0, 0)),
        pl.BlockSpec(memory_space=pltpu.SMEM),
        pl.BlockSpec(memory_space=pltpu.SMEM),
        pl.BlockSpec((BLOCK, BLOCK), lambda t: (0, 0)),
    ]


def _swa_fwd(p, qg, kg, sinks, rel_bias, *, name):
    s = p.shape[0]
    nb = s // BLOCK

    def body(q_ref, kvp_ref, kvc_ref, qg_ref, kg_ref, sink_ref, rb_ref, bucket_ref, y_ref, bias_s):
        n = pl.program_id(0)

        @pl.when(n == 0)
        def _():
            _swa_bias_table(rb_ref, bucket_ref[...], bias_s)

        own, valid = _swa_masks(n)
        for g in range(SWA_KV_HEADS):
            out = _swa_group(_swa_stack(q_ref, g), *_swa_band(kvp_ref, kvc_ref, g), qg_ref[...], kg_ref[...],
                             _swa_sink_column(sink_ref, g), bias_s[g], own, valid)
            _swa_unstack(y_ref, g, out)

    return pl.pallas_call(
        body, name=name, grid=(nb,), in_specs=_swa_specs(lambda t: t),
        out_specs=pl.BlockSpec((BLOCK, SWA_Q_W), lambda t: (t, 0)), out_shape=SDS((s, SWA_Q_W), F32),
        scratch_shapes=[pltpu.VMEM((SWA_KV_HEADS, SWA_STACK, BLOCK), F32)],
        compiler_params=_params("arbitrary"),
    )(p, p, p, qg, kg, sinks, rel_bias, jnp.asarray(_bucket_table()))


def _swa_bwd(p, qg, kg, sinks, rel_bias, dy_all, *, name):
    s = p.shape[0]
    nb = s // BLOCK

    def body(q_ref, kvp_ref, kvc_ref, qg_ref, kg_ref, sink_ref, rb_ref, bucket_ref, dy_ref,
             dq_ref, dkv_ref, dqg_ref, dkg_ref, dsink_ref, drb_ref, bias_s, dbias_s, carry_s):
        t = pl.program_id(0)
        n = nb - 1 - t

        @pl.when(t == 0)
        def _():
            _swa_bias_table(rb_ref, bucket_ref[...], bias_s)
            dbias_s[...] = jnp.zeros_like(dbias_s)
            carry_s[...] = jnp.zeros_like(carry_s)
            dqg_ref[...] = jnp.zeros_like(dqg_ref)
            dkg_ref[...] = jnp.zeros_like(dkg_ref)
            dsink_ref[...] = jnp.zeros_like(dsink_ref)
            drb_ref[...] = jnp.zeros_like(drb_ref)

        own, valid = _swa_masks(n)
        lane = lax.broadcasted_iota(jnp.int32, (1, BLOCK), 1)
        dqg = jnp.zeros((1, HEAD_DIM), F32)
        dkg = jnp.zeros((1, HEAD_DIM), F32)
        dsink_vec = jnp.zeros((1, BLOCK), F32)
        for g in range(SWA_KV_HEADS):
            _, vjp = jax.vjp(functools.partial(_swa_group, own=own, valid=valid), _swa_stack(q_ref, g),
                             *_swa_band(kvp_ref, kvc_ref, g), qg_ref[...], kg_ref[...], _swa_sink_column(sink_ref, g),
                             bias_s[g])
            dq, dkp, dkc, dvp, dvc, dqg_g, dkg_g, dsink_col, dbias = vjp(_swa_stack(dy_ref, g))
            _swa_unstack(dq_ref, g, dq)
            dqg += dqg_g
            dkg += dkg_g
            dbias_s[g] += dbias
            for hh in range(SWA_GROUP):
                dsink_h = jnp.sum(dsink_col[hh * BLOCK:(hh + 1) * BLOCK], axis=0, keepdims=True)
                dsink_vec += jnp.where(lane == g * SWA_GROUP + hh, dsink_h, 0.0)
            lo = g * HEAD_DIM
            dkv_ref[:, lo:lo + HEAD_DIM] = dkc + carry_s[g]
            carry_s[g] = dkp
            lo += SWA_KV_W
            dkv_ref[:, lo:lo + HEAD_DIM] = dvc + carry_s[SWA_KV_HEADS + g]
            carry_s[SWA_KV_HEADS + g] = dvp
        dqg_ref[...] += dqg
        dkg_ref[...] += dkg
        dsink_ref[...] += dsink_vec

        @pl.when(t == nb - 1)
        def _():
            bucket = bucket_ref[...]
            row = lax.broadcasted_iota(jnp.int32, (N_BUCKETS, BLOCK), 0)
            col = lax.broadcasted_iota(jnp.int32, (N_BUCKETS, BLOCK), 1)
            acc = jnp.zeros((N_BUCKETS, BLOCK), F32)
            for h in range(SWA_HEADS):
                dbias = dbias_s[h // SWA_GROUP, (h % SWA_GROUP) * BLOCK:(h % SWA_GROUP + 1) * BLOCK, :]
                for b in range(N_BUCKETS):
                    part = jnp.sum(jnp.where(bucket == b, dbias, 0.0), axis=1, keepdims=True)
                    val = jnp.sum(part, axis=0, keepdims=True)
                    acc = acc + jnp.where((row == b) & (col == h), val, 0.0)
            drb_ref[...] = acc

    order = lambda t: nb - 1 - t
    vec = pl.BlockSpec((1, HEAD_DIM), lambda t: (0, 0))
    return pl.pallas_call(
        body, name=name, grid=(nb,),
        in_specs=_swa_specs(order) + [pl.BlockSpec((BLOCK, SWA_Q_W), lambda t: (order(t), 0))],
        out_specs=[pl.BlockSpec((BLOCK, SWA_Q_W), lambda t: (order(t), 0)),
                   pl.BlockSpec((BLOCK, 2 * SWA_KV_W), lambda t: (order(t), 0)),
                   vec, vec, pl.BlockSpec((1, BLOCK), lambda t: (0, 0)),
                   pl.BlockSpec((N_BUCKETS, BLOCK), lambda t: (0, 0))],
        out_shape=[SDS((s, SWA_Q_W), F32), SDS((s, 2 * SWA_KV_W), F32), SDS((1, HEAD_DIM), F32),
                   SDS((1, HEAD_DIM), F32), SDS((1, BLOCK), F32), SDS((N_BUCKETS, BLOCK), F32)],
        scratch_shapes=[pltpu.VMEM((SWA_KV_HEADS, SWA_STACK, BLOCK), F32),
                        pltpu.VMEM((SWA_KV_HEADS, SWA_STACK, BLOCK), F32),
                        pltpu.VMEM((2 * SWA_KV_HEADS, BLOCK, HEAD_DIM), F32)],
        compiler_params=_params("arbitrary"),
    )(p, p, p, qg, kg, sinks, rel_bias, jnp.asarray(_bucket_table()), dy_all)


def _mem_head(q, k, v, qg, kg):
    qn = _rms(q, qg)
    kn = _rms(k, kg)
    s = _mm(qn, kn, False, True) * (HEAD_DIM ** -0.5)
    m = lax.stop_gradient(jnp.max(s, axis=-1, keepdims=True))
    e = jnp.exp(s - m)
    return _mm(e / jnp.sum(e, axis=-1, keepdims=True), v)


def _mem_fwd(p, kv, qg, kg, *, tq, name):
    s = p.shape[0]
    m = kv.shape[0]

    def body(q_ref, kv_ref, qg_ref, kg_ref, y_ref):
        for h in range(MEM_HEADS):
            cols = slice(h * HEAD_DIM, (h + 1) * HEAD_DIM)
            vcols = slice(MEM_Q_W + h * HEAD_DIM, MEM_Q_W + (h + 1) * HEAD_DIM)
            y_ref[:, cols] = _mem_head(q_ref[:, cols], kv_ref[:, cols], kv_ref[:, vcols], qg_ref[...], kg_ref[...])

    vec = pl.BlockSpec((1, HEAD_DIM), lambda t: (0, 0))
    return pl.pallas_call(
        body, name=name, grid=(s // tq,),
        in_specs=[pl.BlockSpec((tq, MEM_Q_W), lambda t: (t, COL_MQ // MEM_Q_W)),
                  pl.BlockSpec((m, 2 * MEM_Q_W), lambda t: (0, 0)), vec, vec],
        out_specs=pl.BlockSpec((tq, MEM_Q_W), lambda t: (t, 0)), out_shape=SDS((s, MEM_Q_W), F32),
        compiler_params=_params("parallel"),
    )(p, kv, qg, kg)


def _mem_bwd(p, kv, qg, kg, dy_all, *, tq, name):
    s = p.shape[0]
    m = kv.shape[0]

    def body(q_ref, kv_ref, qg_ref, kg_ref, dy_ref, dq_ref, dkv_ref, dqg_ref, dkg_ref):
        @pl.when(pl.program_id(0) == 0)
        def _():
            dkv_ref[...] = jnp.zeros_like(dkv_ref)
            dqg_ref[...] = jnp.zeros_like(dqg_ref)
            dkg_ref[...] = jnp.zeros_like(dkg_ref)

        dqg = jnp.zeros((1, HEAD_DIM), F32)
        dkg = jnp.zeros((1, HEAD_DIM), F32)
        for h in range(MEM_HEADS):
            cols = slice(h * HEAD_DIM, (h + 1) * HEAD_DIM)
            vcols = slice(MEM_Q_W + h * HEAD_DIM, MEM_Q_W + (h + 1) * HEAD_DIM)
            _, vjp = jax.vjp(_mem_head, q_ref[:, cols], kv_ref[:, cols], kv_ref[:, vcols], qg_ref[...], kg_ref[...])
            dq, dk, dv, dqg_h, dkg_h = vjp(dy_ref[:, cols])
            dq_ref[:, cols] = dq
            dkv_ref[:, cols] += dk
            dkv_ref[:, vcols] += dv
            dqg += dqg_h
            dkg += dkg_h
        dqg_ref[...] += dqg
        dkg_ref[...] += dkg

    vec = pl.BlockSpec((1, HEAD_DIM), lambda t: (0, 0))
    full = pl.BlockSpec((m, 2 * MEM_Q_W), lambda t: (0, 0))
    dy_col = (SWA_Q_W + GLA_V_W) // MEM_Q_W
    return pl.pallas_call(
        body, name=name, grid=(s // tq,),
        in_specs=[pl.BlockSpec((tq, MEM_Q_W), lambda t: (t, COL_MQ // MEM_Q_W)), full, vec, vec,
                  pl.BlockSpec((tq, MEM_Q_W), lambda t: (t, dy_col))],
        out_specs=[pl.BlockSpec((tq, MEM_Q_W), lambda t: (t, 0)), full, vec, vec],
        out_shape=[SDS((s, MEM_Q_W), F32), SDS((m, 2 * MEM_Q_W), F32), SDS((1, HEAD_DIM), F32), SDS((1, HEAD_DIM), F32)],
        compiler_params=_params("arbitrary"),
    )(p, kv, qg, kg, dy_all)


GLA_ROWS = 256


GLA_GROUP = 4


def _gla_consts():
    c, h, r = GLA_CHUNK, GLA_HEADS, GLA_GROUP * GLA_CHUNK
    i2 = lax.broadcasted_iota(jnp.int32, (c, c), 0)
    j2 = lax.broadcasted_iota(jnp.int32, (c, c), 1)
    slab_q = lax.broadcasted_iota(jnp.int32, (h, r, GLA_QK_W), 0)
    lane_q = lax.broadcasted_iota(jnp.int32, (h, r, GLA_QK_W), 2)
    row_a = lax.broadcasted_iota(jnp.int32, (h * r, r), 0) % r
    col_a = lax.broadcasted_iota(jnp.int32, (h * r, r), 1)
    slab_o = lax.broadcasted_iota(jnp.int32, (h, r, GLA_V_W), 0)
    lane_o = lax.broadcasted_iota(jnp.int32, (h, r, GLA_V_W), 2)
    row_s = lax.broadcasted_iota(jnp.int32, (GLA_V_W, GLA_QK_W), 0)
    col_s = lax.broadcasted_iota(jnp.int32, (GLA_V_W, GLA_QK_W), 1)
    return dict(
        ltri=(j2 <= i2).astype(F32),
        m_q=(slab_q == lane_q // GLA_DK).astype(F32),
        causal=(col_a <= row_a) & (col_a // c == row_a // c),
        m_o=(slab_o == lane_o // GLA_DV).astype(F32),
        m_s=(row_s // GLA_DV == col_s // GLA_DK).astype(F32),
    )


def _gla_step(q, k, v, z, bg, st, c):
    h = GLA_HEADS
    kt, qt, qe, decay = [], [], [], []
    for qc, kc, zc in zip(q, k, z):
        la = _log_sigmoid(zc + bg) * (1.0 / GLA_TAU)
        b = _mmf(c["ltri"], la)
        bl = jnp.sum(la, axis=0, keepdims=True)
        qs = qc * (GLA_DK ** -0.5)
        kt.append(kc * jnp.exp(bl - b))
        qt.append(qs * jnp.exp(b - bl))
        qe.append(qs * jnp.exp(b))
        decay.append(jnp.exp(bl))
    o_intra = []
    rows = GLA_GROUP * GLA_CHUNK
    for lo in range(0, len(q), GLA_GROUP):
        qt_all, kt_all, v_all = (jnp.concatenate(parts[lo:lo + GLA_GROUP], axis=0) for parts in (qt, kt, v))
        q_stack = (jnp.broadcast_to(qt_all[None], (h, rows, GLA_QK_W)) * c["m_q"]).reshape(h * rows, GLA_QK_W)
        a = jnp.where(c["causal"], _mm3(q_stack, kt_all, False, True), 0.0)
        o_stack = _mm(a, v_all)
        o_intra.append(jnp.sum(o_stack.reshape(h, rows, GLA_V_W) * c["m_o"], axis=0))
    o_intra = jnp.concatenate(o_intra, axis=0)
    o_inter = []
    for qec, ktc, vc, dc in zip(qe, kt, v, decay):
        o_inter.append(_mm(qec, st, False, True))
        st = st * dc + _mm(vc, ktc, True, False) * c["m_s"]
    return o_intra + jnp.concatenate(o_inter, axis=0), st


def _gla_post(o, gg, gain, g64):
    ms = _mmf(o * o, g64) * (1.0 / GLA_DV)
    return o * lax.rsqrt(ms + EPS) * gain * jax.nn.silu(gg)


def _gla_g64():
    r = lax.broadcasted_iota(jnp.int32, (GLA_V_W, GLA_V_W), 0)
    c = lax.broadcasted_iota(jnp.int32, (GLA_V_W, GLA_V_W), 1)
    return (r // GLA_DV == c // GLA_DV).astype(F32)


def _gla_in_specs(order):
    r = GLA_ROWS
    return [
        pl.BlockSpec((r, GLA_QK_W), lambda t: (order(t), COL_GQ // GLA_QK_W)),
        pl.BlockSpec((r, GLA_QK_W), lambda t: (order(t), COL_GK // GLA_QK_W)),
        pl.BlockSpec((r, GLA_V_W), lambda t: (order(t), COL_GV // GLA_V_W)),
        pl.BlockSpec((r, GLA_V_W), lambda t: (order(t), COL_GG // GLA_V_W)),
        pl.BlockSpec((r, GLA_QK_W), lambda t: (order(t), 0)),
        pl.BlockSpec((1, GLA_QK_W), lambda t: (0, 0)),
        pl.BlockSpec((1, GLA_V_W), lambda t: (0, 0)),
    ]


def _gla_pieces(q_ref, k_ref, v_ref, z_ref, cps):
    chunk = lambda ref: [ref[ci * GLA_CHUNK:(ci + 1) * GLA_CHUNK, :] for ci in range(cps)]
    return chunk(q_ref), chunk(k_ref), chunk(v_ref), chunk(z_ref)


def _gla_fwd(p, z, bg, gain, *, name):
    s = p.shape[0]
    r = GLA_ROWS
    cps = r // GLA_CHUNK

    def body(q_ref, k_ref, v_ref, gg_ref, z_ref, bg_ref, gain_ref, y_ref, oraw_ref, stsave_ref, st_s):
        @pl.when(pl.program_id(0) == 0)
        def _():
            st_s[...] = jnp.zeros_like(st_s)

        st = st_s[...]
        stsave_ref[0] = st
        o, st = _gla_step(*_gla_pieces(q_ref, k_ref, v_ref, z_ref, cps), bg_ref[...], st, _gla_consts())
        oraw_ref[...] = o
        st_s[...] = st
        y_ref[...] = _gla_post(o, gg_ref[...], gain_ref[...], _gla_g64())

    rowv = pl.BlockSpec((r, GLA_V_W), lambda t: (t, 0))
    return pl.pallas_call(
        body, name=name, grid=(s // r,), in_specs=_gla_in_specs(lambda t: t),
        out_specs=[rowv, rowv, pl.BlockSpec((1, GLA_V_W, GLA_QK_W), lambda t: (t, 0, 0))],
        out_shape=[SDS((s, GLA_V_W), F32), SDS((s, GLA_V_W), F32), SDS((s // r, GLA_V_W, GLA_QK_W), F32)],
        scratch_shapes=[pltpu.VMEM((GLA_V_W, GLA_QK_W), F32)],
        compiler_params=_params("arbitrary"),
    )(p, p, p, p, z, bg, gain)


def _gla_bwd(p, z, bg, gain, oraw, stsave, dy_all, *, name):
    s = p.shape[0]
    r = GLA_ROWS
    cps = r // GLA_CHUNK
    nsteps = s // r
    w_qkvg = 2 * GLA_QK_W + 2 * GLA_V_W

    def body(q_ref, k_ref, v_ref, gg_ref, z_ref, bg_ref, gain_ref, oraw_ref, stsave_ref, dy_ref,
             dqkvg_ref, dz_ref, dbg_ref, dgain_ref, dst_s):
        @pl.when(pl.program_id(0) == 0)
        def _():
            dst_s[...] = jnp.zeros_like(dst_s)
            dbg_ref[...] = jnp.zeros_like(dbg_ref)
            dgain_ref[...] = jnp.zeros_like(dgain_ref)

        _, vjp = jax.vjp(functools.partial(_gla_post, g64=_gla_g64()), oraw_ref[...], gg_ref[...], gain_ref[...])
        do, dgg, dgain = vjp(dy_ref[...])
        dqkvg_ref[:, 2 * GLA_QK_W + GLA_V_W:] = dgg
        dgain_ref[...] += dgain
        _, vjp = jax.vjp(functools.partial(_gla_step, c=_gla_consts()), *_gla_pieces(q_ref, k_ref, v_ref, z_ref, cps),
                         bg_ref[...], stsave_ref[0])
        dq, dk, dv, dz, dbg, dst = vjp((do, dst_s[...]))
        for ci in range(cps):
            rows = slice(ci * GLA_CHUNK, (ci + 1) * GLA_CHUNK)
            dqkvg_ref[rows, 0:GLA_QK_W] = dq[ci]
            dqkvg_ref[rows, GLA_QK_W:2 * GLA_QK_W] = dk[ci]
            dqkvg_ref[rows, 2 * GLA_QK_W:2 * GLA_QK_W + GLA_V_W] = dv[ci]
            dz_ref[rows, :] = dz[ci]
        dst_s[...] = dst
        dbg_ref[...] += dbg

    order = lambda t: nsteps - 1 - t
    rowv = pl.BlockSpec((r, GLA_V_W), lambda t: (order(t), 0))
    return pl.pallas_call(
        body, name=name, grid=(nsteps,),
        in_specs=_gla_in_specs(order) + [
            rowv, pl.BlockSpec((1, GLA_V_W, GLA_QK_W), lambda t: (order(t), 0, 0)),
            pl.BlockSpec((r, GLA_V_W), lambda t: (order(t), SWA_Q_W // GLA_V_W))],
        out_specs=[pl.BlockSpec((r, w_qkvg), lambda t: (order(t), 0)), pl.BlockSpec((r, GLA_QK_W), lambda t: (order(t), 0)),
                   pl.BlockSpec((1, GLA_QK_W), lambda t: (0, 0)), pl.BlockSpec((1, GLA_V_W), lambda t: (0, 0))],
        out_shape=[SDS((s, w_qkvg), F32), SDS((s, GLA_QK_W), F32), SDS((1, GLA_QK_W), F32), SDS((1, GLA_V_W), F32)],
        scratch_shapes=[pltpu.VMEM((GLA_V_W, GLA_QK_W), F32)],
        compiler_params=_params("arbitrary"),
    )(p, p, p, p, z, bg, gain, oraw, stsave, dy_all)


def _local_step(x, mem, target, small, big, on_grads):
    g1, gmix, gmem, g2, sqg, skg, sinks, rel_bias, wgu, bg, gla_gain, mqg, mkg = small
    w3_1, win_p, wkv, wout, gather_ffn2 = big
    wgu_pad = jnp.zeros((GLA_QK_W, GLA_QK_W), BF16).at[:GLA_RANK].set(wgu.astype(BF16))
    gain256 = jnp.tile(gla_gain, (1, GLA_HEADS))

    (x1, h), saved1 = _ffn_fwd(x, g1, w3_1, "ffn1", next_gain=gmix)
    w3_2 = gather_ffn2(x1)
    p = _matmul(h, win_p, tm=512, tn=IN_W_PAD, name="mix_in")
    hm = _rms_fwd(mem, gmem, tm=256, name="mem_rms")
    kv = _matmul(hm, wkv, tm=256, tn=512, name="mem_kv")
    p_glr = p[:, COL_GLR:]
    z = _matmul(p_glr, wgu_pad, tm=1024, tn=GLA_QK_W, name="gla_gate")
    y_swa = _swa_fwd(p, sqg, skg, sinks, rel_bias, name="swa_fwd")
    y_gla, oraw, stsave = _gla_fwd(p, z, bg, gain256, name="gla_fwd")
    y_mem = _mem_fwd(p, kv, mqg, mkg, tq=512, name="mem_fwd")
    x2 = _matmul([y_swa, y_gla, y_mem], wout, b_blocks=[0, 2, 3], tm=512, tn=1024, res=x1, name="mix_out")
    (dy, dyb, loss), saved2 = _ffn_fwd(x2, g2, w3_2, "ffn2", target=target)

    tiles = w3_2.shape[1] // FFN_TN
    dh2, dw3_2 = _ffn_bwd_part(dyb, w3_2, saved2, 0, tiles, None, name="ffn2_bwd")
    dx2, dx2b, dg2 = _rms_bwd(x2, g2, dh2, dy, tm=512, name="ffn2_drms")
    dx2b = on_grads("ffn2", [dw3_2.reshape(3, 2, -1, dw3_2.shape[-1])], dx2b)
    dy_all = _matmul(dx2b, wout, tb=True, tm=512, tn=1024, name="mix_dy")
    dwout = _matmul(jnp.concatenate([y_swa, y_gla, y_mem], axis=1), dx2b, ta=True, tm=512, tn=1024, out_dtype=BF16,
                    name="mix_dw_out")
    dq_swa, dkv_swa, dsqg, dskg, dsink, drb = _swa_bwd(p, sqg, skg, sinks, rel_bias, dy_all, name="swa_bwd")
    dqkvg, dz, dbg, dgain256 = _gla_bwd(p, z, bg, gain256, oraw, stsave, dy_all, name="gla_bwd")
    dmq, dkv_mem, dmqg, dmkg = _mem_bwd(p, kv, mqg, mkg, dy_all, tq=512, name="mem_bwd")
    dglr = _matmul(dz, wgu_pad, tb=True, tm=1024, tn=GLA_QK_W, name="gla_gate_dx")
    dwgu_pad = _matmul(p_glr, dz, ta=True, tm=GLA_QK_W, tn=GLA_QK_W, name="gla_gate_dw")
    dp = jnp.concatenate([dq_swa, dkv_swa, dqkvg, dmq, dglr], axis=1)
    dh = _matmul(dp, win_p, tb=True, tm=512, tn=1024, name="mix_dh")
    dwin_p = _matmul(h, dp, ta=True, tm=1024, tn=640, out_dtype=BF16, name="mix_dw_in")
    dx1, dx1b, dgmix = _rms_bwd(x1, gmix, dh, dx2, tm=512, name="mix_drms")
    dwkv = _matmul(hm, dkv_mem, ta=True, tm=512, tn=512, out_dtype=BF16, name="mem_dw_kv")
    dx1b = on_grads("mix", (dwin_p, dwkv, dwout), dx1b)
    dhm = _matmul(dkv_mem, wkv, tb=True, tm=256, tn=512, name="mem_dh")
    _, _, dgmem = _rms_bwd(mem, gmem, dhm, None, tm=256, name="mem_drms")
    dh1, dw3_1a = _ffn_bwd_part(dx1b, w3_1, saved1, 0, tiles // 2, None, name="ffn1_bwd_a")
    dgla_gain = dgain256.reshape(GLA_HEADS, GLA_DV).sum(axis=0, keepdims=True)
    dsmall = [dgmix, dgmem, dg2, dsqg, dskg, dsink[:, :SWA_HEADS], drb[:, :SWA_HEADS], dwgu_pad[:GLA_RANK], dbg,
              dgla_gain, dmqg, dmkg, loss]
    dh1, dgmem, dwgu_pad = on_grads("ffn1a", [dw3_1a[:, None]], (dh1, dgmem, dwgu_pad), small=dsmall)
    dh1, dw3_1b = _ffn_bwd_part(dx1b, w3_1, saved1, tiles // 2, tiles // 2, dh1, name="ffn1_bwd_b")
    dx, _, dg1 = _rms_bwd(x, g1, dh1, dx1, tm=512, name="ffn1_drms")
    on_grads("ffn1b", [dw3_1b[:, None]], None, small=[dg1])
    return dx


def _mesh_place():
    x, y, c = lax.axis_index("x"), lax.axis_index("y"), lax.axis_index("c")
    other_chips = [(1 - x, y), (x, 1 - y), (1 - x, 1 - y)]
    return x, y, c, other_chips


def _handshake(peers):
    barrier = pltpu.get_barrier_semaphore()
    for peer in peers:
        pl.semaphore_signal(barrier, inc=1, device_id=peer, device_id_type=MESH)
    pl.semaphore_wait(barrier, len(peers))


def _sequencer_call(body, operands, out_shapes, sems, *, name, collective_id):
    return pl.kernel(
        body, name=name, out_type=out_shapes, mesh=plsc.ScalarSubcoreMesh(axis_name="sequencer", num_cores=1),
        scratch_types=sems, compiler_params=pltpu.CompilerParams(collective_id=collective_id),
    )(*operands)


def _window(ref, kind, slot, shape):
    if kind == "row":
        rows = pl.ds(pl.multiple_of(slot * shape[-2], 8), shape[-2])
        return ref.at[(slice(None),) * (len(shape) - 2) + (rows,)]
    return ref.at[slot]


def _gathered(shape, kind):
    if kind == "row":
        return tuple(shape[:-2]) + (N_DEV * shape[-2], shape[-1])
    return (N_DEV,) + tuple(shape)


def _half(view, hf):
    if len(view.shape) == 4:
        return view.at[:, hf]
    n = view.shape[0] // 2
    return view.at[pl.ds(hf * n, n)]


def _all_gather(shards, kinds, *, name, collective_id):
    nt = len(shards)

    def body(*refs):
        x_refs, o_refs = refs[:nt], refs[nt:2 * nt]
        send_sems, recv_sems, local_sems = refs[2 * nt:]
        x, y, c, _ = _mesh_place()
        me, sibling, xn, yn, diag = (x, y, c), (x, y, 1 - c), (1 - x, y, c), (x, 1 - y, c), (1 - x, 1 - y, c)
        _handshake([sibling, xn, yn])

        def win(t, block):
            bx, by, bc = block
            return _window(o_refs[t], kinds[t], 4 * bx + 2 * by + bc, shards[t].shape)

        def copy(k, t, src, dst, to):
            return pltpu.make_async_remote_copy(src_ref=src, dst_ref=dst, send_sem=send_sems.at[k, t],
                                                recv_sem=recv_sems.at[k, t], device_id=to, device_id_type=MESH)

        def piece(k, t, block, hf, to, from_shard=False):
            dst = _half(win(t, block), hf)
            return copy(k, t, _half(x_refs[t], hf) if from_shard else dst, dst, to)

        mine = [pltpu.make_async_copy(x_refs[t], win(t, me), local_sems.at[t]) for t in range(nt)]
        sent = []

        def start(cp):
            cp.start()
            sent.append(cp)

        for cp in mine:
            cp.start()
        for t in range(nt):
            start(copy(0, t, x_refs[t], win(t, me), sibling))
        for hf_x, hf_y in ((0, 1), (1, 0)):
            for t in range(nt):
                start(piece(1 + hf_x, t, me, hf_x, xn, True))
                start(piece(3 + hf_y, t, me, hf_y, yn, True))
        for k, block, hf, onward, k_sib in ((1, xn, 0, (5, yn), 7), (4, yn, 1, (6, xn), 10), (2, xn, 1, None, 8),
                                           (3, yn, 0, None, 9), (5, diag, 0, None, 11), (6, diag, 1, None, 12)):
            for t in range(nt):
                piece(k, t, block, hf, me).wait_recv()
                if onward is not None:
                    start(piece(onward[0], t, block, hf, onward[1]))
                start(piece(k_sib, t, block, hf, sibling))
        for t in range(nt):
            copy(0, t, x_refs[t], win(t, sibling), me).wait_recv()
        for k_sib, block, hf in ((7, xn, 0), (10, yn, 1), (8, xn, 1), (9, yn, 0), (11, diag, 0), (12, diag, 1)):
            for t in range(nt):
                bx, by, _ = block
                piece(k_sib, t, (bx, by, 1 - c), hf, me).wait_recv()
        for cp in sent:
            cp.wait_send()
        for cp in mine:
            cp.wait()

    return _sequencer_call(
        body, shards, [SDS(_gathered(s.shape, k), s.dtype) for s, k in zip(shards, kinds)],
        [pltpu.SemaphoreType.DMA((13, nt)), pltpu.SemaphoreType.DMA((13, nt)), pltpu.SemaphoreType.DMA((nt,))],
        name=name, collective_id=collective_id)


def _part_shape(shape, kind):
    if kind == "row":
        return tuple(shape[:-2]) + (shape[-2] // N_DEV, shape[-1])
    return tuple(shape[2:])


def _pair_exchange(grads, kinds, *, name, collective_id):
    nt = len(grads)
    part = [_part_shape(g.shape, k) for g, k in zip(grads, kinds)]

    def body(*refs):
        g_refs, o_refs = refs[:nt], refs[nt:2 * nt]
        send_sems, recv_sems = refs[2 * nt:]
        x, y, c, _ = _mesh_place()
        _handshake([(x, y, 1 - c)])
        copies = []
        for t in range(nt):
            for xy in range(4):
                src = g_refs[t].at[1 - c, xy] if kinds[t] == "stack" else _window(g_refs[t], kinds[t], 2 * xy + 1 - c, part[t])
                copies.append(pltpu.make_async_remote_copy(
                    src_ref=src, dst_ref=o_refs[t].at[xy], send_sem=send_sems.at[xy, t], recv_sem=recv_sems.at[xy, t],
                    device_id=(x, y, 1 - c), device_id_type=MESH))
        for cp in copies:
            cp.start()
        for cp in copies:
            cp.wait()

    return _sequencer_call(
        body, grads, [SDS((4,) + p, g.dtype) for p, g in zip(part, grads)],
        [pltpu.SemaphoreType.DMA((4, nt)), pltpu.SemaphoreType.DMA((4, nt))], name=name, collective_id=collective_id)


def _chip_exchange(parts, small, *, name, collective_id):
    nt = len(parts)
    if small is None:
        def body_plain(*refs):
            s_refs, o_refs = refs[:nt], refs[nt:2 * nt]
            send_sems, recv_sems = refs[2 * nt:]
            x, y, c, chips = _mesh_place()
            _handshake([(*chip, c) for chip in chips])
            copies = [pltpu.make_async_remote_copy(
                src_ref=s_refs[t].at[2 * chip[0] + chip[1]], dst_ref=o_refs[t].at[j],
                send_sem=send_sems.at[j, t], recv_sem=recv_sems.at[j, t], device_id=(*chip, c), device_id_type=MESH)
                for j, chip in enumerate(chips) for t in range(nt)]
            for cp in copies:
                cp.start()
            for cp in copies:
                cp.wait()

        return _sequencer_call(
            body_plain, parts, [SDS((3,) + s.shape[1:], s.dtype) for s in parts],
            [pltpu.SemaphoreType.DMA((3, nt)), pltpu.SemaphoreType.DMA((3, nt))], name=name, collective_id=collective_id)

    def body(*refs):
        s_refs, small_ref = refs[:nt], refs[nt]
        o_refs, small_all = refs[nt + 1:2 * nt + 1], refs[2 * nt + 1]
        send_sems, recv_sems, small_send, small_recv, local_sem = refs[2 * nt + 2:]
        x, y, c, chips = _mesh_place()
        _handshake([(px, py, pc) for px in (x, 1 - x) for py in (y, 1 - y) for pc in (c, 1 - c)][1:])

        def copy(j, t, chip):
            return pltpu.make_async_remote_copy(
                src_ref=s_refs[t].at[2 * chip[0] + chip[1]], dst_ref=o_refs[t].at[j],
                send_sem=send_sems.at[j, t], recv_sem=recv_sems.at[j, t], device_id=(*chip, c), device_id_type=MESH)

        flips = [(fx, fy, fc) for fx in (0, 1) for fy in (0, 1) for fc in (0, 1)][1:]

        def small_copy(k):
            fx, fy, fc = flips[k]
            to = (x ^ fx if fx else x, y ^ fy if fy else y, c ^ fc if fc else c)
            rows = small_all.at[4 * x + 2 * y + c]
            return pltpu.make_async_remote_copy(
                src_ref=small_ref, dst_ref=rows, send_sem=small_send.at[k], recv_sem=small_recv.at[k],
                device_id=to, device_id_type=MESH)

        own = pltpu.make_async_copy(small_ref, small_all.at[4 * x + 2 * y + c], local_sem)
        own.start()
        copies = [copy(j, t, chip) for j, chip in enumerate(chips) for t in range(nt)]
        smalls = [small_copy(k) for k in range(7)]
        for cp in smalls + copies:
            cp.start()
        for cp in smalls + copies:
            cp.wait()
        own.wait()

    return _sequencer_call(
        body, list(parts) + [small],
        [SDS((3,) + s.shape[1:], s.dtype) for s in parts] + [SDS((N_DEV,) + small.shape, small.dtype)],
        [pltpu.SemaphoreType.DMA((3, nt)), pltpu.SemaphoreType.DMA((3, nt)),
         pltpu.SemaphoreType.DMA((7,)), pltpu.SemaphoreType.DMA((7,)), pltpu.SemaphoreType.DMA],
        name=name, collective_id=collective_id)


def _pair_sum(grad, theirs, kind, c, *, name):
    if kind == "row":
        r, l = theirs.shape[-2:]
        n = theirs.size // (4 * r * l)
        grad, theirs = grad.reshape(n, N_DEV * r, l), theirs.reshape(4, n, r, l)
        mine_spec = pl.BlockSpec((n, r, l), lambda xy, c_ref: (0, 2 * xy + c_ref[0], 0))
    else:
        r, l = theirs.shape[-2:]
        n = theirs.size // (4 * r * l)
        theirs = theirs.reshape(4, n, r, l)
        grad = grad.reshape(2, 4, n, r, l)
        mine_spec = pl.BlockSpec((None, None, n, r, l), lambda xy, c_ref: (c_ref[0], xy, 0, 0, 0))

    def body(c_ref, a_ref, b_ref, o_ref):
        o_ref[...] = (a_ref[...].astype(F32) + b_ref[...].astype(F32)).astype(BF16)

    part = pl.BlockSpec((None, n, r, l), lambda xy, c_ref: (xy, 0, 0, 0))
    return pl.pallas_call(
        body, name=name,
        grid_spec=pltpu.PrefetchScalarGridSpec(num_scalar_prefetch=1, grid=(4,), in_specs=[mine_spec, part], out_specs=part),
        out_shape=SDS((4, n, r, l), BF16), compiler_params=_params("parallel"),
    )(c, grad, theirs)


def _adamw(w, g, m, v):
    m = ADAM_B1 * m + (1.0 - ADAM_B1) * g
    v = ADAM_B2 * v + (1.0 - ADAM_B2) * jnp.square(g)
    m_hat = m / (1.0 - ADAM_B1 ** ADAM_STEP)
    v_hat = v / (1.0 - ADAM_B2 ** ADAM_STEP)
    delta = -ADAM_LR * (m_hat / (jnp.sqrt(v_hat) + ADAM_EPS) + ADAM_WD * w)
    return delta, m, v


def _adam_big(owns, others, mat, xy, w, m, v, *, tr, name):
    nw, r, l = w.shape
    lp = owns[0].shape[-1]
    nq = len(owns)
    assert nq in (1, nw)

    def body(xy_ref, *refs):
        own_refs, oth_refs = refs[:nq], refs[nq:2 * nq]
        w_ref, m_ref, v_ref, g_out, d_out, m_out, v_out = refs[2 * nq:]
        g = None
        for q in range(nq):
            gq = own_refs[q][0, 0].astype(F32)
            for j in range(3):
                gq = gq + oth_refs[q][j, 0].astype(F32)
            g = gq if g is None else jnp.where(pl.program_id(0) == q, gq, g)
        g = g[:, :l]
        delta, m_new, v_new = _adamw(w_ref[0], g, m_ref[0], v_ref[0])
        g_out[0] = g
        d_out[0] = delta
        m_out[0] = m_new
        v_out[0] = v_new

    def at(p):
        return mat * nw + p if nq == 1 else mat

    blk = pl.BlockSpec((1, tr, l), lambda p, i, xy_ref: (p, i, 0))
    return pl.pallas_call(
        body, name=name,
        grid_spec=pltpu.PrefetchScalarGridSpec(
            num_scalar_prefetch=1, grid=(nw, r // tr),
            in_specs=[pl.BlockSpec((1, 1, tr, lp), lambda p, i, xy_ref: (xy_ref[0], at(p), i, 0))] * nq
            + [pl.BlockSpec((3, 1, tr, lp), lambda p, i, xy_ref: (0, at(p), i, 0))] * nq + [blk, blk, blk],
            out_specs=[blk, blk, blk, blk]),
        out_shape=[SDS(w.shape, F32)] * 4, compiler_params=_params("parallel", "parallel"),
    )(xy, *owns, *others, w, m, v)


def _small_layout(shapes):
    out, at = [], 0
    for r, c in shapes:
        rows = c // 128 if (r == 1 and c > 128) else r
        out.append((at, rows))
        at += -(-rows // 8) * 8
    return out, at


def _pack_small(parts, *, name):
    shapes = [a.shape for a in parts]
    layout, total = _small_layout(shapes)

    def body(*refs):
        o_ref = refs[-1]
        o_ref[...] = jnp.zeros_like(o_ref)
        for x_ref, (r, c), (at, rows) in zip(refs, shapes, layout):
            if r == 1 and c > 128:
                for k in range(rows):
                    o_ref[at + k:at + k + 1, :] = x_ref[:, k * 128:(k + 1) * 128]
            else:
                o_ref[at:at + r, 0:c] = x_ref[...]

    return pl.pallas_call(body, name=name, out_shape=SDS((total, 128), F32))(*parts)


def _adam_small(g_all, ws, ms, vs, *, name):
    n = len(ws)
    shapes = [w.shape for w in ws]
    layout, _ = _small_layout(shapes)

    def body(g_ref, *refs):
        w_refs, m_refs, v_refs, outs = refs[:n], refs[n:2 * n], refs[2 * n:3 * n], refs[3 * n:]
        g_sum = g_ref[0]
        for k in range(1, N_DEV):
            g_sum = g_sum + g_ref[k]
        for i, ((r, c), (at, rows)) in enumerate(zip(shapes, layout)):
            if r == 1 and c > 128:
                g = jnp.concatenate([g_sum[at + k:at + k + 1, :] for k in range(rows)], axis=1)
            else:
                g = g_sum[at:at + r, 0:c]
            delta, m_new, v_new = _adamw(w_refs[i][...], g, m_refs[i][...], v_refs[i][...])
            for q, val in enumerate((g, delta, m_new, v_new)):
                outs[4 * i + q][...] = val

    flat = pl.pallas_call(body, name=name, out_shape=[SDS(s, F32) for s in shapes for _ in range(4)])(g_all, *ws, *ms, *vs)
    return [flat[4 * i:4 * i + 4] for i in range(n)]


def kernel(x, mem, ffn1_norm, ffn1_w_gate, ffn1_w_up, ffn1_w_down, mix_norm, mem_norm, w_in, w_mem_kv, swa_q_norm, swa_k_norm, swa_sinks, rel_bias, gla_w_gate_up, gla_b_gate, gla_out_norm, mem_q_norm, mem_k_norm, w_out, ffn2_norm, ffn2_w_gate, ffn2_w_up, ffn2_w_down, loss_target, m_ffn1_norm, m_ffn1_w_gate, m_ffn1_w_up, m_ffn1_w_down, m_mix_norm, m_mem_norm, m_w_in, m_w_mem_kv, m_swa_q_norm, m_swa_k_norm, m_swa_sinks, m_rel_bias, m_gla_w_gate_up, m_gla_b_gate, m_gla_out_norm, m_mem_q_norm, m_mem_k_norm, m_w_out, m_ffn2_norm, m_ffn2_w_gate, m_ffn2_w_up, m_ffn2_w_down, v_ffn1_norm, v_ffn1_w_gate, v_ffn1_w_up, v_ffn1_w_down, v_mix_norm, v_mem_norm, v_w_in, v_w_mem_kv, v_swa_q_norm, v_swa_k_norm, v_swa_sinks, v_rel_bias, v_gla_w_gate_up, v_gla_b_gate, v_gla_out_norm, v_mem_q_norm, v_mem_k_norm, v_w_out, v_ffn2_norm, v_ffn2_w_gate, v_ffn2_w_up, v_ffn2_w_down):
    xi, yi, ci = lax.axis_index("x"), lax.axis_index("y"), lax.axis_index("c")
    c_arr = jnp.reshape(ci, (1,)).astype(jnp.int32)
    xy_arr = jnp.reshape(2 * xi + yi, (1,)).astype(jnp.int32)
    d = x.shape[-1]

    half_h = ffn1_w_gate.shape[-1] // 2

    def gather_ffn(wg_s, wu_s, wd_s, name, collective_id, after):
        w3_s = jnp.concatenate([wg_s.transpose(0, 2, 1), wu_s.transpose(0, 2, 1), wd_s], axis=0)
        w3_s = jnp.pad(w3_s.reshape(3, 2, half_h, d), ((0, 0), (0, 0), (0, FFN_HALF_PAD - half_h), (0, 0))).astype(BF16)
        if after is not None:
            w3_s, _ = lax.optimization_barrier((w3_s, after))
        return _all_gather([w3_s], ["row"], name=name, collective_id=collective_id)[0].reshape(3, -1, d)

    w3_1 = gather_ffn(ffn1_w_gate, ffn1_w_up, ffn1_w_down, "gather_ffn1", 0, None)
    mix_s = lax.optimization_barrier((w_in[0].astype(BF16), w_mem_kv[0].astype(BF16), w_out[0].astype(BF16), w3_1))[:3]
    win_all, wkv, wout = _all_gather(list(mix_s), ["stack", "row", "row"], name="gather_mix", collective_id=1)

    def gather_ffn2(x1):
        return gather_ffn(ffn2_w_gate, ffn2_w_up, ffn2_w_down, "gather_ffn2", 2, (wout, x1))

    win_p = _pack_win(win_all, tr=256, name="pack_w_in")

    small_w = [ffn1_norm, mix_norm, mem_norm, ffn2_norm, swa_q_norm, swa_k_norm, swa_sinks[0], rel_bias,
               gla_w_gate_up[0], gla_b_gate, gla_out_norm, mem_q_norm, mem_k_norm]
    collective_ids = {"ffn2": (3, 4), "mix": (5, 6), "ffn1a": (7, 8), "ffn1b": (9, 10)}
    reduced, small_box = {}, {}

    def on_grads(group, grads, carry, small=None):
        if group == "mix":
            dwin_p, dwkv, dwout = grads
            grads = [_unpack_win(dwin_p, tr=256, name="unpack_dw_in"), dwkv, dwout]
            kinds = ["stack", "row", "row"]
        else:
            kinds = ["row"]
        if reduced:
            earlier = list(reduced.values())[-1][1]
            *grads, _ = lax.optimization_barrier((*grads, earlier[0]))
        id_pair, id_chip = collective_ids[group]
        from_sibling = _pair_exchange(grads, kinds, name=f"pair_exchange_{group}", collective_id=id_pair)
        chip_sums = [_pair_sum(g, theirs, k, c_arr, name=f"pair_sum_{group}_{t}")
                     for t, (g, theirs, k) in enumerate(zip(grads, from_sibling, kinds))]
        if carry is not None:
            *chip_sums, carry = lax.optimization_barrier((*chip_sums, carry))
        if small is None:
            from_chips = _chip_exchange(chip_sums, None, name=f"chip_exchange_{group}", collective_id=id_chip)
        else:
            packed = _pack_small(small, name=f"pack_small_{group}")
            *from_chips, small_all = _chip_exchange(chip_sums, packed, name=f"chip_exchange_{group}",
                                                    collective_id=id_chip)
            small_box[group] = small_all
        reduced[group] = (chip_sums, from_chips)
        return carry

    grad_x = _local_step(x[0], mem[0], loss_target[0], small_w, (w3_1, win_p, wkv, wout, gather_ffn2), on_grads)

    big_w = {"ffn1_w_gate": ("ffn1", 0, 0, True, ffn1_w_gate, m_ffn1_w_gate, v_ffn1_w_gate),
             "ffn1_w_up": ("ffn1", 0, 1, True, ffn1_w_up, m_ffn1_w_up, v_ffn1_w_up),
             "ffn1_w_down": ("ffn1", 0, 2, False, ffn1_w_down, m_ffn1_w_down, v_ffn1_w_down),
             "w_in": ("mix", 0, 0, False, w_in, m_w_in, v_w_in),
             "w_mem_kv": ("mix", 1, 0, False, w_mem_kv, m_w_mem_kv, v_w_mem_kv),
             "w_out": ("mix", 2, 0, False, w_out, m_w_out, v_w_out),
             "ffn2_w_gate": ("ffn2", 0, 0, True, ffn2_w_gate, m_ffn2_w_gate, v_ffn2_w_gate),
             "ffn2_w_up": ("ffn2", 0, 1, True, ffn2_w_up, m_ffn2_w_up, v_ffn2_w_up),
             "ffn2_w_down": ("ffn2", 0, 2, False, ffn2_w_down, m_ffn2_w_down, v_ffn2_w_down)}
    res = {}
    for nm, (group, t, mat, transposed, w, m, v) in big_w.items():
        shape = w.shape
        if transposed:
            w, m, v = (a.transpose(0, 2, 1) for a in (w, m, v))
        if group != "mix":
            w, m, v = (a.reshape(2, half_h, d) for a in (w, m, v))
        r = w.shape[1]
        tr = 256 if r % 256 == 0 else r
        halves = ["ffn1a", "ffn1b"] if group == "ffn1" else [group]
        out = _adam_big([reduced[k][0][t] for k in halves], [reduced[k][1][t] for k in halves], mat, xy_arr, w, m, v,
                        tr=tr, name=f"adam_{nm}")
        if transposed:
            out = [a.reshape(1, -1, d).transpose(0, 2, 1) for a in out]
        res[nm] = [a.reshape(shape) for a in out]
    small_names = ["ffn1_norm", "mix_norm", "mem_norm", "ffn2_norm", "swa_q_norm", "swa_k_norm", "swa_sinks", "rel_bias",
                   "gla_w_gate_up", "gla_b_gate", "gla_out_norm", "mem_q_norm", "mem_k_norm"]
    small_m = [m_ffn1_norm, m_mix_norm, m_mem_norm, m_ffn2_norm, m_swa_q_norm, m_swa_k_norm, m_swa_sinks, m_rel_bias,
               m_gla_w_gate_up, m_gla_b_gate, m_gla_out_norm, m_mem_q_norm, m_mem_k_norm]
    small_v = [v_ffn1_norm, v_mix_norm, v_mem_norm, v_ffn2_norm, v_swa_q_norm, v_swa_k_norm, v_swa_sinks, v_rel_bias,
               v_gla_w_gate_up, v_gla_b_gate, v_gla_out_norm, v_mem_q_norm, v_mem_k_norm]
    small_full = [ffn1_norm, mix_norm, mem_norm, ffn2_norm, swa_q_norm, swa_k_norm, swa_sinks, rel_bias,
                  gla_w_gate_up, gla_b_gate, gla_out_norm, mem_q_norm, mem_k_norm]
    zero = jnp.zeros((1, 1), F32)
    two_d = lambda a: a.reshape(a.shape[-2:])
    for group, sel in (("ffn1a", slice(1, None)), ("ffn1b", slice(0, 1))):
        extra = [zero] if group == "ffn1a" else []
        ws, ms, vs = ([two_d(a) for a in arrs[sel]] + extra for arrs in (small_full, small_m, small_v))
        updated = _adam_small(small_box[group], ws, ms, vs, name=f"adam_small_{group}")
        for nm, full, out in zip(small_names[sel], small_full[sel], updated):
            res[nm] = [a.reshape(full.shape) for a in out]
        if extra:
            loss = updated[-1][0].reshape(())

    order = ["ffn1_norm", "ffn1_w_gate", "ffn1_w_up", "ffn1_w_down", "mix_norm", "mem_norm", "w_in", "w_mem_kv",
             "swa_q_norm", "swa_k_norm", "swa_sinks", "rel_bias", "gla_w_gate_up", "gla_b_gate", "gla_out_norm",
             "mem_q_norm", "mem_k_norm", "w_out", "ffn2_norm", "ffn2_w_gate", "ffn2_w_up", "ffn2_w_down"]
    outs = [loss, grad_x[None]]
    for q in range(4):
        outs += [res[nm][q] for nm in order]
    return tuple(outs)
```

```python
import functools
import math

import numpy as np
import jax
import jax.numpy as jnp
from jax import lax
from jax.experimental import pallas as pl
from jax.experimental.pallas import tpu as pltpu
from jax.experimental.pallas import tpu_sc as plsc

F32 = jnp.float32
BF16 = jnp.bfloat16
SDS = jax.ShapeDtypeStruct

EPS = 1e-6
HEAD_DIM = 64
SWA_HEADS = 8
SWA_KV_HEADS = 2
SWA_GROUP = SWA_HEADS // SWA_KV_HEADS
BLOCK = 128
N_BUCKETS = 32
MAX_DISTANCE = 128
GLA_HEADS = 4
GLA_DK = 32
GLA_DV = 64
GLA_RANK = 16
GLA_TAU = 16.0
GLA_CHUNK = 32
MEM_HEADS = 4
SWA_Q_W = SWA_HEADS * HEAD_DIM
SWA_KV_W = SWA_KV_HEADS * HEAD_DIM
GLA_QK_W = GLA_HEADS * GLA_DK
GLA_V_W = GLA_HEADS * GLA_DV
MEM_Q_W = MEM_HEADS * HEAD_DIM
IN_W = 1808
IN_W_PAD = 1920
COL_SQ, COL_SKV, COL_GQ, COL_GK, COL_GV, COL_GG, COL_MQ, COL_GLR = 0, 512, 768, 896, 1024, 1280, 1536, 1792

ADAM_LR = 0.001
ADAM_B1 = 0.9
ADAM_B2 = 0.999
ADAM_EPS = 1e-08
ADAM_WD = 0.01
ADAM_STEP = 10

N_DEV = 8
VMEM_LIMIT_BYTES = 56 * 1024 * 1024
MESH = pl.DeviceIdType.MESH


def _params(*sem):
    return pltpu.CompilerParams(dimension_semantics=sem or None, vmem_limit_bytes=VMEM_LIMIT_BYTES)


def _dot(a, b, ta, tb, precision=None):
    dims = (((0 if ta else 1,), (1 if tb else 0,)), ((), ()))
    return lax.dot_general(a, b, dims, preferred_element_type=F32, precision=precision)


def _mm_raw(a, b, ta=False, tb=False):
    return _dot(a.astype(BF16), b.astype(BF16), ta, tb)


def _mmf_raw(a, b, ta=False, tb=False):
    return _dot(a, b, ta, tb, lax.Precision.HIGHEST)


def _make_mm(raw):
    @functools.partial(jax.custom_vjp, nondiff_argnums=(2, 3))
    def mm(a, b, ta=False, tb=False):
        return raw(a, b, ta, tb)

    def fwd(a, b, ta, tb):
        return raw(a, b, ta, tb), (a, b)

    def bwd(ta, tb, res, g):
        a, b = res
        da = raw(b, g, tb, True) if ta else raw(g, b, False, not tb)
        db = raw(g, a, True, ta) if tb else raw(a, g, not ta, False)
        return da, db

    mm.defvjp(fwd, bwd)
    return mm


_mm = _make_mm(_mm_raw)
_mmf = _make_mm(_mmf_raw)


def _mm3(a, b, ta=False, tb=False):
    a_hi, b_hi = a.astype(BF16).astype(F32), b.astype(BF16).astype(F32)
    return _mm(a_hi, b_hi, ta, tb) + _mm(a_hi, b - b_hi, ta, tb) + _mm(a - a_hi, b_hi, ta, tb)


def _rms(x, g):
    return x * lax.rsqrt(jnp.mean(x * x, axis=-1, keepdims=True) + EPS) * g


def _silu_mul(g, u):
    return jax.nn.silu(g) * u


def _log_sigmoid(z):
    return jnp.minimum(z, 0.0) - jnp.log(1.0 + jnp.exp(-jnp.abs(z)))


def _matmul(a_list, b, *, ta=False, tb=False, tm, tn, b_blocks=None, res=None, scale=1.0, out_dtype=F32, name):
    if not isinstance(a_list, (list, tuple)):
        a_list = [a_list]
    n_a = len(a_list)
    m = a_list[0].shape[1] if ta else a_list[0].shape[0]
    ks = [a.shape[0] if ta else a.shape[1] for a in a_list]
    n = b.shape[0] if tb else b.shape[1]
    if b_blocks is None:
        assert n_a == 1
        b_blocks = [0]
    tm, tn = min(tm, m), min(tn, n)
    assert m % tm == 0 and n % tn == 0, (m, n, tm, tn)

    def body(*refs):
        a_refs, b_refs = refs[:n_a], refs[n_a:2 * n_a]
        r_ref = refs[2 * n_a] if res is not None else None
        o_ref = refs[-1]
        acc = _mm_raw(a_refs[0][...], b_refs[0][...], ta, tb)
        for k in range(1, n_a):
            acc = acc + _mm_raw(a_refs[k][...], b_refs[k][...], ta, tb)
        if scale != 1.0:
            acc = acc * scale
        if r_ref is not None:
            acc = r_ref[...] + acc
        o_ref[...] = acc.astype(out_dtype)

    in_specs = []
    for k in ks:
        in_specs.append(pl.BlockSpec((k, tm), lambda i, j: (0, i)) if ta else pl.BlockSpec((tm, k), lambda i, j: (i, 0)))
    for k, blk in zip(ks, b_blocks):
        if tb:
            in_specs.append(pl.BlockSpec((tn, k), functools.partial(lambda i, j, blk: (j, blk), blk=blk)))
        else:
            in_specs.append(pl.BlockSpec((k, tn), functools.partial(lambda i, j, blk: (blk, j), blk=blk)))
    args = list(a_list) + [b] * n_a
    if res is not None:
        in_specs.append(pl.BlockSpec((tm, tn), lambda i, j: (i, j)))
        args.append(res)
    return pl.pallas_call(
        body, name=name, grid=(m // tm, n // tn), in_specs=in_specs,
        out_specs=pl.BlockSpec((tm, tn), lambda i, j: (i, j)), out_shape=SDS((m, n), out_dtype),
        compiler_params=_params("parallel", "parallel"),
    )(*args)


def _win_pieces(w):
    glr_lo, glr_hi = COL_MQ, COL_MQ + GLA_RANK
    out = []
    for j in range(N_DEV):
        for lo, hi, shift in ((0, glr_lo, 0), (glr_lo, glr_hi, COL_GLR - glr_lo), (glr_hi, IN_W, COL_MQ - glr_hi)):
            s, e = max(j * w, lo), min((j + 1) * w, hi)
            if s < e:
                out.append((j, s - j * w, e - j * w, s + shift))
    return out


def _pack_win(win_all, *, tr, name):
    _, d, w = win_all.shape

    def body(i_ref, o_ref):
        for j, a, b, dst in _win_pieces(w):
            o_ref[:, dst:dst + b - a] = i_ref[j][:, a:b]
        o_ref[:, IN_W:] = jnp.zeros((tr, IN_W_PAD - IN_W), o_ref.dtype)

    return pl.pallas_call(
        body, name=name, grid=(d // tr,), in_specs=[pl.BlockSpec((N_DEV, tr, w), lambda i: (0, i, 0))],
        out_specs=pl.BlockSpec((tr, IN_W_PAD), lambda i: (i, 0)), out_shape=SDS((d, IN_W_PAD), win_all.dtype),
        compiler_params=_params("parallel"),
    )(win_all)


def _unpack_win(dwin_p, *, tr, name):
    d = dwin_p.shape[0]
    w = IN_W // N_DEV

    def body(i_ref, o_ref):
        for j, a, b, src in _win_pieces(w):
            o_ref[j % 2, j // 2, :, a:b] = i_ref[:, src:src + b - a]

    return pl.pallas_call(
        body, name=name, grid=(d // tr,), in_specs=[pl.BlockSpec((tr, IN_W_PAD), lambda i: (i, 0))],
        out_specs=pl.BlockSpec((2, 4, tr, w), lambda i: (0, 0, i, 0)), out_shape=SDS((2, 4, d, w), dwin_p.dtype),
        compiler_params=_params("parallel"),
    )(dwin_p)


def _rms_fwd(x, g, *, tm, name):
    s, d = x.shape

    def body(x_ref, g_ref, h_ref):
        h_ref[...] = _rms(x_ref[...], g_ref[...]).astype(BF16)

    return pl.pallas_call(
        body, name=name, grid=(s // tm,),
        in_specs=[pl.BlockSpec((tm, d), lambda i: (i, 0)), pl.BlockSpec((1, d), lambda i: (0, 0))],
        out_specs=pl.BlockSpec((tm, d), lambda i: (i, 0)), out_shape=SDS((s, d), BF16),
        compiler_params=_params("parallel"),
    )(x, g)


def _rms_bwd(x, g, dh, dres, *, tm, name):
    s, d = x.shape
    want_dx = dres is not None

    def body(*refs):
        if want_dx:
            x_ref, g_ref, dh_ref, dres_ref, dx_ref, dxb_ref, dg_ref = refs
        else:
            x_ref, g_ref, dh_ref, dg_ref = refs
        _, vjp = jax.vjp(_rms, x_ref[...], g_ref[...])
        dx, dg = vjp(dh_ref[...])
        if want_dx:
            dx = dres_ref[...] + dx
            dx_ref[...] = dx
            dxb_ref[...] = dx.astype(BF16)

        @pl.when(pl.program_id(0) == 0)
        def _():
            dg_ref[...] = jnp.zeros_like(dg_ref)

        dg_ref[...] += dg

    row = pl.BlockSpec((tm, d), lambda i: (i, 0))
    vec = pl.BlockSpec((1, d), lambda i: (0, 0))
    if want_dx:
        return pl.pallas_call(
            body, name=name, grid=(s // tm,), in_specs=[row, vec, row, row], out_specs=[row, row, vec],
            out_shape=[SDS((s, d), F32), SDS((s, d), BF16), SDS((1, d), F32)], compiler_params=_params("arbitrary"),
        )(x, g, dh, dres)
    return None, None, pl.pallas_call(
        body, name=name, grid=(s // tm,), in_specs=[row, vec, row], out_specs=vec,
        out_shape=SDS((1, d), F32), compiler_params=_params("arbitrary"),
    )(x, g, dh)


FFN_TN = 256
FFN_TN_FWD = 512
FFN_HALF_PAD = 192


def _ffn_fwd(x, gain, w3, tag, *, tm=1024, next_gain=None, target=None, start=None, partial=False):
    s, d = x.shape
    f = w3.shape[1]
    tn = FFN_TN_FWD
    nj = f // tn
    tm = min(tm, s)
    n_extra = (next_gain is not None) + (target is not None) + 2 * (start is not None)

    def body(*refs):
        x_ref, gain_ref, wg_ref, wu_ref, wd_ref = refs[:5]
        extra, outs = refs[5:5 + n_extra], refs[5 + n_extra:-1]
        acc_s = refs[-1]
        g_ref, u_ref = outs[-2:]
        h_ref = extra[-2] if start is not None else outs[-3]
        i, j = pl.program_id(0), pl.program_id(1)

        @pl.when(j == 0)
        def _():
            if start is None:
                h_ref[...] = _rms(x_ref[...], gain_ref[...]).astype(BF16)
                acc_s[...] = jnp.zeros_like(acc_s)
            else:
                acc_s[...] = extra[-1][...]

        hv = h_ref[...]
        g = _mm_raw(hv, wg_ref[...], False, True)
        u = _mm_raw(hv, wu_ref[...], False, True)
        g_ref[...] = g.astype(BF16)
        u_ref[...] = u.astype(BF16)
        acc_s[...] += _mm_raw(_silu_mul(g, u), wd_ref[...])

        @pl.when(j == nj - 1)
        def _():
            y = acc_s[...] if partial else x_ref[...] + 0.5 * acc_s[...]
            if target is None:
                outs[0][...] = y
                if next_gain is not None:
                    outs[1][...] = _rms(y, extra[0][...]).astype(BF16)
            else:
                dy_ref, dyb_ref, loss_ref = outs[:3]
                diff = y - extra[0][...]
                dy_ref[...] = diff * (1.0 / d)
                dyb_ref[...] = (diff * (1.0 / d)).astype(BF16)
                part = 0.5 * jnp.sum(jnp.mean(diff * diff, axis=-1, keepdims=True), axis=0, keepdims=True)

                @pl.when(i == 0)
                def _():
                    loss_ref[...] = part

                @pl.when(i > 0)
                def _():
                    loss_ref[...] += part

    row = pl.BlockSpec((tm, d), lambda i, j: (i, 0))
    vec = pl.BlockSpec((1, d), lambda i, j: (0, 0))
    tile = pl.BlockSpec((tm, tn), lambda i, j: (i, j))
    in_specs = [row, vec] + [pl.BlockSpec((None, tn, d), functools.partial(lambda i, j, k: (k, j, 0), k=k)) for k in range(3)]
    args = [x, gain, w3, w3, w3]
    if target is None:
        out_specs, out_shape = [row], [SDS((s, d), F32)]
        if next_gain is not None:
            in_specs.append(vec)
            args.append(next_gain)
            out_specs.append(row)
            out_shape.append(SDS((s, d), BF16))
    else:
        in_specs.append(row)
        args.append(target)
        out_specs = [row, row, pl.BlockSpec((1, 1), lambda i, j: (0, 0))]
        out_shape = [SDS((s, d), F32), SDS((s, d), BF16), SDS((1, 1), F32)]
    if start is None:
        out_specs.append(row)
        out_shape.append(SDS((s, d), BF16))
    else:
        in_specs += [row, row]
        args += list(start)
    *head, g, u = pl.pallas_call(
        body, name=f"{tag}_fwd", grid=(s // tm, nj), in_specs=in_specs,
        out_specs=out_specs + [tile, tile],
        out_shape=out_shape + [SDS((s, f), BF16), SDS((s, f), BF16)],
        scratch_shapes=[pltpu.VMEM((tm, d), F32)],
        compiler_params=_params("arbitrary", "arbitrary"),
    )(*args)
    if start is None:
        *head, h = head
    else:
        h = start[0]
    return head, (h, g, u)


def _ffn_bwd_part(dyb, w3, saved, first, count, dh_init, *, name):
    h, g, u = saved
    s, d = h.shape
    tn = FFN_TN

    def body(*refs):
        if dh_init is None:
            dy_ref, h_ref, wd_ref, wg_ref, wu_ref, g_ref, u_ref, dh_ref, dw3_ref, dg_s, du_s, a_s = refs
        else:
            dy_ref, h_ref, wd_ref, wg_ref, wu_ref, g_ref, u_ref, dh0_ref, dh_ref, dw3_ref, dg_s, du_s, a_s = refs
        j = pl.program_id(0)

        @pl.when(j == 0)
        def _():
            dh_ref[...] = jnp.zeros_like(dh_ref) if dh_init is None else dh0_ref[...]
            for ref in (dg_s, du_s, a_s):
                ref[...] = jnp.zeros_like(ref)

        now, before = j % 2, 1 - j % 2
        dyv = dy_ref[...]
        hv = h_ref[...]
        dg, du, a = dg_s[before], du_s[before], a_s[before]
        dh_ref[...] += _mm_raw(dg, wg_ref[...]) + _mm_raw(du, wu_ref[...])
        dw3_ref[0] = _mm_raw(dg, hv, True, False).astype(BF16)
        dw3_ref[1] = _mm_raw(du, hv, True, False).astype(BF16)
        dw3_ref[2] = (_mm_raw(a, dyv, True, False) * 0.5).astype(BF16)

        da = _mm_raw(dyv, wd_ref[...], False, True) * 0.5
        a, vjp = jax.vjp(_silu_mul, g_ref[...].astype(F32), u_ref[...].astype(F32))
        dg, du = vjp(da)
        dg_s[now] = dg.astype(BF16)
        du_s[now] = du.astype(BF16)
        a_s[now] = a.astype(BF16)

    this = lambda j: first + jnp.minimum(j, count - 1)
    last = lambda j: first + jnp.maximum(j - 1, 0)
    full = pl.BlockSpec((s, d), lambda j: (0, 0))
    once = pl.BlockSpec((s, d), lambda j: (0, 0), pipeline_mode=pl.Buffered(1))
    tile = pl.BlockSpec((s, tn), lambda j: (0, this(j)))
    in_specs = [once, once, pl.BlockSpec((None, tn, d), lambda j: (2, this(j), 0)),
                pl.BlockSpec((None, tn, d), lambda j: (0, last(j), 0)), pl.BlockSpec((None, tn, d), lambda j: (1, last(j), 0)),
                tile, tile]
    args = [dyb, h, w3, w3, w3, g, u]
    if dh_init is not None:
        in_specs.append(once)
        args.append(dh_init)
    return pl.pallas_call(
        body, name=name, grid=(count + 1,), in_specs=in_specs,
        out_specs=[full, pl.BlockSpec((3, tn, d), lambda j: (0, jnp.maximum(j - 1, 0), 0))],
        out_shape=[SDS((s, d), F32), SDS((3, count * tn, d), BF16)],
        scratch_shapes=[pltpu.VMEM((2, s, tn), BF16)] * 3,
        compiler_params=_params("arbitrary"),
    )(*args)


def _bucket_table():
    qi = np.arange(BLOCK)[:, None]
    kj = np.arange(2 * BLOCK)[None, :]
    dist = np.maximum(qi + BLOCK - kj, 0)
    max_exact = N_BUCKETS // 2
    d = np.maximum(dist, 1).astype(np.float32)
    large = max_exact + (np.log(d / np.float32(max_exact)) / np.float32(math.log(MAX_DISTANCE / max_exact))
                         * np.float32(N_BUCKETS - max_exact)).astype(np.int32)
    large = np.minimum(large, N_BUCKETS - 1)
    band = np.where(dist < max_exact, dist, large).astype(np.int32)
    return np.where(np.tril(np.ones((BLOCK, BLOCK), bool)), band[:, BLOCK:], band[:, :BLOCK])


SWA_STACK = SWA_GROUP * BLOCK


def _swa_masks(n):
    qi = lax.broadcasted_iota(jnp.int32, (SWA_STACK, BLOCK), 0) % BLOCK
    kj = lax.broadcasted_iota(jnp.int32, (SWA_STACK, BLOCK), 1)
    own = kj <= qi
    return own, own | (n > 0)


def _swa_group(q, kp, kc, vp, vc, qg, kg, sink, bias, own, valid):
    qn = _rms(q, qg)
    s = jnp.where(own, _mm(qn, _rms(kc, kg), False, True), _mm(qn, _rms(kp, kg), False, True))
    s = s * (HEAD_DIM ** -0.5) + bias
    s = jnp.where(valid, s, -jnp.inf)
    m = lax.stop_gradient(jnp.maximum(jnp.max(s, axis=-1, keepdims=True), sink))
    p = jnp.exp(s - m)
    p = p / (jnp.sum(p, axis=-1, keepdims=True) + jnp.exp(sink - m))
    return _mm(jnp.where(own, p, 0.0), vc) + _mm(jnp.where(own, 0.0, p), vp)


def _swa_bias_table(rb_ref, bucket, bias_s):
    for h in range(SWA_HEADS):
        acc = jnp.zeros((BLOCK, BLOCK), F32)
        for b in range(N_BUCKETS):
            acc = jnp.where(bucket == b, rb_ref[b, h], acc)
        bias_s[h // SWA_GROUP, (h % SWA_GROUP) * BLOCK:(h % SWA_GROUP + 1) * BLOCK, :] = acc


def _swa_stack(ref, g):
    return jnp.concatenate([ref[:, (g * SWA_GROUP + hh) * HEAD_DIM:(g * SWA_GROUP + hh + 1) * HEAD_DIM]
                            for hh in range(SWA_GROUP)], axis=0)


def _swa_unstack(ref, g, stacked):
    for hh in range(SWA_GROUP):
        h = g * SWA_GROUP + hh
        ref[:, h * HEAD_DIM:(h + 1) * HEAD_DIM] = stacked[hh * BLOCK:(hh + 1) * BLOCK]


def _swa_sink_column(sink_ref, g):
    head = lax.broadcasted_iota(jnp.int32, (SWA_STACK, 1), 0) // BLOCK
    col = jnp.zeros((SWA_STACK, 1), F32)
    for hh in range(SWA_GROUP):
        col = jnp.where(head == hh, sink_ref[g * SWA_GROUP + hh], col)
    return col


def _swa_band(kvp_ref, kvc_ref, g):
    k = slice(g * HEAD_DIM, (g + 1) * HEAD_DIM)
    v = slice(SWA_KV_W + g * HEAD_DIM, SWA_KV_W + (g + 1) * HEAD_DIM)
    return kvp_ref[:, k], kvc_ref[:, k], kvp_ref[:, v], kvc_ref[:, v]


def _swa_specs(order):
    kvc = COL_SKV // (2 * SWA_KV_W)
    return [
        pl.BlockSpec((BLOCK, SWA_Q_W), lambda t: (order(t), 0)),
        pl.BlockSpec((BLOCK, 2 * SWA_KV_W), lambda t: (jnp.maximum(order(t) - 1, 0), kvc)),
        pl.BlockSpec((BLOCK, 2 * SWA_KV_W), lambda t: (order(t), kvc)),
        pl.BlockSpec((1, HEAD_DIM), lambda t: (0, 0)),
        pl.BlockSpec((1, HEAD_DIM), lambda t: (0, 0)),
        pl.BlockSpec(memory_space=pltpu.SMEM),
        pl.BlockSpec(memory_space=pltpu.SMEM),
        pl.BlockSpec((BLOCK, BLOCK), lambda t: (0, 0)),
    ]


def _swa_fwd(p, qg, kg, sinks, rel_bias, *, name):
    s = p.shape[0]
    nb = s // BLOCK

    def body(q_ref, kvp_ref, kvc_ref, qg_ref, kg_ref, sink_ref, rb_ref, bucket_ref, y_ref, bias_s):
        n = pl.program_id(0)

        @pl.when(n == 0)
        def _():
            _swa_bias_table(rb_ref, bucket_ref[...], bias_s)

        own, valid = _swa_masks(n)
        for g in range(SWA_KV_HEADS):
            out = _swa_group(_swa_stack(q_ref, g), *_swa_band(kvp_ref, kvc_ref, g), qg_ref[...], kg_ref[...],
                             _swa_sink_column(sink_ref, g), bias_s[g], own, valid)
            _swa_unstack(y_ref, g, out)

    return pl.pallas_call(
        body, name=name, grid=(nb,), in_specs=_swa_specs(lambda t: t),
        out_specs=pl.BlockSpec((BLOCK, SWA_Q_W), lambda t: (t, 0)), out_shape=SDS((s, SWA_Q_W), F32),
        scratch_shapes=[pltpu.VMEM((SWA_KV_HEADS, SWA_STACK, BLOCK), F32)],
        compiler_params=_params("arbitrary"),
    )(p, p, p, qg, kg, sinks, rel_bias, jnp.asarray(_bucket_table()))


def _swa_bwd(p, qg, kg, sinks, rel_bias, dy_all, *, name):
    s = p.shape[0]
    nb = s // BLOCK

    def body(q_ref, kvp_ref, kvc_ref, qg_ref, kg_ref, sink_ref, rb_ref, bucket_ref, dy_ref,
             dq_ref, dkv_ref, dqg_ref, dkg_ref, dsink_ref, drb_ref, bias_s, dbias_s, carry_s):
        t = pl.program_id(0)
        n = nb - 1 - t

        @pl.when(t == 0)
        def _():
            _swa_bias_table(rb_ref, bucket_ref[...], bias_s)
            dbias_s[...] = jnp.zeros_like(dbias_s)
            carry_s[...] = jnp.zeros_like(carry_s)
            dqg_ref[...] = jnp.zeros_like(dqg_ref)
            dkg_ref[...] = jnp.zeros_like(dkg_ref)
            dsink_ref[...] = jnp.zeros_like(dsink_ref)
            drb_ref[...] = jnp.zeros_like(drb_ref)

        own, valid = _swa_masks(n)
        lane = lax.broadcasted_iota(jnp.int32, (1, BLOCK), 1)
        dqg = jnp.zeros((1, HEAD_DIM), F32)
        dkg = jnp.zeros((1, HEAD_DIM), F32)
        dsink_vec = jnp.zeros((1, BLOCK), F32)
        for g in range(SWA_KV_HEADS):
            _, vjp = jax.vjp(functools.partial(_swa_group, own=own, valid=valid), _swa_stack(q_ref, g),
                             *_swa_band(kvp_ref, kvc_ref, g), qg_ref[...], kg_ref[...], _swa_sink_column(sink_ref, g),
                             bias_s[g])
            dq, dkp, dkc, dvp, dvc, dqg_g, dkg_g, dsink_col, dbias = vjp(_swa_stack(dy_ref, g))
            _swa_unstack(dq_ref, g, dq)
            dqg += dqg_g
            dkg += dkg_g
            dbias_s[g] += dbias
            for hh in range(SWA_GROUP):
                dsink_h = jnp.sum(dsink_col[hh * BLOCK:(hh + 1) * BLOCK], axis=0, keepdims=True)
                dsink_vec += jnp.where(lane == g * SWA_GROUP + hh, dsink_h, 0.0)
            lo = g * HEAD_DIM
            dkv_ref[:, lo:lo + HEAD_DIM] = dkc + carry_s[g]
            carry_s[g] = dkp
            lo += SWA_KV_W
            dkv_ref[:, lo:lo + HEAD_DIM] = dvc + carry_s[SWA_KV_HEADS + g]
            carry_s[SWA_KV_HEADS + g] = dvp
        dqg_ref[...] += dqg
        dkg_ref[...] += dkg
        dsink_ref[...] += dsink_vec

        @pl.when(t == nb - 1)
        def _():
            bucket = bucket_ref[...]
            row = lax.broadcasted_iota(jnp.int32, (N_BUCKETS, BLOCK), 0)
            col = lax.broadcasted_iota(jnp.int32, (N_BUCKETS, BLOCK), 1)
            acc = jnp.zeros((N_BUCKETS, BLOCK), F32)
            for h in range(SWA_HEADS):
                dbias = dbias_s[h // SWA_GROUP, (h % SWA_GROUP) * BLOCK:(h % SWA_GROUP + 1) * BLOCK, :]
                for b in range(N_BUCKETS):
                    part = jnp.sum(jnp.where(bucket == b, dbias, 0.0), axis=1, keepdims=True)
                    val = jnp.sum(part, axis=0, keepdims=True)
                    acc = acc + jnp.where((row == b) & (col == h), val, 0.0)
            drb_ref[...] = acc

    order = lambda t: nb - 1 - t
    vec = pl.BlockSpec((1, HEAD_DIM), lambda t: (0, 0))
    return pl.pallas_call(
        body, name=name, grid=(nb,),
        in_specs=_swa_specs(order) + [pl.BlockSpec((BLOCK, SWA_Q_W), lambda t: (order(t), 0))],
        out_specs=[pl.BlockSpec((BLOCK, SWA_Q_W), lambda t: (order(t), 0)),
                   pl.BlockSpec((BLOCK, 2 * SWA_KV_W), lambda t: (order(t), 0)),
                   vec, vec, pl.BlockSpec((1, BLOCK), lambda t: (0, 0)),
                   pl.BlockSpec((N_BUCKETS, BLOCK), lambda t: (0, 0))],
        out_shape=[SDS((s, SWA_Q_W), F32), SDS((s, 2 * SWA_KV_W), F32), SDS((1, HEAD_DIM), F32),
                   SDS((1, HEAD_DIM), F32), SDS((1, BLOCK), F32), SDS((N_BUCKETS, BLOCK), F32)],
        scratch_shapes=[pltpu.VMEM((SWA_KV_HEADS, SWA_STACK, BLOCK), F32),
                        pltpu.VMEM((SWA_KV_HEADS, SWA_STACK, BLOCK), F32),
                        pltpu.VMEM((2 * SWA_KV_HEADS, BLOCK, HEAD_DIM), F32)],
        compiler_params=_params("arbitrary"),
    )(p, p, p, qg, kg, sinks, rel_bias, jnp.asarray(_bucket_table()), dy_all)


def _mem_head(q, k, v, qg, kg):
    qn = _rms(q, qg)
    kn = _rms(k, kg)
    s = _mm(qn, kn, False, True) * (HEAD_DIM ** -0.5)
    m = lax.stop_gradient(jnp.max(s, axis=-1, keepdims=True))
    e = jnp.exp(s - m)
    return _mm(e / jnp.sum(e, axis=-1, keepdims=True), v)


def _mem_fwd(p, kv, qg, kg, *, tq, name):
    s = p.shape[0]
    m = kv.shape[0]

    def body(q_ref, kv_ref, qg_ref, kg_ref, y_ref):
        for h in range(MEM_HEADS):
            cols = slice(h * HEAD_DIM, (h + 1) * HEAD_DIM)
            vcols = slice(MEM_Q_W + h * HEAD_DIM, MEM_Q_W + (h + 1) * HEAD_DIM)
            y_ref[:, cols] = _mem_head(q_ref[:, cols], kv_ref[:, cols], kv_ref[:, vcols], qg_ref[...], kg_ref[...])

    vec = pl.BlockSpec((1, HEAD_DIM), lambda t: (0, 0))
    return pl.pallas_call(
        body, name=name, grid=(s // tq,),
        in_specs=[pl.BlockSpec((tq, MEM_Q_W), lambda t: (t, COL_MQ // MEM_Q_W)),
                  pl.BlockSpec((m, 2 * MEM_Q_W), lambda t: (0, 0)), vec, vec],
        out_specs=pl.BlockSpec((tq, MEM_Q_W), lambda t: (t, 0)), out_shape=SDS((s, MEM_Q_W), F32),
        compiler_params=_params("parallel"),
    )(p, kv, qg, kg)


def _mem_bwd(p, kv, qg, kg, dy_all, *, tq, name):
    s = p.shape[0]
    m = kv.shape[0]

    def body(q_ref, kv_ref, qg_ref, kg_ref, dy_ref, dq_ref, dkv_ref, dqg_ref, dkg_ref):
        @pl.when(pl.program_id(0) == 0)
        def _():
            dkv_ref[...] = jnp.zeros_like(dkv_ref)
            dqg_ref[...] = jnp.zeros_like(dqg_ref)
            dkg_ref[...] = jnp.zeros_like(dkg_ref)

        dqg = jnp.zeros((1, HEAD_DIM), F32)
        dkg = jnp.zeros((1, HEAD_DIM), F32)
        for h in range(MEM_HEADS):
            cols = slice(h * HEAD_DIM, (h + 1) * HEAD_DIM)
            vcols = slice(MEM_Q_W + h * HEAD_DIM, MEM_Q_W + (h + 1) * HEAD_DIM)
            _, vjp = jax.vjp(_mem_head, q_ref[:, cols], kv_ref[:, cols], kv_ref[:, vcols], qg_ref[...], kg_ref[...])
            dq, dk, dv, dqg_h, dkg_h = vjp(dy_ref[:, cols])
            dq_ref[:, cols] = dq
            dkv_ref[:, cols] += dk
            dkv_ref[:, vcols] += dv
            dqg += dqg_h
            dkg += dkg_h
        dqg_ref[...] += dqg
        dkg_ref[...] += dkg

    vec = pl.BlockSpec((1, HEAD_DIM), lambda t: (0, 0))
    full = pl.BlockSpec((m, 2 * MEM_Q_W), lambda t: (0, 0))
    dy_col = (SWA_Q_W + GLA_V_W) // MEM_Q_W
    return pl.pallas_call(
        body, name=name, grid=(s // tq,),
        in_specs=[pl.BlockSpec((tq, MEM_Q_W), lambda t: (t, COL_MQ // MEM_Q_W)), full, vec, vec,
                  pl.BlockSpec((tq, MEM_Q_W), lambda t: (t, dy_col))],
        out_specs=[pl.BlockSpec((tq, MEM_Q_W), lambda t: (t, 0)), full, vec, vec],
        out_shape=[SDS((s, MEM_Q_W), F32), SDS((m, 2 * MEM_Q_W), F32), SDS((1, HEAD_DIM), F32), SDS((1, HEAD_DIM), F32)],
        compiler_params=_params("arbitrary"),
    )(p, kv, qg, kg, dy_all)


GLA_ROWS = 256


GLA_GROUP = 4


def _gla_consts():
    c, h, r = GLA_CHUNK, GLA_HEADS, GLA_GROUP * GLA_CHUNK
    i2 = lax.broadcasted_iota(jnp.int32, (c, c), 0)
    j2 = lax.broadcasted_iota(jnp.int32, (c, c), 1)
    slab_q = lax.broadcasted_iota(jnp.int32, (h, r, GLA_QK_W), 0)
    lane_q = lax.broadcasted_iota(jnp.int32, (h, r, GLA_QK_W), 2)
    row_a = lax.broadcasted_iota(jnp.int32, (h * r, r), 0) % r
    col_a = lax.broadcasted_iota(jnp.int32, (h * r, r), 1)
    slab_o = lax.broadcasted_iota(jnp.int32, (h, r, GLA_V_W), 0)
    lane_o = lax.broadcasted_iota(jnp.int32, (h, r, GLA_V_W), 2)
    row_s = lax.broadcasted_iota(jnp.int32, (GLA_V_W, GLA_QK_W), 0)
    col_s = lax.broadcasted_iota(jnp.int32, (GLA_V_W, GLA_QK_W), 1)
    return dict(
        ltri=(j2 <= i2).astype(F32),
        m_q=(slab_q == lane_q // GLA_DK).astype(F32),
        causal=(col_a <= row_a) & (col_a // c == row_a // c),
        m_o=(slab_o == lane_o // GLA_DV).astype(F32),
        m_s=(row_s // GLA_DV == col_s // GLA_DK).astype(F32),
    )


def _gla_step(q, k, v, z, bg, st, c):
    h = GLA_HEADS
    kt, qt, qe, decay = [], [], [], []
    for qc, kc, zc in zip(q, k, z):
        la = _log_sigmoid(zc + bg) * (1.0 / GLA_TAU)
        b = _mmf(c["ltri"], la)
        bl = jnp.sum(la, axis=0, keepdims=True)
        qs = qc * (GLA_DK ** -0.5)
        kt.append(kc * jnp.exp(bl - b))
        qt.append(qs * jnp.exp(b - bl))
        qe.append(qs * jnp.exp(b))
        decay.append(jnp.exp(bl))
    o_intra = []
    rows = GLA_GROUP * GLA_CHUNK
    for lo in range(0, len(q), GLA_GROUP):
        qt_all, kt_all, v_all = (jnp.concatenate(parts[lo:lo + GLA_GROUP], axis=0) for parts in (qt, kt, v))
        q_stack = (jnp.broadcast_to(qt_all[None], (h, rows, GLA_QK_W)) * c["m_q"]).reshape(h * rows, GLA_QK_W)
        a = jnp.where(c["causal"], _mm3(q_stack, kt_all, False, True), 0.0)
        o_stack = _mm(a, v_all)
        o_intra.append(jnp.sum(o_stack.reshape(h, rows, GLA_V_W) * c["m_o"], axis=0))
    o_intra = jnp.concatenate(o_intra, axis=0)
    o_inter = []
    for qec, ktc, vc, dc in zip(qe, kt, v, decay):
        o_inter.append(_mm(qec, st, False, True))
        st = st * dc + _mm(vc, ktc, True, False) * c["m_s"]
    return o_intra + jnp.concatenate(o_inter, axis=0), st


def _gla_post(o, gg, gain, g64):
    ms = _mmf(o * o, g64) * (1.0 / GLA_DV)
    return o * lax.rsqrt(ms + EPS) * gain * jax.nn.silu(gg)


def _gla_g64():
    r = lax.broadcasted_iota(jnp.int32, (GLA_V_W, GLA_V_W), 0)
    c = lax.broadcasted_iota(jnp.int32, (GLA_V_W, GLA_V_W), 1)
    return (r // GLA_DV == c // GLA_DV).astype(F32)


def _gla_in_specs(order):
    r = GLA_ROWS
    return [
        pl.BlockSpec((r, GLA_QK_W), lambda t: (order(t), COL_GQ // GLA_QK_W)),
        pl.BlockSpec((r, GLA_QK_W), lambda t: (order(t), COL_GK // GLA_QK_W)),
        pl.BlockSpec((r, GLA_V_W), lambda t: (order(t), COL_GV // GLA_V_W)),
        pl.BlockSpec((r, GLA_V_W), lambda t: (order(t), COL_GG // GLA_V_W)),
        pl.BlockSpec((r, GLA_QK_W), lambda t: (order(t), 0)),
        pl.BlockSpec((1, GLA_QK_W), lambda t: (0, 0)),
        pl.BlockSpec((1, GLA_V_W), lambda t: (0, 0)),
    ]


def _gla_pieces(q_ref, k_ref, v_ref, z_ref, cps):
    chunk = lambda ref: [ref[ci * GLA_CHUNK:(ci + 1) * GLA_CHUNK, :] for ci in range(cps)]
    return chunk(q_ref), chunk(k_ref), chunk(v_ref), chunk(z_ref)


def _gla_fwd(p, z, bg, gain, *, name):
    s = p.shape[0]
    r = GLA_ROWS
    cps = r // GLA_CHUNK

    def body(q_ref, k_ref, v_ref, gg_ref, z_ref, bg_ref, gain_ref, y_ref, oraw_ref, stsave_ref, st_s):
        @pl.when(pl.program_id(0) == 0)
        def _():
            st_s[...] = jnp.zeros_like(st_s)

        st = st_s[...]
        stsave_ref[0] = st
        o, st = _gla_step(*_gla_pieces(q_ref, k_ref, v_ref, z_ref, cps), bg_ref[...], st, _gla_consts())
        oraw_ref[...] = o
        st_s[...] = st
        y_ref[...] = _gla_post(o, gg_ref[...], gain_ref[...], _gla_g64())

    rowv = pl.BlockSpec((r, GLA_V_W), lambda t: (t, 0))
    return pl.pallas_call(
        body, name=name, grid=(s // r,), in_specs=_gla_in_specs(lambda t: t),
        out_specs=[rowv, rowv, pl.BlockSpec((1, GLA_V_W, GLA_QK_W), lambda t: (t, 0, 0))],
        out_shape=[SDS((s, GLA_V_W), F32), SDS((s, GLA_V_W), F32), SDS((s // r, GLA_V_W, GLA_QK_W), F32)],
        scratch_shapes=[pltpu.VMEM((GLA_V_W, GLA_QK_W), F32)],
        compiler_params=_params("arbitrary"),
    )(p, p, p, p, z, bg, gain)


def _gla_bwd(p, z, bg, gain, oraw, stsave, dy_all, *, name):
    s = p.shape[0]
    r = GLA_ROWS
    cps = r // GLA_CHUNK
    nsteps = s // r
    w_qkvg = 2 * GLA_QK_W + 2 * GLA_V_W

    def body(q_ref, k_ref, v_ref, gg_ref, z_ref, bg_ref, gain_ref, oraw_ref, stsave_ref, dy_ref,
             dqkvg_ref, dz_ref, dbg_ref, dgain_ref, dst_s):
        @pl.when(pl.program_id(0) == 0)
        def _():
            dst_s[...] = jnp.zeros_like(dst_s)
            dbg_ref[...] = jnp.zeros_like(dbg_ref)
            dgain_ref[...] = jnp.zeros_like(dgain_ref)

        _, vjp = jax.vjp(functools.partial(_gla_post, g64=_gla_g64()), oraw_ref[...], gg_ref[...], gain_ref[...])
        do, dgg, dgain = vjp(dy_ref[...])
        dqkvg_ref[:, 2 * GLA_QK_W + GLA_V_W:] = dgg
        dgain_ref[...] += dgain
        _, vjp = jax.vjp(functools.partial(_gla_step, c=_gla_consts()), *_gla_pieces(q_ref, k_ref, v_ref, z_ref, cps),
                         bg_ref[...], stsave_ref[0])
        dq, dk, dv, dz, dbg, dst = vjp((do, dst_s[...]))
        for ci in range(cps):
            rows = slice(ci * GLA_CHUNK, (ci + 1) * GLA_CHUNK)
            dqkvg_ref[rows, 0:GLA_QK_W] = dq[ci]
            dqkvg_ref[rows, GLA_QK_W:2 * GLA_QK_W] = dk[ci]
            dqkvg_ref[rows, 2 * GLA_QK_W:2 * GLA_QK_W + GLA_V_W] = dv[ci]
            dz_ref[rows, :] = dz[ci]
        dst_s[...] = dst
        dbg_ref[...] += dbg

    order = lambda t: nsteps - 1 - t
    rowv = pl.BlockSpec((r, GLA_V_W), lambda t: (order(t), 0))
    return pl.pallas_call(
        body, name=name, grid=(nsteps,),
        in_specs=_gla_in_specs(order) + [
            rowv, pl.BlockSpec((1, GLA_V_W, GLA_QK_W), lambda t: (order(t), 0, 0)),
            pl.BlockSpec((r, GLA_V_W), lambda t: (order(t), SWA_Q_W // GLA_V_W))],
        out_specs=[pl.BlockSpec((r, w_qkvg), lambda t: (order(t), 0)), pl.BlockSpec((r, GLA_QK_W), lambda t: (order(t), 0)),
                   pl.BlockSpec((1, GLA_QK_W), lambda t: (0, 0)), pl.BlockSpec((1, GLA_V_W), lambda t: (0, 0))],
        out_shape=[SDS((s, w_qkvg), F32), SDS((s, GLA_QK_W), F32), SDS((1, GLA_QK_W), F32), SDS((1, GLA_V_W), F32)],
        scratch_shapes=[pltpu.VMEM((GLA_V_W, GLA_QK_W), F32)],
        compiler_params=_params("arbitrary"),
    )(p, p, p, p, z, bg, gain, oraw, stsave, dy_all)


def _local_step(x, mem, target, small, big, on_grads):
    g1, gmix, gmem, g2, sqg, skg, sinks, rel_bias, wgu, bg, gla_gain, mqg, mkg = small
    (w3_1a, w3_1b), win_p, wkv, wout, gather_ffn2 = big
    wgu_pad = jnp.zeros((GLA_QK_W, GLA_QK_W), BF16).at[:GLA_RANK].set(wgu.astype(BF16))
    gain256 = jnp.tile(gla_gain, (1, GLA_HEADS))

    (part,), saved1a = _ffn_fwd(x, g1, w3_1a, "ffn1a", partial=True)
    (x1, h), saved1b = _ffn_fwd(x, g1, w3_1b, "ffn1b", next_gain=gmix, start=(saved1a[0], part))
    w3_2 = gather_ffn2(x1)
    p = _matmul(h, win_p, tm=512, tn=IN_W_PAD, name="mix_in")
    hm = _rms_fwd(mem, gmem, tm=256, name="mem_rms")
    kv = _matmul(hm, wkv, tm=256, tn=512, name="mem_kv")
    p_glr = p[:, COL_GLR:]
    z = _matmul(p_glr, wgu_pad, tm=1024, tn=GLA_QK_W, name="gla_gate")
    y_swa = _swa_fwd(p, sqg, skg, sinks, rel_bias, name="swa_fwd")
    y_gla, oraw, stsave = _gla_fwd(p, z, bg, gain256, name="gla_fwd")
    y_mem = _mem_fwd(p, kv, mqg, mkg, tq=512, name="mem_fwd")
    x2 = _matmul([y_swa, y_gla, y_mem], wout, b_blocks=[0, 2, 3], tm=512, tn=1024, res=x1, name="mix_out")
    (dy, dyb, loss), saved2 = _ffn_fwd(x2, g2, w3_2, "ffn2", target=target)

    tiles = w3_2.shape[1] // FFN_TN
    dh2, dw3_2 = _ffn_bwd_part(dyb, w3_2, saved2, 0, tiles, None, name="ffn2_bwd")
    dx2, dx2b, dg2 = _rms_bwd(x2, g2, dh2, dy, tm=512, name="ffn2_drms")
    dx2b = on_grads("ffn2", [dw3_2.reshape(3, 2, -1, dw3_2.shape[-1])], dx2b)
    dy_all = _matmul(dx2b, wout, tb=True, tm=512, tn=1024, name="mix_dy")
    dwout = _matmul(jnp.concatenate([y_swa, y_gla, y_mem], axis=1), dx2b, ta=True, tm=512, tn=1024, out_dtype=BF16,
                    name="mix_dw_out")
    dq_swa, dkv_swa, dsqg, dskg, dsink, drb = _swa_bwd(p, sqg, skg, sinks, rel_bias, dy_all, name="swa_bwd")
    dqkvg, dz, dbg, dgain256 = _gla_bwd(p, z, bg, gain256, oraw, stsave, dy_all, name="gla_bwd")
    dmq, dkv_mem, dmqg, dmkg = _mem_bwd(p, kv, mqg, mkg, dy_all, tq=512, name="mem_bwd")
    dglr = _matmul(dz, wgu_pad, tb=True, tm=1024, tn=GLA_QK_W, name="gla_gate_dx")
    dwgu_pad = _matmul(p_glr, dz, ta=True, tm=GLA_QK_W, tn=GLA_QK_W, name="gla_gate_dw")
    dp = jnp.concatenate([dq_swa, dkv_swa, dqkvg, dmq, dglr], axis=1)
    dh = _matmul(dp, win_p, tb=True, tm=512, tn=1024, name="mix_dh")
    dwin_p = _matmul(h, dp, ta=True, tm=1024, tn=640, out_dtype=BF16, name="mix_dw_in")
    dx1, dx1b, dgmix = _rms_bwd(x1, gmix, dh, dx2, tm=512, name="mix_drms")
    dwkv = _matmul(hm, dkv_mem, ta=True, tm=512, tn=512, out_dtype=BF16, name="mem_dw_kv")
    dx1b = on_grads("mix", (dwin_p, dwkv, dwout), dx1b)
    dhm = _matmul(dkv_mem, wkv, tb=True, tm=256, tn=512, name="mem_dh")
    _, _, dgmem = _rms_bwd(mem, gmem, dhm, None, tm=256, name="mem_drms")
    dh1, dw3_1a = _ffn_bwd_part(dx1b, w3_1a, saved1a, 0, tiles // 2, None, name="ffn1_bwd_a")
    dgla_gain = dgain256.reshape(GLA_HEADS, GLA_DV).sum(axis=0, keepdims=True)
    dsmall = [dgmix, dgmem, dg2, dsqg, dskg, dsink[:, :SWA_HEADS], drb[:, :SWA_HEADS], dwgu_pad[:GLA_RANK], dbg,
              dgla_gain, dmqg, dmkg, loss]
    dh1, dgmem, dwgu_pad = on_grads("ffn1a", [dw3_1a[:, None]], (dh1, dgmem, dwgu_pad), small=dsmall)
    dh1, dw3_1b = _ffn_bwd_part(dx1b, w3_1b, saved1b, 0, tiles // 2, dh1, name="ffn1_bwd_b")
    dx, _, dg1 = _rms_bwd(x, g1, dh1, dx1, tm=512, name="ffn1_drms")
    on_grads("ffn1b", [dw3_1b[:, None]], None, small=[dg1])
    return dx


def _mesh_place():
    x, y, c = lax.axis_index("x"), lax.axis_index("y"), lax.axis_index("c")
    other_chips = [(1 - x, y), (x, 1 - y), (1 - x, 1 - y)]
    return x, y, c, other_chips


def _handshake(peers):
    barrier = pltpu.get_barrier_semaphore()
    for peer in peers:
        pl.semaphore_signal(barrier, inc=1, device_id=peer, device_id_type=MESH)
    pl.semaphore_wait(barrier, len(peers))


def _sequencer_call(body, operands, out_shapes, sems, *, name, collective_id):
    return pl.kernel(
        body, name=name, out_type=out_shapes, mesh=plsc.ScalarSubcoreMesh(axis_name="sequencer", num_cores=1),
        scratch_types=sems, compiler_params=pltpu.CompilerParams(collective_id=collective_id),
    )(*operands)


def _window(ref, kind, slot, shape):
    if kind == "row":
        rows = pl.ds(pl.multiple_of(slot * shape[-2], 8), shape[-2])
        return ref.at[(slice(None),) * (len(shape) - 2) + (rows,)]
    return ref.at[slot]


def _gathered(shape, kind):
    if kind == "row":
        return tuple(shape[:-2]) + (N_DEV * shape[-2], shape[-1])
    return (N_DEV,) + tuple(shape)


def _half(view, hf):
    if len(view.shape) == 4:
        return view.at[:, hf]
    n = view.shape[-2] // 2
    return view.at[(slice(None),) * (len(view.shape) - 2) + (pl.ds(hf * n, n),)]


def _all_gather(shards, kinds, *, name, collective_id):
    nt = len(shards)

    def body(*refs):
        x_refs, o_refs = refs[:nt], refs[nt:2 * nt]
        send_sems, recv_sems, local_sems = refs[2 * nt:]
        x, y, c, _ = _mesh_place()
        me, sibling, xn, yn, diag = (x, y, c), (x, y, 1 - c), (1 - x, y, c), (x, 1 - y, c), (1 - x, 1 - y, c)
        _handshake([sibling, xn, yn])

        def win(t, block):
            bx, by, bc = block
            return _window(o_refs[t], kinds[t], 4 * bx + 2 * by + bc, shards[t].shape)

        def copy(k, t, src, dst, to):
            return pltpu.make_async_remote_copy(src_ref=src, dst_ref=dst, send_sem=send_sems.at[k, t],
                                                recv_sem=recv_sems.at[k, t], device_id=to, device_id_type=MESH)

        def piece(k, t, block, hf, to, from_shard=False):
            dst = _half(win(t, block), hf)
            return copy(k, t, _half(x_refs[t], hf) if from_shard else dst, dst, to)

        mine = [pltpu.make_async_copy(x_refs[t], win(t, me), local_sems.at[t]) for t in range(nt)]
        sent = []

        def start(cp):
            cp.start()
            sent.append(cp)

        for cp in mine:
            cp.start()
        for t in range(nt):
            start(copy(0, t, x_refs[t], win(t, me), sibling))
        for hf_x, hf_y in ((0, 1), (1, 0)):
            for t in range(nt):
                start(piece(1 + hf_x, t, me, hf_x, xn, True))
                start(piece(3 + hf_y, t, me, hf_y, yn, True))
        for k, block, hf, onward, k_sib in ((1, xn, 0, (5, yn), 7), (4, yn, 1, (6, xn), 10), (2, xn, 1, None, 8),
                                           (3, yn, 0, None, 9), (5, diag, 0, None, 11), (6, diag, 1, None, 12)):
            for t in range(nt):
                piece(k, t, block, hf, me).wait_recv()
                if onward is not None:
                    start(piece(onward[0], t, block, hf, onward[1]))
                start(piece(k_sib, t, block, hf, sibling))
        for t in range(nt):
            copy(0, t, x_refs[t], win(t, sibling), me).wait_recv()
        for k_sib, block, hf in ((7, xn, 0), (10, yn, 1), (8, xn, 1), (9, yn, 0), (11, diag, 0), (12, diag, 1)):
            for t in range(nt):
                bx, by, _ = block
                piece(k_sib, t, (bx, by, 1 - c), hf, me).wait_recv()
        for cp in sent:
            cp.wait_send()
        for cp in mine:
            cp.wait()

    return _sequencer_call(
        body, shards, [SDS(_gathered(s.shape, k), s.dtype) for s, k in zip(shards, kinds)],
        [pltpu.SemaphoreType.DMA((13, nt)), pltpu.SemaphoreType.DMA((13, nt)), pltpu.SemaphoreType.DMA((nt,))],
        name=name, collective_id=collective_id)


def _part_shape(shape, kind):
    if kind == "row":
        return tuple(shape[:-2]) + (shape[-2] // N_DEV, shape[-1])
    return tuple(shape[2:])


def _pair_exchange(grads, kinds, *, name, collective_id):
    nt = len(grads)
    part = [_part_shape(g.shape, k) for g, k in zip(grads, kinds)]

    def body(*refs):
        g_refs, o_refs = refs[:nt], refs[nt:2 * nt]
        send_sems, recv_sems = refs[2 * nt:]
        x, y, c, _ = _mesh_place()
        _handshake([(x, y, 1 - c)])
        copies = []
        for t in range(nt):
            for xy in range(4):
                src = g_refs[t].at[1 - c, xy] if kinds[t] == "stack" else _window(g_refs[t], kinds[t], 2 * xy + 1 - c, part[t])
                copies.append(pltpu.make_async_remote_copy(
                    src_ref=src, dst_ref=o_refs[t].at[xy], send_sem=send_sems.at[xy, t], recv_sem=recv_sems.at[xy, t],
                    device_id=(x, y, 1 - c), device_id_type=MESH))
        for cp in copies:
            cp.start()
        for cp in copies:
            cp.wait()

    return _sequencer_call(
        body, grads, [SDS((4,) + p, g.dtype) for p, g in zip(part, grads)],
        [pltpu.SemaphoreType.DMA((4, nt)), pltpu.SemaphoreType.DMA((4, nt))], name=name, collective_id=collective_id)


def _chip_exchange(parts, small, *, name, collective_id):
    nt = len(parts)
    if small is None:
        def body_plain(*refs):
            s_refs, o_refs = refs[:nt], refs[nt:2 * nt]
            send_sems, recv_sems = refs[2 * nt:]
            x, y, c, chips = _mesh_place()
            _handshake([(*chip, c) for chip in chips])
            copies = [pltpu.make_async_remote_copy(
                src_ref=s_refs[t].at[2 * chip[0] + chip[1]], dst_ref=o_refs[t].at[j],
                send_sem=send_sems.at[j, t], recv_sem=recv_sems.at[j, t], device_id=(*chip, c), device_id_type=MESH)
                for j, chip in enumerate(chips) for t in range(nt)]
            for cp in copies:
                cp.start()
            for cp in copies:
                cp.wait()

        return _sequencer_call(
            body_plain, parts, [SDS((3,) + s.shape[1:], s.dtype) for s in parts],
            [pltpu.SemaphoreType.DMA((3, nt)), pltpu.SemaphoreType.DMA((3, nt))], name=name, collective_id=collective_id)

    def body(*refs):
        s_refs, small_ref = refs[:nt], refs[nt]
        o_refs, small_all = refs[nt + 1:2 * nt + 1], refs[2 * nt + 1]
        send_sems, recv_sems, small_send, small_recv, local_sem = refs[2 * nt + 2:]
        x, y, c, chips = _mesh_place()
        _handshake([(px, py, pc) for px in (x, 1 - x) for py in (y, 1 - y) for pc in (c, 1 - c)][1:])

        def copy(j, t, chip):
            return pltpu.make_async_remote_copy(
                src_ref=s_refs[t].at[2 * chip[0] + chip[1]], dst_ref=o_refs[t].at[j],
                send_sem=send_sems.at[j, t], recv_sem=recv_sems.at[j, t], device_id=(*chip, c), device_id_type=MESH)

        flips = [(fx, fy, fc) for fx in (0, 1) for fy in (0, 1) for fc in (0, 1)][1:]

        def small_copy(k):
            fx, fy, fc = flips[k]
            to = (x ^ fx if fx else x, y ^ fy if fy else y, c ^ fc if fc else c)
            rows = small_all.at[4 * x + 2 * y + c]
            return pltpu.make_async_remote_copy(
                src_ref=small_ref, dst_ref=rows, send_sem=small_send.at[k], recv_sem=small_recv.at[k],
                device_id=to, device_id_type=MESH)

        own = pltpu.make_async_copy(small_ref, small_all.at[4 * x + 2 * y + c], local_sem)
        own.start()
        copies = [copy(j, t, chip) for j, chip in enumerate(chips) for t in range(nt)]
        smalls = [small_copy(k) for k in range(7)]
        for cp in smalls + copies:
            cp.start()
        for cp in smalls + copies:
            cp.wait()
        own.wait()

    return _sequencer_call(
        body, list(parts) + [small],
        [SDS((3,) + s.shape[1:], s.dtype) for s in parts] + [SDS((N_DEV,) + small.shape, small.dtype)],
        [pltpu.SemaphoreType.DMA((3, nt)), pltpu.SemaphoreType.DMA((3, nt)),
         pltpu.SemaphoreType.DMA((7,)), pltpu.SemaphoreType.DMA((7,)), pltpu.SemaphoreType.DMA],
        name=name, collective_id=collective_id)


def _pair_sum(grad, theirs, kind, c, *, name):
    if kind == "row":
        r, l = theirs.shape[-2:]
        n = theirs.size // (4 * r * l)
        grad, theirs = grad.reshape(n, N_DEV * r, l), theirs.reshape(4, n, r, l)
        mine_spec = pl.BlockSpec((n, r, l), lambda xy, c_ref: (0, 2 * xy + c_ref[0], 0))
    else:
        r, l = theirs.shape[-2:]
        n = theirs.size // (4 * r * l)
        theirs = theirs.reshape(4, n, r, l)
        grad = grad.reshape(2, 4, n, r, l)
        mine_spec = pl.BlockSpec((None, None, n, r, l), lambda xy, c_ref: (c_ref[0], xy, 0, 0, 0))

    def body(c_ref, a_ref, b_ref, o_ref):
        o_ref[...] = (a_ref[...].astype(F32) + b_ref[...].astype(F32)).astype(BF16)

    part = pl.BlockSpec((None, n, r, l), lambda xy, c_ref: (xy, 0, 0, 0))
    return pl.pallas_call(
        body, name=name,
        grid_spec=pltpu.PrefetchScalarGridSpec(num_scalar_prefetch=1, grid=(4,), in_specs=[mine_spec, part], out_specs=part),
        out_shape=SDS((4, n, r, l), BF16), compiler_params=_params("parallel"),
    )(c, grad, theirs)


def _adamw(w, g, m, v):
    m = ADAM_B1 * m + (1.0 - ADAM_B1) * g
    v = ADAM_B2 * v + (1.0 - ADAM_B2) * jnp.square(g)
    m_hat = m / (1.0 - ADAM_B1 ** ADAM_STEP)
    v_hat = v / (1.0 - ADAM_B2 ** ADAM_STEP)
    delta = -ADAM_LR * (m_hat / (jnp.sqrt(v_hat) + ADAM_EPS) + ADAM_WD * w)
    return delta, m, v


def _adam_big(owns, others, mat, xy, w, m, v, *, tr, name):
    nw, r, l = w.shape
    lp = owns[0].shape[-1]
    nq = len(owns)
    assert nq in (1, nw)

    def body(xy_ref, *refs):
        own_refs, oth_refs = refs[:nq], refs[nq:2 * nq]
        w_ref, m_ref, v_ref, g_out, d_out, m_out, v_out = refs[2 * nq:]
        g = None
        for q in range(nq):
            gq = own_refs[q][0, 0].astype(F32)
            for j in range(3):
                gq = gq + oth_refs[q][j, 0].astype(F32)
            g = gq if g is None else jnp.where(pl.program_id(0) == q, gq, g)
        g = g[:, :l]
        delta, m_new, v_new = _adamw(w_ref[0], g, m_ref[0], v_ref[0])
        g_out[0] = g
        d_out[0] = delta
        m_out[0] = m_new
        v_out[0] = v_new

    def at(p):
        return mat * nw + p if nq == 1 else mat

    blk = pl.BlockSpec((1, tr, l), lambda p, i, xy_ref: (p, i, 0))
    return pl.pallas_call(
        body, name=name,
        grid_spec=pltpu.PrefetchScalarGridSpec(
            num_scalar_prefetch=1, grid=(nw, r // tr),
            in_specs=[pl.BlockSpec((1, 1, tr, lp), lambda p, i, xy_ref: (xy_ref[0], at(p), i, 0))] * nq
            + [pl.BlockSpec((3, 1, tr, lp), lambda p, i, xy_ref: (0, at(p), i, 0))] * nq + [blk, blk, blk],
            out_specs=[blk, blk, blk, blk]),
        out_shape=[SDS(w.shape, F32)] * 4, compiler_params=_params("parallel", "parallel"),
    )(xy, *owns, *others, w, m, v)


def _small_layout(shapes):
    out, at = [], 0
    for r, c in shapes:
        rows = c // 128 if (r == 1 and c > 128) else r
        out.append((at, rows))
        at += -(-rows // 8) * 8
    return out, at


def _pack_small(parts, *, name):
    shapes = [a.shape for a in parts]
    layout, total = _small_layout(shapes)

    def body(*refs):
        o_ref = refs[-1]
        o_ref[...] = jnp.zeros_like(o_ref)
        for x_ref, (r, c), (at, rows) in zip(refs, shapes, layout):
            if r == 1 and c > 128:
                for k in range(rows):
                    o_ref[at + k:at + k + 1, :] = x_ref[:, k * 128:(k + 1) * 128]
            else:
                o_ref[at:at + r, 0:c] = x_ref[...]

    return pl.pallas_call(body, name=name, out_shape=SDS((total, 128), F32))(*parts)


def _adam_small(g_all, ws, ms, vs, *, name):
    n = len(ws)
    shapes = [w.shape for w in ws]
    layout, _ = _small_layout(shapes)

    def body(g_ref, *refs):
        w_refs, m_refs, v_refs, outs = refs[:n], refs[n:2 * n], refs[2 * n:3 * n], refs[3 * n:]
        g_sum = g_ref[0]
        for k in range(1, N_DEV):
            g_sum = g_sum + g_ref[k]
        for i, ((r, c), (at, rows)) in enumerate(zip(shapes, layout)):
            if r == 1 and c > 128:
                g = jnp.concatenate([g_sum[at + k:at + k + 1, :] for k in range(rows)], axis=1)
            else:
                g = g_sum[at:at + r, 0:c]
            delta, m_new, v_new = _adamw(w_refs[i][...], g, m_refs[i][...], v_refs[i][...])
            for q, val in enumerate((g, delta, m_new, v_new)):
                outs[4 * i + q][...] = val

    flat = pl.pallas_call(body, name=name, out_shape=[SDS(s, F32) for s in shapes for _ in range(4)])(g_all, *ws, *ms, *vs)
    return [flat[4 * i:4 * i + 4] for i in range(n)]


def kernel(x, mem, ffn1_norm, ffn1_w_gate, ffn1_w_up, ffn1_w_down, mix_norm, mem_norm, w_in, w_mem_kv, swa_q_norm, swa_k_norm, swa_sinks, rel_bias, gla_w_gate_up, gla_b_gate, gla_out_norm, mem_q_norm, mem_k_norm, w_out, ffn2_norm, ffn2_w_gate, ffn2_w_up, ffn2_w_down, loss_target, m_ffn1_norm, m_ffn1_w_gate, m_ffn1_w_up, m_ffn1_w_down, m_mix_norm, m_mem_norm, m_w_in, m_w_mem_kv, m_swa_q_norm, m_swa_k_norm, m_swa_sinks, m_rel_bias, m_gla_w_gate_up, m_gla_b_gate, m_gla_out_norm, m_mem_q_norm, m_mem_k_norm, m_w_out, m_ffn2_norm, m_ffn2_w_gate, m_ffn2_w_up, m_ffn2_w_down, v_ffn1_norm, v_ffn1_w_gate, v_ffn1_w_up, v_ffn1_w_down, v_mix_norm, v_mem_norm, v_w_in, v_w_mem_kv, v_swa_q_norm, v_swa_k_norm, v_swa_sinks, v_rel_bias, v_gla_w_gate_up, v_gla_b_gate, v_gla_out_norm, v_mem_q_norm, v_mem_k_norm, v_w_out, v_ffn2_norm, v_ffn2_w_gate, v_ffn2_w_up, v_ffn2_w_down):
    xi, yi, ci = lax.axis_index("x"), lax.axis_index("y"), lax.axis_index("c")
    c_arr = jnp.reshape(ci, (1,)).astype(jnp.int32)
    xy_arr = jnp.reshape(2 * xi + yi, (1,)).astype(jnp.int32)
    d = x.shape[-1]

    half_h = ffn1_w_gate.shape[-1] // 2

    def ffn_shards(wg_s, wu_s, wd_s):
        w3_s = jnp.concatenate([wg_s.transpose(0, 2, 1), wu_s.transpose(0, 2, 1), wd_s], axis=0)
        return jnp.pad(w3_s.reshape(3, 2, half_h, d), ((0, 0), (0, 0), (0, FFN_HALF_PAD - half_h), (0, 0))).astype(BF16)

    def gather_ffn(wg_s, wu_s, wd_s, name, collective_id, after):
        w3_s, _ = lax.optimization_barrier((ffn_shards(wg_s, wu_s, wd_s), after))
        return _all_gather([w3_s], ["row"], name=name, collective_id=collective_id)[0].reshape(3, -1, d)

    w3_s = ffn_shards(ffn1_w_gate, ffn1_w_up, ffn1_w_down)
    w3_1a = _all_gather([w3_s[:, 0]], ["row"], name="gather_ffn1a", collective_id=0)[0]
    w3_s, _ = lax.optimization_barrier((w3_s, w3_1a))
    w3_1b = _all_gather([w3_s[:, 1]], ["row"], name="gather_ffn1b", collective_id=11)[0]
    mix_s = lax.optimization_barrier((w_in[0].astype(BF16), w_mem_kv[0].astype(BF16), w_out[0].astype(BF16), w3_1b))[:3]
    win_all, wkv, wout = _all_gather(list(mix_s), ["stack", "row", "row"], name="gather_mix", collective_id=1)

    def gather_ffn2(x1):
        return gather_ffn(ffn2_w_gate, ffn2_w_up, ffn2_w_down, "gather_ffn2", 2, (wout, x1))

    win_p = _pack_win(win_all, tr=256, name="pack_w_in")

    small_w = [ffn1_norm, mix_norm, mem_norm, ffn2_norm, swa_q_norm, swa_k_norm, swa_sinks[0], rel_bias,
               gla_w_gate_up[0], gla_b_gate, gla_out_norm, mem_q_norm, mem_k_norm]
    collective_ids = {"ffn2": (3, 4), "mix": (5, 6), "ffn1a": (7, 8), "ffn1b": (9, 10)}
    reduced, small_box = {}, {}

    def on_grads(group, grads, carry, small=None):
        if group == "mix":
            dwin_p, dwkv, dwout = grads
            grads = [_unpack_win(dwin_p, tr=256, name="unpack_dw_in"), dwkv, dwout]
            kinds = ["stack", "row", "row"]
        else:
            kinds = ["row"]
        if reduced:
            earlier = list(reduced.values())[-1][1]
            *grads, _ = lax.optimization_barrier((*grads, earlier[0]))
        id_pair, id_chip = collective_ids[group]
        from_sibling = _pair_exchange(grads, kinds, name=f"pair_exchange_{group}", collective_id=id_pair)
        chip_sums = [_pair_sum(g, theirs, k, c_arr, name=f"pair_sum_{group}_{t}")
                     for t, (g, theirs, k) in enumerate(zip(grads, from_sibling, kinds))]
        if carry is not None:
            *chip_sums, carry = lax.optimization_barrier((*chip_sums, carry))
        if small is None:
            from_chips = _chip_exchange(chip_sums, None, name=f"chip_exchange_{group}", collective_id=id_chip)
        else:
            packed = _pack_small(small, name=f"pack_small_{group}")
            *from_chips, small_all = _chip_exchange(chip_sums, packed, name=f"chip_exchange_{group}",
                                                    collective_id=id_chip)
            small_box[group] = small_all
        reduced[group] = (chip_sums, from_chips)
        return carry

    grad_x = _local_step(x[0], mem[0], loss_target[0], small_w, ((w3_1a, w3_1b), win_p, wkv, wout, gather_ffn2),
                         on_grads)

    big_w = {"ffn1_w_gate": ("ffn1", 0, 0, True, ffn1_w_gate, m_ffn1_w_gate, v_ffn1_w_gate),
             "ffn1_w_up": ("ffn1", 0, 1, True, ffn1_w_up, m_ffn1_w_up, v_ffn1_w_up),
             "ffn1_w_down": ("ffn1", 0, 2, False, ffn1_w_down, m_ffn1_w_down, v_ffn1_w_down),
             "w_in": ("mix", 0, 0, False, w_in, m_w_in, v_w_in),
             "w_mem_kv": ("mix", 1, 0, False, w_mem_kv, m_w_mem_kv, v_w_mem_kv),
             "w_out": ("mix", 2, 0, False, w_out, m_w_out, v_w_out),
             "ffn2_w_gate": ("ffn2", 0, 0, True, ffn2_w_gate, m_ffn2_w_gate, v_ffn2_w_gate),
             "ffn2_w_up": ("ffn2", 0, 1, True, ffn2_w_up, m_ffn2_w_up, v_ffn2_w_up),
             "ffn2_w_down": ("ffn2", 0, 2, False, ffn2_w_down, m_ffn2_w_down, v_ffn2_w_down)}
    res = {}
    for nm, (group, t, mat, transposed, w, m, v) in big_w.items():
        shape = w.shape
        if transposed:
            w, m, v = (a.transpose(0, 2, 1) for a in (w, m, v))
        if group != "mix":
            w, m, v = (a.reshape(2, half_h, d) for a in (w, m, v))
        r = w.shape[1]
        tr = 256 if r % 256 == 0 else r
        halves = ["ffn1a", "ffn1b"] if group == "ffn1" else [group]
        out = _adam_big([reduced[k][0][t] for k in halves], [reduced[k][1][t] for k in halves], mat, xy_arr, w, m, v,
                        tr=tr, name=f"adam_{nm}")
        if transposed:
            out = [a.reshape(1, -1, d).transpose(0, 2, 1) for a in out]
        res[nm] = [a.reshape(shape) for a in out]
    small_names = ["ffn1_norm", "mix_norm", "mem_norm", "ffn2_norm", "swa_q_norm", "swa_k_norm", "swa_sinks", "rel_bias",
                   "gla_w_gate_up", "gla_b_gate", "gla_out_norm", "mem_q_norm", "mem_k_norm"]
    small_m = [m_ffn1_norm, m_mix_norm, m_mem_norm, m_ffn2_norm, m_swa_q_norm, m_swa_k_norm, m_swa_sinks, m_rel_bias,
               m_gla_w_gate_up, m_gla_b_gate, m_gla_out_norm, m_mem_q_norm, m_mem_k_norm]
    small_v = [v_ffn1_norm, v_mix_norm, v_mem_norm, v_ffn2_norm, v_swa_q_norm, v_swa_k_norm, v_swa_sinks, v_rel_bias,
               v_gla_w_gate_up, v_gla_b_gate, v_gla_out_norm, v_mem_q_norm, v_mem_k_norm]
    small_full = [ffn1_norm, mix_norm, mem_norm, ffn2_norm, swa_q_norm, swa_k_norm, swa_sinks, rel_bias,
                  gla_w_gate_up, gla_b_gate, gla_out_norm, mem_q_norm, mem_k_norm]
    zero = jnp.zeros((1, 1), F32)
    two_d = lambda a: a.reshape(a.shape[-2:])
    for group, sel in (("ffn1a", slice(1, None)), ("ffn1b", slice(0, 1))):
        extra = [zero] if group == "ffn1a" else []
        ws, ms, vs = ([two_d(a) for a in arrs[sel]] + extra for arrs in (small_full, small_m, small_v))
        updated = _adam_small(small_box[group], ws, ms, vs, name=f"adam_small_{group}")
        for nm, full, out in zip(small_names[sel], small_full[sel], updated):
            res[nm] = [a.reshape(full.shape) for a in out]
        if extra:
            loss = updated[-1][0].reshape(())

    order = ["ffn1_norm", "ffn1_w_gate", "ffn1_w_up", "ffn1_w_down", "mix_norm", "mem_norm", "w_in", "w_mem_kv",
             "swa_q_norm", "swa_k_norm", "swa_sinks", "rel_bias", "gla_w_gate_up", "gla_b_gate", "gla_out_norm",
             "mem_q_norm", "mem_k_norm", "w_out", "ffn2_norm", "ffn2_w_gate", "ffn2_w_up", "ffn2_w_down"]
    outs = [loss, grad_x[None]]
    for q in range(4):
        outs += [res[nm][q] for nm in order]
    return tuple(outs)
```

```python
import functools
import math

import numpy as np
import jax
import jax.numpy as jnp
from jax import lax
from jax.experimental import pallas as pl
from jax.experimental.pallas import tpu as pltpu
from jax.experimental.pallas import tpu_sc as plsc

F32 = jnp.float32
BF16 = jnp.bfloat16
SDS = jax.ShapeDtypeStruct

EPS = 1e-6
HEAD_DIM = 64
SWA_HEADS = 8
SWA_KV_HEADS = 2
SWA_GROUP = SWA_HEADS // SWA_KV_HEADS
BLOCK = 128
N_BUCKETS = 32
MAX_DISTANCE = 128
GLA_HEADS = 4
GLA_DK = 32
GLA_DV = 64
GLA_RANK = 16
GLA_TAU = 16.0
GLA_CHUNK = 32
MEM_HEADS = 4
SWA_Q_W = SWA_HEADS * HEAD_DIM
SWA_KV_W = SWA_KV_HEADS * HEAD_DIM
GLA_QK_W = GLA_HEADS * GLA_DK
GLA_V_W = GLA_HEADS * GLA_DV
MEM_Q_W = MEM_HEADS * HEAD_DIM
IN_W = 1808
IN_W_PAD = 1920
COL_SQ, COL_SKV, COL_GQ, COL_GK, COL_GV, COL_GG, COL_MQ, COL_GLR = 0, 512, 768, 896, 1024, 1280, 1536, 1792

ADAM_LR = 0.001
ADAM_B1 = 0.9
ADAM_B2 = 0.999
ADAM_EPS = 1e-08
ADAM_WD = 0.01
ADAM_STEP = 10

N_DEV = 8
VMEM_LIMIT_BYTES = 56 * 1024 * 1024
MESH = pl.DeviceIdType.MESH


def _params(*sem):
    return pltpu.CompilerParams(dimension_semantics=sem or None, vmem_limit_bytes=VMEM_LIMIT_BYTES)


def _dot(a, b, ta, tb, precision=None):
    dims = (((0 if ta else 1,), (1 if tb else 0,)), ((), ()))
    return lax.dot_general(a, b, dims, preferred_element_type=F32, precision=precision)


def _mm_raw(a, b, ta=False, tb=False):
    return _dot(a.astype(BF16), b.astype(BF16), ta, tb)


def _mmf_raw(a, b, ta=False, tb=False):
    return _dot(a, b, ta, tb, lax.Precision.HIGHEST)


def _make_mm(raw):
    @functools.partial(jax.custom_vjp, nondiff_argnums=(2, 3))
    def mm(a, b, ta=False, tb=False):
        return raw(a, b, ta, tb)

    def fwd(a, b, ta, tb):
        return raw(a, b, ta, tb), (a, b)

    def bwd(ta, tb, res, g):
        a, b = res
        da = raw(b, g, tb, True) if ta else raw(g, b, False, not tb)
        db = raw(g, a, True, ta) if tb else raw(a, g, not ta, False)
        return da, db

    mm.defvjp(fwd, bwd)
    return mm


_mm = _make_mm(_mm_raw)
_mmf = _make_mm(_mmf_raw)


def _mm3(a, b, ta=False, tb=False):
    a_hi, b_hi = a.astype(BF16).astype(F32), b.astype(BF16).astype(F32)
    return _mm(a_hi, b_hi, ta, tb) + _mm(a_hi, b - b_hi, ta, tb) + _mm(a - a_hi, b_hi, ta, tb)


def _rms(x, g):
    return x * lax.rsqrt(jnp.mean(x * x, axis=-1, keepdims=True) + EPS) * g


def _silu_mul(g, u):
    return jax.nn.silu(g) * u


def _log_sigmoid(z):
    return jnp.minimum(z, 0.0) - jnp.log(1.0 + jnp.exp(-jnp.abs(z)))


def _matmul(a_list, b, *, ta=False, tb=False, tm, tn, b_blocks=None, res=None, scale=1.0, out_dtype=F32, name):
    if not isinstance(a_list, (list, tuple)):
        a_list = [a_list]
    n_a = len(a_list)
    m = a_list[0].shape[1] if ta else a_list[0].shape[0]
    ks = [a.shape[0] if ta else a.shape[1] for a in a_list]
    n = b.shape[0] if tb else b.shape[1]
    if b_blocks is None:
        assert n_a == 1
        b_blocks = [0]
    tm, tn = min(tm, m), min(tn, n)
    assert m % tm == 0 and n % tn == 0, (m, n, tm, tn)

    def body(*refs):
        a_refs, b_refs = refs[:n_a], refs[n_a:2 * n_a]
        r_ref = refs[2 * n_a] if res is not None else None
        o_ref = refs[-1]
        acc = _mm_raw(a_refs[0][...], b_refs[0][...], ta, tb)
        for k in range(1, n_a):
            acc = acc + _mm_raw(a_refs[k][...], b_refs[k][...], ta, tb)
        if scale != 1.0:
            acc = acc * scale
        if r_ref is not None:
            acc = r_ref[...] + acc
        o_ref[...] = acc.astype(out_dtype)

    in_specs = []
    for k in ks:
        in_specs.append(pl.BlockSpec((k, tm), lambda i, j: (0, i)) if ta else pl.BlockSpec((tm, k), lambda i, j: (i, 0)))
    for k, blk in zip(ks, b_blocks):
        if tb:
            in_specs.append(pl.BlockSpec((tn, k), functools.partial(lambda i, j, blk: (j, blk), blk=blk)))
        else:
            in_specs.append(pl.BlockSpec((k, tn), functools.partial(lambda i, j, blk: (blk, j), blk=blk)))
    args = list(a_list) + [b] * n_a
    if res is not None:
        in_specs.append(pl.BlockSpec((tm, tn), lambda i, j: (i, j)))
        args.append(res)
    return pl.pallas_call(
        body, name=name, grid=(m // tm, n // tn), in_specs=in_specs,
        out_specs=pl.BlockSpec((tm, tn), lambda i, j: (i, j)), out_shape=SDS((m, n), out_dtype),
        compiler_params=_params("parallel", "parallel"),
    )(*args)


def _win_pieces(w):
    glr_lo, glr_hi = COL_MQ, COL_MQ + GLA_RANK
    out = []
    for j in range(N_DEV):
        for lo, hi, shift in ((0, glr_lo, 0), (glr_lo, glr_hi, COL_GLR - glr_lo), (glr_hi, IN_W, COL_MQ - glr_hi)):
            s, e = max(j * w, lo), min((j + 1) * w, hi)
            if s < e:
                out.append((j, s - j * w, e - j * w, s + shift))
    return out


def _pack_win(win_all, *, tr, name):
    _, d, w = win_all.shape

    def body(i_ref, o_ref):
        for j, a, b, dst in _win_pieces(w):
            o_ref[:, dst:dst + b - a] = i_ref[j][:, a:b]
        o_ref[:, IN_W:] = jnp.zeros((tr, IN_W_PAD - IN_W), o_ref.dtype)

    return pl.pallas_call(
        body, name=name, grid=(d // tr,), in_specs=[pl.BlockSpec((N_DEV, tr, w), lambda i: (0, i, 0))],
        out_specs=pl.BlockSpec((tr, IN_W_PAD), lambda i: (i, 0)), out_shape=SDS((d, IN_W_PAD), win_all.dtype),
        compiler_params=_params("parallel"),
    )(win_all)


def _unpack_win(dwin_p, *, tr, name):
    d = dwin_p.shape[0]
    w = IN_W // N_DEV

    def body(i_ref, o_ref):
        for j, a, b, src in _win_pieces(w):
            o_ref[j % 2, j // 2, :, a:b] = i_ref[:, src:src + b - a]

    return pl.pallas_call(
        body, name=name, grid=(d // tr,), in_specs=[pl.BlockSpec((tr, IN_W_PAD), lambda i: (i, 0))],
        out_specs=pl.BlockSpec((2, 4, tr, w), lambda i: (0, 0, i, 0)), out_shape=SDS((2, 4, d, w), dwin_p.dtype),
        compiler_params=_params("parallel"),
    )(dwin_p)


def _rms_fwd(x, g, *, tm, name):
    s, d = x.shape

    def body(x_ref, g_ref, h_ref):
        h_ref[...] = _rms(x_ref[...], g_ref[...]).astype(BF16)

    return pl.pallas_call(
        body, name=name, grid=(s // tm,),
        in_specs=[pl.BlockSpec((tm, d), lambda i: (i, 0)), pl.BlockSpec((1, d), lambda i: (0, 0))],
        out_specs=pl.BlockSpec((tm, d), lambda i: (i, 0)), out_shape=SDS((s, d), BF16),
        compiler_params=_params("parallel"),
    )(x, g)


def _rms_bwd(x, g, dh, dres, *, tm, name):
    s, d = x.shape
    want_dx = dres is not None

    def body(*refs):
        if want_dx:
            x_ref, g_ref, dh_ref, dres_ref, dx_ref, dxb_ref, dg_ref = refs
        else:
            x_ref, g_ref, dh_ref, dg_ref = refs
        _, vjp = jax.vjp(_rms, x_ref[...], g_ref[...])
        dx, dg = vjp(dh_ref[...])
        if want_dx:
            dx = dres_ref[...] + dx
            dx_ref[...] = dx
            dxb_ref[...] = dx.astype(BF16)

        @pl.when(pl.program_id(0) == 0)
        def _():
            dg_ref[...] = jnp.zeros_like(dg_ref)

        dg_ref[...] += dg

    row = pl.BlockSpec((tm, d), lambda i: (i, 0))
    vec = pl.BlockSpec((1, d), lambda i: (0, 0))
    if want_dx:
        return pl.pallas_call(
            body, name=name, grid=(s // tm,), in_specs=[row, vec, row, row], out_specs=[row, row, vec],
            out_shape=[SDS((s, d), F32), SDS((s, d), BF16), SDS((1, d), F32)], compiler_params=_params("arbitrary"),
        )(x, g, dh, dres)
    return None, None, pl.pallas_call(
        body, name=name, grid=(s // tm,), in_specs=[row, vec, row], out_specs=vec,
        out_shape=SDS((1, d), F32), compiler_params=_params("arbitrary"),
    )(x, g, dh)


FFN_TN = 256
FFN_TN_FWD = 512
FFN_HALF_PAD = 192


def _ffn_fwd(x, gain, w3, tag, *, tm=1024, next_gain=None, target=None, start=None, partial=False):
    s, d = x.shape
    f = w3.shape[1]
    tn = FFN_TN_FWD
    nj = f // tn
    tm = min(tm, s)
    n_extra = (next_gain is not None) + (target is not None) + 2 * (start is not None)

    def body(*refs):
        x_ref, gain_ref, wg_ref, wu_ref, wd_ref = refs[:5]
        extra, outs = refs[5:5 + n_extra], refs[5 + n_extra:-1]
        acc_s = refs[-1]
        g_ref, u_ref = outs[-2:]
        h_ref = extra[-2] if start is not None else outs[-3]
        i, j = pl.program_id(0), pl.program_id(1)

        @pl.when(j == 0)
        def _():
            if start is None:
                h_ref[...] = _rms(x_ref[...], gain_ref[...]).astype(BF16)
                acc_s[...] = jnp.zeros_like(acc_s)
            else:
                acc_s[...] = extra[-1][...]

        hv = h_ref[...]
        g = _mm_raw(hv, wg_ref[...], False, True)
        u = _mm_raw(hv, wu_ref[...], False, True)
        g_ref[...] = g.astype(BF16)
        u_ref[...] = u.astype(BF16)
        acc_s[...] += _mm_raw(_silu_mul(g, u), wd_ref[...])

        @pl.when(j == nj - 1)
        def _():
            y = acc_s[...] if partial else x_ref[...] + 0.5 * acc_s[...]
            if target is None:
                outs[0][...] = y
                if next_gain is not None:
                    outs[1][...] = _rms(y, extra[0][...]).astype(BF16)
            else:
                dy_ref, dyb_ref, loss_ref = outs[:3]
                diff = y - extra[0][...]
                dy_ref[...] = diff * (1.0 / d)
                dyb_ref[...] = (diff * (1.0 / d)).astype(BF16)
                part = 0.5 * jnp.sum(jnp.mean(diff * diff, axis=-1, keepdims=True), axis=0, keepdims=True)

                @pl.when(i == 0)
                def _():
                    loss_ref[...] = part

                @pl.when(i > 0)
                def _():
                    loss_ref[...] += part

    row = pl.BlockSpec((tm, d), lambda i, j: (i, 0))
    vec = pl.BlockSpec((1, d), lambda i, j: (0, 0))
    tile = pl.BlockSpec((tm, tn), lambda i, j: (i, j))
    in_specs = [row, vec] + [pl.BlockSpec((None, tn, d), functools.partial(lambda i, j, k: (k, j, 0), k=k)) for k in range(3)]
    args = [x, gain, w3, w3, w3]
    if target is None:
        out_specs, out_shape = [row], [SDS((s, d), F32)]
        if next_gain is not None:
            in_specs.append(vec)
            args.append(next_gain)
            out_specs.append(row)
            out_shape.append(SDS((s, d), BF16))
    else:
        in_specs.append(row)
        args.append(target)
        out_specs = [row, row, pl.BlockSpec((1, 1), lambda i, j: (0, 0))]
        out_shape = [SDS((s, d), F32), SDS((s, d), BF16), SDS((1, 1), F32)]
    if start is None:
        out_specs.append(row)
        out_shape.append(SDS((s, d), BF16))
    else:
        in_specs += [row, row]
        args += list(start)
    *head, g, u = pl.pallas_call(
        body, name=f"{tag}_fwd", grid=(s // tm, nj), in_specs=in_specs,
        out_specs=out_specs + [tile, tile],
        out_shape=out_shape + [SDS((s, f), BF16), SDS((s, f), BF16)],
        scratch_shapes=[pltpu.VMEM((tm, d), F32)],
        compiler_params=_params("arbitrary", "arbitrary"),
    )(*args)
    if start is None:
        *head, h = head
    else:
        h = start[0]
    return head, (h, g, u)


def _ffn_bwd_part(dyb, w3, saved, first, count, dh_init, *, name):
    h, g, u = saved
    s, d = h.shape
    tn = FFN_TN

    def body(*refs):
        if dh_init is None:
            dy_ref, h_ref, wd_ref, wg_ref, wu_ref, g_ref, u_ref, dh_ref, dw3_ref, dg_s, du_s, a_s = refs
        else:
            dy_ref, h_ref, wd_ref, wg_ref, wu_ref, g_ref, u_ref, dh0_ref, dh_ref, dw3_ref, dg_s, du_s, a_s = refs
        j = pl.program_id(0)

        @pl.when(j == 0)
        def _():
            dh_ref[...] = jnp.zeros_like(dh_ref) if dh_init is None else dh0_ref[...]
            for ref in (dg_s, du_s, a_s):
                ref[...] = jnp.zeros_like(ref)

        now, before = j % 2, 1 - j % 2
        dyv = dy_ref[...]
        hv = h_ref[...]
        dg, du, a = dg_s[before], du_s[before], a_s[before]
        dh_ref[...] += _mm_raw(dg, wg_ref[...]) + _mm_raw(du, wu_ref[...])
        dw3_ref[0] = _mm_raw(dg, hv, True, False).astype(BF16)
        dw3_ref[1] = _mm_raw(du, hv, True, False).astype(BF16)
        dw3_ref[2] = (_mm_raw(a, dyv, True, False) * 0.5).astype(BF16)

        da = _mm_raw(dyv, wd_ref[...], False, True) * 0.5
        a, vjp = jax.vjp(_silu_mul, g_ref[...].astype(F32), u_ref[...].astype(F32))
        dg, du = vjp(da)
        dg_s[now] = dg.astype(BF16)
        du_s[now] = du.astype(BF16)
        a_s[now] = a.astype(BF16)

    this = lambda j: first + jnp.minimum(j, count - 1)
    last = lambda j: first + jnp.maximum(j - 1, 0)
    full = pl.BlockSpec((s, d), lambda j: (0, 0))
    once = pl.BlockSpec((s, d), lambda j: (0, 0), pipeline_mode=pl.Buffered(1))
    tile = pl.BlockSpec((s, tn), lambda j: (0, this(j)))
    in_specs = [once, once, pl.BlockSpec((None, tn, d), lambda j: (2, this(j), 0)),
                pl.BlockSpec((None, tn, d), lambda j: (0, last(j), 0)), pl.BlockSpec((None, tn, d), lambda j: (1, last(j), 0)),
                tile, tile]
    args = [dyb, h, w3, w3, w3, g, u]
    if dh_init is not None:
        in_specs.append(once)
        args.append(dh_init)
    return pl.pallas_call(
        body, name=name, grid=(count + 1,), in_specs=in_specs,
        out_specs=[full, pl.BlockSpec((3, tn, d), lambda j: (0, jnp.maximum(j - 1, 0), 0))],
        out_shape=[SDS((s, d), F32), SDS((3, count * tn, d), BF16)],
        scratch_shapes=[pltpu.VMEM((2, s, tn), BF16)] * 3,
        compiler_params=_params("arbitrary"),
    )(*args)


def _bucket_table():
    qi = np.arange(BLOCK)[:, None]
    kj = np.arange(2 * BLOCK)[None, :]
    dist = np.maximum(qi + BLOCK - kj, 0)
    max_exact = N_BUCKETS // 2
    d = np.maximum(dist, 1).astype(np.float32)
    large = max_exact + (np.log(d / np.float32(max_exact)) / np.float32(math.log(MAX_DISTANCE / max_exact))
                         * np.float32(N_BUCKETS - max_exact)).astype(np.int32)
    large = np.minimum(large, N_BUCKETS - 1)
    band = np.where(dist < max_exact, dist, large).astype(np.int32)
    return np.where(np.tril(np.ones((BLOCK, BLOCK), bool)), band[:, BLOCK:], band[:, :BLOCK])


SWA_STACK = SWA_GROUP * BLOCK


def _swa_masks(n):
    qi = lax.broadcasted_iota(jnp.int32, (SWA_STACK, BLOCK), 0) % BLOCK
    kj = lax.broadcasted_iota(jnp.int32, (SWA_STACK, BLOCK), 1)
    own = kj <= qi
    return own, own | (n > 0)


def _swa_group(q, kp, kc, vp, vc, qg, kg, sink, bias, own, valid):
    qn = _rms(q, qg)
    s = jnp.where(own, _mm(qn, _rms(kc, kg), False, True), _mm(qn, _rms(kp, kg), False, True))
    s = s * (HEAD_DIM ** -0.5) + bias
    s = jnp.where(valid, s, -jnp.inf)
    m = lax.stop_gradient(jnp.maximum(jnp.max(s, axis=-1, keepdims=True), sink))
    p = jnp.exp(s - m)
    p = p / (jnp.sum(p, axis=-1, keepdims=True) + jnp.exp(sink - m))
    return _mm(jnp.where(own, p, 0.0), vc) + _mm(jnp.where(own, 0.0, p), vp)


def _swa_bias_table(rb_ref, bucket, bias_s):
    for h in range(SWA_HEADS):
        acc = jnp.zeros((BLOCK, BLOCK), F32)
        for b in range(N_BUCKETS):
            acc = jnp.where(bucket == b, rb_ref[b, h], acc)
        bias_s[h // SWA_GROUP, (h % SWA_GROUP) * BLOCK:(h % SWA_GROUP + 1) * BLOCK, :] = acc


def _swa_stack(ref, g):
    return jnp.concatenate([ref[:, (g * SWA_GROUP + hh) * HEAD_DIM:(g * SWA_GROUP + hh + 1) * HEAD_DIM]
                            for hh in range(SWA_GROUP)], axis=0)


def _swa_unstack(ref, g, stacked):
    for hh in range(SWA_GROUP):
        h = g * SWA_GROUP + hh
        ref[:, h * HEAD_DIM:(h + 1) * HEAD_DIM] = stacked[hh * BLOCK:(hh + 1) * BLOCK]


def _swa_sink_column(sink_ref, g):
    head = lax.broadcasted_iota(jnp.int32, (SWA_STACK, 1), 0) // BLOCK
    col = jnp.zeros((SWA_STACK, 1), F32)
    for hh in range(SWA_GROUP):
        col = jnp.where(head == hh, sink_ref[g * SWA_GROUP + hh], col)
    return col


def _swa_band(kvp_ref, kvc_ref, g):
    k = slice(g * HEAD_DIM, (g + 1) * HEAD_DIM)
    v = slice(SWA_KV_W + g * HEAD_DIM, SWA_KV_W + (g + 1) * HEAD_DIM)
    return kvp_ref[:, k], kvc_ref[:, k], kvp_ref[:, v], kvc_ref[:, v]


def _swa_specs(order):
    kvc = COL_SKV // (2 * SWA_KV_W)
    return [
        pl.BlockSpec((BLOCK, SWA_Q_W), lambda t: (order(t), 0)),
        pl.BlockSpec((BLOCK, 2 * SWA_KV_W), lambda t: (jnp.maximum(order(t) - 1, 0), kvc)),
        pl.BlockSpec((BLOCK, 2 * SWA_KV_W), lambda t: (order(t), kvc)),
        pl.BlockSpec((1, HEAD_DIM), lambda t: (0, 0)),
        pl.BlockSpec((1, HEAD_DIM), lambda t: (0, 0)),
        pl.BlockSpec(memory_space=pltpu.SMEM),
        pl.BlockSpec(memory_space=pltpu.SMEM),
        pl.BlockSpec((BLOCK, BLOCK), lambda t: (0, 0)),
    ]


def _swa_fwd(p, qg, kg, sinks, rel_bias, *, name):
    s = p.shape[0]
    nb = s // BLOCK

    def body(q_ref, kvp_ref, kvc_ref, qg_ref, kg_ref, sink_ref, rb_ref, bucket_ref, y_ref, bias_s):
        n = pl.program_id(0)

        @pl.when(n == 0)
        def _():
            _swa_bias_table(rb_ref, bucket_ref[...], bias_s)

        own, valid = _swa_masks(n)
        for g in range(SWA_KV_HEADS):
            out = _swa_group(_swa_stack(q_ref, g), *_swa_band(kvp_ref, kvc_ref, g), qg_ref[...], kg_ref[...],
                             _swa_sink_column(sink_ref, g), bias_s[g], own, valid)
            _swa_unstack(y_ref, g, out)

    return pl.pallas_call(
        body, name=name, grid=(nb,), in_specs=_swa_specs(lambda t: t),
        out_specs=pl.BlockSpec((BLOCK, SWA_Q_W), lambda t: (t, 0)), out_shape=SDS((s, SWA_Q_W), F32),
        scratch_shapes=[pltpu.VMEM((SWA_KV_HEADS, SWA_STACK, BLOCK), F32)],
        compiler_params=_params("arbitrary"),
    )(p, p, p, qg, kg, sinks, rel_bias, jnp.asarray(_bucket_table()))


def _swa_bwd(p, qg, kg, sinks, rel_bias, dy_all, *, name):
    s = p.shape[0]
    nb = s // BLOCK

    def body(q_ref, kvp_ref, kvc_ref, qg_ref, kg_ref, sink_ref, rb_ref, bucket_ref, dy_ref,
             dq_ref, dkv_ref, dqg_ref, dkg_ref, dsink_ref, drb_ref, bias_s, dbias_s, carry_s):
        t = pl.program_id(0)
        n = nb - 1 - t

        @pl.when(t == 0)
        def _():
            _swa_bias_table(rb_ref, bucket_ref[...], bias_s)
            dbias_s[...] = jnp.zeros_like(dbias_s)
            carry_s[...] = jnp.zeros_like(carry_s)
            dqg_ref[...] = jnp.zeros_like(dqg_ref)
            dkg_ref[...] = jnp.zeros_like(dkg_ref)
            dsink_ref[...] = jnp.zeros_like(dsink_ref)
            drb_ref[...] = jnp.zeros_like(drb_ref)

        own, valid = _swa_masks(n)
        lane = lax.broadcasted_iota(jnp.int32, (1, BLOCK), 1)
        dqg = jnp.zeros((1, HEAD_DIM), F32)
        dkg = jnp.zeros((1, HEAD_DIM), F32)
        dsink_vec = jnp.zeros((1, BLOCK), F32)
        for g in range(SWA_KV_HEADS):
            _, vjp = jax.vjp(functools.partial(_swa_group, own=own, valid=valid), _swa_stack(q_ref, g),
                             *_swa_band(kvp_ref, kvc_ref, g), qg_ref[...], kg_ref[...], _swa_sink_column(sink_ref, g),
                             bias_s[g])
            dq, dkp, dkc, dvp, dvc, dqg_g, dkg_g, dsink_col, dbias = vjp(_swa_stack(dy_ref, g))
            _swa_unstack(dq_ref, g, dq)
            dqg += dqg_g
            dkg += dkg_g
            dbias_s[g] += dbias
            for hh in range(SWA_GROUP):
                dsink_h = jnp.sum(dsink_col[hh * BLOCK:(hh + 1) * BLOCK], axis=0, keepdims=True)
                dsink_vec += jnp.where(lane == g * SWA_GROUP + hh, dsink_h, 0.0)
            lo = g * HEAD_DIM
            dkv_ref[:, lo:lo + HEAD_DIM] = dkc + carry_s[g]
            carry_s[g] = dkp
            lo += SWA_KV_W
            dkv_ref[:, lo:lo + HEAD_DIM] = dvc + carry_s[SWA_KV_HEADS + g]
            carry_s[SWA_KV_HEADS + g] = dvp
        dqg_ref[...] += dqg
        dkg_ref[...] += dkg
        dsink_ref[...] += dsink_vec

        @pl.when(t == nb - 1)
        def _():
            bucket = bucket_ref[...]
            row = lax.broadcasted_iota(jnp.int32, (N_BUCKETS, BLOCK), 0)
            col = lax.broadcasted_iota(jnp.int32, (N_BUCKETS, BLOCK), 1)
            acc = jnp.zeros((N_BUCKETS, BLOCK), F32)
            for h in range(SWA_HEADS):
                dbias = dbias_s[h // SWA_GROUP, (h % SWA_GROUP) * BLOCK:(h % SWA_GROUP + 1) * BLOCK, :]
                for b in range(N_BUCKETS):
                    part = jnp.sum(jnp.where(bucket == b, dbias, 0.0), axis=1, keepdims=True)
                    val = jnp.sum(part, axis=0, keepdims=True)
                    acc = acc + jnp.where((row == b) & (col == h), val, 0.0)
            drb_ref[...] = acc

    order = lambda t: nb - 1 - t
    vec = pl.BlockSpec((1, HEAD_DIM), lambda t: (0, 0))
    return pl.pallas_call(
        body, name=name, grid=(nb,),
        in_specs=_swa_specs(order) + [pl.BlockSpec((BLOCK, SWA_Q_W), lambda t: (order(t), 0))],
        out_specs=[pl.BlockSpec((BLOCK, SWA_Q_W), lambda t: (order(t), 0)),
                   pl.BlockSpec((BLOCK, 2 * SWA_KV_W), lambda t: (order(t), 0)),
                   vec, vec, pl.BlockSpec((1, BLOCK), lambda t: (0, 0)),
                   pl.BlockSpec((N_BUCKETS, BLOCK), lambda t: (0, 0))],
        out_shape=[SDS((s, SWA_Q_W), F32), SDS((s, 2 * SWA_KV_W), F32), SDS((1, HEAD_DIM), F32),
                   SDS((1, HEAD_DIM), F32), SDS((1, BLOCK), F32), SDS((N_BUCKETS, BLOCK), F32)],
        scratch_shapes=[pltpu.VMEM((SWA_KV_HEADS, SWA_STACK, BLOCK), F32),
                        pltpu.VMEM((SWA_KV_HEADS, SWA_STACK, BLOCK), F32),
                        pltpu.VMEM((2 * SWA_KV_HEADS, BLOCK, HEAD_DIM), F32)],
        compiler_params=_params("arbitrary"),
    )(p, p, p, qg, kg, sinks, rel_bias, jnp.asarray(_bucket_table()), dy_all)


def _mem_head(q, k, v, qg, kg):
    qn = _rms(q, qg)
    kn = _rms(k, kg)
    s = _mm(qn, kn, False, True) * (HEAD_DIM ** -0.5)
    m = lax.stop_gradient(jnp.max(s, axis=-1, keepdims=True))
    e = jnp.exp(s - m)
    return _mm(e / jnp.sum(e, axis=-1, keepdims=True), v)


def _mem_fwd(p, kv, qg, kg, *, tq, name):
    s = p.shape[0]
    m = kv.shape[0]

    def body(q_ref, kv_ref, qg_ref, kg_ref, y_ref):
        for h in range(MEM_HEADS):
            cols = slice(h * HEAD_DIM, (h + 1) * HEAD_DIM)
            vcols = slice(MEM_Q_W + h * HEAD_DIM, MEM_Q_W + (h + 1) * HEAD_DIM)
            y_ref[:, cols] = _mem_head(q_ref[:, cols], kv_ref[:, cols], kv_ref[:, vcols], qg_ref[...], kg_ref[...])

    vec = pl.BlockSpec((1, HEAD_DIM), lambda t: (0, 0))
    return pl.pallas_call(
        body, name=name, grid=(s // tq,),
        in_specs=[pl.BlockSpec((tq, MEM_Q_W), lambda t: (t, COL_MQ // MEM_Q_W)),
                  pl.BlockSpec((m, 2 * MEM_Q_W), lambda t: (0, 0)), vec, vec],
        out_specs=pl.BlockSpec((tq, MEM_Q_W), lambda t: (t, 0)), out_shape=SDS((s, MEM_Q_W), F32),
        compiler_params=_params("parallel"),
    )(p, kv, qg, kg)


def _mem_bwd(p, kv, qg, kg, dy_all, *, tq, name):
    s = p.shape[0]
    m = kv.shape[0]

    def body(q_ref, kv_ref, qg_ref, kg_ref, dy_ref, dq_ref, dkv_ref, dqg_ref, dkg_ref):
        @pl.when(pl.program_id(0) == 0)
        def _():
            dkv_ref[...] = jnp.zeros_like(dkv_ref)
            dqg_ref[...] = jnp.zeros_like(dqg_ref)
            dkg_ref[...] = jnp.zeros_like(dkg_ref)

        dqg = jnp.zeros((1, HEAD_DIM), F32)
        dkg = jnp.zeros((1, HEAD_DIM), F32)
        for h in range(MEM_HEADS):
            cols = slice(h * HEAD_DIM, (h + 1) * HEAD_DIM)
            vcols = slice(MEM_Q_W + h * HEAD_DIM, MEM_Q_W + (h + 1) * HEAD_DIM)
            _, vjp = jax.vjp(_mem_head, q_ref[:, cols], kv_ref[:, cols], kv_ref[:, vcols], qg_ref[...], kg_ref[...])
            dq, dk, dv, dqg_h, dkg_h = vjp(dy_ref[:, cols])
            dq_ref[:, cols] = dq
            dkv_ref[:, cols] += dk
            dkv_ref[:, vcols] += dv
            dqg += dqg_h
            dkg += dkg_h
        dqg_ref[...] += dqg
        dkg_ref[...] += dkg

    vec = pl.BlockSpec((1, HEAD_DIM), lambda t: (0, 0))
    full = pl.BlockSpec((m, 2 * MEM_Q_W), lambda t: (0, 0))
    dy_col = (SWA_Q_W + GLA_V_W) // MEM_Q_W
    return pl.pallas_call(
        body, name=name, grid=(s // tq,),
        in_specs=[pl.BlockSpec((tq, MEM_Q_W), lambda t: (t, COL_MQ // MEM_Q_W)), full, vec, vec,
                  pl.BlockSpec((tq, MEM_Q_W), lambda t: (t, dy_col))],
        out_specs=[pl.BlockSpec((tq, MEM_Q_W), lambda t: (t, 0)), full, vec, vec],
        out_shape=[SDS((s, MEM_Q_W), F32), SDS((m, 2 * MEM_Q_W), F32), SDS((1, HEAD_DIM), F32), SDS((1, HEAD_DIM), F32)],
        compiler_params=_params("arbitrary"),
    )(p, kv, qg, kg, dy_all)


GLA_ROWS = 256


GLA_GROUP = 4


def _gla_consts():
    c, h, r = GLA_CHUNK, GLA_HEADS, GLA_GROUP * GLA_CHUNK
    i2 = lax.broadcasted_iota(jnp.int32, (c, c), 0)
    j2 = lax.broadcasted_iota(jnp.int32, (c, c), 1)
    slab_q = lax.broadcasted_iota(jnp.int32, (h, r, GLA_QK_W), 0)
    lane_q = lax.broadcasted_iota(jnp.int32, (h, r, GLA_QK_W), 2)
    row_a = lax.broadcasted_iota(jnp.int32, (h * r, r), 0) % r
    col_a = lax.broadcasted_iota(jnp.int32, (h * r, r), 1)
    slab_o = lax.broadcasted_iota(jnp.int32, (h, r, GLA_V_W), 0)
    lane_o = lax.broadcasted_iota(jnp.int32, (h, r, GLA_V_W), 2)
    row_s = lax.broadcasted_iota(jnp.int32, (GLA_V_W, GLA_QK_W), 0)
    col_s = lax.broadcasted_iota(jnp.int32, (GLA_V_W, GLA_QK_W), 1)
    return dict(
        ltri=(j2 <= i2).astype(F32),
        m_q=(slab_q == lane_q // GLA_DK).astype(F32),
        causal=(col_a <= row_a) & (col_a // c == row_a // c),
        m_o=(slab_o == lane_o // GLA_DV).astype(F32),
        m_s=(row_s // GLA_DV == col_s // GLA_DK).astype(F32),
    )


def _gla_step(q, k, v, z, bg, st, c):
    h = GLA_HEADS
    kt, qt, qe, decay = [], [], [], []
    for qc, kc, zc in zip(q, k, z):
        la = _log_sigmoid(zc + bg) * (1.0 / GLA_TAU)
        b = _mmf(c["ltri"], la)
        bl = jnp.sum(la, axis=0, keepdims=True)
        qs = qc * (GLA_DK ** -0.5)
        kt.append(kc * jnp.exp(bl - b))
        qt.append(qs * jnp.exp(b - bl))
        qe.append(qs * jnp.exp(b))
        decay.append(jnp.exp(bl))
    o_intra = []
    rows = GLA_GROUP * GLA_CHUNK
    for lo in range(0, len(q), GLA_GROUP):
        qt_all, kt_all, v_all = (jnp.concatenate(parts[lo:lo + GLA_GROUP], axis=0) for parts in (qt, kt, v))
        q_stack = (jnp.broadcast_to(qt_all[None], (h, rows, GLA_QK_W)) * c["m_q"]).reshape(h * rows, GLA_QK_W)
        a = jnp.where(c["causal"], _mm3(q_stack, kt_all, False, True), 0.0)
        o_stack = _mm(a, v_all)
        o_intra.append(jnp.sum(o_stack.reshape(h, rows, GLA_V_W) * c["m_o"], axis=0))
    o_intra = jnp.concatenate(o_intra, axis=0)
    o_inter = []
    for qec, ktc, vc, dc in zip(qe, kt, v, decay):
        o_inter.append(_mm(qec, st, False, True))
        st = st * dc + _mm(vc, ktc, True, False) * c["m_s"]
    return o_intra + jnp.concatenate(o_inter, axis=0), st


def _gla_post(o, gg, gain, g64):
    ms = _mmf(o * o, g64) * (1.0 / GLA_DV)
    return o * lax.rsqrt(ms + EPS) * gain * jax.nn.silu(gg)


def _gla_g64():
    r = lax.broadcasted_iota(jnp.int32, (GLA_V_W, GLA_V_W), 0)
    c = lax.broadcasted_iota(jnp.int32, (GLA_V_W, GLA_V_W), 1)
    return (r // GLA_DV == c // GLA_DV).astype(F32)


def _gla_in_specs(order):
    r = GLA_ROWS
    return [
        pl.BlockSpec((r, GLA_QK_W), lambda t: (order(t), COL_GQ // GLA_QK_W)),
        pl.BlockSpec((r, GLA_QK_W), lambda t: (order(t), COL_GK // GLA_QK_W)),
        pl.BlockSpec((r, GLA_V_W), lambda t: (order(t), COL_GV // GLA_V_W)),
        pl.BlockSpec((r, GLA_V_W), lambda t: (order(t), COL_GG // GLA_V_W)),
        pl.BlockSpec((r, GLA_QK_W), lambda t: (order(t), 0)),
        pl.BlockSpec((1, GLA_QK_W), lambda t: (0, 0)),
        pl.BlockSpec((1, GLA_V_W), lambda t: (0, 0)),
    ]


def _gla_pieces(q_ref, k_ref, v_ref, z_ref, cps):
    chunk = lambda ref: [ref[ci * GLA_CHUNK:(ci + 1) * GLA_CHUNK, :] for ci in range(cps)]
    return chunk(q_ref), chunk(k_ref), chunk(v_ref), chunk(z_ref)


def _gla_fwd(p, z, bg, gain, *, name):
    s = p.shape[0]
    r = GLA_ROWS
    cps = r // GLA_CHUNK

    def body(q_ref, k_ref, v_ref, gg_ref, z_ref, bg_ref, gain_ref, y_ref, oraw_ref, stsave_ref, st_s):
        @pl.when(pl.program_id(0) == 0)
        def _():
            st_s[...] = jnp.zeros_like(st_s)

        st = st_s[...]
        stsave_ref[0] = st
        o, st = _gla_step(*_gla_pieces(q_ref, k_ref, v_ref, z_ref, cps), bg_ref[...], st, _gla_consts())
        oraw_ref[...] = o
        st_s[...] = st
        y_ref[...] = _gla_post(o, gg_ref[...], gain_ref[...], _gla_g64())

    rowv = pl.BlockSpec((r, GLA_V_W), lambda t: (t, 0))
    return pl.pallas_call(
        body, name=name, grid=(s // r,), in_specs=_gla_in_specs(lambda t: t),
        out_specs=[rowv, rowv, pl.BlockSpec((1, GLA_V_W, GLA_QK_W), lambda t: (t, 0, 0))],
        out_shape=[SDS((s, GLA_V_W), F32), SDS((s, GLA_V_W), F32), SDS((s // r, GLA_V_W, GLA_QK_W), F32)],
        scratch_shapes=[pltpu.VMEM((GLA_V_W, GLA_QK_W), F32)],
        compiler_params=_params("arbitrary"),
    )(p, p, p, p, z, bg, gain)


def _gla_bwd(p, z, bg, gain, oraw, stsave, dy_all, *, name):
    s = p.shape[0]
    r = GLA_ROWS
    cps = r // GLA_CHUNK
    nsteps = s // r
    w_qkvg = 2 * GLA_QK_W + 2 * GLA_V_W

    def body(q_ref, k_ref, v_ref, gg_ref, z_ref, bg_ref, gain_ref, oraw_ref, stsave_ref, dy_ref,
             dqkvg_ref, dz_ref, dbg_ref, dgain_ref, dst_s):
        @pl.when(pl.program_id(0) == 0)
        def _():
            dst_s[...] = jnp.zeros_like(dst_s)
            dbg_ref[...] = jnp.zeros_like(dbg_ref)
            dgain_ref[...] = jnp.zeros_like(dgain_ref)

        _, vjp = jax.vjp(functools.partial(_gla_post, g64=_gla_g64()), oraw_ref[...], gg_ref[...], gain_ref[...])
        do, dgg, dgain = vjp(dy_ref[...])
        dqkvg_ref[:, 2 * GLA_QK_W + GLA_V_W:] = dgg
        dgain_ref[...] += dgain
        _, vjp = jax.vjp(functools.partial(_gla_step, c=_gla_consts()), *_gla_pieces(q_ref, k_ref, v_ref, z_ref, cps),
                         bg_ref[...], stsave_ref[0])
        dq, dk, dv, dz, dbg, dst = vjp((do, dst_s[...]))
        for ci in range(cps):
            rows = slice(ci * GLA_CHUNK, (ci + 1) * GLA_CHUNK)
            dqkvg_ref[rows, 0:GLA_QK_W] = dq[ci]
            dqkvg_ref[rows, GLA_QK_W:2 * GLA_QK_W] = dk[ci]
            dqkvg_ref[rows, 2 * GLA_QK_W:2 * GLA_QK_W + GLA_V_W] = dv[ci]
            dz_ref[rows, :] = dz[ci]
        dst_s[...] = dst
        dbg_ref[...] += dbg

    order = lambda t: nsteps - 1 - t
    rowv = pl.BlockSpec((r, GLA_V_W), lambda t: (order(t), 0))
    return pl.pallas_call(
        body, name=name, grid=(nsteps,),
        in_specs=_gla_in_specs(order) + [
            rowv, pl.BlockSpec((1, GLA_V_W, GLA_QK_W), lambda t: (order(t), 0, 0)),
            pl.BlockSpec((r, GLA_V_W), lambda t: (order(t), SWA_Q_W // GLA_V_W))],
        out_specs=[pl.BlockSpec((r, w_qkvg), lambda t: (order(t), 0)), pl.BlockSpec((r, GLA_QK_W), lambda t: (order(t), 0)),
                   pl.BlockSpec((1, GLA_QK_W), lambda t: (0, 0)), pl.BlockSpec((1, GLA_V_W), lambda t: (0, 0))],
        out_shape=[SDS((s, w_qkvg), F32), SDS((s, GLA_QK_W), F32), SDS((1, GLA_QK_W), F32), SDS((1, GLA_V_W), F32)],
        scratch_shapes=[pltpu.VMEM((GLA_V_W, GLA_QK_W), F32)],
        compiler_params=_params("arbitrary"),
    )(p, p, p, p, z, bg, gain, oraw, stsave, dy_all)


def _local_step(x, mem, target, small, big, on_grads):
    g1, gmix, gmem, g2, sqg, skg, sinks, rel_bias, wgu, bg, gla_gain, mqg, mkg = small
    (w3_1a, w3_1b), gather_mix, gather_ffn2 = big
    wgu_pad = jnp.zeros((GLA_QK_W, GLA_QK_W), BF16).at[:GLA_RANK].set(wgu.astype(BF16))
    gain256 = jnp.tile(gla_gain, (1, GLA_HEADS))

    (part,), saved1a = _ffn_fwd(x, g1, w3_1a, "ffn1a", partial=True)
    win_p, wkv, wout = gather_mix(part)
    (x1, h), saved1b = _ffn_fwd(x, g1, w3_1b, "ffn1b", next_gain=gmix, start=(saved1a[0], part))
    w3_2 = gather_ffn2((wout, x1))
    p = _matmul(h, win_p, tm=512, tn=IN_W_PAD, name="mix_in")
    hm = _rms_fwd(mem, gmem, tm=256, name="mem_rms")
    kv = _matmul(hm, wkv, tm=256, tn=512, name="mem_kv")
    p_glr = p[:, COL_GLR:]
    z = _matmul(p_glr, wgu_pad, tm=1024, tn=GLA_QK_W, name="gla_gate")
    y_swa = _swa_fwd(p, sqg, skg, sinks, rel_bias, name="swa_fwd")
    y_gla, oraw, stsave = _gla_fwd(p, z, bg, gain256, name="gla_fwd")
    y_mem = _mem_fwd(p, kv, mqg, mkg, tq=512, name="mem_fwd")
    x2 = _matmul([y_swa, y_gla, y_mem], wout, b_blocks=[0, 2, 3], tm=512, tn=1024, res=x1, name="mix_out")
    (dy, dyb, loss), saved2 = _ffn_fwd(x2, g2, w3_2, "ffn2", target=target)

    tiles = w3_2.shape[1] // FFN_TN
    dh2, dw3_2 = _ffn_bwd_part(dyb, w3_2, saved2, 0, tiles, None, name="ffn2_bwd")
    dx2, dx2b, dg2 = _rms_bwd(x2, g2, dh2, dy, tm=512, name="ffn2_drms")
    dx2b = on_grads("ffn2", [dw3_2.reshape(3, 2, -1, dw3_2.shape[-1])], dx2b)
    dy_all = _matmul(dx2b, wout, tb=True, tm=512, tn=1024, name="mix_dy")
    dwout = _matmul(jnp.concatenate([y_swa, y_gla, y_mem], axis=1), dx2b, ta=True, tm=512, tn=1024, out_dtype=BF16,
                    name="mix_dw_out")
    dq_swa, dkv_swa, dsqg, dskg, dsink, drb = _swa_bwd(p, sqg, skg, sinks, rel_bias, dy_all, name="swa_bwd")
    dqkvg, dz, dbg, dgain256 = _gla_bwd(p, z, bg, gain256, oraw, stsave, dy_all, name="gla_bwd")
    dmq, dkv_mem, dmqg, dmkg = _mem_bwd(p, kv, mqg, mkg, dy_all, tq=512, name="mem_bwd")
    dglr = _matmul(dz, wgu_pad, tb=True, tm=1024, tn=GLA_QK_W, name="gla_gate_dx")
    dwgu_pad = _matmul(p_glr, dz, ta=True, tm=GLA_QK_W, tn=GLA_QK_W, name="gla_gate_dw")
    dp = jnp.concatenate([dq_swa, dkv_swa, dqkvg, dmq, dglr], axis=1)
    dh = _matmul(dp, win_p, tb=True, tm=512, tn=1024, name="mix_dh")
    dwin_p = _matmul(h, dp, ta=True, tm=1024, tn=640, out_dtype=BF16, name="mix_dw_in")
    dx1, dx1b, dgmix = _rms_bwd(x1, gmix, dh, dx2, tm=512, name="mix_drms")
    dwkv = _matmul(hm, dkv_mem, ta=True, tm=512, tn=512, out_dtype=BF16, name="mem_dw_kv")
    dx1b = on_grads("mix", (dwin_p, dwkv, dwout), dx1b)
    dhm = _matmul(dkv_mem, wkv, tb=True, tm=256, tn=512, name="mem_dh")
    _, _, dgmem = _rms_bwd(mem, gmem, dhm, None, tm=256, name="mem_drms")
    dh1, dw3_1a = _ffn_bwd_part(dx1b, w3_1a, saved1a, 0, tiles // 2, None, name="ffn1_bwd_a")
    dgla_gain = dgain256.reshape(GLA_HEADS, GLA_DV).sum(axis=0, keepdims=True)
    dsmall = [dgmix, dgmem, dg2, dsqg, dskg, dsink[:, :SWA_HEADS], drb[:, :SWA_HEADS], dwgu_pad[:GLA_RANK], dbg,
              dgla_gain, dmqg, dmkg, loss]
    dh1, dgmem, dwgu_pad = on_grads("ffn1a", [dw3_1a[:, None]], (dh1, dgmem, dwgu_pad), small=dsmall)
    dh1, dw3_1b = _ffn_bwd_part(dx1b, w3_1b, saved1b, 0, tiles // 2, dh1, name="ffn1_bwd_b")
    dx, _, dg1 = _rms_bwd(x, g1, dh1, dx1, tm=512, name="ffn1_drms")
    on_grads("ffn1b", [dw3_1b[:, None]], None, small=[dg1])
    return dx


def _mesh_place():
    x, y, c = lax.axis_index("x"), lax.axis_index("y"), lax.axis_index("c")
    other_chips = [(1 - x, y), (x, 1 - y), (1 - x, 1 - y)]
    return x, y, c, other_chips


def _handshake(peers):
    barrier = pltpu.get_barrier_semaphore()
    for peer in peers:
        pl.semaphore_signal(barrier, inc=1, device_id=peer, device_id_type=MESH)
    pl.semaphore_wait(barrier, len(peers))


def _sequencer_call(body, operands, out_shapes, sems, *, name, collective_id):
    return pl.kernel(
        body, name=name, out_type=out_shapes, mesh=plsc.ScalarSubcoreMesh(axis_name="sequencer", num_cores=1),
        scratch_types=sems, compiler_params=pltpu.CompilerParams(collective_id=collective_id),
    )(*operands)


def _window(ref, kind, slot, shape):
    if kind == "row":
        rows = pl.ds(pl.multiple_of(slot * shape[-2], 8), shape[-2])
        return ref.at[(slice(None),) * (len(shape) - 2) + (rows,)]
    return ref.at[slot]


def _gathered(shape, kind):
    if kind == "row":
        return tuple(shape[:-2]) + (N_DEV * shape[-2], shape[-1])
    return (N_DEV,) + tuple(shape)


def _half(view, hf):
    if len(view.shape) == 4:
        return view.at[:, hf]
    n = view.shape[-2] // 2
    return view.at[(slice(None),) * (len(view.shape) - 2) + (pl.ds(hf * n, n),)]


def _all_gather(shards, kinds, *, name, collective_id):
    nt = len(shards)

    def body(*refs):
        x_refs, o_refs = refs[:nt], refs[nt:2 * nt]
        send_sems, recv_sems, local_sems = refs[2 * nt:]
        x, y, c, _ = _mesh_place()
        me, sibling, xn, yn, diag = (x, y, c), (x, y, 1 - c), (1 - x, y, c), (x, 1 - y, c), (1 - x, 1 - y, c)
        _handshake([sibling, xn, yn])

        def win(t, block):
            bx, by, bc = block
            return _window(o_refs[t], kinds[t], 4 * bx + 2 * by + bc, shards[t].shape)

        def copy(k, t, src, dst, to):
            return pltpu.make_async_remote_copy(src_ref=src, dst_ref=dst, send_sem=send_sems.at[k, t],
                                                recv_sem=recv_sems.at[k, t], device_id=to, device_id_type=MESH)

        def piece(k, t, block, hf, to, from_shard=False):
            dst = _half(win(t, block), hf)
            return copy(k, t, _half(x_refs[t], hf) if from_shard else dst, dst, to)

        mine = [pltpu.make_async_copy(x_refs[t], win(t, me), local_sems.at[t]) for t in range(nt)]
        sent = []

        def start(cp):
            cp.start()
            sent.append(cp)

        for cp in mine:
            cp.start()
        for t in range(nt):
            start(copy(0, t, x_refs[t], win(t, me), sibling))
        for hf_x, hf_y in ((0, 1), (1, 0)):
            for t in range(nt):
                start(piece(1 + hf_x, t, me, hf_x, xn, True))
                start(piece(3 + hf_y, t, me, hf_y, yn, True))
        for k, block, hf, onward, k_sib in ((1, xn, 0, (5, yn), 7), (4, yn, 1, (6, xn), 10), (2, xn, 1, None, 8),
                                           (3, yn, 0, None, 9), (5, diag, 0, None, 11), (6, diag, 1, None, 12)):
            for t in range(nt):
                piece(k, t, block, hf, me).wait_recv()
                if onward is not None:
                    start(piece(onward[0], t, block, hf, onward[1]))
                start(piece(k_sib, t, block, hf, sibling))
        for t in range(nt):
            copy(0, t, x_refs[t], win(t, sibling), me).wait_recv()
        for k_sib, block, hf in ((7, xn, 0), (10, yn, 1), (8, xn, 1), (9, yn, 0), (11, diag, 0), (12, diag, 1)):
            for t in range(nt):
                bx, by, _ = block
                piece(k_sib, t, (bx, by, 1 - c), hf, me).wait_recv()
        for cp in sent:
            cp.wait_send()
        for cp in mine:
            cp.wait()

    return _sequencer_call(
        body, shards, [SDS(_gathered(s.shape, k), s.dtype) for s, k in zip(shards, kinds)],
        [pltpu.SemaphoreType.DMA((13, nt)), pltpu.SemaphoreType.DMA((13, nt)), pltpu.SemaphoreType.DMA((nt,))],
        name=name, collective_id=collective_id)


def _part_shape(shape, kind):
    if kind == "row":
        return tuple(shape[:-2]) + (shape[-2] // N_DEV, shape[-1])
    return tuple(shape[2:])


def _pair_exchange(grads, kinds, *, name, collective_id):
    nt = len(grads)
    part = [_part_shape(g.shape, k) for g, k in zip(grads, kinds)]

    def body(*refs):
        g_refs, o_refs = refs[:nt], refs[nt:2 * nt]
        send_sems, recv_sems = refs[2 * nt:]
        x, y, c, _ = _mesh_place()
        _handshake([(x, y, 1 - c)])
        copies = []
        for t in range(nt):
            for xy in range(4):
                src = g_refs[t].at[1 - c, xy] if kinds[t] == "stack" else _window(g_refs[t], kinds[t], 2 * xy + 1 - c, part[t])
                copies.append(pltpu.make_async_remote_copy(
                    src_ref=src, dst_ref=o_refs[t].at[xy], send_sem=send_sems.at[xy, t], recv_sem=recv_sems.at[xy, t],
                    device_id=(x, y, 1 - c), device_id_type=MESH))
        for cp in copies:
            cp.start()
        for cp in copies:
            cp.wait()

    return _sequencer_call(
        body, grads, [SDS((4,) + p, g.dtype) for p, g in zip(part, grads)],
        [pltpu.SemaphoreType.DMA((4, nt)), pltpu.SemaphoreType.DMA((4, nt))], name=name, collective_id=collective_id)


def _chip_exchange(parts, small, *, name, collective_id):
    nt = len(parts)
    if small is None:
        def body_plain(*refs):
            s_refs, o_refs = refs[:nt], refs[nt:2 * nt]
            send_sems, recv_sems = refs[2 * nt:]
            x, y, c, chips = _mesh_place()
            _handshake([(*chip, c) for chip in chips])
            copies = [pltpu.make_async_remote_copy(
                src_ref=s_refs[t].at[2 * chip[0] + chip[1]], dst_ref=o_refs[t].at[j],
                send_sem=send_sems.at[j, t], recv_sem=recv_sems.at[j, t], device_id=(*chip, c), device_id_type=MESH)
                for j, chip in enumerate(chips) for t in range(nt)]
            for cp in copies:
                cp.start()
            for cp in copies:
                cp.wait()

        return _sequencer_call(
            body_plain, parts, [SDS((3,) + s.shape[1:], s.dtype) for s in parts],
            [pltpu.SemaphoreType.DMA((3, nt)), pltpu.SemaphoreType.DMA((3, nt))], name=name, collective_id=collective_id)

    def body(*refs):
        s_refs, small_ref = refs[:nt], refs[nt]
        o_refs, small_all = refs[nt + 1:2 * nt + 1], refs[2 * nt + 1]
        send_sems, recv_sems, small_send, small_recv, local_sem = refs[2 * nt + 2:]
        x, y, c, chips = _mesh_place()
        _handshake([(px, py, pc) for px in (x, 1 - x) for py in (y, 1 - y) for pc in (c, 1 - c)][1:])

        def copy(j, t, chip):
            return pltpu.make_async_remote_copy(
                src_ref=s_refs[t].at[2 * chip[0] + chip[1]], dst_ref=o_refs[t].at[j],
                send_sem=send_sems.at[j, t], recv_sem=recv_sems.at[j, t], device_id=(*chip, c), device_id_type=MESH)

        flips = [(fx, fy, fc) for fx in (0, 1) for fy in (0, 1) for fc in (0, 1)][1:]

        def small_copy(k):
            fx, fy, fc = flips[k]
            to = (x ^ fx if fx else x, y ^ fy if fy else y, c ^ fc if fc else c)
            rows = small_all.at[4 * x + 2 * y + c]
            return pltpu.make_async_remote_copy(
                src_ref=small_ref, dst_ref=rows, send_sem=small_send.at[k], recv_sem=small_recv.at[k],
                device_id=to, device_id_type=MESH)

        own = pltpu.make_async_copy(small_ref, small_all.at[4 * x + 2 * y + c], local_sem)
        own.start()
        copies = [copy(j, t, chip) for j, chip in enumerate(chips) for t in range(nt)]
        smalls = [small_copy(k) for k in range(7)]
        for cp in smalls + copies:
            cp.start()
        for cp in smalls + copies:
            cp.wait()
        own.wait()

    return _sequencer_call(
        body, list(parts) + [small],
        [SDS((3,) + s.shape[1:], s.dtype) for s in parts] + [SDS((N_DEV,) + small.shape, small.dtype)],
        [pltpu.SemaphoreType.DMA((3, nt)), pltpu.SemaphoreType.DMA((3, nt)),
         pltpu.SemaphoreType.DMA((7,)), pltpu.SemaphoreType.DMA((7,)), pltpu.SemaphoreType.DMA],
        name=name, collective_id=collective_id)


def _pair_sum(grad, theirs, kind, c, *, name):
    if kind == "row":
        r, l = theirs.shape[-2:]
        n = theirs.size // (4 * r * l)
        grad, theirs = grad.reshape(n, N_DEV * r, l), theirs.reshape(4, n, r, l)
        mine_spec = pl.BlockSpec((n, r, l), lambda xy, c_ref: (0, 2 * xy + c_ref[0], 0))
    else:
        r, l = theirs.shape[-2:]
        n = theirs.size // (4 * r * l)
        theirs = theirs.reshape(4, n, r, l)
        grad = grad.reshape(2, 4, n, r, l)
        mine_spec = pl.BlockSpec((None, None, n, r, l), lambda xy, c_ref: (c_ref[0], xy, 0, 0, 0))

    def body(c_ref, a_ref, b_ref, o_ref):
        o_ref[...] = (a_ref[...].astype(F32) + b_ref[...].astype(F32)).astype(BF16)

    part = pl.BlockSpec((None, n, r, l), lambda xy, c_ref: (xy, 0, 0, 0))
    return pl.pallas_call(
        body, name=name,
        grid_spec=pltpu.PrefetchScalarGridSpec(num_scalar_prefetch=1, grid=(4,), in_specs=[mine_spec, part], out_specs=part),
        out_shape=SDS((4, n, r, l), BF16), compiler_params=_params("parallel"),
    )(c, grad, theirs)


def _adamw(w, g, m, v):
    m = ADAM_B1 * m + (1.0 - ADAM_B1) * g
    v = ADAM_B2 * v + (1.0 - ADAM_B2) * jnp.square(g)
    m_hat = m / (1.0 - ADAM_B1 ** ADAM_STEP)
    v_hat = v / (1.0 - ADAM_B2 ** ADAM_STEP)
    delta = -ADAM_LR * (m_hat / (jnp.sqrt(v_hat) + ADAM_EPS) + ADAM_WD * w)
    return delta, m, v


def _adam_big(owns, others, mat, xy, w, m, v, *, tr, name):
    nw, r, l = w.shape
    lp = owns[0].shape[-1]
    nq = len(owns)
    assert nq in (1, nw)

    def body(xy_ref, *refs):
        own_refs, oth_refs = refs[:nq], refs[nq:2 * nq]
        w_ref, m_ref, v_ref, g_out, d_out, m_out, v_out = refs[2 * nq:]
        g = None
        for q in range(nq):
            gq = own_refs[q][0, 0].astype(F32)
            for j in range(3):
                gq = gq + oth_refs[q][j, 0].astype(F32)
            g = gq if g is None else jnp.where(pl.program_id(0) == q, gq, g)
        g = g[:, :l]
        delta, m_new, v_new = _adamw(w_ref[0], g, m_ref[0], v_ref[0])
        g_out[0] = g
        d_out[0] = delta
        m_out[0] = m_new
        v_out[0] = v_new

    def at(p):
        return mat * nw + p if nq == 1 else mat

    blk = pl.BlockSpec((1, tr, l), lambda p, i, xy_ref: (p, i, 0))
    return pl.pallas_call(
        body, name=name,
        grid_spec=pltpu.PrefetchScalarGridSpec(
            num_scalar_prefetch=1, grid=(nw, r // tr),
            in_specs=[pl.BlockSpec((1, 1, tr, lp), lambda p, i, xy_ref: (xy_ref[0], at(p), i, 0))] * nq
            + [pl.BlockSpec((3, 1, tr, lp), lambda p, i, xy_ref: (0, at(p), i, 0))] * nq + [blk, blk, blk],
            out_specs=[blk, blk, blk, blk]),
        out_shape=[SDS(w.shape, F32)] * 4, compiler_params=_params("parallel", "parallel"),
    )(xy, *owns, *others, w, m, v)


def _small_layout(shapes):
    out, at = [], 0
    for r, c in shapes:
        rows = c // 128 if (r == 1 and c > 128) else r
        out.append((at, rows))
        at += -(-rows // 8) * 8
    return out, at


def _pack_small(parts, *, name):
    shapes = [a.shape for a in parts]
    layout, total = _small_layout(shapes)

    def body(*refs):
        o_ref = refs[-1]
        o_ref[...] = jnp.zeros_like(o_ref)
        for x_ref, (r, c), (at, rows) in zip(refs, shapes, layout):
            if r == 1 and c > 128:
                for k in range(rows):
                    o_ref[at + k:at + k + 1, :] = x_ref[:, k * 128:(k + 1) * 128]
            else:
                o_ref[at:at + r, 0:c] = x_ref[...]

    return pl.pallas_call(body, name=name, out_shape=SDS((total, 128), F32))(*parts)


def _adam_small(g_all, ws, ms, vs, *, name):
    n = len(ws)
    shapes = [w.shape for w in ws]
    layout, _ = _small_layout(shapes)

    def body(g_ref, *refs):
        w_refs, m_refs, v_refs, outs = refs[:n], refs[n:2 * n], refs[2 * n:3 * n], refs[3 * n:]
        g_sum = g_ref[0]
        for k in range(1, N_DEV):
            g_sum = g_sum + g_ref[k]
        for i, ((r, c), (at, rows)) in enumerate(zip(shapes, layout)):
            if r == 1 and c > 128:
                g = jnp.concatenate([g_sum[at + k:at + k + 1, :] for k in range(rows)], axis=1)
            else:
                g = g_sum[at:at + r, 0:c]
            delta, m_new, v_new = _adamw(w_refs[i][...], g, m_refs[i][...], v_refs[i][...])
            for q, val in enumerate((g, delta, m_new, v_new)):
                outs[4 * i + q][...] = val

    flat = pl.pallas_call(body, name=name, out_shape=[SDS(s, F32) for s in shapes for _ in range(4)])(g_all, *ws, *ms, *vs)
    return [flat[4 * i:4 * i + 4] for i in range(n)]


def kernel(x, mem, ffn1_norm, ffn1_w_gate, ffn1_w_up, ffn1_w_down, mix_norm, mem_norm, w_in, w_mem_kv, swa_q_norm, swa_k_norm, swa_sinks, rel_bias, gla_w_gate_up, gla_b_gate, gla_out_norm, mem_q_norm, mem_k_norm, w_out, ffn2_norm, ffn2_w_gate, ffn2_w_up, ffn2_w_down, loss_target, m_ffn1_norm, m_ffn1_w_gate, m_ffn1_w_up, m_ffn1_w_down, m_mix_norm, m_mem_norm, m_w_in, m_w_mem_kv, m_swa_q_norm, m_swa_k_norm, m_swa_sinks, m_rel_bias, m_gla_w_gate_up, m_gla_b_gate, m_gla_out_norm, m_mem_q_norm, m_mem_k_norm, m_w_out, m_ffn2_norm, m_ffn2_w_gate, m_ffn2_w_up, m_ffn2_w_down, v_ffn1_norm, v_ffn1_w_gate, v_ffn1_w_up, v_ffn1_w_down, v_mix_norm, v_mem_norm, v_w_in, v_w_mem_kv, v_swa_q_norm, v_swa_k_norm, v_swa_sinks, v_rel_bias, v_gla_w_gate_up, v_gla_b_gate, v_gla_out_norm, v_mem_q_norm, v_mem_k_norm, v_w_out, v_ffn2_norm, v_ffn2_w_gate, v_ffn2_w_up, v_ffn2_w_down):
    xi, yi, ci = lax.axis_index("x"), lax.axis_index("y"), lax.axis_index("c")
    c_arr = jnp.reshape(ci, (1,)).astype(jnp.int32)
    xy_arr = jnp.reshape(2 * xi + yi, (1,)).astype(jnp.int32)
    d = x.shape[-1]

    half_h = ffn1_w_gate.shape[-1] // 2

    def ffn_shards(wg_s, wu_s, wd_s):
        w3_s = jnp.concatenate([wg_s.transpose(0, 2, 1), wu_s.transpose(0, 2, 1), wd_s], axis=0)
        return jnp.pad(w3_s.reshape(3, 2, half_h, d), ((0, 0), (0, 0), (0, FFN_HALF_PAD - half_h), (0, 0))).astype(BF16)

    def gather_ffn(wg_s, wu_s, wd_s, name, collective_id, after):
        w3_s, _ = lax.optimization_barrier((ffn_shards(wg_s, wu_s, wd_s), after))
        return _all_gather([w3_s], ["row"], name=name, collective_id=collective_id)[0].reshape(3, -1, d)

    w3_s = ffn_shards(ffn1_w_gate, ffn1_w_up, ffn1_w_down)
    w3_1a = _all_gather([w3_s[:, 0]], ["row"], name="gather_ffn1a", collective_id=0)[0]
    w3_s, _ = lax.optimization_barrier((w3_s, w3_1a))
    w3_1b = _all_gather([w3_s[:, 1]], ["row"], name="gather_ffn1b", collective_id=11)[0]

    def gather_mix(after):
        mix_s = lax.optimization_barrier((w_in[0].astype(BF16), w_mem_kv[0].astype(BF16), w_out[0].astype(BF16),
                                          (w3_1b, after)))[:3]
        win_all, wkv, wout = _all_gather(list(mix_s), ["stack", "row", "row"], name="gather_mix", collective_id=1)
        return _pack_win(win_all, tr=256, name="pack_w_in"), wkv, wout

    def gather_ffn2(after):
        return gather_ffn(ffn2_w_gate, ffn2_w_up, ffn2_w_down, "gather_ffn2", 2, after)

    small_w = [ffn1_norm, mix_norm, mem_norm, ffn2_norm, swa_q_norm, swa_k_norm, swa_sinks[0], rel_bias,
               gla_w_gate_up[0], gla_b_gate, gla_out_norm, mem_q_norm, mem_k_norm]
    collective_ids = {"ffn2": (3, 4), "mix": (5, 6), "ffn1a": (7, 8), "ffn1b": (9, 10)}
    reduced, small_box = {}, {}

    def on_grads(group, grads, carry, small=None):
        if group == "mix":
            dwin_p, dwkv, dwout = grads
            grads = [_unpack_win(dwin_p, tr=256, name="unpack_dw_in"), dwkv, dwout]
            kinds = ["stack", "row", "row"]
        else:
            kinds = ["row"]
        if reduced:
            earlier = list(reduced.values())[-1][1]
            *grads, _ = lax.optimization_barrier((*grads, earlier[0]))
        id_pair, id_chip = collective_ids[group]
        from_sibling = _pair_exchange(grads, kinds, name=f"pair_exchange_{group}", collective_id=id_pair)
        chip_sums = [_pair_sum(g, theirs, k, c_arr, name=f"pair_sum_{group}_{t}")
                     for t, (g, theirs, k) in enumerate(zip(grads, from_sibling, kinds))]
        if carry is not None:
            *chip_sums, carry = lax.optimization_barrier((*chip_sums, carry))
        if small is None:
            from_chips = _chip_exchange(chip_sums, None, name=f"chip_exchange_{group}", collective_id=id_chip)
        else:
            packed = _pack_small(small, name=f"pack_small_{group}")
            *from_chips, small_all = _chip_exchange(chip_sums, packed, name=f"chip_exchange_{group}",
                                                    collective_id=id_chip)
            small_box[group] = small_all
        reduced[group] = (chip_sums, from_chips)
        return carry

    grad_x = _local_step(x[0], mem[0], loss_target[0], small_w, ((w3_1a, w3_1b), gather_mix, gather_ffn2), on_grads)

    big_w = {"ffn1_w_gate": ("ffn1", 0, 0, True, ffn1_w_gate, m_ffn1_w_gate, v_ffn1_w_gate),
             "ffn1_w_up": ("ffn1", 0, 1, True, ffn1_w_up, m_ffn1_w_up, v_ffn1_w_up),
             "ffn1_w_down": ("ffn1", 0, 2, False, ffn1_w_down, m_ffn1_w_down, v_ffn1_w_down),
             "w_in": ("mix", 0, 0, False, w_in, m_w_in, v_w_in),
             "w_mem_kv": ("mix", 1, 0, False, w_mem_kv, m_w_mem_kv, v_w_mem_kv),
             "w_out": ("mix", 2, 0, False, w_out, m_w_out, v_w_out),
             "ffn2_w_gate": ("ffn2", 0, 0, True, ffn2_w_gate, m_ffn2_w_gate, v_ffn2_w_gate),
             "ffn2_w_up": ("ffn2", 0, 1, True, ffn2_w_up, m_ffn2_w_up, v_ffn2_w_up),
             "ffn2_w_down": ("ffn2", 0, 2, False, ffn2_w_down, m_ffn2_w_down, v_ffn2_w_down)}
    res = {}
    for nm, (group, t, mat, transposed, w, m, v) in big_w.items():
        shape = w.shape
        if transposed:
            w, m, v = (a.transpose(0, 2, 1) for a in (w, m, v))
        if group != "mix":
            w, m, v = (a.reshape(2, half_h, d) for a in (w, m, v))
        r = w.shape[1]
        tr = 256 if r % 256 == 0 else r
        halves = ["ffn1a", "ffn1b"] if group == "ffn1" else [group]
        out = _adam_big([reduced[k][0][t] for k in halves], [reduced[k][1][t] for k in halves], mat, xy_arr, w, m, v,
                        tr=tr, name=f"adam_{nm}")
        if transposed:
            out = [a.reshape(1, -1, d).transpose(0, 2, 1) for a in out]
        res[nm] = [a.reshape(shape) for a in out]
    small_names = ["ffn1_norm", "mix_norm", "mem_norm", "ffn2_norm", "swa_q_norm", "swa_k_norm", "swa_sinks", "rel_bias",
                   "gla_w_gate_up", "gla_b_gate", "gla_out_norm", "mem_q_norm", "mem_k_norm"]
    small_m = [m_ffn1_norm, m_mix_norm, m_mem_norm, m_ffn2_norm, m_swa_q_norm, m_swa_k_norm, m_swa_sinks, m_rel_bias,
               m_gla_w_gate_up, m_gla_b_gate, m_gla_out_norm, m_mem_q_norm, m_mem_k_norm]
    small_v = [v_ffn1_norm, v_mix_norm, v_mem_norm, v_ffn2_norm, v_swa_q_norm, v_swa_k_norm, v_swa_sinks, v_rel_bias,
               v_gla_w_gate_up, v_gla_b_gate, v_gla_out_norm, v_mem_q_norm, v_mem_k_norm]
    small_full = [ffn1_norm, mix_norm, mem_norm, ffn2_norm, swa_q_norm, swa_k_norm, swa_sinks, rel_bias,
                  gla_w_gate_up, gla_b_gate, gla_out_norm, mem_q_norm, mem_k_norm]
    zero = jnp.zeros((1, 1), F32)
    two_d = lambda a: a.reshape(a.shape[-2:])
    for group, sel in (("ffn1a", slice(1, None)), ("ffn1b", slice(0, 1))):
        extra = [zero] if group == "ffn1a" else []
        ws, ms, vs = ([two_d(a) for a in arrs[sel]] + extra for arrs in (small_full, small_m, small_v))
        updated = _adam_small(small_box[group], ws, ms, vs, name=f"adam_small_{group}")
        for nm, full, out in zip(small_names[sel], small_full[sel], updated):
            res[nm] = [a.reshape(full.shape) for a in out]
        if extra:
            loss = updated[-1][0].reshape(())

    order = ["ffn1_norm", "ffn1_w_gate", "ffn1_w_up", "ffn1_w_down", "mix_norm", "mem_norm", "w_in", "w_mem_kv",
             "swa_q_norm", "swa_k_norm", "swa_sinks", "rel_bias", "gla_w_gate_up", "gla_b_gate", "gla_out_norm",
             "mem_q_norm", "mem_k_norm", "w_out", "ffn2_norm", "ffn2_w_gate", "ffn2_w_up", "ffn2_w_down"]
    outs = [loss, grad_x[None]]
    for q in range(4):
        outs += [res[nm][q] for nm in order]
    return tuple(outs)
```

```python
import functools
import math

import numpy as np
import jax
import jax.numpy as jnp
from jax import lax
from jax.experimental import pallas as pl
from jax.experimental.pallas import tpu as pltpu
from jax.experimental.pallas import tpu_sc as plsc

F32 = jnp.float32
BF16 = jnp.bfloat16
SDS = jax.ShapeDtypeStruct

EPS = 1e-6
HEAD_DIM = 64
SWA_HEADS = 8
SWA_KV_HEADS = 2
SWA_GROUP = SWA_HEADS // SWA_KV_HEADS
BLOCK = 128
N_BUCKETS = 32
MAX_DISTANCE = 128
GLA_HEADS = 4
GLA_DK = 32
GLA_DV = 64
GLA_RANK = 16
GLA_TAU = 16.0
GLA_CHUNK = 32
MEM_HEADS = 4
SWA_Q_W = SWA_HEADS * HEAD_DIM
SWA_KV_W = SWA_KV_HEADS * HEAD_DIM
GLA_QK_W = GLA_HEADS * GLA_DK
GLA_V_W = GLA_HEADS * GLA_DV
MEM_Q_W = MEM_HEADS * HEAD_DIM
IN_W = 1808
IN_W_PAD = 1920
COL_SQ, COL_SKV, COL_GQ, COL_GK, COL_GV, COL_GG, COL_MQ, COL_GLR = 0, 512, 768, 896, 1024, 1280, 1536, 1792

ADAM_LR = 0.001
ADAM_B1 = 0.9
ADAM_B2 = 0.999
ADAM_EPS = 1e-08
ADAM_WD = 0.01
ADAM_STEP = 10

N_DEV = 8
VMEM_LIMIT_BYTES = 56 * 1024 * 1024
MESH = pl.DeviceIdType.MESH


def _params(*sem):
    return pltpu.CompilerParams(dimension_semantics=sem or None, vmem_limit_bytes=VMEM_LIMIT_BYTES)


def _dot(a, b, ta, tb, precision=None):
    dims = (((0 if ta else 1,), (1 if tb else 0,)), ((), ()))
    return lax.dot_general(a, b, dims, preferred_element_type=F32, precision=precision)


def _mm_raw(a, b, ta=False, tb=False):
    return _dot(a.astype(BF16), b.astype(BF16), ta, tb)


def _mmf_raw(a, b, ta=False, tb=False):
    return _dot(a, b, ta, tb, lax.Precision.HIGHEST)


def _make_mm(raw):
    @functools.partial(jax.custom_vjp, nondiff_argnums=(2, 3))
    def mm(a, b, ta=False, tb=False):
        return raw(a, b, ta, tb)

    def fwd(a, b, ta, tb):
        return raw(a, b, ta, tb), (a, b)

    def bwd(ta, tb, res, g):
        a, b = res
        da = raw(b, g, tb, True) if ta else raw(g, b, False, not tb)
        db = raw(g, a, True, ta) if tb else raw(a, g, not ta, False)
        return da, db

    mm.defvjp(fwd, bwd)
    return mm


_mm = _make_mm(_mm_raw)
_mmf = _make_mm(_mmf_raw)


def _mm3(a, b, ta=False, tb=False):
    a_hi, b_hi = a.astype(BF16).astype(F32), b.astype(BF16).astype(F32)
    return _mm(a_hi, b_hi, ta, tb) + _mm(a_hi, b - b_hi, ta, tb) + _mm(a - a_hi, b_hi, ta, tb)


def _rms(x, g):
    return x * lax.rsqrt(jnp.mean(x * x, axis=-1, keepdims=True) + EPS) * g


def _silu_mul(g, u):
    return jax.nn.silu(g) * u


def _log_sigmoid(z):
    return jnp.minimum(z, 0.0) - jnp.log(1.0 + jnp.exp(-jnp.abs(z)))


def _matmul(a_list, b, *, ta=False, tb=False, tm, tn, b_blocks=None, res=None, scale=1.0, out_dtype=F32, name):
    if not isinstance(a_list, (list, tuple)):
        a_list = [a_list]
    n_a = len(a_list)
    m = a_list[0].shape[1] if ta else a_list[0].shape[0]
    ks = [a.shape[0] if ta else a.shape[1] for a in a_list]
    n = b.shape[0] if tb else b.shape[1]
    if b_blocks is None:
        assert n_a == 1
        b_blocks = [0]
    tm, tn = min(tm, m), min(tn, n)
    assert m % tm == 0 and n % tn == 0, (m, n, tm, tn)

    def body(*refs):
        a_refs, b_refs = refs[:n_a], refs[n_a:2 * n_a]
        r_ref = refs[2 * n_a] if res is not None else None
        o_ref = refs[-1]
        acc = _mm_raw(a_refs[0][...], b_refs[0][...], ta, tb)
        for k in range(1, n_a):
            acc = acc + _mm_raw(a_refs[k][...], b_refs[k][...], ta, tb)
        if scale != 1.0:
            acc = acc * scale
        if r_ref is not None:
            acc = r_ref[...] + acc
        o_ref[...] = acc.astype(out_dtype)

    in_specs = []
    for k in ks:
        in_specs.append(pl.BlockSpec((k, tm), lambda i, j: (0, i)) if ta else pl.BlockSpec((tm, k), lambda i, j: (i, 0)))
    for k, blk in zip(ks, b_blocks):
        if tb:
            in_specs.append(pl.BlockSpec((tn, k), functools.partial(lambda i, j, blk: (j, blk), blk=blk)))
        else:
            in_specs.append(pl.BlockSpec((k, tn), functools.partial(lambda i, j, blk: (blk, j), blk=blk)))
    args = list(a_list) + [b] * n_a
    if res is not None:
        in_specs.append(pl.BlockSpec((tm, tn), lambda i, j: (i, j)))
        args.append(res)
    return pl.pallas_call(
        body, name=name, grid=(m // tm, n // tn), in_specs=in_specs,
        out_specs=pl.BlockSpec((tm, tn), lambda i, j: (i, j)), out_shape=SDS((m, n), out_dtype),
        compiler_params=_params("parallel", "parallel"),
    )(*args)


def _win_pieces(w):
    glr_lo, glr_hi = COL_MQ, COL_MQ + GLA_RANK
    out = []
    for j in range(N_DEV):
        for lo, hi, shift in ((0, glr_lo, 0), (glr_lo, glr_hi, COL_GLR - glr_lo), (glr_hi, IN_W, COL_MQ - glr_hi)):
            s, e = max(j * w, lo), min((j + 1) * w, hi)
            if s < e:
                out.append((j, s - j * w, e - j * w, s + shift))
    return out


def _pack_win(win_all, *, tr, name):
    _, d, w = win_all.shape

    def body(i_ref, o_ref):
        for j, a, b, dst in _win_pieces(w):
            o_ref[:, dst:dst + b - a] = i_ref[j][:, a:b]
        o_ref[:, IN_W:] = jnp.zeros((tr, IN_W_PAD - IN_W), o_ref.dtype)

    return pl.pallas_call(
        body, name=name, grid=(d // tr,), in_specs=[pl.BlockSpec((N_DEV, tr, w), lambda i: (0, i, 0))],
        out_specs=pl.BlockSpec((tr, IN_W_PAD), lambda i: (i, 0)), out_shape=SDS((d, IN_W_PAD), win_all.dtype),
        compiler_params=_params("parallel"),
    )(win_all)


def _unpack_win(dwin_p, *, tr, name):
    d = dwin_p.shape[0]
    w = IN_W // N_DEV

    def body(i_ref, o_ref):
        for j, a, b, src in _win_pieces(w):
            o_ref[j % 2, j // 2, :, a:b] = i_ref[:, src:src + b - a]

    return pl.pallas_call(
        body, name=name, grid=(d // tr,), in_specs=[pl.BlockSpec((tr, IN_W_PAD), lambda i: (i, 0))],
        out_specs=pl.BlockSpec((2, 4, tr, w), lambda i: (0, 0, i, 0)), out_shape=SDS((2, 4, d, w), dwin_p.dtype),
        compiler_params=_params("parallel"),
    )(dwin_p)


def _rms_fwd(x, g, *, tm, name):
    s, d = x.shape

    def body(x_ref, g_ref, h_ref):
        h_ref[...] = _rms(x_ref[...], g_ref[...]).astype(BF16)

    return pl.pallas_call(
        body, name=name, grid=(s // tm,),
        in_specs=[pl.BlockSpec((tm, d), lambda i: (i, 0)), pl.BlockSpec((1, d), lambda i: (0, 0))],
        out_specs=pl.BlockSpec((tm, d), lambda i: (i, 0)), out_shape=SDS((s, d), BF16),
        compiler_params=_params("parallel"),
    )(x, g)


def _rms_bwd(x, g, dh, dres, *, tm, name):
    s, d = x.shape
    want_dx = dres is not None

    def body(*refs):
        if want_dx:
            x_ref, g_ref, dh_ref, dres_ref, dx_ref, dxb_ref, dg_ref = refs
        else:
            x_ref, g_ref, dh_ref, dg_ref = refs
        _, vjp = jax.vjp(_rms, x_ref[...], g_ref[...])
        dx, dg = vjp(dh_ref[...])
        if want_dx:
            dx = dres_ref[...] + dx
            dx_ref[...] = dx
            dxb_ref[...] = dx.astype(BF16)

        @pl.when(pl.program_id(0) == 0)
        def _():
            dg_ref[...] = jnp.zeros_like(dg_ref)

        dg_ref[...] += dg

    row = pl.BlockSpec((tm, d), lambda i: (i, 0))
    vec = pl.BlockSpec((1, d), lambda i: (0, 0))
    if want_dx:
        return pl.pallas_call(
            body, name=name, grid=(s // tm,), in_specs=[row, vec, row, row], out_specs=[row, row, vec],
            out_shape=[SDS((s, d), F32), SDS((s, d), BF16), SDS((1, d), F32)], compiler_params=_params("arbitrary"),
        )(x, g, dh, dres)
    return None, None, pl.pallas_call(
        body, name=name, grid=(s // tm,), in_specs=[row, vec, row], out_specs=vec,
        out_shape=SDS((1, d), F32), compiler_params=_params("arbitrary"),
    )(x, g, dh)


FFN_TN = 256
FFN_TN_FWD = 512
FFN_HALF_PAD = 192


def _ffn_fwd(x, gain, w3, tag, *, tm=1024, next_gain=None, target=None, start=None, partial=False):
    s, d = x.shape
    f = w3.shape[1]
    tn = FFN_TN_FWD
    nj = f // tn
    tm = min(tm, s)
    n_extra = (next_gain is not None) + (target is not None) + 2 * (start is not None)

    def body(*refs):
        x_ref, gain_ref, wg_ref, wu_ref, wd_ref = refs[:5]
        extra, outs = refs[5:5 + n_extra], refs[5 + n_extra:-1]
        acc_s = refs[-1]
        g_ref, u_ref = outs[-2:]
        h_ref = extra[-2] if start is not None else outs[-3]
        i, j = pl.program_id(0), pl.program_id(1)

        @pl.when(j == 0)
        def _():
            if start is None:
                h_ref[...] = _rms(x_ref[...], gain_ref[...]).astype(BF16)
                acc_s[...] = jnp.zeros_like(acc_s)
            else:
                acc_s[...] = extra[-1][...]

        hv = h_ref[...]
        g = _mm_raw(hv, wg_ref[...], False, True)
        u = _mm_raw(hv, wu_ref[...], False, True)
        g_ref[...] = g.astype(BF16)
        u_ref[...] = u.astype(BF16)
        acc_s[...] += _mm_raw(_silu_mul(g, u), wd_ref[...])

        @pl.when(j == nj - 1)
        def _():
            y = acc_s[...] if partial else x_ref[...] + 0.5 * acc_s[...]
            if target is None:
                outs[0][...] = y
                if next_gain is not None:
                    outs[1][...] = _rms(y, extra[0][...]).astype(BF16)
            else:
                dy_ref, dyb_ref, loss_ref = outs[:3]
                diff = y - extra[0][...]
                dy_ref[...] = diff * (1.0 / d)
                dyb_ref[...] = (diff * (1.0 / d)).astype(BF16)
                part = 0.5 * jnp.sum(jnp.mean(diff * diff, axis=-1, keepdims=True), axis=0, keepdims=True)

                @pl.when(i == 0)
                def _():
                    loss_ref[...] = part

                @pl.when(i > 0)
                def _():
                    loss_ref[...] += part

    row = pl.BlockSpec((tm, d), lambda i, j: (i, 0))
    vec = pl.BlockSpec((1, d), lambda i, j: (0, 0))
    tile = pl.BlockSpec((tm, tn), lambda i, j: (i, j))
    in_specs = [row, vec] + [pl.BlockSpec((None, tn, d), functools.partial(lambda i, j, k: (k, j, 0), k=k)) for k in range(3)]
    args = [x, gain, w3, w3, w3]
    if target is None:
        out_specs, out_shape = [row], [SDS((s, d), F32)]
        if next_gain is not None:
            in_specs.append(vec)
            args.append(next_gain)
            out_specs.append(row)
            out_shape.append(SDS((s, d), BF16))
    else:
        in_specs.append(row)
        args.append(target)
        out_specs = [row, row, pl.BlockSpec((1, 1), lambda i, j: (0, 0))]
        out_shape = [SDS((s, d), F32), SDS((s, d), BF16), SDS((1, 1), F32)]
    if start is None:
        out_specs.append(row)
        out_shape.append(SDS((s, d), BF16))
    else:
        in_specs += [row, row]
        args += list(start)
    *head, g, u = pl.pallas_call(
        body, name=f"{tag}_fwd", grid=(s // tm, nj), in_specs=in_specs,
        out_specs=out_specs + [tile, tile],
        out_shape=out_shape + [SDS((s, f), BF16), SDS((s, f), BF16)],
        scratch_shapes=[pltpu.VMEM((tm, d), F32)],
        compiler_params=_params("arbitrary", "arbitrary"),
    )(*args)
    if start is None:
        *head, h = head
    else:
        h = start[0]
    return head, (h, g, u)


def _ffn_bwd_part(dyb, w3, saved, first, count, dh_init, *, name):
    h, g, u = saved
    s, d = h.shape
    tn = FFN_TN

    def body(*refs):
        if dh_init is None:
            dy_ref, h_ref, wd_ref, wg_ref, wu_ref, g_ref, u_ref, dh_ref, dw3_ref, dg_s, du_s, a_s = refs
        else:
            dy_ref, h_ref, wd_ref, wg_ref, wu_ref, g_ref, u_ref, dh0_ref, dh_ref, dw3_ref, dg_s, du_s, a_s = refs
        j = pl.program_id(0)

        @pl.when(j == 0)
        def _():
            dh_ref[...] = jnp.zeros_like(dh_ref) if dh_init is None else dh0_ref[...]
            for ref in (dg_s, du_s, a_s):
                ref[...] = jnp.zeros_like(ref)

        now, before = j % 2, 1 - j % 2
        dyv = dy_ref[...]
        hv = h_ref[...]
        dg, du, a = dg_s[before], du_s[before], a_s[before]
        dh_ref[...] += _mm_raw(dg, wg_ref[...]) + _mm_raw(du, wu_ref[...])
        dw3_ref[0] = _mm_raw(dg, hv, True, False).astype(BF16)
        dw3_ref[1] = _mm_raw(du, hv, True, False).astype(BF16)
        dw3_ref[2] = (_mm_raw(a, dyv, True, False) * 0.5).astype(BF16)

        da = _mm_raw(dyv, wd_ref[...], False, True) * 0.5
        a, vjp = jax.vjp(_silu_mul, g_ref[...].astype(F32), u_ref[...].astype(F32))
        dg, du = vjp(da)
        dg_s[now] = dg.astype(BF16)
        du_s[now] = du.astype(BF16)
        a_s[now] = a.astype(BF16)

    this = lambda j: first + jnp.minimum(j, count - 1)
    last = lambda j: first + jnp.maximum(j - 1, 0)
    full = pl.BlockSpec((s, d), lambda j: (0, 0))
    once = pl.BlockSpec((s, d), lambda j: (0, 0), pipeline_mode=pl.Buffered(1))
    tile = pl.BlockSpec((s, tn), lambda j: (0, this(j)))
    in_specs = [once, once, pl.BlockSpec((None, tn, d), lambda j: (2, this(j), 0)),
                pl.BlockSpec((None, tn, d), lambda j: (0, last(j), 0)), pl.BlockSpec((None, tn, d), lambda j: (1, last(j), 0)),
                tile, tile]
    args = [dyb, h, w3, w3, w3, g, u]
    if dh_init is not None:
        in_specs.append(once)
        args.append(dh_init)
    return pl.pallas_call(
        body, name=name, grid=(count + 1,), in_specs=in_specs,
        out_specs=[full, pl.BlockSpec((3, tn, d), lambda j: (0, jnp.maximum(j - 1, 0), 0))],
        out_shape=[SDS((s, d), F32), SDS((3, count * tn, d), BF16)],
        scratch_shapes=[pltpu.VMEM((2, s, tn), BF16)] * 3,
        compiler_params=_params("arbitrary"),
    )(*args)


def _bucket_table():
    qi = np.arange(BLOCK)[:, None]
    kj = np.arange(2 * BLOCK)[None, :]
    dist = np.maximum(qi + BLOCK - kj, 0)
    max_exact = N_BUCKETS // 2
    d = np.maximum(dist, 1).astype(np.float32)
    large = max_exact + (np.log(d / np.float32(max_exact)) / np.float32(math.log(MAX_DISTANCE / max_exact))
                         * np.float32(N_BUCKETS - max_exact)).astype(np.int32)
    large = np.minimum(large, N_BUCKETS - 1)
    band = np.where(dist < max_exact, dist, large).astype(np.int32)
    return np.where(np.tril(np.ones((BLOCK, BLOCK), bool)), band[:, BLOCK:], band[:, :BLOCK])


SWA_STACK = SWA_GROUP * BLOCK


def _swa_masks(n):
    qi = lax.broadcasted_iota(jnp.int32, (SWA_STACK, BLOCK), 0) % BLOCK
    kj = lax.broadcasted_iota(jnp.int32, (SWA_STACK, BLOCK), 1)
    own = kj <= qi
    return own, own | (n > 0)


def _swa_group(q, kp, kc, vp, vc, qg, kg, sink, bias, own, valid):
    qn = _rms(q, qg)
    s = jnp.where(own, _mm(qn, _rms(kc, kg), False, True), _mm(qn, _rms(kp, kg), False, True))
    s = s * (HEAD_DIM ** -0.5) + bias
    s = jnp.where(valid, s, -jnp.inf)
    m = lax.stop_gradient(jnp.maximum(jnp.max(s, axis=-1, keepdims=True), sink))
    p = jnp.exp(s - m)
    p = p / (jnp.sum(p, axis=-1, keepdims=True) + jnp.exp(sink - m))
    return _mm(jnp.where(own, p, 0.0), vc) + _mm(jnp.where(own, 0.0, p), vp)


def _swa_bias_table(rb_ref, bucket, bias_s):
    for h in range(SWA_HEADS):
        acc = jnp.zeros((BLOCK, BLOCK), F32)
        for b in range(N_BUCKETS):
            acc = jnp.where(bucket == b, rb_ref[b, h], acc)
        bias_s[h // SWA_GROUP, (h % SWA_GROUP) * BLOCK:(h % SWA_GROUP + 1) * BLOCK, :] = acc


def _swa_stack(ref, g):
    return jnp.concatenate([ref[:, (g * SWA_GROUP + hh) * HEAD_DIM:(g * SWA_GROUP + hh + 1) * HEAD_DIM]
                            for hh in range(SWA_GROUP)], axis=0)


def _swa_unstack(ref, g, stacked):
    for hh in range(SWA_GROUP):
        h = g * SWA_GROUP + hh
        ref[:, h * HEAD_DIM:(h + 1) * HEAD_DIM] = stacked[hh * BLOCK:(hh + 1) * BLOCK]


def _swa_sink_column(sink_ref, g):
    head = lax.broadcasted_iota(jnp.int32, (SWA_STACK, 1), 0) // BLOCK
    col = jnp.zeros((SWA_STACK, 1), F32)
    for hh in range(SWA_GROUP):
        col = jnp.where(head == hh, sink_ref[g * SWA_GROUP + hh], col)
    return col


def _swa_band(kvp_ref, kvc_ref, g):
    k = slice(g * HEAD_DIM, (g + 1) * HEAD_DIM)
    v = slice(SWA_KV_W + g * HEAD_DIM, SWA_KV_W + (g + 1) * HEAD_DIM)
    return kvp_ref[:, k], kvc_ref[:, k], kvp_ref[:, v], kvc_ref[:, v]


def _swa_specs(order):
    kvc = COL_SKV // (2 * SWA_KV_W)
    return [
        pl.BlockSpec((BLOCK, SWA_Q_W), lambda t: (order(t), 0)),
        pl.BlockSpec((BLOCK, 2 * SWA_KV_W), lambda t: (jnp.maximum(order(t) - 1, 0), kvc)),
        pl.BlockSpec((BLOCK, 2 * SWA_KV_W), lambda t: (order(t), kvc)),
        pl.BlockSpec((1, HEAD_DIM), lambda t: (0, 0)),
        pl.BlockSpec((1, HEAD_DIM), lambda t: (0, 0)),
        pl.BlockSpec(memory_space=pltpu.SMEM),
        pl.BlockSpec(memory_space=pltpu.SMEM),
        pl.BlockSpec((BLOCK, BLOCK), lambda t: (0, 0)),
    ]


def _swa_fwd(p, qg, kg, sinks, rel_bias, *, name):
    s = p.shape[0]
    nb = s // BLOCK

    def body(q_ref, kvp_ref, kvc_ref, qg_ref, kg_ref, sink_ref, rb_ref, bucket_ref, y_ref, bias_s):
        n = pl.program_id(0)

        @pl.when(n == 0)
        def _():
            _swa_bias_table(rb_ref, bucket_ref[...], bias_s)

        own, valid = _swa_masks(n)
        for g in range(SWA_KV_HEADS):
            out = _swa_group(_swa_stack(q_ref, g), *_swa_band(kvp_ref, kvc_ref, g), qg_ref[...], kg_ref[...],
                             _swa_sink_column(sink_ref, g), bias_s[g], own, valid)
            _swa_unstack(y_ref, g, out)

    return pl.pallas_call(
        body, name=name, grid=(nb,), in_specs=_swa_specs(lambda t: t),
        out_specs=pl.BlockSpec((BLOCK, SWA_Q_W), lambda t: (t, 0)), out_shape=SDS((s, SWA_Q_W), F32),
        scratch_shapes=[pltpu.VMEM((SWA_KV_HEADS, SWA_STACK, BLOCK), F32)],
        compiler_params=_params("arbitrary"),
    )(p, p, p, qg, kg, sinks, rel_bias, jnp.asarray(_bucket_table()))


def _swa_bwd(p, qg, kg, sinks, rel_bias, dy_all, *, name):
    s = p.shape[0]
    nb = s // BLOCK

    def body(q_ref, kvp_ref, kvc_ref, qg_ref, kg_ref, sink_ref, rb_ref, bucket_ref, dy_ref,
             dq_ref, dkv_ref, dqg_ref, dkg_ref, dsink_ref, drb_ref, bias_s, dbias_s, carry_s):
        t = pl.program_id(0)
        n = nb - 1 - t

        @pl.when(t == 0)
        def _():
            _swa_bias_table(rb_ref, bucket_ref[...], bias_s)
            dbias_s[...] = jnp.zeros_like(dbias_s)
            carry_s[...] = jnp.zeros_like(carry_s)
            dqg_ref[...] = jnp.zeros_like(dqg_ref)
            dkg_ref[...] = jnp.zeros_like(dkg_ref)
            dsink_ref[...] = jnp.zeros_like(dsink_ref)
            drb_ref[...] = jnp.zeros_like(drb_ref)

        own, valid = _swa_masks(n)
        lane = lax.broadcasted_iota(jnp.int32, (1, BLOCK), 1)
        dqg = jnp.zeros((1, HEAD_DIM), F32)
        dkg = jnp.zeros((1, HEAD_DIM), F32)
        dsink_vec = jnp.zeros((1, BLOCK), F32)
        for g in range(SWA_KV_HEADS):
            _, vjp = jax.vjp(functools.partial(_swa_group, own=own, valid=valid), _swa_stack(q_ref, g),
                             *_swa_band(kvp_ref, kvc_ref, g), qg_ref[...], kg_ref[...], _swa_sink_column(sink_ref, g),
                             bias_s[g])
            dq, dkp, dkc, dvp, dvc, dqg_g, dkg_g, dsink_col, dbias = vjp(_swa_stack(dy_ref, g))
            _swa_unstack(dq_ref, g, dq)
            dqg += dqg_g
            dkg += dkg_g
            dbias_s[g] += dbias
            for hh in range(SWA_GROUP):
                dsink_h = jnp.sum(dsink_col[hh * BLOCK:(hh + 1) * BLOCK], axis=0, keepdims=True)
                dsink_vec += jnp.where(lane == g * SWA_GROUP + hh, dsink_h, 0.0)
            lo = g * HEAD_DIM
            dkv_ref[:, lo:lo + HEAD_DIM] = dkc + carry_s[g]
            carry_s[g] = dkp
            lo += SWA_KV_W
            dkv_ref[:, lo:lo + HEAD_DIM] = dvc + carry_s[SWA_KV_HEADS + g]
            carry_s[SWA_KV_HEADS + g] = dvp
        dqg_ref[...] += dqg
        dkg_ref[...] += dkg
        dsink_ref[...] += dsink_vec

        @pl.when(t == nb - 1)
        def _():
            bucket = bucket_ref[...]
            row = lax.broadcasted_iota(jnp.int32, (N_BUCKETS, BLOCK), 0)
            col = lax.broadcasted_iota(jnp.int32, (N_BUCKETS, BLOCK), 1)
            acc = jnp.zeros((N_BUCKETS, BLOCK), F32)
            for h in range(SWA_HEADS):
                dbias = dbias_s[h // SWA_GROUP, (h % SWA_GROUP) * BLOCK:(h % SWA_GROUP + 1) * BLOCK, :]
                for b in range(N_BUCKETS):
                    part = jnp.sum(jnp.where(bucket == b, dbias, 0.0), axis=1, keepdims=True)
                    val = jnp.sum(part, axis=0, keepdims=True)
                    acc = acc + jnp.where((row == b) & (col == h), val, 0.0)
            drb_ref[...] = acc

    order = lambda t: nb - 1 - t
    vec = pl.BlockSpec((1, HEAD_DIM), lambda t: (0, 0))
    return pl.pallas_call(
        body, name=name, grid=(nb,),
        in_specs=_swa_specs(order) + [pl.BlockSpec((BLOCK, SWA_Q_W), lambda t: (order(t), 0))],
        out_specs=[pl.BlockSpec((BLOCK, SWA_Q_W), lambda t: (order(t), 0)),
                   pl.BlockSpec((BLOCK, 2 * SWA_KV_W), lambda t: (order(t), 0)),
                   vec, vec, pl.BlockSpec((1, BLOCK), lambda t: (0, 0)),
                   pl.BlockSpec((N_BUCKETS, BLOCK), lambda t: (0, 0))],
        out_shape=[SDS((s, SWA_Q_W), F32), SDS((s, 2 * SWA_KV_W), F32), SDS((1, HEAD_DIM), F32),
                   SDS((1, HEAD_DIM), F32), SDS((1, BLOCK), F32), SDS((N_BUCKETS, BLOCK), F32)],
        scratch_shapes=[pltpu.VMEM((SWA_KV_HEADS, SWA_STACK, BLOCK), F32),
                        pltpu.VMEM((SWA_KV_HEADS, SWA_STACK, BLOCK), F32),
                        pltpu.VMEM((2 * SWA_KV_HEADS, BLOCK, HEAD_DIM), F32)],
        compiler_params=_params("arbitrary"),
    )(p, p, p, qg, kg, sinks, rel_bias, jnp.asarray(_bucket_table()), dy_all)


def _mem_head(q, k, v, qg, kg):
    qn = _rms(q, qg)
    kn = _rms(k, kg)
    s = _mm(qn, kn, False, True) * (HEAD_DIM ** -0.5)
    m = lax.stop_gradient(jnp.max(s, axis=-1, keepdims=True))
    e = jnp.exp(s - m)
    return _mm(e / jnp.sum(e, axis=-1, keepdims=True), v)


def _mem_fwd(p, kv, qg, kg, *, tq, name):
    s = p.shape[0]
    m = kv.shape[0]

    def body(q_ref, kv_ref, qg_ref, kg_ref, y_ref):
        for h in range(MEM_HEADS):
            cols = slice(h * HEAD_DIM, (h + 1) * HEAD_DIM)
            vcols = slice(MEM_Q_W + h * HEAD_DIM, MEM_Q_W + (h + 1) * HEAD_DIM)
            y_ref[:, cols] = _mem_head(q_ref[:, cols], kv_ref[:, cols], kv_ref[:, vcols], qg_ref[...], kg_ref[...])

    vec = pl.BlockSpec((1, HEAD_DIM), lambda t: (0, 0))
    return pl.pallas_call(
        body, name=name, grid=(s // tq,),
        in_specs=[pl.BlockSpec((tq, MEM_Q_W), lambda t: (t, COL_MQ // MEM_Q_W)),
                  pl.BlockSpec((m, 2 * MEM_Q_W), lambda t: (0, 0)), vec, vec],
        out_specs=pl.BlockSpec((tq, MEM_Q_W), lambda t: (t, 0)), out_shape=SDS((s, MEM_Q_W), F32),
        compiler_params=_params("parallel"),
    )(p, kv, qg, kg)


def _mem_bwd(p, kv, qg, kg, dy_all, *, tq, name):
    s = p.shape[0]
    m = kv.shape[0]

    def body(q_ref, kv_ref, qg_ref, kg_ref, dy_ref, dq_ref, dkv_ref, dqg_ref, dkg_ref):
        @pl.when(pl.program_id(0) == 0)
        def _():
            dkv_ref[...] = jnp.zeros_like(dkv_ref)
            dqg_ref[...] = jnp.zeros_like(dqg_ref)
            dkg_ref[...] = jnp.zeros_like(dkg_ref)

        dqg = jnp.zeros((1, HEAD_DIM), F32)
        dkg = jnp.zeros((1, HEAD_DIM), F32)
        for h in range(MEM_HEADS):
            cols = slice(h * HEAD_DIM, (h + 1) * HEAD_DIM)
            vcols = slice(MEM_Q_W + h * HEAD_DIM, MEM_Q_W + (h + 1) * HEAD_DIM)
            _, vjp = jax.vjp(_mem_head, q_ref[:, cols], kv_ref[:, cols], kv_ref[:, vcols], qg_ref[...], kg_ref[...])
            dq, dk, dv, dqg_h, dkg_h = vjp(dy_ref[:, cols])
            dq_ref[:, cols] = dq
            dkv_ref[:, cols] += dk
            dkv_ref[:, vcols] += dv
            dqg += dqg_h
            dkg += dkg_h
        dqg_ref[...] += dqg
        dkg_ref[...] += dkg

    vec = pl.BlockSpec((1, HEAD_DIM), lambda t: (0, 0))
    full = pl.BlockSpec((m, 2 * MEM_Q_W), lambda t: (0, 0))
    dy_col = (SWA_Q_W + GLA_V_W) // MEM_Q_W
    return pl.pallas_call(
        body, name=name, grid=(s // tq,),
        in_specs=[pl.BlockSpec((tq, MEM_Q_W), lambda t: (t, COL_MQ // MEM_Q_W)), full, vec, vec,
                  pl.BlockSpec((tq, MEM_Q_W), lambda t: (t, dy_col))],
        out_specs=[pl.BlockSpec((tq, MEM_Q_W), lambda t: (t, 0)), full, vec, vec],
        out_shape=[SDS((s, MEM_Q_W), F32), SDS((m, 2 * MEM_Q_W), F32), SDS((1, HEAD_DIM), F32), SDS((1, HEAD_DIM), F32)],
        compiler_params=_params("arbitrary"),
    )(p, kv, qg, kg, dy_all)


GLA_ROWS = 256


GLA_GROUP = 4


def _gla_consts():
    c, h, r = GLA_CHUNK, GLA_HEADS, GLA_GROUP * GLA_CHUNK
    i2 = lax.broadcasted_iota(jnp.int32, (c, c), 0)
    j2 = lax.broadcasted_iota(jnp.int32, (c, c), 1)
    slab_q = lax.broadcasted_iota(jnp.int32, (h, r, GLA_QK_W), 0)
    lane_q = lax.broadcasted_iota(jnp.int32, (h, r, GLA_QK_W), 2)
    row_a = lax.broadcasted_iota(jnp.int32, (h * r, r), 0) % r
    col_a = lax.broadcasted_iota(jnp.int32, (h * r, r), 1)
    slab_o = lax.broadcasted_iota(jnp.int32, (h, r, GLA_V_W), 0)
    lane_o = lax.broadcasted_iota(jnp.int32, (h, r, GLA_V_W), 2)
    row_s = lax.broadcasted_iota(jnp.int32, (GLA_V_W, GLA_QK_W), 0)
    col_s = lax.broadcasted_iota(jnp.int32, (GLA_V_W, GLA_QK_W), 1)
    return dict(
        ltri=(j2 <= i2).astype(F32),
        m_q=(slab_q == lane_q // GLA_DK).astype(F32),
        causal=(col_a <= row_a) & (col_a // c == row_a // c),
        m_o=(slab_o == lane_o // GLA_DV).astype(F32),
        m_s=(row_s // GLA_DV == col_s // GLA_DK).astype(F32),
    )


def _gla_step(q, k, v, z, bg, st, c):
    h = GLA_HEADS
    kt, qt, qe, decay = [], [], [], []
    for qc, kc, zc in zip(q, k, z):
        la = _log_sigmoid(zc + bg) * (1.0 / GLA_TAU)
        b = _mmf(c["ltri"], la)
        bl = jnp.sum(la, axis=0, keepdims=True)
        qs = qc * (GLA_DK ** -0.5)
        kt.append(kc * jnp.exp(bl - b))
        qt.append(qs * jnp.exp(b - bl))
        qe.append(qs * jnp.exp(b))
        decay.append(jnp.exp(bl))
    o_intra = []
    rows = GLA_GROUP * GLA_CHUNK
    for lo in range(0, len(q), GLA_GROUP):
        qt_all, kt_all, v_all = (jnp.concatenate(parts[lo:lo + GLA_GROUP], axis=0) for parts in (qt, kt, v))
        q_stack = (jnp.broadcast_to(qt_all[None], (h, rows, GLA_QK_W)) * c["m_q"]).reshape(h * rows, GLA_QK_W)
        a = jnp.where(c["causal"], _mm3(q_stack, kt_all, False, True), 0.0)
        o_stack = _mm(a, v_all)
        o_intra.append(jnp.sum(o_stack.reshape(h, rows, GLA_V_W) * c["m_o"], axis=0))
    o_intra = jnp.concatenate(o_intra, axis=0)
    o_inter = []
    for qec, ktc, vc, dc in zip(qe, kt, v, decay):
        o_inter.append(_mm(qec, st, False, True))
        st = st * dc + _mm(vc, ktc, True, False) * c["m_s"]
    return o_intra + jnp.concatenate(o_inter, axis=0), st


def _gla_post(o, gg, gain, g64):
    ms = _mmf(o * o, g64) * (1.0 / GLA_DV)
    return o * lax.rsqrt(ms + EPS) * gain * jax.nn.silu(gg)


def _gla_g64():
    r = lax.broadcasted_iota(jnp.int32, (GLA_V_W, GLA_V_W), 0)
    c = lax.broadcasted_iota(jnp.int32, (GLA_V_W, GLA_V_W), 1)
    return (r // GLA_DV == c // GLA_DV).astype(F32)


def _gla_in_specs(order):
    r = GLA_ROWS
    return [
        pl.BlockSpec((r, GLA_QK_W), lambda t: (order(t), COL_GQ // GLA_QK_W)),
        pl.BlockSpec((r, GLA_QK_W), lambda t: (order(t), COL_GK // GLA_QK_W)),
        pl.BlockSpec((r, GLA_V_W), lambda t: (order(t), COL_GV // GLA_V_W)),
        pl.BlockSpec((r, GLA_V_W), lambda t: (order(t), COL_GG // GLA_V_W)),
        pl.BlockSpec((r, GLA_QK_W), lambda t: (order(t), 0)),
        pl.BlockSpec((1, GLA_QK_W), lambda t: (0, 0)),
        pl.BlockSpec((1, GLA_V_W), lambda t: (0, 0)),
    ]


def _gla_pieces(q_ref, k_ref, v_ref, z_ref, cps):
    chunk = lambda ref: [ref[ci * GLA_CHUNK:(ci + 1) * GLA_CHUNK, :] for ci in range(cps)]
    return chunk(q_ref), chunk(k_ref), chunk(v_ref), chunk(z_ref)


def _gla_fwd(p, z, bg, gain, *, name):
    s = p.shape[0]
    r = GLA_ROWS
    cps = r // GLA_CHUNK

    def body(q_ref, k_ref, v_ref, gg_ref, z_ref, bg_ref, gain_ref, y_ref, oraw_ref, stsave_ref, st_s):
        @pl.when(pl.program_id(0) == 0)
        def _():
            st_s[...] = jnp.zeros_like(st_s)

        st = st_s[...]
        stsave_ref[0] = st
        o, st = _gla_step(*_gla_pieces(q_ref, k_ref, v_ref, z_ref, cps), bg_ref[...], st, _gla_consts())
        oraw_ref[...] = o
        st_s[...] = st
        y_ref[...] = _gla_post(o, gg_ref[...], gain_ref[...], _gla_g64())

    rowv = pl.BlockSpec((r, GLA_V_W), lambda t: (t, 0))
    return pl.pallas_call(
        body, name=name, grid=(s // r,), in_specs=_gla_in_specs(lambda t: t),
        out_specs=[rowv, rowv, pl.BlockSpec((1, GLA_V_W, GLA_QK_W), lambda t: (t, 0, 0))],
        out_shape=[SDS((s, GLA_V_W), F32), SDS((s, GLA_V_W), F32), SDS((s // r, GLA_V_W, GLA_QK_W), F32)],
        scratch_shapes=[pltpu.VMEM((GLA_V_W, GLA_QK_W), F32)],
        compiler_params=_params("arbitrary"),
    )(p, p, p, p, z, bg, gain)


def _gla_bwd(p, z, bg, gain, oraw, stsave, dy_all, *, name):
    s = p.shape[0]
    r = GLA_ROWS
    cps = r // GLA_CHUNK
    nsteps = s // r
    w_qkvg = 2 * GLA_QK_W + 2 * GLA_V_W

    def body(q_ref, k_ref, v_ref, gg_ref, z_ref, bg_ref, gain_ref, oraw_ref, stsave_ref, dy_ref,
             dqkvg_ref, dz_ref, dbg_ref, dgain_ref, dst_s):
        @pl.when(pl.program_id(0) == 0)
        def _():
            dst_s[...] = jnp.zeros_like(dst_s)
            dbg_ref[...] = jnp.zeros_like(dbg_ref)
            dgain_ref[...] = jnp.zeros_like(dgain_ref)

        _, vjp = jax.vjp(functools.partial(_gla_post, g64=_gla_g64()), oraw_ref[...], gg_ref[...], gain_ref[...])
        do, dgg, dgain = vjp(dy_ref[...])
        dqkvg_ref[:, 2 * GLA_QK_W + GLA_V_W:] = dgg
        dgain_ref[...] += dgain
        _, vjp = jax.vjp(functools.partial(_gla_step, c=_gla_consts()), *_gla_pieces(q_ref, k_ref, v_ref, z_ref, cps),
                         bg_ref[...], stsave_ref[0])
        dq, dk, dv, dz, dbg, dst = vjp((do, dst_s[...]))
        for ci in range(cps):
            rows = slice(ci * GLA_CHUNK, (ci + 1) * GLA_CHUNK)
            dqkvg_ref[rows, 0:GLA_QK_W] = dq[ci]
            dqkvg_ref[rows, GLA_QK_W:2 * GLA_QK_W] = dk[ci]
            dqkvg_ref[rows, 2 * GLA_QK_W:2 * GLA_QK_W + GLA_V_W] = dv[ci]
            dz_ref[rows, :] = dz[ci]
        dst_s[...] = dst
        dbg_ref[...] += dbg

    order = lambda t: nsteps - 1 - t
    rowv = pl.BlockSpec((r, GLA_V_W), lambda t: (order(t), 0))
    return pl.pallas_call(
        body, name=name, grid=(nsteps,),
        in_specs=_gla_in_specs(order) + [
            rowv, pl.BlockSpec((1, GLA_V_W, GLA_QK_W), lambda t: (order(t), 0, 0)),
            pl.BlockSpec((r, GLA_V_W), lambda t: (order(t), SWA_Q_W // GLA_V_W))],
        out_specs=[pl.BlockSpec((r, w_qkvg), lambda t: (order(t), 0)), pl.BlockSpec((r, GLA_QK_W), lambda t: (order(t), 0)),
                   pl.BlockSpec((1, GLA_QK_W), lambda t: (0, 0)), pl.BlockSpec((1, GLA_V_W), lambda t: (0, 0))],
        out_shape=[SDS((s, w_qkvg), F32), SDS((s, GLA_QK_W), F32), SDS((1, GLA_QK_W), F32), SDS((1, GLA_V_W), F32)],
        scratch_shapes=[pltpu.VMEM((GLA_V_W, GLA_QK_W), F32)],
        compiler_params=_params("arbitrary"),
    )(p, p, p, p, z, bg, gain, oraw, stsave, dy_all)


def _local_step(x, mem, target, small, big, on_grads):
    g1, gmix, gmem, g2, sqg, skg, sinks, rel_bias, wgu, bg, gla_gain, mqg, mkg = small
    (w3_1a, w3_1b), gather_mix, gather_ffn2 = big
    wgu_pad = jnp.zeros((GLA_QK_W, GLA_QK_W), BF16).at[:GLA_RANK].set(wgu.astype(BF16))
    gain256 = jnp.tile(gla_gain, (1, GLA_HEADS))

    (part,), saved1a = _ffn_fwd(x, g1, w3_1a, "ffn1a", partial=True)
    win_p, wkv, wout = gather_mix(part)
    (x1, h), saved1b = _ffn_fwd(x, g1, w3_1b, "ffn1b", next_gain=gmix, start=(saved1a[0], part))
    w3_2 = gather_ffn2((wout, x1))
    p = _matmul(h, win_p, tm=512, tn=IN_W_PAD, name="mix_in")
    hm = _rms_fwd(mem, gmem, tm=256, name="mem_rms")
    kv = _matmul(hm, wkv, tm=256, tn=512, name="mem_kv")
    p_glr = p[:, COL_GLR:]
    z = _matmul(p_glr, wgu_pad, tm=1024, tn=GLA_QK_W, name="gla_gate")
    y_swa = _swa_fwd(p, sqg, skg, sinks, rel_bias, name="swa_fwd")
    y_gla, oraw, stsave = _gla_fwd(p, z, bg, gain256, name="gla_fwd")
    y_mem = _mem_fwd(p, kv, mqg, mkg, tq=512, name="mem_fwd")
    x2 = _matmul([y_swa, y_gla, y_mem], wout, b_blocks=[0, 2, 3], tm=512, tn=1024, res=x1, name="mix_out")
    (dy, dyb, loss), saved2 = _ffn_fwd(x2, g2, w3_2, "ffn2", target=target)

    tiles = w3_2.shape[1] // FFN_TN
    dh2, dw3_2 = _ffn_bwd_part(dyb, w3_2, saved2, 0, tiles, None, name="ffn2_bwd")
    dx2, dx2b, dg2 = _rms_bwd(x2, g2, dh2, dy, tm=512, name="ffn2_drms")
    dx2b = on_grads("ffn2", [dw3_2.reshape(3, 2, -1, dw3_2.shape[-1])], dx2b)
    dy_all = _matmul(dx2b, wout, tb=True, tm=512, tn=1024, name="mix_dy")
    dwout = _matmul(jnp.concatenate([y_swa, y_gla, y_mem], axis=1), dx2b, ta=True, tm=512, tn=1024, out_dtype=BF16,
                    name="mix_dw_out")
    dq_swa, dkv_swa, dsqg, dskg, dsink, drb = _swa_bwd(p, sqg, skg, sinks, rel_bias, dy_all, name="swa_bwd")
    dqkvg, dz, dbg, dgain256 = _gla_bwd(p, z, bg, gain256, oraw, stsave, dy_all, name="gla_bwd")
    dmq, dkv_mem, dmqg, dmkg = _mem_bwd(p, kv, mqg, mkg, dy_all, tq=512, name="mem_bwd")
    dglr = _matmul(dz, wgu_pad, tb=True, tm=1024, tn=GLA_QK_W, name="gla_gate_dx")
    dwgu_pad = _matmul(p_glr, dz, ta=True, tm=GLA_QK_W, tn=GLA_QK_W, name="gla_gate_dw")
    dp = jnp.concatenate([dq_swa, dkv_swa, dqkvg, dmq, dglr], axis=1)
    dh = _matmul(dp, win_p, tb=True, tm=512, tn=1024, name="mix_dh")
    dwin_p = _matmul(h, dp, ta=True, tm=1024, tn=640, out_dtype=BF16, name="mix_dw_in")
    dx1, dx1b, dgmix = _rms_bwd(x1, gmix, dh, dx2, tm=512, name="mix_drms")
    dwkv = _matmul(hm, dkv_mem, ta=True, tm=512, tn=512, out_dtype=BF16, name="mem_dw_kv")
    dx1b = on_grads("mix", (dwin_p, dwkv, dwout), dx1b)
    dhm = _matmul(dkv_mem, wkv, tb=True, tm=256, tn=512, name="mem_dh")
    _, _, dgmem = _rms_bwd(mem, gmem, dhm, None, tm=256, name="mem_drms")
    dh1, dw3_1a = _ffn_bwd_part(dx1b, w3_1a, saved1a, 0, tiles // 2, None, name="ffn1_bwd_a")
    dgla_gain = dgain256.reshape(GLA_HEADS, GLA_DV).sum(axis=0, keepdims=True)
    dsmall = [dgmix, dgmem, dg2, dsqg, dskg, dsink[:, :SWA_HEADS], drb[:, :SWA_HEADS], dwgu_pad[:GLA_RANK], dbg,
              dgla_gain, dmqg, dmkg, loss]
    dh1, dgmem, dwgu_pad = on_grads("ffn1a", [dw3_1a[:, None]], (dh1, dgmem, dwgu_pad), small=dsmall)
    dh1, dw3_1b = _ffn_bwd_part(dx1b, w3_1b, saved1b, 0, tiles // 2, dh1, name="ffn1_bwd_b")
    dx, _, dg1 = _rms_bwd(x, g1, dh1, dx1, tm=512, name="ffn1_drms")
    on_grads("ffn1b", [dw3_1b[:, None]], None, small=[dg1])
    return dx


def _mesh_place():
    x, y, c = lax.axis_index("x"), lax.axis_index("y"), lax.axis_index("c")
    other_chips = [(1 - x, y), (x, 1 - y), (1 - x, 1 - y)]
    return x, y, c, other_chips


def _handshake(peers):
    barrier = pltpu.get_barrier_semaphore()
    for peer in peers:
        pl.semaphore_signal(barrier, inc=1, device_id=peer, device_id_type=MESH)
    pl.semaphore_wait(barrier, len(peers))


def _sequencer_call(body, operands, out_shapes, sems, *, name, collective_id):
    return pl.kernel(
        body, name=name, out_type=out_shapes, mesh=plsc.ScalarSubcoreMesh(axis_name="sequencer", num_cores=1),
        scratch_types=sems, compiler_params=pltpu.CompilerParams(collective_id=collective_id),
    )(*operands)


def _window(ref, kind, slot, shape):
    if kind == "row":
        rows = pl.ds(pl.multiple_of(slot * shape[-2], 8), shape[-2])
        return ref.at[(slice(None),) * (len(shape) - 2) + (rows,)]
    return ref.at[slot]


def _gathered(shape, kind):
    if kind == "row":
        return tuple(shape[:-2]) + (N_DEV * shape[-2], shape[-1])
    return (N_DEV,) + tuple(shape)


def _half(view, hf):
    if len(view.shape) == 4:
        return view.at[:, hf]
    n = view.shape[-2] // 2
    return view.at[(slice(None),) * (len(view.shape) - 2) + (pl.ds(hf * n, n),)]


def _all_gather(shards, kinds, *, name, collective_id):
    nt = len(shards)

    def body(*refs):
        x_refs, o_refs = refs[:nt], refs[nt:2 * nt]
        send_sems, recv_sems, local_sems = refs[2 * nt:]
        x, y, c, _ = _mesh_place()
        me, sibling, xn, yn, diag = (x, y, c), (x, y, 1 - c), (1 - x, y, c), (x, 1 - y, c), (1 - x, 1 - y, c)
        _handshake([sibling, xn, yn])

        def win(t, block):
            bx, by, bc = block
            return _window(o_refs[t], kinds[t], 4 * bx + 2 * by + bc, shards[t].shape)

        def copy(k, t, src, dst, to):
            return pltpu.make_async_remote_copy(src_ref=src, dst_ref=dst, send_sem=send_sems.at[k, t],
                                                recv_sem=recv_sems.at[k, t], device_id=to, device_id_type=MESH)

        def piece(k, t, block, hf, to, from_shard=False):
            dst = _half(win(t, block), hf)
            return copy(k, t, _half(x_refs[t], hf) if from_shard else dst, dst, to)

        mine = [pltpu.make_async_copy(x_refs[t], win(t, me), local_sems.at[t]) for t in range(nt)]
        sent = []

        def start(cp):
            cp.start()
            sent.append(cp)

        for cp in mine:
            cp.start()
        for t in range(nt):
            start(copy(0, t, x_refs[t], win(t, me), sibling))
        for hf_x, hf_y in ((0, 1), (1, 0)):
            for t in range(nt):
                start(piece(1 + hf_x, t, me, hf_x, xn, True))
                start(piece(3 + hf_y, t, me, hf_y, yn, True))
        for k, block, hf, onward, k_sib in ((1, xn, 0, (5, yn), 7), (4, yn, 1, (6, xn), 10), (2, xn, 1, None, 8),
                                           (3, yn, 0, None, 9), (5, diag, 0, None, 11), (6, diag, 1, None, 12)):
            for t in range(nt):
                piece(k, t, block, hf, me).wait_recv()
                if onward is not None:
                    start(piece(onward[0], t, block, hf, onward[1]))
                start(piece(k_sib, t, block, hf, sibling))
        for t in range(nt):
            copy(0, t, x_refs[t], win(t, sibling), me).wait_recv()
        for k_sib, block, hf in ((7, xn, 0), (10, yn, 1), (8, xn, 1), (9, yn, 0), (11, diag, 0), (12, diag, 1)):
            for t in range(nt):
                bx, by, _ = block
                piece(k_sib, t, (bx, by, 1 - c), hf, me).wait_recv()
        for cp in sent:
            cp.wait_send()
        for cp in mine:
            cp.wait()

    return _sequencer_call(
        body, shards, [SDS(_gathered(s.shape, k), s.dtype) for s, k in zip(shards, kinds)],
        [pltpu.SemaphoreType.DMA((13, nt)), pltpu.SemaphoreType.DMA((13, nt)), pltpu.SemaphoreType.DMA((nt,))],
        name=name, collective_id=collective_id)


def _part_shape(shape, kind):
    if kind == "row":
        return tuple(shape[:-2]) + (shape[-2] // N_DEV, shape[-1])
    return tuple(shape[2:])


def _pair_exchange(grads, kinds, *, name, collective_id):
    nt = len(grads)
    part = [_part_shape(g.shape, k) for g, k in zip(grads, kinds)]

    def body(*refs):
        g_refs, o_refs = refs[:nt], refs[nt:2 * nt]
        send_sems, recv_sems = refs[2 * nt:]
        x, y, c, _ = _mesh_place()
        _handshake([(x, y, 1 - c)])
        copies = []
        for t in range(nt):
            for xy in range(4):
                src = g_refs[t].at[1 - c, xy] if kinds[t] == "stack" else _window(g_refs[t], kinds[t], 2 * xy + 1 - c, part[t])
                copies.append(pltpu.make_async_remote_copy(
                    src_ref=src, dst_ref=o_refs[t].at[xy], send_sem=send_sems.at[xy, t], recv_sem=recv_sems.at[xy, t],
                    device_id=(x, y, 1 - c), device_id_type=MESH))
        for cp in copies:
            cp.start()
        for cp in copies:
            cp.wait()

    return _sequencer_call(
        body, grads, [SDS((4,) + p, g.dtype) for p, g in zip(part, grads)],
        [pltpu.SemaphoreType.DMA((4, nt)), pltpu.SemaphoreType.DMA((4, nt))], name=name, collective_id=collective_id)


def _chip_exchange(parts, small, *, name, collective_id):
    nt = len(parts)
    if small is None:
        def body_plain(*refs):
            s_refs, o_refs = refs[:nt], refs[nt:2 * nt]
            send_sems, recv_sems = refs[2 * nt:]
            x, y, c, chips = _mesh_place()
            _handshake([(*chip, c) for chip in chips])
            copies = [pltpu.make_async_remote_copy(
                src_ref=s_refs[t].at[2 * chip[0] + chip[1]], dst_ref=o_refs[t].at[j],
                send_sem=send_sems.at[j, t], recv_sem=recv_sems.at[j, t], device_id=(*chip, c), device_id_type=MESH)
                for j, chip in enumerate(chips) for t in range(nt)]
            for cp in copies:
                cp.start()
            for cp in copies:
                cp.wait()

        return _sequencer_call(
            body_plain, parts, [SDS((3,) + s.shape[1:], s.dtype) for s in parts],
            [pltpu.SemaphoreType.DMA((3, nt)), pltpu.SemaphoreType.DMA((3, nt))], name=name, collective_id=collective_id)

    def body(*refs):
        s_refs, small_ref = refs[:nt], refs[nt]
        o_refs, small_all = refs[nt + 1:2 * nt + 1], refs[2 * nt + 1]
        send_sems, recv_sems, small_send, small_recv, local_sem = refs[2 * nt + 2:]
        x, y, c, chips = _mesh_place()
        _handshake([(px, py, pc) for px in (x, 1 - x) for py in (y, 1 - y) for pc in (c, 1 - c)][1:])

        def copy(j, t, chip):
            return pltpu.make_async_remote_copy(
                src_ref=s_refs[t].at[2 * chip[0] + chip[1]], dst_ref=o_refs[t].at[j],
                send_sem=send_sems.at[j, t], recv_sem=recv_sems.at[j, t], device_id=(*chip, c), device_id_type=MESH)

        flips = [(fx, fy, fc) for fx in (0, 1) for fy in (0, 1) for fc in (0, 1)][1:]

        def small_copy(k):
            fx, fy, fc = flips[k]
            to = (x ^ fx if fx else x, y ^ fy if fy else y, c ^ fc if fc else c)
            rows = small_all.at[4 * x + 2 * y + c]
            return pltpu.make_async_remote_copy(
                src_ref=small_ref, dst_ref=rows, send_sem=small_send.at[k], recv_sem=small_recv.at[k],
                device_id=to, device_id_type=MESH)

        own = pltpu.make_async_copy(small_ref, small_all.at[4 * x + 2 * y + c], local_sem)
        own.start()
        copies = [copy(j, t, chip) for j, chip in enumerate(chips) for t in range(nt)]
        smalls = [small_copy(k) for k in range(7)]
        for cp in smalls + copies:
            cp.start()
        for cp in smalls + copies:
            cp.wait()
        own.wait()

    return _sequencer_call(
        body, list(parts) + [small],
        [SDS((3,) + s.shape[1:], s.dtype) for s in parts] + [SDS((N_DEV,) + small.shape, small.dtype)],
        [pltpu.SemaphoreType.DMA((3, nt)), pltpu.SemaphoreType.DMA((3, nt)),
         pltpu.SemaphoreType.DMA((7,)), pltpu.SemaphoreType.DMA((7,)), pltpu.SemaphoreType.DMA],
        name=name, collective_id=collective_id)


def _pair_sum(grad, theirs, kind, c, *, name):
    if kind == "row":
        r, l = theirs.shape[-2:]
        n = theirs.size // (4 * r * l)
        grad, theirs = grad.reshape(n, N_DEV * r, l), theirs.reshape(4, n, r, l)
        mine_spec = pl.BlockSpec((n, r, l), lambda xy, c_ref: (0, 2 * xy + c_ref[0], 0))
    else:
        r, l = theirs.shape[-2:]
        n = theirs.size // (4 * r * l)
        theirs = theirs.reshape(4, n, r, l)
        grad = grad.reshape(2, 4, n, r, l)
        mine_spec = pl.BlockSpec((None, None, n, r, l), lambda xy, c_ref: (c_ref[0], xy, 0, 0, 0))

    def body(c_ref, a_ref, b_ref, o_ref):
        o_ref[...] = (a_ref[...].astype(F32) + b_ref[...].astype(F32)).astype(BF16)

    part = pl.BlockSpec((None, n, r, l), lambda xy, c_ref: (xy, 0, 0, 0))
    return pl.pallas_call(
        body, name=name,
        grid_spec=pltpu.PrefetchScalarGridSpec(num_scalar_prefetch=1, grid=(4,), in_specs=[mine_spec, part], out_specs=part),
        out_shape=SDS((4, n, r, l), BF16), compiler_params=_params("parallel"),
    )(c, grad, theirs)


def _adamw(w, g, m, v):
    m = ADAM_B1 * m + (1.0 - ADAM_B1) * g
    v = ADAM_B2 * v + (1.0 - ADAM_B2) * jnp.square(g)
    m_hat = m / (1.0 - ADAM_B1 ** ADAM_STEP)
    v_hat = v / (1.0 - ADAM_B2 ** ADAM_STEP)
    delta = -ADAM_LR * (m_hat / (jnp.sqrt(v_hat) + ADAM_EPS) + ADAM_WD * w)
    return delta, m, v


def _adam_big(own, others, mat, xy, w, m, v, *, tr, name, piece=None, into=None):
    nw, r, l = w.shape
    lp = own.shape[-1]
    n_into = 0 if into is None else 4

    def body(xy_ref, own_ref, oth_ref, w_ref, m_ref, v_ref, *refs):
        g_out, d_out, m_out, v_out = refs[n_into:]
        g = own_ref[0, 0].astype(F32)
        for j in range(3):
            g = g + oth_ref[j, 0].astype(F32)
        g = g[:, :l]
        delta, m_new, v_new = _adamw(w_ref[0], g, m_ref[0], v_ref[0])
        g_out[0] = g
        d_out[0] = delta
        m_out[0] = m_new
        v_out[0] = v_new

    row = (lambda p: p) if piece is None else (lambda p: piece)
    at = (lambda p: mat * nw + p) if piece is None else (lambda p: mat)
    blk = pl.BlockSpec((1, tr, l), lambda p, i, xy_ref: (row(p), i, 0))
    return pl.pallas_call(
        body, name=name,
        grid_spec=pltpu.PrefetchScalarGridSpec(
            num_scalar_prefetch=1, grid=(nw if piece is None else 1, r // tr),
            in_specs=[pl.BlockSpec((1, 1, tr, lp), lambda p, i, xy_ref: (xy_ref[0], at(p), i, 0)),
                      pl.BlockSpec((3, 1, tr, lp), lambda p, i, xy_ref: (0, at(p), i, 0)), blk, blk, blk]
            + [pl.BlockSpec(memory_space=pl.ANY)] * n_into,
            out_specs=[blk, blk, blk, blk]),
        out_shape=[SDS(w.shape, F32)] * 4, input_output_aliases={6 + k: k for k in range(n_into)},
        compiler_params=_params("parallel", "parallel"),
    )(xy, own, others, w, m, v, *(into or ()))


def _small_layout(shapes):
    out, at = [], 0
    for r, c in shapes:
        rows = c // 128 if (r == 1 and c > 128) else r
        out.append((at, rows))
        at += -(-rows // 8) * 8
    return out, at


def _pack_small(parts, *, name):
    shapes = [a.shape for a in parts]
    layout, total = _small_layout(shapes)

    def body(*refs):
        o_ref = refs[-1]
        o_ref[...] = jnp.zeros_like(o_ref)
        for x_ref, (r, c), (at, rows) in zip(refs, shapes, layout):
            if r == 1 and c > 128:
                for k in range(rows):
                    o_ref[at + k:at + k + 1, :] = x_ref[:, k * 128:(k + 1) * 128]
            else:
                o_ref[at:at + r, 0:c] = x_ref[...]

    return pl.pallas_call(body, name=name, out_shape=SDS((total, 128), F32))(*parts)


def _adam_small(g_all, ws, ms, vs, *, name):
    n = len(ws)
    shapes = [w.shape for w in ws]
    layout, _ = _small_layout(shapes)

    def body(g_ref, *refs):
        w_refs, m_refs, v_refs, outs = refs[:n], refs[n:2 * n], refs[2 * n:3 * n], refs[3 * n:]
        g_sum = g_ref[0]
        for k in range(1, N_DEV):
            g_sum = g_sum + g_ref[k]
        for i, ((r, c), (at, rows)) in enumerate(zip(shapes, layout)):
            if r == 1 and c > 128:
                g = jnp.concatenate([g_sum[at + k:at + k + 1, :] for k in range(rows)], axis=1)
            else:
                g = g_sum[at:at + r, 0:c]
            delta, m_new, v_new = _adamw(w_refs[i][...], g, m_refs[i][...], v_refs[i][...])
            for q, val in enumerate((g, delta, m_new, v_new)):
                outs[4 * i + q][...] = val

    flat = pl.pallas_call(body, name=name, out_shape=[SDS(s, F32) for s in shapes for _ in range(4)])(g_all, *ws, *ms, *vs)
    return [flat[4 * i:4 * i + 4] for i in range(n)]


def kernel(x, mem, ffn1_norm, ffn1_w_gate, ffn1_w_up, ffn1_w_down, mix_norm, mem_norm, w_in, w_mem_kv, swa_q_norm, swa_k_norm, swa_sinks, rel_bias, gla_w_gate_up, gla_b_gate, gla_out_norm, mem_q_norm, mem_k_norm, w_out, ffn2_norm, ffn2_w_gate, ffn2_w_up, ffn2_w_down, loss_target, m_ffn1_norm, m_ffn1_w_gate, m_ffn1_w_up, m_ffn1_w_down, m_mix_norm, m_mem_norm, m_w_in, m_w_mem_kv, m_swa_q_norm, m_swa_k_norm, m_swa_sinks, m_rel_bias, m_gla_w_gate_up, m_gla_b_gate, m_gla_out_norm, m_mem_q_norm, m_mem_k_norm, m_w_out, m_ffn2_norm, m_ffn2_w_gate, m_ffn2_w_up, m_ffn2_w_down, v_ffn1_norm, v_ffn1_w_gate, v_ffn1_w_up, v_ffn1_w_down, v_mix_norm, v_mem_norm, v_w_in, v_w_mem_kv, v_swa_q_norm, v_swa_k_norm, v_swa_sinks, v_rel_bias, v_gla_w_gate_up, v_gla_b_gate, v_gla_out_norm, v_mem_q_norm, v_mem_k_norm, v_w_out, v_ffn2_norm, v_ffn2_w_gate, v_ffn2_w_up, v_ffn2_w_down):
    xi, yi, ci = lax.axis_index("x"), lax.axis_index("y"), lax.axis_index("c")
    c_arr = jnp.reshape(ci, (1,)).astype(jnp.int32)
    xy_arr = jnp.reshape(2 * xi + yi, (1,)).astype(jnp.int32)
    d = x.shape[-1]

    half_h = ffn1_w_gate.shape[-1] // 2

    def ffn_shards(wg_s, wu_s, wd_s):
        w3_s = jnp.concatenate([wg_s.transpose(0, 2, 1), wu_s.transpose(0, 2, 1), wd_s], axis=0)
        return jnp.pad(w3_s.reshape(3, 2, half_h, d), ((0, 0), (0, 0), (0, FFN_HALF_PAD - half_h), (0, 0))).astype(BF16)

    def gather_ffn(wg_s, wu_s, wd_s, name, collective_id, after):
        w3_s, _ = lax.optimization_barrier((ffn_shards(wg_s, wu_s, wd_s), after))
        return _all_gather([w3_s], ["row"], name=name, collective_id=collective_id)[0].reshape(3, -1, d)

    w3_s = ffn_shards(ffn1_w_gate, ffn1_w_up, ffn1_w_down)
    w3_1a = _all_gather([w3_s[:, 0]], ["row"], name="gather_ffn1a", collective_id=0)[0]
    w3_s, _ = lax.optimization_barrier((w3_s, w3_1a))
    w3_1b = _all_gather([w3_s[:, 1]], ["row"], name="gather_ffn1b", collective_id=11)[0]

    def gather_mix(after):
        mix_s = lax.optimization_barrier((w_in[0].astype(BF16), w_mem_kv[0].astype(BF16), w_out[0].astype(BF16),
                                          (w3_1b, after)))[:3]
        win_all, wkv, wout = _all_gather(list(mix_s), ["stack", "row", "row"], name="gather_mix", collective_id=1)
        return _pack_win(win_all, tr=256, name="pack_w_in"), wkv, wout

    def gather_ffn2(after):
        return gather_ffn(ffn2_w_gate, ffn2_w_up, ffn2_w_down, "gather_ffn2", 2, after)

    small_w = [ffn1_norm, mix_norm, mem_norm, ffn2_norm, swa_q_norm, swa_k_norm, swa_sinks[0], rel_bias,
               gla_w_gate_up[0], gla_b_gate, gla_out_norm, mem_q_norm, mem_k_norm]
    collective_ids = {"ffn2": (3, 4), "mix": (5, 6), "ffn1a": (7, 8), "ffn1b": (9, 10)}
    reduced, small_box = {}, {}

    def on_grads(group, grads, carry, small=None):
        if group == "mix":
            dwin_p, dwkv, dwout = grads
            grads = [_unpack_win(dwin_p, tr=256, name="unpack_dw_in"), dwkv, dwout]
            kinds = ["stack", "row", "row"]
        else:
            kinds = ["row"]
        if reduced:
            earlier = list(reduced.values())[-1][1]
            *grads, _ = lax.optimization_barrier((*grads, earlier[0]))
        id_pair, id_chip = collective_ids[group]
        from_sibling = _pair_exchange(grads, kinds, name=f"pair_exchange_{group}", collective_id=id_pair)
        chip_sums = [_pair_sum(g, theirs, k, c_arr, name=f"pair_sum_{group}_{t}")
                     for t, (g, theirs, k) in enumerate(zip(grads, from_sibling, kinds))]
        if carry is not None:
            *chip_sums, carry = lax.optimization_barrier((*chip_sums, carry))
        if small is None:
            from_chips = _chip_exchange(chip_sums, None, name=f"chip_exchange_{group}", collective_id=id_chip)
        else:
            packed = _pack_small(small, name=f"pack_small_{group}")
            *from_chips, small_all = _chip_exchange(chip_sums, packed, name=f"chip_exchange_{group}",
                                                    collective_id=id_chip)
            small_box[group] = small_all
        reduced[group] = (chip_sums, from_chips)
        return carry

    grad_x = _local_step(x[0], mem[0], loss_target[0], small_w, ((w3_1a, w3_1b), gather_mix, gather_ffn2), on_grads)

    big_w = {"ffn1_w_gate": ("ffn1", 0, 0, True, ffn1_w_gate, m_ffn1_w_gate, v_ffn1_w_gate),
             "ffn1_w_up": ("ffn1", 0, 1, True, ffn1_w_up, m_ffn1_w_up, v_ffn1_w_up),
             "ffn1_w_down": ("ffn1", 0, 2, False, ffn1_w_down, m_ffn1_w_down, v_ffn1_w_down),
             "w_in": ("mix", 0, 0, False, w_in, m_w_in, v_w_in),
             "w_mem_kv": ("mix", 1, 0, False, w_mem_kv, m_w_mem_kv, v_w_mem_kv),
             "w_out": ("mix", 2, 0, False, w_out, m_w_out, v_w_out),
             "ffn2_w_gate": ("ffn2", 0, 0, True, ffn2_w_gate, m_ffn2_w_gate, v_ffn2_w_gate),
             "ffn2_w_up": ("ffn2", 0, 1, True, ffn2_w_up, m_ffn2_w_up, v_ffn2_w_up),
             "ffn2_w_down": ("ffn2", 0, 2, False, ffn2_w_down, m_ffn2_w_down, v_ffn2_w_down)}
    res = {}
    for nm, (group, t, mat, transposed, w, m, v) in big_w.items():
        shape = w.shape
        if transposed:
            w, m, v = (a.transpose(0, 2, 1) for a in (w, m, v))
        if group != "mix":
            w, m, v = (a.reshape(2, half_h, d) for a in (w, m, v))
        r = w.shape[1]
        tr = 256 if r % 256 == 0 else r
        if group == "ffn1":
            out = None
            for piece, half in enumerate(("ffn1a", "ffn1b")):
                out = _adam_big(reduced[half][0][t], reduced[half][1][t], mat, xy_arr, w, m, v, tr=tr,
                                name=f"adam_{nm}_{piece}", piece=piece, into=out)
        else:
            out = _adam_big(reduced[group][0][t], reduced[group][1][t], mat, xy_arr, w, m, v, tr=tr, name=f"adam_{nm}")
        if transposed:
            out = [a.reshape(1, -1, d).transpose(0, 2, 1) for a in out]
        res[nm] = [a.reshape(shape) for a in out]
    small_names = ["ffn1_norm", "mix_norm", "mem_norm", "ffn2_norm", "swa_q_norm", "swa_k_norm", "swa_sinks", "rel_bias",
                   "gla_w_gate_up", "gla_b_gate", "gla_out_norm", "mem_q_norm", "mem_k_norm"]
    small_m = [m_ffn1_norm, m_mix_norm, m_mem_norm, m_ffn2_norm, m_swa_q_norm, m_swa_k_norm, m_swa_sinks, m_rel_bias,
               m_gla_w_gate_up, m_gla_b_gate, m_gla_out_norm, m_mem_q_norm, m_mem_k_norm]
    small_v = [v_ffn1_norm, v_mix_norm, v_mem_norm, v_ffn2_norm, v_swa_q_norm, v_swa_k_norm, v_swa_sinks, v_rel_bias,
               v_gla_w_gate_up, v_gla_b_gate, v_gla_out_norm, v_mem_q_norm, v_mem_k_norm]
    small_full = [ffn1_norm, mix_norm, mem_norm, ffn2_norm, swa_q_norm, swa_k_norm, swa_sinks, rel_bias,
                  gla_w_gate_up, gla_b_gate, gla_out_norm, mem_q_norm, mem_k_norm]
    zero = jnp.zeros((1, 1), F32)
    two_d = lambda a: a.reshape(a.shape[-2:])
    for group, sel in (("ffn1a", slice(1, None)), ("ffn1b", slice(0, 1))):
        extra = [zero] if group == "ffn1a" else []
        ws, ms, vs = ([two_d(a) for a in arrs[sel]] + extra for arrs in (small_full, small_m, small_v))
        updated = _adam_small(small_box[group], ws, ms, vs, name=f"adam_small_{group}")
        for nm, full, out in zip(small_names[sel], small_full[sel], updated):
            res[nm] = [a.reshape(full.shape) for a in out]
        if extra:
            loss = updated[-1][0].reshape(())

    order = ["ffn1_norm", "ffn1_w_gate", "ffn1_w_up", "ffn1_w_down", "mix_norm", "mem_norm", "w_in", "w_mem_kv",
             "swa_q_norm", "swa_k_norm", "swa_sinks", "rel_bias", "gla_w_gate_up", "gla_b_gate", "gla_out_norm",
             "mem_q_norm", "mem_k_norm", "w_out", "ffn2_norm", "ffn2_w_gate", "ffn2_w_up", "ffn2_w_down"]
    outs = [loss, grad_x[None]]
    for q in range(4):
        outs += [res[nm][q] for nm in order]
    return tuple(outs)
```

```python
import functools
import math

import numpy as np
import jax
import jax.numpy as jnp
from jax import lax
from jax.experimental import pallas as pl
from jax.experimental.pallas import tpu as pltpu
from jax.experimental.pallas import tpu_sc as plsc

F32 = jnp.float32
BF16 = jnp.bfloat16
SDS = jax.ShapeDtypeStruct

EPS = 1e-6
HEAD_DIM = 64
SWA_HEADS = 8
SWA_KV_HEADS = 2
SWA_GROUP = SWA_HEADS // SWA_KV_HEADS
BLOCK = 128
N_BUCKETS = 32
MAX_DISTANCE = 128
GLA_HEADS = 4
GLA_DK = 32
GLA_DV = 64
GLA_RANK = 16
GLA_TAU = 16.0
GLA_CHUNK = 32
MEM_HEADS = 4
SWA_Q_W = SWA_HEADS * HEAD_DIM
SWA_KV_W = SWA_KV_HEADS * HEAD_DIM
GLA_QK_W = GLA_HEADS * GLA_DK
GLA_V_W = GLA_HEADS * GLA_DV
MEM_Q_W = MEM_HEADS * HEAD_DIM
IN_W = 1808
IN_W_PAD = 1920
COL_SQ, COL_SKV, COL_GQ, COL_GK, COL_GV, COL_GG, COL_MQ, COL_GLR = 0, 512, 768, 896, 1024, 1280, 1536, 1792

ADAM_LR = 0.001
ADAM_B1 = 0.9
ADAM_B2 = 0.999
ADAM_EPS = 1e-08
ADAM_WD = 0.01
ADAM_STEP = 10

N_DEV = 8
VMEM_LIMIT_BYTES = 56 * 1024 * 1024
MESH = pl.DeviceIdType.MESH


def _params(*sem):
    return pltpu.CompilerParams(dimension_semantics=sem or None, vmem_limit_bytes=VMEM_LIMIT_BYTES)


def _dot(a, b, ta, tb, precision=None):
    dims = (((0 if ta else 1,), (1 if tb else 0,)), ((), ()))
    return lax.dot_general(a, b, dims, preferred_element_type=F32, precision=precision)


def _mm_raw(a, b, ta=False, tb=False):
    return _dot(a.astype(BF16), b.astype(BF16), ta, tb)


def _mmf_raw(a, b, ta=False, tb=False):
    return _dot(a, b, ta, tb, lax.Precision.HIGHEST)


def _make_mm(raw):
    @functools.partial(jax.custom_vjp, nondiff_argnums=(2, 3))
    def mm(a, b, ta=False, tb=False):
        return raw(a, b, ta, tb)

    def fwd(a, b, ta, tb):
        return raw(a, b, ta, tb), (a, b)

    def bwd(ta, tb, res, g):
        a, b = res
        da = raw(b, g, tb, True) if ta else raw(g, b, False, not tb)
        db = raw(g, a, True, ta) if tb else raw(a, g, not ta, False)
        return da, db

    mm.defvjp(fwd, bwd)
    return mm


_mm = _make_mm(_mm_raw)
_mmf = _make_mm(_mmf_raw)


def _mm3(a, b, ta=False, tb=False):
    a_hi, b_hi = a.astype(BF16).astype(F32), b.astype(BF16).astype(F32)
    return _mm(a_hi, b_hi, ta, tb) + _mm(a_hi, b - b_hi, ta, tb) + _mm(a - a_hi, b_hi, ta, tb)


def _rms(x, g):
    return x * lax.rsqrt(jnp.mean(x * x, axis=-1, keepdims=True) + EPS) * g


def _silu_mul(g, u):
    return jax.nn.silu(g) * u


def _log_sigmoid(z):
    return jnp.minimum(z, 0.0) - jnp.log(1.0 + jnp.exp(-jnp.abs(z)))


def _matmul(a_list, b, *, ta=False, tb=False, tm, tn, b_blocks=None, res=None, scale=1.0, out_dtype=F32, name):
    if not isinstance(a_list, (list, tuple)):
        a_list = [a_list]
    n_a = len(a_list)
    m = a_list[0].shape[1] if ta else a_list[0].shape[0]
    ks = [a.shape[0] if ta else a.shape[1] for a in a_list]
    n = b.shape[0] if tb else b.shape[1]
    if b_blocks is None:
        assert n_a == 1
        b_blocks = [0]
    tm, tn = min(tm, m), min(tn, n)
    assert m % tm == 0 and n % tn == 0, (m, n, tm, tn)

    def body(*refs):
        a_refs, b_refs = refs[:n_a], refs[n_a:2 * n_a]
        r_ref = refs[2 * n_a] if res is not None else None
        o_ref = refs[-1]
        acc = _mm_raw(a_refs[0][...], b_refs[0][...], ta, tb)
        for k in range(1, n_a):
            acc = acc + _mm_raw(a_refs[k][...], b_refs[k][...], ta, tb)
        if scale != 1.0:
            acc = acc * scale
        if r_ref is not None:
            acc = r_ref[...] + acc
        o_ref[...] = acc.astype(out_dtype)

    in_specs = []
    for k in ks:
        in_specs.append(pl.BlockSpec((k, tm), lambda i, j: (0, i)) if ta else pl.BlockSpec((tm, k), lambda i, j: (i, 0)))
    for k, blk in zip(ks, b_blocks):
        if tb:
            in_specs.append(pl.BlockSpec((tn, k), functools.partial(lambda i, j, blk: (j, blk), blk=blk)))
        else:
            in_specs.append(pl.BlockSpec((k, tn), functools.partial(lambda i, j, blk: (blk, j), blk=blk)))
    args = list(a_list) + [b] * n_a
    if res is not None:
        in_specs.append(pl.BlockSpec((tm, tn), lambda i, j: (i, j)))
        args.append(res)
    return pl.pallas_call(
        body, name=name, grid=(m // tm, n // tn), in_specs=in_specs,
        out_specs=pl.BlockSpec((tm, tn), lambda i, j: (i, j)), out_shape=SDS((m, n), out_dtype),
        compiler_params=_params("parallel", "parallel"),
    )(*args)


def _win_pieces(w):
    glr_lo, glr_hi = COL_MQ, COL_MQ + GLA_RANK
    out = []
    for j in range(N_DEV):
        for lo, hi, shift in ((0, glr_lo, 0), (glr_lo, glr_hi, COL_GLR - glr_lo), (glr_hi, IN_W, COL_MQ - glr_hi)):
            s, e = max(j * w, lo), min((j + 1) * w, hi)
            if s < e:
                out.append((j, s - j * w, e - j * w, s + shift))
    return out


def _pack_win(win_all, *, tr, name):
    _, d, w = win_all.shape

    def body(i_ref, o_ref):
        for j, a, b, dst in _win_pieces(w):
            o_ref[:, dst:dst + b - a] = i_ref[j][:, a:b]
        o_ref[:, IN_W:] = jnp.zeros((tr, IN_W_PAD - IN_W), o_ref.dtype)

    return pl.pallas_call(
        body, name=name, grid=(d // tr,), in_specs=[pl.BlockSpec((N_DEV, tr, w), lambda i: (0, i, 0))],
        out_specs=pl.BlockSpec((tr, IN_W_PAD), lambda i: (i, 0)), out_shape=SDS((d, IN_W_PAD), win_all.dtype),
        compiler_params=_params("parallel"),
    )(win_all)


def _unpack_win(dwin_p, *, tr, name):
    d = dwin_p.shape[0]
    w = IN_W // N_DEV

    def body(i_ref, o_ref):
        for j, a, b, src in _win_pieces(w):
            o_ref[j % 2, j // 2, :, a:b] = i_ref[:, src:src + b - a]

    return pl.pallas_call(
        body, name=name, grid=(d // tr,), in_specs=[pl.BlockSpec((tr, IN_W_PAD), lambda i: (i, 0))],
        out_specs=pl.BlockSpec((2, 4, tr, w), lambda i: (0, 0, i, 0)), out_shape=SDS((2, 4, d, w), dwin_p.dtype),
        compiler_params=_params("parallel"),
    )(dwin_p)


def _rms_fwd(x, g, *, tm, name):
    s, d = x.shape

    def body(x_ref, g_ref, h_ref):
        h_ref[...] = _rms(x_ref[...], g_ref[...]).astype(BF16)

    return pl.pallas_call(
        body, name=name, grid=(s // tm,),
        in_specs=[pl.BlockSpec((tm, d), lambda i: (i, 0)), pl.BlockSpec((1, d), lambda i: (0, 0))],
        out_specs=pl.BlockSpec((tm, d), lambda i: (i, 0)), out_shape=SDS((s, d), BF16),
        compiler_params=_params("parallel"),
    )(x, g)


def _rms_bwd(x, g, dh, dres, *, tm, name):
    s, d = x.shape
    want_dx = dres is not None
    product = isinstance(dh, tuple)

    def body(*refs):
        n_dh = 2 if product else 1
        x_ref, g_ref = refs[:2]
        dh_refs, rest = refs[2:2 + n_dh], refs[2 + n_dh:]
        if want_dx:
            dres_ref, dx_ref, dxb_ref, dg_ref = rest
        else:
            dg_ref, = rest
        dh_tile = _mm_raw(dh_refs[0][...], dh_refs[1][...], False, True) if product else dh_refs[0][...]
        _, vjp = jax.vjp(_rms, x_ref[...], g_ref[...])
        dx, dg = vjp(dh_tile)
        if want_dx:
            dx = dres_ref[...] + dx
            dx_ref[...] = dx
            dxb_ref[...] = dx.astype(BF16)

        @pl.when(pl.program_id(0) == 0)
        def _():
            dg_ref[...] = jnp.zeros_like(dg_ref)

        dg_ref[...] += dg

    row = pl.BlockSpec((tm, d), lambda i: (i, 0))
    vec = pl.BlockSpec((1, d), lambda i: (0, 0))
    if product:
        k = dh[0].shape[1]
        dh_specs, dh_args = [pl.BlockSpec((tm, k), lambda i: (i, 0)), pl.BlockSpec((d, k), lambda i: (0, 0))], list(dh)
    else:
        dh_specs, dh_args = [row], [dh]
    if want_dx:
        return pl.pallas_call(
            body, name=name, grid=(s // tm,), in_specs=[row, vec] + dh_specs + [row], out_specs=[row, row, vec],
            out_shape=[SDS((s, d), F32), SDS((s, d), BF16), SDS((1, d), F32)], compiler_params=_params("arbitrary"),
        )(x, g, *dh_args, dres)
    return None, None, pl.pallas_call(
        body, name=name, grid=(s // tm,), in_specs=[row, vec] + dh_specs, out_specs=vec,
        out_shape=SDS((1, d), F32), compiler_params=_params("arbitrary"),
    )(x, g, *dh_args)


FFN_TN = 256
FFN_TN_FWD = 512
FFN_HALF_PAD = 192


def _ffn_fwd(x, gain, w3, tag, *, tm=1024, next_gain=None, target=None, start=None, partial=False):
    s, d = x.shape
    f = w3.shape[1]
    tn = FFN_TN_FWD if f % FFN_TN_FWD == 0 else FFN_TN
    nj = f // tn
    tm = min(tm, s)
    n_extra = (next_gain is not None) + (target is not None) + 2 * (start is not None)

    def body(*refs):
        x_ref, gain_ref, wg_ref, wu_ref, wd_ref = refs[:5]
        extra, outs = refs[5:5 + n_extra], refs[5 + n_extra:-1]
        acc_s = refs[-1]
        g_ref, u_ref = outs[-2:]
        h_ref = extra[-2] if start is not None else outs[-3]
        i, j = pl.program_id(0), pl.program_id(1)

        @pl.when(j == 0)
        def _():
            if start is None:
                h_ref[...] = _rms(x_ref[...], gain_ref[...]).astype(BF16)
                acc_s[...] = jnp.zeros_like(acc_s)
            else:
                acc_s[...] = extra[-1][...]

        hv = h_ref[...]
        g = _mm_raw(hv, wg_ref[...], False, True)
        u = _mm_raw(hv, wu_ref[...], False, True)
        g_ref[...] = g.astype(BF16)
        u_ref[...] = u.astype(BF16)
        acc_s[...] += _mm_raw(_silu_mul(g, u), wd_ref[...])

        @pl.when(j == nj - 1)
        def _():
            y = acc_s[...] if partial else x_ref[...] + 0.5 * acc_s[...]
            if target is None:
                outs[0][...] = y
                if next_gain is not None:
                    outs[1][...] = _rms(y, extra[0][...]).astype(BF16)
            else:
                dy_ref, dyb_ref, loss_ref = outs[:3]
                diff = y - extra[0][...]
                dy_ref[...] = diff * (1.0 / d)
                dyb_ref[...] = (diff * (1.0 / d)).astype(BF16)
                part = 0.5 * jnp.sum(jnp.mean(diff * diff, axis=-1, keepdims=True), axis=0, keepdims=True)

                @pl.when(i == 0)
                def _():
                    loss_ref[...] = part

                @pl.when(i > 0)
                def _():
                    loss_ref[...] += part

    row = pl.BlockSpec((tm, d), lambda i, j: (i, 0))
    vec = pl.BlockSpec((1, d), lambda i, j: (0, 0))
    tile = pl.BlockSpec((tm, tn), lambda i, j: (i, j))
    in_specs = [row, vec] + [pl.BlockSpec((None, tn, d), functools.partial(lambda i, j, k: (k, j, 0), k=k)) for k in range(3)]
    args = [x, gain, w3, w3, w3]
    if target is None:
        out_specs, out_shape = [row], [SDS((s, d), F32)]
        if next_gain is not None:
            in_specs.append(vec)
            args.append(next_gain)
            out_specs.append(row)
            out_shape.append(SDS((s, d), BF16))
    else:
        in_specs.append(row)
        args.append(target)
        out_specs = [row, row, pl.BlockSpec((1, 1), lambda i, j: (0, 0))]
        out_shape = [SDS((s, d), F32), SDS((s, d), BF16), SDS((1, 1), F32)]
    if start is None:
        out_specs.append(row)
        out_shape.append(SDS((s, d), BF16))
    else:
        in_specs += [row, row]
        args += list(start)
    *head, g, u = pl.pallas_call(
        body, name=f"{tag}_fwd", grid=(s // tm, nj), in_specs=in_specs,
        out_specs=out_specs + [tile, tile],
        out_shape=out_shape + [SDS((s, f), BF16), SDS((s, f), BF16)],
        scratch_shapes=[pltpu.VMEM((tm, d), F32)],
        compiler_params=_params("arbitrary", "arbitrary"),
    )(*args)
    if start is None:
        *head, h = head
    else:
        h = start[0]
    return head, (h, g, u)


def _ffn_bwd_part(dyb, w3, saved, first, count, dh_init, *, name):
    h, g, u = saved
    s, d = h.shape
    tn = FFN_TN

    def body(*refs):
        if dh_init is None:
            dy_ref, h_ref, wd_ref, wg_ref, wu_ref, g_ref, u_ref, dh_ref, dw3_ref, dg_s, du_s, a_s = refs
        else:
            dy_ref, h_ref, wd_ref, wg_ref, wu_ref, g_ref, u_ref, dh0_ref, dh_ref, dw3_ref, dg_s, du_s, a_s = refs
        j = pl.program_id(0)

        @pl.when(j == 0)
        def _():
            dh_ref[...] = jnp.zeros_like(dh_ref) if dh_init is None else dh0_ref[...]
            for ref in (dg_s, du_s, a_s):
                ref[...] = jnp.zeros_like(ref)

        now, before = j % 2, 1 - j % 2
        dyv = dy_ref[...]
        hv = h_ref[...]
        dg, du, a = dg_s[before], du_s[before], a_s[before]
        dh_ref[...] += _mm_raw(dg, wg_ref[...]) + _mm_raw(du, wu_ref[...])
        dw3_ref[0] = _mm_raw(dg, hv, True, False).astype(BF16)
        dw3_ref[1] = _mm_raw(du, hv, True, False).astype(BF16)
        dw3_ref[2] = (_mm_raw(a, dyv, True, False) * 0.5).astype(BF16)

        da = _mm_raw(dyv, wd_ref[...], False, True) * 0.5
        a, vjp = jax.vjp(_silu_mul, g_ref[...].astype(F32), u_ref[...].astype(F32))
        dg, du = vjp(da)
        dg_s[now] = dg.astype(BF16)
        du_s[now] = du.astype(BF16)
        a_s[now] = a.astype(BF16)

    this = lambda j: first + jnp.minimum(j, count - 1)
    last = lambda j: first + jnp.maximum(j - 1, 0)
    full = pl.BlockSpec((s, d), lambda j: (0, 0))
    once = pl.BlockSpec((s, d), lambda j: (0, 0), pipeline_mode=pl.Buffered(1))
    tile = pl.BlockSpec((s, tn), lambda j: (0, this(j)))
    in_specs = [once, once, pl.BlockSpec((None, tn, d), lambda j: (2, this(j), 0)),
                pl.BlockSpec((None, tn, d), lambda j: (0, last(j), 0)), pl.BlockSpec((None, tn, d), lambda j: (1, last(j), 0)),
                tile, tile]
    args = [dyb, h, w3, w3, w3, g, u]
    if dh_init is not None:
        in_specs.append(once)
        args.append(dh_init)
    return pl.pallas_call(
        body, name=name, grid=(count + 1,), in_specs=in_specs,
        out_specs=[full, pl.BlockSpec((3, tn, d), lambda j: (0, jnp.maximum(j - 1, 0), 0))],
        out_shape=[SDS((s, d), F32), SDS((3, count * tn, d), BF16)],
        scratch_shapes=[pltpu.VMEM((2, s, tn), BF16)] * 3,
        compiler_params=_params("arbitrary"),
    )(*args)


def _bucket_table():
    qi = np.arange(BLOCK)[:, None]
    kj = np.arange(2 * BLOCK)[None, :]
    dist = np.maximum(qi + BLOCK - kj, 0)
    max_exact = N_BUCKETS // 2
    d = np.maximum(dist, 1).astype(np.float32)
    large = max_exact + (np.log(d / np.float32(max_exact)) / np.float32(math.log(MAX_DISTANCE / max_exact))
                         * np.float32(N_BUCKETS - max_exact)).astype(np.int32)
    large = np.minimum(large, N_BUCKETS - 1)
    band = np.where(dist < max_exact, dist, large).astype(np.int32)
    return np.where(np.tril(np.ones((BLOCK, BLOCK), bool)), band[:, BLOCK:], band[:, :BLOCK])


SWA_STACK = SWA_GROUP * BLOCK


def _swa_masks(n):
    qi = lax.broadcasted_iota(jnp.int32, (SWA_STACK, BLOCK), 0) % BLOCK
    kj = lax.broadcasted_iota(jnp.int32, (SWA_STACK, BLOCK), 1)
    own = kj <= qi
    return own, own | (n > 0)


def _swa_group(q, kp, kc, vp, vc, qg, kg, sink, bias, own, valid):
    qn = _rms(q, qg)
    s = jnp.where(own, _mm(qn, _rms(kc, kg), False, True), _mm(qn, _rms(kp, kg), False, True))
    s = s * (HEAD_DIM ** -0.5) + bias
    s = jnp.where(valid, s, -jnp.inf)
    m = lax.stop_gradient(jnp.maximum(jnp.max(s, axis=-1, keepdims=True), sink))
    p = jnp.exp(s - m)
    p = p / (jnp.sum(p, axis=-1, keepdims=True) + jnp.exp(sink - m))
    return _mm(jnp.where(own, p, 0.0), vc) + _mm(jnp.where(own, 0.0, p), vp)


def _swa_bias_table(rb_ref, bucket, bias_s):
    for h in range(SWA_HEADS):
        acc = jnp.zeros((BLOCK, BLOCK), F32)
        for b in range(N_BUCKETS):
            acc = jnp.where(bucket == b, rb_ref[b, h], acc)
        bias_s[h // SWA_GROUP, (h % SWA_GROUP) * BLOCK:(h % SWA_GROUP + 1) * BLOCK, :] = acc


def _swa_stack(ref, g):
    return jnp.concatenate([ref[:, (g * SWA_GROUP + hh) * HEAD_DIM:(g * SWA_GROUP + hh + 1) * HEAD_DIM]
                            for hh in range(SWA_GROUP)], axis=0)


def _swa_unstack(ref, g, stacked):
    for hh in range(SWA_GROUP):
        h = g * SWA_GROUP + hh
        ref[:, h * HEAD_DIM:(h + 1) * HEAD_DIM] = stacked[hh * BLOCK:(hh + 1) * BLOCK]


def _swa_sink_column(sink_ref, g):
    head = lax.broadcasted_iota(jnp.int32, (SWA_STACK, 1), 0) // BLOCK
    col = jnp.zeros((SWA_STACK, 1), F32)
    for hh in range(SWA_GROUP):
        col = jnp.where(head == hh, sink_ref[g * SWA_GROUP + hh], col)
    return col


def _swa_band(kvp_ref, kvc_ref, g):
    k = slice(g * HEAD_DIM, (g + 1) * HEAD_DIM)
    v = slice(SWA_KV_W + g * HEAD_DIM, SWA_KV_W + (g + 1) * HEAD_DIM)
    return kvp_ref[:, k], kvc_ref[:, k], kvp_ref[:, v], kvc_ref[:, v]


def _swa_specs(order):
    kvc = COL_SKV // (2 * SWA_KV_W)
    return [
        pl.BlockSpec((BLOCK, SWA_Q_W), lambda t: (order(t), 0)),
        pl.BlockSpec((BLOCK, 2 * SWA_KV_W), lambda t: (jnp.maximum(order(t) - 1, 0), kvc)),
        pl.BlockSpec((BLOCK, 2 * SWA_KV_W), lambda t: (order(t), kvc)),
        pl.BlockSpec((1, HEAD_DIM), lambda t: (0, 0)),
        pl.BlockSpec((1, HEAD_DIM), lambda t: (0, 0)),
        pl.BlockSpec(memory_space=pltpu.SMEM),
        pl.BlockSpec(memory_space=pltpu.SMEM),
        pl.BlockSpec((BLOCK, BLOCK), lambda t: (0, 0)),
    ]


def _swa_fwd(p, qg, kg, sinks, rel_bias, *, name):
    s = p.shape[0]
    nb = s // BLOCK

    def body(q_ref, kvp_ref, kvc_ref, qg_ref, kg_ref, sink_ref, rb_ref, bucket_ref, y_ref, bias_s):
        n = pl.program_id(0)

        @pl.when(n == 0)
        def _():
            _swa_bias_table(rb_ref, bucket_ref[...], bias_s)

        own, valid = _swa_masks(n)
        for g in range(SWA_KV_HEADS):
            out = _swa_group(_swa_stack(q_ref, g), *_swa_band(kvp_ref, kvc_ref, g), qg_ref[...], kg_ref[...],
                             _swa_sink_column(sink_ref, g), bias_s[g], own, valid)
            _swa_unstack(y_ref, g, out)

    return pl.pallas_call(
        body, name=name, grid=(nb,), in_specs=_swa_specs(lambda t: t),
        out_specs=pl.BlockSpec((BLOCK, SWA_Q_W), lambda t: (t, 0)), out_shape=SDS((s, SWA_Q_W), F32),
        scratch_shapes=[pltpu.VMEM((SWA_KV_HEADS, SWA_STACK, BLOCK), F32)],
        compiler_params=_params("arbitrary"),
    )(p, p, p, qg, kg, sinks, rel_bias, jnp.asarray(_bucket_table()))


def _swa_bwd(p, qg, kg, sinks, rel_bias, dy_all, *, name):
    s = p.shape[0]
    nb = s // BLOCK

    def body(q_ref, kvp_ref, kvc_ref, qg_ref, kg_ref, sink_ref, rb_ref, bucket_ref, dy_ref,
             dq_ref, dkv_ref, dqg_ref, dkg_ref, dsink_ref, drb_ref, bias_s, dbias_s, carry_s):
        t = pl.program_id(0)
        n = nb - 1 - t

        @pl.when(t == 0)
        def _():
            _swa_bias_table(rb_ref, bucket_ref[...], bias_s)
            dbias_s[...] = jnp.zeros_like(dbias_s)
            carry_s[...] = jnp.zeros_like(carry_s)
            dqg_ref[...] = jnp.zeros_like(dqg_ref)
            dkg_ref[...] = jnp.zeros_like(dkg_ref)
            dsink_ref[...] = jnp.zeros_like(dsink_ref)
            drb_ref[...] = jnp.zeros_like(drb_ref)

        own, valid = _swa_masks(n)
        lane = lax.broadcasted_iota(jnp.int32, (1, BLOCK), 1)
        dqg = jnp.zeros((1, HEAD_DIM), F32)
        dkg = jnp.zeros((1, HEAD_DIM), F32)
        dsink_vec = jnp.zeros((1, BLOCK), F32)
        for g in range(SWA_KV_HEADS):
            _, vjp = jax.vjp(functools.partial(_swa_group, own=own, valid=valid), _swa_stack(q_ref, g),
                             *_swa_band(kvp_ref, kvc_ref, g), qg_ref[...], kg_ref[...], _swa_sink_column(sink_ref, g),
                             bias_s[g])
            dq, dkp, dkc, dvp, dvc, dqg_g, dkg_g, dsink_col, dbias = vjp(_swa_stack(dy_ref, g))
            _swa_unstack(dq_ref, g, dq)
            dqg += dqg_g
            dkg += dkg_g
            dbias_s[g] += dbias
            for hh in range(SWA_GROUP):
                dsink_h = jnp.sum(dsink_col[hh * BLOCK:(hh + 1) * BLOCK], axis=0, keepdims=True)
                dsink_vec += jnp.where(lane == g * SWA_GROUP + hh, dsink_h, 0.0)
            lo = g * HEAD_DIM
            dkv_ref[:, lo:lo + HEAD_DIM] = dkc + carry_s[g]
            carry_s[g] = dkp
            lo += SWA_KV_W
            dkv_ref[:, lo:lo + HEAD_DIM] = dvc + carry_s[SWA_KV_HEADS + g]
            carry_s[SWA_KV_HEADS + g] = dvp
        dqg_ref[...] += dqg
        dkg_ref[...] += dkg
        dsink_ref[...] += dsink_vec

        @pl.when(t == nb - 1)
        def _():
            bucket = bucket_ref[...]
            row = lax.broadcasted_iota(jnp.int32, (N_BUCKETS, BLOCK), 0)
            col = lax.broadcasted_iota(jnp.int32, (N_BUCKETS, BLOCK), 1)
            acc = jnp.zeros((N_BUCKETS, BLOCK), F32)
            for h in range(SWA_HEADS):
                dbias = dbias_s[h // SWA_GROUP, (h % SWA_GROUP) * BLOCK:(h % SWA_GROUP + 1) * BLOCK, :]
                for b in range(N_BUCKETS):
                    part = jnp.sum(jnp.where(bucket == b, dbias, 0.0), axis=1, keepdims=True)
                    val = jnp.sum(part, axis=0, keepdims=True)
                    acc = acc + jnp.where((row == b) & (col == h), val, 0.0)
            drb_ref[...] = acc

    order = lambda t: nb - 1 - t
    vec = pl.BlockSpec((1, HEAD_DIM), lambda t: (0, 0))
    return pl.pallas_call(
        body, name=name, grid=(nb,),
        in_specs=_swa_specs(order) + [pl.BlockSpec((BLOCK, SWA_Q_W), lambda t: (order(t), 0))],
        out_specs=[pl.BlockSpec((BLOCK, SWA_Q_W), lambda t: (order(t), 0)),
                   pl.BlockSpec((BLOCK, 2 * SWA_KV_W), lambda t: (order(t), 0)),
                   vec, vec, pl.BlockSpec((1, BLOCK), lambda t: (0, 0)),
                   pl.BlockSpec((N_BUCKETS, BLOCK), lambda t: (0, 0))],
        out_shape=[SDS((s, SWA_Q_W), F32), SDS((s, 2 * SWA_KV_W), F32), SDS((1, HEAD_DIM), F32),
                   SDS((1, HEAD_DIM), F32), SDS((1, BLOCK), F32), SDS((N_BUCKETS, BLOCK), F32)],
        scratch_shapes=[pltpu.VMEM((SWA_KV_HEADS, SWA_STACK, BLOCK), F32),
                        pltpu.VMEM((SWA_KV_HEADS, SWA_STACK, BLOCK), F32),
                        pltpu.VMEM((2 * SWA_KV_HEADS, BLOCK, HEAD_DIM), F32)],
        compiler_params=_params("arbitrary"),
    )(p, p, p, qg, kg, sinks, rel_bias, jnp.asarray(_bucket_table()), dy_all)


def _mem_head(q, k, v, qg, kg):
    qn = _rms(q, qg)
    kn = _rms(k, kg)
    s = _mm(qn, kn, False, True) * (HEAD_DIM ** -0.5)
    m = lax.stop_gradient(jnp.max(s, axis=-1, keepdims=True))
    e = jnp.exp(s - m)
    return _mm(e / jnp.sum(e, axis=-1, keepdims=True), v)


def _mem_fwd(p, kv, qg, kg, *, tq, name):
    s = p.shape[0]
    m = kv.shape[0]

    def body(q_ref, kv_ref, qg_ref, kg_ref, y_ref):
        for h in range(MEM_HEADS):
            cols = slice(h * HEAD_DIM, (h + 1) * HEAD_DIM)
            vcols = slice(MEM_Q_W + h * HEAD_DIM, MEM_Q_W + (h + 1) * HEAD_DIM)
            y_ref[:, cols] = _mem_head(q_ref[:, cols], kv_ref[:, cols], kv_ref[:, vcols], qg_ref[...], kg_ref[...])

    vec = pl.BlockSpec((1, HEAD_DIM), lambda t: (0, 0))
    return pl.pallas_call(
        body, name=name, grid=(s // tq,),
        in_specs=[pl.BlockSpec((tq, MEM_Q_W), lambda t: (t, COL_MQ // MEM_Q_W)),
                  pl.BlockSpec((m, 2 * MEM_Q_W), lambda t: (0, 0)), vec, vec],
        out_specs=pl.BlockSpec((tq, MEM_Q_W), lambda t: (t, 0)), out_shape=SDS((s, MEM_Q_W), F32),
        compiler_params=_params("parallel"),
    )(p, kv, qg, kg)


def _mem_bwd(p, kv, qg, kg, dy_all, *, tq, name):
    s = p.shape[0]
    m = kv.shape[0]

    def body(q_ref, kv_ref, qg_ref, kg_ref, dy_ref, dq_ref, dkv_ref, dqg_ref, dkg_ref):
        @pl.when(pl.program_id(0) == 0)
        def _():
            dkv_ref[...] = jnp.zeros_like(dkv_ref)
            dqg_ref[...] = jnp.zeros_like(dqg_ref)
            dkg_ref[...] = jnp.zeros_like(dkg_ref)

        dqg = jnp.zeros((1, HEAD_DIM), F32)
        dkg = jnp.zeros((1, HEAD_DIM), F32)
        for h in range(MEM_HEADS):
            cols = slice(h * HEAD_DIM, (h + 1) * HEAD_DIM)
            vcols = slice(MEM_Q_W + h * HEAD_DIM, MEM_Q_W + (h + 1) * HEAD_DIM)
            _, vjp = jax.vjp(_mem_head, q_ref[:, cols], kv_ref[:, cols], kv_ref[:, vcols], qg_ref[...], kg_ref[...])
            dq, dk, dv, dqg_h, dkg_h = vjp(dy_ref[:, cols])
            dq_ref[:, cols] = dq
            dkv_ref[:, cols] += dk
            dkv_ref[:, vcols] += dv
            dqg += dqg_h
            dkg += dkg_h
        dqg_ref[...] += dqg
        dkg_ref[...] += dkg

    vec = pl.BlockSpec((1, HEAD_DIM), lambda t: (0, 0))
    full = pl.BlockSpec((m, 2 * MEM_Q_W), lambda t: (0, 0))
    dy_col = (SWA_Q_W + GLA_V_W) // MEM_Q_W
    return pl.pallas_call(
        body, name=name, grid=(s // tq,),
        in_specs=[pl.BlockSpec((tq, MEM_Q_W), lambda t: (t, COL_MQ // MEM_Q_W)), full, vec, vec,
                  pl.BlockSpec((tq, MEM_Q_W), lambda t: (t, dy_col))],
        out_specs=[pl.BlockSpec((tq, MEM_Q_W), lambda t: (t, 0)), full, vec, vec],
        out_shape=[SDS((s, MEM_Q_W), F32), SDS((m, 2 * MEM_Q_W), F32), SDS((1, HEAD_DIM), F32), SDS((1, HEAD_DIM), F32)],
        compiler_params=_params("arbitrary"),
    )(p, kv, qg, kg, dy_all)


GLA_ROWS = 256


GLA_GROUP = 4


def _gla_consts():
    c, h, r = GLA_CHUNK, GLA_HEADS, GLA_GROUP * GLA_CHUNK
    i2 = lax.broadcasted_iota(jnp.int32, (c, c), 0)
    j2 = lax.broadcasted_iota(jnp.int32, (c, c), 1)
    slab_q = lax.broadcasted_iota(jnp.int32, (h, r, GLA_QK_W), 0)
    lane_q = lax.broadcasted_iota(jnp.int32, (h, r, GLA_QK_W), 2)
    row_a = lax.broadcasted_iota(jnp.int32, (h * r, r), 0) % r
    col_a = lax.broadcasted_iota(jnp.int32, (h * r, r), 1)
    slab_o = lax.broadcasted_iota(jnp.int32, (h, r, GLA_V_W), 0)
    lane_o = lax.broadcasted_iota(jnp.int32, (h, r, GLA_V_W), 2)
    row_s = lax.broadcasted_iota(jnp.int32, (GLA_V_W, GLA_QK_W), 0)
    col_s = lax.broadcasted_iota(jnp.int32, (GLA_V_W, GLA_QK_W), 1)
    return dict(
        ltri=(j2 <= i2).astype(F32),
        m_q=(slab_q == lane_q // GLA_DK).astype(F32),
        causal=(col_a <= row_a) & (col_a // c == row_a // c),
        m_o=(slab_o == lane_o // GLA_DV).astype(F32),
        m_s=(row_s // GLA_DV == col_s // GLA_DK).astype(F32),
    )


def _gla_step(q, k, v, z, bg, st, c):
    h = GLA_HEADS
    kt, qt, qe, decay = [], [], [], []
    for qc, kc, zc in zip(q, k, z):
        la = _log_sigmoid(zc + bg) * (1.0 / GLA_TAU)
        b = _mmf(c["ltri"], la)
        bl = jnp.sum(la, axis=0, keepdims=True)
        qs = qc * (GLA_DK ** -0.5)
        kt.append(kc * jnp.exp(bl - b))
        qt.append(qs * jnp.exp(b - bl))
        qe.append(qs * jnp.exp(b))
        decay.append(jnp.exp(bl))
    o_intra = []
    rows = GLA_GROUP * GLA_CHUNK
    for lo in range(0, len(q), GLA_GROUP):
        qt_all, kt_all, v_all = (jnp.concatenate(parts[lo:lo + GLA_GROUP], axis=0) for parts in (qt, kt, v))
        q_stack = (jnp.broadcast_to(qt_all[None], (h, rows, GLA_QK_W)) * c["m_q"]).reshape(h * rows, GLA_QK_W)
        a = jnp.where(c["causal"], _mm3(q_stack, kt_all, False, True), 0.0)
        o_stack = _mm(a, v_all)
        o_intra.append(jnp.sum(o_stack.reshape(h, rows, GLA_V_W) * c["m_o"], axis=0))
    o_intra = jnp.concatenate(o_intra, axis=0)
    o_inter = []
    for qec, ktc, vc, dc in zip(qe, kt, v, decay):
        o_inter.append(_mm(qec, st, False, True))
        st = st * dc + _mm(vc, ktc, True, False) * c["m_s"]
    return o_intra + jnp.concatenate(o_inter, axis=0), st


def _gla_post(o, gg, gain, g64):
    ms = _mmf(o * o, g64) * (1.0 / GLA_DV)
    return o * lax.rsqrt(ms + EPS) * gain * jax.nn.silu(gg)


def _gla_g64():
    r = lax.broadcasted_iota(jnp.int32, (GLA_V_W, GLA_V_W), 0)
    c = lax.broadcasted_iota(jnp.int32, (GLA_V_W, GLA_V_W), 1)
    return (r // GLA_DV == c // GLA_DV).astype(F32)


def _gla_in_specs(order):
    r = GLA_ROWS
    return [
        pl.BlockSpec((r, GLA_QK_W), lambda t: (order(t), COL_GQ // GLA_QK_W)),
        pl.BlockSpec((r, GLA_QK_W), lambda t: (order(t), COL_GK // GLA_QK_W)),
        pl.BlockSpec((r, GLA_V_W), lambda t: (order(t), COL_GV // GLA_V_W)),
        pl.BlockSpec((r, GLA_V_W), lambda t: (order(t), COL_GG // GLA_V_W)),
        pl.BlockSpec((r, GLA_QK_W), lambda t: (order(t), 0)),
        pl.BlockSpec((1, GLA_QK_W), lambda t: (0, 0)),
        pl.BlockSpec((1, GLA_V_W), lambda t: (0, 0)),
    ]


def _gla_pieces(q_ref, k_ref, v_ref, z_ref, cps):
    chunk = lambda ref: [ref[ci * GLA_CHUNK:(ci + 1) * GLA_CHUNK, :] for ci in range(cps)]
    return chunk(q_ref), chunk(k_ref), chunk(v_ref), chunk(z_ref)


def _gla_fwd(p, z, bg, gain, *, name):
    s = p.shape[0]
    r = GLA_ROWS
    cps = r // GLA_CHUNK

    def body(q_ref, k_ref, v_ref, gg_ref, z_ref, bg_ref, gain_ref, y_ref, oraw_ref, stsave_ref, st_s):
        @pl.when(pl.program_id(0) == 0)
        def _():
            st_s[...] = jnp.zeros_like(st_s)

        st = st_s[...]
        stsave_ref[0] = st
        o, st = _gla_step(*_gla_pieces(q_ref, k_ref, v_ref, z_ref, cps), bg_ref[...], st, _gla_consts())
        oraw_ref[...] = o
        st_s[...] = st
        y_ref[...] = _gla_post(o, gg_ref[...], gain_ref[...], _gla_g64())

    rowv = pl.BlockSpec((r, GLA_V_W), lambda t: (t, 0))
    return pl.pallas_call(
        body, name=name, grid=(s // r,), in_specs=_gla_in_specs(lambda t: t),
        out_specs=[rowv, rowv, pl.BlockSpec((1, GLA_V_W, GLA_QK_W), lambda t: (t, 0, 0))],
        out_shape=[SDS((s, GLA_V_W), F32), SDS((s, GLA_V_W), F32), SDS((s // r, GLA_V_W, GLA_QK_W), F32)],
        scratch_shapes=[pltpu.VMEM((GLA_V_W, GLA_QK_W), F32)],
        compiler_params=_params("arbitrary"),
    )(p, p, p, p, z, bg, gain)


def _gla_bwd(p, z, bg, gain, oraw, stsave, dy_all, *, name):
    s = p.shape[0]
    r = GLA_ROWS
    cps = r // GLA_CHUNK
    nsteps = s // r
    w_qkvg = 2 * GLA_QK_W + 2 * GLA_V_W

    def body(q_ref, k_ref, v_ref, gg_ref, z_ref, bg_ref, gain_ref, oraw_ref, stsave_ref, dy_ref,
             dqkvg_ref, dz_ref, dbg_ref, dgain_ref, dst_s):
        @pl.when(pl.program_id(0) == 0)
        def _():
            dst_s[...] = jnp.zeros_like(dst_s)
            dbg_ref[...] = jnp.zeros_like(dbg_ref)
            dgain_ref[...] = jnp.zeros_like(dgain_ref)

        _, vjp = jax.vjp(functools.partial(_gla_post, g64=_gla_g64()), oraw_ref[...], gg_ref[...], gain_ref[...])
        do, dgg, dgain = vjp(dy_ref[...])
        dqkvg_ref[:, 2 * GLA_QK_W + GLA_V_W:] = dgg
        dgain_ref[...] += dgain
        _, vjp = jax.vjp(functools.partial(_gla_step, c=_gla_consts()), *_gla_pieces(q_ref, k_ref, v_ref, z_ref, cps),
                         bg_ref[...], stsave_ref[0])
        dq, dk, dv, dz, dbg, dst = vjp((do, dst_s[...]))
        for ci in range(cps):
            rows = slice(ci * GLA_CHUNK, (ci + 1) * GLA_CHUNK)
            dqkvg_ref[rows, 0:GLA_QK_W] = dq[ci]
            dqkvg_ref[rows, GLA_QK_W:2 * GLA_QK_W] = dk[ci]
            dqkvg_ref[rows, 2 * GLA_QK_W:2 * GLA_QK_W + GLA_V_W] = dv[ci]
            dz_ref[rows, :] = dz[ci]
        dst_s[...] = dst
        dbg_ref[...] += dbg

    order = lambda t: nsteps - 1 - t
    rowv = pl.BlockSpec((r, GLA_V_W), lambda t: (order(t), 0))
    return pl.pallas_call(
        body, name=name, grid=(nsteps,),
        in_specs=_gla_in_specs(order) + [
            rowv, pl.BlockSpec((1, GLA_V_W, GLA_QK_W), lambda t: (order(t), 0, 0)),
            pl.BlockSpec((r, GLA_V_W), lambda t: (order(t), SWA_Q_W // GLA_V_W))],
        out_specs=[pl.BlockSpec((r, w_qkvg), lambda t: (order(t), 0)), pl.BlockSpec((r, GLA_QK_W), lambda t: (order(t), 0)),
                   pl.BlockSpec((1, GLA_QK_W), lambda t: (0, 0)), pl.BlockSpec((1, GLA_V_W), lambda t: (0, 0))],
        out_shape=[SDS((s, w_qkvg), F32), SDS((s, GLA_QK_W), F32), SDS((1, GLA_QK_W), F32), SDS((1, GLA_V_W), F32)],
        scratch_shapes=[pltpu.VMEM((GLA_V_W, GLA_QK_W), F32)],
        compiler_params=_params("arbitrary"),
    )(p, p, p, p, z, bg, gain, oraw, stsave, dy_all)


def _local_step(x, mem, target, small, big, on_grads):
    g1, gmix, gmem, g2, sqg, skg, sinks, rel_bias, wgu, bg, gla_gain, mqg, mkg = small
    (w3_1a, w3_1b), gather_mix, gather_ffn2 = big
    wgu_pad = jnp.zeros((GLA_QK_W, GLA_QK_W), BF16).at[:GLA_RANK].set(wgu.astype(BF16))
    gain256 = jnp.tile(gla_gain, (1, GLA_HEADS))

    (part,), saved1a = _ffn_fwd(x, g1, w3_1a, "ffn1a", partial=True)
    win_p, wkv, wout = gather_mix(part)
    (x1, h), saved1b = _ffn_fwd(x, g1, w3_1b, "ffn1b", next_gain=gmix, start=(saved1a[0], part))
    w3_2 = gather_ffn2((wout, x1))
    p = _matmul(h, win_p, tm=512, tn=IN_W_PAD, name="mix_in")
    hm = _rms_fwd(mem, gmem, tm=256, name="mem_rms")
    kv = _matmul(hm, wkv, tm=256, tn=512, name="mem_kv")
    p_glr = p[:, COL_GLR:]
    z = _matmul(p_glr, wgu_pad, tm=1024, tn=GLA_QK_W, name="gla_gate")
    y_swa = _swa_fwd(p, sqg, skg, sinks, rel_bias, name="swa_fwd")
    y_gla, oraw, stsave = _gla_fwd(p, z, bg, gain256, name="gla_fwd")
    y_mem = _mem_fwd(p, kv, mqg, mkg, tq=512, name="mem_fwd")
    x2 = _matmul([y_swa, y_gla, y_mem], wout, b_blocks=[0, 2, 3], tm=512, tn=1024, res=x1, name="mix_out")
    (dy, dyb, loss), saved2 = _ffn_fwd(x2, g2, w3_2, "ffn2", target=target)

    dh2, dw3_2 = _ffn_bwd_part(dyb, w3_2, saved2, 0, w3_2.shape[1] // FFN_TN, None, name="ffn2_bwd")
    dx2, dx2b, dg2 = _rms_bwd(x2, g2, dh2, dy, tm=512, name="ffn2_drms")
    dx2b = on_grads("ffn2", [dw3_2], dx2b)
    dy_all = _matmul(dx2b, wout, tb=True, tm=512, tn=1024, name="mix_dy")
    dwout = _matmul(jnp.concatenate([y_swa, y_gla, y_mem], axis=1), dx2b, ta=True, tm=512, tn=1024, out_dtype=BF16,
                    name="mix_dw_out")
    dq_swa, dkv_swa, dsqg, dskg, dsink, drb = _swa_bwd(p, sqg, skg, sinks, rel_bias, dy_all, name="swa_bwd")
    dqkvg, dz, dbg, dgain256 = _gla_bwd(p, z, bg, gain256, oraw, stsave, dy_all, name="gla_bwd")
    dmq, dkv_mem, dmqg, dmkg = _mem_bwd(p, kv, mqg, mkg, dy_all, tq=512, name="mem_bwd")
    dglr = _matmul(dz, wgu_pad, tb=True, tm=1024, tn=GLA_QK_W, name="gla_gate_dx")
    dwgu_pad = _matmul(p_glr, dz, ta=True, tm=GLA_QK_W, tn=GLA_QK_W, name="gla_gate_dw")
    dp = jnp.concatenate([dq_swa, dkv_swa, dqkvg, dmq, dglr], axis=1)
    dwin_p = _matmul(h, dp, ta=True, tm=1024, tn=640, out_dtype=BF16, name="mix_dw_in")
    dx1, dx1b, dgmix = _rms_bwd(x1, gmix, (dp, win_p), dx2, tm=512, name="mix_dh_drms")
    dwkv = _matmul(hm, dkv_mem, ta=True, tm=512, tn=512, out_dtype=BF16, name="mem_dw_kv")
    dx1b = on_grads("mix", (dwin_p, dwkv, dwout), dx1b)
    _, _, dgmem = _rms_bwd(mem, gmem, (dkv_mem, wkv), None, tm=256, name="mem_dh_drms")
    tiles = w3_1a.shape[1] // FFN_TN
    dh1, dw3_1a = _ffn_bwd_part(dx1b, w3_1a, saved1a, 0, tiles, None, name="ffn1_bwd_a")
    dgla_gain = dgain256.reshape(GLA_HEADS, GLA_DV).sum(axis=0, keepdims=True)
    dsmall = [dgmix, dgmem, dg2, dsqg, dskg, dsink[:, :SWA_HEADS], drb[:, :SWA_HEADS], dwgu_pad[:GLA_RANK], dbg,
              dgla_gain, dmqg, dmkg, loss]
    dh1, dgmem, dwgu_pad = on_grads("ffn1a", [dw3_1a[:, None]], (dh1, dgmem, dwgu_pad), small=dsmall)
    dh1, dw3_1b = _ffn_bwd_part(dx1b, w3_1b, saved1b, 0, tiles, dh1, name="ffn1_bwd_b")
    dx, _, dg1 = _rms_bwd(x, g1, dh1, dx1, tm=512, name="ffn1_drms")
    on_grads("ffn1b", [dw3_1b[:, None]], None, small=[dg1])
    return dx


def _mesh_place():
    x, y, c = lax.axis_index("x"), lax.axis_index("y"), lax.axis_index("c")
    other_chips = [(1 - x, y), (x, 1 - y), (1 - x, 1 - y)]
    return x, y, c, other_chips


def _handshake(peers):
    barrier = pltpu.get_barrier_semaphore()
    for peer in peers:
        pl.semaphore_signal(barrier, inc=1, device_id=peer, device_id_type=MESH)
    pl.semaphore_wait(barrier, len(peers))


def _sequencer_call(body, operands, out_shapes, sems, *, name, collective_id):
    return pl.kernel(
        body, name=name, out_type=out_shapes, mesh=plsc.ScalarSubcoreMesh(axis_name="sequencer", num_cores=1),
        scratch_types=sems, compiler_params=pltpu.CompilerParams(collective_id=collective_id),
    )(*operands)


def _window(ref, kind, slot, shape):
    if kind == "row":
        rows = pl.ds(pl.multiple_of(slot * shape[-2], 8), shape[-2])
        return ref.at[(slice(None),) * (len(shape) - 2) + (rows,)]
    return ref.at[slot]


def _gathered(shape, kind):
    if kind == "row":
        return tuple(shape[:-2]) + (N_DEV * shape[-2], shape[-1])
    return (N_DEV,) + tuple(shape)


def _half(view, hf):
    if len(view.shape) == 4:
        return view.at[:, hf]
    n = view.shape[-2] // 2
    return view.at[(slice(None),) * (len(view.shape) - 2) + (pl.ds(hf * n, n),)]


def _all_gather(shards, kinds, *, name, collective_id):
    nt = len(shards)

    def body(*refs):
        x_refs, o_refs = refs[:nt], refs[nt:2 * nt]
        send_sems, recv_sems, local_sems = refs[2 * nt:]
        x, y, c, _ = _mesh_place()
        me, sibling, xn, yn, diag = (x, y, c), (x, y, 1 - c), (1 - x, y, c), (x, 1 - y, c), (1 - x, 1 - y, c)
        _handshake([sibling, xn, yn])

        def win(t, block):
            bx, by, bc = block
            return _window(o_refs[t], kinds[t], 4 * bx + 2 * by + bc, shards[t].shape)

        def copy(k, t, src, dst, to):
            return pltpu.make_async_remote_copy(src_ref=src, dst_ref=dst, send_sem=send_sems.at[k, t],
                                                recv_sem=recv_sems.at[k, t], device_id=to, device_id_type=MESH)

        def piece(k, t, block, hf, to, from_shard=False):
            dst = _half(win(t, block), hf)
            return copy(k, t, _half(x_refs[t], hf) if from_shard else dst, dst, to)

        mine = [pltpu.make_async_copy(x_refs[t], win(t, me), local_sems.at[t]) for t in range(nt)]
        sent = []

        def start(cp):
            cp.start()
            sent.append(cp)

        for cp in mine:
            cp.start()
        for t in range(nt):
            start(copy(0, t, x_refs[t], win(t, me), sibling))
        for hf_x, hf_y in ((0, 1), (1, 0)):
            for t in range(nt):
                start(piece(1 + hf_x, t, me, hf_x, xn, True))
                start(piece(3 + hf_y, t, me, hf_y, yn, True))
        for k, block, hf, onward, k_sib in ((1, xn, 0, (5, yn), 7), (4, yn, 1, (6, xn), 10), (2, xn, 1, None, 8),
                                           (3, yn, 0, None, 9), (5, diag, 0, None, 11), (6, diag, 1, None, 12)):
            for t in range(nt):
                piece(k, t, block, hf, me).wait_recv()
                if onward is not None:
                    start(piece(onward[0], t, block, hf, onward[1]))
                start(piece(k_sib, t, block, hf, sibling))
        for t in range(nt):
            copy(0, t, x_refs[t], win(t, sibling), me).wait_recv()
        for k_sib, block, hf in ((7, xn, 0), (10, yn, 1), (8, xn, 1), (9, yn, 0), (11, diag, 0), (12, diag, 1)):
            for t in range(nt):
                bx, by, _ = block
                piece(k_sib, t, (bx, by, 1 - c), hf, me).wait_recv()
        for cp in sent:
            cp.wait_send()
        for cp in mine:
            cp.wait()

    return _sequencer_call(
        body, shards, [SDS(_gathered(s.shape, k), s.dtype) for s, k in zip(shards, kinds)],
        [pltpu.SemaphoreType.DMA((13, nt)), pltpu.SemaphoreType.DMA((13, nt)), pltpu.SemaphoreType.DMA((nt,))],
        name=name, collective_id=collective_id)


def _part_shape(shape, kind):
    if kind == "row":
        return tuple(shape[:-2]) + (shape[-2] // N_DEV, shape[-1])
    return tuple(shape[2:])


def _pair_exchange(grads, kinds, *, name, collective_id):
    nt = len(grads)
    part = [_part_shape(g.shape, k) for g, k in zip(grads, kinds)]

    def body(*refs):
        g_refs, o_refs = refs[:nt], refs[nt:2 * nt]
        send_sems, recv_sems = refs[2 * nt:]
        x, y, c, _ = _mesh_place()
        _handshake([(x, y, 1 - c)])
        copies = []
        for t in range(nt):
            for xy in range(4):
                src = g_refs[t].at[1 - c, xy] if kinds[t] == "stack" else _window(g_refs[t], kinds[t], 2 * xy + 1 - c, part[t])
                copies.append(pltpu.make_async_remote_copy(
                    src_ref=src, dst_ref=o_refs[t].at[xy], send_sem=send_sems.at[xy, t], recv_sem=recv_sems.at[xy, t],
                    device_id=(x, y, 1 - c), device_id_type=MESH))
        for cp in copies:
            cp.start()
        for cp in copies:
            cp.wait()

    return _sequencer_call(
        body, grads, [SDS((4,) + p, g.dtype) for p, g in zip(part, grads)],
        [pltpu.SemaphoreType.DMA((4, nt)), pltpu.SemaphoreType.DMA((4, nt))], name=name, collective_id=collective_id)


def _chip_exchange(parts, small, *, name, collective_id):
    nt = len(parts)
    if small is None:
        def body_plain(*refs):
            s_refs, o_refs = refs[:nt], refs[nt:2 * nt]
            send_sems, recv_sems = refs[2 * nt:]
            x, y, c, chips = _mesh_place()
            _handshake([(*chip, c) for chip in chips])
            copies = [pltpu.make_async_remote_copy(
                src_ref=s_refs[t].at[2 * chip[0] + chip[1]], dst_ref=o_refs[t].at[j],
                send_sem=send_sems.at[j, t], recv_sem=recv_sems.at[j, t], device_id=(*chip, c), device_id_type=MESH)
                for j, chip in enumerate(chips) for t in range(nt)]
            for cp in copies:
                cp.start()
            for cp in copies:
                cp.wait()

        return _sequencer_call(
            body_plain, parts, [SDS((3,) + s.shape[1:], s.dtype) for s in parts],
            [pltpu.SemaphoreType.DMA((3, nt)), pltpu.SemaphoreType.DMA((3, nt))], name=name, collective_id=collective_id)

    def body(*refs):
        s_refs, small_ref = refs[:nt], refs[nt]
        o_refs, small_all = refs[nt + 1:2 * nt + 1], refs[2 * nt + 1]
        send_sems, recv_sems, small_send, small_recv, local_sem = refs[2 * nt + 2:]
        x, y, c, chips = _mesh_place()
        _handshake([(px, py, pc) for px in (x, 1 - x) for py in (y, 1 - y) for pc in (c, 1 - c)][1:])

        def copy(j, t, chip):
            return pltpu.make_async_remote_copy(
                src_ref=s_refs[t].at[2 * chip[0] + chip[1]], dst_ref=o_refs[t].at[j],
                send_sem=send_sems.at[j, t], recv_sem=recv_sems.at[j, t], device_id=(*chip, c), device_id_type=MESH)

        flips = [(fx, fy, fc) for fx in (0, 1) for fy in (0, 1) for fc in (0, 1)][1:]

        def small_copy(k):
            fx, fy, fc = flips[k]
            to = (x ^ fx if fx else x, y ^ fy if fy else y, c ^ fc if fc else c)
            rows = small_all.at[4 * x + 2 * y + c]
            return pltpu.make_async_remote_copy(
                src_ref=small_ref, dst_ref=rows, send_sem=small_send.at[k], recv_sem=small_recv.at[k],
                device_id=to, device_id_type=MESH)

        own = pltpu.make_async_copy(small_ref, small_all.at[4 * x + 2 * y + c], local_sem)
        own.start()
        copies = [copy(j, t, chip) for j, chip in enumerate(chips) for t in range(nt)]
        smalls = [small_copy(k) for k in range(7)]
        for cp in smalls + copies:
            cp.start()
        for cp in smalls + copies:
            cp.wait()
        own.wait()

    return _sequencer_call(
        body, list(parts) + [small],
        [SDS((3,) + s.shape[1:], s.dtype) for s in parts] + [SDS((N_DEV,) + small.shape, small.dtype)],
        [pltpu.SemaphoreType.DMA((3, nt)), pltpu.SemaphoreType.DMA((3, nt)),
         pltpu.SemaphoreType.DMA((7,)), pltpu.SemaphoreType.DMA((7,)), pltpu.SemaphoreType.DMA],
        name=name, collective_id=collective_id)


def _pair_sum(grad, theirs, kind, c, *, name):
    if kind == "row":
        r, l = theirs.shape[-2:]
        n = theirs.size // (4 * r * l)
        grad, theirs = grad.reshape(n, N_DEV * r, l), theirs.reshape(4, n, r, l)
        mine_spec = pl.BlockSpec((n, r, l), lambda xy, c_ref: (0, 2 * xy + c_ref[0], 0))
    else:
        r, l = theirs.shape[-2:]
        n = theirs.size // (4 * r * l)
        theirs = theirs.reshape(4, n, r, l)
        grad = grad.reshape(2, 4, n, r, l)
        mine_spec = pl.BlockSpec((None, None, n, r, l), lambda xy, c_ref: (c_ref[0], xy, 0, 0, 0))

    def body(c_ref, a_ref, b_ref, o_ref):
        o_ref[...] = (a_ref[...].astype(F32) + b_ref[...].astype(F32)).astype(BF16)

    part = pl.BlockSpec((None, n, r, l), lambda xy, c_ref: (xy, 0, 0, 0))
    return pl.pallas_call(
        body, name=name,
        grid_spec=pltpu.PrefetchScalarGridSpec(num_scalar_prefetch=1, grid=(4,), in_specs=[mine_spec, part], out_specs=part),
        out_shape=SDS((4, n, r, l), BF16), compiler_params=_params("parallel"),
    )(c, grad, theirs)


def _adamw(w, g, m, v):
    m = ADAM_B1 * m + (1.0 - ADAM_B1) * g
    v = ADAM_B2 * v + (1.0 - ADAM_B2) * jnp.square(g)
    m_hat = m / (1.0 - ADAM_B1 ** ADAM_STEP)
    v_hat = v / (1.0 - ADAM_B2 ** ADAM_STEP)
    delta = -ADAM_LR * (m_hat / (jnp.sqrt(v_hat) + ADAM_EPS) + ADAM_WD * w)
    return delta, m, v


def _adam_big(owns, others, mat, xy, w, m, v, *, tr, name):
    nw, r, l = w.shape
    lp = owns[0].shape[-1]
    nq = len(owns)
    assert nq in (1, nw)

    def body(xy_ref, *refs):
        own_refs, oth_refs = refs[:nq], refs[nq:2 * nq]
        w_ref, m_ref, v_ref, g_out, d_out, m_out, v_out = refs[2 * nq:]
        g = None
        for q in range(nq):
            gq = own_refs[q][0, 0].astype(F32)
            for j in range(3):
                gq = gq + oth_refs[q][j, 0].astype(F32)
            g = gq if g is None else jnp.where(pl.program_id(0) == q, gq, g)
        g = g[:, :l]
        delta, m_new, v_new = _adamw(w_ref[0], g, m_ref[0], v_ref[0])
        g_out[0] = g
        d_out[0] = delta
        m_out[0] = m_new
        v_out[0] = v_new

    def at(p):
        return mat * nw + p if nq == 1 else mat

    blk = pl.BlockSpec((1, tr, l), lambda p, i, xy_ref: (p, i, 0))
    return pl.pallas_call(
        body, name=name,
        grid_spec=pltpu.PrefetchScalarGridSpec(
            num_scalar_prefetch=1, grid=(nw, r // tr),
            in_specs=[pl.BlockSpec((1, 1, tr, lp), lambda p, i, xy_ref: (xy_ref[0], at(p), i, 0))] * nq
            + [pl.BlockSpec((3, 1, tr, lp), lambda p, i, xy_ref: (0, at(p), i, 0))] * nq + [blk, blk, blk],
            out_specs=[blk, blk, blk, blk]),
        out_shape=[SDS(w.shape, F32)] * 4, compiler_params=_params("parallel", "parallel"),
    )(xy, *owns, *others, w, m, v)


def _small_layout(shapes):
    out, at = [], 0
    for r, c in shapes:
        rows = c // 128 if (r == 1 and c > 128) else r
        out.append((at, rows))
        at += -(-rows // 8) * 8
    return out, at


def _pack_small(parts, *, name):
    shapes = [a.shape for a in parts]
    layout, total = _small_layout(shapes)

    def body(*refs):
        o_ref = refs[-1]
        o_ref[...] = jnp.zeros_like(o_ref)
        for x_ref, (r, c), (at, rows) in zip(refs, shapes, layout):
            if r == 1 and c > 128:
                for k in range(rows):
                    o_ref[at + k:at + k + 1, :] = x_ref[:, k * 128:(k + 1) * 128]
            else:
                o_ref[at:at + r, 0:c] = x_ref[...]

    return pl.pallas_call(body, name=name, out_shape=SDS((total, 128), F32))(*parts)


def _adam_small(g_all, ws, ms, vs, *, name):
    n = len(ws)
    shapes = [w.shape for w in ws]
    layout, _ = _small_layout(shapes)

    def body(g_ref, *refs):
        w_refs, m_refs, v_refs, outs = refs[:n], refs[n:2 * n], refs[2 * n:3 * n], refs[3 * n:]
        g_sum = g_ref[0]
        for k in range(1, N_DEV):
            g_sum = g_sum + g_ref[k]
        for i, ((r, c), (at, rows)) in enumerate(zip(shapes, layout)):
            if r == 1 and c > 128:
                g = jnp.concatenate([g_sum[at + k:at + k + 1, :] for k in range(rows)], axis=1)
            else:
                g = g_sum[at:at + r, 0:c]
            delta, m_new, v_new = _adamw(w_refs[i][...], g, m_refs[i][...], v_refs[i][...])
            for q, val in enumerate((g, delta, m_new, v_new)):
                outs[4 * i + q][...] = val

    flat = pl.pallas_call(body, name=name, out_shape=[SDS(s, F32) for s in shapes for _ in range(4)])(g_all, *ws, *ms, *vs)
    return [flat[4 * i:4 * i + 4] for i in range(n)]


def kernel(x, mem, ffn1_norm, ffn1_w_gate, ffn1_w_up, ffn1_w_down, mix_norm, mem_norm, w_in, w_mem_kv, swa_q_norm, swa_k_norm, swa_sinks, rel_bias, gla_w_gate_up, gla_b_gate, gla_out_norm, mem_q_norm, mem_k_norm, w_out, ffn2_norm, ffn2_w_gate, ffn2_w_up, ffn2_w_down, loss_target, m_ffn1_norm, m_ffn1_w_gate, m_ffn1_w_up, m_ffn1_w_down, m_mix_norm, m_mem_norm, m_w_in, m_w_mem_kv, m_swa_q_norm, m_swa_k_norm, m_swa_sinks, m_rel_bias, m_gla_w_gate_up, m_gla_b_gate, m_gla_out_norm, m_mem_q_norm, m_mem_k_norm, m_w_out, m_ffn2_norm, m_ffn2_w_gate, m_ffn2_w_up, m_ffn2_w_down, v_ffn1_norm, v_ffn1_w_gate, v_ffn1_w_up, v_ffn1_w_down, v_mix_norm, v_mem_norm, v_w_in, v_w_mem_kv, v_swa_q_norm, v_swa_k_norm, v_swa_sinks, v_rel_bias, v_gla_w_gate_up, v_gla_b_gate, v_gla_out_norm, v_mem_q_norm, v_mem_k_norm, v_w_out, v_ffn2_norm, v_ffn2_w_gate, v_ffn2_w_up, v_ffn2_w_down):
    xi, yi, ci = lax.axis_index("x"), lax.axis_index("y"), lax.axis_index("c")
    c_arr = jnp.reshape(ci, (1,)).astype(jnp.int32)
    xy_arr = jnp.reshape(2 * xi + yi, (1,)).astype(jnp.int32)
    d = x.shape[-1]

    half_h = ffn1_w_gate.shape[-1] // 2

    def ffn_shards(wg_s, wu_s, wd_s):
        w3_s = jnp.concatenate([wg_s.transpose(0, 2, 1), wu_s.transpose(0, 2, 1), wd_s], axis=0)
        return jnp.pad(w3_s.reshape(3, 2, half_h, d), ((0, 0), (0, 0), (0, FFN_HALF_PAD - half_h), (0, 0))).astype(BF16)

    def gather_ffn(wg_s, wu_s, wd_s, name, collective_id, after):
        w3_s = jnp.concatenate([wg_s.transpose(0, 2, 1), wu_s.transpose(0, 2, 1), wd_s], axis=0).astype(BF16)
        w3_s, _ = lax.optimization_barrier((w3_s, after))
        return _all_gather([w3_s], ["row"], name=name, collective_id=collective_id)[0]

    w3_s = ffn_shards(ffn1_w_gate, ffn1_w_up, ffn1_w_down)
    w3_1a = _all_gather([w3_s[:, 0]], ["row"], name="gather_ffn1a", collective_id=0)[0]
    w3_s, _ = lax.optimization_barrier((w3_s, w3_1a))
    w3_1b = _all_gather([w3_s[:, 1]], ["row"], name="gather_ffn1b", collective_id=11)[0]

    def gather_mix(after):
        mix_s = lax.optimization_barrier((w_in[0].astype(BF16), w_mem_kv[0].astype(BF16), w_out[0].astype(BF16),
                                          (w3_1b, after)))[:3]
        win_all, wkv, wout = _all_gather(list(mix_s), ["stack", "row", "row"], name="gather_mix", collective_id=1)
        return _pack_win(win_all, tr=256, name="pack_w_in"), wkv, wout

    def gather_ffn2(after):
        return gather_ffn(ffn2_w_gate, ffn2_w_up, ffn2_w_down, "gather_ffn2", 2, after)

    small_w = [ffn1_norm, mix_norm, mem_norm, ffn2_norm, swa_q_norm, swa_k_norm, swa_sinks[0], rel_bias,
               gla_w_gate_up[0], gla_b_gate, gla_out_norm, mem_q_norm, mem_k_norm]
    collective_ids = {"ffn2": (3, 4), "mix": (5, 6), "ffn1a": (7, 8), "ffn1b": (9, 10)}
    reduced, small_box = {}, {}

    def on_grads(group, grads, carry, small=None):
        if group == "mix":
            dwin_p, dwkv, dwout = grads
            grads = [_unpack_win(dwin_p, tr=256, name="unpack_dw_in"), dwkv, dwout]
            kinds = ["stack", "row", "row"]
        else:
            kinds = ["row"]
        if reduced:
            earlier = list(reduced.values())[-1][1]
            *grads, _ = lax.optimization_barrier((*grads, earlier[0]))
        id_pair, id_chip = collective_ids[group]
        from_sibling = _pair_exchange(grads, kinds, name=f"pair_exchange_{group}", collective_id=id_pair)
        chip_sums = [_pair_sum(g, theirs, k, c_arr, name=f"pair_sum_{group}_{t}")
                     for t, (g, theirs, k) in enumerate(zip(grads, from_sibling, kinds))]
        if carry is not None:
            *chip_sums, carry = lax.optimization_barrier((*chip_sums, carry))
        if small is None:
            from_chips = _chip_exchange(chip_sums, None, name=f"chip_exchange_{group}", collective_id=id_chip)
        else:
            packed = _pack_small(small, name=f"pack_small_{group}")
            *from_chips, small_all = _chip_exchange(chip_sums, packed, name=f"chip_exchange_{group}",
                                                    collective_id=id_chip)
            small_box[group] = small_all
        reduced[group] = (chip_sums, from_chips)
        return carry

    grad_x = _local_step(x[0], mem[0], loss_target[0], small_w, ((w3_1a, w3_1b), gather_mix, gather_ffn2), on_grads)

    big_w = {"ffn1_w_gate": ("ffn1", 0, 0, True, ffn1_w_gate, m_ffn1_w_gate, v_ffn1_w_gate),
             "ffn1_w_up": ("ffn1", 0, 1, True, ffn1_w_up, m_ffn1_w_up, v_ffn1_w_up),
             "ffn1_w_down": ("ffn1", 0, 2, False, ffn1_w_down, m_ffn1_w_down, v_ffn1_w_down),
             "w_in": ("mix", 0, 0, False, w_in, m_w_in, v_w_in),
             "w_mem_kv": ("mix", 1, 0, False, w_mem_kv, m_w_mem_kv, v_w_mem_kv),
             "w_out": ("mix", 2, 0, False, w_out, m_w_out, v_w_out),
             "ffn2_w_gate": ("ffn2", 0, 0, True, ffn2_w_gate, m_ffn2_w_gate, v_ffn2_w_gate),
             "ffn2_w_up": ("ffn2", 0, 1, True, ffn2_w_up, m_ffn2_w_up, v_ffn2_w_up),
             "ffn2_w_down": ("ffn2", 0, 2, False, ffn2_w_down, m_ffn2_w_down, v_ffn2_w_down)}
    res = {}
    for nm, (group, t, mat, transposed, w, m, v) in big_w.items():
        shape = w.shape
        if transposed:
            w, m, v = (a.transpose(0, 2, 1) for a in (w, m, v))
        if group == "ffn1":
            w, m, v = (a.reshape(2, half_h, d) for a in (w, m, v))
        r = w.shape[1]
        tr = 256 if r % 256 == 0 else r
        halves = ["ffn1a", "ffn1b"] if group == "ffn1" else [group]
        out = _adam_big([reduced[k][0][t] for k in halves], [reduced[k][1][t] for k in halves], mat, xy_arr, w, m, v,
                        tr=tr, name=f"adam_{nm}")
        if transposed:
            out = [a.reshape(1, -1, d).transpose(0, 2, 1) for a in out]
        res[nm] = [a.reshape(shape) for a in out]
    small_names = ["ffn1_norm", "mix_norm", "mem_norm", "ffn2_norm", "swa_q_norm", "swa_k_norm", "swa_sinks", "rel_bias",
                   "gla_w_gate_up", "gla_b_gate", "gla_out_norm", "mem_q_norm", "mem_k_norm"]
    small_m = [m_ffn1_norm, m_mix_norm, m_mem_norm, m_ffn2_norm, m_swa_q_norm, m_swa_k_norm, m_swa_sinks, m_rel_bias,
               m_gla_w_gate_up, m_gla_b_gate, m_gla_out_norm, m_mem_q_norm, m_mem_k_norm]
    small_v = [v_ffn1_norm, v_mix_norm, v_mem_norm, v_ffn2_norm, v_swa_q_norm, v_swa_k_norm, v_swa_sinks, v_rel_bias,
               v_gla_w_gate_up, v_gla_b_gate, v_gla_out_norm, v_mem_q_norm, v_mem_k_norm]
    small_full = [ffn1_norm, mix_norm, mem_norm, ffn2_norm, swa_q_norm, swa_k_norm, swa_sinks, rel_bias,
                  gla_w_gate_up, gla_b_gate, gla_out_norm, mem_q_norm, mem_k_norm]
    zero = jnp.zeros((1, 1), F32)
    two_d = lambda a: a.reshape(a.shape[-2:])
    for group, sel in (("ffn1a", slice(1, None)), ("ffn1b", slice(0, 1))):
        extra = [zero] if group == "ffn1a" else []
        ws, ms, vs = ([two_d(a) for a in arrs[sel]] + extra for arrs in (small_full, small_m, small_v))
        updated = _adam_small(small_box[group], ws, ms, vs, name=f"adam_small_{group}")
        for nm, full, out in zip(small_names[sel], small_full[sel], updated):
            res[nm] = [a.reshape(full.shape) for a in out]
        if extra:
            loss = updated[-1][0].reshape(())

    order = ["ffn1_norm", "ffn1_w_gate", "ffn1_w_up", "ffn1_w_down", "mix_norm", "mem_norm", "w_in", "w_mem_kv",
             "swa_q_norm", "swa_k_norm", "swa_sinks", "rel_bias", "gla_w_gate_up", "gla_b_gate", "gla_out_norm",
             "mem_q_norm", "mem_k_norm", "w_out", "ffn2_norm", "ffn2_w_gate", "ffn2_w_up", "ffn2_w_down"]
    outs = [loss, grad_x[None]]
    for q in range(4):
        outs += [res[nm][q] for nm in order]
    return tuple(outs)
```

```python
import functools
import math

import numpy as np
import jax
import jax.numpy as jnp
from jax import lax
from jax.experimental import pallas as pl
from jax.experimental.pallas import tpu as pltpu
from jax.experimental.pallas import tpu_sc as plsc

F32 = jnp.float32
BF16 = jnp.bfloat16
SDS = jax.ShapeDtypeStruct

EPS = 1e-6
HEAD_DIM = 64
SWA_HEADS = 8
SWA_KV_HEADS = 2
SWA_GROUP = SWA_HEADS // SWA_KV_HEADS
BLOCK = 128
N_BUCKETS = 32
MAX_DISTANCE = 128
GLA_HEADS = 4
GLA_DK = 32
GLA_DV = 64
GLA_RANK = 16
GLA_TAU = 16.0
GLA_CHUNK = 32
MEM_HEADS = 4
SWA_Q_W = SWA_HEADS * HEAD_DIM
SWA_KV_W = SWA_KV_HEADS * HEAD_DIM
GLA_QK_W = GLA_HEADS * GLA_DK
GLA_V_W = GLA_HEADS * GLA_DV
MEM_Q_W = MEM_HEADS * HEAD_DIM
IN_W = 1808
IN_W_PAD = 1920
COL_SQ, COL_SKV, COL_GQ, COL_GK, COL_GV, COL_GG, COL_MQ, COL_GLR = 0, 512, 768, 896, 1024, 1280, 1536, 1792

ADAM_LR = 0.001
ADAM_B1 = 0.9
ADAM_B2 = 0.999
ADAM_EPS = 1e-08
ADAM_WD = 0.01
ADAM_STEP = 10

N_DEV = 8
VMEM_LIMIT_BYTES = 56 * 1024 * 1024
MESH = pl.DeviceIdType.MESH


def _params(*sem):
    return pltpu.CompilerParams(dimension_semantics=sem or None, vmem_limit_bytes=VMEM_LIMIT_BYTES)


def _dot(a, b, ta, tb, precision=None):
    dims = (((0 if ta else 1,), (1 if tb else 0,)), ((), ()))
    return lax.dot_general(a, b, dims, preferred_element_type=F32, precision=precision)


def _mm_raw(a, b, ta=False, tb=False):
    return _dot(a.astype(BF16), b.astype(BF16), ta, tb)


def _mmf_raw(a, b, ta=False, tb=False):
    return _dot(a, b, ta, tb, lax.Precision.HIGHEST)


def _make_mm(raw):
    @functools.partial(jax.custom_vjp, nondiff_argnums=(2, 3))
    def mm(a, b, ta=False, tb=False):
        return raw(a, b, ta, tb)

    def fwd(a, b, ta, tb):
        return raw(a, b, ta, tb), (a, b)

    def bwd(ta, tb, res, g):
        a, b = res
        da = raw(b, g, tb, True) if ta else raw(g, b, False, not tb)
        db = raw(g, a, True, ta) if tb else raw(a, g, not ta, False)
        return da, db

    mm.defvjp(fwd, bwd)
    return mm


_mm = _make_mm(_mm_raw)
_mmf = _make_mm(_mmf_raw)


def _mm3(a, b, ta=False, tb=False):
    a_hi, b_hi = a.astype(BF16).astype(F32), b.astype(BF16).astype(F32)
    return _mm(a_hi, b_hi, ta, tb) + _mm(a_hi, b - b_hi, ta, tb) + _mm(a - a_hi, b_hi, ta, tb)


def _rms(x, g):
    return x * lax.rsqrt(jnp.mean(x * x, axis=-1, keepdims=True) + EPS) * g


def _silu_mul(g, u):
    return jax.nn.silu(g) * u


def _log_sigmoid(z):
    return jnp.minimum(z, 0.0) - jnp.log(1.0 + jnp.exp(-jnp.abs(z)))


def _matmul(a_list, b, *, ta=False, tb=False, tm, tn, b_blocks=None, res=None, scale=1.0, out_dtype=F32, name):
    if not isinstance(a_list, (list, tuple)):
        a_list = [a_list]
    n_a = len(a_list)
    m = a_list[0].shape[1] if ta else a_list[0].shape[0]
    ks = [a.shape[0] if ta else a.shape[1] for a in a_list]
    n = b.shape[0] if tb else b.shape[1]
    if b_blocks is None:
        assert n_a == 1
        b_blocks = [0]
    tm, tn = min(tm, m), min(tn, n)
    assert m % tm == 0 and n % tn == 0, (m, n, tm, tn)

    def body(*refs):
        a_refs, b_refs = refs[:n_a], refs[n_a:2 * n_a]
        r_ref = refs[2 * n_a] if res is not None else None
        o_ref = refs[-1]
        acc = _mm_raw(a_refs[0][...], b_refs[0][...], ta, tb)
        for k in range(1, n_a):
            acc = acc + _mm_raw(a_refs[k][...], b_refs[k][...], ta, tb)
        if scale != 1.0:
            acc = acc * scale
        if r_ref is not None:
            acc = r_ref[...] + acc
        o_ref[...] = acc.astype(out_dtype)

    in_specs = []
    for k in ks:
        in_specs.append(pl.BlockSpec((k, tm), lambda i, j: (0, i)) if ta else pl.BlockSpec((tm, k), lambda i, j: (i, 0)))
    for k, blk in zip(ks, b_blocks):
        if tb:
            in_specs.append(pl.BlockSpec((tn, k), functools.partial(lambda i, j, blk: (j, blk), blk=blk)))
        else:
            in_specs.append(pl.BlockSpec((k, tn), functools.partial(lambda i, j, blk: (blk, j), blk=blk)))
    args = list(a_list) + [b] * n_a
    if res is not None:
        in_specs.append(pl.BlockSpec((tm, tn), lambda i, j: (i, j)))
        args.append(res)
    return pl.pallas_call(
        body, name=name, grid=(m // tm, n // tn), in_specs=in_specs,
        out_specs=pl.BlockSpec((tm, tn), lambda i, j: (i, j)), out_shape=SDS((m, n), out_dtype),
        compiler_params=_params("parallel", "parallel"),
    )(*args)


def _win_pieces(w):
    glr_lo, glr_hi = COL_MQ, COL_MQ + GLA_RANK
    out = []
    for j in range(N_DEV):
        for lo, hi, shift in ((0, glr_lo, 0), (glr_lo, glr_hi, COL_GLR - glr_lo), (glr_hi, IN_W, COL_MQ - glr_hi)):
            s, e = max(j * w, lo), min((j + 1) * w, hi)
            if s < e:
                out.append((j, s - j * w, e - j * w, s + shift))
    return out


def _pack_win(win_all, *, tr, name):
    _, d, w = win_all.shape

    def body(i_ref, o_ref):
        for j, a, b, dst in _win_pieces(w):
            o_ref[:, dst:dst + b - a] = i_ref[j][:, a:b]
        o_ref[:, IN_W:] = jnp.zeros((tr, IN_W_PAD - IN_W), o_ref.dtype)

    return pl.pallas_call(
        body, name=name, grid=(d // tr,), in_specs=[pl.BlockSpec((N_DEV, tr, w), lambda i: (0, i, 0))],
        out_specs=pl.BlockSpec((tr, IN_W_PAD), lambda i: (i, 0)), out_shape=SDS((d, IN_W_PAD), win_all.dtype),
        compiler_params=_params("parallel"),
    )(win_all)


def _unpack_win(dwin_p, *, tr, name):
    d = dwin_p.shape[0]
    w = IN_W // N_DEV

    def body(i_ref, o_ref):
        for j, a, b, src in _win_pieces(w):
            o_ref[j % 2, j // 2, :, a:b] = i_ref[:, src:src + b - a]

    return pl.pallas_call(
        body, name=name, grid=(d // tr,), in_specs=[pl.BlockSpec((tr, IN_W_PAD), lambda i: (i, 0))],
        out_specs=pl.BlockSpec((2, 4, tr, w), lambda i: (0, 0, i, 0)), out_shape=SDS((2, 4, d, w), dwin_p.dtype),
        compiler_params=_params("parallel"),
    )(dwin_p)


def _rms_fwd(x, g, *, tm, name):
    s, d = x.shape

    def body(x_ref, g_ref, h_ref):
        h_ref[...] = _rms(x_ref[...], g_ref[...]).astype(BF16)

    return pl.pallas_call(
        body, name=name, grid=(s // tm,),
        in_specs=[pl.BlockSpec((tm, d), lambda i: (i, 0)), pl.BlockSpec((1, d), lambda i: (0, 0))],
        out_specs=pl.BlockSpec((tm, d), lambda i: (i, 0)), out_shape=SDS((s, d), BF16),
        compiler_params=_params("parallel"),
    )(x, g)


def _rms_bwd(x, g, dh, dres, *, tm, name):
    s, d = x.shape
    want_dx = dres is not None
    product = isinstance(dh, tuple)

    def body(*refs):
        n_dh = 2 if product else 1
        x_ref, g_ref = refs[:2]
        dh_refs, rest = refs[2:2 + n_dh], refs[2 + n_dh:]
        if want_dx:
            dres_ref, dx_ref, dxb_ref, dg_ref = rest
        else:
            dg_ref, = rest
        dh_tile = _mm_raw(dh_refs[0][...], dh_refs[1][...], False, True) if product else dh_refs[0][...]
        _, vjp = jax.vjp(_rms, x_ref[...], g_ref[...])
        dx, dg = vjp(dh_tile)
        if want_dx:
            dx = dres_ref[...] + dx
            dx_ref[...] = dx
            dxb_ref[...] = dx.astype(BF16)

        @pl.when(pl.program_id(0) == 0)
        def _():
            dg_ref[...] = jnp.zeros_like(dg_ref)

        dg_ref[...] += dg

    row = pl.BlockSpec((tm, d), lambda i: (i, 0))
    vec = pl.BlockSpec((1, d), lambda i: (0, 0))
    if product:
        k = dh[0].shape[1]
        dh_specs, dh_args = [pl.BlockSpec((tm, k), lambda i: (i, 0)), pl.BlockSpec((d, k), lambda i: (0, 0))], list(dh)
    else:
        dh_specs, dh_args = [row], [dh]
    if want_dx:
        return pl.pallas_call(
            body, name=name, grid=(s // tm,), in_specs=[row, vec] + dh_specs + [row], out_specs=[row, row, vec],
            out_shape=[SDS((s, d), F32), SDS((s, d), BF16), SDS((1, d), F32)], compiler_params=_params("arbitrary"),
        )(x, g, *dh_args, dres)
    return None, None, pl.pallas_call(
        body, name=name, grid=(s // tm,), in_specs=[row, vec] + dh_specs, out_specs=vec,
        out_shape=SDS((1, d), F32), compiler_params=_params("arbitrary"),
    )(x, g, *dh_args)


FFN_TN = 256
FFN_TN_FWD = 512
FFN1_FIRST = 192


def _ffn_fwd(x, gain, w3, tag, *, tm=1024, next_gain=None, target=None, start=None, partial=False):
    s, d = x.shape
    f = w3.shape[1]
    tn = FFN_TN_FWD if f % FFN_TN_FWD == 0 else FFN_TN
    nj = f // tn
    tm = min(tm, s)
    n_extra = (next_gain is not None) + (target is not None) + 2 * (start is not None)

    def body(*refs):
        x_ref, gain_ref, wg_ref, wu_ref, wd_ref = refs[:5]
        extra, outs = refs[5:5 + n_extra], refs[5 + n_extra:-1]
        acc_s = refs[-1]
        g_ref, u_ref = outs[-2:]
        h_ref = extra[-2] if start is not None else outs[-3]
        i, j = pl.program_id(0), pl.program_id(1)

        @pl.when(j == 0)
        def _():
            if start is None:
                h_ref[...] = _rms(x_ref[...], gain_ref[...]).astype(BF16)
                acc_s[...] = jnp.zeros_like(acc_s)
            else:
                acc_s[...] = extra[-1][...]

        hv = h_ref[...]
        g = _mm_raw(hv, wg_ref[...], False, True)
        u = _mm_raw(hv, wu_ref[...], False, True)
        g_ref[...] = g.astype(BF16)
        u_ref[...] = u.astype(BF16)
        acc_s[...] += _mm_raw(_silu_mul(g, u), wd_ref[...])

        @pl.when(j == nj - 1)
        def _():
            y = acc_s[...] if partial else x_ref[...] + 0.5 * acc_s[...]
            if target is None:
                outs[0][...] = y
                if next_gain is not None:
                    outs[1][...] = _rms(y, extra[0][...]).astype(BF16)
            else:
                dy_ref, dyb_ref, loss_ref = outs[:3]
                diff = y - extra[0][...]
                dy_ref[...] = diff * (1.0 / d)
                dyb_ref[...] = (diff * (1.0 / d)).astype(BF16)
                part = 0.5 * jnp.sum(jnp.mean(diff * diff, axis=-1, keepdims=True), axis=0, keepdims=True)

                @pl.when(i == 0)
                def _():
                    loss_ref[...] = part

                @pl.when(i > 0)
                def _():
                    loss_ref[...] += part

    row = pl.BlockSpec((tm, d), lambda i, j: (i, 0))
    vec = pl.BlockSpec((1, d), lambda i, j: (0, 0))
    tile = pl.BlockSpec((tm, tn), lambda i, j: (i, j))
    in_specs = [row, vec] + [pl.BlockSpec((None, tn, d), functools.partial(lambda i, j, k: (k, j, 0), k=k)) for k in range(3)]
    args = [x, gain, w3, w3, w3]
    if target is None:
        out_specs, out_shape = [row], [SDS((s, d), F32)]
        if next_gain is not None:
            in_specs.append(vec)
            args.append(next_gain)
            out_specs.append(row)
            out_shape.append(SDS((s, d), BF16))
    else:
        in_specs.append(row)
        args.append(target)
        out_specs = [row, row, pl.BlockSpec((1, 1), lambda i, j: (0, 0))]
        out_shape = [SDS((s, d), F32), SDS((s, d), BF16), SDS((1, 1), F32)]
    if start is None:
        out_specs.append(row)
        out_shape.append(SDS((s, d), BF16))
    else:
        in_specs += [row, row]
        args += list(start)
    *head, g, u = pl.pallas_call(
        body, name=f"{tag}_fwd", grid=(s // tm, nj), in_specs=in_specs,
        out_specs=out_specs + [tile, tile],
        out_shape=out_shape + [SDS((s, f), BF16), SDS((s, f), BF16)],
        scratch_shapes=[pltpu.VMEM((tm, d), F32)],
        compiler_params=_params("arbitrary", "arbitrary"),
    )(*args)
    if start is None:
        *head, h = head
    else:
        h = start[0]
    return head, (h, g, u)


def _ffn_bwd_part(dyb, w3, saved, first, count, dh_init, *, name):
    h, g, u = saved
    s, d = h.shape
    tn = FFN_TN

    def body(*refs):
        if dh_init is None:
            dy_ref, h_ref, wd_ref, wg_ref, wu_ref, g_ref, u_ref, dh_ref, dw3_ref, dg_s, du_s, a_s = refs
        else:
            dy_ref, h_ref, wd_ref, wg_ref, wu_ref, g_ref, u_ref, dh0_ref, dh_ref, dw3_ref, dg_s, du_s, a_s = refs
        j = pl.program_id(0)

        @pl.when(j == 0)
        def _():
            dh_ref[...] = jnp.zeros_like(dh_ref) if dh_init is None else dh0_ref[...]
            for ref in (dg_s, du_s, a_s):
                ref[...] = jnp.zeros_like(ref)

        now, before = j % 2, 1 - j % 2
        dyv = dy_ref[...]
        hv = h_ref[...]
        dg, du, a = dg_s[before], du_s[before], a_s[before]
        dh_ref[...] += _mm_raw(dg, wg_ref[...]) + _mm_raw(du, wu_ref[...])
        dw3_ref[0] = _mm_raw(dg, hv, True, False).astype(BF16)
        dw3_ref[1] = _mm_raw(du, hv, True, False).astype(BF16)
        dw3_ref[2] = (_mm_raw(a, dyv, True, False) * 0.5).astype(BF16)

        da = _mm_raw(dyv, wd_ref[...], False, True) * 0.5
        a, vjp = jax.vjp(_silu_mul, g_ref[...].astype(F32), u_ref[...].astype(F32))
        dg, du = vjp(da)
        dg_s[now] = dg.astype(BF16)
        du_s[now] = du.astype(BF16)
        a_s[now] = a.astype(BF16)

    this = lambda j: first + jnp.minimum(j, count - 1)
    last = lambda j: first + jnp.maximum(j - 1, 0)
    full = pl.BlockSpec((s, d), lambda j: (0, 0))
    once = pl.BlockSpec((s, d), lambda j: (0, 0), pipeline_mode=pl.Buffered(1))
    tile = pl.BlockSpec((s, tn), lambda j: (0, this(j)))
    in_specs = [once, once, pl.BlockSpec((None, tn, d), lambda j: (2, this(j), 0)),
                pl.BlockSpec((None, tn, d), lambda j: (0, last(j), 0)), pl.BlockSpec((None, tn, d), lambda j: (1, last(j), 0)),
                tile, tile]
    args = [dyb, h, w3, w3, w3, g, u]
    if dh_init is not None:
        in_specs.append(once)
        args.append(dh_init)
    return pl.pallas_call(
        body, name=name, grid=(count + 1,), in_specs=in_specs,
        out_specs=[full, pl.BlockSpec((3, tn, d), lambda j: (0, jnp.maximum(j - 1, 0), 0))],
        out_shape=[SDS((s, d), F32), SDS((3, count * tn, d), BF16)],
        scratch_shapes=[pltpu.VMEM((2, s, tn), BF16)] * 3,
        compiler_params=_params("arbitrary"),
    )(*args)


def _bucket_table():
    qi = np.arange(BLOCK)[:, None]
    kj = np.arange(2 * BLOCK)[None, :]
    dist = np.maximum(qi + BLOCK - kj, 0)
    max_exact = N_BUCKETS // 2
    d = np.maximum(dist, 1).astype(np.float32)
    large = max_exact + (np.log(d / np.float32(max_exact)) / np.float32(math.log(MAX_DISTANCE / max_exact))
                         * np.float32(N_BUCKETS - max_exact)).astype(np.int32)
    large = np.minimum(large, N_BUCKETS - 1)
    band = np.where(dist < max_exact, dist, large).astype(np.int32)
    return np.where(np.tril(np.ones((BLOCK, BLOCK), bool)), band[:, BLOCK:], band[:, :BLOCK])


SWA_STACK = SWA_GROUP * BLOCK


def _swa_masks(n):
    qi = lax.broadcasted_iota(jnp.int32, (SWA_STACK, BLOCK), 0) % BLOCK
    kj = lax.broadcasted_iota(jnp.int32, (SWA_STACK, BLOCK), 1)
    own = kj <= qi
    return own, own | (n > 0)


def _swa_group(q, kp, kc, vp, vc, qg, kg, sink, bias, own, valid):
    qn = _rms(q, qg)
    s = jnp.where(own, _mm(qn, _rms(kc, kg), False, True), _mm(qn, _rms(kp, kg), False, True))
    s = s * (HEAD_DIM ** -0.5) + bias
    s = jnp.where(valid, s, -jnp.inf)
    m = lax.stop_gradient(jnp.maximum(jnp.max(s, axis=-1, keepdims=True), sink))
    p = jnp.exp(s - m)
    p = p / (jnp.sum(p, axis=-1, keepdims=True) + jnp.exp(sink - m))
    return _mm(jnp.where(own, p, 0.0), vc) + _mm(jnp.where(own, 0.0, p), vp)


def _swa_bias_table(rb_ref, bucket, bias_s):
    for h in range(SWA_HEADS):
        acc = jnp.zeros((BLOCK, BLOCK), F32)
        for b in range(N_BUCKETS):
            acc = jnp.where(bucket == b, rb_ref[b, h], acc)
        bias_s[h // SWA_GROUP, (h % SWA_GROUP) * BLOCK:(h % SWA_GROUP + 1) * BLOCK, :] = acc


def _swa_stack(ref, g):
    return jnp.concatenate([ref[:, (g * SWA_GROUP + hh) * HEAD_DIM:(g * SWA_GROUP + hh + 1) * HEAD_DIM]
                            for hh in range(SWA_GROUP)], axis=0)


def _swa_unstack(ref, g, stacked):
    for hh in range(SWA_GROUP):
        h = g * SWA_GROUP + hh
        ref[:, h * HEAD_DIM:(h + 1) * HEAD_DIM] = stacked[hh * BLOCK:(hh + 1) * BLOCK]


def _swa_sink_column(sink_ref, g):
    head = lax.broadcasted_iota(jnp.int32, (SWA_STACK, 1), 0) // BLOCK
    col = jnp.zeros((SWA_STACK, 1), F32)
    for hh in range(SWA_GROUP):
        col = jnp.where(head == hh, sink_ref[g * SWA_GROUP + hh], col)
    return col


def _swa_band(kvp_ref, kvc_ref, g):
    k = slice(g * HEAD_DIM, (g + 1) * HEAD_DIM)
    v = slice(SWA_KV_W + g * HEAD_DIM, SWA_KV_W + (g + 1) * HEAD_DIM)
    return kvp_ref[:, k], kvc_ref[:, k], kvp_ref[:, v], kvc_ref[:, v]


def _swa_specs(order):
    kvc = COL_SKV // (2 * SWA_KV_W)
    return [
        pl.BlockSpec((BLOCK, SWA_Q_W), lambda t: (order(t), 0)),
        pl.BlockSpec((BLOCK, 2 * SWA_KV_W), lambda t: (jnp.maximum(order(t) - 1, 0), kvc)),
        pl.BlockSpec((BLOCK, 2 * SWA_KV_W), lambda t: (order(t), kvc)),
        pl.BlockSpec((1, HEAD_DIM), lambda t: (0, 0)),
        pl.BlockSpec((1, HEAD_DIM), lambda t: (0, 0)),
        pl.BlockSpec(memory_space=pltpu.SMEM),
        pl.BlockSpec(memory_space=pltpu.SMEM),
        pl.BlockSpec((BLOCK, BLOCK), lambda t: (0, 0)),
    ]


def _swa_fwd(p, qg, kg, sinks, rel_bias, *, name):
    s = p.shape[0]
    nb = s // BLOCK

    def body(q_ref, kvp_ref, kvc_ref, qg_ref, kg_ref, sink_ref, rb_ref, bucket_ref, y_ref, bias_s):
        n = pl.program_id(0)

        @pl.when(n == 0)
        def _():
            _swa_bias_table(rb_ref, bucket_ref[...], bias_s)

        own, valid = _swa_masks(n)
        for g in range(SWA_KV_HEADS):
            out = _swa_group(_swa_stack(q_ref, g), *_swa_band(kvp_ref, kvc_ref, g), qg_ref[...], kg_ref[...],
                             _swa_sink_column(sink_ref, g), bias_s[g], own, valid)
            _swa_unstack(y_ref, g, out)

    return pl.pallas_call(
        body, name=name, grid=(nb,), in_specs=_swa_specs(lambda t: t),
        out_specs=pl.BlockSpec((BLOCK, SWA_Q_W), lambda t: (t, 0)), out_shape=SDS((s, SWA_Q_W), F32),
        scratch_shapes=[pltpu.VMEM((SWA_KV_HEADS, SWA_STACK, BLOCK), F32)],
        compiler_params=_params("arbitrary"),
    )(p, p, p, qg, kg, sinks, rel_bias, jnp.asarray(_bucket_table()))


def _swa_bwd(p, qg, kg, sinks, rel_bias, dy_all, *, name):
    s = p.shape[0]
    nb = s // BLOCK

    def body(q_ref, kvp_ref, kvc_ref, qg_ref, kg_ref, sink_ref, rb_ref, bucket_ref, dy_ref,
             dq_ref, dkv_ref, dqg_ref, dkg_ref, dsink_ref, drb_ref, bias_s, dbias_s, carry_s):
        t = pl.program_id(0)
        n = nb - 1 - t

        @pl.when(t == 0)
        def _():
            _swa_bias_table(rb_ref, bucket_ref[...], bias_s)
            dbias_s[...] = jnp.zeros_like(dbias_s)
            carry_s[...] = jnp.zeros_like(carry_s)
            dqg_ref[...] = jnp.zeros_like(dqg_ref)
            dkg_ref[...] = jnp.zeros_like(dkg_ref)
            dsink_ref[...] = jnp.zeros_like(dsink_ref)
            drb_ref[...] = jnp.zeros_like(drb_ref)

        own, valid = _swa_masks(n)
        lane = lax.broadcasted_iota(jnp.int32, (1, BLOCK), 1)
        dqg = jnp.zeros((1, HEAD_DIM), F32)
        dkg = jnp.zeros((1, HEAD_DIM), F32)
        dsink_vec = jnp.zeros((1, BLOCK), F32)
        for g in range(SWA_KV_HEADS):
            _, vjp = jax.vjp(functools.partial(_swa_group, own=own, valid=valid), _swa_stack(q_ref, g),
                             *_swa_band(kvp_ref, kvc_ref, g), qg_ref[...], kg_ref[...], _swa_sink_column(sink_ref, g),
                             bias_s[g])
            dq, dkp, dkc, dvp, dvc, dqg_g, dkg_g, dsink_col, dbias = vjp(_swa_stack(dy_ref, g))
            _swa_unstack(dq_ref, g, dq)
            dqg += dqg_g
            dkg += dkg_g
            dbias_s[g] += dbias
            for hh in range(SWA_GROUP):
                dsink_h = jnp.sum(dsink_col[hh * BLOCK:(hh + 1) * BLOCK], axis=0, keepdims=True)
                dsink_vec += jnp.where(lane == g * SWA_GROUP + hh, dsink_h, 0.0)
            lo = g * HEAD_DIM
            dkv_ref[:, lo:lo + HEAD_DIM] = dkc + carry_s[g]
            carry_s[g] = dkp
            lo += SWA_KV_W
            dkv_ref[:, lo:lo + HEAD_DIM] = dvc + carry_s[SWA_KV_HEADS + g]
            carry_s[SWA_KV_HEADS + g] = dvp
        dqg_ref[...] += dqg
        dkg_ref[...] += dkg
        dsink_ref[...] += dsink_vec

        @pl.when(t == nb - 1)
        def _():
            bucket = bucket_ref[...]
            row = lax.broadcasted_iota(jnp.int32, (N_BUCKETS, BLOCK), 0)
            col = lax.broadcasted_iota(jnp.int32, (N_BUCKETS, BLOCK), 1)
            acc = jnp.zeros((N_BUCKETS, BLOCK), F32)
            for h in range(SWA_HEADS):
                dbias = dbias_s[h // SWA_GROUP, (h % SWA_GROUP) * BLOCK:(h % SWA_GROUP + 1) * BLOCK, :]
                for b in range(N_BUCKETS):
                    part = jnp.sum(jnp.where(bucket == b, dbias, 0.0), axis=1, keepdims=True)
                    val = jnp.sum(part, axis=0, keepdims=True)
                    acc = acc + jnp.where((row == b) & (col == h), val, 0.0)
            drb_ref[...] = acc

    order = lambda t: nb - 1 - t
    vec = pl.BlockSpec((1, HEAD_DIM), lambda t: (0, 0))
    return pl.pallas_call(
        body, name=name, grid=(nb,),
        in_specs=_swa_specs(order) + [pl.BlockSpec((BLOCK, SWA_Q_W), lambda t: (order(t), 0))],
        out_specs=[pl.BlockSpec((BLOCK, SWA_Q_W), lambda t: (order(t), 0)),
                   pl.BlockSpec((BLOCK, 2 * SWA_KV_W), lambda t: (order(t), 0)),
                   vec, vec, pl.BlockSpec((1, BLOCK), lambda t: (0, 0)),
                   pl.BlockSpec((N_BUCKETS, BLOCK), lambda t: (0, 0))],
        out_shape=[SDS((s, SWA_Q_W), F32), SDS((s, 2 * SWA_KV_W), F32), SDS((1, HEAD_DIM), F32),
                   SDS((1, HEAD_DIM), F32), SDS((1, BLOCK), F32), SDS((N_BUCKETS, BLOCK), F32)],
        scratch_shapes=[pltpu.VMEM((SWA_KV_HEADS, SWA_STACK, BLOCK), F32),
                        pltpu.VMEM((SWA_KV_HEADS, SWA_STACK, BLOCK), F32),
                        pltpu.VMEM((2 * SWA_KV_HEADS, BLOCK, HEAD_DIM), F32)],
        compiler_params=_params("arbitrary"),
    )(p, p, p, qg, kg, sinks, rel_bias, jnp.asarray(_bucket_table()), dy_all)


def _mem_head(q, k, v, qg, kg):
    qn = _rms(q, qg)
    kn = _rms(k, kg)
    s = _mm(qn, kn, False, True) * (HEAD_DIM ** -0.5)
    m = lax.stop_gradient(jnp.max(s, axis=-1, keepdims=True))
    e = jnp.exp(s - m)
    return _mm(e / jnp.sum(e, axis=-1, keepdims=True), v)


def _mem_fwd(p, kv, qg, kg, *, tq, name):
    s = p.shape[0]
    m = kv.shape[0]

    def body(q_ref, kv_ref, qg_ref, kg_ref, y_ref):
        for h in range(MEM_HEADS):
            cols = slice(h * HEAD_DIM, (h + 1) * HEAD_DIM)
            vcols = slice(MEM_Q_W + h * HEAD_DIM, MEM_Q_W + (h + 1) * HEAD_DIM)
            y_ref[:, cols] = _mem_head(q_ref[:, cols], kv_ref[:, cols], kv_ref[:, vcols], qg_ref[...], kg_ref[...])

    vec = pl.BlockSpec((1, HEAD_DIM), lambda t: (0, 0))
    return pl.pallas_call(
        body, name=name, grid=(s // tq,),
        in_specs=[pl.BlockSpec((tq, MEM_Q_W), lambda t: (t, COL_MQ // MEM_Q_W)),
                  pl.BlockSpec((m, 2 * MEM_Q_W), lambda t: (0, 0)), vec, vec],
        out_specs=pl.BlockSpec((tq, MEM_Q_W), lambda t: (t, 0)), out_shape=SDS((s, MEM_Q_W), F32),
        compiler_params=_params("parallel"),
    )(p, kv, qg, kg)


def _mem_bwd(p, kv, qg, kg, dy_all, *, tq, name):
    s = p.shape[0]
    m = kv.shape[0]

    def body(q_ref, kv_ref, qg_ref, kg_ref, dy_ref, dq_ref, dkv_ref, dqg_ref, dkg_ref):
        @pl.when(pl.program_id(0) == 0)
        def _():
            dkv_ref[...] = jnp.zeros_like(dkv_ref)
            dqg_ref[...] = jnp.zeros_like(dqg_ref)
            dkg_ref[...] = jnp.zeros_like(dkg_ref)

        dqg = jnp.zeros((1, HEAD_DIM), F32)
        dkg = jnp.zeros((1, HEAD_DIM), F32)
        for h in range(MEM_HEADS):
            cols = slice(h * HEAD_DIM, (h + 1) * HEAD_DIM)
            vcols = slice(MEM_Q_W + h * HEAD_DIM, MEM_Q_W + (h + 1) * HEAD_DIM)
            _, vjp = jax.vjp(_mem_head, q_ref[:, cols], kv_ref[:, cols], kv_ref[:, vcols], qg_ref[...], kg_ref[...])
            dq, dk, dv, dqg_h, dkg_h = vjp(dy_ref[:, cols])
            dq_ref[:, cols] = dq
            dkv_ref[:, cols] += dk
            dkv_ref[:, vcols] += dv
            dqg += dqg_h
            dkg += dkg_h
        dqg_ref[...] += dqg
        dkg_ref[...] += dkg

    vec = pl.BlockSpec((1, HEAD_DIM), lambda t: (0, 0))
    full = pl.BlockSpec((m, 2 * MEM_Q_W), lambda t: (0, 0))
    dy_col = (SWA_Q_W + GLA_V_W) // MEM_Q_W
    return pl.pallas_call(
        body, name=name, grid=(s // tq,),
        in_specs=[pl.BlockSpec((tq, MEM_Q_W), lambda t: (t, COL_MQ // MEM_Q_W)), full, vec, vec,
                  pl.BlockSpec((tq, MEM_Q_W), lambda t: (t, dy_col))],
        out_specs=[pl.BlockSpec((tq, MEM_Q_W), lambda t: (t, 0)), full, vec, vec],
        out_shape=[SDS((s, MEM_Q_W), F32), SDS((m, 2 * MEM_Q_W), F32), SDS((1, HEAD_DIM), F32), SDS((1, HEAD_DIM), F32)],
        compiler_params=_params("arbitrary"),
    )(p, kv, qg, kg, dy_all)


GLA_ROWS = 256


GLA_GROUP = 4


def _gla_consts():
    c, h, r = GLA_CHUNK, GLA_HEADS, GLA_GROUP * GLA_CHUNK
    i2 = lax.broadcasted_iota(jnp.int32, (c, c), 0)
    j2 = lax.broadcasted_iota(jnp.int32, (c, c), 1)
    slab_q = lax.broadcasted_iota(jnp.int32, (h, r, GLA_QK_W), 0)
    lane_q = lax.broadcasted_iota(jnp.int32, (h, r, GLA_QK_W), 2)
    row_a = lax.broadcasted_iota(jnp.int32, (h * r, r), 0) % r
    col_a = lax.broadcasted_iota(jnp.int32, (h * r, r), 1)
    slab_o = lax.broadcasted_iota(jnp.int32, (h, r, GLA_V_W), 0)
    lane_o = lax.broadcasted_iota(jnp.int32, (h, r, GLA_V_W), 2)
    row_s = lax.broadcasted_iota(jnp.int32, (GLA_V_W, GLA_QK_W), 0)
    col_s = lax.broadcasted_iota(jnp.int32, (GLA_V_W, GLA_QK_W), 1)
    return dict(
        ltri=(j2 <= i2).astype(F32),
        m_q=(slab_q == lane_q // GLA_DK).astype(F32),
        causal=(col_a <= row_a) & (col_a // c == row_a // c),
        m_o=(slab_o == lane_o // GLA_DV).astype(F32),
        m_s=(row_s // GLA_DV == col_s // GLA_DK).astype(F32),
    )


def _gla_step(q, k, v, z, bg, st, c):
    h = GLA_HEADS
    kt, qt, qe, decay = [], [], [], []
    for qc, kc, zc in zip(q, k, z):
        la = _log_sigmoid(zc + bg) * (1.0 / GLA_TAU)
        b = _mmf(c["ltri"], la)
        bl = jnp.sum(la, axis=0, keepdims=True)
        qs = qc * (GLA_DK ** -0.5)
        kt.append(kc * jnp.exp(bl - b))
        qt.append(qs * jnp.exp(b - bl))
        qe.append(qs * jnp.exp(b))
        decay.append(jnp.exp(bl))
    o_intra = []
    rows = GLA_GROUP * GLA_CHUNK
    for lo in range(0, len(q), GLA_GROUP):
        qt_all, kt_all, v_all = (jnp.concatenate(parts[lo:lo + GLA_GROUP], axis=0) for parts in (qt, kt, v))
        q_stack = (jnp.broadcast_to(qt_all[None], (h, rows, GLA_QK_W)) * c["m_q"]).reshape(h * rows, GLA_QK_W)
        a = jnp.where(c["causal"], _mm3(q_stack, kt_all, False, True), 0.0)
        o_stack = _mm(a, v_all)
        o_intra.append(jnp.sum(o_stack.reshape(h, rows, GLA_V_W) * c["m_o"], axis=0))
    o_intra = jnp.concatenate(o_intra, axis=0)
    o_inter = []
    for qec, ktc, vc, dc in zip(qe, kt, v, decay):
        o_inter.append(_mm(qec, st, False, True))
        st = st * dc + _mm(vc, ktc, True, False) * c["m_s"]
    return o_intra + jnp.concatenate(o_inter, axis=0), st


def _gla_post(o, gg, gain, g64):
    ms = _mmf(o * o, g64) * (1.0 / GLA_DV)
    return o * lax.rsqrt(ms + EPS) * gain * jax.nn.silu(gg)


def _gla_g64():
    r = lax.broadcasted_iota(jnp.int32, (GLA_V_W, GLA_V_W), 0)
    c = lax.broadcasted_iota(jnp.int32, (GLA_V_W, GLA_V_W), 1)
    return (r // GLA_DV == c // GLA_DV).astype(F32)


def _gla_in_specs(order):
    r = GLA_ROWS
    return [
        pl.BlockSpec((r, GLA_QK_W), lambda t: (order(t), COL_GQ // GLA_QK_W)),
        pl.BlockSpec((r, GLA_QK_W), lambda t: (order(t), COL_GK // GLA_QK_W)),
        pl.BlockSpec((r, GLA_V_W), lambda t: (order(t), COL_GV // GLA_V_W)),
        pl.BlockSpec((r, GLA_V_W), lambda t: (order(t), COL_GG // GLA_V_W)),
        pl.BlockSpec((r, GLA_QK_W), lambda t: (order(t), 0)),
        pl.BlockSpec((1, GLA_QK_W), lambda t: (0, 0)),
        pl.BlockSpec((1, GLA_V_W), lambda t: (0, 0)),
    ]


def _gla_pieces(q_ref, k_ref, v_ref, z_ref, cps):
    chunk = lambda ref: [ref[ci * GLA_CHUNK:(ci + 1) * GLA_CHUNK, :] for ci in range(cps)]
    return chunk(q_ref), chunk(k_ref), chunk(v_ref), chunk(z_ref)


def _gla_fwd(p, z, bg, gain, *, name):
    s = p.shape[0]
    r = GLA_ROWS
    cps = r // GLA_CHUNK

    def body(q_ref, k_ref, v_ref, gg_ref, z_ref, bg_ref, gain_ref, y_ref, oraw_ref, stsave_ref, st_s):
        @pl.when(pl.program_id(0) == 0)
        def _():
            st_s[...] = jnp.zeros_like(st_s)

        st = st_s[...]
        stsave_ref[0] = st
        o, st = _gla_step(*_gla_pieces(q_ref, k_ref, v_ref, z_ref, cps), bg_ref[...], st, _gla_consts())
        oraw_ref[...] = o
        st_s[...] = st
        y_ref[...] = _gla_post(o, gg_ref[...], gain_ref[...], _gla_g64())

    rowv = pl.BlockSpec((r, GLA_V_W), lambda t: (t, 0))
    return pl.pallas_call(
        body, name=name, grid=(s // r,), in_specs=_gla_in_specs(lambda t: t),
        out_specs=[rowv, rowv, pl.BlockSpec((1, GLA_V_W, GLA_QK_W), lambda t: (t, 0, 0))],
        out_shape=[SDS((s, GLA_V_W), F32), SDS((s, GLA_V_W), F32), SDS((s // r, GLA_V_W, GLA_QK_W), F32)],
        scratch_shapes=[pltpu.VMEM((GLA_V_W, GLA_QK_W), F32)],
        compiler_params=_params("arbitrary"),
    )(p, p, p, p, z, bg, gain)


def _gla_bwd(p, z, bg, gain, oraw, stsave, dy_all, *, name):
    s = p.shape[0]
    r = GLA_ROWS
    cps = r // GLA_CHUNK
    nsteps = s // r
    w_qkvg = 2 * GLA_QK_W + 2 * GLA_V_W

    def body(q_ref, k_ref, v_ref, gg_ref, z_ref, bg_ref, gain_ref, oraw_ref, stsave_ref, dy_ref,
             dqkvg_ref, dz_ref, dbg_ref, dgain_ref, dst_s):
        @pl.when(pl.program_id(0) == 0)
        def _():
            dst_s[...] = jnp.zeros_like(dst_s)
            dbg_ref[...] = jnp.zeros_like(dbg_ref)
            dgain_ref[...] = jnp.zeros_like(dgain_ref)

        _, vjp = jax.vjp(functools.partial(_gla_post, g64=_gla_g64()), oraw_ref[...], gg_ref[...], gain_ref[...])
        do, dgg, dgain = vjp(dy_ref[...])
        dqkvg_ref[:, 2 * GLA_QK_W + GLA_V_W:] = dgg
        dgain_ref[...] += dgain
        _, vjp = jax.vjp(functools.partial(_gla_step, c=_gla_consts()), *_gla_pieces(q_ref, k_ref, v_ref, z_ref, cps),
                         bg_ref[...], stsave_ref[0])
        dq, dk, dv, dz, dbg, dst = vjp((do, dst_s[...]))
        for ci in range(cps):
            rows = slice(ci * GLA_CHUNK, (ci + 1) * GLA_CHUNK)
            dqkvg_ref[rows, 0:GLA_QK_W] = dq[ci]
            dqkvg_ref[rows, GLA_QK_W:2 * GLA_QK_W] = dk[ci]
            dqkvg_ref[rows, 2 * GLA_QK_W:2 * GLA_QK_W + GLA_V_W] = dv[ci]
            dz_ref[rows, :] = dz[ci]
        dst_s[...] = dst
        dbg_ref[...] += dbg

    order = lambda t: nsteps - 1 - t
    rowv = pl.BlockSpec((r, GLA_V_W), lambda t: (order(t), 0))
    return pl.pallas_call(
        body, name=name, grid=(nsteps,),
        in_specs=_gla_in_specs(order) + [
            rowv, pl.BlockSpec((1, GLA_V_W, GLA_QK_W), lambda t: (order(t), 0, 0)),
            pl.BlockSpec((r, GLA_V_W), lambda t: (order(t), SWA_Q_W // GLA_V_W))],
        out_specs=[pl.BlockSpec((r, w_qkvg), lambda t: (order(t), 0)), pl.BlockSpec((r, GLA_QK_W), lambda t: (order(t), 0)),
                   pl.BlockSpec((1, GLA_QK_W), lambda t: (0, 0)), pl.BlockSpec((1, GLA_V_W), lambda t: (0, 0))],
        out_shape=[SDS((s, w_qkvg), F32), SDS((s, GLA_QK_W), F32), SDS((1, GLA_QK_W), F32), SDS((1, GLA_V_W), F32)],
        scratch_shapes=[pltpu.VMEM((GLA_V_W, GLA_QK_W), F32)],
        compiler_params=_params("arbitrary"),
    )(p, p, p, p, z, bg, gain, oraw, stsave, dy_all)


def _local_step(x, mem, target, small, big, on_grads):
    g1, gmix, gmem, g2, sqg, skg, sinks, rel_bias, wgu, bg, gla_gain, mqg, mkg = small
    (w3_1a, w3_1b), gather_mix, gather_ffn2 = big
    wgu_pad = jnp.zeros((GLA_QK_W, GLA_QK_W), BF16).at[:GLA_RANK].set(wgu.astype(BF16))
    gain256 = jnp.tile(gla_gain, (1, GLA_HEADS))

    (part,), saved1a = _ffn_fwd(x, g1, w3_1a, "ffn1a", partial=True)
    win_p, wkv, wout = gather_mix(part)
    (x1, h), saved1b = _ffn_fwd(x, g1, w3_1b, "ffn1b", next_gain=gmix, start=(saved1a[0], part))
    w3_2 = gather_ffn2((wout, x1))
    p = _matmul(h, win_p, tm=512, tn=IN_W_PAD, name="mix_in")
    hm = _rms_fwd(mem, gmem, tm=256, name="mem_rms")
    kv = _matmul(hm, wkv, tm=256, tn=512, name="mem_kv")
    p_glr = p[:, COL_GLR:]
    z = _matmul(p_glr, wgu_pad, tm=1024, tn=GLA_QK_W, name="gla_gate")
    y_swa = _swa_fwd(p, sqg, skg, sinks, rel_bias, name="swa_fwd")
    y_gla, oraw, stsave = _gla_fwd(p, z, bg, gain256, name="gla_fwd")
    y_mem = _mem_fwd(p, kv, mqg, mkg, tq=512, name="mem_fwd")
    x2 = _matmul([y_swa, y_gla, y_mem], wout, b_blocks=[0, 2, 3], tm=512, tn=1024, res=x1, name="mix_out")
    (dy, dyb, loss), saved2 = _ffn_fwd(x2, g2, w3_2, "ffn2", target=target)

    dh2, dw3_2 = _ffn_bwd_part(dyb, w3_2, saved2, 0, w3_2.shape[1] // FFN_TN, None, name="ffn2_bwd")
    dx2, dx2b, dg2 = _rms_bwd(x2, g2, dh2, dy, tm=512, name="ffn2_drms")
    dx2b = on_grads("ffn2", [dw3_2], dx2b)
    dy_all = _matmul(dx2b, wout, tb=True, tm=512, tn=1024, name="mix_dy")
    dwout = _matmul(jnp.concatenate([y_swa, y_gla, y_mem], axis=1), dx2b, ta=True, tm=512, tn=1024, out_dtype=BF16,
                    name="mix_dw_out")
    dq_swa, dkv_swa, dsqg, dskg, dsink, drb = _swa_bwd(p, sqg, skg, sinks, rel_bias, dy_all, name="swa_bwd")
    dqkvg, dz, dbg, dgain256 = _gla_bwd(p, z, bg, gain256, oraw, stsave, dy_all, name="gla_bwd")
    dmq, dkv_mem, dmqg, dmkg = _mem_bwd(p, kv, mqg, mkg, dy_all, tq=512, name="mem_bwd")
    dglr = _matmul(dz, wgu_pad, tb=True, tm=1024, tn=GLA_QK_W, name="gla_gate_dx")
    dwgu_pad = _matmul(p_glr, dz, ta=True, tm=GLA_QK_W, tn=GLA_QK_W, name="gla_gate_dw")
    dp = jnp.concatenate([dq_swa, dkv_swa, dqkvg, dmq, dglr], axis=1)
    dwin_p = _matmul(h, dp, ta=True, tm=1024, tn=640, out_dtype=BF16, name="mix_dw_in")
    dx1, dx1b, dgmix = _rms_bwd(x1, gmix, (dp, win_p), dx2, tm=512, name="mix_dh_drms")
    dwkv = _matmul(hm, dkv_mem, ta=True, tm=512, tn=512, out_dtype=BF16, name="mem_dw_kv")
    dx1b = on_grads("mix", (dwin_p, dwkv, dwout), dx1b)
    _, _, dgmem = _rms_bwd(mem, gmem, (dkv_mem, wkv), None, tm=256, name="mem_dh_drms")
    dh1, dw3_1a = _ffn_bwd_part(dx1b, w3_1a, saved1a, 0, w3_1a.shape[1] // FFN_TN, None, name="ffn1_bwd_a")
    dgla_gain = dgain256.reshape(GLA_HEADS, GLA_DV).sum(axis=0, keepdims=True)
    dsmall = [dgmix, dgmem, dg2, dsqg, dskg, dsink[:, :SWA_HEADS], drb[:, :SWA_HEADS], dwgu_pad[:GLA_RANK], dbg,
              dgla_gain, dmqg, dmkg, loss]
    dh1, dgmem, dwgu_pad = on_grads("ffn1a", [dw3_1a], (dh1, dgmem, dwgu_pad), small=dsmall)
    dh1, dw3_1b = _ffn_bwd_part(dx1b, w3_1b, saved1b, 0, w3_1b.shape[1] // FFN_TN, dh1, name="ffn1_bwd_b")
    dx, _, dg1 = _rms_bwd(x, g1, dh1, dx1, tm=512, name="ffn1_drms")
    on_grads("ffn1b", [dw3_1b], None, small=[dg1])
    return dx


def _mesh_place():
    x, y, c = lax.axis_index("x"), lax.axis_index("y"), lax.axis_index("c")
    other_chips = [(1 - x, y), (x, 1 - y), (1 - x, 1 - y)]
    return x, y, c, other_chips


def _handshake(peers):
    barrier = pltpu.get_barrier_semaphore()
    for peer in peers:
        pl.semaphore_signal(barrier, inc=1, device_id=peer, device_id_type=MESH)
    pl.semaphore_wait(barrier, len(peers))


def _sequencer_call(body, operands, out_shapes, sems, *, name, collective_id):
    return pl.kernel(
        body, name=name, out_type=out_shapes, mesh=plsc.ScalarSubcoreMesh(axis_name="sequencer", num_cores=1),
        scratch_types=sems, compiler_params=pltpu.CompilerParams(collective_id=collective_id),
    )(*operands)


def _window(ref, kind, slot, shape):
    if kind == "row":
        rows = pl.ds(pl.multiple_of(slot * shape[-2], 8), shape[-2])
        return ref.at[(slice(None),) * (len(shape) - 2) + (rows,)]
    return ref.at[slot]


def _gathered(shape, kind):
    if kind == "row":
        return tuple(shape[:-2]) + (N_DEV * shape[-2], shape[-1])
    return (N_DEV,) + tuple(shape)


def _half(view, hf):
    if len(view.shape) == 4:
        return view.at[:, hf]
    n = view.shape[-2] // 2
    return view.at[(slice(None),) * (len(view.shape) - 2) + (pl.ds(hf * n, n),)]


def _all_gather(shards, kinds, *, name, collective_id):
    nt = len(shards)

    def body(*refs):
        x_refs, o_refs = refs[:nt], refs[nt:2 * nt]
        send_sems, recv_sems, local_sems = refs[2 * nt:]
        x, y, c, _ = _mesh_place()
        me, sibling, xn, yn, diag = (x, y, c), (x, y, 1 - c), (1 - x, y, c), (x, 1 - y, c), (1 - x, 1 - y, c)
        _handshake([sibling, xn, yn])

        def win(t, block):
            bx, by, bc = block
            return _window(o_refs[t], kinds[t], 4 * bx + 2 * by + bc, shards[t].shape)

        def copy(k, t, src, dst, to):
            return pltpu.make_async_remote_copy(src_ref=src, dst_ref=dst, send_sem=send_sems.at[k, t],
                                                recv_sem=recv_sems.at[k, t], device_id=to, device_id_type=MESH)

        def piece(k, t, block, hf, to, from_shard=False):
            dst = _half(win(t, block), hf)
            return copy(k, t, _half(x_refs[t], hf) if from_shard else dst, dst, to)

        mine = [pltpu.make_async_copy(x_refs[t], win(t, me), local_sems.at[t]) for t in range(nt)]
        sent = []

        def start(cp):
            cp.start()
            sent.append(cp)

        for cp in mine:
            cp.start()
        for t in range(nt):
            start(copy(0, t, x_refs[t], win(t, me), sibling))
        for hf_x, hf_y in ((0, 1), (1, 0)):
            for t in range(nt):
                start(piece(1 + hf_x, t, me, hf_x, xn, True))
                start(piece(3 + hf_y, t, me, hf_y, yn, True))
        for k, block, hf, onward, k_sib in ((1, xn, 0, (5, yn), 7), (4, yn, 1, (6, xn), 10), (2, xn, 1, None, 8),
                                           (3, yn, 0, None, 9), (5, diag, 0, None, 11), (6, diag, 1, None, 12)):
            for t in range(nt):
                piece(k, t, block, hf, me).wait_recv()
                if onward is not None:
                    start(piece(onward[0], t, block, hf, onward[1]))
                start(piece(k_sib, t, block, hf, sibling))
        for t in range(nt):
            copy(0, t, x_refs[t], win(t, sibling), me).wait_recv()
        for k_sib, block, hf in ((7, xn, 0), (10, yn, 1), (8, xn, 1), (9, yn, 0), (11, diag, 0), (12, diag, 1)):
            for t in range(nt):
                bx, by, _ = block
                piece(k_sib, t, (bx, by, 1 - c), hf, me).wait_recv()
        for cp in sent:
            cp.wait_send()
        for cp in mine:
            cp.wait()

    return _sequencer_call(
        body, shards, [SDS(_gathered(s.shape, k), s.dtype) for s, k in zip(shards, kinds)],
        [pltpu.SemaphoreType.DMA((13, nt)), pltpu.SemaphoreType.DMA((13, nt)), pltpu.SemaphoreType.DMA((nt,))],
        name=name, collective_id=collective_id)


def _part_shape(shape, kind):
    if kind == "row":
        return tuple(shape[:-2]) + (shape[-2] // N_DEV, shape[-1])
    return tuple(shape[2:])


def _pair_exchange(grads, kinds, *, name, collective_id):
    nt = len(grads)
    part = [_part_shape(g.shape, k) for g, k in zip(grads, kinds)]

    def body(*refs):
        g_refs, o_refs = refs[:nt], refs[nt:2 * nt]
        send_sems, recv_sems = refs[2 * nt:]
        x, y, c, _ = _mesh_place()
        _handshake([(x, y, 1 - c)])
        copies = []
        for t in range(nt):
            for xy in range(4):
                src = g_refs[t].at[1 - c, xy] if kinds[t] == "stack" else _window(g_refs[t], kinds[t], 2 * xy + 1 - c, part[t])
                copies.append(pltpu.make_async_remote_copy(
                    src_ref=src, dst_ref=o_refs[t].at[xy], send_sem=send_sems.at[xy, t], recv_sem=recv_sems.at[xy, t],
                    device_id=(x, y, 1 - c), device_id_type=MESH))
        for cp in copies:
            cp.start()
        for cp in copies:
            cp.wait()

    return _sequencer_call(
        body, grads, [SDS((4,) + p, g.dtype) for p, g in zip(part, grads)],
        [pltpu.SemaphoreType.DMA((4, nt)), pltpu.SemaphoreType.DMA((4, nt))], name=name, collective_id=collective_id)


def _chip_exchange(parts, small, *, name, collective_id):
    nt = len(parts)
    if small is None:
        def body_plain(*refs):
            s_refs, o_refs = refs[:nt], refs[nt:2 * nt]
            send_sems, recv_sems = refs[2 * nt:]
            x, y, c, chips = _mesh_place()
            _handshake([(*chip, c) for chip in chips])
            copies = [pltpu.make_async_remote_copy(
                src_ref=s_refs[t].at[2 * chip[0] + chip[1]], dst_ref=o_refs[t].at[j],
                send_sem=send_sems.at[j, t], recv_sem=recv_sems.at[j, t], device_id=(*chip, c), device_id_type=MESH)
                for j, chip in enumerate(chips) for t in range(nt)]
            for cp in copies:
                cp.start()
            for cp in copies:
                cp.wait()

        return _sequencer_call(
            body_plain, parts, [SDS((3,) + s.shape[1:], s.dtype) for s in parts],
            [pltpu.SemaphoreType.DMA((3, nt)), pltpu.SemaphoreType.DMA((3, nt))], name=name, collective_id=collective_id)

    def body(*refs):
        s_refs, small_ref = refs[:nt], refs[nt]
        o_refs, small_all = refs[nt + 1:2 * nt + 1], refs[2 * nt + 1]
        send_sems, recv_sems, small_send, small_recv, local_sem = refs[2 * nt + 2:]
        x, y, c, chips = _mesh_place()
        _handshake([(px, py, pc) for px in (x, 1 - x) for py in (y, 1 - y) for pc in (c, 1 - c)][1:])

        def copy(j, t, chip):
            return pltpu.make_async_remote_copy(
                src_ref=s_refs[t].at[2 * chip[0] + chip[1]], dst_ref=o_refs[t].at[j],
                send_sem=send_sems.at[j, t], recv_sem=recv_sems.at[j, t], device_id=(*chip, c), device_id_type=MESH)

        flips = [(fx, fy, fc) for fx in (0, 1) for fy in (0, 1) for fc in (0, 1)][1:]

        def small_copy(k):
            fx, fy, fc = flips[k]
            to = (x ^ fx if fx else x, y ^ fy if fy else y, c ^ fc if fc else c)
            rows = small_all.at[4 * x + 2 * y + c]
            return pltpu.make_async_remote_copy(
                src_ref=small_ref, dst_ref=rows, send_sem=small_send.at[k], recv_sem=small_recv.at[k],
                device_id=to, device_id_type=MESH)

        own = pltpu.make_async_copy(small_ref, small_all.at[4 * x + 2 * y + c], local_sem)
        own.start()
        copies = [copy(j, t, chip) for j, chip in enumerate(chips) for t in range(nt)]
        smalls = [small_copy(k) for k in range(7)]
        for cp in smalls + copies:
            cp.start()
        for cp in smalls + copies:
            cp.wait()
        own.wait()

    return _sequencer_call(
        body, list(parts) + [small],
        [SDS((3,) + s.shape[1:], s.dtype) for s in parts] + [SDS((N_DEV,) + small.shape, small.dtype)],
        [pltpu.SemaphoreType.DMA((3, nt)), pltpu.SemaphoreType.DMA((3, nt)),
         pltpu.SemaphoreType.DMA((7,)), pltpu.SemaphoreType.DMA((7,)), pltpu.SemaphoreType.DMA],
        name=name, collective_id=collective_id)


def _pair_sum(grad, theirs, kind, c, *, name):
    if kind == "row":
        r, l = theirs.shape[-2:]
        n = theirs.size // (4 * r * l)
        grad, theirs = grad.reshape(n, N_DEV * r, l), theirs.reshape(4, n, r, l)
        mine_spec = pl.BlockSpec((n, r, l), lambda xy, c_ref: (0, 2 * xy + c_ref[0], 0))
    else:
        r, l = theirs.shape[-2:]
        n = theirs.size // (4 * r * l)
        theirs = theirs.reshape(4, n, r, l)
        grad = grad.reshape(2, 4, n, r, l)
        mine_spec = pl.BlockSpec((None, None, n, r, l), lambda xy, c_ref: (c_ref[0], xy, 0, 0, 0))

    def body(c_ref, a_ref, b_ref, o_ref):
        o_ref[...] = (a_ref[...].astype(F32) + b_ref[...].astype(F32)).astype(BF16)

    part = pl.BlockSpec((None, n, r, l), lambda xy, c_ref: (xy, 0, 0, 0))
    return pl.pallas_call(
        body, name=name,
        grid_spec=pltpu.PrefetchScalarGridSpec(num_scalar_prefetch=1, grid=(4,), in_specs=[mine_spec, part], out_specs=part),
        out_shape=SDS((4, n, r, l), BF16), compiler_params=_params("parallel"),
    )(c, grad, theirs)


def _adamw(w, g, m, v):
    m = ADAM_B1 * m + (1.0 - ADAM_B1) * g
    v = ADAM_B2 * v + (1.0 - ADAM_B2) * jnp.square(g)
    m_hat = m / (1.0 - ADAM_B1 ** ADAM_STEP)
    v_hat = v / (1.0 - ADAM_B2 ** ADAM_STEP)
    delta = -ADAM_LR * (m_hat / (jnp.sqrt(v_hat) + ADAM_EPS) + ADAM_WD * w)
    return delta, m, v


def _adam_big(owns, others, mat, xy, w, m, v, *, tr, name):
    _, r, l = w.shape
    lp = owns[0].shape[-1]
    nq = len(owns)
    counts = [o.shape[-2] // tr for o in owns]
    starts = [sum(counts[:q]) for q in range(nq)]
    assert sum(counts) * tr == r and all(o.shape[-2] % tr == 0 for o in owns), (r, tr, counts)

    def body(xy_ref, *refs):
        own_refs, oth_refs = refs[:nq], refs[nq:2 * nq]
        w_ref, m_ref, v_ref, g_out, d_out, m_out, v_out = refs[2 * nq:]
        g = None
        for q in range(nq):
            gq = own_refs[q][0, 0].astype(F32)
            for j in range(3):
                gq = gq + oth_refs[q][j, 0].astype(F32)
            g = gq if g is None else jnp.where(pl.program_id(0) >= starts[q], gq, g)
        g = g[:, :l]
        delta, m_new, v_new = _adamw(w_ref[0], g, m_ref[0], v_ref[0])
        g_out[0] = g
        d_out[0] = delta
        m_out[0] = m_new
        v_out[0] = v_new

    def at(q):
        return lambda i: jnp.clip(i - starts[q], 0, counts[q] - 1)

    blk = pl.BlockSpec((1, tr, l), lambda i, xy_ref: (0, i, 0))
    own_specs = [pl.BlockSpec((1, 1, tr, lp), functools.partial(lambda i, xy_ref, q: (xy_ref[0], mat, at(q)(i), 0), q=q))
                 for q in range(nq)]
    oth_specs = [pl.BlockSpec((3, 1, tr, lp), functools.partial(lambda i, xy_ref, q: (0, mat, at(q)(i), 0), q=q))
                 for q in range(nq)]
    return pl.pallas_call(
        body, name=name,
        grid_spec=pltpu.PrefetchScalarGridSpec(
            num_scalar_prefetch=1, grid=(r // tr,), in_specs=own_specs + oth_specs + [blk, blk, blk],
            out_specs=[blk, blk, blk, blk]),
        out_shape=[SDS(w.shape, F32)] * 4, compiler_params=_params("parallel"),
    )(xy, *owns, *others, w, m, v)


def _small_layout(shapes):
    out, at = [], 0
    for r, c in shapes:
        rows = c // 128 if (r == 1 and c > 128) else r
        out.append((at, rows))
        at += -(-rows // 8) * 8
    return out, at


def _pack_small(parts, *, name):
    shapes = [a.shape for a in parts]
    layout, total = _small_layout(shapes)

    def body(*refs):
        o_ref = refs[-1]
        o_ref[...] = jnp.zeros_like(o_ref)
        for x_ref, (r, c), (at, rows) in zip(refs, shapes, layout):
            if r == 1 and c > 128:
                for k in range(rows):
                    o_ref[at + k:at + k + 1, :] = x_ref[:, k * 128:(k + 1) * 128]
            else:
                o_ref[at:at + r, 0:c] = x_ref[...]

    return pl.pallas_call(body, name=name, out_shape=SDS((total, 128), F32))(*parts)


def _adam_small(g_all, ws, ms, vs, *, name):
    n = len(ws)
    shapes = [w.shape for w in ws]
    layout, _ = _small_layout(shapes)

    def body(g_ref, *refs):
        w_refs, m_refs, v_refs, outs = refs[:n], refs[n:2 * n], refs[2 * n:3 * n], refs[3 * n:]
        g_sum = g_ref[0]
        for k in range(1, N_DEV):
            g_sum = g_sum + g_ref[k]
        for i, ((r, c), (at, rows)) in enumerate(zip(shapes, layout)):
            if r == 1 and c > 128:
                g = jnp.concatenate([g_sum[at + k:at + k + 1, :] for k in range(rows)], axis=1)
            else:
                g = g_sum[at:at + r, 0:c]
            delta, m_new, v_new = _adamw(w_refs[i][...], g, m_refs[i][...], v_refs[i][...])
            for q, val in enumerate((g, delta, m_new, v_new)):
                outs[4 * i + q][...] = val

    flat = pl.pallas_call(body, name=name, out_shape=[SDS(s, F32) for s in shapes for _ in range(4)])(g_all, *ws, *ms, *vs)
    return [flat[4 * i:4 * i + 4] for i in range(n)]


def kernel(x, mem, ffn1_norm, ffn1_w_gate, ffn1_w_up, ffn1_w_down, mix_norm, mem_norm, w_in, w_mem_kv, swa_q_norm, swa_k_norm, swa_sinks, rel_bias, gla_w_gate_up, gla_b_gate, gla_out_norm, mem_q_norm, mem_k_norm, w_out, ffn2_norm, ffn2_w_gate, ffn2_w_up, ffn2_w_down, loss_target, m_ffn1_norm, m_ffn1_w_gate, m_ffn1_w_up, m_ffn1_w_down, m_mix_norm, m_mem_norm, m_w_in, m_w_mem_kv, m_swa_q_norm, m_swa_k_norm, m_swa_sinks, m_rel_bias, m_gla_w_gate_up, m_gla_b_gate, m_gla_out_norm, m_mem_q_norm, m_mem_k_norm, m_w_out, m_ffn2_norm, m_ffn2_w_gate, m_ffn2_w_up, m_ffn2_w_down, v_ffn1_norm, v_ffn1_w_gate, v_ffn1_w_up, v_ffn1_w_down, v_mix_norm, v_mem_norm, v_w_in, v_w_mem_kv, v_swa_q_norm, v_swa_k_norm, v_swa_sinks, v_rel_bias, v_gla_w_gate_up, v_gla_b_gate, v_gla_out_norm, v_mem_q_norm, v_mem_k_norm, v_w_out, v_ffn2_norm, v_ffn2_w_gate, v_ffn2_w_up, v_ffn2_w_down):
    xi, yi, ci = lax.axis_index("x"), lax.axis_index("y"), lax.axis_index("c")
    c_arr = jnp.reshape(ci, (1,)).astype(jnp.int32)
    xy_arr = jnp.reshape(2 * xi + yi, (1,)).astype(jnp.int32)
    d = x.shape[-1]

    def ffn_shards(wg_s, wu_s, wd_s):
        return jnp.concatenate([wg_s.transpose(0, 2, 1), wu_s.transpose(0, 2, 1), wd_s], axis=0).astype(BF16)

    def gather_ffn(wg_s, wu_s, wd_s, name, collective_id, after):
        w3_s, _ = lax.optimization_barrier((ffn_shards(wg_s, wu_s, wd_s), after))
        return _all_gather([w3_s], ["row"], name=name, collective_id=collective_id)[0]

    w3_s = ffn_shards(ffn1_w_gate, ffn1_w_up, ffn1_w_down)
    w3_1a = _all_gather([w3_s[:, :FFN1_FIRST]], ["row"], name="gather_ffn1a", collective_id=0)[0]
    w3_s, _ = lax.optimization_barrier((w3_s, w3_1a))
    w3_1b = _all_gather([w3_s[:, FFN1_FIRST:]], ["row"], name="gather_ffn1b", collective_id=11)[0]

    def gather_mix(after):
        mix_s = lax.optimization_barrier((w_in[0].astype(BF16), w_mem_kv[0].astype(BF16), w_out[0].astype(BF16),
                                          (w3_1b, after)))[:3]
        win_all, wkv, wout = _all_gather(list(mix_s), ["stack", "row", "row"], name="gather_mix", collective_id=1)
        return _pack_win(win_all, tr=256, name="pack_w_in"), wkv, wout

    def gather_ffn2(after):
        return gather_ffn(ffn2_w_gate, ffn2_w_up, ffn2_w_down, "gather_ffn2", 2, after)

    small_w = [ffn1_norm, mix_norm, mem_norm, ffn2_norm, swa_q_norm, swa_k_norm, swa_sinks[0], rel_bias,
               gla_w_gate_up[0], gla_b_gate, gla_out_norm, mem_q_norm, mem_k_norm]
    collective_ids = {"ffn2": (3, 4), "mix": (5, 6), "ffn1a": (7, 8), "ffn1b": (9, 10)}
    reduced, small_box = {}, {}

    def on_grads(group, grads, carry, small=None):
        if group == "mix":
            dwin_p, dwkv, dwout = grads
            grads = [_unpack_win(dwin_p, tr=256, name="unpack_dw_in"), dwkv, dwout]
            kinds = ["stack", "row", "row"]
        else:
            kinds = ["row"]
        if reduced:
            earlier = list(reduced.values())[-1][1]
            *grads, _ = lax.optimization_barrier((*grads, earlier[0]))
        id_pair, id_chip = collective_ids[group]
        from_sibling = _pair_exchange(grads, kinds, name=f"pair_exchange_{group}", collective_id=id_pair)
        chip_sums = [_pair_sum(g, theirs, k, c_arr, name=f"pair_sum_{group}_{t}")
                     for t, (g, theirs, k) in enumerate(zip(grads, from_sibling, kinds))]
        if carry is not None:
            *chip_sums, carry = lax.optimization_barrier((*chip_sums, carry))
        if small is None:
            from_chips = _chip_exchange(chip_sums, None, name=f"chip_exchange_{group}", collective_id=id_chip)
        else:
            packed = _pack_small(small, name=f"pack_small_{group}")
            *from_chips, small_all = _chip_exchange(chip_sums, packed, name=f"chip_exchange_{group}",
                                                    collective_id=id_chip)
            small_box[group] = small_all
        reduced[group] = (chip_sums, from_chips)
        return carry

    grad_x = _local_step(x[0], mem[0], loss_target[0], small_w, ((w3_1a, w3_1b), gather_mix, gather_ffn2), on_grads)

    big_w = {"ffn1_w_gate": ("ffn1", 0, 0, True, ffn1_w_gate, m_ffn1_w_gate, v_ffn1_w_gate),
             "ffn1_w_up": ("ffn1", 0, 1, True, ffn1_w_up, m_ffn1_w_up, v_ffn1_w_up),
             "ffn1_w_down": ("ffn1", 0, 2, False, ffn1_w_down, m_ffn1_w_down, v_ffn1_w_down),
             "w_in": ("mix", 0, 0, False, w_in, m_w_in, v_w_in),
             "w_mem_kv": ("mix", 1, 0, False, w_mem_kv, m_w_mem_kv, v_w_mem_kv),
             "w_out": ("mix", 2, 0, False, w_out, m_w_out, v_w_out),
             "ffn2_w_gate": ("ffn2", 0, 0, True, ffn2_w_gate, m_ffn2_w_gate, v_ffn2_w_gate),
             "ffn2_w_up": ("ffn2", 0, 1, True, ffn2_w_up, m_ffn2_w_up, v_ffn2_w_up),
             "ffn2_w_down": ("ffn2", 0, 2, False, ffn2_w_down, m_ffn2_w_down, v_ffn2_w_down)}
    res = {}
    for nm, (group, t, mat, transposed, w, m, v) in big_w.items():
        shape = w.shape
        if transposed:
            w, m, v = (a.transpose(0, 2, 1) for a in (w, m, v))
        r = w.shape[1]
        halves = ["ffn1a", "ffn1b"] if group == "ffn1" else [group]
        if len(halves) == 1:
            tr = 256 if r % 256 == 0 else r
        else:
            tr = math.gcd(*(reduced[k][0][t].shape[-2] for k in halves))
        out = _adam_big([reduced[k][0][t] for k in halves], [reduced[k][1][t] for k in halves], mat, xy_arr, w, m, v,
                        tr=tr, name=f"adam_{nm}")
        if transposed:
            out = [a.reshape(1, -1, d).transpose(0, 2, 1) for a in out]
        res[nm] = [a.reshape(shape) for a in out]
    small_names = ["ffn1_norm", "mix_norm", "mem_norm", "ffn2_norm", "swa_q_norm", "swa_k_norm", "swa_sinks", "rel_bias",
                   "gla_w_gate_up", "gla_b_gate", "gla_out_norm", "mem_q_norm", "mem_k_norm"]
    small_m = [m_ffn1_norm, m_mix_norm, m_mem_norm, m_ffn2_norm, m_swa_q_norm, m_swa_k_norm, m_swa_sinks, m_rel_bias,
               m_gla_w_gate_up, m_gla_b_gate, m_gla_out_norm, m_mem_q_norm, m_mem_k_norm]
    small_v = [v_ffn1_norm, v_mix_norm, v_mem_norm, v_ffn2_norm, v_swa_q_norm, v_swa_k_norm, v_swa_sinks, v_rel_bias,
               v_gla_w_gate_up, v_gla_b_gate, v_gla_out_norm, v_mem_q_norm, v_mem_k_norm]
    small_full = [ffn1_norm, mix_norm, mem_norm, ffn2_norm, swa_q_norm, swa_k_norm, swa_sinks, rel_bias,
                  gla_w_gate_up, gla_b_gate, gla_out_norm, mem_q_norm, mem_k_norm]
    zero = jnp.zeros((1, 1), F32)
    two_d = lambda a: a.reshape(a.shape[-2:])
    for group, sel in (("ffn1a", slice(1, None)), ("ffn1b", slice(0, 1))):
        extra = [zero] if group == "ffn1a" else []
        ws, ms, vs = ([two_d(a) for a in arrs[sel]] + extra for arrs in (small_full, small_m, small_v))
        updated = _adam_small(small_box[group], ws, ms, vs, name=f"adam_small_{group}")
        for nm, full, out in zip(small_names[sel], small_full[sel], updated):
            res[nm] = [a.reshape(full.shape) for a in out]
        if extra:
            loss = updated[-1][0].reshape(())

    order = ["ffn1_norm", "ffn1_w_gate", "ffn1_w_up", "ffn1_w_down", "mix_norm", "mem_norm", "w_in", "w_mem_kv",
             "swa_q_norm", "swa_k_norm", "swa_sinks", "rel_bias", "gla_w_gate_up", "gla_b_gate", "gla_out_norm",
             "mem_q_norm", "mem_k_norm", "w_out", "ffn2_norm", "ffn2_w_gate", "ffn2_w_up", "ffn2_w_down"]
    outs = [loss, grad_x[None]]
    for q in range(4):
        outs += [res[nm][q] for nm in order]
    return tuple(outs)
```

```python
import functools
import math

import numpy as np
import jax
import jax.numpy as jnp
from jax import lax
from jax.experimental import pallas as pl
from jax.experimental.pallas import tpu as pltpu
from jax.experimental.pallas import tpu_sc as plsc

F32 = jnp.float32
BF16 = jnp.bfloat16
SDS = jax.ShapeDtypeStruct

EPS = 1e-6
HEAD_DIM = 64
SWA_HEADS = 8
SWA_KV_HEADS = 2
SWA_GROUP = SWA_HEADS // SWA_KV_HEADS
BLOCK = 128
N_BUCKETS = 32
MAX_DISTANCE = 128
GLA_HEADS = 4
GLA_DK = 32
GLA_DV = 64
GLA_RANK = 16
GLA_TAU = 16.0
GLA_CHUNK = 32
MEM_HEADS = 4
SWA_Q_W = SWA_HEADS * HEAD_DIM
SWA_KV_W = SWA_KV_HEADS * HEAD_DIM
GLA_QK_W = GLA_HEADS * GLA_DK
GLA_V_W = GLA_HEADS * GLA_DV
MEM_Q_W = MEM_HEADS * HEAD_DIM
IN_W = 1808
IN_W_PAD = 1920
COL_SQ, COL_SKV, COL_GQ, COL_GK, COL_GV, COL_GG, COL_MQ, COL_GLR = 0, 512, 768, 896, 1024, 1280, 1536, 1792

ADAM_LR = 0.001
ADAM_B1 = 0.9
ADAM_B2 = 0.999
ADAM_EPS = 1e-08
ADAM_WD = 0.01
ADAM_STEP = 10

N_DEV = 8
VMEM_LIMIT_BYTES = 56 * 1024 * 1024
MESH = pl.DeviceIdType.MESH


def _params(*sem):
    return pltpu.CompilerParams(dimension_semantics=sem or None, vmem_limit_bytes=VMEM_LIMIT_BYTES)


def _dot(a, b, ta, tb, precision=None):
    dims = (((0 if ta else 1,), (1 if tb else 0,)), ((), ()))
    return lax.dot_general(a, b, dims, preferred_element_type=F32, precision=precision)


def _mm_raw(a, b, ta=False, tb=False):
    return _dot(a.astype(BF16), b.astype(BF16), ta, tb)


def _mmf_raw(a, b, ta=False, tb=False):
    return _dot(a, b, ta, tb, lax.Precision.HIGHEST)


def _make_mm(raw):
    @functools.partial(jax.custom_vjp, nondiff_argnums=(2, 3))
    def mm(a, b, ta=False, tb=False):
        return raw(a, b, ta, tb)

    def fwd(a, b, ta, tb):
        return raw(a, b, ta, tb), (a, b)

    def bwd(ta, tb, res, g):
        a, b = res
        da = raw(b, g, tb, True) if ta else raw(g, b, False, not tb)
        db = raw(g, a, True, ta) if tb else raw(a, g, not ta, False)
        return da, db

    mm.defvjp(fwd, bwd)
    return mm


_mm = _make_mm(_mm_raw)
_mmf = _make_mm(_mmf_raw)


def _mm3(a, b, ta=False, tb=False):
    a_hi, b_hi = a.astype(BF16).astype(F32), b.astype(BF16).astype(F32)
    return _mm(a_hi, b_hi, ta, tb) + _mm(a_hi, b - b_hi, ta, tb) + _mm(a - a_hi, b_hi, ta, tb)


def _rms(x, g):
    return x * lax.rsqrt(jnp.mean(x * x, axis=-1, keepdims=True) + EPS) * g


def _silu_mul(g, u):
    return jax.nn.silu(g) * u


def _log_sigmoid(z):
    return jnp.minimum(z, 0.0) - jnp.log(1.0 + jnp.exp(-jnp.abs(z)))


def _matmul(a_list, b, *, ta=False, tb=False, tm, tn, b_blocks=None, res=None, scale=1.0, out_dtype=F32, name):
    if not isinstance(a_list, (list, tuple)):
        a_list = [a_list]
    n_a = len(a_list)
    m = a_list[0].shape[1] if ta else a_list[0].shape[0]
    ks = [a.shape[0] if ta else a.shape[1] for a in a_list]
    n = b.shape[0] if tb else b.shape[1]
    if b_blocks is None:
        assert n_a == 1
        b_blocks = [0]
    tm, tn = min(tm, m), min(tn, n)
    assert m % tm == 0 and n % tn == 0, (m, n, tm, tn)

    def body(*refs):
        a_refs, b_refs = refs[:n_a], refs[n_a:2 * n_a]
        r_ref = refs[2 * n_a] if res is not None else None
        o_ref = refs[-1]
        acc = _mm_raw(a_refs[0][...], b_refs[0][...], ta, tb)
        for k in range(1, n_a):
            acc = acc + _mm_raw(a_refs[k][...], b_refs[k][...], ta, tb)
        if scale != 1.0:
            acc = acc * scale
        if r_ref is not None:
            acc = r_ref[...] + acc
        o_ref[...] = acc.astype(out_dtype)

    in_specs = []
    for k in ks:
        in_specs.append(pl.BlockSpec((k, tm), lambda i, j: (0, i)) if ta else pl.BlockSpec((tm, k), lambda i, j: (i, 0)))
    for k, blk in zip(ks, b_blocks):
        if tb:
            in_specs.append(pl.BlockSpec((tn, k), functools.partial(lambda i, j, blk: (j, blk), blk=blk)))
        else:
            in_specs.append(pl.BlockSpec((k, tn), functools.partial(lambda i, j, blk: (blk, j), blk=blk)))
    args = list(a_list) + [b] * n_a
    if res is not None:
        in_specs.append(pl.BlockSpec((tm, tn), lambda i, j: (i, j)))
        args.append(res)
    return pl.pallas_call(
        body, name=name, grid=(m // tm, n // tn), in_specs=in_specs,
        out_specs=pl.BlockSpec((tm, tn), lambda i, j: (i, j)), out_shape=SDS((m, n), out_dtype),
        compiler_params=_params("parallel", "parallel"),
    )(*args)


def _win_pieces(w):
    glr_lo, glr_hi = COL_MQ, COL_MQ + GLA_RANK
    out = []
    for j in range(N_DEV):
        for lo, hi, shift in ((0, glr_lo, 0), (glr_lo, glr_hi, COL_GLR - glr_lo), (glr_hi, IN_W, COL_MQ - glr_hi)):
            s, e = max(j * w, lo), min((j + 1) * w, hi)
            if s < e:
                out.append((j, s - j * w, e - j * w, s + shift))
    return out


def _pack_win(win_all, *, tr, name):
    _, d, w = win_all.shape

    def body(i_ref, o_ref):
        for j, a, b, dst in _win_pieces(w):
            o_ref[:, dst:dst + b - a] = i_ref[j][:, a:b]
        o_ref[:, IN_W:] = jnp.zeros((tr, IN_W_PAD - IN_W), o_ref.dtype)

    return pl.pallas_call(
        body, name=name, grid=(d // tr,), in_specs=[pl.BlockSpec((N_DEV, tr, w), lambda i: (0, i, 0))],
        out_specs=pl.BlockSpec((tr, IN_W_PAD), lambda i: (i, 0)), out_shape=SDS((d, IN_W_PAD), win_all.dtype),
        compiler_params=_params("parallel"),
    )(win_all)


def _unpack_win(dwin_p, *, tr, name):
    d = dwin_p.shape[0]
    w = IN_W // N_DEV

    def body(i_ref, o_ref):
        for j, a, b, src in _win_pieces(w):
            o_ref[j % 2, j // 2, :, a:b] = i_ref[:, src:src + b - a]

    return pl.pallas_call(
        body, name=name, grid=(d // tr,), in_specs=[pl.BlockSpec((tr, IN_W_PAD), lambda i: (i, 0))],
        out_specs=pl.BlockSpec((2, 4, tr, w), lambda i: (0, 0, i, 0)), out_shape=SDS((2, 4, d, w), dwin_p.dtype),
        compiler_params=_params("parallel"),
    )(dwin_p)


def _rms_fwd(x, g, *, tm, name):
    s, d = x.shape

    def body(x_ref, g_ref, h_ref):
        h_ref[...] = _rms(x_ref[...], g_ref[...]).astype(BF16)

    return pl.pallas_call(
        body, name=name, grid=(s // tm,),
        in_specs=[pl.BlockSpec((tm, d), lambda i: (i, 0)), pl.BlockSpec((1, d), lambda i: (0, 0))],
        out_specs=pl.BlockSpec((tm, d), lambda i: (i, 0)), out_shape=SDS((s, d), BF16),
        compiler_params=_params("parallel"),
    )(x, g)


def _rms_bwd(x, g, dh, dres, *, tm, name):
    s, d = x.shape
    want_dx = dres is not None
    product = isinstance(dh, tuple)

    def body(*refs):
        n_dh = 2 if product else 1
        x_ref, g_ref = refs[:2]
        dh_refs, rest = refs[2:2 + n_dh], refs[2 + n_dh:]
        if want_dx:
            dres_ref, dx_ref, dxb_ref, dg_ref = rest
        else:
            dg_ref, = rest
        dh_tile = _mm_raw(dh_refs[0][...], dh_refs[1][...], False, True) if product else dh_refs[0][...]
        _, vjp = jax.vjp(_rms, x_ref[...], g_ref[...])
        dx, dg = vjp(dh_tile)
        if want_dx:
            dx = dres_ref[...] + dx
            dx_ref[...] = dx
            dxb_ref[...] = dx.astype(BF16)

        @pl.when(pl.program_id(0) == 0)
        def _():
            dg_ref[...] = jnp.zeros_like(dg_ref)

        dg_ref[...] += dg

    row = pl.BlockSpec((tm, d), lambda i: (i, 0))
    vec = pl.BlockSpec((1, d), lambda i: (0, 0))
    if product:
        k = dh[0].shape[1]
        dh_specs, dh_args = [pl.BlockSpec((tm, k), lambda i: (i, 0)), pl.BlockSpec((d, k), lambda i: (0, 0))], list(dh)
    else:
        dh_specs, dh_args = [row], [dh]
    if want_dx:
        return pl.pallas_call(
            body, name=name, grid=(s // tm,), in_specs=[row, vec] + dh_specs + [row], out_specs=[row, row, vec],
            out_shape=[SDS((s, d), F32), SDS((s, d), BF16), SDS((1, d), F32)], compiler_params=_params("arbitrary"),
        )(x, g, *dh_args, dres)
    return None, None, pl.pallas_call(
        body, name=name, grid=(s // tm,), in_specs=[row, vec] + dh_specs, out_specs=vec,
        out_shape=SDS((1, d), F32), compiler_params=_params("arbitrary"),
    )(x, g, *dh_args)


FFN_TN = 256
FFN_TN_FWD = 512
FFN1_FIRST = 192


def _ffn_fwd(x, gain, w3, tag, *, tm=1024, next_gain=None, target=None, start=None, partial=False):
    s, d = x.shape
    f = w3.shape[1]
    tn = FFN_TN_FWD if f % FFN_TN_FWD == 0 else FFN_TN
    nj = f // tn
    tm = min(tm, s)
    n_extra = (next_gain is not None) + (target is not None) + 2 * (start is not None)

    def body(*refs):
        x_ref, gain_ref, wg_ref, wu_ref, wd_ref = refs[:5]
        extra, outs = refs[5:5 + n_extra], refs[5 + n_extra:-1]
        acc_s = refs[-1]
        g_ref, u_ref = outs[-2:]
        h_ref = extra[-2] if start is not None else outs[-3]
        i, j = pl.program_id(0), pl.program_id(1)

        @pl.when(j == 0)
        def _():
            if start is None:
                h_ref[...] = _rms(x_ref[...], gain_ref[...]).astype(BF16)
                acc_s[...] = jnp.zeros_like(acc_s)
            else:
                acc_s[...] = extra[-1][...]

        hv = h_ref[...]
        g = _mm_raw(hv, wg_ref[...], False, True)
        u = _mm_raw(hv, wu_ref[...], False, True)
        g_ref[...] = g.astype(BF16)
        u_ref[...] = u.astype(BF16)
        acc_s[...] += _mm_raw(_silu_mul(g, u), wd_ref[...])

        @pl.when(j == nj - 1)
        def _():
            y = acc_s[...] if partial else x_ref[...] + 0.5 * acc_s[...]
            if target is None:
                outs[0][...] = y
                if next_gain is not None:
                    outs[1][...] = _rms(y, extra[0][...]).astype(BF16)
            else:
                dy_ref, dyb_ref, loss_ref = outs[:3]
                diff = y - extra[0][...]
                dy_ref[...] = diff * (1.0 / d)
                dyb_ref[...] = (diff * (1.0 / d)).astype(BF16)
                part = 0.5 * jnp.sum(jnp.mean(diff * diff, axis=-1, keepdims=True), axis=0, keepdims=True)

                @pl.when(i == 0)
                def _():
                    loss_ref[...] = part

                @pl.when(i > 0)
                def _():
                    loss_ref[...] += part

    row = pl.BlockSpec((tm, d), lambda i, j: (i, 0))
    vec = pl.BlockSpec((1, d), lambda i, j: (0, 0))
    tile = pl.BlockSpec((tm, tn), lambda i, j: (i, j))
    in_specs = [row, vec] + [pl.BlockSpec((None, tn, d), functools.partial(lambda i, j, k: (k, j, 0), k=k)) for k in range(3)]
    args = [x, gain, w3, w3, w3]
    if target is None:
        out_specs, out_shape = [row], [SDS((s, d), F32)]
        if next_gain is not None:
            in_specs.append(vec)
            args.append(next_gain)
            out_specs.append(row)
            out_shape.append(SDS((s, d), BF16))
    else:
        in_specs.append(row)
        args.append(target)
        out_specs = [row, row, pl.BlockSpec((1, 1), lambda i, j: (0, 0))]
        out_shape = [SDS((s, d), F32), SDS((s, d), BF16), SDS((1, 1), F32)]
    if start is None:
        out_specs.append(row)
        out_shape.append(SDS((s, d), BF16))
    else:
        in_specs += [row, row]
        args += list(start)
    *head, g, u = pl.pallas_call(
        body, name=f"{tag}_fwd", grid=(s // tm, nj), in_specs=in_specs,
        out_specs=out_specs + [tile, tile],
        out_shape=out_shape + [SDS((s, f), BF16), SDS((s, f), BF16)],
        scratch_shapes=[pltpu.VMEM((tm, d), F32)],
        compiler_params=_params("arbitrary", "arbitrary"),
    )(*args)
    if start is None:
        *head, h = head
    else:
        h = start[0]
    return head, (h, g, u)


def _ffn_bwd_part(dyb, w3, saved, first, count, dh_init, *, name):
    h, g, u = saved
    s, d = h.shape
    tn = FFN_TN

    def body(*refs):
        if dh_init is None:
            dy_ref, h_ref, wd_ref, wg_ref, wu_ref, g_ref, u_ref, dh_ref, dw3_ref, dg_s, du_s, a_s = refs
        else:
            dy_ref, h_ref, wd_ref, wg_ref, wu_ref, g_ref, u_ref, dh0_ref, dh_ref, dw3_ref, dg_s, du_s, a_s = refs
        j = pl.program_id(0)

        @pl.when(j == 0)
        def _():
            dh_ref[...] = jnp.zeros_like(dh_ref) if dh_init is None else dh0_ref[...]
            for ref in (dg_s, du_s, a_s):
                ref[...] = jnp.zeros_like(ref)

        now, before = j % 2, 1 - j % 2
        dyv = dy_ref[...]
        hv = h_ref[...]
        dg, du, a = dg_s[before], du_s[before], a_s[before]
        dh_ref[...] += _mm_raw(dg, wg_ref[...]) + _mm_raw(du, wu_ref[...])
        dw3_ref[0] = _mm_raw(dg, hv, True, False).astype(BF16)
        dw3_ref[1] = _mm_raw(du, hv, True, False).astype(BF16)
        dw3_ref[2] = (_mm_raw(a, dyv, True, False) * 0.5).astype(BF16)

        da = _mm_raw(dyv, wd_ref[...], False, True) * 0.5
        a, vjp = jax.vjp(_silu_mul, g_ref[...].astype(F32), u_ref[...].astype(F32))
        dg, du = vjp(da)
        dg_s[now] = dg.astype(BF16)
        du_s[now] = du.astype(BF16)
        a_s[now] = a.astype(BF16)

    this = lambda j: first + jnp.minimum(j, count - 1)
    last = lambda j: first + jnp.maximum(j - 1, 0)
    full = pl.BlockSpec((s, d), lambda j: (0, 0))
    once = pl.BlockSpec((s, d), lambda j: (0, 0), pipeline_mode=pl.Buffered(1))
    tile = pl.BlockSpec((s, tn), lambda j: (0, this(j)))
    in_specs = [once, once, pl.BlockSpec((None, tn, d), lambda j: (2, this(j), 0)),
                pl.BlockSpec((None, tn, d), lambda j: (0, last(j), 0)), pl.BlockSpec((None, tn, d), lambda j: (1, last(j), 0)),
                tile, tile]
    args = [dyb, h, w3, w3, w3, g, u]
    if dh_init is not None:
        in_specs.append(once)
        args.append(dh_init)
    return pl.pallas_call(
        body, name=name, grid=(count + 1,), in_specs=in_specs,
        out_specs=[full, pl.BlockSpec((3, tn, d), lambda j: (0, jnp.maximum(j - 1, 0), 0))],
        out_shape=[SDS((s, d), F32), SDS((3, count * tn, d), BF16)],
        scratch_shapes=[pltpu.VMEM((2, s, tn), BF16)] * 3,
        compiler_params=_params("arbitrary"),
    )(*args)


def _bucket_table():
    qi = np.arange(BLOCK)[:, None]
    kj = np.arange(2 * BLOCK)[None, :]
    dist = np.maximum(qi + BLOCK - kj, 0)
    max_exact = N_BUCKETS // 2
    d = np.maximum(dist, 1).astype(np.float32)
    large = max_exact + (np.log(d / np.float32(max_exact)) / np.float32(math.log(MAX_DISTANCE / max_exact))
                         * np.float32(N_BUCKETS - max_exact)).astype(np.int32)
    large = np.minimum(large, N_BUCKETS - 1)
    band = np.where(dist < max_exact, dist, large).astype(np.int32)
    return np.where(np.tril(np.ones((BLOCK, BLOCK), bool)), band[:, BLOCK:], band[:, :BLOCK])


SWA_STACK = SWA_GROUP * BLOCK


def _swa_masks(n):
    qi = lax.broadcasted_iota(jnp.int32, (SWA_STACK, BLOCK), 0) % BLOCK
    kj = lax.broadcasted_iota(jnp.int32, (SWA_STACK, BLOCK), 1)
    own = kj <= qi
    return own, own | (n > 0)


def _swa_group(q, kp, kc, vp, vc, qg, kg, sink, bias, own, valid):
    qn = _rms(q, qg)
    s = jnp.where(own, _mm(qn, _rms(kc, kg), False, True), _mm(qn, _rms(kp, kg), False, True))
    s = s * (HEAD_DIM ** -0.5) + bias
    s = jnp.where(valid, s, -jnp.inf)
    m = lax.stop_gradient(jnp.maximum(jnp.max(s, axis=-1, keepdims=True), sink))
    p = jnp.exp(s - m)
    p = p / (jnp.sum(p, axis=-1, keepdims=True) + jnp.exp(sink - m))
    return _mm(jnp.where(own, p, 0.0), vc) + _mm(jnp.where(own, 0.0, p), vp)


def _swa_bias_table(rb_ref, bucket, bias_s):
    for h in range(SWA_HEADS):
        acc = jnp.zeros((BLOCK, BLOCK), F32)
        for b in range(N_BUCKETS):
            acc = jnp.where(bucket == b, rb_ref[b, h], acc)
        bias_s[h // SWA_GROUP, (h % SWA_GROUP) * BLOCK:(h % SWA_GROUP + 1) * BLOCK, :] = acc


def _swa_stack(ref, g):
    return jnp.concatenate([ref[:, (g * SWA_GROUP + hh) * HEAD_DIM:(g * SWA_GROUP + hh + 1) * HEAD_DIM]
                            for hh in range(SWA_GROUP)], axis=0)


def _swa_unstack(ref, g, stacked):
    for hh in range(SWA_GROUP):
        h = g * SWA_GROUP + hh
        ref[:, h * HEAD_DIM:(h + 1) * HEAD_DIM] = stacked[hh * BLOCK:(hh + 1) * BLOCK]


def _swa_sink_column(sink_ref, g):
    head = lax.broadcasted_iota(jnp.int32, (SWA_STACK, 1), 0) // BLOCK
    col = jnp.zeros((SWA_STACK, 1), F32)
    for hh in range(SWA_GROUP):
        col = jnp.where(head == hh, sink_ref[g * SWA_GROUP + hh], col)
    return col


def _swa_band(kvp_ref, kvc_ref, g):
    k = slice(g * HEAD_DIM, (g + 1) * HEAD_DIM)
    v = slice(SWA_KV_W + g * HEAD_DIM, SWA_KV_W + (g + 1) * HEAD_DIM)
    return kvp_ref[:, k], kvc_ref[:, k], kvp_ref[:, v], kvc_ref[:, v]


def _swa_specs(order):
    kvc = COL_SKV // (2 * SWA_KV_W)
    return [
        pl.BlockSpec((BLOCK, SWA_Q_W), lambda t: (order(t), 0)),
        pl.BlockSpec((BLOCK, 2 * SWA_KV_W), lambda t: (jnp.maximum(order(t) - 1, 0), kvc)),
        pl.BlockSpec((BLOCK, 2 * SWA_KV_W), lambda t: (order(t), kvc)),
        pl.BlockSpec((1, HEAD_DIM), lambda t: (0, 0)),
        pl.BlockSpec((1, HEAD_DIM), lambda t: (0, 0)),
        pl.BlockSpec(memory_space=pltpu.SMEM),
        pl.BlockSpec(memory_space=pltpu.SMEM),
        pl.BlockSpec((BLOCK, BLOCK), lambda t: (0, 0)),
    ]


def _swa_fwd(p, qg, kg, sinks, rel_bias, *, name):
    s = p.shape[0]
    nb = s // BLOCK

    def body(q_ref, kvp_ref, kvc_ref, qg_ref, kg_ref, sink_ref, rb_ref, bucket_ref, y_ref, bias_s):
        n = pl.program_id(0)

        @pl.when(n == 0)
        def _():
            _swa_bias_table(rb_ref, bucket_ref[...], bias_s)

        own, valid = _swa_masks(n)
        for g in range(SWA_KV_HEADS):
            out = _swa_group(_swa_stack(q_ref, g), *_swa_band(kvp_ref, kvc_ref, g), qg_ref[...], kg_ref[...],
                             _swa_sink_column(sink_ref, g), bias_s[g], own, valid)
            _swa_unstack(y_ref, g, out)

    return pl.pallas_call(
        body, name=name, grid=(nb,), in_specs=_swa_specs(lambda t: t),
        out_specs=pl.BlockSpec((BLOCK, SWA_Q_W), lambda t: (t, 0)), out_shape=SDS((s, SWA_Q_W), F32),
        scratch_shapes=[pltpu.VMEM((SWA_KV_HEADS, SWA_STACK, BLOCK), F32)],
        compiler_params=_params("arbitrary"),
    )(p, p, p, qg, kg, sinks, rel_bias, jnp.asarray(_bucket_table()))


def _swa_bwd(p, qg, kg, sinks, rel_bias, dy_all, *, name):
    s = p.shape[0]
    nb = s // BLOCK

    def body(q_ref, kvp_ref, kvc_ref, qg_ref, kg_ref, sink_ref, rb_ref, bucket_ref, dy_ref,
             dq_ref, dkv_ref, dqg_ref, dkg_ref, dsink_ref, drb_ref, bias_s, dbias_s, carry_s):
        t = pl.program_id(0)
        n = nb - 1 - t

        @pl.when(t == 0)
        def _():
            _swa_bias_table(rb_ref, bucket_ref[...], bias_s)
            dbias_s[...] = jnp.zeros_like(dbias_s)
            carry_s[...] = jnp.zeros_like(carry_s)
            dqg_ref[...] = jnp.zeros_like(dqg_ref)
            dkg_ref[...] = jnp.zeros_like(dkg_ref)
            dsink_ref[...] = jnp.zeros_like(dsink_ref)
            drb_ref[...] = jnp.zeros_like(drb_ref)

        own, valid = _swa_masks(n)
        lane = lax.broadcasted_iota(jnp.int32, (1, BLOCK), 1)
        dqg = jnp.zeros((1, HEAD_DIM), F32)
        dkg = jnp.zeros((1, HEAD_DIM), F32)
        dsink_vec = jnp.zeros((1, BLOCK), F32)
        for g in range(SWA_KV_HEADS):
            _, vjp = jax.vjp(functools.partial(_swa_group, own=own, valid=valid), _swa_stack(q_ref, g),
                             *_swa_band(kvp_ref, kvc_ref, g), qg_ref[...], kg_ref[...], _swa_sink_column(sink_ref, g),
                             bias_s[g])
            dq, dkp, dkc, dvp, dvc, dqg_g, dkg_g, dsink_col, dbias = vjp(_swa_stack(dy_ref, g))
            _swa_unstack(dq_ref, g, dq)
            dqg += dqg_g
            dkg += dkg_g
            dbias_s[g] += dbias
            for hh in range(SWA_GROUP):
                dsink_h = jnp.sum(dsink_col[hh * BLOCK:(hh + 1) * BLOCK], axis=0, keepdims=True)
                dsink_vec += jnp.where(lane == g * SWA_GROUP + hh, dsink_h, 0.0)
            lo = g * HEAD_DIM
            dkv_ref[:, lo:lo + HEAD_DIM] = dkc + carry_s[g]
            carry_s[g] = dkp
            lo += SWA_KV_W
            dkv_ref[:, lo:lo + HEAD_DIM] = dvc + carry_s[SWA_KV_HEADS + g]
            carry_s[SWA_KV_HEADS + g] = dvp
        dqg_ref[...] += dqg
        dkg_ref[...] += dkg
        dsink_ref[...] += dsink_vec

        @pl.when(t == nb - 1)
        def _():
            bucket = bucket_ref[...]
            row = lax.broadcasted_iota(jnp.int32, (N_BUCKETS, BLOCK), 0)
            col = lax.broadcasted_iota(jnp.int32, (N_BUCKETS, BLOCK), 1)
            acc = jnp.zeros((N_BUCKETS, BLOCK), F32)
            for h in range(SWA_HEADS):
                dbias = dbias_s[h // SWA_GROUP, (h % SWA_GROUP) * BLOCK:(h % SWA_GROUP + 1) * BLOCK, :]
                for b in range(N_BUCKETS):
                    part = jnp.sum(jnp.where(bucket == b, dbias, 0.0), axis=1, keepdims=True)
                    val = jnp.sum(part, axis=0, keepdims=True)
                    acc = acc + jnp.where((row == b) & (col == h), val, 0.0)
            drb_ref[...] = acc

    order = lambda t: nb - 1 - t
    vec = pl.BlockSpec((1, HEAD_DIM), lambda t: (0, 0))
    return pl.pallas_call(
        body, name=name, grid=(nb,),
        in_specs=_swa_specs(order) + [pl.BlockSpec((BLOCK, SWA_Q_W), lambda t: (order(t), 0))],
        out_specs=[pl.BlockSpec((BLOCK, SWA_Q_W), lambda t: (order(t), 0)),
                   pl.BlockSpec((BLOCK, 2 * SWA_KV_W), lambda t: (order(t), 0)),
                   vec, vec, pl.BlockSpec((1, BLOCK), lambda t: (0, 0)),
                   pl.BlockSpec((N_BUCKETS, BLOCK), lambda t: (0, 0))],
        out_shape=[SDS((s, SWA_Q_W), F32), SDS((s, 2 * SWA_KV_W), F32), SDS((1, HEAD_DIM), F32),
                   SDS((1, HEAD_DIM), F32), SDS((1, BLOCK), F32), SDS((N_BUCKETS, BLOCK), F32)],
        scratch_shapes=[pltpu.VMEM((SWA_KV_HEADS, SWA_STACK, BLOCK), F32),
                        pltpu.VMEM((SWA_KV_HEADS, SWA_STACK, BLOCK), F32),
                        pltpu.VMEM((2 * SWA_KV_HEADS, BLOCK, HEAD_DIM), F32)],
        compiler_params=_params("arbitrary"),
    )(p, p, p, qg, kg, sinks, rel_bias, jnp.asarray(_bucket_table()), dy_all)


def _mem_head(q, k, v, qg, kg):
    qn = _rms(q, qg)
    kn = _rms(k, kg)
    s = _mm(qn, kn, False, True) * (HEAD_DIM ** -0.5)
    m = lax.stop_gradient(jnp.max(s, axis=-1, keepdims=True))
    e = jnp.exp(s - m)
    return _mm(e / jnp.sum(e, axis=-1, keepdims=True), v)


def _mem_fwd(p, kv, qg, kg, *, tq, name):
    s = p.shape[0]
    m = kv.shape[0]

    def body(q_ref, kv_ref, qg_ref, kg_ref, y_ref):
        for h in range(MEM_HEADS):
            cols = slice(h * HEAD_DIM, (h + 1) * HEAD_DIM)
            vcols = slice(MEM_Q_W + h * HEAD_DIM, MEM_Q_W + (h + 1) * HEAD_DIM)
            y_ref[:, cols] = _mem_head(q_ref[:, cols], kv_ref[:, cols], kv_ref[:, vcols], qg_ref[...], kg_ref[...])

    vec = pl.BlockSpec((1, HEAD_DIM), lambda t: (0, 0))
    return pl.pallas_call(
        body, name=name, grid=(s // tq,),
        in_specs=[pl.BlockSpec((tq, MEM_Q_W), lambda t: (t, COL_MQ // MEM_Q_W)),
                  pl.BlockSpec((m, 2 * MEM_Q_W), lambda t: (0, 0)), vec, vec],
        out_specs=pl.BlockSpec((tq, MEM_Q_W), lambda t: (t, 0)), out_shape=SDS((s, MEM_Q_W), F32),
        compiler_params=_params("parallel"),
    )(p, kv, qg, kg)


def _mem_bwd(p, kv, qg, kg, dy_all, *, tq, name):
    s = p.shape[0]
    m = kv.shape[0]

    def body(q_ref, kv_ref, qg_ref, kg_ref, dy_ref, dq_ref, dkv_ref, dqg_ref, dkg_ref):
        @pl.when(pl.program_id(0) == 0)
        def _():
            dkv_ref[...] = jnp.zeros_like(dkv_ref)
            dqg_ref[...] = jnp.zeros_like(dqg_ref)
            dkg_ref[...] = jnp.zeros_like(dkg_ref)

        dqg = jnp.zeros((1, HEAD_DIM), F32)
        dkg = jnp.zeros((1, HEAD_DIM), F32)
        for h in range(MEM_HEADS):
            cols = slice(h * HEAD_DIM, (h + 1) * HEAD_DIM)
            vcols = slice(MEM_Q_W + h * HEAD_DIM, MEM_Q_W + (h + 1) * HEAD_DIM)
            _, vjp = jax.vjp(_mem_head, q_ref[:, cols], kv_ref[:, cols], kv_ref[:, vcols], qg_ref[...], kg_ref[...])
            dq, dk, dv, dqg_h, dkg_h = vjp(dy_ref[:, cols])
            dq_ref[:, cols] = dq
            dkv_ref[:, cols] += dk
            dkv_ref[:, vcols] += dv
            dqg += dqg_h
            dkg += dkg_h
        dqg_ref[...] += dqg
        dkg_ref[...] += dkg

    vec = pl.BlockSpec((1, HEAD_DIM), lambda t: (0, 0))
    full = pl.BlockSpec((m, 2 * MEM_Q_W), lambda t: (0, 0))
    dy_col = (SWA_Q_W + GLA_V_W) // MEM_Q_W
    return pl.pallas_call(
        body, name=name, grid=(s // tq,),
        in_specs=[pl.BlockSpec((tq, MEM_Q_W), lambda t: (t, COL_MQ // MEM_Q_W)), full, vec, vec,
                  pl.BlockSpec((tq, MEM_Q_W), lambda t: (t, dy_col))],
        out_specs=[pl.BlockSpec((tq, MEM_Q_W), lambda t: (t, 0)), full, vec, vec],
        out_shape=[SDS((s, MEM_Q_W), F32), SDS((m, 2 * MEM_Q_W), F32), SDS((1, HEAD_DIM), F32), SDS((1, HEAD_DIM), F32)],
        compiler_params=_params("arbitrary"),
    )(p, kv, qg, kg, dy_all)


GLA_ROWS = 256


GLA_GROUP = 4


def _gla_consts():
    c, h, r = GLA_CHUNK, GLA_HEADS, GLA_GROUP * GLA_CHUNK
    i2 = lax.broadcasted_iota(jnp.int32, (c, c), 0)
    j2 = lax.broadcasted_iota(jnp.int32, (c, c), 1)
    slab_q = lax.broadcasted_iota(jnp.int32, (h, r, GLA_QK_W), 0)
    lane_q = lax.broadcasted_iota(jnp.int32, (h, r, GLA_QK_W), 2)
    row_a = lax.broadcasted_iota(jnp.int32, (h * r, r), 0) % r
    col_a = lax.broadcasted_iota(jnp.int32, (h * r, r), 1)
    slab_o = lax.broadcasted_iota(jnp.int32, (h, r, GLA_V_W), 0)
    lane_o = lax.broadcasted_iota(jnp.int32, (h, r, GLA_V_W), 2)
    row_s = lax.broadcasted_iota(jnp.int32, (GLA_V_W, GLA_QK_W), 0)
    col_s = lax.broadcasted_iota(jnp.int32, (GLA_V_W, GLA_QK_W), 1)
    return dict(
        ltri=(j2 <= i2).astype(F32),
        m_q=(slab_q == lane_q // GLA_DK).astype(F32),
        causal=(col_a <= row_a) & (col_a // c == row_a // c),
        m_o=(slab_o == lane_o // GLA_DV).astype(F32),
        m_s=(row_s // GLA_DV == col_s // GLA_DK).astype(F32),
    )


def _gla_step(q, k, v, z, bg, st, c):
    h = GLA_HEADS
    kt, qt, qe, decay = [], [], [], []
    for qc, kc, zc in zip(q, k, z):
        la = _log_sigmoid(zc + bg) * (1.0 / GLA_TAU)
        b = _mmf(c["ltri"], la)
        bl = jnp.sum(la, axis=0, keepdims=True)
        qs = qc * (GLA_DK ** -0.5)
        kt.append(kc * jnp.exp(bl - b))
        qt.append(qs * jnp.exp(b - bl))
        qe.append(qs * jnp.exp(b))
        decay.append(jnp.exp(bl))
    o_intra = []
    rows = GLA_GROUP * GLA_CHUNK
    for lo in range(0, len(q), GLA_GROUP):
        qt_all, kt_all, v_all = (jnp.concatenate(parts[lo:lo + GLA_GROUP], axis=0) for parts in (qt, kt, v))
        q_stack = (jnp.broadcast_to(qt_all[None], (h, rows, GLA_QK_W)) * c["m_q"]).reshape(h * rows, GLA_QK_W)
        a = jnp.where(c["causal"], _mm3(q_stack, kt_all, False, True), 0.0)
        o_stack = _mm(a, v_all)
        o_intra.append(jnp.sum(o_stack.reshape(h, rows, GLA_V_W) * c["m_o"], axis=0))
    o_intra = jnp.concatenate(o_intra, axis=0)
    o_inter = []
    for qec, ktc, vc, dc in zip(qe, kt, v, decay):
        o_inter.append(_mm(qec, st, False, True))
        st = st * dc + _mm(vc, ktc, True, False) * c["m_s"]
    return o_intra + jnp.concatenate(o_inter, axis=0), st


def _gla_post(o, gg, gain, g64):
    ms = _mmf(o * o, g64) * (1.0 / GLA_DV)
    return o * lax.rsqrt(ms + EPS) * gain * jax.nn.silu(gg)


def _gla_g64():
    r = lax.broadcasted_iota(jnp.int32, (GLA_V_W, GLA_V_W), 0)
    c = lax.broadcasted_iota(jnp.int32, (GLA_V_W, GLA_V_W), 1)
    return (r // GLA_DV == c // GLA_DV).astype(F32)


def _gla_in_specs(order):
    r = GLA_ROWS
    return [
        pl.BlockSpec((r, GLA_QK_W), lambda t: (order(t), COL_GQ // GLA_QK_W)),
        pl.BlockSpec((r, GLA_QK_W), lambda t: (order(t), COL_GK // GLA_QK_W)),
        pl.BlockSpec((r, GLA_V_W), lambda t: (order(t), COL_GV // GLA_V_W)),
        pl.BlockSpec((r, GLA_V_W), lambda t: (order(t), COL_GG // GLA_V_W)),
        pl.BlockSpec((r, GLA_QK_W), lambda t: (order(t), 0)),
        pl.BlockSpec((1, GLA_QK_W), lambda t: (0, 0)),
        pl.BlockSpec((1, GLA_V_W), lambda t: (0, 0)),
    ]


def _gla_pieces(q_ref, k_ref, v_ref, z_ref, cps):
    chunk = lambda ref: [ref[ci * GLA_CHUNK:(ci + 1) * GLA_CHUNK, :] for ci in range(cps)]
    return chunk(q_ref), chunk(k_ref), chunk(v_ref), chunk(z_ref)


def _gla_fwd(p, z, bg, gain, *, name):
    s = p.shape[0]
    r = GLA_ROWS
    cps = r // GLA_CHUNK

    def body(q_ref, k_ref, v_ref, gg_ref, z_ref, bg_ref, gain_ref, y_ref, oraw_ref, stsave_ref, st_s):
        @pl.when(pl.program_id(0) == 0)
        def _():
            st_s[...] = jnp.zeros_like(st_s)

        st = st_s[...]
        stsave_ref[0] = st
        o, st = _gla_step(*_gla_pieces(q_ref, k_ref, v_ref, z_ref, cps), bg_ref[...], st, _gla_consts())
        oraw_ref[...] = o
        st_s[...] = st
        y_ref[...] = _gla_post(o, gg_ref[...], gain_ref[...], _gla_g64())

    rowv = pl.BlockSpec((r, GLA_V_W), lambda t: (t, 0))
    return pl.pallas_call(
        body, name=name, grid=(s // r,), in_specs=_gla_in_specs(lambda t: t),
        out_specs=[rowv, rowv, pl.BlockSpec((1, GLA_V_W, GLA_QK_W), lambda t: (t, 0, 0))],
        out_shape=[SDS((s, GLA_V_W), F32), SDS((s, GLA_V_W), F32), SDS((s // r, GLA_V_W, GLA_QK_W), F32)],
        scratch_shapes=[pltpu.VMEM((GLA_V_W, GLA_QK_W), F32)],
        compiler_params=_params("arbitrary"),
    )(p, p, p, p, z, bg, gain)


def _gla_bwd(p, z, bg, gain, oraw, stsave, dy_all, *, name):
    s = p.shape[0]
    r = GLA_ROWS
    cps = r // GLA_CHUNK
    nsteps = s // r
    w_qkvg = 2 * GLA_QK_W + 2 * GLA_V_W

    def body(q_ref, k_ref, v_ref, gg_ref, z_ref, bg_ref, gain_ref, oraw_ref, stsave_ref, dy_ref,
             dqkvg_ref, dz_ref, dbg_ref, dgain_ref, dst_s):
        @pl.when(pl.program_id(0) == 0)
        def _():
            dst_s[...] = jnp.zeros_like(dst_s)
            dbg_ref[...] = jnp.zeros_like(dbg_ref)
            dgain_ref[...] = jnp.zeros_like(dgain_ref)

        _, vjp = jax.vjp(functools.partial(_gla_post, g64=_gla_g64()), oraw_ref[...], gg_ref[...], gain_ref[...])
        do, dgg, dgain = vjp(dy_ref[...])
        dqkvg_ref[:, 2 * GLA_QK_W + GLA_V_W:] = dgg
        dgain_ref[...] += dgain
        _, vjp = jax.vjp(functools.partial(_gla_step, c=_gla_consts()), *_gla_pieces(q_ref, k_ref, v_ref, z_ref, cps),
                         bg_ref[...], stsave_ref[0])
        dq, dk, dv, dz, dbg, dst = vjp((do, dst_s[...]))
        for ci in range(cps):
            rows = slice(ci * GLA_CHUNK, (ci + 1) * GLA_CHUNK)
            dqkvg_ref[rows, 0:GLA_QK_W] = dq[ci]
            dqkvg_ref[rows, GLA_QK_W:2 * GLA_QK_W] = dk[ci]
            dqkvg_ref[rows, 2 * GLA_QK_W:2 * GLA_QK_W + GLA_V_W] = dv[ci]
            dz_ref[rows, :] = dz[ci]
        dst_s[...] = dst
        dbg_ref[...] += dbg

    order = lambda t: nsteps - 1 - t
    rowv = pl.BlockSpec((r, GLA_V_W), lambda t: (order(t), 0))
    return pl.pallas_call(
        body, name=name, grid=(nsteps,),
        in_specs=_gla_in_specs(order) + [
            rowv, pl.BlockSpec((1, GLA_V_W, GLA_QK_W), lambda t: (order(t), 0, 0)),
            pl.BlockSpec((r, GLA_V_W), lambda t: (order(t), SWA_Q_W // GLA_V_W))],
        out_specs=[pl.BlockSpec((r, w_qkvg), lambda t: (order(t), 0)), pl.BlockSpec((r, GLA_QK_W), lambda t: (order(t), 0)),
                   pl.BlockSpec((1, GLA_QK_W), lambda t: (0, 0)), pl.BlockSpec((1, GLA_V_W), lambda t: (0, 0))],
        out_shape=[SDS((s, w_qkvg), F32), SDS((s, GLA_QK_W), F32), SDS((1, GLA_QK_W), F32), SDS((1, GLA_V_W), F32)],
        scratch_shapes=[pltpu.VMEM((GLA_V_W, GLA_QK_W), F32)],
        compiler_params=_params("arbitrary"),
    )(p, p, p, p, z, bg, gain, oraw, stsave, dy_all)


def _local_step(x, mem, target, small, big, on_grads):
    g1, gmix, gmem, g2, sqg, skg, sinks, rel_bias, wgu, bg, gla_gain, mqg, mkg = small
    (w3_1a, w3_1b), gather_mix, gather_ffn2 = big
    wgu_pad = jnp.zeros((GLA_QK_W, GLA_QK_W), BF16).at[:GLA_RANK].set(wgu.astype(BF16))
    gain256 = jnp.tile(gla_gain, (1, GLA_HEADS))

    (part,), saved1a = _ffn_fwd(x, g1, w3_1a, "ffn1a", partial=True)
    win_p, wkv, wout = gather_mix(part)
    (x1, h), saved1b = _ffn_fwd(x, g1, w3_1b, "ffn1b", next_gain=gmix, start=(saved1a[0], part))
    w3_2 = gather_ffn2((wout, x1))
    p = _matmul(h, win_p, tm=512, tn=IN_W_PAD, name="mix_in")
    hm = _rms_fwd(mem, gmem, tm=256, name="mem_rms")
    kv = _matmul(hm, wkv, tm=256, tn=512, name="mem_kv")
    p_glr = p[:, COL_GLR:]
    z = _matmul(p_glr, wgu_pad, tm=1024, tn=GLA_QK_W, name="gla_gate")
    y_swa = _swa_fwd(p, sqg, skg, sinks, rel_bias, name="swa_fwd")
    y_gla, oraw, stsave = _gla_fwd(p, z, bg, gain256, name="gla_fwd")
    y_mem = _mem_fwd(p, kv, mqg, mkg, tq=512, name="mem_fwd")
    x2 = _matmul([y_swa, y_gla, y_mem], wout, b_blocks=[0, 2, 3], tm=512, tn=1024, res=x1, name="mix_out")
    (dy, dyb, loss), saved2 = _ffn_fwd(x2, g2, w3_2, "ffn2", target=target)

    dh2, dw3_2 = _ffn_bwd_part(dyb, w3_2, saved2, 0, w3_2.shape[1] // FFN_TN, None, name="ffn2_bwd")
    dx2, dx2b, dg2 = _rms_bwd(x2, g2, dh2, dy, tm=512, name="ffn2_drms")
    dx2b = on_grads("ffn2", [dw3_2], dx2b)
    dy_all = _matmul(dx2b, wout, tb=True, tm=512, tn=1024, name="mix_dy")
    dwout = _matmul(jnp.concatenate([y_swa, y_gla, y_mem], axis=1), dx2b, ta=True, tm=512, tn=1024, out_dtype=BF16,
                    name="mix_dw_out")
    dq_swa, dkv_swa, dsqg, dskg, dsink, drb = _swa_bwd(p, sqg, skg, sinks, rel_bias, dy_all, name="swa_bwd")
    dqkvg, dz, dbg, dgain256 = _gla_bwd(p, z, bg, gain256, oraw, stsave, dy_all, name="gla_bwd")
    dmq, dkv_mem, dmqg, dmkg = _mem_bwd(p, kv, mqg, mkg, dy_all, tq=512, name="mem_bwd")
    dglr = _matmul(dz, wgu_pad, tb=True, tm=1024, tn=GLA_QK_W, name="gla_gate_dx")
    dwgu_pad = _matmul(p_glr, dz, ta=True, tm=GLA_QK_W, tn=GLA_QK_W, name="gla_gate_dw")
    dp = jnp.concatenate([dq_swa, dkv_swa, dqkvg, dmq, dglr], axis=1)
    dwin_p = _matmul(h, dp, ta=True, tm=1024, tn=640, out_dtype=BF16, name="mix_dw_in")
    dx1, dx1b, dgmix = _rms_bwd(x1, gmix, (dp, win_p), dx2, tm=512, name="mix_dh_drms")
    dwkv = _matmul(hm, dkv_mem, ta=True, tm=512, tn=512, out_dtype=BF16, name="mem_dw_kv")
    dx1b = on_grads("mix", (dwin_p, dwkv, dwout), dx1b)
    _, _, dgmem = _rms_bwd(mem, gmem, (dkv_mem, wkv), None, tm=256, name="mem_dh_drms")
    dh1, dw3_1a = _ffn_bwd_part(dx1b, w3_1a, saved1a, 0, w3_1a.shape[1] // FFN_TN, None, name="ffn1_bwd_a")
    dgla_gain = dgain256.reshape(GLA_HEADS, GLA_DV).sum(axis=0, keepdims=True)
    dsmall = [dgmix, dgmem, dg2, dsqg, dskg, dsink[:, :SWA_HEADS], drb[:, :SWA_HEADS], dwgu_pad[:GLA_RANK], dbg,
              dgla_gain, dmqg, dmkg, loss]
    dh1, dgmem, dwgu_pad = on_grads("ffn1a", [dw3_1a], (dh1, dgmem, dwgu_pad), small=dsmall)
    dh1, dw3_1b = _ffn_bwd_part(dx1b, w3_1b, saved1b, 0, w3_1b.shape[1] // FFN_TN, dh1, name="ffn1_bwd_b")
    dx, _, dg1 = _rms_bwd(x, g1, dh1, dx1, tm=512, name="ffn1_drms")
    on_grads("ffn1b", [dw3_1b], None, small=[dg1])
    return dx


def _mesh_place():
    x, y, c = lax.axis_index("x"), lax.axis_index("y"), lax.axis_index("c")
    other_chips = [(1 - x, y), (x, 1 - y), (1 - x, 1 - y)]
    return x, y, c, other_chips


def _handshake(peers):
    barrier = pltpu.get_barrier_semaphore()
    for peer in peers:
        pl.semaphore_signal(barrier, inc=1, device_id=peer, device_id_type=MESH)
    pl.semaphore_wait(barrier, len(peers))


def _sequencer_call(body, operands, out_shapes, sems, *, name, collective_id):
    return pl.kernel(
        body, name=name, out_type=out_shapes, mesh=plsc.ScalarSubcoreMesh(axis_name="sequencer", num_cores=1),
        scratch_types=sems, compiler_params=pltpu.CompilerParams(collective_id=collective_id),
    )(*operands)


def _window(ref, kind, slot, shape):
    if kind == "row":
        rows = pl.ds(pl.multiple_of(slot * shape[-2], 8), shape[-2])
        return ref.at[(slice(None),) * (len(shape) - 2) + (rows,)]
    return ref.at[slot]


def _gathered(shape, kind):
    if kind == "row":
        return tuple(shape[:-2]) + (N_DEV * shape[-2], shape[-1])
    return (N_DEV,) + tuple(shape)


def _half(view, hf):
    if len(view.shape) == 4:
        return view.at[:, hf]
    n = view.shape[-2] // 2
    return view.at[(slice(None),) * (len(view.shape) - 2) + (pl.ds(hf * n, n),)]


def _all_gather(shards, kinds, *, name, collective_id):
    nt = len(shards)

    def body(*refs):
        x_refs, o_refs = refs[:nt], refs[nt:2 * nt]
        send_sems, recv_sems, local_sems = refs[2 * nt:]
        x, y, c, _ = _mesh_place()
        me, sibling, xn, yn, diag = (x, y, c), (x, y, 1 - c), (1 - x, y, c), (x, 1 - y, c), (1 - x, 1 - y, c)
        _handshake([sibling, xn, yn])

        def win(t, block):
            bx, by, bc = block
            return _window(o_refs[t], kinds[t], 4 * bx + 2 * by + bc, shards[t].shape)

        def copy(k, t, src, dst, to):
            return pltpu.make_async_remote_copy(src_ref=src, dst_ref=dst, send_sem=send_sems.at[k, t],
                                                recv_sem=recv_sems.at[k, t], device_id=to, device_id_type=MESH)

        def piece(k, t, block, hf, to, from_shard=False):
            dst = _half(win(t, block), hf)
            return copy(k, t, _half(x_refs[t], hf) if from_shard else dst, dst, to)

        mine = [pltpu.make_async_copy(x_refs[t], win(t, me), local_sems.at[t]) for t in range(nt)]
        sent = []

        def start(cp):
            cp.start()
            sent.append(cp)

        for cp in mine:
            cp.start()
        for t in range(nt):
            start(copy(0, t, x_refs[t], win(t, me), sibling))
        for hf_x, hf_y in ((0, 1), (1, 0)):
            for t in range(nt):
                start(piece(1 + hf_x, t, me, hf_x, xn, True))
                start(piece(3 + hf_y, t, me, hf_y, yn, True))
        for k, block, hf, onward, k_sib in ((1, xn, 0, (5, yn), 7), (4, yn, 1, (6, xn), 10), (2, xn, 1, None, 8),
                                           (3, yn, 0, None, 9), (5, diag, 0, None, 11), (6, diag, 1, None, 12)):
            for t in range(nt):
                piece(k, t, block, hf, me).wait_recv()
                if onward is not None:
                    start(piece(onward[0], t, block, hf, onward[1]))
                start(piece(k_sib, t, block, hf, sibling))
        for t in range(nt):
            copy(0, t, x_refs[t], win(t, sibling), me).wait_recv()
        for k_sib, block, hf in ((7, xn, 0), (10, yn, 1), (8, xn, 1), (9, yn, 0), (11, diag, 0), (12, diag, 1)):
            for t in range(nt):
                bx, by, _ = block
                piece(k_sib, t, (bx, by, 1 - c), hf, me).wait_recv()
        for cp in sent:
            cp.wait_send()
        for cp in mine:
            cp.wait()

    return _sequencer_call(
        body, shards, [SDS(_gathered(s.shape, k), s.dtype) for s, k in zip(shards, kinds)],
        [pltpu.SemaphoreType.DMA((13, nt)), pltpu.SemaphoreType.DMA((13, nt)), pltpu.SemaphoreType.DMA((nt,))],
        name=name, collective_id=collective_id)


def _part_shape(shape, kind):
    if kind == "row":
        return tuple(shape[:-2]) + (shape[-2] // N_DEV, shape[-1])
    return tuple(shape[2:])


def _pair_exchange(grads, kinds, *, name, collective_id):
    nt = len(grads)
    part = [_part_shape(g.shape, k) for g, k in zip(grads, kinds)]

    def body(*refs):
        g_refs, o_refs = refs[:nt], refs[nt:2 * nt]
        send_sems, recv_sems = refs[2 * nt:]
        x, y, c, _ = _mesh_place()
        _handshake([(x, y, 1 - c)])
        copies = []
        for t in range(nt):
            for xy in range(4):
                src = g_refs[t].at[1 - c, xy] if kinds[t] == "stack" else _window(g_refs[t], kinds[t], 2 * xy + 1 - c, part[t])
                copies.append(pltpu.make_async_remote_copy(
                    src_ref=src, dst_ref=o_refs[t].at[xy], send_sem=send_sems.at[xy, t], recv_sem=recv_sems.at[xy, t],
                    device_id=(x, y, 1 - c), device_id_type=MESH))
        for cp in copies:
            cp.start()
        for cp in copies:
            cp.wait()

    return _sequencer_call(
        body, grads, [SDS((4,) + p, g.dtype) for p, g in zip(part, grads)],
        [pltpu.SemaphoreType.DMA((4, nt)), pltpu.SemaphoreType.DMA((4, nt))], name=name, collective_id=collective_id)


def _chip_exchange(parts, small, *, name, collective_id):
    nt = len(parts)
    if small is None:
        def body_plain(*refs):
            s_refs, o_refs = refs[:nt], refs[nt:2 * nt]
            send_sems, recv_sems = refs[2 * nt:]
            x, y, c, chips = _mesh_place()
            _handshake([(*chip, c) for chip in chips])
            copies = [pltpu.make_async_remote_copy(
                src_ref=s_refs[t].at[2 * chip[0] + chip[1]], dst_ref=o_refs[t].at[j],
                send_sem=send_sems.at[j, t], recv_sem=recv_sems.at[j, t], device_id=(*chip, c), device_id_type=MESH)
                for j, chip in enumerate(chips) for t in range(nt)]
            for cp in copies:
                cp.start()
            for cp in copies:
                cp.wait()

        return _sequencer_call(
            body_plain, parts, [SDS((3,) + s.shape[1:], s.dtype) for s in parts],
            [pltpu.SemaphoreType.DMA((3, nt)), pltpu.SemaphoreType.DMA((3, nt))], name=name, collective_id=collective_id)

    def body(*refs):
        s_refs, small_ref = refs[:nt], refs[nt]
        o_refs, small_all = refs[nt + 1:2 * nt + 1], refs[2 * nt + 1]
        send_sems, recv_sems, small_send, small_recv, local_sem = refs[2 * nt + 2:]
        x, y, c, chips = _mesh_place()
        _handshake([(px, py, pc) for px in (x, 1 - x) for py in (y, 1 - y) for pc in (c, 1 - c)][1:])

        def copy(j, t, chip):
            return pltpu.make_async_remote_copy(
                src_ref=s_refs[t].at[2 * chip[0] + chip[1]], dst_ref=o_refs[t].at[j],
                send_sem=send_sems.at[j, t], recv_sem=recv_sems.at[j, t], device_id=(*chip, c), device_id_type=MESH)

        flips = [(fx, fy, fc) for fx in (0, 1) for fy in (0, 1) for fc in (0, 1)][1:]

        def small_copy(k):
            fx, fy, fc = flips[k]
            to = (x ^ fx if fx else x, y ^ fy if fy else y, c ^ fc if fc else c)
            rows = small_all.at[4 * x + 2 * y + c]
            return pltpu.make_async_remote_copy(
                src_ref=small_ref, dst_ref=rows, send_sem=small_send.at[k], recv_sem=small_recv.at[k],
                device_id=to, device_id_type=MESH)

        own = pltpu.make_async_copy(small_ref, small_all.at[4 * x + 2 * y + c], local_sem)
        own.start()
        copies = [copy(j, t, chip) for j, chip in enumerate(chips) for t in range(nt)]
        smalls = [small_copy(k) for k in range(7)]
        for cp in smalls + copies:
            cp.start()
        for cp in smalls + copies:
            cp.wait()
        own.wait()

    return _sequencer_call(
        body, list(parts) + [small],
        [SDS((3,) + s.shape[1:], s.dtype) for s in parts] + [SDS((N_DEV,) + small.shape, small.dtype)],
        [pltpu.SemaphoreType.DMA((3, nt)), pltpu.SemaphoreType.DMA((3, nt)),
         pltpu.SemaphoreType.DMA((7,)), pltpu.SemaphoreType.DMA((7,)), pltpu.SemaphoreType.DMA],
        name=name, collective_id=collective_id)


def _pair_sum(grad, theirs, kind, c, *, name):
    if kind == "row":
        r, l = theirs.shape[-2:]
        n = theirs.size // (4 * r * l)
        grad, theirs = grad.reshape(n, N_DEV * r, l), theirs.reshape(4, n, r, l)
        mine_spec = pl.BlockSpec((n, r, l), lambda xy, c_ref: (0, 2 * xy + c_ref[0], 0))
    else:
        r, l = theirs.shape[-2:]
        n = theirs.size // (4 * r * l)
        theirs = theirs.reshape(4, n, r, l)
        grad = grad.reshape(2, 4, n, r, l)
        mine_spec = pl.BlockSpec((None, None, n, r, l), lambda xy, c_ref: (c_ref[0], xy, 0, 0, 0))

    def body(c_ref, a_ref, b_ref, o_ref):
        o_ref[...] = (a_ref[...].astype(F32) + b_ref[...].astype(F32)).astype(BF16)

    part = pl.BlockSpec((None, n, r, l), lambda xy, c_ref: (xy, 0, 0, 0))
    return pl.pallas_call(
        body, name=name,
        grid_spec=pltpu.PrefetchScalarGridSpec(num_scalar_prefetch=1, grid=(4,), in_specs=[mine_spec, part], out_specs=part),
        out_shape=SDS((4, n, r, l), BF16), compiler_params=_params("parallel"),
    )(c, grad, theirs)


def _adamw(w, g, m, v):
    m = ADAM_B1 * m + (1.0 - ADAM_B1) * g
    v = ADAM_B2 * v + (1.0 - ADAM_B2) * jnp.square(g)
    m_hat = m / (1.0 - ADAM_B1 ** ADAM_STEP)
    v_hat = v / (1.0 - ADAM_B2 ** ADAM_STEP)
    delta = -ADAM_LR * (m_hat / (jnp.sqrt(v_hat) + ADAM_EPS) + ADAM_WD * w)
    return delta, m, v


def _adam_big(owns, others, mat, xy, w, m, v, *, tr, name):
    _, r, l = w.shape
    lp = owns[0].shape[-1]
    nq = len(owns)
    rows = [tr] if nq == 1 else [o.shape[-2] for o in owns]
    assert sum(rows) == tr and r % tr == 0, (r, tr, rows)

    def body(xy_ref, *refs):
        own_refs, oth_refs = refs[:nq], refs[nq:2 * nq]
        w_ref, m_ref, v_ref, g_out, d_out, m_out, v_out = refs[2 * nq:]
        parts = []
        for q in range(nq):
            gq = own_refs[q][0, 0].astype(F32)
            for j in range(3):
                gq = gq + oth_refs[q][j, 0].astype(F32)
            parts.append(gq)
        g = (parts[0] if nq == 1 else jnp.concatenate(parts, axis=0))[:, :l]
        delta, m_new, v_new = _adamw(w_ref[0], g, m_ref[0], v_ref[0])
        g_out[0] = g
        d_out[0] = delta
        m_out[0] = m_new
        v_out[0] = v_new

    blk = pl.BlockSpec((1, tr, l), lambda i, xy_ref: (0, i, 0))
    own_specs = [pl.BlockSpec((1, 1, n, lp), lambda i, xy_ref: (xy_ref[0], mat, i, 0)) for n in rows]
    oth_specs = [pl.BlockSpec((3, 1, n, lp), lambda i, xy_ref: (0, mat, i, 0)) for n in rows]
    return pl.pallas_call(
        body, name=name,
        grid_spec=pltpu.PrefetchScalarGridSpec(
            num_scalar_prefetch=1, grid=(r // tr,), in_specs=own_specs + oth_specs + [blk, blk, blk],
            out_specs=[blk, blk, blk, blk]),
        out_shape=[SDS(w.shape, F32)] * 4, compiler_params=_params("parallel"),
    )(xy, *owns, *others, w, m, v)


def _small_layout(shapes):
    out, at = [], 0
    for r, c in shapes:
        rows = c // 128 if (r == 1 and c > 128) else r
        out.append((at, rows))
        at += -(-rows // 8) * 8
    return out, at


def _pack_small(parts, *, name):
    shapes = [a.shape for a in parts]
    layout, total = _small_layout(shapes)

    def body(*refs):
        o_ref = refs[-1]
        o_ref[...] = jnp.zeros_like(o_ref)
        for x_ref, (r, c), (at, rows) in zip(refs, shapes, layout):
            if r == 1 and c > 128:
                for k in range(rows):
                    o_ref[at + k:at + k + 1, :] = x_ref[:, k * 128:(k + 1) * 128]
            else:
                o_ref[at:at + r, 0:c] = x_ref[...]

    return pl.pallas_call(body, name=name, out_shape=SDS((total, 128), F32))(*parts)


def _adam_small(g_all, ws, ms, vs, *, name):
    n = len(ws)
    shapes = [w.shape for w in ws]
    layout, _ = _small_layout(shapes)

    def body(g_ref, *refs):
        w_refs, m_refs, v_refs, outs = refs[:n], refs[n:2 * n], refs[2 * n:3 * n], refs[3 * n:]
        g_sum = g_ref[0]
        for k in range(1, N_DEV):
            g_sum = g_sum + g_ref[k]
        for i, ((r, c), (at, rows)) in enumerate(zip(shapes, layout)):
            if r == 1 and c > 128:
                g = jnp.concatenate([g_sum[at + k:at + k + 1, :] for k in range(rows)], axis=1)
            else:
                g = g_sum[at:at + r, 0:c]
            delta, m_new, v_new = _adamw(w_refs[i][...], g, m_refs[i][...], v_refs[i][...])
            for q, val in enumerate((g, delta, m_new, v_new)):
                outs[4 * i + q][...] = val

    flat = pl.pallas_call(body, name=name, out_shape=[SDS(s, F32) for s in shapes for _ in range(4)])(g_all, *ws, *ms, *vs)
    return [flat[4 * i:4 * i + 4] for i in range(n)]


def kernel(x, mem, ffn1_norm, ffn1_w_gate, ffn1_w_up, ffn1_w_down, mix_norm, mem_norm, w_in, w_mem_kv, swa_q_norm, swa_k_norm, swa_sinks, rel_bias, gla_w_gate_up, gla_b_gate, gla_out_norm, mem_q_norm, mem_k_norm, w_out, ffn2_norm, ffn2_w_gate, ffn2_w_up, ffn2_w_down, loss_target, m_ffn1_norm, m_ffn1_w_gate, m_ffn1_w_up, m_ffn1_w_down, m_mix_norm, m_mem_norm, m_w_in, m_w_mem_kv, m_swa_q_norm, m_swa_k_norm, m_swa_sinks, m_rel_bias, m_gla_w_gate_up, m_gla_b_gate, m_gla_out_norm, m_mem_q_norm, m_mem_k_norm, m_w_out, m_ffn2_norm, m_ffn2_w_gate, m_ffn2_w_up, m_ffn2_w_down, v_ffn1_norm, v_ffn1_w_gate, v_ffn1_w_up, v_ffn1_w_down, v_mix_norm, v_mem_norm, v_w_in, v_w_mem_kv, v_swa_q_norm, v_swa_k_norm, v_swa_sinks, v_rel_bias, v_gla_w_gate_up, v_gla_b_gate, v_gla_out_norm, v_mem_q_norm, v_mem_k_norm, v_w_out, v_ffn2_norm, v_ffn2_w_gate, v_ffn2_w_up, v_ffn2_w_down):
    xi, yi, ci = lax.axis_index("x"), lax.axis_index("y"), lax.axis_index("c")
    c_arr = jnp.reshape(ci, (1,)).astype(jnp.int32)
    xy_arr = jnp.reshape(2 * xi + yi, (1,)).astype(jnp.int32)
    d = x.shape[-1]

    def ffn_shards(wg_s, wu_s, wd_s):
        return jnp.concatenate([wg_s.transpose(0, 2, 1), wu_s.transpose(0, 2, 1), wd_s], axis=0).astype(BF16)

    def gather_ffn(wg_s, wu_s, wd_s, name, collective_id, after):
        w3_s, _ = lax.optimization_barrier((ffn_shards(wg_s, wu_s, wd_s), after))
        return _all_gather([w3_s], ["row"], name=name, collective_id=collective_id)[0]

    w3_s = ffn_shards(ffn1_w_gate, ffn1_w_up, ffn1_w_down)
    w3_1a = _all_gather([w3_s[:, :FFN1_FIRST]], ["row"], name="gather_ffn1a", collective_id=0)[0]
    w3_s, _ = lax.optimization_barrier((w3_s, w3_1a))
    w3_1b = _all_gather([w3_s[:, FFN1_FIRST:]], ["row"], name="gather_ffn1b", collective_id=11)[0]

    def gather_mix(after):
        mix_s = lax.optimization_barrier((w_in[0].astype(BF16), w_mem_kv[0].astype(BF16), w_out[0].astype(BF16),
                                          (w3_1b, after)))[:3]
        win_all, wkv, wout = _all_gather(list(mix_s), ["stack", "row", "row"], name="gather_mix", collective_id=1)
        return _pack_win(win_all, tr=256, name="pack_w_in"), wkv, wout

    def gather_ffn2(after):
        return gather_ffn(ffn2_w_gate, ffn2_w_up, ffn2_w_down, "gather_ffn2", 2, after)

    small_w = [ffn1_norm, mix_norm, mem_norm, ffn2_norm, swa_q_norm, swa_k_norm, swa_sinks[0], rel_bias,
               gla_w_gate_up[0], gla_b_gate, gla_out_norm, mem_q_norm, mem_k_norm]
    collective_ids = {"ffn2": (3, 4), "mix": (5, 6), "ffn1a": (7, 8), "ffn1b": (9, 10)}
    reduced, small_box = {}, {}

    def on_grads(group, grads, carry, small=None):
        if group == "mix":
            dwin_p, dwkv, dwout = grads
            grads = [_unpack_win(dwin_p, tr=256, name="unpack_dw_in"), dwkv, dwout]
            kinds = ["stack", "row", "row"]
        else:
            kinds = ["row"]
        if reduced:
            earlier = list(reduced.values())[-1][1]
            *grads, _ = lax.optimization_barrier((*grads, earlier[0]))
        id_pair, id_chip = collective_ids[group]
        from_sibling = _pair_exchange(grads, kinds, name=f"pair_exchange_{group}", collective_id=id_pair)
        chip_sums = [_pair_sum(g, theirs, k, c_arr, name=f"pair_sum_{group}_{t}")
                     for t, (g, theirs, k) in enumerate(zip(grads, from_sibling, kinds))]
        if carry is not None:
            *chip_sums, carry = lax.optimization_barrier((*chip_sums, carry))
        if small is None:
            from_chips = _chip_exchange(chip_sums, None, name=f"chip_exchange_{group}", collective_id=id_chip)
        else:
            packed = _pack_small(small, name=f"pack_small_{group}")
            *from_chips, small_all = _chip_exchange(chip_sums, packed, name=f"chip_exchange_{group}",
                                                    collective_id=id_chip)
            small_box[group] = small_all
        reduced[group] = (chip_sums, from_chips)
        return carry

    grad_x = _local_step(x[0], mem[0], loss_target[0], small_w, ((w3_1a, w3_1b), gather_mix, gather_ffn2), on_grads)

    big_w = {"ffn1_w_gate": ("ffn1", 0, 0, True, ffn1_w_gate, m_ffn1_w_gate, v_ffn1_w_gate),
             "ffn1_w_up": ("ffn1", 0, 1, True, ffn1_w_up, m_ffn1_w_up, v_ffn1_w_up),
             "ffn1_w_down": ("ffn1", 0, 2, False, ffn1_w_down, m_ffn1_w_down, v_ffn1_w_down),
             "w_in": ("mix", 0, 0, False, w_in, m_w_in, v_w_in),
             "w_mem_kv": ("mix", 1, 0, False, w_mem_kv, m_w_mem_kv, v_w_mem_kv),
             "w_out": ("mix", 2, 0, False, w_out, m_w_out, v_w_out),
             "ffn2_w_gate": ("ffn2", 0, 0, True, ffn2_w_gate, m_ffn2_w_gate, v_ffn2_w_gate),
             "ffn2_w_up": ("ffn2", 0, 1, True, ffn2_w_up, m_ffn2_w_up, v_ffn2_w_up),
             "ffn2_w_down": ("ffn2", 0, 2, False, ffn2_w_down, m_ffn2_w_down, v_ffn2_w_down)}
    res = {}
    for nm, (group, t, mat, transposed, w, m, v) in big_w.items():
        shape = w.shape
        if transposed:
            w, m, v = (a.transpose(0, 2, 1) for a in (w, m, v))
        r = w.shape[1]
        halves = ["ffn1a", "ffn1b"] if group == "ffn1" else [group]
        if len(halves) == 1:
            tr = 256 if r % 256 == 0 else r
        else:
            tr = r
        out = _adam_big([reduced[k][0][t] for k in halves], [reduced[k][1][t] for k in halves], mat, xy_arr, w, m, v,
                        tr=tr, name=f"adam_{nm}")
        if transposed:
            out = [a.reshape(1, -1, d).transpose(0, 2, 1) for a in out]
        res[nm] = [a.reshape(shape) for a in out]
    small_names = ["ffn1_norm", "mix_norm", "mem_norm", "ffn2_norm", "swa_q_norm", "swa_k_norm", "swa_sinks", "rel_bias",
                   "gla_w_gate_up", "gla_b_gate", "gla_out_norm", "mem_q_norm", "mem_k_norm"]
    small_m = [m_ffn1_norm, m_mix_norm, m_mem_norm, m_ffn2_norm, m_swa_q_norm, m_swa_k_norm, m_swa_sinks, m_rel_bias,
               m_gla_w_gate_up, m_gla_b_gate, m_gla_out_norm, m_mem_q_norm, m_mem_k_norm]
    small_v = [v_ffn1_norm, v_mix_norm, v_mem_norm, v_ffn2_norm, v_swa_q_norm, v_swa_k_norm, v_swa_sinks, v_rel_bias,
               v_gla_w_gate_up, v_gla_b_gate, v_gla_out_norm, v_mem_q_norm, v_mem_k_norm]
    small_full = [ffn1_norm, mix_norm, mem_norm, ffn2_norm, swa_q_norm, swa_k_norm, swa_sinks, rel_bias,
                  gla_w_gate_up, gla_b_gate, gla_out_norm, mem_q_norm, mem_k_norm]
    zero = jnp.zeros((1, 1), F32)
    two_d = lambda a: a.reshape(a.shape[-2:])
    for group, sel in (("ffn1a", slice(1, None)), ("ffn1b", slice(0, 1))):
        extra = [zero] if group == "ffn1a" else []
        ws, ms, vs = ([two_d(a) for a in arrs[sel]] + extra for arrs in (small_full, small_m, small_v))
        updated = _adam_small(small_box[group], ws, ms, vs, name=f"adam_small_{group}")
        for nm, full, out in zip(small_names[sel], small_full[sel], updated):
            res[nm] = [a.reshape(full.shape) for a in out]
        if extra:
            loss = updated[-1][0].reshape(())

    order = ["ffn1_norm", "ffn1_w_gate", "ffn1_w_up", "ffn1_w_down", "mix_norm", "mem_norm", "w_in", "w_mem_kv",
             "swa_q_norm", "swa_k_norm", "swa_sinks", "rel_bias", "gla_w_gate_up", "gla_b_gate", "gla_out_norm",
             "mem_q_norm", "mem_k_norm", "w_out", "ffn2_norm", "ffn2_w_gate", "ffn2_w_up", "ffn2_w_down"]
    outs = [loss, grad_x[None]]
    for q in range(4):
        outs += [res[nm][q] for nm in order]
    return tuple(outs)
```

```python
import functools
import math

import numpy as np
import jax
import jax.numpy as jnp
from jax import lax
from jax.experimental import pallas as pl
from jax.experimental.pallas import tpu as pltpu
from jax.experimental.pallas import tpu_sc as plsc

F32 = jnp.float32
BF16 = jnp.bfloat16
SDS = jax.ShapeDtypeStruct

EPS = 1e-6
HEAD_DIM = 64
SWA_HEADS = 8
SWA_KV_HEADS = 2
SWA_GROUP = SWA_HEADS // SWA_KV_HEADS
BLOCK = 128
N_BUCKETS = 32
MAX_DISTANCE = 128
GLA_HEADS = 4
GLA_DK = 32
GLA_DV = 64
GLA_RANK = 16
GLA_TAU = 16.0
GLA_CHUNK = 32
MEM_HEADS = 4
SWA_Q_W = SWA_HEADS * HEAD_DIM
SWA_KV_W = SWA_KV_HEADS * HEAD_DIM
GLA_QK_W = GLA_HEADS * GLA_DK
GLA_V_W = GLA_HEADS * GLA_DV
MEM_Q_W = MEM_HEADS * HEAD_DIM
IN_W = 1808
IN_W_PAD = 1920
COL_SQ, COL_SKV, COL_GQ, COL_GK, COL_GV, COL_GG, COL_MQ, COL_GLR = 0, 512, 768, 896, 1024, 1280, 1536, 1792

ADAM_LR = 0.001
ADAM_B1 = 0.9
ADAM_B2 = 0.999
ADAM_EPS = 1e-08
ADAM_WD = 0.01
ADAM_STEP = 10

N_DEV = 8
VMEM_LIMIT_BYTES = 56 * 1024 * 1024
MESH = pl.DeviceIdType.MESH


def _params(*sem):
    return pltpu.CompilerParams(dimension_semantics=sem or None, vmem_limit_bytes=VMEM_LIMIT_BYTES)


def _dot(a, b, ta, tb, precision=None):
    dims = (((0 if ta else 1,), (1 if tb else 0,)), ((), ()))
    return lax.dot_general(a, b, dims, preferred_element_type=F32, precision=precision)


def _mm_raw(a, b, ta=False, tb=False):
    return _dot(a.astype(BF16), b.astype(BF16), ta, tb)


def _mmf_raw(a, b, ta=False, tb=False):
    return _dot(a, b, ta, tb, lax.Precision.HIGHEST)


def _make_mm(raw):
    @functools.partial(jax.custom_vjp, nondiff_argnums=(2, 3))
    def mm(a, b, ta=False, tb=False):
        return raw(a, b, ta, tb)

    def fwd(a, b, ta, tb):
        return raw(a, b, ta, tb), (a, b)

    def bwd(ta, tb, res, g):
        a, b = res
        da = raw(b, g, tb, True) if ta else raw(g, b, False, not tb)
        db = raw(g, a, True, ta) if tb else raw(a, g, not ta, False)
        return da, db

    mm.defvjp(fwd, bwd)
    return mm


_mm = _make_mm(_mm_raw)
_mmf = _make_mm(_mmf_raw)


def _mm3(a, b, ta=False, tb=False):
    a_hi, b_hi = a.astype(BF16).astype(F32), b.astype(BF16).astype(F32)
    return _mm(a_hi, b_hi, ta, tb) + _mm(a_hi, b - b_hi, ta, tb) + _mm(a - a_hi, b_hi, ta, tb)


def _rms(x, g):
    return x * lax.rsqrt(jnp.mean(x * x, axis=-1, keepdims=True) + EPS) * g


def _silu_mul(g, u):
    return jax.nn.silu(g) * u


def _log_sigmoid(z):
    return jnp.minimum(z, 0.0) - jnp.log(1.0 + jnp.exp(-jnp.abs(z)))


def _matmul(a_list, b, *, ta=False, tb=False, tm, tn, b_blocks=None, res=None, scale=1.0, out_dtype=F32, name):
    if not isinstance(a_list, (list, tuple)):
        a_list = [a_list]
    n_a = len(a_list)
    m = a_list[0].shape[1] if ta else a_list[0].shape[0]
    ks = [a.shape[0] if ta else a.shape[1] for a in a_list]
    n = b.shape[0] if tb else b.shape[1]
    if b_blocks is None:
        assert n_a == 1
        b_blocks = [0]
    tm, tn = min(tm, m), min(tn, n)
    assert m % tm == 0 and n % tn == 0, (m, n, tm, tn)

    def body(*refs):
        a_refs, b_refs = refs[:n_a], refs[n_a:2 * n_a]
        r_ref = refs[2 * n_a] if res is not None else None
        o_ref = refs[-1]
        acc = _mm_raw(a_refs[0][...], b_refs[0][...], ta, tb)
        for k in range(1, n_a):
            acc = acc + _mm_raw(a_refs[k][...], b_refs[k][...], ta, tb)
        if scale != 1.0:
            acc = acc * scale
        if r_ref is not None:
            acc = r_ref[...] + acc
        o_ref[...] = acc.astype(out_dtype)

    in_specs = []
    for k in ks:
        in_specs.append(pl.BlockSpec((k, tm), lambda i, j: (0, i)) if ta else pl.BlockSpec((tm, k), lambda i, j: (i, 0)))
    for k, blk in zip(ks, b_blocks):
        if tb:
            in_specs.append(pl.BlockSpec((tn, k), functools.partial(lambda i, j, blk: (j, blk), blk=blk)))
        else:
            in_specs.append(pl.BlockSpec((k, tn), functools.partial(lambda i, j, blk: (blk, j), blk=blk)))
    args = list(a_list) + [b] * n_a
    if res is not None:
        in_specs.append(pl.BlockSpec((tm, tn), lambda i, j: (i, j)))
        args.append(res)
    return pl.pallas_call(
        body, name=name, grid=(m // tm, n // tn), in_specs=in_specs,
        out_specs=pl.BlockSpec((tm, tn), lambda i, j: (i, j)), out_shape=SDS((m, n), out_dtype),
        compiler_params=_params("parallel", "parallel"),
    )(*args)


def _win_pieces(w):
    glr_lo, glr_hi = COL_MQ, COL_MQ + GLA_RANK
    out = []
    for j in range(N_DEV):
        for lo, hi, shift in ((0, glr_lo, 0), (glr_lo, glr_hi, COL_GLR - glr_lo), (glr_hi, IN_W, COL_MQ - glr_hi)):
            s, e = max(j * w, lo), min((j + 1) * w, hi)
            if s < e:
                out.append((j, s - j * w, e - j * w, s + shift))
    return out


def _pack_win(win_all, *, tr, name):
    _, d, w = win_all.shape

    def body(i_ref, o_ref):
        for j, a, b, dst in _win_pieces(w):
            o_ref[:, dst:dst + b - a] = i_ref[j][:, a:b]
        o_ref[:, IN_W:] = jnp.zeros((tr, IN_W_PAD - IN_W), o_ref.dtype)

    return pl.pallas_call(
        body, name=name, grid=(d // tr,), in_specs=[pl.BlockSpec((N_DEV, tr, w), lambda i: (0, i, 0))],
        out_specs=pl.BlockSpec((tr, IN_W_PAD), lambda i: (i, 0)), out_shape=SDS((d, IN_W_PAD), win_all.dtype),
        compiler_params=_params("parallel"),
    )(win_all)


def _unpack_win(dwin_p, *, tr, name):
    d = dwin_p.shape[0]
    w = IN_W // N_DEV

    def body(i_ref, o_ref):
        for j, a, b, src in _win_pieces(w):
            o_ref[j % 2, j // 2, :, a:b] = i_ref[:, src:src + b - a]

    return pl.pallas_call(
        body, name=name, grid=(d // tr,), in_specs=[pl.BlockSpec((tr, IN_W_PAD), lambda i: (i, 0))],
        out_specs=pl.BlockSpec((2, 4, tr, w), lambda i: (0, 0, i, 0)), out_shape=SDS((2, 4, d, w), dwin_p.dtype),
        compiler_params=_params("parallel"),
    )(dwin_p)


def _rms_fwd(x, g, *, tm, name):
    s, d = x.shape

    def body(x_ref, g_ref, h_ref):
        h_ref[...] = _rms(x_ref[...], g_ref[...]).astype(BF16)

    return pl.pallas_call(
        body, name=name, grid=(s // tm,),
        in_specs=[pl.BlockSpec((tm, d), lambda i: (i, 0)), pl.BlockSpec((1, d), lambda i: (0, 0))],
        out_specs=pl.BlockSpec((tm, d), lambda i: (i, 0)), out_shape=SDS((s, d), BF16),
        compiler_params=_params("parallel"),
    )(x, g)


def _rms_bwd(x, g, dh, dres, *, tm, name):
    s, d = x.shape
    want_dx = dres is not None
    product = isinstance(dh, tuple)

    def body(*refs):
        n_dh = 2 if product else 1
        x_ref, g_ref = refs[:2]
        dh_refs, rest = refs[2:2 + n_dh], refs[2 + n_dh:]
        if want_dx:
            dres_ref, dx_ref, dxb_ref, dg_ref = rest
        else:
            dg_ref, = rest
        dh_tile = _mm_raw(dh_refs[0][...], dh_refs[1][...], False, True) if product else dh_refs[0][...]
        _, vjp = jax.vjp(_rms, x_ref[...], g_ref[...])
        dx, dg = vjp(dh_tile)
        if want_dx:
            dx = dres_ref[...] + dx
            dx_ref[...] = dx
            dxb_ref[...] = dx.astype(BF16)

        @pl.when(pl.program_id(0) == 0)
        def _():
            dg_ref[...] = jnp.zeros_like(dg_ref)

        dg_ref[...] += dg

    row = pl.BlockSpec((tm, d), lambda i: (i, 0))
    vec = pl.BlockSpec((1, d), lambda i: (0, 0))
    if product:
        k = dh[0].shape[1]
        dh_specs, dh_args = [pl.BlockSpec((tm, k), lambda i: (i, 0)), pl.BlockSpec((d, k), lambda i: (0, 0))], list(dh)
    else:
        dh_specs, dh_args = [row], [dh]
    if want_dx:
        return pl.pallas_call(
            body, name=name, grid=(s // tm,), in_specs=[row, vec] + dh_specs + [row], out_specs=[row, row, vec],
            out_shape=[SDS((s, d), F32), SDS((s, d), BF16), SDS((1, d), F32)], compiler_params=_params("arbitrary"),
        )(x, g, *dh_args, dres)
    return None, None, pl.pallas_call(
        body, name=name, grid=(s // tm,), in_specs=[row, vec] + dh_specs, out_specs=vec,
        out_shape=SDS((1, d), F32), compiler_params=_params("arbitrary"),
    )(x, g, *dh_args)


FFN_TN = 256
FFN_TN_FWD = 512
FFN1_FIRST = 192


def _ffn_fwd(x, gain, w3, tag, *, tm=1024, next_gain=None, target=None, start=None, partial=False):
    s, d = x.shape
    f = w3.shape[1]
    tn = FFN_TN_FWD if f % FFN_TN_FWD == 0 else FFN_TN
    nj = f // tn
    tm = min(tm, s)
    n_extra = (next_gain is not None) + (target is not None) + 2 * (start is not None)

    def body(*refs):
        x_ref, gain_ref, wg_ref, wu_ref, wd_ref = refs[:5]
        extra, outs = refs[5:5 + n_extra], refs[5 + n_extra:-1]
        acc_s = refs[-1]
        g_ref, u_ref = outs[-2:]
        h_ref = extra[-2] if start is not None else outs[-3]
        i, j = pl.program_id(0), pl.program_id(1)

        @pl.when(j == 0)
        def _():
            if start is None:
                h_ref[...] = _rms(x_ref[...], gain_ref[...]).astype(BF16)
                acc_s[...] = jnp.zeros_like(acc_s)
            else:
                acc_s[...] = extra[-1][...]

        hv = h_ref[...]
        g = _mm_raw(hv, wg_ref[...], False, True)
        u = _mm_raw(hv, wu_ref[...], False, True)
        g_ref[...] = g.astype(BF16)
        u_ref[...] = u.astype(BF16)
        acc_s[...] += _mm_raw(_silu_mul(g, u), wd_ref[...])

        @pl.when(j == nj - 1)
        def _():
            y = acc_s[...] if partial else x_ref[...] + 0.5 * acc_s[...]
            if target is None:
                outs[0][...] = y
                if next_gain is not None:
                    outs[1][...] = _rms(y, extra[0][...]).astype(BF16)
            else:
                dy_ref, dyb_ref, loss_ref = outs[:3]
                diff = y - extra[0][...]
                dy_ref[...] = diff * (1.0 / d)
                dyb_ref[...] = (diff * (1.0 / d)).astype(BF16)
                part = 0.5 * jnp.sum(jnp.mean(diff * diff, axis=-1, keepdims=True), axis=0, keepdims=True)

                @pl.when(i == 0)
                def _():
                    loss_ref[...] = part

                @pl.when(i > 0)
                def _():
                    loss_ref[...] += part

    row = pl.BlockSpec((tm, d), lambda i, j: (i, 0))
    vec = pl.BlockSpec((1, d), lambda i, j: (0, 0))
    tile = pl.BlockSpec((tm, tn), lambda i, j: (i, j))
    in_specs = [row, vec] + [pl.BlockSpec((None, tn, d), functools.partial(lambda i, j, k: (k, j, 0), k=k)) for k in range(3)]
    args = [x, gain, w3, w3, w3]
    if target is None:
        out_specs, out_shape = [row], [SDS((s, d), F32)]
        if next_gain is not None:
            in_specs.append(vec)
            args.append(next_gain)
            out_specs.append(row)
            out_shape.append(SDS((s, d), BF16))
    else:
        in_specs.append(row)
        args.append(target)
        out_specs = [row, row, pl.BlockSpec((1, 1), lambda i, j: (0, 0))]
        out_shape = [SDS((s, d), F32), SDS((s, d), BF16), SDS((1, 1), F32)]
    if start is None:
        out_specs.append(row)
        out_shape.append(SDS((s, d), BF16))
    else:
        in_specs += [row, row]
        args += list(start)
    *head, g, u = pl.pallas_call(
        body, name=f"{tag}_fwd", grid=(s // tm, nj), in_specs=in_specs,
        out_specs=out_specs + [tile, tile],
        out_shape=out_shape + [SDS((s, f), BF16), SDS((s, f), BF16)],
        scratch_shapes=[pltpu.VMEM((tm, d), F32)],
        compiler_params=_params("arbitrary", "arbitrary"),
    )(*args)
    if start is None:
        *head, h = head
    else:
        h = start[0]
    return head, (h, g, u)


def _ffn_bwd_part(dyb, w3, saved, first, count, dh_init, *, name):
    h, g, u = saved
    s, d = h.shape
    tn = FFN_TN

    def body(*refs):
        if dh_init is None:
            dy_ref, h_ref, wd_ref, wg_ref, wu_ref, g_ref, u_ref, dh_ref, dw3_ref, dg_s, du_s, a_s = refs
        else:
            dy_ref, h_ref, wd_ref, wg_ref, wu_ref, g_ref, u_ref, dh0_ref, dh_ref, dw3_ref, dg_s, du_s, a_s = refs
        j = pl.program_id(0)

        @pl.when(j == 0)
        def _():
            dh_ref[...] = jnp.zeros_like(dh_ref) if dh_init is None else dh0_ref[...]
            for ref in (dg_s, du_s, a_s):
                ref[...] = jnp.zeros_like(ref)

        now, before = j % 2, 1 - j % 2
        dyv = dy_ref[...]
        hv = h_ref[...]
        dg, du, a = dg_s[before], du_s[before], a_s[before]
        dh_ref[...] += _mm_raw(dg, wg_ref[...]) + _mm_raw(du, wu_ref[...])
        dw3_ref[0] = _mm_raw(dg, hv, True, False).astype(BF16)
        dw3_ref[1] = _mm_raw(du, hv, True, False).astype(BF16)
        dw3_ref[2] = (_mm_raw(a, dyv, True, False) * 0.5).astype(BF16)

        da = _mm_raw(dyv, wd_ref[...], False, True) * 0.5
        a, vjp = jax.vjp(_silu_mul, g_ref[...].astype(F32), u_ref[...].astype(F32))
        dg, du = vjp(da)
        dg_s[now] = dg.astype(BF16)
        du_s[now] = du.astype(BF16)
        a_s[now] = a.astype(BF16)

    this = lambda j: first + jnp.minimum(j, count - 1)
    last = lambda j: first + jnp.maximum(j - 1, 0)
    full = pl.BlockSpec((s, d), lambda j: (0, 0))
    once = pl.BlockSpec((s, d), lambda j: (0, 0), pipeline_mode=pl.Buffered(1))
    tile = pl.BlockSpec((s, tn), lambda j: (0, this(j)))
    in_specs = [once, once, pl.BlockSpec((None, tn, d), lambda j: (2, this(j), 0)),
                pl.BlockSpec((None, tn, d), lambda j: (0, last(j), 0)), pl.BlockSpec((None, tn, d), lambda j: (1, last(j), 0)),
                tile, tile]
    args = [dyb, h, w3, w3, w3, g, u]
    if dh_init is not None:
        in_specs.append(once)
        args.append(dh_init)
    return pl.pallas_call(
        body, name=name, grid=(count + 1,), in_specs=in_specs,
        out_specs=[full, pl.BlockSpec((3, tn, d), lambda j: (0, jnp.maximum(j - 1, 0), 0))],
        out_shape=[SDS((s, d), F32), SDS((3, count * tn, d), BF16)],
        scratch_shapes=[pltpu.VMEM((2, s, tn), BF16)] * 3,
        compiler_params=_params("arbitrary"),
    )(*args)


def _bucket_table():
    qi = np.arange(BLOCK)[:, None]
    kj = np.arange(2 * BLOCK)[None, :]
    dist = np.maximum(qi + BLOCK - kj, 0)
    max_exact = N_BUCKETS // 2
    d = np.maximum(dist, 1).astype(np.float32)
    large = max_exact + (np.log(d / np.float32(max_exact)) / np.float32(math.log(MAX_DISTANCE / max_exact))
                         * np.float32(N_BUCKETS - max_exact)).astype(np.int32)
    large = np.minimum(large, N_BUCKETS - 1)
    band = np.where(dist < max_exact, dist, large).astype(np.int32)
    return np.where(np.tril(np.ones((BLOCK, BLOCK), bool)), band[:, BLOCK:], band[:, :BLOCK])


SWA_STACK = SWA_GROUP * BLOCK


def _swa_masks(n):
    qi = lax.broadcasted_iota(jnp.int32, (SWA_STACK, BLOCK), 0) % BLOCK
    kj = lax.broadcasted_iota(jnp.int32, (SWA_STACK, BLOCK), 1)
    own = kj <= qi
    return own, own | (n > 0)


def _swa_group(q, kp, kc, vp, vc, qg, kg, sink, bias, own, valid):
    qn = _rms(q, qg)
    s = jnp.where(own, _mm(qn, _rms(kc, kg), False, True), _mm(qn, _rms(kp, kg), False, True))
    s = s * (HEAD_DIM ** -0.5) + bias
    s = jnp.where(valid, s, -jnp.inf)
    m = lax.stop_gradient(jnp.maximum(jnp.max(s, axis=-1, keepdims=True), sink))
    p = jnp.exp(s - m)
    p = p / (jnp.sum(p, axis=-1, keepdims=True) + jnp.exp(sink - m))
    return _mm(jnp.where(own, p, 0.0), vc) + _mm(jnp.where(own, 0.0, p), vp)


def _swa_bias_table(rb_ref, bucket, bias_s):
    for h in range(SWA_HEADS):
        acc = jnp.zeros((BLOCK, BLOCK), F32)
        for b in range(N_BUCKETS):
            acc = jnp.where(bucket == b, rb_ref[b, h], acc)
        bias_s[h // SWA_GROUP, (h % SWA_GROUP) * BLOCK:(h % SWA_GROUP + 1) * BLOCK, :] = acc


def _swa_stack(ref, g):
    return jnp.concatenate([ref[:, (g * SWA_GROUP + hh) * HEAD_DIM:(g * SWA_GROUP + hh + 1) * HEAD_DIM]
                            for hh in range(SWA_GROUP)], axis=0)


def _swa_unstack(ref, g, stacked):
    for hh in range(SWA_GROUP):
        h = g * SWA_GROUP + hh
        ref[:, h * HEAD_DIM:(h + 1) * HEAD_DIM] = stacked[hh * BLOCK:(hh + 1) * BLOCK]


def _swa_sink_column(sink_ref, g):
    head = lax.broadcasted_iota(jnp.int32, (SWA_STACK, 1), 0) // BLOCK
    col = jnp.zeros((SWA_STACK, 1), F32)
    for hh in range(SWA_GROUP):
        col = jnp.where(head == hh, sink_ref[g * SWA_GROUP + hh], col)
    return col


def _swa_band(kvp_ref, kvc_ref, g):
    k = slice(g * HEAD_DIM, (g + 1) * HEAD_DIM)
    v = slice(SWA_KV_W + g * HEAD_DIM, SWA_KV_W + (g + 1) * HEAD_DIM)
    return kvp_ref[:, k], kvc_ref[:, k], kvp_ref[:, v], kvc_ref[:, v]


def _swa_specs(order):
    kvc = COL_SKV // (2 * SWA_KV_W)
    return [
        pl.BlockSpec((BLOCK, SWA_Q_W), lambda t: (order(t), 0)),
        pl.BlockSpec((BLOCK, 2 * SWA_KV_W), lambda t: (jnp.maximum(order(t) - 1, 0), kvc)),
        pl.BlockSpec((BLOCK, 2 * SWA_KV_W), lambda t: (order(t), kvc)),
        pl.BlockSpec((1, HEAD_DIM), lambda t: (0, 0)),
        pl.BlockSpec((1, HEAD_DIM), lambda t: (0, 0)),
        pl.BlockSpec(memory_space=pltpu.SMEM),
        pl.BlockSpec(memory_space=pltpu.SMEM),
        pl.BlockSpec((BLOCK, BLOCK), lambda t: (0, 0)),
    ]


def _swa_fwd(p, qg, kg, sinks, rel_bias, *, name):
    s = p.shape[0]
    nb = s // BLOCK

    def body(q_ref, kvp_ref, kvc_ref, qg_ref, kg_ref, sink_ref, rb_ref, bucket_ref, y_ref, bias_s):
        n = pl.program_id(0)

        @pl.when(n == 0)
        def _():
            _swa_bias_table(rb_ref, bucket_ref[...], bias_s)

        own, valid = _swa_masks(n)
        for g in range(SWA_KV_HEADS):
            out = _swa_group(_swa_stack(q_ref, g), *_swa_band(kvp_ref, kvc_ref, g), qg_ref[...], kg_ref[...],
                             _swa_sink_column(sink_ref, g), bias_s[g], own, valid)
            _swa_unstack(y_ref, g, out)

    return pl.pallas_call(
        body, name=name, grid=(nb,), in_specs=_swa_specs(lambda t: t),
        out_specs=pl.BlockSpec((BLOCK, SWA_Q_W), lambda t: (t, 0)), out_shape=SDS((s, SWA_Q_W), F32),
        scratch_shapes=[pltpu.VMEM((SWA_KV_HEADS, SWA_STACK, BLOCK), F32)],
        compiler_params=_params("arbitrary"),
    )(p, p, p, qg, kg, sinks, rel_bias, jnp.asarray(_bucket_table()))


def _swa_bwd(p, qg, kg, sinks, rel_bias, dy_all, *, name):
    s = p.shape[0]
    nb = s // BLOCK

    def body(q_ref, kvp_ref, kvc_ref, qg_ref, kg_ref, sink_ref, rb_ref, bucket_ref, dy_ref,
             dq_ref, dkv_ref, dqg_ref, dkg_ref, dsink_ref, drb_ref, bias_s, dbias_s, carry_s):
        t = pl.program_id(0)
        n = nb - 1 - t

        @pl.when(t == 0)
        def _():
            _swa_bias_table(rb_ref, bucket_ref[...], bias_s)
            dbias_s[...] = jnp.zeros_like(dbias_s)
            carry_s[...] = jnp.zeros_like(carry_s)
            dqg_ref[...] = jnp.zeros_like(dqg_ref)
            dkg_ref[...] = jnp.zeros_like(dkg_ref)
            dsink_ref[...] = jnp.zeros_like(dsink_ref)
            drb_ref[...] = jnp.zeros_like(drb_ref)

        own, valid = _swa_masks(n)
        lane = lax.broadcasted_iota(jnp.int32, (1, BLOCK), 1)
        dqg = jnp.zeros((1, HEAD_DIM), F32)
        dkg = jnp.zeros((1, HEAD_DIM), F32)
        dsink_vec = jnp.zeros((1, BLOCK), F32)
        for g in range(SWA_KV_HEADS):
            _, vjp = jax.vjp(functools.partial(_swa_group, own=own, valid=valid), _swa_stack(q_ref, g),
                             *_swa_band(kvp_ref, kvc_ref, g), qg_ref[...], kg_ref[...], _swa_sink_column(sink_ref, g),
                             bias_s[g])
            dq, dkp, dkc, dvp, dvc, dqg_g, dkg_g, dsink_col, dbias = vjp(_swa_stack(dy_ref, g))
            _swa_unstack(dq_ref, g, dq)
            dqg += dqg_g
            dkg += dkg_g
            dbias_s[g] += dbias
            for hh in range(SWA_GROUP):
                dsink_h = jnp.sum(dsink_col[hh * BLOCK:(hh + 1) * BLOCK], axis=0, keepdims=True)
                dsink_vec += jnp.where(lane == g * SWA_GROUP + hh, dsink_h, 0.0)
            lo = g * HEAD_DIM
            dkv_ref[:, lo:lo + HEAD_DIM] = dkc + carry_s[g]
            carry_s[g] = dkp
            lo += SWA_KV_W
            dkv_ref[:, lo:lo + HEAD_DIM] = dvc + carry_s[SWA_KV_HEADS + g]
            carry_s[SWA_KV_HEADS + g] = dvp
        dqg_ref[...] += dqg
        dkg_ref[...] += dkg
        dsink_ref[...] += dsink_vec

        @pl.when(t == nb - 1)
        def _():
            bucket = bucket_ref[...]
            row = lax.broadcasted_iota(jnp.int32, (N_BUCKETS, BLOCK), 0)
            col = lax.broadcasted_iota(jnp.int32, (N_BUCKETS, BLOCK), 1)
            acc = jnp.zeros((N_BUCKETS, BLOCK), F32)
            for h in range(SWA_HEADS):
                dbias = dbias_s[h // SWA_GROUP, (h % SWA_GROUP) * BLOCK:(h % SWA_GROUP + 1) * BLOCK, :]
                for b in range(N_BUCKETS):
                    part = jnp.sum(jnp.where(bucket == b, dbias, 0.0), axis=1, keepdims=True)
                    val = jnp.sum(part, axis=0, keepdims=True)
                    acc = acc + jnp.where((row == b) & (col == h), val, 0.0)
            drb_ref[...] = acc

    order = lambda t: nb - 1 - t
    vec = pl.BlockSpec((1, HEAD_DIM), lambda t: (0, 0))
    return pl.pallas_call(
        body, name=name, grid=(nb,),
        in_specs=_swa_specs(order) + [pl.BlockSpec((BLOCK, SWA_Q_W), lambda t: (order(t), 0))],
        out_specs=[pl.BlockSpec((BLOCK, SWA_Q_W), lambda t: (order(t), 0)),
                   pl.BlockSpec((BLOCK, 2 * SWA_KV_W), lambda t: (order(t), 0)),
                   vec, vec, pl.BlockSpec((1, BLOCK), lambda t: (0, 0)),
                   pl.BlockSpec((N_BUCKETS, BLOCK), lambda t: (0, 0))],
        out_shape=[SDS((s, SWA_Q_W), F32), SDS((s, 2 * SWA_KV_W), F32), SDS((1, HEAD_DIM), F32),
                   SDS((1, HEAD_DIM), F32), SDS((1, BLOCK), F32), SDS((N_BUCKETS, BLOCK), F32)],
        scratch_shapes=[pltpu.VMEM((SWA_KV_HEADS, SWA_STACK, BLOCK), F32),
                        pltpu.VMEM((SWA_KV_HEADS, SWA_STACK, BLOCK), F32),
                        pltpu.VMEM((2 * SWA_KV_HEADS, BLOCK, HEAD_DIM), F32)],
        compiler_params=_params("arbitrary"),
    )(p, p, p, qg, kg, sinks, rel_bias, jnp.asarray(_bucket_table()), dy_all)


def _mem_head(q, k, v, qg, kg):
    qn = _rms(q, qg)
    kn = _rms(k, kg)
    s = _mm(qn, kn, False, True) * (HEAD_DIM ** -0.5)
    m = lax.stop_gradient(jnp.max(s, axis=-1, keepdims=True))
    e = jnp.exp(s - m)
    return _mm(e / jnp.sum(e, axis=-1, keepdims=True), v)


def _mem_fwd(p, kv, qg, kg, *, tq, name):
    s = p.shape[0]
    m = kv.shape[0]

    def body(q_ref, kv_ref, qg_ref, kg_ref, y_ref):
        for h in range(MEM_HEADS):
            cols = slice(h * HEAD_DIM, (h + 1) * HEAD_DIM)
            vcols = slice(MEM_Q_W + h * HEAD_DIM, MEM_Q_W + (h + 1) * HEAD_DIM)
            y_ref[:, cols] = _mem_head(q_ref[:, cols], kv_ref[:, cols], kv_ref[:, vcols], qg_ref[...], kg_ref[...])

    vec = pl.BlockSpec((1, HEAD_DIM), lambda t: (0, 0))
    return pl.pallas_call(
        body, name=name, grid=(s // tq,),
        in_specs=[pl.BlockSpec((tq, MEM_Q_W), lambda t: (t, COL_MQ // MEM_Q_W)),
                  pl.BlockSpec((m, 2 * MEM_Q_W), lambda t: (0, 0)), vec, vec],
        out_specs=pl.BlockSpec((tq, MEM_Q_W), lambda t: (t, 0)), out_shape=SDS((s, MEM_Q_W), F32),
        compiler_params=_params("parallel"),
    )(p, kv, qg, kg)


def _mem_bwd(p, kv, qg, kg, dy_all, *, tq, name):
    s = p.shape[0]
    m = kv.shape[0]

    def body(q_ref, kv_ref, qg_ref, kg_ref, dy_ref, dq_ref, dkv_ref, dqg_ref, dkg_ref):
        @pl.when(pl.program_id(0) == 0)
        def _():
            dkv_ref[...] = jnp.zeros_like(dkv_ref)
            dqg_ref[...] = jnp.zeros_like(dqg_ref)
            dkg_ref[...] = jnp.zeros_like(dkg_ref)

        dqg = jnp.zeros((1, HEAD_DIM), F32)
        dkg = jnp.zeros((1, HEAD_DIM), F32)
        for h in range(MEM_HEADS):
            cols = slice(h * HEAD_DIM, (h + 1) * HEAD_DIM)
            vcols = slice(MEM_Q_W + h * HEAD_DIM, MEM_Q_W + (h + 1) * HEAD_DIM)
            _, vjp = jax.vjp(_mem_head, q_ref[:, cols], kv_ref[:, cols], kv_ref[:, vcols], qg_ref[...], kg_ref[...])
            dq, dk, dv, dqg_h, dkg_h = vjp(dy_ref[:, cols])
            dq_ref[:, cols] = dq
            dkv_ref[:, cols] += dk
            dkv_ref[:, vcols] += dv
            dqg += dqg_h
            dkg += dkg_h
        dqg_ref[...] += dqg
        dkg_ref[...] += dkg

    vec = pl.BlockSpec((1, HEAD_DIM), lambda t: (0, 0))
    full = pl.BlockSpec((m, 2 * MEM_Q_W), lambda t: (0, 0))
    dy_col = (SWA_Q_W + GLA_V_W) // MEM_Q_W
    return pl.pallas_call(
        body, name=name, grid=(s // tq,),
        in_specs=[pl.BlockSpec((tq, MEM_Q_W), lambda t: (t, COL_MQ // MEM_Q_W)), full, vec, vec,
                  pl.BlockSpec((tq, MEM_Q_W), lambda t: (t, dy_col))],
        out_specs=[pl.BlockSpec((tq, MEM_Q_W), lambda t: (t, 0)), full, vec, vec],
        out_shape=[SDS((s, MEM_Q_W), F32), SDS((m, 2 * MEM_Q_W), F32), SDS((1, HEAD_DIM), F32), SDS((1, HEAD_DIM), F32)],
        compiler_params=_params("arbitrary"),
    )(p, kv, qg, kg, dy_all)


GLA_ROWS = 256


GLA_GROUP = 4


def _gla_consts():
    c, h, r = GLA_CHUNK, GLA_HEADS, GLA_GROUP * GLA_CHUNK
    i2 = lax.broadcasted_iota(jnp.int32, (c, c), 0)
    j2 = lax.broadcasted_iota(jnp.int32, (c, c), 1)
    slab_q = lax.broadcasted_iota(jnp.int32, (h, r, GLA_QK_W), 0)
    lane_q = lax.broadcasted_iota(jnp.int32, (h, r, GLA_QK_W), 2)
    row_a = lax.broadcasted_iota(jnp.int32, (h * r, r), 0) % r
    col_a = lax.broadcasted_iota(jnp.int32, (h * r, r), 1)
    slab_o = lax.broadcasted_iota(jnp.int32, (h, r, GLA_V_W), 0)
    lane_o = lax.broadcasted_iota(jnp.int32, (h, r, GLA_V_W), 2)
    row_s = lax.broadcasted_iota(jnp.int32, (GLA_V_W, GLA_QK_W), 0)
    col_s = lax.broadcasted_iota(jnp.int32, (GLA_V_W, GLA_QK_W), 1)
    return dict(
        ltri=(j2 <= i2).astype(F32),
        m_q=(slab_q == lane_q // GLA_DK).astype(F32),
        causal=(col_a <= row_a) & (col_a // c == row_a // c),
        m_o=(slab_o == lane_o // GLA_DV).astype(F32),
        m_s=(row_s // GLA_DV == col_s // GLA_DK).astype(F32),
    )


def _gla_step(q, k, v, z, bg, st, c):
    h = GLA_HEADS
    kt, ka, qt, qe, decay = [], [], [], [], []
    for qc, kc, zc in zip(q, k, z):
        la = _log_sigmoid(zc + bg) * (1.0 / GLA_TAU)
        b = _mmf(c["ltri"], la)
        bl = jnp.sum(la, axis=0, keepdims=True)
        qs = qc * (GLA_DK ** -0.5)
        kt.append(kc * jnp.exp(bl - b))
        ka.append(kc * jnp.exp(0.5 * bl - b))
        qt.append(qs * jnp.exp(b - 0.5 * bl))
        qe.append(qs * jnp.exp(b))
        decay.append(jnp.exp(bl))
    o_intra = []
    rows = GLA_GROUP * GLA_CHUNK
    for lo in range(0, len(q), GLA_GROUP):
        qt_all, kt_all, v_all = (jnp.concatenate(parts[lo:lo + GLA_GROUP], axis=0) for parts in (qt, ka, v))
        q_stack = (jnp.broadcast_to(qt_all[None], (h, rows, GLA_QK_W)) * c["m_q"]).reshape(h * rows, GLA_QK_W)
        a = jnp.where(c["causal"], _mm3(q_stack, kt_all, False, True), 0.0)
        o_stack = _mm(a, v_all)
        o_intra.append(jnp.sum(o_stack.reshape(h, rows, GLA_V_W) * c["m_o"], axis=0))
    o_intra = jnp.concatenate(o_intra, axis=0)
    o_inter = []
    for qec, ktc, vc, dc in zip(qe, kt, v, decay):
        o_inter.append(_mm(qec, st, False, True))
        st = st * dc + _mm(vc, ktc, True, False) * c["m_s"]
    return o_intra + jnp.concatenate(o_inter, axis=0), st


def _gla_post(o, gg, gain, g64):
    ms = _mmf(o * o, g64) * (1.0 / GLA_DV)
    return o * lax.rsqrt(ms + EPS) * gain * jax.nn.silu(gg)


def _gla_g64():
    r = lax.broadcasted_iota(jnp.int32, (GLA_V_W, GLA_V_W), 0)
    c = lax.broadcasted_iota(jnp.int32, (GLA_V_W, GLA_V_W), 1)
    return (r // GLA_DV == c // GLA_DV).astype(F32)


def _gla_in_specs(order):
    r = GLA_ROWS
    return [
        pl.BlockSpec((r, GLA_QK_W), lambda t: (order(t), COL_GQ // GLA_QK_W)),
        pl.BlockSpec((r, GLA_QK_W), lambda t: (order(t), COL_GK // GLA_QK_W)),
        pl.BlockSpec((r, GLA_V_W), lambda t: (order(t), COL_GV // GLA_V_W)),
        pl.BlockSpec((r, GLA_V_W), lambda t: (order(t), COL_GG // GLA_V_W)),
        pl.BlockSpec((r, GLA_QK_W), lambda t: (order(t), 0)),
        pl.BlockSpec((1, GLA_QK_W), lambda t: (0, 0)),
        pl.BlockSpec((1, GLA_V_W), lambda t: (0, 0)),
    ]


def _gla_pieces(q_ref, k_ref, v_ref, z_ref, cps):
    chunk = lambda ref: [ref[ci * GLA_CHUNK:(ci + 1) * GLA_CHUNK, :] for ci in range(cps)]
    return chunk(q_ref), chunk(k_ref), chunk(v_ref), chunk(z_ref)


def _gla_fwd(p, z, bg, gain, *, name):
    s = p.shape[0]
    r = GLA_ROWS
    cps = r // GLA_CHUNK

    def body(q_ref, k_ref, v_ref, gg_ref, z_ref, bg_ref, gain_ref, y_ref, oraw_ref, stsave_ref, st_s):
        @pl.when(pl.program_id(0) == 0)
        def _():
            st_s[...] = jnp.zeros_like(st_s)

        st = st_s[...]
        stsave_ref[0] = st
        o, st = _gla_step(*_gla_pieces(q_ref, k_ref, v_ref, z_ref, cps), bg_ref[...], st, _gla_consts())
        oraw_ref[...] = o
        st_s[...] = st
        y_ref[...] = _gla_post(o, gg_ref[...], gain_ref[...], _gla_g64())

    rowv = pl.BlockSpec((r, GLA_V_W), lambda t: (t, 0))
    return pl.pallas_call(
        body, name=name, grid=(s // r,), in_specs=_gla_in_specs(lambda t: t),
        out_specs=[rowv, rowv, pl.BlockSpec((1, GLA_V_W, GLA_QK_W), lambda t: (t, 0, 0))],
        out_shape=[SDS((s, GLA_V_W), F32), SDS((s, GLA_V_W), F32), SDS((s // r, GLA_V_W, GLA_QK_W), F32)],
        scratch_shapes=[pltpu.VMEM((GLA_V_W, GLA_QK_W), F32)],
        compiler_params=_params("arbitrary"),
    )(p, p, p, p, z, bg, gain)


def _gla_bwd(p, z, bg, gain, oraw, stsave, dy_all, *, name):
    s = p.shape[0]
    r = GLA_ROWS
    cps = r // GLA_CHUNK
    nsteps = s // r
    w_qkvg = 2 * GLA_QK_W + 2 * GLA_V_W

    def body(q_ref, k_ref, v_ref, gg_ref, z_ref, bg_ref, gain_ref, oraw_ref, stsave_ref, dy_ref,
             dqkvg_ref, dz_ref, dbg_ref, dgain_ref, dst_s):
        @pl.when(pl.program_id(0) == 0)
        def _():
            dst_s[...] = jnp.zeros_like(dst_s)
            dbg_ref[...] = jnp.zeros_like(dbg_ref)
            dgain_ref[...] = jnp.zeros_like(dgain_ref)

        _, vjp = jax.vjp(functools.partial(_gla_post, g64=_gla_g64()), oraw_ref[...], gg_ref[...], gain_ref[...])
        do, dgg, dgain = vjp(dy_ref[...])
        dqkvg_ref[:, 2 * GLA_QK_W + GLA_V_W:] = dgg
        dgain_ref[...] += dgain
        _, vjp = jax.vjp(functools.partial(_gla_step, c=_gla_consts()), *_gla_pieces(q_ref, k_ref, v_ref, z_ref, cps),
                         bg_ref[...], stsave_ref[0])
        dq, dk, dv, dz, dbg, dst = vjp((do, dst_s[...]))
        for ci in range(cps):
            rows = slice(ci * GLA_CHUNK, (ci + 1) * GLA_CHUNK)
            dqkvg_ref[rows, 0:GLA_QK_W] = dq[ci]
            dqkvg_ref[rows, GLA_QK_W:2 * GLA_QK_W] = dk[ci]
            dqkvg_ref[rows, 2 * GLA_QK_W:2 * GLA_QK_W + GLA_V_W] = dv[ci]
            dz_ref[rows, :] = dz[ci]
        dst_s[...] = dst
        dbg_ref[...] += dbg

    order = lambda t: nsteps - 1 - t
    rowv = pl.BlockSpec((r, GLA_V_W), lambda t: (order(t), 0))
    return pl.pallas_call(
        body, name=name, grid=(nsteps,),
        in_specs=_gla_in_specs(order) + [
            rowv, pl.BlockSpec((1, GLA_V_W, GLA_QK_W), lambda t: (order(t), 0, 0)),
            pl.BlockSpec((r, GLA_V_W), lambda t: (order(t), SWA_Q_W // GLA_V_W))],
        out_specs=[pl.BlockSpec((r, w_qkvg), lambda t: (order(t), 0)), pl.BlockSpec((r, GLA_QK_W), lambda t: (order(t), 0)),
                   pl.BlockSpec((1, GLA_QK_W), lambda t: (0, 0)), pl.BlockSpec((1, GLA_V_W), lambda t: (0, 0))],
        out_shape=[SDS((s, w_qkvg), F32), SDS((s, GLA_QK_W), F32), SDS((1, GLA_QK_W), F32), SDS((1, GLA_V_W), F32)],
        scratch_shapes=[pltpu.VMEM((GLA_V_W, GLA_QK_W), F32)],
        compiler_params=_params("arbitrary"),
    )(p, p, p, p, z, bg, gain, oraw, stsave, dy_all)


def _local_step(x, mem, target, small, big, on_grads):
    g1, gmix, gmem, g2, sqg, skg, sinks, rel_bias, wgu, bg, gla_gain, mqg, mkg = small
    (w3_1a, w3_1b), gather_mix, gather_ffn2 = big
    wgu_pad = jnp.zeros((GLA_QK_W, GLA_QK_W), BF16).at[:GLA_RANK].set(wgu.astype(BF16))
    gain256 = jnp.tile(gla_gain, (1, GLA_HEADS))

    (part,), saved1a = _ffn_fwd(x, g1, w3_1a, "ffn1a", partial=True)
    win_p, wkv, wout = gather_mix(part)
    (x1, h), saved1b = _ffn_fwd(x, g1, w3_1b, "ffn1b", next_gain=gmix, start=(saved1a[0], part))
    w3_2 = gather_ffn2((wout, x1))
    p = _matmul(h, win_p, tm=512, tn=IN_W_PAD, name="mix_in")
    hm = _rms_fwd(mem, gmem, tm=256, name="mem_rms")
    kv = _matmul(hm, wkv, tm=256, tn=512, name="mem_kv")
    p_glr = p[:, COL_GLR:]
    z = _matmul(p_glr, wgu_pad, tm=1024, tn=GLA_QK_W, name="gla_gate")
    y_swa = _swa_fwd(p, sqg, skg, sinks, rel_bias, name="swa_fwd")
    y_gla, oraw, stsave = _gla_fwd(p, z, bg, gain256, name="gla_fwd")
    y_mem = _mem_fwd(p, kv, mqg, mkg, tq=512, name="mem_fwd")
    x2 = _matmul([y_swa, y_gla, y_mem], wout, b_blocks=[0, 2, 3], tm=512, tn=1024, res=x1, name="mix_out")
    (dy, dyb, loss), saved2 = _ffn_fwd(x2, g2, w3_2, "ffn2", target=target)

    dh2, dw3_2 = _ffn_bwd_part(dyb, w3_2, saved2, 0, w3_2.shape[1] // FFN_TN, None, name="ffn2_bwd")
    dx2, dx2b, dg2 = _rms_bwd(x2, g2, dh2, dy, tm=512, name="ffn2_drms")
    dx2b = on_grads("ffn2", [dw3_2], dx2b)
    dy_all = _matmul(dx2b, wout, tb=True, tm=512, tn=1024, name="mix_dy")
    dwout = _matmul(jnp.concatenate([y_swa, y_gla, y_mem], axis=1), dx2b, ta=True, tm=512, tn=1024, out_dtype=BF16,
                    name="mix_dw_out")
    dq_swa, dkv_swa, dsqg, dskg, dsink, drb = _swa_bwd(p, sqg, skg, sinks, rel_bias, dy_all, name="swa_bwd")
    dqkvg, dz, dbg, dgain256 = _gla_bwd(p, z, bg, gain256, oraw, stsave, dy_all, name="gla_bwd")
    dmq, dkv_mem, dmqg, dmkg = _mem_bwd(p, kv, mqg, mkg, dy_all, tq=512, name="mem_bwd")
    dglr = _matmul(dz, wgu_pad, tb=True, tm=1024, tn=GLA_QK_W, name="gla_gate_dx")
    dwgu_pad = _matmul(p_glr, dz, ta=True, tm=GLA_QK_W, tn=GLA_QK_W, name="gla_gate_dw")
    dp = jnp.concatenate([dq_swa, dkv_swa, dqkvg, dmq, dglr], axis=1)
    dwin_p = _matmul(h, dp, ta=True, tm=1024, tn=640, out_dtype=BF16, name="mix_dw_in")
    dx1, dx1b, dgmix = _rms_bwd(x1, gmix, (dp, win_p), dx2, tm=512, name="mix_dh_drms")
    dwkv = _matmul(hm, dkv_mem, ta=True, tm=512, tn=512, out_dtype=BF16, name="mem_dw_kv")
    dx1b = on_grads("mix", (dwin_p, dwkv, dwout), dx1b)
    _, _, dgmem = _rms_bwd(mem, gmem, (dkv_mem, wkv), None, tm=256, name="mem_dh_drms")
    dh1, dw3_1a = _ffn_bwd_part(dx1b, w3_1a, saved1a, 0, w3_1a.shape[1] // FFN_TN, None, name="ffn1_bwd_a")
    dgla_gain = dgain256.reshape(GLA_HEADS, GLA_DV).sum(axis=0, keepdims=True)
    dsmall = [dgmix, dgmem, dg2, dsqg, dskg, dsink[:, :SWA_HEADS], drb[:, :SWA_HEADS], dwgu_pad[:GLA_RANK], dbg,
              dgla_gain, dmqg, dmkg, loss]
    dh1, dgmem, dwgu_pad = on_grads("ffn1a", [dw3_1a], (dh1, dgmem, dwgu_pad), small=dsmall)
    dh1, dw3_1b = _ffn_bwd_part(dx1b, w3_1b, saved1b, 0, w3_1b.shape[1] // FFN_TN, dh1, name="ffn1_bwd_b")
    dx, _, dg1 = _rms_bwd(x, g1, dh1, dx1, tm=512, name="ffn1_drms")
    on_grads("ffn1b", [dw3_1b], None, small=[dg1])
    return dx


def _mesh_place():
    x, y, c = lax.axis_index("x"), lax.axis_index("y"), lax.axis_index("c")
    other_chips = [(1 - x, y), (x, 1 - y), (1 - x, 1 - y)]
    return x, y, c, other_chips


def _handshake(peers):
    barrier = pltpu.get_barrier_semaphore()
    for peer in peers:
        pl.semaphore_signal(barrier, inc=1, device_id=peer, device_id_type=MESH)
    pl.semaphore_wait(barrier, len(peers))


def _sequencer_call(body, operands, out_shapes, sems, *, name, collective_id):
    return pl.kernel(
        body, name=name, out_type=out_shapes, mesh=plsc.ScalarSubcoreMesh(axis_name="sequencer", num_cores=1),
        scratch_types=sems, compiler_params=pltpu.CompilerParams(collective_id=collective_id),
    )(*operands)


def _window(ref, kind, slot, shape):
    if kind == "row":
        rows = pl.ds(pl.multiple_of(slot * shape[-2], 8), shape[-2])
        return ref.at[(slice(None),) * (len(shape) - 2) + (rows,)]
    return ref.at[slot]


def _gathered(shape, kind):
    if kind == "row":
        return tuple(shape[:-2]) + (N_DEV * shape[-2], shape[-1])
    return (N_DEV,) + tuple(shape)


def _half(view, hf):
    if len(view.shape) == 4:
        return view.at[:, hf]
    n = view.shape[-2] // 2
    return view.at[(slice(None),) * (len(view.shape) - 2) + (pl.ds(hf * n, n),)]


def _all_gather(shards, kinds, *, name, collective_id):
    nt = len(shards)

    def body(*refs):
        x_refs, o_refs = refs[:nt], refs[nt:2 * nt]
        send_sems, recv_sems, local_sems = refs[2 * nt:]
        x, y, c, _ = _mesh_place()
        me, sibling, xn, yn, diag = (x, y, c), (x, y, 1 - c), (1 - x, y, c), (x, 1 - y, c), (1 - x, 1 - y, c)
        _handshake([sibling, xn, yn])

        def win(t, block):
            bx, by, bc = block
            return _window(o_refs[t], kinds[t], 4 * bx + 2 * by + bc, shards[t].shape)

        def copy(k, t, src, dst, to):
            return pltpu.make_async_remote_copy(src_ref=src, dst_ref=dst, send_sem=send_sems.at[k, t],
                                                recv_sem=recv_sems.at[k, t], device_id=to, device_id_type=MESH)

        def piece(k, t, block, hf, to, from_shard=False):
            dst = _half(win(t, block), hf)
            return copy(k, t, _half(x_refs[t], hf) if from_shard else dst, dst, to)

        mine = [pltpu.make_async_copy(x_refs[t], win(t, me), local_sems.at[t]) for t in range(nt)]
        sent = []

        def start(cp):
            cp.start()
            sent.append(cp)

        for cp in mine:
            cp.start()
        for t in range(nt):
            start(copy(0, t, x_refs[t], win(t, me), sibling))
        for hf_x, hf_y in ((0, 1), (1, 0)):
            for t in range(nt):
                start(piece(1 + hf_x, t, me, hf_x, xn, True))
                start(piece(3 + hf_y, t, me, hf_y, yn, True))
        for k, block, hf, onward, k_sib in ((1, xn, 0, (5, yn), 7), (4, yn, 1, (6, xn), 10), (2, xn, 1, None, 8),
                                           (3, yn, 0, None, 9), (5, diag, 0, None, 11), (6, diag, 1, None, 12)):
            for t in range(nt):
                piece(k, t, block, hf, me).wait_recv()
                if onward is not None:
                    start(piece(onward[0], t, block, hf, onward[1]))
                start(piece(k_sib, t, block, hf, sibling))
        for t in range(nt):
            copy(0, t, x_refs[t], win(t, sibling), me).wait_recv()
        for k_sib, block, hf in ((7, xn, 0), (10, yn, 1), (8, xn, 1), (9, yn, 0), (11, diag, 0), (12, diag, 1)):
            for t in range(nt):
                bx, by, _ = block
                piece(k_sib, t, (bx, by, 1 - c), hf, me).wait_recv()
        for cp in sent:
            cp.wait_send()
        for cp in mine:
            cp.wait()

    return _sequencer_call(
        body, shards, [SDS(_gathered(s.shape, k), s.dtype) for s, k in zip(shards, kinds)],
        [pltpu.SemaphoreType.DMA((13, nt)), pltpu.SemaphoreType.DMA((13, nt)), pltpu.SemaphoreType.DMA((nt,))],
        name=name, collective_id=collective_id)


def _part_shape(shape, kind):
    if kind == "row":
        return tuple(shape[:-2]) + (shape[-2] // N_DEV, shape[-1])
    return tuple(shape[2:])


def _pair_exchange(grads, kinds, *, name, collective_id):
    nt = len(grads)
    part = [_part_shape(g.shape, k) for g, k in zip(grads, kinds)]

    def body(*refs):
        g_refs, o_refs = refs[:nt], refs[nt:2 * nt]
        send_sems, recv_sems = refs[2 * nt:]
        x, y, c, _ = _mesh_place()
        _handshake([(x, y, 1 - c)])
        copies = []
        for t in range(nt):
            for xy in range(4):
                src = g_refs[t].at[1 - c, xy] if kinds[t] == "stack" else _window(g_refs[t], kinds[t], 2 * xy + 1 - c, part[t])
                copies.append(pltpu.make_async_remote_copy(
                    src_ref=src, dst_ref=o_refs[t].at[xy], send_sem=send_sems.at[xy, t], recv_sem=recv_sems.at[xy, t],
                    device_id=(x, y, 1 - c), device_id_type=MESH))
        for cp in copies:
            cp.start()
        for cp in copies:
            cp.wait()

    return _sequencer_call(
        body, grads, [SDS((4,) + p, g.dtype) for p, g in zip(part, grads)],
        [pltpu.SemaphoreType.DMA((4, nt)), pltpu.SemaphoreType.DMA((4, nt))], name=name, collective_id=collective_id)


def _chip_exchange(parts, small, *, name, collective_id):
    nt = len(parts)
    if small is None:
        def body_plain(*refs):
            s_refs, o_refs = refs[:nt], refs[nt:2 * nt]
            send_sems, recv_sems = refs[2 * nt:]
            x, y, c, chips = _mesh_place()
            _handshake([(*chip, c) for chip in chips])
            copies = [pltpu.make_async_remote_copy(
                src_ref=s_refs[t].at[2 * chip[0] + chip[1]], dst_ref=o_refs[t].at[j],
                send_sem=send_sems.at[j, t], recv_sem=recv_sems.at[j, t], device_id=(*chip, c), device_id_type=MESH)
                for j, chip in enumerate(chips) for t in range(nt)]
            for cp in copies:
                cp.start()
            for cp in copies:
                cp.wait()

        return _sequencer_call(
            body_plain, parts, [SDS((3,) + s.shape[1:], s.dtype) for s in parts],
            [pltpu.SemaphoreType.DMA((3, nt)), pltpu.SemaphoreType.DMA((3, nt))], name=name, collective_id=collective_id)

    def body(*refs):
        s_refs, small_ref = refs[:nt], refs[nt]
        o_refs, small_all = refs[nt + 1:2 * nt + 1], refs[2 * nt + 1]
        send_sems, recv_sems, small_send, small_recv, local_sem = refs[2 * nt + 2:]
        x, y, c, chips = _mesh_place()
        _handshake([(px, py, pc) for px in (x, 1 - x) for py in (y, 1 - y) for pc in (c, 1 - c)][1:])

        def copy(j, t, chip):
            return pltpu.make_async_remote_copy(
                src_ref=s_refs[t].at[2 * chip[0] + chip[1]], dst_ref=o_refs[t].at[j],
                send_sem=send_sems.at[j, t], recv_sem=recv_sems.at[j, t], device_id=(*chip, c), device_id_type=MESH)

        flips = [(fx, fy, fc) for fx in (0, 1) for fy in (0, 1) for fc in (0, 1)][1:]

        def small_copy(k):
            fx, fy, fc = flips[k]
            to = (x ^ fx if fx else x, y ^ fy if fy else y, c ^ fc if fc else c)
            rows = small_all.at[4 * x + 2 * y + c]
            return pltpu.make_async_remote_copy(
                src_ref=small_ref, dst_ref=rows, send_sem=small_send.at[k], recv_sem=small_recv.at[k],
                device_id=to, device_id_type=MESH)

        own = pltpu.make_async_copy(small_ref, small_all.at[4 * x + 2 * y + c], local_sem)
        own.start()
        copies = [copy(j, t, chip) for j, chip in enumerate(chips) for t in range(nt)]
        smalls = [small_copy(k) for k in range(7)]
        for cp in smalls + copies:
            cp.start()
        for cp in smalls + copies:
            cp.wait()
        own.wait()

    return _sequencer_call(
        body, list(parts) + [small],
        [SDS((3,) + s.shape[1:], s.dtype) for s in parts] + [SDS((N_DEV,) + small.shape, small.dtype)],
        [pltpu.SemaphoreType.DMA((3, nt)), pltpu.SemaphoreType.DMA((3, nt)),
         pltpu.SemaphoreType.DMA((7,)), pltpu.SemaphoreType.DMA((7,)), pltpu.SemaphoreType.DMA],
        name=name, collective_id=collective_id)


def _pair_sum(grad, theirs, kind, c, *, name):
    if kind == "row":
        r, l = theirs.shape[-2:]
        n = theirs.size // (4 * r * l)
        grad, theirs = grad.reshape(n, N_DEV * r, l), theirs.reshape(4, n, r, l)
        mine_spec = pl.BlockSpec((n, r, l), lambda xy, c_ref: (0, 2 * xy + c_ref[0], 0))
    else:
        r, l = theirs.shape[-2:]
        n = theirs.size // (4 * r * l)
        theirs = theirs.reshape(4, n, r, l)
        grad = grad.reshape(2, 4, n, r, l)
        mine_spec = pl.BlockSpec((None, None, n, r, l), lambda xy, c_ref: (c_ref[0], xy, 0, 0, 0))

    def body(c_ref, a_ref, b_ref, o_ref):
        o_ref[...] = (a_ref[...].astype(F32) + b_ref[...].astype(F32)).astype(BF16)

    part = pl.BlockSpec((None, n, r, l), lambda xy, c_ref: (xy, 0, 0, 0))
    return pl.pallas_call(
        body, name=name,
        grid_spec=pltpu.PrefetchScalarGridSpec(num_scalar_prefetch=1, grid=(4,), in_specs=[mine_spec, part], out_specs=part),
        out_shape=SDS((4, n, r, l), BF16), compiler_params=_params("parallel"),
    )(c, grad, theirs)


def _adamw(w, g, m, v):
    m = ADAM_B1 * m + (1.0 - ADAM_B1) * g
    v = ADAM_B2 * v + (1.0 - ADAM_B2) * jnp.square(g)
    m_hat = m / (1.0 - ADAM_B1 ** ADAM_STEP)
    v_hat = v / (1.0 - ADAM_B2 ** ADAM_STEP)
    delta = -ADAM_LR * (m_hat / (jnp.sqrt(v_hat) + ADAM_EPS) + ADAM_WD * w)
    return delta, m, v


def _adam_big(owns, others, mat, xy, w, m, v, *, tr, name):
    _, r, l = w.shape
    lp = owns[0].shape[-1]
    nq = len(owns)
    rows = [tr] if nq == 1 else [o.shape[-2] for o in owns]
    assert sum(rows) == tr and r % tr == 0, (r, tr, rows)

    def body(xy_ref, *refs):
        own_refs, oth_refs = refs[:nq], refs[nq:2 * nq]
        w_ref, m_ref, v_ref, g_out, d_out, m_out, v_out = refs[2 * nq:]
        parts = []
        for q in range(nq):
            gq = own_refs[q][0, 0].astype(F32)
            for j in range(3):
                gq = gq + oth_refs[q][j, 0].astype(F32)
            parts.append(gq)
        g = (parts[0] if nq == 1 else jnp.concatenate(parts, axis=0))[:, :l]
        delta, m_new, v_new = _adamw(w_ref[0], g, m_ref[0], v_ref[0])
        g_out[0] = g
        d_out[0] = delta
        m_out[0] = m_new
        v_out[0] = v_new

    blk = pl.BlockSpec((1, tr, l), lambda i, xy_ref: (0, i, 0))
    own_specs = [pl.BlockSpec((1, 1, n, lp), lambda i, xy_ref: (xy_ref[0], mat, i, 0)) for n in rows]
    oth_specs = [pl.BlockSpec((3, 1, n, lp), lambda i, xy_ref: (0, mat, i, 0)) for n in rows]
    return pl.pallas_call(
        body, name=name,
        grid_spec=pltpu.PrefetchScalarGridSpec(
            num_scalar_prefetch=1, grid=(r // tr,), in_specs=own_specs + oth_specs + [blk, blk, blk],
            out_specs=[blk, blk, blk, blk]),
        out_shape=[SDS(w.shape, F32)] * 4, compiler_params=_params("parallel"),
    )(xy, *owns, *others, w, m, v)


def _small_layout(shapes):
    out, at = [], 0
    for r, c in shapes:
        rows = c // 128 if (r == 1 and c > 128) else r
        out.append((at, rows))
        at += -(-rows // 8) * 8
    return out, at


def _pack_small(parts, *, name):
    shapes = [a.shape for a in parts]
    layout, total = _small_layout(shapes)

    def body(*refs):
        o_ref = refs[-1]
        o_ref[...] = jnp.zeros_like(o_ref)
        for x_ref, (r, c), (at, rows) in zip(refs, shapes, layout):
            if r == 1 and c > 128:
                for k in range(rows):
                    o_ref[at + k:at + k + 1, :] = x_ref[:, k * 128:(k + 1) * 128]
            else:
                o_ref[at:at + r, 0:c] = x_ref[...]

    return pl.pallas_call(body, name=name, out_shape=SDS((total, 128), F32))(*parts)


def _adam_small(g_all, ws, ms, vs, *, name):
    n = len(ws)
    shapes = [w.shape for w in ws]
    layout, _ = _small_layout(shapes)

    def body(g_ref, *refs):
        w_refs, m_refs, v_refs, outs = refs[:n], refs[n:2 * n], refs[2 * n:3 * n], refs[3 * n:]
        g_sum = g_ref[0]
        for k in range(1, N_DEV):
            g_sum = g_sum + g_ref[k]
        for i, ((r, c), (at, rows)) in enumerate(zip(shapes, layout)):
            if r == 1 and c > 128:
                g = jnp.concatenate([g_sum[at + k:at + k + 1, :] for k in range(rows)], axis=1)
            else:
                g = g_sum[at:at + r, 0:c]
            delta, m_new, v_new = _adamw(w_refs[i][...], g, m_refs[i][...], v_refs[i][...])
            for q, val in enumerate((g, delta, m_new, v_new)):
                outs[4 * i + q][...] = val

    flat = pl.pallas_call(body, name=name, out_shape=[SDS(s, F32) for s in shapes for _ in range(4)])(g_all, *ws, *ms, *vs)
    return [flat[4 * i:4 * i + 4] for i in range(n)]


def kernel(x, mem, ffn1_norm, ffn1_w_gate, ffn1_w_up, ffn1_w_down, mix_norm, mem_norm, w_in, w_mem_kv, swa_q_norm, swa_k_norm, swa_sinks, rel_bias, gla_w_gate_up, gla_b_gate, gla_out_norm, mem_q_norm, mem_k_norm, w_out, ffn2_norm, ffn2_w_gate, ffn2_w_up, ffn2_w_down, loss_target, m_ffn1_norm, m_ffn1_w_gate, m_ffn1_w_up, m_ffn1_w_down, m_mix_norm, m_mem_norm, m_w_in, m_w_mem_kv, m_swa_q_norm, m_swa_k_norm, m_swa_sinks, m_rel_bias, m_gla_w_gate_up, m_gla_b_gate, m_gla_out_norm, m_mem_q_norm, m_mem_k_norm, m_w_out, m_ffn2_norm, m_ffn2_w_gate, m_ffn2_w_up, m_ffn2_w_down, v_ffn1_norm, v_ffn1_w_gate, v_ffn1_w_up, v_ffn1_w_down, v_mix_norm, v_mem_norm, v_w_in, v_w_mem_kv, v_swa_q_norm, v_swa_k_norm, v_swa_sinks, v_rel_bias, v_gla_w_gate_up, v_gla_b_gate, v_gla_out_norm, v_mem_q_norm, v_mem_k_norm, v_w_out, v_ffn2_norm, v_ffn2_w_gate, v_ffn2_w_up, v_ffn2_w_down):
    xi, yi, ci = lax.axis_index("x"), lax.axis_index("y"), lax.axis_index("c")
    c_arr = jnp.reshape(ci, (1,)).astype(jnp.int32)
    xy_arr = jnp.reshape(2 * xi + yi, (1,)).astype(jnp.int32)
    d = x.shape[-1]

    def ffn_shards(wg_s, wu_s, wd_s):
        return jnp.concatenate([wg_s.transpose(0, 2, 1), wu_s.transpose(0, 2, 1), wd_s], axis=0).astype(BF16)

    def gather_ffn(wg_s, wu_s, wd_s, name, collective_id, after):
        w3_s, _ = lax.optimization_barrier((ffn_shards(wg_s, wu_s, wd_s), after))
        return _all_gather([w3_s], ["row"], name=name, collective_id=collective_id)[0]

    w3_s = ffn_shards(ffn1_w_gate, ffn1_w_up, ffn1_w_down)
    w3_1a = _all_gather([w3_s[:, :FFN1_FIRST]], ["row"], name="gather_ffn1a", collective_id=0)[0]
    w3_s, _ = lax.optimization_barrier((w3_s, w3_1a))
    w3_1b = _all_gather([w3_s[:, FFN1_FIRST:]], ["row"], name="gather_ffn1b", collective_id=11)[0]

    def gather_mix(after):
        mix_s = lax.optimization_barrier((w_in[0].astype(BF16), w_mem_kv[0].astype(BF16), w_out[0].astype(BF16),
                                          (w3_1b, after)))[:3]
        win_all, wkv, wout = _all_gather(list(mix_s), ["stack", "row", "row"], name="gather_mix", collective_id=1)
        return _pack_win(win_all, tr=256, name="pack_w_in"), wkv, wout

    def gather_ffn2(after):
        return gather_ffn(ffn2_w_gate, ffn2_w_up, ffn2_w_down, "gather_ffn2", 2, after)

    small_w = [ffn1_norm, mix_norm, mem_norm, ffn2_norm, swa_q_norm, swa_k_norm, swa_sinks[0], rel_bias,
               gla_w_gate_up[0], gla_b_gate, gla_out_norm, mem_q_norm, mem_k_norm]
    collective_ids = {"ffn2": (3, 4), "mix": (5, 6), "ffn1a": (7, 8), "ffn1b": (9, 10)}
    reduced, small_box = {}, {}

    def on_grads(group, grads, carry, small=None):
        if group == "mix":
            dwin_p, dwkv, dwout = grads
            grads = [_unpack_win(dwin_p, tr=256, name="unpack_dw_in"), dwkv, dwout]
            kinds = ["stack", "row", "row"]
        else:
            kinds = ["row"]
        if reduced:
            earlier = list(reduced.values())[-1][1]
            *grads, _ = lax.optimization_barrier((*grads, earlier[0]))
        id_pair, id_chip = collective_ids[group]
        from_sibling = _pair_exchange(grads, kinds, name=f"pair_exchange_{group}", collective_id=id_pair)
        chip_sums = [_pair_sum(g, theirs, k, c_arr, name=f"pair_sum_{group}_{t}")
                     for t, (g, theirs, k) in enumerate(zip(grads, from_sibling, kinds))]
        if carry is not None:
            *chip_sums, carry = lax.optimization_barrier((*chip_sums, carry))
        if small is None:
            from_chips = _chip_exchange(chip_sums, None, name=f"chip_exchange_{group}", collective_id=id_chip)
        else:
            packed = _pack_small(small, name=f"pack_small_{group}")
            *from_chips, small_all = _chip_exchange(chip_sums, packed, name=f"chip_exchange_{group}",
                                                    collective_id=id_chip)
            small_box[group] = small_all
        reduced[group] = (chip_sums, from_chips)
        return carry

    grad_x = _local_step(x[0], mem[0], loss_target[0], small_w, ((w3_1a, w3_1b), gather_mix, gather_ffn2), on_grads)

    big_w = {"ffn1_w_gate": ("ffn1", 0, 0, True, ffn1_w_gate, m_ffn1_w_gate, v_ffn1_w_gate),
             "ffn1_w_up": ("ffn1", 0, 1, True, ffn1_w_up, m_ffn1_w_up, v_ffn1_w_up),
             "ffn1_w_down": ("ffn1", 0, 2, False, ffn1_w_down, m_ffn1_w_down, v_ffn1_w_down),
             "w_in": ("mix", 0, 0, False, w_in, m_w_in, v_w_in),
             "w_mem_kv": ("mix", 1, 0, False, w_mem_kv, m_w_mem_kv, v_w_mem_kv),
             "w_out": ("mix", 2, 0, False, w_out, m_w_out, v_w_out),
             "ffn2_w_gate": ("ffn2", 0, 0, True, ffn2_w_gate, m_ffn2_w_gate, v_ffn2_w_gate),
             "ffn2_w_up": ("ffn2", 0, 1, True, ffn2_w_up, m_ffn2_w_up, v_ffn2_w_up),
             "ffn2_w_down": ("ffn2", 0, 2, False, ffn2_w_down, m_ffn2_w_down, v_ffn2_w_down)}
    res = {}
    for nm, (group, t, mat, transposed, w, m, v) in big_w.items():
        shape = w.shape
        if transposed:
            w, m, v = (a.transpose(0, 2, 1) for a in (w, m, v))
        r = w.shape[1]
        halves = ["ffn1a", "ffn1b"] if group == "ffn1" else [group]
        if len(halves) == 1:
            tr = 256 if r % 256 == 0 else r
        else:
            tr = r
        out = _adam_big([reduced[k][0][t] for k in halves], [reduced[k][1][t] for k in halves], mat, xy_arr, w, m, v,
                        tr=tr, name=f"adam_{nm}")
        if transposed:
            out = [a.reshape(1, -1, d).transpose(0, 2, 1) for a in out]
        res[nm] = [a.reshape(shape) for a in out]
    small_names = ["ffn1_norm", "mix_norm", "mem_norm", "ffn2_norm", "swa_q_norm", "swa_k_norm", "swa_sinks", "rel_bias",
                   "gla_w_gate_up", "gla_b_gate", "gla_out_norm", "mem_q_norm", "mem_k_norm"]
    small_m = [m_ffn1_norm, m_mix_norm, m_mem_norm, m_ffn2_norm, m_swa_q_norm, m_swa_k_norm, m_swa_sinks, m_rel_bias,
               m_gla_w_gate_up, m_gla_b_gate, m_gla_out_norm, m_mem_q_norm, m_mem_k_norm]
    small_v = [v_ffn1_norm, v_mix_norm, v_mem_norm, v_ffn2_norm, v_swa_q_norm, v_swa_k_norm, v_swa_sinks, v_rel_bias,
               v_gla_w_gate_up, v_gla_b_gate, v_gla_out_norm, v_mem_q_norm, v_mem_k_norm]
    small_full = [ffn1_norm, mix_norm, mem_norm, ffn2_norm, swa_q_norm, swa_k_norm, swa_sinks, rel_bias,
                  gla_w_gate_up, gla_b_gate, gla_out_norm, mem_q_norm, mem_k_norm]
    zero = jnp.zeros((1, 1), F32)
    two_d = lambda a: a.reshape(a.shape[-2:])
    for group, sel in (("ffn1a", slice(1, None)), ("ffn1b", slice(0, 1))):
        extra = [zero] if group == "ffn1a" else []
        ws, ms, vs = ([two_d(a) for a in arrs[sel]] + extra for arrs in (small_full, small_m, small_v))
        updated = _adam_small(small_box[group], ws, ms, vs, name=f"adam_small_{group}")
        for nm, full, out in zip(small_names[sel], small_full[sel], updated):
            res[nm] = [a.reshape(full.shape) for a in out]
        if extra:
            loss = updated[-1][0].reshape(())

    order = ["ffn1_norm", "ffn1_w_gate", "ffn1_w_up", "ffn1_w_down", "mix_norm", "mem_norm", "w_in", "w_mem_kv",
             "swa_q_norm", "swa_k_norm", "swa_sinks", "rel_bias", "gla_w_gate_up", "gla_b_gate", "gla_out_norm",
             "mem_q_norm", "mem_k_norm", "w_out", "ffn2_norm", "ffn2_w_gate", "ffn2_w_up", "ffn2_w_down"]
    outs = [loss, grad_x[None]]
    for q in range(4):
        outs += [res[nm][q] for nm in order]
    return tuple(outs)
```

```python
import functools
import math

import numpy as np
import jax
import jax.numpy as jnp
from jax import lax
from jax.experimental import pallas as pl
from jax.experimental.pallas import tpu as pltpu
from jax.experimental.pallas import tpu_sc as plsc

F32 = jnp.float32
BF16 = jnp.bfloat16
SDS = jax.ShapeDtypeStruct

EPS = 1e-6
HEAD_DIM = 64
SWA_HEADS = 8
SWA_KV_HEADS = 2
SWA_GROUP = SWA_HEADS // SWA_KV_HEADS
BLOCK = 128
N_BUCKETS = 32
MAX_DISTANCE = 128
GLA_HEADS = 4
GLA_DK = 32
GLA_DV = 64
GLA_RANK = 16
GLA_TAU = 16.0
GLA_CHUNK = 32
MEM_HEADS = 4
SWA_Q_W = SWA_HEADS * HEAD_DIM
SWA_KV_W = SWA_KV_HEADS * HEAD_DIM
GLA_QK_W = GLA_HEADS * GLA_DK
GLA_V_W = GLA_HEADS * GLA_DV
MEM_Q_W = MEM_HEADS * HEAD_DIM
IN_W = 1808
IN_W_PAD = 1920
COL_SQ, COL_SKV, COL_GQ, COL_GK, COL_GV, COL_GG, COL_MQ, COL_GLR = 0, 512, 768, 896, 1024, 1280, 1536, 1792

ADAM_LR = 0.001
ADAM_B1 = 0.9
ADAM_B2 = 0.999
ADAM_EPS = 1e-08
ADAM_WD = 0.01
ADAM_STEP = 10

N_DEV = 8
VMEM_LIMIT_BYTES = 56 * 1024 * 1024
MESH = pl.DeviceIdType.MESH


def _params(*sem):
    return pltpu.CompilerParams(dimension_semantics=sem or None, vmem_limit_bytes=VMEM_LIMIT_BYTES)


def _dot(a, b, ta, tb, precision=None):
    dims = (((0 if ta else 1,), (1 if tb else 0,)), ((), ()))
    return lax.dot_general(a, b, dims, preferred_element_type=F32, precision=precision)


def _mm_raw(a, b, ta=False, tb=False):
    return _dot(a.astype(BF16), b.astype(BF16), ta, tb)


def _mmf_raw(a, b, ta=False, tb=False):
    return _dot(a, b, ta, tb, lax.Precision.HIGHEST)


def _make_mm(raw):
    @functools.partial(jax.custom_vjp, nondiff_argnums=(2, 3))
    def mm(a, b, ta=False, tb=False):
        return raw(a, b, ta, tb)

    def fwd(a, b, ta, tb):
        return raw(a, b, ta, tb), (a, b)

    def bwd(ta, tb, res, g):
        a, b = res
        da = raw(b, g, tb, True) if ta else raw(g, b, False, not tb)
        db = raw(g, a, True, ta) if tb else raw(a, g, not ta, False)
        return da, db

    mm.defvjp(fwd, bwd)
    return mm


_mm = _make_mm(_mm_raw)
_mmf = _make_mm(_mmf_raw)


def _mm3(a, b, ta=False, tb=False):
    a_hi, b_hi = a.astype(BF16).astype(F32), b.astype(BF16).astype(F32)
    return _mm(a_hi, b_hi, ta, tb) + _mm(a_hi, b - b_hi, ta, tb) + _mm(a - a_hi, b_hi, ta, tb)


def _rms(x, g):
    return x * lax.rsqrt(jnp.mean(x * x, axis=-1, keepdims=True) + EPS) * g


def _silu_mul(g, u):
    return jax.nn.silu(g) * u


def _log_sigmoid(z):
    return jnp.minimum(z, 0.0) - jnp.log(1.0 + jnp.exp(-jnp.abs(z)))


def _matmul(a_list, b, *, ta=False, tb=False, tm, tn, b_blocks=None, res=None, scale=1.0, out_dtype=F32, name):
    if not isinstance(a_list, (list, tuple)):
        a_list = [a_list]
    n_a = len(a_list)
    m = a_list[0].shape[1] if ta else a_list[0].shape[0]
    ks = [a.shape[0] if ta else a.shape[1] for a in a_list]
    n = b.shape[0] if tb else b.shape[1]
    if b_blocks is None:
        assert n_a == 1
        b_blocks = [0]
    tm, tn = min(tm, m), min(tn, n)
    assert m % tm == 0 and n % tn == 0, (m, n, tm, tn)

    def body(*refs):
        a_refs, b_refs = refs[:n_a], refs[n_a:2 * n_a]
        r_ref = refs[2 * n_a] if res is not None else None
        o_ref = refs[-1]
        acc = _mm_raw(a_refs[0][...], b_refs[0][...], ta, tb)
        for k in range(1, n_a):
            acc = acc + _mm_raw(a_refs[k][...], b_refs[k][...], ta, tb)
        if scale != 1.0:
            acc = acc * scale
        if r_ref is not None:
            acc = r_ref[...] + acc
        o_ref[...] = acc.astype(out_dtype)

    in_specs = []
    for k in ks:
        in_specs.append(pl.BlockSpec((k, tm), lambda i, j: (0, i)) if ta else pl.BlockSpec((tm, k), lambda i, j: (i, 0)))
    for k, blk in zip(ks, b_blocks):
        if tb:
            in_specs.append(pl.BlockSpec((tn, k), functools.partial(lambda i, j, blk: (j, blk), blk=blk)))
        else:
            in_specs.append(pl.BlockSpec((k, tn), functools.partial(lambda i, j, blk: (blk, j), blk=blk)))
    args = list(a_list) + [b] * n_a
    if res is not None:
        in_specs.append(pl.BlockSpec((tm, tn), lambda i, j: (i, j)))
        args.append(res)
    return pl.pallas_call(
        body, name=name, grid=(m // tm, n // tn), in_specs=in_specs,
        out_specs=pl.BlockSpec((tm, tn), lambda i, j: (i, j)), out_shape=SDS((m, n), out_dtype),
        compiler_params=_params("parallel", "parallel"),
    )(*args)


def _win_pieces(w):
    glr_lo, glr_hi = COL_MQ, COL_MQ + GLA_RANK
    out = []
    for j in range(N_DEV):
        for lo, hi, shift in ((0, glr_lo, 0), (glr_lo, glr_hi, COL_GLR - glr_lo), (glr_hi, IN_W, COL_MQ - glr_hi)):
            s, e = max(j * w, lo), min((j + 1) * w, hi)
            if s < e:
                out.append((j, s - j * w, e - j * w, s + shift))
    return out


def _pack_win(win_all, *, tr, name):
    _, d, w = win_all.shape

    def body(i_ref, o_ref):
        for j, a, b, dst in _win_pieces(w):
            o_ref[:, dst:dst + b - a] = i_ref[j][:, a:b]
        o_ref[:, IN_W:] = jnp.zeros((tr, IN_W_PAD - IN_W), o_ref.dtype)

    return pl.pallas_call(
        body, name=name, grid=(d // tr,), in_specs=[pl.BlockSpec((N_DEV, tr, w), lambda i: (0, i, 0))],
        out_specs=pl.BlockSpec((tr, IN_W_PAD), lambda i: (i, 0)), out_shape=SDS((d, IN_W_PAD), win_all.dtype),
        compiler_params=_params("parallel"),
    )(win_all)


def _unpack_win(dwin_p, *, tr, name):
    d = dwin_p.shape[0]
    w = IN_W // N_DEV

    def body(i_ref, o_ref):
        for j, a, b, src in _win_pieces(w):
            o_ref[j % 2, j // 2, :, a:b] = i_ref[:, src:src + b - a]

    return pl.pallas_call(
        body, name=name, grid=(d // tr,), in_specs=[pl.BlockSpec((tr, IN_W_PAD), lambda i: (i, 0))],
        out_specs=pl.BlockSpec((2, 4, tr, w), lambda i: (0, 0, i, 0)), out_shape=SDS((2, 4, d, w), dwin_p.dtype),
        compiler_params=_params("parallel"),
    )(dwin_p)


def _rms_fwd(x, g, *, tm, name):
    s, d = x.shape

    def body(x_ref, g_ref, h_ref):
        h_ref[...] = _rms(x_ref[...], g_ref[...]).astype(BF16)

    return pl.pallas_call(
        body, name=name, grid=(s // tm,),
        in_specs=[pl.BlockSpec((tm, d), lambda i: (i, 0)), pl.BlockSpec((1, d), lambda i: (0, 0))],
        out_specs=pl.BlockSpec((tm, d), lambda i: (i, 0)), out_shape=SDS((s, d), BF16),
        compiler_params=_params("parallel"),
    )(x, g)


def _rms_bwd(x, g, dh, dres, *, tm, name):
    s, d = x.shape
    want_dx = dres is not None
    product = isinstance(dh, tuple)

    def body(*refs):
        n_dh = 2 if product else 1
        x_ref, g_ref = refs[:2]
        dh_refs, rest = refs[2:2 + n_dh], refs[2 + n_dh:]
        if want_dx:
            dres_ref, dx_ref, dxb_ref, dg_ref = rest
        else:
            dg_ref, = rest
        dh_tile = _mm_raw(dh_refs[0][...], dh_refs[1][...], False, True) if product else dh_refs[0][...]
        _, vjp = jax.vjp(_rms, x_ref[...], g_ref[...])
        dx, dg = vjp(dh_tile)
        if want_dx:
            dx = dres_ref[...] + dx
            dx_ref[...] = dx
            dxb_ref[...] = dx.astype(BF16)

        @pl.when(pl.program_id(0) == 0)
        def _():
            dg_ref[...] = jnp.zeros_like(dg_ref)

        dg_ref[...] += dg

    row = pl.BlockSpec((tm, d), lambda i: (i, 0))
    vec = pl.BlockSpec((1, d), lambda i: (0, 0))
    if product:
        k = dh[0].shape[1]
        dh_specs, dh_args = [pl.BlockSpec((tm, k), lambda i: (i, 0)), pl.BlockSpec((d, k), lambda i: (0, 0))], list(dh)
    else:
        dh_specs, dh_args = [row], [dh]
    if want_dx:
        return pl.pallas_call(
            body, name=name, grid=(s // tm,), in_specs=[row, vec] + dh_specs + [row], out_specs=[row, row, vec],
            out_shape=[SDS((s, d), F32), SDS((s, d), BF16), SDS((1, d), F32)], compiler_params=_params("arbitrary"),
        )(x, g, *dh_args, dres)
    return None, None, pl.pallas_call(
        body, name=name, grid=(s // tm,), in_specs=[row, vec] + dh_specs, out_specs=vec,
        out_shape=SDS((1, d), F32), compiler_params=_params("arbitrary"),
    )(x, g, *dh_args)


FFN_TN = 256
FFN_TN_FWD = 512
FFN1_FIRST = 192


def _ffn_fwd(x, gain, w3, tag, *, tm=1024, next_gain=None, target=None, start=None, partial=False):
    s, d = x.shape
    f = w3.shape[1]
    tn = FFN_TN_FWD if f % FFN_TN_FWD == 0 else FFN_TN
    nj = f // tn
    tm = min(tm, s)
    n_extra = (next_gain is not None) + (target is not None) + 2 * (start is not None)

    def body(*refs):
        x_ref, gain_ref, wg_ref, wu_ref, wd_ref = refs[:5]
        extra, outs = refs[5:5 + n_extra], refs[5 + n_extra:-1]
        acc_s = refs[-1]
        g_ref, u_ref = outs[-2:]
        h_ref = extra[-2] if start is not None else outs[-3]
        i, j = pl.program_id(0), pl.program_id(1)

        @pl.when(j == 0)
        def _():
            if start is None:
                h_ref[...] = _rms(x_ref[...], gain_ref[...]).astype(BF16)
                acc_s[...] = jnp.zeros_like(acc_s)
            else:
                acc_s[...] = extra[-1][...]

        hv = h_ref[...]
        g = _mm_raw(hv, wg_ref[...], False, True)
        u = _mm_raw(hv, wu_ref[...], False, True)
        g_ref[...] = g.astype(BF16)
        u_ref[...] = u.astype(BF16)
        acc_s[...] += _mm_raw(_silu_mul(g, u), wd_ref[...])

        @pl.when(j == nj - 1)
        def _():
            y = acc_s[...] if partial else x_ref[...] + 0.5 * acc_s[...]
            if target is None:
                outs[0][...] = y
                if next_gain is not None:
                    outs[1][...] = _rms(y, extra[0][...]).astype(BF16)
            else:
                dy_ref, dyb_ref, loss_ref = outs[:3]
                diff = y - extra[0][...]
                dy_ref[...] = diff * (1.0 / d)
                dyb_ref[...] = (diff * (1.0 / d)).astype(BF16)
                part = 0.5 * jnp.sum(jnp.mean(diff * diff, axis=-1, keepdims=True), axis=0, keepdims=True)

                @pl.when(i == 0)
                def _():
                    loss_ref[...] = part

                @pl.when(i > 0)
                def _():
                    loss_ref[...] += part

    row = pl.BlockSpec((tm, d), lambda i, j: (i, 0))
    vec = pl.BlockSpec((1, d), lambda i, j: (0, 0))
    tile = pl.BlockSpec((tm, tn), lambda i, j: (i, j))
    in_specs = [row, vec] + [pl.BlockSpec((None, tn, d), functools.partial(lambda i, j, k: (k, j, 0), k=k)) for k in range(3)]
    args = [x, gain, w3, w3, w3]
    if target is None:
        out_specs, out_shape = [row], [SDS((s, d), F32)]
        if next_gain is not None:
            in_specs.append(vec)
            args.append(next_gain)
            out_specs.append(row)
            out_shape.append(SDS((s, d), BF16))
    else:
        in_specs.append(row)
        args.append(target)
        out_specs = [row, row, pl.BlockSpec((1, 1), lambda i, j: (0, 0))]
        out_shape = [SDS((s, d), F32), SDS((s, d), BF16), SDS((1, 1), F32)]
    if start is None:
        out_specs.append(row)
        out_shape.append(SDS((s, d), BF16))
    else:
        in_specs += [row, row]
        args += list(start)
    *head, g, u = pl.pallas_call(
        body, name=f"{tag}_fwd", grid=(s // tm, nj), in_specs=in_specs,
        out_specs=out_specs + [tile, tile],
        out_shape=out_shape + [SDS((s, f), BF16), SDS((s, f), BF16)],
        scratch_shapes=[pltpu.VMEM((tm, d), F32)],
        compiler_params=_params("arbitrary", "arbitrary"),
    )(*args)
    if start is None:
        *head, h = head
    else:
        h = start[0]
    return head, (h, g, u)


def _ffn_bwd_part(dyb, w3, saved, first, count, dh_init, *, name):
    h, g, u = saved
    s, d = h.shape
    tn = FFN_TN

    def body(*refs):
        if dh_init is None:
            dy_ref, h_ref, wd_ref, wg_ref, wu_ref, g_ref, u_ref, dh_ref, dw3_ref, dg_s, du_s, a_s = refs
        else:
            dy_ref, h_ref, wd_ref, wg_ref, wu_ref, g_ref, u_ref, dh0_ref, dh_ref, dw3_ref, dg_s, du_s, a_s = refs
        j = pl.program_id(0)

        @pl.when(j == 0)
        def _():
            dh_ref[...] = jnp.zeros_like(dh_ref) if dh_init is None else dh0_ref[...]
            for ref in (dg_s, du_s, a_s):
                ref[...] = jnp.zeros_like(ref)

        now, before = j % 2, 1 - j % 2
        dyv = dy_ref[...]
        hv = h_ref[...]
        dg, du, a = dg_s[before], du_s[before], a_s[before]
        dh_ref[...] += _mm_raw(dg, wg_ref[...]) + _mm_raw(du, wu_ref[...])
        dw3_ref[0] = _mm_raw(dg, hv, True, False).astype(BF16)
        dw3_ref[1] = _mm_raw(du, hv, True, False).astype(BF16)
        dw3_ref[2] = (_mm_raw(a, dyv, True, False) * 0.5).astype(BF16)

        da = _mm_raw(dyv, wd_ref[...], False, True) * 0.5
        a, vjp = jax.vjp(_silu_mul, g_ref[...].astype(F32), u_ref[...].astype(F32))
        dg, du = vjp(da)
        dg_s[now] = dg.astype(BF16)
        du_s[now] = du.astype(BF16)
        a_s[now] = a.astype(BF16)

    this = lambda j: first + jnp.minimum(j, count - 1)
    last = lambda j: first + jnp.maximum(j - 1, 0)
    full = pl.BlockSpec((s, d), lambda j: (0, 0))
    once = pl.BlockSpec((s, d), lambda j: (0, 0), pipeline_mode=pl.Buffered(1))
    tile = pl.BlockSpec((s, tn), lambda j: (0, this(j)))
    in_specs = [once, once, pl.BlockSpec((None, tn, d), lambda j: (2, this(j), 0)),
                pl.BlockSpec((None, tn, d), lambda j: (0, last(j), 0)), pl.BlockSpec((None, tn, d), lambda j: (1, last(j), 0)),
                tile, tile]
    args = [dyb, h, w3, w3, w3, g, u]
    if dh_init is not None:
        in_specs.append(once)
        args.append(dh_init)
    return pl.pallas_call(
        body, name=name, grid=(count + 1,), in_specs=in_specs,
        out_specs=[full, pl.BlockSpec((3, tn, d), lambda j: (0, jnp.maximum(j - 1, 0), 0))],
        out_shape=[SDS((s, d), F32), SDS((3, count * tn, d), BF16)],
        scratch_shapes=[pltpu.VMEM((2, s, tn), BF16)] * 3,
        compiler_params=_params("arbitrary"),
    )(*args)


def _bucket_table():
    qi = np.arange(BLOCK)[:, None]
    kj = np.arange(2 * BLOCK)[None, :]
    dist = np.maximum(qi + BLOCK - kj, 0)
    max_exact = N_BUCKETS // 2
    d = np.maximum(dist, 1).astype(np.float32)
    large = max_exact + (np.log(d / np.float32(max_exact)) / np.float32(math.log(MAX_DISTANCE / max_exact))
                         * np.float32(N_BUCKETS - max_exact)).astype(np.int32)
    large = np.minimum(large, N_BUCKETS - 1)
    band = np.where(dist < max_exact, dist, large).astype(np.int32)
    return np.where(np.tril(np.ones((BLOCK, BLOCK), bool)), band[:, BLOCK:], band[:, :BLOCK])


SWA_STACK = SWA_GROUP * BLOCK


def _swa_masks(n):
    qi = lax.broadcasted_iota(jnp.int32, (SWA_STACK, BLOCK), 0) % BLOCK
    kj = lax.broadcasted_iota(jnp.int32, (SWA_STACK, BLOCK), 1)
    own = kj <= qi
    return own, own | (n > 0)


def _swa_group(q, kp, kc, vp, vc, qg, kg, sink, bias, own, valid):
    qn = _rms(q, qg)
    s = jnp.where(own, _mm(qn, _rms(kc, kg), False, True), _mm(qn, _rms(kp, kg), False, True))
    s = s * (HEAD_DIM ** -0.5) + bias
    s = jnp.where(valid, s, -jnp.inf)
    m = lax.stop_gradient(jnp.maximum(jnp.max(s, axis=-1, keepdims=True), sink))
    p = jnp.exp(s - m)
    p = p / (jnp.sum(p, axis=-1, keepdims=True) + jnp.exp(sink - m))
    return _mm(jnp.where(own, p, 0.0), vc) + _mm(jnp.where(own, 0.0, p), vp)


def _swa_bias_table(rb_ref, bucket, bias_s):
    for h in range(SWA_HEADS):
        acc = jnp.zeros((BLOCK, BLOCK), F32)
        for b in range(N_BUCKETS):
            acc = jnp.where(bucket == b, rb_ref[b, h], acc)
        bias_s[h // SWA_GROUP, (h % SWA_GROUP) * BLOCK:(h % SWA_GROUP + 1) * BLOCK, :] = acc


def _swa_stack(ref, g):
    return jnp.concatenate([ref[:, (g * SWA_GROUP + hh) * HEAD_DIM:(g * SWA_GROUP + hh + 1) * HEAD_DIM]
                            for hh in range(SWA_GROUP)], axis=0)


def _swa_unstack(ref, g, stacked):
    for hh in range(SWA_GROUP):
        h = g * SWA_GROUP + hh
        ref[:, h * HEAD_DIM:(h + 1) * HEAD_DIM] = stacked[hh * BLOCK:(hh + 1) * BLOCK]


def _swa_sink_column(sink_ref, g):
    head = lax.broadcasted_iota(jnp.int32, (SWA_STACK, 1), 0) // BLOCK
    col = jnp.zeros((SWA_STACK, 1), F32)
    for hh in range(SWA_GROUP):
        col = jnp.where(head == hh, sink_ref[g * SWA_GROUP + hh], col)
    return col


def _swa_band(kvp_ref, kvc_ref, g):
    k = slice(g * HEAD_DIM, (g + 1) * HEAD_DIM)
    v = slice(SWA_KV_W + g * HEAD_DIM, SWA_KV_W + (g + 1) * HEAD_DIM)
    return kvp_ref[:, k], kvc_ref[:, k], kvp_ref[:, v], kvc_ref[:, v]


def _swa_specs(order):
    kvc = COL_SKV // (2 * SWA_KV_W)
    return [
        pl.BlockSpec((BLOCK, SWA_Q_W), lambda t: (order(t), 0)),
        pl.BlockSpec((BLOCK, 2 * SWA_KV_W), lambda t: (jnp.maximum(order(t) - 1, 0), kvc)),
        pl.BlockSpec((BLOCK, 2 * SWA_KV_W), lambda t: (order(t), kvc)),
        pl.BlockSpec((1, HEAD_DIM), lambda t: (0, 0)),
        pl.BlockSpec((1, HEAD_DIM), lambda t: (0, 0)),
        pl.BlockSpec(memory_space=pltpu.SMEM),
        pl.BlockSpec(memory_space=pltpu.SMEM),
        pl.BlockSpec((BLOCK, BLOCK), lambda t: (0, 0)),
    ]


def _swa_fwd(p, qg, kg, sinks, rel_bias, *, name):
    s = p.shape[0]
    nb = s // BLOCK

    def body(q_ref, kvp_ref, kvc_ref, qg_ref, kg_ref, sink_ref, rb_ref, bucket_ref, y_ref, bias_s):
        n = pl.program_id(0)

        @pl.when(n == 0)
        def _():
            _swa_bias_table(rb_ref, bucket_ref[...], bias_s)

        own, valid = _swa_masks(n)
        for g in range(SWA_KV_HEADS):
            out = _swa_group(_swa_stack(q_ref, g), *_swa_band(kvp_ref, kvc_ref, g), qg_ref[...], kg_ref[...],
                             _swa_sink_column(sink_ref, g), bias_s[g], own, valid)
            _swa_unstack(y_ref, g, out)

    return pl.pallas_call(
        body, name=name, grid=(nb,), in_specs=_swa_specs(lambda t: t),
        out_specs=pl.BlockSpec((BLOCK, SWA_Q_W), lambda t: (t, 0)), out_shape=SDS((s, SWA_Q_W), F32),
        scratch_shapes=[pltpu.VMEM((SWA_KV_HEADS, SWA_STACK, BLOCK), F32)],
        compiler_params=_params("arbitrary"),
    )(p, p, p, qg, kg, sinks, rel_bias, jnp.asarray(_bucket_table()))


def _swa_bwd(p, qg, kg, sinks, rel_bias, dy_all, *, name):
    s = p.shape[0]
    nb = s // BLOCK

    def body(q_ref, kvp_ref, kvc_ref, qg_ref, kg_ref, sink_ref, rb_ref, bucket_ref, dy_ref,
             dq_ref, dkv_ref, dqg_ref, dkg_ref, dsink_ref, drb_ref, bias_s, dbias_s, carry_s):
        t = pl.program_id(0)
        n = nb - 1 - t

        @pl.when(t == 0)
        def _():
            _swa_bias_table(rb_ref, bucket_ref[...], bias_s)
            dbias_s[...] = jnp.zeros_like(dbias_s)
            carry_s[...] = jnp.zeros_like(carry_s)
            dqg_ref[...] = jnp.zeros_like(dqg_ref)
            dkg_ref[...] = jnp.zeros_like(dkg_ref)
            dsink_ref[...] = jnp.zeros_like(dsink_ref)
            drb_ref[...] = jnp.zeros_like(drb_ref)

        own, valid = _swa_masks(n)
        lane = lax.broadcasted_iota(jnp.int32, (1, BLOCK), 1)
        dqg = jnp.zeros((1, HEAD_DIM), F32)
        dkg = jnp.zeros((1, HEAD_DIM), F32)
        dsink_vec = jnp.zeros((1, BLOCK), F32)
        for g in range(SWA_KV_HEADS):
            _, vjp = jax.vjp(functools.partial(_swa_group, own=own, valid=valid), _swa_stack(q_ref, g),
                             *_swa_band(kvp_ref, kvc_ref, g), qg_ref[...], kg_ref[...], _swa_sink_column(sink_ref, g),
                             bias_s[g])
            dq, dkp, dkc, dvp, dvc, dqg_g, dkg_g, dsink_col, dbias = vjp(_swa_stack(dy_ref, g))
            _swa_unstack(dq_ref, g, dq)
            dqg += dqg_g
            dkg += dkg_g
            dbias_s[g] += dbias
            for hh in range(SWA_GROUP):
                dsink_h = jnp.sum(dsink_col[hh * BLOCK:(hh + 1) * BLOCK], axis=0, keepdims=True)
                dsink_vec += jnp.where(lane == g * SWA_GROUP + hh, dsink_h, 0.0)
            lo = g * HEAD_DIM
            dkv_ref[:, lo:lo + HEAD_DIM] = dkc + carry_s[g]
            carry_s[g] = dkp
            lo += SWA_KV_W
            dkv_ref[:, lo:lo + HEAD_DIM] = dvc + carry_s[SWA_KV_HEADS + g]
            carry_s[SWA_KV_HEADS + g] = dvp
        dqg_ref[...] += dqg
        dkg_ref[...] += dkg
        dsink_ref[...] += dsink_vec

        @pl.when(t == nb - 1)
        def _():
            bucket = bucket_ref[...]
            row = lax.broadcasted_iota(jnp.int32, (N_BUCKETS, BLOCK), 0)
            col = lax.broadcasted_iota(jnp.int32, (N_BUCKETS, BLOCK), 1)
            acc = jnp.zeros((N_BUCKETS, BLOCK), F32)
            for h in range(SWA_HEADS):
                dbias = dbias_s[h // SWA_GROUP, (h % SWA_GROUP) * BLOCK:(h % SWA_GROUP + 1) * BLOCK, :]
                for b in range(N_BUCKETS):
                    part = jnp.sum(jnp.where(bucket == b, dbias, 0.0), axis=1, keepdims=True)
                    val = jnp.sum(part, axis=0, keepdims=True)
                    acc = acc + jnp.where((row == b) & (col == h), val, 0.0)
            drb_ref[...] = acc

    order = lambda t: nb - 1 - t
    vec = pl.BlockSpec((1, HEAD_DIM), lambda t: (0, 0))
    return pl.pallas_call(
        body, name=name, grid=(nb,),
        in_specs=_swa_specs(order) + [pl.BlockSpec((BLOCK, SWA_Q_W), lambda t: (order(t), 0))],
        out_specs=[pl.BlockSpec((BLOCK, SWA_Q_W), lambda t: (order(t), 0)),
                   pl.BlockSpec((BLOCK, 2 * SWA_KV_W), lambda t: (order(t), 0)),
                   vec, vec, pl.BlockSpec((1, BLOCK), lambda t: (0, 0)),
                   pl.BlockSpec((N_BUCKETS, BLOCK), lambda t: (0, 0))],
        out_shape=[SDS((s, SWA_Q_W), F32), SDS((s, 2 * SWA_KV_W), F32), SDS((1, HEAD_DIM), F32),
                   SDS((1, HEAD_DIM), F32), SDS((1, BLOCK), F32), SDS((N_BUCKETS, BLOCK), F32)],
        scratch_shapes=[pltpu.VMEM((SWA_KV_HEADS, SWA_STACK, BLOCK), F32),
                        pltpu.VMEM((SWA_KV_HEADS, SWA_STACK, BLOCK), F32),
                        pltpu.VMEM((2 * SWA_KV_HEADS, BLOCK, HEAD_DIM), F32)],
        compiler_params=_params("arbitrary"),
    )(p, p, p, qg, kg, sinks, rel_bias, jnp.asarray(_bucket_table()), dy_all)


def _mem_head(q, k, v, qg, kg):
    qn = _rms(q, qg)
    kn = _rms(k, kg)
    s = _mm(qn, kn, False, True) * (HEAD_DIM ** -0.5)
    m = lax.stop_gradient(jnp.max(s, axis=-1, keepdims=True))
    e = jnp.exp(s - m)
    return _mm(e / jnp.sum(e, axis=-1, keepdims=True), v)


def _mem_fwd(p, kv, qg, kg, *, tq, name):
    s = p.shape[0]
    m = kv.shape[0]

    def body(q_ref, kv_ref, qg_ref, kg_ref, y_ref):
        for h in range(MEM_HEADS):
            cols = slice(h * HEAD_DIM, (h + 1) * HEAD_DIM)
            vcols = slice(MEM_Q_W + h * HEAD_DIM, MEM_Q_W + (h + 1) * HEAD_DIM)
            y_ref[:, cols] = _mem_head(q_ref[:, cols], kv_ref[:, cols], kv_ref[:, vcols], qg_ref[...], kg_ref[...])

    vec = pl.BlockSpec((1, HEAD_DIM), lambda t: (0, 0))
    return pl.pallas_call(
        body, name=name, grid=(s // tq,),
        in_specs=[pl.BlockSpec((tq, MEM_Q_W), lambda t: (t, COL_MQ // MEM_Q_W)),
                  pl.BlockSpec((m, 2 * MEM_Q_W), lambda t: (0, 0)), vec, vec],
        out_specs=pl.BlockSpec((tq, MEM_Q_W), lambda t: (t, 0)), out_shape=SDS((s, MEM_Q_W), F32),
        compiler_params=_params("parallel"),
    )(p, kv, qg, kg)


def _mem_bwd(p, kv, qg, kg, dy_all, *, tq, name):
    s = p.shape[0]
    m = kv.shape[0]

    def body(q_ref, kv_ref, qg_ref, kg_ref, dy_ref, dq_ref, dkv_ref, dqg_ref, dkg_ref):
        @pl.when(pl.program_id(0) == 0)
        def _():
            dkv_ref[...] = jnp.zeros_like(dkv_ref)
            dqg_ref[...] = jnp.zeros_like(dqg_ref)
            dkg_ref[...] = jnp.zeros_like(dkg_ref)

        dqg = jnp.zeros((1, HEAD_DIM), F32)
        dkg = jnp.zeros((1, HEAD_DIM), F32)
        for h in range(MEM_HEADS):
            cols = slice(h * HEAD_DIM, (h + 1) * HEAD_DIM)
            vcols = slice(MEM_Q_W + h * HEAD_DIM, MEM_Q_W + (h + 1) * HEAD_DIM)
            _, vjp = jax.vjp(_mem_head, q_ref[:, cols], kv_ref[:, cols], kv_ref[:, vcols], qg_ref[...], kg_ref[...])
            dq, dk, dv, dqg_h, dkg_h = vjp(dy_ref[:, cols])
            dq_ref[:, cols] = dq
            dkv_ref[:, cols] += dk
            dkv_ref[:, vcols] += dv
            dqg += dqg_h
            dkg += dkg_h
        dqg_ref[...] += dqg
        dkg_ref[...] += dkg

    vec = pl.BlockSpec((1, HEAD_DIM), lambda t: (0, 0))
    full = pl.BlockSpec((m, 2 * MEM_Q_W), lambda t: (0, 0))
    dy_col = (SWA_Q_W + GLA_V_W) // MEM_Q_W
    return pl.pallas_call(
        body, name=name, grid=(s // tq,),
        in_specs=[pl.BlockSpec((tq, MEM_Q_W), lambda t: (t, COL_MQ // MEM_Q_W)), full, vec, vec,
                  pl.BlockSpec((tq, MEM_Q_W), lambda t: (t, dy_col))],
        out_specs=[pl.BlockSpec((tq, MEM_Q_W), lambda t: (t, 0)), full, vec, vec],
        out_shape=[SDS((s, MEM_Q_W), F32), SDS((m, 2 * MEM_Q_W), F32), SDS((1, HEAD_DIM), F32), SDS((1, HEAD_DIM), F32)],
        compiler_params=_params("arbitrary"),
    )(p, kv, qg, kg, dy_all)


GLA_ROWS = 256


GLA_GROUP = 4


def _gla_consts():
    c, h, r = GLA_CHUNK, GLA_HEADS, GLA_GROUP * GLA_CHUNK
    i2 = lax.broadcasted_iota(jnp.int32, (c, c), 0)
    j2 = lax.broadcasted_iota(jnp.int32, (c, c), 1)
    slab_q = lax.broadcasted_iota(jnp.int32, (h, r, GLA_QK_W), 0)
    lane_q = lax.broadcasted_iota(jnp.int32, (h, r, GLA_QK_W), 2)
    row_a = lax.broadcasted_iota(jnp.int32, (h * r, r), 0) % r
    col_a = lax.broadcasted_iota(jnp.int32, (h * r, r), 1)
    slab_o = lax.broadcasted_iota(jnp.int32, (h, r, GLA_V_W), 0)
    lane_o = lax.broadcasted_iota(jnp.int32, (h, r, GLA_V_W), 2)
    row_s = lax.broadcasted_iota(jnp.int32, (GLA_V_W, GLA_QK_W), 0)
    col_s = lax.broadcasted_iota(jnp.int32, (GLA_V_W, GLA_QK_W), 1)
    return dict(
        ltri=(j2 <= i2).astype(F32),
        m_q=(slab_q == lane_q // GLA_DK).astype(F32),
        causal=(col_a <= row_a) & (col_a // c == row_a // c),
        m_o=(slab_o == lane_o // GLA_DV).astype(F32),
        m_s=(row_s // GLA_DV == col_s // GLA_DK).astype(F32),
    )


def _gla_step(q, k, v, z, bg, st, c):
    h = GLA_HEADS
    kt, ka, qt, qe, decay = [], [], [], [], []
    for qc, kc, zc in zip(q, k, z):
        la = _log_sigmoid(zc + bg) * (1.0 / GLA_TAU)
        b = _mmf(c["ltri"], la)
        bl = jnp.sum(la, axis=0, keepdims=True)
        qs = qc * (GLA_DK ** -0.5)
        kt.append(kc * jnp.exp(bl - b))
        ka.append(kc * jnp.exp(0.5 * bl - b))
        qt.append(qs * jnp.exp(b - 0.5 * bl))
        qe.append(qs * jnp.exp(b))
        decay.append(jnp.exp(bl))
    o_intra = []
    rows = GLA_GROUP * GLA_CHUNK
    for lo in range(0, len(q), GLA_GROUP):
        qt_all, kt_all, v_all = (jnp.concatenate(parts[lo:lo + GLA_GROUP], axis=0) for parts in (qt, ka, v))
        q_stack = (jnp.broadcast_to(qt_all[None], (h, rows, GLA_QK_W)) * c["m_q"]).reshape(h * rows, GLA_QK_W)
        a = jnp.where(c["causal"], _mm3(q_stack, kt_all, False, True), 0.0)
        o_stack = _mm(a, v_all)
        o_intra.append(jnp.sum(o_stack.reshape(h, rows, GLA_V_W) * c["m_o"], axis=0))
    o_intra = jnp.concatenate(o_intra, axis=0)
    o_inter = []
    for qec, ktc, vc, dc in zip(qe, kt, v, decay):
        o_inter.append(_mm(qec, st, False, True))
        st = st * dc + _mm(vc, ktc, True, False) * c["m_s"]
    return o_intra + jnp.concatenate(o_inter, axis=0), st


def _gla_post(o, gg, gain, g64):
    ms = _mmf(o * o, g64) * (1.0 / GLA_DV)
    return o * lax.rsqrt(ms + EPS) * gain * jax.nn.silu(gg)


def _gla_g64():
    r = lax.broadcasted_iota(jnp.int32, (GLA_V_W, GLA_V_W), 0)
    c = lax.broadcasted_iota(jnp.int32, (GLA_V_W, GLA_V_W), 1)
    return (r // GLA_DV == c // GLA_DV).astype(F32)


def _gla_in_specs(order):
    r = GLA_ROWS
    return [
        pl.BlockSpec((r, GLA_QK_W), lambda t: (order(t), COL_GQ // GLA_QK_W)),
        pl.BlockSpec((r, GLA_QK_W), lambda t: (order(t), COL_GK // GLA_QK_W)),
        pl.BlockSpec((r, GLA_V_W), lambda t: (order(t), COL_GV // GLA_V_W)),
        pl.BlockSpec((r, GLA_V_W), lambda t: (order(t), COL_GG // GLA_V_W)),
        pl.BlockSpec((r, GLA_QK_W), lambda t: (order(t), 0)),
        pl.BlockSpec((1, GLA_QK_W), lambda t: (0, 0)),
        pl.BlockSpec((1, GLA_V_W), lambda t: (0, 0)),
    ]


def _gla_pieces(q_ref, k_ref, v_ref, z_ref, cps):
    chunk = lambda ref: [ref[ci * GLA_CHUNK:(ci + 1) * GLA_CHUNK, :] for ci in range(cps)]
    return chunk(q_ref), chunk(k_ref), chunk(v_ref), chunk(z_ref)


def _gla_fwd(p, z, bg, gain, *, name):
    s = p.shape[0]
    r = GLA_ROWS
    cps = r // GLA_CHUNK

    def body(q_ref, k_ref, v_ref, gg_ref, z_ref, bg_ref, gain_ref, y_ref, oraw_ref, stsave_ref, st_s):
        @pl.when(pl.program_id(0) == 0)
        def _():
            st_s[...] = jnp.zeros_like(st_s)

        st = st_s[...]
        stsave_ref[0] = st
        o, st = _gla_step(*_gla_pieces(q_ref, k_ref, v_ref, z_ref, cps), bg_ref[...], st, _gla_consts())
        oraw_ref[...] = o
        st_s[...] = st
        y_ref[...] = _gla_post(o, gg_ref[...], gain_ref[...], _gla_g64())

    rowv = pl.BlockSpec((r, GLA_V_W), lambda t: (t, 0))
    return pl.pallas_call(
        body, name=name, grid=(s // r,), in_specs=_gla_in_specs(lambda t: t),
        out_specs=[rowv, rowv, pl.BlockSpec((1, GLA_V_W, GLA_QK_W), lambda t: (t, 0, 0))],
        out_shape=[SDS((s, GLA_V_W), F32), SDS((s, GLA_V_W), F32), SDS((s // r, GLA_V_W, GLA_QK_W), F32)],
        scratch_shapes=[pltpu.VMEM((GLA_V_W, GLA_QK_W), F32)],
        compiler_params=_params("arbitrary"),
    )(p, p, p, p, z, bg, gain)


def _gla_bwd(p, z, bg, gain, oraw, stsave, dy_all, *, name):
    s = p.shape[0]
    r = GLA_ROWS
    cps = r // GLA_CHUNK
    nsteps = s // r
    w_qkvg = 2 * GLA_QK_W + 2 * GLA_V_W

    def body(q_ref, k_ref, v_ref, gg_ref, z_ref, bg_ref, gain_ref, oraw_ref, stsave_ref, dy_ref,
             dqkvg_ref, dz_ref, dbg_ref, dgain_ref, dst_s):
        @pl.when(pl.program_id(0) == 0)
        def _():
            dst_s[...] = jnp.zeros_like(dst_s)
            dbg_ref[...] = jnp.zeros_like(dbg_ref)
            dgain_ref[...] = jnp.zeros_like(dgain_ref)

        _, vjp = jax.vjp(functools.partial(_gla_post, g64=_gla_g64()), oraw_ref[...], gg_ref[...], gain_ref[...])
        do, dgg, dgain = vjp(dy_ref[...])
        dqkvg_ref[:, 2 * GLA_QK_W + GLA_V_W:] = dgg
        dgain_ref[...] += dgain
        _, vjp = jax.vjp(functools.partial(_gla_step, c=_gla_consts()), *_gla_pieces(q_ref, k_ref, v_ref, z_ref, cps),
                         bg_ref[...], stsave_ref[0])
        dq, dk, dv, dz, dbg, dst = vjp((do, dst_s[...]))
        for ci in range(cps):
            rows = slice(ci * GLA_CHUNK, (ci + 1) * GLA_CHUNK)
            dqkvg_ref[rows, 0:GLA_QK_W] = dq[ci]
            dqkvg_ref[rows, GLA_QK_W:2 * GLA_QK_W] = dk[ci]
            dqkvg_ref[rows, 2 * GLA_QK_W:2 * GLA_QK_W + GLA_V_W] = dv[ci]
            dz_ref[rows, :] = dz[ci]
        dst_s[...] = dst
        dbg_ref[...] += dbg

    order = lambda t: nsteps - 1 - t
    rowv = pl.BlockSpec((r, GLA_V_W), lambda t: (order(t), 0))
    return pl.pallas_call(
        body, name=name, grid=(nsteps,),
        in_specs=_gla_in_specs(order) + [
            rowv, pl.BlockSpec((1, GLA_V_W, GLA_QK_W), lambda t: (order(t), 0, 0)),
            pl.BlockSpec((r, GLA_V_W), lambda t: (order(t), SWA_Q_W // GLA_V_W))],
        out_specs=[pl.BlockSpec((r, w_qkvg), lambda t: (order(t), 0)), pl.BlockSpec((r, GLA_QK_W), lambda t: (order(t), 0)),
                   pl.BlockSpec((1, GLA_QK_W), lambda t: (0, 0)), pl.BlockSpec((1, GLA_V_W), lambda t: (0, 0))],
        out_shape=[SDS((s, w_qkvg), F32), SDS((s, GLA_QK_W), F32), SDS((1, GLA_QK_W), F32), SDS((1, GLA_V_W), F32)],
        scratch_shapes=[pltpu.VMEM((GLA_V_W, GLA_QK_W), F32)],
        compiler_params=_params("arbitrary"),
    )(p, p, p, p, z, bg, gain, oraw, stsave, dy_all)


def _local_step(x, mem, target, small, big, on_grads):
    g1, gmix, gmem, g2, sqg, skg, sinks, rel_bias, wgu, bg, gla_gain, mqg, mkg = small
    (w3_1a, w3_1b), gather_mix, gather_ffn2 = big
    wgu_pad = jnp.zeros((GLA_QK_W, GLA_QK_W), BF16).at[:GLA_RANK].set(wgu.astype(BF16))
    gain256 = jnp.tile(gla_gain, (1, GLA_HEADS))

    (part,), saved1a = _ffn_fwd(x, g1, w3_1a, "ffn1a", partial=True)
    win_p, wkv, wout = gather_mix(part)
    (x1, h), saved1b = _ffn_fwd(x, g1, w3_1b, "ffn1b", next_gain=gmix, start=(saved1a[0], part))
    w3_2 = gather_ffn2((wout, x1))
    p = _matmul(h, win_p, tm=512, tn=IN_W_PAD, name="mix_in")
    hm = _rms_fwd(mem, gmem, tm=256, name="mem_rms")
    kv = _matmul(hm, wkv, tm=256, tn=512, name="mem_kv")
    p_glr = p[:, COL_GLR:]
    z = _matmul(p_glr, wgu_pad, tm=1024, tn=GLA_QK_W, name="gla_gate")
    y_swa = _swa_fwd(p, sqg, skg, sinks, rel_bias, name="swa_fwd")
    y_gla, oraw, stsave = _gla_fwd(p, z, bg, gain256, name="gla_fwd")
    y_mem = _mem_fwd(p, kv, mqg, mkg, tq=512, name="mem_fwd")
    x2 = _matmul([y_swa, y_gla, y_mem], wout, b_blocks=[0, 2, 3], tm=512, tn=1024, res=x1, name="mix_out")
    (dy, dyb, loss), saved2 = _ffn_fwd(x2, g2, w3_2, "ffn2", target=target)

    dh2, dw3_2 = _ffn_bwd_part(dyb, w3_2, saved2, 0, w3_2.shape[1] // FFN_TN, None, name="ffn2_bwd")
    dx2, dx2b, dg2 = _rms_bwd(x2, g2, dh2, dy, tm=512, name="ffn2_drms")
    dx2b = on_grads("ffn2", [dw3_2], dx2b)
    dy_all = _matmul(dx2b, wout, tb=True, tm=512, tn=1024, name="mix_dy")
    dwout = _matmul(jnp.concatenate([y_swa, y_gla, y_mem], axis=1), dx2b, ta=True, tm=512, tn=1024, out_dtype=BF16,
                    name="mix_dw_out")
    dq_swa, dkv_swa, dsqg, dskg, dsink, drb = _swa_bwd(p, sqg, skg, sinks, rel_bias, dy_all, name="swa_bwd")
    dqkvg, dz, dbg, dgain256 = _gla_bwd(p, z, bg, gain256, oraw, stsave, dy_all, name="gla_bwd")
    dmq, dkv_mem, dmqg, dmkg = _mem_bwd(p, kv, mqg, mkg, dy_all, tq=512, name="mem_bwd")
    dglr = _matmul(dz, wgu_pad, tb=True, tm=1024, tn=GLA_QK_W, name="gla_gate_dx")
    dwgu_pad = _matmul(p_glr, dz, ta=True, tm=GLA_QK_W, tn=GLA_QK_W, name="gla_gate_dw")
    dp = jnp.concatenate([dq_swa, dkv_swa, dqkvg, dmq, dglr], axis=1)
    dwin_p = _matmul(h, dp, ta=True, tm=1024, tn=640, out_dtype=BF16, name="mix_dw_in")
    dx1, dx1b, dgmix = _rms_bwd(x1, gmix, (dp, win_p), dx2, tm=512, name="mix_dh_drms")
    dwkv = _matmul(hm, dkv_mem, ta=True, tm=512, tn=512, out_dtype=BF16, name="mem_dw_kv")
    dx1b = on_grads("mix", (dwin_p, dwkv, dwout), dx1b)
    _, _, dgmem = _rms_bwd(mem, gmem, (dkv_mem, wkv), None, tm=256, name="mem_dh_drms")
    dh1, dw3_1a = _ffn_bwd_part(dx1b, w3_1a, saved1a, 0, w3_1a.shape[1] // FFN_TN, None, name="ffn1_bwd_a")
    dgla_gain = dgain256.reshape(GLA_HEADS, GLA_DV).sum(axis=0, keepdims=True)
    dsmall = [dgmix, dgmem, dg2, dsqg, dskg, dsink[:, :SWA_HEADS], drb[:, :SWA_HEADS], dwgu_pad[:GLA_RANK], dbg,
              dgla_gain, dmqg, dmkg, loss]
    dh1, dgmem, dwgu_pad = on_grads("ffn1a", [dw3_1a], (dh1, dgmem, dwgu_pad), small=dsmall)
    dh1, dw3_1b = _ffn_bwd_part(dx1b, w3_1b, saved1b, 0, w3_1b.shape[1] // FFN_TN, dh1, name="ffn1_bwd_b")
    dx, _, dg1 = _rms_bwd(x, g1, dh1, dx1, tm=512, name="ffn1_drms")
    on_grads("ffn1b", [dw3_1b], None, small=[dg1])
    return dx


def _mesh_place():
    x, y, c = lax.axis_index("x"), lax.axis_index("y"), lax.axis_index("c")
    other_chips = [(1 - x, y), (x, 1 - y), (1 - x, 1 - y)]
    return x, y, c, other_chips


def _handshake(peers):
    barrier = pltpu.get_barrier_semaphore()
    for peer in peers:
        pl.semaphore_signal(barrier, inc=1, device_id=peer, device_id_type=MESH)
    pl.semaphore_wait(barrier, len(peers))


def _sequencer_call(body, operands, out_shapes, sems, *, name, collective_id):
    return pl.kernel(
        body, name=name, out_type=out_shapes, mesh=plsc.ScalarSubcoreMesh(axis_name="sequencer", num_cores=1),
        scratch_types=sems, compiler_params=pltpu.CompilerParams(collective_id=collective_id),
    )(*operands)


def _window(ref, kind, slot, shape):
    if kind == "row":
        rows = pl.ds(pl.multiple_of(slot * shape[-2], 8), shape[-2])
        return ref.at[(slice(None),) * (len(shape) - 2) + (rows,)]
    return ref.at[slot]


def _gathered(shape, kind):
    if kind == "row":
        return tuple(shape[:-2]) + (N_DEV * shape[-2], shape[-1])
    return (N_DEV,) + tuple(shape)


def _half(view, hf):
    if len(view.shape) == 4:
        return view.at[:, hf]
    n = view.shape[-2] // 2
    return view.at[(slice(None),) * (len(view.shape) - 2) + (pl.ds(hf * n, n),)]


def _all_gather(shards, kinds, *, name, collective_id):
    nt = len(shards)

    def body(*refs):
        x_refs, o_refs = refs[:nt], refs[nt:2 * nt]
        send_sems, recv_sems, local_sems = refs[2 * nt:]
        x, y, c, _ = _mesh_place()
        me, sibling, xn, yn, diag = (x, y, c), (x, y, 1 - c), (1 - x, y, c), (x, 1 - y, c), (1 - x, 1 - y, c)
        _handshake([sibling, xn, yn])

        def win(t, block):
            bx, by, bc = block
            return _window(o_refs[t], kinds[t], 4 * bx + 2 * by + bc, shards[t].shape)

        def copy(k, t, src, dst, to):
            return pltpu.make_async_remote_copy(src_ref=src, dst_ref=dst, send_sem=send_sems.at[k, t],
                                                recv_sem=recv_sems.at[k, t], device_id=to, device_id_type=MESH)

        def piece(k, t, block, hf, to, from_shard=False):
            dst = _half(win(t, block), hf)
            return copy(k, t, _half(x_refs[t], hf) if from_shard else dst, dst, to)

        mine = [pltpu.make_async_copy(x_refs[t], win(t, me), local_sems.at[t]) for t in range(nt)]
        sent = []

        def start(cp):
            cp.start()
            sent.append(cp)

        for cp in mine:
            cp.start()
        for t in range(nt):
            start(copy(0, t, x_refs[t], win(t, me), sibling))
        for hf_x, hf_y in ((0, 1), (1, 0)):
            for t in range(nt):
                start(piece(1 + hf_x, t, me, hf_x, xn, True))
                start(piece(3 + hf_y, t, me, hf_y, yn, True))
        for k, block, hf, onward, k_sib in ((1, xn, 0, (5, yn), 7), (4, yn, 1, (6, xn), 10), (2, xn, 1, None, 8),
                                           (3, yn, 0, None, 9), (5, diag, 0, None, 11), (6, diag, 1, None, 12)):
            for t in range(nt):
                piece(k, t, block, hf, me).wait_recv()
                if onward is not None:
                    start(piece(onward[0], t, block, hf, onward[1]))
                start(piece(k_sib, t, block, hf, sibling))
        for t in range(nt):
            copy(0, t, x_refs[t], win(t, sibling), me).wait_recv()
        for k_sib, block, hf in ((7, xn, 0), (10, yn, 1), (8, xn, 1), (9, yn, 0), (11, diag, 0), (12, diag, 1)):
            for t in range(nt):
                bx, by, _ = block
                piece(k_sib, t, (bx, by, 1 - c), hf, me).wait_recv()
        for cp in sent:
            cp.wait_send()
        for cp in mine:
            cp.wait()

    return _sequencer_call(
        body, shards, [SDS(_gathered(s.shape, k), s.dtype) for s, k in zip(shards, kinds)],
        [pltpu.SemaphoreType.DMA((13, nt)), pltpu.SemaphoreType.DMA((13, nt)), pltpu.SemaphoreType.DMA((nt,))],
        name=name, collective_id=collective_id)


def _part_shape(shape, kind):
    if kind == "row":
        return tuple(shape[:-2]) + (shape[-2] // N_DEV, shape[-1])
    return tuple(shape[2:])


def _pair_exchange(grads, kinds, *, name, collective_id):
    nt = len(grads)
    part = [_part_shape(g.shape, k) for g, k in zip(grads, kinds)]

    def body(*refs):
        g_refs, o_refs = refs[:nt], refs[nt:2 * nt]
        send_sems, recv_sems = refs[2 * nt:]
        x, y, c, _ = _mesh_place()
        _handshake([(x, y, 1 - c)])
        copies = []
        for t in range(nt):
            for xy in range(4):
                src = g_refs[t].at[1 - c, xy] if kinds[t] == "stack" else _window(g_refs[t], kinds[t], 2 * xy + 1 - c, part[t])
                copies.append(pltpu.make_async_remote_copy(
                    src_ref=src, dst_ref=o_refs[t].at[xy], send_sem=send_sems.at[xy, t], recv_sem=recv_sems.at[xy, t],
                    device_id=(x, y, 1 - c), device_id_type=MESH))
        for cp in copies:
            cp.start()
        for cp in copies:
            cp.wait()

    return _sequencer_call(
        body, grads, [SDS((4,) + p, g.dtype) for p, g in zip(part, grads)],
        [pltpu.SemaphoreType.DMA((4, nt)), pltpu.SemaphoreType.DMA((4, nt))], name=name, collective_id=collective_id)


def _chip_exchange(parts, small, *, name, collective_id):
    nt = len(parts)
    if small is None:
        def body_plain(*refs):
            s_refs, o_refs = refs[:nt], refs[nt:2 * nt]
            send_sems, recv_sems = refs[2 * nt:]
            x, y, c, chips = _mesh_place()
            _handshake([(*chip, c) for chip in chips])
            copies = [pltpu.make_async_remote_copy(
                src_ref=s_refs[t].at[2 * chip[0] + chip[1]], dst_ref=o_refs[t].at[j],
                send_sem=send_sems.at[j, t], recv_sem=recv_sems.at[j, t], device_id=(*chip, c), device_id_type=MESH)
                for j, chip in enumerate(chips) for t in range(nt)]
            for cp in copies:
                cp.start()
            for cp in copies:
                cp.wait()

        return _sequencer_call(
            body_plain, parts, [SDS((3,) + s.shape[1:], s.dtype) for s in parts],
            [pltpu.SemaphoreType.DMA((3, nt)), pltpu.SemaphoreType.DMA((3, nt))], name=name, collective_id=collective_id)

    def body(*refs):
        s_refs, small_ref = refs[:nt], refs[nt]
        o_refs, small_all = refs[nt + 1:2 * nt + 1], refs[2 * nt + 1]
        send_sems, recv_sems, small_send, small_recv, local_sem = refs[2 * nt + 2:]
        x, y, c, chips = _mesh_place()
        _handshake([(px, py, pc) for px in (x, 1 - x) for py in (y, 1 - y) for pc in (c, 1 - c)][1:])

        def copy(j, t, chip):
            return pltpu.make_async_remote_copy(
                src_ref=s_refs[t].at[2 * chip[0] + chip[1]], dst_ref=o_refs[t].at[j],
                send_sem=send_sems.at[j, t], recv_sem=recv_sems.at[j, t], device_id=(*chip, c), device_id_type=MESH)

        flips = [(fx, fy, fc) for fx in (0, 1) for fy in (0, 1) for fc in (0, 1)][1:]

        def small_copy(k):
            fx, fy, fc = flips[k]
            to = (x ^ fx if fx else x, y ^ fy if fy else y, c ^ fc if fc else c)
            rows = small_all.at[4 * x + 2 * y + c]
            return pltpu.make_async_remote_copy(
                src_ref=small_ref, dst_ref=rows, send_sem=small_send.at[k], recv_sem=small_recv.at[k],
                device_id=to, device_id_type=MESH)

        own = pltpu.make_async_copy(small_ref, small_all.at[4 * x + 2 * y + c], local_sem)
        own.start()
        copies = [copy(j, t, chip) for j, chip in enumerate(chips) for t in range(nt)]
        smalls = [small_copy(k) for k in range(7)]
        for cp in smalls + copies:
            cp.start()
        for cp in smalls + copies:
            cp.wait()
        own.wait()

    return _sequencer_call(
        body, list(parts) + [small],
        [SDS((3,) + s.shape[1:], s.dtype) for s in parts] + [SDS((N_DEV,) + small.shape, small.dtype)],
        [pltpu.SemaphoreType.DMA((3, nt)), pltpu.SemaphoreType.DMA((3, nt)),
         pltpu.SemaphoreType.DMA((7,)), pltpu.SemaphoreType.DMA((7,)), pltpu.SemaphoreType.DMA],
        name=name, collective_id=collective_id)


def _pair_sum(grad, theirs, kind, c, *, name):
    if kind == "row":
        r, l = theirs.shape[-2:]
        n = theirs.size // (4 * r * l)
        grad, theirs = grad.reshape(n, N_DEV * r, l), theirs.reshape(4, n, r, l)
        mine_spec = pl.BlockSpec((n, r, l), lambda xy, c_ref: (0, 2 * xy + c_ref[0], 0))
    else:
        r, l = theirs.shape[-2:]
        n = theirs.size // (4 * r * l)
        theirs = theirs.reshape(4, n, r, l)
        grad = grad.reshape(2, 4, n, r, l)
        mine_spec = pl.BlockSpec((None, None, n, r, l), lambda xy, c_ref: (c_ref[0], xy, 0, 0, 0))

    def body(c_ref, a_ref, b_ref, o_ref):
        o_ref[...] = (a_ref[...].astype(F32) + b_ref[...].astype(F32)).astype(BF16)

    part = pl.BlockSpec((None, n, r, l), lambda xy, c_ref: (xy, 0, 0, 0))
    return pl.pallas_call(
        body, name=name,
        grid_spec=pltpu.PrefetchScalarGridSpec(num_scalar_prefetch=1, grid=(4,), in_specs=[mine_spec, part], out_specs=part),
        out_shape=SDS((4, n, r, l), BF16), compiler_params=_params("parallel"),
    )(c, grad, theirs)


def _adamw(w, g, m, v):
    m = ADAM_B1 * m + (1.0 - ADAM_B1) * g
    v = ADAM_B2 * v + (1.0 - ADAM_B2) * jnp.square(g)
    m_hat = m / (1.0 - ADAM_B1 ** ADAM_STEP)
    v_hat = v / (1.0 - ADAM_B2 ** ADAM_STEP)
    delta = -ADAM_LR * (m_hat / (jnp.sqrt(v_hat) + ADAM_EPS) + ADAM_WD * w)
    return delta, m, v


def _adam_big(owns, others, mat, xy, w, m, v, *, tr, name):
    _, r, l = w.shape
    lp = owns[0].shape[-1]
    nq = len(owns)
    rows = [tr] if nq == 1 else [o.shape[-2] for o in owns]
    assert sum(rows) == tr and r % tr == 0, (r, tr, rows)

    def body(xy_ref, *refs):
        own_refs, oth_refs = refs[:nq], refs[nq:2 * nq]
        w_ref, m_ref, v_ref, g_out, d_out, m_out, v_out = refs[2 * nq:]
        parts = []
        for q in range(nq):
            gq = own_refs[q][0, 0].astype(F32)
            for j in range(3):
                gq = gq + oth_refs[q][j, 0].astype(F32)
            parts.append(gq)
        g = (parts[0] if nq == 1 else jnp.concatenate(parts, axis=0))[:, :l]
        delta, m_new, v_new = _adamw(w_ref[0], g, m_ref[0], v_ref[0])
        g_out[0] = g
        d_out[0] = delta
        m_out[0] = m_new
        v_out[0] = v_new

    blk = pl.BlockSpec((1, tr, l), lambda i, xy_ref: (0, i, 0))
    own_specs = [pl.BlockSpec((1, 1, n, lp), lambda i, xy_ref: (xy_ref[0], mat, i, 0)) for n in rows]
    oth_specs = [pl.BlockSpec((3, 1, n, lp), lambda i, xy_ref: (0, mat, i, 0)) for n in rows]
    return pl.pallas_call(
        body, name=name,
        grid_spec=pltpu.PrefetchScalarGridSpec(
            num_scalar_prefetch=1, grid=(r // tr,), in_specs=own_specs + oth_specs + [blk, blk, blk],
            out_specs=[blk, blk, blk, blk]),
        out_shape=[SDS(w.shape, F32)] * 4, compiler_params=_params("parallel"),
    )(xy, *owns, *others, w, m, v)


def _adam_cols(own, other, xy, w, m, v, *, tc, name):
    _, wc, d = w.shape
    wp = -(-wc // 128) * 128

    def body(xy_ref, own_ref, oth_ref, w_ref, m_ref, v_ref, g_out, d_out, m_out, v_out, buf):
        g = own_ref[0, 0].astype(F32)
        for j in range(3):
            g = g + oth_ref[j, 0].astype(F32)
        buf[:, :wc] = g
        buf[:, wc:] = jnp.zeros((tc, wp - wc), F32)
        g = buf[...].T[:wc]
        delta, m_new, v_new = _adamw(w_ref[0], g, m_ref[0], v_ref[0])
        g_out[0] = g
        d_out[0] = delta
        m_out[0] = m_new
        v_out[0] = v_new

    blk = pl.BlockSpec((1, wc, tc), lambda i, xy_ref: (0, 0, i))
    in_specs = [pl.BlockSpec((1, 1, tc, wc), lambda i, xy_ref: (xy_ref[0], 0, i, 0)),
                pl.BlockSpec((3, 1, tc, wc), lambda i, xy_ref: (0, 0, i, 0)), blk, blk, blk]
    return pl.pallas_call(
        body, name=name,
        grid_spec=pltpu.PrefetchScalarGridSpec(
            num_scalar_prefetch=1, grid=(d // tc,), in_specs=in_specs, out_specs=[blk, blk, blk, blk],
            scratch_shapes=[pltpu.VMEM((tc, wp), F32)]),
        out_shape=[SDS(w.shape, F32)] * 4, compiler_params=_params("parallel"),
    )(xy, own, other, w, m, v)


def _small_layout(shapes):
    out, at = [], 0
    for r, c in shapes:
        rows = c // 128 if (r == 1 and c > 128) else r
        out.append((at, rows))
        at += -(-rows // 8) * 8
    return out, at


def _pack_small(parts, *, name):
    shapes = [a.shape for a in parts]
    layout, total = _small_layout(shapes)

    def body(*refs):
        o_ref = refs[-1]
        o_ref[...] = jnp.zeros_like(o_ref)
        for x_ref, (r, c), (at, rows) in zip(refs, shapes, layout):
            if r == 1 and c > 128:
                for k in range(rows):
                    o_ref[at + k:at + k + 1, :] = x_ref[:, k * 128:(k + 1) * 128]
            else:
                o_ref[at:at + r, 0:c] = x_ref[...]

    return pl.pallas_call(body, name=name, out_shape=SDS((total, 128), F32))(*parts)


def _adam_small(g_all, ws, ms, vs, *, name):
    n = len(ws)
    shapes = [w.shape for w in ws]
    layout, _ = _small_layout(shapes)

    def body(g_ref, *refs):
        w_refs, m_refs, v_refs, outs = refs[:n], refs[n:2 * n], refs[2 * n:3 * n], refs[3 * n:]
        g_sum = g_ref[0]
        for k in range(1, N_DEV):
            g_sum = g_sum + g_ref[k]
        for i, ((r, c), (at, rows)) in enumerate(zip(shapes, layout)):
            if r == 1 and c > 128:
                g = jnp.concatenate([g_sum[at + k:at + k + 1, :] for k in range(rows)], axis=1)
            else:
                g = g_sum[at:at + r, 0:c]
            delta, m_new, v_new = _adamw(w_refs[i][...], g, m_refs[i][...], v_refs[i][...])
            for q, val in enumerate((g, delta, m_new, v_new)):
                outs[4 * i + q][...] = val

    flat = pl.pallas_call(body, name=name, out_shape=[SDS(s, F32) for s in shapes for _ in range(4)])(g_all, *ws, *ms, *vs)
    return [flat[4 * i:4 * i + 4] for i in range(n)]


def kernel(x, mem, ffn1_norm, ffn1_w_gate, ffn1_w_up, ffn1_w_down, mix_norm, mem_norm, w_in, w_mem_kv, swa_q_norm, swa_k_norm, swa_sinks, rel_bias, gla_w_gate_up, gla_b_gate, gla_out_norm, mem_q_norm, mem_k_norm, w_out, ffn2_norm, ffn2_w_gate, ffn2_w_up, ffn2_w_down, loss_target, m_ffn1_norm, m_ffn1_w_gate, m_ffn1_w_up, m_ffn1_w_down, m_mix_norm, m_mem_norm, m_w_in, m_w_mem_kv, m_swa_q_norm, m_swa_k_norm, m_swa_sinks, m_rel_bias, m_gla_w_gate_up, m_gla_b_gate, m_gla_out_norm, m_mem_q_norm, m_mem_k_norm, m_w_out, m_ffn2_norm, m_ffn2_w_gate, m_ffn2_w_up, m_ffn2_w_down, v_ffn1_norm, v_ffn1_w_gate, v_ffn1_w_up, v_ffn1_w_down, v_mix_norm, v_mem_norm, v_w_in, v_w_mem_kv, v_swa_q_norm, v_swa_k_norm, v_swa_sinks, v_rel_bias, v_gla_w_gate_up, v_gla_b_gate, v_gla_out_norm, v_mem_q_norm, v_mem_k_norm, v_w_out, v_ffn2_norm, v_ffn2_w_gate, v_ffn2_w_up, v_ffn2_w_down):
    xi, yi, ci = lax.axis_index("x"), lax.axis_index("y"), lax.axis_index("c")
    c_arr = jnp.reshape(ci, (1,)).astype(jnp.int32)
    xy_arr = jnp.reshape(2 * xi + yi, (1,)).astype(jnp.int32)
    d = x.shape[-1]

    def ffn_shards(wg_s, wu_s, wd_s):
        return jnp.concatenate([wg_s.transpose(0, 2, 1), wu_s.transpose(0, 2, 1), wd_s], axis=0).astype(BF16)

    def gather_ffn(wg_s, wu_s, wd_s, name, collective_id, after):
        w3_s, _ = lax.optimization_barrier((ffn_shards(wg_s, wu_s, wd_s), after))
        return _all_gather([w3_s], ["row"], name=name, collective_id=collective_id)[0]

    w3_s = ffn_shards(ffn1_w_gate, ffn1_w_up, ffn1_w_down)
    w3_1a = _all_gather([w3_s[:, :FFN1_FIRST]], ["row"], name="gather_ffn1a", collective_id=0)[0]
    w3_s, _ = lax.optimization_barrier((w3_s, w3_1a))
    w3_1b = _all_gather([w3_s[:, FFN1_FIRST:]], ["row"], name="gather_ffn1b", collective_id=11)[0]

    def gather_mix(after):
        mix_s = lax.optimization_barrier((w_in[0].astype(BF16), w_mem_kv[0].astype(BF16), w_out[0].astype(BF16),
                                          (w3_1b, after)))[:3]
        win_all, wkv, wout = _all_gather(list(mix_s), ["stack", "row", "row"], name="gather_mix", collective_id=1)
        return _pack_win(win_all, tr=256, name="pack_w_in"), wkv, wout

    def gather_ffn2(after):
        return gather_ffn(ffn2_w_gate, ffn2_w_up, ffn2_w_down, "gather_ffn2", 2, after)

    small_w = [ffn1_norm, mix_norm, mem_norm, ffn2_norm, swa_q_norm, swa_k_norm, swa_sinks[0], rel_bias,
               gla_w_gate_up[0], gla_b_gate, gla_out_norm, mem_q_norm, mem_k_norm]
    collective_ids = {"ffn2": (3, 4), "mix": (5, 6), "ffn1a": (7, 8), "ffn1b": (9, 10)}
    reduced, small_box = {}, {}

    def on_grads(group, grads, carry, small=None):
        if group == "mix":
            dwin_p, dwkv, dwout = grads
            grads = [_unpack_win(dwin_p, tr=256, name="unpack_dw_in"), dwkv, dwout]
            kinds = ["stack", "row", "row"]
        else:
            kinds = ["row"]
        if reduced:
            earlier = list(reduced.values())[-1][1]
            *grads, _ = lax.optimization_barrier((*grads, earlier[0]))
        id_pair, id_chip = collective_ids[group]
        from_sibling = _pair_exchange(grads, kinds, name=f"pair_exchange_{group}", collective_id=id_pair)
        chip_sums = [_pair_sum(g, theirs, k, c_arr, name=f"pair_sum_{group}_{t}")
                     for t, (g, theirs, k) in enumerate(zip(grads, from_sibling, kinds))]
        if carry is not None:
            *chip_sums, carry = lax.optimization_barrier((*chip_sums, carry))
        if small is None:
            from_chips = _chip_exchange(chip_sums, None, name=f"chip_exchange_{group}", collective_id=id_chip)
        else:
            packed = _pack_small(small, name=f"pack_small_{group}")
            *from_chips, small_all = _chip_exchange(chip_sums, packed, name=f"chip_exchange_{group}",
                                                    collective_id=id_chip)
            small_box[group] = small_all
        reduced[group] = (chip_sums, from_chips)
        return carry

    grad_x = _local_step(x[0], mem[0], loss_target[0], small_w, ((w3_1a, w3_1b), gather_mix, gather_ffn2), on_grads)

    big_w = {"ffn1_w_gate": ("ffn1", 0, 0, True, ffn1_w_gate, m_ffn1_w_gate, v_ffn1_w_gate),
             "ffn1_w_up": ("ffn1", 0, 1, True, ffn1_w_up, m_ffn1_w_up, v_ffn1_w_up),
             "ffn1_w_down": ("ffn1", 0, 2, False, ffn1_w_down, m_ffn1_w_down, v_ffn1_w_down),
             "w_in": ("mix", 0, 0, True, w_in, m_w_in, v_w_in),
             "w_mem_kv": ("mix", 1, 0, False, w_mem_kv, m_w_mem_kv, v_w_mem_kv),
             "w_out": ("mix", 2, 0, False, w_out, m_w_out, v_w_out),
             "ffn2_w_gate": ("ffn2", 0, 0, True, ffn2_w_gate, m_ffn2_w_gate, v_ffn2_w_gate),
             "ffn2_w_up": ("ffn2", 0, 1, True, ffn2_w_up, m_ffn2_w_up, v_ffn2_w_up),
             "ffn2_w_down": ("ffn2", 0, 2, False, ffn2_w_down, m_ffn2_w_down, v_ffn2_w_down)}
    res = {}
    for nm, (group, t, mat, transposed, w, m, v) in big_w.items():
        shape = w.shape
        if transposed:
            w, m, v = (a.transpose(0, 2, 1) for a in (w, m, v))
        if nm == "w_in":
            out = _adam_cols(reduced[group][0][t], reduced[group][1][t], xy_arr, w, m, v, tc=256, name=f"adam_{nm}")
            res[nm] = [a.transpose(0, 2, 1) for a in out]
            continue
        r = w.shape[1]
        halves = ["ffn1a", "ffn1b"] if group == "ffn1" else [group]
        if len(halves) == 1:
            tr = 256 if r % 256 == 0 else r
        else:
            tr = r
        out = _adam_big([reduced[k][0][t] for k in halves], [reduced[k][1][t] for k in halves], mat, xy_arr, w, m, v,
                        tr=tr, name=f"adam_{nm}")
        if transposed:
            out = [a.reshape(1, -1, d).transpose(0, 2, 1) for a in out]
        res[nm] = [a.reshape(shape) for a in out]
    small_names = ["ffn1_norm", "mix_norm", "mem_norm", "ffn2_norm", "swa_q_norm", "swa_k_norm", "swa_sinks", "rel_bias",
                   "gla_w_gate_up", "gla_b_gate", "gla_out_norm", "mem_q_norm", "mem_k_norm"]
    small_m = [m_ffn1_norm, m_mix_norm, m_mem_norm, m_ffn2_norm, m_swa_q_norm, m_swa_k_norm, m_swa_sinks, m_rel_bias,
               m_gla_w_gate_up, m_gla_b_gate, m_gla_out_norm, m_mem_q_norm, m_mem_k_norm]
    small_v = [v_ffn1_norm, v_mix_norm, v_mem_norm, v_ffn2_norm, v_swa_q_norm, v_swa_k_norm, v_swa_sinks, v_rel_bias,
               v_gla_w_gate_up, v_gla_b_gate, v_gla_out_norm, v_mem_q_norm, v_mem_k_norm]
    small_full = [ffn1_norm, mix_norm, mem_norm, ffn2_norm, swa_q_norm, swa_k_norm, swa_sinks, rel_bias,
                  gla_w_gate_up, gla_b_gate, gla_out_norm, mem_q_norm, mem_k_norm]
    zero = jnp.zeros((1, 1), F32)
    two_d = lambda a: a.reshape(a.shape[-2:])
    for group, sel in (("ffn1a", slice(1, None)), ("ffn1b", slice(0, 1))):
        extra = [zero] if group == "ffn1a" else []
        ws, ms, vs = ([two_d(a) for a in arrs[sel]] + extra for arrs in (small_full, small_m, small_v))
        updated = _adam_small(small_box[group], ws, ms, vs, name=f"adam_small_{group}")
        for nm, full, out in zip(small_names[sel], small_full[sel], updated):
            res[nm] = [a.reshape(full.shape) for a in out]
        if extra:
            loss = updated[-1][0].reshape(())

    order = ["ffn1_norm", "ffn1_w_gate", "ffn1_w_up", "ffn1_w_down", "mix_norm", "mem_norm", "w_in", "w_mem_kv",
             "swa_q_norm", "swa_k_norm", "swa_sinks", "rel_bias", "gla_w_gate_up", "gla_b_gate", "gla_out_norm",
             "mem_q_norm", "mem_k_norm", "w_out", "ffn2_norm", "ffn2_w_gate", "ffn2_w_up", "ffn2_w_down"]
    outs = [loss, grad_x[None]]
    for q in range(4):
        outs += [res[nm][q] for nm in order]
    return tuple(outs)
```

```python
import functools
import math

import numpy as np
import jax
import jax.numpy as jnp
from jax import lax
from jax.experimental import pallas as pl
from jax.experimental.pallas import tpu as pltpu
from jax.experimental.pallas import tpu_sc as plsc

F32 = jnp.float32
BF16 = jnp.bfloat16
SDS = jax.ShapeDtypeStruct

EPS = 1e-6
HEAD_DIM = 64
SWA_HEADS = 8
SWA_KV_HEADS = 2
SWA_GROUP = SWA_HEADS // SWA_KV_HEADS
BLOCK = 128
N_BUCKETS = 32
MAX_DISTANCE = 128
GLA_HEADS = 4
GLA_DK = 32
GLA_DV = 64
GLA_RANK = 16
GLA_TAU = 16.0
GLA_CHUNK = 32
MEM_HEADS = 4
SWA_Q_W = SWA_HEADS * HEAD_DIM
SWA_KV_W = SWA_KV_HEADS * HEAD_DIM
GLA_QK_W = GLA_HEADS * GLA_DK
GLA_V_W = GLA_HEADS * GLA_DV
MEM_Q_W = MEM_HEADS * HEAD_DIM
IN_W = 1808
IN_W_PAD = 1920
COL_SQ, COL_SKV, COL_GQ, COL_GK, COL_GV, COL_GG, COL_MQ, COL_GLR = 0, 512, 768, 896, 1024, 1280, 1536, 1792

ADAM_LR = 0.001
ADAM_B1 = 0.9
ADAM_B2 = 0.999
ADAM_EPS = 1e-08
ADAM_WD = 0.01
ADAM_STEP = 10

N_DEV = 8
VMEM_LIMIT_BYTES = 56 * 1024 * 1024
MESH = pl.DeviceIdType.MESH


def _params(*sem):
    return pltpu.CompilerParams(dimension_semantics=sem or None, vmem_limit_bytes=VMEM_LIMIT_BYTES)


def _dot(a, b, ta, tb, precision=None):
    dims = (((0 if ta else 1,), (1 if tb else 0,)), ((), ()))
    return lax.dot_general(a, b, dims, preferred_element_type=F32, precision=precision)


def _mm_raw(a, b, ta=False, tb=False):
    return _dot(a.astype(BF16), b.astype(BF16), ta, tb)


def _mmf_raw(a, b, ta=False, tb=False):
    return _dot(a, b, ta, tb, lax.Precision.HIGHEST)


def _make_mm(raw):
    @functools.partial(jax.custom_vjp, nondiff_argnums=(2, 3))
    def mm(a, b, ta=False, tb=False):
        return raw(a, b, ta, tb)

    def fwd(a, b, ta, tb):
        return raw(a, b, ta, tb), (a, b)

    def bwd(ta, tb, res, g):
        a, b = res
        da = raw(b, g, tb, True) if ta else raw(g, b, False, not tb)
        db = raw(g, a, True, ta) if tb else raw(a, g, not ta, False)
        return da, db

    mm.defvjp(fwd, bwd)
    return mm


_mm = _make_mm(_mm_raw)
_mmf = _make_mm(_mmf_raw)


def _mm3(a, b, ta=False, tb=False):
    a_hi, b_hi = a.astype(BF16).astype(F32), b.astype(BF16).astype(F32)
    return _mm(a_hi, b_hi, ta, tb) + _mm(a_hi, b - b_hi, ta, tb) + _mm(a - a_hi, b_hi, ta, tb)


def _rms(x, g):
    return x * lax.rsqrt(jnp.mean(x * x, axis=-1, keepdims=True) + EPS) * g


def _silu_mul(g, u):
    return jax.nn.silu(g) * u


def _log_sigmoid(z):
    return jnp.minimum(z, 0.0) - jnp.log(1.0 + jnp.exp(-jnp.abs(z)))


def _matmul(a_list, b, *, ta=False, tb=False, tm, tn, b_blocks=None, res=None, scale=1.0, out_dtype=F32, name):
    if not isinstance(a_list, (list, tuple)):
        a_list = [a_list]
    n_a = len(a_list)
    m = a_list[0].shape[1] if ta else a_list[0].shape[0]
    ks = [a.shape[0] if ta else a.shape[1] for a in a_list]
    n = b.shape[0] if tb else b.shape[1]
    if b_blocks is None:
        assert n_a == 1
        b_blocks = [0]
    tm, tn = min(tm, m), min(tn, n)
    assert m % tm == 0 and n % tn == 0, (m, n, tm, tn)

    def body(*refs):
        a_refs, b_refs = refs[:n_a], refs[n_a:2 * n_a]
        r_ref = refs[2 * n_a] if res is not None else None
        o_ref = refs[-1]
        acc = _mm_raw(a_refs[0][...], b_refs[0][...], ta, tb)
        for k in range(1, n_a):
            acc = acc + _mm_raw(a_refs[k][...], b_refs[k][...], ta, tb)
        if scale != 1.0:
            acc = acc * scale
        if r_ref is not None:
            acc = r_ref[...] + acc
        o_ref[...] = acc.astype(out_dtype)

    in_specs = []
    for k in ks:
        in_specs.append(pl.BlockSpec((k, tm), lambda i, j: (0, i)) if ta else pl.BlockSpec((tm, k), lambda i, j: (i, 0)))
    for k, blk in zip(ks, b_blocks):
        if tb:
            in_specs.append(pl.BlockSpec((tn, k), functools.partial(lambda i, j, blk: (j, blk), blk=blk)))
        else:
            in_specs.append(pl.BlockSpec((k, tn), functools.partial(lambda i, j, blk: (blk, j), blk=blk)))
    args = list(a_list) + [b] * n_a
    if res is not None:
        in_specs.append(pl.BlockSpec((tm, tn), lambda i, j: (i, j)))
        args.append(res)
    return pl.pallas_call(
        body, name=name, grid=(m // tm, n // tn), in_specs=in_specs,
        out_specs=pl.BlockSpec((tm, tn), lambda i, j: (i, j)), out_shape=SDS((m, n), out_dtype),
        compiler_params=_params("parallel", "parallel"),
    )(*args)


def _win_pieces(w):
    glr_lo, glr_hi = COL_MQ, COL_MQ + GLA_RANK
    out = []
    for j in range(N_DEV):
        for lo, hi, shift in ((0, glr_lo, 0), (glr_lo, glr_hi, COL_GLR - glr_lo), (glr_hi, IN_W, COL_MQ - glr_hi)):
            s, e = max(j * w, lo), min((j + 1) * w, hi)
            if s < e:
                out.append((j, s - j * w, e - j * w, s + shift))
    return out


def _pack_win(win_all, *, tr, name):
    _, d, w = win_all.shape

    def body(i_ref, o_ref):
        for j, a, b, dst in _win_pieces(w):
            o_ref[:, dst:dst + b - a] = i_ref[j][:, a:b]
        o_ref[:, IN_W:] = jnp.zeros((tr, IN_W_PAD - IN_W), o_ref.dtype)

    return pl.pallas_call(
        body, name=name, grid=(d // tr,), in_specs=[pl.BlockSpec((N_DEV, tr, w), lambda i: (0, i, 0))],
        out_specs=pl.BlockSpec((tr, IN_W_PAD), lambda i: (i, 0)), out_shape=SDS((d, IN_W_PAD), win_all.dtype),
        compiler_params=_params("parallel"),
    )(win_all)


def _unpack_win(dwin_p, *, tr, name):
    d = dwin_p.shape[0]
    w = IN_W // N_DEV

    def body(i_ref, o_ref):
        for j, a, b, src in _win_pieces(w):
            o_ref[j % 2, j // 2, :, a:b] = i_ref[:, src:src + b - a]

    return pl.pallas_call(
        body, name=name, grid=(d // tr,), in_specs=[pl.BlockSpec((tr, IN_W_PAD), lambda i: (i, 0))],
        out_specs=pl.BlockSpec((2, 4, tr, w), lambda i: (0, 0, i, 0)), out_shape=SDS((2, 4, d, w), dwin_p.dtype),
        compiler_params=_params("parallel"),
    )(dwin_p)


def _rms_fwd(x, g, *, tm, name):
    s, d = x.shape

    def body(x_ref, g_ref, h_ref):
        h_ref[...] = _rms(x_ref[...], g_ref[...]).astype(BF16)

    return pl.pallas_call(
        body, name=name, grid=(s // tm,),
        in_specs=[pl.BlockSpec((tm, d), lambda i: (i, 0)), pl.BlockSpec((1, d), lambda i: (0, 0))],
        out_specs=pl.BlockSpec((tm, d), lambda i: (i, 0)), out_shape=SDS((s, d), BF16),
        compiler_params=_params("parallel"),
    )(x, g)


def _rms_bwd(x, g, dh, dres, *, tm, name):
    s, d = x.shape
    want_dx = dres is not None
    product = isinstance(dh, tuple)

    def body(*refs):
        n_dh = 2 if product else 1
        x_ref, g_ref = refs[:2]
        dh_refs, rest = refs[2:2 + n_dh], refs[2 + n_dh:]
        if want_dx:
            dres_ref, dx_ref, dxb_ref, dg_ref = rest
        else:
            dg_ref, = rest
        dh_tile = _mm_raw(dh_refs[0][...], dh_refs[1][...], False, True) if product else dh_refs[0][...]
        _, vjp = jax.vjp(_rms, x_ref[...], g_ref[...])
        dx, dg = vjp(dh_tile)
        if want_dx:
            dx = dres_ref[...] + dx
            dx_ref[...] = dx
            dxb_ref[...] = dx.astype(BF16)

        @pl.when(pl.program_id(0) == 0)
        def _():
            dg_ref[...] = jnp.zeros_like(dg_ref)

        dg_ref[...] += dg

    row = pl.BlockSpec((tm, d), lambda i: (i, 0))
    vec = pl.BlockSpec((1, d), lambda i: (0, 0))
    if product:
        k = dh[0].shape[1]
        dh_specs, dh_args = [pl.BlockSpec((tm, k), lambda i: (i, 0)), pl.BlockSpec((d, k), lambda i: (0, 0))], list(dh)
    else:
        dh_specs, dh_args = [row], [dh]
    if want_dx:
        return pl.pallas_call(
            body, name=name, grid=(s // tm,), in_specs=[row, vec] + dh_specs + [row], out_specs=[row, row, vec],
            out_shape=[SDS((s, d), F32), SDS((s, d), BF16), SDS((1, d), F32)], compiler_params=_params("arbitrary"),
        )(x, g, *dh_args, dres)
    return None, None, pl.pallas_call(
        body, name=name, grid=(s // tm,), in_specs=[row, vec] + dh_specs, out_specs=vec,
        out_shape=SDS((1, d), F32), compiler_params=_params("arbitrary"),
    )(x, g, *dh_args)


FFN_TN = 256
FFN_TN_FWD = 512
FFN1_FIRST = 192


def _ffn_fwd(x, gain, w3, tag, *, tm=1024, next_gain=None, target=None, start=None, partial=False):
    s, d = x.shape
    f = w3.shape[1]
    tn = FFN_TN_FWD if f % FFN_TN_FWD == 0 else FFN_TN
    nj = f // tn
    tm = min(tm, s)
    n_extra = (next_gain is not None) + (target is not None) + 2 * (start is not None)

    def body(*refs):
        x_ref, gain_ref, wg_ref, wu_ref, wd_ref = refs[:5]
        extra, outs = refs[5:5 + n_extra], refs[5 + n_extra:-1]
        acc_s = refs[-1]
        g_ref, u_ref = outs[-2:]
        h_ref = extra[-2] if start is not None else outs[-3]
        i, j = pl.program_id(0), pl.program_id(1)

        @pl.when(j == 0)
        def _():
            if start is None:
                h_ref[...] = _rms(x_ref[...], gain_ref[...]).astype(BF16)
                acc_s[...] = jnp.zeros_like(acc_s)
            else:
                acc_s[...] = extra[-1][...]

        hv = h_ref[...]
        g = _mm_raw(hv, wg_ref[...], False, True)
        u = _mm_raw(hv, wu_ref[...], False, True)
        g_ref[...] = g.astype(BF16)
        u_ref[...] = u.astype(BF16)
        acc_s[...] += _mm_raw(_silu_mul(g, u), wd_ref[...])

        @pl.when(j == nj - 1)
        def _():
            y = acc_s[...] if partial else x_ref[...] + 0.5 * acc_s[...]
            if target is None:
                outs[0][...] = y
                if next_gain is not None:
                    outs[1][...] = _rms(y, extra[0][...]).astype(BF16)
            else:
                dy_ref, dyb_ref, loss_ref = outs[:3]
                diff = y - extra[0][...]
                dy_ref[...] = diff * (1.0 / d)
                dyb_ref[...] = (diff * (1.0 / d)).astype(BF16)
                part = 0.5 * jnp.sum(jnp.mean(diff * diff, axis=-1, keepdims=True), axis=0, keepdims=True)

                @pl.when(i == 0)
                def _():
                    loss_ref[...] = part

                @pl.when(i > 0)
                def _():
                    loss_ref[...] += part

    row = pl.BlockSpec((tm, d), lambda i, j: (i, 0))
    vec = pl.BlockSpec((1, d), lambda i, j: (0, 0))
    tile = pl.BlockSpec((tm, tn), lambda i, j: (i, j))
    in_specs = [row, vec] + [pl.BlockSpec((None, tn, d), functools.partial(lambda i, j, k: (k, j, 0), k=k)) for k in range(3)]
    args = [x, gain, w3, w3, w3]
    if target is None:
        out_specs, out_shape = [row], [SDS((s, d), F32)]
        if next_gain is not None:
            in_specs.append(vec)
            args.append(next_gain)
            out_specs.append(row)
            out_shape.append(SDS((s, d), BF16))
    else:
        in_specs.append(row)
        args.append(target)
        out_specs = [row, row, pl.BlockSpec((1, 1), lambda i, j: (0, 0))]
        out_shape = [SDS((s, d), F32), SDS((s, d), BF16), SDS((1, 1), F32)]
    if start is None:
        out_specs.append(row)
        out_shape.append(SDS((s, d), BF16))
    else:
        in_specs += [row, row]
        args += list(start)
    *head, g, u = pl.pallas_call(
        body, name=f"{tag}_fwd", grid=(s // tm, nj), in_specs=in_specs,
        out_specs=out_specs + [tile, tile],
        out_shape=out_shape + [SDS((s, f), BF16), SDS((s, f), BF16)],
        scratch_shapes=[pltpu.VMEM((tm, d), F32)],
        compiler_params=_params("arbitrary", "arbitrary"),
    )(*args)
    if start is None:
        *head, h = head
    else:
        h = start[0]
    return head, (h, g, u)


def _ffn_bwd_part(dyb, w3, saved, first, count, dh_init, *, name):
    h, g, u = saved
    s, d = h.shape
    tn = FFN_TN

    def body(*refs):
        if dh_init is None:
            dy_ref, h_ref, wd_ref, wg_ref, wu_ref, g_ref, u_ref, dh_ref, dw3_ref, dg_s, du_s, a_s = refs
        else:
            dy_ref, h_ref, wd_ref, wg_ref, wu_ref, g_ref, u_ref, dh0_ref, dh_ref, dw3_ref, dg_s, du_s, a_s = refs
        j = pl.program_id(0)

        @pl.when(j == 0)
        def _():
            dh_ref[...] = jnp.zeros_like(dh_ref) if dh_init is None else dh0_ref[...]
            for ref in (dg_s, du_s, a_s):
                ref[...] = jnp.zeros_like(ref)

        now, before = j % 2, 1 - j % 2
        dyv = dy_ref[...]
        hv = h_ref[...]
        dg, du, a = dg_s[before], du_s[before], a_s[before]
        dh_ref[...] += _mm_raw(dg, wg_ref[...]) + _mm_raw(du, wu_ref[...])
        dw3_ref[0] = _mm_raw(dg, hv, True, False).astype(BF16)
        dw3_ref[1] = _mm_raw(du, hv, True, False).astype(BF16)
        dw3_ref[2] = (_mm_raw(a, dyv, True, False) * 0.5).astype(BF16)

        da = _mm_raw(dyv, wd_ref[...], False, True) * 0.5
        a, vjp = jax.vjp(_silu_mul, g_ref[...].astype(F32), u_ref[...].astype(F32))
        dg, du = vjp(da)
        dg_s[now] = dg.astype(BF16)
        du_s[now] = du.astype(BF16)
        a_s[now] = a.astype(BF16)

    this = lambda j: first + jnp.minimum(j, count - 1)
    last = lambda j: first + jnp.maximum(j - 1, 0)
    full = pl.BlockSpec((s, d), lambda j: (0, 0))
    once = pl.BlockSpec((s, d), lambda j: (0, 0), pipeline_mode=pl.Buffered(1))
    tile = pl.BlockSpec((s, tn), lambda j: (0, this(j)))
    in_specs = [once, once, pl.BlockSpec((None, tn, d), lambda j: (2, this(j), 0)),
                pl.BlockSpec((None, tn, d), lambda j: (0, last(j), 0)), pl.BlockSpec((None, tn, d), lambda j: (1, last(j), 0)),
                tile, tile]
    args = [dyb, h, w3, w3, w3, g, u]
    if dh_init is not None:
        in_specs.append(once)
        args.append(dh_init)
    return pl.pallas_call(
        body, name=name, grid=(count + 1,), in_specs=in_specs,
        out_specs=[full, pl.BlockSpec((3, tn, d), lambda j: (0, jnp.maximum(j - 1, 0), 0))],
        out_shape=[SDS((s, d), F32), SDS((3, count * tn, d), BF16)],
        scratch_shapes=[pltpu.VMEM((2, s, tn), BF16)] * 3,
        compiler_params=_params("arbitrary"),
    )(*args)


def _bucket_table():
    qi = np.arange(BLOCK)[:, None]
    kj = np.arange(2 * BLOCK)[None, :]
    dist = np.maximum(qi + BLOCK - kj, 0)
    max_exact = N_BUCKETS // 2
    d = np.maximum(dist, 1).astype(np.float32)
    large = max_exact + (np.log(d / np.float32(max_exact)) / np.float32(math.log(MAX_DISTANCE / max_exact))
                         * np.float32(N_BUCKETS - max_exact)).astype(np.int32)
    large = np.minimum(large, N_BUCKETS - 1)
    band = np.where(dist < max_exact, dist, large).astype(np.int32)
    return np.where(np.tril(np.ones((BLOCK, BLOCK), bool)), band[:, BLOCK:], band[:, :BLOCK])


SWA_STACK = SWA_GROUP * BLOCK


def _swa_masks(n):
    qi = lax.broadcasted_iota(jnp.int32, (SWA_STACK, BLOCK), 0) % BLOCK
    kj = lax.broadcasted_iota(jnp.int32, (SWA_STACK, BLOCK), 1)
    own = kj <= qi
    return own, own | (n > 0)


def _swa_group(q, kp, kc, vp, vc, qg, kg, sink, bias, own, valid):
    qn = _rms(q, qg)
    s = jnp.where(own, _mm(qn, _rms(kc, kg), False, True), _mm(qn, _rms(kp, kg), False, True))
    s = s * (HEAD_DIM ** -0.5) + bias
    s = jnp.where(valid, s, -jnp.inf)
    m = lax.stop_gradient(jnp.maximum(jnp.max(s, axis=-1, keepdims=True), sink))
    p = jnp.exp(s - m)
    p = p / (jnp.sum(p, axis=-1, keepdims=True) + jnp.exp(sink - m))
    return _mm(jnp.where(own, p, 0.0), vc) + _mm(jnp.where(own, 0.0, p), vp)


def _swa_bias_table(rb_ref, bucket, bias_s):
    for h in range(SWA_HEADS):
        acc = jnp.zeros((BLOCK, BLOCK), F32)
        for b in range(N_BUCKETS):
            acc = jnp.where(bucket == b, rb_ref[b, h], acc)
        bias_s[h // SWA_GROUP, (h % SWA_GROUP) * BLOCK:(h % SWA_GROUP + 1) * BLOCK, :] = acc


def _swa_stack(ref, g):
    return jnp.concatenate([ref[:, (g * SWA_GROUP + hh) * HEAD_DIM:(g * SWA_GROUP + hh + 1) * HEAD_DIM]
                            for hh in range(SWA_GROUP)], axis=0)


def _swa_unstack(ref, g, stacked):
    for hh in range(SWA_GROUP):
        h = g * SWA_GROUP + hh
        ref[:, h * HEAD_DIM:(h + 1) * HEAD_DIM] = stacked[hh * BLOCK:(hh + 1) * BLOCK]


def _swa_sink_column(sink_ref, g):
    head = lax.broadcasted_iota(jnp.int32, (SWA_STACK, 1), 0) // BLOCK
    col = jnp.zeros((SWA_STACK, 1), F32)
    for hh in range(SWA_GROUP):
        col = jnp.where(head == hh, sink_ref[g * SWA_GROUP + hh], col)
    return col


def _swa_band(kvp_ref, kvc_ref, g):
    k = slice(g * HEAD_DIM, (g + 1) * HEAD_DIM)
    v = slice(SWA_KV_W + g * HEAD_DIM, SWA_KV_W + (g + 1) * HEAD_DIM)
    return kvp_ref[:, k], kvc_ref[:, k], kvp_ref[:, v], kvc_ref[:, v]


def _swa_specs(order):
    kvc = COL_SKV // (2 * SWA_KV_W)
    return [
        pl.BlockSpec((BLOCK, SWA_Q_W), lambda t: (order(t), 0)),
        pl.BlockSpec((BLOCK, 2 * SWA_KV_W), lambda t: (jnp.maximum(order(t) - 1, 0), kvc)),
        pl.BlockSpec((BLOCK, 2 * SWA_KV_W), lambda t: (order(t), kvc)),
        pl.BlockSpec((1, HEAD_DIM), lambda t: (0, 0)),
        pl.BlockSpec((1, HEAD_DIM), lambda t: (0, 0)),
        pl.BlockSpec(memory_space=pltpu.SMEM),
        pl.BlockSpec(memory_space=pltpu.SMEM),
        pl.BlockSpec((BLOCK, BLOCK), lambda t: (0, 0)),
    ]


def _swa_fwd(p, qg, kg, sinks, rel_bias, *, name):
    s = p.shape[0]
    nb = s // BLOCK

    def body(q_ref, kvp_ref, kvc_ref, qg_ref, kg_ref, sink_ref, rb_ref, bucket_ref, y_ref, bias_s):
        n = pl.program_id(0)

        @pl.when(n == 0)
        def _():
            _swa_bias_table(rb_ref, bucket_ref[...], bias_s)

        own, valid = _swa_masks(n)
        for g in range(SWA_KV_HEADS):
            out = _swa_group(_swa_stack(q_ref, g), *_swa_band(kvp_ref, kvc_ref, g), qg_ref[...], kg_ref[...],
                             _swa_sink_column(sink_ref, g), bias_s[g], own, valid)
            _swa_unstack(y_ref, g, out)

    return pl.pallas_call(
        body, name=name, grid=(nb,), in_specs=_swa_specs(lambda t: t),
        out_specs=pl.BlockSpec((BLOCK, SWA_Q_W), lambda t: (t, 0)), out_shape=SDS((s, SWA_Q_W), F32),
        scratch_shapes=[pltpu.VMEM((SWA_KV_HEADS, SWA_STACK, BLOCK), F32)],
        compiler_params=_params("arbitrary"),
    )(p, p, p, qg, kg, sinks, rel_bias, jnp.asarray(_bucket_table()))


def _swa_bwd(p, qg, kg, sinks, rel_bias, dy_all, *, name):
    s = p.shape[0]
    nb = s // BLOCK

    def body(q_ref, kvp_ref, kvc_ref, qg_ref, kg_ref, sink_ref, rb_ref, bucket_ref, dy_ref,
             dq_ref, dkv_ref, dqg_ref, dkg_ref, dsink_ref, drb_ref, bias_s, dbias_s, carry_s):
        t = pl.program_id(0)
        n = nb - 1 - t

        @pl.when(t == 0)
        def _():
            _swa_bias_table(rb_ref, bucket_ref[...], bias_s)
            dbias_s[...] = jnp.zeros_like(dbias_s)
            carry_s[...] = jnp.zeros_like(carry_s)
            dqg_ref[...] = jnp.zeros_like(dqg_ref)
            dkg_ref[...] = jnp.zeros_like(dkg_ref)
            dsink_ref[...] = jnp.zeros_like(dsink_ref)
            drb_ref[...] = jnp.zeros_like(drb_ref)

        own, valid = _swa_masks(n)
        lane = lax.broadcasted_iota(jnp.int32, (1, BLOCK), 1)
        dqg = jnp.zeros((1, HEAD_DIM), F32)
        dkg = jnp.zeros((1, HEAD_DIM), F32)
        dsink_vec = jnp.zeros((1, BLOCK), F32)
        for g in range(SWA_KV_HEADS):
            _, vjp = jax.vjp(functools.partial(_swa_group, own=own, valid=valid), _swa_stack(q_ref, g),
                             *_swa_band(kvp_ref, kvc_ref, g), qg_ref[...], kg_ref[...], _swa_sink_column(sink_ref, g),
                             bias_s[g])
            dq, dkp, dkc, dvp, dvc, dqg_g, dkg_g, dsink_col, dbias = vjp(_swa_stack(dy_ref, g))
            _swa_unstack(dq_ref, g, dq)
            dqg += dqg_g
            dkg += dkg_g
            dbias_s[g] += dbias
            for hh in range(SWA_GROUP):
                dsink_h = jnp.sum(dsink_col[hh * BLOCK:(hh + 1) * BLOCK], axis=0, keepdims=True)
                dsink_vec += jnp.where(lane == g * SWA_GROUP + hh, dsink_h, 0.0)
            lo = g * HEAD_DIM
            dkv_ref[:, lo:lo + HEAD_DIM] = dkc + carry_s[g]
            carry_s[g] = dkp
            lo += SWA_KV_W
            dkv_ref[:, lo:lo + HEAD_DIM] = dvc + carry_s[SWA_KV_HEADS + g]
            carry_s[SWA_KV_HEADS + g] = dvp
        dqg_ref[...] += dqg
        dkg_ref[...] += dkg
        dsink_ref[...] += dsink_vec

        @pl.when(t == nb - 1)
        def _():
            bucket = bucket_ref[...]
            row = lax.broadcasted_iota(jnp.int32, (N_BUCKETS, BLOCK), 0)
            col = lax.broadcasted_iota(jnp.int32, (N_BUCKETS, BLOCK), 1)
            acc = jnp.zeros((N_BUCKETS, BLOCK), F32)
            for h in range(SWA_HEADS):
                dbias = dbias_s[h // SWA_GROUP, (h % SWA_GROUP) * BLOCK:(h % SWA_GROUP + 1) * BLOCK, :]
                for b in range(N_BUCKETS):
                    part = jnp.sum(jnp.where(bucket == b, dbias, 0.0), axis=1, keepdims=True)
                    val = jnp.sum(part, axis=0, keepdims=True)
                    acc = acc + jnp.where((row == b) & (col == h), val, 0.0)
            drb_ref[...] = acc

    order = lambda t: nb - 1 - t
    vec = pl.BlockSpec((1, HEAD_DIM), lambda t: (0, 0))
    return pl.pallas_call(
        body, name=name, grid=(nb,),
        in_specs=_swa_specs(order) + [pl.BlockSpec((BLOCK, SWA_Q_W), lambda t: (order(t), 0))],
        out_specs=[pl.BlockSpec((BLOCK, SWA_Q_W), lambda t: (order(t), 0)),
                   pl.BlockSpec((BLOCK, 2 * SWA_KV_W), lambda t: (order(t), 0)),
                   vec, vec, pl.BlockSpec((1, BLOCK), lambda t: (0, 0)),
                   pl.BlockSpec((N_BUCKETS, BLOCK), lambda t: (0, 0))],
        out_shape=[SDS((s, SWA_Q_W), F32), SDS((s, 2 * SWA_KV_W), F32), SDS((1, HEAD_DIM), F32),
                   SDS((1, HEAD_DIM), F32), SDS((1, BLOCK), F32), SDS((N_BUCKETS, BLOCK), F32)],
        scratch_shapes=[pltpu.VMEM((SWA_KV_HEADS, SWA_STACK, BLOCK), F32),
                        pltpu.VMEM((SWA_KV_HEADS, SWA_STACK, BLOCK), F32),
                        pltpu.VMEM((2 * SWA_KV_HEADS, BLOCK, HEAD_DIM), F32)],
        compiler_params=_params("arbitrary"),
    )(p, p, p, qg, kg, sinks, rel_bias, jnp.asarray(_bucket_table()), dy_all)


def _mem_head(q, k, v, qg, kg):
    qn = _rms(q, qg)
    kn = _rms(k, kg)
    s = _mm(qn, kn, False, True) * (HEAD_DIM ** -0.5)
    m = lax.stop_gradient(jnp.max(s, axis=-1, keepdims=True))
    e = jnp.exp(s - m)
    return _mm(e / jnp.sum(e, axis=-1, keepdims=True), v)


def _mem_fwd(p, kv, qg, kg, *, tq, name):
    s = p.shape[0]
    m = kv.shape[0]

    def body(q_ref, kv_ref, qg_ref, kg_ref, y_ref):
        for h in range(MEM_HEADS):
            cols = slice(h * HEAD_DIM, (h + 1) * HEAD_DIM)
            vcols = slice(MEM_Q_W + h * HEAD_DIM, MEM_Q_W + (h + 1) * HEAD_DIM)
            y_ref[:, cols] = _mem_head(q_ref[:, cols], kv_ref[:, cols], kv_ref[:, vcols], qg_ref[...], kg_ref[...])

    vec = pl.BlockSpec((1, HEAD_DIM), lambda t: (0, 0))
    return pl.pallas_call(
        body, name=name, grid=(s // tq,),
        in_specs=[pl.BlockSpec((tq, MEM_Q_W), lambda t: (t, COL_MQ // MEM_Q_W)),
                  pl.BlockSpec((m, 2 * MEM_Q_W), lambda t: (0, 0)), vec, vec],
        out_specs=pl.BlockSpec((tq, MEM_Q_W), lambda t: (t, 0)), out_shape=SDS((s, MEM_Q_W), F32),
        compiler_params=_params("parallel"),
    )(p, kv, qg, kg)


def _mem_bwd(p, kv, qg, kg, dy_all, *, tq, name):
    s = p.shape[0]
    m = kv.shape[0]

    def body(q_ref, kv_ref, qg_ref, kg_ref, dy_ref, dq_ref, dkv_ref, dqg_ref, dkg_ref):
        @pl.when(pl.program_id(0) == 0)
        def _():
            dkv_ref[...] = jnp.zeros_like(dkv_ref)
            dqg_ref[...] = jnp.zeros_like(dqg_ref)
            dkg_ref[...] = jnp.zeros_like(dkg_ref)

        dqg = jnp.zeros((1, HEAD_DIM), F32)
        dkg = jnp.zeros((1, HEAD_DIM), F32)
        for h in range(MEM_HEADS):
            cols = slice(h * HEAD_DIM, (h + 1) * HEAD_DIM)
            vcols = slice(MEM_Q_W + h * HEAD_DIM, MEM_Q_W + (h + 1) * HEAD_DIM)
            _, vjp = jax.vjp(_mem_head, q_ref[:, cols], kv_ref[:, cols], kv_ref[:, vcols], qg_ref[...], kg_ref[...])
            dq, dk, dv, dqg_h, dkg_h = vjp(dy_ref[:, cols])
            dq_ref[:, cols] = dq
            dkv_ref[:, cols] += dk
            dkv_ref[:, vcols] += dv
            dqg += dqg_h
            dkg += dkg_h
        dqg_ref[...] += dqg
        dkg_ref[...] += dkg

    vec = pl.BlockSpec((1, HEAD_DIM), lambda t: (0, 0))
    full = pl.BlockSpec((m, 2 * MEM_Q_W), lambda t: (0, 0))
    dy_col = (SWA_Q_W + GLA_V_W) // MEM_Q_W
    return pl.pallas_call(
        body, name=name, grid=(s // tq,),
        in_specs=[pl.BlockSpec((tq, MEM_Q_W), lambda t: (t, COL_MQ // MEM_Q_W)), full, vec, vec,
                  pl.BlockSpec((tq, MEM_Q_W), lambda t: (t, dy_col))],
        out_specs=[pl.BlockSpec((tq, MEM_Q_W), lambda t: (t, 0)), full, vec, vec],
        out_shape=[SDS((s, MEM_Q_W), F32), SDS((m, 2 * MEM_Q_W), F32), SDS((1, HEAD_DIM), F32), SDS((1, HEAD_DIM), F32)],
        compiler_params=_params("arbitrary"),
    )(p, kv, qg, kg, dy_all)


GLA_ROWS = 256


GLA_GROUP = 4


def _gla_consts():
    c, h, r = GLA_CHUNK, GLA_HEADS, GLA_GROUP * GLA_CHUNK
    i2 = lax.broadcasted_iota(jnp.int32, (c, c), 0)
    j2 = lax.broadcasted_iota(jnp.int32, (c, c), 1)
    slab_q = lax.broadcasted_iota(jnp.int32, (h, r, GLA_QK_W), 0)
    lane_q = lax.broadcasted_iota(jnp.int32, (h, r, GLA_QK_W), 2)
    row_a = lax.broadcasted_iota(jnp.int32, (h * r, r), 0) % r
    col_a = lax.broadcasted_iota(jnp.int32, (h * r, r), 1)
    slab_o = lax.broadcasted_iota(jnp.int32, (h, r, GLA_V_W), 0)
    lane_o = lax.broadcasted_iota(jnp.int32, (h, r, GLA_V_W), 2)
    row_s = lax.broadcasted_iota(jnp.int32, (GLA_V_W, GLA_QK_W), 0)
    col_s = lax.broadcasted_iota(jnp.int32, (GLA_V_W, GLA_QK_W), 1)
    return dict(
        ltri=(j2 <= i2).astype(F32),
        m_q=(slab_q == lane_q // GLA_DK).astype(F32),
        causal=(col_a <= row_a) & (col_a // c == row_a // c),
        m_o=(slab_o == lane_o // GLA_DV).astype(F32),
        m_s=(row_s // GLA_DV == col_s // GLA_DK).astype(F32),
    )


def _gla_step(q, k, v, z, bg, st, c):
    h = GLA_HEADS
    kt, ka, qt, qe, decay = [], [], [], [], []
    for qc, kc, zc in zip(q, k, z):
        la = _log_sigmoid(zc + bg) * (1.0 / GLA_TAU)
        b = _mmf(c["ltri"], la)
        bl = jnp.sum(la, axis=0, keepdims=True)
        qs = qc * (GLA_DK ** -0.5)
        kt.append(kc * jnp.exp(bl - b))
        ka.append(kc * jnp.exp(0.5 * bl - b))
        qt.append(qs * jnp.exp(b - 0.5 * bl))
        qe.append(qs * jnp.exp(b))
        decay.append(jnp.exp(bl))
    o_intra = []
    rows = GLA_GROUP * GLA_CHUNK
    for lo in range(0, len(q), GLA_GROUP):
        qt_all, kt_all, v_all = (jnp.concatenate(parts[lo:lo + GLA_GROUP], axis=0) for parts in (qt, ka, v))
        q_stack = (jnp.broadcast_to(qt_all[None], (h, rows, GLA_QK_W)) * c["m_q"]).reshape(h * rows, GLA_QK_W)
        a = jnp.where(c["causal"], _mm3(q_stack, kt_all, False, True), 0.0)
        o_stack = _mm(a, v_all)
        o_intra.append(jnp.sum(o_stack.reshape(h, rows, GLA_V_W) * c["m_o"], axis=0))
    o_intra = jnp.concatenate(o_intra, axis=0)
    o_inter = []
    for qec, ktc, vc, dc in zip(qe, kt, v, decay):
        o_inter.append(_mm(qec, st, False, True))
        st = st * dc + _mm(vc, ktc, True, False) * c["m_s"]
    return o_intra + jnp.concatenate(o_inter, axis=0), st


def _gla_post(o, gg, gain, g64):
    ms = _mmf(o * o, g64) * (1.0 / GLA_DV)
    return o * lax.rsqrt(ms + EPS) * gain * jax.nn.silu(gg)


def _gla_g64():
    r = lax.broadcasted_iota(jnp.int32, (GLA_V_W, GLA_V_W), 0)
    c = lax.broadcasted_iota(jnp.int32, (GLA_V_W, GLA_V_W), 1)
    return (r // GLA_DV == c // GLA_DV).astype(F32)


def _gla_in_specs(order):
    r = GLA_ROWS
    return [
        pl.BlockSpec((r, GLA_QK_W), lambda t: (order(t), COL_GQ // GLA_QK_W)),
        pl.BlockSpec((r, GLA_QK_W), lambda t: (order(t), COL_GK // GLA_QK_W)),
        pl.BlockSpec((r, GLA_V_W), lambda t: (order(t), COL_GV // GLA_V_W)),
        pl.BlockSpec((r, GLA_V_W), lambda t: (order(t), COL_GG // GLA_V_W)),
        pl.BlockSpec((r, GLA_QK_W), lambda t: (order(t), 0)),
        pl.BlockSpec((1, GLA_QK_W), lambda t: (0, 0)),
        pl.BlockSpec((1, GLA_V_W), lambda t: (0, 0)),
    ]


def _gla_pieces(q_ref, k_ref, v_ref, z_ref, cps):
    chunk = lambda ref: [ref[ci * GLA_CHUNK:(ci + 1) * GLA_CHUNK, :] for ci in range(cps)]
    return chunk(q_ref), chunk(k_ref), chunk(v_ref), chunk(z_ref)


def _gla_fwd(p, z, bg, gain, *, name):
    s = p.shape[0]
    r = GLA_ROWS
    cps = r // GLA_CHUNK

    def body(q_ref, k_ref, v_ref, gg_ref, z_ref, bg_ref, gain_ref, y_ref, oraw_ref, stsave_ref, st_s):
        @pl.when(pl.program_id(0) == 0)
        def _():
            st_s[...] = jnp.zeros_like(st_s)

        st = st_s[...]
        stsave_ref[0] = st
        o, st = _gla_step(*_gla_pieces(q_ref, k_ref, v_ref, z_ref, cps), bg_ref[...], st, _gla_consts())
        oraw_ref[...] = o
        st_s[...] = st
        y_ref[...] = _gla_post(o, gg_ref[...], gain_ref[...], _gla_g64())

    rowv = pl.BlockSpec((r, GLA_V_W), lambda t: (t, 0))
    return pl.pallas_call(
        body, name=name, grid=(s // r,), in_specs=_gla_in_specs(lambda t: t),
        out_specs=[rowv, rowv, pl.BlockSpec((1, GLA_V_W, GLA_QK_W), lambda t: (t, 0, 0))],
        out_shape=[SDS((s, GLA_V_W), F32), SDS((s, GLA_V_W), F32), SDS((s // r, GLA_V_W, GLA_QK_W), F32)],
        scratch_shapes=[pltpu.VMEM((GLA_V_W, GLA_QK_W), F32)],
        compiler_params=_params("arbitrary"),
    )(p, p, p, p, z, bg, gain)


def _gla_bwd(p, z, bg, gain, oraw, stsave, dy_all, *, name):
    s = p.shape[0]
    r = GLA_ROWS
    cps = r // GLA_CHUNK
    nsteps = s // r
    w_qkvg = 2 * GLA_QK_W + 2 * GLA_V_W

    def body(q_ref, k_ref, v_ref, gg_ref, z_ref, bg_ref, gain_ref, oraw_ref, stsave_ref, dy_ref,
             dqkvg_ref, dz_ref, dbg_ref, dgain_ref, dst_s):
        @pl.when(pl.program_id(0) == 0)
        def _():
            dst_s[...] = jnp.zeros_like(dst_s)
            dbg_ref[...] = jnp.zeros_like(dbg_ref)
            dgain_ref[...] = jnp.zeros_like(dgain_ref)

        _, vjp = jax.vjp(functools.partial(_gla_post, g64=_gla_g64()), oraw_ref[...], gg_ref[...], gain_ref[...])
        do, dgg, dgain = vjp(dy_ref[...])
        dqkvg_ref[:, 2 * GLA_QK_W + GLA_V_W:] = dgg
        dgain_ref[...] += dgain
        _, vjp = jax.vjp(functools.partial(_gla_step, c=_gla_consts()), *_gla_pieces(q_ref, k_ref, v_ref, z_ref, cps),
                         bg_ref[...], stsave_ref[0])
        dq, dk, dv, dz, dbg, dst = vjp((do, dst_s[...]))
        for ci in range(cps):
            rows = slice(ci * GLA_CHUNK, (ci + 1) * GLA_CHUNK)
            dqkvg_ref[rows, 0:GLA_QK_W] = dq[ci]
            dqkvg_ref[rows, GLA_QK_W:2 * GLA_QK_W] = dk[ci]
            dqkvg_ref[rows, 2 * GLA_QK_W:2 * GLA_QK_W + GLA_V_W] = dv[ci]
            dz_ref[rows, :] = dz[ci]
        dst_s[...] = dst
        dbg_ref[...] += dbg

    order = lambda t: nsteps - 1 - t
    rowv = pl.BlockSpec((r, GLA_V_W), lambda t: (order(t), 0))
    return pl.pallas_call(
        body, name=name, grid=(nsteps,),
        in_specs=_gla_in_specs(order) + [
            rowv, pl.BlockSpec((1, GLA_V_W, GLA_QK_W), lambda t: (order(t), 0, 0)),
            pl.BlockSpec((r, GLA_V_W), lambda t: (order(t), SWA_Q_W // GLA_V_W))],
        out_specs=[pl.BlockSpec((r, w_qkvg), lambda t: (order(t), 0)), pl.BlockSpec((r, GLA_QK_W), lambda t: (order(t), 0)),
                   pl.BlockSpec((1, GLA_QK_W), lambda t: (0, 0)), pl.BlockSpec((1, GLA_V_W), lambda t: (0, 0))],
        out_shape=[SDS((s, w_qkvg), F32), SDS((s, GLA_QK_W), F32), SDS((1, GLA_QK_W), F32), SDS((1, GLA_V_W), F32)],
        scratch_shapes=[pltpu.VMEM((GLA_V_W, GLA_QK_W), F32)],
        compiler_params=_params("arbitrary"),
    )(p, p, p, p, z, bg, gain, oraw, stsave, dy_all)


def _local_step(x, mem, target, small, big, on_grads):
    g1, gmix, gmem, g2, sqg, skg, sinks, rel_bias, wgu, bg, gla_gain, mqg, mkg = small
    (w3_1a, w3_1b), gather_mix, gather_ffn2 = big
    wgu_pad = jnp.zeros((GLA_QK_W, GLA_QK_W), BF16).at[:GLA_RANK].set(wgu.astype(BF16))
    gain256 = jnp.tile(gla_gain, (1, GLA_HEADS))

    (part,), saved1a = _ffn_fwd(x, g1, w3_1a, "ffn1a", partial=True)
    win_p, wkv, wout = gather_mix(part)
    (x1, h), saved1b = _ffn_fwd(x, g1, w3_1b, "ffn1b", next_gain=gmix, start=(saved1a[0], part))
    w3_2 = gather_ffn2((wout, x1))
    p = _matmul(h, win_p, tm=512, tn=IN_W_PAD, name="mix_in")
    hm = _rms_fwd(mem, gmem, tm=256, name="mem_rms")
    kv = _matmul(hm, wkv, tm=256, tn=512, name="mem_kv")
    p_glr = p[:, COL_GLR:]
    z = _matmul(p_glr, wgu_pad, tm=1024, tn=GLA_QK_W, name="gla_gate")
    y_swa = _swa_fwd(p, sqg, skg, sinks, rel_bias, name="swa_fwd")
    y_gla, oraw, stsave = _gla_fwd(p, z, bg, gain256, name="gla_fwd")
    y_mem = _mem_fwd(p, kv, mqg, mkg, tq=512, name="mem_fwd")
    x2 = _matmul([y_swa, y_gla, y_mem], wout, b_blocks=[0, 2, 3], tm=512, tn=1024, res=x1, name="mix_out")
    (dy, dyb, loss), saved2 = _ffn_fwd(x2, g2, w3_2, "ffn2", target=target)

    dh2, dw3_2 = _ffn_bwd_part(dyb, w3_2, saved2, 0, w3_2.shape[1] // FFN_TN, None, name="ffn2_bwd")
    dx2, dx2b, dg2 = _rms_bwd(x2, g2, dh2, dy, tm=512, name="ffn2_drms")
    dx2b = on_grads("ffn2", [dw3_2], dx2b)
    dy_all = _matmul(dx2b, wout, tb=True, tm=512, tn=1024, name="mix_dy")
    dwout = _matmul(jnp.concatenate([y_swa, y_gla, y_mem], axis=1), dx2b, ta=True, tm=512, tn=1024, out_dtype=BF16,
                    name="mix_dw_out")
    dq_swa, dkv_swa, dsqg, dskg, dsink, drb = _swa_bwd(p, sqg, skg, sinks, rel_bias, dy_all, name="swa_bwd")
    dqkvg, dz, dbg, dgain256 = _gla_bwd(p, z, bg, gain256, oraw, stsave, dy_all, name="gla_bwd")
    dmq, dkv_mem, dmqg, dmkg = _mem_bwd(p, kv, mqg, mkg, dy_all, tq=512, name="mem_bwd")
    dglr = _matmul(dz, wgu_pad, tb=True, tm=1024, tn=GLA_QK_W, name="gla_gate_dx")
    dwgu_pad = _matmul(p_glr, dz, ta=True, tm=GLA_QK_W, tn=GLA_QK_W, name="gla_gate_dw")
    dp = jnp.concatenate([dq_swa, dkv_swa, dqkvg, dmq, dglr], axis=1)
    dwin_p = _matmul(h, dp, ta=True, tm=1024, tn=640, out_dtype=BF16, name="mix_dw_in")
    dx1, dx1b, dgmix = _rms_bwd(x1, gmix, (dp, win_p), dx2, tm=512, name="mix_dh_drms")
    dwkv = _matmul(hm, dkv_mem, ta=True, tm=512, tn=512, out_dtype=BF16, name="mem_dw_kv")
    dx1b = on_grads("mix", (dwin_p, dwkv, dwout), dx1b)
    _, _, dgmem = _rms_bwd(mem, gmem, (dkv_mem, wkv), None, tm=256, name="mem_dh_drms")
    dh1, dw3_1a = _ffn_bwd_part(dx1b, w3_1a, saved1a, 0, w3_1a.shape[1] // FFN_TN, None, name="ffn1_bwd_a")
    dgla_gain = dgain256.reshape(GLA_HEADS, GLA_DV).sum(axis=0, keepdims=True)
    dsmall = [dgmix, dgmem, dg2, dsqg, dskg, dsink[:, :SWA_HEADS], drb[:, :SWA_HEADS].T, dwgu_pad[:GLA_RANK], dbg,
              dgla_gain, dmqg, dmkg, loss]
    dh1, dgmem, dwgu_pad = on_grads("ffn1a", [dw3_1a], (dh1, dgmem, dwgu_pad), small=dsmall)
    dh1, dw3_1b = _ffn_bwd_part(dx1b, w3_1b, saved1b, 0, w3_1b.shape[1] // FFN_TN, dh1, name="ffn1_bwd_b")
    dx, _, dg1 = _rms_bwd(x, g1, dh1, dx1, tm=512, name="ffn1_drms")
    on_grads("ffn1b", [dw3_1b], None, small=[dg1])
    return dx


def _mesh_place():
    x, y, c = lax.axis_index("x"), lax.axis_index("y"), lax.axis_index("c")
    other_chips = [(1 - x, y), (x, 1 - y), (1 - x, 1 - y)]
    return x, y, c, other_chips


def _handshake(peers):
    barrier = pltpu.get_barrier_semaphore()
    for peer in peers:
        pl.semaphore_signal(barrier, inc=1, device_id=peer, device_id_type=MESH)
    pl.semaphore_wait(barrier, len(peers))


def _sequencer_call(body, operands, out_shapes, sems, *, name, collective_id):
    return pl.kernel(
        body, name=name, out_type=out_shapes, mesh=plsc.ScalarSubcoreMesh(axis_name="sequencer", num_cores=1),
        scratch_types=sems, compiler_params=pltpu.CompilerParams(collective_id=collective_id),
    )(*operands)


def _window(ref, kind, slot, shape):
    if kind == "row":
        rows = pl.ds(pl.multiple_of(slot * shape[-2], 8), shape[-2])
        return ref.at[(slice(None),) * (len(shape) - 2) + (rows,)]
    return ref.at[slot]


def _gathered(shape, kind):
    if kind == "row":
        return tuple(shape[:-2]) + (N_DEV * shape[-2], shape[-1])
    return (N_DEV,) + tuple(shape)


def _half(view, hf):
    if len(view.shape) == 4:
        return view.at[:, hf]
    n = view.shape[-2] // 2
    return view.at[(slice(None),) * (len(view.shape) - 2) + (pl.ds(hf * n, n),)]


def _all_gather(shards, kinds, *, name, collective_id):
    nt = len(shards)

    def body(*refs):
        x_refs, o_refs = refs[:nt], refs[nt:2 * nt]
        send_sems, recv_sems, local_sems = refs[2 * nt:]
        x, y, c, _ = _mesh_place()
        me, sibling, xn, yn, diag = (x, y, c), (x, y, 1 - c), (1 - x, y, c), (x, 1 - y, c), (1 - x, 1 - y, c)
        _handshake([sibling, xn, yn])

        def win(t, block):
            bx, by, bc = block
            return _window(o_refs[t], kinds[t], 4 * bx + 2 * by + bc, shards[t].shape)

        def copy(k, t, src, dst, to):
            return pltpu.make_async_remote_copy(src_ref=src, dst_ref=dst, send_sem=send_sems.at[k, t],
                                                recv_sem=recv_sems.at[k, t], device_id=to, device_id_type=MESH)

        def piece(k, t, block, hf, to, from_shard=False):
            dst = _half(win(t, block), hf)
            return copy(k, t, _half(x_refs[t], hf) if from_shard else dst, dst, to)

        mine = [pltpu.make_async_copy(x_refs[t], win(t, me), local_sems.at[t]) for t in range(nt)]
        sent = []

        def start(cp):
            cp.start()
            sent.append(cp)

        for cp in mine:
            cp.start()
        for t in range(nt):
            start(copy(0, t, x_refs[t], win(t, me), sibling))
        for hf_x, hf_y in ((0, 1), (1, 0)):
            for t in range(nt):
                start(piece(1 + hf_x, t, me, hf_x, xn, True))
                start(piece(3 + hf_y, t, me, hf_y, yn, True))
        for k, block, hf, onward, k_sib in ((1, xn, 0, (5, yn), 7), (4, yn, 1, (6, xn), 10), (2, xn, 1, None, 8),
                                           (3, yn, 0, None, 9), (5, diag, 0, None, 11), (6, diag, 1, None, 12)):
            for t in range(nt):
                piece(k, t, block, hf, me).wait_recv()
                if onward is not None:
                    start(piece(onward[0], t, block, hf, onward[1]))
                start(piece(k_sib, t, block, hf, sibling))
        for t in range(nt):
            copy(0, t, x_refs[t], win(t, sibling), me).wait_recv()
        for k_sib, block, hf in ((7, xn, 0), (10, yn, 1), (8, xn, 1), (9, yn, 0), (11, diag, 0), (12, diag, 1)):
            for t in range(nt):
                bx, by, _ = block
                piece(k_sib, t, (bx, by, 1 - c), hf, me).wait_recv()
        for cp in sent:
            cp.wait_send()
        for cp in mine:
            cp.wait()

    return _sequencer_call(
        body, shards, [SDS(_gathered(s.shape, k), s.dtype) for s, k in zip(shards, kinds)],
        [pltpu.SemaphoreType.DMA((13, nt)), pltpu.SemaphoreType.DMA((13, nt)), pltpu.SemaphoreType.DMA((nt,))],
        name=name, collective_id=collective_id)


def _part_shape(shape, kind):
    if kind == "row":
        return tuple(shape[:-2]) + (shape[-2] // N_DEV, shape[-1])
    return tuple(shape[2:])


def _pair_exchange(grads, kinds, *, name, collective_id):
    nt = len(grads)
    part = [_part_shape(g.shape, k) for g, k in zip(grads, kinds)]

    def body(*refs):
        g_refs, o_refs = refs[:nt], refs[nt:2 * nt]
        send_sems, recv_sems = refs[2 * nt:]
        x, y, c, _ = _mesh_place()
        _handshake([(x, y, 1 - c)])
        copies = []
        for t in range(nt):
            for xy in range(4):
                src = g_refs[t].at[1 - c, xy] if kinds[t] == "stack" else _window(g_refs[t], kinds[t], 2 * xy + 1 - c, part[t])
                copies.append(pltpu.make_async_remote_copy(
                    src_ref=src, dst_ref=o_refs[t].at[xy], send_sem=send_sems.at[xy, t], recv_sem=recv_sems.at[xy, t],
                    device_id=(x, y, 1 - c), device_id_type=MESH))
        for cp in copies:
            cp.start()
        for cp in copies:
            cp.wait()

    return _sequencer_call(
        body, grads, [SDS((4,) + p, g.dtype) for p, g in zip(part, grads)],
        [pltpu.SemaphoreType.DMA((4, nt)), pltpu.SemaphoreType.DMA((4, nt))], name=name, collective_id=collective_id)


def _chip_exchange(parts, small, *, name, collective_id):
    nt = len(parts)
    if small is None:
        def body_plain(*refs):
            s_refs, o_refs = refs[:nt], refs[nt:2 * nt]
            send_sems, recv_sems = refs[2 * nt:]
            x, y, c, chips = _mesh_place()
            _handshake([(*chip, c) for chip in chips])
            copies = [pltpu.make_async_remote_copy(
                src_ref=s_refs[t].at[2 * chip[0] + chip[1]], dst_ref=o_refs[t].at[j],
                send_sem=send_sems.at[j, t], recv_sem=recv_sems.at[j, t], device_id=(*chip, c), device_id_type=MESH)
                for j, chip in enumerate(chips) for t in range(nt)]
            for cp in copies:
                cp.start()
            for cp in copies:
                cp.wait()

        return _sequencer_call(
            body_plain, parts, [SDS((3,) + s.shape[1:], s.dtype) for s in parts],
            [pltpu.SemaphoreType.DMA((3, nt)), pltpu.SemaphoreType.DMA((3, nt))], name=name, collective_id=collective_id)

    def body(*refs):
        s_refs, small_ref = refs[:nt], refs[nt]
        o_refs, small_all = refs[nt + 1:2 * nt + 1], refs[2 * nt + 1]
        send_sems, recv_sems, small_send, small_recv, local_sem = refs[2 * nt + 2:]
        x, y, c, chips = _mesh_place()
        _handshake([(px, py, pc) for px in (x, 1 - x) for py in (y, 1 - y) for pc in (c, 1 - c)][1:])

        def copy(j, t, chip):
            return pltpu.make_async_remote_copy(
                src_ref=s_refs[t].at[2 * chip[0] + chip[1]], dst_ref=o_refs[t].at[j],
                send_sem=send_sems.at[j, t], recv_sem=recv_sems.at[j, t], device_id=(*chip, c), device_id_type=MESH)

        flips = [(fx, fy, fc) for fx in (0, 1) for fy in (0, 1) for fc in (0, 1)][1:]

        def small_copy(k):
            fx, fy, fc = flips[k]
            to = (x ^ fx if fx else x, y ^ fy if fy else y, c ^ fc if fc else c)
            rows = small_all.at[4 * x + 2 * y + c]
            return pltpu.make_async_remote_copy(
                src_ref=small_ref, dst_ref=rows, send_sem=small_send.at[k], recv_sem=small_recv.at[k],
                device_id=to, device_id_type=MESH)

        own = pltpu.make_async_copy(small_ref, small_all.at[4 * x + 2 * y + c], local_sem)
        own.start()
        copies = [copy(j, t, chip) for j, chip in enumerate(chips) for t in range(nt)]
        smalls = [small_copy(k) for k in range(7)]
        for cp in smalls + copies:
            cp.start()
        for cp in smalls + copies:
            cp.wait()
        own.wait()

    return _sequencer_call(
        body, list(parts) + [small],
        [SDS((3,) + s.shape[1:], s.dtype) for s in parts] + [SDS((N_DEV,) + small.shape, small.dtype)],
        [pltpu.SemaphoreType.DMA((3, nt)), pltpu.SemaphoreType.DMA((3, nt)),
         pltpu.SemaphoreType.DMA((7,)), pltpu.SemaphoreType.DMA((7,)), pltpu.SemaphoreType.DMA],
        name=name, collective_id=collective_id)


def _pair_sum(grad, theirs, kind, c, *, name):
    if kind == "row":
        r, l = theirs.shape[-2:]
        n = theirs.size // (4 * r * l)
        grad, theirs = grad.reshape(n, N_DEV * r, l), theirs.reshape(4, n, r, l)
        mine_spec = pl.BlockSpec((n, r, l), lambda xy, c_ref: (0, 2 * xy + c_ref[0], 0))
    else:
        r, l = theirs.shape[-2:]
        n = theirs.size // (4 * r * l)
        theirs = theirs.reshape(4, n, r, l)
        grad = grad.reshape(2, 4, n, r, l)
        mine_spec = pl.BlockSpec((None, None, n, r, l), lambda xy, c_ref: (c_ref[0], xy, 0, 0, 0))

    def body(c_ref, a_ref, b_ref, o_ref):
        o_ref[...] = (a_ref[...].astype(F32) + b_ref[...].astype(F32)).astype(BF16)

    part = pl.BlockSpec((None, n, r, l), lambda xy, c_ref: (xy, 0, 0, 0))
    return pl.pallas_call(
        body, name=name,
        grid_spec=pltpu.PrefetchScalarGridSpec(num_scalar_prefetch=1, grid=(4,), in_specs=[mine_spec, part], out_specs=part),
        out_shape=SDS((4, n, r, l), BF16), compiler_params=_params("parallel"),
    )(c, grad, theirs)


def _adamw(w, g, m, v):
    m = ADAM_B1 * m + (1.0 - ADAM_B1) * g
    v = ADAM_B2 * v + (1.0 - ADAM_B2) * jnp.square(g)
    m_hat = m / (1.0 - ADAM_B1 ** ADAM_STEP)
    v_hat = v / (1.0 - ADAM_B2 ** ADAM_STEP)
    delta = -ADAM_LR * (m_hat / (jnp.sqrt(v_hat) + ADAM_EPS) + ADAM_WD * w)
    return delta, m, v


def _adam_big(owns, others, mat, xy, w, m, v, *, tr, name):
    _, r, l = w.shape
    lp = owns[0].shape[-1]
    nq = len(owns)
    rows = [tr] if nq == 1 else [o.shape[-2] for o in owns]
    assert sum(rows) == tr and r % tr == 0, (r, tr, rows)

    def body(xy_ref, *refs):
        own_refs, oth_refs = refs[:nq], refs[nq:2 * nq]
        w_ref, m_ref, v_ref, g_out, d_out, m_out, v_out = refs[2 * nq:]
        parts = []
        for q in range(nq):
            gq = own_refs[q][0, 0].astype(F32)
            for j in range(3):
                gq = gq + oth_refs[q][j, 0].astype(F32)
            parts.append(gq)
        g = (parts[0] if nq == 1 else jnp.concatenate(parts, axis=0))[:, :l]
        delta, m_new, v_new = _adamw(w_ref[0], g, m_ref[0], v_ref[0])
        g_out[0] = g
        d_out[0] = delta
        m_out[0] = m_new
        v_out[0] = v_new

    blk = pl.BlockSpec((1, tr, l), lambda i, xy_ref: (0, i, 0))
    own_specs = [pl.BlockSpec((1, 1, n, lp), lambda i, xy_ref: (xy_ref[0], mat, i, 0)) for n in rows]
    oth_specs = [pl.BlockSpec((3, 1, n, lp), lambda i, xy_ref: (0, mat, i, 0)) for n in rows]
    return pl.pallas_call(
        body, name=name,
        grid_spec=pltpu.PrefetchScalarGridSpec(
            num_scalar_prefetch=1, grid=(r // tr,), in_specs=own_specs + oth_specs + [blk, blk, blk],
            out_specs=[blk, blk, blk, blk]),
        out_shape=[SDS(w.shape, F32)] * 4, compiler_params=_params("parallel"),
    )(xy, *owns, *others, w, m, v)


def _adam_cols(own, other, xy, w, m, v, *, tc, name):
    _, wc, d = w.shape
    wp = -(-wc // 128) * 128

    def body(xy_ref, own_ref, oth_ref, w_ref, m_ref, v_ref, g_out, d_out, m_out, v_out, buf):
        g = own_ref[0, 0].astype(F32)
        for j in range(3):
            g = g + oth_ref[j, 0].astype(F32)
        buf[:, :wc] = g
        buf[:, wc:] = jnp.zeros((tc, wp - wc), F32)
        g = buf[...].T[:wc]
        delta, m_new, v_new = _adamw(w_ref[0], g, m_ref[0], v_ref[0])
        g_out[0] = g
        d_out[0] = delta
        m_out[0] = m_new
        v_out[0] = v_new

    blk = pl.BlockSpec((1, wc, tc), lambda i, xy_ref: (0, 0, i))
    in_specs = [pl.BlockSpec((1, 1, tc, wc), lambda i, xy_ref: (xy_ref[0], 0, i, 0)),
                pl.BlockSpec((3, 1, tc, wc), lambda i, xy_ref: (0, 0, i, 0)), blk, blk, blk]
    return pl.pallas_call(
        body, name=name,
        grid_spec=pltpu.PrefetchScalarGridSpec(
            num_scalar_prefetch=1, grid=(d // tc,), in_specs=in_specs, out_specs=[blk, blk, blk, blk],
            scratch_shapes=[pltpu.VMEM((tc, wp), F32)]),
        out_shape=[SDS(w.shape, F32)] * 4, compiler_params=_params("parallel"),
    )(xy, own, other, w, m, v)


def _small_layout(shapes):
    out, at = [], 0
    for r, c in shapes:
        rows = c // 128 if (r == 1 and c > 128) else r
        out.append((at, rows))
        at += -(-rows // 8) * 8
    return out, at


def _pack_small(parts, *, name):
    shapes = [a.shape for a in parts]
    layout, total = _small_layout(shapes)

    def body(*refs):
        o_ref = refs[-1]
        o_ref[...] = jnp.zeros_like(o_ref)
        for x_ref, (r, c), (at, rows) in zip(refs, shapes, layout):
            if r == 1 and c > 128:
                for k in range(rows):
                    o_ref[at + k:at + k + 1, :] = x_ref[:, k * 128:(k + 1) * 128]
            else:
                o_ref[at:at + r, 0:c] = x_ref[...]

    return pl.pallas_call(body, name=name, out_shape=SDS((total, 128), F32))(*parts)


def _adam_small(g_all, ws, ms, vs, *, name):
    n = len(ws)
    shapes = [w.shape for w in ws]
    layout, _ = _small_layout(shapes)

    def body(g_ref, *refs):
        w_refs, m_refs, v_refs, outs = refs[:n], refs[n:2 * n], refs[2 * n:3 * n], refs[3 * n:]
        g_sum = g_ref[0]
        for k in range(1, N_DEV):
            g_sum = g_sum + g_ref[k]
        for i, ((r, c), (at, rows)) in enumerate(zip(shapes, layout)):
            if r == 1 and c > 128:
                g = jnp.concatenate([g_sum[at + k:at + k + 1, :] for k in range(rows)], axis=1)
            else:
                g = g_sum[at:at + r, 0:c]
            delta, m_new, v_new = _adamw(w_refs[i][...], g, m_refs[i][...], v_refs[i][...])
            for q, val in enumerate((g, delta, m_new, v_new)):
                outs[4 * i + q][...] = val

    flat = pl.pallas_call(body, name=name, out_shape=[SDS(s, F32) for s in shapes for _ in range(4)])(g_all, *ws, *ms, *vs)
    return [flat[4 * i:4 * i + 4] for i in range(n)]


def kernel(x, mem, ffn1_norm, ffn1_w_gate, ffn1_w_up, ffn1_w_down, mix_norm, mem_norm, w_in, w_mem_kv, swa_q_norm, swa_k_norm, swa_sinks, rel_bias, gla_w_gate_up, gla_b_gate, gla_out_norm, mem_q_norm, mem_k_norm, w_out, ffn2_norm, ffn2_w_gate, ffn2_w_up, ffn2_w_down, loss_target, m_ffn1_norm, m_ffn1_w_gate, m_ffn1_w_up, m_ffn1_w_down, m_mix_norm, m_mem_norm, m_w_in, m_w_mem_kv, m_swa_q_norm, m_swa_k_norm, m_swa_sinks, m_rel_bias, m_gla_w_gate_up, m_gla_b_gate, m_gla_out_norm, m_mem_q_norm, m_mem_k_norm, m_w_out, m_ffn2_norm, m_ffn2_w_gate, m_ffn2_w_up, m_ffn2_w_down, v_ffn1_norm, v_ffn1_w_gate, v_ffn1_w_up, v_ffn1_w_down, v_mix_norm, v_mem_norm, v_w_in, v_w_mem_kv, v_swa_q_norm, v_swa_k_norm, v_swa_sinks, v_rel_bias, v_gla_w_gate_up, v_gla_b_gate, v_gla_out_norm, v_mem_q_norm, v_mem_k_norm, v_w_out, v_ffn2_norm, v_ffn2_w_gate, v_ffn2_w_up, v_ffn2_w_down):
    xi, yi, ci = lax.axis_index("x"), lax.axis_index("y"), lax.axis_index("c")
    c_arr = jnp.reshape(ci, (1,)).astype(jnp.int32)
    xy_arr = jnp.reshape(2 * xi + yi, (1,)).astype(jnp.int32)
    d = x.shape[-1]

    def ffn_shards(wg_s, wu_s, wd_s):
        return jnp.concatenate([wg_s.transpose(0, 2, 1), wu_s.transpose(0, 2, 1), wd_s], axis=0).astype(BF16)

    def gather_ffn(wg_s, wu_s, wd_s, name, collective_id, after):
        w3_s, _ = lax.optimization_barrier((ffn_shards(wg_s, wu_s, wd_s), after))
        return _all_gather([w3_s], ["row"], name=name, collective_id=collective_id)[0]

    w3_s = ffn_shards(ffn1_w_gate, ffn1_w_up, ffn1_w_down)
    w3_1a = _all_gather([w3_s[:, :FFN1_FIRST]], ["row"], name="gather_ffn1a", collective_id=0)[0]
    w3_s, _ = lax.optimization_barrier((w3_s, w3_1a))
    w3_1b = _all_gather([w3_s[:, FFN1_FIRST:]], ["row"], name="gather_ffn1b", collective_id=11)[0]

    def gather_mix(after):
        mix_s = lax.optimization_barrier((w_in[0].astype(BF16), w_mem_kv[0].astype(BF16), w_out[0].astype(BF16),
                                          (w3_1b, after)))[:3]
        win_all, wkv, wout = _all_gather(list(mix_s), ["stack", "row", "row"], name="gather_mix", collective_id=1)
        return _pack_win(win_all, tr=256, name="pack_w_in"), wkv, wout

    def gather_ffn2(after):
        return gather_ffn(ffn2_w_gate, ffn2_w_up, ffn2_w_down, "gather_ffn2", 2, after)

    small_w = [ffn1_norm, mix_norm, mem_norm, ffn2_norm, swa_q_norm, swa_k_norm, swa_sinks[0], rel_bias,
               gla_w_gate_up[0], gla_b_gate, gla_out_norm, mem_q_norm, mem_k_norm]
    collective_ids = {"ffn2": (3, 4), "mix": (5, 6), "ffn1a": (7, 8), "ffn1b": (9, 10)}
    reduced, small_box = {}, {}

    def on_grads(group, grads, carry, small=None):
        if group == "mix":
            dwin_p, dwkv, dwout = grads
            grads = [_unpack_win(dwin_p, tr=256, name="unpack_dw_in"), dwkv, dwout]
            kinds = ["stack", "row", "row"]
        else:
            kinds = ["row"]
        if reduced:
            earlier = list(reduced.values())[-1][1]
            *grads, _ = lax.optimization_barrier((*grads, earlier[0]))
        id_pair, id_chip = collective_ids[group]
        from_sibling = _pair_exchange(grads, kinds, name=f"pair_exchange_{group}", collective_id=id_pair)
        chip_sums = [_pair_sum(g, theirs, k, c_arr, name=f"pair_sum_{group}_{t}")
                     for t, (g, theirs, k) in enumerate(zip(grads, from_sibling, kinds))]
        if carry is not None:
            *chip_sums, carry = lax.optimization_barrier((*chip_sums, carry))
        if small is None:
            from_chips = _chip_exchange(chip_sums, None, name=f"chip_exchange_{group}", collective_id=id_chip)
        else:
            packed = _pack_small(small, name=f"pack_small_{group}")
            *from_chips, small_all = _chip_exchange(chip_sums, packed, name=f"chip_exchange_{group}",
                                                    collective_id=id_chip)
            small_box[group] = small_all
        reduced[group] = (chip_sums, from_chips)
        return carry

    grad_x = _local_step(x[0], mem[0], loss_target[0], small_w, ((w3_1a, w3_1b), gather_mix, gather_ffn2), on_grads)

    big_w = {"ffn1_w_gate": ("ffn1", 0, 0, True, ffn1_w_gate, m_ffn1_w_gate, v_ffn1_w_gate),
             "ffn1_w_up": ("ffn1", 0, 1, True, ffn1_w_up, m_ffn1_w_up, v_ffn1_w_up),
             "ffn1_w_down": ("ffn1", 0, 2, False, ffn1_w_down, m_ffn1_w_down, v_ffn1_w_down),
             "w_in": ("mix", 0, 0, True, w_in, m_w_in, v_w_in),
             "w_mem_kv": ("mix", 1, 0, False, w_mem_kv, m_w_mem_kv, v_w_mem_kv),
             "w_out": ("mix", 2, 0, False, w_out, m_w_out, v_w_out),
             "ffn2_w_gate": ("ffn2", 0, 0, True, ffn2_w_gate, m_ffn2_w_gate, v_ffn2_w_gate),
             "ffn2_w_up": ("ffn2", 0, 1, True, ffn2_w_up, m_ffn2_w_up, v_ffn2_w_up),
             "ffn2_w_down": ("ffn2", 0, 2, False, ffn2_w_down, m_ffn2_w_down, v_ffn2_w_down)}
    res = {}
    for nm, (group, t, mat, transposed, w, m, v) in big_w.items():
        shape = w.shape
        if transposed:
            w, m, v = (a.transpose(0, 2, 1) for a in (w, m, v))
        if nm == "w_in":
            out = _adam_cols(reduced[group][0][t], reduced[group][1][t], xy_arr, w, m, v, tc=256, name=f"adam_{nm}")
            res[nm] = [a.transpose(0, 2, 1) for a in out]
            continue
        r = w.shape[1]
        halves = ["ffn1a", "ffn1b"] if group == "ffn1" else [group]
        if len(halves) == 1:
            tr = 256 if r % 256 == 0 else r
        else:
            tr = r
        out = _adam_big([reduced[k][0][t] for k in halves], [reduced[k][1][t] for k in halves], mat, xy_arr, w, m, v,
                        tr=tr, name=f"adam_{nm}")
        if transposed:
            out = [a.reshape(1, -1, d).transpose(0, 2, 1) for a in out]
        res[nm] = [a.reshape(shape) for a in out]
    small_names = ["ffn1_norm", "mix_norm", "mem_norm", "ffn2_norm", "swa_q_norm", "swa_k_norm", "swa_sinks", "rel_bias",
                   "gla_w_gate_up", "gla_b_gate", "gla_out_norm", "mem_q_norm", "mem_k_norm"]
    small_m = [m_ffn1_norm, m_mix_norm, m_mem_norm, m_ffn2_norm, m_swa_q_norm, m_swa_k_norm, m_swa_sinks, m_rel_bias,
               m_gla_w_gate_up, m_gla_b_gate, m_gla_out_norm, m_mem_q_norm, m_mem_k_norm]
    small_v = [v_ffn1_norm, v_mix_norm, v_mem_norm, v_ffn2_norm, v_swa_q_norm, v_swa_k_norm, v_swa_sinks, v_rel_bias,
               v_gla_w_gate_up, v_gla_b_gate, v_gla_out_norm, v_mem_q_norm, v_mem_k_norm]
    small_full = [ffn1_norm, mix_norm, mem_norm, ffn2_norm, swa_q_norm, swa_k_norm, swa_sinks, rel_bias,
                  gla_w_gate_up, gla_b_gate, gla_out_norm, mem_q_norm, mem_k_norm]
    zero = jnp.zeros((1, 1), F32)
    turned = ("rel_bias",)

    def two_d(nm, a):
        a = a.reshape(a.shape[-2:])
        return a.T if nm in turned else a

    for group, sel in (("ffn1a", slice(1, None)), ("ffn1b", slice(0, 1))):
        extra = [zero] if group == "ffn1a" else []
        ws, ms, vs = ([two_d(nm, a) for nm, a in zip(small_names[sel], arrs[sel])] + extra
                      for arrs in (small_full, small_m, small_v))
        updated = _adam_small(small_box[group], ws, ms, vs, name=f"adam_small_{group}")
        for nm, full, out in zip(small_names[sel], small_full[sel], updated):
            res[nm] = [(a.T if nm in turned else a).reshape(full.shape) for a in out]
        if extra:
            loss = updated[-1][0].reshape(())

    order = ["ffn1_norm", "ffn1_w_gate", "ffn1_w_up", "ffn1_w_down", "mix_norm", "mem_norm", "w_in", "w_mem_kv",
             "swa_q_norm", "swa_k_norm", "swa_sinks", "rel_bias", "gla_w_gate_up", "gla_b_gate", "gla_out_norm",
             "mem_q_norm", "mem_k_norm", "w_out", "ffn2_norm", "ffn2_w_gate", "ffn2_w_up", "ffn2_w_down"]
    outs = [loss, grad_x[None]]
    for q in range(4):
        outs += [res[nm][q] for nm in order]
    return tuple(outs)
```

```python
import functools
import math

import numpy as np
import jax
import jax.numpy as jnp
from jax import lax
from jax.experimental import pallas as pl
from jax.experimental.pallas import tpu as pltpu
from jax.experimental.pallas import tpu_sc as plsc

F32 = jnp.float32
BF16 = jnp.bfloat16
SDS = jax.ShapeDtypeStruct

EPS = 1e-6
HEAD_DIM = 64
SWA_HEADS = 8
SWA_KV_HEADS = 2
SWA_GROUP = SWA_HEADS // SWA_KV_HEADS
BLOCK = 128
N_BUCKETS = 32
MAX_DISTANCE = 128
GLA_HEADS = 4
GLA_DK = 32
GLA_DV = 64
GLA_RANK = 16
GLA_TAU = 16.0
GLA_CHUNK = 32
MEM_HEADS = 4
SWA_Q_W = SWA_HEADS * HEAD_DIM
SWA_KV_W = SWA_KV_HEADS * HEAD_DIM
GLA_QK_W = GLA_HEADS * GLA_DK
GLA_V_W = GLA_HEADS * GLA_DV
MEM_Q_W = MEM_HEADS * HEAD_DIM
IN_W = 1808
IN_W_PAD = 1920
COL_SQ, COL_SKV, COL_GQ, COL_GK, COL_GV, COL_GG, COL_MQ, COL_GLR = 0, 512, 768, 896, 1024, 1280, 1536, 1792

ADAM_LR = 0.001
ADAM_B1 = 0.9
ADAM_B2 = 0.999
ADAM_EPS = 1e-08
ADAM_WD = 0.01
ADAM_STEP = 10

N_DEV = 8
VMEM_LIMIT_BYTES = 56 * 1024 * 1024
MESH = pl.DeviceIdType.MESH


def _params(*sem):
    return pltpu.CompilerParams(dimension_semantics=sem or None, vmem_limit_bytes=VMEM_LIMIT_BYTES)


def _dot(a, b, ta, tb, precision=None):
    dims = (((0 if ta else 1,), (1 if tb else 0,)), ((), ()))
    return lax.dot_general(a, b, dims, preferred_element_type=F32, precision=precision)


def _mm_raw(a, b, ta=False, tb=False):
    return _dot(a.astype(BF16), b.astype(BF16), ta, tb)


def _mmf_raw(a, b, ta=False, tb=False):
    return _dot(a, b, ta, tb, lax.Precision.HIGHEST)


def _make_mm(raw):
    @functools.partial(jax.custom_vjp, nondiff_argnums=(2, 3))
    def mm(a, b, ta=False, tb=False):
        return raw(a, b, ta, tb)

    def fwd(a, b, ta, tb):
        return raw(a, b, ta, tb), (a, b)

    def bwd(ta, tb, res, g):
        a, b = res
        da = raw(b, g, tb, True) if ta else raw(g, b, False, not tb)
        db = raw(g, a, True, ta) if tb else raw(a, g, not ta, False)
        return da, db

    mm.defvjp(fwd, bwd)
    return mm


_mm = _make_mm(_mm_raw)
_mmf = _make_mm(_mmf_raw)


def _mm3(a, b, ta=False, tb=False):
    a_hi, b_hi = a.astype(BF16).astype(F32), b.astype(BF16).astype(F32)
    return _mm(a_hi, b_hi, ta, tb) + _mm(a_hi, b - b_hi, ta, tb) + _mm(a - a_hi, b_hi, ta, tb)


def _rms(x, g):
    return x * lax.rsqrt(jnp.mean(x * x, axis=-1, keepdims=True) + EPS) * g


def _silu_mul(g, u):
    return jax.nn.silu(g) * u


def _log_sigmoid(z):
    return jnp.minimum(z, 0.0) - jnp.log(1.0 + jnp.exp(-jnp.abs(z)))


def _matmul(a_list, b, *, ta=False, tb=False, tm, tn, b_blocks=None, res=None, scale=1.0, out_dtype=F32, name):
    if not isinstance(a_list, (list, tuple)):
        a_list = [a_list]
    n_a = len(a_list)
    m = a_list[0].shape[1] if ta else a_list[0].shape[0]
    ks = [a.shape[0] if ta else a.shape[1] for a in a_list]
    n = b.shape[0] if tb else b.shape[1]
    if b_blocks is None:
        assert n_a == 1
        b_blocks = [0]
    tm, tn = min(tm, m), min(tn, n)
    assert m % tm == 0 and n % tn == 0, (m, n, tm, tn)

    def body(*refs):
        a_refs, b_refs = refs[:n_a], refs[n_a:2 * n_a]
        r_ref = refs[2 * n_a] if res is not None else None
        o_ref = refs[-1]
        acc = _mm_raw(a_refs[0][...], b_refs[0][...], ta, tb)
        for k in range(1, n_a):
            acc = acc + _mm_raw(a_refs[k][...], b_refs[k][...], ta, tb)
        if scale != 1.0:
            acc = acc * scale
        if r_ref is not None:
            acc = r_ref[...] + acc
        o_ref[...] = acc.astype(out_dtype)

    in_specs = []
    for k in ks:
        in_specs.append(pl.BlockSpec((k, tm), lambda i, j: (0, i)) if ta else pl.BlockSpec((tm, k), lambda i, j: (i, 0)))
    for k, blk in zip(ks, b_blocks):
        if tb:
            in_specs.append(pl.BlockSpec((tn, k), functools.partial(lambda i, j, blk: (j, blk), blk=blk)))
        else:
            in_specs.append(pl.BlockSpec((k, tn), functools.partial(lambda i, j, blk: (blk, j), blk=blk)))
    args = list(a_list) + [b] * n_a
    if res is not None:
        in_specs.append(pl.BlockSpec((tm, tn), lambda i, j: (i, j)))
        args.append(res)
    return pl.pallas_call(
        body, name=name, grid=(m // tm, n // tn), in_specs=in_specs,
        out_specs=pl.BlockSpec((tm, tn), lambda i, j: (i, j)), out_shape=SDS((m, n), out_dtype),
        compiler_params=_params("parallel", "parallel"),
    )(*args)


def _win_pieces(w):
    glr_lo, glr_hi = COL_MQ, COL_MQ + GLA_RANK
    out = []
    for j in range(N_DEV):
        for lo, hi, shift in ((0, glr_lo, 0), (glr_lo, glr_hi, COL_GLR - glr_lo), (glr_hi, IN_W, COL_MQ - glr_hi)):
            s, e = max(j * w, lo), min((j + 1) * w, hi)
            if s < e:
                out.append((j, s - j * w, e - j * w, s + shift))
    return out


def _pack_win(win_all, *, tr, name):
    _, d, w = win_all.shape

    def body(i_ref, o_ref):
        for j, a, b, dst in _win_pieces(w):
            o_ref[:, dst:dst + b - a] = i_ref[j][:, a:b]
        o_ref[:, IN_W:] = jnp.zeros((tr, IN_W_PAD - IN_W), o_ref.dtype)

    return pl.pallas_call(
        body, name=name, grid=(d // tr,), in_specs=[pl.BlockSpec((N_DEV, tr, w), lambda i: (0, i, 0))],
        out_specs=pl.BlockSpec((tr, IN_W_PAD), lambda i: (i, 0)), out_shape=SDS((d, IN_W_PAD), win_all.dtype),
        compiler_params=_params("parallel"),
    )(win_all)


def _unpack_win(dwin_parts, *, tr, name):
    d = dwin_parts[0].shape[0]
    w = IN_W // N_DEV
    starts = [sum(p.shape[1] for p in dwin_parts[:k]) for k in range(len(dwin_parts) + 1)]

    def body(*refs):
        o_ref = refs[-1]
        for j, a, b, src in _win_pieces(w):
            for k, i_ref in enumerate(refs[:-1]):
                lo, hi = max(src, starts[k]), min(src + b - a, starts[k + 1])
                if lo < hi:
                    o_ref[j % 2, j // 2, :, a + lo - src:a + hi - src] = i_ref[:, lo - starts[k]:hi - starts[k]]

    return pl.pallas_call(
        body, name=name, grid=(d // tr,), in_specs=[pl.BlockSpec((tr, p.shape[1]), lambda i: (i, 0)) for p in dwin_parts],
        out_specs=pl.BlockSpec((2, 4, tr, w), lambda i: (0, 0, i, 0)), out_shape=SDS((2, 4, d, w), dwin_parts[0].dtype),
        compiler_params=_params("parallel"),
    )(*dwin_parts)


def _dw_parts(h, parts, *, tk, name):
    s, d = h.shape
    n_p = len(parts)

    def body(*refs):
        h_ref, p_refs, o_refs, acc_refs = refs[0], refs[1:1 + n_p], refs[1 + n_p:1 + 2 * n_p], refs[1 + 2 * n_p:]
        i = pl.program_id(0)
        for p_ref, o_ref, acc_ref in zip(p_refs, o_refs, acc_refs):
            prod = _mm_raw(h_ref[...], p_ref[...], True, False)

            @pl.when(i == 0)
            def _():
                acc_ref[...] = prod

            @pl.when(i > 0)
            def _():
                acc_ref[...] += prod

            @pl.when(i == pl.num_programs(0) - 1)
            def _():
                o_ref[...] = acc_ref[...].astype(BF16)

    return pl.pallas_call(
        body, name=name, grid=(s // tk,),
        in_specs=[pl.BlockSpec((tk, d), lambda i: (i, 0))] + [pl.BlockSpec((tk, p.shape[1]), lambda i: (i, 0)) for p in parts],
        out_specs=[pl.BlockSpec((d, p.shape[1]), lambda i: (0, 0)) for p in parts],
        out_shape=[SDS((d, p.shape[1]), BF16) for p in parts],
        scratch_shapes=[pltpu.VMEM((d, p.shape[1]), F32) for p in parts],
        compiler_params=_params("arbitrary"),
    )(h, *parts)


def _rms_fwd(x, g, *, tm, name):
    s, d = x.shape

    def body(x_ref, g_ref, h_ref):
        h_ref[...] = _rms(x_ref[...], g_ref[...]).astype(BF16)

    return pl.pallas_call(
        body, name=name, grid=(s // tm,),
        in_specs=[pl.BlockSpec((tm, d), lambda i: (i, 0)), pl.BlockSpec((1, d), lambda i: (0, 0))],
        out_specs=pl.BlockSpec((tm, d), lambda i: (i, 0)), out_shape=SDS((s, d), BF16),
        compiler_params=_params("parallel"),
    )(x, g)


def _rms_bwd(x, g, dh, dres, *, tm, name):
    s, d = x.shape
    want_dx = dres is not None
    product = isinstance(dh, tuple)
    if product:
        a_parts = list(dh[0]) if isinstance(dh[0], (list, tuple)) else [dh[0]]
        starts = [sum(a.shape[1] for a in a_parts[:k]) for k in range(len(a_parts) + 1)]

    def body(*refs):
        n_dh = len(a_parts) + 1 if product else 1
        x_ref, g_ref = refs[:2]
        dh_refs, rest = refs[2:2 + n_dh], refs[2 + n_dh:]
        if want_dx:
            dres_ref, dx_ref, dxb_ref, dg_ref = rest
        else:
            dg_ref, = rest
        if product:
            b_ref = dh_refs[-1]
            dh_tile = _mm_raw(dh_refs[0][...], b_ref[:, starts[0]:starts[1]], False, True)
            for k in range(1, len(a_parts)):
                dh_tile = dh_tile + _mm_raw(dh_refs[k][...], b_ref[:, starts[k]:starts[k + 1]], False, True)
        else:
            dh_tile = dh_refs[0][...]
        _, vjp = jax.vjp(_rms, x_ref[...], g_ref[...])
        dx, dg = vjp(dh_tile)
        if want_dx:
            dx = dres_ref[...] + dx
            dx_ref[...] = dx
            dxb_ref[...] = dx.astype(BF16)

        @pl.when(pl.program_id(0) == 0)
        def _():
            dg_ref[...] = jnp.zeros_like(dg_ref)

        dg_ref[...] += dg

    row = pl.BlockSpec((tm, d), lambda i: (i, 0))
    vec = pl.BlockSpec((1, d), lambda i: (0, 0))
    if product:
        dh_specs = [pl.BlockSpec((tm, a.shape[1]), lambda i: (i, 0)) for a in a_parts]
        dh_specs.append(pl.BlockSpec((d, starts[-1]), lambda i: (0, 0)))
        dh_args = a_parts + [dh[1]]
    else:
        dh_specs, dh_args = [row], [dh]
    if want_dx:
        return pl.pallas_call(
            body, name=name, grid=(s // tm,), in_specs=[row, vec] + dh_specs + [row], out_specs=[row, row, vec],
            out_shape=[SDS((s, d), F32), SDS((s, d), BF16), SDS((1, d), F32)], compiler_params=_params("arbitrary"),
        )(x, g, *dh_args, dres)
    return None, None, pl.pallas_call(
        body, name=name, grid=(s // tm,), in_specs=[row, vec] + dh_specs, out_specs=vec,
        out_shape=SDS((1, d), F32), compiler_params=_params("arbitrary"),
    )(x, g, *dh_args)


FFN_TN = 256
FFN_TN_FWD = 512
FFN1_FIRST = 192


def _ffn_fwd(x, gain, w3, tag, *, tm=1024, next_gain=None, target=None, start=None, partial=False):
    s, d = x.shape
    f = w3.shape[1]
    tn = FFN_TN_FWD if f % FFN_TN_FWD == 0 else FFN_TN
    nj = f // tn
    tm = min(tm, s)
    n_extra = (next_gain is not None) + (target is not None) + 2 * (start is not None)

    def body(*refs):
        x_ref, gain_ref, wg_ref, wu_ref, wd_ref = refs[:5]
        extra, outs = refs[5:5 + n_extra], refs[5 + n_extra:-1]
        acc_s = refs[-1]
        g_ref, u_ref = outs[-2:]
        h_ref = extra[-2] if start is not None else outs[-3]
        i, j = pl.program_id(0), pl.program_id(1)

        @pl.when(j == 0)
        def _():
            if start is None:
                h_ref[...] = _rms(x_ref[...], gain_ref[...]).astype(BF16)
                acc_s[...] = jnp.zeros_like(acc_s)
            else:
                acc_s[...] = extra[-1][...]

        hv = h_ref[...]
        g = _mm_raw(hv, wg_ref[...], False, True)
        u = _mm_raw(hv, wu_ref[...], False, True)
        g_ref[...] = g.astype(BF16)
        u_ref[...] = u.astype(BF16)
        acc_s[...] += _mm_raw(_silu_mul(g, u), wd_ref[...])

        @pl.when(j == nj - 1)
        def _():
            y = acc_s[...] if partial else x_ref[...] + 0.5 * acc_s[...]
            if target is None:
                outs[0][...] = y
                if next_gain is not None:
                    outs[1][...] = _rms(y, extra[0][...]).astype(BF16)
            else:
                dy_ref, dyb_ref, loss_ref = outs[:3]
                diff = y - extra[0][...]
                dy_ref[...] = diff * (1.0 / d)
                dyb_ref[...] = (diff * (1.0 / d)).astype(BF16)
                part = 0.5 * jnp.sum(jnp.mean(diff * diff, axis=-1, keepdims=True), axis=0, keepdims=True)

                @pl.when(i == 0)
                def _():
                    loss_ref[...] = part

                @pl.when(i > 0)
                def _():
                    loss_ref[...] += part

    row = pl.BlockSpec((tm, d), lambda i, j: (i, 0))
    vec = pl.BlockSpec((1, d), lambda i, j: (0, 0))
    tile = pl.BlockSpec((tm, tn), lambda i, j: (i, j))
    in_specs = [row, vec] + [pl.BlockSpec((None, tn, d), functools.partial(lambda i, j, k: (k, j, 0), k=k)) for k in range(3)]
    args = [x, gain, w3, w3, w3]
    if target is None:
        out_specs, out_shape = [row], [SDS((s, d), F32)]
        if next_gain is not None:
            in_specs.append(vec)
            args.append(next_gain)
            out_specs.append(row)
            out_shape.append(SDS((s, d), BF16))
    else:
        in_specs.append(row)
        args.append(target)
        out_specs = [row, row, pl.BlockSpec((1, 1), lambda i, j: (0, 0))]
        out_shape = [SDS((s, d), F32), SDS((s, d), BF16), SDS((1, 1), F32)]
    if start is None:
        out_specs.append(row)
        out_shape.append(SDS((s, d), BF16))
    else:
        in_specs += [row, row]
        args += list(start)
    *head, g, u = pl.pallas_call(
        body, name=f"{tag}_fwd", grid=(s // tm, nj), in_specs=in_specs,
        out_specs=out_specs + [tile, tile],
        out_shape=out_shape + [SDS((s, f), BF16), SDS((s, f), BF16)],
        scratch_shapes=[pltpu.VMEM((tm, d), F32)],
        compiler_params=_params("arbitrary", "arbitrary"),
    )(*args)
    if start is None:
        *head, h = head
    else:
        h = start[0]
    return head, (h, g, u)


def _ffn_bwd_part(dyb, w3, saved, first, count, dh_init, *, name):
    h, g, u = saved
    s, d = h.shape
    tn = FFN_TN

    def body(*refs):
        if dh_init is None:
            dy_ref, h_ref, wd_ref, wg_ref, wu_ref, g_ref, u_ref, dh_ref, dw3_ref, dg_s, du_s, a_s = refs
        else:
            dy_ref, h_ref, wd_ref, wg_ref, wu_ref, g_ref, u_ref, dh0_ref, dh_ref, dw3_ref, dg_s, du_s, a_s = refs
        j = pl.program_id(0)

        @pl.when(j == 0)
        def _():
            dh_ref[...] = jnp.zeros_like(dh_ref) if dh_init is None else dh0_ref[...]
            for ref in (dg_s, du_s, a_s):
                ref[...] = jnp.zeros_like(ref)

        now, before = j % 2, 1 - j % 2
        dyv = dy_ref[...]
        hv = h_ref[...]
        dg, du, a = dg_s[before], du_s[before], a_s[before]
        dh_ref[...] += _mm_raw(dg, wg_ref[...]) + _mm_raw(du, wu_ref[...])
        dw3_ref[0] = _mm_raw(dg, hv, True, False).astype(BF16)
        dw3_ref[1] = _mm_raw(du, hv, True, False).astype(BF16)
        dw3_ref[2] = (_mm_raw(a, dyv, True, False) * 0.5).astype(BF16)

        da = _mm_raw(dyv, wd_ref[...], False, True) * 0.5
        a, vjp = jax.vjp(_silu_mul, g_ref[...].astype(F32), u_ref[...].astype(F32))
        dg, du = vjp(da)
        dg_s[now] = dg.astype(BF16)
        du_s[now] = du.astype(BF16)
        a_s[now] = a.astype(BF16)

    this = lambda j: first + jnp.minimum(j, count - 1)
    last = lambda j: first + jnp.maximum(j - 1, 0)
    full = pl.BlockSpec((s, d), lambda j: (0, 0))
    once = pl.BlockSpec((s, d), lambda j: (0, 0), pipeline_mode=pl.Buffered(1))
    tile = pl.BlockSpec((s, tn), lambda j: (0, this(j)))
    in_specs = [once, once, pl.BlockSpec((None, tn, d), lambda j: (2, this(j), 0)),
                pl.BlockSpec((None, tn, d), lambda j: (0, last(j), 0)), pl.BlockSpec((None, tn, d), lambda j: (1, last(j), 0)),
                tile, tile]
    args = [dyb, h, w3, w3, w3, g, u]
    if dh_init is not None:
        in_specs.append(once)
        args.append(dh_init)
    return pl.pallas_call(
        body, name=name, grid=(count + 1,), in_specs=in_specs,
        out_specs=[full, pl.BlockSpec((3, tn, d), lambda j: (0, jnp.maximum(j - 1, 0), 0))],
        out_shape=[SDS((s, d), F32), SDS((3, count * tn, d), BF16)],
        scratch_shapes=[pltpu.VMEM((2, s, tn), BF16)] * 3,
        compiler_params=_params("arbitrary"),
    )(*args)


def _bucket_table():
    qi = np.arange(BLOCK)[:, None]
    kj = np.arange(2 * BLOCK)[None, :]
    dist = np.maximum(qi + BLOCK - kj, 0)
    max_exact = N_BUCKETS // 2
    d = np.maximum(dist, 1).astype(np.float32)
    large = max_exact + (np.log(d / np.float32(max_exact)) / np.float32(math.log(MAX_DISTANCE / max_exact))
                         * np.float32(N_BUCKETS - max_exact)).astype(np.int32)
    large = np.minimum(large, N_BUCKETS - 1)
    band = np.where(dist < max_exact, dist, large).astype(np.int32)
    return np.where(np.tril(np.ones((BLOCK, BLOCK), bool)), band[:, BLOCK:], band[:, :BLOCK])


SWA_STACK = SWA_GROUP * BLOCK


def _swa_masks(n):
    qi = lax.broadcasted_iota(jnp.int32, (SWA_STACK, BLOCK), 0) % BLOCK
    kj = lax.broadcasted_iota(jnp.int32, (SWA_STACK, BLOCK), 1)
    own = kj <= qi
    return own, own | (n > 0)


def _swa_group(q, kp, kc, vp, vc, qg, kg, sink, bias, own, valid):
    qn = _rms(q, qg)
    s = jnp.where(own, _mm(qn, _rms(kc, kg), False, True), _mm(qn, _rms(kp, kg), False, True))
    s = s * (HEAD_DIM ** -0.5) + bias
    s = jnp.where(valid, s, -jnp.inf)
    m = lax.stop_gradient(jnp.maximum(jnp.max(s, axis=-1, keepdims=True), sink))
    p = jnp.exp(s - m)
    p = p / (jnp.sum(p, axis=-1, keepdims=True) + jnp.exp(sink - m))
    return _mm(jnp.where(own, p, 0.0), vc) + _mm(jnp.where(own, 0.0, p), vp)


def _swa_bias_table(rb_ref, bucket, bias_s):
    for h in range(SWA_HEADS):
        acc = jnp.zeros((BLOCK, BLOCK), F32)
        for b in range(N_BUCKETS):
            acc = jnp.where(bucket == b, rb_ref[b, h], acc)
        bias_s[h // SWA_GROUP, (h % SWA_GROUP) * BLOCK:(h % SWA_GROUP + 1) * BLOCK, :] = acc


def _swa_stack(ref, g):
    return jnp.concatenate([ref[:, (g * SWA_GROUP + hh) * HEAD_DIM:(g * SWA_GROUP + hh + 1) * HEAD_DIM]
                            for hh in range(SWA_GROUP)], axis=0)


def _swa_unstack(ref, g, stacked):
    for hh in range(SWA_GROUP):
        h = g * SWA_GROUP + hh
        ref[:, h * HEAD_DIM:(h + 1) * HEAD_DIM] = stacked[hh * BLOCK:(hh + 1) * BLOCK]


def _swa_sink_column(sink_ref, g):
    head = lax.broadcasted_iota(jnp.int32, (SWA_STACK, 1), 0) // BLOCK
    col = jnp.zeros((SWA_STACK, 1), F32)
    for hh in range(SWA_GROUP):
        col = jnp.where(head == hh, sink_ref[g * SWA_GROUP + hh], col)
    return col


def _swa_band(kvp_ref, kvc_ref, g):
    k = slice(g * HEAD_DIM, (g + 1) * HEAD_DIM)
    v = slice(SWA_KV_W + g * HEAD_DIM, SWA_KV_W + (g + 1) * HEAD_DIM)
    return kvp_ref[:, k], kvc_ref[:, k], kvp_ref[:, v], kvc_ref[:, v]


def _swa_specs(order):
    kvc = COL_SKV // (2 * SWA_KV_W)
    return [
        pl.BlockSpec((BLOCK, SWA_Q_W), lambda t: (order(t), 0)),
        pl.BlockSpec((BLOCK, 2 * SWA_KV_W), lambda t: (jnp.maximum(order(t) - 1, 0), kvc)),
        pl.BlockSpec((BLOCK, 2 * SWA_KV_W), lambda t: (order(t), kvc)),
        pl.BlockSpec((1, HEAD_DIM), lambda t: (0, 0)),
        pl.BlockSpec((1, HEAD_DIM), lambda t: (0, 0)),
        pl.BlockSpec(memory_space=pltpu.SMEM),
        pl.BlockSpec(memory_space=pltpu.SMEM),
        pl.BlockSpec((BLOCK, BLOCK), lambda t: (0, 0)),
    ]


def _swa_fwd(p, qg, kg, sinks, rel_bias, *, name):
    s = p.shape[0]
    nb = s // BLOCK

    def body(q_ref, kvp_ref, kvc_ref, qg_ref, kg_ref, sink_ref, rb_ref, bucket_ref, y_ref, bias_s):
        n = pl.program_id(0)

        @pl.when(n == 0)
        def _():
            _swa_bias_table(rb_ref, bucket_ref[...], bias_s)

        own, valid = _swa_masks(n)
        for g in range(SWA_KV_HEADS):
            out = _swa_group(_swa_stack(q_ref, g), *_swa_band(kvp_ref, kvc_ref, g), qg_ref[...], kg_ref[...],
                             _swa_sink_column(sink_ref, g), bias_s[g], own, valid)
            _swa_unstack(y_ref, g, out)

    return pl.pallas_call(
        body, name=name, grid=(nb,), in_specs=_swa_specs(lambda t: t),
        out_specs=pl.BlockSpec((BLOCK, SWA_Q_W), lambda t: (t, 0)), out_shape=SDS((s, SWA_Q_W), F32),
        scratch_shapes=[pltpu.VMEM((SWA_KV_HEADS, SWA_STACK, BLOCK), F32)],
        compiler_params=_params("arbitrary"),
    )(p, p, p, qg, kg, sinks, rel_bias, jnp.asarray(_bucket_table()))


def _swa_bwd(p, qg, kg, sinks, rel_bias, dy_all, *, name):
    s = p.shape[0]
    nb = s // BLOCK

    def body(q_ref, kvp_ref, kvc_ref, qg_ref, kg_ref, sink_ref, rb_ref, bucket_ref, dy_ref,
             dq_ref, dkv_ref, dqg_ref, dkg_ref, dsink_ref, drb_ref, bias_s, dbias_s, carry_s):
        t = pl.program_id(0)
        n = nb - 1 - t

        @pl.when(t == 0)
        def _():
            _swa_bias_table(rb_ref, bucket_ref[...], bias_s)
            dbias_s[...] = jnp.zeros_like(dbias_s)
            carry_s[...] = jnp.zeros_like(carry_s)
            dqg_ref[...] = jnp.zeros_like(dqg_ref)
            dkg_ref[...] = jnp.zeros_like(dkg_ref)
            dsink_ref[...] = jnp.zeros_like(dsink_ref)
            drb_ref[...] = jnp.zeros_like(drb_ref)

        own, valid = _swa_masks(n)
        lane = lax.broadcasted_iota(jnp.int32, (1, BLOCK), 1)
        dqg = jnp.zeros((1, HEAD_DIM), F32)
        dkg = jnp.zeros((1, HEAD_DIM), F32)
        dsink_vec = jnp.zeros((1, BLOCK), F32)
        for g in range(SWA_KV_HEADS):
            _, vjp = jax.vjp(functools.partial(_swa_group, own=own, valid=valid), _swa_stack(q_ref, g),
                             *_swa_band(kvp_ref, kvc_ref, g), qg_ref[...], kg_ref[...], _swa_sink_column(sink_ref, g),
                             bias_s[g])
            dq, dkp, dkc, dvp, dvc, dqg_g, dkg_g, dsink_col, dbias = vjp(_swa_stack(dy_ref, g))
            _swa_unstack(dq_ref, g, dq)
            dqg += dqg_g
            dkg += dkg_g
            dbias_s[g] += dbias
            for hh in range(SWA_GROUP):
                dsink_h = jnp.sum(dsink_col[hh * BLOCK:(hh + 1) * BLOCK], axis=0, keepdims=True)
                dsink_vec += jnp.where(lane == g * SWA_GROUP + hh, dsink_h, 0.0)
            lo = g * HEAD_DIM
            dkv_ref[:, lo:lo + HEAD_DIM] = dkc + carry_s[g]
            carry_s[g] = dkp
            lo += SWA_KV_W
            dkv_ref[:, lo:lo + HEAD_DIM] = dvc + carry_s[SWA_KV_HEADS + g]
            carry_s[SWA_KV_HEADS + g] = dvp
        dqg_ref[...] += dqg
        dkg_ref[...] += dkg
        dsink_ref[...] += dsink_vec

        @pl.when(t == nb - 1)
        def _():
            bucket = bucket_ref[...]
            row = lax.broadcasted_iota(jnp.int32, (N_BUCKETS, BLOCK), 0)
            col = lax.broadcasted_iota(jnp.int32, (N_BUCKETS, BLOCK), 1)
            acc = jnp.zeros((N_BUCKETS, BLOCK), F32)
            for h in range(SWA_HEADS):
                dbias = dbias_s[h // SWA_GROUP, (h % SWA_GROUP) * BLOCK:(h % SWA_GROUP + 1) * BLOCK, :]
                for b in range(N_BUCKETS):
                    part = jnp.sum(jnp.where(bucket == b, dbias, 0.0), axis=1, keepdims=True)
                    val = jnp.sum(part, axis=0, keepdims=True)
                    acc = acc + jnp.where((row == b) & (col == h), val, 0.0)
            drb_ref[...] = acc

    order = lambda t: nb - 1 - t
    vec = pl.BlockSpec((1, HEAD_DIM), lambda t: (0, 0))
    return pl.pallas_call(
        body, name=name, grid=(nb,),
        in_specs=_swa_specs(order) + [pl.BlockSpec((BLOCK, SWA_Q_W), lambda t: (order(t), 0))],
        out_specs=[pl.BlockSpec((BLOCK, SWA_Q_W), lambda t: (order(t), 0)),
                   pl.BlockSpec((BLOCK, 2 * SWA_KV_W), lambda t: (order(t), 0)),
                   vec, vec, pl.BlockSpec((1, BLOCK), lambda t: (0, 0)),
                   pl.BlockSpec((N_BUCKETS, BLOCK), lambda t: (0, 0))],
        out_shape=[SDS((s, SWA_Q_W), F32), SDS((s, 2 * SWA_KV_W), F32), SDS((1, HEAD_DIM), F32),
                   SDS((1, HEAD_DIM), F32), SDS((1, BLOCK), F32), SDS((N_BUCKETS, BLOCK), F32)],
        scratch_shapes=[pltpu.VMEM((SWA_KV_HEADS, SWA_STACK, BLOCK), F32),
                        pltpu.VMEM((SWA_KV_HEADS, SWA_STACK, BLOCK), F32),
                        pltpu.VMEM((2 * SWA_KV_HEADS, BLOCK, HEAD_DIM), F32)],
        compiler_params=_params("arbitrary"),
    )(p, p, p, qg, kg, sinks, rel_bias, jnp.asarray(_bucket_table()), dy_all)


def _mem_head(q, k, v, qg, kg):
    qn = _rms(q, qg)
    kn = _rms(k, kg)
    s = _mm(qn, kn, False, True) * (HEAD_DIM ** -0.5)
    m = lax.stop_gradient(jnp.max(s, axis=-1, keepdims=True))
    e = jnp.exp(s - m)
    return _mm(e / jnp.sum(e, axis=-1, keepdims=True), v)


def _mem_fwd(p, kv, qg, kg, *, tq, name):
    s = p.shape[0]
    m = kv.shape[0]

    def body(q_ref, kv_ref, qg_ref, kg_ref, y_ref):
        for h in range(MEM_HEADS):
            cols = slice(h * HEAD_DIM, (h + 1) * HEAD_DIM)
            vcols = slice(MEM_Q_W + h * HEAD_DIM, MEM_Q_W + (h + 1) * HEAD_DIM)
            y_ref[:, cols] = _mem_head(q_ref[:, cols], kv_ref[:, cols], kv_ref[:, vcols], qg_ref[...], kg_ref[...])

    vec = pl.BlockSpec((1, HEAD_DIM), lambda t: (0, 0))
    return pl.pallas_call(
        body, name=name, grid=(s // tq,),
        in_specs=[pl.BlockSpec((tq, MEM_Q_W), lambda t: (t, COL_MQ // MEM_Q_W)),
                  pl.BlockSpec((m, 2 * MEM_Q_W), lambda t: (0, 0)), vec, vec],
        out_specs=pl.BlockSpec((tq, MEM_Q_W), lambda t: (t, 0)), out_shape=SDS((s, MEM_Q_W), F32),
        compiler_params=_params("parallel"),
    )(p, kv, qg, kg)


def _mem_bwd(p, kv, qg, kg, dy_all, *, tq, name):
    s = p.shape[0]
    m = kv.shape[0]

    def body(q_ref, kv_ref, qg_ref, kg_ref, dy_ref, dq_ref, dkv_ref, dqg_ref, dkg_ref):
        @pl.when(pl.program_id(0) == 0)
        def _():
            dkv_ref[...] = jnp.zeros_like(dkv_ref)
            dqg_ref[...] = jnp.zeros_like(dqg_ref)
            dkg_ref[...] = jnp.zeros_like(dkg_ref)

        dqg = jnp.zeros((1, HEAD_DIM), F32)
        dkg = jnp.zeros((1, HEAD_DIM), F32)
        for h in range(MEM_HEADS):
            cols = slice(h * HEAD_DIM, (h + 1) * HEAD_DIM)
            vcols = slice(MEM_Q_W + h * HEAD_DIM, MEM_Q_W + (h + 1) * HEAD_DIM)
            _, vjp = jax.vjp(_mem_head, q_ref[:, cols], kv_ref[:, cols], kv_ref[:, vcols], qg_ref[...], kg_ref[...])
            dq, dk, dv, dqg_h, dkg_h = vjp(dy_ref[:, cols])
            dq_ref[:, cols] = dq
            dkv_ref[:, cols] += dk
            dkv_ref[:, vcols] += dv
            dqg += dqg_h
            dkg += dkg_h
        dqg_ref[...] += dqg
        dkg_ref[...] += dkg

    vec = pl.BlockSpec((1, HEAD_DIM), lambda t: (0, 0))
    full = pl.BlockSpec((m, 2 * MEM_Q_W), lambda t: (0, 0))
    dy_col = (SWA_Q_W + GLA_V_W) // MEM_Q_W
    return pl.pallas_call(
        body, name=name, grid=(s // tq,),
        in_specs=[pl.BlockSpec((tq, MEM_Q_W), lambda t: (t, COL_MQ // MEM_Q_W)), full, vec, vec,
                  pl.BlockSpec((tq, MEM_Q_W), lambda t: (t, dy_col))],
        out_specs=[pl.BlockSpec((tq, MEM_Q_W), lambda t: (t, 0)), full, vec, vec],
        out_shape=[SDS((s, MEM_Q_W), F32), SDS((m, 2 * MEM_Q_W), F32), SDS((1, HEAD_DIM), F32), SDS((1, HEAD_DIM), F32)],
        compiler_params=_params("arbitrary"),
    )(p, kv, qg, kg, dy_all)


GLA_ROWS = 256


GLA_GROUP = 4


def _gla_consts():
    c, h, r = GLA_CHUNK, GLA_HEADS, GLA_GROUP * GLA_CHUNK
    i2 = lax.broadcasted_iota(jnp.int32, (c, c), 0)
    j2 = lax.broadcasted_iota(jnp.int32, (c, c), 1)
    slab_q = lax.broadcasted_iota(jnp.int32, (h, r, GLA_QK_W), 0)
    lane_q = lax.broadcasted_iota(jnp.int32, (h, r, GLA_QK_W), 2)
    row_a = lax.broadcasted_iota(jnp.int32, (h * r, r), 0) % r
    col_a = lax.broadcasted_iota(jnp.int32, (h * r, r), 1)
    slab_o = lax.broadcasted_iota(jnp.int32, (h, r, GLA_V_W), 0)
    lane_o = lax.broadcasted_iota(jnp.int32, (h, r, GLA_V_W), 2)
    row_s = lax.broadcasted_iota(jnp.int32, (GLA_V_W, GLA_QK_W), 0)
    col_s = lax.broadcasted_iota(jnp.int32, (GLA_V_W, GLA_QK_W), 1)
    return dict(
        ltri=(j2 <= i2).astype(F32),
        m_q=(slab_q == lane_q // GLA_DK).astype(F32),
        causal=(col_a <= row_a) & (col_a // c == row_a // c),
        m_o=(slab_o == lane_o // GLA_DV).astype(F32),
        m_s=(row_s // GLA_DV == col_s // GLA_DK).astype(F32),
    )


def _gla_step(q, k, v, z, bg, st, c):
    h = GLA_HEADS
    kt, ka, qt, qe, decay = [], [], [], [], []
    for qc, kc, zc in zip(q, k, z):
        la = _log_sigmoid(zc + bg) * (1.0 / GLA_TAU)
        b = _mmf(c["ltri"], la)
        bl = jnp.sum(la, axis=0, keepdims=True)
        qs = qc * (GLA_DK ** -0.5)
        kt.append(kc * jnp.exp(bl - b))
        ka.append(kc * jnp.exp(0.5 * bl - b))
        qt.append(qs * jnp.exp(b - 0.5 * bl))
        qe.append(qs * jnp.exp(b))
        decay.append(jnp.exp(bl))
    o_intra = []
    rows = GLA_GROUP * GLA_CHUNK
    for lo in range(0, len(q), GLA_GROUP):
        qt_all, kt_all, v_all = (jnp.concatenate(parts[lo:lo + GLA_GROUP], axis=0) for parts in (qt, ka, v))
        q_stack = (jnp.broadcast_to(qt_all[None], (h, rows, GLA_QK_W)) * c["m_q"]).reshape(h * rows, GLA_QK_W)
        a = jnp.where(c["causal"], _mm3(q_stack, kt_all, False, True), 0.0)
        o_stack = _mm(a, v_all)
        o_intra.append(jnp.sum(o_stack.reshape(h, rows, GLA_V_W) * c["m_o"], axis=0))
    o_intra = jnp.concatenate(o_intra, axis=0)
    o_inter = []
    for qec, ktc, vc, dc in zip(qe, kt, v, decay):
        o_inter.append(_mm(qec, st, False, True))
        st = st * dc + _mm(vc, ktc, True, False) * c["m_s"]
    return o_intra + jnp.concatenate(o_inter, axis=0), st


def _gla_post(o, gg, gain, g64):
    ms = _mmf(o * o, g64) * (1.0 / GLA_DV)
    return o * lax.rsqrt(ms + EPS) * gain * jax.nn.silu(gg)


def _gla_g64():
    r = lax.broadcasted_iota(jnp.int32, (GLA_V_W, GLA_V_W), 0)
    c = lax.broadcasted_iota(jnp.int32, (GLA_V_W, GLA_V_W), 1)
    return (r // GLA_DV == c // GLA_DV).astype(F32)


def _gla_in_specs(order):
    r = GLA_ROWS
    return [
        pl.BlockSpec((r, GLA_QK_W), lambda t: (order(t), COL_GQ // GLA_QK_W)),
        pl.BlockSpec((r, GLA_QK_W), lambda t: (order(t), COL_GK // GLA_QK_W)),
        pl.BlockSpec((r, GLA_V_W), lambda t: (order(t), COL_GV // GLA_V_W)),
        pl.BlockSpec((r, GLA_V_W), lambda t: (order(t), COL_GG // GLA_V_W)),
        pl.BlockSpec((r, GLA_QK_W), lambda t: (order(t), 0)),
        pl.BlockSpec((1, GLA_QK_W), lambda t: (0, 0)),
        pl.BlockSpec((1, GLA_V_W), lambda t: (0, 0)),
    ]


def _gla_pieces(q_ref, k_ref, v_ref, z_ref, cps):
    chunk = lambda ref: [ref[ci * GLA_CHUNK:(ci + 1) * GLA_CHUNK, :] for ci in range(cps)]
    return chunk(q_ref), chunk(k_ref), chunk(v_ref), chunk(z_ref)


def _gla_fwd(p, z, bg, gain, *, name):
    s = p.shape[0]
    r = GLA_ROWS
    cps = r // GLA_CHUNK

    def body(q_ref, k_ref, v_ref, gg_ref, z_ref, bg_ref, gain_ref, y_ref, oraw_ref, stsave_ref, st_s):
        @pl.when(pl.program_id(0) == 0)
        def _():
            st_s[...] = jnp.zeros_like(st_s)

        st = st_s[...]
        stsave_ref[0] = st
        o, st = _gla_step(*_gla_pieces(q_ref, k_ref, v_ref, z_ref, cps), bg_ref[...], st, _gla_consts())
        oraw_ref[...] = o
        st_s[...] = st
        y_ref[...] = _gla_post(o, gg_ref[...], gain_ref[...], _gla_g64())

    rowv = pl.BlockSpec((r, GLA_V_W), lambda t: (t, 0))
    return pl.pallas_call(
        body, name=name, grid=(s // r,), in_specs=_gla_in_specs(lambda t: t),
        out_specs=[rowv, rowv, pl.BlockSpec((1, GLA_V_W, GLA_QK_W), lambda t: (t, 0, 0))],
        out_shape=[SDS((s, GLA_V_W), F32), SDS((s, GLA_V_W), F32), SDS((s // r, GLA_V_W, GLA_QK_W), F32)],
        scratch_shapes=[pltpu.VMEM((GLA_V_W, GLA_QK_W), F32)],
        compiler_params=_params("arbitrary"),
    )(p, p, p, p, z, bg, gain)


def _gla_bwd(p, z, bg, gain, oraw, stsave, dy_all, *, name):
    s = p.shape[0]
    r = GLA_ROWS
    cps = r // GLA_CHUNK
    nsteps = s // r
    w_qkvg = 2 * GLA_QK_W + 2 * GLA_V_W

    def body(q_ref, k_ref, v_ref, gg_ref, z_ref, bg_ref, gain_ref, oraw_ref, stsave_ref, dy_ref,
             dqkvg_ref, dz_ref, dbg_ref, dgain_ref, dst_s):
        @pl.when(pl.program_id(0) == 0)
        def _():
            dst_s[...] = jnp.zeros_like(dst_s)
            dbg_ref[...] = jnp.zeros_like(dbg_ref)
            dgain_ref[...] = jnp.zeros_like(dgain_ref)

        _, vjp = jax.vjp(functools.partial(_gla_post, g64=_gla_g64()), oraw_ref[...], gg_ref[...], gain_ref[...])
        do, dgg, dgain = vjp(dy_ref[...])
        dqkvg_ref[:, 2 * GLA_QK_W + GLA_V_W:] = dgg
        dgain_ref[...] += dgain
        _, vjp = jax.vjp(functools.partial(_gla_step, c=_gla_consts()), *_gla_pieces(q_ref, k_ref, v_ref, z_ref, cps),
                         bg_ref[...], stsave_ref[0])
        dq, dk, dv, dz, dbg, dst = vjp((do, dst_s[...]))
        for ci in range(cps):
            rows = slice(ci * GLA_CHUNK, (ci + 1) * GLA_CHUNK)
            dqkvg_ref[rows, 0:GLA_QK_W] = dq[ci]
            dqkvg_ref[rows, GLA_QK_W:2 * GLA_QK_W] = dk[ci]
            dqkvg_ref[rows, 2 * GLA_QK_W:2 * GLA_QK_W + GLA_V_W] = dv[ci]
            dz_ref[rows, :] = dz[ci]
        dst_s[...] = dst
        dbg_ref[...] += dbg

    order = lambda t: nsteps - 1 - t
    rowv = pl.BlockSpec((r, GLA_V_W), lambda t: (order(t), 0))
    return pl.pallas_call(
        body, name=name, grid=(nsteps,),
        in_specs=_gla_in_specs(order) + [
            rowv, pl.BlockSpec((1, GLA_V_W, GLA_QK_W), lambda t: (order(t), 0, 0)),
            pl.BlockSpec((r, GLA_V_W), lambda t: (order(t), SWA_Q_W // GLA_V_W))],
        out_specs=[pl.BlockSpec((r, w_qkvg), lambda t: (order(t), 0)), pl.BlockSpec((r, GLA_QK_W), lambda t: (order(t), 0)),
                   pl.BlockSpec((1, GLA_QK_W), lambda t: (0, 0)), pl.BlockSpec((1, GLA_V_W), lambda t: (0, 0))],
        out_shape=[SDS((s, w_qkvg), F32), SDS((s, GLA_QK_W), F32), SDS((1, GLA_QK_W), F32), SDS((1, GLA_V_W), F32)],
        scratch_shapes=[pltpu.VMEM((GLA_V_W, GLA_QK_W), F32)],
        compiler_params=_params("arbitrary"),
    )(p, p, p, p, z, bg, gain, oraw, stsave, dy_all)


def _local_step(x, mem, target, small, big, on_grads):
    g1, gmix, gmem, g2, sqg, skg, sinks, rel_bias, wgu, bg, gla_gain, mqg, mkg = small
    (w3_1a, w3_1b), gather_mix, gather_ffn2 = big
    wgu_pad = jnp.zeros((GLA_QK_W, GLA_QK_W), BF16).at[:GLA_RANK].set(wgu.astype(BF16))
    gain256 = jnp.tile(gla_gain, (1, GLA_HEADS))

    (part,), saved1a = _ffn_fwd(x, g1, w3_1a, "ffn1a", partial=True)
    win_p, wkv, wout = gather_mix(part)
    (x1, h), saved1b = _ffn_fwd(x, g1, w3_1b, "ffn1b", next_gain=gmix, start=(saved1a[0], part))
    w3_2 = gather_ffn2((wout, x1))
    p = _matmul(h, win_p, tm=512, tn=IN_W_PAD, name="mix_in")
    hm = _rms_fwd(mem, gmem, tm=256, name="mem_rms")
    kv = _matmul(hm, wkv, tm=256, tn=512, name="mem_kv")
    p_glr = p[:, COL_GLR:]
    z = _matmul(p_glr, wgu_pad, tm=1024, tn=GLA_QK_W, name="gla_gate")
    y_swa = _swa_fwd(p, sqg, skg, sinks, rel_bias, name="swa_fwd")
    y_gla, oraw, stsave = _gla_fwd(p, z, bg, gain256, name="gla_fwd")
    y_mem = _mem_fwd(p, kv, mqg, mkg, tq=512, name="mem_fwd")
    x2 = _matmul([y_swa, y_gla, y_mem], wout, b_blocks=[0, 2, 3], tm=512, tn=1024, res=x1, name="mix_out")
    (dy, dyb, loss), saved2 = _ffn_fwd(x2, g2, w3_2, "ffn2", target=target)

    dh2, dw3_2 = _ffn_bwd_part(dyb, w3_2, saved2, 0, w3_2.shape[1] // FFN_TN, None, name="ffn2_bwd")
    dx2, dx2b, dg2 = _rms_bwd(x2, g2, dh2, dy, tm=512, name="ffn2_drms")
    dx2b = on_grads("ffn2", [dw3_2], dx2b)
    dy_all = _matmul(dx2b, wout, tb=True, tm=512, tn=1024, name="mix_dy")
    dwout = _matmul(jnp.concatenate([y_swa, y_gla, y_mem], axis=1), dx2b, ta=True, tm=512, tn=1024, out_dtype=BF16,
                    name="mix_dw_out")
    dq_swa, dkv_swa, dsqg, dskg, dsink, drb = _swa_bwd(p, sqg, skg, sinks, rel_bias, dy_all, name="swa_bwd")
    dqkvg, dz, dbg, dgain256 = _gla_bwd(p, z, bg, gain256, oraw, stsave, dy_all, name="gla_bwd")
    dmq, dkv_mem, dmqg, dmkg = _mem_bwd(p, kv, mqg, mkg, dy_all, tq=512, name="mem_bwd")
    dglr = _matmul(dz, wgu_pad, tb=True, tm=1024, tn=GLA_QK_W, name="gla_gate_dx")
    dwgu_pad = _matmul(p_glr, dz, ta=True, tm=GLA_QK_W, tn=GLA_QK_W, name="gla_gate_dw")
    dp = [dq_swa, dkv_swa, dqkvg, dmq, dglr]
    dwin_p = _dw_parts(h, dp, tk=512, name="mix_dw_in")
    dx1, dx1b, dgmix = _rms_bwd(x1, gmix, (dp, win_p), dx2, tm=512, name="mix_dh_drms")
    dwkv = _matmul(hm, dkv_mem, ta=True, tm=512, tn=512, out_dtype=BF16, name="mem_dw_kv")
    dx1b = on_grads("mix", (dwin_p, dwkv, dwout), dx1b)
    _, _, dgmem = _rms_bwd(mem, gmem, (dkv_mem, wkv), None, tm=256, name="mem_dh_drms")
    dh1, dw3_1a = _ffn_bwd_part(dx1b, w3_1a, saved1a, 0, w3_1a.shape[1] // FFN_TN, None, name="ffn1_bwd_a")
    dgla_gain = dgain256.reshape(GLA_HEADS, GLA_DV).sum(axis=0, keepdims=True)
    dsmall = [dgmix, dgmem, dg2, dsqg, dskg, dsink[:, :SWA_HEADS], drb[:, :SWA_HEADS].T, dwgu_pad[:GLA_RANK], dbg,
              dgla_gain, dmqg, dmkg, loss]
    dh1, dgmem, dwgu_pad = on_grads("ffn1a", [dw3_1a], (dh1, dgmem, dwgu_pad), small=dsmall)
    dh1, dw3_1b = _ffn_bwd_part(dx1b, w3_1b, saved1b, 0, w3_1b.shape[1] // FFN_TN, dh1, name="ffn1_bwd_b")
    dx, _, dg1 = _rms_bwd(x, g1, dh1, dx1, tm=512, name="ffn1_drms")
    on_grads("ffn1b", [dw3_1b], None, small=[dg1])
    return dx


def _mesh_place():
    x, y, c = lax.axis_index("x"), lax.axis_index("y"), lax.axis_index("c")
    other_chips = [(1 - x, y), (x, 1 - y), (1 - x, 1 - y)]
    return x, y, c, other_chips


def _handshake(peers):
    barrier = pltpu.get_barrier_semaphore()
    for peer in peers:
        pl.semaphore_signal(barrier, inc=1, device_id=peer, device_id_type=MESH)
    pl.semaphore_wait(barrier, len(peers))


def _sequencer_call(body, operands, out_shapes, sems, *, name, collective_id):
    return pl.kernel(
        body, name=name, out_type=out_shapes, mesh=plsc.ScalarSubcoreMesh(axis_name="sequencer", num_cores=1),
        scratch_types=sems, compiler_params=pltpu.CompilerParams(collective_id=collective_id),
    )(*operands)


def _window(ref, kind, slot, shape):
    if kind == "row":
        rows = pl.ds(pl.multiple_of(slot * shape[-2], 8), shape[-2])
        return ref.at[(slice(None),) * (len(shape) - 2) + (rows,)]
    return ref.at[slot]


def _gathered(shape, kind):
    if kind == "row":
        return tuple(shape[:-2]) + (N_DEV * shape[-2], shape[-1])
    return (N_DEV,) + tuple(shape)


def _half(view, hf):
    if len(view.shape) == 4:
        return view.at[:, hf]
    n = view.shape[-2] // 2
    return view.at[(slice(None),) * (len(view.shape) - 2) + (pl.ds(hf * n, n),)]


def _all_gather(shards, kinds, *, name, collective_id):
    nt = len(shards)

    def body(*refs):
        x_refs, o_refs = refs[:nt], refs[nt:2 * nt]
        send_sems, recv_sems, local_sems = refs[2 * nt:]
        x, y, c, _ = _mesh_place()
        me, sibling, xn, yn, diag = (x, y, c), (x, y, 1 - c), (1 - x, y, c), (x, 1 - y, c), (1 - x, 1 - y, c)
        _handshake([sibling, xn, yn])

        def win(t, block):
            bx, by, bc = block
            return _window(o_refs[t], kinds[t], 4 * bx + 2 * by + bc, shards[t].shape)

        def copy(k, t, src, dst, to):
            return pltpu.make_async_remote_copy(src_ref=src, dst_ref=dst, send_sem=send_sems.at[k, t],
                                                recv_sem=recv_sems.at[k, t], device_id=to, device_id_type=MESH)

        def piece(k, t, block, hf, to, from_shard=False):
            dst = _half(win(t, block), hf)
            return copy(k, t, _half(x_refs[t], hf) if from_shard else dst, dst, to)

        mine = [pltpu.make_async_copy(x_refs[t], win(t, me), local_sems.at[t]) for t in range(nt)]
        sent = []

        def start(cp):
            cp.start()
            sent.append(cp)

        for cp in mine:
            cp.start()
        for t in range(nt):
            start(copy(0, t, x_refs[t], win(t, me), sibling))
        for hf_x, hf_y in ((0, 1), (1, 0)):
            for t in range(nt):
                start(piece(1 + hf_x, t, me, hf_x, xn, True))
                start(piece(3 + hf_y, t, me, hf_y, yn, True))
        for k, block, hf, onward, k_sib in ((1, xn, 0, (5, yn), 7), (4, yn, 1, (6, xn), 10), (2, xn, 1, None, 8),
                                           (3, yn, 0, None, 9), (5, diag, 0, None, 11), (6, diag, 1, None, 12)):
            for t in range(nt):
                piece(k, t, block, hf, me).wait_recv()
                if onward is not None:
                    start(piece(onward[0], t, block, hf, onward[1]))
                start(piece(k_sib, t, block, hf, sibling))
        for t in range(nt):
            copy(0, t, x_refs[t], win(t, sibling), me).wait_recv()
        for k_sib, block, hf in ((7, xn, 0), (10, yn, 1), (8, xn, 1), (9, yn, 0), (11, diag, 0), (12, diag, 1)):
            for t in range(nt):
                bx, by, _ = block
                piece(k_sib, t, (bx, by, 1 - c), hf, me).wait_recv()
        for cp in sent:
            cp.wait_send()
        for cp in mine:
            cp.wait()

    return _sequencer_call(
        body, shards, [SDS(_gathered(s.shape, k), s.dtype) for s, k in zip(shards, kinds)],
        [pltpu.SemaphoreType.DMA((13, nt)), pltpu.SemaphoreType.DMA((13, nt)), pltpu.SemaphoreType.DMA((nt,))],
        name=name, collective_id=collective_id)


def _part_shape(shape, kind):
    if kind == "row":
        return tuple(shape[:-2]) + (shape[-2] // N_DEV, shape[-1])
    return tuple(shape[2:])


def _pair_exchange(grads, kinds, *, name, collective_id):
    nt = len(grads)
    part = [_part_shape(g.shape, k) for g, k in zip(grads, kinds)]

    def body(*refs):
        g_refs, o_refs = refs[:nt], refs[nt:2 * nt]
        send_sems, recv_sems = refs[2 * nt:]
        x, y, c, _ = _mesh_place()
        _handshake([(x, y, 1 - c)])
        copies = []
        for t in range(nt):
            for xy in range(4):
                src = g_refs[t].at[1 - c, xy] if kinds[t] == "stack" else _window(g_refs[t], kinds[t], 2 * xy + 1 - c, part[t])
                copies.append(pltpu.make_async_remote_copy(
                    src_ref=src, dst_ref=o_refs[t].at[xy], send_sem=send_sems.at[xy, t], recv_sem=recv_sems.at[xy, t],
                    device_id=(x, y, 1 - c), device_id_type=MESH))
        for cp in copies:
            cp.start()
        for cp in copies:
            cp.wait()

    return _sequencer_call(
        body, grads, [SDS((4,) + p, g.dtype) for p, g in zip(part, grads)],
        [pltpu.SemaphoreType.DMA((4, nt)), pltpu.SemaphoreType.DMA((4, nt))], name=name, collective_id=collective_id)


def _chip_exchange(parts, small, *, name, collective_id):
    nt = len(parts)
    if small is None:
        def body_plain(*refs):
            s_refs, o_refs = refs[:nt], refs[nt:2 * nt]
            send_sems, recv_sems = refs[2 * nt:]
            x, y, c, chips = _mesh_place()
            _handshake([(*chip, c) for chip in chips])
            copies = [pltpu.make_async_remote_copy(
                src_ref=s_refs[t].at[2 * chip[0] + chip[1]], dst_ref=o_refs[t].at[j],
                send_sem=send_sems.at[j, t], recv_sem=recv_sems.at[j, t], device_id=(*chip, c), device_id_type=MESH)
                for j, chip in enumerate(chips) for t in range(nt)]
            for cp in copies:
                cp.start()
            for cp in copies:
                cp.wait()

        return _sequencer_call(
            body_plain, parts, [SDS((3,) + s.shape[1:], s.dtype) for s in parts],
            [pltpu.SemaphoreType.DMA((3, nt)), pltpu.SemaphoreType.DMA((3, nt))], name=name, collective_id=collective_id)

    def body(*refs):
        s_refs, small_ref = refs[:nt], refs[nt]
        o_refs, small_all = refs[nt + 1:2 * nt + 1], refs[2 * nt + 1]
        send_sems, recv_sems, small_send, small_recv, local_sem = refs[2 * nt + 2:]
        x, y, c, chips = _mesh_place()
        _handshake([(px, py, pc) for px in (x, 1 - x) for py in (y, 1 - y) for pc in (c, 1 - c)][1:])

        def copy(j, t, chip):
            return pltpu.make_async_remote_copy(
                src_ref=s_refs[t].at[2 * chip[0] + chip[1]], dst_ref=o_refs[t].at[j],
                send_sem=send_sems.at[j, t], recv_sem=recv_sems.at[j, t], device_id=(*chip, c), device_id_type=MESH)

        flips = [(fx, fy, fc) for fx in (0, 1) for fy in (0, 1) for fc in (0, 1)][1:]

        def small_copy(k):
            fx, fy, fc = flips[k]
            to = (x ^ fx if fx else x, y ^ fy if fy else y, c ^ fc if fc else c)
            rows = small_all.at[4 * x + 2 * y + c]
            return pltpu.make_async_remote_copy(
                src_ref=small_ref, dst_ref=rows, send_sem=small_send.at[k], recv_sem=small_recv.at[k],
                device_id=to, device_id_type=MESH)

        own = pltpu.make_async_copy(small_ref, small_all.at[4 * x + 2 * y + c], local_sem)
        own.start()
        copies = [copy(j, t, chip) for j, chip in enumerate(chips) for t in range(nt)]
        smalls = [small_copy(k) for k in range(7)]
        for cp in smalls + copies:
            cp.start()
        for cp in smalls + copies:
            cp.wait()
        own.wait()

    return _sequencer_call(
        body, list(parts) + [small],
        [SDS((3,) + s.shape[1:], s.dtype) for s in parts] + [SDS((N_DEV,) + small.shape, small.dtype)],
        [pltpu.SemaphoreType.DMA((3, nt)), pltpu.SemaphoreType.DMA((3, nt)),
         pltpu.SemaphoreType.DMA((7,)), pltpu.SemaphoreType.DMA((7,)), pltpu.SemaphoreType.DMA],
        name=name, collective_id=collective_id)


def _pair_sum(grad, theirs, kind, c, *, name):
    if kind == "row":
        r, l = theirs.shape[-2:]
        n = theirs.size // (4 * r * l)
        grad, theirs = grad.reshape(n, N_DEV * r, l), theirs.reshape(4, n, r, l)
        mine_spec = pl.BlockSpec((n, r, l), lambda xy, c_ref: (0, 2 * xy + c_ref[0], 0))
    else:
        r, l = theirs.shape[-2:]
        n = theirs.size // (4 * r * l)
        theirs = theirs.reshape(4, n, r, l)
        grad = grad.reshape(2, 4, n, r, l)
        mine_spec = pl.BlockSpec((None, None, n, r, l), lambda xy, c_ref: (c_ref[0], xy, 0, 0, 0))

    def body(c_ref, a_ref, b_ref, o_ref):
        o_ref[...] = (a_ref[...].astype(F32) + b_ref[...].astype(F32)).astype(BF16)

    part = pl.BlockSpec((None, n, r, l), lambda xy, c_ref: (xy, 0, 0, 0))
    return pl.pallas_call(
        body, name=name,
        grid_spec=pltpu.PrefetchScalarGridSpec(num_scalar_prefetch=1, grid=(4,), in_specs=[mine_spec, part], out_specs=part),
        out_shape=SDS((4, n, r, l), BF16), compiler_params=_params("parallel"),
    )(c, grad, theirs)


def _adamw(w, g, m, v):
    m = ADAM_B1 * m + (1.0 - ADAM_B1) * g
    v = ADAM_B2 * v + (1.0 - ADAM_B2) * jnp.square(g)
    m_hat = m / (1.0 - ADAM_B1 ** ADAM_STEP)
    v_hat = v / (1.0 - ADAM_B2 ** ADAM_STEP)
    delta = -ADAM_LR * (m_hat / (jnp.sqrt(v_hat) + ADAM_EPS) + ADAM_WD * w)
    return delta, m, v


def _adam_big(owns, others, mat, xy, w, m, v, *, tr, name):
    _, r, l = w.shape
    lp = owns[0].shape[-1]
    nq = len(owns)
    rows = [tr] if nq == 1 else [o.shape[-2] for o in owns]
    assert sum(rows) == tr and r % tr == 0, (r, tr, rows)

    def body(xy_ref, *refs):
        own_refs, oth_refs = refs[:nq], refs[nq:2 * nq]
        w_ref, m_ref, v_ref, g_out, d_out, m_out, v_out = refs[2 * nq:]
        parts = []
        for q in range(nq):
            gq = own_refs[q][0, 0].astype(F32)
            for j in range(3):
                gq = gq + oth_refs[q][j, 0].astype(F32)
            parts.append(gq)
        g = (parts[0] if nq == 1 else jnp.concatenate(parts, axis=0))[:, :l]
        delta, m_new, v_new = _adamw(w_ref[0], g, m_ref[0], v_ref[0])
        g_out[0] = g
        d_out[0] = delta
        m_out[0] = m_new
        v_out[0] = v_new

    blk = pl.BlockSpec((1, tr, l), lambda i, xy_ref: (0, i, 0))
    own_specs = [pl.BlockSpec((1, 1, n, lp), lambda i, xy_ref: (xy_ref[0], mat, i, 0)) for n in rows]
    oth_specs = [pl.BlockSpec((3, 1, n, lp), lambda i, xy_ref: (0, mat, i, 0)) for n in rows]
    return pl.pallas_call(
        body, name=name,
        grid_spec=pltpu.PrefetchScalarGridSpec(
            num_scalar_prefetch=1, grid=(r // tr,), in_specs=own_specs + oth_specs + [blk, blk, blk],
            out_specs=[blk, blk, blk, blk]),
        out_shape=[SDS(w.shape, F32)] * 4, compiler_params=_params("parallel"),
    )(xy, *owns, *others, w, m, v)


def _adam_cols(own, other, xy, w, m, v, *, tc, name):
    _, wc, d = w.shape
    wp = -(-wc // 128) * 128

    def body(xy_ref, own_ref, oth_ref, w_ref, m_ref, v_ref, g_out, d_out, m_out, v_out, buf):
        g = own_ref[0, 0].astype(F32)
        for j in range(3):
            g = g + oth_ref[j, 0].astype(F32)
        buf[:, :wc] = g
        buf[:, wc:] = jnp.zeros((tc, wp - wc), F32)
        g = buf[...].T[:wc]
        delta, m_new, v_new = _adamw(w_ref[0], g, m_ref[0], v_ref[0])
        g_out[0] = g
        d_out[0] = delta
        m_out[0] = m_new
        v_out[0] = v_new

    blk = pl.BlockSpec((1, wc, tc), lambda i, xy_ref: (0, 0, i))
    in_specs = [pl.BlockSpec((1, 1, tc, wc), lambda i, xy_ref: (xy_ref[0], 0, i, 0)),
                pl.BlockSpec((3, 1, tc, wc), lambda i, xy_ref: (0, 0, i, 0)), blk, blk, blk]
    return pl.pallas_call(
        body, name=name,
        grid_spec=pltpu.PrefetchScalarGridSpec(
            num_scalar_prefetch=1, grid=(d // tc,), in_specs=in_specs, out_specs=[blk, blk, blk, blk],
            scratch_shapes=[pltpu.VMEM((tc, wp), F32)]),
        out_shape=[SDS(w.shape, F32)] * 4, compiler_params=_params("parallel"),
    )(xy, own, other, w, m, v)


def _small_layout(shapes):
    out, at = [], 0
    for r, c in shapes:
        rows = c // 128 if (r == 1 and c > 128) else r
        out.append((at, rows))
        at += -(-rows // 8) * 8
    return out, at


def _pack_small(parts, *, name):
    shapes = [a.shape for a in parts]
    layout, total = _small_layout(shapes)

    def body(*refs):
        o_ref = refs[-1]
        o_ref[...] = jnp.zeros_like(o_ref)
        for x_ref, (r, c), (at, rows) in zip(refs, shapes, layout):
            if r == 1 and c > 128:
                for k in range(rows):
                    o_ref[at + k:at + k + 1, :] = x_ref[:, k * 128:(k + 1) * 128]
            else:
                o_ref[at:at + r, 0:c] = x_ref[...]

    return pl.pallas_call(body, name=name, out_shape=SDS((total, 128), F32))(*parts)


def _adam_small(g_all, ws, ms, vs, *, name):
    n = len(ws)
    shapes = [w.shape for w in ws]
    layout, _ = _small_layout(shapes)

    def body(g_ref, *refs):
        w_refs, m_refs, v_refs, outs = refs[:n], refs[n:2 * n], refs[2 * n:3 * n], refs[3 * n:]
        g_sum = g_ref[0]
        for k in range(1, N_DEV):
            g_sum = g_sum + g_ref[k]
        for i, ((r, c), (at, rows)) in enumerate(zip(shapes, layout)):
            if r == 1 and c > 128:
                g = jnp.concatenate([g_sum[at + k:at + k + 1, :] for k in range(rows)], axis=1)
            else:
                g = g_sum[at:at + r, 0:c]
            delta, m_new, v_new = _adamw(w_refs[i][...], g, m_refs[i][...], v_refs[i][...])
            for q, val in enumerate((g, delta, m_new, v_new)):
                outs[4 * i + q][...] = val

    flat = pl.pallas_call(body, name=name, out_shape=[SDS(s, F32) for s in shapes for _ in range(4)])(g_all, *ws, *ms, *vs)
    return [flat[4 * i:4 * i + 4] for i in range(n)]


def kernel(x, mem, ffn1_norm, ffn1_w_gate, ffn1_w_up, ffn1_w_down, mix_norm, mem_norm, w_in, w_mem_kv, swa_q_norm, swa_k_norm, swa_sinks, rel_bias, gla_w_gate_up, gla_b_gate, gla_out_norm, mem_q_norm, mem_k_norm, w_out, ffn2_norm, ffn2_w_gate, ffn2_w_up, ffn2_w_down, loss_target, m_ffn1_norm, m_ffn1_w_gate, m_ffn1_w_up, m_ffn1_w_down, m_mix_norm, m_mem_norm, m_w_in, m_w_mem_kv, m_swa_q_norm, m_swa_k_norm, m_swa_sinks, m_rel_bias, m_gla_w_gate_up, m_gla_b_gate, m_gla_out_norm, m_mem_q_norm, m_mem_k_norm, m_w_out, m_ffn2_norm, m_ffn2_w_gate, m_ffn2_w_up, m_ffn2_w_down, v_ffn1_norm, v_ffn1_w_gate, v_ffn1_w_up, v_ffn1_w_down, v_mix_norm, v_mem_norm, v_w_in, v_w_mem_kv, v_swa_q_norm, v_swa_k_norm, v_swa_sinks, v_rel_bias, v_gla_w_gate_up, v_gla_b_gate, v_gla_out_norm, v_mem_q_norm, v_mem_k_norm, v_w_out, v_ffn2_norm, v_ffn2_w_gate, v_ffn2_w_up, v_ffn2_w_down):
    xi, yi, ci = lax.axis_index("x"), lax.axis_index("y"), lax.axis_index("c")
    c_arr = jnp.reshape(ci, (1,)).astype(jnp.int32)
    xy_arr = jnp.reshape(2 * xi + yi, (1,)).astype(jnp.int32)
    d = x.shape[-1]

    def ffn_shards(wg_s, wu_s, wd_s):
        return jnp.concatenate([wg_s.transpose(0, 2, 1), wu_s.transpose(0, 2, 1), wd_s], axis=0).astype(BF16)

    def gather_ffn(wg_s, wu_s, wd_s, name, collective_id, after):
        w3_s, _ = lax.optimization_barrier((ffn_shards(wg_s, wu_s, wd_s), after))
        return _all_gather([w3_s], ["row"], name=name, collective_id=collective_id)[0]

    w3_s = ffn_shards(ffn1_w_gate, ffn1_w_up, ffn1_w_down)
    w3_1a = _all_gather([w3_s[:, :FFN1_FIRST]], ["row"], name="gather_ffn1a", collective_id=0)[0]
    w3_s, _ = lax.optimization_barrier((w3_s, w3_1a))
    w3_1b = _all_gather([w3_s[:, FFN1_FIRST:]], ["row"], name="gather_ffn1b", collective_id=11)[0]

    def gather_mix(after):
        mix_s = lax.optimization_barrier((w_in[0].astype(BF16), w_mem_kv[0].astype(BF16), w_out[0].astype(BF16),
                                          (w3_1b, after)))[:3]
        win_all, wkv, wout = _all_gather(list(mix_s), ["stack", "row", "row"], name="gather_mix", collective_id=1)
        return _pack_win(win_all, tr=256, name="pack_w_in"), wkv, wout

    def gather_ffn2(after):
        return gather_ffn(ffn2_w_gate, ffn2_w_up, ffn2_w_down, "gather_ffn2", 2, after)

    small_w = [ffn1_norm, mix_norm, mem_norm, ffn2_norm, swa_q_norm, swa_k_norm, swa_sinks[0], rel_bias,
               gla_w_gate_up[0], gla_b_gate, gla_out_norm, mem_q_norm, mem_k_norm]
    collective_ids = {"ffn2": (3, 4), "mix": (5, 6), "ffn1a": (7, 8), "ffn1b": (9, 10)}
    reduced, small_box = {}, {}

    def on_grads(group, grads, carry, small=None):
        if group == "mix":
            dwin_p, dwkv, dwout = grads
            grads = [_unpack_win(dwin_p, tr=256, name="unpack_dw_in"), dwkv, dwout]
            kinds = ["stack", "row", "row"]
        else:
            kinds = ["row"]
        if reduced:
            earlier = list(reduced.values())[-1][1]
            *grads, _ = lax.optimization_barrier((*grads, earlier[0]))
        id_pair, id_chip = collective_ids[group]
        from_sibling = _pair_exchange(grads, kinds, name=f"pair_exchange_{group}", collective_id=id_pair)
        chip_sums = [_pair_sum(g, theirs, k, c_arr, name=f"pair_sum_{group}_{t}")
                     for t, (g, theirs, k) in enumerate(zip(grads, from_sibling, kinds))]
        if carry is not None:
            *chip_sums, carry = lax.optimization_barrier((*chip_sums, carry))
        if small is None:
            from_chips = _chip_exchange(chip_sums, None, name=f"chip_exchange_{group}", collective_id=id_chip)
        else:
            packed = _pack_small(small, name=f"pack_small_{group}")
            *from_chips, small_all = _chip_exchange(chip_sums, packed, name=f"chip_exchange_{group}",
                                                    collective_id=id_chip)
            small_box[group] = small_all
        reduced[group] = (chip_sums, from_chips)
        return carry

    grad_x = _local_step(x[0], mem[0], loss_target[0], small_w, ((w3_1a, w3_1b), gather_mix, gather_ffn2), on_grads)

    big_w = {"ffn1_w_gate": ("ffn1", 0, 0, True, ffn1_w_gate, m_ffn1_w_gate, v_ffn1_w_gate),
             "ffn1_w_up": ("ffn1", 0, 1, True, ffn1_w_up, m_ffn1_w_up, v_ffn1_w_up),
             "ffn1_w_down": ("ffn1", 0, 2, False, ffn1_w_down, m_ffn1_w_down, v_ffn1_w_down),
             "w_in": ("mix", 0, 0, True, w_in, m_w_in, v_w_in),
             "w_mem_kv": ("mix", 1, 0, False, w_mem_kv, m_w_mem_kv, v_w_mem_kv),
             "w_out": ("mix", 2, 0, False, w_out, m_w_out, v_w_out),
             "ffn2_w_gate": ("ffn2", 0, 0, True, ffn2_w_gate, m_ffn2_w_gate, v_ffn2_w_gate),
             "ffn2_w_up": ("ffn2", 0, 1, True, ffn2_w_up, m_ffn2_w_up, v_ffn2_w_up),
             "ffn2_w_down": ("ffn2", 0, 2, False, ffn2_w_down, m_ffn2_w_down, v_ffn2_w_down)}
    res = {}
    for nm, (group, t, mat, transposed, w, m, v) in big_w.items():
        shape = w.shape
        if transposed:
            w, m, v = (a.transpose(0, 2, 1) for a in (w, m, v))
        if nm == "w_in":
            out = _adam_cols(reduced[group][0][t], reduced[group][1][t], xy_arr, w, m, v, tc=256, name=f"adam_{nm}")
            res[nm] = [a.transpose(0, 2, 1) for a in out]
            continue
        r = w.shape[1]
        halves = ["ffn1a", "ffn1b"] if group == "ffn1" else [group]
        if len(halves) == 1:
            tr = 256 if r % 256 == 0 else r
        else:
            tr = r
        out = _adam_big([reduced[k][0][t] for k in halves], [reduced[k][1][t] for k in halves], mat, xy_arr, w, m, v,
                        tr=tr, name=f"adam_{nm}")
        if transposed:
            out = [a.reshape(1, -1, d).transpose(0, 2, 1) for a in out]
        res[nm] = [a.reshape(shape) for a in out]
    small_names = ["ffn1_norm", "mix_norm", "mem_norm", "ffn2_norm", "swa_q_norm", "swa_k_norm", "swa_sinks", "rel_bias",
                   "gla_w_gate_up", "gla_b_gate", "gla_out_norm", "mem_q_norm", "mem_k_norm"]
    small_m = [m_ffn1_norm, m_mix_norm, m_mem_norm, m_ffn2_norm, m_swa_q_norm, m_swa_k_norm, m_swa_sinks, m_rel_bias,
               m_gla_w_gate_up, m_gla_b_gate, m_gla_out_norm, m_mem_q_norm, m_mem_k_norm]
    small_v = [v_ffn1_norm, v_mix_norm, v_mem_norm, v_ffn2_norm, v_swa_q_norm, v_swa_k_norm, v_swa_sinks, v_rel_bias,
               v_gla_w_gate_up, v_gla_b_gate, v_gla_out_norm, v_mem_q_norm, v_mem_k_norm]
    small_full = [ffn1_norm, mix_norm, mem_norm, ffn2_norm, swa_q_norm, swa_k_norm, swa_sinks, rel_bias,
                  gla_w_gate_up, gla_b_gate, gla_out_norm, mem_q_norm, mem_k_norm]
    zero = jnp.zeros((1, 1), F32)
    turned = ("rel_bias",)

    def two_d(nm, a):
        a = a.reshape(a.shape[-2:])
        return a.T if nm in turned else a

    for group, sel in (("ffn1a", slice(1, None)), ("ffn1b", slice(0, 1))):
        extra = [zero] if group == "ffn1a" else []
        ws, ms, vs = ([two_d(nm, a) for nm, a in zip(small_names[sel], arrs[sel])] + extra
                      for arrs in (small_full, small_m, small_v))
        updated = _adam_small(small_box[group], ws, ms, vs, name=f"adam_small_{group}")
        for nm, full, out in zip(small_names[sel], small_full[sel], updated):
            res[nm] = [(a.T if nm in turned else a).reshape(full.shape) for a in out]
        if extra:
            loss = updated[-1][0].reshape(())

    order = ["ffn1_norm", "ffn1_w_gate", "ffn1_w_up", "ffn1_w_down", "mix_norm", "mem_norm", "w_in", "w_mem_kv",
             "swa_q_norm", "swa_k_norm", "swa_sinks", "rel_bias", "gla_w_gate_up", "gla_b_gate", "gla_out_norm",
             "mem_q_norm", "mem_k_norm", "w_out", "ffn2_norm", "ffn2_w_gate", "ffn2_w_up", "ffn2_w_down"]
    outs = [loss, grad_x[None]]
    for q in range(4):
        outs += [res[nm][q] for nm in order]
    return tuple(outs)
```

```python
import functools
import math

import numpy as np
import jax
import jax.numpy as jnp
from jax import lax
from jax.experimental import pallas as pl
from jax.experimental.pallas import tpu as pltpu
from jax.experimental.pallas import tpu_sc as plsc

F32 = jnp.float32
BF16 = jnp.bfloat16
SDS = jax.ShapeDtypeStruct

EPS = 1e-6
HEAD_DIM = 64
SWA_HEADS = 8
SWA_KV_HEADS = 2
SWA_GROUP = SWA_HEADS // SWA_KV_HEADS
BLOCK = 128
N_BUCKETS = 32
MAX_DISTANCE = 128
GLA_HEADS = 4
GLA_DK = 32
GLA_DV = 64
GLA_RANK = 16
GLA_TAU = 16.0
GLA_CHUNK = 32
MEM_HEADS = 4
SWA_Q_W = SWA_HEADS * HEAD_DIM
SWA_KV_W = SWA_KV_HEADS * HEAD_DIM
GLA_QK_W = GLA_HEADS * GLA_DK
GLA_V_W = GLA_HEADS * GLA_DV
MEM_Q_W = MEM_HEADS * HEAD_DIM
IN_W = 1808
IN_W_PAD = 1920
COL_SQ, COL_SKV, COL_GQ, COL_GK, COL_GV, COL_GG, COL_MQ, COL_GLR = 0, 512, 768, 896, 1024, 1280, 1536, 1792

ADAM_LR = 0.001
ADAM_B1 = 0.9
ADAM_B2 = 0.999
ADAM_EPS = 1e-08
ADAM_WD = 0.01
ADAM_STEP = 10

N_DEV = 8
VMEM_LIMIT_BYTES = 56 * 1024 * 1024
MESH = pl.DeviceIdType.MESH


def _params(*sem):
    return pltpu.CompilerParams(dimension_semantics=sem or None, vmem_limit_bytes=VMEM_LIMIT_BYTES)


def _dot(a, b, ta, tb, precision=None):
    dims = (((0 if ta else 1,), (1 if tb else 0,)), ((), ()))
    return lax.dot_general(a, b, dims, preferred_element_type=F32, precision=precision)


def _mm_raw(a, b, ta=False, tb=False):
    return _dot(a.astype(BF16), b.astype(BF16), ta, tb)


def _mmf_raw(a, b, ta=False, tb=False):
    return _dot(a, b, ta, tb, lax.Precision.HIGHEST)


def _make_mm(raw):
    @functools.partial(jax.custom_vjp, nondiff_argnums=(2, 3))
    def mm(a, b, ta=False, tb=False):
        return raw(a, b, ta, tb)

    def fwd(a, b, ta, tb):
        return raw(a, b, ta, tb), (a, b)

    def bwd(ta, tb, res, g):
        a, b = res
        da = raw(b, g, tb, True) if ta else raw(g, b, False, not tb)
        db = raw(g, a, True, ta) if tb else raw(a, g, not ta, False)
        return da, db

    mm.defvjp(fwd, bwd)
    return mm


_mm = _make_mm(_mm_raw)
_mmf = _make_mm(_mmf_raw)


def _mm3(a, b, ta=False, tb=False):
    a_hi, b_hi = a.astype(BF16).astype(F32), b.astype(BF16).astype(F32)
    return _mm(a_hi, b_hi, ta, tb) + _mm(a_hi, b - b_hi, ta, tb) + _mm(a - a_hi, b_hi, ta, tb)


def _rms(x, g):
    return x * lax.rsqrt(jnp.mean(x * x, axis=-1, keepdims=True) + EPS) * g


def _silu_mul(g, u):
    return jax.nn.silu(g) * u


def _log_sigmoid(z):
    return jnp.minimum(z, 0.0) - jnp.log(1.0 + jnp.exp(-jnp.abs(z)))


def _matmul(a_list, b, *, ta=False, tb=False, tm, tn, b_blocks=None, res=None, scale=1.0, out_dtype=F32, name):
    if not isinstance(a_list, (list, tuple)):
        a_list = [a_list]
    n_a = len(a_list)
    m = a_list[0].shape[1] if ta else a_list[0].shape[0]
    ks = [a.shape[0] if ta else a.shape[1] for a in a_list]
    n = b.shape[0] if tb else b.shape[1]
    if b_blocks is None:
        assert n_a == 1
        b_blocks = [0]
    tm, tn = min(tm, m), min(tn, n)
    assert m % tm == 0 and n % tn == 0, (m, n, tm, tn)

    def body(*refs):
        a_refs, b_refs = refs[:n_a], refs[n_a:2 * n_a]
        r_ref = refs[2 * n_a] if res is not None else None
        o_ref = refs[-1]
        acc = _mm_raw(a_refs[0][...], b_refs[0][...], ta, tb)
        for k in range(1, n_a):
            acc = acc + _mm_raw(a_refs[k][...], b_refs[k][...], ta, tb)
        if scale != 1.0:
            acc = acc * scale
        if r_ref is not None:
            acc = r_ref[...] + acc
        o_ref[...] = acc.astype(out_dtype)

    in_specs = []
    for k in ks:
        in_specs.append(pl.BlockSpec((k, tm), lambda i, j: (0, i)) if ta else pl.BlockSpec((tm, k), lambda i, j: (i, 0)))
    for k, blk in zip(ks, b_blocks):
        if tb:
            in_specs.append(pl.BlockSpec((tn, k), functools.partial(lambda i, j, blk: (j, blk), blk=blk)))
        else:
            in_specs.append(pl.BlockSpec((k, tn), functools.partial(lambda i, j, blk: (blk, j), blk=blk)))
    args = list(a_list) + [b] * n_a
    if res is not None:
        in_specs.append(pl.BlockSpec((tm, tn), lambda i, j: (i, j)))
        args.append(res)
    return pl.pallas_call(
        body, name=name, grid=(m // tm, n // tn), in_specs=in_specs,
        out_specs=pl.BlockSpec((tm, tn), lambda i, j: (i, j)), out_shape=SDS((m, n), out_dtype),
        compiler_params=_params("parallel", "parallel"),
    )(*args)


def _win_pieces(w):
    glr_lo, glr_hi = COL_MQ, COL_MQ + GLA_RANK
    out = []
    for j in range(N_DEV):
        for lo, hi, shift in ((0, glr_lo, 0), (glr_lo, glr_hi, COL_GLR - glr_lo), (glr_hi, IN_W, COL_MQ - glr_hi)):
            s, e = max(j * w, lo), min((j + 1) * w, hi)
            if s < e:
                out.append((j, s - j * w, e - j * w, s + shift))
    return out


def _pack_win(win_all, *, tr, name):
    _, d, w = win_all.shape

    def body(i_ref, o_ref):
        for j, a, b, dst in _win_pieces(w):
            o_ref[:, dst:dst + b - a] = i_ref[j][:, a:b]
        o_ref[:, IN_W:] = jnp.zeros((tr, IN_W_PAD - IN_W), o_ref.dtype)

    return pl.pallas_call(
        body, name=name, grid=(d // tr,), in_specs=[pl.BlockSpec((N_DEV, tr, w), lambda i: (0, i, 0))],
        out_specs=pl.BlockSpec((tr, IN_W_PAD), lambda i: (i, 0)), out_shape=SDS((d, IN_W_PAD), win_all.dtype),
        compiler_params=_params("parallel"),
    )(win_all)


def _unpack_win(dwin_parts, *, tr, name):
    d = dwin_parts[0].shape[0]
    w = IN_W // N_DEV
    starts = [sum(p.shape[1] for p in dwin_parts[:k]) for k in range(len(dwin_parts) + 1)]

    def body(*refs):
        o_ref = refs[-1]
        for j, a, b, src in _win_pieces(w):
            for k, i_ref in enumerate(refs[:-1]):
                lo, hi = max(src, starts[k]), min(src + b - a, starts[k + 1])
                if lo < hi:
                    o_ref[j % 2, j // 2, :, a + lo - src:a + hi - src] = i_ref[:, lo - starts[k]:hi - starts[k]]

    return pl.pallas_call(
        body, name=name, grid=(d // tr,), in_specs=[pl.BlockSpec((tr, p.shape[1]), lambda i: (i, 0)) for p in dwin_parts],
        out_specs=pl.BlockSpec((2, 4, tr, w), lambda i: (0, 0, i, 0)), out_shape=SDS((2, 4, d, w), dwin_parts[0].dtype),
        compiler_params=_params("parallel"),
    )(*dwin_parts)


def _dw_parts(h, parts, *, tk, name):
    s, d = h.shape
    n_p = len(parts)

    def body(*refs):
        h_ref, p_refs, o_refs, acc_refs = refs[0], refs[1:1 + n_p], refs[1 + n_p:1 + 2 * n_p], refs[1 + 2 * n_p:]
        i = pl.program_id(0)
        h_t = h_ref[...].T
        for p_ref, o_ref, acc_ref in zip(p_refs, o_refs, acc_refs):
            prod = _mm_raw(h_t, p_ref[...], False, False)

            @pl.when(i == 0)
            def _():
                acc_ref[...] = prod

            @pl.when(i > 0)
            def _():
                acc_ref[...] += prod

            @pl.when(i == pl.num_programs(0) - 1)
            def _():
                o_ref[...] = acc_ref[...].astype(BF16)

    return pl.pallas_call(
        body, name=name, grid=(s // tk,),
        in_specs=[pl.BlockSpec((tk, d), lambda i: (i, 0))] + [pl.BlockSpec((tk, p.shape[1]), lambda i: (i, 0)) for p in parts],
        out_specs=[pl.BlockSpec((d, p.shape[1]), lambda i: (0, 0)) for p in parts],
        out_shape=[SDS((d, p.shape[1]), BF16) for p in parts],
        scratch_shapes=[pltpu.VMEM((d, p.shape[1]), F32) for p in parts],
        compiler_params=_params("arbitrary"),
    )(h, *parts)


def _rms_fwd(x, g, *, tm, name):
    s, d = x.shape

    def body(x_ref, g_ref, h_ref):
        h_ref[...] = _rms(x_ref[...], g_ref[...]).astype(BF16)

    return pl.pallas_call(
        body, name=name, grid=(s // tm,),
        in_specs=[pl.BlockSpec((tm, d), lambda i: (i, 0)), pl.BlockSpec((1, d), lambda i: (0, 0))],
        out_specs=pl.BlockSpec((tm, d), lambda i: (i, 0)), out_shape=SDS((s, d), BF16),
        compiler_params=_params("parallel"),
    )(x, g)


def _rms_bwd(x, g, dh, dres, *, tm, name):
    s, d = x.shape
    want_dx = dres is not None
    product = isinstance(dh, tuple)
    if product:
        a_parts = list(dh[0]) if isinstance(dh[0], (list, tuple)) else [dh[0]]
        starts = [sum(a.shape[1] for a in a_parts[:k]) for k in range(len(a_parts) + 1)]

    def body(*refs):
        n_dh = len(a_parts) + 1 if product else 1
        x_ref, g_ref = refs[:2]
        dh_refs, rest = refs[2:2 + n_dh], refs[2 + n_dh:]
        if want_dx:
            dres_ref, dx_ref, dxb_ref, dg_ref = rest
        else:
            dg_ref, = rest
        if product:
            b_ref = dh_refs[-1]
            dh_tile = _mm_raw(dh_refs[0][...], b_ref[:, starts[0]:starts[1]], False, True)
            for k in range(1, len(a_parts)):
                dh_tile = dh_tile + _mm_raw(dh_refs[k][...], b_ref[:, starts[k]:starts[k + 1]], False, True)
        else:
            dh_tile = dh_refs[0][...]
        _, vjp = jax.vjp(_rms, x_ref[...], g_ref[...])
        dx, dg = vjp(dh_tile)
        if want_dx:
            dx = dres_ref[...] + dx
            dx_ref[...] = dx
            dxb_ref[...] = dx.astype(BF16)

        @pl.when(pl.program_id(0) == 0)
        def _():
            dg_ref[...] = jnp.zeros_like(dg_ref)

        dg_ref[...] += dg

    row = pl.BlockSpec((tm, d), lambda i: (i, 0))
    vec = pl.BlockSpec((1, d), lambda i: (0, 0))
    if product:
        dh_specs = [pl.BlockSpec((tm, a.shape[1]), lambda i: (i, 0)) for a in a_parts]
        dh_specs.append(pl.BlockSpec((d, starts[-1]), lambda i: (0, 0)))
        dh_args = a_parts + [dh[1]]
    else:
        dh_specs, dh_args = [row], [dh]
    if want_dx:
        return pl.pallas_call(
            body, name=name, grid=(s // tm,), in_specs=[row, vec] + dh_specs + [row], out_specs=[row, row, vec],
            out_shape=[SDS((s, d), F32), SDS((s, d), BF16), SDS((1, d), F32)], compiler_params=_params("arbitrary"),
        )(x, g, *dh_args, dres)
    return None, None, pl.pallas_call(
        body, name=name, grid=(s // tm,), in_specs=[row, vec] + dh_specs, out_specs=vec,
        out_shape=SDS((1, d), F32), compiler_params=_params("arbitrary"),
    )(x, g, *dh_args)


FFN_TN = 256
FFN_TN_FWD = 512
FFN1_FIRST = 192


def _ffn_fwd(x, gain, w3, tag, *, tm=1024, next_gain=None, target=None, start=None, partial=False):
    s, d = x.shape
    f = w3.shape[1]
    tn = FFN_TN_FWD if f % FFN_TN_FWD == 0 else FFN_TN
    nj = f // tn
    tm = min(tm, s)
    n_extra = (next_gain is not None) + (target is not None) + 2 * (start is not None)

    def body(*refs):
        x_ref, gain_ref, wg_ref, wu_ref, wd_ref = refs[:5]
        extra, outs = refs[5:5 + n_extra], refs[5 + n_extra:-1]
        acc_s = refs[-1]
        g_ref, u_ref = outs[-2:]
        h_ref = extra[-2] if start is not None else outs[-3]
        i, j = pl.program_id(0), pl.program_id(1)

        @pl.when(j == 0)
        def _():
            if start is None:
                h_ref[...] = _rms(x_ref[...], gain_ref[...]).astype(BF16)
                acc_s[...] = jnp.zeros_like(acc_s)
            else:
                acc_s[...] = extra[-1][...]

        hv = h_ref[...]
        g = _mm_raw(hv, wg_ref[...], False, True)
        u = _mm_raw(hv, wu_ref[...], False, True)
        g_ref[...] = g.astype(BF16)
        u_ref[...] = u.astype(BF16)
        acc_s[...] += _mm_raw(_silu_mul(g, u), wd_ref[...])

        @pl.when(j == nj - 1)
        def _():
            y = acc_s[...] if partial else x_ref[...] + 0.5 * acc_s[...]
            if target is None:
                outs[0][...] = y
                if next_gain is not None:
                    outs[1][...] = _rms(y, extra[0][...]).astype(BF16)
            else:
                dy_ref, dyb_ref, loss_ref = outs[:3]
                diff = y - extra[0][...]
                dy_ref[...] = diff * (1.0 / d)
                dyb_ref[...] = (diff * (1.0 / d)).astype(BF16)
                part = 0.5 * jnp.sum(jnp.mean(diff * diff, axis=-1, keepdims=True), axis=0, keepdims=True)

                @pl.when(i == 0)
                def _():
                    loss_ref[...] = part

                @pl.when(i > 0)
                def _():
                    loss_ref[...] += part

    row = pl.BlockSpec((tm, d), lambda i, j: (i, 0))
    vec = pl.BlockSpec((1, d), lambda i, j: (0, 0))
    tile = pl.BlockSpec((tm, tn), lambda i, j: (i, j))
    in_specs = [row, vec] + [pl.BlockSpec((None, tn, d), functools.partial(lambda i, j, k: (k, j, 0), k=k)) for k in range(3)]
    args = [x, gain, w3, w3, w3]
    if target is None:
        out_specs, out_shape = [row], [SDS((s, d), F32)]
        if next_gain is not None:
            in_specs.append(vec)
            args.append(next_gain)
            out_specs.append(row)
            out_shape.append(SDS((s, d), BF16))
    else:
        in_specs.append(row)
        args.append(target)
        out_specs = [row, row, pl.BlockSpec((1, 1), lambda i, j: (0, 0))]
        out_shape = [SDS((s, d), F32), SDS((s, d), BF16), SDS((1, 1), F32)]
    if start is None:
        out_specs.append(row)
        out_shape.append(SDS((s, d), BF16))
    else:
        in_specs += [row, row]
        args += list(start)
    *head, g, u = pl.pallas_call(
        body, name=f"{tag}_fwd", grid=(s // tm, nj), in_specs=in_specs,
        out_specs=out_specs + [tile, tile],
        out_shape=out_shape + [SDS((s, f), BF16), SDS((s, f), BF16)],
        scratch_shapes=[pltpu.VMEM((tm, d), F32)],
        compiler_params=_params("arbitrary", "arbitrary"),
    )(*args)
    if start is None:
        *head, h = head
    else:
        h = start[0]
    return head, (h, g, u)


def _ffn_bwd_part(dyb, w3, saved, first, count, dh_init, *, name):
    h, g, u = saved
    s, d = h.shape
    tn = FFN_TN

    def body(*refs):
        if dh_init is None:
            dy_ref, h_ref, wd_ref, wg_ref, wu_ref, g_ref, u_ref, dh_ref, dw3_ref, dg_s, du_s, a_s = refs
        else:
            dy_ref, h_ref, wd_ref, wg_ref, wu_ref, g_ref, u_ref, dh0_ref, dh_ref, dw3_ref, dg_s, du_s, a_s = refs
        j = pl.program_id(0)

        @pl.when(j == 0)
        def _():
            dh_ref[...] = jnp.zeros_like(dh_ref) if dh_init is None else dh0_ref[...]
            for ref in (dg_s, du_s, a_s):
                ref[...] = jnp.zeros_like(ref)

        now, before = j % 2, 1 - j % 2
        dyv = dy_ref[...]
        hv = h_ref[...]
        dg, du, a = dg_s[before], du_s[before], a_s[before]
        dh_ref[...] += _mm_raw(dg, wg_ref[...]) + _mm_raw(du, wu_ref[...])
        dw3_ref[0] = _mm_raw(dg, hv, True, False).astype(BF16)
        dw3_ref[1] = _mm_raw(du, hv, True, False).astype(BF16)
        dw3_ref[2] = (_mm_raw(a, dyv, True, False) * 0.5).astype(BF16)

        da = _mm_raw(dyv, wd_ref[...], False, True) * 0.5
        a, vjp = jax.vjp(_silu_mul, g_ref[...].astype(F32), u_ref[...].astype(F32))
        dg, du = vjp(da)
        dg_s[now] = dg.astype(BF16)
        du_s[now] = du.astype(BF16)
        a_s[now] = a.astype(BF16)

    this = lambda j: first + jnp.minimum(j, count - 1)
    last = lambda j: first + jnp.maximum(j - 1, 0)
    full = pl.BlockSpec((s, d), lambda j: (0, 0))
    once = pl.BlockSpec((s, d), lambda j: (0, 0), pipeline_mode=pl.Buffered(1))
    tile = pl.BlockSpec((s, tn), lambda j: (0, this(j)))
    in_specs = [once, once, pl.BlockSpec((None, tn, d), lambda j: (2, this(j), 0)),
                pl.BlockSpec((None, tn, d), lambda j: (0, last(j), 0)), pl.BlockSpec((None, tn, d), lambda j: (1, last(j), 0)),
                tile, tile]
    args = [dyb, h, w3, w3, w3, g, u]
    if dh_init is not None:
        in_specs.append(once)
        args.append(dh_init)
    return pl.pallas_call(
        body, name=name, grid=(count + 1,), in_specs=in_specs,
        out_specs=[full, pl.BlockSpec((3, tn, d), lambda j: (0, jnp.maximum(j - 1, 0), 0))],
        out_shape=[SDS((s, d), F32), SDS((3, count * tn, d), BF16)],
        scratch_shapes=[pltpu.VMEM((2, s, tn), BF16)] * 3,
        compiler_params=_params("arbitrary"),
    )(*args)


def _bucket_table():
    qi = np.arange(BLOCK)[:, None]
    kj = np.arange(2 * BLOCK)[None, :]
    dist = np.maximum(qi + BLOCK - kj, 0)
    max_exact = N_BUCKETS // 2
    d = np.maximum(dist, 1).astype(np.float32)
    large = max_exact + (np.log(d / np.float32(max_exact)) / np.float32(math.log(MAX_DISTANCE / max_exact))
                         * np.float32(N_BUCKETS - max_exact)).astype(np.int32)
    large = np.minimum(large, N_BUCKETS - 1)
    band = np.where(dist < max_exact, dist, large).astype(np.int32)
    return np.where(np.tril(np.ones((BLOCK, BLOCK), bool)), band[:, BLOCK:], band[:, :BLOCK])


SWA_STACK = SWA_GROUP * BLOCK


def _swa_masks(n):
    qi = lax.broadcasted_iota(jnp.int32, (SWA_STACK, BLOCK), 0) % BLOCK
    kj = lax.broadcasted_iota(jnp.int32, (SWA_STACK, BLOCK), 1)
    own = kj <= qi
    return own, own | (n > 0)


def _swa_group(q, kp, kc, vp, vc, qg, kg, sink, bias, own, valid):
    qn = _rms(q, qg)
    s = jnp.where(own, _mm(qn, _rms(kc, kg), False, True), _mm(qn, _rms(kp, kg), False, True))
    s = s * (HEAD_DIM ** -0.5) + bias
    s = jnp.where(valid, s, -jnp.inf)
    m = lax.stop_gradient(jnp.maximum(jnp.max(s, axis=-1, keepdims=True), sink))
    p = jnp.exp(s - m)
    p = p / (jnp.sum(p, axis=-1, keepdims=True) + jnp.exp(sink - m))
    return _mm(jnp.where(own, p, 0.0), vc) + _mm(jnp.where(own, 0.0, p), vp)


def _swa_bias_table(rb_ref, bucket, bias_s):
    for h in range(SWA_HEADS):
        acc = jnp.zeros((BLOCK, BLOCK), F32)
        for b in range(N_BUCKETS):
            acc = jnp.where(bucket == b, rb_ref[b, h], acc)
        bias_s[h // SWA_GROUP, (h % SWA_GROUP) * BLOCK:(h % SWA_GROUP + 1) * BLOCK, :] = acc


def _swa_stack(ref, g):
    return jnp.concatenate([ref[:, (g * SWA_GROUP + hh) * HEAD_DIM:(g * SWA_GROUP + hh + 1) * HEAD_DIM]
                            for hh in range(SWA_GROUP)], axis=0)


def _swa_unstack(ref, g, stacked):
    for hh in range(SWA_GROUP):
        h = g * SWA_GROUP + hh
        ref[:, h * HEAD_DIM:(h + 1) * HEAD_DIM] = stacked[hh * BLOCK:(hh + 1) * BLOCK]


def _swa_sink_column(sink_ref, g):
    head = lax.broadcasted_iota(jnp.int32, (SWA_STACK, 1), 0) // BLOCK
    col = jnp.zeros((SWA_STACK, 1), F32)
    for hh in range(SWA_GROUP):
        col = jnp.where(head == hh, sink_ref[g * SWA_GROUP + hh], col)
    return col


def _swa_band(kvp_ref, kvc_ref, g):
    k = slice(g * HEAD_DIM, (g + 1) * HEAD_DIM)
    v = slice(SWA_KV_W + g * HEAD_DIM, SWA_KV_W + (g + 1) * HEAD_DIM)
    return kvp_ref[:, k], kvc_ref[:, k], kvp_ref[:, v], kvc_ref[:, v]


def _swa_specs(order):
    kvc = COL_SKV // (2 * SWA_KV_W)
    return [
        pl.BlockSpec((BLOCK, SWA_Q_W), lambda t: (order(t), 0)),
        pl.BlockSpec((BLOCK, 2 * SWA_KV_W), lambda t: (jnp.maximum(order(t) - 1, 0), kvc)),
        pl.BlockSpec((BLOCK, 2 * SWA_KV_W), lambda t: (order(t), kvc)),
        pl.BlockSpec((1, HEAD_DIM), lambda t: (0, 0)),
        pl.BlockSpec((1, HEAD_DIM), lambda t: (0, 0)),
        pl.BlockSpec(memory_space=pltpu.SMEM),
        pl.BlockSpec(memory_space=pltpu.SMEM),
        pl.BlockSpec((BLOCK, BLOCK), lambda t: (0, 0)),
    ]


def _swa_fwd(p, qg, kg, sinks, rel_bias, *, name):
    s = p.shape[0]
    nb = s // BLOCK

    def body(q_ref, kvp_ref, kvc_ref, qg_ref, kg_ref, sink_ref, rb_ref, bucket_ref, y_ref, bias_s):
        n = pl.program_id(0)

        @pl.when(n == 0)
        def _():
            _swa_bias_table(rb_ref, bucket_ref[...], bias_s)

        own, valid = _swa_masks(n)
        for g in range(SWA_KV_HEADS):
            out = _swa_group(_swa_stack(q_ref, g), *_swa_band(kvp_ref, kvc_ref, g), qg_ref[...], kg_ref[...],
                             _swa_sink_column(sink_ref, g), bias_s[g], own, valid)
            _swa_unstack(y_ref, g, out)

    return pl.pallas_call(
        body, name=name, grid=(nb,), in_specs=_swa_specs(lambda t: t),
        out_specs=pl.BlockSpec((BLOCK, SWA_Q_W), lambda t: (t, 0)), out_shape=SDS((s, SWA_Q_W), F32),
        scratch_shapes=[pltpu.VMEM((SWA_KV_HEADS, SWA_STACK, BLOCK), F32)],
        compiler_params=_params("arbitrary"),
    )(p, p, p, qg, kg, sinks, rel_bias, jnp.asarray(_bucket_table()))


def _swa_bwd(p, qg, kg, sinks, rel_bias, dy_all, *, name):
    s = p.shape[0]
    nb = s // BLOCK

    def body(q_ref, kvp_ref, kvc_ref, qg_ref, kg_ref, sink_ref, rb_ref, bucket_ref, dy_ref,
             dq_ref, dkv_ref, dqg_ref, dkg_ref, dsink_ref, drb_ref, bias_s, dbias_s, carry_s):
        t = pl.program_id(0)
        n = nb - 1 - t

        @pl.when(t == 0)
        def _():
            _swa_bias_table(rb_ref, bucket_ref[...], bias_s)
            dbias_s[...] = jnp.zeros_like(dbias_s)
            carry_s[...] = jnp.zeros_like(carry_s)
            dqg_ref[...] = jnp.zeros_like(dqg_ref)
            dkg_ref[...] = jnp.zeros_like(dkg_ref)
            dsink_ref[...] = jnp.zeros_like(dsink_ref)
            drb_ref[...] = jnp.zeros_like(drb_ref)

        own, valid = _swa_masks(n)
        lane = lax.broadcasted_iota(jnp.int32, (1, BLOCK), 1)
        dqg = jnp.zeros((1, HEAD_DIM), F32)
        dkg = jnp.zeros((1, HEAD_DIM), F32)
        dsink_vec = jnp.zeros((1, BLOCK), F32)
        for g in range(SWA_KV_HEADS):
            _, vjp = jax.vjp(functools.partial(_swa_group, own=own, valid=valid), _swa_stack(q_ref, g),
                             *_swa_band(kvp_ref, kvc_ref, g), qg_ref[...], kg_ref[...], _swa_sink_column(sink_ref, g),
                             bias_s[g])
            dq, dkp, dkc, dvp, dvc, dqg_g, dkg_g, dsink_col, dbias = vjp(_swa_stack(dy_ref, g))
            _swa_unstack(dq_ref, g, dq)
            dqg += dqg_g
            dkg += dkg_g
            dbias_s[g] += dbias
            for hh in range(SWA_GROUP):
                dsink_h = jnp.sum(dsink_col[hh * BLOCK:(hh + 1) * BLOCK], axis=0, keepdims=True)
                dsink_vec += jnp.where(lane == g * SWA_GROUP + hh, dsink_h, 0.0)
            lo = g * HEAD_DIM
            dkv_ref[:, lo:lo + HEAD_DIM] = dkc + carry_s[g]
            carry_s[g] = dkp
            lo += SWA_KV_W
            dkv_ref[:, lo:lo + HEAD_DIM] = dvc + carry_s[SWA_KV_HEADS + g]
            carry_s[SWA_KV_HEADS + g] = dvp
        dqg_ref[...] += dqg
        dkg_ref[...] += dkg
        dsink_ref[...] += dsink_vec

        @pl.when(t == nb - 1)
        def _():
            bucket = bucket_ref[...]
            row = lax.broadcasted_iota(jnp.int32, (N_BUCKETS, BLOCK), 0)
            col = lax.broadcasted_iota(jnp.int32, (N_BUCKETS, BLOCK), 1)
            acc = jnp.zeros((N_BUCKETS, BLOCK), F32)
            for h in range(SWA_HEADS):
                dbias = dbias_s[h // SWA_GROUP, (h % SWA_GROUP) * BLOCK:(h % SWA_GROUP + 1) * BLOCK, :]
                for b in range(N_BUCKETS):
                    part = jnp.sum(jnp.where(bucket == b, dbias, 0.0), axis=1, keepdims=True)
                    val = jnp.sum(part, axis=0, keepdims=True)
                    acc = acc + jnp.where((row == b) & (col == h), val, 0.0)
            drb_ref[...] = acc

    order = lambda t: nb - 1 - t
    vec = pl.BlockSpec((1, HEAD_DIM), lambda t: (0, 0))
    return pl.pallas_call(
        body, name=name, grid=(nb,),
        in_specs=_swa_specs(order) + [pl.BlockSpec((BLOCK, SWA_Q_W), lambda t: (order(t), 0))],
        out_specs=[pl.BlockSpec((BLOCK, SWA_Q_W), lambda t: (order(t), 0)),
                   pl.BlockSpec((BLOCK, 2 * SWA_KV_W), lambda t: (order(t), 0)),
                   vec, vec, pl.BlockSpec((1, BLOCK), lambda t: (0, 0)),
                   pl.BlockSpec((N_BUCKETS, BLOCK), lambda t: (0, 0))],
        out_shape=[SDS((s, SWA_Q_W), F32), SDS((s, 2 * SWA_KV_W), F32), SDS((1, HEAD_DIM), F32),
                   SDS((1, HEAD_DIM), F32), SDS((1, BLOCK), F32), SDS((N_BUCKETS, BLOCK), F32)],
        scratch_shapes=[pltpu.VMEM((SWA_KV_HEADS, SWA_STACK, BLOCK), F32),
                        pltpu.VMEM((SWA_KV_HEADS, SWA_STACK, BLOCK), F32),
                        pltpu.VMEM((2 * SWA_KV_HEADS, BLOCK, HEAD_DIM), F32)],
        compiler_params=_params("arbitrary"),
    )(p, p, p, qg, kg, sinks, rel_bias, jnp.asarray(_bucket_table()), dy_all)


def _mem_head(q, k, v, qg, kg):
    qn = _rms(q, qg)
    kn = _rms(k, kg)
    s = _mm(qn, kn, False, True) * (HEAD_DIM ** -0.5)
    m = lax.stop_gradient(jnp.max(s, axis=-1, keepdims=True))
    e = jnp.exp(s - m)
    return _mm(e / jnp.sum(e, axis=-1, keepdims=True), v)


def _mem_fwd(p, kv, qg, kg, *, tq, name):
    s = p.shape[0]
    m = kv.shape[0]

    def body(q_ref, kv_ref, qg_ref, kg_ref, y_ref):
        for h in range(MEM_HEADS):
            cols = slice(h * HEAD_DIM, (h + 1) * HEAD_DIM)
            vcols = slice(MEM_Q_W + h * HEAD_DIM, MEM_Q_W + (h + 1) * HEAD_DIM)
            y_ref[:, cols] = _mem_head(q_ref[:, cols], kv_ref[:, cols], kv_ref[:, vcols], qg_ref[...], kg_ref[...])

    vec = pl.BlockSpec((1, HEAD_DIM), lambda t: (0, 0))
    return pl.pallas_call(
        body, name=name, grid=(s // tq,),
        in_specs=[pl.BlockSpec((tq, MEM_Q_W), lambda t: (t, COL_MQ // MEM_Q_W)),
                  pl.BlockSpec((m, 2 * MEM_Q_W), lambda t: (0, 0)), vec, vec],
        out_specs=pl.BlockSpec((tq, MEM_Q_W), lambda t: (t, 0)), out_shape=SDS((s, MEM_Q_W), F32),
        compiler_params=_params("parallel"),
    )(p, kv, qg, kg)


def _mem_bwd(p, kv, qg, kg, dy_all, *, tq, name):
    s = p.shape[0]
    m = kv.shape[0]

    def body(q_ref, kv_ref, qg_ref, kg_ref, dy_ref, dq_ref, dkv_ref, dqg_ref, dkg_ref):
        @pl.when(pl.program_id(0) == 0)
        def _():
            dkv_ref[...] = jnp.zeros_like(dkv_ref)
            dqg_ref[...] = jnp.zeros_like(dqg_ref)
            dkg_ref[...] = jnp.zeros_like(dkg_ref)

        dqg = jnp.zeros((1, HEAD_DIM), F32)
        dkg = jnp.zeros((1, HEAD_DIM), F32)
        for h in range(MEM_HEADS):
            cols = slice(h * HEAD_DIM, (h + 1) * HEAD_DIM)
            vcols = slice(MEM_Q_W + h * HEAD_DIM, MEM_Q_W + (h + 1) * HEAD_DIM)
            _, vjp = jax.vjp(_mem_head, q_ref[:, cols], kv_ref[:, cols], kv_ref[:, vcols], qg_ref[...], kg_ref[...])
            dq, dk, dv, dqg_h, dkg_h = vjp(dy_ref[:, cols])
            dq_ref[:, cols] = dq
            dkv_ref[:, cols] += dk
            dkv_ref[:, vcols] += dv
            dqg += dqg_h
            dkg += dkg_h
        dqg_ref[...] += dqg
        dkg_ref[...] += dkg

    vec = pl.BlockSpec((1, HEAD_DIM), lambda t: (0, 0))
    full = pl.BlockSpec((m, 2 * MEM_Q_W), lambda t: (0, 0))
    dy_col = (SWA_Q_W + GLA_V_W) // MEM_Q_W
    return pl.pallas_call(
        body, name=name, grid=(s // tq,),
        in_specs=[pl.BlockSpec((tq, MEM_Q_W), lambda t: (t, COL_MQ // MEM_Q_W)), full, vec, vec,
                  pl.BlockSpec((tq, MEM_Q_W), lambda t: (t, dy_col))],
        out_specs=[pl.BlockSpec((tq, MEM_Q_W), lambda t: (t, 0)), full, vec, vec],
        out_shape=[SDS((s, MEM_Q_W), F32), SDS((m, 2 * MEM_Q_W), F32), SDS((1, HEAD_DIM), F32), SDS((1, HEAD_DIM), F32)],
        compiler_params=_params("arbitrary"),
    )(p, kv, qg, kg, dy_all)


GLA_ROWS = 256


GLA_GROUP = 4


def _gla_consts():
    c, h, r = GLA_CHUNK, GLA_HEADS, GLA_GROUP * GLA_CHUNK
    i2 = lax.broadcasted_iota(jnp.int32, (c, c), 0)
    j2 = lax.broadcasted_iota(jnp.int32, (c, c), 1)
    slab_q = lax.broadcasted_iota(jnp.int32, (h, r, GLA_QK_W), 0)
    lane_q = lax.broadcasted_iota(jnp.int32, (h, r, GLA_QK_W), 2)
    row_a = lax.broadcasted_iota(jnp.int32, (h * r, r), 0) % r
    col_a = lax.broadcasted_iota(jnp.int32, (h * r, r), 1)
    slab_o = lax.broadcasted_iota(jnp.int32, (h, r, GLA_V_W), 0)
    lane_o = lax.broadcasted_iota(jnp.int32, (h, r, GLA_V_W), 2)
    row_s = lax.broadcasted_iota(jnp.int32, (GLA_V_W, GLA_QK_W), 0)
    col_s = lax.broadcasted_iota(jnp.int32, (GLA_V_W, GLA_QK_W), 1)
    return dict(
        ltri=(j2 <= i2).astype(F32),
        m_q=(slab_q == lane_q // GLA_DK).astype(F32),
        causal=(col_a <= row_a) & (col_a // c == row_a // c),
        m_o=(slab_o == lane_o // GLA_DV).astype(F32),
        m_s=(row_s // GLA_DV == col_s // GLA_DK).astype(F32),
    )


def _gla_step(q, k, v, z, bg, st, c):
    h = GLA_HEADS
    kt, ka, qt, qe, decay = [], [], [], [], []
    for qc, kc, zc in zip(q, k, z):
        la = _log_sigmoid(zc + bg) * (1.0 / GLA_TAU)
        b = _mmf(c["ltri"], la)
        bl = jnp.sum(la, axis=0, keepdims=True)
        qs = qc * (GLA_DK ** -0.5)
        kt.append(kc * jnp.exp(bl - b))
        ka.append(kc * jnp.exp(0.5 * bl - b))
        qt.append(qs * jnp.exp(b - 0.5 * bl))
        qe.append(qs * jnp.exp(b))
        decay.append(jnp.exp(bl))
    o_intra = []
    rows = GLA_GROUP * GLA_CHUNK
    for lo in range(0, len(q), GLA_GROUP):
        qt_all, kt_all, v_all = (jnp.concatenate(parts[lo:lo + GLA_GROUP], axis=0) for parts in (qt, ka, v))
        q_stack = (jnp.broadcast_to(qt_all[None], (h, rows, GLA_QK_W)) * c["m_q"]).reshape(h * rows, GLA_QK_W)
        a = jnp.where(c["causal"], _mm3(q_stack, kt_all, False, True), 0.0)
        o_stack = _mm(a, v_all)
        o_intra.append(jnp.sum(o_stack.reshape(h, rows, GLA_V_W) * c["m_o"], axis=0))
    o_intra = jnp.concatenate(o_intra, axis=0)
    o_inter = []
    for qec, ktc, vc, dc in zip(qe, kt, v, decay):
        o_inter.append(_mm(qec, st, False, True))
        st = st * dc + _mm(vc, ktc, True, False) * c["m_s"]
    return o_intra + jnp.concatenate(o_inter, axis=0), st


def _gla_post(o, gg, gain, g64):
    ms = _mmf(o * o, g64) * (1.0 / GLA_DV)
    return o * lax.rsqrt(ms + EPS) * gain * jax.nn.silu(gg)


def _gla_g64():
    r = lax.broadcasted_iota(jnp.int32, (GLA_V_W, GLA_V_W), 0)
    c = lax.broadcasted_iota(jnp.int32, (GLA_V_W, GLA_V_W), 1)
    return (r // GLA_DV == c // GLA_DV).astype(F32)


def _gla_in_specs(order):
    r = GLA_ROWS
    return [
        pl.BlockSpec((r, GLA_QK_W), lambda t: (order(t), COL_GQ // GLA_QK_W)),
        pl.BlockSpec((r, GLA_QK_W), lambda t: (order(t), COL_GK // GLA_QK_W)),
        pl.BlockSpec((r, GLA_V_W), lambda t: (order(t), COL_GV // GLA_V_W)),
        pl.BlockSpec((r, GLA_V_W), lambda t: (order(t), COL_GG // GLA_V_W)),
        pl.BlockSpec((r, GLA_QK_W), lambda t: (order(t), 0)),
        pl.BlockSpec((1, GLA_QK_W), lambda t: (0, 0)),
        pl.BlockSpec((1, GLA_V_W), lambda t: (0, 0)),
    ]


def _gla_pieces(q_ref, k_ref, v_ref, z_ref, cps):
    chunk = lambda ref: [ref[ci * GLA_CHUNK:(ci + 1) * GLA_CHUNK, :] for ci in range(cps)]
    return chunk(q_ref), chunk(k_ref), chunk(v_ref), chunk(z_ref)


def _gla_fwd(p, z, bg, gain, *, name):
    s = p.shape[0]
    r = GLA_ROWS
    cps = r // GLA_CHUNK

    def body(q_ref, k_ref, v_ref, gg_ref, z_ref, bg_ref, gain_ref, y_ref, oraw_ref, stsave_ref, st_s):
        @pl.when(pl.program_id(0) == 0)
        def _():
            st_s[...] = jnp.zeros_like(st_s)

        st = st_s[...]
        stsave_ref[0] = st
        o, st = _gla_step(*_gla_pieces(q_ref, k_ref, v_ref, z_ref, cps), bg_ref[...], st, _gla_consts())
        oraw_ref[...] = o
        st_s[...] = st
        y_ref[...] = _gla_post(o, gg_ref[...], gain_ref[...], _gla_g64())

    rowv = pl.BlockSpec((r, GLA_V_W), lambda t: (t, 0))
    return pl.pallas_call(
        body, name=name, grid=(s // r,), in_specs=_gla_in_specs(lambda t: t),
        out_specs=[rowv, rowv, pl.BlockSpec((1, GLA_V_W, GLA_QK_W), lambda t: (t, 0, 0))],
        out_shape=[SDS((s, GLA_V_W), F32), SDS((s, GLA_V_W), F32), SDS((s // r, GLA_V_W, GLA_QK_W), F32)],
        scratch_shapes=[pltpu.VMEM((GLA_V_W, GLA_QK_W), F32)],
        compiler_params=_params("arbitrary"),
    )(p, p, p, p, z, bg, gain)


def _gla_bwd(p, z, bg, gain, oraw, stsave, dy_all, *, name):
    s = p.shape[0]
    r = GLA_ROWS
    cps = r // GLA_CHUNK
    nsteps = s // r
    w_qkvg = 2 * GLA_QK_W + 2 * GLA_V_W

    def body(q_ref, k_ref, v_ref, gg_ref, z_ref, bg_ref, gain_ref, oraw_ref, stsave_ref, dy_ref,
             dqkvg_ref, dz_ref, dbg_ref, dgain_ref, dst_s):
        @pl.when(pl.program_id(0) == 0)
        def _():
            dst_s[...] = jnp.zeros_like(dst_s)
            dbg_ref[...] = jnp.zeros_like(dbg_ref)
            dgain_ref[...] = jnp.zeros_like(dgain_ref)

        _, vjp = jax.vjp(functools.partial(_gla_post, g64=_gla_g64()), oraw_ref[...], gg_ref[...], gain_ref[...])
        do, dgg, dgain = vjp(dy_ref[...])
        dqkvg_ref[:, 2 * GLA_QK_W + GLA_V_W:] = dgg
        dgain_ref[...] += dgain
        _, vjp = jax.vjp(functools.partial(_gla_step, c=_gla_consts()), *_gla_pieces(q_ref, k_ref, v_ref, z_ref, cps),
                         bg_ref[...], stsave_ref[0])
        dq, dk, dv, dz, dbg, dst = vjp((do, dst_s[...]))
        for ci in range(cps):
            rows = slice(ci * GLA_CHUNK, (ci + 1) * GLA_CHUNK)
            dqkvg_ref[rows, 0:GLA_QK_W] = dq[ci]
            dqkvg_ref[rows, GLA_QK_W:2 * GLA_QK_W] = dk[ci]
            dqkvg_ref[rows, 2 * GLA_QK_W:2 * GLA_QK_W + GLA_V_W] = dv[ci]
            dz_ref[rows, :] = dz[ci]
        dst_s[...] = dst
        dbg_ref[...] += dbg

    order = lambda t: nsteps - 1 - t
    rowv = pl.BlockSpec((r, GLA_V_W), lambda t: (order(t), 0))
    return pl.pallas_call(
        body, name=name, grid=(nsteps,),
        in_specs=_gla_in_specs(order) + [
            rowv, pl.BlockSpec((1, GLA_V_W, GLA_QK_W), lambda t: (order(t), 0, 0)),
            pl.BlockSpec((r, GLA_V_W), lambda t: (order(t), SWA_Q_W // GLA_V_W))],
        out_specs=[pl.BlockSpec((r, w_qkvg), lambda t: (order(t), 0)), pl.BlockSpec((r, GLA_QK_W), lambda t: (order(t), 0)),
                   pl.BlockSpec((1, GLA_QK_W), lambda t: (0, 0)), pl.BlockSpec((1, GLA_V_W), lambda t: (0, 0))],
        out_shape=[SDS((s, w_qkvg), F32), SDS((s, GLA_QK_W), F32), SDS((1, GLA_QK_W), F32), SDS((1, GLA_V_W), F32)],
        scratch_shapes=[pltpu.VMEM((GLA_V_W, GLA_QK_W), F32)],
        compiler_params=_params("arbitrary"),
    )(p, p, p, p, z, bg, gain, oraw, stsave, dy_all)


def _local_step(x, mem, target, small, big, on_grads):
    g1, gmix, gmem, g2, sqg, skg, sinks, rel_bias, wgu, bg, gla_gain, mqg, mkg = small
    (w3_1a, w3_1b), gather_mix, gather_ffn2 = big
    wgu_pad = jnp.zeros((GLA_QK_W, GLA_QK_W), BF16).at[:GLA_RANK].set(wgu.astype(BF16))
    gain256 = jnp.tile(gla_gain, (1, GLA_HEADS))

    (part,), saved1a = _ffn_fwd(x, g1, w3_1a, "ffn1a", partial=True)
    win_p, wkv, wout = gather_mix(part)
    (x1, h), saved1b = _ffn_fwd(x, g1, w3_1b, "ffn1b", next_gain=gmix, start=(saved1a[0], part))
    w3_2 = gather_ffn2((wout, x1))
    p = _matmul(h, win_p, tm=512, tn=IN_W_PAD, name="mix_in")
    hm = _rms_fwd(mem, gmem, tm=256, name="mem_rms")
    kv = _matmul(hm, wkv, tm=256, tn=512, name="mem_kv")
    p_glr = p[:, COL_GLR:]
    z = _matmul(p_glr, wgu_pad, tm=1024, tn=GLA_QK_W, name="gla_gate")
    y_swa = _swa_fwd(p, sqg, skg, sinks, rel_bias, name="swa_fwd")
    y_gla, oraw, stsave = _gla_fwd(p, z, bg, gain256, name="gla_fwd")
    y_mem = _mem_fwd(p, kv, mqg, mkg, tq=512, name="mem_fwd")
    x2 = _matmul([y_swa, y_gla, y_mem], wout, b_blocks=[0, 2, 3], tm=512, tn=1024, res=x1, name="mix_out")
    (dy, dyb, loss), saved2 = _ffn_fwd(x2, g2, w3_2, "ffn2", target=target)

    dh2, dw3_2 = _ffn_bwd_part(dyb, w3_2, saved2, 0, w3_2.shape[1] // FFN_TN, None, name="ffn2_bwd")
    dx2, dx2b, dg2 = _rms_bwd(x2, g2, dh2, dy, tm=512, name="ffn2_drms")
    dx2b = on_grads("ffn2", [dw3_2], dx2b)
    dy_all = _matmul(dx2b, wout, tb=True, tm=512, tn=1024, name="mix_dy")
    dwout = _matmul(jnp.concatenate([y_swa, y_gla, y_mem], axis=1), dx2b, ta=True, tm=512, tn=1024, out_dtype=BF16,
                    name="mix_dw_out")
    dq_swa, dkv_swa, dsqg, dskg, dsink, drb = _swa_bwd(p, sqg, skg, sinks, rel_bias, dy_all, name="swa_bwd")
    dqkvg, dz, dbg, dgain256 = _gla_bwd(p, z, bg, gain256, oraw, stsave, dy_all, name="gla_bwd")
    dmq, dkv_mem, dmqg, dmkg = _mem_bwd(p, kv, mqg, mkg, dy_all, tq=512, name="mem_bwd")
    dglr = _matmul(dz, wgu_pad, tb=True, tm=1024, tn=GLA_QK_W, name="gla_gate_dx")
    dwgu_pad = _matmul(p_glr, dz, ta=True, tm=GLA_QK_W, tn=GLA_QK_W, name="gla_gate_dw")
    dp = [dq_swa, dkv_swa, dqkvg, dmq, dglr]
    dwin_p = _dw_parts(h, dp, tk=512, name="mix_dw_in")
    dx1, dx1b, dgmix = _rms_bwd(x1, gmix, (dp, win_p), dx2, tm=512, name="mix_dh_drms")
    dwkv = _matmul(hm, dkv_mem, ta=True, tm=512, tn=512, out_dtype=BF16, name="mem_dw_kv")
    dx1b = on_grads("mix", (dwin_p, dwkv, dwout), dx1b)
    _, _, dgmem = _rms_bwd(mem, gmem, (dkv_mem, wkv), None, tm=256, name="mem_dh_drms")
    dh1, dw3_1a = _ffn_bwd_part(dx1b, w3_1a, saved1a, 0, w3_1a.shape[1] // FFN_TN, None, name="ffn1_bwd_a")
    dgla_gain = dgain256.reshape(GLA_HEADS, GLA_DV).sum(axis=0, keepdims=True)
    dsmall = [dgmix, dgmem, dg2, dsqg, dskg, dsink[:, :SWA_HEADS], drb[:, :SWA_HEADS].T, dwgu_pad[:GLA_RANK], dbg,
              dgla_gain, dmqg, dmkg, loss]
    dh1, dgmem, dwgu_pad = on_grads("ffn1a", [dw3_1a], (dh1, dgmem, dwgu_pad), small=dsmall)
    dh1, dw3_1b = _ffn_bwd_part(dx1b, w3_1b, saved1b, 0, w3_1b.shape[1] // FFN_TN, dh1, name="ffn1_bwd_b")
    dx, _, dg1 = _rms_bwd(x, g1, dh1, dx1, tm=512, name="ffn1_drms")
    on_grads("ffn1b", [dw3_1b], None, small=[dg1])
    return dx


def _mesh_place():
    x, y, c = lax.axis_index("x"), lax.axis_index("y"), lax.axis_index("c")
    other_chips = [(1 - x, y), (x, 1 - y), (1 - x, 1 - y)]
    return x, y, c, other_chips


def _handshake(peers):
    barrier = pltpu.get_barrier_semaphore()
    for peer in peers:
        pl.semaphore_signal(barrier, inc=1, device_id=peer, device_id_type=MESH)
    pl.semaphore_wait(barrier, len(peers))


def _sequencer_call(body, operands, out_shapes, sems, *, name, collective_id):
    return pl.kernel(
        body, name=name, out_type=out_shapes, mesh=plsc.ScalarSubcoreMesh(axis_name="sequencer", num_cores=1),
        scratch_types=sems, compiler_params=pltpu.CompilerParams(collective_id=collective_id),
    )(*operands)


def _window(ref, kind, slot, shape):
    if kind == "row":
        rows = pl.ds(pl.multiple_of(slot * shape[-2], 8), shape[-2])
        return ref.at[(slice(None),) * (len(shape) - 2) + (rows,)]
    return ref.at[slot]


def _gathered(shape, kind):
    if kind == "row":
        return tuple(shape[:-2]) + (N_DEV * shape[-2], shape[-1])
    return (N_DEV,) + tuple(shape)


def _half(view, hf):
    if len(view.shape) == 4:
        return view.at[:, hf]
    n = view.shape[-2] // 2
    return view.at[(slice(None),) * (len(view.shape) - 2) + (pl.ds(hf * n, n),)]


def _all_gather(shards, kinds, *, name, collective_id):
    nt = len(shards)

    def body(*refs):
        x_refs, o_refs = refs[:nt], refs[nt:2 * nt]
        send_sems, recv_sems, local_sems = refs[2 * nt:]
        x, y, c, _ = _mesh_place()
        me, sibling, xn, yn, diag = (x, y, c), (x, y, 1 - c), (1 - x, y, c), (x, 1 - y, c), (1 - x, 1 - y, c)
        _handshake([sibling, xn, yn])

        def win(t, block):
            bx, by, bc = block
            return _window(o_refs[t], kinds[t], 4 * bx + 2 * by + bc, shards[t].shape)

        def copy(k, t, src, dst, to):
            return pltpu.make_async_remote_copy(src_ref=src, dst_ref=dst, send_sem=send_sems.at[k, t],
                                                recv_sem=recv_sems.at[k, t], device_id=to, device_id_type=MESH)

        def piece(k, t, block, hf, to, from_shard=False):
            dst = _half(win(t, block), hf)
            return copy(k, t, _half(x_refs[t], hf) if from_shard else dst, dst, to)

        mine = [pltpu.make_async_copy(x_refs[t], win(t, me), local_sems.at[t]) for t in range(nt)]
        sent = []

        def start(cp):
            cp.start()
            sent.append(cp)

        for cp in mine:
            cp.start()
        for t in range(nt):
            start(copy(0, t, x_refs[t], win(t, me), sibling))
        for hf_x, hf_y in ((0, 1), (1, 0)):
            for t in range(nt):
                start(piece(1 + hf_x, t, me, hf_x, xn, True))
                start(piece(3 + hf_y, t, me, hf_y, yn, True))
        for k, block, hf, onward, k_sib in ((1, xn, 0, (5, yn), 7), (4, yn, 1, (6, xn), 10), (2, xn, 1, None, 8),
                                           (3, yn, 0, None, 9), (5, diag, 0, None, 11), (6, diag, 1, None, 12)):
            for t in range(nt):
                piece(k, t, block, hf, me).wait_recv()
                if onward is not None:
                    start(piece(onward[0], t, block, hf, onward[1]))
                start(piece(k_sib, t, block, hf, sibling))
        for t in range(nt):
            copy(0, t, x_refs[t], win(t, sibling), me).wait_recv()
        for k_sib, block, hf in ((7, xn, 0), (10, yn, 1), (8, xn, 1), (9, yn, 0), (11, diag, 0), (12, diag, 1)):
            for t in range(nt):
                bx, by, _ = block
                piece(k_sib, t, (bx, by, 1 - c), hf, me).wait_recv()
        for cp in sent:
            cp.wait_send()
        for cp in mine:
            cp.wait()

    return _sequencer_call(
        body, shards, [SDS(_gathered(s.shape, k), s.dtype) for s, k in zip(shards, kinds)],
        [pltpu.SemaphoreType.DMA((13, nt)), pltpu.SemaphoreType.DMA((13, nt)), pltpu.SemaphoreType.DMA((nt,))],
        name=name, collective_id=collective_id)


def _part_shape(shape, kind):
    if kind == "row":
        return tuple(shape[:-2]) + (shape[-2] // N_DEV, shape[-1])
    return tuple(shape[2:])


def _pair_exchange(grads, kinds, *, name, collective_id):
    nt = len(grads)
    part = [_part_shape(g.shape, k) for g, k in zip(grads, kinds)]

    def body(*refs):
        g_refs, o_refs = refs[:nt], refs[nt:2 * nt]
        send_sems, recv_sems = refs[2 * nt:]
        x, y, c, _ = _mesh_place()
        _handshake([(x, y, 1 - c)])
        copies = []
        for t in range(nt):
            for xy in range(4):
                src = g_refs[t].at[1 - c, xy] if kinds[t] == "stack" else _window(g_refs[t], kinds[t], 2 * xy + 1 - c, part[t])
                copies.append(pltpu.make_async_remote_copy(
                    src_ref=src, dst_ref=o_refs[t].at[xy], send_sem=send_sems.at[xy, t], recv_sem=recv_sems.at[xy, t],
                    device_id=(x, y, 1 - c), device_id_type=MESH))
        for cp in copies:
            cp.start()
        for cp in copies:
            cp.wait()

    return _sequencer_call(
        body, grads, [SDS((4,) + p, g.dtype) for p, g in zip(part, grads)],
        [pltpu.SemaphoreType.DMA((4, nt)), pltpu.SemaphoreType.DMA((4, nt))], name=name, collective_id=collective_id)


def _chip_exchange(parts, small, *, name, collective_id):
    nt = len(parts)
    if small is None:
        def body_plain(*refs):
            s_refs, o_refs = refs[:nt], refs[nt:2 * nt]
            send_sems, recv_sems = refs[2 * nt:]
            x, y, c, chips = _mesh_place()
            _handshake([(*chip, c) for chip in chips])
            copies = [pltpu.make_async_remote_copy(
                src_ref=s_refs[t].at[2 * chip[0] + chip[1]], dst_ref=o_refs[t].at[j],
                send_sem=send_sems.at[j, t], recv_sem=recv_sems.at[j, t], device_id=(*chip, c), device_id_type=MESH)
                for j, chip in enumerate(chips) for t in range(nt)]
            for cp in copies:
                cp.start()
            for cp in copies:
                cp.wait()

        return _sequencer_call(
            body_plain, parts, [SDS((3,) + s.shape[1:], s.dtype) for s in parts],
            [pltpu.SemaphoreType.DMA((3, nt)), pltpu.SemaphoreType.DMA((3, nt))], name=name, collective_id=collective_id)

    def body(*refs):
        s_refs, small_ref = refs[:nt], refs[nt]
        o_refs, small_all = refs[nt + 1:2 * nt + 1], refs[2 * nt + 1]
        send_sems, recv_sems, small_send, small_recv, local_sem = refs[2 * nt + 2:]
        x, y, c, chips = _mesh_place()
        _handshake([(px, py, pc) for px in (x, 1 - x) for py in (y, 1 - y) for pc in (c, 1 - c)][1:])

        def copy(j, t, chip):
            return pltpu.make_async_remote_copy(
                src_ref=s_refs[t].at[2 * chip[0] + chip[1]], dst_ref=o_refs[t].at[j],
                send_sem=send_sems.at[j, t], recv_sem=recv_sems.at[j, t], device_id=(*chip, c), device_id_type=MESH)

        flips = [(fx, fy, fc) for fx in (0, 1) for fy in (0, 1) for fc in (0, 1)][1:]

        def small_copy(k):
            fx, fy, fc = flips[k]
            to = (x ^ fx if fx else x, y ^ fy if fy else y, c ^ fc if fc else c)
            rows = small_all.at[4 * x + 2 * y + c]
            return pltpu.make_async_remote_copy(
                src_ref=small_ref, dst_ref=rows, send_sem=small_send.at[k], recv_sem=small_recv.at[k],
                device_id=to, device_id_type=MESH)

        own = pltpu.make_async_copy(small_ref, small_all.at[4 * x + 2 * y + c], local_sem)
        own.start()
        copies = [copy(j, t, chip) for j, chip in enumerate(chips) for t in range(nt)]
        smalls = [small_copy(k) for k in range(7)]
        for cp in smalls + copies:
            cp.start()
        for cp in smalls + copies:
            cp.wait()
        own.wait()

    return _sequencer_call(
        body, list(parts) + [small],
        [SDS((3,) + s.shape[1:], s.dtype) for s in parts] + [SDS((N_DEV,) + small.shape, small.dtype)],
        [pltpu.SemaphoreType.DMA((3, nt)), pltpu.SemaphoreType.DMA((3, nt)),
         pltpu.SemaphoreType.DMA((7,)), pltpu.SemaphoreType.DMA((7,)), pltpu.SemaphoreType.DMA],
        name=name, collective_id=collective_id)


def _pair_sum(grad, theirs, kind, c, *, name):
    if kind == "row":
        r, l = theirs.shape[-2:]
        n = theirs.size // (4 * r * l)
        grad, theirs = grad.reshape(n, N_DEV * r, l), theirs.reshape(4, n, r, l)
        mine_spec = pl.BlockSpec((n, r, l), lambda xy, c_ref: (0, 2 * xy + c_ref[0], 0))
    else:
        r, l = theirs.shape[-2:]
        n = theirs.size // (4 * r * l)
        theirs = theirs.reshape(4, n, r, l)
        grad = grad.reshape(2, 4, n, r, l)
        mine_spec = pl.BlockSpec((None, None, n, r, l), lambda xy, c_ref: (c_ref[0], xy, 0, 0, 0))

    def body(c_ref, a_ref, b_ref, o_ref):
        o_ref[...] = (a_ref[...].astype(F32) + b_ref[...].astype(F32)).astype(BF16)

    part = pl.BlockSpec((None, n, r, l), lambda xy, c_ref: (xy, 0, 0, 0))
    return pl.pallas_call(
        body, name=name,
        grid_spec=pltpu.PrefetchScalarGridSpec(num_scalar_prefetch=1, grid=(4,), in_specs=[mine_spec, part], out_specs=part),
        out_shape=SDS((4, n, r, l), BF16), compiler_params=_params("parallel"),
    )(c, grad, theirs)


def _adamw(w, g, m, v):
    m = ADAM_B1 * m + (1.0 - ADAM_B1) * g
    v = ADAM_B2 * v + (1.0 - ADAM_B2) * jnp.square(g)
    m_hat = m / (1.0 - ADAM_B1 ** ADAM_STEP)
    v_hat = v / (1.0 - ADAM_B2 ** ADAM_STEP)
    delta = -ADAM_LR * (m_hat / (jnp.sqrt(v_hat) + ADAM_EPS) + ADAM_WD * w)
    return delta, m, v


def _adam_big(owns, others, mat, xy, w, m, v, *, tr, name):
    _, r, l = w.shape
    lp = owns[0].shape[-1]
    nq = len(owns)
    rows = [tr] if nq == 1 else [o.shape[-2] for o in owns]
    assert sum(rows) == tr and r % tr == 0, (r, tr, rows)

    def body(xy_ref, *refs):
        own_refs, oth_refs = refs[:nq], refs[nq:2 * nq]
        w_ref, m_ref, v_ref, g_out, d_out, m_out, v_out = refs[2 * nq:]
        parts = []
        for q in range(nq):
            gq = own_refs[q][0, 0].astype(F32)
            for j in range(3):
                gq = gq + oth_refs[q][j, 0].astype(F32)
            parts.append(gq)
        g = (parts[0] if nq == 1 else jnp.concatenate(parts, axis=0))[:, :l]
        delta, m_new, v_new = _adamw(w_ref[0], g, m_ref[0], v_ref[0])
        g_out[0] = g
        d_out[0] = delta
        m_out[0] = m_new
        v_out[0] = v_new

    blk = pl.BlockSpec((1, tr, l), lambda i, xy_ref: (0, i, 0))
    own_specs = [pl.BlockSpec((1, 1, n, lp), lambda i, xy_ref: (xy_ref[0], mat, i, 0)) for n in rows]
    oth_specs = [pl.BlockSpec((3, 1, n, lp), lambda i, xy_ref: (0, mat, i, 0)) for n in rows]
    return pl.pallas_call(
        body, name=name,
        grid_spec=pltpu.PrefetchScalarGridSpec(
            num_scalar_prefetch=1, grid=(r // tr,), in_specs=own_specs + oth_specs + [blk, blk, blk],
            out_specs=[blk, blk, blk, blk]),
        out_shape=[SDS(w.shape, F32)] * 4, compiler_params=_params("parallel"),
    )(xy, *owns, *others, w, m, v)


def _adam_cols(own, other, xy, w, m, v, *, tc, name):
    _, wc, d = w.shape
    wp = -(-wc // 128) * 128

    def body(xy_ref, own_ref, oth_ref, w_ref, m_ref, v_ref, g_out, d_out, m_out, v_out, buf):
        g = own_ref[0, 0].astype(F32)
        for j in range(3):
            g = g + oth_ref[j, 0].astype(F32)
        buf[:, :wc] = g
        buf[:, wc:] = jnp.zeros((tc, wp - wc), F32)
        g = buf[...].T[:wc]
        delta, m_new, v_new = _adamw(w_ref[0], g, m_ref[0], v_ref[0])
        g_out[0] = g
        d_out[0] = delta
        m_out[0] = m_new
        v_out[0] = v_new

    blk = pl.BlockSpec((1, wc, tc), lambda i, xy_ref: (0, 0, i))
    in_specs = [pl.BlockSpec((1, 1, tc, wc), lambda i, xy_ref: (xy_ref[0], 0, i, 0)),
                pl.BlockSpec((3, 1, tc, wc), lambda i, xy_ref: (0, 0, i, 0)), blk, blk, blk]
    return pl.pallas_call(
        body, name=name,
        grid_spec=pltpu.PrefetchScalarGridSpec(
            num_scalar_prefetch=1, grid=(d // tc,), in_specs=in_specs, out_specs=[blk, blk, blk, blk],
            scratch_shapes=[pltpu.VMEM((tc, wp), F32)]),
        out_shape=[SDS(w.shape, F32)] * 4, compiler_params=_params("parallel"),
    )(xy, own, other, w, m, v)


def _small_layout(shapes):
    out, at = [], 0
    for r, c in shapes:
        rows = c // 128 if (r == 1 and c > 128) else r
        out.append((at, rows))
        at += -(-rows // 8) * 8
    return out, at


def _pack_small(parts, *, name):
    shapes = [a.shape for a in parts]
    layout, total = _small_layout(shapes)

    def body(*refs):
        o_ref = refs[-1]
        o_ref[...] = jnp.zeros_like(o_ref)
        for x_ref, (r, c), (at, rows) in zip(refs, shapes, layout):
            if r == 1 and c > 128:
                for k in range(rows):
                    o_ref[at + k:at + k + 1, :] = x_ref[:, k * 128:(k + 1) * 128]
            else:
                o_ref[at:at + r, 0:c] = x_ref[...]

    return pl.pallas_call(body, name=name, out_shape=SDS((total, 128), F32))(*parts)


def _adam_small(g_all, ws, ms, vs, *, name):
    n = len(ws)
    shapes = [w.shape for w in ws]
    layout, _ = _small_layout(shapes)

    def body(g_ref, *refs):
        w_refs, m_refs, v_refs, outs = refs[:n], refs[n:2 * n], refs[2 * n:3 * n], refs[3 * n:]
        g_sum = g_ref[0]
        for k in range(1, N_DEV):
            g_sum = g_sum + g_ref[k]
        for i, ((r, c), (at, rows)) in enumerate(zip(shapes, layout)):
            if r == 1 and c > 128:
                g = jnp.concatenate([g_sum[at + k:at + k + 1, :] for k in range(rows)], axis=1)
            else:
                g = g_sum[at:at + r, 0:c]
            delta, m_new, v_new = _adamw(w_refs[i][...], g, m_refs[i][...], v_refs[i][...])
            for q, val in enumerate((g, delta, m_new, v_new)):
                outs[4 * i + q][...] = val

    flat = pl.pallas_call(body, name=name, out_shape=[SDS(s, F32) for s in shapes for _ in range(4)])(g_all, *ws, *ms, *vs)
    return [flat[4 * i:4 * i + 4] for i in range(n)]


def kernel(x, mem, ffn1_norm, ffn1_w_gate, ffn1_w_up, ffn1_w_down, mix_norm, mem_norm, w_in, w_mem_kv, swa_q_norm, swa_k_norm, swa_sinks, rel_bias, gla_w_gate_up, gla_b_gate, gla_out_norm, mem_q_norm, mem_k_norm, w_out, ffn2_norm, ffn2_w_gate, ffn2_w_up, ffn2_w_down, loss_target, m_ffn1_norm, m_ffn1_w_gate, m_ffn1_w_up, m_ffn1_w_down, m_mix_norm, m_mem_norm, m_w_in, m_w_mem_kv, m_swa_q_norm, m_swa_k_norm, m_swa_sinks, m_rel_bias, m_gla_w_gate_up, m_gla_b_gate, m_gla_out_norm, m_mem_q_norm, m_mem_k_norm, m_w_out, m_ffn2_norm, m_ffn2_w_gate, m_ffn2_w_up, m_ffn2_w_down, v_ffn1_norm, v_ffn1_w_gate, v_ffn1_w_up, v_ffn1_w_down, v_mix_norm, v_mem_norm, v_w_in, v_w_mem_kv, v_swa_q_norm, v_swa_k_norm, v_swa_sinks, v_rel_bias, v_gla_w_gate_up, v_gla_b_gate, v_gla_out_norm, v_mem_q_norm, v_mem_k_norm, v_w_out, v_ffn2_norm, v_ffn2_w_gate, v_ffn2_w_up, v_ffn2_w_down):
    xi, yi, ci = lax.axis_index("x"), lax.axis_index("y"), lax.axis_index("c")
    c_arr = jnp.reshape(ci, (1,)).astype(jnp.int32)
    xy_arr = jnp.reshape(2 * xi + yi, (1,)).astype(jnp.int32)
    d = x.shape[-1]

    def ffn_shards(wg_s, wu_s, wd_s):
        return jnp.concatenate([wg_s.transpose(0, 2, 1), wu_s.transpose(0, 2, 1), wd_s], axis=0).astype(BF16)

    def gather_ffn(wg_s, wu_s, wd_s, name, collective_id, after):
        w3_s, _ = lax.optimization_barrier((ffn_shards(wg_s, wu_s, wd_s), after))
        return _all_gather([w3_s], ["row"], name=name, collective_id=collective_id)[0]

    w3_s = ffn_shards(ffn1_w_gate, ffn1_w_up, ffn1_w_down)
    w3_1a = _all_gather([w3_s[:, :FFN1_FIRST]], ["row"], name="gather_ffn1a", collective_id=0)[0]
    w3_s, _ = lax.optimization_barrier((w3_s, w3_1a))
    w3_1b = _all_gather([w3_s[:, FFN1_FIRST:]], ["row"], name="gather_ffn1b", collective_id=11)[0]

    def gather_mix(after):
        mix_s = lax.optimization_barrier((w_in[0].astype(BF16), w_mem_kv[0].astype(BF16), w_out[0].astype(BF16),
                                          (w3_1b, after)))[:3]
        win_all, wkv, wout = _all_gather(list(mix_s), ["stack", "row", "row"], name="gather_mix", collective_id=1)
        return _pack_win(win_all, tr=256, name="pack_w_in"), wkv, wout

    def gather_ffn2(after):
        return gather_ffn(ffn2_w_gate, ffn2_w_up, ffn2_w_down, "gather_ffn2", 2, after)

    small_w = [ffn1_norm, mix_norm, mem_norm, ffn2_norm, swa_q_norm, swa_k_norm, swa_sinks[0], rel_bias,
               gla_w_gate_up[0], gla_b_gate, gla_out_norm, mem_q_norm, mem_k_norm]
    collective_ids = {"ffn2": (3, 4), "mix": (5, 6), "ffn1a": (7, 8), "ffn1b": (9, 10)}
    reduced, small_box = {}, {}

    def on_grads(group, grads, carry, small=None):
        if group == "mix":
            dwin_p, dwkv, dwout = grads
            grads = [_unpack_win(dwin_p, tr=256, name="unpack_dw_in"), dwkv, dwout]
            kinds = ["stack", "row", "row"]
        else:
            kinds = ["row"]
        if reduced:
            earlier = list(reduced.values())[-1][1]
            *grads, _ = lax.optimization_barrier((*grads, earlier[0]))
        id_pair, id_chip = collective_ids[group]
        from_sibling = _pair_exchange(grads, kinds, name=f"pair_exchange_{group}", collective_id=id_pair)
        chip_sums = [_pair_sum(g, theirs, k, c_arr, name=f"pair_sum_{group}_{t}")
                     for t, (g, theirs, k) in enumerate(zip(grads, from_sibling, kinds))]
        if carry is not None:
            *chip_sums, carry = lax.optimization_barrier((*chip_sums, carry))
        if small is None:
            from_chips = _chip_exchange(chip_sums, None, name=f"chip_exchange_{group}", collective_id=id_chip)
        else:
            packed = _pack_small(small, name=f"pack_small_{group}")
            *from_chips, small_all = _chip_exchange(chip_sums, packed, name=f"chip_exchange_{group}",
                                                    collective_id=id_chip)
            small_box[group] = small_all
        reduced[group] = (chip_sums, from_chips)
        return carry

    grad_x = _local_step(x[0], mem[0], loss_target[0], small_w, ((w3_1a, w3_1b), gather_mix, gather_ffn2), on_grads)

    big_w = {"ffn1_w_gate": ("ffn1", 0, 0, True, ffn1_w_gate, m_ffn1_w_gate, v_ffn1_w_gate),
             "ffn1_w_up": ("ffn1", 0, 1, True, ffn1_w_up, m_ffn1_w_up, v_ffn1_w_up),
             "ffn1_w_down": ("ffn1", 0, 2, False, ffn1_w_down, m_ffn1_w_down, v_ffn1_w_down),
             "w_in": ("mix", 0, 0, True, w_in, m_w_in, v_w_in),
             "w_mem_kv": ("mix", 1, 0, False, w_mem_kv, m_w_mem_kv, v_w_mem_kv),
             "w_out": ("mix", 2, 0, False, w_out, m_w_out, v_w_out),
             "ffn2_w_gate": ("ffn2", 0, 0, True, ffn2_w_gate, m_ffn2_w_gate, v_ffn2_w_gate),
             "ffn2_w_up": ("ffn2", 0, 1, True, ffn2_w_up, m_ffn2_w_up, v_ffn2_w_up),
             "ffn2_w_down": ("ffn2", 0, 2, False, ffn2_w_down, m_ffn2_w_down, v_ffn2_w_down)}
    res = {}
    for nm, (group, t, mat, transposed, w, m, v) in big_w.items():
        shape = w.shape
        if transposed:
            w, m, v = (a.transpose(0, 2, 1) for a in (w, m, v))
        if nm == "w_in":
            out = _adam_cols(reduced[group][0][t], reduced[group][1][t], xy_arr, w, m, v, tc=256, name=f"adam_{nm}")
            res[nm] = [a.transpose(0, 2, 1) for a in out]
            continue
        r = w.shape[1]
        halves = ["ffn1a", "ffn1b"] if group == "ffn1" else [group]
        if len(halves) == 1:
            tr = 256 if r % 256 == 0 else r
        else:
            tr = r
        out = _adam_big([reduced[k][0][t] for k in halves], [reduced[k][1][t] for k in halves], mat, xy_arr, w, m, v,
                        tr=tr, name=f"adam_{nm}")
        if transposed:
            out = [a.reshape(1, -1, d).transpose(0, 2, 1) for a in out]
        res[nm] = [a.reshape(shape) for a in out]
    small_names = ["ffn1_norm", "mix_norm", "mem_norm", "ffn2_norm", "swa_q_norm", "swa_k_norm", "swa_sinks", "rel_bias",
                   "gla_w_gate_up", "gla_b_gate", "gla_out_norm", "mem_q_norm", "mem_k_norm"]
    small_m = [m_ffn1_norm, m_mix_norm, m_mem_norm, m_ffn2_norm, m_swa_q_norm, m_swa_k_norm, m_swa_sinks, m_rel_bias,
               m_gla_w_gate_up, m_gla_b_gate, m_gla_out_norm, m_mem_q_norm, m_mem_k_norm]
    small_v = [v_ffn1_norm, v_mix_norm, v_mem_norm, v_ffn2_norm, v_swa_q_norm, v_swa_k_norm, v_swa_sinks, v_rel_bias,
               v_gla_w_gate_up, v_gla_b_gate, v_gla_out_norm, v_mem_q_norm, v_mem_k_norm]
    small_full = [ffn1_norm, mix_norm, mem_norm, ffn2_norm, swa_q_norm, swa_k_norm, swa_sinks, rel_bias,
                  gla_w_gate_up, gla_b_gate, gla_out_norm, mem_q_norm, mem_k_norm]
    zero = jnp.zeros((1, 1), F32)
    turned = ("rel_bias",)

    def two_d(nm, a):
        a = a.reshape(a.shape[-2:])
        return a.T if nm in turned else a

    for group, sel in (("ffn1a", slice(1, None)), ("ffn1b", slice(0, 1))):
        extra = [zero] if group == "ffn1a" else []
        ws, ms, vs = ([two_d(nm, a) for nm, a in zip(small_names[sel], arrs[sel])] + extra
                      for arrs in (small_full, small_m, small_v))
        updated = _adam_small(small_box[group], ws, ms, vs, name=f"adam_small_{group}")
        for nm, full, out in zip(small_names[sel], small_full[sel], updated):
            res[nm] = [(a.T if nm in turned else a).reshape(full.shape) for a in out]
        if extra:
            loss = updated[-1][0].reshape(())

    order = ["ffn1_norm", "ffn1_w_gate", "ffn1_w_up", "ffn1_w_down", "mix_norm", "mem_norm", "w_in", "w_mem_kv",
             "swa_q_norm", "swa_k_norm", "swa_sinks", "rel_bias", "gla_w_gate_up", "gla_b_gate", "gla_out_norm",
             "mem_q_norm", "mem_k_norm", "w_out", "ffn2_norm", "ffn2_w_gate", "ffn2_w_up", "ffn2_w_down"]
    outs = [loss, grad_x[None]]
    for q in range(4):
        outs += [res[nm][q] for nm in order]
    return tuple(outs)
```

```python
import functools
import math

import numpy as np
import jax
import jax.numpy as jnp
from jax import lax
from jax.experimental import pallas as pl
from jax.experimental.pallas import tpu as pltpu
from jax.experimental.pallas import tpu_sc as plsc

F32 = jnp.float32
BF16 = jnp.bfloat16
SDS = jax.ShapeDtypeStruct

EPS = 1e-6
HEAD_DIM = 64
SWA_HEADS = 8
SWA_KV_HEADS = 2
SWA_GROUP = SWA_HEADS // SWA_KV_HEADS
BLOCK = 128
N_BUCKETS = 32
MAX_DISTANCE = 128
GLA_HEADS = 4
GLA_DK = 32
GLA_DV = 64
GLA_RANK = 16
GLA_TAU = 16.0
GLA_CHUNK = 32
MEM_HEADS = 4
SWA_Q_W = SWA_HEADS * HEAD_DIM
SWA_KV_W = SWA_KV_HEADS * HEAD_DIM
GLA_QK_W = GLA_HEADS * GLA_DK
GLA_V_W = GLA_HEADS * GLA_DV
MEM_Q_W = MEM_HEADS * HEAD_DIM
IN_W = 1808
IN_W_PAD = 1920
COL_SQ, COL_SKV, COL_GQ, COL_GK, COL_GV, COL_GG, COL_MQ, COL_GLR = 0, 512, 768, 896, 1024, 1280, 1536, 1792

ADAM_LR = 0.001
ADAM_B1 = 0.9
ADAM_B2 = 0.999
ADAM_EPS = 1e-08
ADAM_WD = 0.01
ADAM_STEP = 10

N_DEV = 8
VMEM_LIMIT_BYTES = 56 * 1024 * 1024
MESH = pl.DeviceIdType.MESH


def _params(*sem):
    return pltpu.CompilerParams(dimension_semantics=sem or None, vmem_limit_bytes=VMEM_LIMIT_BYTES)


def _dot(a, b, ta, tb, precision=None):
    dims = (((0 if ta else 1,), (1 if tb else 0,)), ((), ()))
    return lax.dot_general(a, b, dims, preferred_element_type=F32, precision=precision)


def _mm_raw(a, b, ta=False, tb=False):
    return _dot(a.astype(BF16), b.astype(BF16), ta, tb)


def _mmf_raw(a, b, ta=False, tb=False):
    return _dot(a, b, ta, tb, lax.Precision.HIGHEST)


def _make_mm(raw):
    @functools.partial(jax.custom_vjp, nondiff_argnums=(2, 3))
    def mm(a, b, ta=False, tb=False):
        return raw(a, b, ta, tb)

    def fwd(a, b, ta, tb):
        return raw(a, b, ta, tb), (a, b)

    def bwd(ta, tb, res, g):
        a, b = res
        da = raw(b, g, tb, True) if ta else raw(g, b, False, not tb)
        db = raw(g, a, True, ta) if tb else raw(a, g, not ta, False)
        return da, db

    mm.defvjp(fwd, bwd)
    return mm


_mm = _make_mm(_mm_raw)
_mmf = _make_mm(_mmf_raw)


def _mm3(a, b, ta=False, tb=False):
    a_hi, b_hi = a.astype(BF16).astype(F32), b.astype(BF16).astype(F32)
    return _mm(a_hi, b_hi, ta, tb) + _mm(a_hi, b - b_hi, ta, tb) + _mm(a - a_hi, b_hi, ta, tb)


def _rms(x, g):
    return x * lax.rsqrt(jnp.mean(x * x, axis=-1, keepdims=True) + EPS) * g


def _silu_mul(g, u):
    return jax.nn.silu(g) * u


def _log_sigmoid(z):
    return jnp.minimum(z, 0.0) - jnp.log(1.0 + jnp.exp(-jnp.abs(z)))


def _matmul(a_list, b, *, ta=False, tb=False, tm, tn, b_blocks=None, res=None, scale=1.0, out_dtype=F32, name):
    if not isinstance(a_list, (list, tuple)):
        a_list = [a_list]
    n_a = len(a_list)
    m = a_list[0].shape[1] if ta else a_list[0].shape[0]
    ks = [a.shape[0] if ta else a.shape[1] for a in a_list]
    n = b.shape[0] if tb else b.shape[1]
    if b_blocks is None:
        assert n_a == 1
        b_blocks = [0]
    tm, tn = min(tm, m), min(tn, n)
    assert m % tm == 0 and n % tn == 0, (m, n, tm, tn)

    def body(*refs):
        a_refs, b_refs = refs[:n_a], refs[n_a:2 * n_a]
        r_ref = refs[2 * n_a] if res is not None else None
        o_ref = refs[-1]
        acc = _mm_raw(a_refs[0][...], b_refs[0][...], ta, tb)
        for k in range(1, n_a):
            acc = acc + _mm_raw(a_refs[k][...], b_refs[k][...], ta, tb)
        if scale != 1.0:
            acc = acc * scale
        if r_ref is not None:
            acc = r_ref[...] + acc
        o_ref[...] = acc.astype(out_dtype)

    in_specs = []
    for k in ks:
        in_specs.append(pl.BlockSpec((k, tm), lambda i, j: (0, i)) if ta else pl.BlockSpec((tm, k), lambda i, j: (i, 0)))
    for k, blk in zip(ks, b_blocks):
        if tb:
            in_specs.append(pl.BlockSpec((tn, k), functools.partial(lambda i, j, blk: (j, blk), blk=blk)))
        else:
            in_specs.append(pl.BlockSpec((k, tn), functools.partial(lambda i, j, blk: (blk, j), blk=blk)))
    args = list(a_list) + [b] * n_a
    if res is not None:
        in_specs.append(pl.BlockSpec((tm, tn), lambda i, j: (i, j)))
        args.append(res)
    return pl.pallas_call(
        body, name=name, grid=(m // tm, n // tn), in_specs=in_specs,
        out_specs=pl.BlockSpec((tm, tn), lambda i, j: (i, j)), out_shape=SDS((m, n), out_dtype),
        compiler_params=_params("parallel", "parallel"),
    )(*args)


def _win_pieces(w):
    glr_lo, glr_hi = COL_MQ, COL_MQ + GLA_RANK
    out = []
    for j in range(N_DEV):
        for lo, hi, shift in ((0, glr_lo, 0), (glr_lo, glr_hi, COL_GLR - glr_lo), (glr_hi, IN_W, COL_MQ - glr_hi)):
            s, e = max(j * w, lo), min((j + 1) * w, hi)
            if s < e:
                out.append((j, s - j * w, e - j * w, s + shift))
    return out


def _pack_win(win_all, *, tr, name):
    _, d, w = win_all.shape

    def body(i_ref, o_ref):
        for j, a, b, dst in _win_pieces(w):
            o_ref[:, dst:dst + b - a] = i_ref[j][:, a:b]
        o_ref[:, IN_W:] = jnp.zeros((tr, IN_W_PAD - IN_W), o_ref.dtype)

    return pl.pallas_call(
        body, name=name, grid=(d // tr,), in_specs=[pl.BlockSpec((N_DEV, tr, w), lambda i: (0, i, 0))],
        out_specs=pl.BlockSpec((tr, IN_W_PAD), lambda i: (i, 0)), out_shape=SDS((d, IN_W_PAD), win_all.dtype),
        compiler_params=_params("parallel"),
    )(win_all)


def _unpack_win(dwin_parts, *, tr, name):
    d = dwin_parts[0].shape[0]
    w = IN_W // N_DEV
    starts = [sum(p.shape[1] for p in dwin_parts[:k]) for k in range(len(dwin_parts) + 1)]

    def body(*refs):
        o_ref = refs[-1]
        for j, a, b, src in _win_pieces(w):
            for k, i_ref in enumerate(refs[:-1]):
                lo, hi = max(src, starts[k]), min(src + b - a, starts[k + 1])
                if lo < hi:
                    o_ref[j % 2, j // 2, :, a + lo - src:a + hi - src] = i_ref[:, lo - starts[k]:hi - starts[k]]

    return pl.pallas_call(
        body, name=name, grid=(d // tr,), in_specs=[pl.BlockSpec((tr, p.shape[1]), lambda i: (i, 0)) for p in dwin_parts],
        out_specs=pl.BlockSpec((2, 4, tr, w), lambda i: (0, 0, i, 0)), out_shape=SDS((2, 4, d, w), dwin_parts[0].dtype),
        compiler_params=_params("parallel"),
    )(*dwin_parts)


def _dw_parts(h, parts, *, tk, name):
    s, d = h.shape
    n_p = len(parts)

    def body(*refs):
        h_ref, p_refs, o_refs, acc_refs = refs[0], refs[1:1 + n_p], refs[1 + n_p:1 + 2 * n_p], refs[1 + 2 * n_p:]
        i = pl.program_id(0)
        h_t = h_ref[...].T

        @pl.when(i == 0)
        def _():
            for acc_ref in acc_refs:
                acc_ref[...] = jnp.zeros_like(acc_ref)

        for p_ref, acc_ref in zip(p_refs, acc_refs):
            acc_ref[...] += _mm_raw(h_t, p_ref[...], False, False)

        @pl.when(i == pl.num_programs(0) - 1)
        def _():
            for o_ref, acc_ref in zip(o_refs, acc_refs):
                o_ref[...] = acc_ref[...].astype(BF16)

    return pl.pallas_call(
        body, name=name, grid=(s // tk,),
        in_specs=[pl.BlockSpec((tk, d), lambda i: (i, 0))] + [pl.BlockSpec((tk, p.shape[1]), lambda i: (i, 0)) for p in parts],
        out_specs=[pl.BlockSpec((d, p.shape[1]), lambda i: (0, 0)) for p in parts],
        out_shape=[SDS((d, p.shape[1]), BF16) for p in parts],
        scratch_shapes=[pltpu.VMEM((d, p.shape[1]), F32) for p in parts],
        compiler_params=_params("arbitrary"),
    )(h, *parts)


def _rms_fwd(x, g, *, tm, name):
    s, d = x.shape

    def body(x_ref, g_ref, h_ref):
        h_ref[...] = _rms(x_ref[...], g_ref[...]).astype(BF16)

    return pl.pallas_call(
        body, name=name, grid=(s // tm,),
        in_specs=[pl.BlockSpec((tm, d), lambda i: (i, 0)), pl.BlockSpec((1, d), lambda i: (0, 0))],
        out_specs=pl.BlockSpec((tm, d), lambda i: (i, 0)), out_shape=SDS((s, d), BF16),
        compiler_params=_params("parallel"),
    )(x, g)


def _rms_bwd(x, g, dh, dres, *, tm, name):
    s, d = x.shape
    want_dx = dres is not None
    product = isinstance(dh, tuple)
    if product:
        a_parts = list(dh[0]) if isinstance(dh[0], (list, tuple)) else [dh[0]]
        starts = [sum(a.shape[1] for a in a_parts[:k]) for k in range(len(a_parts) + 1)]

    def body(*refs):
        n_dh = len(a_parts) + 1 if product else 1
        x_ref, g_ref = refs[:2]
        dh_refs, rest = refs[2:2 + n_dh], refs[2 + n_dh:]
        if want_dx:
            dres_ref, dx_ref, dxb_ref, dg_ref = rest
        else:
            dg_ref, = rest
        if product:
            b_ref = dh_refs[-1]
            dh_tile = _mm_raw(dh_refs[0][...], b_ref[:, starts[0]:starts[1]], False, True)
            for k in range(1, len(a_parts)):
                dh_tile = dh_tile + _mm_raw(dh_refs[k][...], b_ref[:, starts[k]:starts[k + 1]], False, True)
        else:
            dh_tile = dh_refs[0][...]
        _, vjp = jax.vjp(_rms, x_ref[...], g_ref[...])
        dx, dg = vjp(dh_tile)
        if want_dx:
            dx = dres_ref[...] + dx
            dx_ref[...] = dx
            dxb_ref[...] = dx.astype(BF16)

        @pl.when(pl.program_id(0) == 0)
        def _():
            dg_ref[...] = jnp.zeros_like(dg_ref)

        dg_ref[...] += dg

    row = pl.BlockSpec((tm, d), lambda i: (i, 0))
    vec = pl.BlockSpec((1, d), lambda i: (0, 0))
    if product:
        dh_specs = [pl.BlockSpec((tm, a.shape[1]), lambda i: (i, 0)) for a in a_parts]
        dh_specs.append(pl.BlockSpec((d, starts[-1]), lambda i: (0, 0)))
        dh_args = a_parts + [dh[1]]
    else:
        dh_specs, dh_args = [row], [dh]
    if want_dx:
        return pl.pallas_call(
            body, name=name, grid=(s // tm,), in_specs=[row, vec] + dh_specs + [row], out_specs=[row, row, vec],
            out_shape=[SDS((s, d), F32), SDS((s, d), BF16), SDS((1, d), F32)], compiler_params=_params("arbitrary"),
        )(x, g, *dh_args, dres)
    return None, None, pl.pallas_call(
        body, name=name, grid=(s // tm,), in_specs=[row, vec] + dh_specs, out_specs=vec,
        out_shape=SDS((1, d), F32), compiler_params=_params("arbitrary"),
    )(x, g, *dh_args)


FFN_TN = 256
FFN_TN_FWD = 512
FFN1_FIRST = 192


def _ffn_fwd(x, gain, w3, tag, *, tm=1024, next_gain=None, target=None, start=None, partial=False):
    s, d = x.shape
    f = w3.shape[1]
    tn = FFN_TN_FWD if f % FFN_TN_FWD == 0 else FFN_TN
    nj = f // tn
    tm = min(tm, s)
    n_extra = (next_gain is not None) + (target is not None) + 2 * (start is not None)

    def body(*refs):
        x_ref, gain_ref, wg_ref, wu_ref, wd_ref = refs[:5]
        extra, outs = refs[5:5 + n_extra], refs[5 + n_extra:-1]
        acc_s = refs[-1]
        g_ref, u_ref = outs[-2:]
        h_ref = extra[-2] if start is not None else outs[-3]
        i, j = pl.program_id(0), pl.program_id(1)

        @pl.when(j == 0)
        def _():
            if start is None:
                h_ref[...] = _rms(x_ref[...], gain_ref[...]).astype(BF16)
                acc_s[...] = jnp.zeros_like(acc_s)
            else:
                acc_s[...] = extra[-1][...]

        hv = h_ref[...]
        g = _mm_raw(hv, wg_ref[...], False, True)
        u = _mm_raw(hv, wu_ref[...], False, True)
        g_ref[...] = g.astype(BF16)
        u_ref[...] = u.astype(BF16)
        acc_s[...] += _mm_raw(_silu_mul(g, u), wd_ref[...])

        @pl.when(j == nj - 1)
        def _():
            y = acc_s[...] if partial else x_ref[...] + 0.5 * acc_s[...]
            if target is None:
                outs[0][...] = y
                if next_gain is not None:
                    outs[1][...] = _rms(y, extra[0][...]).astype(BF16)
            else:
                dy_ref, dyb_ref, loss_ref = outs[:3]
                diff = y - extra[0][...]
                dy_ref[...] = diff * (1.0 / d)
                dyb_ref[...] = (diff * (1.0 / d)).astype(BF16)
                part = 0.5 * jnp.sum(jnp.mean(diff * diff, axis=-1, keepdims=True), axis=0, keepdims=True)

                @pl.when(i == 0)
                def _():
                    loss_ref[...] = part

                @pl.when(i > 0)
                def _():
                    loss_ref[...] += part

    row = pl.BlockSpec((tm, d), lambda i, j: (i, 0))
    vec = pl.BlockSpec((1, d), lambda i, j: (0, 0))
    tile = pl.BlockSpec((tm, tn), lambda i, j: (i, j))
    in_specs = [row, vec] + [pl.BlockSpec((None, tn, d), functools.partial(lambda i, j, k: (k, j, 0), k=k)) for k in range(3)]
    args = [x, gain, w3, w3, w3]
    if target is None:
        out_specs, out_shape = [row], [SDS((s, d), F32)]
        if next_gain is not None:
            in_specs.append(vec)
            args.append(next_gain)
            out_specs.append(row)
            out_shape.append(SDS((s, d), BF16))
    else:
        in_specs.append(row)
        args.append(target)
        out_specs = [row, row, pl.BlockSpec((1, 1), lambda i, j: (0, 0))]
        out_shape = [SDS((s, d), F32), SDS((s, d), BF16), SDS((1, 1), F32)]
    if start is None:
        out_specs.append(row)
        out_shape.append(SDS((s, d), BF16))
    else:
        in_specs += [row, row]
        args += list(start)
    *head, g, u = pl.pallas_call(
        body, name=f"{tag}_fwd", grid=(s // tm, nj), in_specs=in_specs,
        out_specs=out_specs + [tile, tile],
        out_shape=out_shape + [SDS((s, f), BF16), SDS((s, f), BF16)],
        scratch_shapes=[pltpu.VMEM((tm, d), F32)],
        compiler_params=_params("arbitrary", "arbitrary"),
    )(*args)
    if start is None:
        *head, h = head
    else:
        h = start[0]
    return head, (h, g, u)


def _ffn_bwd_part(dyb, w3, saved, first, count, dh_init, *, name):
    h, g, u = saved
    s, d = h.shape
    tn = FFN_TN

    def body(*refs):
        if dh_init is None:
            dy_ref, h_ref, wd_ref, wg_ref, wu_ref, g_ref, u_ref, dh_ref, dw3_ref, dg_s, du_s, a_s = refs
        else:
            dy_ref, h_ref, wd_ref, wg_ref, wu_ref, g_ref, u_ref, dh0_ref, dh_ref, dw3_ref, dg_s, du_s, a_s = refs
        j = pl.program_id(0)

        @pl.when(j == 0)
        def _():
            dh_ref[...] = jnp.zeros_like(dh_ref) if dh_init is None else dh0_ref[...]
            for ref in (dg_s, du_s, a_s):
                ref[...] = jnp.zeros_like(ref)

        now, before = j % 2, 1 - j % 2
        dyv = dy_ref[...]
        hv = h_ref[...]
        dg, du, a = dg_s[before], du_s[before], a_s[before]
        dh_ref[...] += _mm_raw(dg, wg_ref[...]) + _mm_raw(du, wu_ref[...])
        dw3_ref[0] = _mm_raw(dg, hv, True, False).astype(BF16)
        dw3_ref[1] = _mm_raw(du, hv, True, False).astype(BF16)
        dw3_ref[2] = (_mm_raw(a, dyv, True, False) * 0.5).astype(BF16)

        da = _mm_raw(dyv, wd_ref[...], False, True) * 0.5
        a, vjp = jax.vjp(_silu_mul, g_ref[...].astype(F32), u_ref[...].astype(F32))
        dg, du = vjp(da)
        dg_s[now] = dg.astype(BF16)
        du_s[now] = du.astype(BF16)
        a_s[now] = a.astype(BF16)

    this = lambda j: first + jnp.minimum(j, count - 1)
    last = lambda j: first + jnp.maximum(j - 1, 0)
    full = pl.BlockSpec((s, d), lambda j: (0, 0))
    once = pl.BlockSpec((s, d), lambda j: (0, 0), pipeline_mode=pl.Buffered(1))
    tile = pl.BlockSpec((s, tn), lambda j: (0, this(j)))
    in_specs = [once, once, pl.BlockSpec((None, tn, d), lambda j: (2, this(j), 0)),
                pl.BlockSpec((None, tn, d), lambda j: (0, last(j), 0)), pl.BlockSpec((None, tn, d), lambda j: (1, last(j), 0)),
                tile, tile]
    args = [dyb, h, w3, w3, w3, g, u]
    if dh_init is not None:
        in_specs.append(once)
        args.append(dh_init)
    return pl.pallas_call(
        body, name=name, grid=(count + 1,), in_specs=in_specs,
        out_specs=[full, pl.BlockSpec((3, tn, d), lambda j: (0, jnp.maximum(j - 1, 0), 0))],
        out_shape=[SDS((s, d), F32), SDS((3, count * tn, d), BF16)],
        scratch_shapes=[pltpu.VMEM((2, s, tn), BF16)] * 3,
        compiler_params=_params("arbitrary"),
    )(*args)


def _bucket_table():
    qi = np.arange(BLOCK)[:, None]
    kj = np.arange(2 * BLOCK)[None, :]
    dist = np.maximum(qi + BLOCK - kj, 0)
    max_exact = N_BUCKETS // 2
    d = np.maximum(dist, 1).astype(np.float32)
    large = max_exact + (np.log(d / np.float32(max_exact)) / np.float32(math.log(MAX_DISTANCE / max_exact))
                         * np.float32(N_BUCKETS - max_exact)).astype(np.int32)
    large = np.minimum(large, N_BUCKETS - 1)
    band = np.where(dist < max_exact, dist, large).astype(np.int32)
    return np.where(np.tril(np.ones((BLOCK, BLOCK), bool)), band[:, BLOCK:], band[:, :BLOCK])


SWA_STACK = SWA_GROUP * BLOCK


def _swa_masks(n):
    qi = lax.broadcasted_iota(jnp.int32, (SWA_STACK, BLOCK), 0) % BLOCK
    kj = lax.broadcasted_iota(jnp.int32, (SWA_STACK, BLOCK), 1)
    own = kj <= qi
    return own, own | (n > 0)


def _swa_group(q, kp, kc, vp, vc, qg, kg, sink, bias, own, valid):
    qn = _rms(q, qg)
    s = jnp.where(own, _mm(qn, _rms(kc, kg), False, True), _mm(qn, _rms(kp, kg), False, True))
    s = s * (HEAD_DIM ** -0.5) + bias
    s = jnp.where(valid, s, -jnp.inf)
    m = lax.stop_gradient(jnp.maximum(jnp.max(s, axis=-1, keepdims=True), sink))
    p = jnp.exp(s - m)
    p = p / (jnp.sum(p, axis=-1, keepdims=True) + jnp.exp(sink - m))
    return _mm(jnp.where(own, p, 0.0), vc) + _mm(jnp.where(own, 0.0, p), vp)


def _swa_bias_table(rb_ref, bucket, bias_s):
    for h in range(SWA_HEADS):
        acc = jnp.zeros((BLOCK, BLOCK), F32)
        for b in range(N_BUCKETS):
            acc = jnp.where(bucket == b, rb_ref[b, h], acc)
        bias_s[h // SWA_GROUP, (h % SWA_GROUP) * BLOCK:(h % SWA_GROUP + 1) * BLOCK, :] = acc


def _swa_stack(ref, g):
    return jnp.concatenate([ref[:, (g * SWA_GROUP + hh) * HEAD_DIM:(g * SWA_GROUP + hh + 1) * HEAD_DIM]
                            for hh in range(SWA_GROUP)], axis=0)


def _swa_unstack(ref, g, stacked):
    for hh in range(SWA_GROUP):
        h = g * SWA_GROUP + hh
        ref[:, h * HEAD_DIM:(h + 1) * HEAD_DIM] = stacked[hh * BLOCK:(hh + 1) * BLOCK]


def _swa_sink_column(sink_ref, g):
    head = lax.broadcasted_iota(jnp.int32, (SWA_STACK, 1), 0) // BLOCK
    col = jnp.zeros((SWA_STACK, 1), F32)
    for hh in range(SWA_GROUP):
        col = jnp.where(head == hh, sink_ref[g * SWA_GROUP + hh], col)
    return col


def _swa_band(kvp_ref, kvc_ref, g):
    k = slice(g * HEAD_DIM, (g + 1) * HEAD_DIM)
    v = slice(SWA_KV_W + g * HEAD_DIM, SWA_KV_W + (g + 1) * HEAD_DIM)
    return kvp_ref[:, k], kvc_ref[:, k], kvp_ref[:, v], kvc_ref[:, v]


def _swa_specs(order):
    kvc = COL_SKV // (2 * SWA_KV_W)
    return [
        pl.BlockSpec((BLOCK, SWA_Q_W), lambda t: (order(t), 0)),
        pl.BlockSpec((BLOCK, 2 * SWA_KV_W), lambda t: (jnp.maximum(order(t) - 1, 0), kvc)),
        pl.BlockSpec((BLOCK, 2 * SWA_KV_W), lambda t: (order(t), kvc)),
        pl.BlockSpec((1, HEAD_DIM), lambda t: (0, 0)),
        pl.BlockSpec((1, HEAD_DIM), lambda t: (0, 0)),
        pl.BlockSpec(memory_space=pltpu.SMEM),
        pl.BlockSpec(memory_space=pltpu.SMEM),
        pl.BlockSpec((BLOCK, BLOCK), lambda t: (0, 0)),
    ]


def _swa_fwd(p, qg, kg, sinks, rel_bias, *, name):
    s = p.shape[0]
    nb = s // BLOCK

    def body(q_ref, kvp_ref, kvc_ref, qg_ref, kg_ref, sink_ref, rb_ref, bucket_ref, y_ref, bias_s):
        n = pl.program_id(0)

        @pl.when(n == 0)
        def _():
            _swa_bias_table(rb_ref, bucket_ref[...], bias_s)

        own, valid = _swa_masks(n)
        for g in range(SWA_KV_HEADS):
            out = _swa_group(_swa_stack(q_ref, g), *_swa_band(kvp_ref, kvc_ref, g), qg_ref[...], kg_ref[...],
                             _swa_sink_column(sink_ref, g), bias_s[g], own, valid)
            _swa_unstack(y_ref, g, out)

    return pl.pallas_call(
        body, name=name, grid=(nb,), in_specs=_swa_specs(lambda t: t),
        out_specs=pl.BlockSpec((BLOCK, SWA_Q_W), lambda t: (t, 0)), out_shape=SDS((s, SWA_Q_W), F32),
        scratch_shapes=[pltpu.VMEM((SWA_KV_HEADS, SWA_STACK, BLOCK), F32)],
        compiler_params=_params("arbitrary"),
    )(p, p, p, qg, kg, sinks, rel_bias, jnp.asarray(_bucket_table()))


def _swa_bwd(p, qg, kg, sinks, rel_bias, dy_all, *, name):
    s = p.shape[0]
    nb = s // BLOCK

    def body(q_ref, kvp_ref, kvc_ref, qg_ref, kg_ref, sink_ref, rb_ref, bucket_ref, dy_ref,
             dq_ref, dkv_ref, dqg_ref, dkg_ref, dsink_ref, drb_ref, bias_s, dbias_s, carry_s):
        t = pl.program_id(0)
        n = nb - 1 - t

        @pl.when(t == 0)
        def _():
            _swa_bias_table(rb_ref, bucket_ref[...], bias_s)
            dbias_s[...] = jnp.zeros_like(dbias_s)
            carry_s[...] = jnp.zeros_like(carry_s)
            dqg_ref[...] = jnp.zeros_like(dqg_ref)
            dkg_ref[...] = jnp.zeros_like(dkg_ref)
            dsink_ref[...] = jnp.zeros_like(dsink_ref)
            drb_ref[...] = jnp.zeros_like(drb_ref)

        own, valid = _swa_masks(n)
        lane = lax.broadcasted_iota(jnp.int32, (1, BLOCK), 1)
        dqg = jnp.zeros((1, HEAD_DIM), F32)
        dkg = jnp.zeros((1, HEAD_DIM), F32)
        dsink_vec = jnp.zeros((1, BLOCK), F32)
        for g in range(SWA_KV_HEADS):
            _, vjp = jax.vjp(functools.partial(_swa_group, own=own, valid=valid), _swa_stack(q_ref, g),
                             *_swa_band(kvp_ref, kvc_ref, g), qg_ref[...], kg_ref[...], _swa_sink_column(sink_ref, g),
                             bias_s[g])
            dq, dkp, dkc, dvp, dvc, dqg_g, dkg_g, dsink_col, dbias = vjp(_swa_stack(dy_ref, g))
            _swa_unstack(dq_ref, g, dq)
            dqg += dqg_g
            dkg += dkg_g
            dbias_s[g] += dbias
            for hh in range(SWA_GROUP):
                dsink_h = jnp.sum(dsink_col[hh * BLOCK:(hh + 1) * BLOCK], axis=0, keepdims=True)
                dsink_vec += jnp.where(lane == g * SWA_GROUP + hh, dsink_h, 0.0)
            lo = g * HEAD_DIM
            dkv_ref[:, lo:lo + HEAD_DIM] = dkc + carry_s[g]
            carry_s[g] = dkp
            lo += SWA_KV_W
            dkv_ref[:, lo:lo + HEAD_DIM] = dvc + carry_s[SWA_KV_HEADS + g]
            carry_s[SWA_KV_HEADS + g] = dvp
        dqg_ref[...] += dqg
        dkg_ref[...] += dkg
        dsink_ref[...] += dsink_vec

        @pl.when(t == nb - 1)
        def _():
            bucket = bucket_ref[...]
            row = lax.broadcasted_iota(jnp.int32, (N_BUCKETS, BLOCK), 0)
            col = lax.broadcasted_iota(jnp.int32, (N_BUCKETS, BLOCK), 1)
            acc = jnp.zeros((N_BUCKETS, BLOCK), F32)
            for h in range(SWA_HEADS):
                dbias = dbias_s[h // SWA_GROUP, (h % SWA_GROUP) * BLOCK:(h % SWA_GROUP + 1) * BLOCK, :]
                for b in range(N_BUCKETS):
                    part = jnp.sum(jnp.where(bucket == b, dbias, 0.0), axis=1, keepdims=True)
                    val = jnp.sum(part, axis=0, keepdims=True)
                    acc = acc + jnp.where((row == b) & (col == h), val, 0.0)
            drb_ref[...] = acc

    order = lambda t: nb - 1 - t
    vec = pl.BlockSpec((1, HEAD_DIM), lambda t: (0, 0))
    return pl.pallas_call(
        body, name=name, grid=(nb,),
        in_specs=_swa_specs(order) + [pl.BlockSpec((BLOCK, SWA_Q_W), lambda t: (order(t), 0))],
        out_specs=[pl.BlockSpec((BLOCK, SWA_Q_W), lambda t: (order(t), 0)),
                   pl.BlockSpec((BLOCK, 2 * SWA_KV_W), lambda t: (order(t), 0)),
                   vec, vec, pl.BlockSpec((1, BLOCK), lambda t: (0, 0)),
                   pl.BlockSpec((N_BUCKETS, BLOCK), lambda t: (0, 0))],
        out_shape=[SDS((s, SWA_Q_W), F32), SDS((s, 2 * SWA_KV_W), F32), SDS((1, HEAD_DIM), F32),
                   SDS((1, HEAD_DIM), F32), SDS((1, BLOCK), F32), SDS((N_BUCKETS, BLOCK), F32)],
        scratch_shapes=[pltpu.VMEM((SWA_KV_HEADS, SWA_STACK, BLOCK), F32),
                        pltpu.VMEM((SWA_KV_HEADS, SWA_STACK, BLOCK), F32),
                        pltpu.VMEM((2 * SWA_KV_HEADS, BLOCK, HEAD_DIM), F32)],
        compiler_params=_params("arbitrary"),
    )(p, p, p, qg, kg, sinks, rel_bias, jnp.asarray(_bucket_table()), dy_all)


def _mem_head(q, k, v, qg, kg):
    qn = _rms(q, qg)
    kn = _rms(k, kg)
    s = _mm(qn, kn, False, True) * (HEAD_DIM ** -0.5)
    m = lax.stop_gradient(jnp.max(s, axis=-1, keepdims=True))
    e = jnp.exp(s - m)
    return _mm(e / jnp.sum(e, axis=-1, keepdims=True), v)


def _mem_fwd(p, kv, qg, kg, *, tq, name):
    s = p.shape[0]
    m = kv.shape[0]

    def body(q_ref, kv_ref, qg_ref, kg_ref, y_ref):
        for h in range(MEM_HEADS):
            cols = slice(h * HEAD_DIM, (h + 1) * HEAD_DIM)
            vcols = slice(MEM_Q_W + h * HEAD_DIM, MEM_Q_W + (h + 1) * HEAD_DIM)
            y_ref[:, cols] = _mem_head(q_ref[:, cols], kv_ref[:, cols], kv_ref[:, vcols], qg_ref[...], kg_ref[...])

    vec = pl.BlockSpec((1, HEAD_DIM), lambda t: (0, 0))
    return pl.pallas_call(
        body, name=name, grid=(s // tq,),
        in_specs=[pl.BlockSpec((tq, MEM_Q_W), lambda t: (t, COL_MQ // MEM_Q_W)),
                  pl.BlockSpec((m, 2 * MEM_Q_W), lambda t: (0, 0)), vec, vec],
        out_specs=pl.BlockSpec((tq, MEM_Q_W), lambda t: (t, 0)), out_shape=SDS((s, MEM_Q_W), F32),
        compiler_params=_params("parallel"),
    )(p, kv, qg, kg)


def _mem_bwd(p, kv, qg, kg, dy_all, *, tq, name):
    s = p.shape[0]
    m = kv.shape[0]

    def body(q_ref, kv_ref, qg_ref, kg_ref, dy_ref, dq_ref, dkv_ref, dqg_ref, dkg_ref):
        @pl.when(pl.program_id(0) == 0)
        def _():
            dkv_ref[...] = jnp.zeros_like(dkv_ref)
            dqg_ref[...] = jnp.zeros_like(dqg_ref)
            dkg_ref[...] = jnp.zeros_like(dkg_ref)

        dqg = jnp.zeros((1, HEAD_DIM), F32)
        dkg = jnp.zeros((1, HEAD_DIM), F32)
        for h in range(MEM_HEADS):
            cols = slice(h * HEAD_DIM, (h + 1) * HEAD_DIM)
            vcols = slice(MEM_Q_W + h * HEAD_DIM, MEM_Q_W + (h + 1) * HEAD_DIM)
            _, vjp = jax.vjp(_mem_head, q_ref[:, cols], kv_ref[:, cols], kv_ref[:, vcols], qg_ref[...], kg_ref[...])
            dq, dk, dv, dqg_h, dkg_h = vjp(dy_ref[:, cols])
            dq_ref[:, cols] = dq
            dkv_ref[:, cols] += dk
            dkv_ref[:, vcols] += dv
            dqg += dqg_h
            dkg += dkg_h
        dqg_ref[...] += dqg
        dkg_ref[...] += dkg

    vec = pl.BlockSpec((1, HEAD_DIM), lambda t: (0, 0))
    full = pl.BlockSpec((m, 2 * MEM_Q_W), lambda t: (0, 0))
    dy_col = (SWA_Q_W + GLA_V_W) // MEM_Q_W
    return pl.pallas_call(
        body, name=name, grid=(s // tq,),
        in_specs=[pl.BlockSpec((tq, MEM_Q_W), lambda t: (t, COL_MQ // MEM_Q_W)), full, vec, vec,
                  pl.BlockSpec((tq, MEM_Q_W), lambda t: (t, dy_col))],
        out_specs=[pl.BlockSpec((tq, MEM_Q_W), lambda t: (t, 0)), full, vec, vec],
        out_shape=[SDS((s, MEM_Q_W), F32), SDS((m, 2 * MEM_Q_W), F32), SDS((1, HEAD_DIM), F32), SDS((1, HEAD_DIM), F32)],
        compiler_params=_params("arbitrary"),
    )(p, kv, qg, kg, dy_all)


GLA_ROWS = 256


GLA_GROUP = 4


def _gla_consts():
    c, h, r = GLA_CHUNK, GLA_HEADS, GLA_GROUP * GLA_CHUNK
    i2 = lax.broadcasted_iota(jnp.int32, (c, c), 0)
    j2 = lax.broadcasted_iota(jnp.int32, (c, c), 1)
    slab_q = lax.broadcasted_iota(jnp.int32, (h, r, GLA_QK_W), 0)
    lane_q = lax.broadcasted_iota(jnp.int32, (h, r, GLA_QK_W), 2)
    row_a = lax.broadcasted_iota(jnp.int32, (h * r, r), 0) % r
    col_a = lax.broadcasted_iota(jnp.int32, (h * r, r), 1)
    slab_o = lax.broadcasted_iota(jnp.int32, (h, r, GLA_V_W), 0)
    lane_o = lax.broadcasted_iota(jnp.int32, (h, r, GLA_V_W), 2)
    row_s = lax.broadcasted_iota(jnp.int32, (GLA_V_W, GLA_QK_W), 0)
    col_s = lax.broadcasted_iota(jnp.int32, (GLA_V_W, GLA_QK_W), 1)
    return dict(
        ltri=(j2 <= i2).astype(F32),
        m_q=(slab_q == lane_q // GLA_DK).astype(F32),
        causal=(col_a <= row_a) & (col_a // c == row_a // c),
        m_o=(slab_o == lane_o // GLA_DV).astype(F32),
        m_s=(row_s // GLA_DV == col_s // GLA_DK).astype(F32),
    )


def _gla_step(q, k, v, z, bg, st, c):
    h = GLA_HEADS
    kt, ka, qt, qe, decay = [], [], [], [], []
    for qc, kc, zc in zip(q, k, z):
        la = _log_sigmoid(zc + bg) * (1.0 / GLA_TAU)
        b = _mmf(c["ltri"], la)
        bl = jnp.sum(la, axis=0, keepdims=True)
        qs = qc * (GLA_DK ** -0.5)
        kt.append(kc * jnp.exp(bl - b))
        ka.append(kc * jnp.exp(0.5 * bl - b))
        qt.append(qs * jnp.exp(b - 0.5 * bl))
        qe.append(qs * jnp.exp(b))
        decay.append(jnp.exp(bl))
    o_intra = []
    rows = GLA_GROUP * GLA_CHUNK
    for lo in range(0, len(q), GLA_GROUP):
        qt_all, kt_all, v_all = (jnp.concatenate(parts[lo:lo + GLA_GROUP], axis=0) for parts in (qt, ka, v))
        q_stack = (jnp.broadcast_to(qt_all[None], (h, rows, GLA_QK_W)) * c["m_q"]).reshape(h * rows, GLA_QK_W)
        a = jnp.where(c["causal"], _mm3(q_stack, kt_all, False, True), 0.0)
        o_stack = _mm(a, v_all)
        o_intra.append(jnp.sum(o_stack.reshape(h, rows, GLA_V_W) * c["m_o"], axis=0))
    o_intra = jnp.concatenate(o_intra, axis=0)
    o_inter = []
    for qec, ktc, vc, dc in zip(qe, kt, v, decay):
        o_inter.append(_mm(qec, st, False, True))
        st = st * dc + _mm(vc, ktc, True, False) * c["m_s"]
    return o_intra + jnp.concatenate(o_inter, axis=0), st


def _gla_post(o, gg, gain, g64):
    ms = _mmf(o * o, g64) * (1.0 / GLA_DV)
    return o * lax.rsqrt(ms + EPS) * gain * jax.nn.silu(gg)


def _gla_g64():
    r = lax.broadcasted_iota(jnp.int32, (GLA_V_W, GLA_V_W), 0)
    c = lax.broadcasted_iota(jnp.int32, (GLA_V_W, GLA_V_W), 1)
    return (r // GLA_DV == c // GLA_DV).astype(F32)


def _gla_in_specs(order):
    r = GLA_ROWS
    return [
        pl.BlockSpec((r, GLA_QK_W), lambda t: (order(t), COL_GQ // GLA_QK_W)),
        pl.BlockSpec((r, GLA_QK_W), lambda t: (order(t), COL_GK // GLA_QK_W)),
        pl.BlockSpec((r, GLA_V_W), lambda t: (order(t), COL_GV // GLA_V_W)),
        pl.BlockSpec((r, GLA_V_W), lambda t: (order(t), COL_GG // GLA_V_W)),
        pl.BlockSpec((r, GLA_QK_W), lambda t: (order(t), 0)),
        pl.BlockSpec((1, GLA_QK_W), lambda t: (0, 0)),
        pl.BlockSpec((1, GLA_V_W), lambda t: (0, 0)),
    ]


def _gla_pieces(q_ref, k_ref, v_ref, z_ref, cps):
    chunk = lambda ref: [ref[ci * GLA_CHUNK:(ci + 1) * GLA_CHUNK, :] for ci in range(cps)]
    return chunk(q_ref), chunk(k_ref), chunk(v_ref), chunk(z_ref)


def _gla_fwd(p, z, bg, gain, *, name):
    s = p.shape[0]
    r = GLA_ROWS
    cps = r // GLA_CHUNK

    def body(q_ref, k_ref, v_ref, gg_ref, z_ref, bg_ref, gain_ref, y_ref, oraw_ref, stsave_ref, st_s):
        @pl.when(pl.program_id(0) == 0)
        def _():
            st_s[...] = jnp.zeros_like(st_s)

        st = st_s[...]
        stsave_ref[0] = st
        o, st = _gla_step(*_gla_pieces(q_ref, k_ref, v_ref, z_ref, cps), bg_ref[...], st, _gla_consts())
        oraw_ref[...] = o
        st_s[...] = st
        y_ref[...] = _gla_post(o, gg_ref[...], gain_ref[...], _gla_g64())

    rowv = pl.BlockSpec((r, GLA_V_W), lambda t: (t, 0))
    return pl.pallas_call(
        body, name=name, grid=(s // r,), in_specs=_gla_in_specs(lambda t: t),
        out_specs=[rowv, rowv, pl.BlockSpec((1, GLA_V_W, GLA_QK_W), lambda t: (t, 0, 0))],
        out_shape=[SDS((s, GLA_V_W), F32), SDS((s, GLA_V_W), F32), SDS((s // r, GLA_V_W, GLA_QK_W), F32)],
        scratch_shapes=[pltpu.VMEM((GLA_V_W, GLA_QK_W), F32)],
        compiler_params=_params("arbitrary"),
    )(p, p, p, p, z, bg, gain)


def _gla_bwd(p, z, bg, gain, oraw, stsave, dy_all, *, name):
    s = p.shape[0]
    r = GLA_ROWS
    cps = r // GLA_CHUNK
    nsteps = s // r
    w_qkvg = 2 * GLA_QK_W + 2 * GLA_V_W

    def body(q_ref, k_ref, v_ref, gg_ref, z_ref, bg_ref, gain_ref, oraw_ref, stsave_ref, dy_ref,
             dqkvg_ref, dz_ref, dbg_ref, dgain_ref, dst_s):
        @pl.when(pl.program_id(0) == 0)
        def _():
            dst_s[...] = jnp.zeros_like(dst_s)
            dbg_ref[...] = jnp.zeros_like(dbg_ref)
            dgain_ref[...] = jnp.zeros_like(dgain_ref)

        _, vjp = jax.vjp(functools.partial(_gla_post, g64=_gla_g64()), oraw_ref[...], gg_ref[...], gain_ref[...])
        do, dgg, dgain = vjp(dy_ref[...])
        dqkvg_ref[:, 2 * GLA_QK_W + GLA_V_W:] = dgg
        dgain_ref[...] += dgain
        _, vjp = jax.vjp(functools.partial(_gla_step, c=_gla_consts()), *_gla_pieces(q_ref, k_ref, v_ref, z_ref, cps),
                         bg_ref[...], stsave_ref[0])
        dq, dk, dv, dz, dbg, dst = vjp((do, dst_s[...]))
        for ci in range(cps):
            rows = slice(ci * GLA_CHUNK, (ci + 1) * GLA_CHUNK)
            dqkvg_ref[rows, 0:GLA_QK_W] = dq[ci]
            dqkvg_ref[rows, GLA_QK_W:2 * GLA_QK_W] = dk[ci]
            dqkvg_ref[rows, 2 * GLA_QK_W:2 * GLA_QK_W + GLA_V_W] = dv[ci]
            dz_ref[rows, :] = dz[ci]
        dst_s[...] = dst
        dbg_ref[...] += dbg

    order = lambda t: nsteps - 1 - t
    rowv = pl.BlockSpec((r, GLA_V_W), lambda t: (order(t), 0))
    return pl.pallas_call(
        body, name=name, grid=(nsteps,),
        in_specs=_gla_in_specs(order) + [
            rowv, pl.BlockSpec((1, GLA_V_W, GLA_QK_W), lambda t: (order(t), 0, 0)),
            pl.BlockSpec((r, GLA_V_W), lambda t: (order(t), SWA_Q_W // GLA_V_W))],
        out_specs=[pl.BlockSpec((r, w_qkvg), lambda t: (order(t), 0)), pl.BlockSpec((r, GLA_QK_W), lambda t: (order(t), 0)),
                   pl.BlockSpec((1, GLA_QK_W), lambda t: (0, 0)), pl.BlockSpec((1, GLA_V_W), lambda t: (0, 0))],
        out_shape=[SDS((s, w_qkvg), F32), SDS((s, GLA_QK_W), F32), SDS((1, GLA_QK_W), F32), SDS((1, GLA_V_W), F32)],
        scratch_shapes=[pltpu.VMEM((GLA_V_W, GLA_QK_W), F32)],
        compiler_params=_params("arbitrary"),
    )(p, p, p, p, z, bg, gain, oraw, stsave, dy_all)


def _local_step(x, mem, target, small, big, on_grads):
    g1, gmix, gmem, g2, sqg, skg, sinks, rel_bias, wgu, bg, gla_gain, mqg, mkg = small
    (w3_1a, w3_1b), gather_mix, gather_ffn2 = big
    wgu_pad = jnp.zeros((GLA_QK_W, GLA_QK_W), BF16).at[:GLA_RANK].set(wgu.astype(BF16))
    gain256 = jnp.tile(gla_gain, (1, GLA_HEADS))

    (part,), saved1a = _ffn_fwd(x, g1, w3_1a, "ffn1a", partial=True)
    win_p, wkv, wout = gather_mix(part)
    (x1, h), saved1b = _ffn_fwd(x, g1, w3_1b, "ffn1b", next_gain=gmix, start=(saved1a[0], part))
    w3_2 = gather_ffn2((wout, x1))
    p = _matmul(h, win_p, tm=512, tn=IN_W_PAD, name="mix_in")
    hm = _rms_fwd(mem, gmem, tm=256, name="mem_rms")
    kv = _matmul(hm, wkv, tm=256, tn=512, name="mem_kv")
    p_glr = p[:, COL_GLR:]
    z = _matmul(p_glr, wgu_pad, tm=1024, tn=GLA_QK_W, name="gla_gate")
    y_swa = _swa_fwd(p, sqg, skg, sinks, rel_bias, name="swa_fwd")
    y_gla, oraw, stsave = _gla_fwd(p, z, bg, gain256, name="gla_fwd")
    y_mem = _mem_fwd(p, kv, mqg, mkg, tq=512, name="mem_fwd")
    x2 = _matmul([y_swa, y_gla, y_mem], wout, b_blocks=[0, 2, 3], tm=512, tn=1024, res=x1, name="mix_out")
    (dy, dyb, loss), saved2 = _ffn_fwd(x2, g2, w3_2, "ffn2", target=target)

    dh2, dw3_2 = _ffn_bwd_part(dyb, w3_2, saved2, 0, w3_2.shape[1] // FFN_TN, None, name="ffn2_bwd")
    dx2, dx2b, dg2 = _rms_bwd(x2, g2, dh2, dy, tm=512, name="ffn2_drms")
    dx2b = on_grads("ffn2", [dw3_2], dx2b)
    dy_all = _matmul(dx2b, wout, tb=True, tm=512, tn=1024, name="mix_dy")
    dwout = _matmul(jnp.concatenate([y_swa, y_gla, y_mem], axis=1), dx2b, ta=True, tm=512, tn=1024, out_dtype=BF16,
                    name="mix_dw_out")
    dq_swa, dkv_swa, dsqg, dskg, dsink, drb = _swa_bwd(p, sqg, skg, sinks, rel_bias, dy_all, name="swa_bwd")
    dqkvg, dz, dbg, dgain256 = _gla_bwd(p, z, bg, gain256, oraw, stsave, dy_all, name="gla_bwd")
    dmq, dkv_mem, dmqg, dmkg = _mem_bwd(p, kv, mqg, mkg, dy_all, tq=512, name="mem_bwd")
    dglr = _matmul(dz, wgu_pad, tb=True, tm=1024, tn=GLA_QK_W, name="gla_gate_dx")
    dwgu_pad = _matmul(p_glr, dz, ta=True, tm=GLA_QK_W, tn=GLA_QK_W, name="gla_gate_dw")
    dp = [dq_swa, dkv_swa, dqkvg, dmq, dglr]
    dwin_p = _dw_parts(h, dp, tk=512, name="mix_dw_in")
    dx1, dx1b, dgmix = _rms_bwd(x1, gmix, (dp, win_p), dx2, tm=512, name="mix_dh_drms")
    dwkv = _matmul(hm, dkv_mem, ta=True, tm=512, tn=512, out_dtype=BF16, name="mem_dw_kv")
    dx1b = on_grads("mix", (dwin_p, dwkv, dwout), dx1b)
    _, _, dgmem = _rms_bwd(mem, gmem, (dkv_mem, wkv), None, tm=256, name="mem_dh_drms")
    dh1, dw3_1a = _ffn_bwd_part(dx1b, w3_1a, saved1a, 0, w3_1a.shape[1] // FFN_TN, None, name="ffn1_bwd_a")
    dgla_gain = dgain256.reshape(GLA_HEADS, GLA_DV).sum(axis=0, keepdims=True)
    dsmall = [dgmix, dgmem, dg2, dsqg, dskg, dsink[:, :SWA_HEADS], drb[:, :SWA_HEADS].T, dwgu_pad[:GLA_RANK], dbg,
              dgla_gain, dmqg, dmkg, loss]
    dh1, dgmem, dwgu_pad = on_grads("ffn1a", [dw3_1a], (dh1, dgmem, dwgu_pad), small=dsmall)
    dh1, dw3_1b = _ffn_bwd_part(dx1b, w3_1b, saved1b, 0, w3_1b.shape[1] // FFN_TN, dh1, name="ffn1_bwd_b")
    dx, _, dg1 = _rms_bwd(x, g1, dh1, dx1, tm=512, name="ffn1_drms")
    on_grads("ffn1b", [dw3_1b], None, small=[dg1])
    return dx


def _mesh_place():
    x, y, c = lax.axis_index("x"), lax.axis_index("y"), lax.axis_index("c")
    other_chips = [(1 - x, y), (x, 1 - y), (1 - x, 1 - y)]
    return x, y, c, other_chips


def _handshake(peers):
    barrier = pltpu.get_barrier_semaphore()
    for peer in peers:
        pl.semaphore_signal(barrier, inc=1, device_id=peer, device_id_type=MESH)
    pl.semaphore_wait(barrier, len(peers))


def _sequencer_call(body, operands, out_shapes, sems, *, name, collective_id):
    return pl.kernel(
        body, name=name, out_type=out_shapes, mesh=plsc.ScalarSubcoreMesh(axis_name="sequencer", num_cores=1),
        scratch_types=sems, compiler_params=pltpu.CompilerParams(collective_id=collective_id),
    )(*operands)


def _window(ref, kind, slot, shape):
    if kind == "row":
        rows = pl.ds(pl.multiple_of(slot * shape[-2], 8), shape[-2])
        return ref.at[(slice(None),) * (len(shape) - 2) + (rows,)]
    return ref.at[slot]


def _gathered(shape, kind):
    if kind == "row":
        return tuple(shape[:-2]) + (N_DEV * shape[-2], shape[-1])
    return (N_DEV,) + tuple(shape)


def _half(view, hf):
    if len(view.shape) == 4:
        return view.at[:, hf]
    n = view.shape[-2] // 2
    return view.at[(slice(None),) * (len(view.shape) - 2) + (pl.ds(hf * n, n),)]


def _all_gather(shards, kinds, *, name, collective_id):
    nt = len(shards)

    def body(*refs):
        x_refs, o_refs = refs[:nt], refs[nt:2 * nt]
        send_sems, recv_sems, local_sems = refs[2 * nt:]
        x, y, c, _ = _mesh_place()
        me, sibling, xn, yn, diag = (x, y, c), (x, y, 1 - c), (1 - x, y, c), (x, 1 - y, c), (1 - x, 1 - y, c)
        _handshake([sibling, xn, yn])

        def win(t, block):
            bx, by, bc = block
            return _window(o_refs[t], kinds[t], 4 * bx + 2 * by + bc, shards[t].shape)

        def copy(k, t, src, dst, to):
            return pltpu.make_async_remote_copy(src_ref=src, dst_ref=dst, send_sem=send_sems.at[k, t],
                                                recv_sem=recv_sems.at[k, t], device_id=to, device_id_type=MESH)

        def piece(k, t, block, hf, to, from_shard=False):
            dst = _half(win(t, block), hf)
            return copy(k, t, _half(x_refs[t], hf) if from_shard else dst, dst, to)

        mine = [pltpu.make_async_copy(x_refs[t], win(t, me), local_sems.at[t]) for t in range(nt)]
        sent = []

        def start(cp):
            cp.start()
            sent.append(cp)

        for cp in mine:
            cp.start()
        for t in range(nt):
            start(copy(0, t, x_refs[t], win(t, me), sibling))
        for hf_x, hf_y in ((0, 1), (1, 0)):
            for t in range(nt):
                start(piece(1 + hf_x, t, me, hf_x, xn, True))
                start(piece(3 + hf_y, t, me, hf_y, yn, True))
        for k, block, hf, onward, k_sib in ((1, xn, 0, (5, yn), 7), (4, yn, 1, (6, xn), 10), (2, xn, 1, None, 8),
                                           (3, yn, 0, None, 9), (5, diag, 0, None, 11), (6, diag, 1, None, 12)):
            for t in range(nt):
                piece(k, t, block, hf, me).wait_recv()
                if onward is not None:
                    start(piece(onward[0], t, block, hf, onward[1]))
                start(piece(k_sib, t, block, hf, sibling))
        for t in range(nt):
            copy(0, t, x_refs[t], win(t, sibling), me).wait_recv()
        for k_sib, block, hf in ((7, xn, 0), (10, yn, 1), (8, xn, 1), (9, yn, 0), (11, diag, 0), (12, diag, 1)):
            for t in range(nt):
                bx, by, _ = block
                piece(k_sib, t, (bx, by, 1 - c), hf, me).wait_recv()
        for cp in sent:
            cp.wait_send()
        for cp in mine:
            cp.wait()

    return _sequencer_call(
        body, shards, [SDS(_gathered(s.shape, k), s.dtype) for s, k in zip(shards, kinds)],
        [pltpu.SemaphoreType.DMA((13, nt)), pltpu.SemaphoreType.DMA((13, nt)), pltpu.SemaphoreType.DMA((nt,))],
        name=name, collective_id=collective_id)


def _part_shape(shape, kind):
    if kind == "row":
        return tuple(shape[:-2]) + (shape[-2] // N_DEV, shape[-1])
    return tuple(shape[2:])


def _pair_exchange(grads, kinds, *, name, collective_id):
    nt = len(grads)
    part = [_part_shape(g.shape, k) for g, k in zip(grads, kinds)]

    def body(*refs):
        g_refs, o_refs = refs[:nt], refs[nt:2 * nt]
        send_sems, recv_sems = refs[2 * nt:]
        x, y, c, _ = _mesh_place()
        _handshake([(x, y, 1 - c)])
        copies = []
        for t in range(nt):
            for xy in range(4):
                src = g_refs[t].at[1 - c, xy] if kinds[t] == "stack" else _window(g_refs[t], kinds[t], 2 * xy + 1 - c, part[t])
                copies.append(pltpu.make_async_remote_copy(
                    src_ref=src, dst_ref=o_refs[t].at[xy], send_sem=send_sems.at[xy, t], recv_sem=recv_sems.at[xy, t],
                    device_id=(x, y, 1 - c), device_id_type=MESH))
        for cp in copies:
            cp.start()
        for cp in copies:
            cp.wait()

    return _sequencer_call(
        body, grads, [SDS((4,) + p, g.dtype) for p, g in zip(part, grads)],
        [pltpu.SemaphoreType.DMA((4, nt)), pltpu.SemaphoreType.DMA((4, nt))], name=name, collective_id=collective_id)


def _chip_exchange(parts, small, *, name, collective_id):
    nt = len(parts)
    if small is None:
        def body_plain(*refs):
            s_refs, o_refs = refs[:nt], refs[nt:2 * nt]
            send_sems, recv_sems = refs[2 * nt:]
            x, y, c, chips = _mesh_place()
            _handshake([(*chip, c) for chip in chips])
            copies = [pltpu.make_async_remote_copy(
                src_ref=s_refs[t].at[2 * chip[0] + chip[1]], dst_ref=o_refs[t].at[j],
                send_sem=send_sems.at[j, t], recv_sem=recv_sems.at[j, t], device_id=(*chip, c), device_id_type=MESH)
                for j, chip in enumerate(chips) for t in range(nt)]
            for cp in copies:
                cp.start()
            for cp in copies:
                cp.wait()

        return _sequencer_call(
            body_plain, parts, [SDS((3,) + s.shape[1:], s.dtype) for s in parts],
            [pltpu.SemaphoreType.DMA((3, nt)), pltpu.SemaphoreType.DMA((3, nt))], name=name, collective_id=collective_id)

    def body(*refs):
        s_refs, small_ref = refs[:nt], refs[nt]
        o_refs, small_all = refs[nt + 1:2 * nt + 1], refs[2 * nt + 1]
        send_sems, recv_sems, small_send, small_recv, local_sem = refs[2 * nt + 2:]
        x, y, c, chips = _mesh_place()
        _handshake([(px, py, pc) for px in (x, 1 - x) for py in (y, 1 - y) for pc in (c, 1 - c)][1:])

        def copy(j, t, chip):
            return pltpu.make_async_remote_copy(
                src_ref=s_refs[t].at[2 * chip[0] + chip[1]], dst_ref=o_refs[t].at[j],
                send_sem=send_sems.at[j, t], recv_sem=recv_sems.at[j, t], device_id=(*chip, c), device_id_type=MESH)

        flips = [(fx, fy, fc) for fx in (0, 1) for fy in (0, 1) for fc in (0, 1)][1:]

        def small_copy(k):
            fx, fy, fc = flips[k]
            to = (x ^ fx if fx else x, y ^ fy if fy else y, c ^ fc if fc else c)
            rows = small_all.at[4 * x + 2 * y + c]
            return pltpu.make_async_remote_copy(
                src_ref=small_ref, dst_ref=rows, send_sem=small_send.at[k], recv_sem=small_recv.at[k],
                device_id=to, device_id_type=MESH)

        own = pltpu.make_async_copy(small_ref, small_all.at[4 * x + 2 * y + c], local_sem)
        own.start()
        copies = [copy(j, t, chip) for j, chip in enumerate(chips) for t in range(nt)]
        smalls = [small_copy(k) for k in range(7)]
        for cp in smalls + copies:
            cp.start()
        for cp in smalls + copies:
            cp.wait()
        own.wait()

    return _sequencer_call(
        body, list(parts) + [small],
        [SDS((3,) + s.shape[1:], s.dtype) for s in parts] + [SDS((N_DEV,) + small.shape, small.dtype)],
        [pltpu.SemaphoreType.DMA((3, nt)), pltpu.SemaphoreType.DMA((3, nt)),
         pltpu.SemaphoreType.DMA((7,)), pltpu.SemaphoreType.DMA((7,)), pltpu.SemaphoreType.DMA],
        name=name, collective_id=collective_id)


def _pair_sum(grad, theirs, kind, c, *, name):
    if kind == "row":
        r, l = theirs.shape[-2:]
        n = theirs.size // (4 * r * l)
        grad, theirs = grad.reshape(n, N_DEV * r, l), theirs.reshape(4, n, r, l)
        mine_spec = pl.BlockSpec((n, r, l), lambda xy, c_ref: (0, 2 * xy + c_ref[0], 0))
    else:
        r, l = theirs.shape[-2:]
        n = theirs.size // (4 * r * l)
        theirs = theirs.reshape(4, n, r, l)
        grad = grad.reshape(2, 4, n, r, l)
        mine_spec = pl.BlockSpec((None, None, n, r, l), lambda xy, c_ref: (c_ref[0], xy, 0, 0, 0))

    def body(c_ref, a_ref, b_ref, o_ref):
        o_ref[...] = (a_ref[...].astype(F32) + b_ref[...].astype(F32)).astype(BF16)

    part = pl.BlockSpec((None, n, r, l), lambda xy, c_ref: (xy, 0, 0, 0))
    return pl.pallas_call(
        body, name=name,
        grid_spec=pltpu.PrefetchScalarGridSpec(num_scalar_prefetch=1, grid=(4,), in_specs=[mine_spec, part], out_specs=part),
        out_shape=SDS((4, n, r, l), BF16), compiler_params=_params("parallel"),
    )(c, grad, theirs)


def _adamw(w, g, m, v):
    m = ADAM_B1 * m + (1.0 - ADAM_B1) * g
    v = ADAM_B2 * v + (1.0 - ADAM_B2) * jnp.square(g)
    m_hat = m / (1.0 - ADAM_B1 ** ADAM_STEP)
    v_hat = v / (1.0 - ADAM_B2 ** ADAM_STEP)
    delta = -ADAM_LR * (m_hat / (jnp.sqrt(v_hat) + ADAM_EPS) + ADAM_WD * w)
    return delta, m, v


def _adam_big(owns, others, mat, xy, w, m, v, *, tr, name):
    _, r, l = w.shape
    lp = owns[0].shape[-1]
    nq = len(owns)
    rows = [tr] if nq == 1 else [o.shape[-2] for o in owns]
    assert sum(rows) == tr and r % tr == 0, (r, tr, rows)

    def body(xy_ref, *refs):
        own_refs, oth_refs = refs[:nq], refs[nq:2 * nq]
        w_ref, m_ref, v_ref, g_out, d_out, m_out, v_out = refs[2 * nq:]
        parts = []
        for q in range(nq):
            gq = own_refs[q][0, 0].astype(F32)
            for j in range(3):
                gq = gq + oth_refs[q][j, 0].astype(F32)
            parts.append(gq)
        g = (parts[0] if nq == 1 else jnp.concatenate(parts, axis=0))[:, :l]
        delta, m_new, v_new = _adamw(w_ref[0], g, m_ref[0], v_ref[0])
        g_out[0] = g
        d_out[0] = delta
        m_out[0] = m_new
        v_out[0] = v_new

    blk = pl.BlockSpec((1, tr, l), lambda i, xy_ref: (0, i, 0))
    own_specs = [pl.BlockSpec((1, 1, n, lp), lambda i, xy_ref: (xy_ref[0], mat, i, 0)) for n in rows]
    oth_specs = [pl.BlockSpec((3, 1, n, lp), lambda i, xy_ref: (0, mat, i, 0)) for n in rows]
    return pl.pallas_call(
        body, name=name,
        grid_spec=pltpu.PrefetchScalarGridSpec(
            num_scalar_prefetch=1, grid=(r // tr,), in_specs=own_specs + oth_specs + [blk, blk, blk],
            out_specs=[blk, blk, blk, blk]),
        out_shape=[SDS(w.shape, F32)] * 4, compiler_params=_params("parallel"),
    )(xy, *owns, *others, w, m, v)


def _adam_cols(own, other, xy, w, m, v, *, tc, name):
    _, wc, d = w.shape
    wp = -(-wc // 128) * 128

    def body(xy_ref, own_ref, oth_ref, w_ref, m_ref, v_ref, g_out, d_out, m_out, v_out, buf):
        g = own_ref[0, 0].astype(F32)
        for j in range(3):
            g = g + oth_ref[j, 0].astype(F32)
        buf[:, :wc] = g
        buf[:, wc:] = jnp.zeros((tc, wp - wc), F32)
        g = buf[...].T[:wc]
        delta, m_new, v_new = _adamw(w_ref[0], g, m_ref[0], v_ref[0])
        g_out[0] = g
        d_out[0] = delta
        m_out[0] = m_new
        v_out[0] = v_new

    blk = pl.BlockSpec((1, wc, tc), lambda i, xy_ref: (0, 0, i))
    in_specs = [pl.BlockSpec((1, 1, tc, wc), lambda i, xy_ref: (xy_ref[0], 0, i, 0)),
                pl.BlockSpec((3, 1, tc, wc), lambda i, xy_ref: (0, 0, i, 0)), blk, blk, blk]
    return pl.pallas_call(
        body, name=name,
        grid_spec=pltpu.PrefetchScalarGridSpec(
            num_scalar_prefetch=1, grid=(d // tc,), in_specs=in_specs, out_specs=[blk, blk, blk, blk],
            scratch_shapes=[pltpu.VMEM((tc, wp), F32)]),
        out_shape=[SDS(w.shape, F32)] * 4, compiler_params=_params("parallel"),
    )(xy, own, other, w, m, v)


def _small_layout(shapes):
    out, at = [], 0
    for r, c in shapes:
        rows = c // 128 if (r == 1 and c > 128) else r
        out.append((at, rows))
        at += -(-rows // 8) * 8
    return out, at


def _pack_small(parts, *, name):
    shapes = [a.shape for a in parts]
    layout, total = _small_layout(shapes)

    def body(*refs):
        o_ref = refs[-1]
        o_ref[...] = jnp.zeros_like(o_ref)
        for x_ref, (r, c), (at, rows) in zip(refs, shapes, layout):
            if r == 1 and c > 128:
                for k in range(rows):
                    o_ref[at + k:at + k + 1, :] = x_ref[:, k * 128:(k + 1) * 128]
            else:
                o_ref[at:at + r, 0:c] = x_ref[...]

    return pl.pallas_call(body, name=name, out_shape=SDS((total, 128), F32))(*parts)


def _adam_small(g_all, ws, ms, vs, *, name):
    n = len(ws)
    shapes = [w.shape for w in ws]
    layout, _ = _small_layout(shapes)

    def body(g_ref, *refs):
        w_refs, m_refs, v_refs, outs = refs[:n], refs[n:2 * n], refs[2 * n:3 * n], refs[3 * n:]
        g_sum = g_ref[0]
        for k in range(1, N_DEV):
            g_sum = g_sum + g_ref[k]
        for i, ((r, c), (at, rows)) in enumerate(zip(shapes, layout)):
            if r == 1 and c > 128:
                g = jnp.concatenate([g_sum[at + k:at + k + 1, :] for k in range(rows)], axis=1)
            else:
                g = g_sum[at:at + r, 0:c]
            delta, m_new, v_new = _adamw(w_refs[i][...], g, m_refs[i][...], v_refs[i][...])
            for q, val in enumerate((g, delta, m_new, v_new)):
                outs[4 * i + q][...] = val

    flat = pl.pallas_call(body, name=name, out_shape=[SDS(s, F32) for s in shapes for _ in range(4)])(g_all, *ws, *ms, *vs)
    return [flat[4 * i:4 * i + 4] for i in range(n)]


def kernel(x, mem, ffn1_norm, ffn1_w_gate, ffn1_w_up, ffn1_w_down, mix_norm, mem_norm, w_in, w_mem_kv, swa_q_norm, swa_k_norm, swa_sinks, rel_bias, gla_w_gate_up, gla_b_gate, gla_out_norm, mem_q_norm, mem_k_norm, w_out, ffn2_norm, ffn2_w_gate, ffn2_w_up, ffn2_w_down, loss_target, m_ffn1_norm, m_ffn1_w_gate, m_ffn1_w_up, m_ffn1_w_down, m_mix_norm, m_mem_norm, m_w_in, m_w_mem_kv, m_swa_q_norm, m_swa_k_norm, m_swa_sinks, m_rel_bias, m_gla_w_gate_up, m_gla_b_gate, m_gla_out_norm, m_mem_q_norm, m_mem_k_norm, m_w_out, m_ffn2_norm, m_ffn2_w_gate, m_ffn2_w_up, m_ffn2_w_down, v_ffn1_norm, v_ffn1_w_gate, v_ffn1_w_up, v_ffn1_w_down, v_mix_norm, v_mem_norm, v_w_in, v_w_mem_kv, v_swa_q_norm, v_swa_k_norm, v_swa_sinks, v_rel_bias, v_gla_w_gate_up, v_gla_b_gate, v_gla_out_norm, v_mem_q_norm, v_mem_k_norm, v_w_out, v_ffn2_norm, v_ffn2_w_gate, v_ffn2_w_up, v_ffn2_w_down):
    xi, yi, ci = lax.axis_index("x"), lax.axis_index("y"), lax.axis_index("c")
    c_arr = jnp.reshape(ci, (1,)).astype(jnp.int32)
    xy_arr = jnp.reshape(2 * xi + yi, (1,)).astype(jnp.int32)
    d = x.shape[-1]

    def ffn_shards(wg_s, wu_s, wd_s):
        return jnp.concatenate([wg_s.transpose(0, 2, 1), wu_s.transpose(0, 2, 1), wd_s], axis=0).astype(BF16)

    def gather_ffn(wg_s, wu_s, wd_s, name, collective_id, after):
        w3_s, _ = lax.optimization_barrier((ffn_shards(wg_s, wu_s, wd_s), after))
        return _all_gather([w3_s], ["row"], name=name, collective_id=collective_id)[0]

    w3_s = ffn_shards(ffn1_w_gate, ffn1_w_up, ffn1_w_down)
    w3_1a = _all_gather([w3_s[:, :FFN1_FIRST]], ["row"], name="gather_ffn1a", collective_id=0)[0]
    w3_s, _ = lax.optimization_barrier((w3_s, w3_1a))
    w3_1b = _all_gather([w3_s[:, FFN1_FIRST:]], ["row"], name="gather_ffn1b", collective_id=11)[0]

    def gather_mix(after):
        mix_s = lax.optimization_barrier((w_in[0].astype(BF16), w_mem_kv[0].astype(BF16), w_out[0].astype(BF16),
                                          (w3_1b, after)))[:3]
        win_all, wkv, wout = _all_gather(list(mix_s), ["stack", "row", "row"], name="gather_mix", collective_id=1)
        return _pack_win(win_all, tr=256, name="pack_w_in"), wkv, wout

    def gather_ffn2(after):
        return gather_ffn(ffn2_w_gate, ffn2_w_up, ffn2_w_down, "gather_ffn2", 2, after)

    small_w = [ffn1_norm, mix_norm, mem_norm, ffn2_norm, swa_q_norm, swa_k_norm, swa_sinks[0], rel_bias,
               gla_w_gate_up[0], gla_b_gate, gla_out_norm, mem_q_norm, mem_k_norm]
    collective_ids = {"ffn2": (3, 4), "mix": (5, 6), "ffn1a": (7, 8), "ffn1b": (9, 10)}
    reduced, small_box = {}, {}

    def on_grads(group, grads, carry, small=None):
        if group == "mix":
            dwin_p, dwkv, dwout = grads
            grads = [_unpack_win(dwin_p, tr=256, name="unpack_dw_in"), dwkv, dwout]
            kinds = ["stack", "row", "row"]
        else:
            kinds = ["row"]
        if reduced:
            earlier = list(reduced.values())[-1][1]
            *grads, _ = lax.optimization_barrier((*grads, earlier[0]))
        id_pair, id_chip = collective_ids[group]
        from_sibling = _pair_exchange(grads, kinds, name=f"pair_exchange_{group}", collective_id=id_pair)
        chip_sums = [_pair_sum(g, theirs, k, c_arr, name=f"pair_sum_{group}_{t}")
                     for t, (g, theirs, k) in enumerate(zip(grads, from_sibling, kinds))]
        if carry is not None:
            *chip_sums, carry = lax.optimization_barrier((*chip_sums, carry))
        if small is None:
            from_chips = _chip_exchange(chip_sums, None, name=f"chip_exchange_{group}", collective_id=id_chip)
        else:
            packed = _pack_small(small, name=f"pack_small_{group}")
            *from_chips, small_all = _chip_exchange(chip_sums, packed, name=f"chip_exchange_{group}",
                                                    collective_id=id_chip)
            small_box[group] = small_all
        reduced[group] = (chip_sums, from_chips)
        return carry

    grad_x = _local_step(x[0], mem[0], loss_target[0], small_w, ((w3_1a, w3_1b), gather_mix, gather_ffn2), on_grads)

    big_w = {"ffn1_w_gate": ("ffn1", 0, 0, True, ffn1_w_gate, m_ffn1_w_gate, v_ffn1_w_gate),
             "ffn1_w_up": ("ffn1", 0, 1, True, ffn1_w_up, m_ffn1_w_up, v_ffn1_w_up),
             "ffn1_w_down": ("ffn1", 0, 2, False, ffn1_w_down, m_ffn1_w_down, v_ffn1_w_down),
             "w_in": ("mix", 0, 0, True, w_in, m_w_in, v_w_in),
             "w_mem_kv": ("mix", 1, 0, False, w_mem_kv, m_w_mem_kv, v_w_mem_kv),
             "w_out": ("mix", 2, 0, False, w_out, m_w_out, v_w_out),
             "ffn2_w_gate": ("ffn2", 0, 0, True, ffn2_w_gate, m_ffn2_w_gate, v_ffn2_w_gate),
             "ffn2_w_up": ("ffn2", 0, 1, True, ffn2_w_up, m_ffn2_w_up, v_ffn2_w_up),
             "ffn2_w_down": ("ffn2", 0, 2, False, ffn2_w_down, m_ffn2_w_down, v_ffn2_w_down)}
    res = {}
    for nm, (group, t, mat, transposed, w, m, v) in big_w.items():
        shape = w.shape
        if transposed:
            w, m, v = (a.transpose(0, 2, 1) for a in (w, m, v))
        if nm == "w_in":
            out = _adam_cols(reduced[group][0][t], reduced[group][1][t], xy_arr, w, m, v, tc=256, name=f"adam_{nm}")
            res[nm] = [a.transpose(0, 2, 1) for a in out]
            continue
        r = w.shape[1]
        halves = ["ffn1a", "ffn1b"] if group == "ffn1" else [group]
        if len(halves) == 1:
            tr = 256 if r % 256 == 0 else r
        else:
            tr = r
        out = _adam_big([reduced[k][0][t] for k in halves], [reduced[k][1][t] for k in halves], mat, xy_arr, w, m, v,
                        tr=tr, name=f"adam_{nm}")
        if transposed:
            out = [a.reshape(1, -1, d).transpose(0, 2, 1) for a in out]
        res[nm] = [a.reshape(shape) for a in out]
    small_names = ["ffn1_norm", "mix_norm", "mem_norm", "ffn2_norm", "swa_q_norm", "swa_k_norm", "swa_sinks", "rel_bias",
                   "gla_w_gate_up", "gla_b_gate", "gla_out_norm", "mem_q_norm", "mem_k_norm"]
    small_m = [m_ffn1_norm, m_mix_norm, m_mem_norm, m_ffn2_norm, m_swa_q_norm, m_swa_k_norm, m_swa_sinks, m_rel_bias,
               m_gla_w_gate_up, m_gla_b_gate, m_gla_out_norm, m_mem_q_norm, m_mem_k_norm]
    small_v = [v_ffn1_norm, v_mix_norm, v_mem_norm, v_ffn2_norm, v_swa_q_norm, v_swa_k_norm, v_swa_sinks, v_rel_bias,
               v_gla_w_gate_up, v_gla_b_gate, v_gla_out_norm, v_mem_q_norm, v_mem_k_norm]
    small_full = [ffn1_norm, mix_norm, mem_norm, ffn2_norm, swa_q_norm, swa_k_norm, swa_sinks, rel_bias,
                  gla_w_gate_up, gla_b_gate, gla_out_norm, mem_q_norm, mem_k_norm]
    zero = jnp.zeros((1, 1), F32)
    turned = ("rel_bias",)

    def two_d(nm, a):
        a = a.reshape(a.shape[-2:])
        return a.T if nm in turned else a

    for group, sel in (("ffn1a", slice(1, None)), ("ffn1b", slice(0, 1))):
        extra = [zero] if group == "ffn1a" else []
        ws, ms, vs = ([two_d(nm, a) for nm, a in zip(small_names[sel], arrs[sel])] + extra
                      for arrs in (small_full, small_m, small_v))
        updated = _adam_small(small_box[group], ws, ms, vs, name=f"adam_small_{group}")
        for nm, full, out in zip(small_names[sel], small_full[sel], updated):
            res[nm] = [(a.T if nm in turned else a).reshape(full.shape) for a in out]
        if extra:
            loss = updated[-1][0].reshape(())

    order = ["ffn1_norm", "ffn1_w_gate", "ffn1_w_up", "ffn1_w_down", "mix_norm", "mem_norm", "w_in", "w_mem_kv",
             "swa_q_norm", "swa_k_norm", "swa_sinks", "rel_bias", "gla_w_gate_up", "gla_b_gate", "gla_out_norm",
             "mem_q_norm", "mem_k_norm", "w_out", "ffn2_norm", "ffn2_w_gate", "ffn2_w_up", "ffn2_w_down"]
    outs = [loss, grad_x[None]]
    for q in range(4):
        outs += [res[nm][q] for nm in order]
    return tuple(outs)
```

```python
import functools
import math

import numpy as np
import jax
import jax.numpy as jnp
from jax import lax
from jax.experimental import pallas as pl
from jax.experimental.pallas import tpu as pltpu
from jax.experimental.pallas import tpu_sc as plsc

F32 = jnp.float32
BF16 = jnp.bfloat16
SDS = jax.ShapeDtypeStruct

EPS = 1e-6
HEAD_DIM = 64
SWA_HEADS = 8
SWA_KV_HEADS = 2
SWA_GROUP = SWA_HEADS // SWA_KV_HEADS
BLOCK = 128
N_BUCKETS = 32
MAX_DISTANCE = 128
GLA_HEADS = 4
GLA_DK = 32
GLA_DV = 64
GLA_RANK = 16
GLA_TAU = 16.0
GLA_CHUNK = 32
MEM_HEADS = 4
SWA_Q_W = SWA_HEADS * HEAD_DIM
SWA_KV_W = SWA_KV_HEADS * HEAD_DIM
GLA_QK_W = GLA_HEADS * GLA_DK
GLA_V_W = GLA_HEADS * GLA_DV
MEM_Q_W = MEM_HEADS * HEAD_DIM
IN_W = 1808
IN_W_PAD = 1920
COL_SQ, COL_SKV, COL_GQ, COL_GK, COL_GV, COL_GG, COL_MQ, COL_GLR = 0, 512, 768, 896, 1024, 1280, 1536, 1792

ADAM_LR = 0.001
ADAM_B1 = 0.9
ADAM_B2 = 0.999
ADAM_EPS = 1e-08
ADAM_WD = 0.01
ADAM_STEP = 10

N_DEV = 8
VMEM_LIMIT_BYTES = 56 * 1024 * 1024
MESH = pl.DeviceIdType.MESH


def _params(*sem):
    return pltpu.CompilerParams(dimension_semantics=sem or None, vmem_limit_bytes=VMEM_LIMIT_BYTES)


def _dot(a, b, ta, tb, precision=None):
    dims = (((0 if ta else 1,), (1 if tb else 0,)), ((), ()))
    return lax.dot_general(a, b, dims, preferred_element_type=F32, precision=precision)


def _mm_raw(a, b, ta=False, tb=False):
    return _dot(a.astype(BF16), b.astype(BF16), ta, tb)


def _mmf_raw(a, b, ta=False, tb=False):
    return _dot(a, b, ta, tb, lax.Precision.HIGHEST)


def _make_mm(raw):
    @functools.partial(jax.custom_vjp, nondiff_argnums=(2, 3))
    def mm(a, b, ta=False, tb=False):
        return raw(a, b, ta, tb)

    def fwd(a, b, ta, tb):
        return raw(a, b, ta, tb), (a, b)

    def bwd(ta, tb, res, g):
        a, b = res
        da = raw(b, g, tb, True) if ta else raw(g, b, False, not tb)
        db = raw(g, a, True, ta) if tb else raw(a, g, not ta, False)
        return da, db

    mm.defvjp(fwd, bwd)
    return mm


_mm = _make_mm(_mm_raw)
_mmf = _make_mm(_mmf_raw)


def _mm3(a, b, ta=False, tb=False):
    a_hi, b_hi = a.astype(BF16).astype(F32), b.astype(BF16).astype(F32)
    return _mm(a_hi, b_hi, ta, tb) + _mm(a_hi, b - b_hi, ta, tb) + _mm(a - a_hi, b_hi, ta, tb)


def _rms(x, g):
    return x * lax.rsqrt(jnp.mean(x * x, axis=-1, keepdims=True) + EPS) * g


def _silu_mul(g, u):
    return jax.nn.silu(g) * u


def _log_sigmoid(z):
    return jnp.minimum(z, 0.0) - jnp.log(1.0 + jnp.exp(-jnp.abs(z)))


def _matmul(a_list, b, *, ta=False, tb=False, tm, tn, b_blocks=None, res=None, scale=1.0, out_dtype=F32, name):
    if not isinstance(a_list, (list, tuple)):
        a_list = [a_list]
    n_a = len(a_list)
    m = a_list[0].shape[1] if ta else a_list[0].shape[0]
    ks = [a.shape[0] if ta else a.shape[1] for a in a_list]
    n = b.shape[0] if tb else b.shape[1]
    if b_blocks is None:
        assert n_a == 1
        b_blocks = [0]
    tm, tn = min(tm, m), min(tn, n)
    assert m % tm == 0 and n % tn == 0, (m, n, tm, tn)

    def body(*refs):
        a_refs, b_refs = refs[:n_a], refs[n_a:2 * n_a]
        r_ref = refs[2 * n_a] if res is not None else None
        o_ref = refs[-1]
        acc = _mm_raw(a_refs[0][...], b_refs[0][...], ta, tb)
        for k in range(1, n_a):
            acc = acc + _mm_raw(a_refs[k][...], b_refs[k][...], ta, tb)
        if scale != 1.0:
            acc = acc * scale
        if r_ref is not None:
            acc = r_ref[...] + acc
        o_ref[...] = acc.astype(out_dtype)

    in_specs = []
    for k in ks:
        in_specs.append(pl.BlockSpec((k, tm), lambda i, j: (0, i)) if ta else pl.BlockSpec((tm, k), lambda i, j: (i, 0)))
    for k, blk in zip(ks, b_blocks):
        if tb:
            in_specs.append(pl.BlockSpec((tn, k), functools.partial(lambda i, j, blk: (j, blk), blk=blk)))
        else:
            in_specs.append(pl.BlockSpec((k, tn), functools.partial(lambda i, j, blk: (blk, j), blk=blk)))
    args = list(a_list) + [b] * n_a
    if res is not None:
        in_specs.append(pl.BlockSpec((tm, tn), lambda i, j: (i, j)))
        args.append(res)
    return pl.pallas_call(
        body, name=name, grid=(m // tm, n // tn), in_specs=in_specs,
        out_specs=pl.BlockSpec((tm, tn), lambda i, j: (i, j)), out_shape=SDS((m, n), out_dtype),
        compiler_params=_params("parallel", "parallel"),
    )(*args)


def _win_pieces(w):
    glr_lo, glr_hi = COL_MQ, COL_MQ + GLA_RANK
    out = []
    for j in range(N_DEV):
        for lo, hi, shift in ((0, glr_lo, 0), (glr_lo, glr_hi, COL_GLR - glr_lo), (glr_hi, IN_W, COL_MQ - glr_hi)):
            s, e = max(j * w, lo), min((j + 1) * w, hi)
            if s < e:
                out.append((j, s - j * w, e - j * w, s + shift))
    return out


def _pack_win(win_all, *, tr, name):
    _, d, w = win_all.shape

    def body(i_ref, o_ref):
        for j, a, b, dst in _win_pieces(w):
            o_ref[:, dst:dst + b - a] = i_ref[j][:, a:b]
        o_ref[:, IN_W:] = jnp.zeros((tr, IN_W_PAD - IN_W), o_ref.dtype)

    return pl.pallas_call(
        body, name=name, grid=(d // tr,), in_specs=[pl.BlockSpec((N_DEV, tr, w), lambda i: (0, i, 0))],
        out_specs=pl.BlockSpec((tr, IN_W_PAD), lambda i: (i, 0)), out_shape=SDS((d, IN_W_PAD), win_all.dtype),
        compiler_params=_params("parallel"),
    )(win_all)


def _unpack_win(dwin_parts, *, tr, name):
    d = dwin_parts[0].shape[0]
    w = IN_W // N_DEV
    starts = [sum(p.shape[1] for p in dwin_parts[:k]) for k in range(len(dwin_parts) + 1)]

    def body(*refs):
        o_ref = refs[-1]
        for j, a, b, src in _win_pieces(w):
            for k, i_ref in enumerate(refs[:-1]):
                lo, hi = max(src, starts[k]), min(src + b - a, starts[k + 1])
                if lo < hi:
                    o_ref[j % 2, j // 2, :, a + lo - src:a + hi - src] = i_ref[:, lo - starts[k]:hi - starts[k]]

    return pl.pallas_call(
        body, name=name, grid=(d // tr,), in_specs=[pl.BlockSpec((tr, p.shape[1]), lambda i: (i, 0)) for p in dwin_parts],
        out_specs=pl.BlockSpec((2, 4, tr, w), lambda i: (0, 0, i, 0)), out_shape=SDS((2, 4, d, w), dwin_parts[0].dtype),
        compiler_params=_params("parallel"),
    )(*dwin_parts)


def _dw_parts(h, parts, *, tk, name):
    s, d = h.shape
    n_p = len(parts)

    def body(*refs):
        h_ref, p_refs, o_refs, acc_refs = refs[0], refs[1:1 + n_p], refs[1 + n_p:1 + 2 * n_p], refs[1 + 2 * n_p:]
        i = pl.program_id(0)
        h_t = h_ref[...].T

        @pl.when(i == 0)
        def _():
            for acc_ref in acc_refs:
                acc_ref[...] = jnp.zeros_like(acc_ref)

        for p_ref, acc_ref in zip(p_refs, acc_refs):
            acc_ref[...] += _mm_raw(h_t, p_ref[...], False, False)

        @pl.when(i == pl.num_programs(0) - 1)
        def _():
            for o_ref, acc_ref in zip(o_refs, acc_refs):
                o_ref[...] = acc_ref[...].astype(BF16)

    return pl.pallas_call(
        body, name=name, grid=(s // tk,),
        in_specs=[pl.BlockSpec((tk, d), lambda i: (i, 0))] + [pl.BlockSpec((tk, p.shape[1]), lambda i: (i, 0)) for p in parts],
        out_specs=[pl.BlockSpec((d, p.shape[1]), lambda i: (0, 0)) for p in parts],
        out_shape=[SDS((d, p.shape[1]), BF16) for p in parts],
        scratch_shapes=[pltpu.VMEM((d, p.shape[1]), F32) for p in parts],
        compiler_params=_params("arbitrary"),
    )(h, *parts)


def _dw_rows(parts, b, *, tk, name):
    s, n = b.shape
    n_p = len(parts)
    starts = [sum(p.shape[1] for p in parts[:k]) for k in range(n_p + 1)]

    def body(*refs):
        p_refs, b_ref, o_ref, acc_ref = refs[:n_p], refs[n_p], refs[n_p + 1], refs[n_p + 2]
        i = pl.program_id(0)

        @pl.when(i == 0)
        def _():
            acc_ref[...] = jnp.zeros_like(acc_ref)

        for k, p_ref in enumerate(p_refs):
            acc_ref[starts[k]:starts[k + 1], :] += _mm_raw(p_ref[...], b_ref[...], True, False)

        @pl.when(i == pl.num_programs(0) - 1)
        def _():
            o_ref[...] = acc_ref[...].astype(BF16)

    return pl.pallas_call(
        body, name=name, grid=(s // tk,),
        in_specs=[pl.BlockSpec((tk, p.shape[1]), lambda i: (i, 0)) for p in parts] + [pl.BlockSpec((tk, n), lambda i: (i, 0))],
        out_specs=pl.BlockSpec((starts[-1], n), lambda i: (0, 0)), out_shape=SDS((starts[-1], n), BF16),
        scratch_shapes=[pltpu.VMEM((starts[-1], n), F32)], compiler_params=_params("arbitrary"),
    )(*parts, b)


def _rms_fwd(x, g, *, tm, name):
    s, d = x.shape

    def body(x_ref, g_ref, h_ref):
        h_ref[...] = _rms(x_ref[...], g_ref[...]).astype(BF16)

    return pl.pallas_call(
        body, name=name, grid=(s // tm,),
        in_specs=[pl.BlockSpec((tm, d), lambda i: (i, 0)), pl.BlockSpec((1, d), lambda i: (0, 0))],
        out_specs=pl.BlockSpec((tm, d), lambda i: (i, 0)), out_shape=SDS((s, d), BF16),
        compiler_params=_params("parallel"),
    )(x, g)


def _rms_bwd(x, g, dh, dres, *, tm, name):
    s, d = x.shape
    want_dx = dres is not None
    product = isinstance(dh, tuple)
    if product:
        a_parts = list(dh[0]) if isinstance(dh[0], (list, tuple)) else [dh[0]]
        starts = [sum(a.shape[1] for a in a_parts[:k]) for k in range(len(a_parts) + 1)]

    def body(*refs):
        n_dh = len(a_parts) + 1 if product else 1
        x_ref, g_ref = refs[:2]
        dh_refs, rest = refs[2:2 + n_dh], refs[2 + n_dh:]
        if want_dx:
            dres_ref, dx_ref, dxb_ref, dg_ref = rest
        else:
            dg_ref, = rest
        if product:
            b_ref = dh_refs[-1]
            dh_tile = _mm_raw(dh_refs[0][...], b_ref[:, starts[0]:starts[1]], False, True)
            for k in range(1, len(a_parts)):
                dh_tile = dh_tile + _mm_raw(dh_refs[k][...], b_ref[:, starts[k]:starts[k + 1]], False, True)
        else:
            dh_tile = dh_refs[0][...]
        _, vjp = jax.vjp(_rms, x_ref[...], g_ref[...])
        dx, dg = vjp(dh_tile)
        if want_dx:
            dx = dres_ref[...] + dx
            dx_ref[...] = dx
            dxb_ref[...] = dx.astype(BF16)

        @pl.when(pl.program_id(0) == 0)
        def _():
            dg_ref[...] = jnp.zeros_like(dg_ref)

        dg_ref[...] += dg

    row = pl.BlockSpec((tm, d), lambda i: (i, 0))
    vec = pl.BlockSpec((1, d), lambda i: (0, 0))
    if product:
        dh_specs = [pl.BlockSpec((tm, a.shape[1]), lambda i: (i, 0)) for a in a_parts]
        dh_specs.append(pl.BlockSpec((d, starts[-1]), lambda i: (0, 0)))
        dh_args = a_parts + [dh[1]]
    else:
        dh_specs, dh_args = [row], [dh]
    if want_dx:
        return pl.pallas_call(
            body, name=name, grid=(s // tm,), in_specs=[row, vec] + dh_specs + [row], out_specs=[row, row, vec],
            out_shape=[SDS((s, d), F32), SDS((s, d), BF16), SDS((1, d), F32)], compiler_params=_params("arbitrary"),
        )(x, g, *dh_args, dres)
    return None, None, pl.pallas_call(
        body, name=name, grid=(s // tm,), in_specs=[row, vec] + dh_specs, out_specs=vec,
        out_shape=SDS((1, d), F32), compiler_params=_params("arbitrary"),
    )(x, g, *dh_args)


FFN_TN = 256
FFN_TN_FWD = 512
FFN1_FIRST = 192


def _ffn_fwd(x, gain, w3, tag, *, tm=1024, next_gain=None, target=None, start=None, partial=False):
    s, d = x.shape
    f = w3.shape[1]
    tn = FFN_TN_FWD if f % FFN_TN_FWD == 0 else FFN_TN
    nj = f // tn
    tm = min(tm, s)
    n_extra = (next_gain is not None) + (target is not None) + 2 * (start is not None)

    def body(*refs):
        x_ref, gain_ref, wg_ref, wu_ref, wd_ref = refs[:5]
        extra, outs = refs[5:5 + n_extra], refs[5 + n_extra:-1]
        acc_s = refs[-1]
        g_ref, u_ref = outs[-2:]
        h_ref = extra[-2] if start is not None else outs[-3]
        i, j = pl.program_id(0), pl.program_id(1)

        @pl.when(j == 0)
        def _():
            if start is None:
                h_ref[...] = _rms(x_ref[...], gain_ref[...]).astype(BF16)
                acc_s[...] = jnp.zeros_like(acc_s)
            else:
                acc_s[...] = extra[-1][...]

        hv = h_ref[...]
        g = _mm_raw(hv, wg_ref[...], False, True)
        u = _mm_raw(hv, wu_ref[...], False, True)
        g_ref[...] = g.astype(BF16)
        u_ref[...] = u.astype(BF16)
        acc_s[...] += _mm_raw(_silu_mul(g, u), wd_ref[...])

        @pl.when(j == nj - 1)
        def _():
            y = acc_s[...] if partial else x_ref[...] + 0.5 * acc_s[...]
            if target is None:
                outs[0][...] = y
                if next_gain is not None:
                    outs[1][...] = _rms(y, extra[0][...]).astype(BF16)
            else:
                dy_ref, dyb_ref, loss_ref = outs[:3]
                diff = y - extra[0][...]
                dy_ref[...] = diff * (1.0 / d)
                dyb_ref[...] = (diff * (1.0 / d)).astype(BF16)
                part = 0.5 * jnp.sum(jnp.mean(diff * diff, axis=-1, keepdims=True), axis=0, keepdims=True)

                @pl.when(i == 0)
                def _():
                    loss_ref[...] = part

                @pl.when(i > 0)
                def _():
                    loss_ref[...] += part

    row = pl.BlockSpec((tm, d), lambda i, j: (i, 0))
    vec = pl.BlockSpec((1, d), lambda i, j: (0, 0))
    tile = pl.BlockSpec((tm, tn), lambda i, j: (i, j))
    in_specs = [row, vec] + [pl.BlockSpec((None, tn, d), functools.partial(lambda i, j, k: (k, j, 0), k=k)) for k in range(3)]
    args = [x, gain, w3, w3, w3]
    if target is None:
        out_specs, out_shape = [row], [SDS((s, d), F32)]
        if next_gain is not None:
            in_specs.append(vec)
            args.append(next_gain)
            out_specs.append(row)
            out_shape.append(SDS((s, d), BF16))
    else:
        in_specs.append(row)
        args.append(target)
        out_specs = [row, row, pl.BlockSpec((1, 1), lambda i, j: (0, 0))]
        out_shape = [SDS((s, d), F32), SDS((s, d), BF16), SDS((1, 1), F32)]
    if start is None:
        out_specs.append(row)
        out_shape.append(SDS((s, d), BF16))
    else:
        in_specs += [row, row]
        args += list(start)
    *head, g, u = pl.pallas_call(
        body, name=f"{tag}_fwd", grid=(s // tm, nj), in_specs=in_specs,
        out_specs=out_specs + [tile, tile],
        out_shape=out_shape + [SDS((s, f), BF16), SDS((s, f), BF16)],
        scratch_shapes=[pltpu.VMEM((tm, d), F32)],
        compiler_params=_params("arbitrary", "arbitrary"),
    )(*args)
    if start is None:
        *head, h = head
    else:
        h = start[0]
    return head, (h, g, u)


def _ffn_bwd_part(dyb, w3, saved, first, count, dh_init, *, name):
    h, g, u = saved
    s, d = h.shape
    tn = FFN_TN

    def body(*refs):
        if dh_init is None:
            dy_ref, h_ref, wd_ref, wg_ref, wu_ref, g_ref, u_ref, dh_ref, dw3_ref, dg_s, du_s, a_s = refs
        else:
            dy_ref, h_ref, wd_ref, wg_ref, wu_ref, g_ref, u_ref, dh0_ref, dh_ref, dw3_ref, dg_s, du_s, a_s = refs
        j = pl.program_id(0)

        @pl.when(j == 0)
        def _():
            dh_ref[...] = jnp.zeros_like(dh_ref) if dh_init is None else dh0_ref[...]
            for ref in (dg_s, du_s, a_s):
                ref[...] = jnp.zeros_like(ref)

        now, before = j % 2, 1 - j % 2
        dyv = dy_ref[...]
        hv = h_ref[...]
        dg, du, a = dg_s[before], du_s[before], a_s[before]
        dh_ref[...] += _mm_raw(dg, wg_ref[...]) + _mm_raw(du, wu_ref[...])
        dw3_ref[0] = _mm_raw(dg, hv, True, False).astype(BF16)
        dw3_ref[1] = _mm_raw(du, hv, True, False).astype(BF16)
        dw3_ref[2] = (_mm_raw(a, dyv, True, False) * 0.5).astype(BF16)

        da = _mm_raw(dyv, wd_ref[...], False, True) * 0.5
        a, vjp = jax.vjp(_silu_mul, g_ref[...].astype(F32), u_ref[...].astype(F32))
        dg, du = vjp(da)
        dg_s[now] = dg.astype(BF16)
        du_s[now] = du.astype(BF16)
        a_s[now] = a.astype(BF16)

    this = lambda j: first + jnp.minimum(j, count - 1)
    last = lambda j: first + jnp.maximum(j - 1, 0)
    full = pl.BlockSpec((s, d), lambda j: (0, 0))
    once = pl.BlockSpec((s, d), lambda j: (0, 0), pipeline_mode=pl.Buffered(1))
    tile = pl.BlockSpec((s, tn), lambda j: (0, this(j)))
    in_specs = [once, once, pl.BlockSpec((None, tn, d), lambda j: (2, this(j), 0)),
                pl.BlockSpec((None, tn, d), lambda j: (0, last(j), 0)), pl.BlockSpec((None, tn, d), lambda j: (1, last(j), 0)),
                tile, tile]
    args = [dyb, h, w3, w3, w3, g, u]
    if dh_init is not None:
        in_specs.append(once)
        args.append(dh_init)
    return pl.pallas_call(
        body, name=name, grid=(count + 1,), in_specs=in_specs,
        out_specs=[full, pl.BlockSpec((3, tn, d), lambda j: (0, jnp.maximum(j - 1, 0), 0))],
        out_shape=[SDS((s, d), F32), SDS((3, count * tn, d), BF16)],
        scratch_shapes=[pltpu.VMEM((2, s, tn), BF16)] * 3,
        compiler_params=_params("arbitrary"),
    )(*args)


def _bucket_table():
    qi = np.arange(BLOCK)[:, None]
    kj = np.arange(2 * BLOCK)[None, :]
    dist = np.maximum(qi + BLOCK - kj, 0)
    max_exact = N_BUCKETS // 2
    d = np.maximum(dist, 1).astype(np.float32)
    large = max_exact + (np.log(d / np.float32(max_exact)) / np.float32(math.log(MAX_DISTANCE / max_exact))
                         * np.float32(N_BUCKETS - max_exact)).astype(np.int32)
    large = np.minimum(large, N_BUCKETS - 1)
    band = np.where(dist < max_exact, dist, large).astype(np.int32)
    return np.where(np.tril(np.ones((BLOCK, BLOCK), bool)), band[:, BLOCK:], band[:, :BLOCK])


SWA_STACK = SWA_GROUP * BLOCK


def _swa_masks(n):
    qi = lax.broadcasted_iota(jnp.int32, (SWA_STACK, BLOCK), 0) % BLOCK
    kj = lax.broadcasted_iota(jnp.int32, (SWA_STACK, BLOCK), 1)
    own = kj <= qi
    return own, own | (n > 0)


def _swa_group(q, kp, kc, vp, vc, qg, kg, sink, bias, own, valid):
    qn = _rms(q, qg)
    s = jnp.where(own, _mm(qn, _rms(kc, kg), False, True), _mm(qn, _rms(kp, kg), False, True))
    s = s * (HEAD_DIM ** -0.5) + bias
    s = jnp.where(valid, s, -jnp.inf)
    m = lax.stop_gradient(jnp.maximum(jnp.max(s, axis=-1, keepdims=True), sink))
    p = jnp.exp(s - m)
    p = p / (jnp.sum(p, axis=-1, keepdims=True) + jnp.exp(sink - m))
    return _mm(jnp.where(own, p, 0.0), vc) + _mm(jnp.where(own, 0.0, p), vp)


def _swa_bias_table(rb_ref, bucket, bias_s):
    for h in range(SWA_HEADS):
        acc = jnp.zeros((BLOCK, BLOCK), F32)
        for b in range(N_BUCKETS):
            acc = jnp.where(bucket == b, rb_ref[b, h], acc)
        bias_s[h // SWA_GROUP, (h % SWA_GROUP) * BLOCK:(h % SWA_GROUP + 1) * BLOCK, :] = acc


def _swa_stack(ref, g):
    return jnp.concatenate([ref[:, (g * SWA_GROUP + hh) * HEAD_DIM:(g * SWA_GROUP + hh + 1) * HEAD_DIM]
                            for hh in range(SWA_GROUP)], axis=0)


def _swa_unstack(ref, g, stacked):
    for hh in range(SWA_GROUP):
        h = g * SWA_GROUP + hh
        ref[:, h * HEAD_DIM:(h + 1) * HEAD_DIM] = stacked[hh * BLOCK:(hh + 1) * BLOCK]


def _swa_sink_column(sink_ref, g):
    head = lax.broadcasted_iota(jnp.int32, (SWA_STACK, 1), 0) // BLOCK
    col = jnp.zeros((SWA_STACK, 1), F32)
    for hh in range(SWA_GROUP):
        col = jnp.where(head == hh, sink_ref[g * SWA_GROUP + hh], col)
    return col


def _swa_band(kvp_ref, kvc_ref, g):
    k = slice(g * HEAD_DIM, (g + 1) * HEAD_DIM)
    v = slice(SWA_KV_W + g * HEAD_DIM, SWA_KV_W + (g + 1) * HEAD_DIM)
    return kvp_ref[:, k], kvc_ref[:, k], kvp_ref[:, v], kvc_ref[:, v]


def _swa_specs(order):
    kvc = COL_SKV // (2 * SWA_KV_W)
    return [
        pl.BlockSpec((BLOCK, SWA_Q_W), lambda t: (order(t), 0)),
        pl.BlockSpec((BLOCK, 2 * SWA_KV_W), lambda t: (jnp.maximum(order(t) - 1, 0), kvc)),
        pl.BlockSpec((BLOCK, 2 * SWA_KV_W), lambda t: (order(t), kvc)),
        pl.BlockSpec((1, HEAD_DIM), lambda t: (0, 0)),
        pl.BlockSpec((1, HEAD_DIM), lambda t: (0, 0)),
        pl.BlockSpec(memory_space=pltpu.SMEM),
        pl.BlockSpec(memory_space=pltpu.SMEM),
        pl.BlockSpec((BLOCK, BLOCK), lambda t: (0, 0)),
    ]


def _swa_fwd(p, qg, kg, sinks, rel_bias, *, name):
    s = p.shape[0]
    nb = s // BLOCK

    def body(q_ref, kvp_ref, kvc_ref, qg_ref, kg_ref, sink_ref, rb_ref, bucket_ref, y_ref, bias_s):
        n = pl.program_id(0)

        @pl.when(n == 0)
        def _():
            _swa_bias_table(rb_ref, bucket_ref[...], bias_s)

        own, valid = _swa_masks(n)
        for g in range(SWA_KV_HEADS):
            out = _swa_group(_swa_stack(q_ref, g), *_swa_band(kvp_ref, kvc_ref, g), qg_ref[...], kg_ref[...],
                             _swa_sink_column(sink_ref, g), bias_s[g], own, valid)
            _swa_unstack(y_ref, g, out)

    return pl.pallas_call(
        body, name=name, grid=(nb,), in_specs=_swa_specs(lambda t: t),
        out_specs=pl.BlockSpec((BLOCK, SWA_Q_W), lambda t: (t, 0)), out_shape=SDS((s, SWA_Q_W), F32),
        scratch_shapes=[pltpu.VMEM((SWA_KV_HEADS, SWA_STACK, BLOCK), F32)],
        compiler_params=_params("arbitrary"),
    )(p, p, p, qg, kg, sinks, rel_bias, jnp.asarray(_bucket_table()))


def _swa_bwd(p, qg, kg, sinks, rel_bias, dy_all, *, name):
    s = p.shape[0]
    nb = s // BLOCK

    def body(q_ref, kvp_ref, kvc_ref, qg_ref, kg_ref, sink_ref, rb_ref, bucket_ref, dy_ref,
             dq_ref, dkv_ref, dqg_ref, dkg_ref, dsink_ref, drb_ref, bias_s, dbias_s, carry_s):
        t = pl.program_id(0)
        n = nb - 1 - t

        @pl.when(t == 0)
        def _():
            _swa_bias_table(rb_ref, bucket_ref[...], bias_s)
            dbias_s[...] = jnp.zeros_like(dbias_s)
            carry_s[...] = jnp.zeros_like(carry_s)
            dqg_ref[...] = jnp.zeros_like(dqg_ref)
            dkg_ref[...] = jnp.zeros_like(dkg_ref)
            dsink_ref[...] = jnp.zeros_like(dsink_ref)
            drb_ref[...] = jnp.zeros_like(drb_ref)

        own, valid = _swa_masks(n)
        lane = lax.broadcasted_iota(jnp.int32, (1, BLOCK), 1)
        dqg = jnp.zeros((1, HEAD_DIM), F32)
        dkg = jnp.zeros((1, HEAD_DIM), F32)
        dsink_vec = jnp.zeros((1, BLOCK), F32)
        for g in range(SWA_KV_HEADS):
            _, vjp = jax.vjp(functools.partial(_swa_group, own=own, valid=valid), _swa_stack(q_ref, g),
                             *_swa_band(kvp_ref, kvc_ref, g), qg_ref[...], kg_ref[...], _swa_sink_column(sink_ref, g),
                             bias_s[g])
            dq, dkp, dkc, dvp, dvc, dqg_g, dkg_g, dsink_col, dbias = vjp(_swa_stack(dy_ref, g))
            _swa_unstack(dq_ref, g, dq)
            dqg += dqg_g
            dkg += dkg_g
            dbias_s[g] += dbias
            for hh in range(SWA_GROUP):
                dsink_h = jnp.sum(dsink_col[hh * BLOCK:(hh + 1) * BLOCK], axis=0, keepdims=True)
                dsink_vec += jnp.where(lane == g * SWA_GROUP + hh, dsink_h, 0.0)
            lo = g * HEAD_DIM
            dkv_ref[:, lo:lo + HEAD_DIM] = dkc + carry_s[g]
            carry_s[g] = dkp
            lo += SWA_KV_W
            dkv_ref[:, lo:lo + HEAD_DIM] = dvc + carry_s[SWA_KV_HEADS + g]
            carry_s[SWA_KV_HEADS + g] = dvp
        dqg_ref[...] += dqg
        dkg_ref[...] += dkg
        dsink_ref[...] += dsink_vec

        @pl.when(t == nb - 1)
        def _():
            bucket = bucket_ref[...]
            row = lax.broadcasted_iota(jnp.int32, (N_BUCKETS, BLOCK), 0)
            col = lax.broadcasted_iota(jnp.int32, (N_BUCKETS, BLOCK), 1)
            acc = jnp.zeros((N_BUCKETS, BLOCK), F32)
            for h in range(SWA_HEADS):
                dbias = dbias_s[h // SWA_GROUP, (h % SWA_GROUP) * BLOCK:(h % SWA_GROUP + 1) * BLOCK, :]
                for b in range(N_BUCKETS):
                    part = jnp.sum(jnp.where(bucket == b, dbias, 0.0), axis=1, keepdims=True)
                    val = jnp.sum(part, axis=0, keepdims=True)
                    acc = acc + jnp.where((row == b) & (col == h), val, 0.0)
            drb_ref[...] = acc

    order = lambda t: nb - 1 - t
    vec = pl.BlockSpec((1, HEAD_DIM), lambda t: (0, 0))
    return pl.pallas_call(
        body, name=name, grid=(nb,),
        in_specs=_swa_specs(order) + [pl.BlockSpec((BLOCK, SWA_Q_W), lambda t: (order(t), 0))],
        out_specs=[pl.BlockSpec((BLOCK, SWA_Q_W), lambda t: (order(t), 0)),
                   pl.BlockSpec((BLOCK, 2 * SWA_KV_W), lambda t: (order(t), 0)),
                   vec, vec, pl.BlockSpec((1, BLOCK), lambda t: (0, 0)),
                   pl.BlockSpec((N_BUCKETS, BLOCK), lambda t: (0, 0))],
        out_shape=[SDS((s, SWA_Q_W), F32), SDS((s, 2 * SWA_KV_W), F32), SDS((1, HEAD_DIM), F32),
                   SDS((1, HEAD_DIM), F32), SDS((1, BLOCK), F32), SDS((N_BUCKETS, BLOCK), F32)],
        scratch_shapes=[pltpu.VMEM((SWA_KV_HEADS, SWA_STACK, BLOCK), F32),
                        pltpu.VMEM((SWA_KV_HEADS, SWA_STACK, BLOCK), F32),
                        pltpu.VMEM((2 * SWA_KV_HEADS, BLOCK, HEAD_DIM), F32)],
        compiler_params=_params("arbitrary"),
    )(p, p, p, qg, kg, sinks, rel_bias, jnp.asarray(_bucket_table()), dy_all)


def _mem_head(q, k, v, qg, kg):
    qn = _rms(q, qg)
    kn = _rms(k, kg)
    s = _mm(qn, kn, False, True) * (HEAD_DIM ** -0.5)
    m = lax.stop_gradient(jnp.max(s, axis=-1, keepdims=True))
    e = jnp.exp(s - m)
    return _mm(e / jnp.sum(e, axis=-1, keepdims=True), v)


def _mem_fwd(p, kv, qg, kg, *, tq, name):
    s = p.shape[0]
    m = kv.shape[0]

    def body(q_ref, kv_ref, qg_ref, kg_ref, y_ref):
        for h in range(MEM_HEADS):
            cols = slice(h * HEAD_DIM, (h + 1) * HEAD_DIM)
            vcols = slice(MEM_Q_W + h * HEAD_DIM, MEM_Q_W + (h + 1) * HEAD_DIM)
            y_ref[:, cols] = _mem_head(q_ref[:, cols], kv_ref[:, cols], kv_ref[:, vcols], qg_ref[...], kg_ref[...])

    vec = pl.BlockSpec((1, HEAD_DIM), lambda t: (0, 0))
    return pl.pallas_call(
        body, name=name, grid=(s // tq,),
        in_specs=[pl.BlockSpec((tq, MEM_Q_W), lambda t: (t, COL_MQ // MEM_Q_W)),
                  pl.BlockSpec((m, 2 * MEM_Q_W), lambda t: (0, 0)), vec, vec],
        out_specs=pl.BlockSpec((tq, MEM_Q_W), lambda t: (t, 0)), out_shape=SDS((s, MEM_Q_W), F32),
        compiler_params=_params("parallel"),
    )(p, kv, qg, kg)


def _mem_bwd(p, kv, qg, kg, dy_all, *, tq, name):
    s = p.shape[0]
    m = kv.shape[0]

    def body(q_ref, kv_ref, qg_ref, kg_ref, dy_ref, dq_ref, dkv_ref, dqg_ref, dkg_ref):
        @pl.when(pl.program_id(0) == 0)
        def _():
            dkv_ref[...] = jnp.zeros_like(dkv_ref)
            dqg_ref[...] = jnp.zeros_like(dqg_ref)
            dkg_ref[...] = jnp.zeros_like(dkg_ref)

        dqg = jnp.zeros((1, HEAD_DIM), F32)
        dkg = jnp.zeros((1, HEAD_DIM), F32)
        for h in range(MEM_HEADS):
            cols = slice(h * HEAD_DIM, (h + 1) * HEAD_DIM)
            vcols = slice(MEM_Q_W + h * HEAD_DIM, MEM_Q_W + (h + 1) * HEAD_DIM)
            _, vjp = jax.vjp(_mem_head, q_ref[:, cols], kv_ref[:, cols], kv_ref[:, vcols], qg_ref[...], kg_ref[...])
            dq, dk, dv, dqg_h, dkg_h = vjp(dy_ref[:, cols])
            dq_ref[:, cols] = dq
            dkv_ref[:, cols] += dk
            dkv_ref[:, vcols] += dv
            dqg += dqg_h
            dkg += dkg_h
        dqg_ref[...] += dqg
        dkg_ref[...] += dkg

    vec = pl.BlockSpec((1, HEAD_DIM), lambda t: (0, 0))
    full = pl.BlockSpec((m, 2 * MEM_Q_W), lambda t: (0, 0))
    dy_col = (SWA_Q_W + GLA_V_W) // MEM_Q_W
    return pl.pallas_call(
        body, name=name, grid=(s // tq,),
        in_specs=[pl.BlockSpec((tq, MEM_Q_W), lambda t: (t, COL_MQ // MEM_Q_W)), full, vec, vec,
                  pl.BlockSpec((tq, MEM_Q_W), lambda t: (t, dy_col))],
        out_specs=[pl.BlockSpec((tq, MEM_Q_W), lambda t: (t, 0)), full, vec, vec],
        out_shape=[SDS((s, MEM_Q_W), F32), SDS((m, 2 * MEM_Q_W), F32), SDS((1, HEAD_DIM), F32), SDS((1, HEAD_DIM), F32)],
        compiler_params=_params("arbitrary"),
    )(p, kv, qg, kg, dy_all)


GLA_ROWS = 256


GLA_GROUP = 4


def _gla_consts():
    c, h, r = GLA_CHUNK, GLA_HEADS, GLA_GROUP * GLA_CHUNK
    i2 = lax.broadcasted_iota(jnp.int32, (c, c), 0)
    j2 = lax.broadcasted_iota(jnp.int32, (c, c), 1)
    slab_q = lax.broadcasted_iota(jnp.int32, (h, r, GLA_QK_W), 0)
    lane_q = lax.broadcasted_iota(jnp.int32, (h, r, GLA_QK_W), 2)
    row_a = lax.broadcasted_iota(jnp.int32, (h * r, r), 0) % r
    col_a = lax.broadcasted_iota(jnp.int32, (h * r, r), 1)
    slab_o = lax.broadcasted_iota(jnp.int32, (h, r, GLA_V_W), 0)
    lane_o = lax.broadcasted_iota(jnp.int32, (h, r, GLA_V_W), 2)
    row_s = lax.broadcasted_iota(jnp.int32, (GLA_V_W, GLA_QK_W), 0)
    col_s = lax.broadcasted_iota(jnp.int32, (GLA_V_W, GLA_QK_W), 1)
    return dict(
        ltri=(j2 <= i2).astype(F32),
        m_q=(slab_q == lane_q // GLA_DK).astype(F32),
        causal=(col_a <= row_a) & (col_a // c == row_a // c),
        m_o=(slab_o == lane_o // GLA_DV).astype(F32),
        m_s=(row_s // GLA_DV == col_s // GLA_DK).astype(F32),
    )


def _gla_step(q, k, v, z, bg, st, c):
    h = GLA_HEADS
    kt, ka, qt, qe, decay = [], [], [], [], []
    for qc, kc, zc in zip(q, k, z):
        la = _log_sigmoid(zc + bg) * (1.0 / GLA_TAU)
        b = _mmf(c["ltri"], la)
        bl = jnp.sum(la, axis=0, keepdims=True)
        qs = qc * (GLA_DK ** -0.5)
        kt.append(kc * jnp.exp(bl - b))
        ka.append(kc * jnp.exp(0.5 * bl - b))
        qt.append(qs * jnp.exp(b - 0.5 * bl))
        qe.append(qs * jnp.exp(b))
        decay.append(jnp.exp(bl))
    o_intra = []
    rows = GLA_GROUP * GLA_CHUNK
    for lo in range(0, len(q), GLA_GROUP):
        qt_all, kt_all, v_all = (jnp.concatenate(parts[lo:lo + GLA_GROUP], axis=0) for parts in (qt, ka, v))
        q_stack = (jnp.broadcast_to(qt_all[None], (h, rows, GLA_QK_W)) * c["m_q"]).reshape(h * rows, GLA_QK_W)
        a = jnp.where(c["causal"], _mm3(q_stack, kt_all, False, True), 0.0)
        o_stack = _mm(a, v_all)
        o_intra.append(jnp.sum(o_stack.reshape(h, rows, GLA_V_W) * c["m_o"], axis=0))
    o_intra = jnp.concatenate(o_intra, axis=0)
    o_inter = []
    for qec, ktc, vc, dc in zip(qe, kt, v, decay):
        o_inter.append(_mm(qec, st, False, True))
        st = st * dc + _mm(vc, ktc, True, False) * c["m_s"]
    return o_intra + jnp.concatenate(o_inter, axis=0), st


def _gla_post(o, gg, gain, g64):
    ms = _mmf(o * o, g64) * (1.0 / GLA_DV)
    return o * lax.rsqrt(ms + EPS) * gain * jax.nn.silu(gg)


def _gla_g64():
    r = lax.broadcasted_iota(jnp.int32, (GLA_V_W, GLA_V_W), 0)
    c = lax.broadcasted_iota(jnp.int32, (GLA_V_W, GLA_V_W), 1)
    return (r // GLA_DV == c // GLA_DV).astype(F32)


def _gla_in_specs(order):
    r = GLA_ROWS
    return [
        pl.BlockSpec((r, GLA_QK_W), lambda t: (order(t), COL_GQ // GLA_QK_W)),
        pl.BlockSpec((r, GLA_QK_W), lambda t: (order(t), COL_GK // GLA_QK_W)),
        pl.BlockSpec((r, GLA_V_W), lambda t: (order(t), COL_GV // GLA_V_W)),
        pl.BlockSpec((r, GLA_V_W), lambda t: (order(t), COL_GG // GLA_V_W)),
        pl.BlockSpec((r, GLA_QK_W), lambda t: (order(t), 0)),
        pl.BlockSpec((1, GLA_QK_W), lambda t: (0, 0)),
        pl.BlockSpec((1, GLA_V_W), lambda t: (0, 0)),
    ]


def _gla_pieces(q_ref, k_ref, v_ref, z_ref, cps):
    chunk = lambda ref: [ref[ci * GLA_CHUNK:(ci + 1) * GLA_CHUNK, :] for ci in range(cps)]
    return chunk(q_ref), chunk(k_ref), chunk(v_ref), chunk(z_ref)


def _gla_fwd(p, z, bg, gain, *, name):
    s = p.shape[0]
    r = GLA_ROWS
    cps = r // GLA_CHUNK

    def body(q_ref, k_ref, v_ref, gg_ref, z_ref, bg_ref, gain_ref, y_ref, oraw_ref, stsave_ref, st_s):
        @pl.when(pl.program_id(0) == 0)
        def _():
            st_s[...] = jnp.zeros_like(st_s)

        st = st_s[...]
        stsave_ref[0] = st
        o, st = _gla_step(*_gla_pieces(q_ref, k_ref, v_ref, z_ref, cps), bg_ref[...], st, _gla_consts())
        oraw_ref[...] = o
        st_s[...] = st
        y_ref[...] = _gla_post(o, gg_ref[...], gain_ref[...], _gla_g64())

    rowv = pl.BlockSpec((r, GLA_V_W), lambda t: (t, 0))
    return pl.pallas_call(
        body, name=name, grid=(s // r,), in_specs=_gla_in_specs(lambda t: t),
        out_specs=[rowv, rowv, pl.BlockSpec((1, GLA_V_W, GLA_QK_W), lambda t: (t, 0, 0))],
        out_shape=[SDS((s, GLA_V_W), F32), SDS((s, GLA_V_W), F32), SDS((s // r, GLA_V_W, GLA_QK_W), F32)],
        scratch_shapes=[pltpu.VMEM((GLA_V_W, GLA_QK_W), F32)],
        compiler_params=_params("arbitrary"),
    )(p, p, p, p, z, bg, gain)


def _gla_bwd(p, z, bg, gain, oraw, stsave, dy_all, *, name):
    s = p.shape[0]
    r = GLA_ROWS
    cps = r // GLA_CHUNK
    nsteps = s // r
    w_qkvg = 2 * GLA_QK_W + 2 * GLA_V_W

    def body(q_ref, k_ref, v_ref, gg_ref, z_ref, bg_ref, gain_ref, oraw_ref, stsave_ref, dy_ref,
             dqkvg_ref, dz_ref, dbg_ref, dgain_ref, dst_s):
        @pl.when(pl.program_id(0) == 0)
        def _():
            dst_s[...] = jnp.zeros_like(dst_s)
            dbg_ref[...] = jnp.zeros_like(dbg_ref)
            dgain_ref[...] = jnp.zeros_like(dgain_ref)

        _, vjp = jax.vjp(functools.partial(_gla_post, g64=_gla_g64()), oraw_ref[...], gg_ref[...], gain_ref[...])
        do, dgg, dgain = vjp(dy_ref[...])
        dqkvg_ref[:, 2 * GLA_QK_W + GLA_V_W:] = dgg
        dgain_ref[...] += dgain
        _, vjp = jax.vjp(functools.partial(_gla_step, c=_gla_consts()), *_gla_pieces(q_ref, k_ref, v_ref, z_ref, cps),
                         bg_ref[...], stsave_ref[0])
        dq, dk, dv, dz, dbg, dst = vjp((do, dst_s[...]))
        for ci in range(cps):
            rows = slice(ci * GLA_CHUNK, (ci + 1) * GLA_CHUNK)
            dqkvg_ref[rows, 0:GLA_QK_W] = dq[ci]
            dqkvg_ref[rows, GLA_QK_W:2 * GLA_QK_W] = dk[ci]
            dqkvg_ref[rows, 2 * GLA_QK_W:2 * GLA_QK_W + GLA_V_W] = dv[ci]
            dz_ref[rows, :] = dz[ci]
        dst_s[...] = dst
        dbg_ref[...] += dbg

    order = lambda t: nsteps - 1 - t
    rowv = pl.BlockSpec((r, GLA_V_W), lambda t: (order(t), 0))
    return pl.pallas_call(
        body, name=name, grid=(nsteps,),
        in_specs=_gla_in_specs(order) + [
            rowv, pl.BlockSpec((1, GLA_V_W, GLA_QK_W), lambda t: (order(t), 0, 0)),
            pl.BlockSpec((r, GLA_V_W), lambda t: (order(t), SWA_Q_W // GLA_V_W))],
        out_specs=[pl.BlockSpec((r, w_qkvg), lambda t: (order(t), 0)), pl.BlockSpec((r, GLA_QK_W), lambda t: (order(t), 0)),
                   pl.BlockSpec((1, GLA_QK_W), lambda t: (0, 0)), pl.BlockSpec((1, GLA_V_W), lambda t: (0, 0))],
        out_shape=[SDS((s, w_qkvg), F32), SDS((s, GLA_QK_W), F32), SDS((1, GLA_QK_W), F32), SDS((1, GLA_V_W), F32)],
        scratch_shapes=[pltpu.VMEM((GLA_V_W, GLA_QK_W), F32)],
        compiler_params=_params("arbitrary"),
    )(p, p, p, p, z, bg, gain, oraw, stsave, dy_all)


def _local_step(x, mem, target, small, big, on_grads):
    g1, gmix, gmem, g2, sqg, skg, sinks, rel_bias, wgu, bg, gla_gain, mqg, mkg = small
    (w3_1a, w3_1b), gather_mix, gather_ffn2 = big
    wgu_pad = jnp.zeros((GLA_QK_W, GLA_QK_W), BF16).at[:GLA_RANK].set(wgu.astype(BF16))
    gain256 = jnp.tile(gla_gain, (1, GLA_HEADS))

    (part,), saved1a = _ffn_fwd(x, g1, w3_1a, "ffn1a", partial=True)
    win_p, wkv, wout = gather_mix(part)
    (x1, h), saved1b = _ffn_fwd(x, g1, w3_1b, "ffn1b", next_gain=gmix, start=(saved1a[0], part))
    w3_2 = gather_ffn2((wout, x1))
    p = _matmul(h, win_p, tm=512, tn=IN_W_PAD, name="mix_in")
    hm = _rms_fwd(mem, gmem, tm=256, name="mem_rms")
    kv = _matmul(hm, wkv, tm=256, tn=512, name="mem_kv")
    p_glr = p[:, COL_GLR:]
    z = _matmul(p_glr, wgu_pad, tm=1024, tn=GLA_QK_W, name="gla_gate")
    y_swa = _swa_fwd(p, sqg, skg, sinks, rel_bias, name="swa_fwd")
    y_gla, oraw, stsave = _gla_fwd(p, z, bg, gain256, name="gla_fwd")
    y_mem = _mem_fwd(p, kv, mqg, mkg, tq=512, name="mem_fwd")
    x2 = _matmul([y_swa, y_gla, y_mem], wout, b_blocks=[0, 2, 3], tm=512, tn=1024, res=x1, name="mix_out")
    (dy, dyb, loss), saved2 = _ffn_fwd(x2, g2, w3_2, "ffn2", target=target)

    dh2, dw3_2 = _ffn_bwd_part(dyb, w3_2, saved2, 0, w3_2.shape[1] // FFN_TN, None, name="ffn2_bwd")
    dx2, dx2b, dg2 = _rms_bwd(x2, g2, dh2, dy, tm=512, name="ffn2_drms")
    dx2b = on_grads("ffn2", [dw3_2], dx2b)
    dy_all = _matmul(dx2b, wout, tb=True, tm=512, tn=1024, name="mix_dy")
    dwout = _dw_rows([y_swa, y_gla, y_mem], dx2b, tk=512, name="mix_dw_out")
    dq_swa, dkv_swa, dsqg, dskg, dsink, drb = _swa_bwd(p, sqg, skg, sinks, rel_bias, dy_all, name="swa_bwd")
    dqkvg, dz, dbg, dgain256 = _gla_bwd(p, z, bg, gain256, oraw, stsave, dy_all, name="gla_bwd")
    dmq, dkv_mem, dmqg, dmkg = _mem_bwd(p, kv, mqg, mkg, dy_all, tq=512, name="mem_bwd")
    dglr = _matmul(dz, wgu_pad, tb=True, tm=1024, tn=GLA_QK_W, name="gla_gate_dx")
    dwgu_pad = _matmul(p_glr, dz, ta=True, tm=GLA_QK_W, tn=GLA_QK_W, name="gla_gate_dw")
    dp = [dq_swa, dkv_swa, dqkvg, dmq, dglr]
    dwin_p = _dw_parts(h, dp, tk=512, name="mix_dw_in")
    dx1, dx1b, dgmix = _rms_bwd(x1, gmix, (dp, win_p), dx2, tm=512, name="mix_dh_drms")
    dwkv = _matmul(hm, dkv_mem, ta=True, tm=512, tn=512, out_dtype=BF16, name="mem_dw_kv")
    dx1b = on_grads("mix", (dwin_p, dwkv, dwout), dx1b)
    _, _, dgmem = _rms_bwd(mem, gmem, (dkv_mem, wkv), None, tm=256, name="mem_dh_drms")
    dh1, dw3_1a = _ffn_bwd_part(dx1b, w3_1a, saved1a, 0, w3_1a.shape[1] // FFN_TN, None, name="ffn1_bwd_a")
    dgla_gain = dgain256.reshape(GLA_HEADS, GLA_DV).sum(axis=0, keepdims=True)
    dsmall = [dgmix, dgmem, dg2, dsqg, dskg, dsink[:, :SWA_HEADS], drb[:, :SWA_HEADS].T, dwgu_pad[:GLA_RANK], dbg,
              dgla_gain, dmqg, dmkg, loss]
    dh1, dgmem, dwgu_pad = on_grads("ffn1a", [dw3_1a], (dh1, dgmem, dwgu_pad), small=dsmall)
    dh1, dw3_1b = _ffn_bwd_part(dx1b, w3_1b, saved1b, 0, w3_1b.shape[1] // FFN_TN, dh1, name="ffn1_bwd_b")
    dx, _, dg1 = _rms_bwd(x, g1, dh1, dx1, tm=512, name="ffn1_drms")
    on_grads("ffn1b", [dw3_1b], None, small=[dg1])
    return dx


def _mesh_place():
    x, y, c = lax.axis_index("x"), lax.axis_index("y"), lax.axis_index("c")
    other_chips = [(1 - x, y), (x, 1 - y), (1 - x, 1 - y)]
    return x, y, c, other_chips


def _handshake(peers):
    barrier = pltpu.get_barrier_semaphore()
    for peer in peers:
        pl.semaphore_signal(barrier, inc=1, device_id=peer, device_id_type=MESH)
    pl.semaphore_wait(barrier, len(peers))


def _sequencer_call(body, operands, out_shapes, sems, *, name, collective_id):
    return pl.kernel(
        body, name=name, out_type=out_shapes, mesh=plsc.ScalarSubcoreMesh(axis_name="sequencer", num_cores=1),
        scratch_types=sems, compiler_params=pltpu.CompilerParams(collective_id=collective_id),
    )(*operands)


def _window(ref, kind, slot, shape):
    if kind == "row":
        rows = pl.ds(pl.multiple_of(slot * shape[-2], 8), shape[-2])
        return ref.at[(slice(None),) * (len(shape) - 2) + (rows,)]
    return ref.at[slot]


def _gathered(shape, kind):
    if kind == "row":
        return tuple(shape[:-2]) + (N_DEV * shape[-2], shape[-1])
    return (N_DEV,) + tuple(shape)


def _half(view, hf):
    if len(view.shape) == 4:
        return view.at[:, hf]
    n = view.shape[-2] // 2
    return view.at[(slice(None),) * (len(view.shape) - 2) + (pl.ds(hf * n, n),)]


def _all_gather(shards, kinds, *, name, collective_id):
    nt = len(shards)

    def body(*refs):
        x_refs, o_refs = refs[:nt], refs[nt:2 * nt]
        send_sems, recv_sems, local_sems = refs[2 * nt:]
        x, y, c, _ = _mesh_place()
        me, sibling, xn, yn, diag = (x, y, c), (x, y, 1 - c), (1 - x, y, c), (x, 1 - y, c), (1 - x, 1 - y, c)
        _handshake([sibling, xn, yn])

        def win(t, block):
            bx, by, bc = block
            return _window(o_refs[t], kinds[t], 4 * bx + 2 * by + bc, shards[t].shape)

        def copy(k, t, src, dst, to):
            return pltpu.make_async_remote_copy(src_ref=src, dst_ref=dst, send_sem=send_sems.at[k, t],
                                                recv_sem=recv_sems.at[k, t], device_id=to, device_id_type=MESH)

        def piece(k, t, block, hf, to, from_shard=False):
            dst = _half(win(t, block), hf)
            return copy(k, t, _half(x_refs[t], hf) if from_shard else dst, dst, to)

        mine = [pltpu.make_async_copy(x_refs[t], win(t, me), local_sems.at[t]) for t in range(nt)]
        sent = []

        def start(cp):
            cp.start()
            sent.append(cp)

        for cp in mine:
            cp.start()
        for t in range(nt):
            start(copy(0, t, x_refs[t], win(t, me), sibling))
        for hf_x, hf_y in ((0, 1), (1, 0)):
            for t in range(nt):
                start(piece(1 + hf_x, t, me, hf_x, xn, True))
                start(piece(3 + hf_y, t, me, hf_y, yn, True))
        for k, block, hf, onward, k_sib in ((1, xn, 0, (5, yn), 7), (4, yn, 1, (6, xn), 10), (2, xn, 1, None, 8),
                                           (3, yn, 0, None, 9), (5, diag, 0, None, 11), (6, diag, 1, None, 12)):
            for t in range(nt):
                piece(k, t, block, hf, me).wait_recv()
                if onward is not None:
                    start(piece(onward[0], t, block, hf, onward[1]))
                start(piece(k_sib, t, block, hf, sibling))
        for t in range(nt):
            copy(0, t, x_refs[t], win(t, sibling), me).wait_recv()
        for k_sib, block, hf in ((7, xn, 0), (10, yn, 1), (8, xn, 1), (9, yn, 0), (11, diag, 0), (12, diag, 1)):
            for t in range(nt):
                bx, by, _ = block
                piece(k_sib, t, (bx, by, 1 - c), hf, me).wait_recv()
        for cp in sent:
            cp.wait_send()
        for cp in mine:
            cp.wait()

    return _sequencer_call(
        body, shards, [SDS(_gathered(s.shape, k), s.dtype) for s, k in zip(shards, kinds)],
        [pltpu.SemaphoreType.DMA((13, nt)), pltpu.SemaphoreType.DMA((13, nt)), pltpu.SemaphoreType.DMA((nt,))],
        name=name, collective_id=collective_id)


def _part_shape(shape, kind):
    if kind == "row":
        return tuple(shape[:-2]) + (shape[-2] // N_DEV, shape[-1])
    return tuple(shape[2:])


def _pair_exchange(grads, kinds, *, name, collective_id):
    nt = len(grads)
    part = [_part_shape(g.shape, k) for g, k in zip(grads, kinds)]

    def body(*refs):
        g_refs, o_refs = refs[:nt], refs[nt:2 * nt]
        send_sems, recv_sems = refs[2 * nt:]
        x, y, c, _ = _mesh_place()
        _handshake([(x, y, 1 - c)])
        copies = []
        for t in range(nt):
            for xy in range(4):
                src = g_refs[t].at[1 - c, xy] if kinds[t] == "stack" else _window(g_refs[t], kinds[t], 2 * xy + 1 - c, part[t])
                copies.append(pltpu.make_async_remote_copy(
                    src_ref=src, dst_ref=o_refs[t].at[xy], send_sem=send_sems.at[xy, t], recv_sem=recv_sems.at[xy, t],
                    device_id=(x, y, 1 - c), device_id_type=MESH))
        for cp in copies:
            cp.start()
        for cp in copies:
            cp.wait()

    return _sequencer_call(
        body, grads, [SDS((4,) + p, g.dtype) for p, g in zip(part, grads)],
        [pltpu.SemaphoreType.DMA((4, nt)), pltpu.SemaphoreType.DMA((4, nt))], name=name, collective_id=collective_id)


def _chip_exchange(parts, small, *, name, collective_id):
    nt = len(parts)
    if small is None:
        def body_plain(*refs):
            s_refs, o_refs = refs[:nt], refs[nt:2 * nt]
            send_sems, recv_sems = refs[2 * nt:]
            x, y, c, chips = _mesh_place()
            _handshake([(*chip, c) for chip in chips])
            copies = [pltpu.make_async_remote_copy(
                src_ref=s_refs[t].at[2 * chip[0] + chip[1]], dst_ref=o_refs[t].at[j],
                send_sem=send_sems.at[j, t], recv_sem=recv_sems.at[j, t], device_id=(*chip, c), device_id_type=MESH)
                for j, chip in enumerate(chips) for t in range(nt)]
            for cp in copies:
                cp.start()
            for cp in copies:
                cp.wait()

        return _sequencer_call(
            body_plain, parts, [SDS((3,) + s.shape[1:], s.dtype) for s in parts],
            [pltpu.SemaphoreType.DMA((3, nt)), pltpu.SemaphoreType.DMA((3, nt))], name=name, collective_id=collective_id)

    def body(*refs):
        s_refs, small_ref = refs[:nt], refs[nt]
        o_refs, small_all = refs[nt + 1:2 * nt + 1], refs[2 * nt + 1]
        send_sems, recv_sems, small_send, small_recv, local_sem = refs[2 * nt + 2:]
        x, y, c, chips = _mesh_place()
        _handshake([(px, py, pc) for px in (x, 1 - x) for py in (y, 1 - y) for pc in (c, 1 - c)][1:])

        def copy(j, t, chip):
            return pltpu.make_async_remote_copy(
                src_ref=s_refs[t].at[2 * chip[0] + chip[1]], dst_ref=o_refs[t].at[j],
                send_sem=send_sems.at[j, t], recv_sem=recv_sems.at[j, t], device_id=(*chip, c), device_id_type=MESH)

        flips = [(fx, fy, fc) for fx in (0, 1) for fy in (0, 1) for fc in (0, 1)][1:]

        def small_copy(k):
            fx, fy, fc = flips[k]
            to = (x ^ fx if fx else x, y ^ fy if fy else y, c ^ fc if fc else c)
            rows = small_all.at[4 * x + 2 * y + c]
            return pltpu.make_async_remote_copy(
                src_ref=small_ref, dst_ref=rows, send_sem=small_send.at[k], recv_sem=small_recv.at[k],
                device_id=to, device_id_type=MESH)

        own = pltpu.make_async_copy(small_ref, small_all.at[4 * x + 2 * y + c], local_sem)
        own.start()
        copies = [copy(j, t, chip) for j, chip in enumerate(chips) for t in range(nt)]
        smalls = [small_copy(k) for k in range(7)]
        for cp in smalls + copies:
            cp.start()
        for cp in smalls + copies:
            cp.wait()
        own.wait()

    return _sequencer_call(
        body, list(parts) + [small],
        [SDS((3,) + s.shape[1:], s.dtype) for s in parts] + [SDS((N_DEV,) + small.shape, small.dtype)],
        [pltpu.SemaphoreType.DMA((3, nt)), pltpu.SemaphoreType.DMA((3, nt)),
         pltpu.SemaphoreType.DMA((7,)), pltpu.SemaphoreType.DMA((7,)), pltpu.SemaphoreType.DMA],
        name=name, collective_id=collective_id)


def _pair_sum(grad, theirs, kind, c, *, name):
    if kind == "row":
        r, l = theirs.shape[-2:]
        n = theirs.size // (4 * r * l)
        grad, theirs = grad.reshape(n, N_DEV * r, l), theirs.reshape(4, n, r, l)
        mine_spec = pl.BlockSpec((n, r, l), lambda xy, c_ref: (0, 2 * xy + c_ref[0], 0))
    else:
        r, l = theirs.shape[-2:]
        n = theirs.size // (4 * r * l)
        theirs = theirs.reshape(4, n, r, l)
        grad = grad.reshape(2, 4, n, r, l)
        mine_spec = pl.BlockSpec((None, None, n, r, l), lambda xy, c_ref: (c_ref[0], xy, 0, 0, 0))

    def body(c_ref, a_ref, b_ref, o_ref):
        o_ref[...] = (a_ref[...].astype(F32) + b_ref[...].astype(F32)).astype(BF16)

    part = pl.BlockSpec((None, n, r, l), lambda xy, c_ref: (xy, 0, 0, 0))
    return pl.pallas_call(
        body, name=name,
        grid_spec=pltpu.PrefetchScalarGridSpec(num_scalar_prefetch=1, grid=(4,), in_specs=[mine_spec, part], out_specs=part),
        out_shape=SDS((4, n, r, l), BF16), compiler_params=_params("parallel"),
    )(c, grad, theirs)


def _adamw(w, g, m, v):
    m = ADAM_B1 * m + (1.0 - ADAM_B1) * g
    v = ADAM_B2 * v + (1.0 - ADAM_B2) * jnp.square(g)
    m_hat = m / (1.0 - ADAM_B1 ** ADAM_STEP)
    v_hat = v / (1.0 - ADAM_B2 ** ADAM_STEP)
    delta = -ADAM_LR * (m_hat / (jnp.sqrt(v_hat) + ADAM_EPS) + ADAM_WD * w)
    return delta, m, v


def _adam_big(owns, others, mat, xy, w, m, v, *, tr, name):
    _, r, l = w.shape
    lp = owns[0].shape[-1]
    nq = len(owns)
    rows = [tr] if nq == 1 else [o.shape[-2] for o in owns]
    assert sum(rows) == tr and r % tr == 0, (r, tr, rows)

    def body(xy_ref, *refs):
        own_refs, oth_refs = refs[:nq], refs[nq:2 * nq]
        w_ref, m_ref, v_ref, g_out, d_out, m_out, v_out = refs[2 * nq:]
        parts = []
        for q in range(nq):
            gq = own_refs[q][0, 0].astype(F32)
            for j in range(3):
                gq = gq + oth_refs[q][j, 0].astype(F32)
            parts.append(gq)
        g = (parts[0] if nq == 1 else jnp.concatenate(parts, axis=0))[:, :l]
        delta, m_new, v_new = _adamw(w_ref[0], g, m_ref[0], v_ref[0])
        g_out[0] = g
        d_out[0] = delta
        m_out[0] = m_new
        v_out[0] = v_new

    blk = pl.BlockSpec((1, tr, l), lambda i, xy_ref: (0, i, 0))
    own_specs = [pl.BlockSpec((1, 1, n, lp), lambda i, xy_ref: (xy_ref[0], mat, i, 0)) for n in rows]
    oth_specs = [pl.BlockSpec((3, 1, n, lp), lambda i, xy_ref: (0, mat, i, 0)) for n in rows]
    return pl.pallas_call(
        body, name=name,
        grid_spec=pltpu.PrefetchScalarGridSpec(
            num_scalar_prefetch=1, grid=(r // tr,), in_specs=own_specs + oth_specs + [blk, blk, blk],
            out_specs=[blk, blk, blk, blk]),
        out_shape=[SDS(w.shape, F32)] * 4, compiler_params=_params("parallel"),
    )(xy, *owns, *others, w, m, v)


def _adam_cols(own, other, xy, w, m, v, *, tc, name):
    _, wc, d = w.shape
    wp = -(-wc // 128) * 128

    def body(xy_ref, own_ref, oth_ref, w_ref, m_ref, v_ref, g_out, d_out, m_out, v_out, buf):
        g = own_ref[0, 0].astype(F32)
        for j in range(3):
            g = g + oth_ref[j, 0].astype(F32)
        buf[:, :wc] = g
        buf[:, wc:] = jnp.zeros((tc, wp - wc), F32)
        g = buf[...].T[:wc]
        delta, m_new, v_new = _adamw(w_ref[0], g, m_ref[0], v_ref[0])
        g_out[0] = g
        d_out[0] = delta
        m_out[0] = m_new
        v_out[0] = v_new

    blk = pl.BlockSpec((1, wc, tc), lambda i, xy_ref: (0, 0, i))
    in_specs = [pl.BlockSpec((1, 1, tc, wc), lambda i, xy_ref: (xy_ref[0], 0, i, 0)),
                pl.BlockSpec((3, 1, tc, wc), lambda i, xy_ref: (0, 0, i, 0)), blk, blk, blk]
    return pl.pallas_call(
        body, name=name,
        grid_spec=pltpu.PrefetchScalarGridSpec(
            num_scalar_prefetch=1, grid=(d // tc,), in_specs=in_specs, out_specs=[blk, blk, blk, blk],
            scratch_shapes=[pltpu.VMEM((tc, wp), F32)]),
        out_shape=[SDS(w.shape, F32)] * 4, compiler_params=_params("parallel"),
    )(xy, own, other, w, m, v)


def _small_layout(shapes):
    out, at = [], 0
    for r, c in shapes:
        rows = c // 128 if (r == 1 and c > 128) else r
        out.append((at, rows))
        at += -(-rows // 8) * 8
    return out, at


def _pack_small(parts, *, name):
    shapes = [a.shape for a in parts]
    layout, total = _small_layout(shapes)

    def body(*refs):
        o_ref = refs[-1]
        o_ref[...] = jnp.zeros_like(o_ref)
        for x_ref, (r, c), (at, rows) in zip(refs, shapes, layout):
            if r == 1 and c > 128:
                for k in range(rows):
                    o_ref[at + k:at + k + 1, :] = x_ref[:, k * 128:(k + 1) * 128]
            else:
                o_ref[at:at + r, 0:c] = x_ref[...]

    return pl.pallas_call(body, name=name, out_shape=SDS((total, 128), F32))(*parts)


def _adam_small(g_all, ws, ms, vs, *, name):
    n = len(ws)
    shapes = [w.shape for w in ws]
    layout, _ = _small_layout(shapes)

    def body(g_ref, *refs):
        w_refs, m_refs, v_refs, outs = refs[:n], refs[n:2 * n], refs[2 * n:3 * n], refs[3 * n:]
        g_sum = g_ref[0]
        for k in range(1, N_DEV):
            g_sum = g_sum + g_ref[k]
        for i, ((r, c), (at, rows)) in enumerate(zip(shapes, layout)):
            if r == 1 and c > 128:
                g = jnp.concatenate([g_sum[at + k:at + k + 1, :] for k in range(rows)], axis=1)
            else:
                g = g_sum[at:at + r, 0:c]
            delta, m_new, v_new = _adamw(w_refs[i][...], g, m_refs[i][...], v_refs[i][...])
            for q, val in enumerate((g, delta, m_new, v_new)):
                outs[4 * i + q][...] = val

    flat = pl.pallas_call(body, name=name, out_shape=[SDS(s, F32) for s in shapes for _ in range(4)])(g_all, *ws, *ms, *vs)
    return [flat[4 * i:4 * i + 4] for i in range(n)]


def kernel(x, mem, ffn1_norm, ffn1_w_gate, ffn1_w_up, ffn1_w_down, mix_norm, mem_norm, w_in, w_mem_kv, swa_q_norm, swa_k_norm, swa_sinks, rel_bias, gla_w_gate_up, gla_b_gate, gla_out_norm, mem_q_norm, mem_k_norm, w_out, ffn2_norm, ffn2_w_gate, ffn2_w_up, ffn2_w_down, loss_target, m_ffn1_norm, m_ffn1_w_gate, m_ffn1_w_up, m_ffn1_w_down, m_mix_norm, m_mem_norm, m_w_in, m_w_mem_kv, m_swa_q_norm, m_swa_k_norm, m_swa_sinks, m_rel_bias, m_gla_w_gate_up, m_gla_b_gate, m_gla_out_norm, m_mem_q_norm, m_mem_k_norm, m_w_out, m_ffn2_norm, m_ffn2_w_gate, m_ffn2_w_up, m_ffn2_w_down, v_ffn1_norm, v_ffn1_w_gate, v_ffn1_w_up, v_ffn1_w_down, v_mix_norm, v_mem_norm, v_w_in, v_w_mem_kv, v_swa_q_norm, v_swa_k_norm, v_swa_sinks, v_rel_bias, v_gla_w_gate_up, v_gla_b_gate, v_gla_out_norm, v_mem_q_norm, v_mem_k_norm, v_w_out, v_ffn2_norm, v_ffn2_w_gate, v_ffn2_w_up, v_ffn2_w_down):
    xi, yi, ci = lax.axis_index("x"), lax.axis_index("y"), lax.axis_index("c")
    c_arr = jnp.reshape(ci, (1,)).astype(jnp.int32)
    xy_arr = jnp.reshape(2 * xi + yi, (1,)).astype(jnp.int32)
    d = x.shape[-1]

    def ffn_shards(wg_s, wu_s, wd_s):
        return jnp.concatenate([wg_s.transpose(0, 2, 1), wu_s.transpose(0, 2, 1), wd_s], axis=0).astype(BF16)

    def gather_ffn(wg_s, wu_s, wd_s, name, collective_id, after):
        w3_s, _ = lax.optimization_barrier((ffn_shards(wg_s, wu_s, wd_s), after))
        return _all_gather([w3_s], ["row"], name=name, collective_id=collective_id)[0]

    w3_s = ffn_shards(ffn1_w_gate, ffn1_w_up, ffn1_w_down)
    w3_1a = _all_gather([w3_s[:, :FFN1_FIRST]], ["row"], name="gather_ffn1a", collective_id=0)[0]
    w3_s, _ = lax.optimization_barrier((w3_s, w3_1a))
    w3_1b = _all_gather([w3_s[:, FFN1_FIRST:]], ["row"], name="gather_ffn1b", collective_id=11)[0]

    def gather_mix(after):
        mix_s = lax.optimization_barrier((w_in[0].astype(BF16), w_mem_kv[0].astype(BF16), w_out[0].astype(BF16),
                                          (w3_1b, after)))[:3]
        win_all, wkv, wout = _all_gather(list(mix_s), ["stack", "row", "row"], name="gather_mix", collective_id=1)
        return _pack_win(win_all, tr=256, name="pack_w_in"), wkv, wout

    def gather_ffn2(after):
        return gather_ffn(ffn2_w_gate, ffn2_w_up, ffn2_w_down, "gather_ffn2", 2, after)

    small_w = [ffn1_norm, mix_norm, mem_norm, ffn2_norm, swa_q_norm, swa_k_norm, swa_sinks[0], rel_bias,
               gla_w_gate_up[0], gla_b_gate, gla_out_norm, mem_q_norm, mem_k_norm]
    collective_ids = {"ffn2": (3, 4), "mix": (5, 6), "ffn1a": (7, 8), "ffn1b": (9, 10)}
    reduced, small_box = {}, {}

    def on_grads(group, grads, carry, small=None):
        if group == "mix":
            dwin_p, dwkv, dwout = grads
            grads = [_unpack_win(dwin_p, tr=256, name="unpack_dw_in"), dwkv, dwout]
            kinds = ["stack", "row", "row"]
        else:
            kinds = ["row"]
        if reduced:
            earlier = list(reduced.values())[-1][1]
            *grads, _ = lax.optimization_barrier((*grads, earlier[0]))
        id_pair, id_chip = collective_ids[group]
        from_sibling = _pair_exchange(grads, kinds, name=f"pair_exchange_{group}", collective_id=id_pair)
        chip_sums = [_pair_sum(g, theirs, k, c_arr, name=f"pair_sum_{group}_{t}")
                     for t, (g, theirs, k) in enumerate(zip(grads, from_sibling, kinds))]
        if carry is not None:
            *chip_sums, carry = lax.optimization_barrier((*chip_sums, carry))
        if small is None:
            from_chips = _chip_exchange(chip_sums, None, name=f"chip_exchange_{group}", collective_id=id_chip)
        else:
            packed = _pack_small(small, name=f"pack_small_{group}")
            *from_chips, small_all = _chip_exchange(chip_sums, packed, name=f"chip_exchange_{group}",
                                                    collective_id=id_chip)
            small_box[group] = small_all
        reduced[group] = (chip_sums, from_chips)
        return carry

    grad_x = _local_step(x[0], mem[0], loss_target[0], small_w, ((w3_1a, w3_1b), gather_mix, gather_ffn2), on_grads)

    big_w = {"ffn1_w_gate": ("ffn1", 0, 0, True, ffn1_w_gate, m_ffn1_w_gate, v_ffn1_w_gate),
             "ffn1_w_up": ("ffn1", 0, 1, True, ffn1_w_up, m_ffn1_w_up, v_ffn1_w_up),
             "ffn1_w_down": ("ffn1", 0, 2, False, ffn1_w_down, m_ffn1_w_down, v_ffn1_w_down),
             "w_in": ("mix", 0, 0, True, w_in, m_w_in, v_w_in),
             "w_mem_kv": ("mix", 1, 0, False, w_mem_kv, m_w_mem_kv, v_w_mem_kv),
             "w_out": ("mix", 2, 0, False, w_out, m_w_out, v_w_out),
             "ffn2_w_gate": ("ffn2", 0, 0, True, ffn2_w_gate, m_ffn2_w_gate, v_ffn2_w_gate),
             "ffn2_w_up": ("ffn2", 0, 1, True, ffn2_w_up, m_ffn2_w_up, v_ffn2_w_up),
             "ffn2_w_down": ("ffn2", 0, 2, False, ffn2_w_down, m_ffn2_w_down, v_ffn2_w_down)}
    res = {}
    for nm, (group, t, mat, transposed, w, m, v) in big_w.items():
        shape = w.shape
        if transposed:
            w, m, v = (a.transpose(0, 2, 1) for a in (w, m, v))
        if nm == "w_in":
            out = _adam_cols(reduced[group][0][t], reduced[group][1][t], xy_arr, w, m, v, tc=256, name=f"adam_{nm}")
            res[nm] = [a.transpose(0, 2, 1) for a in out]
            continue
        r = w.shape[1]
        halves = ["ffn1a", "ffn1b"] if group == "ffn1" else [group]
        if len(halves) == 1:
            tr = 256 if r % 256 == 0 else r
        else:
            tr = r
        out = _adam_big([reduced[k][0][t] for k in halves], [reduced[k][1][t] for k in halves], mat, xy_arr, w, m, v,
                        tr=tr, name=f"adam_{nm}")
        if transposed:
            out = [a.reshape(1, -1, d).transpose(0, 2, 1) for a in out]
        res[nm] = [a.reshape(shape) for a in out]
    small_names = ["ffn1_norm", "mix_norm", "mem_norm", "ffn2_norm", "swa_q_norm", "swa_k_norm", "swa_sinks", "rel_bias",
                   "gla_w_gate_up", "gla_b_gate", "gla_out_norm", "mem_q_norm", "mem_k_norm"]
    small_m = [m_ffn1_norm, m_mix_norm, m_mem_norm, m_ffn2_norm, m_swa_q_norm, m_swa_k_norm, m_swa_sinks, m_rel_bias,
               m_gla_w_gate_up, m_gla_b_gate, m_gla_out_norm, m_mem_q_norm, m_mem_k_norm]
    small_v = [v_ffn1_norm, v_mix_norm, v_mem_norm, v_ffn2_norm, v_swa_q_norm, v_swa_k_norm, v_swa_sinks, v_rel_bias,
               v_gla_w_gate_up, v_gla_b_gate, v_gla_out_norm, v_mem_q_norm, v_mem_k_norm]
    small_full = [ffn1_norm, mix_norm, mem_norm, ffn2_norm, swa_q_norm, swa_k_norm, swa_sinks, rel_bias,
                  gla_w_gate_up, gla_b_gate, gla_out_norm, mem_q_norm, mem_k_norm]
    zero = jnp.zeros((1, 1), F32)
    turned = ("rel_bias",)

    def two_d(nm, a):
        a = a.reshape(a.shape[-2:])
        return a.T if nm in turned else a

    for group, sel in (("ffn1a", slice(1, None)), ("ffn1b", slice(0, 1))):
        extra = [zero] if group == "ffn1a" else []
        ws, ms, vs = ([two_d(nm, a) for nm, a in zip(small_names[sel], arrs[sel])] + extra
                      for arrs in (small_full, small_m, small_v))
        updated = _adam_small(small_box[group], ws, ms, vs, name=f"adam_small_{group}")
        for nm, full, out in zip(small_names[sel], small_full[sel], updated):
            res[nm] = [(a.T if nm in turned else a).reshape(full.shape) for a in out]
        if extra:
            loss = updated[-1][0].reshape(())

    order = ["ffn1_norm", "ffn1_w_gate", "ffn1_w_up", "ffn1_w_down", "mix_norm", "mem_norm", "w_in", "w_mem_kv",
             "swa_q_norm", "swa_k_norm", "swa_sinks", "rel_bias", "gla_w_gate_up", "gla_b_gate", "gla_out_norm",
             "mem_q_norm", "mem_k_norm", "w_out", "ffn2_norm", "ffn2_w_gate", "ffn2_w_up", "ffn2_w_down"]
    outs = [loss, grad_x[None]]
    for q in range(4):
        outs += [res[nm][q] for nm in order]
    return tuple(outs)
```

```python
import functools
import math

import numpy as np
import jax
import jax.numpy as jnp
from jax import lax
from jax.experimental import pallas as pl
from jax.experimental.pallas import tpu as pltpu
from jax.experimental.pallas import tpu_sc as plsc

F32 = jnp.float32
BF16 = jnp.bfloat16
SDS = jax.ShapeDtypeStruct

EPS = 1e-6
HEAD_DIM = 64
SWA_HEADS = 8
SWA_KV_HEADS = 2
SWA_GROUP = SWA_HEADS // SWA_KV_HEADS
BLOCK = 128
N_BUCKETS = 32
MAX_DISTANCE = 128
GLA_HEADS = 4
GLA_DK = 32
GLA_DV = 64
GLA_RANK = 16
GLA_TAU = 16.0
GLA_CHUNK = 32
MEM_HEADS = 4
SWA_Q_W = SWA_HEADS * HEAD_DIM
SWA_KV_W = SWA_KV_HEADS * HEAD_DIM
GLA_QK_W = GLA_HEADS * GLA_DK
GLA_V_W = GLA_HEADS * GLA_DV
MEM_Q_W = MEM_HEADS * HEAD_DIM
IN_W = 1808
IN_W_PAD = 1920
COL_SQ, COL_SKV, COL_GQ, COL_GK, COL_GV, COL_GG, COL_MQ, COL_GLR = 0, 512, 768, 896, 1024, 1280, 1536, 1792

ADAM_LR = 0.001
ADAM_B1 = 0.9
ADAM_B2 = 0.999
ADAM_EPS = 1e-08
ADAM_WD = 0.01
ADAM_STEP = 10

N_DEV = 8
VMEM_LIMIT_BYTES = 56 * 1024 * 1024
MESH = pl.DeviceIdType.MESH


def _params(*sem):
    return pltpu.CompilerParams(dimension_semantics=sem or None, vmem_limit_bytes=VMEM_LIMIT_BYTES)


def _dot(a, b, ta, tb, precision=None):
    dims = (((0 if ta else 1,), (1 if tb else 0,)), ((), ()))
    return lax.dot_general(a, b, dims, preferred_element_type=F32, precision=precision)


def _mm_raw(a, b, ta=False, tb=False):
    return _dot(a.astype(BF16), b.astype(BF16), ta, tb)


def _mmf_raw(a, b, ta=False, tb=False):
    return _dot(a, b, ta, tb, lax.Precision.HIGHEST)


def _make_mm(raw):
    @functools.partial(jax.custom_vjp, nondiff_argnums=(2, 3))
    def mm(a, b, ta=False, tb=False):
        return raw(a, b, ta, tb)

    def fwd(a, b, ta, tb):
        return raw(a, b, ta, tb), (a, b)

    def bwd(ta, tb, res, g):
        a, b = res
        da = raw(b, g, tb, True) if ta else raw(g, b, False, not tb)
        db = raw(g, a, True, ta) if tb else raw(a, g, not ta, False)
        return da, db

    mm.defvjp(fwd, bwd)
    return mm


_mm = _make_mm(_mm_raw)
_mmf = _make_mm(_mmf_raw)


def _mm3(a, b, ta=False, tb=False):
    a_hi, b_hi = a.astype(BF16).astype(F32), b.astype(BF16).astype(F32)
    return _mm(a_hi, b_hi, ta, tb) + _mm(a_hi, b - b_hi, ta, tb) + _mm(a - a_hi, b_hi, ta, tb)


def _rms(x, g):
    return x * lax.rsqrt(jnp.mean(x * x, axis=-1, keepdims=True) + EPS) * g


def _silu_mul(g, u):
    return jax.nn.silu(g) * u


def _log_sigmoid(z):
    return jnp.minimum(z, 0.0) - jnp.log(1.0 + jnp.exp(-jnp.abs(z)))


def _matmul(a_list, b, *, ta=False, tb=False, tm, tn, b_blocks=None, res=None, scale=1.0, out_dtype=F32, name):
    if not isinstance(a_list, (list, tuple)):
        a_list = [a_list]
    n_a = len(a_list)
    m = a_list[0].shape[1] if ta else a_list[0].shape[0]
    ks = [a.shape[0] if ta else a.shape[1] for a in a_list]
    n = b.shape[0] if tb else b.shape[1]
    if b_blocks is None:
        assert n_a == 1
        b_blocks = [0]
    tm, tn = min(tm, m), min(tn, n)
    assert m % tm == 0 and n % tn == 0, (m, n, tm, tn)

    def body(*refs):
        a_refs, b_refs = refs[:n_a], refs[n_a:2 * n_a]
        r_ref = refs[2 * n_a] if res is not None else None
        o_ref = refs[-1]
        acc = _mm_raw(a_refs[0][...], b_refs[0][...], ta, tb)
        for k in range(1, n_a):
            acc = acc + _mm_raw(a_refs[k][...], b_refs[k][...], ta, tb)
        if scale != 1.0:
            acc = acc * scale
        if r_ref is not None:
            acc = r_ref[...] + acc
        o_ref[...] = acc.astype(out_dtype)

    in_specs = []
    for k in ks:
        in_specs.append(pl.BlockSpec((k, tm), lambda i, j: (0, i)) if ta else pl.BlockSpec((tm, k), lambda i, j: (i, 0)))
    for k, blk in zip(ks, b_blocks):
        if tb:
            in_specs.append(pl.BlockSpec((tn, k), functools.partial(lambda i, j, blk: (j, blk), blk=blk)))
        else:
            in_specs.append(pl.BlockSpec((k, tn), functools.partial(lambda i, j, blk: (blk, j), blk=blk)))
    args = list(a_list) + [b] * n_a
    if res is not None:
        in_specs.append(pl.BlockSpec((tm, tn), lambda i, j: (i, j)))
        args.append(res)
    return pl.pallas_call(
        body, name=name, grid=(m // tm, n // tn), in_specs=in_specs,
        out_specs=pl.BlockSpec((tm, tn), lambda i, j: (i, j)), out_shape=SDS((m, n), out_dtype),
        compiler_params=_params("parallel", "parallel"),
    )(*args)


def _win_pieces(w):
    glr_lo, glr_hi = COL_MQ, COL_MQ + GLA_RANK
    out = []
    for j in range(N_DEV):
        for lo, hi, shift in ((0, glr_lo, 0), (glr_lo, glr_hi, COL_GLR - glr_lo), (glr_hi, IN_W, COL_MQ - glr_hi)):
            s, e = max(j * w, lo), min((j + 1) * w, hi)
            if s < e:
                out.append((j, s - j * w, e - j * w, s + shift))
    return out


def _pack_win(win_all, *, tr, name):
    _, d, w = win_all.shape

    def body(i_ref, o_ref):
        for j, a, b, dst in _win_pieces(w):
            o_ref[:, dst:dst + b - a] = i_ref[j][:, a:b]
        o_ref[:, IN_W:] = jnp.zeros((tr, IN_W_PAD - IN_W), o_ref.dtype)

    return pl.pallas_call(
        body, name=name, grid=(d // tr,), in_specs=[pl.BlockSpec((N_DEV, tr, w), lambda i: (0, i, 0))],
        out_specs=pl.BlockSpec((tr, IN_W_PAD), lambda i: (i, 0)), out_shape=SDS((d, IN_W_PAD), win_all.dtype),
        compiler_params=_params("parallel"),
    )(win_all)


def _dw_in(h, parts, *, tk, name):
    s, d = h.shape
    n_p = len(parts)
    w = IN_W // N_DEV
    starts = [sum(p.shape[1] for p in parts[:k]) for k in range(n_p + 1)]

    def body(*refs):
        h_ref, p_refs, o_ref, acc_refs = refs[0], refs[1:1 + n_p], refs[1 + n_p], refs[2 + n_p:]
        i = pl.program_id(0)
        h_t = h_ref[...].T

        @pl.when(i == 0)
        def _():
            for acc_ref in acc_refs:
                acc_ref[...] = jnp.zeros_like(acc_ref)

        for p_ref, acc_ref in zip(p_refs, acc_refs):
            acc_ref[...] += _mm_raw(h_t, p_ref[...], False, False)

        @pl.when(i == pl.num_programs(0) - 1)
        def _():
            for j, a, b, src in _win_pieces(w):
                for k, acc_ref in enumerate(acc_refs):
                    lo, hi = max(src, starts[k]), min(src + b - a, starts[k + 1])
                    if lo < hi:
                        o_ref[j % 2, j // 2, :, a + lo - src:a + hi - src] = (
                            acc_ref[:, lo - starts[k]:hi - starts[k]].astype(BF16))

    return pl.pallas_call(
        body, name=name, grid=(s // tk,),
        in_specs=[pl.BlockSpec((tk, d), lambda i: (i, 0))] + [pl.BlockSpec((tk, p.shape[1]), lambda i: (i, 0)) for p in parts],
        out_specs=pl.BlockSpec((2, 4, d, w), lambda i: (0, 0, 0, 0)), out_shape=SDS((2, 4, d, w), BF16),
        scratch_shapes=[pltpu.VMEM((d, p.shape[1]), F32) for p in parts],
        compiler_params=_params("arbitrary"),
    )(h, *parts)


def _dw_rows(parts, b, *, tk, name):
    s, n = b.shape
    n_p = len(parts)
    starts = [sum(p.shape[1] for p in parts[:k]) for k in range(n_p + 1)]

    def body(*refs):
        p_refs, b_ref, o_ref, acc_ref = refs[:n_p], refs[n_p], refs[n_p + 1], refs[n_p + 2]
        i = pl.program_id(0)

        @pl.when(i == 0)
        def _():
            acc_ref[...] = jnp.zeros_like(acc_ref)

        for k, p_ref in enumerate(p_refs):
            acc_ref[starts[k]:starts[k + 1], :] += _mm_raw(p_ref[...], b_ref[...], True, False)

        @pl.when(i == pl.num_programs(0) - 1)
        def _():
            o_ref[...] = acc_ref[...].astype(BF16)

    return pl.pallas_call(
        body, name=name, grid=(s // tk,),
        in_specs=[pl.BlockSpec((tk, p.shape[1]), lambda i: (i, 0)) for p in parts] + [pl.BlockSpec((tk, n), lambda i: (i, 0))],
        out_specs=pl.BlockSpec((starts[-1], n), lambda i: (0, 0)), out_shape=SDS((starts[-1], n), BF16),
        scratch_shapes=[pltpu.VMEM((starts[-1], n), F32)], compiler_params=_params("arbitrary"),
    )(*parts, b)


def _rms_fwd(x, g, *, tm, name):
    s, d = x.shape

    def body(x_ref, g_ref, h_ref):
        h_ref[...] = _rms(x_ref[...], g_ref[...]).astype(BF16)

    return pl.pallas_call(
        body, name=name, grid=(s // tm,),
        in_specs=[pl.BlockSpec((tm, d), lambda i: (i, 0)), pl.BlockSpec((1, d), lambda i: (0, 0))],
        out_specs=pl.BlockSpec((tm, d), lambda i: (i, 0)), out_shape=SDS((s, d), BF16),
        compiler_params=_params("parallel"),
    )(x, g)


def _rms_bwd(x, g, dh, dres, *, tm, name):
    s, d = x.shape
    want_dx = dres is not None
    product = isinstance(dh, tuple)
    if product:
        a_parts = list(dh[0]) if isinstance(dh[0], (list, tuple)) else [dh[0]]
        starts = [sum(a.shape[1] for a in a_parts[:k]) for k in range(len(a_parts) + 1)]

    def body(*refs):
        n_dh = len(a_parts) + 1 if product else 1
        x_ref, g_ref = refs[:2]
        dh_refs, rest = refs[2:2 + n_dh], refs[2 + n_dh:]
        if want_dx:
            dres_ref, dx_ref, dxb_ref, dg_ref = rest
        else:
            dg_ref, = rest
        if product:
            b_ref = dh_refs[-1]
            dh_tile = _mm_raw(dh_refs[0][...], b_ref[:, starts[0]:starts[1]], False, True)
            for k in range(1, len(a_parts)):
                dh_tile = dh_tile + _mm_raw(dh_refs[k][...], b_ref[:, starts[k]:starts[k + 1]], False, True)
        else:
            dh_tile = dh_refs[0][...]
        _, vjp = jax.vjp(_rms, x_ref[...], g_ref[...])
        dx, dg = vjp(dh_tile)
        if want_dx:
            dx = dres_ref[...] + dx
            dx_ref[...] = dx
            dxb_ref[...] = dx.astype(BF16)

        @pl.when(pl.program_id(0) == 0)
        def _():
            dg_ref[...] = jnp.zeros_like(dg_ref)

        dg_ref[...] += dg

    row = pl.BlockSpec((tm, d), lambda i: (i, 0))
    vec = pl.BlockSpec((1, d), lambda i: (0, 0))
    if product:
        dh_specs = [pl.BlockSpec((tm, a.shape[1]), lambda i: (i, 0)) for a in a_parts]
        dh_specs.append(pl.BlockSpec((d, starts[-1]), lambda i: (0, 0)))
        dh_args = a_parts + [dh[1]]
    else:
        dh_specs, dh_args = [row], [dh]
    if want_dx:
        return pl.pallas_call(
            body, name=name, grid=(s // tm,), in_specs=[row, vec] + dh_specs + [row], out_specs=[row, row, vec],
            out_shape=[SDS((s, d), F32), SDS((s, d), BF16), SDS((1, d), F32)], compiler_params=_params("arbitrary"),
        )(x, g, *dh_args, dres)
    return None, None, pl.pallas_call(
        body, name=name, grid=(s // tm,), in_specs=[row, vec] + dh_specs, out_specs=vec,
        out_shape=SDS((1, d), F32), compiler_params=_params("arbitrary"),
    )(x, g, *dh_args)


FFN_TN = 256
FFN_TN_FWD = 512
FFN1_FIRST = 192


def _ffn_fwd(x, gain, w3, tag, *, tm=1024, next_gain=None, target=None, start=None, partial=False):
    s, d = x.shape
    f = w3.shape[1]
    tn = FFN_TN_FWD if f % FFN_TN_FWD == 0 else FFN_TN
    nj = f // tn
    tm = min(tm, s)
    n_extra = (next_gain is not None) + (target is not None) + 2 * (start is not None)

    def body(*refs):
        x_ref, gain_ref, wg_ref, wu_ref, wd_ref = refs[:5]
        extra, outs = refs[5:5 + n_extra], refs[5 + n_extra:-1]
        acc_s = refs[-1]
        g_ref, u_ref = outs[-2:]
        h_ref = extra[-2] if start is not None else outs[-3]
        i, j = pl.program_id(0), pl.program_id(1)

        @pl.when(j == 0)
        def _():
            if start is None:
                h_ref[...] = _rms(x_ref[...], gain_ref[...]).astype(BF16)
                acc_s[...] = jnp.zeros_like(acc_s)
            else:
                acc_s[...] = extra[-1][...]

        hv = h_ref[...]
        g = _mm_raw(hv, wg_ref[...], False, True)
        u = _mm_raw(hv, wu_ref[...], False, True)
        g_ref[...] = g.astype(BF16)
        u_ref[...] = u.astype(BF16)
        acc_s[...] += _mm_raw(_silu_mul(g, u), wd_ref[...])

        @pl.when(j == nj - 1)
        def _():
            y = acc_s[...] if partial else x_ref[...] + 0.5 * acc_s[...]
            if target is None:
                outs[0][...] = y
                if next_gain is not None:
                    outs[1][...] = _rms(y, extra[0][...]).astype(BF16)
            else:
                dy_ref, dyb_ref, loss_ref = outs[:3]
                diff = y - extra[0][...]
                dy_ref[...] = diff * (1.0 / d)
                dyb_ref[...] = (diff * (1.0 / d)).astype(BF16)
                part = 0.5 * jnp.sum(jnp.mean(diff * diff, axis=-1, keepdims=True), axis=0, keepdims=True)

                @pl.when(i == 0)
                def _():
                    loss_ref[...] = part

                @pl.when(i > 0)
                def _():
                    loss_ref[...] += part

    row = pl.BlockSpec((tm, d), lambda i, j: (i, 0))
    vec = pl.BlockSpec((1, d), lambda i, j: (0, 0))
    tile = pl.BlockSpec((tm, tn), lambda i, j: (i, j))
    in_specs = [row, vec] + [pl.BlockSpec((None, tn, d), functools.partial(lambda i, j, k: (k, j, 0), k=k)) for k in range(3)]
    args = [x, gain, w3, w3, w3]
    if target is None:
        out_specs, out_shape = [row], [SDS((s, d), F32)]
        if next_gain is not None:
            in_specs.append(vec)
            args.append(next_gain)
            out_specs.append(row)
            out_shape.append(SDS((s, d), BF16))
    else:
        in_specs.append(row)
        args.append(target)
        out_specs = [row, row, pl.BlockSpec((1, 1), lambda i, j: (0, 0))]
        out_shape = [SDS((s, d), F32), SDS((s, d), BF16), SDS((1, 1), F32)]
    if start is None:
        out_specs.append(row)
        out_shape.append(SDS((s, d), BF16))
    else:
        in_specs += [row, row]
        args += list(start)
    *head, g, u = pl.pallas_call(
        body, name=f"{tag}_fwd", grid=(s // tm, nj), in_specs=in_specs,
        out_specs=out_specs + [tile, tile],
        out_shape=out_shape + [SDS((s, f), BF16), SDS((s, f), BF16)],
        scratch_shapes=[pltpu.VMEM((tm, d), F32)],
        compiler_params=_params("arbitrary", "arbitrary"),
    )(*args)
    if start is None:
        *head, h = head
    else:
        h = start[0]
    return head, (h, g, u)


def _ffn_bwd_part(dyb, w3, saved, first, count, dh_init, *, name):
    h, g, u = saved
    s, d = h.shape
    tn = FFN_TN

    def body(*refs):
        if dh_init is None:
            dy_ref, h_ref, wd_ref, wg_ref, wu_ref, g_ref, u_ref, dh_ref, dw3_ref, dg_s, du_s, a_s = refs
        else:
            dy_ref, h_ref, wd_ref, wg_ref, wu_ref, g_ref, u_ref, dh0_ref, dh_ref, dw3_ref, dg_s, du_s, a_s = refs
        j = pl.program_id(0)

        @pl.when(j == 0)
        def _():
            dh_ref[...] = jnp.zeros_like(dh_ref) if dh_init is None else dh0_ref[...]
            for ref in (dg_s, du_s, a_s):
                ref[...] = jnp.zeros_like(ref)

        now, before = j % 2, 1 - j % 2
        dyv = dy_ref[...]
        hv = h_ref[...]
        dg, du, a = dg_s[before], du_s[before], a_s[before]
        dh_ref[...] += _mm_raw(dg, wg_ref[...]) + _mm_raw(du, wu_ref[...])
        dw3_ref[0] = _mm_raw(dg, hv, True, False).astype(BF16)
        dw3_ref[1] = _mm_raw(du, hv, True, False).astype(BF16)
        dw3_ref[2] = (_mm_raw(a, dyv, True, False) * 0.5).astype(BF16)

        da = _mm_raw(dyv, wd_ref[...], False, True) * 0.5
        a, vjp = jax.vjp(_silu_mul, g_ref[...].astype(F32), u_ref[...].astype(F32))
        dg, du = vjp(da)
        dg_s[now] = dg.astype(BF16)
        du_s[now] = du.astype(BF16)
        a_s[now] = a.astype(BF16)

    this = lambda j: first + jnp.minimum(j, count - 1)
    last = lambda j: first + jnp.maximum(j - 1, 0)
    full = pl.BlockSpec((s, d), lambda j: (0, 0))
    once = pl.BlockSpec((s, d), lambda j: (0, 0), pipeline_mode=pl.Buffered(1))
    tile = pl.BlockSpec((s, tn), lambda j: (0, this(j)))
    in_specs = [once, once, pl.BlockSpec((None, tn, d), lambda j: (2, this(j), 0)),
                pl.BlockSpec((None, tn, d), lambda j: (0, last(j), 0)), pl.BlockSpec((None, tn, d), lambda j: (1, last(j), 0)),
                tile, tile]
    args = [dyb, h, w3, w3, w3, g, u]
    if dh_init is not None:
        in_specs.append(once)
        args.append(dh_init)
    return pl.pallas_call(
        body, name=name, grid=(count + 1,), in_specs=in_specs,
        out_specs=[full, pl.BlockSpec((3, tn, d), lambda j: (0, jnp.maximum(j - 1, 0), 0))],
        out_shape=[SDS((s, d), F32), SDS((3, count * tn, d), BF16)],
        scratch_shapes=[pltpu.VMEM((2, s, tn), BF16)] * 3,
        compiler_params=_params("arbitrary"),
    )(*args)


def _bucket_table():
    qi = np.arange(BLOCK)[:, None]
    kj = np.arange(2 * BLOCK)[None, :]
    dist = np.maximum(qi + BLOCK - kj, 0)
    max_exact = N_BUCKETS // 2
    d = np.maximum(dist, 1).astype(np.float32)
    large = max_exact + (np.log(d / np.float32(max_exact)) / np.float32(math.log(MAX_DISTANCE / max_exact))
                         * np.float32(N_BUCKETS - max_exact)).astype(np.int32)
    large = np.minimum(large, N_BUCKETS - 1)
    band = np.where(dist < max_exact, dist, large).astype(np.int32)
    return np.where(np.tril(np.ones((BLOCK, BLOCK), bool)), band[:, BLOCK:], band[:, :BLOCK])


SWA_STACK = SWA_GROUP * BLOCK


def _swa_masks(n):
    qi = lax.broadcasted_iota(jnp.int32, (SWA_STACK, BLOCK), 0) % BLOCK
    kj = lax.broadcasted_iota(jnp.int32, (SWA_STACK, BLOCK), 1)
    own = kj <= qi
    return own, own | (n > 0)


def _swa_group(q, kp, kc, vp, vc, qg, kg, sink, bias, own, valid):
    qn = _rms(q, qg)
    s = jnp.where(own, _mm(qn, _rms(kc, kg), False, True), _mm(qn, _rms(kp, kg), False, True))
    s = s * (HEAD_DIM ** -0.5) + bias
    s = jnp.where(valid, s, -jnp.inf)
    m = lax.stop_gradient(jnp.maximum(jnp.max(s, axis=-1, keepdims=True), sink))
    p = jnp.exp(s - m)
    p = p / (jnp.sum(p, axis=-1, keepdims=True) + jnp.exp(sink - m))
    return _mm(jnp.where(own, p, 0.0), vc) + _mm(jnp.where(own, 0.0, p), vp)


def _swa_bias_table(rb_ref, bucket, bias_s):
    for h in range(SWA_HEADS):
        acc = jnp.zeros((BLOCK, BLOCK), F32)
        for b in range(N_BUCKETS):
            acc = jnp.where(bucket == b, rb_ref[b, h], acc)
        bias_s[h // SWA_GROUP, (h % SWA_GROUP) * BLOCK:(h % SWA_GROUP + 1) * BLOCK, :] = acc


def _swa_stack(ref, g):
    return jnp.concatenate([ref[:, (g * SWA_GROUP + hh) * HEAD_DIM:(g * SWA_GROUP + hh + 1) * HEAD_DIM]
                            for hh in range(SWA_GROUP)], axis=0)


def _swa_unstack(ref, g, stacked):
    for hh in range(SWA_GROUP):
        h = g * SWA_GROUP + hh
        ref[:, h * HEAD_DIM:(h + 1) * HEAD_DIM] = stacked[hh * BLOCK:(hh + 1) * BLOCK]


def _swa_sink_column(sink_ref, g):
    head = lax.broadcasted_iota(jnp.int32, (SWA_STACK, 1), 0) // BLOCK
    col = jnp.zeros((SWA_STACK, 1), F32)
    for hh in range(SWA_GROUP):
        col = jnp.where(head == hh, sink_ref[g * SWA_GROUP + hh], col)
    return col


def _swa_band(kvp_ref, kvc_ref, g):
    k = slice(g * HEAD_DIM, (g + 1) * HEAD_DIM)
    v = slice(SWA_KV_W + g * HEAD_DIM, SWA_KV_W + (g + 1) * HEAD_DIM)
    return kvp_ref[:, k], kvc_ref[:, k], kvp_ref[:, v], kvc_ref[:, v]


def _swa_specs(order):
    kvc = COL_SKV // (2 * SWA_KV_W)
    return [
        pl.BlockSpec((BLOCK, SWA_Q_W), lambda t: (order(t), 0)),
        pl.BlockSpec((BLOCK, 2 * SWA_KV_W), lambda t: (jnp.maximum(order(t) - 1, 0), kvc)),
        pl.BlockSpec((BLOCK, 2 * SWA_KV_W), lambda t: (order(t), kvc)),
        pl.BlockSpec((1, HEAD_DIM), lambda t: (0, 0)),
        pl.BlockSpec((1, HEAD_DIM), lambda t: (0, 0)),
        pl.BlockSpec(memory_space=pltpu.SMEM),
        pl.BlockSpec(memory_space=pltpu.SMEM),
        pl.BlockSpec((BLOCK, BLOCK), lambda t: (0, 0)),
    ]


def _swa_fwd(p, qg, kg, sinks, rel_bias, *, name):
    s = p.shape[0]
    nb = s // BLOCK

    def body(q_ref, kvp_ref, kvc_ref, qg_ref, kg_ref, sink_ref, rb_ref, bucket_ref, y_ref, bias_s):
        n = pl.program_id(0)

        @pl.when(n == 0)
        def _():
            _swa_bias_table(rb_ref, bucket_ref[...], bias_s)

        own, valid = _swa_masks(n)
        for g in range(SWA_KV_HEADS):
            out = _swa_group(_swa_stack(q_ref, g), *_swa_band(kvp_ref, kvc_ref, g), qg_ref[...], kg_ref[...],
                             _swa_sink_column(sink_ref, g), bias_s[g], own, valid)
            _swa_unstack(y_ref, g, out)

    return pl.pallas_call(
        body, name=name, grid=(nb,), in_specs=_swa_specs(lambda t: t),
        out_specs=pl.BlockSpec((BLOCK, SWA_Q_W), lambda t: (t, 0)), out_shape=SDS((s, SWA_Q_W), F32),
        scratch_shapes=[pltpu.VMEM((SWA_KV_HEADS, SWA_STACK, BLOCK), F32)],
        compiler_params=_params("arbitrary"),
    )(p, p, p, qg, kg, sinks, rel_bias, jnp.asarray(_bucket_table()))


def _swa_bwd(p, qg, kg, sinks, rel_bias, dy_all, *, name):
    s = p.shape[0]
    nb = s // BLOCK

    def body(q_ref, kvp_ref, kvc_ref, qg_ref, kg_ref, sink_ref, rb_ref, bucket_ref, dy_ref,
             dq_ref, dkv_ref, dqg_ref, dkg_ref, dsink_ref, drb_ref, bias_s, dbias_s, carry_s):
        t = pl.program_id(0)
        n = nb - 1 - t

        @pl.when(t == 0)
        def _():
            _swa_bias_table(rb_ref, bucket_ref[...], bias_s)
            dbias_s[...] = jnp.zeros_like(dbias_s)
            carry_s[...] = jnp.zeros_like(carry_s)
            dqg_ref[...] = jnp.zeros_like(dqg_ref)
            dkg_ref[...] = jnp.zeros_like(dkg_ref)
            dsink_ref[...] = jnp.zeros_like(dsink_ref)
            drb_ref[...] = jnp.zeros_like(drb_ref)

        own, valid = _swa_masks(n)
        lane = lax.broadcasted_iota(jnp.int32, (1, BLOCK), 1)
        dqg = jnp.zeros((1, HEAD_DIM), F32)
        dkg = jnp.zeros((1, HEAD_DIM), F32)
        dsink_vec = jnp.zeros((1, BLOCK), F32)
        for g in range(SWA_KV_HEADS):
            _, vjp = jax.vjp(functools.partial(_swa_group, own=own, valid=valid), _swa_stack(q_ref, g),
                             *_swa_band(kvp_ref, kvc_ref, g), qg_ref[...], kg_ref[...], _swa_sink_column(sink_ref, g),
                             bias_s[g])
            dq, dkp, dkc, dvp, dvc, dqg_g, dkg_g, dsink_col, dbias = vjp(_swa_stack(dy_ref, g))
            _swa_unstack(dq_ref, g, dq)
            dqg += dqg_g
            dkg += dkg_g
            dbias_s[g] += dbias
            for hh in range(SWA_GROUP):
                dsink_h = jnp.sum(dsink_col[hh * BLOCK:(hh + 1) * BLOCK], axis=0, keepdims=True)
                dsink_vec += jnp.where(lane == g * SWA_GROUP + hh, dsink_h, 0.0)
            lo = g * HEAD_DIM
            dkv_ref[:, lo:lo + HEAD_DIM] = dkc + carry_s[g]
            carry_s[g] = dkp
            lo += SWA_KV_W
            dkv_ref[:, lo:lo + HEAD_DIM] = dvc + carry_s[SWA_KV_HEADS + g]
            carry_s[SWA_KV_HEADS + g] = dvp
        dqg_ref[...] += dqg
        dkg_ref[...] += dkg
        dsink_ref[...] += dsink_vec

        @pl.when(t == nb - 1)
        def _():
            bucket = bucket_ref[...]
            row = lax.broadcasted_iota(jnp.int32, (N_BUCKETS, BLOCK), 0)
            col = lax.broadcasted_iota(jnp.int32, (N_BUCKETS, BLOCK), 1)
            acc = jnp.zeros((N_BUCKETS, BLOCK), F32)
            for h in range(SWA_HEADS):
                dbias = dbias_s[h // SWA_GROUP, (h % SWA_GROUP) * BLOCK:(h % SWA_GROUP + 1) * BLOCK, :]
                for b in range(N_BUCKETS):
                    part = jnp.sum(jnp.where(bucket == b, dbias, 0.0), axis=1, keepdims=True)
                    val = jnp.sum(part, axis=0, keepdims=True)
                    acc = acc + jnp.where((row == b) & (col == h), val, 0.0)
            drb_ref[...] = acc

    order = lambda t: nb - 1 - t
    vec = pl.BlockSpec((1, HEAD_DIM), lambda t: (0, 0))
    return pl.pallas_call(
        body, name=name, grid=(nb,),
        in_specs=_swa_specs(order) + [pl.BlockSpec((BLOCK, SWA_Q_W), lambda t: (order(t), 0))],
        out_specs=[pl.BlockSpec((BLOCK, SWA_Q_W), lambda t: (order(t), 0)),
                   pl.BlockSpec((BLOCK, 2 * SWA_KV_W), lambda t: (order(t), 0)),
                   vec, vec, pl.BlockSpec((1, BLOCK), lambda t: (0, 0)),
                   pl.BlockSpec((N_BUCKETS, BLOCK), lambda t: (0, 0))],
        out_shape=[SDS((s, SWA_Q_W), F32), SDS((s, 2 * SWA_KV_W), F32), SDS((1, HEAD_DIM), F32),
                   SDS((1, HEAD_DIM), F32), SDS((1, BLOCK), F32), SDS((N_BUCKETS, BLOCK), F32)],
        scratch_shapes=[pltpu.VMEM((SWA_KV_HEADS, SWA_STACK, BLOCK), F32),
                        pltpu.VMEM((SWA_KV_HEADS, SWA_STACK, BLOCK), F32),
                        pltpu.VMEM((2 * SWA_KV_HEADS, BLOCK, HEAD_DIM), F32)],
        compiler_params=_params("arbitrary"),
    )(p, p, p, qg, kg, sinks, rel_bias, jnp.asarray(_bucket_table()), dy_all)


def _mem_head(q, k, v, qg, kg):
    qn = _rms(q, qg)
    kn = _rms(k, kg)
    s = _mm(qn, kn, False, True) * (HEAD_DIM ** -0.5)
    m = lax.stop_gradient(jnp.max(s, axis=-1, keepdims=True))
    e = jnp.exp(s - m)
    return _mm(e / jnp.sum(e, axis=-1, keepdims=True), v)


def _mem_fwd(p, kv, qg, kg, *, tq, name):
    s = p.shape[0]
    m = kv.shape[0]

    def body(q_ref, kv_ref, qg_ref, kg_ref, y_ref):
        for h in range(MEM_HEADS):
            cols = slice(h * HEAD_DIM, (h + 1) * HEAD_DIM)
            vcols = slice(MEM_Q_W + h * HEAD_DIM, MEM_Q_W + (h + 1) * HEAD_DIM)
            y_ref[:, cols] = _mem_head(q_ref[:, cols], kv_ref[:, cols], kv_ref[:, vcols], qg_ref[...], kg_ref[...])

    vec = pl.BlockSpec((1, HEAD_DIM), lambda t: (0, 0))
    return pl.pallas_call(
        body, name=name, grid=(s // tq,),
        in_specs=[pl.BlockSpec((tq, MEM_Q_W), lambda t: (t, COL_MQ // MEM_Q_W)),
                  pl.BlockSpec((m, 2 * MEM_Q_W), lambda t: (0, 0)), vec, vec],
        out_specs=pl.BlockSpec((tq, MEM_Q_W), lambda t: (t, 0)), out_shape=SDS((s, MEM_Q_W), F32),
        compiler_params=_params("parallel"),
    )(p, kv, qg, kg)


def _mem_bwd(p, kv, qg, kg, dy_all, *, tq, name):
    s = p.shape[0]
    m = kv.shape[0]

    def body(q_ref, kv_ref, qg_ref, kg_ref, dy_ref, dq_ref, dkv_ref, dqg_ref, dkg_ref):
        @pl.when(pl.program_id(0) == 0)
        def _():
            dkv_ref[...] = jnp.zeros_like(dkv_ref)
            dqg_ref[...] = jnp.zeros_like(dqg_ref)
            dkg_ref[...] = jnp.zeros_like(dkg_ref)

        dqg = jnp.zeros((1, HEAD_DIM), F32)
        dkg = jnp.zeros((1, HEAD_DIM), F32)
        for h in range(MEM_HEADS):
            cols = slice(h * HEAD_DIM, (h + 1) * HEAD_DIM)
            vcols = slice(MEM_Q_W + h * HEAD_DIM, MEM_Q_W + (h + 1) * HEAD_DIM)
            _, vjp = jax.vjp(_mem_head, q_ref[:, cols], kv_ref[:, cols], kv_ref[:, vcols], qg_ref[...], kg_ref[...])
            dq, dk, dv, dqg_h, dkg_h = vjp(dy_ref[:, cols])
            dq_ref[:, cols] = dq
            dkv_ref[:, cols] += dk
            dkv_ref[:, vcols] += dv
            dqg += dqg_h
            dkg += dkg_h
        dqg_ref[...] += dqg
        dkg_ref[...] += dkg

    vec = pl.BlockSpec((1, HEAD_DIM), lambda t: (0, 0))
    full = pl.BlockSpec((m, 2 * MEM_Q_W), lambda t: (0, 0))
    dy_col = (SWA_Q_W + GLA_V_W) // MEM_Q_W
    return pl.pallas_call(
        body, name=name, grid=(s // tq,),
        in_specs=[pl.BlockSpec((tq, MEM_Q_W), lambda t: (t, COL_MQ // MEM_Q_W)), full, vec, vec,
                  pl.BlockSpec((tq, MEM_Q_W), lambda t: (t, dy_col))],
        out_specs=[pl.BlockSpec((tq, MEM_Q_W), lambda t: (t, 0)), full, vec, vec],
        out_shape=[SDS((s, MEM_Q_W), F32), SDS((m, 2 * MEM_Q_W), F32), SDS((1, HEAD_DIM), F32), SDS((1, HEAD_DIM), F32)],
        compiler_params=_params("arbitrary"),
    )(p, kv, qg, kg, dy_all)


GLA_ROWS = 256


GLA_GROUP = 4


def _gla_consts():
    c, h, r = GLA_CHUNK, GLA_HEADS, GLA_GROUP * GLA_CHUNK
    i2 = lax.broadcasted_iota(jnp.int32, (c, c), 0)
    j2 = lax.broadcasted_iota(jnp.int32, (c, c), 1)
    slab_q = lax.broadcasted_iota(jnp.int32, (h, r, GLA_QK_W), 0)
    lane_q = lax.broadcasted_iota(jnp.int32, (h, r, GLA_QK_W), 2)
    row_a = lax.broadcasted_iota(jnp.int32, (h * r, r), 0) % r
    col_a = lax.broadcasted_iota(jnp.int32, (h * r, r), 1)
    slab_o = lax.broadcasted_iota(jnp.int32, (h, r, GLA_V_W), 0)
    lane_o = lax.broadcasted_iota(jnp.int32, (h, r, GLA_V_W), 2)
    row_s = lax.broadcasted_iota(jnp.int32, (GLA_V_W, GLA_QK_W), 0)
    col_s = lax.broadcasted_iota(jnp.int32, (GLA_V_W, GLA_QK_W), 1)
    return dict(
        ltri=(j2 <= i2).astype(F32),
        m_q=(slab_q == lane_q // GLA_DK).astype(F32),
        causal=(col_a <= row_a) & (col_a // c == row_a // c),
        m_o=(slab_o == lane_o // GLA_DV).astype(F32),
        m_s=(row_s // GLA_DV == col_s // GLA_DK).astype(F32),
    )


def _gla_step(q, k, v, z, bg, st, c):
    h = GLA_HEADS
    kt, ka, qt, qe, decay = [], [], [], [], []
    for qc, kc, zc in zip(q, k, z):
        la = _log_sigmoid(zc + bg) * (1.0 / GLA_TAU)
        b = _mmf(c["ltri"], la)
        bl = jnp.sum(la, axis=0, keepdims=True)
        qs = qc * (GLA_DK ** -0.5)
        kt.append(kc * jnp.exp(bl - b))
        ka.append(kc * jnp.exp(0.5 * bl - b))
        qt.append(qs * jnp.exp(b - 0.5 * bl))
        qe.append(qs * jnp.exp(b))
        decay.append(jnp.exp(bl))
    o_intra = []
    rows = GLA_GROUP * GLA_CHUNK
    for lo in range(0, len(q), GLA_GROUP):
        qt_all, kt_all, v_all = (jnp.concatenate(parts[lo:lo + GLA_GROUP], axis=0) for parts in (qt, ka, v))
        q_stack = (jnp.broadcast_to(qt_all[None], (h, rows, GLA_QK_W)) * c["m_q"]).reshape(h * rows, GLA_QK_W)
        a = jnp.where(c["causal"], _mm3(q_stack, kt_all, False, True), 0.0)
        o_stack = _mm(a, v_all)
        o_intra.append(jnp.sum(o_stack.reshape(h, rows, GLA_V_W) * c["m_o"], axis=0))
    o_intra = jnp.concatenate(o_intra, axis=0)
    o_inter = []
    for qec, ktc, vc, dc in zip(qe, kt, v, decay):
        o_inter.append(_mm(qec, st, False, True))
        st = st * dc + _mm(vc, ktc, True, False) * c["m_s"]
    return o_intra + jnp.concatenate(o_inter, axis=0), st


def _gla_post(o, gg, gain, g64):
    ms = _mmf(o * o, g64) * (1.0 / GLA_DV)
    return o * lax.rsqrt(ms + EPS) * gain * jax.nn.silu(gg)


def _gla_g64():
    r = lax.broadcasted_iota(jnp.int32, (GLA_V_W, GLA_V_W), 0)
    c = lax.broadcasted_iota(jnp.int32, (GLA_V_W, GLA_V_W), 1)
    return (r // GLA_DV == c // GLA_DV).astype(F32)


def _gla_in_specs(order):
    r = GLA_ROWS
    return [
        pl.BlockSpec((r, GLA_QK_W), lambda t: (order(t), COL_GQ // GLA_QK_W)),
        pl.BlockSpec((r, GLA_QK_W), lambda t: (order(t), COL_GK // GLA_QK_W)),
        pl.BlockSpec((r, GLA_V_W), lambda t: (order(t), COL_GV // GLA_V_W)),
        pl.BlockSpec((r, GLA_V_W), lambda t: (order(t), COL_GG // GLA_V_W)),
        pl.BlockSpec((r, GLA_QK_W), lambda t: (order(t), 0)),
        pl.BlockSpec((1, GLA_QK_W), lambda t: (0, 0)),
        pl.BlockSpec((1, GLA_V_W), lambda t: (0, 0)),
    ]


def _gla_pieces(q_ref, k_ref, v_ref, z_ref, cps):
    chunk = lambda ref: [ref[ci * GLA_CHUNK:(ci + 1) * GLA_CHUNK, :] for ci in range(cps)]
    return chunk(q_ref), chunk(k_ref), chunk(v_ref), chunk(z_ref)


def _gla_fwd(p, z, bg, gain, *, name):
    s = p.shape[0]
    r = GLA_ROWS
    cps = r // GLA_CHUNK

    def body(q_ref, k_ref, v_ref, gg_ref, z_ref, bg_ref, gain_ref, y_ref, oraw_ref, stsave_ref, st_s):
        @pl.when(pl.program_id(0) == 0)
        def _():
            st_s[...] = jnp.zeros_like(st_s)

        st = st_s[...]
        stsave_ref[0] = st
        o, st = _gla_step(*_gla_pieces(q_ref, k_ref, v_ref, z_ref, cps), bg_ref[...], st, _gla_consts())
        oraw_ref[...] = o
        st_s[...] = st
        y_ref[...] = _gla_post(o, gg_ref[...], gain_ref[...], _gla_g64())

    rowv = pl.BlockSpec((r, GLA_V_W), lambda t: (t, 0))
    return pl.pallas_call(
        body, name=name, grid=(s // r,), in_specs=_gla_in_specs(lambda t: t),
        out_specs=[rowv, rowv, pl.BlockSpec((1, GLA_V_W, GLA_QK_W), lambda t: (t, 0, 0))],
        out_shape=[SDS((s, GLA_V_W), F32), SDS((s, GLA_V_W), F32), SDS((s // r, GLA_V_W, GLA_QK_W), F32)],
        scratch_shapes=[pltpu.VMEM((GLA_V_W, GLA_QK_W), F32)],
        compiler_params=_params("arbitrary"),
    )(p, p, p, p, z, bg, gain)


def _gla_bwd(p, z, bg, gain, oraw, stsave, dy_all, *, name):
    s = p.shape[0]
    r = GLA_ROWS
    cps = r // GLA_CHUNK
    nsteps = s // r
    w_qkvg = 2 * GLA_QK_W + 2 * GLA_V_W

    def body(q_ref, k_ref, v_ref, gg_ref, z_ref, bg_ref, gain_ref, oraw_ref, stsave_ref, dy_ref,
             dqkvg_ref, dz_ref, dbg_ref, dgain_ref, dst_s):
        @pl.when(pl.program_id(0) == 0)
        def _():
            dst_s[...] = jnp.zeros_like(dst_s)
            dbg_ref[...] = jnp.zeros_like(dbg_ref)
            dgain_ref[...] = jnp.zeros_like(dgain_ref)

        _, vjp = jax.vjp(functools.partial(_gla_post, g64=_gla_g64()), oraw_ref[...], gg_ref[...], gain_ref[...])
        do, dgg, dgain = vjp(dy_ref[...])
        dqkvg_ref[:, 2 * GLA_QK_W + GLA_V_W:] = dgg
        dgain_ref[...] += dgain
        _, vjp = jax.vjp(functools.partial(_gla_step, c=_gla_consts()), *_gla_pieces(q_ref, k_ref, v_ref, z_ref, cps),
                         bg_ref[...], stsave_ref[0])
        dq, dk, dv, dz, dbg, dst = vjp((do, dst_s[...]))
        for ci in range(cps):
            rows = slice(ci * GLA_CHUNK, (ci + 1) * GLA_CHUNK)
            dqkvg_ref[rows, 0:GLA_QK_W] = dq[ci]
            dqkvg_ref[rows, GLA_QK_W:2 * GLA_QK_W] = dk[ci]
            dqkvg_ref[rows, 2 * GLA_QK_W:2 * GLA_QK_W + GLA_V_W] = dv[ci]
            dz_ref[rows, :] = dz[ci]
        dst_s[...] = dst
        dbg_ref[...] += dbg

    order = lambda t: nsteps - 1 - t
    rowv = pl.BlockSpec((r, GLA_V_W), lambda t: (order(t), 0))
    return pl.pallas_call(
        body, name=name, grid=(nsteps,),
        in_specs=_gla_in_specs(order) + [
            rowv, pl.BlockSpec((1, GLA_V_W, GLA_QK_W), lambda t: (order(t), 0, 0)),
            pl.BlockSpec((r, GLA_V_W), lambda t: (order(t), SWA_Q_W // GLA_V_W))],
        out_specs=[pl.BlockSpec((r, w_qkvg), lambda t: (order(t), 0)), pl.BlockSpec((r, GLA_QK_W), lambda t: (order(t), 0)),
                   pl.BlockSpec((1, GLA_QK_W), lambda t: (0, 0)), pl.BlockSpec((1, GLA_V_W), lambda t: (0, 0))],
        out_shape=[SDS((s, w_qkvg), F32), SDS((s, GLA_QK_W), F32), SDS((1, GLA_QK_W), F32), SDS((1, GLA_V_W), F32)],
        scratch_shapes=[pltpu.VMEM((GLA_V_W, GLA_QK_W), F32)],
        compiler_params=_params("arbitrary"),
    )(p, p, p, p, z, bg, gain, oraw, stsave, dy_all)


def _local_step(x, mem, target, small, big, on_grads):
    g1, gmix, gmem, g2, sqg, skg, sinks, rel_bias, wgu, bg, gla_gain, mqg, mkg = small
    (w3_1a, w3_1b), gather_mix, gather_ffn2 = big
    wgu_pad = jnp.zeros((GLA_QK_W, GLA_QK_W), BF16).at[:GLA_RANK].set(wgu.astype(BF16))
    gain256 = jnp.tile(gla_gain, (1, GLA_HEADS))

    (part,), saved1a = _ffn_fwd(x, g1, w3_1a, "ffn1a", partial=True)
    win_p, wkv, wout = gather_mix(part)
    (x1, h), saved1b = _ffn_fwd(x, g1, w3_1b, "ffn1b", next_gain=gmix, start=(saved1a[0], part))
    w3_2 = gather_ffn2((wout, x1))
    p = _matmul(h, win_p, tm=512, tn=IN_W_PAD, name="mix_in")
    hm = _rms_fwd(mem, gmem, tm=256, name="mem_rms")
    kv = _matmul(hm, wkv, tm=256, tn=512, name="mem_kv")
    p_glr = p[:, COL_GLR:]
    z = _matmul(p_glr, wgu_pad, tm=1024, tn=GLA_QK_W, name="gla_gate")
    y_swa = _swa_fwd(p, sqg, skg, sinks, rel_bias, name="swa_fwd")
    y_gla, oraw, stsave = _gla_fwd(p, z, bg, gain256, name="gla_fwd")
    y_mem = _mem_fwd(p, kv, mqg, mkg, tq=512, name="mem_fwd")
    x2 = _matmul([y_swa, y_gla, y_mem], wout, b_blocks=[0, 2, 3], tm=512, tn=1024, res=x1, name="mix_out")
    (dy, dyb, loss), saved2 = _ffn_fwd(x2, g2, w3_2, "ffn2", target=target)

    dh2, dw3_2 = _ffn_bwd_part(dyb, w3_2, saved2, 0, w3_2.shape[1] // FFN_TN, None, name="ffn2_bwd")
    dx2, dx2b, dg2 = _rms_bwd(x2, g2, dh2, dy, tm=512, name="ffn2_drms")
    dx2b = on_grads("ffn2", [dw3_2], dx2b)
    dy_all = _matmul(dx2b, wout, tb=True, tm=512, tn=1024, name="mix_dy")
    dwout = _dw_rows([y_swa, y_gla, y_mem], dx2b, tk=512, name="mix_dw_out")
    dq_swa, dkv_swa, dsqg, dskg, dsink, drb = _swa_bwd(p, sqg, skg, sinks, rel_bias, dy_all, name="swa_bwd")
    dqkvg, dz, dbg, dgain256 = _gla_bwd(p, z, bg, gain256, oraw, stsave, dy_all, name="gla_bwd")
    dmq, dkv_mem, dmqg, dmkg = _mem_bwd(p, kv, mqg, mkg, dy_all, tq=512, name="mem_bwd")
    dglr = _matmul(dz, wgu_pad, tb=True, tm=1024, tn=GLA_QK_W, name="gla_gate_dx")
    dwgu_pad = _matmul(p_glr, dz, ta=True, tm=GLA_QK_W, tn=GLA_QK_W, name="gla_gate_dw")
    dp = [dq_swa, dkv_swa, dqkvg, dmq, dglr]
    dwin_p = _dw_in(h, dp, tk=512, name="mix_dw_in")
    dx1, dx1b, dgmix = _rms_bwd(x1, gmix, (dp, win_p), dx2, tm=512, name="mix_dh_drms")
    dwkv = _matmul(hm, dkv_mem, ta=True, tm=512, tn=512, out_dtype=BF16, name="mem_dw_kv")
    dx1b = on_grads("mix", (dwin_p, dwkv, dwout), dx1b)
    _, _, dgmem = _rms_bwd(mem, gmem, (dkv_mem, wkv), None, tm=256, name="mem_dh_drms")
    dh1, dw3_1a = _ffn_bwd_part(dx1b, w3_1a, saved1a, 0, w3_1a.shape[1] // FFN_TN, None, name="ffn1_bwd_a")
    dgla_gain = dgain256.reshape(GLA_HEADS, GLA_DV).sum(axis=0, keepdims=True)
    dsmall = [dgmix, dgmem, dg2, dsqg, dskg, dsink[:, :SWA_HEADS], drb[:, :SWA_HEADS].T, dwgu_pad[:GLA_RANK], dbg,
              dgla_gain, dmqg, dmkg, loss]
    dh1, dgmem, dwgu_pad = on_grads("ffn1a", [dw3_1a], (dh1, dgmem, dwgu_pad), small=dsmall)
    dh1, dw3_1b = _ffn_bwd_part(dx1b, w3_1b, saved1b, 0, w3_1b.shape[1] // FFN_TN, dh1, name="ffn1_bwd_b")
    dx, _, dg1 = _rms_bwd(x, g1, dh1, dx1, tm=512, name="ffn1_drms")
    on_grads("ffn1b", [dw3_1b], None, small=[dg1])
    return dx


def _mesh_place():
    x, y, c = lax.axis_index("x"), lax.axis_index("y"), lax.axis_index("c")
    other_chips = [(1 - x, y), (x, 1 - y), (1 - x, 1 - y)]
    return x, y, c, other_chips


def _handshake(peers):
    barrier = pltpu.get_barrier_semaphore()
    for peer in peers:
        pl.semaphore_signal(barrier, inc=1, device_id=peer, device_id_type=MESH)
    pl.semaphore_wait(barrier, len(peers))


def _sequencer_call(body, operands, out_shapes, sems, *, name, collective_id):
    return pl.kernel(
        body, name=name, out_type=out_shapes, mesh=plsc.ScalarSubcoreMesh(axis_name="sequencer", num_cores=1),
        scratch_types=sems, compiler_params=pltpu.CompilerParams(collective_id=collective_id),
    )(*operands)


def _window(ref, kind, slot, shape):
    if kind == "row":
        rows = pl.ds(pl.multiple_of(slot * shape[-2], 8), shape[-2])
        return ref.at[(slice(None),) * (len(shape) - 2) + (rows,)]
    return ref.at[slot]


def _gathered(shape, kind):
    if kind == "row":
        return tuple(shape[:-2]) + (N_DEV * shape[-2], shape[-1])
    return (N_DEV,) + tuple(shape)


def _half(view, hf):
    if len(view.shape) == 4:
        return view.at[:, hf]
    n = view.shape[-2] // 2
    return view.at[(slice(None),) * (len(view.shape) - 2) + (pl.ds(hf * n, n),)]


def _all_gather(shards, kinds, *, name, collective_id):
    nt = len(shards)

    def body(*refs):
        x_refs, o_refs = refs[:nt], refs[nt:2 * nt]
        send_sems, recv_sems, local_sems = refs[2 * nt:]
        x, y, c, _ = _mesh_place()
        me, sibling, xn, yn, diag = (x, y, c), (x, y, 1 - c), (1 - x, y, c), (x, 1 - y, c), (1 - x, 1 - y, c)
        _handshake([sibling, xn, yn])

        def win(t, block):
            bx, by, bc = block
            return _window(o_refs[t], kinds[t], 4 * bx + 2 * by + bc, shards[t].shape)

        def copy(k, t, src, dst, to):
            return pltpu.make_async_remote_copy(src_ref=src, dst_ref=dst, send_sem=send_sems.at[k, t],
                                                recv_sem=recv_sems.at[k, t], device_id=to, device_id_type=MESH)

        def piece(k, t, block, hf, to, from_shard=False):
            dst = _half(win(t, block), hf)
            return copy(k, t, _half(x_refs[t], hf) if from_shard else dst, dst, to)

        mine = [pltpu.make_async_copy(x_refs[t], win(t, me), local_sems.at[t]) for t in range(nt)]
        sent = []

        def start(cp):
            cp.start()
            sent.append(cp)

        for cp in mine:
            cp.start()
        for t in range(nt):
            start(copy(0, t, x_refs[t], win(t, me), sibling))
        for hf_x, hf_y in ((0, 1), (1, 0)):
            for t in range(nt):
                start(piece(1 + hf_x, t, me, hf_x, xn, True))
                start(piece(3 + hf_y, t, me, hf_y, yn, True))
        for k, block, hf, onward, k_sib in ((1, xn, 0, (5, yn), 7), (4, yn, 1, (6, xn), 10), (2, xn, 1, None, 8),
                                           (3, yn, 0, None, 9), (5, diag, 0, None, 11), (6, diag, 1, None, 12)):
            for t in range(nt):
                piece(k, t, block, hf, me).wait_recv()
                if onward is not None:
                    start(piece(onward[0], t, block, hf, onward[1]))
                start(piece(k_sib, t, block, hf, sibling))
        for t in range(nt):
            copy(0, t, x_refs[t], win(t, sibling), me).wait_recv()
        for k_sib, block, hf in ((7, xn, 0), (10, yn, 1), (8, xn, 1), (9, yn, 0), (11, diag, 0), (12, diag, 1)):
            for t in range(nt):
                bx, by, _ = block
                piece(k_sib, t, (bx, by, 1 - c), hf, me).wait_recv()
        for cp in sent:
            cp.wait_send()
        for cp in mine:
            cp.wait()

    return _sequencer_call(
        body, shards, [SDS(_gathered(s.shape, k), s.dtype) for s, k in zip(shards, kinds)],
        [pltpu.SemaphoreType.DMA((13, nt)), pltpu.SemaphoreType.DMA((13, nt)), pltpu.SemaphoreType.DMA((nt,))],
        name=name, collective_id=collective_id)


def _part_shape(shape, kind):
    if kind == "row":
        return tuple(shape[:-2]) + (shape[-2] // N_DEV, shape[-1])
    return tuple(shape[2:])


def _pair_exchange(grads, kinds, *, name, collective_id):
    nt = len(grads)
    part = [_part_shape(g.shape, k) for g, k in zip(grads, kinds)]

    def body(*refs):
        g_refs, o_refs = refs[:nt], refs[nt:2 * nt]
        send_sems, recv_sems = refs[2 * nt:]
        x, y, c, _ = _mesh_place()
        _handshake([(x, y, 1 - c)])
        copies = []
        for t in range(nt):
            for xy in range(4):
                src = g_refs[t].at[1 - c, xy] if kinds[t] == "stack" else _window(g_refs[t], kinds[t], 2 * xy + 1 - c, part[t])
                copies.append(pltpu.make_async_remote_copy(
                    src_ref=src, dst_ref=o_refs[t].at[xy], send_sem=send_sems.at[xy, t], recv_sem=recv_sems.at[xy, t],
                    device_id=(x, y, 1 - c), device_id_type=MESH))
        for cp in copies:
            cp.start()
        for cp in copies:
            cp.wait()

    return _sequencer_call(
        body, grads, [SDS((4,) + p, g.dtype) for p, g in zip(part, grads)],
        [pltpu.SemaphoreType.DMA((4, nt)), pltpu.SemaphoreType.DMA((4, nt))], name=name, collective_id=collective_id)


def _chip_exchange(parts, small, *, name, collective_id):
    nt = len(parts)
    if small is None:
        def body_plain(*refs):
            s_refs, o_refs = refs[:nt], refs[nt:2 * nt]
            send_sems, recv_sems = refs[2 * nt:]
            x, y, c, chips = _mesh_place()
            _handshake([(*chip, c) for chip in chips])
            copies = [pltpu.make_async_remote_copy(
                src_ref=s_refs[t].at[2 * chip[0] + chip[1]], dst_ref=o_refs[t].at[j],
                send_sem=send_sems.at[j, t], recv_sem=recv_sems.at[j, t], device_id=(*chip, c), device_id_type=MESH)
                for j, chip in enumerate(chips) for t in range(nt)]
            for cp in copies:
                cp.start()
            for cp in copies:
                cp.wait()

        return _sequencer_call(
            body_plain, parts, [SDS((3,) + s.shape[1:], s.dtype) for s in parts],
            [pltpu.SemaphoreType.DMA((3, nt)), pltpu.SemaphoreType.DMA((3, nt))], name=name, collective_id=collective_id)

    def body(*refs):
        s_refs, small_ref = refs[:nt], refs[nt]
        o_refs, small_all = refs[nt + 1:2 * nt + 1], refs[2 * nt + 1]
        send_sems, recv_sems, small_send, small_recv, local_sem = refs[2 * nt + 2:]
        x, y, c, chips = _mesh_place()
        _handshake([(px, py, pc) for px in (x, 1 - x) for py in (y, 1 - y) for pc in (c, 1 - c)][1:])

        def copy(j, t, chip):
            return pltpu.make_async_remote_copy(
                src_ref=s_refs[t].at[2 * chip[0] + chip[1]], dst_ref=o_refs[t].at[j],
                send_sem=send_sems.at[j, t], recv_sem=recv_sems.at[j, t], device_id=(*chip, c), device_id_type=MESH)

        flips = [(fx, fy, fc) for fx in (0, 1) for fy in (0, 1) for fc in (0, 1)][1:]

        def small_copy(k):
            fx, fy, fc = flips[k]
            to = (x ^ fx if fx else x, y ^ fy if fy else y, c ^ fc if fc else c)
            rows = small_all.at[4 * x + 2 * y + c]
            return pltpu.make_async_remote_copy(
                src_ref=small_ref, dst_ref=rows, send_sem=small_send.at[k], recv_sem=small_recv.at[k],
                device_id=to, device_id_type=MESH)

        own = pltpu.make_async_copy(small_ref, small_all.at[4 * x + 2 * y + c], local_sem)
        own.start()
        copies = [copy(j, t, chip) for j, chip in enumerate(chips) for t in range(nt)]
        smalls = [small_copy(k) for k in range(7)]
        for cp in smalls + copies:
            cp.start()
        for cp in smalls + copies:
            cp.wait()
        own.wait()

    return _sequencer_call(
        body, list(parts) + [small],
        [SDS((3,) + s.shape[1:], s.dtype) for s in parts] + [SDS((N_DEV,) + small.shape, small.dtype)],
        [pltpu.SemaphoreType.DMA((3, nt)), pltpu.SemaphoreType.DMA((3, nt)),
         pltpu.SemaphoreType.DMA((7,)), pltpu.SemaphoreType.DMA((7,)), pltpu.SemaphoreType.DMA],
        name=name, collective_id=collective_id)


def _pair_sum(grad, theirs, kind, c, *, name):
    if kind == "row":
        r, l = theirs.shape[-2:]
        n = theirs.size // (4 * r * l)
        grad, theirs = grad.reshape(n, N_DEV * r, l), theirs.reshape(4, n, r, l)
        mine_spec = pl.BlockSpec((n, r, l), lambda xy, c_ref: (0, 2 * xy + c_ref[0], 0))
    else:
        r, l = theirs.shape[-2:]
        n = theirs.size // (4 * r * l)
        theirs = theirs.reshape(4, n, r, l)
        grad = grad.reshape(2, 4, n, r, l)
        mine_spec = pl.BlockSpec((None, None, n, r, l), lambda xy, c_ref: (c_ref[0], xy, 0, 0, 0))

    def body(c_ref, a_ref, b_ref, o_ref):
        o_ref[...] = (a_ref[...].astype(F32) + b_ref[...].astype(F32)).astype(BF16)

    part = pl.BlockSpec((None, n, r, l), lambda xy, c_ref: (xy, 0, 0, 0))
    return pl.pallas_call(
        body, name=name,
        grid_spec=pltpu.PrefetchScalarGridSpec(num_scalar_prefetch=1, grid=(4,), in_specs=[mine_spec, part], out_specs=part),
        out_shape=SDS((4, n, r, l), BF16), compiler_params=_params("parallel"),
    )(c, grad, theirs)


def _adamw(w, g, m, v):
    m = ADAM_B1 * m + (1.0 - ADAM_B1) * g
    v = ADAM_B2 * v + (1.0 - ADAM_B2) * jnp.square(g)
    m_hat = m / (1.0 - ADAM_B1 ** ADAM_STEP)
    v_hat = v / (1.0 - ADAM_B2 ** ADAM_STEP)
    delta = -ADAM_LR * (m_hat / (jnp.sqrt(v_hat) + ADAM_EPS) + ADAM_WD * w)
    return delta, m, v


def _adam_big(owns, others, mat, xy, w, m, v, *, tr, name):
    _, r, l = w.shape
    lp = owns[0].shape[-1]
    nq = len(owns)
    rows = [tr] if nq == 1 else [o.shape[-2] for o in owns]
    assert sum(rows) == tr and r % tr == 0, (r, tr, rows)

    def body(xy_ref, *refs):
        own_refs, oth_refs = refs[:nq], refs[nq:2 * nq]
        w_ref, m_ref, v_ref, g_out, d_out, m_out, v_out = refs[2 * nq:]
        parts = []
        for q in range(nq):
            gq = own_refs[q][0, 0].astype(F32)
            for j in range(3):
                gq = gq + oth_refs[q][j, 0].astype(F32)
            parts.append(gq)
        g = (parts[0] if nq == 1 else jnp.concatenate(parts, axis=0))[:, :l]
        delta, m_new, v_new = _adamw(w_ref[0], g, m_ref[0], v_ref[0])
        g_out[0] = g
        d_out[0] = delta
        m_out[0] = m_new
        v_out[0] = v_new

    blk = pl.BlockSpec((1, tr, l), lambda i, xy_ref: (0, i, 0))
    own_specs = [pl.BlockSpec((1, 1, n, lp), lambda i, xy_ref: (xy_ref[0], mat, i, 0)) for n in rows]
    oth_specs = [pl.BlockSpec((3, 1, n, lp), lambda i, xy_ref: (0, mat, i, 0)) for n in rows]
    return pl.pallas_call(
        body, name=name,
        grid_spec=pltpu.PrefetchScalarGridSpec(
            num_scalar_prefetch=1, grid=(r // tr,), in_specs=own_specs + oth_specs + [blk, blk, blk],
            out_specs=[blk, blk, blk, blk]),
        out_shape=[SDS(w.shape, F32)] * 4, compiler_params=_params("parallel"),
    )(xy, *owns, *others, w, m, v)


def _adam_cols(own, other, xy, w, m, v, *, tc, name):
    _, wc, d = w.shape
    wp = -(-wc // 128) * 128

    def body(xy_ref, own_ref, oth_ref, w_ref, m_ref, v_ref, g_out, d_out, m_out, v_out, buf):
        g = own_ref[0, 0].astype(F32)
        for j in range(3):
            g = g + oth_ref[j, 0].astype(F32)
        buf[:, :wc] = g
        buf[:, wc:] = jnp.zeros((tc, wp - wc), F32)
        g = buf[...].T[:wc]
        delta, m_new, v_new = _adamw(w_ref[0], g, m_ref[0], v_ref[0])
        g_out[0] = g
        d_out[0] = delta
        m_out[0] = m_new
        v_out[0] = v_new

    blk = pl.BlockSpec((1, wc, tc), lambda i, xy_ref: (0, 0, i))
    in_specs = [pl.BlockSpec((1, 1, tc, wc), lambda i, xy_ref: (xy_ref[0], 0, i, 0)),
                pl.BlockSpec((3, 1, tc, wc), lambda i, xy_ref: (0, 0, i, 0)), blk, blk, blk]
    return pl.pallas_call(
        body, name=name,
        grid_spec=pltpu.PrefetchScalarGridSpec(
            num_scalar_prefetch=1, grid=(d // tc,), in_specs=in_specs, out_specs=[blk, blk, blk, blk],
            scratch_shapes=[pltpu.VMEM((tc, wp), F32)]),
        out_shape=[SDS(w.shape, F32)] * 4, compiler_params=_params("parallel"),
    )(xy, own, other, w, m, v)


def _small_layout(shapes):
    out, at = [], 0
    for r, c in shapes:
        rows = c // 128 if (r == 1 and c > 128) else r
        out.append((at, rows))
        at += -(-rows // 8) * 8
    return out, at


def _pack_small(parts, *, name):
    shapes = [a.shape for a in parts]
    layout, total = _small_layout(shapes)

    def body(*refs):
        o_ref = refs[-1]
        o_ref[...] = jnp.zeros_like(o_ref)
        for x_ref, (r, c), (at, rows) in zip(refs, shapes, layout):
            if r == 1 and c > 128:
                for k in range(rows):
                    o_ref[at + k:at + k + 1, :] = x_ref[:, k * 128:(k + 1) * 128]
            else:
                o_ref[at:at + r, 0:c] = x_ref[...]

    return pl.pallas_call(body, name=name, out_shape=SDS((total, 128), F32))(*parts)


def _adam_small(g_all, ws, ms, vs, *, name):
    n = len(ws)
    shapes = [w.shape for w in ws]
    layout, _ = _small_layout(shapes)

    def body(g_ref, *refs):
        w_refs, m_refs, v_refs, outs = refs[:n], refs[n:2 * n], refs[2 * n:3 * n], refs[3 * n:]
        g_sum = g_ref[0]
        for k in range(1, N_DEV):
            g_sum = g_sum + g_ref[k]
        for i, ((r, c), (at, rows)) in enumerate(zip(shapes, layout)):
            if r == 1 and c > 128:
                g = jnp.concatenate([g_sum[at + k:at + k + 1, :] for k in range(rows)], axis=1)
            else:
                g = g_sum[at:at + r, 0:c]
            delta, m_new, v_new = _adamw(w_refs[i][...], g, m_refs[i][...], v_refs[i][...])
            for q, val in enumerate((g, delta, m_new, v_new)):
                outs[4 * i + q][...] = val

    flat = pl.pallas_call(body, name=name, out_shape=[SDS(s, F32) for s in shapes for _ in range(4)])(g_all, *ws, *ms, *vs)
    return [flat[4 * i:4 * i + 4] for i in range(n)]


def kernel(x, mem, ffn1_norm, ffn1_w_gate, ffn1_w_up, ffn1_w_down, mix_norm, mem_norm, w_in, w_mem_kv, swa_q_norm, swa_k_norm, swa_sinks, rel_bias, gla_w_gate_up, gla_b_gate, gla_out_norm, mem_q_norm, mem_k_norm, w_out, ffn2_norm, ffn2_w_gate, ffn2_w_up, ffn2_w_down, loss_target, m_ffn1_norm, m_ffn1_w_gate, m_ffn1_w_up, m_ffn1_w_down, m_mix_norm, m_mem_norm, m_w_in, m_w_mem_kv, m_swa_q_norm, m_swa_k_norm, m_swa_sinks, m_rel_bias, m_gla_w_gate_up, m_gla_b_gate, m_gla_out_norm, m_mem_q_norm, m_mem_k_norm, m_w_out, m_ffn2_norm, m_ffn2_w_gate, m_ffn2_w_up, m_ffn2_w_down, v_ffn1_norm, v_ffn1_w_gate, v_ffn1_w_up, v_ffn1_w_down, v_mix_norm, v_mem_norm, v_w_in, v_w_mem_kv, v_swa_q_norm, v_swa_k_norm, v_swa_sinks, v_rel_bias, v_gla_w_gate_up, v_gla_b_gate, v_gla_out_norm, v_mem_q_norm, v_mem_k_norm, v_w_out, v_ffn2_norm, v_ffn2_w_gate, v_ffn2_w_up, v_ffn2_w_down):
    xi, yi, ci = lax.axis_index("x"), lax.axis_index("y"), lax.axis_index("c")
    c_arr = jnp.reshape(ci, (1,)).astype(jnp.int32)
    xy_arr = jnp.reshape(2 * xi + yi, (1,)).astype(jnp.int32)
    d = x.shape[-1]

    def ffn_shards(wg_s, wu_s, wd_s):
        return jnp.concatenate([wg_s.transpose(0, 2, 1), wu_s.transpose(0, 2, 1), wd_s], axis=0).astype(BF16)

    def gather_ffn(wg_s, wu_s, wd_s, name, collective_id, after):
        w3_s, _ = lax.optimization_barrier((ffn_shards(wg_s, wu_s, wd_s), after))
        return _all_gather([w3_s], ["row"], name=name, collective_id=collective_id)[0]

    w3_s = ffn_shards(ffn1_w_gate, ffn1_w_up, ffn1_w_down)
    w3_1a = _all_gather([w3_s[:, :FFN1_FIRST]], ["row"], name="gather_ffn1a", collective_id=0)[0]
    w3_s, _ = lax.optimization_barrier((w3_s, w3_1a))
    w3_1b = _all_gather([w3_s[:, FFN1_FIRST:]], ["row"], name="gather_ffn1b", collective_id=11)[0]

    def gather_mix(after):
        mix_s = lax.optimization_barrier((w_in[0].astype(BF16), w_mem_kv[0].astype(BF16), w_out[0].astype(BF16),
                                          (w3_1b, after)))[:3]
        win_all, wkv, wout = _all_gather(list(mix_s), ["stack", "row", "row"], name="gather_mix", collective_id=1)
        return _pack_win(win_all, tr=256, name="pack_w_in"), wkv, wout

    def gather_ffn2(after):
        return gather_ffn(ffn2_w_gate, ffn2_w_up, ffn2_w_down, "gather_ffn2", 2, after)

    small_w = [ffn1_norm, mix_norm, mem_norm, ffn2_norm, swa_q_norm, swa_k_norm, swa_sinks[0], rel_bias,
               gla_w_gate_up[0], gla_b_gate, gla_out_norm, mem_q_norm, mem_k_norm]
    collective_ids = {"ffn2": (3, 4), "mix": (5, 6), "ffn1a": (7, 8), "ffn1b": (9, 10)}
    reduced, small_box = {}, {}

    def on_grads(group, grads, carry, small=None):
        if group == "mix":
            grads = list(grads)
            kinds = ["stack", "row", "row"]
        else:
            kinds = ["row"]
        if reduced:
            earlier = list(reduced.values())[-1][1]
            *grads, _ = lax.optimization_barrier((*grads, earlier[0]))
        id_pair, id_chip = collective_ids[group]
        from_sibling = _pair_exchange(grads, kinds, name=f"pair_exchange_{group}", collective_id=id_pair)
        chip_sums = [_pair_sum(g, theirs, k, c_arr, name=f"pair_sum_{group}_{t}")
                     for t, (g, theirs, k) in enumerate(zip(grads, from_sibling, kinds))]
        if carry is not None:
            *chip_sums, carry = lax.optimization_barrier((*chip_sums, carry))
        if small is None:
            from_chips = _chip_exchange(chip_sums, None, name=f"chip_exchange_{group}", collective_id=id_chip)
        else:
            packed = _pack_small(small, name=f"pack_small_{group}")
            *from_chips, small_all = _chip_exchange(chip_sums, packed, name=f"chip_exchange_{group}",
                                                    collective_id=id_chip)
            small_box[group] = small_all
        reduced[group] = (chip_sums, from_chips)
        return carry

    grad_x = _local_step(x[0], mem[0], loss_target[0], small_w, ((w3_1a, w3_1b), gather_mix, gather_ffn2), on_grads)

    big_w = {"ffn1_w_gate": ("ffn1", 0, 0, True, ffn1_w_gate, m_ffn1_w_gate, v_ffn1_w_gate),
             "ffn1_w_up": ("ffn1", 0, 1, True, ffn1_w_up, m_ffn1_w_up, v_ffn1_w_up),
             "ffn1_w_down": ("ffn1", 0, 2, False, ffn1_w_down, m_ffn1_w_down, v_ffn1_w_down),
             "w_in": ("mix", 0, 0, True, w_in, m_w_in, v_w_in),
             "w_mem_kv": ("mix", 1, 0, False, w_mem_kv, m_w_mem_kv, v_w_mem_kv),
             "w_out": ("mix", 2, 0, False, w_out, m_w_out, v_w_out),
             "ffn2_w_gate": ("ffn2", 0, 0, True, ffn2_w_gate, m_ffn2_w_gate, v_ffn2_w_gate),
             "ffn2_w_up": ("ffn2", 0, 1, True, ffn2_w_up, m_ffn2_w_up, v_ffn2_w_up),
             "ffn2_w_down": ("ffn2", 0, 2, False, ffn2_w_down, m_ffn2_w_down, v_ffn2_w_down)}
    res = {}
    for nm, (group, t, mat, transposed, w, m, v) in big_w.items():
        shape = w.shape
        if transposed:
            w, m, v = (a.transpose(0, 2, 1) for a in (w, m, v))
        if nm == "w_in":
            out = _adam_cols(reduced[group][0][t], reduced[group][1][t], xy_arr, w, m, v, tc=256, name=f"adam_{nm}")
            res[nm] = [a.transpose(0, 2, 1) for a in out]
            continue
        r = w.shape[1]
        halves = ["ffn1a", "ffn1b"] if group == "ffn1" else [group]
        if len(halves) == 1:
            tr = 256 if r % 256 == 0 else r
        else:
            tr = r
        out = _adam_big([reduced[k][0][t] for k in halves], [reduced[k][1][t] for k in halves], mat, xy_arr, w, m, v,
                        tr=tr, name=f"adam_{nm}")
        if transposed:
            out = [a.reshape(1, -1, d).transpose(0, 2, 1) for a in out]
        res[nm] = [a.reshape(shape) for a in out]
    small_names = ["ffn1_norm", "mix_norm", "mem_norm", "ffn2_norm", "swa_q_norm", "swa_k_norm", "swa_sinks", "rel_bias",
                   "gla_w_gate_up", "gla_b_gate", "gla_out_norm", "mem_q_norm", "mem_k_norm"]
    small_m = [m_ffn1_norm, m_mix_norm, m_mem_norm, m_ffn2_norm, m_swa_q_norm, m_swa_k_norm, m_swa_sinks, m_rel_bias,
               m_gla_w_gate_up, m_gla_b_gate, m_gla_out_norm, m_mem_q_norm, m_mem_k_norm]
    small_v = [v_ffn1_norm, v_mix_norm, v_mem_norm, v_ffn2_norm, v_swa_q_norm, v_swa_k_norm, v_swa_sinks, v_rel_bias,
               v_gla_w_gate_up, v_gla_b_gate, v_gla_out_norm, v_mem_q_norm, v_mem_k_norm]
    small_full = [ffn1_norm, mix_norm, mem_norm, ffn2_norm, swa_q_norm, swa_k_norm, swa_sinks, rel_bias,
                  gla_w_gate_up, gla_b_gate, gla_out_norm, mem_q_norm, mem_k_norm]
    zero = jnp.zeros((1, 1), F32)
    turned = ("rel_bias",)

    def two_d(nm, a):
        a = a.reshape(a.shape[-2:])
        return a.T if nm in turned else a

    for group, sel in (("ffn1a", slice(1, None)), ("ffn1b", slice(0, 1))):
        extra = [zero] if group == "ffn1a" else []
        ws, ms, vs = ([two_d(nm, a) for nm, a in zip(small_names[sel], arrs[sel])] + extra
                      for arrs in (small_full, small_m, small_v))
        updated = _adam_small(small_box[group], ws, ms, vs, name=f"adam_small_{group}")
        for nm, full, out in zip(small_names[sel], small_full[sel], updated):
            res[nm] = [(a.T if nm in turned else a).reshape(full.shape) for a in out]
        if extra:
            loss = updated[-1][0].reshape(())

    order = ["ffn1_norm", "ffn1_w_gate", "ffn1_w_up", "ffn1_w_down", "mix_norm", "mem_norm", "w_in", "w_mem_kv",
             "swa_q_norm", "swa_k_norm", "swa_sinks", "rel_bias", "gla_w_gate_up", "gla_b_gate", "gla_out_norm",
             "mem_q_norm", "mem_k_norm", "w_out", "ffn2_norm", "ffn2_w_gate", "ffn2_w_up", "ffn2_w_down"]
    outs = [loss, grad_x[None]]
    for q in range(4):
        outs += [res[nm][q] for nm in order]
    return tuple(outs)
```

```python
import functools
import math

import numpy as np
import jax
import jax.numpy as jnp
from jax import lax
from jax.experimental import pallas as pl
from jax.experimental.pallas import tpu as pltpu
from jax.experimental.pallas import tpu_sc as plsc

F32 = jnp.float32
BF16 = jnp.bfloat16
SDS = jax.ShapeDtypeStruct

EPS = 1e-6
HEAD_DIM = 64
SWA_HEADS = 8
SWA_KV_HEADS = 2
SWA_GROUP = SWA_HEADS // SWA_KV_HEADS
BLOCK = 128
N_BUCKETS = 32
MAX_DISTANCE = 128
GLA_HEADS = 4
GLA_DK = 32
GLA_DV = 64
GLA_RANK = 16
GLA_TAU = 16.0
GLA_CHUNK = 32
MEM_HEADS = 4
SWA_Q_W = SWA_HEADS * HEAD_DIM
SWA_KV_W = SWA_KV_HEADS * HEAD_DIM
GLA_QK_W = GLA_HEADS * GLA_DK
GLA_V_W = GLA_HEADS * GLA_DV
MEM_Q_W = MEM_HEADS * HEAD_DIM
IN_W = 1808
IN_W_PAD = 1920
COL_SQ, COL_SKV, COL_GQ, COL_GK, COL_GV, COL_GG, COL_MQ, COL_GLR = 0, 512, 768, 896, 1024, 1280, 1536, 1792

ADAM_LR = 0.001
ADAM_B1 = 0.9
ADAM_B2 = 0.999
ADAM_EPS = 1e-08
ADAM_WD = 0.01
ADAM_STEP = 10

N_DEV = 8
VMEM_LIMIT_BYTES = 56 * 1024 * 1024
MESH = pl.DeviceIdType.MESH


def _params(*sem):
    return pltpu.CompilerParams(dimension_semantics=sem or None, vmem_limit_bytes=VMEM_LIMIT_BYTES)


def _dot(a, b, ta, tb, precision=None):
    dims = (((0 if ta else 1,), (1 if tb else 0,)), ((), ()))
    return lax.dot_general(a, b, dims, preferred_element_type=F32, precision=precision)


def _mm_raw(a, b, ta=False, tb=False):
    return _dot(a.astype(BF16), b.astype(BF16), ta, tb)


def _mmf_raw(a, b, ta=False, tb=False):
    return _dot(a, b, ta, tb, lax.Precision.HIGHEST)


def _make_mm(raw):
    @functools.partial(jax.custom_vjp, nondiff_argnums=(2, 3))
    def mm(a, b, ta=False, tb=False):
        return raw(a, b, ta, tb)

    def fwd(a, b, ta, tb):
        return raw(a, b, ta, tb), (a, b)

    def bwd(ta, tb, res, g):
        a, b = res
        da = raw(b, g, tb, True) if ta else raw(g, b, False, not tb)
        db = raw(g, a, True, ta) if tb else raw(a, g, not ta, False)
        return da, db

    mm.defvjp(fwd, bwd)
    return mm


_mm = _make_mm(_mm_raw)
_mmf = _make_mm(_mmf_raw)


def _mm3(a, b, ta=False, tb=False):
    a_hi, b_hi = a.astype(BF16).astype(F32), b.astype(BF16).astype(F32)
    return _mm(a_hi, b_hi, ta, tb) + _mm(a_hi, b - b_hi, ta, tb) + _mm(a - a_hi, b_hi, ta, tb)


def _rms(x, g):
    return x * lax.rsqrt(jnp.mean(x * x, axis=-1, keepdims=True) + EPS) * g


def _silu_mul(g, u):
    return jax.nn.silu(g) * u


def _log_sigmoid(z):
    return jnp.minimum(z, 0.0) - jnp.log(1.0 + jnp.exp(-jnp.abs(z)))


def _matmul(a_list, b, *, ta=False, tb=False, tm, tn, b_blocks=None, res=None, scale=1.0, out_dtype=F32, name):
    if not isinstance(a_list, (list, tuple)):
        a_list = [a_list]
    n_a = len(a_list)
    m = a_list[0].shape[1] if ta else a_list[0].shape[0]
    ks = [a.shape[0] if ta else a.shape[1] for a in a_list]
    n = b.shape[0] if tb else b.shape[1]
    if b_blocks is None:
        assert n_a == 1
        b_blocks = [0]
    tm, tn = min(tm, m), min(tn, n)
    assert m % tm == 0 and n % tn == 0, (m, n, tm, tn)

    def body(*refs):
        a_refs, b_refs = refs[:n_a], refs[n_a:2 * n_a]
        r_ref = refs[2 * n_a] if res is not None else None
        o_ref = refs[-1]
        acc = _mm_raw(a_refs[0][...], b_refs[0][...], ta, tb)
        for k in range(1, n_a):
            acc = acc + _mm_raw(a_refs[k][...], b_refs[k][...], ta, tb)
        if scale != 1.0:
            acc = acc * scale
        if r_ref is not None:
            acc = r_ref[...] + acc
        o_ref[...] = acc.astype(out_dtype)

    in_specs = []
    for k in ks:
        in_specs.append(pl.BlockSpec((k, tm), lambda i, j: (0, i)) if ta else pl.BlockSpec((tm, k), lambda i, j: (i, 0)))
    for k, blk in zip(ks, b_blocks):
        if tb:
            in_specs.append(pl.BlockSpec((tn, k), functools.partial(lambda i, j, blk: (j, blk), blk=blk)))
        else:
            in_specs.append(pl.BlockSpec((k, tn), functools.partial(lambda i, j, blk: (blk, j), blk=blk)))
    args = list(a_list) + [b] * n_a
    if res is not None:
        in_specs.append(pl.BlockSpec((tm, tn), lambda i, j: (i, j)))
        args.append(res)
    return pl.pallas_call(
        body, name=name, grid=(m // tm, n // tn), in_specs=in_specs,
        out_specs=pl.BlockSpec((tm, tn), lambda i, j: (i, j)), out_shape=SDS((m, n), out_dtype),
        compiler_params=_params("parallel", "parallel"),
    )(*args)


def _win_pieces(w):
    glr_lo, glr_hi = COL_MQ, COL_MQ + GLA_RANK
    out = []
    for j in range(N_DEV):
        for lo, hi, shift in ((0, glr_lo, 0), (glr_lo, glr_hi, COL_GLR - glr_lo), (glr_hi, IN_W, COL_MQ - glr_hi)):
            s, e = max(j * w, lo), min((j + 1) * w, hi)
            if s < e:
                out.append((j, s - j * w, e - j * w, s + shift))
    return out


def _pack_win(win_all, *, tr, name):
    _, d, w = win_all.shape

    def body(i_ref, o_ref):
        for j, a, b, dst in _win_pieces(w):
            o_ref[:, dst:dst + b - a] = i_ref[j][:, a:b]
        o_ref[:, IN_W:] = jnp.zeros((tr, IN_W_PAD - IN_W), o_ref.dtype)

    return pl.pallas_call(
        body, name=name, grid=(d // tr,), in_specs=[pl.BlockSpec((N_DEV, tr, w), lambda i: (0, i, 0))],
        out_specs=pl.BlockSpec((tr, IN_W_PAD), lambda i: (i, 0)), out_shape=SDS((d, IN_W_PAD), win_all.dtype),
        compiler_params=_params("parallel"),
    )(win_all)


def _dw_in(h, parts, *, tk, name):
    s, d = h.shape
    n_p = len(parts)
    w = IN_W // N_DEV
    starts = [sum(p.shape[1] for p in parts[:k]) for k in range(n_p + 1)]

    def body(*refs):
        h_ref, p_refs, o_ref, acc_refs = refs[0], refs[1:1 + n_p], refs[1 + n_p], refs[2 + n_p:]
        i = pl.program_id(0)
        h_t = h_ref[...].T

        @pl.when(i == 0)
        def _():
            for acc_ref in acc_refs:
                acc_ref[...] = jnp.zeros_like(acc_ref)

        for p_ref, acc_ref in zip(p_refs, acc_refs):
            acc_ref[...] += _mm_raw(h_t, p_ref[...], False, False)

        @pl.when(i == pl.num_programs(0) - 1)
        def _():
            for j, a, b, src in _win_pieces(w):
                for k, acc_ref in enumerate(acc_refs):
                    lo, hi = max(src, starts[k]), min(src + b - a, starts[k + 1])
                    if lo < hi:
                        o_ref[j % 2, j // 2, :, a + lo - src:a + hi - src] = (
                            acc_ref[:, lo - starts[k]:hi - starts[k]].astype(BF16))

    return pl.pallas_call(
        body, name=name, grid=(s // tk,),
        in_specs=[pl.BlockSpec((tk, d), lambda i: (i, 0))] + [pl.BlockSpec((tk, p.shape[1]), lambda i: (i, 0)) for p in parts],
        out_specs=pl.BlockSpec((2, 4, d, w), lambda i: (0, 0, 0, 0)), out_shape=SDS((2, 4, d, w), BF16),
        scratch_shapes=[pltpu.VMEM((d, p.shape[1]), F32) for p in parts],
        compiler_params=_params("arbitrary"),
    )(h, *parts)


def _dw_rows(parts, b, *, tk, name):
    s, n = b.shape
    n_p = len(parts)
    starts = [sum(p.shape[1] for p in parts[:k]) for k in range(n_p + 1)]

    def body(*refs):
        p_refs, b_ref, o_ref, acc_ref = refs[:n_p], refs[n_p], refs[n_p + 1], refs[n_p + 2]
        i = pl.program_id(0)

        @pl.when(i == 0)
        def _():
            acc_ref[...] = jnp.zeros_like(acc_ref)

        for k, p_ref in enumerate(p_refs):
            acc_ref[starts[k]:starts[k + 1], :] += _mm_raw(p_ref[...], b_ref[...], True, False)

        @pl.when(i == pl.num_programs(0) - 1)
        def _():
            o_ref[...] = acc_ref[...].astype(BF16)

    return pl.pallas_call(
        body, name=name, grid=(s // tk,),
        in_specs=[pl.BlockSpec((tk, p.shape[1]), lambda i: (i, 0)) for p in parts] + [pl.BlockSpec((tk, n), lambda i: (i, 0))],
        out_specs=pl.BlockSpec((starts[-1], n), lambda i: (0, 0)), out_shape=SDS((starts[-1], n), BF16),
        scratch_shapes=[pltpu.VMEM((starts[-1], n), F32)], compiler_params=_params("arbitrary"),
    )(*parts, b)


def _rms_fwd(x, g, *, tm, name):
    s, d = x.shape

    def body(x_ref, g_ref, h_ref):
        h_ref[...] = _rms(x_ref[...], g_ref[...]).astype(BF16)

    return pl.pallas_call(
        body, name=name, grid=(s // tm,),
        in_specs=[pl.BlockSpec((tm, d), lambda i: (i, 0)), pl.BlockSpec((1, d), lambda i: (0, 0))],
        out_specs=pl.BlockSpec((tm, d), lambda i: (i, 0)), out_shape=SDS((s, d), BF16),
        compiler_params=_params("parallel"),
    )(x, g)


def _rms_bwd(x, g, dh, dres, *, tm, name):
    s, d = x.shape
    want_dx = dres is not None
    product = isinstance(dh, tuple)
    if product:
        a_parts = list(dh[0]) if isinstance(dh[0], (list, tuple)) else [dh[0]]
        starts = [sum(a.shape[1] for a in a_parts[:k]) for k in range(len(a_parts) + 1)]

    def body(*refs):
        n_dh = len(a_parts) + 1 if product else 1
        x_ref, g_ref = refs[:2]
        dh_refs, rest = refs[2:2 + n_dh], refs[2 + n_dh:]
        if want_dx:
            dres_ref, dx_ref, dxb_ref, dg_ref = rest
        else:
            dg_ref, = rest
        if product:
            b_ref = dh_refs[-1]
            dh_tile = _mm_raw(dh_refs[0][...], b_ref[:, starts[0]:starts[1]], False, True)
            for k in range(1, len(a_parts)):
                dh_tile = dh_tile + _mm_raw(dh_refs[k][...], b_ref[:, starts[k]:starts[k + 1]], False, True)
        else:
            dh_tile = dh_refs[0][...]
        _, vjp = jax.vjp(_rms, x_ref[...], g_ref[...])
        dx, dg = vjp(dh_tile)
        if want_dx:
            dx = dres_ref[...] + dx
            dx_ref[...] = dx
            dxb_ref[...] = dx.astype(BF16)

        @pl.when(pl.program_id(0) == 0)
        def _():
            dg_ref[...] = jnp.zeros_like(dg_ref)

        dg_ref[...] += dg

    row = pl.BlockSpec((tm, d), lambda i: (i, 0))
    vec = pl.BlockSpec((1, d), lambda i: (0, 0))
    if product:
        dh_specs = [pl.BlockSpec((tm, a.shape[1]), lambda i: (i, 0)) for a in a_parts]
        dh_specs.append(pl.BlockSpec((d, starts[-1]), lambda i: (0, 0)))
        dh_args = a_parts + [dh[1]]
    else:
        dh_specs, dh_args = [row], [dh]
    if want_dx:
        return pl.pallas_call(
            body, name=name, grid=(s // tm,), in_specs=[row, vec] + dh_specs + [row], out_specs=[row, row, vec],
            out_shape=[SDS((s, d), F32), SDS((s, d), BF16), SDS((1, d), F32)], compiler_params=_params("arbitrary"),
        )(x, g, *dh_args, dres)
    return None, None, pl.pallas_call(
        body, name=name, grid=(s // tm,), in_specs=[row, vec] + dh_specs, out_specs=vec,
        out_shape=SDS((1, d), F32), compiler_params=_params("arbitrary"),
    )(x, g, *dh_args)


FFN_TN = 256
FFN_TN_FWD = 512
FFN1_FIRST = 192


def _ffn_fwd(x, gain, w3, tag, *, tm=1024, next_gain=None, target=None, start=None, partial=False):
    s, d = x.shape
    f = w3.shape[1]
    tn = FFN_TN_FWD if f % FFN_TN_FWD == 0 else FFN_TN
    nj = f // tn
    tm = min(tm, s)
    n_extra = (next_gain is not None) + (target is not None) + 2 * (start is not None)

    def body(*refs):
        x_ref, gain_ref, wg_ref, wu_ref, wd_ref = refs[:5]
        extra, outs = refs[5:5 + n_extra], refs[5 + n_extra:-1]
        acc_s = refs[-1]
        g_ref, u_ref = outs[-2:]
        h_ref = extra[-2] if start is not None else outs[-3]
        i, j = pl.program_id(0), pl.program_id(1)

        @pl.when(j == 0)
        def _():
            if start is None:
                h_ref[...] = _rms(x_ref[...], gain_ref[...]).astype(BF16)
                acc_s[...] = jnp.zeros_like(acc_s)
            else:
                acc_s[...] = extra[-1][...]

        hv = h_ref[...]
        g = _mm_raw(hv, wg_ref[...], False, True)
        u = _mm_raw(hv, wu_ref[...], False, True)
        g_ref[...] = g.astype(BF16)
        u_ref[...] = u.astype(BF16)
        acc_s[...] += _mm_raw(_silu_mul(g, u), wd_ref[...])

        @pl.when(j == nj - 1)
        def _():
            y = acc_s[...] if partial else x_ref[...] + 0.5 * acc_s[...]
            if target is None:
                outs[0][...] = y
                if next_gain is not None:
                    outs[1][...] = _rms(y, extra[0][...]).astype(BF16)
            else:
                dy_ref, dyb_ref, loss_ref = outs[:3]
                diff = y - extra[0][...]
                dy_ref[...] = diff * (1.0 / d)
                dyb_ref[...] = (diff * (1.0 / d)).astype(BF16)
                part = 0.5 * jnp.sum(jnp.mean(diff * diff, axis=-1, keepdims=True), axis=0, keepdims=True)

                @pl.when(i == 0)
                def _():
                    loss_ref[...] = part

                @pl.when(i > 0)
                def _():
                    loss_ref[...] += part

    row = pl.BlockSpec((tm, d), lambda i, j: (i, 0))
    vec = pl.BlockSpec((1, d), lambda i, j: (0, 0))
    tile = pl.BlockSpec((tm, tn), lambda i, j: (i, j))
    in_specs = [row, vec] + [pl.BlockSpec((None, tn, d), functools.partial(lambda i, j, k: (k, j, 0), k=k)) for k in range(3)]
    args = [x, gain, w3, w3, w3]
    if target is None:
        out_specs, out_shape = [row], [SDS((s, d), F32)]
        if next_gain is not None:
            in_specs.append(vec)
            args.append(next_gain)
            out_specs.append(row)
            out_shape.append(SDS((s, d), BF16))
    else:
        in_specs.append(row)
        args.append(target)
        out_specs = [row, row, pl.BlockSpec((1, 1), lambda i, j: (0, 0))]
        out_shape = [SDS((s, d), F32), SDS((s, d), BF16), SDS((1, 1), F32)]
    if start is None:
        out_specs.append(row)
        out_shape.append(SDS((s, d), BF16))
    else:
        in_specs += [row, row]
        args += list(start)
    *head, g, u = pl.pallas_call(
        body, name=f"{tag}_fwd", grid=(s // tm, nj), in_specs=in_specs,
        out_specs=out_specs + [tile, tile],
        out_shape=out_shape + [SDS((s, f), BF16), SDS((s, f), BF16)],
        scratch_shapes=[pltpu.VMEM((tm, d), F32)],
        compiler_params=_params("arbitrary", "arbitrary"),
    )(*args)
    if start is None:
        *head, h = head
    else:
        h = start[0]
    return head, (h, g, u)


def _ffn_bwd_part(dyb, w3, saved, first, count, dh_init, *, name):
    h, g, u = saved
    s, d = h.shape
    tn = FFN_TN

    def body(*refs):
        if dh_init is None:
            dy_ref, h_ref, wd_ref, wg_ref, wu_ref, g_ref, u_ref, dh_ref, dw3_ref, dg_s, du_s, a_s = refs
        else:
            (dy_ref, h_ref, wd_ref, wg_ref, wu_ref, g_ref, u_ref, dh0_hbm, dh_ref, dw3_ref, dg_s, du_s, a_s,
             dh0_s, dh0_sem) = refs
            dh0_copy = pltpu.make_async_copy(dh0_hbm, dh0_s, dh0_sem)
        j = pl.program_id(0)

        @pl.when(j == 0)
        def _():
            if dh_init is not None:
                dh0_copy.start()
            dh_ref[...] = jnp.zeros_like(dh_ref)
            for ref in (dg_s, du_s, a_s):
                ref[...] = jnp.zeros_like(ref)

        now, before = j % 2, 1 - j % 2
        dyv = dy_ref[...]
        hv = h_ref[...]
        dg, du, a = dg_s[before], du_s[before], a_s[before]
        dh_ref[...] += _mm_raw(dg, wg_ref[...]) + _mm_raw(du, wu_ref[...])
        dw3_ref[0] = _mm_raw(dg, hv, True, False).astype(BF16)
        dw3_ref[1] = _mm_raw(du, hv, True, False).astype(BF16)
        dw3_ref[2] = (_mm_raw(a, dyv, True, False) * 0.5).astype(BF16)

        da = _mm_raw(dyv, wd_ref[...], False, True) * 0.5
        a, vjp = jax.vjp(_silu_mul, g_ref[...].astype(F32), u_ref[...].astype(F32))
        dg, du = vjp(da)
        dg_s[now] = dg.astype(BF16)
        du_s[now] = du.astype(BF16)
        a_s[now] = a.astype(BF16)

        if dh_init is not None:
            @pl.when(j == count)
            def _():
                dh0_copy.wait()
                dh_ref[...] += dh0_s[...]

    this = lambda j: first + jnp.minimum(j, count - 1)
    last = lambda j: first + jnp.maximum(j - 1, 0)
    full = pl.BlockSpec((s, d), lambda j: (0, 0))
    once = pl.BlockSpec((s, d), lambda j: (0, 0), pipeline_mode=pl.Buffered(1))
    tile = pl.BlockSpec((s, tn), lambda j: (0, this(j)))
    in_specs = [once, once, pl.BlockSpec((None, tn, d), lambda j: (2, this(j), 0)),
                pl.BlockSpec((None, tn, d), lambda j: (0, last(j), 0)), pl.BlockSpec((None, tn, d), lambda j: (1, last(j), 0)),
                tile, tile]
    args = [dyb, h, w3, w3, w3, g, u]
    scratch = [pltpu.VMEM((2, s, tn), BF16)] * 3
    if dh_init is not None:
        in_specs.append(pl.BlockSpec(memory_space=pl.ANY))
        args.append(dh_init)
        scratch += [pltpu.VMEM((s, d), F32), pltpu.SemaphoreType.DMA(())]
    return pl.pallas_call(
        body, name=name, grid=(count + 1,), in_specs=in_specs,
        out_specs=[full, pl.BlockSpec((3, tn, d), lambda j: (0, jnp.maximum(j - 1, 0), 0))],
        out_shape=[SDS((s, d), F32), SDS((3, count * tn, d), BF16)],
        scratch_shapes=scratch,
        compiler_params=_params("arbitrary"),
    )(*args)


def _bucket_table():
    qi = np.arange(BLOCK)[:, None]
    kj = np.arange(2 * BLOCK)[None, :]
    dist = np.maximum(qi + BLOCK - kj, 0)
    max_exact = N_BUCKETS // 2
    d = np.maximum(dist, 1).astype(np.float32)
    large = max_exact + (np.log(d / np.float32(max_exact)) / np.float32(math.log(MAX_DISTANCE / max_exact))
                         * np.float32(N_BUCKETS - max_exact)).astype(np.int32)
    large = np.minimum(large, N_BUCKETS - 1)
    band = np.where(dist < max_exact, dist, large).astype(np.int32)
    return np.where(np.tril(np.ones((BLOCK, BLOCK), bool)), band[:, BLOCK:], band[:, :BLOCK])


SWA_STACK = SWA_GROUP * BLOCK


def _swa_masks(n):
    qi = lax.broadcasted_iota(jnp.int32, (SWA_STACK, BLOCK), 0) % BLOCK
    kj = lax.broadcasted_iota(jnp.int32, (SWA_STACK, BLOCK), 1)
    own = kj <= qi
    return own, own | (n > 0)


def _swa_group(q, kp, kc, vp, vc, qg, kg, sink, bias, own, valid):
    qn = _rms(q, qg)
    s = jnp.where(own, _mm(qn, _rms(kc, kg), False, True), _mm(qn, _rms(kp, kg), False, True))
    s = s * (HEAD_DIM ** -0.5) + bias
    s = jnp.where(valid, s, -jnp.inf)
    m = lax.stop_gradient(jnp.maximum(jnp.max(s, axis=-1, keepdims=True), sink))
    p = jnp.exp(s - m)
    p = p / (jnp.sum(p, axis=-1, keepdims=True) + jnp.exp(sink - m))
    return _mm(jnp.where(own, p, 0.0), vc) + _mm(jnp.where(own, 0.0, p), vp)


def _swa_bias_table(rb_ref, bucket, bias_s):
    for h in range(SWA_HEADS):
        acc = jnp.zeros((BLOCK, BLOCK), F32)
        for b in range(N_BUCKETS):
            acc = jnp.where(bucket == b, rb_ref[b, h], acc)
        bias_s[h // SWA_GROUP, (h % SWA_GROUP) * BLOCK:(h % SWA_GROUP + 1) * BLOCK, :] = acc


def _swa_stack(ref, g):
    return jnp.concatenate([ref[:, (g * SWA_GROUP + hh) * HEAD_DIM:(g * SWA_GROUP + hh + 1) * HEAD_DIM]
                            for hh in range(SWA_GROUP)], axis=0)


def _swa_unstack(ref, g, stacked):
    for hh in range(SWA_GROUP):
        h = g * SWA_GROUP + hh
        ref[:, h * HEAD_DIM:(h + 1) * HEAD_DIM] = stacked[hh * BLOCK:(hh + 1) * BLOCK]


def _swa_sink_column(sink_ref, g):
    head = lax.broadcasted_iota(jnp.int32, (SWA_STACK, 1), 0) // BLOCK
    col = jnp.zeros((SWA_STACK, 1), F32)
    for hh in range(SWA_GROUP):
        col = jnp.where(head == hh, sink_ref[g * SWA_GROUP + hh], col)
    return col


def _swa_band(kvp_ref, kvc_ref, g):
    k = slice(g * HEAD_DIM, (g + 1) * HEAD_DIM)
    v = slice(SWA_KV_W + g * HEAD_DIM, SWA_KV_W + (g + 1) * HEAD_DIM)
    return kvp_ref[:, k], kvc_ref[:, k], kvp_ref[:, v], kvc_ref[:, v]


def _swa_specs(order):
    kvc = COL_SKV // (2 * SWA_KV_W)
    return [
        pl.BlockSpec((BLOCK, SWA_Q_W), lambda t: (order(t), 0)),
        pl.BlockSpec((BLOCK, 2 * SWA_KV_W), lambda t: (jnp.maximum(order(t) - 1, 0), kvc)),
        pl.BlockSpec((BLOCK, 2 * SWA_KV_W), lambda t: (order(t), kvc)),
        pl.BlockSpec((1, HEAD_DIM), lambda t: (0, 0)),
        pl.BlockSpec((1, HEAD_DIM), lambda t: (0, 0)),
        pl.BlockSpec(memory_space=pltpu.SMEM),
        pl.BlockSpec(memory_space=pltpu.SMEM),
        pl.BlockSpec((BLOCK, BLOCK), lambda t: (0, 0)),
    ]


def _swa_fwd(p, qg, kg, sinks, rel_bias, *, name):
    s = p.shape[0]
    nb = s // BLOCK

    def body(q_ref, kvp_ref, kvc_ref, qg_ref, kg_ref, sink_ref, rb_ref, bucket_ref, y_ref, bias_s):
        n = pl.program_id(0)

        @pl.when(n == 0)
        def _():
            _swa_bias_table(rb_ref, bucket_ref[...], bias_s)

        own, valid = _swa_masks(n)
        for g in range(SWA_KV_HEADS):
            out = _swa_group(_swa_stack(q_ref, g), *_swa_band(kvp_ref, kvc_ref, g), qg_ref[...], kg_ref[...],
                             _swa_sink_column(sink_ref, g), bias_s[g], own, valid)
            _swa_unstack(y_ref, g, out)

    return pl.pallas_call(
        body, name=name, grid=(nb,), in_specs=_swa_specs(lambda t: t),
        out_specs=pl.BlockSpec((BLOCK, SWA_Q_W), lambda t: (t, 0)), out_shape=SDS((s, SWA_Q_W), F32),
        scratch_shapes=[pltpu.VMEM((SWA_KV_HEADS, SWA_STACK, BLOCK), F32)],
        compiler_params=_params("arbitrary"),
    )(p, p, p, qg, kg, sinks, rel_bias, jnp.asarray(_bucket_table()))


def _swa_bwd(p, qg, kg, sinks, rel_bias, dy_all, *, name):
    s = p.shape[0]
    nb = s // BLOCK

    def body(q_ref, kvp_ref, kvc_ref, qg_ref, kg_ref, sink_ref, rb_ref, bucket_ref, dy_ref,
             dq_ref, dkv_ref, dqg_ref, dkg_ref, dsink_ref, drb_ref, bias_s, dbias_s, carry_s):
        t = pl.program_id(0)
        n = nb - 1 - t

        @pl.when(t == 0)
        def _():
            _swa_bias_table(rb_ref, bucket_ref[...], bias_s)
            dbias_s[...] = jnp.zeros_like(dbias_s)
            carry_s[...] = jnp.zeros_like(carry_s)
            dqg_ref[...] = jnp.zeros_like(dqg_ref)
            dkg_ref[...] = jnp.zeros_like(dkg_ref)
            dsink_ref[...] = jnp.zeros_like(dsink_ref)
            drb_ref[...] = jnp.zeros_like(drb_ref)

        own, valid = _swa_masks(n)
        lane = lax.broadcasted_iota(jnp.int32, (1, BLOCK), 1)
        dqg = jnp.zeros((1, HEAD_DIM), F32)
        dkg = jnp.zeros((1, HEAD_DIM), F32)
        dsink_vec = jnp.zeros((1, BLOCK), F32)
        for g in range(SWA_KV_HEADS):
            _, vjp = jax.vjp(functools.partial(_swa_group, own=own, valid=valid), _swa_stack(q_ref, g),
                             *_swa_band(kvp_ref, kvc_ref, g), qg_ref[...], kg_ref[...], _swa_sink_column(sink_ref, g),
                             bias_s[g])
            dq, dkp, dkc, dvp, dvc, dqg_g, dkg_g, dsink_col, dbias = vjp(_swa_stack(dy_ref, g))
            _swa_unstack(dq_ref, g, dq)
            dqg += dqg_g
            dkg += dkg_g
            dbias_s[g] += dbias
            for hh in range(SWA_GROUP):
                dsink_h = jnp.sum(dsink_col[hh * BLOCK:(hh + 1) * BLOCK], axis=0, keepdims=True)
                dsink_vec += jnp.where(lane == g * SWA_GROUP + hh, dsink_h, 0.0)
            lo = g * HEAD_DIM
            dkv_ref[:, lo:lo + HEAD_DIM] = dkc + carry_s[g]
            carry_s[g] = dkp
            lo += SWA_KV_W
            dkv_ref[:, lo:lo + HEAD_DIM] = dvc + carry_s[SWA_KV_HEADS + g]
            carry_s[SWA_KV_HEADS + g] = dvp
        dqg_ref[...] += dqg
        dkg_ref[...] += dkg
        dsink_ref[...] += dsink_vec

        @pl.when(t == nb - 1)
        def _():
            bucket = bucket_ref[...]
            row = lax.broadcasted_iota(jnp.int32, (N_BUCKETS, BLOCK), 0)
            col = lax.broadcasted_iota(jnp.int32, (N_BUCKETS, BLOCK), 1)
            acc = jnp.zeros((N_BUCKETS, BLOCK), F32)
            for h in range(SWA_HEADS):
                dbias = dbias_s[h // SWA_GROUP, (h % SWA_GROUP) * BLOCK:(h % SWA_GROUP + 1) * BLOCK, :]
                for b in range(N_BUCKETS):
                    part = jnp.sum(jnp.where(bucket == b, dbias, 0.0), axis=1, keepdims=True)
                    val = jnp.sum(part, axis=0, keepdims=True)
                    acc = acc + jnp.where((row == b) & (col == h), val, 0.0)
            drb_ref[...] = acc

    order = lambda t: nb - 1 - t
    vec = pl.BlockSpec((1, HEAD_DIM), lambda t: (0, 0))
    return pl.pallas_call(
        body, name=name, grid=(nb,),
        in_specs=_swa_specs(order) + [pl.BlockSpec((BLOCK, SWA_Q_W), lambda t: (order(t), 0))],
        out_specs=[pl.BlockSpec((BLOCK, SWA_Q_W), lambda t: (order(t), 0)),
                   pl.BlockSpec((BLOCK, 2 * SWA_KV_W), lambda t: (order(t), 0)),
                   vec, vec, pl.BlockSpec((1, BLOCK), lambda t: (0, 0)),
                   pl.BlockSpec((N_BUCKETS, BLOCK), lambda t: (0, 0))],
        out_shape=[SDS((s, SWA_Q_W), F32), SDS((s, 2 * SWA_KV_W), F32), SDS((1, HEAD_DIM), F32),
                   SDS((1, HEAD_DIM), F32), SDS((1, BLOCK), F32), SDS((N_BUCKETS, BLOCK), F32)],
        scratch_shapes=[pltpu.VMEM((SWA_KV_HEADS, SWA_STACK, BLOCK), F32),
                        pltpu.VMEM((SWA_KV_HEADS, SWA_STACK, BLOCK), F32),
                        pltpu.VMEM((2 * SWA_KV_HEADS, BLOCK, HEAD_DIM), F32)],
        compiler_params=_params("arbitrary"),
    )(p, p, p, qg, kg, sinks, rel_bias, jnp.asarray(_bucket_table()), dy_all)


def _mem_head(q, k, v, qg, kg):
    qn = _rms(q, qg)
    kn = _rms(k, kg)
    s = _mm(qn, kn, False, True) * (HEAD_DIM ** -0.5)
    m = lax.stop_gradient(jnp.max(s, axis=-1, keepdims=True))
    e = jnp.exp(s - m)
    return _mm(e / jnp.sum(e, axis=-1, keepdims=True), v)


def _mem_fwd(p, kv, qg, kg, *, tq, name):
    s = p.shape[0]
    m = kv.shape[0]

    def body(q_ref, kv_ref, qg_ref, kg_ref, y_ref):
        for h in range(MEM_HEADS):
            cols = slice(h * HEAD_DIM, (h + 1) * HEAD_DIM)
            vcols = slice(MEM_Q_W + h * HEAD_DIM, MEM_Q_W + (h + 1) * HEAD_DIM)
            y_ref[:, cols] = _mem_head(q_ref[:, cols], kv_ref[:, cols], kv_ref[:, vcols], qg_ref[...], kg_ref[...])

    vec = pl.BlockSpec((1, HEAD_DIM), lambda t: (0, 0))
    return pl.pallas_call(
        body, name=name, grid=(s // tq,),
        in_specs=[pl.BlockSpec((tq, MEM_Q_W), lambda t: (t, COL_MQ // MEM_Q_W)),
                  pl.BlockSpec((m, 2 * MEM_Q_W), lambda t: (0, 0)), vec, vec],
        out_specs=pl.BlockSpec((tq, MEM_Q_W), lambda t: (t, 0)), out_shape=SDS((s, MEM_Q_W), F32),
        compiler_params=_params("parallel"),
    )(p, kv, qg, kg)


def _mem_bwd(p, kv, qg, kg, dy_all, *, tq, name):
    s = p.shape[0]
    m = kv.shape[0]

    def body(q_ref, kv_ref, qg_ref, kg_ref, dy_ref, dq_ref, dkv_ref, dqg_ref, dkg_ref):
        @pl.when(pl.program_id(0) == 0)
        def _():
            dkv_ref[...] = jnp.zeros_like(dkv_ref)
            dqg_ref[...] = jnp.zeros_like(dqg_ref)
            dkg_ref[...] = jnp.zeros_like(dkg_ref)

        dqg = jnp.zeros((1, HEAD_DIM), F32)
        dkg = jnp.zeros((1, HEAD_DIM), F32)
        for h in range(MEM_HEADS):
            cols = slice(h * HEAD_DIM, (h + 1) * HEAD_DIM)
            vcols = slice(MEM_Q_W + h * HEAD_DIM, MEM_Q_W + (h + 1) * HEAD_DIM)
            _, vjp = jax.vjp(_mem_head, q_ref[:, cols], kv_ref[:, cols], kv_ref[:, vcols], qg_ref[...], kg_ref[...])
            dq, dk, dv, dqg_h, dkg_h = vjp(dy_ref[:, cols])
            dq_ref[:, cols] = dq
            dkv_ref[:, cols] += dk
            dkv_ref[:, vcols] += dv
            dqg += dqg_h
            dkg += dkg_h
        dqg_ref[...] += dqg
        dkg_ref[...] += dkg

    vec = pl.BlockSpec((1, HEAD_DIM), lambda t: (0, 0))
    full = pl.BlockSpec((m, 2 * MEM_Q_W), lambda t: (0, 0))
    dy_col = (SWA_Q_W + GLA_V_W) // MEM_Q_W
    return pl.pallas_call(
        body, name=name, grid=(s // tq,),
        in_specs=[pl.BlockSpec((tq, MEM_Q_W), lambda t: (t, COL_MQ // MEM_Q_W)), full, vec, vec,
                  pl.BlockSpec((tq, MEM_Q_W), lambda t: (t, dy_col))],
        out_specs=[pl.BlockSpec((tq, MEM_Q_W), lambda t: (t, 0)), full, vec, vec],
        out_shape=[SDS((s, MEM_Q_W), F32), SDS((m, 2 * MEM_Q_W), F32), SDS((1, HEAD_DIM), F32), SDS((1, HEAD_DIM), F32)],
        compiler_params=_params("arbitrary"),
    )(p, kv, qg, kg, dy_all)


GLA_ROWS = 256


GLA_GROUP = 4


def _gla_consts():
    c, h, r = GLA_CHUNK, GLA_HEADS, GLA_GROUP * GLA_CHUNK
    i2 = lax.broadcasted_iota(jnp.int32, (c, c), 0)
    j2 = lax.broadcasted_iota(jnp.int32, (c, c), 1)
    slab_q = lax.broadcasted_iota(jnp.int32, (h, r, GLA_QK_W), 0)
    lane_q = lax.broadcasted_iota(jnp.int32, (h, r, GLA_QK_W), 2)
    row_a = lax.broadcasted_iota(jnp.int32, (h * r, r), 0) % r
    col_a = lax.broadcasted_iota(jnp.int32, (h * r, r), 1)
    slab_o = lax.broadcasted_iota(jnp.int32, (h, r, GLA_V_W), 0)
    lane_o = lax.broadcasted_iota(jnp.int32, (h, r, GLA_V_W), 2)
    row_s = lax.broadcasted_iota(jnp.int32, (GLA_V_W, GLA_QK_W), 0)
    col_s = lax.broadcasted_iota(jnp.int32, (GLA_V_W, GLA_QK_W), 1)
    return dict(
        ltri=(j2 <= i2).astype(F32),
        m_q=(slab_q == lane_q // GLA_DK).astype(F32),
        causal=(col_a <= row_a) & (col_a // c == row_a // c),
        m_o=(slab_o == lane_o // GLA_DV).astype(F32),
        m_s=(row_s // GLA_DV == col_s // GLA_DK).astype(F32),
    )


def _gla_step(q, k, v, z, bg, st, c):
    h = GLA_HEADS
    kt, ka, qt, qe, decay = [], [], [], [], []
    for qc, kc, zc in zip(q, k, z):
        la = _log_sigmoid(zc + bg) * (1.0 / GLA_TAU)
        b = _mmf(c["ltri"], la)
        bl = jnp.sum(la, axis=0, keepdims=True)
        qs = qc * (GLA_DK ** -0.5)
        kt.append(kc * jnp.exp(bl - b))
        ka.append(kc * jnp.exp(0.5 * bl - b))
        qt.append(qs * jnp.exp(b - 0.5 * bl))
        qe.append(qs * jnp.exp(b))
        decay.append(jnp.exp(bl))
    o_intra = []
    rows = GLA_GROUP * GLA_CHUNK
    for lo in range(0, len(q), GLA_GROUP):
        qt_all, kt_all, v_all = (jnp.concatenate(parts[lo:lo + GLA_GROUP], axis=0) for parts in (qt, ka, v))
        q_stack = (jnp.broadcast_to(qt_all[None], (h, rows, GLA_QK_W)) * c["m_q"]).reshape(h * rows, GLA_QK_W)
        a = jnp.where(c["causal"], _mm3(q_stack, kt_all, False, True), 0.0)
        o_stack = _mm(a, v_all)
        o_intra.append(jnp.sum(o_stack.reshape(h, rows, GLA_V_W) * c["m_o"], axis=0))
    o_intra = jnp.concatenate(o_intra, axis=0)
    o_inter = []
    for qec, ktc, vc, dc in zip(qe, kt, v, decay):
        o_inter.append(_mm(qec, st, False, True))
        st = st * dc + _mm(vc, ktc, True, False) * c["m_s"]
    return o_intra + jnp.concatenate(o_inter, axis=0), st


def _gla_post(o, gg, gain, g64):
    ms = _mmf(o * o, g64) * (1.0 / GLA_DV)
    return o * lax.rsqrt(ms + EPS) * gain * jax.nn.silu(gg)


def _gla_g64():
    r = lax.broadcasted_iota(jnp.int32, (GLA_V_W, GLA_V_W), 0)
    c = lax.broadcasted_iota(jnp.int32, (GLA_V_W, GLA_V_W), 1)
    return (r // GLA_DV == c // GLA_DV).astype(F32)


def _gla_in_specs(order):
    r = GLA_ROWS
    return [
        pl.BlockSpec((r, GLA_QK_W), lambda t: (order(t), COL_GQ // GLA_QK_W)),
        pl.BlockSpec((r, GLA_QK_W), lambda t: (order(t), COL_GK // GLA_QK_W)),
        pl.BlockSpec((r, GLA_V_W), lambda t: (order(t), COL_GV // GLA_V_W)),
        pl.BlockSpec((r, GLA_V_W), lambda t: (order(t), COL_GG // GLA_V_W)),
        pl.BlockSpec((r, GLA_QK_W), lambda t: (order(t), 0)),
        pl.BlockSpec((1, GLA_QK_W), lambda t: (0, 0)),
        pl.BlockSpec((1, GLA_V_W), lambda t: (0, 0)),
    ]


def _gla_pieces(q_ref, k_ref, v_ref, z_ref, cps):
    chunk = lambda ref: [ref[ci * GLA_CHUNK:(ci + 1) * GLA_CHUNK, :] for ci in range(cps)]
    return chunk(q_ref), chunk(k_ref), chunk(v_ref), chunk(z_ref)


def _gla_fwd(p, z, bg, gain, *, name):
    s = p.shape[0]
    r = GLA_ROWS
    cps = r // GLA_CHUNK

    def body(q_ref, k_ref, v_ref, gg_ref, z_ref, bg_ref, gain_ref, y_ref, oraw_ref, stsave_ref, st_s):
        @pl.when(pl.program_id(0) == 0)
        def _():
            st_s[...] = jnp.zeros_like(st_s)

        st = st_s[...]
        stsave_ref[0] = st
        o, st = _gla_step(*_gla_pieces(q_ref, k_ref, v_ref, z_ref, cps), bg_ref[...], st, _gla_consts())
        oraw_ref[...] = o
        st_s[...] = st
        y_ref[...] = _gla_post(o, gg_ref[...], gain_ref[...], _gla_g64())

    rowv = pl.BlockSpec((r, GLA_V_W), lambda t: (t, 0))
    return pl.pallas_call(
        body, name=name, grid=(s // r,), in_specs=_gla_in_specs(lambda t: t),
        out_specs=[rowv, rowv, pl.BlockSpec((1, GLA_V_W, GLA_QK_W), lambda t: (t, 0, 0))],
        out_shape=[SDS((s, GLA_V_W), F32), SDS((s, GLA_V_W), F32), SDS((s // r, GLA_V_W, GLA_QK_W), F32)],
        scratch_shapes=[pltpu.VMEM((GLA_V_W, GLA_QK_W), F32)],
        compiler_params=_params("arbitrary"),
    )(p, p, p, p, z, bg, gain)


def _gla_bwd(p, z, bg, gain, oraw, stsave, dy_all, *, name):
    s = p.shape[0]
    r = GLA_ROWS
    cps = r // GLA_CHUNK
    nsteps = s // r
    w_qkvg = 2 * GLA_QK_W + 2 * GLA_V_W

    def body(q_ref, k_ref, v_ref, gg_ref, z_ref, bg_ref, gain_ref, oraw_ref, stsave_ref, dy_ref,
             dqkvg_ref, dz_ref, dbg_ref, dgain_ref, dst_s):
        @pl.when(pl.program_id(0) == 0)
        def _():
            dst_s[...] = jnp.zeros_like(dst_s)
            dbg_ref[...] = jnp.zeros_like(dbg_ref)
            dgain_ref[...] = jnp.zeros_like(dgain_ref)

        _, vjp = jax.vjp(functools.partial(_gla_post, g64=_gla_g64()), oraw_ref[...], gg_ref[...], gain_ref[...])
        do, dgg, dgain = vjp(dy_ref[...])
        dqkvg_ref[:, 2 * GLA_QK_W + GLA_V_W:] = dgg
        dgain_ref[...] += dgain
        _, vjp = jax.vjp(functools.partial(_gla_step, c=_gla_consts()), *_gla_pieces(q_ref, k_ref, v_ref, z_ref, cps),
                         bg_ref[...], stsave_ref[0])
        dq, dk, dv, dz, dbg, dst = vjp((do, dst_s[...]))
        for ci in range(cps):
            rows = slice(ci * GLA_CHUNK, (ci + 1) * GLA_CHUNK)
            dqkvg_ref[rows, 0:GLA_QK_W] = dq[ci]
            dqkvg_ref[rows, GLA_QK_W:2 * GLA_QK_W] = dk[ci]
            dqkvg_ref[rows, 2 * GLA_QK_W:2 * GLA_QK_W + GLA_V_W] = dv[ci]
            dz_ref[rows, :] = dz[ci]
        dst_s[...] = dst
        dbg_ref[...] += dbg

    order = lambda t: nsteps - 1 - t
    rowv = pl.BlockSpec((r, GLA_V_W), lambda t: (order(t), 0))
    return pl.pallas_call(
        body, name=name, grid=(nsteps,),
        in_specs=_gla_in_specs(order) + [
            rowv, pl.BlockSpec((1, GLA_V_W, GLA_QK_W), lambda t: (order(t), 0, 0)),
            pl.BlockSpec((r, GLA_V_W), lambda t: (order(t), SWA_Q_W // GLA_V_W))],
        out_specs=[pl.BlockSpec((r, w_qkvg), lambda t: (order(t), 0)), pl.BlockSpec((r, GLA_QK_W), lambda t: (order(t), 0)),
                   pl.BlockSpec((1, GLA_QK_W), lambda t: (0, 0)), pl.BlockSpec((1, GLA_V_W), lambda t: (0, 0))],
        out_shape=[SDS((s, w_qkvg), F32), SDS((s, GLA_QK_W), F32), SDS((1, GLA_QK_W), F32), SDS((1, GLA_V_W), F32)],
        scratch_shapes=[pltpu.VMEM((GLA_V_W, GLA_QK_W), F32)],
        compiler_params=_params("arbitrary"),
    )(p, p, p, p, z, bg, gain, oraw, stsave, dy_all)


def _local_step(x, mem, target, small, big, on_grads):
    g1, gmix, gmem, g2, sqg, skg, sinks, rel_bias, wgu, bg, gla_gain, mqg, mkg = small
    (w3_1a, w3_1b), gather_mix, gather_ffn2 = big
    wgu_pad = jnp.zeros((GLA_QK_W, GLA_QK_W), BF16).at[:GLA_RANK].set(wgu.astype(BF16))
    gain256 = jnp.tile(gla_gain, (1, GLA_HEADS))

    (part,), saved1a = _ffn_fwd(x, g1, w3_1a, "ffn1a", partial=True)
    win_p, wkv, wout = gather_mix(part)
    (x1, h), saved1b = _ffn_fwd(x, g1, w3_1b, "ffn1b", next_gain=gmix, start=(saved1a[0], part))
    w3_2 = gather_ffn2((wout, x1))
    p = _matmul(h, win_p, tm=512, tn=IN_W_PAD, name="mix_in")
    hm = _rms_fwd(mem, gmem, tm=256, name="mem_rms")
    kv = _matmul(hm, wkv, tm=256, tn=512, name="mem_kv")
    p_glr = p[:, COL_GLR:]
    z = _matmul(p_glr, wgu_pad, tm=1024, tn=GLA_QK_W, name="gla_gate")
    y_swa = _swa_fwd(p, sqg, skg, sinks, rel_bias, name="swa_fwd")
    y_gla, oraw, stsave = _gla_fwd(p, z, bg, gain256, name="gla_fwd")
    y_mem = _mem_fwd(p, kv, mqg, mkg, tq=512, name="mem_fwd")
    x2 = _matmul([y_swa, y_gla, y_mem], wout, b_blocks=[0, 2, 3], tm=512, tn=1024, res=x1, name="mix_out")
    (dy, dyb, loss), saved2 = _ffn_fwd(x2, g2, w3_2, "ffn2", target=target)

    dh2, dw3_2 = _ffn_bwd_part(dyb, w3_2, saved2, 0, w3_2.shape[1] // FFN_TN, None, name="ffn2_bwd")
    dx2, dx2b, dg2 = _rms_bwd(x2, g2, dh2, dy, tm=512, name="ffn2_drms")
    dx2b = on_grads("ffn2", [dw3_2], dx2b)
    dy_all = _matmul(dx2b, wout, tb=True, tm=512, tn=1024, name="mix_dy")
    dwout = _dw_rows([y_swa, y_gla, y_mem], dx2b, tk=512, name="mix_dw_out")
    dq_swa, dkv_swa, dsqg, dskg, dsink, drb = _swa_bwd(p, sqg, skg, sinks, rel_bias, dy_all, name="swa_bwd")
    dqkvg, dz, dbg, dgain256 = _gla_bwd(p, z, bg, gain256, oraw, stsave, dy_all, name="gla_bwd")
    dmq, dkv_mem, dmqg, dmkg = _mem_bwd(p, kv, mqg, mkg, dy_all, tq=512, name="mem_bwd")
    dglr = _matmul(dz, wgu_pad, tb=True, tm=1024, tn=GLA_QK_W, name="gla_gate_dx")
    dwgu_pad = _matmul(p_glr, dz, ta=True, tm=GLA_QK_W, tn=GLA_QK_W, name="gla_gate_dw")
    dp = [dq_swa, dkv_swa, dqkvg, dmq, dglr]
    dwin_p = _dw_in(h, dp, tk=512, name="mix_dw_in")
    dx1, dx1b, dgmix = _rms_bwd(x1, gmix, (dp, win_p), dx2, tm=512, name="mix_dh_drms")
    dwkv = _matmul(hm, dkv_mem, ta=True, tm=512, tn=512, out_dtype=BF16, name="mem_dw_kv")
    dx1b = on_grads("mix", (dwin_p, dwkv, dwout), dx1b)
    _, _, dgmem = _rms_bwd(mem, gmem, (dkv_mem, wkv), None, tm=256, name="mem_dh_drms")
    dh1, dw3_1a = _ffn_bwd_part(dx1b, w3_1a, saved1a, 0, w3_1a.shape[1] // FFN_TN, None, name="ffn1_bwd_a")
    dgla_gain = dgain256.reshape(GLA_HEADS, GLA_DV).sum(axis=0, keepdims=True)
    dsmall = [dgmix, dgmem, dg2, dsqg, dskg, dsink[:, :SWA_HEADS], drb[:, :SWA_HEADS].T, dwgu_pad[:GLA_RANK], dbg,
              dgla_gain, dmqg, dmkg, loss]
    dh1, dgmem, dwgu_pad = on_grads("ffn1a", [dw3_1a], (dh1, dgmem, dwgu_pad), small=dsmall)
    dh1, dw3_1b = _ffn_bwd_part(dx1b, w3_1b, saved1b, 0, w3_1b.shape[1] // FFN_TN, dh1, name="ffn1_bwd_b")
    dx, _, dg1 = _rms_bwd(x, g1, dh1, dx1, tm=512, name="ffn1_drms")
    on_grads("ffn1b", [dw3_1b], None, small=[dg1])
    return dx


def _mesh_place():
    x, y, c = lax.axis_index("x"), lax.axis_index("y"), lax.axis_index("c")
    other_chips = [(1 - x, y), (x, 1 - y), (1 - x, 1 - y)]
    return x, y, c, other_chips


def _handshake(peers):
    barrier = pltpu.get_barrier_semaphore()
    for peer in peers:
        pl.semaphore_signal(barrier, inc=1, device_id=peer, device_id_type=MESH)
    pl.semaphore_wait(barrier, len(peers))


def _sequencer_call(body, operands, out_shapes, sems, *, name, collective_id):
    return pl.kernel(
        body, name=name, out_type=out_shapes, mesh=plsc.ScalarSubcoreMesh(axis_name="sequencer", num_cores=1),
        scratch_types=sems, compiler_params=pltpu.CompilerParams(collective_id=collective_id),
    )(*operands)


def _window(ref, kind, slot, shape):
    if kind == "row":
        rows = pl.ds(pl.multiple_of(slot * shape[-2], 8), shape[-2])
        return ref.at[(slice(None),) * (len(shape) - 2) + (rows,)]
    return ref.at[slot]


def _gathered(shape, kind):
    if kind == "row":
        return tuple(shape[:-2]) + (N_DEV * shape[-2], shape[-1])
    return (N_DEV,) + tuple(shape)


def _half(view, hf):
    if len(view.shape) == 4:
        return view.at[:, hf]
    n = view.shape[-2] // 2
    return view.at[(slice(None),) * (len(view.shape) - 2) + (pl.ds(hf * n, n),)]


def _all_gather(shards, kinds, *, name, collective_id):
    nt = len(shards)

    def body(*refs):
        x_refs, o_refs = refs[:nt], refs[nt:2 * nt]
        send_sems, recv_sems, local_sems = refs[2 * nt:]
        x, y, c, _ = _mesh_place()
        me, sibling, xn, yn, diag = (x, y, c), (x, y, 1 - c), (1 - x, y, c), (x, 1 - y, c), (1 - x, 1 - y, c)
        _handshake([sibling, xn, yn])

        def win(t, block):
            bx, by, bc = block
            return _window(o_refs[t], kinds[t], 4 * bx + 2 * by + bc, shards[t].shape)

        def copy(k, t, src, dst, to):
            return pltpu.make_async_remote_copy(src_ref=src, dst_ref=dst, send_sem=send_sems.at[k, t],
                                                recv_sem=recv_sems.at[k, t], device_id=to, device_id_type=MESH)

        def piece(k, t, block, hf, to, from_shard=False):
            dst = _half(win(t, block), hf)
            return copy(k, t, _half(x_refs[t], hf) if from_shard else dst, dst, to)

        mine = [pltpu.make_async_copy(x_refs[t], win(t, me), local_sems.at[t]) for t in range(nt)]
        sent = []

        def start(cp):
            cp.start()
            sent.append(cp)

        for cp in mine:
            cp.start()
        for t in range(nt):
            start(copy(0, t, x_refs[t], win(t, me), sibling))
        for hf_x, hf_y in ((0, 1), (1, 0)):
            for t in range(nt):
                start(piece(1 + hf_x, t, me, hf_x, xn, True))
                start(piece(3 + hf_y, t, me, hf_y, yn, True))
        for k, block, hf, onward, k_sib in ((1, xn, 0, (5, yn), 7), (4, yn, 1, (6, xn), 10), (2, xn, 1, None, 8),
                                           (3, yn, 0, None, 9), (5, diag, 0, None, 11), (6, diag, 1, None, 12)):
            for t in range(nt):
                piece(k, t, block, hf, me).wait_recv()
                if onward is not None:
                    start(piece(onward[0], t, block, hf, onward[1]))
                start(piece(k_sib, t, block, hf, sibling))
        for t in range(nt):
            copy(0, t, x_refs[t], win(t, sibling), me).wait_recv()
        for k_sib, block, hf in ((7, xn, 0), (10, yn, 1), (8, xn, 1), (9, yn, 0), (11, diag, 0), (12, diag, 1)):
            for t in range(nt):
                bx, by, _ = block
                piece(k_sib, t, (bx, by, 1 - c), hf, me).wait_recv()
        for cp in sent:
            cp.wait_send()
        for cp in mine:
            cp.wait()

    return _sequencer_call(
        body, shards, [SDS(_gathered(s.shape, k), s.dtype) for s, k in zip(shards, kinds)],
        [pltpu.SemaphoreType.DMA((13, nt)), pltpu.SemaphoreType.DMA((13, nt)), pltpu.SemaphoreType.DMA((nt,))],
        name=name, collective_id=collective_id)


def _part_shape(shape, kind):
    if kind == "row":
        return tuple(shape[:-2]) + (shape[-2] // N_DEV, shape[-1])
    return tuple(shape[2:])


def _pair_exchange(grads, kinds, *, name, collective_id):
    nt = len(grads)
    part = [_part_shape(g.shape, k) for g, k in zip(grads, kinds)]

    def body(*refs):
        g_refs, o_refs = refs[:nt], refs[nt:2 * nt]
        send_sems, recv_sems = refs[2 * nt:]
        x, y, c, _ = _mesh_place()
        _handshake([(x, y, 1 - c)])
        copies = []
        for t in range(nt):
            for xy in range(4):
                src = g_refs[t].at[1 - c, xy] if kinds[t] == "stack" else _window(g_refs[t], kinds[t], 2 * xy + 1 - c, part[t])
                copies.append(pltpu.make_async_remote_copy(
                    src_ref=src, dst_ref=o_refs[t].at[xy], send_sem=send_sems.at[xy, t], recv_sem=recv_sems.at[xy, t],
                    device_id=(x, y, 1 - c), device_id_type=MESH))
        for cp in copies:
            cp.start()
        for cp in copies:
            cp.wait()

    return _sequencer_call(
        body, grads, [SDS((4,) + p, g.dtype) for p, g in zip(part, grads)],
        [pltpu.SemaphoreType.DMA((4, nt)), pltpu.SemaphoreType.DMA((4, nt))], name=name, collective_id=collective_id)


def _chip_exchange(parts, small, *, name, collective_id):
    nt = len(parts)
    if small is None:
        def body_plain(*refs):
            s_refs, o_refs = refs[:nt], refs[nt:2 * nt]
            send_sems, recv_sems = refs[2 * nt:]
            x, y, c, chips = _mesh_place()
            _handshake([(*chip, c) for chip in chips])
            copies = [pltpu.make_async_remote_copy(
                src_ref=s_refs[t].at[2 * chip[0] + chip[1]], dst_ref=o_refs[t].at[j],
                send_sem=send_sems.at[j, t], recv_sem=recv_sems.at[j, t], device_id=(*chip, c), device_id_type=MESH)
                for j, chip in enumerate(chips) for t in range(nt)]
            for cp in copies:
                cp.start()
            for cp in copies:
                cp.wait()

        return _sequencer_call(
            body_plain, parts, [SDS((3,) + s.shape[1:], s.dtype) for s in parts],
            [pltpu.SemaphoreType.DMA((3, nt)), pltpu.SemaphoreType.DMA((3, nt))], name=name, collective_id=collective_id)

    def body(*refs):
        s_refs, small_ref = refs[:nt], refs[nt]
        o_refs, small_all = refs[nt + 1:2 * nt + 1], refs[2 * nt + 1]
        send_sems, recv_sems, small_send, small_recv, local_sem = refs[2 * nt + 2:]
        x, y, c, chips = _mesh_place()
        _handshake([(px, py, pc) for px in (x, 1 - x) for py in (y, 1 - y) for pc in (c, 1 - c)][1:])

        def copy(j, t, chip):
            return pltpu.make_async_remote_copy(
                src_ref=s_refs[t].at[2 * chip[0] + chip[1]], dst_ref=o_refs[t].at[j],
                send_sem=send_sems.at[j, t], recv_sem=recv_sems.at[j, t], device_id=(*chip, c), device_id_type=MESH)

        flips = [(fx, fy, fc) for fx in (0, 1) for fy in (0, 1) for fc in (0, 1)][1:]

        def small_copy(k):
            fx, fy, fc = flips[k]
            to = (x ^ fx if fx else x, y ^ fy if fy else y, c ^ fc if fc else c)
            rows = small_all.at[4 * x + 2 * y + c]
            return pltpu.make_async_remote_copy(
                src_ref=small_ref, dst_ref=rows, send_sem=small_send.at[k], recv_sem=small_recv.at[k],
                device_id=to, device_id_type=MESH)

        own = pltpu.make_async_copy(small_ref, small_all.at[4 * x + 2 * y + c], local_sem)
        own.start()
        copies = [copy(j, t, chip) for j, chip in enumerate(chips) for t in range(nt)]
        smalls = [small_copy(k) for k in range(7)]
        for cp in smalls + copies:
            cp.start()
        for cp in smalls + copies:
            cp.wait()
        own.wait()

    return _sequencer_call(
        body, list(parts) + [small],
        [SDS((3,) + s.shape[1:], s.dtype) for s in parts] + [SDS((N_DEV,) + small.shape, small.dtype)],
        [pltpu.SemaphoreType.DMA((3, nt)), pltpu.SemaphoreType.DMA((3, nt)),
         pltpu.SemaphoreType.DMA((7,)), pltpu.SemaphoreType.DMA((7,)), pltpu.SemaphoreType.DMA],
        name=name, collective_id=collective_id)


def _pair_sum(grad, theirs, kind, c, *, name):
    if kind == "row":
        r, l = theirs.shape[-2:]
        n = theirs.size // (4 * r * l)
        grad, theirs = grad.reshape(n, N_DEV * r, l), theirs.reshape(4, n, r, l)
        mine_spec = pl.BlockSpec((n, r, l), lambda xy, c_ref: (0, 2 * xy + c_ref[0], 0))
    else:
        r, l = theirs.shape[-2:]
        n = theirs.size // (4 * r * l)
        theirs = theirs.reshape(4, n, r, l)
        grad = grad.reshape(2, 4, n, r, l)
        mine_spec = pl.BlockSpec((None, None, n, r, l), lambda xy, c_ref: (c_ref[0], xy, 0, 0, 0))

    def body(c_ref, a_ref, b_ref, o_ref):
        o_ref[...] = (a_ref[...].astype(F32) + b_ref[...].astype(F32)).astype(BF16)

    part = pl.BlockSpec((None, n, r, l), lambda xy, c_ref: (xy, 0, 0, 0))
    return pl.pallas_call(
        body, name=name,
        grid_spec=pltpu.PrefetchScalarGridSpec(num_scalar_prefetch=1, grid=(4,), in_specs=[mine_spec, part], out_specs=part),
        out_shape=SDS((4, n, r, l), BF16), compiler_params=_params("parallel"),
    )(c, grad, theirs)


def _adamw(w, g, m, v):
    m = ADAM_B1 * m + (1.0 - ADAM_B1) * g
    v = ADAM_B2 * v + (1.0 - ADAM_B2) * jnp.square(g)
    m_hat = m / (1.0 - ADAM_B1 ** ADAM_STEP)
    v_hat = v / (1.0 - ADAM_B2 ** ADAM_STEP)
    delta = -ADAM_LR * (m_hat / (jnp.sqrt(v_hat) + ADAM_EPS) + ADAM_WD * w)
    return delta, m, v


def _adam_big(owns, others, mat, xy, w, m, v, *, tr, name):
    _, r, l = w.shape
    lp = owns[0].shape[-1]
    nq = len(owns)
    rows = [tr] if nq == 1 else [o.shape[-2] for o in owns]
    assert sum(rows) == tr and r % tr == 0, (r, tr, rows)

    def body(xy_ref, *refs):
        own_refs, oth_refs = refs[:nq], refs[nq:2 * nq]
        w_ref, m_ref, v_ref, g_out, d_out, m_out, v_out = refs[2 * nq:]
        parts = []
        for q in range(nq):
            gq = own_refs[q][0, 0].astype(F32)
            for j in range(3):
                gq = gq + oth_refs[q][j, 0].astype(F32)
            parts.append(gq)
        g = (parts[0] if nq == 1 else jnp.concatenate(parts, axis=0))[:, :l]
        delta, m_new, v_new = _adamw(w_ref[0], g, m_ref[0], v_ref[0])
        g_out[0] = g
        d_out[0] = delta
        m_out[0] = m_new
        v_out[0] = v_new

    blk = pl.BlockSpec((1, tr, l), lambda i, xy_ref: (0, i, 0))
    own_specs = [pl.BlockSpec((1, 1, n, lp), lambda i, xy_ref: (xy_ref[0], mat, i, 0)) for n in rows]
    oth_specs = [pl.BlockSpec((3, 1, n, lp), lambda i, xy_ref: (0, mat, i, 0)) for n in rows]
    return pl.pallas_call(
        body, name=name,
        grid_spec=pltpu.PrefetchScalarGridSpec(
            num_scalar_prefetch=1, grid=(r // tr,), in_specs=own_specs + oth_specs + [blk, blk, blk],
            out_specs=[blk, blk, blk, blk]),
        out_shape=[SDS(w.shape, F32)] * 4, compiler_params=_params("parallel"),
    )(xy, *owns, *others, w, m, v)


def _adam_cols(own, other, xy, w, m, v, *, tc, name):
    _, wc, d = w.shape
    wp = -(-wc // 128) * 128

    def body(xy_ref, own_ref, oth_ref, w_ref, m_ref, v_ref, g_out, d_out, m_out, v_out, buf):
        g = own_ref[0, 0].astype(F32)
        for j in range(3):
            g = g + oth_ref[j, 0].astype(F32)
        buf[:, :wc] = g
        buf[:, wc:] = jnp.zeros((tc, wp - wc), F32)
        g = buf[...].T[:wc]
        delta, m_new, v_new = _adamw(w_ref[0], g, m_ref[0], v_ref[0])
        g_out[0] = g
        d_out[0] = delta
        m_out[0] = m_new
        v_out[0] = v_new

    blk = pl.BlockSpec((1, wc, tc), lambda i, xy_ref: (0, 0, i))
    in_specs = [pl.BlockSpec((1, 1, tc, wc), lambda i, xy_ref: (xy_ref[0], 0, i, 0)),
                pl.BlockSpec((3, 1, tc, wc), lambda i, xy_ref: (0, 0, i, 0)), blk, blk, blk]
    return pl.pallas_call(
        body, name=name,
        grid_spec=pltpu.PrefetchScalarGridSpec(
            num_scalar_prefetch=1, grid=(d // tc,), in_specs=in_specs, out_specs=[blk, blk, blk, blk],
            scratch_shapes=[pltpu.VMEM((tc, wp), F32)]),
        out_shape=[SDS(w.shape, F32)] * 4, compiler_params=_params("parallel"),
    )(xy, own, other, w, m, v)


def _small_layout(shapes):
    out, at = [], 0
    for r, c in shapes:
        rows = c // 128 if (r == 1 and c > 128) else r
        out.append((at, rows))
        at += -(-rows // 8) * 8
    return out, at


def _pack_small(parts, *, name):
    shapes = [a.shape for a in parts]
    layout, total = _small_layout(shapes)

    def body(*refs):
        o_ref = refs[-1]
        o_ref[...] = jnp.zeros_like(o_ref)
        for x_ref, (r, c), (at, rows) in zip(refs, shapes, layout):
            if r == 1 and c > 128:
                for k in range(rows):
                    o_ref[at + k:at + k + 1, :] = x_ref[:, k * 128:(k + 1) * 128]
            else:
                o_ref[at:at + r, 0:c] = x_ref[...]

    return pl.pallas_call(body, name=name, out_shape=SDS((total, 128), F32))(*parts)


def _adam_small(g_all, ws, ms, vs, *, name):
    n = len(ws)
    shapes = [w.shape for w in ws]
    layout, _ = _small_layout(shapes)

    def body(g_ref, *refs):
        w_refs, m_refs, v_refs, outs = refs[:n], refs[n:2 * n], refs[2 * n:3 * n], refs[3 * n:]
        g_sum = g_ref[0]
        for k in range(1, N_DEV):
            g_sum = g_sum + g_ref[k]
        for i, ((r, c), (at, rows)) in enumerate(zip(shapes, layout)):
            if r == 1 and c > 128:
                g = jnp.concatenate([g_sum[at + k:at + k + 1, :] for k in range(rows)], axis=1)
            else:
                g = g_sum[at:at + r, 0:c]
            delta, m_new, v_new = _adamw(w_refs[i][...], g, m_refs[i][...], v_refs[i][...])
            for q, val in enumerate((g, delta, m_new, v_new)):
                outs[4 * i + q][...] = val

    flat = pl.pallas_call(body, name=name, out_shape=[SDS(s, F32) for s in shapes for _ in range(4)])(g_all, *ws, *ms, *vs)
    return [flat[4 * i:4 * i + 4] for i in range(n)]


def kernel(x, mem, ffn1_norm, ffn1_w_gate, ffn1_w_up, ffn1_w_down, mix_norm, mem_norm, w_in, w_mem_kv, swa_q_norm, swa_k_norm, swa_sinks, rel_bias, gla_w_gate_up, gla_b_gate, gla_out_norm, mem_q_norm, mem_k_norm, w_out, ffn2_norm, ffn2_w_gate, ffn2_w_up, ffn2_w_down, loss_target, m_ffn1_norm, m_ffn1_w_gate, m_ffn1_w_up, m_ffn1_w_down, m_mix_norm, m_mem_norm, m_w_in, m_w_mem_kv, m_swa_q_norm, m_swa_k_norm, m_swa_sinks, m_rel_bias, m_gla_w_gate_up, m_gla_b_gate, m_gla_out_norm, m_mem_q_norm, m_mem_k_norm, m_w_out, m_ffn2_norm, m_ffn2_w_gate, m_ffn2_w_up, m_ffn2_w_down, v_ffn1_norm, v_ffn1_w_gate, v_ffn1_w_up, v_ffn1_w_down, v_mix_norm, v_mem_norm, v_w_in, v_w_mem_kv, v_swa_q_norm, v_swa_k_norm, v_swa_sinks, v_rel_bias, v_gla_w_gate_up, v_gla_b_gate, v_gla_out_norm, v_mem_q_norm, v_mem_k_norm, v_w_out, v_ffn2_norm, v_ffn2_w_gate, v_ffn2_w_up, v_ffn2_w_down):
    xi, yi, ci = lax.axis_index("x"), lax.axis_index("y"), lax.axis_index("c")
    c_arr = jnp.reshape(ci, (1,)).astype(jnp.int32)
    xy_arr = jnp.reshape(2 * xi + yi, (1,)).astype(jnp.int32)
    d = x.shape[-1]

    def ffn_shards(wg_s, wu_s, wd_s):
        return jnp.concatenate([wg_s.transpose(0, 2, 1), wu_s.transpose(0, 2, 1), wd_s], axis=0).astype(BF16)

    def gather_ffn(wg_s, wu_s, wd_s, name, collective_id, after):
        w3_s, _ = lax.optimization_barrier((ffn_shards(wg_s, wu_s, wd_s), after))
        return _all_gather([w3_s], ["row"], name=name, collective_id=collective_id)[0]

    w3_s = ffn_shards(ffn1_w_gate, ffn1_w_up, ffn1_w_down)
    w3_1a = _all_gather([w3_s[:, :FFN1_FIRST]], ["row"], name="gather_ffn1a", collective_id=0)[0]
    w3_s, _ = lax.optimization_barrier((w3_s, w3_1a))
    w3_1b = _all_gather([w3_s[:, FFN1_FIRST:]], ["row"], name="gather_ffn1b", collective_id=11)[0]

    def gather_mix(after):
        mix_s = lax.optimization_barrier((w_in[0].astype(BF16), w_mem_kv[0].astype(BF16), w_out[0].astype(BF16),
                                          (w3_1b, after)))[:3]
        win_all, wkv, wout = _all_gather(list(mix_s), ["stack", "row", "row"], name="gather_mix", collective_id=1)
        return _pack_win(win_all, tr=256, name="pack_w_in"), wkv, wout

    def gather_ffn2(after):
        return gather_ffn(ffn2_w_gate, ffn2_w_up, ffn2_w_down, "gather_ffn2", 2, after)

    small_w = [ffn1_norm, mix_norm, mem_norm, ffn2_norm, swa_q_norm, swa_k_norm, swa_sinks[0], rel_bias,
               gla_w_gate_up[0], gla_b_gate, gla_out_norm, mem_q_norm, mem_k_norm]
    collective_ids = {"ffn2": (3, 4), "mix": (5, 6), "ffn1a": (7, 8), "ffn1b": (9, 10)}
    reduced, small_box = {}, {}

    def on_grads(group, grads, carry, small=None):
        if group == "mix":
            grads = list(grads)
            kinds = ["stack", "row", "row"]
        else:
            kinds = ["row"]
        if reduced:
            earlier = list(reduced.values())[-1][1]
            *grads, _ = lax.optimization_barrier((*grads, earlier[0]))
        id_pair, id_chip = collective_ids[group]
        from_sibling = _pair_exchange(grads, kinds, name=f"pair_exchange_{group}", collective_id=id_pair)
        chip_sums = [_pair_sum(g, theirs, k, c_arr, name=f"pair_sum_{group}_{t}")
                     for t, (g, theirs, k) in enumerate(zip(grads, from_sibling, kinds))]
        if carry is not None:
            *chip_sums, carry = lax.optimization_barrier((*chip_sums, carry))
        if small is None:
            from_chips = _chip_exchange(chip_sums, None, name=f"chip_exchange_{group}", collective_id=id_chip)
        else:
            packed = _pack_small(small, name=f"pack_small_{group}")
            *from_chips, small_all = _chip_exchange(chip_sums, packed, name=f"chip_exchange_{group}",
                                                    collective_id=id_chip)
            small_box[group] = small_all
        reduced[group] = (chip_sums, from_chips)
        return carry

    grad_x = _local_step(x[0], mem[0], loss_target[0], small_w, ((w3_1a, w3_1b), gather_mix, gather_ffn2), on_grads)

    big_w = {"ffn1_w_gate": ("ffn1", 0, 0, True, ffn1_w_gate, m_ffn1_w_gate, v_ffn1_w_gate),
             "ffn1_w_up": ("ffn1", 0, 1, True, ffn1_w_up, m_ffn1_w_up, v_ffn1_w_up),
             "ffn1_w_down": ("ffn1", 0, 2, False, ffn1_w_down, m_ffn1_w_down, v_ffn1_w_down),
             "w_in": ("mix", 0, 0, True, w_in, m_w_in, v_w_in),
             "w_mem_kv": ("mix", 1, 0, False, w_mem_kv, m_w_mem_kv, v_w_mem_kv),
             "w_out": ("mix", 2, 0, False, w_out, m_w_out, v_w_out),
             "ffn2_w_gate": ("ffn2", 0, 0, True, ffn2_w_gate, m_ffn2_w_gate, v_ffn2_w_gate),
             "ffn2_w_up": ("ffn2", 0, 1, True, ffn2_w_up, m_ffn2_w_up, v_ffn2_w_up),
             "ffn2_w_down": ("ffn2", 0, 2, False, ffn2_w_down, m_ffn2_w_down, v_ffn2_w_down)}
    res = {}
    for nm, (group, t, mat, transposed, w, m, v) in big_w.items():
        shape = w.shape
        if transposed:
            w, m, v = (a.transpose(0, 2, 1) for a in (w, m, v))
        if nm == "w_in":
            out = _adam_cols(reduced[group][0][t], reduced[group][1][t], xy_arr, w, m, v, tc=256, name=f"adam_{nm}")
            res[nm] = [a.transpose(0, 2, 1) for a in out]
            continue
        r = w.shape[1]
        halves = ["ffn1a", "ffn1b"] if group == "ffn1" else [group]
        if len(halves) == 1:
            tr = 256 if r % 256 == 0 else r
        else:
            tr = r
        out = _adam_big([reduced[k][0][t] for k in halves], [reduced[k][1][t] for k in halves], mat, xy_arr, w, m, v,
                        tr=tr, name=f"adam_{nm}")
        if transposed:
            out = [a.reshape(1, -1, d).transpose(0, 2, 1) for a in out]
        res[nm] = [a.reshape(shape) for a in out]
    small_names = ["ffn1_norm", "mix_norm", "mem_norm", "ffn2_norm", "swa_q_norm", "swa_k_norm", "swa_sinks", "rel_bias",
                   "gla_w_gate_up", "gla_b_gate", "gla_out_norm", "mem_q_norm", "mem_k_norm"]
    small_m = [m_ffn1_norm, m_mix_norm, m_mem_norm, m_ffn2_norm, m_swa_q_norm, m_swa_k_norm, m_swa_sinks, m_rel_bias,
               m_gla_w_gate_up, m_gla_b_gate, m_gla_out_norm, m_mem_q_norm, m_mem_k_norm]
    small_v = [v_ffn1_norm, v_mix_norm, v_mem_norm, v_ffn2_norm, v_swa_q_norm, v_swa_k_norm, v_swa_sinks, v_rel_bias,
               v_gla_w_gate_up, v_gla_b_gate, v_gla_out_norm, v_mem_q_norm, v_mem_k_norm]
    small_full = [ffn1_norm, mix_norm, mem_norm, ffn2_norm, swa_q_norm, swa_k_norm, swa_sinks, rel_bias,
                  gla_w_gate_up, gla_b_gate, gla_out_norm, mem_q_norm, mem_k_norm]
    zero = jnp.zeros((1, 1), F32)
    turned = ("rel_bias",)

    def two_d(nm, a):
        a = a.reshape(a.shape[-2:])
        return a.T if nm in turned else a

    for group, sel in (("ffn1a", slice(1, None)), ("ffn1b", slice(0, 1))):
        extra = [zero] if group == "ffn1a" else []
        ws, ms, vs = ([two_d(nm, a) for nm, a in zip(small_names[sel], arrs[sel])] + extra
                      for arrs in (small_full, small_m, small_v))
        updated = _adam_small(small_box[group], ws, ms, vs, name=f"adam_small_{group}")
        for nm, full, out in zip(small_names[sel], small_full[sel], updated):
            res[nm] = [(a.T if nm in turned else a).reshape(full.shape) for a in out]
        if extra:
            loss = updated[-1][0].reshape(())

    order = ["ffn1_norm", "ffn1_w_gate", "ffn1_w_up", "ffn1_w_down", "mix_norm", "mem_norm", "w_in", "w_mem_kv",
             "swa_q_norm", "swa_k_norm", "swa_sinks", "rel_bias", "gla_w_gate_up", "gla_b_gate", "gla_out_norm",
             "mem_q_norm", "mem_k_norm", "w_out", "ffn2_norm", "ffn2_w_gate", "ffn2_w_up", "ffn2_w_down"]
    outs = [loss, grad_x[None]]
    for q in range(4):
        outs += [res[nm][q] for nm in order]
    return tuple(outs)
```

```python
import functools
import math

import numpy as np
import jax
import jax.numpy as jnp
from jax import lax
from jax.experimental import pallas as pl
from jax.experimental.pallas import tpu as pltpu
from jax.experimental.pallas import tpu_sc as plsc

F32 = jnp.float32
BF16 = jnp.bfloat16
SDS = jax.ShapeDtypeStruct

EPS = 1e-6
HEAD_DIM = 64
SWA_HEADS = 8
SWA_KV_HEADS = 2
SWA_GROUP = SWA_HEADS // SWA_KV_HEADS
BLOCK = 128
N_BUCKETS = 32
MAX_DISTANCE = 128
GLA_HEADS = 4
GLA_DK = 32
GLA_DV = 64
GLA_RANK = 16
GLA_TAU = 16.0
GLA_CHUNK = 32
MEM_HEADS = 4
SWA_Q_W = SWA_HEADS * HEAD_DIM
SWA_KV_W = SWA_KV_HEADS * HEAD_DIM
GLA_QK_W = GLA_HEADS * GLA_DK
GLA_V_W = GLA_HEADS * GLA_DV
MEM_Q_W = MEM_HEADS * HEAD_DIM
IN_W = 1808
IN_W_PAD = 1920
COL_SQ, COL_SKV, COL_GQ, COL_GK, COL_GV, COL_GG, COL_MQ, COL_GLR = 0, 512, 768, 896, 1024, 1280, 1536, 1792

ADAM_LR = 0.001
ADAM_B1 = 0.9
ADAM_B2 = 0.999
ADAM_EPS = 1e-08
ADAM_WD = 0.01
ADAM_STEP = 10

N_DEV = 8
VMEM_LIMIT_BYTES = 56 * 1024 * 1024
MESH = pl.DeviceIdType.MESH


def _params(*sem):
    return pltpu.CompilerParams(dimension_semantics=sem or None, vmem_limit_bytes=VMEM_LIMIT_BYTES)


def _dot(a, b, ta, tb, precision=None):
    dims = (((0 if ta else 1,), (1 if tb else 0,)), ((), ()))
    return lax.dot_general(a, b, dims, preferred_element_type=F32, precision=precision)


def _mm_raw(a, b, ta=False, tb=False):
    return _dot(a.astype(BF16), b.astype(BF16), ta, tb)


def _mmf_raw(a, b, ta=False, tb=False):
    return _dot(a, b, ta, tb, lax.Precision.HIGHEST)


def _make_mm(raw):
    @functools.partial(jax.custom_vjp, nondiff_argnums=(2, 3))
    def mm(a, b, ta=False, tb=False):
        return raw(a, b, ta, tb)

    def fwd(a, b, ta, tb):
        return raw(a, b, ta, tb), (a, b)

    def bwd(ta, tb, res, g):
        a, b = res
        da = raw(b, g, tb, True) if ta else raw(g, b, False, not tb)
        db = raw(g, a, True, ta) if tb else raw(a, g, not ta, False)
        return da, db

    mm.defvjp(fwd, bwd)
    return mm


_mm = _make_mm(_mm_raw)
_mmf = _make_mm(_mmf_raw)


def _mm3(a, b, ta=False, tb=False):
    a_hi, b_hi = a.astype(BF16).astype(F32), b.astype(BF16).astype(F32)
    return _mm(a_hi, b_hi, ta, tb) + _mm(a_hi, b - b_hi, ta, tb) + _mm(a - a_hi, b_hi, ta, tb)


def _rms(x, g):
    return x * lax.rsqrt(jnp.mean(x * x, axis=-1, keepdims=True) + EPS) * g


def _silu_mul(g, u):
    return jax.nn.silu(g) * u


def _log_sigmoid(z):
    return jnp.minimum(z, 0.0) - jnp.log(1.0 + jnp.exp(-jnp.abs(z)))


def _matmul(a_list, b, *, ta=False, tb=False, tm, tn, b_blocks=None, res=None, scale=1.0, out_dtype=F32, name):
    if not isinstance(a_list, (list, tuple)):
        a_list = [a_list]
    n_a = len(a_list)
    m = a_list[0].shape[1] if ta else a_list[0].shape[0]
    ks = [a.shape[0] if ta else a.shape[1] for a in a_list]
    n = b.shape[0] if tb else b.shape[1]
    if b_blocks is None:
        assert n_a == 1
        b_blocks = [0]
    tm, tn = min(tm, m), min(tn, n)
    assert m % tm == 0 and n % tn == 0, (m, n, tm, tn)

    def body(*refs):
        a_refs, b_refs = refs[:n_a], refs[n_a:2 * n_a]
        r_ref = refs[2 * n_a] if res is not None else None
        o_ref = refs[-1]
        acc = _mm_raw(a_refs[0][...], b_refs[0][...], ta, tb)
        for k in range(1, n_a):
            acc = acc + _mm_raw(a_refs[k][...], b_refs[k][...], ta, tb)
        if scale != 1.0:
            acc = acc * scale
        if r_ref is not None:
            acc = r_ref[...] + acc
        o_ref[...] = acc.astype(out_dtype)

    in_specs = []
    for k in ks:
        in_specs.append(pl.BlockSpec((k, tm), lambda i, j: (0, i)) if ta else pl.BlockSpec((tm, k), lambda i, j: (i, 0)))
    for k, blk in zip(ks, b_blocks):
        if tb:
            in_specs.append(pl.BlockSpec((tn, k), functools.partial(lambda i, j, blk: (j, blk), blk=blk)))
        else:
            in_specs.append(pl.BlockSpec((k, tn), functools.partial(lambda i, j, blk: (blk, j), blk=blk)))
    args = list(a_list) + [b] * n_a
    if res is not None:
        in_specs.append(pl.BlockSpec((tm, tn), lambda i, j: (i, j)))
        args.append(res)
    return pl.pallas_call(
        body, name=name, grid=(m // tm, n // tn), in_specs=in_specs,
        out_specs=pl.BlockSpec((tm, tn), lambda i, j: (i, j)), out_shape=SDS((m, n), out_dtype),
        compiler_params=_params("parallel", "parallel"),
    )(*args)


def _win_pieces(w):
    glr_lo, glr_hi = COL_MQ, COL_MQ + GLA_RANK
    out = []
    for j in range(N_DEV):
        for lo, hi, shift in ((0, glr_lo, 0), (glr_lo, glr_hi, COL_GLR - glr_lo), (glr_hi, IN_W, COL_MQ - glr_hi)):
            s, e = max(j * w, lo), min((j + 1) * w, hi)
            if s < e:
                out.append((j, s - j * w, e - j * w, s + shift))
    return out


def _pack_win(win_all, *, tr, name):
    _, d, w = win_all.shape

    def body(i_ref, o_ref):
        for j, a, b, dst in _win_pieces(w):
            o_ref[:, dst:dst + b - a] = i_ref[j][:, a:b]
        o_ref[:, IN_W:] = jnp.zeros((tr, IN_W_PAD - IN_W), o_ref.dtype)

    return pl.pallas_call(
        body, name=name, grid=(d // tr,), in_specs=[pl.BlockSpec((N_DEV, tr, w), lambda i: (0, i, 0))],
        out_specs=pl.BlockSpec((tr, IN_W_PAD), lambda i: (i, 0)), out_shape=SDS((d, IN_W_PAD), win_all.dtype),
        compiler_params=_params("parallel"),
    )(win_all)


def _dw_in(h, parts, *, tk, name):
    s, d = h.shape
    n_p = len(parts)
    w = IN_W // N_DEV
    starts = [sum(p.shape[1] for p in parts[:k]) for k in range(n_p + 1)]

    def body(*refs):
        h_ref, p_refs, o_ref, acc_refs = refs[0], refs[1:1 + n_p], refs[1 + n_p], refs[2 + n_p:]
        i = pl.program_id(0)
        h_t = h_ref[...].T

        @pl.when(i == 0)
        def _():
            for acc_ref in acc_refs:
                acc_ref[...] = jnp.zeros_like(acc_ref)

        for p_ref, acc_ref in zip(p_refs, acc_refs):
            acc_ref[...] += _mm_raw(h_t, p_ref[...], False, False)

        @pl.when(i == pl.num_programs(0) - 1)
        def _():
            for j, a, b, src in _win_pieces(w):
                for k, acc_ref in enumerate(acc_refs):
                    lo, hi = max(src, starts[k]), min(src + b - a, starts[k + 1])
                    if lo < hi:
                        o_ref[j % 2, j // 2, :, a + lo - src:a + hi - src] = (
                            acc_ref[:, lo - starts[k]:hi - starts[k]].astype(BF16))

    return pl.pallas_call(
        body, name=name, grid=(s // tk,),
        in_specs=[pl.BlockSpec((tk, d), lambda i: (i, 0))] + [pl.BlockSpec((tk, p.shape[1]), lambda i: (i, 0)) for p in parts],
        out_specs=pl.BlockSpec((2, 4, d, w), lambda i: (0, 0, 0, 0)), out_shape=SDS((2, 4, d, w), BF16),
        scratch_shapes=[pltpu.VMEM((d, p.shape[1]), F32) for p in parts],
        compiler_params=_params("arbitrary"),
    )(h, *parts)


def _dw_rows(parts, b, *, tk, name):
    s, n = b.shape
    n_p = len(parts)
    starts = [sum(p.shape[1] for p in parts[:k]) for k in range(n_p + 1)]

    def body(*refs):
        p_refs, b_ref, o_ref, acc_ref = refs[:n_p], refs[n_p], refs[n_p + 1], refs[n_p + 2]
        i = pl.program_id(0)

        @pl.when(i == 0)
        def _():
            acc_ref[...] = jnp.zeros_like(acc_ref)

        for k, p_ref in enumerate(p_refs):
            acc_ref[starts[k]:starts[k + 1], :] += _mm_raw(p_ref[...], b_ref[...], True, False)

        @pl.when(i == pl.num_programs(0) - 1)
        def _():
            o_ref[...] = acc_ref[...].astype(BF16)

    return pl.pallas_call(
        body, name=name, grid=(s // tk,),
        in_specs=[pl.BlockSpec((tk, p.shape[1]), lambda i: (i, 0)) for p in parts] + [pl.BlockSpec((tk, n), lambda i: (i, 0))],
        out_specs=pl.BlockSpec((starts[-1], n), lambda i: (0, 0)), out_shape=SDS((starts[-1], n), BF16),
        scratch_shapes=[pltpu.VMEM((starts[-1], n), F32)], compiler_params=_params("arbitrary"),
    )(*parts, b)


def _rms_fwd(x, g, *, tm, name):
    s, d = x.shape

    def body(x_ref, g_ref, h_ref):
        h_ref[...] = _rms(x_ref[...], g_ref[...]).astype(BF16)

    return pl.pallas_call(
        body, name=name, grid=(s // tm,),
        in_specs=[pl.BlockSpec((tm, d), lambda i: (i, 0)), pl.BlockSpec((1, d), lambda i: (0, 0))],
        out_specs=pl.BlockSpec((tm, d), lambda i: (i, 0)), out_shape=SDS((s, d), BF16),
        compiler_params=_params("parallel"),
    )(x, g)


def _rms_bwd(x, g, dh, dres, *, tm, name):
    s, d = x.shape
    want_dx = dres is not None
    product = isinstance(dh, tuple)
    if product:
        a_parts = list(dh[0]) if isinstance(dh[0], (list, tuple)) else [dh[0]]
        starts = [sum(a.shape[1] for a in a_parts[:k]) for k in range(len(a_parts) + 1)]

    def body(*refs):
        n_dh = len(a_parts) + 1 if product else 1
        x_ref, g_ref = refs[:2]
        dh_refs, rest = refs[2:2 + n_dh], refs[2 + n_dh:]
        if want_dx:
            dres_ref, dx_ref, dxb_ref, dg_ref = rest
        else:
            dg_ref, = rest
        if product:
            b_ref = dh_refs[-1]
            dh_tile = _mm_raw(dh_refs[0][...], b_ref[:, starts[0]:starts[1]], False, True)
            for k in range(1, len(a_parts)):
                dh_tile = dh_tile + _mm_raw(dh_refs[k][...], b_ref[:, starts[k]:starts[k + 1]], False, True)
        else:
            dh_tile = dh_refs[0][...]
        _, vjp = jax.vjp(_rms, x_ref[...], g_ref[...])
        dx, dg = vjp(dh_tile)
        if want_dx:
            dx = dres_ref[...] + dx
            dx_ref[...] = dx
            dxb_ref[...] = dx.astype(BF16)

        @pl.when(pl.program_id(0) == 0)
        def _():
            dg_ref[...] = jnp.zeros_like(dg_ref)

        dg_ref[...] += dg

    row = pl.BlockSpec((tm, d), lambda i: (i, 0))
    vec = pl.BlockSpec((1, d), lambda i: (0, 0))
    if product:
        dh_specs = [pl.BlockSpec((tm, a.shape[1]), lambda i: (i, 0)) for a in a_parts]
        dh_specs.append(pl.BlockSpec((d, starts[-1]), lambda i: (0, 0)))
        dh_args = a_parts + [dh[1]]
    else:
        dh_specs, dh_args = [row], [dh]
    if want_dx:
        return pl.pallas_call(
            body, name=name, grid=(s // tm,), in_specs=[row, vec] + dh_specs + [row], out_specs=[row, row, vec],
            out_shape=[SDS((s, d), F32), SDS((s, d), BF16), SDS((1, d), F32)], compiler_params=_params("arbitrary"),
        )(x, g, *dh_args, dres)
    return None, None, pl.pallas_call(
        body, name=name, grid=(s // tm,), in_specs=[row, vec] + dh_specs, out_specs=vec,
        out_shape=SDS((1, d), F32), compiler_params=_params("arbitrary"),
    )(x, g, *dh_args)


FFN_TN = 256
FFN_TN_FWD = 512
FFN1_FIRST = 192


def _ffn_fwd(x, gain, w3, tag, *, tm=1024, next_gain=None, target=None, start=None, partial=False):
    s, d = x.shape
    f = w3.shape[1]
    tn = FFN_TN_FWD if f % FFN_TN_FWD == 0 else FFN_TN
    nj = f // tn
    tm = min(tm, s)
    n_extra = (next_gain is not None) + (target is not None) + 2 * (start is not None)

    def body(*refs):
        x_ref, gain_ref, wg_ref, wu_ref, wd_ref = refs[:5]
        extra, outs = refs[5:5 + n_extra], refs[5 + n_extra:-1]
        acc_s = refs[-1]
        g_ref, u_ref = outs[-2:]
        h_ref = extra[-2] if start is not None else outs[-3]
        i, j = pl.program_id(0), pl.program_id(1)

        @pl.when(j == 0)
        def _():
            if start is None:
                h_ref[...] = _rms(x_ref[...], gain_ref[...]).astype(BF16)
                acc_s[...] = jnp.zeros_like(acc_s)
            else:
                acc_s[...] = extra[-1][...]

        hv = h_ref[...]
        g = _mm_raw(hv, wg_ref[...], False, True)
        u = _mm_raw(hv, wu_ref[...], False, True)
        g_ref[...] = g.astype(BF16)
        u_ref[...] = u.astype(BF16)
        acc_s[...] += _mm_raw(_silu_mul(g, u), wd_ref[...])

        @pl.when(j == nj - 1)
        def _():
            y = acc_s[...] if partial else x_ref[...] + 0.5 * acc_s[...]
            if target is None:
                outs[0][...] = y
                if next_gain is not None:
                    outs[1][...] = _rms(y, extra[0][...]).astype(BF16)
            else:
                dy_ref, dyb_ref, loss_ref = outs[:3]
                diff = y - extra[0][...]
                dy_ref[...] = diff * (1.0 / d)
                dyb_ref[...] = (diff * (1.0 / d)).astype(BF16)
                part = 0.5 * jnp.sum(jnp.mean(diff * diff, axis=-1, keepdims=True), axis=0, keepdims=True)

                @pl.when(i == 0)
                def _():
                    loss_ref[...] = part

                @pl.when(i > 0)
                def _():
                    loss_ref[...] += part

    row = pl.BlockSpec((tm, d), lambda i, j: (i, 0))
    vec = pl.BlockSpec((1, d), lambda i, j: (0, 0))
    tile = pl.BlockSpec((tm, tn), lambda i, j: (i, j))
    in_specs = [row, vec] + [pl.BlockSpec((None, tn, d), functools.partial(lambda i, j, k: (k, j, 0), k=k)) for k in range(3)]
    args = [x, gain, w3, w3, w3]
    if target is None:
        out_specs, out_shape = [row], [SDS((s, d), F32)]
        if next_gain is not None:
            in_specs.append(vec)
            args.append(next_gain)
            out_specs.append(row)
            out_shape.append(SDS((s, d), BF16))
    else:
        in_specs.append(row)
        args.append(target)
        out_specs = [row, row, pl.BlockSpec((1, 1), lambda i, j: (0, 0))]
        out_shape = [SDS((s, d), F32), SDS((s, d), BF16), SDS((1, 1), F32)]
    if start is None:
        out_specs.append(row)
        out_shape.append(SDS((s, d), BF16))
    else:
        in_specs += [row, row]
        args += list(start)
    *head, g, u = pl.pallas_call(
        body, name=f"{tag}_fwd", grid=(s // tm, nj), in_specs=in_specs,
        out_specs=out_specs + [tile, tile],
        out_shape=out_shape + [SDS((s, f), BF16), SDS((s, f), BF16)],
        scratch_shapes=[pltpu.VMEM((tm, d), F32)],
        compiler_params=_params("arbitrary", "arbitrary"),
    )(*args)
    if start is None:
        *head, h = head
    else:
        h = start[0]
    return head, (h, g, u)


def _ffn_bwd_part(dyb, w3, saved, first, count, dh_init, *, name):
    h, g, u = saved
    s, d = h.shape
    tn = FFN_TN

    def body(*refs):
        if dh_init is None:
            dy_ref, h_ref, wd_ref, wg_ref, wu_ref, g_ref, u_ref, dh_ref, dw3_ref, dg_s, du_s, a_s = refs
        else:
            dy_ref, h_ref, wd_ref, wg_ref, wu_ref, g_ref, u_ref, dh0_ref, dh_ref, dw3_ref, dg_s, du_s, a_s = refs
        j = pl.program_id(0)

        @pl.when(j == 0)
        def _():
            dh_ref[...] = jnp.zeros_like(dh_ref) if dh_init is None else dh0_ref[...]
            for ref in (dg_s, du_s, a_s):
                ref[...] = jnp.zeros_like(ref)

        now, before = j % 2, 1 - j % 2
        dyv = dy_ref[...]
        hv = h_ref[...]
        dg, du, a = dg_s[before], du_s[before], a_s[before]
        dh_ref[...] += _mm_raw(dg, wg_ref[...]) + _mm_raw(du, wu_ref[...])
        dw3_ref[0] = _mm_raw(dg, hv, True, False).astype(BF16)
        dw3_ref[1] = _mm_raw(du, hv, True, False).astype(BF16)
        dw3_ref[2] = (_mm_raw(a, dyv, True, False) * 0.5).astype(BF16)

        da = _mm_raw(dyv, wd_ref[...], False, True) * 0.5
        a, vjp = jax.vjp(_silu_mul, g_ref[...].astype(F32), u_ref[...].astype(F32))
        dg, du = vjp(da)
        dg_s[now] = dg.astype(BF16)
        du_s[now] = du.astype(BF16)
        a_s[now] = a.astype(BF16)

    this = lambda j: first + jnp.minimum(j, count - 1)
    last = lambda j: first + jnp.maximum(j - 1, 0)
    full = pl.BlockSpec((s, d), lambda j: (0, 0))
    once = pl.BlockSpec((s, d), lambda j: (0, 0), pipeline_mode=pl.Buffered(1))
    tile = pl.BlockSpec((s, tn), lambda j: (0, this(j)))
    in_specs = [once, once, pl.BlockSpec((None, tn, d), lambda j: (2, this(j), 0)),
                pl.BlockSpec((None, tn, d), lambda j: (0, last(j), 0)), pl.BlockSpec((None, tn, d), lambda j: (1, last(j), 0)),
                tile, tile]
    args = [dyb, h, w3, w3, w3, g, u]
    if dh_init is not None:
        in_specs.append(once)
        args.append(dh_init)
    return pl.pallas_call(
        body, name=name, grid=(count + 1,), in_specs=in_specs,
        out_specs=[full, pl.BlockSpec((3, tn, d), lambda j: (0, jnp.maximum(j - 1, 0), 0))],
        out_shape=[SDS((s, d), F32), SDS((3, count * tn, d), BF16)],
        scratch_shapes=[pltpu.VMEM((2, s, tn), BF16)] * 3,
        compiler_params=_params("arbitrary"),
    )(*args)


def _bucket_table():
    qi = np.arange(BLOCK)[:, None]
    kj = np.arange(2 * BLOCK)[None, :]
    dist = np.maximum(qi + BLOCK - kj, 0)
    max_exact = N_BUCKETS // 2
    d = np.maximum(dist, 1).astype(np.float32)
    large = max_exact + (np.log(d / np.float32(max_exact)) / np.float32(math.log(MAX_DISTANCE / max_exact))
                         * np.float32(N_BUCKETS - max_exact)).astype(np.int32)
    large = np.minimum(large, N_BUCKETS - 1)
    band = np.where(dist < max_exact, dist, large).astype(np.int32)
    return np.where(np.tril(np.ones((BLOCK, BLOCK), bool)), band[:, BLOCK:], band[:, :BLOCK])


SWA_STACK = SWA_GROUP * BLOCK


def _swa_masks(n):
    qi = lax.broadcasted_iota(jnp.int32, (SWA_STACK, BLOCK), 0) % BLOCK
    kj = lax.broadcasted_iota(jnp.int32, (SWA_STACK, BLOCK), 1)
    own = kj <= qi
    return own, own | (n > 0)


def _swa_group(q, kp, kc, vp, vc, qg, kg, sink, bias, own, valid):
    qn = _rms(q, qg)
    s = jnp.where(own, _mm(qn, _rms(kc, kg), False, True), _mm(qn, _rms(kp, kg), False, True))
    s = s * (HEAD_DIM ** -0.5) + bias
    s = jnp.where(valid, s, -jnp.inf)
    m = lax.stop_gradient(jnp.maximum(jnp.max(s, axis=-1, keepdims=True), sink))
    p = jnp.exp(s - m)
    p = p / (jnp.sum(p, axis=-1, keepdims=True) + jnp.exp(sink - m))
    return _mm(jnp.where(own, p, 0.0), vc) + _mm(jnp.where(own, 0.0, p), vp)


def _swa_bias_table(rb_ref, bucket, bias_s):
    for h in range(SWA_HEADS):
        acc = jnp.zeros((BLOCK, BLOCK), F32)
        for b in range(N_BUCKETS):
            acc = jnp.where(bucket == b, rb_ref[b, h], acc)
        bias_s[h // SWA_GROUP, (h % SWA_GROUP) * BLOCK:(h % SWA_GROUP + 1) * BLOCK, :] = acc


def _swa_stack(ref, g):
    return jnp.concatenate([ref[:, (g * SWA_GROUP + hh) * HEAD_DIM:(g * SWA_GROUP + hh + 1) * HEAD_DIM]
                            for hh in range(SWA_GROUP)], axis=0)


def _swa_unstack(ref, g, stacked):
    for hh in range(SWA_GROUP):
        h = g * SWA_GROUP + hh
        ref[:, h * HEAD_DIM:(h + 1) * HEAD_DIM] = stacked[hh * BLOCK:(hh + 1) * BLOCK]


def _swa_sink_column(sink_ref, g):
    head = lax.broadcasted_iota(jnp.int32, (SWA_STACK, 1), 0) // BLOCK
    col = jnp.zeros((SWA_STACK, 1), F32)
    for hh in range(SWA_GROUP):
        col = jnp.where(head == hh, sink_ref[g * SWA_GROUP + hh], col)
    return col


def _swa_band(kvp_ref, kvc_ref, g):
    k = slice(g * HEAD_DIM, (g + 1) * HEAD_DIM)
    v = slice(SWA_KV_W + g * HEAD_DIM, SWA_KV_W + (g + 1) * HEAD_DIM)
    return kvp_ref[:, k], kvc_ref[:, k], kvp_ref[:, v], kvc_ref[:, v]


def _swa_specs(order):
    kvc = COL_SKV // (2 * SWA_KV_W)
    return [
        pl.BlockSpec((BLOCK, SWA_Q_W), lambda t: (order(t), 0)),
        pl.BlockSpec((BLOCK, 2 * SWA_KV_W), lambda t: (jnp.maximum(order(t) - 1, 0), kvc)),
        pl.BlockSpec((BLOCK, 2 * SWA_KV_W), lambda t: (order(t), kvc)),
        pl.BlockSpec((1, HEAD_DIM), lambda t: (0, 0)),
        pl.BlockSpec((1, HEAD_DIM), lambda t: (0, 0)),
        pl.BlockSpec(memory_space=pltpu.SMEM),
        pl.BlockSpec(memory_space=pltpu.SMEM),
        pl.BlockSpec((BLOCK, BLOCK), lambda t: (0, 0)),
    ]


def _swa_fwd(p, qg, kg, sinks, rel_bias, *, name):
    s = p.shape[0]
    nb = s // BLOCK

    def body(q_ref, kvp_ref, kvc_ref, qg_ref, kg_ref, sink_ref, rb_ref, bucket_ref, y_ref, bias_s):
        n = pl.program_id(0)

        @pl.when(n == 0)
        def _():
            _swa_bias_table(rb_ref, bucket_ref[...], bias_s)

        own, valid = _swa_masks(n)
        for g in range(SWA_KV_HEADS):
            out = _swa_group(_swa_stack(q_ref, g), *_swa_band(kvp_ref, kvc_ref, g), qg_ref[...], kg_ref[...],
                             _swa_sink_column(sink_ref, g), bias_s[g], own, valid)
            _swa_unstack(y_ref, g, out)

    return pl.pallas_call(
        body, name=name, grid=(nb,), in_specs=_swa_specs(lambda t: t),
        out_specs=pl.BlockSpec((BLOCK, SWA_Q_W), lambda t: (t, 0)), out_shape=SDS((s, SWA_Q_W), F32),
        scratch_shapes=[pltpu.VMEM((SWA_KV_HEADS, SWA_STACK, BLOCK), F32)],
        compiler_params=_params("arbitrary"),
    )(p, p, p, qg, kg, sinks, rel_bias, jnp.asarray(_bucket_table()))


def _swa_bwd(p, qg, kg, sinks, rel_bias, dy_all, *, name):
    s = p.shape[0]
    nb = s // BLOCK

    def body(q_ref, kvp_ref, kvc_ref, qg_ref, kg_ref, sink_ref, rb_ref, bucket_ref, dy_ref,
             dq_ref, dkv_ref, dqg_ref, dkg_ref, dsink_ref, drb_ref, bias_s, dbias_s, carry_s):
        t = pl.program_id(0)
        n = nb - 1 - t

        @pl.when(t == 0)
        def _():
            _swa_bias_table(rb_ref, bucket_ref[...], bias_s)
            dbias_s[...] = jnp.zeros_like(dbias_s)
            carry_s[...] = jnp.zeros_like(carry_s)
            dqg_ref[...] = jnp.zeros_like(dqg_ref)
            dkg_ref[...] = jnp.zeros_like(dkg_ref)
            dsink_ref[...] = jnp.zeros_like(dsink_ref)
            drb_ref[...] = jnp.zeros_like(drb_ref)

        own, valid = _swa_masks(n)
        lane = lax.broadcasted_iota(jnp.int32, (1, BLOCK), 1)
        dqg = jnp.zeros((1, HEAD_DIM), F32)
        dkg = jnp.zeros((1, HEAD_DIM), F32)
        dsink_vec = jnp.zeros((1, BLOCK), F32)
        for g in range(SWA_KV_HEADS):
            _, vjp = jax.vjp(functools.partial(_swa_group, own=own, valid=valid), _swa_stack(q_ref, g),
                             *_swa_band(kvp_ref, kvc_ref, g), qg_ref[...], kg_ref[...], _swa_sink_column(sink_ref, g),
                             bias_s[g])
            dq, dkp, dkc, dvp, dvc, dqg_g, dkg_g, dsink_col, dbias = vjp(_swa_stack(dy_ref, g))
            _swa_unstack(dq_ref, g, dq)
            dqg += dqg_g
            dkg += dkg_g
            dbias_s[g] += dbias
            for hh in range(SWA_GROUP):
                dsink_h = jnp.sum(dsink_col[hh * BLOCK:(hh + 1) * BLOCK], axis=0, keepdims=True)
                dsink_vec += jnp.where(lane == g * SWA_GROUP + hh, dsink_h, 0.0)
            lo = g * HEAD_DIM
            dkv_ref[:, lo:lo + HEAD_DIM] = dkc + carry_s[g]
            carry_s[g] = dkp
            lo += SWA_KV_W
            dkv_ref[:, lo:lo + HEAD_DIM] = dvc + carry_s[SWA_KV_HEADS + g]
            carry_s[SWA_KV_HEADS + g] = dvp
        dqg_ref[...] += dqg
        dkg_ref[...] += dkg
        dsink_ref[...] += dsink_vec

        @pl.when(t == nb - 1)
        def _():
            bucket = bucket_ref[...]
            row = lax.broadcasted_iota(jnp.int32, (N_BUCKETS, BLOCK), 0)
            col = lax.broadcasted_iota(jnp.int32, (N_BUCKETS, BLOCK), 1)
            acc = jnp.zeros((N_BUCKETS, BLOCK), F32)
            for h in range(SWA_HEADS):
                dbias = dbias_s[h // SWA_GROUP, (h % SWA_GROUP) * BLOCK:(h % SWA_GROUP + 1) * BLOCK, :]
                for b in range(N_BUCKETS):
                    part = jnp.sum(jnp.where(bucket == b, dbias, 0.0), axis=1, keepdims=True)
                    val = jnp.sum(part, axis=0, keepdims=True)
                    acc = acc + jnp.where((row == b) & (col == h), val, 0.0)
            drb_ref[...] = acc

    order = lambda t: nb - 1 - t
    vec = pl.BlockSpec((1, HEAD_DIM), lambda t: (0, 0))
    return pl.pallas_call(
        body, name=name, grid=(nb,),
        in_specs=_swa_specs(order) + [pl.BlockSpec((BLOCK, SWA_Q_W), lambda t: (order(t), 0))],
        out_specs=[pl.BlockSpec((BLOCK, SWA_Q_W), lambda t: (order(t), 0)),
                   pl.BlockSpec((BLOCK, 2 * SWA_KV_W), lambda t: (order(t), 0)),
                   vec, vec, pl.BlockSpec((1, BLOCK), lambda t: (0, 0)),
                   pl.BlockSpec((N_BUCKETS, BLOCK), lambda t: (0, 0))],
        out_shape=[SDS((s, SWA_Q_W), F32), SDS((s, 2 * SWA_KV_W), F32), SDS((1, HEAD_DIM), F32),
                   SDS((1, HEAD_DIM), F32), SDS((1, BLOCK), F32), SDS((N_BUCKETS, BLOCK), F32)],
        scratch_shapes=[pltpu.VMEM((SWA_KV_HEADS, SWA_STACK, BLOCK), F32),
                        pltpu.VMEM((SWA_KV_HEADS, SWA_STACK, BLOCK), F32),
                        pltpu.VMEM((2 * SWA_KV_HEADS, BLOCK, HEAD_DIM), F32)],
        compiler_params=_params("arbitrary"),
    )(p, p, p, qg, kg, sinks, rel_bias, jnp.asarray(_bucket_table()), dy_all)


def _mem_head(q, k, v, qg, kg):
    qn = _rms(q, qg)
    kn = _rms(k, kg)
    s = _mm(qn, kn, False, True) * (HEAD_DIM ** -0.5)
    m = lax.stop_gradient(jnp.max(s, axis=-1, keepdims=True))
    e = jnp.exp(s - m)
    return _mm(e / jnp.sum(e, axis=-1, keepdims=True), v)


def _mem_fwd(p, kv, qg, kg, *, tq, name):
    s = p.shape[0]
    m = kv.shape[0]

    def body(q_ref, kv_ref, qg_ref, kg_ref, y_ref):
        for h in range(MEM_HEADS):
            cols = slice(h * HEAD_DIM, (h + 1) * HEAD_DIM)
            vcols = slice(MEM_Q_W + h * HEAD_DIM, MEM_Q_W + (h + 1) * HEAD_DIM)
            y_ref[:, cols] = _mem_head(q_ref[:, cols], kv_ref[:, cols], kv_ref[:, vcols], qg_ref[...], kg_ref[...])

    vec = pl.BlockSpec((1, HEAD_DIM), lambda t: (0, 0))
    return pl.pallas_call(
        body, name=name, grid=(s // tq,),
        in_specs=[pl.BlockSpec((tq, MEM_Q_W), lambda t: (t, COL_MQ // MEM_Q_W)),
                  pl.BlockSpec((m, 2 * MEM_Q_W), lambda t: (0, 0)), vec, vec],
        out_specs=pl.BlockSpec((tq, MEM_Q_W), lambda t: (t, 0)), out_shape=SDS((s, MEM_Q_W), F32),
        compiler_params=_params("parallel"),
    )(p, kv, qg, kg)


def _mem_bwd(p, kv, qg, kg, dy_all, *, tq, name):
    s = p.shape[0]
    m = kv.shape[0]

    def body(q_ref, kv_ref, qg_ref, kg_ref, dy_ref, dq_ref, dkv_ref, dqg_ref, dkg_ref):
        @pl.when(pl.program_id(0) == 0)
        def _():
            dkv_ref[...] = jnp.zeros_like(dkv_ref)
            dqg_ref[...] = jnp.zeros_like(dqg_ref)
            dkg_ref[...] = jnp.zeros_like(dkg_ref)

        dqg = jnp.zeros((1, HEAD_DIM), F32)
        dkg = jnp.zeros((1, HEAD_DIM), F32)
        for h in range(MEM_HEADS):
            cols = slice(h * HEAD_DIM, (h + 1) * HEAD_DIM)
            vcols = slice(MEM_Q_W + h * HEAD_DIM, MEM_Q_W + (h + 1) * HEAD_DIM)
            _, vjp = jax.vjp(_mem_head, q_ref[:, cols], kv_ref[:, cols], kv_ref[:, vcols], qg_ref[...], kg_ref[...])
            dq, dk, dv, dqg_h, dkg_h = vjp(dy_ref[:, cols])
            dq_ref[:, cols] = dq.astype(BF16)
            dkv_ref[:, cols] += dk
            dkv_ref[:, vcols] += dv
            dqg += dqg_h
            dkg += dkg_h
        dqg_ref[...] += dqg
        dkg_ref[...] += dkg

    vec = pl.BlockSpec((1, HEAD_DIM), lambda t: (0, 0))
    full = pl.BlockSpec((m, 2 * MEM_Q_W), lambda t: (0, 0))
    dy_col = (SWA_Q_W + GLA_V_W) // MEM_Q_W
    return pl.pallas_call(
        body, name=name, grid=(s // tq,),
        in_specs=[pl.BlockSpec((tq, MEM_Q_W), lambda t: (t, COL_MQ // MEM_Q_W)), full, vec, vec,
                  pl.BlockSpec((tq, MEM_Q_W), lambda t: (t, dy_col))],
        out_specs=[pl.BlockSpec((tq, MEM_Q_W), lambda t: (t, 0)), full, vec, vec],
        out_shape=[SDS((s, MEM_Q_W), BF16), SDS((m, 2 * MEM_Q_W), F32), SDS((1, HEAD_DIM), F32), SDS((1, HEAD_DIM), F32)],
        compiler_params=_params("arbitrary"),
    )(p, kv, qg, kg, dy_all)


GLA_ROWS = 256


GLA_GROUP = 4


def _gla_consts():
    c, h, r = GLA_CHUNK, GLA_HEADS, GLA_GROUP * GLA_CHUNK
    i2 = lax.broadcasted_iota(jnp.int32, (c, c), 0)
    j2 = lax.broadcasted_iota(jnp.int32, (c, c), 1)
    slab_q = lax.broadcasted_iota(jnp.int32, (h, r, GLA_QK_W), 0)
    lane_q = lax.broadcasted_iota(jnp.int32, (h, r, GLA_QK_W), 2)
    row_a = lax.broadcasted_iota(jnp.int32, (h * r, r), 0) % r
    col_a = lax.broadcasted_iota(jnp.int32, (h * r, r), 1)
    slab_o = lax.broadcasted_iota(jnp.int32, (h, r, GLA_V_W), 0)
    lane_o = lax.broadcasted_iota(jnp.int32, (h, r, GLA_V_W), 2)
    row_s = lax.broadcasted_iota(jnp.int32, (GLA_V_W, GLA_QK_W), 0)
    col_s = lax.broadcasted_iota(jnp.int32, (GLA_V_W, GLA_QK_W), 1)
    return dict(
        ltri=(j2 <= i2).astype(F32),
        m_q=(slab_q == lane_q // GLA_DK).astype(F32),
        causal=(col_a <= row_a) & (col_a // c == row_a // c),
        m_o=(slab_o == lane_o // GLA_DV).astype(F32),
        m_s=(row_s // GLA_DV == col_s // GLA_DK).astype(F32),
    )


def _gla_step(q, k, v, z, bg, st, c):
    h = GLA_HEADS
    kt, ka, qt, qe, decay = [], [], [], [], []
    for qc, kc, zc in zip(q, k, z):
        la = _log_sigmoid(zc + bg) * (1.0 / GLA_TAU)
        b = _mmf(c["ltri"], la)
        bl = jnp.sum(la, axis=0, keepdims=True)
        qs = qc * (GLA_DK ** -0.5)
        kt.append(kc * jnp.exp(bl - b))
        ka.append(kc * jnp.exp(0.5 * bl - b))
        qt.append(qs * jnp.exp(b - 0.5 * bl))
        qe.append(qs * jnp.exp(b))
        decay.append(jnp.exp(bl))
    o_intra = []
    rows = GLA_GROUP * GLA_CHUNK
    for lo in range(0, len(q), GLA_GROUP):
        qt_all, kt_all, v_all = (jnp.concatenate(parts[lo:lo + GLA_GROUP], axis=0) for parts in (qt, ka, v))
        q_stack = (jnp.broadcast_to(qt_all[None], (h, rows, GLA_QK_W)) * c["m_q"]).reshape(h * rows, GLA_QK_W)
        a = jnp.where(c["causal"], _mm3(q_stack, kt_all, False, True), 0.0)
        o_stack = _mm(a, v_all)
        o_intra.append(jnp.sum(o_stack.reshape(h, rows, GLA_V_W) * c["m_o"], axis=0))
    o_intra = jnp.concatenate(o_intra, axis=0)
    o_inter = []
    for qec, ktc, vc, dc in zip(qe, kt, v, decay):
        o_inter.append(_mm(qec, st, False, True))
        st = st * dc + _mm(vc, ktc, True, False) * c["m_s"]
    return o_intra + jnp.concatenate(o_inter, axis=0), st


def _gla_post(o, gg, gain, g64):
    ms = _mmf(o * o, g64) * (1.0 / GLA_DV)
    return o * lax.rsqrt(ms + EPS) * gain * jax.nn.silu(gg)


def _gla_g64():
    r = lax.broadcasted_iota(jnp.int32, (GLA_V_W, GLA_V_W), 0)
    c = lax.broadcasted_iota(jnp.int32, (GLA_V_W, GLA_V_W), 1)
    return (r // GLA_DV == c // GLA_DV).astype(F32)


def _gla_in_specs(order):
    r = GLA_ROWS
    return [
        pl.BlockSpec((r, GLA_QK_W), lambda t: (order(t), COL_GQ // GLA_QK_W)),
        pl.BlockSpec((r, GLA_QK_W), lambda t: (order(t), COL_GK // GLA_QK_W)),
        pl.BlockSpec((r, GLA_V_W), lambda t: (order(t), COL_GV // GLA_V_W)),
        pl.BlockSpec((r, GLA_V_W), lambda t: (order(t), COL_GG // GLA_V_W)),
        pl.BlockSpec((r, GLA_QK_W), lambda t: (order(t), 0)),
        pl.BlockSpec((1, GLA_QK_W), lambda t: (0, 0)),
        pl.BlockSpec((1, GLA_V_W), lambda t: (0, 0)),
    ]


def _gla_pieces(q_ref, k_ref, v_ref, z_ref, cps):
    chunk = lambda ref: [ref[ci * GLA_CHUNK:(ci + 1) * GLA_CHUNK, :] for ci in range(cps)]
    return chunk(q_ref), chunk(k_ref), chunk(v_ref), chunk(z_ref)


def _gla_fwd(p, z, bg, gain, *, name):
    s = p.shape[0]
    r = GLA_ROWS
    cps = r // GLA_CHUNK

    def body(q_ref, k_ref, v_ref, gg_ref, z_ref, bg_ref, gain_ref, y_ref, oraw_ref, stsave_ref, st_s):
        @pl.when(pl.program_id(0) == 0)
        def _():
            st_s[...] = jnp.zeros_like(st_s)

        st = st_s[...]
        stsave_ref[0] = st
        o, st = _gla_step(*_gla_pieces(q_ref, k_ref, v_ref, z_ref, cps), bg_ref[...], st, _gla_consts())
        oraw_ref[...] = o
        st_s[...] = st
        y_ref[...] = _gla_post(o, gg_ref[...], gain_ref[...], _gla_g64())

    rowv = pl.BlockSpec((r, GLA_V_W), lambda t: (t, 0))
    return pl.pallas_call(
        body, name=name, grid=(s // r,), in_specs=_gla_in_specs(lambda t: t),
        out_specs=[rowv, rowv, pl.BlockSpec((1, GLA_V_W, GLA_QK_W), lambda t: (t, 0, 0))],
        out_shape=[SDS((s, GLA_V_W), F32), SDS((s, GLA_V_W), F32), SDS((s // r, GLA_V_W, GLA_QK_W), F32)],
        scratch_shapes=[pltpu.VMEM((GLA_V_W, GLA_QK_W), F32)],
        compiler_params=_params("arbitrary"),
    )(p, p, p, p, z, bg, gain)


def _gla_bwd(p, z, bg, gain, oraw, stsave, dy_all, *, name):
    s = p.shape[0]
    r = GLA_ROWS
    cps = r // GLA_CHUNK
    nsteps = s // r
    w_qkvg = 2 * GLA_QK_W + 2 * GLA_V_W

    def body(q_ref, k_ref, v_ref, gg_ref, z_ref, bg_ref, gain_ref, oraw_ref, stsave_ref, dy_ref,
             dqkvg_ref, dz_ref, dbg_ref, dgain_ref, dst_s):
        @pl.when(pl.program_id(0) == 0)
        def _():
            dst_s[...] = jnp.zeros_like(dst_s)
            dbg_ref[...] = jnp.zeros_like(dbg_ref)
            dgain_ref[...] = jnp.zeros_like(dgain_ref)

        _, vjp = jax.vjp(functools.partial(_gla_post, g64=_gla_g64()), oraw_ref[...], gg_ref[...], gain_ref[...])
        do, dgg, dgain = vjp(dy_ref[...])
        dqkvg_ref[:, 2 * GLA_QK_W + GLA_V_W:] = dgg
        dgain_ref[...] += dgain
        _, vjp = jax.vjp(functools.partial(_gla_step, c=_gla_consts()), *_gla_pieces(q_ref, k_ref, v_ref, z_ref, cps),
                         bg_ref[...], stsave_ref[0])
        dq, dk, dv, dz, dbg, dst = vjp((do, dst_s[...]))
        for ci in range(cps):
            rows = slice(ci * GLA_CHUNK, (ci + 1) * GLA_CHUNK)
            dqkvg_ref[rows, 0:GLA_QK_W] = dq[ci]
            dqkvg_ref[rows, GLA_QK_W:2 * GLA_QK_W] = dk[ci]
            dqkvg_ref[rows, 2 * GLA_QK_W:2 * GLA_QK_W + GLA_V_W] = dv[ci]
            dz_ref[rows, :] = dz[ci]
        dst_s[...] = dst
        dbg_ref[...] += dbg

    order = lambda t: nsteps - 1 - t
    rowv = pl.BlockSpec((r, GLA_V_W), lambda t: (order(t), 0))
    return pl.pallas_call(
        body, name=name, grid=(nsteps,),
        in_specs=_gla_in_specs(order) + [
            rowv, pl.BlockSpec((1, GLA_V_W, GLA_QK_W), lambda t: (order(t), 0, 0)),
            pl.BlockSpec((r, GLA_V_W), lambda t: (order(t), SWA_Q_W // GLA_V_W))],
        out_specs=[pl.BlockSpec((r, w_qkvg), lambda t: (order(t), 0)), pl.BlockSpec((r, GLA_QK_W), lambda t: (order(t), 0)),
                   pl.BlockSpec((1, GLA_QK_W), lambda t: (0, 0)), pl.BlockSpec((1, GLA_V_W), lambda t: (0, 0))],
        out_shape=[SDS((s, w_qkvg), F32), SDS((s, GLA_QK_W), F32), SDS((1, GLA_QK_W), F32), SDS((1, GLA_V_W), F32)],
        scratch_shapes=[pltpu.VMEM((GLA_V_W, GLA_QK_W), F32)],
        compiler_params=_params("arbitrary"),
    )(p, p, p, p, z, bg, gain, oraw, stsave, dy_all)


def _local_step(x, mem, target, small, big, on_grads):
    g1, gmix, gmem, g2, sqg, skg, sinks, rel_bias, wgu, bg, gla_gain, mqg, mkg = small
    (w3_1a, w3_1b), gather_mix, gather_ffn2 = big
    wgu_pad = jnp.zeros((GLA_QK_W, GLA_QK_W), BF16).at[:GLA_RANK].set(wgu.astype(BF16))
    gain256 = jnp.tile(gla_gain, (1, GLA_HEADS))

    (part,), saved1a = _ffn_fwd(x, g1, w3_1a, "ffn1a", partial=True)
    win_p, wkv, wout = gather_mix(part)
    (x1, h), saved1b = _ffn_fwd(x, g1, w3_1b, "ffn1b", next_gain=gmix, start=(saved1a[0], part))
    w3_2 = gather_ffn2((wout, x1))
    p = _matmul(h, win_p, tm=512, tn=IN_W_PAD, name="mix_in")
    hm = _rms_fwd(mem, gmem, tm=256, name="mem_rms")
    kv = _matmul(hm, wkv, tm=256, tn=512, name="mem_kv")
    p_glr = p[:, COL_GLR:]
    z = _matmul(p_glr, wgu_pad, tm=1024, tn=GLA_QK_W, name="gla_gate")
    y_swa = _swa_fwd(p, sqg, skg, sinks, rel_bias, name="swa_fwd")
    y_gla, oraw, stsave = _gla_fwd(p, z, bg, gain256, name="gla_fwd")
    y_mem = _mem_fwd(p, kv, mqg, mkg, tq=512, name="mem_fwd")
    x2 = _matmul([y_swa, y_gla, y_mem], wout, b_blocks=[0, 2, 3], tm=512, tn=1024, res=x1, name="mix_out")
    (dy, dyb, loss), saved2 = _ffn_fwd(x2, g2, w3_2, "ffn2", target=target)

    dh2, dw3_2 = _ffn_bwd_part(dyb, w3_2, saved2, 0, w3_2.shape[1] // FFN_TN, None, name="ffn2_bwd")
    dx2, dx2b, dg2 = _rms_bwd(x2, g2, dh2, dy, tm=512, name="ffn2_drms")
    dx2b = on_grads("ffn2", [dw3_2], dx2b)
    dy_all = _matmul(dx2b, wout, tb=True, tm=512, tn=1024, name="mix_dy")
    dwout = _dw_rows([y_swa, y_gla, y_mem], dx2b, tk=512, name="mix_dw_out")
    dq_swa, dkv_swa, dsqg, dskg, dsink, drb = _swa_bwd(p, sqg, skg, sinks, rel_bias, dy_all, name="swa_bwd")
    dqkvg, dz, dbg, dgain256 = _gla_bwd(p, z, bg, gain256, oraw, stsave, dy_all, name="gla_bwd")
    dmq, dkv_mem, dmqg, dmkg = _mem_bwd(p, kv, mqg, mkg, dy_all, tq=512, name="mem_bwd")
    dglr = _matmul(dz, wgu_pad, tb=True, tm=1024, tn=GLA_QK_W, out_dtype=BF16, name="gla_gate_dx")
    dwgu_pad = _matmul(p_glr, dz, ta=True, tm=GLA_QK_W, tn=GLA_QK_W, name="gla_gate_dw")
    dp = [dq_swa, dkv_swa, dqkvg, dmq, dglr]
    dwin_p = _dw_in(h, dp, tk=512, name="mix_dw_in")
    dx1, dx1b, dgmix = _rms_bwd(x1, gmix, (dp, win_p), dx2, tm=512, name="mix_dh_drms")
    dwkv = _matmul(hm, dkv_mem, ta=True, tm=512, tn=512, out_dtype=BF16, name="mem_dw_kv")
    dx1b = on_grads("mix", (dwin_p, dwkv, dwout), dx1b)
    _, _, dgmem = _rms_bwd(mem, gmem, (dkv_mem, wkv), None, tm=256, name="mem_dh_drms")
    dh1, dw3_1a = _ffn_bwd_part(dx1b, w3_1a, saved1a, 0, w3_1a.shape[1] // FFN_TN, None, name="ffn1_bwd_a")
    dgla_gain = dgain256.reshape(GLA_HEADS, GLA_DV).sum(axis=0, keepdims=True)
    dsmall = [dgmix, dgmem, dg2, dsqg, dskg, dsink[:, :SWA_HEADS], drb[:, :SWA_HEADS].T, dwgu_pad[:GLA_RANK], dbg,
              dgla_gain, dmqg, dmkg, loss]
    dh1, dgmem, dwgu_pad = on_grads("ffn1a", [dw3_1a], (dh1, dgmem, dwgu_pad), small=dsmall)
    dh1, dw3_1b = _ffn_bwd_part(dx1b, w3_1b, saved1b, 0, w3_1b.shape[1] // FFN_TN, dh1, name="ffn1_bwd_b")
    dx, _, dg1 = _rms_bwd(x, g1, dh1, dx1, tm=512, name="ffn1_drms")
    on_grads("ffn1b", [dw3_1b], None, small=[dg1])
    return dx


def _mesh_place():
    x, y, c = lax.axis_index("x"), lax.axis_index("y"), lax.axis_index("c")
    other_chips = [(1 - x, y), (x, 1 - y), (1 - x, 1 - y)]
    return x, y, c, other_chips


def _handshake(peers):
    barrier = pltpu.get_barrier_semaphore()
    for peer in peers:
        pl.semaphore_signal(barrier, inc=1, device_id=peer, device_id_type=MESH)
    pl.semaphore_wait(barrier, len(peers))


def _sequencer_call(body, operands, out_shapes, sems, *, name, collective_id):
    return pl.kernel(
        body, name=name, out_type=out_shapes, mesh=plsc.ScalarSubcoreMesh(axis_name="sequencer", num_cores=1),
        scratch_types=sems, compiler_params=pltpu.CompilerParams(collective_id=collective_id),
    )(*operands)


def _window(ref, kind, slot, shape):
    if kind == "row":
        rows = pl.ds(pl.multiple_of(slot * shape[-2], 8), shape[-2])
        return ref.at[(slice(None),) * (len(shape) - 2) + (rows,)]
    return ref.at[slot]


def _gathered(shape, kind):
    if kind == "row":
        return tuple(shape[:-2]) + (N_DEV * shape[-2], shape[-1])
    return (N_DEV,) + tuple(shape)


def _half(view, hf):
    if len(view.shape) == 4:
        return view.at[:, hf]
    n = view.shape[-2] // 2
    return view.at[(slice(None),) * (len(view.shape) - 2) + (pl.ds(hf * n, n),)]


def _all_gather(shards, kinds, *, name, collective_id):
    nt = len(shards)

    def body(*refs):
        x_refs, o_refs = refs[:nt], refs[nt:2 * nt]
        send_sems, recv_sems, local_sems = refs[2 * nt:]
        x, y, c, _ = _mesh_place()
        me, sibling, xn, yn, diag = (x, y, c), (x, y, 1 - c), (1 - x, y, c), (x, 1 - y, c), (1 - x, 1 - y, c)
        _handshake([sibling, xn, yn])

        def win(t, block):
            bx, by, bc = block
            return _window(o_refs[t], kinds[t], 4 * bx + 2 * by + bc, shards[t].shape)

        def copy(k, t, src, dst, to):
            return pltpu.make_async_remote_copy(src_ref=src, dst_ref=dst, send_sem=send_sems.at[k, t],
                                                recv_sem=recv_sems.at[k, t], device_id=to, device_id_type=MESH)

        def piece(k, t, block, hf, to, from_shard=False):
            dst = _half(win(t, block), hf)
            return copy(k, t, _half(x_refs[t], hf) if from_shard else dst, dst, to)

        mine = [pltpu.make_async_copy(x_refs[t], win(t, me), local_sems.at[t]) for t in range(nt)]
        sent = []

        def start(cp):
            cp.start()
            sent.append(cp)

        for cp in mine:
            cp.start()
        for t in range(nt):
            start(copy(0, t, x_refs[t], win(t, me), sibling))
        for hf_x, hf_y in ((0, 1), (1, 0)):
            for t in range(nt):
                start(piece(1 + hf_x, t, me, hf_x, xn, True))
                start(piece(3 + hf_y, t, me, hf_y, yn, True))
        for k, block, hf, onward, k_sib in ((1, xn, 0, (5, yn), 7), (4, yn, 1, (6, xn), 10), (2, xn, 1, None, 8),
                                           (3, yn, 0, None, 9), (5, diag, 0, None, 11), (6, diag, 1, None, 12)):
            for t in range(nt):
                piece(k, t, block, hf, me).wait_recv()
                if onward is not None:
                    start(piece(onward[0], t, block, hf, onward[1]))
                start(piece(k_sib, t, block, hf, sibling))
        for t in range(nt):
            copy(0, t, x_refs[t], win(t, sibling), me).wait_recv()
        for k_sib, block, hf in ((7, xn, 0), (10, yn, 1), (8, xn, 1), (9, yn, 0), (11, diag, 0), (12, diag, 1)):
            for t in range(nt):
                bx, by, _ = block
                piece(k_sib, t, (bx, by, 1 - c), hf, me).wait_recv()
        for cp in sent:
            cp.wait_send()
        for cp in mine:
            cp.wait()

    return _sequencer_call(
        body, shards, [SDS(_gathered(s.shape, k), s.dtype) for s, k in zip(shards, kinds)],
        [pltpu.SemaphoreType.DMA((13, nt)), pltpu.SemaphoreType.DMA((13, nt)), pltpu.SemaphoreType.DMA((nt,))],
        name=name, collective_id=collective_id)


def _part_shape(shape, kind):
    if kind == "row":
        return tuple(shape[:-2]) + (shape[-2] // N_DEV, shape[-1])
    return tuple(shape[2:])


def _pair_exchange(grads, kinds, *, name, collective_id):
    nt = len(grads)
    part = [_part_shape(g.shape, k) for g, k in zip(grads, kinds)]

    def body(*refs):
        g_refs, o_refs = refs[:nt], refs[nt:2 * nt]
        send_sems, recv_sems = refs[2 * nt:]
        x, y, c, _ = _mesh_place()
        _handshake([(x, y, 1 - c)])
        copies = []
        for t in range(nt):
            for xy in range(4):
                src = g_refs[t].at[1 - c, xy] if kinds[t] == "stack" else _window(g_refs[t], kinds[t], 2 * xy + 1 - c, part[t])
                copies.append(pltpu.make_async_remote_copy(
                    src_ref=src, dst_ref=o_refs[t].at[xy], send_sem=send_sems.at[xy, t], recv_sem=recv_sems.at[xy, t],
                    device_id=(x, y, 1 - c), device_id_type=MESH))
        for cp in copies:
            cp.start()
        for cp in copies:
            cp.wait()

    return _sequencer_call(
        body, grads, [SDS((4,) + p, g.dtype) for p, g in zip(part, grads)],
        [pltpu.SemaphoreType.DMA((4, nt)), pltpu.SemaphoreType.DMA((4, nt))], name=name, collective_id=collective_id)


def _chip_exchange(parts, small, *, name, collective_id):
    nt = len(parts)
    if small is None:
        def body_plain(*refs):
            s_refs, o_refs = refs[:nt], refs[nt:2 * nt]
            send_sems, recv_sems = refs[2 * nt:]
            x, y, c, chips = _mesh_place()
            _handshake([(*chip, c) for chip in chips])
            copies = [pltpu.make_async_remote_copy(
                src_ref=s_refs[t].at[2 * chip[0] + chip[1]], dst_ref=o_refs[t].at[j],
                send_sem=send_sems.at[j, t], recv_sem=recv_sems.at[j, t], device_id=(*chip, c), device_id_type=MESH)
                for j, chip in enumerate(chips) for t in range(nt)]
            for cp in copies:
                cp.start()
            for cp in copies:
                cp.wait()

        return _sequencer_call(
            body_plain, parts, [SDS((3,) + s.shape[1:], s.dtype) for s in parts],
            [pltpu.SemaphoreType.DMA((3, nt)), pltpu.SemaphoreType.DMA((3, nt))], name=name, collective_id=collective_id)

    def body(*refs):
        s_refs, small_ref = refs[:nt], refs[nt]
        o_refs, small_all = refs[nt + 1:2 * nt + 1], refs[2 * nt + 1]
        send_sems, recv_sems, small_send, small_recv, local_sem = refs[2 * nt + 2:]
        x, y, c, chips = _mesh_place()
        _handshake([(px, py, pc) for px in (x, 1 - x) for py in (y, 1 - y) for pc in (c, 1 - c)][1:])

        def copy(j, t, chip):
            return pltpu.make_async_remote_copy(
                src_ref=s_refs[t].at[2 * chip[0] + chip[1]], dst_ref=o_refs[t].at[j],
                send_sem=send_sems.at[j, t], recv_sem=recv_sems.at[j, t], device_id=(*chip, c), device_id_type=MESH)

        flips = [(fx, fy, fc) for fx in (0, 1) for fy in (0, 1) for fc in (0, 1)][1:]

        def small_copy(k):
            fx, fy, fc = flips[k]
            to = (x ^ fx if fx else x, y ^ fy if fy else y, c ^ fc if fc else c)
            rows = small_all.at[4 * x + 2 * y + c]
            return pltpu.make_async_remote_copy(
                src_ref=small_ref, dst_ref=rows, send_sem=small_send.at[k], recv_sem=small_recv.at[k],
                device_id=to, device_id_type=MESH)

        own = pltpu.make_async_copy(small_ref, small_all.at[4 * x + 2 * y + c], local_sem)
        own.start()
        copies = [copy(j, t, chip) for j, chip in enumerate(chips) for t in range(nt)]
        smalls = [small_copy(k) for k in range(7)]
        for cp in smalls + copies:
            cp.start()
        for cp in smalls + copies:
            cp.wait()
        own.wait()

    return _sequencer_call(
        body, list(parts) + [small],
        [SDS((3,) + s.shape[1:], s.dtype) for s in parts] + [SDS((N_DEV,) + small.shape, small.dtype)],
        [pltpu.SemaphoreType.DMA((3, nt)), pltpu.SemaphoreType.DMA((3, nt)),
         pltpu.SemaphoreType.DMA((7,)), pltpu.SemaphoreType.DMA((7,)), pltpu.SemaphoreType.DMA],
        name=name, collective_id=collective_id)


def _pair_sum(grad, theirs, kind, c, *, name):
    if kind == "row":
        r, l = theirs.shape[-2:]
        n = theirs.size // (4 * r * l)
        grad, theirs = grad.reshape(n, N_DEV * r, l), theirs.reshape(4, n, r, l)
        mine_spec = pl.BlockSpec((n, r, l), lambda xy, c_ref: (0, 2 * xy + c_ref[0], 0))
    else:
        r, l = theirs.shape[-2:]
        n = theirs.size // (4 * r * l)
        theirs = theirs.reshape(4, n, r, l)
        grad = grad.reshape(2, 4, n, r, l)
        mine_spec = pl.BlockSpec((None, None, n, r, l), lambda xy, c_ref: (c_ref[0], xy, 0, 0, 0))

    def body(c_ref, a_ref, b_ref, o_ref):
        o_ref[...] = (a_ref[...].astype(F32) + b_ref[...].astype(F32)).astype(BF16)

    part = pl.BlockSpec((None, n, r, l), lambda xy, c_ref: (xy, 0, 0, 0))
    return pl.pallas_call(
        body, name=name,
        grid_spec=pltpu.PrefetchScalarGridSpec(num_scalar_prefetch=1, grid=(4,), in_specs=[mine_spec, part], out_specs=part),
        out_shape=SDS((4, n, r, l), BF16), compiler_params=_params("parallel"),
    )(c, grad, theirs)


def _adamw(w, g, m, v):
    m = ADAM_B1 * m + (1.0 - ADAM_B1) * g
    v = ADAM_B2 * v + (1.0 - ADAM_B2) * jnp.square(g)
    m_hat = m / (1.0 - ADAM_B1 ** ADAM_STEP)
    v_hat = v / (1.0 - ADAM_B2 ** ADAM_STEP)
    delta = -ADAM_LR * (m_hat / (jnp.sqrt(v_hat) + ADAM_EPS) + ADAM_WD * w)
    return delta, m, v


def _adam_big(owns, others, mat, xy, w, m, v, *, tr, name):
    _, r, l = w.shape
    lp = owns[0].shape[-1]
    nq = len(owns)
    rows = [tr] if nq == 1 else [o.shape[-2] for o in owns]
    assert sum(rows) == tr and r % tr == 0, (r, tr, rows)

    def body(xy_ref, *refs):
        own_refs, oth_refs = refs[:nq], refs[nq:2 * nq]
        w_ref, m_ref, v_ref, g_out, d_out, m_out, v_out = refs[2 * nq:]
        parts = []
        for q in range(nq):
            gq = own_refs[q][0, 0].astype(F32)
            for j in range(3):
                gq = gq + oth_refs[q][j, 0].astype(F32)
            parts.append(gq)
        g = (parts[0] if nq == 1 else jnp.concatenate(parts, axis=0))[:, :l]
        delta, m_new, v_new = _adamw(w_ref[0], g, m_ref[0], v_ref[0])
        g_out[0] = g
        d_out[0] = delta
        m_out[0] = m_new
        v_out[0] = v_new

    blk = pl.BlockSpec((1, tr, l), lambda i, xy_ref: (0, i, 0))
    own_specs = [pl.BlockSpec((1, 1, n, lp), lambda i, xy_ref: (xy_ref[0], mat, i, 0)) for n in rows]
    oth_specs = [pl.BlockSpec((3, 1, n, lp), lambda i, xy_ref: (0, mat, i, 0)) for n in rows]
    return pl.pallas_call(
        body, name=name,
        grid_spec=pltpu.PrefetchScalarGridSpec(
            num_scalar_prefetch=1, grid=(r // tr,), in_specs=own_specs + oth_specs + [blk, blk, blk],
            out_specs=[blk, blk, blk, blk]),
        out_shape=[SDS(w.shape, F32)] * 4, compiler_params=_params("parallel"),
    )(xy, *owns, *others, w, m, v)


def _adam_cols(own, other, xy, w, m, v, *, tc, name):
    _, wc, d = w.shape
    wp = -(-wc // 128) * 128

    def body(xy_ref, own_ref, oth_ref, w_ref, m_ref, v_ref, g_out, d_out, m_out, v_out, buf):
        g = own_ref[0, 0].astype(F32)
        for j in range(3):
            g = g + oth_ref[j, 0].astype(F32)
        buf[:, :wc] = g
        buf[:, wc:] = jnp.zeros((tc, wp - wc), F32)
        g = buf[...].T[:wc]
        delta, m_new, v_new = _adamw(w_ref[0], g, m_ref[0], v_ref[0])
        g_out[0] = g
        d_out[0] = delta
        m_out[0] = m_new
        v_out[0] = v_new

    blk = pl.BlockSpec((1, wc, tc), lambda i, xy_ref: (0, 0, i))
    in_specs = [pl.BlockSpec((1, 1, tc, wc), lambda i, xy_ref: (xy_ref[0], 0, i, 0)),
                pl.BlockSpec((3, 1, tc, wc), lambda i, xy_ref: (0, 0, i, 0)), blk, blk, blk]
    return pl.pallas_call(
        body, name=name,
        grid_spec=pltpu.PrefetchScalarGridSpec(
            num_scalar_prefetch=1, grid=(d // tc,), in_specs=in_specs, out_specs=[blk, blk, blk, blk],
            scratch_shapes=[pltpu.VMEM((tc, wp), F32)]),
        out_shape=[SDS(w.shape, F32)] * 4, compiler_params=_params("parallel"),
    )(xy, own, other, w, m, v)


def _small_layout(shapes):
    out, at = [], 0
    for r, c in shapes:
        rows = c // 128 if (r == 1 and c > 128) else r
        out.append((at, rows))
        at += -(-rows // 8) * 8
    return out, at


def _pack_small(parts, *, name):
    shapes = [a.shape for a in parts]
    layout, total = _small_layout(shapes)

    def body(*refs):
        o_ref = refs[-1]
        o_ref[...] = jnp.zeros_like(o_ref)
        for x_ref, (r, c), (at, rows) in zip(refs, shapes, layout):
            if r == 1 and c > 128:
                for k in range(rows):
                    o_ref[at + k:at + k + 1, :] = x_ref[:, k * 128:(k + 1) * 128]
            else:
                o_ref[at:at + r, 0:c] = x_ref[...]

    return pl.pallas_call(body, name=name, out_shape=SDS((total, 128), F32))(*parts)


def _adam_small(g_all, ws, ms, vs, *, name):
    n = len(ws)
    shapes = [w.shape for w in ws]
    layout, _ = _small_layout(shapes)

    def body(g_ref, *refs):
        w_refs, m_refs, v_refs, outs = refs[:n], refs[n:2 * n], refs[2 * n:3 * n], refs[3 * n:]
        g_sum = g_ref[0]
        for k in range(1, N_DEV):
            g_sum = g_sum + g_ref[k]
        for i, ((r, c), (at, rows)) in enumerate(zip(shapes, layout)):
            if r == 1 and c > 128:
                g = jnp.concatenate([g_sum[at + k:at + k + 1, :] for k in range(rows)], axis=1)
            else:
                g = g_sum[at:at + r, 0:c]
            delta, m_new, v_new = _adamw(w_refs[i][...], g, m_refs[i][...], v_refs[i][...])
            for q, val in enumerate((g, delta, m_new, v_new)):
                outs[4 * i + q][...] = val

    flat = pl.pallas_call(body, name=name, out_shape=[SDS(s, F32) for s in shapes for _ in range(4)])(g_all, *ws, *ms, *vs)
    return [flat[4 * i:4 * i + 4] for i in range(n)]


def kernel(x, mem, ffn1_norm, ffn1_w_gate, ffn1_w_up, ffn1_w_down, mix_norm, mem_norm, w_in, w_mem_kv, swa_q_norm, swa_k_norm, swa_sinks, rel_bias, gla_w_gate_up, gla_b_gate, gla_out_norm, mem_q_norm, mem_k_norm, w_out, ffn2_norm, ffn2_w_gate, ffn2_w_up, ffn2_w_down, loss_target, m_ffn1_norm, m_ffn1_w_gate, m_ffn1_w_up, m_ffn1_w_down, m_mix_norm, m_mem_norm, m_w_in, m_w_mem_kv, m_swa_q_norm, m_swa_k_norm, m_swa_sinks, m_rel_bias, m_gla_w_gate_up, m_gla_b_gate, m_gla_out_norm, m_mem_q_norm, m_mem_k_norm, m_w_out, m_ffn2_norm, m_ffn2_w_gate, m_ffn2_w_up, m_ffn2_w_down, v_ffn1_norm, v_ffn1_w_gate, v_ffn1_w_up, v_ffn1_w_down, v_mix_norm, v_mem_norm, v_w_in, v_w_mem_kv, v_swa_q_norm, v_swa_k_norm, v_swa_sinks, v_rel_bias, v_gla_w_gate_up, v_gla_b_gate, v_gla_out_norm, v_mem_q_norm, v_mem_k_norm, v_w_out, v_ffn2_norm, v_ffn2_w_gate, v_ffn2_w_up, v_ffn2_w_down):
    xi, yi, ci = lax.axis_index("x"), lax.axis_index("y"), lax.axis_index("c")
    c_arr = jnp.reshape(ci, (1,)).astype(jnp.int32)
    xy_arr = jnp.reshape(2 * xi + yi, (1,)).astype(jnp.int32)
    d = x.shape[-1]

    def ffn_shards(wg_s, wu_s, wd_s):
        return jnp.concatenate([wg_s.transpose(0, 2, 1), wu_s.transpose(0, 2, 1), wd_s], axis=0).astype(BF16)

    def gather_ffn(wg_s, wu_s, wd_s, name, collective_id, after):
        w3_s, _ = lax.optimization_barrier((ffn_shards(wg_s, wu_s, wd_s), after))
        return _all_gather([w3_s], ["row"], name=name, collective_id=collective_id)[0]

    w3_s = ffn_shards(ffn1_w_gate, ffn1_w_up, ffn1_w_down)
    w3_1a = _all_gather([w3_s[:, :FFN1_FIRST]], ["row"], name="gather_ffn1a", collective_id=0)[0]
    w3_s, _ = lax.optimization_barrier((w3_s, w3_1a))
    w3_1b = _all_gather([w3_s[:, FFN1_FIRST:]], ["row"], name="gather_ffn1b", collective_id=11)[0]

    def gather_mix(after):
        mix_s = lax.optimization_barrier((w_in[0].astype(BF16), w_mem_kv[0].astype(BF16), w_out[0].astype(BF16),
                                          (w3_1b, after)))[:3]
        win_all, wkv, wout = _all_gather(list(mix_s), ["stack", "row", "row"], name="gather_mix", collective_id=1)
        return _pack_win(win_all, tr=256, name="pack_w_in"), wkv, wout

    def gather_ffn2(after):
        return gather_ffn(ffn2_w_gate, ffn2_w_up, ffn2_w_down, "gather_ffn2", 2, after)

    small_w = [ffn1_norm, mix_norm, mem_norm, ffn2_norm, swa_q_norm, swa_k_norm, swa_sinks[0], rel_bias,
               gla_w_gate_up[0], gla_b_gate, gla_out_norm, mem_q_norm, mem_k_norm]
    collective_ids = {"ffn2": (3, 4), "mix": (5, 6), "ffn1a": (7, 8), "ffn1b": (9, 10)}
    reduced, small_box = {}, {}

    def on_grads(group, grads, carry, small=None):
        if group == "mix":
            grads = list(grads)
            kinds = ["stack", "row", "row"]
        else:
            kinds = ["row"]
        if reduced:
            earlier = list(reduced.values())[-1][1]
            *grads, _ = lax.optimization_barrier((*grads, earlier[0]))
        id_pair, id_chip = collective_ids[group]
        from_sibling = _pair_exchange(grads, kinds, name=f"pair_exchange_{group}", collective_id=id_pair)
        chip_sums = [_pair_sum(g, theirs, k, c_arr, name=f"pair_sum_{group}_{t}")
                     for t, (g, theirs, k) in enumerate(zip(grads, from_sibling, kinds))]
        if carry is not None:
            *chip_sums, carry = lax.optimization_barrier((*chip_sums, carry))
        if small is None:
            from_chips = _chip_exchange(chip_sums, None, name=f"chip_exchange_{group}", collective_id=id_chip)
        else:
            packed = _pack_small(small, name=f"pack_small_{group}")
            *from_chips, small_all = _chip_exchange(chip_sums, packed, name=f"chip_exchange_{group}",
                                                    collective_id=id_chip)
            small_box[group] = small_all
        reduced[group] = (chip_sums, from_chips)
        return carry

    grad_x = _local_step(x[0], mem[0], loss_target[0], small_w, ((w3_1a, w3_1b), gather_mix, gather_ffn2), on_grads)

    big_w = {"ffn1_w_gate": ("ffn1", 0, 0, True, ffn1_w_gate, m_ffn1_w_gate, v_ffn1_w_gate),
             "ffn1_w_up": ("ffn1", 0, 1, True, ffn1_w_up, m_ffn1_w_up, v_ffn1_w_up),
             "ffn1_w_down": ("ffn1", 0, 2, False, ffn1_w_down, m_ffn1_w_down, v_ffn1_w_down),
             "w_in": ("mix", 0, 0, True, w_in, m_w_in, v_w_in),
             "w_mem_kv": ("mix", 1, 0, False, w_mem_kv, m_w_mem_kv, v_w_mem_kv),
             "w_out": ("mix", 2, 0, False, w_out, m_w_out, v_w_out),
             "ffn2_w_gate": ("ffn2", 0, 0, True, ffn2_w_gate, m_ffn2_w_gate, v_ffn2_w_gate),
             "ffn2_w_up": ("ffn2", 0, 1, True, ffn2_w_up, m_ffn2_w_up, v_ffn2_w_up),
             "ffn2_w_down": ("ffn2", 0, 2, False, ffn2_w_down, m_ffn2_w_down, v_ffn2_w_down)}
    res = {}
    for nm, (group, t, mat, transposed, w, m, v) in big_w.items():
        shape = w.shape
        if transposed:
            w, m, v = (a.transpose(0, 2, 1) for a in (w, m, v))
        if nm == "w_in":
            out = _adam_cols(reduced[group][0][t], reduced[group][1][t], xy_arr, w, m, v, tc=256, name=f"adam_{nm}")
            res[nm] = [a.transpose(0, 2, 1) for a in out]
            continue
        r = w.shape[1]
        halves = ["ffn1a", "ffn1b"] if group == "ffn1" else [group]
        if len(halves) == 1:
            tr = 256 if r % 256 == 0 else r
        else:
            tr = r
        out = _adam_big([reduced[k][0][t] for k in halves], [reduced[k][1][t] for k in halves], mat, xy_arr, w, m, v,
                        tr=tr, name=f"adam_{nm}")
        if transposed:
            out = [a.reshape(1, -1, d).transpose(0, 2, 1) for a in out]
        res[nm] = [a.reshape(shape) for a in out]
    small_names = ["ffn1_norm", "mix_norm", "mem_norm", "ffn2_norm", "swa_q_norm", "swa_k_norm", "swa_sinks", "rel_bias",
                   "gla_w_gate_up", "gla_b_gate", "gla_out_norm", "mem_q_norm", "mem_k_norm"]
    small_m = [m_ffn1_norm, m_mix_norm, m_mem_norm, m_ffn2_norm, m_swa_q_norm, m_swa_k_norm, m_swa_sinks, m_rel_bias,
               m_gla_w_gate_up, m_gla_b_gate, m_gla_out_norm, m_mem_q_norm, m_mem_k_norm]
    small_v = [v_ffn1_norm, v_mix_norm, v_mem_norm, v_ffn2_norm, v_swa_q_norm, v_swa_k_norm, v_swa_sinks, v_rel_bias,
               v_gla_w_gate_up, v_gla_b_gate, v_gla_out_norm, v_mem_q_norm, v_mem_k_norm]
    small_full = [ffn1_norm, mix_norm, mem_norm, ffn2_norm, swa_q_norm, swa_k_norm, swa_sinks, rel_bias,
                  gla_w_gate_up, gla_b_gate, gla_out_norm, mem_q_norm, mem_k_norm]
    zero = jnp.zeros((1, 1), F32)
    turned = ("rel_bias",)

    def two_d(nm, a):
        a = a.reshape(a.shape[-2:])
        return a.T if nm in turned else a

    for group, sel in (("ffn1a", slice(1, None)), ("ffn1b", slice(0, 1))):
        extra = [zero] if group == "ffn1a" else []
        ws, ms, vs = ([two_d(nm, a) for nm, a in zip(small_names[sel], arrs[sel])] + extra
                      for arrs in (small_full, small_m, small_v))
        updated = _adam_small(small_box[group], ws, ms, vs, name=f"adam_small_{group}")
        for nm, full, out in zip(small_names[sel], small_full[sel], updated):
            res[nm] = [(a.T if nm in turned else a).reshape(full.shape) for a in out]
        if extra:
            loss = updated[-1][0].reshape(())

    order = ["ffn1_norm", "ffn1_w_gate", "ffn1_w_up", "ffn1_w_down", "mix_norm", "mem_norm", "w_in", "w_mem_kv",
             "swa_q_norm", "swa_k_norm", "swa_sinks", "rel_bias", "gla_w_gate_up", "gla_b_gate", "gla_out_norm",
             "mem_q_norm", "mem_k_norm", "w_out", "ffn2_norm", "ffn2_w_gate", "ffn2_w_up", "ffn2_w_down"]
    outs = [loss, grad_x[None]]
    for q in range(4):
        outs += [res[nm][q] for nm in order]
    return tuple(outs)
```
